```python
import math
import jax, jax.numpy as jnp
from jax import lax
import numpy as np

D_MODEL = 1024
BATCH = 8
SEQ = 4096
DEPTH = 2

CHUNK = 64
HEAD_DIM = 64
N_A_LAYERS = DEPTH // 2
N_B_LAYERS = DEPTH - N_A_LAYERS
RMS_EPS = 1e-6
A_HEADS = D_MODEL // HEAD_DIM
A_WIDTH = A_HEADS * HEAD_DIM
A_LEFT_CHUNKS = 8
A_BAND = (A_LEFT_CHUNKS + 1) * CHUNK
A_REL_CLIP = 256
B_Q_HEADS = D_MODEL // HEAD_DIM
B_KV_HEADS = max(1, B_Q_HEADS // 8)
B_GROUP = B_Q_HEADS // B_KV_HEADS
B_WIDTH = B_Q_HEADS * HEAD_DIM
B_KV_WIDTH = B_KV_HEADS * HEAD_DIM
B_WINDOW = 128
B_LEFT_CHUNKS = (B_WINDOW - 1) // CHUNK + 1
B_BAND = (B_LEFT_CHUNKS + 1) * CHUNK
T5_BUCKETS = 32
T5_MAX_DIST = 128

kernel_name = "yoco_chunk_relbias_swa_sink_hybrid"


def rmsnorm(x, g):
    xf = x.astype(jnp.float32)
    y = xf * lax.rsqrt(jnp.mean(xf * xf, axis=-1, keepdims=True) + RMS_EPS)
    return (y * g.astype(jnp.float32)).astype(x.dtype)


def t5_bucket(rel):
    nb = T5_BUCKETS // 2
    max_exact = nb // 2
    ret = jnp.where(rel > 0, nb, 0)
    n = jnp.abs(rel)
    nf = jnp.maximum(n, 1).astype(jnp.float32)
    large = max_exact + (jnp.log(nf / max_exact) / math.log(T5_MAX_DIST / max_exact)
                         * (nb - max_exact)).astype(jnp.int32)
    large = jnp.minimum(large, nb - 1)
    return ret + jnp.where(n < max_exact, n, large)


def mixer_a(x, w_in, rel_bias, w_out):
    b, s, _ = x.shape
    nc = s // CHUNK
    pad = A_LEFT_CHUNKS * CHUNK
    q, k, v, g = jnp.split(x @ w_in, 4, axis=-1)
    q = q.reshape(b, nc, CHUNK, A_HEADS, HEAD_DIM)
    k = jnp.pad(k.reshape(b, s, A_HEADS, HEAD_DIM), ((0, 0), (pad, 0), (0, 0), (0, 0)))
    v = jnp.pad(v.reshape(b, s, A_HEADS, HEAD_DIM), ((0, 0), (pad, 0), (0, 0), (0, 0)))
    dist = jnp.arange(CHUNK)[:, None] + pad - jnp.arange(A_BAND)[None, :]
    idx = jnp.clip(dist, -A_REL_CLIP, A_REL_CLIP) + A_REL_CLIP
    bias = jnp.transpose(rel_bias[idx], (2, 0, 1)).astype(jnp.float32)
    scale = HEAD_DIM ** -0.5

    def one_chunk(args):
        qc, c = args
        kc = lax.dynamic_slice_in_dim(k, c * CHUNK, A_BAND, axis=1)
        vc = lax.dynamic_slice_in_dim(v, c * CHUNK, A_BAND, axis=1)
        logits = jnp.einsum('bqhd,bkhd->bhqk', qc, kc).astype(jnp.float32) * scale + bias
        key_pos = c * CHUNK - pad + jnp.arange(A_BAND)
        logits = jnp.where((key_pos >= 0)[None, None, None, :], logits, -jnp.inf)
        p = jax.nn.softmax(logits, axis=-1).astype(vc.dtype)
        return jnp.einsum('bhqk,bkhd->bqhd', p, vc)

    out = lax.map(one_chunk, (jnp.moveaxis(q, 1, 0), jnp.arange(nc)))
    out = jnp.moveaxis(out, 0, 1).reshape(b, s, A_WIDTH)
    return (out * jax.nn.silu(g)) @ w_out


def mixer_b(x, w_in, sinks, t5_table, k_sh, v_sh, w_out):
    b, s, _ = x.shape
    nc = s // CHUNK
    pad = B_LEFT_CHUNKS * CHUNK
    q, g = jnp.split(x @ w_in, 2, axis=-1)
    q = q.reshape(b, nc, CHUNK, B_KV_HEADS, B_GROUP, HEAD_DIM)
    kp = jnp.pad(k_sh, ((0, 0), (pad, 0), (0, 0), (0, 0)))
    vp = jnp.pad(v_sh, ((0, 0), (pad, 0), (0, 0), (0, 0)))
    band_idx = jnp.arange(nc)[:, None] * CHUNK + jnp.arange(B_BAND)[None, :]
    kb = kp[:, band_idx]
    vb = vp[:, band_idx]
    rel = jnp.arange(B_BAND)[None, :] - pad - jnp.arange(CHUNK)[:, None]
    bias = jnp.transpose(t5_table[t5_bucket(rel)], (2, 0, 1)).astype(jnp.float32)
    bias = bias.reshape(B_KV_HEADS, B_GROUP, 1, CHUNK, B_BAND)
    scale = HEAD_DIM ** -0.5
    logits = jnp.einsum('bcqhgd,bckhd->bhgcqk', q, kb).astype(jnp.float32) * scale + bias
    key_pos = band_idx - pad
    logits = jnp.where((key_pos >= 0)[:, None, :], logits, -jnp.inf)
    sink = jnp.broadcast_to(
        sinks.astype(jnp.float32).reshape(1, B_KV_HEADS, B_GROUP, 1, 1, 1),
        logits.shape[:-1] + (1,))
    p = jax.nn.softmax(jnp.concatenate([logits, sink], axis=-1), axis=-1)[..., :-1]
    out = jnp.einsum('bhgcqk,bckhd->bcqhgd', p.astype(vb.dtype), vb)
    out = out.reshape(b, s, B_WIDTH)
    return (out * jax.nn.silu(g)) @ w_out


def _fwd_setup_inputs(seed: int = 0) -> dict:
    key = jax.random.key(seed)
    ks = jax.random.split(key, 13)
    f32 = jnp.float32
    nrm = lambda k, shape, sc: (jax.random.normal(k, shape, f32) * sc).astype(f32)
    return {
        "x": nrm(ks[0], (BATCH, SEQ, D_MODEL), 1.0),
        "a_norm": 1.0 + nrm(ks[1], (N_A_LAYERS, D_MODEL), 0.02),
        "a_w_in": nrm(ks[2], (N_A_LAYERS, D_MODEL, 4 * A_WIDTH), D_MODEL ** -0.5),
        "a_rel_bias": nrm(ks[3], (N_A_LAYERS, 2 * A_REL_CLIP + 1, A_HEADS), 0.3),
        "a_w_out": nrm(ks[4], (N_A_LAYERS, A_WIDTH, D_MODEL), A_WIDTH ** -0.5),
        "kv_norm": 1.0 + nrm(ks[5], (D_MODEL,), 0.02),
        "kv_w": nrm(ks[6], (D_MODEL, 2 * B_KV_WIDTH), D_MODEL ** -0.5),
        "t5_bias": nrm(ks[7], (T5_BUCKETS, B_Q_HEADS), 0.3),
        "b_norm": 1.0 + nrm(ks[8], (N_B_LAYERS, D_MODEL), 0.02),
        "b_w_in": nrm(ks[9], (N_B_LAYERS, D_MODEL, 2 * B_WIDTH), D_MODEL ** -0.5),
        "b_sinks": nrm(ks[10], (N_B_LAYERS, B_Q_HEADS), 0.5),
        "b_w_out": nrm(ks[11], (N_B_LAYERS, B_WIDTH, D_MODEL), B_WIDTH ** -0.5),
        "final_norm": 1.0 + nrm(ks[12], (D_MODEL,), 0.02),
    }


def _fwd_reference(x, a_norm, a_w_in, a_rel_bias, a_w_out, kv_norm, kv_w, t5_bias,
              b_norm, b_w_in, b_sinks, b_w_out, final_norm):
    h = x
    k_sh = v_sh = None
    for layer in range(DEPTH):
        if layer == N_A_LAYERS:
            kv = rmsnorm(h, kv_norm) @ kv_w
            k_sh, v_sh = jnp.split(kv, 2, axis=-1)
            k_sh = k_sh.reshape(h.shape[0], h.shape[1], B_KV_HEADS, HEAD_DIM)
            v_sh = v_sh.reshape(h.shape[0], h.shape[1], B_KV_HEADS, HEAD_DIM)
        if layer < N_A_LAYERS:
            h = h + mixer_a(rmsnorm(h, a_norm[layer]), a_w_in[layer],
                            a_rel_bias[layer], a_w_out[layer])
        else:
            j = layer - N_A_LAYERS
            h = h + mixer_b(rmsnorm(h, b_norm[j]), b_w_in[j], b_sinks[j], t5_bias,
                            k_sh, v_sh, b_w_out[j])
    return rmsnorm(h, final_norm)


import jax as _jax
import jax.numpy as _jnp

TWIN_FORMAT = 'train_step'
FWD_PARAMS = ['x', 'a_norm', 'a_w_in', 'a_rel_bias', 'a_w_out', 'kv_norm', 'kv_w', 't5_bias', 'b_norm', 'b_w_in', 'b_sinks', 'b_w_out', 'final_norm']
TWIN_WEIGHTS = ['a_norm', 'a_w_in', 'a_rel_bias', 'a_w_out', 'kv_norm', 'kv_w', 't5_bias', 'b_norm', 'b_w_in', 'b_sinks', 'b_w_out', 'final_norm']
TWIN_DIFF_INPUT = 'x'
TWIN_INPUTS = ['x', 'a_norm', 'a_w_in', 'a_rel_bias', 'a_w_out', 'kv_norm', 'kv_w', 't5_bias', 'b_norm', 'b_w_in', 'b_sinks', 'b_w_out', 'final_norm', 'loss_target', 'm_a_norm', 'm_a_w_in', 'm_a_rel_bias', 'm_a_w_out', 'm_kv_norm', 'm_kv_w', 'm_t5_bias', 'm_b_norm', 'm_b_w_in', 'm_b_sinks', 'm_b_w_out', 'm_final_norm', 'v_a_norm', 'v_a_w_in', 'v_a_rel_bias', 'v_a_w_out', 'v_kv_norm', 'v_kv_w', 'v_t5_bias', 'v_b_norm', 'v_b_w_in', 'v_b_sinks', 'v_b_w_out', 'v_final_norm']
TWIN_OUTPUTS = ['loss', 'grad_x', 'grad_a_norm', 'grad_a_w_in', 'grad_a_rel_bias', 'grad_a_w_out', 'grad_kv_norm', 'grad_kv_w', 'grad_t5_bias', 'grad_b_norm', 'grad_b_w_in', 'grad_b_sinks', 'grad_b_w_out', 'grad_final_norm', 'delta_a_norm', 'delta_a_w_in', 'delta_a_rel_bias', 'delta_a_w_out', 'delta_kv_norm', 'delta_kv_w', 'delta_t5_bias', 'delta_b_norm', 'delta_b_w_in', 'delta_b_sinks', 'delta_b_w_out', 'delta_final_norm', 'new_m_a_norm', 'new_m_a_w_in', 'new_m_a_rel_bias', 'new_m_a_w_out', 'new_m_kv_norm', 'new_m_kv_w', 'new_m_t5_bias', 'new_m_b_norm', 'new_m_b_w_in', 'new_m_b_sinks', 'new_m_b_w_out', 'new_m_final_norm', 'new_v_a_norm', 'new_v_a_w_in', 'new_v_a_rel_bias', 'new_v_a_w_out', 'new_v_kv_norm', 'new_v_kv_w', 'new_v_t5_bias', 'new_v_b_norm', 'new_v_b_w_in', 'new_v_b_sinks', 'new_v_b_w_out', 'new_v_final_norm']
TWIN_LEAF_KINDS = {'loss': 'loss', 'grad_x': 'grad_x', 'grad_a_norm': 'grad_w', 'grad_a_w_in': 'grad_w', 'grad_a_rel_bias': 'grad_w', 'grad_a_w_out': 'grad_w', 'grad_kv_norm': 'grad_w', 'grad_kv_w': 'grad_w', 'grad_t5_bias': 'grad_w', 'grad_b_norm': 'grad_w', 'grad_b_w_in': 'grad_w', 'grad_b_sinks': 'grad_w', 'grad_b_w_out': 'grad_w', 'grad_final_norm': 'grad_w', 'delta_a_norm': 'delta_w', 'delta_a_w_in': 'delta_w', 'delta_a_rel_bias': 'delta_w', 'delta_a_w_out': 'delta_w', 'delta_kv_norm': 'delta_w', 'delta_kv_w': 'delta_w', 'delta_t5_bias': 'delta_w', 'delta_b_norm': 'delta_w', 'delta_b_w_in': 'delta_w', 'delta_b_sinks': 'delta_w', 'delta_b_w_out': 'delta_w', 'delta_final_norm': 'delta_w', 'new_m_a_norm': 'new_m', 'new_m_a_w_in': 'new_m', 'new_m_a_rel_bias': 'new_m', 'new_m_a_w_out': 'new_m', 'new_m_kv_norm': 'new_m', 'new_m_kv_w': 'new_m', 'new_m_t5_bias': 'new_m', 'new_m_b_norm': 'new_m', 'new_m_b_w_in': 'new_m', 'new_m_b_sinks': 'new_m', 'new_m_b_w_out': 'new_m', 'new_m_final_norm': 'new_m', 'new_v_a_norm': 'new_v', 'new_v_a_w_in': 'new_v', 'new_v_a_rel_bias': 'new_v', 'new_v_a_w_out': 'new_v', 'new_v_kv_norm': 'new_v', 'new_v_kv_w': 'new_v', 'new_v_t5_bias': 'new_v', 'new_v_b_norm': 'new_v', 'new_v_b_w_in': 'new_v', 'new_v_b_sinks': 'new_v', 'new_v_b_w_out': 'new_v', 'new_v_final_norm': 'new_v'}


def _forward(args):
    return _fwd_reference(*[args[k] for k in FWD_PARAMS])


def _output_shape():
    def fwd():
        inp = _fwd_setup_inputs(0)
        return _fwd_reference(*[inp[k] for k in FWD_PARAMS])
    out = _jax.eval_shape(fwd)
    return out.shape, out.dtype

N_MICROBATCH = 1
ADAM_LR = 0.001
ADAM_B1 = 0.9
ADAM_B2 = 0.999
ADAM_EPS = 1e-08
ADAM_WD = 0.01
ADAM_STEP = 10
PER_EXAMPLE_BATCH_AXIS = {'x': 0, 'loss_target': 0}
SHARED_INPUTS = []
_WEIGHT_DTYPES = {'a_norm': _jnp.float32, 'a_w_in': _jnp.float32, 'a_rel_bias': _jnp.float32, 'a_w_out': _jnp.float32, 'kv_norm': _jnp.float32, 'kv_w': _jnp.float32, 't5_bias': _jnp.float32, 'b_norm': _jnp.float32, 'b_w_in': _jnp.float32, 'b_sinks': _jnp.float32, 'b_w_out': _jnp.float32, 'final_norm': _jnp.float32}
MOMENT_SCALE = {'a_norm': 3.260627e-02, 'a_w_in': 1.614177e-02, 'a_rel_bias': 5.427683e-03, 'a_w_out': 1.649633e-02, 'kv_norm': 2.244656e-02, 'kv_w': 4.471463e-02, 't5_bias': 1.909355e-02, 'b_norm': 2.213570e-02, 'b_w_in': 1.596009e-02, 'b_sinks': 1.104108e-03, 'b_w_out': 1.612968e-02, 'final_norm': 3.203400e+01}


def _to_microbatches(a, axis):
    t = _jnp.moveaxis(a, axis, 0)
    t = t.reshape((N_MICROBATCH, t.shape[0] // N_MICROBATCH) + t.shape[1:])
    return _jnp.moveaxis(t, 1, axis + 1)


def setup_inputs(seed: int = 0) -> dict:
    inp = _fwd_setup_inputs(seed)
    key = _jax.random.fold_in(_jax.random.key(seed), 7919)
    shape, _ = _output_shape()
    out = dict(inp)
    out["loss_target"] = _jax.random.normal(_jax.random.fold_in(key, 0), shape, _jnp.float32)
    for i, name in enumerate(TWIN_WEIGHTS):
        w = inp[name].astype(_jnp.float32)
        if MOMENT_SCALE is None:
            s = _jnp.sqrt(_jnp.mean(_jnp.square(w)) + 1e-30)
        else:
            s = MOMENT_SCALE[name]
        km, kv = _jax.random.split(_jax.random.fold_in(key, i + 1))
        out[name] = w
        out["m_" + name] = s * _jax.random.normal(km, w.shape, _jnp.float32)
        out["v_" + name] = (s * s) * _jax.random.uniform(kv, w.shape, _jnp.float32, 0.5, 1.5)
    if N_MICROBATCH > 1:
        for name, axis in PER_EXAMPLE_BATCH_AXIS.items():
            out[name] = _to_microbatches(out[name], axis)
    return {'x': out['x'], 'a_norm': out['a_norm'], 'a_w_in': out['a_w_in'], 'a_rel_bias': out['a_rel_bias'], 'a_w_out': out['a_w_out'], 'kv_norm': out['kv_norm'], 'kv_w': out['kv_w'], 't5_bias': out['t5_bias'], 'b_norm': out['b_norm'], 'b_w_in': out['b_w_in'], 'b_sinks': out['b_sinks'], 'b_w_out': out['b_w_out'], 'final_norm': out['final_norm'], 'loss_target': out['loss_target'], 'm_a_norm': out['m_a_norm'], 'm_a_w_in': out['m_a_w_in'], 'm_a_rel_bias': out['m_a_rel_bias'], 'm_a_w_out': out['m_a_w_out'], 'm_kv_norm': out['m_kv_norm'], 'm_kv_w': out['m_kv_w'], 'm_t5_bias': out['m_t5_bias'], 'm_b_norm': out['m_b_norm'], 'm_b_w_in': out['m_b_w_in'], 'm_b_sinks': out['m_b_sinks'], 'm_b_w_out': out['m_b_w_out'], 'm_final_norm': out['m_final_norm'], 'v_a_norm': out['v_a_norm'], 'v_a_w_in': out['v_a_w_in'], 'v_a_rel_bias': out['v_a_rel_bias'], 'v_a_w_out': out['v_a_w_out'], 'v_kv_norm': out['v_kv_norm'], 'v_kv_w': out['v_kv_w'], 'v_t5_bias': out['v_t5_bias'], 'v_b_norm': out['v_b_norm'], 'v_b_w_in': out['v_b_w_in'], 'v_b_sinks': out['v_b_sinks'], 'v_b_w_out': out['v_b_w_out'], 'v_final_norm': out['v_final_norm']}


def _loss(weights, diff, rest, loss_target):
    with _jax.named_scope("forward"):
        args = {**rest, TWIN_DIFF_INPUT: diff, **{k: w.astype(_WEIGHT_DTYPES[k]) for k, w in weights.items()}}
        y = _forward(args)
    with _jax.named_scope("loss_head"):
        err = _jnp.square(y.astype(_jnp.float32) - loss_target)
        return 0.5 * _jnp.sum(_jnp.mean(err, axis=-1)) if err.ndim else 0.5 * err


def _adamw(w, g, m, v):
    m = ADAM_B1 * m + (1.0 - ADAM_B1) * g
    v = ADAM_B2 * v + (1.0 - ADAM_B2) * _jnp.square(g)
    m_hat = m / (1.0 - ADAM_B1 ** ADAM_STEP)
    v_hat = v / (1.0 - ADAM_B2 ** ADAM_STEP)
    delta = -ADAM_LR * (m_hat / (_jnp.sqrt(v_hat) + ADAM_EPS) + ADAM_WD * w)
    return delta, m, v


def reference(x, a_norm, a_w_in, a_rel_bias, a_w_out, kv_norm, kv_w, t5_bias, b_norm, b_w_in, b_sinks, b_w_out, final_norm, loss_target, m_a_norm, m_a_w_in, m_a_rel_bias, m_a_w_out, m_kv_norm, m_kv_w, m_t5_bias, m_b_norm, m_b_w_in, m_b_sinks, m_b_w_out, m_final_norm, v_a_norm, v_a_w_in, v_a_rel_bias, v_a_w_out, v_kv_norm, v_kv_w, v_t5_bias, v_b_norm, v_b_w_in, v_b_sinks, v_b_w_out, v_final_norm):
    given = dict(x=x, a_norm=a_norm, a_w_in=a_w_in, a_rel_bias=a_rel_bias, a_w_out=a_w_out, kv_norm=kv_norm, kv_w=kv_w, t5_bias=t5_bias, b_norm=b_norm, b_w_in=b_w_in, b_sinks=b_sinks, b_w_out=b_w_out, final_norm=final_norm, loss_target=loss_target, m_a_norm=m_a_norm, m_a_w_in=m_a_w_in, m_a_rel_bias=m_a_rel_bias, m_a_w_out=m_a_w_out, m_kv_norm=m_kv_norm, m_kv_w=m_kv_w, m_t5_bias=m_t5_bias, m_b_norm=m_b_norm, m_b_w_in=m_b_w_in, m_b_sinks=m_b_sinks, m_b_w_out=m_b_w_out, m_final_norm=m_final_norm, v_a_norm=v_a_norm, v_a_w_in=v_a_w_in, v_a_rel_bias=v_a_rel_bias, v_a_w_out=v_a_w_out, v_kv_norm=v_kv_norm, v_kv_w=v_kv_w, v_t5_bias=v_t5_bias, v_b_norm=v_b_norm, v_b_w_in=v_b_w_in, v_b_sinks=v_b_sinks, v_b_w_out=v_b_w_out, v_final_norm=v_final_norm)
    weights = {n: given[n] for n in TWIN_WEIGHTS}
    shared = {n: given[n] for n in SHARED_INPUTS}
    per_example = {n: given[n] for n in ['x']}
    grad_fn = _jax.value_and_grad(_loss, argnums=(0, 1))

    def one_microbatch(ex, loss_target):
        ex = dict(ex)
        diff = ex.pop(TWIN_DIFF_INPUT)
        return grad_fn(weights, diff, {**shared, **ex}, loss_target)

    if N_MICROBATCH == 1:
        loss, (grad_w, grad_x) = one_microbatch(per_example, given["loss_target"])
    else:
        def body(carry, xs):
            loss_sum, grad_sum = carry
            l_k, (gw_k, gx_k) = one_microbatch(xs[0], xs[1])
            with _jax.named_scope("update"):
                return (loss_sum + l_k, _jax.tree.map(_jnp.add, grad_sum, gw_k)), gx_k

        init = (_jnp.zeros((), _jnp.float32), _jax.tree.map(_jnp.zeros_like, weights))
        (loss, grad_w), grad_x = _jax.lax.scan(body, init, (per_example, given["loss_target"]))
    with _jax.named_scope("update"):
        delta_w, new_m, new_v = {}, {}, {}
        for n in TWIN_WEIGHTS:
            delta_w[n], new_m[n], new_v[n] = _adamw(weights[n], grad_w[n], given["m_" + n], given["v_" + n])
    return (loss, grad_x, *[grad_w[n] for n in TWIN_WEIGHTS], *[delta_w[n] for n in TWIN_WEIGHTS],
            *[new_m[n] for n in TWIN_WEIGHTS], *[new_v[n] for n in TWIN_WEIGHTS])
```

```python
import functools
import math

import numpy as np
import jax
import jax.numpy as jnp
from jax import lax
from jax.experimental import pallas as pl
from jax.experimental.pallas import tpu as pltpu

F32 = jnp.float32
BF16 = jnp.bfloat16
SDS = jax.ShapeDtypeStruct

D_MODEL = 1024
HEAD_DIM = 64
CHUNK = 64
N_HEADS = 16
RMS_EPS = 1e-6
A_LEFT_CHUNKS = 8
A_BAND = (A_LEFT_CHUNKS + 1) * CHUNK
A_REL_CLIP = 256
B_KV_HEADS = 2
B_GROUP = 8
B_LEFT_CHUNKS = 2
B_BAND = (B_LEFT_CHUNKS + 1) * CHUNK
T5_BUCKETS = 32
T5_MAX_DIST = 128
QBLK = 256
A_KEYS = 3 * QBLK
B_KEYS = QBLK + 128
A_DIAG = A_KEYS
B_DIAG = B_KEYS
NEG = -1e30
SCALE = HEAD_DIM ** -0.5
N_DEV = 8

ADAM_LR = 0.001
ADAM_B1 = 0.9
ADAM_B2 = 0.999
ADAM_EPS = 1e-08
ADAM_WD = 0.01
ADAM_STEP = 10

VMEM_LIMIT_BYTES = 56 * 1024 * 1024
MESH = pl.DeviceIdType.MESH


def _cparams():
    return pltpu.CompilerParams(vmem_limit_bytes=VMEM_LIMIT_BYTES)


def _dot(a, b):
    return jnp.dot(a, b, preferred_element_type=F32)


def _dot_nt(a, b):
    return lax.dot_general(a, b, (((1,), (1,)), ((), ())), preferred_element_type=F32)


def _dot_tn(a, b):
    return lax.dot_general(a, b, (((0,), (0,)), ((), ())), preferred_element_type=F32)


def _rstd(xf):
    return lax.rsqrt(jnp.mean(xf * xf, axis=-1, keepdims=True) + RMS_EPS)


def _sigmoid(x):
    return 1.0 / (1.0 + jnp.exp(-x))


def _norm_matmul(x, gain, w):
    t = x.shape[0]
    nb, _, tn = w.shape
    tm = min(t, 1024)

    def body(x_ref, g_ref, w_ref, xn_ref, o_ref):
        @pl.when(pl.program_id(1) == 0)
        def _():
            xf = x_ref[...]
            xn_ref[...] = ((xf * _rstd(xf)) * g_ref[...]).astype(BF16)

        o_ref[...] = _dot(xn_ref[...], w_ref[0]).astype(BF16)

    return pl.pallas_call(
        body, name="norm_matmul", grid=(t // tm, nb),
        in_specs=[pl.BlockSpec((tm, D_MODEL), lambda m, n: (m, 0)),
                  pl.BlockSpec((1, D_MODEL), lambda m, n: (0, 0)),
                  pl.BlockSpec((1, D_MODEL, tn), lambda m, n: (n, 0, 0))],
        out_specs=[pl.BlockSpec((tm, D_MODEL), lambda m, n: (m, 0)),
                   pl.BlockSpec((tm, tn), lambda m, n: (m, n))],
        out_shape=[SDS((t, D_MODEL), BF16), SDS((t, nb * tn), BF16)],
        compiler_params=_cparams(),
    )(x, gain, w)


def _layer_a_out(x, z, w_out, kv_gain, b_gain, kv_w, w_in_b):
    t = x.shape[0]
    tm = min(t, 512)
    nb, _, tn = w_in_b.shape

    def body(x_ref, z_ref, wo_ref, kvg_ref, bg_ref, kvw_ref, wb_ref,
             h1_ref, kvn_ref, hb_ref, kv_ref, qg_ref):
        h1 = x_ref[...] + _dot(z_ref[...], wo_ref[...])
        h1_ref[...] = h1
        y0 = h1 * _rstd(h1)
        kvn = (y0 * kvg_ref[...]).astype(BF16)
        hb = (y0 * bg_ref[...]).astype(BF16)
        kvn_ref[...] = kvn
        hb_ref[...] = hb
        kv_ref[...] = _dot(kvn, kvw_ref[...]).astype(BF16)
        for i in range(nb):
            qg_ref[:, i * tn:(i + 1) * tn] = _dot(hb, wb_ref[i]).astype(BF16)

    row = lambda m: (m, 0)
    fix2 = lambda m: (0, 0)
    return pl.pallas_call(
        body, name="layer_a_out", grid=(t // tm,),
        in_specs=[pl.BlockSpec((tm, D_MODEL), row), pl.BlockSpec((tm, D_MODEL), row),
                  pl.BlockSpec((D_MODEL, D_MODEL), fix2),
                  pl.BlockSpec((1, D_MODEL), fix2), pl.BlockSpec((1, D_MODEL), fix2),
                  pl.BlockSpec((D_MODEL, 256), fix2),
                  pl.BlockSpec((nb, D_MODEL, tn), lambda m: (0, 0, 0))],
        out_specs=[pl.BlockSpec((tm, D_MODEL), row), pl.BlockSpec((tm, D_MODEL), row),
                   pl.BlockSpec((tm, D_MODEL), row), pl.BlockSpec((tm, 256), row),
                   pl.BlockSpec((tm, nb * tn), row)],
        out_shape=[SDS((t, D_MODEL), F32), SDS((t, D_MODEL), BF16), SDS((t, D_MODEL), BF16),
                   SDS((t, 256), BF16), SDS((t, nb * tn), BF16)],
        compiler_params=_cparams(),
    )(x, z, w_out, kv_gain, b_gain, kv_w, w_in_b)


def _layer_b_out_loss(h1, z, w_out, f_gain, target):
    t = h1.shape[0]
    tm = min(t, 512)

    def body(h1_ref, z_ref, wo_ref, fg_ref, tgt_ref,
             dh2_ref, dh2b_ref, dz_ref, loss_ref, dfn_ref):
        @pl.when(pl.program_id(0) == 0)
        def _():
            loss_ref[...] = jnp.zeros_like(loss_ref)
            dfn_ref[...] = jnp.zeros_like(dfn_ref)

        h2 = h1_ref[...] + _dot(z_ref[...], wo_ref[...])
        r = _rstd(h2)
        yn = h2 * r
        fg = fg_ref[...]
        err = yn * fg - tgt_ref[...]
        loss_ref[...] += (0.5 / D_MODEL) * jnp.sum(err * err)
        dy = err * (1.0 / D_MODEL)
        dfn_ref[...] += jnp.sum(dy * yn, axis=0, keepdims=True)
        u = dy * fg
        dh2 = r * u - h2 * ((r * r * r) * jnp.mean(u * h2, axis=-1, keepdims=True))
        dh2_ref[...] = dh2
        dh2b = dh2.astype(BF16)
        dh2b_ref[...] = dh2b
        dz_ref[...] = _dot_nt(dh2b, wo_ref[...]).astype(BF16)

    row = lambda m: (m, 0)
    fix2 = lambda m: (0, 0)
    return pl.pallas_call(
        body, name="layer_b_out_loss", grid=(t // tm,),
        in_specs=[pl.BlockSpec((tm, D_MODEL), row), pl.BlockSpec((tm, D_MODEL), row),
                  pl.BlockSpec((D_MODEL, D_MODEL), fix2), pl.BlockSpec((1, D_MODEL), fix2),
                  pl.BlockSpec((tm, D_MODEL), row)],
        out_specs=[pl.BlockSpec((tm, D_MODEL), row), pl.BlockSpec((tm, D_MODEL), row),
                   pl.BlockSpec((tm, D_MODEL), row), pl.BlockSpec((1, 128), fix2),
                   pl.BlockSpec((1, D_MODEL), fix2)],
        out_shape=[SDS((t, D_MODEL), F32), SDS((t, D_MODEL), BF16), SDS((t, D_MODEL), BF16),
                   SDS((1, 128), F32), SDS((1, D_MODEL), F32)],
        compiler_params=_cparams(),
    )(h1, z, w_out, f_gain, target)


def _layer_b_in_bwd(dqg, dkv, w_in_b, kv_w, h1, dh2, b_gain, kv_gain, w_out_a):
    t = h1.shape[0]
    tm = min(t, 256)
    nb, _, tn = w_in_b.shape
    per = D_MODEL // tn

    def body(dqg_ref, dkv_ref, wb_ref, kvw_ref, h1_ref, dh2_ref, bg_ref, kvg_ref, wo_ref,
             dh1_ref, dh1b_ref, dz_ref, dbn_ref, dkn_ref):
        @pl.when(pl.program_id(0) == 0)
        def _():
            dbn_ref[...] = jnp.zeros_like(dbn_ref)
            dkn_ref[...] = jnp.zeros_like(dkn_ref)

        dhb = jnp.zeros((tm, D_MODEL), F32)
        for i in range(nb):
            blk = dqg_ref[i // per, :, (i % per) * tn:(i % per + 1) * tn]
            dhb = dhb + _dot_nt(blk, wb_ref[i])
        dkn = (_dot_nt(dkv_ref[0].astype(BF16), kvw_ref[:, 0:128])
               + _dot_nt(dkv_ref[1].astype(BF16), kvw_ref[:, 128:256]))
        h1 = h1_ref[...]
        r = _rstd(h1)
        xr = h1 * r
        dbn_ref[...] += jnp.sum(dhb * xr, axis=0, keepdims=True)
        dkn_ref[...] += jnp.sum(dkn * xr, axis=0, keepdims=True)
        u = dhb * bg_ref[...] + dkn * kvg_ref[...]
        dh1 = dh2_ref[...] + r * u - h1 * ((r * r * r) * jnp.mean(u * h1, axis=-1, keepdims=True))
        dh1_ref[...] = dh1
        dh1b = dh1.astype(BF16)
        dh1b_ref[...] = dh1b
        dz_ref[...] = _dot_nt(dh1b, wo_ref[...]).astype(BF16)

    row = lambda m: (m, 0)
    fix2 = lambda m: (0, 0)
    return pl.pallas_call(
        body, name="layer_b_in_bwd", grid=(t // tm,),
        in_specs=[pl.BlockSpec((2, tm, D_MODEL), lambda m: (0, m, 0)),
                  pl.BlockSpec((2, tm, 128), lambda m: (0, m, 0)),
                  pl.BlockSpec((nb, D_MODEL, tn), lambda m: (0, 0, 0)),
                  pl.BlockSpec((D_MODEL, 256), fix2),
                  pl.BlockSpec((tm, D_MODEL), row), pl.BlockSpec((tm, D_MODEL), row),
                  pl.BlockSpec((1, D_MODEL), fix2), pl.BlockSpec((1, D_MODEL), fix2),
                  pl.BlockSpec((D_MODEL, D_MODEL), fix2)],
        out_specs=[pl.BlockSpec((tm, D_MODEL), row), pl.BlockSpec((tm, D_MODEL), row),
                   pl.BlockSpec((tm, D_MODEL), row), pl.BlockSpec((1, D_MODEL), fix2),
                   pl.BlockSpec((1, D_MODEL), fix2)],
        out_shape=[SDS((t, D_MODEL), F32), SDS((t, D_MODEL), BF16), SDS((t, D_MODEL), BF16),
                   SDS((1, D_MODEL), F32), SDS((1, D_MODEL), F32)],
        compiler_params=_cparams(),
    )(dqg, dkv, w_in_b, kv_w, h1, dh2, b_gain, kv_gain, w_out_a)


def _layer_a_in_bwd(dqg, dkv, w_in_a, x, dh1, a_gain):
    t = x.shape[0]
    tm = min(t, 256)
    nb, _, tn = w_in_a.shape
    per = D_MODEL // tn

    def body(dqg_ref, dkv_ref, w_ref, x_ref, dh1_ref, ag_ref, dx_ref, dan_ref):
        @pl.when(pl.program_id(0) == 0)
        def _():
            dan_ref[...] = jnp.zeros_like(dan_ref)

        dxn = jnp.zeros((tm, D_MODEL), F32)
        for i in range(nb):
            part = i // per
            src = dqg_ref if part in (0, 3) else dkv_ref
            outer = {0: 0, 3: 1, 1: 0, 2: 1}[part]
            blk = src[outer, :, (i % per) * tn:(i % per + 1) * tn]
            dxn = dxn + _dot_nt(blk, w_ref[i])
        xf = x_ref[...]
        r = _rstd(xf)
        dan_ref[...] += jnp.sum(dxn * (xf * r), axis=0, keepdims=True)
        u = dxn * ag_ref[...]
        dx_ref[...] = dh1_ref[...] + r * u - xf * ((r * r * r) * jnp.mean(u * xf, axis=-1, keepdims=True))

    row = lambda m: (m, 0)
    fix2 = lambda m: (0, 0)
    return pl.pallas_call(
        body, name="layer_a_in_bwd", grid=(t // tm,),
        in_specs=[pl.BlockSpec((2, tm, D_MODEL), lambda m: (0, m, 0)),
                  pl.BlockSpec((2, tm, D_MODEL), lambda m: (0, m, 0)),
                  pl.BlockSpec((nb, D_MODEL, tn), lambda m: (0, 0, 0)),
                  pl.BlockSpec((tm, D_MODEL), row), pl.BlockSpec((tm, D_MODEL), row),
                  pl.BlockSpec((1, D_MODEL), fix2)],
        out_specs=[pl.BlockSpec((tm, D_MODEL), row), pl.BlockSpec((1, D_MODEL), fix2)],
        out_shape=[SDS((t, D_MODEL), F32), SDS((1, D_MODEL), F32)],
        compiler_params=_cparams(),
    )(dqg, dkv, w_in_a, x, dh1, a_gain)


def _lut(s, vals):
    r = jnp.int32(vals[0])
    for i in range(1, len(vals)):
        r = jnp.where(s == i, jnp.int32(vals[i]), r)
    return r


def _weight_grad_cols(name, a, b, steps, tn, prev=None):
    t, dw = a.shape
    outers = [s[0] for s in steps]
    cols = [s[1] for s in steps]
    blks = [s[2] for s in steps]

    def body(*refs):
        if prev is None:
            a_ref, b_ref, o_ref, at_ref = refs
        else:
            a_ref, b_ref, _, o_ref, at_ref = refs

        @pl.when(pl.program_id(0) == 0)
        def _():
            at_ref[...] = a_ref[...].T

        o_ref[0] = _dot(at_ref[...], b_ref[0])

    in_specs = [pl.BlockSpec((t, dw), lambda s: (0, 0)),
                pl.BlockSpec((1, t, tn), lambda s: (_lut(s, outers), 0, _lut(s, cols)))]
    args = [a, b]
    aliases = {}
    if prev is not None:
        in_specs.append(pl.BlockSpec(memory_space=pl.ANY))
        args.append(prev)
        aliases = {2: 0}
    return pl.pallas_call(
        body, name=name, grid=(len(steps),),
        in_specs=in_specs,
        out_specs=pl.BlockSpec((1, dw, tn), lambda s: (_lut(s, blks), 0, 0)),
        out_shape=SDS((N_DEV, dw, tn), F32),
        scratch_shapes=[pltpu.VMEM((dw, t), BF16)],
        input_output_aliases=aliases,
        compiler_params=_cparams(),
    )(*args)


def _weight_grad_rows(name, a, b):
    t, dw = a.shape
    n_o, _, c = b.shape
    rows = dw // N_DEV

    def body(a_ref, b_ref, o_ref):
        at = a_ref[...].T
        for o in range(n_o):
            o_ref[0, :, o * c:(o + 1) * c] = _dot(at, b_ref[o].astype(BF16))

    return pl.pallas_call(
        body, name=name, grid=(N_DEV,),
        in_specs=[pl.BlockSpec((t, rows), lambda s: (0, s)),
                  pl.BlockSpec((n_o, t, c), lambda s: (0, 0, 0))],
        out_specs=pl.BlockSpec((1, rows, n_o * c), lambda s: (s, 0, 0)),
        out_shape=SDS((N_DEV, rows, n_o * c), F32),
        compiler_params=_cparams(),
    )(a, b)


def _lane_lo():
    return lax.broadcasted_iota(jnp.int32, (1, 128), 1) < HEAD_DIM


def _offset_sums(gt):
    keys = gt.shape[1]
    gc = gt[0:CHUNK]
    for cc in range(1, QBLK // CHUNK):
        gc = gc + pltpu.roll(gt[cc * CHUNK:(cc + 1) * CHUNK], keys - cc * CHUNK, 1)
    hi = gc.astype(BF16)
    lo = (gc - hi.astype(F32)).astype(BF16)
    flip = (lax.broadcasted_iota(jnp.int32, (CHUNK, CHUNK), 0)
            + lax.broadcasted_iota(jnp.int32, (CHUNK, CHUNK), 1) == CHUNK - 1).astype(BF16)
    gf = _dot(flip, hi) + _dot(flip, lo)
    skew = pltpu.roll(gf, 0, 1, stride=1, stride_axis=0)
    return jnp.sum(skew, axis=0, keepdims=True)


def _silu_parts(g):
    sg = _sigmoid(g)
    return g * sg, sg * (1.0 + g * (1.0 - sg))


def _a_specs(t):
    nq = t // QBLK
    del nq
    q = pl.BlockSpec((QBLK, 128), lambda p, j: (j, p))
    ks = [pl.BlockSpec((QBLK, 128), lambda p, j, b=b: (jnp.maximum(j - 2 + b, 0), 8 + p)) for b in range(3)]
    vs = [pl.BlockSpec((QBLK, 128), lambda p, j, b=b: (jnp.maximum(j - 2 + b, 0), 16 + p)) for b in range(3)]
    g = pl.BlockSpec((QBLK, 128), lambda p, j: (j, 24 + p))
    bias = pl.BlockSpec((2, QBLK, A_KEYS), lambda p, j: (p, 0, 0))
    return q, ks, vs, g, bias


def _a_probs(q, k, bias, j, hh, lane_lo):
    sel = lane_lo if hh == 0 else jnp.logical_not(lane_lo)
    qm = jnp.where(sel, q, jnp.zeros_like(q)) * SCALE
    s = _dot_nt(qm, k) + bias
    col = lax.broadcasted_iota(jnp.int32, (1, A_KEYS), 1)
    s = jnp.where(col >= QBLK * (2 - j), s, NEG)
    mx = jnp.max(s, axis=-1, keepdims=True)
    e = jnp.exp(s - mx)
    return sel, qm, e, jnp.sum(e, axis=-1, keepdims=True)


def _attn_a_fwd(qkvg, bias):
    t = qkvg.shape[0]
    q_spec, k_specs, v_specs, g_spec, bias_spec = _a_specs(t)

    def body(q_ref, k0, k1, k2, v0, v1, v2, g_ref, b_ref, z_ref, o_ref):
        j = pl.program_id(1)
        lane_lo = _lane_lo()
        q = q_ref[...]
        k = jnp.concatenate([k0[...], k1[...], k2[...]], axis=0)
        v = jnp.concatenate([v0[...], v1[...], v2[...]], axis=0)
        outs = []
        for hh in range(2):
            _, _, e, l = _a_probs(q, k, b_ref[hh], j, hh, lane_lo)
            outs.append(_dot(e.astype(BF16), v) / l)
        o = jnp.where(lane_lo, outs[0], outs[1])
        silu, _ = _silu_parts(g_ref[...].astype(F32))
        o_ref[...] = o.astype(BF16)
        z_ref[...] = (o * silu).astype(BF16)

    out_spec = pl.BlockSpec((QBLK, 128), lambda p, j: (j, p))
    return pl.pallas_call(
        body, name="attn_a_fwd", grid=(N_HEADS // 2, t // QBLK),
        in_specs=[q_spec, *k_specs, *v_specs, g_spec, bias_spec],
        out_specs=[out_spec, out_spec],
        out_shape=[SDS((t, D_MODEL), BF16), SDS((t, D_MODEL), BF16)],
        compiler_params=_cparams(),
    )(qkvg, qkvg, qkvg, qkvg, qkvg, qkvg, qkvg, qkvg, bias)


def _attn_a_bwd(qkvg, bias, out_a, dz):
    t = qkvg.shape[0]
    nq = t // QBLK
    q_spec, k_specs, v_specs, g_spec, bias_spec = _a_specs(t)

    def body(q_ref, k0, k1, k2, v0, v1, v2, g_ref, b_ref, o_ref, dz_ref,
             dqg_ref, dkv_ref, dg_ref, dk_acc, dv_acc, gt_acc):
        j = pl.program_id(1)

        @pl.when(j == 0)
        def _():
            dk_acc[...] = jnp.zeros_like(dk_acc)
            dv_acc[...] = jnp.zeros_like(dv_acc)
            gt_acc[...] = jnp.zeros_like(gt_acc)

        lane_lo = _lane_lo()
        q = q_ref[...]
        k = jnp.concatenate([k0[...], k1[...], k2[...]], axis=0)
        v = jnp.concatenate([v0[...], v1[...], v2[...]], axis=0)
        o = o_ref[...].astype(F32)
        dzf = dz_ref[...].astype(F32)
        silu, dsilu = _silu_parts(g_ref[...].astype(F32))
        do = dzf * silu
        dqg_ref[1] = (dzf * o * dsilu).astype(BF16)
        doo = do * o
        dqs = []
        dk_blk = jnp.zeros((A_KEYS, 128), F32)
        dv_blk = jnp.zeros((A_KEYS, 128), F32)
        for hh in range(2):
            sel, qm, e, l = _a_probs(q, k, b_ref[hh], j, hh, lane_lo)
            p = e / l
            delta = jnp.sum(jnp.where(sel, doo, 0.0), axis=-1, keepdims=True)
            dom = jnp.where(sel, do, 0.0).astype(BF16)
            dp = _dot_nt(dom, v)
            ds = p * (dp - delta)
            gt_acc[hh] += ds
            dsb = ds.astype(BF16)
            dqs.append(_dot(dsb, k) * SCALE)
            dk_blk = dk_blk + _dot_tn(dsb, qm)
            dv_blk = dv_blk + _dot_tn(p.astype(BF16), dom)
        dqg_ref[0] = jnp.where(lane_lo, dqs[0], dqs[1]).astype(BF16)
        for b in range(3):
            @pl.when(j - 2 + b >= 0)
            def _(b=b):
                rows = pl.ds(pl.multiple_of((j - 2 + b) * QBLK, QBLK), QBLK)
                dk_acc[rows, :] += dk_blk[b * QBLK:(b + 1) * QBLK]
                dv_acc[rows, :] += dv_blk[b * QBLK:(b + 1) * QBLK]

        @pl.when(j == nq - 1)
        def _():
            dkv_ref[0] = dk_acc[...].astype(BF16)
            dkv_ref[1] = dv_acc[...].astype(BF16)
            dg_ref[0] = jnp.concatenate([_offset_sums(gt_acc[0]), _offset_sums(gt_acc[1]),
                                         jnp.zeros((6, A_DIAG), F32)], axis=0)

    blk = pl.BlockSpec((QBLK, 128), lambda p, j: (j, p))
    return pl.pallas_call(
        body, name="attn_a_bwd", grid=(N_HEADS // 2, nq),
        in_specs=[q_spec, *k_specs, *v_specs, g_spec, bias_spec, blk, blk],
        out_specs=[pl.BlockSpec((2, QBLK, 128), lambda p, j: (0, j, p)),
                   pl.BlockSpec((2, t, 128), lambda p, j: (0, 0, p)),
                   pl.BlockSpec((1, 8, A_DIAG), lambda p, j: (p, 0, 0))],
        out_shape=[SDS((2, t, D_MODEL), BF16), SDS((2, t, D_MODEL), BF16), SDS((N_HEADS // 2, 8, A_DIAG), F32)],
        scratch_shapes=[pltpu.VMEM((t, 128), F32), pltpu.VMEM((t, 128), F32),
                        pltpu.VMEM((2, QBLK, A_KEYS), F32)],
        compiler_params=_cparams(),
    )(qkvg, qkvg, qkvg, qkvg, qkvg, qkvg, qkvg, qkvg, bias, out_a, dz)


def _b_specs():
    q = pl.BlockSpec((QBLK, 512), lambda h, j: (j, h))
    g = pl.BlockSpec((QBLK, 512), lambda h, j: (j, 2 + h))
    kp = pl.BlockSpec((128, 128), lambda h, j: (jnp.maximum(2 * j - 1, 0), 0))
    kc = pl.BlockSpec((QBLK, 128), lambda h, j: (j, 0))
    vp = pl.BlockSpec((128, 128), lambda h, j: (jnp.maximum(2 * j - 1, 0), 1))
    vc = pl.BlockSpec((QBLK, 128), lambda h, j: (j, 1))
    bias = pl.BlockSpec((B_GROUP, QBLK, B_KEYS), lambda h, j: (h, 0, 0))
    sinks = pl.BlockSpec(memory_space=pltpu.SMEM)
    return q, g, kp, kc, vp, vc, bias, sinks


def _b_operands(kp, kc, vp, vc, kvh):
    k = jnp.concatenate([kp[...], kc[...]], axis=0)
    v = jnp.concatenate([vp[...], vc[...]], axis=0)
    kr = pltpu.roll(k, HEAD_DIM, 1)
    vr = pltpu.roll(v, HEAD_DIM, 1)
    first = kvh == 0
    return (jnp.where(first, k, kr), jnp.where(first, kr, k),
            jnp.where(first, v, vr), jnp.where(first, vr, v))


def _b_probs(qp, kk, bias, sink, j, hh, lane_lo):
    sel = lane_lo if hh == 0 else jnp.logical_not(lane_lo)
    qm = jnp.where(sel, qp, jnp.zeros_like(qp)) * SCALE
    s = _dot_nt(qm, kk) + bias
    col = lax.broadcasted_iota(jnp.int32, (1, B_KEYS), 1)
    s = jnp.where(col >= 128 - QBLK * j, s, NEG)
    mx = jnp.maximum(jnp.max(s, axis=-1, keepdims=True), sink)
    e = jnp.exp(s - mx)
    es = jnp.exp(sink - mx)
    return sel, qm, e, es, jnp.sum(e, axis=-1, keepdims=True) + es


def _attn_b_fwd(qg, kv, bias, sinks):
    t = qg.shape[0]
    q_spec, g_spec, kp_spec, kc_spec, vp_spec, vc_spec, bias_spec, sink_spec = _b_specs()

    def body(q_ref, g_ref, kp, kc, vp, vc, b_ref, sink_ref, z_ref, o_ref):
        kvh = pl.program_id(0)
        j = pl.program_id(1)
        lane_lo = _lane_lo()
        k_lo, k_hi, v_lo, v_hi = _b_operands(kp, kc, vp, vc, kvh)
        for pp in range(B_GROUP // 2):
            cols = slice(128 * pp, 128 * (pp + 1))
            qp = q_ref[:, cols]
            outs = []
            for hh in range(2):
                sink = sink_ref[kvh * B_GROUP + 2 * pp + hh]
                _, _, e, _, l = _b_probs(qp, k_lo if hh == 0 else k_hi, b_ref[2 * pp + hh], sink, j, hh, lane_lo)
                outs.append(_dot(e.astype(BF16), v_lo if hh == 0 else v_hi) / l)
            o = jnp.where(lane_lo, outs[0], outs[1])
            silu, _ = _silu_parts(g_ref[:, cols].astype(F32))
            o_ref[:, cols] = o.astype(BF16)
            z_ref[:, cols] = (o * silu).astype(BF16)

    out_spec = pl.BlockSpec((QBLK, 512), lambda h, j: (j, h))
    return pl.pallas_call(
        body, name="attn_b_fwd", grid=(B_KV_HEADS, t // QBLK),
        in_specs=[q_spec, g_spec, kp_spec, kc_spec, vp_spec, vc_spec, bias_spec, sink_spec],
        out_specs=[out_spec, out_spec],
        out_shape=[SDS((t, D_MODEL), BF16), SDS((t, D_MODEL), BF16)],
        compiler_params=_cparams(),
    )(qg, qg, kv, kv, kv, kv, bias, sinks)


def _attn_b_bwd(qg, kv, bias, sinks, out_b, dz, bucket_onehot):
    t = qg.shape[0]
    nq = t // QBLK
    q_spec, g_spec, kp_spec, kc_spec, vp_spec, vc_spec, bias_spec, sink_spec = _b_specs()

    def body(q_ref, g_ref, kp, kc, vp, vc, b_ref, sink_ref, o_ref, dz_ref, oh_ref,
             dqg_ref, dkv_ref, dt5_ref, dsink_ref, gt_acc):
        kvh = pl.program_id(0)
        j = pl.program_id(1)

        @pl.when(jnp.logical_and(kvh == 0, j == 0))
        def _():
            dkv_ref[...] = jnp.zeros_like(dkv_ref)

        @pl.when(j == 0)
        def _():
            gt_acc[...] = jnp.zeros_like(gt_acc)
            dsink_ref[...] = jnp.zeros_like(dsink_ref)

        lane_lo = _lane_lo()
        k_lo, k_hi, v_lo, v_hi = _b_operands(kp, kc, vp, vc, kvh)
        dk_blk = jnp.zeros((B_KEYS, 128), F32)
        dv_blk = jnp.zeros((B_KEYS, 128), F32)
        for pp in range(B_GROUP // 2):
            cols = slice(128 * pp, 128 * (pp + 1))
            qp = q_ref[:, cols]
            o = o_ref[:, cols].astype(F32)
            dzf = dz_ref[:, cols].astype(F32)
            silu, dsilu = _silu_parts(g_ref[:, cols].astype(F32))
            do = dzf * silu
            dqg_ref[1, :, cols] = (dzf * o * dsilu).astype(BF16)
            doo = do * o
            dqs = []
            for hh in range(2):
                g = 2 * pp + hh
                sink = sink_ref[kvh * B_GROUP + g]
                kk = k_lo if hh == 0 else k_hi
                vv = v_lo if hh == 0 else v_hi
                sel, qm, e, es, l = _b_probs(qp, kk, b_ref[g], sink, j, hh, lane_lo)
                p = e / l
                delta = jnp.sum(jnp.where(sel, doo, 0.0), axis=-1, keepdims=True)
                dom = jnp.where(sel, do, 0.0).astype(BF16)
                dp = _dot_nt(dom, vv)
                ds = p * (dp - delta)
                gt_acc[g] += ds
                dsink_ref[g:g + 1, :] += jnp.broadcast_to(-jnp.sum((es / l) * delta), (1, 128))
                dsb = ds.astype(BF16)
                dqs.append(_dot(dsb, kk) * SCALE)
                dk_blk = dk_blk + _dot_tn(dsb, qm)
                dv_blk = dv_blk + _dot_tn(p.astype(BF16), dom)
            dqg_ref[0, :, cols] = jnp.where(lane_lo, dqs[0], dqs[1]).astype(BF16)
        mine = lane_lo == (kvh == 0)
        dk_add = jnp.where(mine, dk_blk + pltpu.roll(dk_blk, HEAD_DIM, 1), 0.0)
        dv_add = jnp.where(mine, dv_blk + pltpu.roll(dv_blk, HEAD_DIM, 1), 0.0)

        @pl.when(j >= 1)
        def _():
            rows = pl.ds(pl.multiple_of((2 * j - 1) * 128, 128), 128)
            dkv_ref[0, rows, :] += dk_add[0:128]
            dkv_ref[1, rows, :] += dv_add[0:128]

        rows = pl.ds(pl.multiple_of(j * QBLK, QBLK), QBLK)
        dkv_ref[0, rows, :] += dk_add[128:B_KEYS]
        dkv_ref[1, rows, :] += dv_add[128:B_KEYS]

        @pl.when(j == nq - 1)
        def _():
            dd = jnp.concatenate([_offset_sums(gt_acc[g]) for g in range(B_GROUP)], axis=0)
            hi = dd.astype(BF16)
            lo = (dd - hi.astype(F32)).astype(BF16)
            dt5_ref[...] = _dot(hi, oh_ref[...]) + _dot(lo, oh_ref[...])

    blk = pl.BlockSpec((QBLK, 512), lambda h, j: (j, h))
    return pl.pallas_call(
        body, name="attn_b_bwd", grid=(B_KV_HEADS, nq),
        in_specs=[q_spec, g_spec, kp_spec, kc_spec, vp_spec, vc_spec, bias_spec, sink_spec, blk, blk,
                  pl.BlockSpec((B_DIAG, 128), lambda h, j: (0, 0))],
        out_specs=[pl.BlockSpec((2, QBLK, 512), lambda h, j: (0, j, h)),
                   pl.BlockSpec((2, t, 128), lambda h, j: (0, 0, 0)),
                   pl.BlockSpec((B_GROUP, 128), lambda h, j: (h, 0)),
                   pl.BlockSpec((B_GROUP, 128), lambda h, j: (h, 0))],
        out_shape=[SDS((2, t, D_MODEL), BF16), SDS((2, t, 128), F32),
                   SDS((N_HEADS, 128), F32), SDS((N_HEADS, 128), F32)],
        scratch_shapes=[pltpu.VMEM((B_GROUP, QBLK, B_KEYS), F32)],
        compiler_params=_cparams(),
    )(qg, qg, kv, kv, kv, kv, bias, sinks, out_b, dz, bucket_onehot)


def _tile_geometry(keys, band):
    r = np.arange(QBLK)[:, None]
    c = np.arange(keys)[None, :]
    kb = c - CHUNK * (r // CHUNK)
    return r, c, (kb >= 0) & (kb < band)


def _a_bias_tile(rel_bias):
    r, c, valid = _tile_geometry(A_KEYS, A_BAND)
    idx = np.clip(r - c + A_LEFT_CHUNKS * CHUNK, -A_REL_CLIP, A_REL_CLIP) + A_REL_CLIP
    tile = jnp.where(valid[..., None], rel_bias[idx], NEG)
    return jnp.transpose(tile, (2, 0, 1))


def _a_bias_grad(offset_sums):
    first = 319
    tail = jnp.sum(offset_sums[:, :first], axis=1)
    body = jnp.flip(offset_sums[:, first:first + 320], axis=1)
    body = body.at[:, -1].add(tail)
    full = jnp.concatenate([jnp.zeros((N_HEADS, 193), F32), body], axis=1)
    return full.T


def _t5_bucket(rel):
    nb = T5_BUCKETS // 2
    max_exact = nb // 2
    ret = jnp.where(rel > 0, nb, 0)
    n = jnp.abs(rel)
    nf = jnp.maximum(n, 1).astype(jnp.float32)
    large = max_exact + (jnp.log(nf / max_exact) / math.log(T5_MAX_DIST / max_exact)
                         * (nb - max_exact)).astype(jnp.int32)
    large = jnp.minimum(large, nb - 1)
    return ret + jnp.where(n < max_exact, n, large)


def _b_bias_tile(t5_table):
    r, c, valid = _tile_geometry(B_KEYS, B_BAND)
    rel = jnp.asarray(c - r - B_LEFT_CHUNKS * CHUNK, jnp.int32)
    tile = jnp.where(valid[..., None], t5_table[_t5_bucket(rel)], NEG)
    return jnp.transpose(tile, (2, 0, 1))


def _b_bucket_onehot():
    rel = jnp.arange(B_DIAG, dtype=jnp.int32) - (B_LEFT_CHUNKS * CHUNK + CHUNK - 1)
    return (_t5_bucket(rel)[:, None] == jnp.arange(128)[None, :]).astype(BF16)


def _local_step(x, target, a_gain, w_in_a, rel_bias, w_out_a, kv_gain, kv_w, t5_table,
                b_gain, w_in_b, sinks, w_out_b, f_gain):
    a_bias = _a_bias_tile(rel_bias)
    b_bias = _b_bias_tile(t5_table)
    sinks_flat = sinks.reshape(N_HEADS)

    xn, qkvg = _norm_matmul(x, a_gain, w_in_a)
    z_a, out_a = _attn_a_fwd(qkvg, a_bias)
    h1, kvn, hb, kv, qg = _layer_a_out(x, z_a, w_out_a, kv_gain, b_gain, kv_w, w_in_b)
    z_b, out_b = _attn_b_fwd(qg, kv, b_bias, sinks_flat)
    dh2, dh2b, dz_b, loss, d_fn = _layer_b_out_loss(h1, z_b, w_out_b, f_gain, target)

    dqg_b, dkv_b, d_t5, d_sink = _attn_b_bwd(qg, kv, b_bias, sinks_flat, out_b, dz_b, _b_bucket_onehot())
    dh1, dh1b, dz_a, d_bn, d_kn = _layer_b_in_bwd(dqg_b, dkv_b, w_in_b, kv_w, h1, dh2, b_gain, kv_gain, w_out_a)
    dqg_a, dkv_a, d_rel = _attn_a_bwd(qkvg, a_bias, out_a, dz_a)
    grad_x, d_an = _layer_a_in_bwd(dqg_a, dkv_a, w_in_a, x, dh1, a_gain)

    g_w_out_b = _weight_grad_rows("grad_b_w_out", z_b, dh2b[None])
    g_w_in_b = _weight_grad_cols("grad_b_w_in", hb, dqg_b,
                                 [(o, c, 4 * o + c) for o in range(2) for c in range(4)], 256)
    g_kv_w = _weight_grad_rows("grad_kv_w", kvn, dkv_b)
    g_w_out_a = _weight_grad_rows("grad_a_w_out", z_a, dh1b[None])
    g_w_in_a = _weight_grad_cols("grad_a_w_in_qg", xn, dqg_a, [(0, 0, 0), (0, 1, 1), (1, 0, 6), (1, 1, 7)], 512)
    g_w_in_a = _weight_grad_cols("grad_a_w_in_kv", xn, dkv_a, [(0, 0, 2), (0, 1, 3), (1, 0, 4), (1, 1, 5)], 512,
                                 prev=g_w_in_a)

    return dict(
        loss=loss[0, 0], grad_x=grad_x,
        a_norm=d_an, a_w_in=g_w_in_a, a_rel_bias=_a_bias_grad(d_rel[:, :2].reshape(N_HEADS, A_DIAG)),
        a_w_out=g_w_out_a,
        kv_norm=d_kn, kv_w=g_kv_w, t5_bias=d_t5[:, :T5_BUCKETS].T, b_norm=d_bn, b_w_in=g_w_in_b,
        b_sinks=d_sink[:, 0].reshape(1, N_HEADS), b_w_out=g_w_out_b, final_norm=d_fn)


def _place():
    x, y, c = lax.axis_index("x"), lax.axis_index("y"), lax.axis_index("c")
    chips = [(1 - x, y), (x, 1 - y), (1 - x, 1 - y)]
    return x, y, c, chips


def _slot(px, py, pc):
    return 4 * px + 2 * py + pc


ANY = pl.BlockSpec(memory_space=pl.ANY)


def _all_gather(shards):
    n = len(shards)

    def body(*refs):
        ins, outs = refs[:n], refs[n:2 * n]
        send_sems, recv_sems, local_sems = refs[2 * n:]
        x, y, c, chips = _place()
        me, sibling = (x, y, c), (x, y, 1 - c)

        def copy(t, k, block, to, src=None):
            dst = outs[t].at[_slot(*block)]
            return pltpu.make_async_remote_copy(
                src_ref=dst if src is None else src, dst_ref=dst,
                send_sem=send_sems.at[7 * t + k], recv_sem=recv_sems.at[7 * t + k],
                device_id=to, device_id_type=MESH)

        mine = [pltpu.make_async_copy(ins[t], outs[t].at[_slot(*me)], local_sems.at[t]) for t in range(n)]
        for cp in mine:
            cp.start()
        first = []
        for t in range(n):
            first.append(copy(t, 0, me, sibling, src=ins[t]))
            first += [copy(t, 1 + j, me, (*chip, c), src=ins[t]) for j, chip in enumerate(chips)]
        for cp in first:
            cp.start()
        passed = []
        for t in range(n):
            for j, chip in enumerate(chips):
                copy(t, 1 + j, (*chip, c), me).wait_recv()
                cp = copy(t, 4 + j, (*chip, c), sibling)
                cp.start()
                passed.append(cp)
        for t in range(n):
            copy(t, 0, sibling, me).wait_recv()
            for j, chip in enumerate(chips):
                copy(t, 4 + j, (*chip, 1 - c), me).wait_recv()
        for cp in first + passed:
            cp.wait_send()
        for cp in mine:
            cp.wait()

    return pl.pallas_call(
        body, name="all_gather_weights",
        in_specs=[ANY] * n, out_specs=[ANY] * n,
        out_shape=[SDS((N_DEV, *s.shape), s.dtype) for s in shards],
        scratch_shapes=[pltpu.SemaphoreType.DMA((7 * n,)), pltpu.SemaphoreType.DMA((7 * n,)),
                        pltpu.SemaphoreType.DMA((n,))],
    )(*shards)


def _exchange_sibling(grads):
    n = len(grads)

    def body(*refs):
        ins, outs = refs[:n], refs[n:2 * n]
        send_sems, recv_sems = refs[2 * n:]
        x, y, c, chips = _place()
        sibling = (x, y, 1 - c)
        copies = []
        for t in range(n):
            blocks = [(*chip, 1 - c) for chip in chips] + [sibling]
            for k, block in enumerate(blocks):
                copies.append(pltpu.make_async_remote_copy(
                    src_ref=ins[t].at[_slot(*block)], dst_ref=outs[t].at[k],
                    send_sem=send_sems.at[4 * t + k], recv_sem=recv_sems.at[4 * t + k],
                    device_id=sibling, device_id_type=MESH))
        for cp in copies:
            cp.start()
        for cp in copies:
            cp.wait()

    return pl.pallas_call(
        body, name="grads_to_sibling",
        in_specs=[ANY] * n, out_specs=[ANY] * n,
        out_shape=[SDS((4, *g.shape[1:]), g.dtype) for g in grads],
        scratch_shapes=[pltpu.SemaphoreType.DMA((4 * n,)), pltpu.SemaphoreType.DMA((4 * n,))],
    )(*grads)


def _exchange_chips(pre):
    n = len(pre)

    def body(*refs):
        ins, outs = refs[:n], refs[n:2 * n]
        send_sems, recv_sems = refs[2 * n:]
        x, y, c, chips = _place()
        copies = []
        for t in range(n):
            for j, chip in enumerate(chips):
                copies.append(pltpu.make_async_remote_copy(
                    src_ref=ins[t].at[j], dst_ref=outs[t].at[j],
                    send_sem=send_sems.at[3 * t + j], recv_sem=recv_sems.at[3 * t + j],
                    device_id=(*chip, c), device_id_type=MESH))
        for cp in copies:
            cp.start()
        for cp in copies:
            cp.wait()

    return pl.pallas_call(
        body, name="grads_to_chips",
        in_specs=[ANY] * n, out_specs=[ANY] * n,
        out_shape=[SDS(p.shape, p.dtype) for p in pre],
        scratch_shapes=[pltpu.SemaphoreType.DMA((3 * n,)), pltpu.SemaphoreType.DMA((3 * n,))],
    )(*pre)


def _row_tile(rows):
    return min(rows, 256)


def _pre_reduce(name, g, from_sibling, slots):
    _, r, c = g.shape
    tr = _row_tile(r)

    def body(slots_ref, g_ref, s_ref, o_ref):
        del slots_ref
        o_ref[...] = g_ref[...] + s_ref[...]

    return pl.pallas_call(
        body, name=name,
        grid_spec=pltpu.PrefetchScalarGridSpec(
            num_scalar_prefetch=1, grid=(3, r // tr),
            in_specs=[pl.BlockSpec((1, tr, c), lambda j, i, s: (s[j], i, 0)),
                      pl.BlockSpec((1, tr, c), lambda j, i, s: (j, i, 0))],
            out_specs=pl.BlockSpec((1, tr, c), lambda j, i, s: (j, i, 0))),
        out_shape=SDS((3, r, c), F32),
        compiler_params=_cparams(),
    )(slots, g, from_sibling)


def _adamw(w, g, m, v):
    m2 = ADAM_B1 * m + (1.0 - ADAM_B1) * g
    v2 = ADAM_B2 * v + (1.0 - ADAM_B2) * jnp.square(g)
    m_hat = m2 / (1.0 - ADAM_B1 ** ADAM_STEP)
    v_hat = v2 / (1.0 - ADAM_B2 ** ADAM_STEP)
    delta = -ADAM_LR * (m_hat / (jnp.sqrt(v_hat) + ADAM_EPS) + ADAM_WD * w)
    return delta, m2, v2


def _reduce_adamw(name, g, from_sibling, from_chips, my_slot, w, m, v):
    _, r, c = g.shape
    tr = _row_tile(r)

    def body(slot_ref, g_ref, s_ref, c_ref, w_ref, m_ref, v_ref, grad_ref, d_ref, nm_ref, nv_ref):
        del slot_ref
        grad = g_ref[0] + s_ref[0]
        for j in range(3):
            grad = grad + c_ref[j]
        grad_ref[...] = grad
        d_ref[...], nm_ref[...], nv_ref[...] = _adamw(w_ref[...], grad, m_ref[...], v_ref[...])

    flat = pl.BlockSpec((tr, c), lambda i, s: (i, 0))
    return pl.pallas_call(
        body, name=name,
        grid_spec=pltpu.PrefetchScalarGridSpec(
            num_scalar_prefetch=1, grid=(r // tr,),
            in_specs=[pl.BlockSpec((1, tr, c), lambda i, s: (s[0], i, 0)),
                      pl.BlockSpec((1, tr, c), lambda i, s: (3, i, 0)),
                      pl.BlockSpec((3, tr, c), lambda i, s: (0, i, 0)),
                      flat, flat, flat],
            out_specs=[flat, flat, flat, flat]),
        out_shape=[SDS((r, c), F32)] * 4,
        compiler_params=_cparams(),
    )(my_slot, g, from_sibling, from_chips, w, m, v)


_SMALL = (("a_norm", 8), ("a_rel_bias", 65), ("kv_norm", 8), ("t5_bias", 4), ("b_norm", 8),
          ("b_sinks", 1), ("final_norm", 8), ("loss", 1))
_SMALL_ROWS = 104


def _pack_small(parts):
    rows = []
    for name, n_rows in _SMALL:
        flat = parts[name].reshape(-1).astype(F32)
        rows.append(jnp.pad(flat, (0, n_rows * 128 - flat.shape[0])).reshape(n_rows, 128))
    rows.append(jnp.zeros((_SMALL_ROWS - sum(r for _, r in _SMALL), 128), F32))
    return jnp.concatenate(rows, axis=0)


def _unpack_small(buf, shapes):
    out, at = {}, 0
    for name, n_rows in _SMALL:
        size = int(np.prod(shapes[name])) if shapes[name] else 1
        out[name] = buf[at:at + n_rows].reshape(-1)[:size].reshape(shapes[name])
        at += n_rows
    return out


def _small_allreduce_adamw(gbuf, wbuf, mbuf, vbuf):
    def body(g_ref, w_ref, m_ref, v_ref, sum_ref, d_ref, nm_ref, nv_ref, land_ref, send_sems, recv_sems):
        x, y, c, _ = _place()
        my_slot = _slot(x, y, c)
        land_ref[my_slot] = g_ref[...]
        copies = []
        for k in range(1, N_DEV):
            peer = (x ^ (k >> 2), y ^ ((k >> 1) & 1), c ^ (k & 1))
            copies.append(pltpu.make_async_remote_copy(
                src_ref=g_ref, dst_ref=land_ref.at[my_slot],
                send_sem=send_sems.at[k - 1], recv_sem=recv_sems.at[k - 1],
                device_id=peer, device_id_type=MESH))
        for cp in copies:
            cp.start()
        for k in range(1, N_DEV):
            peer_slot = _slot(x ^ (k >> 2), y ^ ((k >> 1) & 1), c ^ (k & 1))
            pltpu.make_async_remote_copy(
                src_ref=g_ref, dst_ref=land_ref.at[peer_slot],
                send_sem=send_sems.at[k - 1], recv_sem=recv_sems.at[k - 1],
                device_id=(x, y, c), device_id_type=MESH).wait_recv()
        for cp in copies:
            cp.wait_send()
        total = land_ref[0]
        for s in range(1, N_DEV):
            total = total + land_ref[s]
        sum_ref[...] = total
        d_ref[...], nm_ref[...], nv_ref[...] = _adamw(w_ref[...], total, m_ref[...], v_ref[...])

    vm = pl.BlockSpec(memory_space=pltpu.VMEM)
    shape = SDS((_SMALL_ROWS, 128), F32)
    return pl.pallas_call(
        body, name="small_allreduce_adamw",
        in_specs=[vm] * 4, out_specs=[vm] * 4, out_shape=[shape] * 4,
        scratch_shapes=[pltpu.VMEM((N_DEV, _SMALL_ROWS, 128), F32),
                        pltpu.SemaphoreType.DMA((N_DEV - 1,)), pltpu.SemaphoreType.DMA((N_DEV - 1,))],
    )(gbuf, wbuf, mbuf, vbuf)


def kernel(x, a_norm, a_w_in, a_rel_bias, a_w_out, kv_norm, kv_w, t5_bias, b_norm, b_w_in, b_sinks, b_w_out, final_norm, loss_target, m_a_norm, m_a_w_in, m_a_rel_bias, m_a_w_out, m_kv_norm, m_kv_w, m_t5_bias, m_b_norm, m_b_w_in, m_b_sinks, m_b_w_out, m_final_norm, v_a_norm, v_a_w_in, v_a_rel_bias, v_a_w_out, v_kv_norm, v_kv_w, v_t5_bias, v_b_norm, v_b_w_in, v_b_sinks, v_b_w_out, v_final_norm):
    xi, yi, ci = lax.axis_index("x"), lax.axis_index("y"), lax.axis_index("c")
    my_slot = _slot(xi, yi, ci)

    w_in_a, w_in_b, w_out_a, w_out_b, kv_full, a_gain = _all_gather([
        a_w_in[0].astype(BF16), b_w_in[0].astype(BF16), a_w_out[0].astype(BF16),
        b_w_out[0].astype(BF16), kv_w.astype(BF16), a_norm])
    w_out_a = w_out_a.reshape(D_MODEL, D_MODEL)
    w_out_b = w_out_b.reshape(D_MODEL, D_MODEL)
    kv_full = kv_full.reshape(D_MODEL, 2 * 128)
    a_gain = a_gain.reshape(1, D_MODEL)

    loc = _local_step(x[0], loss_target[0], a_gain, w_in_a, a_rel_bias[0], w_out_a,
                      kv_norm.reshape(1, D_MODEL), kv_full, t5_bias, b_norm, w_in_b, b_sinks,
                      w_out_b, final_norm.reshape(1, D_MODEL))

    names = ("a_w_in", "b_w_in", "a_w_out", "b_w_out", "kv_w")
    shard_w = dict(a_w_in=a_w_in[0], b_w_in=b_w_in[0], a_w_out=a_w_out[0], b_w_out=b_w_out[0], kv_w=kv_w)
    shard_m = dict(a_w_in=m_a_w_in[0], b_w_in=m_b_w_in[0], a_w_out=m_a_w_out[0], b_w_out=m_b_w_out[0], kv_w=m_kv_w)
    shard_v = dict(a_w_in=v_a_w_in[0], b_w_in=v_b_w_in[0], a_w_out=v_a_w_out[0], b_w_out=v_b_w_out[0], kv_w=v_kv_w)
    grads = [loc[n] for n in names]
    from_sibling = _exchange_sibling(grads)
    forward_slots = jnp.stack([_slot(1 - xi, yi, ci), _slot(xi, 1 - yi, ci), _slot(1 - xi, 1 - yi, ci)]).astype(jnp.int32)
    pre = [_pre_reduce("chip_sum_" + n, g, s, forward_slots) for n, g, s in zip(names, grads, from_sibling)]
    from_chips = _exchange_chips(pre)
    slot_arr = jnp.reshape(my_slot, (1,)).astype(jnp.int32)
    big = {n: _reduce_adamw("adamw_" + n, g, s, f, slot_arr, shard_w[n], shard_m[n], shard_v[n])
           for n, g, s, f in zip(names, grads, from_sibling, from_chips)}

    def own_row(vec):
        return lax.dynamic_update_slice(jnp.zeros((N_DEV, 128), F32), vec, (my_slot, 0))

    zero = jnp.zeros((), F32)
    small_w = dict(a_norm=own_row(a_norm), a_rel_bias=a_rel_bias, kv_norm=kv_norm, t5_bias=t5_bias,
                   b_norm=b_norm, b_sinks=b_sinks, final_norm=final_norm, loss=zero)
    small_m = dict(a_norm=own_row(m_a_norm), a_rel_bias=m_a_rel_bias, kv_norm=m_kv_norm, t5_bias=m_t5_bias,
                   b_norm=m_b_norm, b_sinks=m_b_sinks, final_norm=m_final_norm, loss=zero)
    small_v = dict(a_norm=own_row(v_a_norm), a_rel_bias=v_a_rel_bias, kv_norm=v_kv_norm, t5_bias=v_t5_bias,
                   b_norm=v_b_norm, b_sinks=v_b_sinks, final_norm=v_final_norm, loss=zero)
    small = _small_allreduce_adamw(_pack_small(loc), _pack_small(small_w), _pack_small(small_m), _pack_small(small_v))
    shapes = dict(a_norm=(N_DEV, 128), a_rel_bias=a_rel_bias.shape, kv_norm=kv_norm.shape, t5_bias=t5_bias.shape,
                  b_norm=b_norm.shape, b_sinks=b_sinks.shape, final_norm=final_norm.shape, loss=())
    sm = [_unpack_small(buf, shapes) for buf in small]
    for part in sm:
        part["a_norm"] = lax.dynamic_slice(part["a_norm"], (my_slot, 0), (1, 128))

    order = ("a_norm", "a_w_in", "a_rel_bias", "a_w_out", "kv_norm", "kv_w", "t5_bias", "b_norm",
             "b_w_in", "b_sinks", "b_w_out", "final_norm")
    lead = dict(a_w_in=True, b_w_in=True, a_w_out=True, b_w_out=True, kv_w=False)

    def pick(kind, name):
        if name in big:
            val = big[name][kind]
            return val[None] if lead[name] else val
        return sm[kind][name]

    outs = [sm[0]["loss"], loc["grad_x"][None]]
    for kind in range(4):
        outs += [pick(kind, n) for n in order]
    return tuple(outs)
```

```python
import functools
import math

import numpy as np
import jax
import jax.numpy as jnp
from jax import lax
from jax.experimental import pallas as pl
from jax.experimental.pallas import tpu as pltpu

F32 = jnp.float32
BF16 = jnp.bfloat16
SDS = jax.ShapeDtypeStruct

D_MODEL = 1024
HEAD_DIM = 64
CHUNK = 64
N_HEADS = 16
RMS_EPS = 1e-6
A_LEFT_CHUNKS = 8
A_BAND = (A_LEFT_CHUNKS + 1) * CHUNK
A_REL_CLIP = 256
B_KV_HEADS = 2
B_GROUP = 8
B_LEFT_CHUNKS = 2
B_BAND = (B_LEFT_CHUNKS + 1) * CHUNK
T5_BUCKETS = 32
T5_MAX_DIST = 128
QBLK = 256
A_KEYS = 3 * QBLK
B_KEYS = QBLK + 128
A_DIAG = A_KEYS
B_DIAG = B_KEYS
NEG = -1e30
SCALE = HEAD_DIM ** -0.5
N_DEV = 8

ADAM_LR = 0.001
ADAM_B1 = 0.9
ADAM_B2 = 0.999
ADAM_EPS = 1e-08
ADAM_WD = 0.01
ADAM_STEP = 10

VMEM_LIMIT_BYTES = 56 * 1024 * 1024
MESH = pl.DeviceIdType.MESH


def _cparams():
    return pltpu.CompilerParams(vmem_limit_bytes=VMEM_LIMIT_BYTES)


def _dot(a, b):
    return jnp.dot(a, b, preferred_element_type=F32)


def _dot_nt(a, b):
    return lax.dot_general(a, b, (((1,), (1,)), ((), ())), preferred_element_type=F32)


def _dot_tn(a, b):
    return lax.dot_general(a, b, (((0,), (0,)), ((), ())), preferred_element_type=F32)


def _rstd(xf):
    return lax.rsqrt(jnp.mean(xf * xf, axis=-1, keepdims=True) + RMS_EPS)


def _sigmoid(x):
    return 1.0 / (1.0 + jnp.exp(-x))


def _norm_matmul(x, gain, w):
    t = x.shape[0]
    nb, _, tn = w.shape
    tm = min(t, 1024)

    def body(x_ref, g_ref, w_ref, xn_ref, o_ref):
        @pl.when(pl.program_id(1) == 0)
        def _():
            xf = x_ref[...]
            xn_ref[...] = ((xf * _rstd(xf)) * g_ref[...]).astype(BF16)

        o_ref[...] = _dot(xn_ref[...], w_ref[0]).astype(BF16)

    return pl.pallas_call(
        body, name="norm_matmul", grid=(t // tm, nb),
        in_specs=[pl.BlockSpec((tm, D_MODEL), lambda m, n: (m, 0)),
                  pl.BlockSpec((1, D_MODEL), lambda m, n: (0, 0)),
                  pl.BlockSpec((1, D_MODEL, tn), lambda m, n: (n, 0, 0))],
        out_specs=[pl.BlockSpec((tm, D_MODEL), lambda m, n: (m, 0)),
                   pl.BlockSpec((tm, tn), lambda m, n: (m, n))],
        out_shape=[SDS((t, D_MODEL), BF16), SDS((t, nb * tn), BF16)],
        compiler_params=_cparams(),
    )(x, gain, w)


def _layer_a_out(x, z, w_out, kv_gain, b_gain, kv_w, w_in_b):
    t = x.shape[0]
    tm = min(t, 512)
    nb, _, tn = w_in_b.shape

    def body(x_ref, z_ref, wo_ref, kvg_ref, bg_ref, kvw_ref, wb_ref,
             h1_ref, kvn_ref, hb_ref, kv_ref, qg_ref):
        h1 = x_ref[...] + _dot(z_ref[...], wo_ref[...])
        h1_ref[...] = h1
        y0 = h1 * _rstd(h1)
        kvn = (y0 * kvg_ref[...]).astype(BF16)
        hb = (y0 * bg_ref[...]).astype(BF16)
        kvn_ref[...] = kvn
        hb_ref[...] = hb
        kv_ref[...] = _dot(kvn, kvw_ref[...]).astype(BF16)
        for i in range(nb):
            qg_ref[:, i * tn:(i + 1) * tn] = _dot(hb, wb_ref[i]).astype(BF16)

    row = lambda m: (m, 0)
    fix2 = lambda m: (0, 0)
    return pl.pallas_call(
        body, name="layer_a_out", grid=(t // tm,),
        in_specs=[pl.BlockSpec((tm, D_MODEL), row), pl.BlockSpec((tm, D_MODEL), row),
                  pl.BlockSpec((D_MODEL, D_MODEL), fix2),
                  pl.BlockSpec((1, D_MODEL), fix2), pl.BlockSpec((1, D_MODEL), fix2),
                  pl.BlockSpec((D_MODEL, 256), fix2),
                  pl.BlockSpec((nb, D_MODEL, tn), lambda m: (0, 0, 0))],
        out_specs=[pl.BlockSpec((tm, D_MODEL), row), pl.BlockSpec((tm, D_MODEL), row),
                   pl.BlockSpec((tm, D_MODEL), row), pl.BlockSpec((tm, 256), row),
                   pl.BlockSpec((tm, nb * tn), row)],
        out_shape=[SDS((t, D_MODEL), F32), SDS((t, D_MODEL), BF16), SDS((t, D_MODEL), BF16),
                   SDS((t, 256), BF16), SDS((t, nb * tn), BF16)],
        compiler_params=_cparams(),
    )(x, z, w_out, kv_gain, b_gain, kv_w, w_in_b)


def _layer_b_out_loss(h1, z, w_out, f_gain, target):
    t = h1.shape[0]
    tm = min(t, 512)

    def body(h1_ref, z_ref, wo_ref, fg_ref, tgt_ref,
             dh2_ref, dh2b_ref, dz_ref, loss_ref, dfn_ref):
        @pl.when(pl.program_id(0) == 0)
        def _():
            loss_ref[...] = jnp.zeros_like(loss_ref)
            dfn_ref[...] = jnp.zeros_like(dfn_ref)

        h2 = h1_ref[...] + _dot(z_ref[...], wo_ref[...])
        r = _rstd(h2)
        yn = h2 * r
        fg = fg_ref[...]
        err = yn * fg - tgt_ref[...]
        loss_ref[...] += (0.5 / D_MODEL) * jnp.sum(err * err)
        dy = err * (1.0 / D_MODEL)
        dfn_ref[...] += jnp.sum(dy * yn, axis=0, keepdims=True)
        u = dy * fg
        dh2 = r * u - h2 * ((r * r * r) * jnp.mean(u * h2, axis=-1, keepdims=True))
        dh2_ref[...] = dh2
        dh2b = dh2.astype(BF16)
        dh2b_ref[...] = dh2b
        dz_ref[...] = _dot_nt(dh2b, wo_ref[...]).astype(BF16)

    row = lambda m: (m, 0)
    fix2 = lambda m: (0, 0)
    return pl.pallas_call(
        body, name="layer_b_out_loss", grid=(t // tm,),
        in_specs=[pl.BlockSpec((tm, D_MODEL), row), pl.BlockSpec((tm, D_MODEL), row),
                  pl.BlockSpec((D_MODEL, D_MODEL), fix2), pl.BlockSpec((1, D_MODEL), fix2),
                  pl.BlockSpec((tm, D_MODEL), row)],
        out_specs=[pl.BlockSpec((tm, D_MODEL), row), pl.BlockSpec((tm, D_MODEL), row),
                   pl.BlockSpec((tm, D_MODEL), row), pl.BlockSpec((1, 128), fix2),
                   pl.BlockSpec((1, D_MODEL), fix2)],
        out_shape=[SDS((t, D_MODEL), F32), SDS((t, D_MODEL), BF16), SDS((t, D_MODEL), BF16),
                   SDS((1, 128), F32), SDS((1, D_MODEL), F32)],
        compiler_params=_cparams(),
    )(h1, z, w_out, f_gain, target)


def _layer_b_in_bwd(dqg, dkv, w_in_b, kv_w, h1, dh2, b_gain, kv_gain, w_out_a):
    t = h1.shape[0]
    tm = min(t, 256)
    nb, _, tn = w_in_b.shape
    per = D_MODEL // tn

    def body(dqg_ref, dkv_ref, wb_ref, kvw_ref, h1_ref, dh2_ref, bg_ref, kvg_ref, wo_ref,
             dh1_ref, dh1b_ref, dz_ref, dbn_ref, dkn_ref):
        @pl.when(pl.program_id(0) == 0)
        def _():
            dbn_ref[...] = jnp.zeros_like(dbn_ref)
            dkn_ref[...] = jnp.zeros_like(dkn_ref)

        dhb = jnp.zeros((tm, D_MODEL), F32)
        for i in range(nb):
            blk = dqg_ref[i // per, :, (i % per) * tn:(i % per + 1) * tn]
            dhb = dhb + _dot_nt(blk, wb_ref[i])
        dkn = (_dot_nt(dkv_ref[0].astype(BF16), kvw_ref[:, 0:128])
               + _dot_nt(dkv_ref[1].astype(BF16), kvw_ref[:, 128:256]))
        h1 = h1_ref[...]
        r = _rstd(h1)
        xr = h1 * r
        dbn_ref[...] += jnp.sum(dhb * xr, axis=0, keepdims=True)
        dkn_ref[...] += jnp.sum(dkn * xr, axis=0, keepdims=True)
        u = dhb * bg_ref[...] + dkn * kvg_ref[...]
        dh1 = dh2_ref[...] + r * u - h1 * ((r * r * r) * jnp.mean(u * h1, axis=-1, keepdims=True))
        dh1_ref[...] = dh1
        dh1b = dh1.astype(BF16)
        dh1b_ref[...] = dh1b
        dz_ref[...] = _dot_nt(dh1b, wo_ref[...]).astype(BF16)

    row = lambda m: (m, 0)
    fix2 = lambda m: (0, 0)
    return pl.pallas_call(
        body, name="layer_b_in_bwd", grid=(t // tm,),
        in_specs=[pl.BlockSpec((2, tm, D_MODEL), lambda m: (0, m, 0)),
                  pl.BlockSpec((2, tm, 128), lambda m: (0, m, 0)),
                  pl.BlockSpec((nb, D_MODEL, tn), lambda m: (0, 0, 0)),
                  pl.BlockSpec((D_MODEL, 256), fix2),
                  pl.BlockSpec((tm, D_MODEL), row), pl.BlockSpec((tm, D_MODEL), row),
                  pl.BlockSpec((1, D_MODEL), fix2), pl.BlockSpec((1, D_MODEL), fix2),
                  pl.BlockSpec((D_MODEL, D_MODEL), fix2)],
        out_specs=[pl.BlockSpec((tm, D_MODEL), row), pl.BlockSpec((tm, D_MODEL), row),
                   pl.BlockSpec((tm, D_MODEL), row), pl.BlockSpec((1, D_MODEL), fix2),
                   pl.BlockSpec((1, D_MODEL), fix2)],
        out_shape=[SDS((t, D_MODEL), F32), SDS((t, D_MODEL), BF16), SDS((t, D_MODEL), BF16),
                   SDS((1, D_MODEL), F32), SDS((1, D_MODEL), F32)],
        compiler_params=_cparams(),
    )(dqg, dkv, w_in_b, kv_w, h1, dh2, b_gain, kv_gain, w_out_a)


def _layer_a_in_bwd(dqg, dkv, w_in_a, x, dh1, a_gain):
    t = x.shape[0]
    tm = min(t, 256)
    nb, _, tn = w_in_a.shape
    per = D_MODEL // tn

    def body(dqg_ref, dkv_ref, w_ref, x_ref, dh1_ref, ag_ref, dx_ref, dan_ref):
        @pl.when(pl.program_id(0) == 0)
        def _():
            dan_ref[...] = jnp.zeros_like(dan_ref)

        dxn = jnp.zeros((tm, D_MODEL), F32)
        for i in range(nb):
            part = i // per
            src = dqg_ref if part in (0, 3) else dkv_ref
            outer = {0: 0, 3: 1, 1: 0, 2: 1}[part]
            blk = src[outer, :, (i % per) * tn:(i % per + 1) * tn]
            dxn = dxn + _dot_nt(blk, w_ref[i])
        xf = x_ref[...]
        r = _rstd(xf)
        dan_ref[...] += jnp.sum(dxn * (xf * r), axis=0, keepdims=True)
        u = dxn * ag_ref[...]
        dx_ref[...] = dh1_ref[...] + r * u - xf * ((r * r * r) * jnp.mean(u * xf, axis=-1, keepdims=True))

    row = lambda m: (m, 0)
    fix2 = lambda m: (0, 0)
    return pl.pallas_call(
        body, name="layer_a_in_bwd", grid=(t // tm,),
        in_specs=[pl.BlockSpec((2, tm, D_MODEL), lambda m: (0, m, 0)),
                  pl.BlockSpec((2, tm, D_MODEL), lambda m: (0, m, 0)),
                  pl.BlockSpec((nb, D_MODEL, tn), lambda m: (0, 0, 0)),
                  pl.BlockSpec((tm, D_MODEL), row), pl.BlockSpec((tm, D_MODEL), row),
                  pl.BlockSpec((1, D_MODEL), fix2)],
        out_specs=[pl.BlockSpec((tm, D_MODEL), row), pl.BlockSpec((1, D_MODEL), fix2)],
        out_shape=[SDS((t, D_MODEL), F32), SDS((1, D_MODEL), F32)],
        compiler_params=_cparams(),
    )(dqg, dkv, w_in_a, x, dh1, a_gain)


def _lut(s, vals):
    r = jnp.int32(vals[0])
    for i in range(1, len(vals)):
        r = jnp.where(s == i, jnp.int32(vals[i]), r)
    return r


def _weight_grad_cols(name, a, b, steps, tn, prev=None):
    t, dw = a.shape
    outers = [s[0] for s in steps]
    cols = [s[1] for s in steps]
    blks = [s[2] for s in steps]

    def body(*refs):
        if prev is None:
            a_ref, b_ref, o_ref, at_ref = refs
        else:
            a_ref, b_ref, _, o_ref, at_ref = refs

        @pl.when(pl.program_id(0) == 0)
        def _():
            at_ref[...] = a_ref[...].T

        o_ref[0] = _dot(at_ref[...], b_ref[0])

    in_specs = [pl.BlockSpec((t, dw), lambda s: (0, 0)),
                pl.BlockSpec((1, t, tn), lambda s: (_lut(s, outers), 0, _lut(s, cols)))]
    args = [a, b]
    aliases = {}
    if prev is not None:
        in_specs.append(pl.BlockSpec(memory_space=pl.ANY))
        args.append(prev)
        aliases = {2: 0}
    return pl.pallas_call(
        body, name=name, grid=(len(steps),),
        in_specs=in_specs,
        out_specs=pl.BlockSpec((1, dw, tn), lambda s: (_lut(s, blks), 0, 0)),
        out_shape=SDS((N_DEV, dw, tn), F32),
        scratch_shapes=[pltpu.VMEM((dw, t), BF16)],
        input_output_aliases=aliases,
        compiler_params=_cparams(),
    )(*args)


def _weight_grad_rows(name, a, b):
    t, dw = a.shape
    n_o, _, c = b.shape
    rows = dw // N_DEV

    def body(a_ref, b_ref, o_ref):
        at = a_ref[...].T
        for o in range(n_o):
            o_ref[0, :, o * c:(o + 1) * c] = _dot(at, b_ref[o].astype(BF16))

    return pl.pallas_call(
        body, name=name, grid=(N_DEV,),
        in_specs=[pl.BlockSpec((t, rows), lambda s: (0, s)),
                  pl.BlockSpec((n_o, t, c), lambda s: (0, 0, 0))],
        out_specs=pl.BlockSpec((1, rows, n_o * c), lambda s: (s, 0, 0)),
        out_shape=SDS((N_DEV, rows, n_o * c), F32),
        compiler_params=_cparams(),
    )(a, b)


def _lane_lo():
    return lax.broadcasted_iota(jnp.int32, (1, 128), 1) < HEAD_DIM


def _offset_sums(gt):
    keys = gt.shape[1]
    gc = gt[0:CHUNK]
    for cc in range(1, QBLK // CHUNK):
        gc = gc + pltpu.roll(gt[cc * CHUNK:(cc + 1) * CHUNK], keys - cc * CHUNK, 1)
    hi = gc.astype(BF16)
    lo = (gc - hi.astype(F32)).astype(BF16)
    flip = (lax.broadcasted_iota(jnp.int32, (CHUNK, CHUNK), 0)
            + lax.broadcasted_iota(jnp.int32, (CHUNK, CHUNK), 1) == CHUNK - 1).astype(BF16)
    gf = _dot(flip, hi) + _dot(flip, lo)
    skew = pltpu.roll(gf, 0, 1, stride=1, stride_axis=0)
    return jnp.sum(skew, axis=0, keepdims=True)


def _band_bias(w_row, band):
    keys = w_row.shape[1]
    base = jnp.broadcast_to(w_row, (CHUNK, keys))
    skew = pltpu.roll(base, 0, 1, stride=1, stride_axis=0)
    skew = pltpu.roll(skew, keys - (CHUNK - 1), 1)
    col = lax.broadcasted_iota(jnp.int32, (CHUNK, keys), 1)
    chunk0 = jnp.where(col < band, skew, NEG)
    return jnp.concatenate(
        [chunk0] + [pltpu.roll(chunk0, cc * CHUNK, 1) for cc in range(1, QBLK // CHUNK)], axis=0)


def _silu_parts(g):
    sg = _sigmoid(g)
    return g * sg, sg * (1.0 + g * (1.0 - sg))


def _a_specs(t):
    nq = t // QBLK
    del nq
    q = pl.BlockSpec((QBLK, 128), lambda p, j: (j, p))
    ks = [pl.BlockSpec((QBLK, 128), lambda p, j, b=b: (jnp.maximum(j - 2 + b, 0), 8 + p)) for b in range(3)]
    vs = [pl.BlockSpec((QBLK, 128), lambda p, j, b=b: (jnp.maximum(j - 2 + b, 0), 16 + p)) for b in range(3)]
    g = pl.BlockSpec((QBLK, 128), lambda p, j: (j, 24 + p))
    bias = pl.BlockSpec((1, 8, A_KEYS), lambda p, j: (p, 0, 0))
    return q, ks, vs, g, bias


def _a_fill_bias(w_ref, bias_scr):
    for hh in range(2):
        bias_scr[hh] = _band_bias(w_ref[0, hh:hh + 1, :], A_BAND)


def _a_probs(q, k, bias, j, hh, lane_lo):
    sel = lane_lo if hh == 0 else jnp.logical_not(lane_lo)
    qm = jnp.where(sel, q, jnp.zeros_like(q)) * SCALE
    s = _dot_nt(qm, k) + bias
    col = lax.broadcasted_iota(jnp.int32, (1, A_KEYS), 1)
    s = jnp.where(col >= QBLK * (2 - j), s, NEG)
    mx = jnp.max(s, axis=-1, keepdims=True)
    e = jnp.exp(s - mx)
    return sel, qm, e, jnp.sum(e, axis=-1, keepdims=True)


def _attn_a_fwd(qkvg, bias):
    t = qkvg.shape[0]
    q_spec, k_specs, v_specs, g_spec, bias_spec = _a_specs(t)

    def body(q_ref, k0, k1, k2, v0, v1, v2, g_ref, w_ref, z_ref, o_ref, b_ref):
        j = pl.program_id(1)

        @pl.when(j == 0)
        def _():
            _a_fill_bias(w_ref, b_ref)

        lane_lo = _lane_lo()
        q = q_ref[...]
        k = jnp.concatenate([k0[...], k1[...], k2[...]], axis=0)
        v = jnp.concatenate([v0[...], v1[...], v2[...]], axis=0)
        outs = []
        for hh in range(2):
            _, _, e, l = _a_probs(q, k, b_ref[hh], j, hh, lane_lo)
            outs.append(_dot(e.astype(BF16), v) / l)
        o = jnp.where(lane_lo, outs[0], outs[1])
        silu, _ = _silu_parts(g_ref[...].astype(F32))
        o_ref[...] = o.astype(BF16)
        z_ref[...] = (o * silu).astype(BF16)

    out_spec = pl.BlockSpec((QBLK, 128), lambda p, j: (j, p))
    return pl.pallas_call(
        body, name="attn_a_fwd", grid=(N_HEADS // 2, t // QBLK),
        in_specs=[q_spec, *k_specs, *v_specs, g_spec, bias_spec],
        out_specs=[out_spec, out_spec],
        out_shape=[SDS((t, D_MODEL), BF16), SDS((t, D_MODEL), BF16)],
        scratch_shapes=[pltpu.VMEM((2, QBLK, A_KEYS), F32)],
        compiler_params=_cparams(),
    )(qkvg, qkvg, qkvg, qkvg, qkvg, qkvg, qkvg, qkvg, bias)


def _attn_a_bwd(qkvg, bias, out_a, dz):
    t = qkvg.shape[0]
    nq = t // QBLK
    q_spec, k_specs, v_specs, g_spec, bias_spec = _a_specs(t)

    def body(q_ref, k0, k1, k2, v0, v1, v2, g_ref, w_ref, o_ref, dz_ref,
             dqg_ref, dkv_ref, dg_ref, dk_acc, dv_acc, gt_acc, b_ref):
        j = pl.program_id(1)

        @pl.when(j == 0)
        def _():
            _a_fill_bias(w_ref, b_ref)
            dk_acc[...] = jnp.zeros_like(dk_acc)
            dv_acc[...] = jnp.zeros_like(dv_acc)
            gt_acc[...] = jnp.zeros_like(gt_acc)

        lane_lo = _lane_lo()
        q = q_ref[...]
        k = jnp.concatenate([k0[...], k1[...], k2[...]], axis=0)
        v = jnp.concatenate([v0[...], v1[...], v2[...]], axis=0)
        o = o_ref[...].astype(F32)
        dzf = dz_ref[...].astype(F32)
        silu, dsilu = _silu_parts(g_ref[...].astype(F32))
        do = dzf * silu
        dqg_ref[1] = (dzf * o * dsilu).astype(BF16)
        doo = do * o
        dqs = []
        dk_blk = jnp.zeros((A_KEYS, 128), F32)
        dv_blk = jnp.zeros((A_KEYS, 128), F32)
        for hh in range(2):
            sel, qm, e, l = _a_probs(q, k, b_ref[hh], j, hh, lane_lo)
            p = e / l
            delta = jnp.sum(jnp.where(sel, doo, 0.0), axis=-1, keepdims=True)
            dom = jnp.where(sel, do, 0.0).astype(BF16)
            dp = _dot_nt(dom, v)
            ds = p * (dp - delta)
            gt_acc[hh] += ds
            dsb = ds.astype(BF16)
            dqs.append(_dot(dsb, k) * SCALE)
            dk_blk = dk_blk + _dot_tn(dsb, qm)
            dv_blk = dv_blk + _dot_tn(p.astype(BF16), dom)
        dqg_ref[0] = jnp.where(lane_lo, dqs[0], dqs[1]).astype(BF16)
        for b in range(3):
            @pl.when(j - 2 + b >= 0)
            def _(b=b):
                rows = pl.ds(pl.multiple_of((j - 2 + b) * QBLK, QBLK), QBLK)
                dk_acc[rows, :] += dk_blk[b * QBLK:(b + 1) * QBLK]
                dv_acc[rows, :] += dv_blk[b * QBLK:(b + 1) * QBLK]

        @pl.when(j == nq - 1)
        def _():
            dkv_ref[0] = dk_acc[...].astype(BF16)
            dkv_ref[1] = dv_acc[...].astype(BF16)
            dg_ref[0] = jnp.concatenate([_offset_sums(gt_acc[0]), _offset_sums(gt_acc[1]),
                                         jnp.zeros((6, A_DIAG), F32)], axis=0)

    blk = pl.BlockSpec((QBLK, 128), lambda p, j: (j, p))
    return pl.pallas_call(
        body, name="attn_a_bwd", grid=(N_HEADS // 2, nq),
        in_specs=[q_spec, *k_specs, *v_specs, g_spec, bias_spec, blk, blk],
        out_specs=[pl.BlockSpec((2, QBLK, 128), lambda p, j: (0, j, p)),
                   pl.BlockSpec((2, t, 128), lambda p, j: (0, 0, p)),
                   pl.BlockSpec((1, 8, A_DIAG), lambda p, j: (p, 0, 0))],
        out_shape=[SDS((2, t, D_MODEL), BF16), SDS((2, t, D_MODEL), BF16), SDS((N_HEADS // 2, 8, A_DIAG), F32)],
        scratch_shapes=[pltpu.VMEM((t, 128), F32), pltpu.VMEM((t, 128), F32),
                        pltpu.VMEM((2, QBLK, A_KEYS), F32), pltpu.VMEM((2, QBLK, A_KEYS), F32)],
        compiler_params=_cparams(),
    )(qkvg, qkvg, qkvg, qkvg, qkvg, qkvg, qkvg, qkvg, bias, out_a, dz)


def _b_specs():
    q = pl.BlockSpec((QBLK, 512), lambda h, j: (j, h))
    g = pl.BlockSpec((QBLK, 512), lambda h, j: (j, 2 + h))
    kp = pl.BlockSpec((128, 128), lambda h, j: (jnp.maximum(2 * j - 1, 0), 0))
    kc = pl.BlockSpec((QBLK, 128), lambda h, j: (j, 0))
    vp = pl.BlockSpec((128, 128), lambda h, j: (jnp.maximum(2 * j - 1, 0), 1))
    vc = pl.BlockSpec((QBLK, 128), lambda h, j: (j, 1))
    bias = pl.BlockSpec((B_GROUP, B_KEYS), lambda h, j: (h, 0))
    sinks = pl.BlockSpec(memory_space=pltpu.SMEM)
    return q, g, kp, kc, vp, vc, bias, sinks


def _b_fill_bias(w_ref, bias_scr):
    for g in range(B_GROUP):
        bias_scr[g] = _band_bias(w_ref[g:g + 1, :], B_BAND)


def _b_operands(kp, kc, vp, vc, kvh):
    k = jnp.concatenate([kp[...], kc[...]], axis=0)
    v = jnp.concatenate([vp[...], vc[...]], axis=0)
    kr = pltpu.roll(k, HEAD_DIM, 1)
    vr = pltpu.roll(v, HEAD_DIM, 1)
    first = kvh == 0
    return (jnp.where(first, k, kr), jnp.where(first, kr, k),
            jnp.where(first, v, vr), jnp.where(first, vr, v))


def _b_probs(qp, kk, bias, sink, j, hh, lane_lo):
    sel = lane_lo if hh == 0 else jnp.logical_not(lane_lo)
    qm = jnp.where(sel, qp, jnp.zeros_like(qp)) * SCALE
    s = _dot_nt(qm, kk) + bias
    col = lax.broadcasted_iota(jnp.int32, (1, B_KEYS), 1)
    s = jnp.where(col >= 128 - QBLK * j, s, NEG)
    mx = jnp.maximum(jnp.max(s, axis=-1, keepdims=True), sink)
    e = jnp.exp(s - mx)
    es = jnp.exp(sink - mx)
    return sel, qm, e, es, jnp.sum(e, axis=-1, keepdims=True) + es


def _attn_b_fwd(qg, kv, bias, sinks):
    t = qg.shape[0]
    q_spec, g_spec, kp_spec, kc_spec, vp_spec, vc_spec, bias_spec, sink_spec = _b_specs()

    def body(q_ref, g_ref, kp, kc, vp, vc, w_ref, sink_ref, z_ref, o_ref, b_ref):
        kvh = pl.program_id(0)
        j = pl.program_id(1)

        @pl.when(j == 0)
        def _():
            _b_fill_bias(w_ref, b_ref)

        lane_lo = _lane_lo()
        k_lo, k_hi, v_lo, v_hi = _b_operands(kp, kc, vp, vc, kvh)
        for pp in range(B_GROUP // 2):
            cols = slice(128 * pp, 128 * (pp + 1))
            qp = q_ref[:, cols]
            outs = []
            for hh in range(2):
                sink = sink_ref[kvh * B_GROUP + 2 * pp + hh]
                _, _, e, _, l = _b_probs(qp, k_lo if hh == 0 else k_hi, b_ref[2 * pp + hh], sink, j, hh, lane_lo)
                outs.append(_dot(e.astype(BF16), v_lo if hh == 0 else v_hi) / l)
            o = jnp.where(lane_lo, outs[0], outs[1])
            silu, _ = _silu_parts(g_ref[:, cols].astype(F32))
            o_ref[:, cols] = o.astype(BF16)
            z_ref[:, cols] = (o * silu).astype(BF16)

    out_spec = pl.BlockSpec((QBLK, 512), lambda h, j: (j, h))
    return pl.pallas_call(
        body, name="attn_b_fwd", grid=(B_KV_HEADS, t // QBLK),
        in_specs=[q_spec, g_spec, kp_spec, kc_spec, vp_spec, vc_spec, bias_spec, sink_spec],
        out_specs=[out_spec, out_spec],
        out_shape=[SDS((t, D_MODEL), BF16), SDS((t, D_MODEL), BF16)],
        scratch_shapes=[pltpu.VMEM((B_GROUP, QBLK, B_KEYS), F32)],
        compiler_params=_cparams(),
    )(qg, qg, kv, kv, kv, kv, bias, sinks)


def _attn_b_bwd(qg, kv, bias, sinks, out_b, dz, bucket_onehot):
    t = qg.shape[0]
    nq = t // QBLK
    q_spec, g_spec, kp_spec, kc_spec, vp_spec, vc_spec, bias_spec, sink_spec = _b_specs()

    def body(q_ref, g_ref, kp, kc, vp, vc, w_ref, sink_ref, o_ref, dz_ref, oh_ref,
             dqg_ref, dkv_ref, dt5_ref, dsink_ref, gt_acc, b_ref):
        kvh = pl.program_id(0)
        j = pl.program_id(1)

        @pl.when(jnp.logical_and(kvh == 0, j == 0))
        def _():
            dkv_ref[...] = jnp.zeros_like(dkv_ref)

        @pl.when(j == 0)
        def _():
            _b_fill_bias(w_ref, b_ref)
            gt_acc[...] = jnp.zeros_like(gt_acc)
            dsink_ref[...] = jnp.zeros_like(dsink_ref)

        lane_lo = _lane_lo()
        k_lo, k_hi, v_lo, v_hi = _b_operands(kp, kc, vp, vc, kvh)
        dk_blk = jnp.zeros((B_KEYS, 128), F32)
        dv_blk = jnp.zeros((B_KEYS, 128), F32)
        for pp in range(B_GROUP // 2):
            cols = slice(128 * pp, 128 * (pp + 1))
            qp = q_ref[:, cols]
            o = o_ref[:, cols].astype(F32)
            dzf = dz_ref[:, cols].astype(F32)
            silu, dsilu = _silu_parts(g_ref[:, cols].astype(F32))
            do = dzf * silu
            dqg_ref[1, :, cols] = (dzf * o * dsilu).astype(BF16)
            doo = do * o
            dqs = []
            for hh in range(2):
                g = 2 * pp + hh
                sink = sink_ref[kvh * B_GROUP + g]
                kk = k_lo if hh == 0 else k_hi
                vv = v_lo if hh == 0 else v_hi
                sel, qm, e, es, l = _b_probs(qp, kk, b_ref[g], sink, j, hh, lane_lo)
                p = e / l
                delta = jnp.sum(jnp.where(sel, doo, 0.0), axis=-1, keepdims=True)
                dom = jnp.where(sel, do, 0.0).astype(BF16)
                dp = _dot_nt(dom, vv)
                ds = p * (dp - delta)
                gt_acc[g] += ds
                dsink_ref[g:g + 1, :] += jnp.broadcast_to(-jnp.sum((es / l) * delta), (1, 128))
                dsb = ds.astype(BF16)
                dqs.append(_dot(dsb, kk) * SCALE)
                dk_blk = dk_blk + _dot_tn(dsb, qm)
                dv_blk = dv_blk + _dot_tn(p.astype(BF16), dom)
            dqg_ref[0, :, cols] = jnp.where(lane_lo, dqs[0], dqs[1]).astype(BF16)
        mine = lane_lo == (kvh == 0)
        dk_add = jnp.where(mine, dk_blk + pltpu.roll(dk_blk, HEAD_DIM, 1), 0.0)
        dv_add = jnp.where(mine, dv_blk + pltpu.roll(dv_blk, HEAD_DIM, 1), 0.0)

        @pl.when(j >= 1)
        def _():
            rows = pl.ds(pl.multiple_of((2 * j - 1) * 128, 128), 128)
            dkv_ref[0, rows, :] += dk_add[0:128]
            dkv_ref[1, rows, :] += dv_add[0:128]

        rows = pl.ds(pl.multiple_of(j * QBLK, QBLK), QBLK)
        dkv_ref[0, rows, :] += dk_add[128:B_KEYS]
        dkv_ref[1, rows, :] += dv_add[128:B_KEYS]

        @pl.when(j == nq - 1)
        def _():
            dd = jnp.concatenate([_offset_sums(gt_acc[g]) for g in range(B_GROUP)], axis=0)
            hi = dd.astype(BF16)
            lo = (dd - hi.astype(F32)).astype(BF16)
            dt5_ref[...] = _dot(hi, oh_ref[...]) + _dot(lo, oh_ref[...])

    blk = pl.BlockSpec((QBLK, 512), lambda h, j: (j, h))
    return pl.pallas_call(
        body, name="attn_b_bwd", grid=(B_KV_HEADS, nq),
        in_specs=[q_spec, g_spec, kp_spec, kc_spec, vp_spec, vc_spec, bias_spec, sink_spec, blk, blk,
                  pl.BlockSpec((B_DIAG, 128), lambda h, j: (0, 0))],
        out_specs=[pl.BlockSpec((2, QBLK, 512), lambda h, j: (0, j, h)),
                   pl.BlockSpec((2, t, 128), lambda h, j: (0, 0, 0)),
                   pl.BlockSpec((B_GROUP, 128), lambda h, j: (h, 0)),
                   pl.BlockSpec((B_GROUP, 128), lambda h, j: (h, 0))],
        out_shape=[SDS((2, t, D_MODEL), BF16), SDS((2, t, 128), F32),
                   SDS((N_HEADS, 128), F32), SDS((N_HEADS, 128), F32)],
        scratch_shapes=[pltpu.VMEM((B_GROUP, QBLK, B_KEYS), F32), pltpu.VMEM((B_GROUP, QBLK, B_KEYS), F32)],
        compiler_params=_cparams(),
    )(qg, qg, kv, kv, kv, kv, bias, sinks, out_b, dz, bucket_onehot)


def _a_bias_by_offset(rel_bias):
    m = np.arange(A_DIAG)
    idx = np.clip(A_BAND - 1 - m, -A_REL_CLIP, A_REL_CLIP) + A_REL_CLIP
    by_head = rel_bias[idx].T.reshape(N_HEADS // 2, 2, A_DIAG)
    return jnp.concatenate([by_head, jnp.zeros((N_HEADS // 2, 6, A_DIAG), F32)], axis=1)


def _a_bias_grad(offset_sums):
    first = 319
    tail = jnp.sum(offset_sums[:, :first], axis=1)
    body = jnp.flip(offset_sums[:, first:first + 320], axis=1)
    body = body.at[:, -1].add(tail)
    full = jnp.concatenate([jnp.zeros((N_HEADS, 193), F32), body], axis=1)
    return full.T


def _t5_bucket(rel):
    nb = T5_BUCKETS // 2
    max_exact = nb // 2
    ret = jnp.where(rel > 0, nb, 0)
    n = jnp.abs(rel)
    nf = jnp.maximum(n, 1).astype(jnp.float32)
    large = max_exact + (jnp.log(nf / max_exact) / math.log(T5_MAX_DIST / max_exact)
                         * (nb - max_exact)).astype(jnp.int32)
    large = jnp.minimum(large, nb - 1)
    return ret + jnp.where(n < max_exact, n, large)


def _b_offset_buckets():
    return _t5_bucket(jnp.arange(B_DIAG, dtype=jnp.int32) - (B_LEFT_CHUNKS * CHUNK + CHUNK - 1))


def _b_bias_by_offset(t5_table):
    return t5_table[_b_offset_buckets()].T


def _b_bucket_onehot():
    return (_b_offset_buckets()[:, None] == jnp.arange(128)[None, :]).astype(BF16)


def _local_step(x, target, a_gain, w_in_a, rel_bias, w_out_a, kv_gain, kv_w, t5_table,
                b_gain, w_in_b, sinks, w_out_b, f_gain):
    a_bias = _a_bias_by_offset(rel_bias)
    b_bias = _b_bias_by_offset(t5_table)
    sinks_flat = sinks.reshape(N_HEADS)

    xn, qkvg = _norm_matmul(x, a_gain, w_in_a)
    z_a, out_a = _attn_a_fwd(qkvg, a_bias)
    h1, kvn, hb, kv, qg = _layer_a_out(x, z_a, w_out_a, kv_gain, b_gain, kv_w, w_in_b)
    z_b, out_b = _attn_b_fwd(qg, kv, b_bias, sinks_flat)
    dh2, dh2b, dz_b, loss, d_fn = _layer_b_out_loss(h1, z_b, w_out_b, f_gain, target)

    dqg_b, dkv_b, d_t5, d_sink = _attn_b_bwd(qg, kv, b_bias, sinks_flat, out_b, dz_b, _b_bucket_onehot())
    dh1, dh1b, dz_a, d_bn, d_kn = _layer_b_in_bwd(dqg_b, dkv_b, w_in_b, kv_w, h1, dh2, b_gain, kv_gain, w_out_a)
    dqg_a, dkv_a, d_rel = _attn_a_bwd(qkvg, a_bias, out_a, dz_a)
    grad_x, d_an = _layer_a_in_bwd(dqg_a, dkv_a, w_in_a, x, dh1, a_gain)

    g_w_out_b = _weight_grad_rows("grad_b_w_out", z_b, dh2b[None])
    g_w_in_b = _weight_grad_cols("grad_b_w_in", hb, dqg_b,
                                 [(o, c, 4 * o + c) for o in range(2) for c in range(4)], 256)
    g_kv_w = _weight_grad_rows("grad_kv_w", kvn, dkv_b)
    g_w_out_a = _weight_grad_rows("grad_a_w_out", z_a, dh1b[None])
    g_w_in_a = _weight_grad_cols("grad_a_w_in_qg", xn, dqg_a, [(0, 0, 0), (0, 1, 1), (1, 0, 6), (1, 1, 7)], 512)
    g_w_in_a = _weight_grad_cols("grad_a_w_in_kv", xn, dkv_a, [(0, 0, 2), (0, 1, 3), (1, 0, 4), (1, 1, 5)], 512,
                                 prev=g_w_in_a)

    return dict(
        loss=loss[0, 0], grad_x=grad_x,
        a_norm=d_an, a_w_in=g_w_in_a, a_rel_bias=_a_bias_grad(d_rel[:, :2].reshape(N_HEADS, A_DIAG)),
        a_w_out=g_w_out_a,
        kv_norm=d_kn, kv_w=g_kv_w, t5_bias=d_t5[:, :T5_BUCKETS].T, b_norm=d_bn, b_w_in=g_w_in_b,
        b_sinks=d_sink[:, 0].reshape(1, N_HEADS), b_w_out=g_w_out_b, final_norm=d_fn)


def _place():
    x, y, c = lax.axis_index("x"), lax.axis_index("y"), lax.axis_index("c")
    chips = [(1 - x, y), (x, 1 - y), (1 - x, 1 - y)]
    return x, y, c, chips


def _slot(px, py, pc):
    return 4 * px + 2 * py + pc


ANY = pl.BlockSpec(memory_space=pl.ANY)


def _all_gather(shards):
    n = len(shards)

    def body(*refs):
        ins, outs = refs[:n], refs[n:2 * n]
        send_sems, recv_sems, local_sems = refs[2 * n:]
        x, y, c, chips = _place()
        me, sibling = (x, y, c), (x, y, 1 - c)

        def copy(t, k, block, to, src=None):
            dst = outs[t].at[_slot(*block)]
            return pltpu.make_async_remote_copy(
                src_ref=dst if src is None else src, dst_ref=dst,
                send_sem=send_sems.at[7 * t + k], recv_sem=recv_sems.at[7 * t + k],
                device_id=to, device_id_type=MESH)

        mine = [pltpu.make_async_copy(ins[t], outs[t].at[_slot(*me)], local_sems.at[t]) for t in range(n)]
        for cp in mine:
            cp.start()
        first = []
        for t in range(n):
            first.append(copy(t, 0, me, sibling, src=ins[t]))
            first += [copy(t, 1 + j, me, (*chip, c), src=ins[t]) for j, chip in enumerate(chips)]
        for cp in first:
            cp.start()
        passed = []
        for t in range(n):
            for j, chip in enumerate(chips):
                copy(t, 1 + j, (*chip, c), me).wait_recv()
                cp = copy(t, 4 + j, (*chip, c), sibling)
                cp.start()
                passed.append(cp)
        for t in range(n):
            copy(t, 0, sibling, me).wait_recv()
            for j, chip in enumerate(chips):
                copy(t, 4 + j, (*chip, 1 - c), me).wait_recv()
        for cp in first + passed:
            cp.wait_send()
        for cp in mine:
            cp.wait()

    return pl.pallas_call(
        body, name="all_gather_weights",
        in_specs=[ANY] * n, out_specs=[ANY] * n,
        out_shape=[SDS((N_DEV, *s.shape), s.dtype) for s in shards],
        scratch_shapes=[pltpu.SemaphoreType.DMA((7 * n,)), pltpu.SemaphoreType.DMA((7 * n,)),
                        pltpu.SemaphoreType.DMA((n,))],
    )(*shards)


def _exchange_sibling(grads):
    n = len(grads)

    def body(*refs):
        ins, outs = refs[:n], refs[n:2 * n]
        send_sems, recv_sems = refs[2 * n:]
        x, y, c, chips = _place()
        sibling = (x, y, 1 - c)
        copies = []
        for t in range(n):
            blocks = [(*chip, 1 - c) for chip in chips] + [sibling]
            for k, block in enumerate(blocks):
                copies.append(pltpu.make_async_remote_copy(
                    src_ref=ins[t].at[_slot(*block)], dst_ref=outs[t].at[k],
                    send_sem=send_sems.at[4 * t + k], recv_sem=recv_sems.at[4 * t + k],
                    device_id=sibling, device_id_type=MESH))
        for cp in copies:
            cp.start()
        for cp in copies:
            cp.wait()

    return pl.pallas_call(
        body, name="grads_to_sibling",
        in_specs=[ANY] * n, out_specs=[ANY] * n,
        out_shape=[SDS((4, *g.shape[1:]), g.dtype) for g in grads],
        scratch_shapes=[pltpu.SemaphoreType.DMA((4 * n,)), pltpu.SemaphoreType.DMA((4 * n,))],
    )(*grads)


def _exchange_chips(pre):
    n = len(pre)

    def body(*refs):
        ins, outs = refs[:n], refs[n:2 * n]
        send_sems, recv_sems = refs[2 * n:]
        x, y, c, chips = _place()
        copies = []
        for t in range(n):
            for j, chip in enumerate(chips):
                copies.append(pltpu.make_async_remote_copy(
                    src_ref=ins[t].at[j], dst_ref=outs[t].at[j],
                    send_sem=send_sems.at[3 * t + j], recv_sem=recv_sems.at[3 * t + j],
                    device_id=(*chip, c), device_id_type=MESH))
        for cp in copies:
            cp.start()
        for cp in copies:
            cp.wait()

    return pl.pallas_call(
        body, name="grads_to_chips",
        in_specs=[ANY] * n, out_specs=[ANY] * n,
        out_shape=[SDS(p.shape, p.dtype) for p in pre],
        scratch_shapes=[pltpu.SemaphoreType.DMA((3 * n,)), pltpu.SemaphoreType.DMA((3 * n,))],
    )(*pre)


def _row_tile(rows):
    return min(rows, 256)


def _pre_reduce(name, g, from_sibling, slots):
    _, r, c = g.shape
    tr = _row_tile(r)

    def body(slots_ref, g_ref, s_ref, o_ref):
        del slots_ref
        o_ref[...] = g_ref[...] + s_ref[...]

    return pl.pallas_call(
        body, name=name,
        grid_spec=pltpu.PrefetchScalarGridSpec(
            num_scalar_prefetch=1, grid=(3, r // tr),
            in_specs=[pl.BlockSpec((1, tr, c), lambda j, i, s: (s[j], i, 0)),
                      pl.BlockSpec((1, tr, c), lambda j, i, s: (j, i, 0))],
            out_specs=pl.BlockSpec((1, tr, c), lambda j, i, s: (j, i, 0))),
        out_shape=SDS((3, r, c), F32),
        compiler_params=_cparams(),
    )(slots, g, from_sibling)


def _adamw(w, g, m, v):
    m2 = ADAM_B1 * m + (1.0 - ADAM_B1) * g
    v2 = ADAM_B2 * v + (1.0 - ADAM_B2) * jnp.square(g)
    m_hat = m2 / (1.0 - ADAM_B1 ** ADAM_STEP)
    v_hat = v2 / (1.0 - ADAM_B2 ** ADAM_STEP)
    delta = -ADAM_LR * (m_hat / (jnp.sqrt(v_hat) + ADAM_EPS) + ADAM_WD * w)
    return delta, m2, v2


def _reduce_adamw(name, g, from_sibling, from_chips, my_slot, w, m, v):
    _, r, c = g.shape
    tr = _row_tile(r)

    def body(slot_ref, g_ref, s_ref, c_ref, w_ref, m_ref, v_ref, grad_ref, d_ref, nm_ref, nv_ref):
        del slot_ref
        grad = g_ref[0] + s_ref[0]
        for j in range(3):
            grad = grad + c_ref[j]
        grad_ref[...] = grad
        d_ref[...], nm_ref[...], nv_ref[...] = _adamw(w_ref[...], grad, m_ref[...], v_ref[...])

    flat = pl.BlockSpec((tr, c), lambda i, s: (i, 0))
    return pl.pallas_call(
        body, name=name,
        grid_spec=pltpu.PrefetchScalarGridSpec(
            num_scalar_prefetch=1, grid=(r // tr,),
            in_specs=[pl.BlockSpec((1, tr, c), lambda i, s: (s[0], i, 0)),
                      pl.BlockSpec((1, tr, c), lambda i, s: (3, i, 0)),
                      pl.BlockSpec((3, tr, c), lambda i, s: (0, i, 0)),
                      flat, flat, flat],
            out_specs=[flat, flat, flat, flat]),
        out_shape=[SDS((r, c), F32)] * 4,
        compiler_params=_cparams(),
    )(my_slot, g, from_sibling, from_chips, w, m, v)


_SMALL = (("a_norm", 8), ("a_rel_bias", 72), ("kv_norm", 8), ("t5_bias", 8), ("b_norm", 8),
          ("b_sinks", 8), ("final_norm", 8), ("loss", 8))
_SMALL_ROWS = sum(r for _, r in _SMALL)


def _pack_small(parts):
    rows = []
    for name, n_rows in _SMALL:
        flat = parts[name].reshape(-1).astype(F32)
        rows.append(jnp.pad(flat, (0, n_rows * 128 - flat.shape[0])).reshape(n_rows, 128))
    return jnp.concatenate(rows, axis=0)


def _unpack_small(buf, shapes):
    out, at = {}, 0
    for name, n_rows in _SMALL:
        size = int(np.prod(shapes[name])) if shapes[name] else 1
        out[name] = buf[at:at + n_rows].reshape(-1)[:size].reshape(shapes[name])
        at += n_rows
    return out


def _small_allreduce_adamw(gbuf, wbuf, mbuf, vbuf):
    def body(g_ref, w_ref, m_ref, v_ref, sum_ref, d_ref, nm_ref, nv_ref, land_ref, send_sems, recv_sems):
        x, y, c, _ = _place()
        my_slot = _slot(x, y, c)
        land_ref[my_slot] = g_ref[...]
        copies = []
        for k in range(1, N_DEV):
            peer = (x ^ (k >> 2), y ^ ((k >> 1) & 1), c ^ (k & 1))
            copies.append(pltpu.make_async_remote_copy(
                src_ref=g_ref, dst_ref=land_ref.at[my_slot],
                send_sem=send_sems.at[k - 1], recv_sem=recv_sems.at[k - 1],
                device_id=peer, device_id_type=MESH))
        for cp in copies:
            cp.start()
        for k in range(1, N_DEV):
            peer_slot = _slot(x ^ (k >> 2), y ^ ((k >> 1) & 1), c ^ (k & 1))
            pltpu.make_async_remote_copy(
                src_ref=g_ref, dst_ref=land_ref.at[peer_slot],
                send_sem=send_sems.at[k - 1], recv_sem=recv_sems.at[k - 1],
                device_id=(x, y, c), device_id_type=MESH).wait_recv()
        for cp in copies:
            cp.wait_send()
        total = land_ref[0]
        for s in range(1, N_DEV):
            total = total + land_ref[s]
        sum_ref[...] = total
        d_ref[...], nm_ref[...], nv_ref[...] = _adamw(w_ref[...], total, m_ref[...], v_ref[...])

    vm = pl.BlockSpec(memory_space=pltpu.VMEM)
    shape = SDS((_SMALL_ROWS, 128), F32)
    return pl.pallas_call(
        body, name="small_allreduce_adamw",
        in_specs=[vm] * 4, out_specs=[vm] * 4, out_shape=[shape] * 4,
        scratch_shapes=[pltpu.VMEM((N_DEV, _SMALL_ROWS, 128), F32),
                        pltpu.SemaphoreType.DMA((N_DEV - 1,)), pltpu.SemaphoreType.DMA((N_DEV - 1,))],
    )(gbuf, wbuf, mbuf, vbuf)


def kernel(x, a_norm, a_w_in, a_rel_bias, a_w_out, kv_norm, kv_w, t5_bias, b_norm, b_w_in, b_sinks, b_w_out, final_norm, loss_target, m_a_norm, m_a_w_in, m_a_rel_bias, m_a_w_out, m_kv_norm, m_kv_w, m_t5_bias, m_b_norm, m_b_w_in, m_b_sinks, m_b_w_out, m_final_norm, v_a_norm, v_a_w_in, v_a_rel_bias, v_a_w_out, v_kv_norm, v_kv_w, v_t5_bias, v_b_norm, v_b_w_in, v_b_sinks, v_b_w_out, v_final_norm):
    xi, yi, ci = lax.axis_index("x"), lax.axis_index("y"), lax.axis_index("c")
    my_slot = _slot(xi, yi, ci)

    w_in_a, w_in_b, w_out_a, w_out_b, kv_full, a_gain = _all_gather([
        a_w_in[0].astype(BF16), b_w_in[0].astype(BF16), a_w_out[0].astype(BF16),
        b_w_out[0].astype(BF16), kv_w.astype(BF16), a_norm])
    w_out_a = w_out_a.reshape(D_MODEL, D_MODEL)
    w_out_b = w_out_b.reshape(D_MODEL, D_MODEL)
    kv_full = kv_full.reshape(D_MODEL, 2 * 128)
    a_gain = a_gain.reshape(1, D_MODEL)

    loc = _local_step(x[0], loss_target[0], a_gain, w_in_a, a_rel_bias[0], w_out_a,
                      kv_norm.reshape(1, D_MODEL), kv_full, t5_bias, b_norm, w_in_b, b_sinks,
                      w_out_b, final_norm.reshape(1, D_MODEL))

    names = ("a_w_in", "b_w_in", "a_w_out", "b_w_out", "kv_w")
    shard_w = dict(a_w_in=a_w_in[0], b_w_in=b_w_in[0], a_w_out=a_w_out[0], b_w_out=b_w_out[0], kv_w=kv_w)
    shard_m = dict(a_w_in=m_a_w_in[0], b_w_in=m_b_w_in[0], a_w_out=m_a_w_out[0], b_w_out=m_b_w_out[0], kv_w=m_kv_w)
    shard_v = dict(a_w_in=v_a_w_in[0], b_w_in=v_b_w_in[0], a_w_out=v_a_w_out[0], b_w_out=v_b_w_out[0], kv_w=v_kv_w)
    grads = [loc[n] for n in names]
    from_sibling = _exchange_sibling(grads)
    forward_slots = jnp.stack([_slot(1 - xi, yi, ci), _slot(xi, 1 - yi, ci), _slot(1 - xi, 1 - yi, ci)]).astype(jnp.int32)
    pre = [_pre_reduce("chip_sum_" + n, g, s, forward_slots) for n, g, s in zip(names, grads, from_sibling)]
    from_chips = _exchange_chips(pre)
    slot_arr = jnp.reshape(my_slot, (1,)).astype(jnp.int32)
    big = {n: _reduce_adamw("adamw_" + n, g, s, f, slot_arr, shard_w[n], shard_m[n], shard_v[n])
           for n, g, s, f in zip(names, grads, from_sibling, from_chips)}

    def own_row(vec):
        return lax.dynamic_update_slice(jnp.zeros((N_DEV, 128), F32), vec, (my_slot, 0))

    zero = jnp.zeros((), F32)
    small_w = dict(a_norm=own_row(a_norm), a_rel_bias=a_rel_bias, kv_norm=kv_norm, t5_bias=t5_bias,
                   b_norm=b_norm, b_sinks=b_sinks, final_norm=final_norm, loss=zero)
    small_m = dict(a_norm=own_row(m_a_norm), a_rel_bias=m_a_rel_bias, kv_norm=m_kv_norm, t5_bias=m_t5_bias,
                   b_norm=m_b_norm, b_sinks=m_b_sinks, final_norm=m_final_norm, loss=zero)
    small_v = dict(a_norm=own_row(v_a_norm), a_rel_bias=v_a_rel_bias, kv_norm=v_kv_norm, t5_bias=v_t5_bias,
                   b_norm=v_b_norm, b_sinks=v_b_sinks, final_norm=v_final_norm, loss=zero)
    small = _small_allreduce_adamw(_pack_small(loc), _pack_small(small_w), _pack_small(small_m), _pack_small(small_v))
    shapes = dict(a_norm=(N_DEV, 128), a_rel_bias=a_rel_bias.shape, kv_norm=kv_norm.shape, t5_bias=t5_bias.shape,
                  b_norm=b_norm.shape, b_sinks=b_sinks.shape, final_norm=final_norm.shape, loss=())
    sm = [_unpack_small(buf, shapes) for buf in small]
    for part in sm:
        part["a_norm"] = lax.dynamic_slice(part["a_norm"], (my_slot, 0), (1, 128))

    order = ("a_norm", "a_w_in", "a_rel_bias", "a_w_out", "kv_norm", "kv_w", "t5_bias", "b_norm",
             "b_w_in", "b_sinks", "b_w_out", "final_norm")
    lead = dict(a_w_in=True, b_w_in=True, a_w_out=True, b_w_out=True, kv_w=False)

    def pick(kind, name):
        if name in big:
            val = big[name][kind]
            return val[None] if lead[name] else val
        return sm[kind][name]

    outs = [sm[0]["loss"], loc["grad_x"][None]]
    for kind in range(4):
        outs += [pick(kind, n) for n in order]
    return tuple(outs)
```

```python
import functools
import math

import numpy as np
import jax
import jax.numpy as jnp
from jax import lax
from jax.experimental import pallas as pl
from jax.experimental.pallas import tpu as pltpu

F32 = jnp.float32
BF16 = jnp.bfloat16
SDS = jax.ShapeDtypeStruct

D_MODEL = 1024
HEAD_DIM = 64
CHUNK = 64
N_HEADS = 16
RMS_EPS = 1e-6
A_LEFT_CHUNKS = 8
A_BAND = (A_LEFT_CHUNKS + 1) * CHUNK
A_REL_CLIP = 256
B_KV_HEADS = 2
B_GROUP = 8
B_LEFT_CHUNKS = 2
B_BAND = (B_LEFT_CHUNKS + 1) * CHUNK
T5_BUCKETS = 32
T5_MAX_DIST = 128
QBLK = 256
A_KEYS = 3 * QBLK
B_KEYS = QBLK + 128
A_DIAG = A_KEYS
B_DIAG = B_KEYS
NEG = -1e30
SCALE = HEAD_DIM ** -0.5
N_DEV = 8

ADAM_LR = 0.001
ADAM_B1 = 0.9
ADAM_B2 = 0.999
ADAM_EPS = 1e-08
ADAM_WD = 0.01
ADAM_STEP = 10

VMEM_LIMIT_BYTES = 56 * 1024 * 1024
MESH = pl.DeviceIdType.MESH


def _cparams():
    return pltpu.CompilerParams(vmem_limit_bytes=VMEM_LIMIT_BYTES)


def _dot(a, b):
    return jnp.dot(a, b, preferred_element_type=F32)


def _dot_nt(a, b):
    return lax.dot_general(a, b, (((1,), (1,)), ((), ())), preferred_element_type=F32)


def _dot_tn(a, b):
    return lax.dot_general(a, b, (((0,), (0,)), ((), ())), preferred_element_type=F32)


def _rstd(xf):
    return lax.rsqrt(jnp.mean(xf * xf, axis=-1, keepdims=True) + RMS_EPS)


def _sigmoid(x):
    return 1.0 / (1.0 + jnp.exp(-x))


def _norm_matmul(x, gain, w):
    t = x.shape[0]
    nb, _, tn = w.shape
    tm = min(t, 1024)

    def body(x_ref, g_ref, w_ref, xn_ref, o_ref):
        @pl.when(pl.program_id(1) == 0)
        def _():
            xf = x_ref[...]
            xn_ref[...] = ((xf * _rstd(xf)) * g_ref[...]).astype(BF16)

        o_ref[...] = _dot(xn_ref[...], w_ref[0]).astype(BF16)

    return pl.pallas_call(
        body, name="norm_matmul", grid=(t // tm, nb),
        in_specs=[pl.BlockSpec((tm, D_MODEL), lambda m, n: (m, 0)),
                  pl.BlockSpec((1, D_MODEL), lambda m, n: (0, 0)),
                  pl.BlockSpec((1, D_MODEL, tn), lambda m, n: (n, 0, 0))],
        out_specs=[pl.BlockSpec((tm, D_MODEL), lambda m, n: (m, 0)),
                   pl.BlockSpec((tm, tn), lambda m, n: (m, n))],
        out_shape=[SDS((t, D_MODEL), BF16), SDS((t, nb * tn), BF16)],
        compiler_params=_cparams(),
    )(x, gain, w)


def _layer_a_out(x, z, w_out, kv_gain, b_gain, kv_w, w_in_b):
    t = x.shape[0]
    tm = min(t, 512)
    nb, _, tn = w_in_b.shape

    def body(x_ref, z_ref, wo_ref, kvg_ref, bg_ref, kvw_ref, wb_ref,
             h1_ref, kvn_ref, hb_ref, kv_ref, qg_ref):
        h1 = x_ref[...] + _dot(z_ref[...], wo_ref[...])
        h1_ref[...] = h1
        y0 = h1 * _rstd(h1)
        kvn = (y0 * kvg_ref[...]).astype(BF16)
        hb = (y0 * bg_ref[...]).astype(BF16)
        kvn_ref[...] = kvn
        hb_ref[...] = hb
        kv_ref[...] = _dot(kvn, kvw_ref[...]).astype(BF16)
        for i in range(nb):
            qg_ref[:, i * tn:(i + 1) * tn] = _dot(hb, wb_ref[i]).astype(BF16)

    row = lambda m: (m, 0)
    fix2 = lambda m: (0, 0)
    return pl.pallas_call(
        body, name="layer_a_out", grid=(t // tm,),
        in_specs=[pl.BlockSpec((tm, D_MODEL), row), pl.BlockSpec((tm, D_MODEL), row),
                  pl.BlockSpec((D_MODEL, D_MODEL), fix2),
                  pl.BlockSpec((1, D_MODEL), fix2), pl.BlockSpec((1, D_MODEL), fix2),
                  pl.BlockSpec((D_MODEL, 256), fix2),
                  pl.BlockSpec((nb, D_MODEL, tn), lambda m: (0, 0, 0))],
        out_specs=[pl.BlockSpec((tm, D_MODEL), row), pl.BlockSpec((tm, D_MODEL), row),
                   pl.BlockSpec((tm, D_MODEL), row), pl.BlockSpec((tm, 256), row),
                   pl.BlockSpec((tm, nb * tn), row)],
        out_shape=[SDS((t, D_MODEL), F32), SDS((t, D_MODEL), BF16), SDS((t, D_MODEL), BF16),
                   SDS((t, 256), BF16), SDS((t, nb * tn), BF16)],
        compiler_params=_cparams(),
    )(x, z, w_out, kv_gain, b_gain, kv_w, w_in_b)


def _layer_b_out_loss(h1, z, w_out, f_gain, target):
    t = h1.shape[0]
    tm = min(t, 512)

    def body(h1_ref, z_ref, wo_ref, fg_ref, tgt_ref,
             dh2_ref, dh2b_ref, dz_ref, loss_ref, dfn_ref):
        @pl.when(pl.program_id(0) == 0)
        def _():
            loss_ref[...] = jnp.zeros_like(loss_ref)
            dfn_ref[...] = jnp.zeros_like(dfn_ref)

        h2 = h1_ref[...] + _dot(z_ref[...], wo_ref[...])
        r = _rstd(h2)
        yn = h2 * r
        fg = fg_ref[...]
        err = yn * fg - tgt_ref[...]
        loss_ref[...] += (0.5 / D_MODEL) * jnp.sum(err * err)
        dy = err * (1.0 / D_MODEL)
        dfn_ref[...] += jnp.sum(dy * yn, axis=0, keepdims=True)
        u = dy * fg
        dh2 = r * u - h2 * ((r * r * r) * jnp.mean(u * h2, axis=-1, keepdims=True))
        dh2_ref[...] = dh2
        dh2b = dh2.astype(BF16)
        dh2b_ref[...] = dh2b
        dz_ref[...] = _dot_nt(dh2b, wo_ref[...]).astype(BF16)

    row = lambda m: (m, 0)
    fix2 = lambda m: (0, 0)
    return pl.pallas_call(
        body, name="layer_b_out_loss", grid=(t // tm,),
        in_specs=[pl.BlockSpec((tm, D_MODEL), row), pl.BlockSpec((tm, D_MODEL), row),
                  pl.BlockSpec((D_MODEL, D_MODEL), fix2), pl.BlockSpec((1, D_MODEL), fix2),
                  pl.BlockSpec((tm, D_MODEL), row)],
        out_specs=[pl.BlockSpec((tm, D_MODEL), row), pl.BlockSpec((tm, D_MODEL), row),
                   pl.BlockSpec((tm, D_MODEL), row), pl.BlockSpec((1, 128), fix2),
                   pl.BlockSpec((1, D_MODEL), fix2)],
        out_shape=[SDS((t, D_MODEL), F32), SDS((t, D_MODEL), BF16), SDS((t, D_MODEL), BF16),
                   SDS((1, 128), F32), SDS((1, D_MODEL), F32)],
        compiler_params=_cparams(),
    )(h1, z, w_out, f_gain, target)


def _layer_b_in_bwd(dqg, dkv, w_in_b, kv_w, h1, dh2, b_gain, kv_gain, w_out_a):
    t = h1.shape[0]
    tm = min(t, 256)
    nb, _, tn = w_in_b.shape
    per = D_MODEL // tn

    def body(dqg_ref, dkv_ref, wb_ref, kvw_ref, h1_ref, dh2_ref, bg_ref, kvg_ref, wo_ref,
             dh1_ref, dh1b_ref, dz_ref, dbn_ref, dkn_ref):
        @pl.when(pl.program_id(0) == 0)
        def _():
            dbn_ref[...] = jnp.zeros_like(dbn_ref)
            dkn_ref[...] = jnp.zeros_like(dkn_ref)

        dhb = jnp.zeros((tm, D_MODEL), F32)
        for i in range(nb):
            blk = dqg_ref[i // per, :, (i % per) * tn:(i % per + 1) * tn]
            dhb = dhb + _dot_nt(blk, wb_ref[i])
        dkn = (_dot_nt(dkv_ref[0].astype(BF16), kvw_ref[:, 0:128])
               + _dot_nt(dkv_ref[1].astype(BF16), kvw_ref[:, 128:256]))
        h1 = h1_ref[...]
        r = _rstd(h1)
        xr = h1 * r
        dbn_ref[...] += jnp.sum(dhb * xr, axis=0, keepdims=True)
        dkn_ref[...] += jnp.sum(dkn * xr, axis=0, keepdims=True)
        u = dhb * bg_ref[...] + dkn * kvg_ref[...]
        dh1 = dh2_ref[...] + r * u - h1 * ((r * r * r) * jnp.mean(u * h1, axis=-1, keepdims=True))
        dh1_ref[...] = dh1
        dh1b = dh1.astype(BF16)
        dh1b_ref[...] = dh1b
        dz_ref[...] = _dot_nt(dh1b, wo_ref[...]).astype(BF16)

    row = lambda m: (m, 0)
    fix2 = lambda m: (0, 0)
    return pl.pallas_call(
        body, name="layer_b_in_bwd", grid=(t // tm,),
        in_specs=[pl.BlockSpec((2, tm, D_MODEL), lambda m: (0, m, 0)),
                  pl.BlockSpec((2, tm, 128), lambda m: (0, m, 0)),
                  pl.BlockSpec((nb, D_MODEL, tn), lambda m: (0, 0, 0)),
                  pl.BlockSpec((D_MODEL, 256), fix2),
                  pl.BlockSpec((tm, D_MODEL), row), pl.BlockSpec((tm, D_MODEL), row),
                  pl.BlockSpec((1, D_MODEL), fix2), pl.BlockSpec((1, D_MODEL), fix2),
                  pl.BlockSpec((D_MODEL, D_MODEL), fix2)],
        out_specs=[pl.BlockSpec((tm, D_MODEL), row), pl.BlockSpec((tm, D_MODEL), row),
                   pl.BlockSpec((tm, D_MODEL), row), pl.BlockSpec((1, D_MODEL), fix2),
                   pl.BlockSpec((1, D_MODEL), fix2)],
        out_shape=[SDS((t, D_MODEL), F32), SDS((t, D_MODEL), BF16), SDS((t, D_MODEL), BF16),
                   SDS((1, D_MODEL), F32), SDS((1, D_MODEL), F32)],
        compiler_params=_cparams(),
    )(dqg, dkv, w_in_b, kv_w, h1, dh2, b_gain, kv_gain, w_out_a)


def _layer_a_in_bwd(dqg, dkv, w_in_a, x, dh1, a_gain):
    t = x.shape[0]
    tm = min(t, 256)
    nb, _, tn = w_in_a.shape
    per = D_MODEL // tn

    def body(dqg_ref, dkv_ref, w_ref, x_ref, dh1_ref, ag_ref, dx_ref, dan_ref):
        @pl.when(pl.program_id(0) == 0)
        def _():
            dan_ref[...] = jnp.zeros_like(dan_ref)

        dxn = jnp.zeros((tm, D_MODEL), F32)
        for i in range(nb):
            part = i // per
            src = dqg_ref if part in (0, 3) else dkv_ref
            outer = {0: 0, 3: 1, 1: 0, 2: 1}[part]
            blk = src[outer, :, (i % per) * tn:(i % per + 1) * tn]
            dxn = dxn + _dot_nt(blk, w_ref[i])
        xf = x_ref[...]
        r = _rstd(xf)
        dan_ref[...] += jnp.sum(dxn * (xf * r), axis=0, keepdims=True)
        u = dxn * ag_ref[...]
        dx_ref[...] = dh1_ref[...] + r * u - xf * ((r * r * r) * jnp.mean(u * xf, axis=-1, keepdims=True))

    row = lambda m: (m, 0)
    fix2 = lambda m: (0, 0)
    return pl.pallas_call(
        body, name="layer_a_in_bwd", grid=(t // tm,),
        in_specs=[pl.BlockSpec((2, tm, D_MODEL), lambda m: (0, m, 0)),
                  pl.BlockSpec((2, tm, D_MODEL), lambda m: (0, m, 0)),
                  pl.BlockSpec((nb, D_MODEL, tn), lambda m: (0, 0, 0)),
                  pl.BlockSpec((tm, D_MODEL), row), pl.BlockSpec((tm, D_MODEL), row),
                  pl.BlockSpec((1, D_MODEL), fix2)],
        out_specs=[pl.BlockSpec((tm, D_MODEL), row), pl.BlockSpec((1, D_MODEL), fix2)],
        out_shape=[SDS((t, D_MODEL), F32), SDS((1, D_MODEL), F32)],
        compiler_params=_cparams(),
    )(dqg, dkv, w_in_a, x, dh1, a_gain)


def _lut(s, vals):
    r = jnp.int32(vals[0])
    for i in range(1, len(vals)):
        r = jnp.where(s == i, jnp.int32(vals[i]), r)
    return r


def _weight_grad_cols(name, a, b, steps, tn, prev=None):
    t, dw = a.shape
    outers = [s[0] for s in steps]
    cols = [s[1] for s in steps]
    blks = [s[2] for s in steps]

    def body(*refs):
        if prev is None:
            a_ref, b_ref, o_ref, at_ref = refs
        else:
            a_ref, b_ref, _, o_ref, at_ref = refs

        @pl.when(pl.program_id(0) == 0)
        def _():
            at_ref[...] = a_ref[...].T

        o_ref[0] = _dot(at_ref[...], b_ref[0])

    in_specs = [pl.BlockSpec((t, dw), lambda s: (0, 0)),
                pl.BlockSpec((1, t, tn), lambda s: (_lut(s, outers), 0, _lut(s, cols)))]
    args = [a, b]
    aliases = {}
    if prev is not None:
        in_specs.append(pl.BlockSpec(memory_space=pl.ANY))
        args.append(prev)
        aliases = {2: 0}
    return pl.pallas_call(
        body, name=name, grid=(len(steps),),
        in_specs=in_specs,
        out_specs=pl.BlockSpec((1, dw, tn), lambda s: (_lut(s, blks), 0, 0)),
        out_shape=SDS((N_DEV, dw, tn), F32),
        scratch_shapes=[pltpu.VMEM((dw, t), BF16)],
        input_output_aliases=aliases,
        compiler_params=_cparams(),
    )(*args)


def _weight_grad_rows(name, a, b):
    t, dw = a.shape
    n_o, _, c = b.shape
    rows = dw // N_DEV

    def body(a_ref, b_ref, o_ref):
        at = a_ref[...].T
        for o in range(n_o):
            o_ref[0, :, o * c:(o + 1) * c] = _dot(at, b_ref[o].astype(BF16))

    return pl.pallas_call(
        body, name=name, grid=(N_DEV,),
        in_specs=[pl.BlockSpec((t, rows), lambda s: (0, s)),
                  pl.BlockSpec((n_o, t, c), lambda s: (0, 0, 0))],
        out_specs=pl.BlockSpec((1, rows, n_o * c), lambda s: (s, 0, 0)),
        out_shape=SDS((N_DEV, rows, n_o * c), F32),
        compiler_params=_cparams(),
    )(a, b)


def _lane_lo():
    return lax.broadcasted_iota(jnp.int32, (1, 128), 1) < HEAD_DIM


def _offset_sums(gt):
    keys = gt.shape[1]
    gc = gt[0:CHUNK]
    for cc in range(1, QBLK // CHUNK):
        gc = gc + pltpu.roll(gt[cc * CHUNK:(cc + 1) * CHUNK], keys - cc * CHUNK, 1)
    hi = gc.astype(BF16)
    lo = (gc - hi.astype(F32)).astype(BF16)
    flip = (lax.broadcasted_iota(jnp.int32, (CHUNK, CHUNK), 0)
            + lax.broadcasted_iota(jnp.int32, (CHUNK, CHUNK), 1) == CHUNK - 1).astype(BF16)
    gf = _dot(flip, hi) + _dot(flip, lo)
    skew = pltpu.roll(gf, 0, 1, stride=1, stride_axis=0)
    return jnp.sum(skew, axis=0, keepdims=True)


def _band_bias(w_row, band):
    keys = w_row.shape[1]
    base = jnp.broadcast_to(w_row, (CHUNK, keys))
    skew = pltpu.roll(base, 0, 1, stride=1, stride_axis=0)
    skew = pltpu.roll(skew, keys - (CHUNK - 1), 1)
    col = lax.broadcasted_iota(jnp.int32, (CHUNK, keys), 1)
    chunk0 = jnp.where(col < band, skew, NEG)
    return jnp.concatenate(
        [chunk0] + [pltpu.roll(chunk0, cc * CHUNK, 1) for cc in range(1, QBLK // CHUNK)], axis=0)


def _silu_parts(g):
    sg = _sigmoid(g)
    return g * sg, sg * (1.0 + g * (1.0 - sg))


def _a_specs(t):
    nq = t // QBLK
    del nq
    q = pl.BlockSpec((QBLK, 128), lambda p, j: (j, p))
    ks = [pl.BlockSpec((QBLK, 128), lambda p, j, b=b: (jnp.maximum(j - 2 + b, 0), 8 + p)) for b in range(3)]
    vs = [pl.BlockSpec((QBLK, 128), lambda p, j, b=b: (jnp.maximum(j - 2 + b, 0), 16 + p)) for b in range(3)]
    g = pl.BlockSpec((QBLK, 128), lambda p, j: (j, 24 + p))
    bias = pl.BlockSpec((1, 8, A_KEYS), lambda p, j: (p, 0, 0))
    return q, ks, vs, g, bias


def _a_fill_bias(w_ref, bias_scr):
    for hh in range(2):
        bias_scr[hh] = _band_bias(w_ref[0, hh:hh + 1, :], A_BAND)


def _a_probs(q, k, bias, j, hh, lane_lo):
    sel = lane_lo if hh == 0 else jnp.logical_not(lane_lo)
    qm = jnp.where(sel, q, jnp.zeros_like(q)) * SCALE
    s = _dot_nt(qm, k) + bias
    col = lax.broadcasted_iota(jnp.int32, (1, A_KEYS), 1)
    s = jnp.where(col >= QBLK * (2 - j), s, NEG)
    mx = jnp.max(s, axis=-1, keepdims=True)
    e = jnp.exp(s - mx)
    return sel, qm, e, jnp.sum(e, axis=-1, keepdims=True)


def _attn_a_fwd(qkvg, bias):
    t = qkvg.shape[0]
    q_spec, k_specs, v_specs, g_spec, bias_spec = _a_specs(t)

    def body(q_ref, k0, k1, k2, v0, v1, v2, g_ref, w_ref, z_ref, o_ref, b_ref):
        j = pl.program_id(1)

        @pl.when(j == 0)
        def _():
            _a_fill_bias(w_ref, b_ref)

        lane_lo = _lane_lo()
        q = q_ref[...]
        k = jnp.concatenate([k0[...], k1[...], k2[...]], axis=0)
        v = jnp.concatenate([v0[...], v1[...], v2[...]], axis=0)
        outs = []
        for hh in range(2):
            _, _, e, l = _a_probs(q, k, b_ref[hh], j, hh, lane_lo)
            outs.append(_dot(e.astype(BF16), v) / l)
        o = jnp.where(lane_lo, outs[0], outs[1])
        silu, _ = _silu_parts(g_ref[...].astype(F32))
        o_ref[...] = o.astype(BF16)
        z_ref[...] = (o * silu).astype(BF16)

    out_spec = pl.BlockSpec((QBLK, 128), lambda p, j: (j, p))
    return pl.pallas_call(
        body, name="attn_a_fwd", grid=(N_HEADS // 2, t // QBLK),
        in_specs=[q_spec, *k_specs, *v_specs, g_spec, bias_spec],
        out_specs=[out_spec, out_spec],
        out_shape=[SDS((t, D_MODEL), BF16), SDS((t, D_MODEL), BF16)],
        scratch_shapes=[pltpu.VMEM((2, QBLK, A_KEYS), F32)],
        compiler_params=_cparams(),
    )(qkvg, qkvg, qkvg, qkvg, qkvg, qkvg, qkvg, qkvg, bias)


def _attn_a_bwd(qkvg, bias, out_a, dz):
    t = qkvg.shape[0]
    nq = t // QBLK
    q_spec, k_specs, v_specs, g_spec, bias_spec = _a_specs(t)

    def body(q_ref, k0, k1, k2, v0, v1, v2, g_ref, w_ref, o_ref, dz_ref,
             dqg_ref, dkv_ref, dg_ref, dk_acc, dv_acc, gt_acc, b_ref):
        j = pl.program_id(1)

        @pl.when(j == 0)
        def _():
            _a_fill_bias(w_ref, b_ref)
            dk_acc[...] = jnp.zeros_like(dk_acc)
            dv_acc[...] = jnp.zeros_like(dv_acc)
            gt_acc[...] = jnp.zeros_like(gt_acc)

        lane_lo = _lane_lo()
        q = q_ref[...]
        k = jnp.concatenate([k0[...], k1[...], k2[...]], axis=0)
        v = jnp.concatenate([v0[...], v1[...], v2[...]], axis=0)
        o = o_ref[...].astype(F32)
        dzf = dz_ref[...].astype(F32)
        silu, dsilu = _silu_parts(g_ref[...].astype(F32))
        do = dzf * silu
        dqg_ref[1] = (dzf * o * dsilu).astype(BF16)
        doo = do * o
        dqs = []
        dk_blk = jnp.zeros((A_KEYS, 128), F32)
        dv_blk = jnp.zeros((A_KEYS, 128), F32)
        for hh in range(2):
            sel, qm, e, l = _a_probs(q, k, b_ref[hh], j, hh, lane_lo)
            p = e / l
            delta = jnp.sum(jnp.where(sel, doo, 0.0), axis=-1, keepdims=True)
            dom = jnp.where(sel, do, 0.0).astype(BF16)
            dp = _dot_nt(dom, v)
            ds = p * (dp - delta)
            gt_acc[hh] += ds
            dsb = ds.astype(BF16)
            dqs.append(_dot(dsb, k) * SCALE)
            dk_blk = dk_blk + _dot_tn(dsb, qm)
            dv_blk = dv_blk + _dot_tn(p.astype(BF16), dom)
        dqg_ref[0] = jnp.where(lane_lo, dqs[0], dqs[1]).astype(BF16)
        for b in range(3):
            @pl.when(j - 2 + b >= 0)
            def _(b=b):
                rows = pl.ds(pl.multiple_of((j - 2 + b) * QBLK, QBLK), QBLK)
                dk_acc[rows, :] += dk_blk[b * QBLK:(b + 1) * QBLK]
                dv_acc[rows, :] += dv_blk[b * QBLK:(b + 1) * QBLK]

        @pl.when(j == nq - 1)
        def _():
            dkv_ref[0] = dk_acc[...].astype(BF16)
            dkv_ref[1] = dv_acc[...].astype(BF16)
            dg_ref[0] = jnp.concatenate([_offset_sums(gt_acc[0]), _offset_sums(gt_acc[1]),
                                         jnp.zeros((6, A_DIAG), F32)], axis=0)

    blk = pl.BlockSpec((QBLK, 128), lambda p, j: (j, p))
    return pl.pallas_call(
        body, name="attn_a_bwd", grid=(N_HEADS // 2, nq),
        in_specs=[q_spec, *k_specs, *v_specs, g_spec, bias_spec, blk, blk],
        out_specs=[pl.BlockSpec((2, QBLK, 128), lambda p, j: (0, j, p)),
                   pl.BlockSpec((2, t, 128), lambda p, j: (0, 0, p)),
                   pl.BlockSpec((1, 8, A_DIAG), lambda p, j: (p, 0, 0))],
        out_shape=[SDS((2, t, D_MODEL), BF16), SDS((2, t, D_MODEL), BF16), SDS((N_HEADS // 2, 8, A_DIAG), F32)],
        scratch_shapes=[pltpu.VMEM((t, 128), F32), pltpu.VMEM((t, 128), F32),
                        pltpu.VMEM((2, QBLK, A_KEYS), F32), pltpu.VMEM((2, QBLK, A_KEYS), F32)],
        compiler_params=_cparams(),
    )(qkvg, qkvg, qkvg, qkvg, qkvg, qkvg, qkvg, qkvg, bias, out_a, dz)


def _b_specs():
    q = pl.BlockSpec((QBLK, 512), lambda h, j: (j, h))
    g = pl.BlockSpec((QBLK, 512), lambda h, j: (j, 2 + h))
    kp = pl.BlockSpec((128, 128), lambda h, j: (jnp.maximum(2 * j - 1, 0), 0))
    kc = pl.BlockSpec((QBLK, 128), lambda h, j: (j, 0))
    vp = pl.BlockSpec((128, 128), lambda h, j: (jnp.maximum(2 * j - 1, 0), 1))
    vc = pl.BlockSpec((QBLK, 128), lambda h, j: (j, 1))
    bias = pl.BlockSpec((B_GROUP, B_KEYS), lambda h, j: (h, 0))
    sinks = pl.BlockSpec(memory_space=pltpu.SMEM)
    return q, g, kp, kc, vp, vc, bias, sinks


def _b_fill_bias(w_ref, bias_scr):
    for g in range(B_GROUP):
        bias_scr[g] = _band_bias(w_ref[g:g + 1, :], B_BAND)


def _b_operands(kp, kc, vp, vc, kvh):
    k = jnp.concatenate([kp[...], kc[...]], axis=0)
    v = jnp.concatenate([vp[...], vc[...]], axis=0)
    kr = pltpu.roll(k, HEAD_DIM, 1)
    vr = pltpu.roll(v, HEAD_DIM, 1)
    first = kvh == 0
    return (jnp.where(first, k, kr), jnp.where(first, kr, k),
            jnp.where(first, v, vr), jnp.where(first, vr, v))


def _b_probs(qp, kk, bias, sink, j, hh, lane_lo):
    sel = lane_lo if hh == 0 else jnp.logical_not(lane_lo)
    qm = jnp.where(sel, qp, jnp.zeros_like(qp)) * SCALE
    s = _dot_nt(qm, kk) + bias
    col = lax.broadcasted_iota(jnp.int32, (1, B_KEYS), 1)
    s = jnp.where(col >= 128 - QBLK * j, s, NEG)
    mx = jnp.maximum(jnp.max(s, axis=-1, keepdims=True), sink)
    e = jnp.exp(s - mx)
    es = jnp.exp(sink - mx)
    return sel, qm, e, es, jnp.sum(e, axis=-1, keepdims=True) + es


def _attn_b_fwd(qg, kv, bias, sinks):
    t = qg.shape[0]
    q_spec, g_spec, kp_spec, kc_spec, vp_spec, vc_spec, bias_spec, sink_spec = _b_specs()

    def body(q_ref, g_ref, kp, kc, vp, vc, w_ref, sink_ref, z_ref, o_ref, b_ref):
        kvh = pl.program_id(0)
        j = pl.program_id(1)

        @pl.when(j == 0)
        def _():
            _b_fill_bias(w_ref, b_ref)

        lane_lo = _lane_lo()
        k_lo, k_hi, v_lo, v_hi = _b_operands(kp, kc, vp, vc, kvh)
        for pp in range(B_GROUP // 2):
            cols = slice(128 * pp, 128 * (pp + 1))
            qp = q_ref[:, cols]
            outs = []
            for hh in range(2):
                sink = sink_ref[kvh * B_GROUP + 2 * pp + hh]
                _, _, e, _, l = _b_probs(qp, k_lo if hh == 0 else k_hi, b_ref[2 * pp + hh], sink, j, hh, lane_lo)
                outs.append(_dot(e.astype(BF16), v_lo if hh == 0 else v_hi) / l)
            o = jnp.where(lane_lo, outs[0], outs[1])
            silu, _ = _silu_parts(g_ref[:, cols].astype(F32))
            o_ref[:, cols] = o.astype(BF16)
            z_ref[:, cols] = (o * silu).astype(BF16)

    out_spec = pl.BlockSpec((QBLK, 512), lambda h, j: (j, h))
    return pl.pallas_call(
        body, name="attn_b_fwd", grid=(B_KV_HEADS, t // QBLK),
        in_specs=[q_spec, g_spec, kp_spec, kc_spec, vp_spec, vc_spec, bias_spec, sink_spec],
        out_specs=[out_spec, out_spec],
        out_shape=[SDS((t, D_MODEL), BF16), SDS((t, D_MODEL), BF16)],
        scratch_shapes=[pltpu.VMEM((B_GROUP, QBLK, B_KEYS), F32)],
        compiler_params=_cparams(),
    )(qg, qg, kv, kv, kv, kv, bias, sinks)


def _attn_b_bwd(qg, kv, bias, sinks, out_b, dz, bucket_onehot):
    t = qg.shape[0]
    nq = t // QBLK
    q_spec, g_spec, kp_spec, kc_spec, vp_spec, vc_spec, bias_spec, sink_spec = _b_specs()

    def body(q_ref, g_ref, kp, kc, vp, vc, w_ref, sink_ref, o_ref, dz_ref, oh_ref,
             dqg_ref, dkv_ref, dt5_ref, dsink_ref, gt_acc, b_ref):
        kvh = pl.program_id(0)
        j = pl.program_id(1)

        @pl.when(jnp.logical_and(kvh == 0, j == 0))
        def _():
            dkv_ref[...] = jnp.zeros_like(dkv_ref)

        @pl.when(j == 0)
        def _():
            _b_fill_bias(w_ref, b_ref)
            gt_acc[...] = jnp.zeros_like(gt_acc)
            dsink_ref[...] = jnp.zeros_like(dsink_ref)

        lane_lo = _lane_lo()
        k_lo, k_hi, v_lo, v_hi = _b_operands(kp, kc, vp, vc, kvh)
        dk_blk = jnp.zeros((B_KEYS, 128), F32)
        dv_blk = jnp.zeros((B_KEYS, 128), F32)
        for pp in range(B_GROUP // 2):
            cols = slice(128 * pp, 128 * (pp + 1))
            qp = q_ref[:, cols]
            o = o_ref[:, cols].astype(F32)
            dzf = dz_ref[:, cols].astype(F32)
            silu, dsilu = _silu_parts(g_ref[:, cols].astype(F32))
            do = dzf * silu
            dqg_ref[1, :, cols] = (dzf * o * dsilu).astype(BF16)
            doo = do * o
            dqs = []
            for hh in range(2):
                g = 2 * pp + hh
                sink = sink_ref[kvh * B_GROUP + g]
                kk = k_lo if hh == 0 else k_hi
                vv = v_lo if hh == 0 else v_hi
                sel, qm, e, es, l = _b_probs(qp, kk, b_ref[g], sink, j, hh, lane_lo)
                p = e / l
                delta = jnp.sum(jnp.where(sel, doo, 0.0), axis=-1, keepdims=True)
                dom = jnp.where(sel, do, 0.0).astype(BF16)
                dp = _dot_nt(dom, vv)
                ds = p * (dp - delta)
                gt_acc[g] += ds
                dsink_ref[g:g + 1, :] += jnp.broadcast_to(-jnp.sum((es / l) * delta), (1, 128))
                dsb = ds.astype(BF16)
                dqs.append(_dot(dsb, kk) * SCALE)
                dk_blk = dk_blk + _dot_tn(dsb, qm)
                dv_blk = dv_blk + _dot_tn(p.astype(BF16), dom)
            dqg_ref[0, :, cols] = jnp.where(lane_lo, dqs[0], dqs[1]).astype(BF16)
        mine = lane_lo == (kvh == 0)
        dk_add = jnp.where(mine, dk_blk + pltpu.roll(dk_blk, HEAD_DIM, 1), 0.0)
        dv_add = jnp.where(mine, dv_blk + pltpu.roll(dv_blk, HEAD_DIM, 1), 0.0)

        @pl.when(j >= 1)
        def _():
            rows = pl.ds(pl.multiple_of((2 * j - 1) * 128, 128), 128)
            dkv_ref[0, rows, :] += dk_add[0:128]
            dkv_ref[1, rows, :] += dv_add[0:128]

        rows = pl.ds(pl.multiple_of(j * QBLK, QBLK), QBLK)
        dkv_ref[0, rows, :] += dk_add[128:B_KEYS]
        dkv_ref[1, rows, :] += dv_add[128:B_KEYS]

        @pl.when(j == nq - 1)
        def _():
            dd = jnp.concatenate([_offset_sums(gt_acc[g]) for g in range(B_GROUP)], axis=0)
            hi = dd.astype(BF16)
            lo = (dd - hi.astype(F32)).astype(BF16)
            dt5_ref[...] = _dot(hi, oh_ref[...]) + _dot(lo, oh_ref[...])

    blk = pl.BlockSpec((QBLK, 512), lambda h, j: (j, h))
    return pl.pallas_call(
        body, name="attn_b_bwd", grid=(B_KV_HEADS, nq),
        in_specs=[q_spec, g_spec, kp_spec, kc_spec, vp_spec, vc_spec, bias_spec, sink_spec, blk, blk,
                  pl.BlockSpec((B_DIAG, 128), lambda h, j: (0, 0))],
        out_specs=[pl.BlockSpec((2, QBLK, 512), lambda h, j: (0, j, h)),
                   pl.BlockSpec((2, t, 128), lambda h, j: (0, 0, 0)),
                   pl.BlockSpec((B_GROUP, 128), lambda h, j: (h, 0)),
                   pl.BlockSpec((B_GROUP, 128), lambda h, j: (h, 0))],
        out_shape=[SDS((2, t, D_MODEL), BF16), SDS((2, t, 128), F32),
                   SDS((N_HEADS, 128), F32), SDS((N_HEADS, 128), F32)],
        scratch_shapes=[pltpu.VMEM((B_GROUP, QBLK, B_KEYS), F32), pltpu.VMEM((B_GROUP, QBLK, B_KEYS), F32)],
        compiler_params=_cparams(),
    )(qg, qg, kv, kv, kv, kv, bias, sinks, out_b, dz, bucket_onehot)


def _a_bias_by_offset(rel_bias):
    m = np.arange(A_DIAG)
    idx = np.clip(A_BAND - 1 - m, -A_REL_CLIP, A_REL_CLIP) + A_REL_CLIP
    by_head = rel_bias[idx].T.reshape(N_HEADS // 2, 2, A_DIAG)
    return jnp.concatenate([by_head, jnp.zeros((N_HEADS // 2, 6, A_DIAG), F32)], axis=1)


def _a_bias_grad(offset_sums):
    first = 319
    tail = jnp.sum(offset_sums[:, :first], axis=1)
    body = jnp.flip(offset_sums[:, first:first + 320], axis=1)
    body = body.at[:, -1].add(tail)
    full = jnp.concatenate([jnp.zeros((N_HEADS, 193), F32), body], axis=1)
    return full.T


def _t5_bucket(rel):
    nb = T5_BUCKETS // 2
    max_exact = nb // 2
    ret = jnp.where(rel > 0, nb, 0)
    n = jnp.abs(rel)
    nf = jnp.maximum(n, 1).astype(jnp.float32)
    large = max_exact + (jnp.log(nf / max_exact) / math.log(T5_MAX_DIST / max_exact)
                         * (nb - max_exact)).astype(jnp.int32)
    large = jnp.minimum(large, nb - 1)
    return ret + jnp.where(n < max_exact, n, large)


def _b_offset_buckets():
    return _t5_bucket(jnp.arange(B_DIAG, dtype=jnp.int32) - (B_LEFT_CHUNKS * CHUNK + CHUNK - 1))


def _b_bias_by_offset(t5_table):
    return t5_table[_b_offset_buckets()].T


def _b_bucket_onehot():
    return (_b_offset_buckets()[:, None] == jnp.arange(128)[None, :]).astype(BF16)


def _local_step(x, target, a_gain, w_in_a, rel_bias, w_out_a, kv_gain, kv_w, t5_table,
                b_gain, w_in_b, sinks, w_out_b, f_gain):
    a_bias = _a_bias_by_offset(rel_bias)
    b_bias = _b_bias_by_offset(t5_table)
    sinks_flat = sinks.reshape(N_HEADS)

    xn, qkvg = _norm_matmul(x, a_gain, w_in_a)
    z_a, out_a = _attn_a_fwd(qkvg, a_bias)
    h1, kvn, hb, kv, qg = _layer_a_out(x, z_a, w_out_a, kv_gain, b_gain, kv_w, w_in_b)
    z_b, out_b = _attn_b_fwd(qg, kv, b_bias, sinks_flat)
    dh2, dh2b, dz_b, loss, d_fn = _layer_b_out_loss(h1, z_b, w_out_b, f_gain, target)

    dqg_b, dkv_b, d_t5, d_sink = _attn_b_bwd(qg, kv, b_bias, sinks_flat, out_b, dz_b, _b_bucket_onehot())
    dh1, dh1b, dz_a, d_bn, d_kn = _layer_b_in_bwd(dqg_b, dkv_b, w_in_b, kv_w, h1, dh2, b_gain, kv_gain, w_out_a)
    dqg_a, dkv_a, d_rel = _attn_a_bwd(qkvg, a_bias, out_a, dz_a)
    grad_x, d_an = _layer_a_in_bwd(dqg_a, dkv_a, w_in_a, x, dh1, a_gain)

    g_w_out_b = _weight_grad_rows("grad_b_w_out", z_b, dh2b[None])
    g_w_in_b = _weight_grad_cols("grad_b_w_in", hb, dqg_b,
                                 [(o, c, 4 * o + c) for o in range(2) for c in range(4)], 256)
    g_kv_w = _weight_grad_rows("grad_kv_w", kvn, dkv_b)
    g_w_out_a = _weight_grad_rows("grad_a_w_out", z_a, dh1b[None])
    g_w_in_a = _weight_grad_cols("grad_a_w_in_qg", xn, dqg_a, [(0, 0, 0), (0, 1, 1), (1, 0, 6), (1, 1, 7)], 512)
    g_w_in_a = _weight_grad_cols("grad_a_w_in_kv", xn, dkv_a, [(0, 0, 2), (0, 1, 3), (1, 0, 4), (1, 1, 5)], 512,
                                 prev=g_w_in_a)

    return dict(
        loss=loss[0, 0], grad_x=grad_x,
        a_norm=d_an, a_w_in=g_w_in_a, a_rel_bias=_a_bias_grad(d_rel[:, :2].reshape(N_HEADS, A_DIAG)),
        a_w_out=g_w_out_a,
        kv_norm=d_kn, kv_w=g_kv_w, t5_bias=d_t5[:, :T5_BUCKETS].T, b_norm=d_bn, b_w_in=g_w_in_b,
        b_sinks=d_sink[:, 0].reshape(1, N_HEADS), b_w_out=g_w_out_b, final_norm=d_fn)


def _place():
    x, y, c = lax.axis_index("x"), lax.axis_index("y"), lax.axis_index("c")
    chips = [(1 - x, y), (x, 1 - y), (1 - x, 1 - y)]
    return x, y, c, chips


def _slot(px, py, pc):
    return 4 * px + 2 * py + pc


ANY = pl.BlockSpec(memory_space=pl.ANY)


def _all_gather(shards):
    n = len(shards)

    def body(*refs):
        ins, outs = refs[:n], refs[n:2 * n]
        send_sems, recv_sems, local_sems = refs[2 * n:]
        x, y, c, chips = _place()
        me, sibling = (x, y, c), (x, y, 1 - c)

        def copy(t, k, block, to, src=None):
            dst = outs[t].at[_slot(*block)]
            return pltpu.make_async_remote_copy(
                src_ref=dst if src is None else src, dst_ref=dst,
                send_sem=send_sems.at[7 * t + k], recv_sem=recv_sems.at[7 * t + k],
                device_id=to, device_id_type=MESH)

        mine = [pltpu.make_async_copy(ins[t], outs[t].at[_slot(*me)], local_sems.at[t]) for t in range(n)]
        for cp in mine:
            cp.start()
        first = []
        for t in range(n):
            first.append(copy(t, 0, me, sibling, src=ins[t]))
            first += [copy(t, 1 + j, me, (*chip, c), src=ins[t]) for j, chip in enumerate(chips)]
        for cp in first:
            cp.start()
        passed = []
        for t in range(n):
            for j, chip in enumerate(chips):
                copy(t, 1 + j, (*chip, c), me).wait_recv()
                cp = copy(t, 4 + j, (*chip, c), sibling)
                cp.start()
                passed.append(cp)
        for t in range(n):
            copy(t, 0, sibling, me).wait_recv()
            for j, chip in enumerate(chips):
                copy(t, 4 + j, (*chip, 1 - c), me).wait_recv()
        for cp in first + passed:
            cp.wait_send()
        for cp in mine:
            cp.wait()

    return pl.pallas_call(
        body, name="all_gather_weights",
        in_specs=[ANY] * n, out_specs=[ANY] * n,
        out_shape=[SDS((N_DEV, *s.shape), s.dtype) for s in shards],
        scratch_shapes=[pltpu.SemaphoreType.DMA((7 * n,)), pltpu.SemaphoreType.DMA((7 * n,)),
                        pltpu.SemaphoreType.DMA((n,))],
    )(*shards)


def _exchange_sibling(grads):
    n = len(grads)

    def body(*refs):
        ins, outs = refs[:n], refs[n:2 * n]
        send_sems, recv_sems = refs[2 * n:]
        x, y, c, chips = _place()
        sibling = (x, y, 1 - c)
        copies = []
        for t in range(n):
            blocks = [(*chip, 1 - c) for chip in chips] + [sibling]
            for k, block in enumerate(blocks):
                copies.append(pltpu.make_async_remote_copy(
                    src_ref=ins[t].at[_slot(*block)], dst_ref=outs[t].at[k],
                    send_sem=send_sems.at[4 * t + k], recv_sem=recv_sems.at[4 * t + k],
                    device_id=sibling, device_id_type=MESH))
        for cp in copies:
            cp.start()
        for cp in copies:
            cp.wait()

    return pl.pallas_call(
        body, name="grads_to_sibling",
        in_specs=[ANY] * n, out_specs=[ANY] * n,
        out_shape=[SDS((4, *g.shape[1:]), g.dtype) for g in grads],
        scratch_shapes=[pltpu.SemaphoreType.DMA((4 * n,)), pltpu.SemaphoreType.DMA((4 * n,))],
    )(*grads)


def _exchange_chips(pre):
    n = len(pre)

    def body(*refs):
        ins, outs = refs[:n], refs[n:2 * n]
        send_sems, recv_sems = refs[2 * n:]
        x, y, c, chips = _place()
        copies = []
        for t in range(n):
            for j, chip in enumerate(chips):
                copies.append(pltpu.make_async_remote_copy(
                    src_ref=ins[t].at[j], dst_ref=outs[t].at[j],
                    send_sem=send_sems.at[3 * t + j], recv_sem=recv_sems.at[3 * t + j],
                    device_id=(*chip, c), device_id_type=MESH))
        for cp in copies:
            cp.start()
        for cp in copies:
            cp.wait()

    return pl.pallas_call(
        body, name="grads_to_chips",
        in_specs=[ANY] * n, out_specs=[ANY] * n,
        out_shape=[SDS(p.shape, p.dtype) for p in pre],
        scratch_shapes=[pltpu.SemaphoreType.DMA((3 * n,)), pltpu.SemaphoreType.DMA((3 * n,))],
    )(*pre)


def _row_tile(rows):
    return min(rows, 256)


def _pre_reduce(name, g, from_sibling, slots):
    _, r, c = g.shape
    tr = _row_tile(r)

    def body(slots_ref, g_ref, s_ref, o_ref):
        del slots_ref
        o_ref[...] = (g_ref[...] + s_ref[...]).astype(BF16)

    return pl.pallas_call(
        body, name=name,
        grid_spec=pltpu.PrefetchScalarGridSpec(
            num_scalar_prefetch=1, grid=(3, r // tr),
            in_specs=[pl.BlockSpec((1, tr, c), lambda j, i, s: (s[j], i, 0)),
                      pl.BlockSpec((1, tr, c), lambda j, i, s: (j, i, 0))],
            out_specs=pl.BlockSpec((1, tr, c), lambda j, i, s: (j, i, 0))),
        out_shape=SDS((3, r, c), BF16),
        compiler_params=_cparams(),
    )(slots, g, from_sibling)


def _adamw(w, g, m, v):
    m2 = ADAM_B1 * m + (1.0 - ADAM_B1) * g
    v2 = ADAM_B2 * v + (1.0 - ADAM_B2) * jnp.square(g)
    m_hat = m2 / (1.0 - ADAM_B1 ** ADAM_STEP)
    v_hat = v2 / (1.0 - ADAM_B2 ** ADAM_STEP)
    delta = -ADAM_LR * (m_hat / (jnp.sqrt(v_hat) + ADAM_EPS) + ADAM_WD * w)
    return delta, m2, v2


def _reduce_adamw(name, g, from_sibling, from_chips, my_slot, w, m, v):
    _, r, c = g.shape
    tr = _row_tile(r)

    def body(slot_ref, g_ref, s_ref, c_ref, w_ref, m_ref, v_ref, grad_ref, d_ref, nm_ref, nv_ref):
        del slot_ref
        grad = g_ref[0] + s_ref[0]
        for j in range(3):
            grad = grad + c_ref[j].astype(F32)
        grad_ref[...] = grad
        d_ref[...], nm_ref[...], nv_ref[...] = _adamw(w_ref[...], grad, m_ref[...], v_ref[...])

    flat = pl.BlockSpec((tr, c), lambda i, s: (i, 0))
    return pl.pallas_call(
        body, name=name,
        grid_spec=pltpu.PrefetchScalarGridSpec(
            num_scalar_prefetch=1, grid=(r // tr,),
            in_specs=[pl.BlockSpec((1, tr, c), lambda i, s: (s[0], i, 0)),
                      pl.BlockSpec((1, tr, c), lambda i, s: (3, i, 0)),
                      pl.BlockSpec((3, tr, c), lambda i, s: (0, i, 0)),
                      flat, flat, flat],
            out_specs=[flat, flat, flat, flat]),
        out_shape=[SDS((r, c), F32)] * 4,
        compiler_params=_cparams(),
    )(my_slot, g, from_sibling, from_chips, w, m, v)


_SMALL = (("a_norm", 8), ("a_rel_bias", 72), ("kv_norm", 8), ("t5_bias", 8), ("b_norm", 8),
          ("b_sinks", 8), ("final_norm", 8), ("loss", 8))
_SMALL_ROWS = sum(r for _, r in _SMALL)


def _pack_small(parts):
    rows = []
    for name, n_rows in _SMALL:
        flat = parts[name].reshape(-1).astype(F32)
        rows.append(jnp.pad(flat, (0, n_rows * 128 - flat.shape[0])).reshape(n_rows, 128))
    return jnp.concatenate(rows, axis=0)


def _unpack_small(buf, shapes):
    out, at = {}, 0
    for name, n_rows in _SMALL:
        size = int(np.prod(shapes[name])) if shapes[name] else 1
        out[name] = buf[at:at + n_rows].reshape(-1)[:size].reshape(shapes[name])
        at += n_rows
    return out


def _small_allreduce_adamw(gbuf, wbuf, mbuf, vbuf):
    def body(g_ref, w_ref, m_ref, v_ref, sum_ref, d_ref, nm_ref, nv_ref, land_ref, send_sems, recv_sems):
        x, y, c, _ = _place()
        my_slot = _slot(x, y, c)
        land_ref[my_slot] = g_ref[...]
        copies = []
        for k in range(1, N_DEV):
            peer = (x ^ (k >> 2), y ^ ((k >> 1) & 1), c ^ (k & 1))
            copies.append(pltpu.make_async_remote_copy(
                src_ref=g_ref, dst_ref=land_ref.at[my_slot],
                send_sem=send_sems.at[k - 1], recv_sem=recv_sems.at[k - 1],
                device_id=peer, device_id_type=MESH))
        for cp in copies:
            cp.start()
        for k in range(1, N_DEV):
            peer_slot = _slot(x ^ (k >> 2), y ^ ((k >> 1) & 1), c ^ (k & 1))
            pltpu.make_async_remote_copy(
                src_ref=g_ref, dst_ref=land_ref.at[peer_slot],
                send_sem=send_sems.at[k - 1], recv_sem=recv_sems.at[k - 1],
                device_id=(x, y, c), device_id_type=MESH).wait_recv()
        for cp in copies:
            cp.wait_send()
        total = land_ref[0]
        for s in range(1, N_DEV):
            total = total + land_ref[s]
        sum_ref[...] = total
        d_ref[...], nm_ref[...], nv_ref[...] = _adamw(w_ref[...], total, m_ref[...], v_ref[...])

    vm = pl.BlockSpec(memory_space=pltpu.VMEM)
    shape = SDS((_SMALL_ROWS, 128), F32)
    return pl.pallas_call(
        body, name="small_allreduce_adamw",
        in_specs=[vm] * 4, out_specs=[vm] * 4, out_shape=[shape] * 4,
        scratch_shapes=[pltpu.VMEM((N_DEV, _SMALL_ROWS, 128), F32),
                        pltpu.SemaphoreType.DMA((N_DEV - 1,)), pltpu.SemaphoreType.DMA((N_DEV - 1,))],
    )(gbuf, wbuf, mbuf, vbuf)


def kernel(x, a_norm, a_w_in, a_rel_bias, a_w_out, kv_norm, kv_w, t5_bias, b_norm, b_w_in, b_sinks, b_w_out, final_norm, loss_target, m_a_norm, m_a_w_in, m_a_rel_bias, m_a_w_out, m_kv_norm, m_kv_w, m_t5_bias, m_b_norm, m_b_w_in, m_b_sinks, m_b_w_out, m_final_norm, v_a_norm, v_a_w_in, v_a_rel_bias, v_a_w_out, v_kv_norm, v_kv_w, v_t5_bias, v_b_norm, v_b_w_in, v_b_sinks, v_b_w_out, v_final_norm):
    xi, yi, ci = lax.axis_index("x"), lax.axis_index("y"), lax.axis_index("c")
    my_slot = _slot(xi, yi, ci)

    w_in_a, w_in_b, w_out_a, w_out_b, kv_full, a_gain = _all_gather([
        a_w_in[0].astype(BF16), b_w_in[0].astype(BF16), a_w_out[0].astype(BF16),
        b_w_out[0].astype(BF16), kv_w.astype(BF16), a_norm])
    w_out_a = w_out_a.reshape(D_MODEL, D_MODEL)
    w_out_b = w_out_b.reshape(D_MODEL, D_MODEL)
    kv_full = kv_full.reshape(D_MODEL, 2 * 128)
    a_gain = a_gain.reshape(1, D_MODEL)

    loc = _local_step(x[0], loss_target[0], a_gain, w_in_a, a_rel_bias[0], w_out_a,
                      kv_norm.reshape(1, D_MODEL), kv_full, t5_bias, b_norm, w_in_b, b_sinks,
                      w_out_b, final_norm.reshape(1, D_MODEL))

    names = ("a_w_in", "b_w_in", "a_w_out", "b_w_out", "kv_w")
    shard_w = dict(a_w_in=a_w_in[0], b_w_in=b_w_in[0], a_w_out=a_w_out[0], b_w_out=b_w_out[0], kv_w=kv_w)
    shard_m = dict(a_w_in=m_a_w_in[0], b_w_in=m_b_w_in[0], a_w_out=m_a_w_out[0], b_w_out=m_b_w_out[0], kv_w=m_kv_w)
    shard_v = dict(a_w_in=v_a_w_in[0], b_w_in=v_b_w_in[0], a_w_out=v_a_w_out[0], b_w_out=v_b_w_out[0], kv_w=v_kv_w)
    grads = [loc[n] for n in names]
    from_sibling = _exchange_sibling(grads)
    forward_slots = jnp.stack([_slot(1 - xi, yi, ci), _slot(xi, 1 - yi, ci), _slot(1 - xi, 1 - yi, ci)]).astype(jnp.int32)
    pre = [_pre_reduce("chip_sum_" + n, g, s, forward_slots) for n, g, s in zip(names, grads, from_sibling)]
    from_chips = _exchange_chips(pre)
    slot_arr = jnp.reshape(my_slot, (1,)).astype(jnp.int32)
    big = {n: _reduce_adamw("adamw_" + n, g, s, f, slot_arr, shard_w[n], shard_m[n], shard_v[n])
           for n, g, s, f in zip(names, grads, from_sibling, from_chips)}

    def own_row(vec):
        return lax.dynamic_update_slice(jnp.zeros((N_DEV, 128), F32), vec, (my_slot, 0))

    zero = jnp.zeros((), F32)
    small_w = dict(a_norm=own_row(a_norm), a_rel_bias=a_rel_bias, kv_norm=kv_norm, t5_bias=t5_bias,
                   b_norm=b_norm, b_sinks=b_sinks, final_norm=final_norm, loss=zero)
    small_m = dict(a_norm=own_row(m_a_norm), a_rel_bias=m_a_rel_bias, kv_norm=m_kv_norm, t5_bias=m_t5_bias,
                   b_norm=m_b_norm, b_sinks=m_b_sinks, final_norm=m_final_norm, loss=zero)
    small_v = dict(a_norm=own_row(v_a_norm), a_rel_bias=v_a_rel_bias, kv_norm=v_kv_norm, t5_bias=v_t5_bias,
                   b_norm=v_b_norm, b_sinks=v_b_sinks, final_norm=v_final_norm, loss=zero)
    small = _small_allreduce_adamw(_pack_small(loc), _pack_small(small_w), _pack_small(small_m), _pack_small(small_v))
    shapes = dict(a_norm=(N_DEV, 128), a_rel_bias=a_rel_bias.shape, kv_norm=kv_norm.shape, t5_bias=t5_bias.shape,
                  b_norm=b_norm.shape, b_sinks=b_sinks.shape, final_norm=final_norm.shape, loss=())
    sm = [_unpack_small(buf, shapes) for buf in small]
    for part in sm:
        part["a_norm"] = lax.dynamic_slice(part["a_norm"], (my_slot, 0), (1, 128))

    order = ("a_norm", "a_w_in", "a_rel_bias", "a_w_out", "kv_norm", "kv_w", "t5_bias", "b_norm",
             "b_w_in", "b_sinks", "b_w_out", "final_norm")
    lead = dict(a_w_in=True, b_w_in=True, a_w_out=True, b_w_out=True, kv_w=False)

    def pick(kind, name):
        if name in big:
            val = big[name][kind]
            return val[None] if lead[name] else val
        return sm[kind][name]

    outs = [sm[0]["loss"], loc["grad_x"][None]]
    for kind in range(4):
        outs += [pick(kind, n) for n in order]
    return tuple(outs)
```

```python
import functools
import math

import numpy as np
import jax
import jax.numpy as jnp
from jax import lax
from jax.experimental import pallas as pl
from jax.experimental.pallas import tpu as pltpu

F32 = jnp.float32
BF16 = jnp.bfloat16
SDS = jax.ShapeDtypeStruct

D_MODEL = 1024
HEAD_DIM = 64
CHUNK = 64
N_HEADS = 16
RMS_EPS = 1e-6
A_LEFT_CHUNKS = 8
A_BAND = (A_LEFT_CHUNKS + 1) * CHUNK
A_REL_CLIP = 256
B_KV_HEADS = 2
B_GROUP = 8
B_LEFT_CHUNKS = 2
B_BAND = (B_LEFT_CHUNKS + 1) * CHUNK
T5_BUCKETS = 32
T5_MAX_DIST = 128
QBLK = 256
A_KEYS = 3 * QBLK
B_KEYS = QBLK + 128
A_DIAG = A_KEYS
B_DIAG = B_KEYS
NEG = -1e30
SCALE = HEAD_DIM ** -0.5
N_DEV = 8

ADAM_LR = 0.001
ADAM_B1 = 0.9
ADAM_B2 = 0.999
ADAM_EPS = 1e-08
ADAM_WD = 0.01
ADAM_STEP = 10

VMEM_LIMIT_BYTES = 56 * 1024 * 1024
MESH = pl.DeviceIdType.MESH


def _cparams():
    return pltpu.CompilerParams(vmem_limit_bytes=VMEM_LIMIT_BYTES)


def _dot(a, b):
    return jnp.dot(a, b, preferred_element_type=F32)


def _dot_nt(a, b):
    return lax.dot_general(a, b, (((1,), (1,)), ((), ())), preferred_element_type=F32)


def _dot_tn(a, b):
    return lax.dot_general(a, b, (((0,), (0,)), ((), ())), preferred_element_type=F32)


def _rstd(xf):
    return lax.rsqrt(jnp.mean(xf * xf, axis=-1, keepdims=True) + RMS_EPS)


def _sigmoid(x):
    return 1.0 / (1.0 + jnp.exp(-x))


def _norm_matmul(x, gain, w):
    t = x.shape[0]
    nb, _, tn = w.shape
    tm = min(t, 1024)

    def body(x_ref, g_ref, w_ref, xn_ref, o_ref):
        @pl.when(pl.program_id(1) == 0)
        def _():
            xf = x_ref[...]
            xn_ref[...] = ((xf * _rstd(xf)) * g_ref[...]).astype(BF16)

        o_ref[...] = _dot(xn_ref[...], w_ref[0]).astype(BF16)

    return pl.pallas_call(
        body, name="norm_matmul", grid=(t // tm, nb),
        in_specs=[pl.BlockSpec((tm, D_MODEL), lambda m, n: (m, 0)),
                  pl.BlockSpec((1, D_MODEL), lambda m, n: (0, 0)),
                  pl.BlockSpec((1, D_MODEL, tn), lambda m, n: (n, 0, 0))],
        out_specs=[pl.BlockSpec((tm, D_MODEL), lambda m, n: (m, 0)),
                   pl.BlockSpec((tm, tn), lambda m, n: (m, n))],
        out_shape=[SDS((t, D_MODEL), BF16), SDS((t, nb * tn), BF16)],
        compiler_params=_cparams(),
    )(x, gain, w)


def _layer_a_out(x, z, w_out, kv_gain, b_gain, kv_w, w_in_b):
    t = x.shape[0]
    tm = min(t, 512)
    nb, _, tn = w_in_b.shape

    def body(x_ref, z_ref, wo_ref, kvg_ref, bg_ref, kvw_ref, wb_ref,
             h1_ref, kvn_ref, hb_ref, kv_ref, qg_ref):
        h1 = x_ref[...] + _dot(z_ref[...], wo_ref[...])
        h1_ref[...] = h1
        y0 = h1 * _rstd(h1)
        kvn = (y0 * kvg_ref[...]).astype(BF16)
        hb = (y0 * bg_ref[...]).astype(BF16)
        kvn_ref[...] = kvn
        hb_ref[...] = hb
        kv_ref[...] = _dot(kvn, kvw_ref[...]).astype(BF16)
        for i in range(nb):
            qg_ref[:, i * tn:(i + 1) * tn] = _dot(hb, wb_ref[i]).astype(BF16)

    row = lambda m: (m, 0)
    fix2 = lambda m: (0, 0)
    return pl.pallas_call(
        body, name="layer_a_out", grid=(t // tm,),
        in_specs=[pl.BlockSpec((tm, D_MODEL), row), pl.BlockSpec((tm, D_MODEL), row),
                  pl.BlockSpec((D_MODEL, D_MODEL), fix2),
                  pl.BlockSpec((1, D_MODEL), fix2), pl.BlockSpec((1, D_MODEL), fix2),
                  pl.BlockSpec((D_MODEL, 256), fix2),
                  pl.BlockSpec((nb, D_MODEL, tn), lambda m: (0, 0, 0))],
        out_specs=[pl.BlockSpec((tm, D_MODEL), row), pl.BlockSpec((tm, D_MODEL), row),
                   pl.BlockSpec((tm, D_MODEL), row), pl.BlockSpec((tm, 256), row),
                   pl.BlockSpec((tm, nb * tn), row)],
        out_shape=[SDS((t, D_MODEL), F32), SDS((t, D_MODEL), BF16), SDS((t, D_MODEL), BF16),
                   SDS((t, 256), BF16), SDS((t, nb * tn), BF16)],
        compiler_params=_cparams(),
    )(x, z, w_out, kv_gain, b_gain, kv_w, w_in_b)


def _layer_b_out_loss(h1, z, w_out, f_gain, target):
    t = h1.shape[0]
    tm = min(t, 512)

    def body(h1_ref, z_ref, wo_ref, fg_ref, tgt_ref,
             dh2_ref, dh2b_ref, dz_ref, loss_ref, dfn_ref):
        @pl.when(pl.program_id(0) == 0)
        def _():
            loss_ref[...] = jnp.zeros_like(loss_ref)
            dfn_ref[...] = jnp.zeros_like(dfn_ref)

        h2 = h1_ref[...] + _dot(z_ref[...], wo_ref[...])
        r = _rstd(h2)
        yn = h2 * r
        fg = fg_ref[...]
        err = yn * fg - tgt_ref[...]
        loss_ref[...] += (0.5 / D_MODEL) * jnp.sum(err * err)
        dy = err * (1.0 / D_MODEL)
        dfn_ref[...] += jnp.sum(dy * yn, axis=0, keepdims=True)
        u = dy * fg
        dh2 = r * u - h2 * ((r * r * r) * jnp.mean(u * h2, axis=-1, keepdims=True))
        dh2_ref[...] = dh2
        dh2b = dh2.astype(BF16)
        dh2b_ref[...] = dh2b
        dz_ref[...] = _dot_nt(dh2b, wo_ref[...]).astype(BF16)

    row = lambda m: (m, 0)
    fix2 = lambda m: (0, 0)
    return pl.pallas_call(
        body, name="layer_b_out_loss", grid=(t // tm,),
        in_specs=[pl.BlockSpec((tm, D_MODEL), row), pl.BlockSpec((tm, D_MODEL), row),
                  pl.BlockSpec((D_MODEL, D_MODEL), fix2), pl.BlockSpec((1, D_MODEL), fix2),
                  pl.BlockSpec((tm, D_MODEL), row)],
        out_specs=[pl.BlockSpec((tm, D_MODEL), row), pl.BlockSpec((tm, D_MODEL), row),
                   pl.BlockSpec((tm, D_MODEL), row), pl.BlockSpec((1, 128), fix2),
                   pl.BlockSpec((1, D_MODEL), fix2)],
        out_shape=[SDS((t, D_MODEL), F32), SDS((t, D_MODEL), BF16), SDS((t, D_MODEL), BF16),
                   SDS((1, 128), F32), SDS((1, D_MODEL), F32)],
        compiler_params=_cparams(),
    )(h1, z, w_out, f_gain, target)


def _layer_b_in_bwd(dqg, dkv, w_in_b, kv_w, h1, dh2, b_gain, kv_gain, w_out_a):
    t = h1.shape[0]
    tm = min(t, 256)
    nb, _, tn = w_in_b.shape
    per = D_MODEL // tn

    def body(dqg_ref, dkv_ref, wb_ref, kvw_ref, h1_ref, dh2_ref, bg_ref, kvg_ref, wo_ref,
             dh1_ref, dh1b_ref, dz_ref, dbn_ref, dkn_ref):
        @pl.when(pl.program_id(0) == 0)
        def _():
            dbn_ref[...] = jnp.zeros_like(dbn_ref)
            dkn_ref[...] = jnp.zeros_like(dkn_ref)

        dhb = jnp.zeros((tm, D_MODEL), F32)
        for i in range(nb):
            blk = dqg_ref[i // per, :, (i % per) * tn:(i % per + 1) * tn]
            dhb = dhb + _dot_nt(blk, wb_ref[i])
        dkn = (_dot_nt(dkv_ref[0].astype(BF16), kvw_ref[:, 0:128])
               + _dot_nt(dkv_ref[1].astype(BF16), kvw_ref[:, 128:256]))
        h1 = h1_ref[...]
        r = _rstd(h1)
        xr = h1 * r
        dbn_ref[...] += jnp.sum(dhb * xr, axis=0, keepdims=True)
        dkn_ref[...] += jnp.sum(dkn * xr, axis=0, keepdims=True)
        u = dhb * bg_ref[...] + dkn * kvg_ref[...]
        dh1 = dh2_ref[...] + r * u - h1 * ((r * r * r) * jnp.mean(u * h1, axis=-1, keepdims=True))
        dh1_ref[...] = dh1
        dh1b = dh1.astype(BF16)
        dh1b_ref[...] = dh1b
        dz_ref[...] = _dot_nt(dh1b, wo_ref[...]).astype(BF16)

    row = lambda m: (m, 0)
    fix2 = lambda m: (0, 0)
    return pl.pallas_call(
        body, name="layer_b_in_bwd", grid=(t // tm,),
        in_specs=[pl.BlockSpec((2, tm, D_MODEL), lambda m: (0, m, 0)),
                  pl.BlockSpec((2, tm, 128), lambda m: (0, m, 0)),
                  pl.BlockSpec((nb, D_MODEL, tn), lambda m: (0, 0, 0)),
                  pl.BlockSpec((D_MODEL, 256), fix2),
                  pl.BlockSpec((tm, D_MODEL), row), pl.BlockSpec((tm, D_MODEL), row),
                  pl.BlockSpec((1, D_MODEL), fix2), pl.BlockSpec((1, D_MODEL), fix2),
                  pl.BlockSpec((D_MODEL, D_MODEL), fix2)],
        out_specs=[pl.BlockSpec((tm, D_MODEL), row), pl.BlockSpec((tm, D_MODEL), row),
                   pl.BlockSpec((tm, D_MODEL), row), pl.BlockSpec((1, D_MODEL), fix2),
                   pl.BlockSpec((1, D_MODEL), fix2)],
        out_shape=[SDS((t, D_MODEL), F32), SDS((t, D_MODEL), BF16), SDS((t, D_MODEL), BF16),
                   SDS((1, D_MODEL), F32), SDS((1, D_MODEL), F32)],
        compiler_params=_cparams(),
    )(dqg, dkv, w_in_b, kv_w, h1, dh2, b_gain, kv_gain, w_out_a)


def _layer_a_in_bwd(dqg, dkv, w_in_a, x, dh1, a_gain):
    t = x.shape[0]
    tm = min(t, 256)
    nb, _, tn = w_in_a.shape
    per = D_MODEL // tn

    def body(dqg_ref, dkv_ref, w_ref, x_ref, dh1_ref, ag_ref, dx_ref, dan_ref):
        @pl.when(pl.program_id(0) == 0)
        def _():
            dan_ref[...] = jnp.zeros_like(dan_ref)

        dxn = jnp.zeros((tm, D_MODEL), F32)
        for i in range(nb):
            part = i // per
            src = dqg_ref if part in (0, 3) else dkv_ref
            outer = {0: 0, 3: 1, 1: 0, 2: 1}[part]
            blk = src[outer, :, (i % per) * tn:(i % per + 1) * tn]
            dxn = dxn + _dot_nt(blk, w_ref[i])
        xf = x_ref[...]
        r = _rstd(xf)
        dan_ref[...] += jnp.sum(dxn * (xf * r), axis=0, keepdims=True)
        u = dxn * ag_ref[...]
        dx_ref[...] = dh1_ref[...] + r * u - xf * ((r * r * r) * jnp.mean(u * xf, axis=-1, keepdims=True))

    row = lambda m: (m, 0)
    fix2 = lambda m: (0, 0)
    return pl.pallas_call(
        body, name="layer_a_in_bwd", grid=(t // tm,),
        in_specs=[pl.BlockSpec((2, tm, D_MODEL), lambda m: (0, m, 0)),
                  pl.BlockSpec((2, tm, D_MODEL), lambda m: (0, m, 0)),
                  pl.BlockSpec((nb, D_MODEL, tn), lambda m: (0, 0, 0)),
                  pl.BlockSpec((tm, D_MODEL), row), pl.BlockSpec((tm, D_MODEL), row),
                  pl.BlockSpec((1, D_MODEL), fix2)],
        out_specs=[pl.BlockSpec((tm, D_MODEL), row), pl.BlockSpec((1, D_MODEL), fix2)],
        out_shape=[SDS((t, D_MODEL), F32), SDS((1, D_MODEL), F32)],
        compiler_params=_cparams(),
    )(dqg, dkv, w_in_a, x, dh1, a_gain)


def _lut(s, vals):
    r = jnp.int32(vals[0])
    for i in range(1, len(vals)):
        r = jnp.where(s == i, jnp.int32(vals[i]), r)
    return r


def _weight_grad_cols(name, a, b, steps, tn, prev=None):
    t, dw = a.shape
    outers = [s[0] for s in steps]
    cols = [s[1] for s in steps]
    blks = [s[2] for s in steps]

    def body(*refs):
        if prev is None:
            a_ref, b_ref, o_ref, at_ref = refs
        else:
            a_ref, b_ref, _, o_ref, at_ref = refs

        @pl.when(pl.program_id(0) == 0)
        def _():
            at_ref[...] = a_ref[...].T

        o_ref[0] = _dot(at_ref[...], b_ref[0])

    in_specs = [pl.BlockSpec((t, dw), lambda s: (0, 0)),
                pl.BlockSpec((1, t, tn), lambda s: (_lut(s, outers), 0, _lut(s, cols)))]
    args = [a, b]
    aliases = {}
    if prev is not None:
        in_specs.append(pl.BlockSpec(memory_space=pl.ANY))
        args.append(prev)
        aliases = {2: 0}
    return pl.pallas_call(
        body, name=name, grid=(len(steps),),
        in_specs=in_specs,
        out_specs=pl.BlockSpec((1, dw, tn), lambda s: (_lut(s, blks), 0, 0)),
        out_shape=SDS((N_DEV, dw, tn), F32),
        scratch_shapes=[pltpu.VMEM((dw, t), BF16)],
        input_output_aliases=aliases,
        compiler_params=_cparams(),
    )(*args)


def _weight_grad_rows(name, a, b):
    t, dw = a.shape
    n_o, _, c = b.shape
    rows = dw // N_DEV

    def body(a_ref, b_ref, o_ref):
        at = a_ref[...].T
        for o in range(n_o):
            o_ref[0, :, o * c:(o + 1) * c] = _dot(at, b_ref[o].astype(BF16))

    return pl.pallas_call(
        body, name=name, grid=(N_DEV,),
        in_specs=[pl.BlockSpec((t, rows), lambda s: (0, s)),
                  pl.BlockSpec((n_o, t, c), lambda s: (0, 0, 0))],
        out_specs=pl.BlockSpec((1, rows, n_o * c), lambda s: (s, 0, 0)),
        out_shape=SDS((N_DEV, rows, n_o * c), F32),
        compiler_params=_cparams(),
    )(a, b)


def _lane_lo():
    return lax.broadcasted_iota(jnp.int32, (1, 128), 1) < HEAD_DIM


def _offset_sums(gt):
    keys = gt.shape[1]
    gc = gt[0:CHUNK]
    for cc in range(1, QBLK // CHUNK):
        gc = gc + pltpu.roll(gt[cc * CHUNK:(cc + 1) * CHUNK], keys - cc * CHUNK, 1)
    hi = gc.astype(BF16)
    lo = (gc - hi.astype(F32)).astype(BF16)
    flip = (lax.broadcasted_iota(jnp.int32, (CHUNK, CHUNK), 0)
            + lax.broadcasted_iota(jnp.int32, (CHUNK, CHUNK), 1) == CHUNK - 1).astype(BF16)
    gf = _dot(flip, hi) + _dot(flip, lo)
    skew = pltpu.roll(gf, 0, 1, stride=1, stride_axis=0)
    return jnp.sum(skew, axis=0, keepdims=True)


def _band_bias(w_row, band):
    keys = w_row.shape[1]
    base = jnp.broadcast_to(w_row, (CHUNK, keys))
    skew = pltpu.roll(base, 0, 1, stride=1, stride_axis=0)
    skew = pltpu.roll(skew, keys - (CHUNK - 1), 1)
    col = lax.broadcasted_iota(jnp.int32, (CHUNK, keys), 1)
    chunk0 = jnp.where(col < band, skew, NEG)
    return jnp.concatenate(
        [chunk0] + [pltpu.roll(chunk0, cc * CHUNK, 1) for cc in range(1, QBLK // CHUNK)], axis=0)


def _silu_parts(g):
    sg = _sigmoid(g)
    return g * sg, sg * (1.0 + g * (1.0 - sg))


def _a_specs(t):
    nq = t // QBLK
    del nq
    q = pl.BlockSpec((QBLK, 128), lambda p, j: (j, p))
    ks = [pl.BlockSpec((QBLK, 128), lambda p, j, b=b: (jnp.maximum(j - 2 + b, 0), 8 + p)) for b in range(3)]
    vs = [pl.BlockSpec((QBLK, 128), lambda p, j, b=b: (jnp.maximum(j - 2 + b, 0), 16 + p)) for b in range(3)]
    g = pl.BlockSpec((QBLK, 128), lambda p, j: (j, 24 + p))
    bias = pl.BlockSpec((1, 8, A_KEYS), lambda p, j: (p, 0, 0))
    return q, ks, vs, g, bias


def _fill_bias(n, get_row, band, first_valid_col, early, bias_scr, j):
    @pl.when(j == 0)
    def _():
        for h in range(n):
            bias_scr[h] = _band_bias(get_row(h), band)

    @pl.when(j < early)
    def _():
        keys = bias_scr.shape[2]
        col_ok = lax.broadcasted_iota(jnp.int32, (1, keys), 1) >= first_valid_col
        for h in range(n):
            bias_scr[n + h] = jnp.where(col_ok, bias_scr[h], NEG)


def _head_logits(q, k, bias_scr, idx, sel):
    qm = jnp.where(sel, q, jnp.zeros_like(q)) * SCALE
    return qm, _dot_nt(qm, k) + bias_scr[idx]


def _row_sums_everywhere(r, sel):
    return jnp.where(sel, pltpu.roll(r, HEAD_DIM, 1), r)


def _own_everywhere(x, sel):
    return jnp.where(sel, x, pltpu.roll(x, HEAD_DIM, 1))


def _minus_rows(s, row_full):
    return jnp.concatenate([s[:, i:i + 128] - row_full for i in range(0, s.shape[1], 128)], axis=1)


def _attn_a_fwd(qkvg, bias):
    t = qkvg.shape[0]
    q_spec, k_specs, v_specs, g_spec, bias_spec = _a_specs(t)

    def body(q_ref, k0, k1, k2, v0, v1, v2, g_ref, w_ref, z_ref, o_ref, lse_ref, b_ref):
        j = pl.program_id(1)
        _fill_bias(2, lambda h: w_ref[0, h:h + 1, :], A_BAND, QBLK * (2 - j), 2, b_ref, j)
        early = (j < 2).astype(jnp.int32)
        lane_lo = _lane_lo()
        q = q_ref[...]
        k = jnp.concatenate([k0[...], k1[...], k2[...]], axis=0)
        v = jnp.concatenate([v0[...], v1[...], v2[...]], axis=0)
        outs, lses = [], []
        for hh in range(2):
            sel = lane_lo if hh == 0 else jnp.logical_not(lane_lo)
            _, s = _head_logits(q, k, b_ref, hh + 2 * early, sel)
            mx = jnp.max(s, axis=-1, keepdims=True)
            e = jnp.exp(s - mx).astype(BF16)
            r = _dot(e, jnp.where(sel, v, jnp.ones_like(v)))
            l = _row_sums_everywhere(r, sel)
            outs.append(r / l)
            lses.append(mx + jnp.log(l))
        o = jnp.where(lane_lo, outs[0], outs[1])
        silu, _ = _silu_parts(g_ref[...].astype(F32))
        o_ref[...] = o.astype(BF16)
        z_ref[...] = (o * silu).astype(BF16)
        lse_ref[...] = jnp.where(lane_lo, lses[0], lses[1])

    out_spec = pl.BlockSpec((QBLK, 128), lambda p, j: (j, p))
    return pl.pallas_call(
        body, name="attn_a_fwd", grid=(N_HEADS // 2, t // QBLK),
        in_specs=[q_spec, *k_specs, *v_specs, g_spec, bias_spec],
        out_specs=[out_spec, out_spec, out_spec],
        out_shape=[SDS((t, D_MODEL), BF16), SDS((t, D_MODEL), BF16), SDS((t, D_MODEL), F32)],
        scratch_shapes=[pltpu.VMEM((4, QBLK, A_KEYS), F32)],
        compiler_params=_cparams(),
    )(qkvg, qkvg, qkvg, qkvg, qkvg, qkvg, qkvg, qkvg, bias)


def _attn_a_bwd(qkvg, bias, out_a, lse, dz):
    t = qkvg.shape[0]
    nq = t // QBLK
    q_spec, k_specs, v_specs, g_spec, bias_spec = _a_specs(t)

    def body(q_ref, k0, k1, k2, v0, v1, v2, g_ref, w_ref, o_ref, lse_ref, dz_ref,
             dqg_ref, dkv_ref, dg_ref, dk_acc, dv_acc, gt_acc, b_ref):
        j = pl.program_id(1)
        _fill_bias(2, lambda h: w_ref[0, h:h + 1, :], A_BAND, QBLK * (2 - j), 2, b_ref, j)

        @pl.when(j == 0)
        def _():
            dk_acc[...] = jnp.zeros_like(dk_acc)
            dv_acc[...] = jnp.zeros_like(dv_acc)
            gt_acc[...] = jnp.zeros_like(gt_acc)

        early = (j < 2).astype(jnp.int32)
        lane_lo = _lane_lo()
        q = q_ref[...]
        k = jnp.concatenate([k0[...], k1[...], k2[...]], axis=0)
        v = jnp.concatenate([v0[...], v1[...], v2[...]], axis=0)
        o = o_ref[...].astype(F32)
        lse_pair = lse_ref[...]
        dzf = dz_ref[...].astype(F32)
        silu, dsilu = _silu_parts(g_ref[...].astype(F32))
        do = dzf * silu
        dqg_ref[1] = (dzf * o * dsilu).astype(BF16)
        doo = do * o
        dqs = []
        dk_blk = jnp.zeros((A_KEYS, 128), F32)
        dv_blk = jnp.zeros((A_KEYS, 128), F32)
        for hh in range(2):
            sel = lane_lo if hh == 0 else jnp.logical_not(lane_lo)
            qm, s = _head_logits(q, k, b_ref, hh + 2 * early, sel)
            p = jnp.exp(_minus_rows(s, _own_everywhere(lse_pair, sel)))
            delta = jnp.sum(jnp.where(sel, doo, 0.0), axis=-1, keepdims=True)
            dom = jnp.where(sel, do, 0.0).astype(BF16)
            dp = _dot_nt(dom, v)
            ds = p * (dp - delta)
            gt_acc[hh] += ds
            dsb = ds.astype(BF16)
            dqs.append(_dot(dsb, k) * SCALE)
            dk_blk = dk_blk + _dot_tn(dsb, qm)
            dv_blk = dv_blk + _dot_tn(p.astype(BF16), dom)
        dqg_ref[0] = jnp.where(lane_lo, dqs[0], dqs[1]).astype(BF16)
        for b in range(3):
            @pl.when(j - 2 + b >= 0)
            def _(b=b):
                rows = pl.ds(pl.multiple_of((j - 2 + b) * QBLK, QBLK), QBLK)
                dk_acc[rows, :] += dk_blk[b * QBLK:(b + 1) * QBLK]
                dv_acc[rows, :] += dv_blk[b * QBLK:(b + 1) * QBLK]

        @pl.when(j == nq - 1)
        def _():
            dkv_ref[0] = dk_acc[...].astype(BF16)
            dkv_ref[1] = dv_acc[...].astype(BF16)
            dg_ref[0] = jnp.concatenate([_offset_sums(gt_acc[0]), _offset_sums(gt_acc[1]),
                                         jnp.zeros((6, A_DIAG), F32)], axis=0)

    blk = pl.BlockSpec((QBLK, 128), lambda p, j: (j, p))
    return pl.pallas_call(
        body, name="attn_a_bwd", grid=(N_HEADS // 2, nq),
        in_specs=[q_spec, *k_specs, *v_specs, g_spec, bias_spec, blk, blk, blk],
        out_specs=[pl.BlockSpec((2, QBLK, 128), lambda p, j: (0, j, p)),
                   pl.BlockSpec((2, t, 128), lambda p, j: (0, 0, p)),
                   pl.BlockSpec((1, 8, A_DIAG), lambda p, j: (p, 0, 0))],
        out_shape=[SDS((2, t, D_MODEL), BF16), SDS((2, t, D_MODEL), BF16), SDS((N_HEADS // 2, 8, A_DIAG), F32)],
        scratch_shapes=[pltpu.VMEM((t, 128), F32), pltpu.VMEM((t, 128), F32),
                        pltpu.VMEM((2, QBLK, A_KEYS), F32), pltpu.VMEM((4, QBLK, A_KEYS), F32)],
        compiler_params=_cparams(),
    )(qkvg, qkvg, qkvg, qkvg, qkvg, qkvg, qkvg, qkvg, bias, out_a, lse, dz)


def _b_specs():
    q = pl.BlockSpec((QBLK, 512), lambda h, j: (j, h))
    g = pl.BlockSpec((QBLK, 512), lambda h, j: (j, 2 + h))
    kp = pl.BlockSpec((128, 128), lambda h, j: (jnp.maximum(2 * j - 1, 0), 0))
    kc = pl.BlockSpec((QBLK, 128), lambda h, j: (j, 0))
    vp = pl.BlockSpec((128, 128), lambda h, j: (jnp.maximum(2 * j - 1, 0), 1))
    vc = pl.BlockSpec((QBLK, 128), lambda h, j: (j, 1))
    bias = pl.BlockSpec((B_GROUP, B_KEYS), lambda h, j: (h, 0))
    sinks = pl.BlockSpec(memory_space=pltpu.SMEM)
    return q, g, kp, kc, vp, vc, bias, sinks


def _b_operands(kp, kc, vp, vc, kvh):
    k = jnp.concatenate([kp[...], kc[...]], axis=0)
    v = jnp.concatenate([vp[...], vc[...]], axis=0)
    kr = pltpu.roll(k, HEAD_DIM, 1)
    vr = pltpu.roll(v, HEAD_DIM, 1)
    first = kvh == 0
    return (jnp.where(first, k, kr), jnp.where(first, kr, k),
            jnp.where(first, v, vr), jnp.where(first, vr, v))


def _attn_b_fwd(qg, kv, bias, sinks):
    t = qg.shape[0]
    q_spec, g_spec, kp_spec, kc_spec, vp_spec, vc_spec, bias_spec, sink_spec = _b_specs()

    def body(q_ref, g_ref, kp, kc, vp, vc, w_ref, sink_ref, z_ref, o_ref, lse_ref, b_ref):
        kvh = pl.program_id(0)
        j = pl.program_id(1)
        _fill_bias(B_GROUP, lambda h: w_ref[h:h + 1, :], B_BAND, 128, 1, b_ref, j)
        early = (j < 1).astype(jnp.int32)
        lane_lo = _lane_lo()
        k_lo, k_hi, v_lo, v_hi = _b_operands(kp, kc, vp, vc, kvh)
        for pp in range(B_GROUP // 2):
            cols = slice(128 * pp, 128 * (pp + 1))
            qp = q_ref[:, cols]
            outs, lses = [], []
            for hh in range(2):
                g = 2 * pp + hh
                sel = lane_lo if hh == 0 else jnp.logical_not(lane_lo)
                sink = sink_ref[kvh * B_GROUP + g]
                vv = v_lo if hh == 0 else v_hi
                _, s = _head_logits(qp, k_lo if hh == 0 else k_hi, b_ref, g + B_GROUP * early, sel)
                mx = jnp.maximum(jnp.max(s, axis=-1, keepdims=True), sink)
                e = jnp.exp(s - mx).astype(BF16)
                r = _dot(e, jnp.where(sel, vv, jnp.ones_like(vv)))
                l = _row_sums_everywhere(r, sel) + jnp.exp(sink - mx)
                outs.append(r / l)
                lses.append(mx + jnp.log(l))
            o = jnp.where(lane_lo, outs[0], outs[1])
            silu, _ = _silu_parts(g_ref[:, cols].astype(F32))
            o_ref[:, cols] = o.astype(BF16)
            z_ref[:, cols] = (o * silu).astype(BF16)
            lse_ref[:, cols] = jnp.where(lane_lo, lses[0], lses[1])

    out_spec = pl.BlockSpec((QBLK, 512), lambda h, j: (j, h))
    return pl.pallas_call(
        body, name="attn_b_fwd", grid=(B_KV_HEADS, t // QBLK),
        in_specs=[q_spec, g_spec, kp_spec, kc_spec, vp_spec, vc_spec, bias_spec, sink_spec],
        out_specs=[out_spec, out_spec, out_spec],
        out_shape=[SDS((t, D_MODEL), BF16), SDS((t, D_MODEL), BF16), SDS((t, D_MODEL), F32)],
        scratch_shapes=[pltpu.VMEM((2 * B_GROUP, QBLK, B_KEYS), F32)],
        compiler_params=_cparams(),
    )(qg, qg, kv, kv, kv, kv, bias, sinks)


def _attn_b_bwd(qg, kv, bias, sinks, out_b, lse, dz, bucket_onehot):
    t = qg.shape[0]
    nq = t // QBLK
    q_spec, g_spec, kp_spec, kc_spec, vp_spec, vc_spec, bias_spec, sink_spec = _b_specs()

    def body(q_ref, g_ref, kp, kc, vp, vc, w_ref, sink_ref, o_ref, lse_ref, dz_ref, oh_ref,
             dqg_ref, dkv_ref, dt5_ref, dsink_ref, gt_acc, b_ref):
        kvh = pl.program_id(0)
        j = pl.program_id(1)
        _fill_bias(B_GROUP, lambda h: w_ref[h:h + 1, :], B_BAND, 128, 1, b_ref, j)

        @pl.when(jnp.logical_and(kvh == 0, j == 0))
        def _():
            dkv_ref[...] = jnp.zeros_like(dkv_ref)

        @pl.when(j == 0)
        def _():
            gt_acc[...] = jnp.zeros_like(gt_acc)
            dsink_ref[...] = jnp.zeros_like(dsink_ref)

        early = (j < 1).astype(jnp.int32)
        lane_lo = _lane_lo()
        k_lo, k_hi, v_lo, v_hi = _b_operands(kp, kc, vp, vc, kvh)
        dk_blk = jnp.zeros((B_KEYS, 128), F32)
        dv_blk = jnp.zeros((B_KEYS, 128), F32)
        for pp in range(B_GROUP // 2):
            cols = slice(128 * pp, 128 * (pp + 1))
            qp = q_ref[:, cols]
            o = o_ref[:, cols].astype(F32)
            lse_pair = lse_ref[:, cols]
            dzf = dz_ref[:, cols].astype(F32)
            silu, dsilu = _silu_parts(g_ref[:, cols].astype(F32))
            do = dzf * silu
            dqg_ref[1, :, cols] = (dzf * o * dsilu).astype(BF16)
            doo = do * o
            dqs = []
            for hh in range(2):
                g = 2 * pp + hh
                sel = lane_lo if hh == 0 else jnp.logical_not(lane_lo)
                sink = sink_ref[kvh * B_GROUP + g]
                kk = k_lo if hh == 0 else k_hi
                vv = v_lo if hh == 0 else v_hi
                qm, s = _head_logits(qp, kk, b_ref, g + B_GROUP * early, sel)
                lse_h = _own_everywhere(lse_pair, sel)
                p = jnp.exp(_minus_rows(s, lse_h))
                delta = jnp.sum(jnp.where(sel, doo, 0.0), axis=-1, keepdims=True)
                dom = jnp.where(sel, do, 0.0).astype(BF16)
                dp = _dot_nt(dom, vv)
                ds = p * (dp - delta)
                gt_acc[g] += ds
                dsink_ref[g:g + 1, :] -= jnp.sum(jnp.exp(sink - lse_h) * delta, axis=0, keepdims=True)
                dsb = ds.astype(BF16)
                dqs.append(_dot(dsb, kk) * SCALE)
                dk_blk = dk_blk + _dot_tn(dsb, qm)
                dv_blk = dv_blk + _dot_tn(p.astype(BF16), dom)
            dqg_ref[0, :, cols] = jnp.where(lane_lo, dqs[0], dqs[1]).astype(BF16)
        mine = lane_lo == (kvh == 0)
        dk_add = jnp.where(mine, dk_blk + pltpu.roll(dk_blk, HEAD_DIM, 1), 0.0)
        dv_add = jnp.where(mine, dv_blk + pltpu.roll(dv_blk, HEAD_DIM, 1), 0.0)

        @pl.when(j >= 1)
        def _():
            rows = pl.ds(pl.multiple_of((2 * j - 1) * 128, 128), 128)
            dkv_ref[0, rows, :] += dk_add[0:128]
            dkv_ref[1, rows, :] += dv_add[0:128]

        rows = pl.ds(pl.multiple_of(j * QBLK, QBLK), QBLK)
        dkv_ref[0, rows, :] += dk_add[128:B_KEYS]
        dkv_ref[1, rows, :] += dv_add[128:B_KEYS]

        @pl.when(j == nq - 1)
        def _():
            dd = jnp.concatenate([_offset_sums(gt_acc[g]) for g in range(B_GROUP)], axis=0)
            hi = dd.astype(BF16)
            lo = (dd - hi.astype(F32)).astype(BF16)
            dt5_ref[...] = _dot(hi, oh_ref[...]) + _dot(lo, oh_ref[...])

    blk = pl.BlockSpec((QBLK, 512), lambda h, j: (j, h))
    return pl.pallas_call(
        body, name="attn_b_bwd", grid=(B_KV_HEADS, nq),
        in_specs=[q_spec, g_spec, kp_spec, kc_spec, vp_spec, vc_spec, bias_spec, sink_spec, blk, blk, blk,
                  pl.BlockSpec((B_DIAG, 128), lambda h, j: (0, 0))],
        out_specs=[pl.BlockSpec((2, QBLK, 512), lambda h, j: (0, j, h)),
                   pl.BlockSpec((2, t, 128), lambda h, j: (0, 0, 0)),
                   pl.BlockSpec((B_GROUP, 128), lambda h, j: (h, 0)),
                   pl.BlockSpec((B_GROUP, 128), lambda h, j: (h, 0))],
        out_shape=[SDS((2, t, D_MODEL), BF16), SDS((2, t, 128), F32),
                   SDS((N_HEADS, 128), F32), SDS((N_HEADS, 128), F32)],
        scratch_shapes=[pltpu.VMEM((B_GROUP, QBLK, B_KEYS), F32), pltpu.VMEM((2 * B_GROUP, QBLK, B_KEYS), F32)],
        compiler_params=_cparams(),
    )(qg, qg, kv, kv, kv, kv, bias, sinks, out_b, lse, dz, bucket_onehot)


def _a_bias_by_offset(rel_bias):
    m = np.arange(A_DIAG)
    idx = np.clip(A_BAND - 1 - m, -A_REL_CLIP, A_REL_CLIP) + A_REL_CLIP
    by_head = rel_bias[idx].T.reshape(N_HEADS // 2, 2, A_DIAG)
    return jnp.concatenate([by_head, jnp.zeros((N_HEADS // 2, 6, A_DIAG), F32)], axis=1)


def _a_bias_grad(offset_sums):
    first = 319
    tail = jnp.sum(offset_sums[:, :first], axis=1)
    body = jnp.flip(offset_sums[:, first:first + 320], axis=1)
    body = body.at[:, -1].add(tail)
    full = jnp.concatenate([jnp.zeros((N_HEADS, 193), F32), body], axis=1)
    return full.T


def _t5_bucket(rel):
    nb = T5_BUCKETS // 2
    max_exact = nb // 2
    ret = jnp.where(rel > 0, nb, 0)
    n = jnp.abs(rel)
    nf = jnp.maximum(n, 1).astype(jnp.float32)
    large = max_exact + (jnp.log(nf / max_exact) / math.log(T5_MAX_DIST / max_exact)
                         * (nb - max_exact)).astype(jnp.int32)
    large = jnp.minimum(large, nb - 1)
    return ret + jnp.where(n < max_exact, n, large)


def _b_offset_buckets():
    return _t5_bucket(jnp.arange(B_DIAG, dtype=jnp.int32) - (B_LEFT_CHUNKS * CHUNK + CHUNK - 1))


def _b_bias_by_offset(t5_table):
    return t5_table[_b_offset_buckets()].T


def _b_bucket_onehot():
    return (_b_offset_buckets()[:, None] == jnp.arange(128)[None, :]).astype(BF16)


def _local_step(x, target, a_gain, w_in_a, rel_bias, w_out_a, kv_gain, kv_w, t5_table,
                b_gain, w_in_b, sinks, w_out_b, f_gain):
    a_bias = _a_bias_by_offset(rel_bias)
    b_bias = _b_bias_by_offset(t5_table)
    sinks_flat = sinks.reshape(N_HEADS)

    xn, qkvg = _norm_matmul(x, a_gain, w_in_a)
    z_a, out_a, lse_a = _attn_a_fwd(qkvg, a_bias)
    h1, kvn, hb, kv, qg = _layer_a_out(x, z_a, w_out_a, kv_gain, b_gain, kv_w, w_in_b)
    z_b, out_b, lse_b = _attn_b_fwd(qg, kv, b_bias, sinks_flat)
    dh2, dh2b, dz_b, loss, d_fn = _layer_b_out_loss(h1, z_b, w_out_b, f_gain, target)

    dqg_b, dkv_b, d_t5, d_sink = _attn_b_bwd(qg, kv, b_bias, sinks_flat, out_b, lse_b, dz_b, _b_bucket_onehot())
    dh1, dh1b, dz_a, d_bn, d_kn = _layer_b_in_bwd(dqg_b, dkv_b, w_in_b, kv_w, h1, dh2, b_gain, kv_gain, w_out_a)
    dqg_a, dkv_a, d_rel = _attn_a_bwd(qkvg, a_bias, out_a, lse_a, dz_a)
    grad_x, d_an = _layer_a_in_bwd(dqg_a, dkv_a, w_in_a, x, dh1, a_gain)

    g_w_out_b = _weight_grad_rows("grad_b_w_out", z_b, dh2b[None])
    g_w_in_b = _weight_grad_cols("grad_b_w_in", hb, dqg_b,
                                 [(o, c, 4 * o + c) for o in range(2) for c in range(4)], 256)
    g_kv_w = _weight_grad_rows("grad_kv_w", kvn, dkv_b)
    g_w_out_a = _weight_grad_rows("grad_a_w_out", z_a, dh1b[None])
    g_w_in_a = _weight_grad_cols("grad_a_w_in_qg", xn, dqg_a, [(0, 0, 0), (0, 1, 1), (1, 0, 6), (1, 1, 7)], 512)
    g_w_in_a = _weight_grad_cols("grad_a_w_in_kv", xn, dkv_a, [(0, 0, 2), (0, 1, 3), (1, 0, 4), (1, 1, 5)], 512,
                                 prev=g_w_in_a)

    return dict(
        loss=loss[0, 0], grad_x=grad_x,
        a_norm=d_an, a_w_in=g_w_in_a, a_rel_bias=_a_bias_grad(d_rel[:, :2].reshape(N_HEADS, A_DIAG)),
        a_w_out=g_w_out_a,
        kv_norm=d_kn, kv_w=g_kv_w, t5_bias=d_t5[:, :T5_BUCKETS].T, b_norm=d_bn, b_w_in=g_w_in_b,
        b_sinks=d_sink[:, 0].reshape(1, N_HEADS), b_w_out=g_w_out_b, final_norm=d_fn)


def _place():
    x, y, c = lax.axis_index("x"), lax.axis_index("y"), lax.axis_index("c")
    chips = [(1 - x, y), (x, 1 - y), (1 - x, 1 - y)]
    return x, y, c, chips


def _slot(px, py, pc):
    return 4 * px + 2 * py + pc


ANY = pl.BlockSpec(memory_space=pl.ANY)


def _all_gather(shards):
    n = len(shards)

    def body(*refs):
        ins, outs = refs[:n], refs[n:2 * n]
        send_sems, recv_sems, local_sems = refs[2 * n:]
        x, y, c, chips = _place()
        me, sibling = (x, y, c), (x, y, 1 - c)

        def copy(t, k, block, to, src=None):
            dst = outs[t].at[_slot(*block)]
            return pltpu.make_async_remote_copy(
                src_ref=dst if src is None else src, dst_ref=dst,
                send_sem=send_sems.at[7 * t + k], recv_sem=recv_sems.at[7 * t + k],
                device_id=to, device_id_type=MESH)

        mine = [pltpu.make_async_copy(ins[t], outs[t].at[_slot(*me)], local_sems.at[t]) for t in range(n)]
        for cp in mine:
            cp.start()
        first = []
        for t in range(n):
            first.append(copy(t, 0, me, sibling, src=ins[t]))
            first += [copy(t, 1 + j, me, (*chip, c), src=ins[t]) for j, chip in enumerate(chips)]
        for cp in first:
            cp.start()
        passed = []
        for t in range(n):
            for j, chip in enumerate(chips):
                copy(t, 1 + j, (*chip, c), me).wait_recv()
                cp = copy(t, 4 + j, (*chip, c), sibling)
                cp.start()
                passed.append(cp)
        for t in range(n):
            copy(t, 0, sibling, me).wait_recv()
            for j, chip in enumerate(chips):
                copy(t, 4 + j, (*chip, 1 - c), me).wait_recv()
        for cp in first + passed:
            cp.wait_send()
        for cp in mine:
            cp.wait()

    return pl.pallas_call(
        body, name="all_gather_weights",
        in_specs=[ANY] * n, out_specs=[ANY] * n,
        out_shape=[SDS((N_DEV, *s.shape), s.dtype) for s in shards],
        scratch_shapes=[pltpu.SemaphoreType.DMA((7 * n,)), pltpu.SemaphoreType.DMA((7 * n,)),
                        pltpu.SemaphoreType.DMA((n,))],
    )(*shards)


def _exchange_sibling(grads):
    n = len(grads)

    def body(*refs):
        ins, outs = refs[:n], refs[n:2 * n]
        send_sems, recv_sems = refs[2 * n:]
        x, y, c, chips = _place()
        sibling = (x, y, 1 - c)
        copies = []
        for t in range(n):
            blocks = [(*chip, 1 - c) for chip in chips] + [sibling]
            for k, block in enumerate(blocks):
                copies.append(pltpu.make_async_remote_copy(
                    src_ref=ins[t].at[_slot(*block)], dst_ref=outs[t].at[k],
                    send_sem=send_sems.at[4 * t + k], recv_sem=recv_sems.at[4 * t + k],
                    device_id=sibling, device_id_type=MESH))
        for cp in copies:
            cp.start()
        for cp in copies:
            cp.wait()

    return pl.pallas_call(
        body, name="grads_to_sibling",
        in_specs=[ANY] * n, out_specs=[ANY] * n,
        out_shape=[SDS((4, *g.shape[1:]), g.dtype) for g in grads],
        scratch_shapes=[pltpu.SemaphoreType.DMA((4 * n,)), pltpu.SemaphoreType.DMA((4 * n,))],
    )(*grads)


def _exchange_chips(pre):
    n = len(pre)

    def body(*refs):
        ins, outs = refs[:n], refs[n:2 * n]
        send_sems, recv_sems = refs[2 * n:]
        x, y, c, chips = _place()
        copies = []
        for t in range(n):
            for j, chip in enumerate(chips):
                copies.append(pltpu.make_async_remote_copy(
                    src_ref=ins[t].at[j], dst_ref=outs[t].at[j],
                    send_sem=send_sems.at[3 * t + j], recv_sem=recv_sems.at[3 * t + j],
                    device_id=(*chip, c), device_id_type=MESH))
        for cp in copies:
            cp.start()
        for cp in copies:
            cp.wait()

    return pl.pallas_call(
        body, name="grads_to_chips",
        in_specs=[ANY] * n, out_specs=[ANY] * n,
        out_shape=[SDS(p.shape, p.dtype) for p in pre],
        scratch_shapes=[pltpu.SemaphoreType.DMA((3 * n,)), pltpu.SemaphoreType.DMA((3 * n,))],
    )(*pre)


def _row_tile(rows):
    return min(rows, 256)


def _pre_reduce(name, g, from_sibling, slots):
    _, r, c = g.shape
    tr = _row_tile(r)

    def body(slots_ref, g_ref, s_ref, o_ref):
        del slots_ref
        o_ref[...] = (g_ref[...] + s_ref[...]).astype(BF16)

    return pl.pallas_call(
        body, name=name,
        grid_spec=pltpu.PrefetchScalarGridSpec(
            num_scalar_prefetch=1, grid=(3, r // tr),
            in_specs=[pl.BlockSpec((1, tr, c), lambda j, i, s: (s[j], i, 0)),
                      pl.BlockSpec((1, tr, c), lambda j, i, s: (j, i, 0))],
            out_specs=pl.BlockSpec((1, tr, c), lambda j, i, s: (j, i, 0))),
        out_shape=SDS((3, r, c), BF16),
        compiler_params=_cparams(),
    )(slots, g, from_sibling)


def _adamw(w, g, m, v):
    m2 = ADAM_B1 * m + (1.0 - ADAM_B1) * g
    v2 = ADAM_B2 * v + (1.0 - ADAM_B2) * jnp.square(g)
    m_hat = m2 / (1.0 - ADAM_B1 ** ADAM_STEP)
    v_hat = v2 / (1.0 - ADAM_B2 ** ADAM_STEP)
    delta = -ADAM_LR * (m_hat / (jnp.sqrt(v_hat) + ADAM_EPS) + ADAM_WD * w)
    return delta, m2, v2


def _reduce_adamw(name, g, from_sibling, from_chips, my_slot, w, m, v):
    _, r, c = g.shape
    tr = _row_tile(r)

    def body(slot_ref, g_ref, s_ref, c_ref, w_ref, m_ref, v_ref, grad_ref, d_ref, nm_ref, nv_ref):
        del slot_ref
        grad = g_ref[0] + s_ref[0]
        for j in range(3):
            grad = grad + c_ref[j].astype(F32)
        grad_ref[...] = grad
        d_ref[...], nm_ref[...], nv_ref[...] = _adamw(w_ref[...], grad, m_ref[...], v_ref[...])

    flat = pl.BlockSpec((tr, c), lambda i, s: (i, 0))
    return pl.pallas_call(
        body, name=name,
        grid_spec=pltpu.PrefetchScalarGridSpec(
            num_scalar_prefetch=1, grid=(r // tr,),
            in_specs=[pl.BlockSpec((1, tr, c), lambda i, s: (s[0], i, 0)),
                      pl.BlockSpec((1, tr, c), lambda i, s: (3, i, 0)),
                      pl.BlockSpec((3, tr, c), lambda i, s: (0, i, 0)),
                      flat, flat, flat],
            out_specs=[flat, flat, flat, flat]),
        out_shape=[SDS((r, c), F32)] * 4,
        compiler_params=_cparams(),
    )(my_slot, g, from_sibling, from_chips, w, m, v)


_SMALL = (("a_norm", 8), ("a_rel_bias", 72), ("kv_norm", 8), ("t5_bias", 8), ("b_norm", 8),
          ("b_sinks", 8), ("final_norm", 8), ("loss", 8))
_SMALL_ROWS = sum(r for _, r in _SMALL)


def _pack_small(parts):
    rows = []
    for name, n_rows in _SMALL:
        flat = parts[name].reshape(-1).astype(F32)
        rows.append(jnp.pad(flat, (0, n_rows * 128 - flat.shape[0])).reshape(n_rows, 128))
    return jnp.concatenate(rows, axis=0)


def _unpack_small(buf, shapes):
    out, at = {}, 0
    for name, n_rows in _SMALL:
        size = int(np.prod(shapes[name])) if shapes[name] else 1
        out[name] = buf[at:at + n_rows].reshape(-1)[:size].reshape(shapes[name])
        at += n_rows
    return out


def _small_allreduce_adamw(gbuf, wbuf, mbuf, vbuf):
    def body(g_ref, w_ref, m_ref, v_ref, sum_ref, d_ref, nm_ref, nv_ref, land_ref, send_sems, recv_sems):
        x, y, c, _ = _place()
        my_slot = _slot(x, y, c)
        land_ref[my_slot] = g_ref[...]
        copies = []
        for k in range(1, N_DEV):
            peer = (x ^ (k >> 2), y ^ ((k >> 1) & 1), c ^ (k & 1))
            copies.append(pltpu.make_async_remote_copy(
                src_ref=g_ref, dst_ref=land_ref.at[my_slot],
                send_sem=send_sems.at[k - 1], recv_sem=recv_sems.at[k - 1],
                device_id=peer, device_id_type=MESH))
        for cp in copies:
            cp.start()
        for k in range(1, N_DEV):
            peer_slot = _slot(x ^ (k >> 2), y ^ ((k >> 1) & 1), c ^ (k & 1))
            pltpu.make_async_remote_copy(
                src_ref=g_ref, dst_ref=land_ref.at[peer_slot],
                send_sem=send_sems.at[k - 1], recv_sem=recv_sems.at[k - 1],
                device_id=(x, y, c), device_id_type=MESH).wait_recv()
        for cp in copies:
            cp.wait_send()
        total = land_ref[0]
        for s in range(1, N_DEV):
            total = total + land_ref[s]
        sum_ref[...] = total
        d_ref[...], nm_ref[...], nv_ref[...] = _adamw(w_ref[...], total, m_ref[...], v_ref[...])

    vm = pl.BlockSpec(memory_space=pltpu.VMEM)
    shape = SDS((_SMALL_ROWS, 128), F32)
    return pl.pallas_call(
        body, name="small_allreduce_adamw",
        in_specs=[vm] * 4, out_specs=[vm] * 4, out_shape=[shape] * 4,
        scratch_shapes=[pltpu.VMEM((N_DEV, _SMALL_ROWS, 128), F32),
                        pltpu.SemaphoreType.DMA((N_DEV - 1,)), pltpu.SemaphoreType.DMA((N_DEV - 1,))],
    )(gbuf, wbuf, mbuf, vbuf)


def kernel(x, a_norm, a_w_in, a_rel_bias, a_w_out, kv_norm, kv_w, t5_bias, b_norm, b_w_in, b_sinks, b_w_out, final_norm, loss_target, m_a_norm, m_a_w_in, m_a_rel_bias, m_a_w_out, m_kv_norm, m_kv_w, m_t5_bias, m_b_norm, m_b_w_in, m_b_sinks, m_b_w_out, m_final_norm, v_a_norm, v_a_w_in, v_a_rel_bias, v_a_w_out, v_kv_norm, v_kv_w, v_t5_bias, v_b_norm, v_b_w_in, v_b_sinks, v_b_w_out, v_final_norm):
    xi, yi, ci = lax.axis_index("x"), lax.axis_index("y"), lax.axis_index("c")
    my_slot = _slot(xi, yi, ci)

    w_in_a, w_in_b, w_out_a, w_out_b, kv_full, a_gain = _all_gather([
        a_w_in[0].astype(BF16), b_w_in[0].astype(BF16), a_w_out[0].astype(BF16),
        b_w_out[0].astype(BF16), kv_w.astype(BF16), a_norm])
    w_out_a = w_out_a.reshape(D_MODEL, D_MODEL)
    w_out_b = w_out_b.reshape(D_MODEL, D_MODEL)
    kv_full = kv_full.reshape(D_MODEL, 2 * 128)
    a_gain = a_gain.reshape(1, D_MODEL)

    loc = _local_step(x[0], loss_target[0], a_gain, w_in_a, a_rel_bias[0], w_out_a,
                      kv_norm.reshape(1, D_MODEL), kv_full, t5_bias, b_norm, w_in_b, b_sinks,
                      w_out_b, final_norm.reshape(1, D_MODEL))

    names = ("a_w_in", "b_w_in", "a_w_out", "b_w_out", "kv_w")
    shard_w = dict(a_w_in=a_w_in[0], b_w_in=b_w_in[0], a_w_out=a_w_out[0], b_w_out=b_w_out[0], kv_w=kv_w)
    shard_m = dict(a_w_in=m_a_w_in[0], b_w_in=m_b_w_in[0], a_w_out=m_a_w_out[0], b_w_out=m_b_w_out[0], kv_w=m_kv_w)
    shard_v = dict(a_w_in=v_a_w_in[0], b_w_in=v_b_w_in[0], a_w_out=v_a_w_out[0], b_w_out=v_b_w_out[0], kv_w=v_kv_w)
    grads = [loc[n] for n in names]
    from_sibling = _exchange_sibling(grads)
    forward_slots = jnp.stack([_slot(1 - xi, yi, ci), _slot(xi, 1 - yi, ci), _slot(1 - xi, 1 - yi, ci)]).astype(jnp.int32)
    pre = [_pre_reduce("chip_sum_" + n, g, s, forward_slots) for n, g, s in zip(names, grads, from_sibling)]
    from_chips = _exchange_chips(pre)
    slot_arr = jnp.reshape(my_slot, (1,)).astype(jnp.int32)
    big = {n: _reduce_adamw("adamw_" + n, g, s, f, slot_arr, shard_w[n], shard_m[n], shard_v[n])
           for n, g, s, f in zip(names, grads, from_sibling, from_chips)}

    def own_row(vec):
        return lax.dynamic_update_slice(jnp.zeros((N_DEV, 128), F32), vec, (my_slot, 0))

    zero = jnp.zeros((), F32)
    small_w = dict(a_norm=own_row(a_norm), a_rel_bias=a_rel_bias, kv_norm=kv_norm, t5_bias=t5_bias,
                   b_norm=b_norm, b_sinks=b_sinks, final_norm=final_norm, loss=zero)
    small_m = dict(a_norm=own_row(m_a_norm), a_rel_bias=m_a_rel_bias, kv_norm=m_kv_norm, t5_bias=m_t5_bias,
                   b_norm=m_b_norm, b_sinks=m_b_sinks, final_norm=m_final_norm, loss=zero)
    small_v = dict(a_norm=own_row(v_a_norm), a_rel_bias=v_a_rel_bias, kv_norm=v_kv_norm, t5_bias=v_t5_bias,
                   b_norm=v_b_norm, b_sinks=v_b_sinks, final_norm=v_final_norm, loss=zero)
    small = _small_allreduce_adamw(_pack_small(loc), _pack_small(small_w), _pack_small(small_m), _pack_small(small_v))
    shapes = dict(a_norm=(N_DEV, 128), a_rel_bias=a_rel_bias.shape, kv_norm=kv_norm.shape, t5_bias=t5_bias.shape,
                  b_norm=b_norm.shape, b_sinks=b_sinks.shape, final_norm=final_norm.shape, loss=())
    sm = [_unpack_small(buf, shapes) for buf in small]
    for part in sm:
        part["a_norm"] = lax.dynamic_slice(part["a_norm"], (my_slot, 0), (1, 128))

    order = ("a_norm", "a_w_in", "a_rel_bias", "a_w_out", "kv_norm", "kv_w", "t5_bias", "b_norm",
             "b_w_in", "b_sinks", "b_w_out", "final_norm")
    lead = dict(a_w_in=True, b_w_in=True, a_w_out=True, b_w_out=True, kv_w=False)

    def pick(kind, name):
        if name in big:
            val = big[name][kind]
            return val[None] if lead[name] else val
        return sm[kind][name]

    outs = [sm[0]["loss"], loc["grad_x"][None]]
    for kind in range(4):
        outs += [pick(kind, n) for n in order]
    return tuple(outs)
```

```python
import functools
import math

import numpy as np
import jax
import jax.numpy as jnp
from jax import lax
from jax.experimental import pallas as pl
from jax.experimental.pallas import tpu as pltpu

F32 = jnp.float32
BF16 = jnp.bfloat16
SDS = jax.ShapeDtypeStruct

D_MODEL = 1024
HEAD_DIM = 64
CHUNK = 64
N_HEADS = 16
RMS_EPS = 1e-6
A_LEFT_CHUNKS = 8
A_BAND = (A_LEFT_CHUNKS + 1) * CHUNK
A_REL_CLIP = 256
B_KV_HEADS = 2
B_GROUP = 8
B_LEFT_CHUNKS = 2
B_BAND = (B_LEFT_CHUNKS + 1) * CHUNK
T5_BUCKETS = 32
T5_MAX_DIST = 128
QBLK = 256
A_KEYS = 3 * QBLK
B_KEYS = QBLK + 128
A_DIAG = A_KEYS
B_DIAG = B_KEYS
NEG = -1e30
SCALE = HEAD_DIM ** -0.5
N_DEV = 8

ADAM_LR = 0.001
ADAM_B1 = 0.9
ADAM_B2 = 0.999
ADAM_EPS = 1e-08
ADAM_WD = 0.01
ADAM_STEP = 10

VMEM_LIMIT_BYTES = 56 * 1024 * 1024
MESH = pl.DeviceIdType.MESH


def _cparams():
    return pltpu.CompilerParams(vmem_limit_bytes=VMEM_LIMIT_BYTES)


def _dot(a, b):
    return jnp.dot(a, b, preferred_element_type=F32)


def _dot_nt(a, b):
    return lax.dot_general(a, b, (((1,), (1,)), ((), ())), preferred_element_type=F32)


def _dot_tn(a, b):
    return lax.dot_general(a, b, (((0,), (0,)), ((), ())), preferred_element_type=F32)


def _rstd(xf):
    return lax.rsqrt(jnp.mean(xf * xf, axis=-1, keepdims=True) + RMS_EPS)


def _sigmoid(x):
    return 1.0 / (1.0 + jnp.exp(-x))


def _norm_matmul(x, gain, w):
    t = x.shape[0]
    nb, _, tn = w.shape
    tm = min(t, 1024)

    def body(x_ref, g_ref, w_ref, xn_ref, o_ref):
        @pl.when(pl.program_id(1) == 0)
        def _():
            xf = x_ref[...]
            xn_ref[...] = ((xf * _rstd(xf)) * g_ref[...]).astype(BF16)

        o_ref[...] = _dot(xn_ref[...], w_ref[0]).astype(BF16)

    return pl.pallas_call(
        body, name="norm_matmul", grid=(t // tm, nb),
        in_specs=[pl.BlockSpec((tm, D_MODEL), lambda m, n: (m, 0)),
                  pl.BlockSpec((1, D_MODEL), lambda m, n: (0, 0)),
                  pl.BlockSpec((1, D_MODEL, tn), lambda m, n: (n, 0, 0))],
        out_specs=[pl.BlockSpec((tm, D_MODEL), lambda m, n: (m, 0)),
                   pl.BlockSpec((tm, tn), lambda m, n: (m, n))],
        out_shape=[SDS((t, D_MODEL), BF16), SDS((t, nb * tn), BF16)],
        compiler_params=_cparams(),
    )(x, gain, w)


def _layer_a_out(x, z, w_out, kv_gain, b_gain, kv_w, w_in_b):
    t = x.shape[0]
    tm = min(t, 512)
    nb, _, tn = w_in_b.shape

    def body(x_ref, z_ref, wo_ref, kvg_ref, bg_ref, kvw_ref, wb_ref,
             h1_ref, kvn_ref, hb_ref, kv_ref, qg_ref):
        h1 = x_ref[...] + _dot(z_ref[...], wo_ref[...])
        h1_ref[...] = h1
        y0 = h1 * _rstd(h1)
        kvn = (y0 * kvg_ref[...]).astype(BF16)
        hb = (y0 * bg_ref[...]).astype(BF16)
        kvn_ref[...] = kvn
        hb_ref[...] = hb
        kv_ref[...] = _dot(kvn, kvw_ref[...]).astype(BF16)
        for i in range(nb):
            qg_ref[:, i * tn:(i + 1) * tn] = _dot(hb, wb_ref[i]).astype(BF16)

    row = lambda m: (m, 0)
    fix2 = lambda m: (0, 0)
    return pl.pallas_call(
        body, name="layer_a_out", grid=(t // tm,),
        in_specs=[pl.BlockSpec((tm, D_MODEL), row), pl.BlockSpec((tm, D_MODEL), row),
                  pl.BlockSpec((D_MODEL, D_MODEL), fix2),
                  pl.BlockSpec((1, D_MODEL), fix2), pl.BlockSpec((1, D_MODEL), fix2),
                  pl.BlockSpec((D_MODEL, 256), fix2),
                  pl.BlockSpec((nb, D_MODEL, tn), lambda m: (0, 0, 0))],
        out_specs=[pl.BlockSpec((tm, D_MODEL), row), pl.BlockSpec((tm, D_MODEL), row),
                   pl.BlockSpec((tm, D_MODEL), row), pl.BlockSpec((tm, 256), row),
                   pl.BlockSpec((tm, nb * tn), row)],
        out_shape=[SDS((t, D_MODEL), F32), SDS((t, D_MODEL), BF16), SDS((t, D_MODEL), BF16),
                   SDS((t, 256), BF16), SDS((t, nb * tn), BF16)],
        compiler_params=_cparams(),
    )(x, z, w_out, kv_gain, b_gain, kv_w, w_in_b)


def _layer_b_out_loss(h1, z, w_out, f_gain, target):
    t = h1.shape[0]
    tm = min(t, 512)

    def body(h1_ref, z_ref, wo_ref, fg_ref, tgt_ref,
             dh2_ref, dh2b_ref, dz_ref, loss_ref, dfn_ref):
        @pl.when(pl.program_id(0) == 0)
        def _():
            loss_ref[...] = jnp.zeros_like(loss_ref)
            dfn_ref[...] = jnp.zeros_like(dfn_ref)

        h2 = h1_ref[...] + _dot(z_ref[...], wo_ref[...])
        r = _rstd(h2)
        yn = h2 * r
        fg = fg_ref[...]
        err = yn * fg - tgt_ref[...]
        loss_ref[...] += (0.5 / D_MODEL) * jnp.sum(err * err)
        dy = err * (1.0 / D_MODEL)
        dfn_ref[...] += jnp.sum(dy * yn, axis=0, keepdims=True)
        u = dy * fg
        dh2 = r * u - h2 * ((r * r * r) * jnp.mean(u * h2, axis=-1, keepdims=True))
        dh2_ref[...] = dh2
        dh2b = dh2.astype(BF16)
        dh2b_ref[...] = dh2b
        dz_ref[...] = _dot_nt(dh2b, wo_ref[...]).astype(BF16)

    row = lambda m: (m, 0)
    fix2 = lambda m: (0, 0)
    return pl.pallas_call(
        body, name="layer_b_out_loss", grid=(t // tm,),
        in_specs=[pl.BlockSpec((tm, D_MODEL), row), pl.BlockSpec((tm, D_MODEL), row),
                  pl.BlockSpec((D_MODEL, D_MODEL), fix2), pl.BlockSpec((1, D_MODEL), fix2),
                  pl.BlockSpec((tm, D_MODEL), row)],
        out_specs=[pl.BlockSpec((tm, D_MODEL), row), pl.BlockSpec((tm, D_MODEL), row),
                   pl.BlockSpec((tm, D_MODEL), row), pl.BlockSpec((1, 128), fix2),
                   pl.BlockSpec((1, D_MODEL), fix2)],
        out_shape=[SDS((t, D_MODEL), F32), SDS((t, D_MODEL), BF16), SDS((t, D_MODEL), BF16),
                   SDS((1, 128), F32), SDS((1, D_MODEL), F32)],
        compiler_params=_cparams(),
    )(h1, z, w_out, f_gain, target)


def _layer_b_in_bwd(dqg, dkv, w_in_b, kv_w, h1, dh2, b_gain, kv_gain, w_out_a):
    t = h1.shape[0]
    tm = min(t, 256)
    nb, _, tn = w_in_b.shape
    per = D_MODEL // tn

    def body(dqg_ref, dkv_ref, wb_ref, kvw_ref, h1_ref, dh2_ref, bg_ref, kvg_ref, wo_ref,
             dh1_ref, dh1b_ref, dz_ref, dbn_ref, dkn_ref):
        @pl.when(pl.program_id(0) == 0)
        def _():
            dbn_ref[...] = jnp.zeros_like(dbn_ref)
            dkn_ref[...] = jnp.zeros_like(dkn_ref)

        dhb = jnp.zeros((tm, D_MODEL), F32)
        for i in range(nb):
            blk = dqg_ref[i // per, :, (i % per) * tn:(i % per + 1) * tn]
            dhb = dhb + _dot_nt(blk, wb_ref[i])
        dkn = (_dot_nt(dkv_ref[0].astype(BF16), kvw_ref[:, 0:128])
               + _dot_nt(dkv_ref[1].astype(BF16), kvw_ref[:, 128:256]))
        h1 = h1_ref[...]
        r = _rstd(h1)
        xr = h1 * r
        dbn_ref[...] += jnp.sum(dhb * xr, axis=0, keepdims=True)
        dkn_ref[...] += jnp.sum(dkn * xr, axis=0, keepdims=True)
        u = dhb * bg_ref[...] + dkn * kvg_ref[...]
        dh1 = dh2_ref[...] + r * u - h1 * ((r * r * r) * jnp.mean(u * h1, axis=-1, keepdims=True))
        dh1_ref[...] = dh1
        dh1b = dh1.astype(BF16)
        dh1b_ref[...] = dh1b
        dz_ref[...] = _dot_nt(dh1b, wo_ref[...]).astype(BF16)

    row = lambda m: (m, 0)
    fix2 = lambda m: (0, 0)
    return pl.pallas_call(
        body, name="layer_b_in_bwd", grid=(t // tm,),
        in_specs=[pl.BlockSpec((2, tm, D_MODEL), lambda m: (0, m, 0)),
                  pl.BlockSpec((2, tm, 128), lambda m: (0, m, 0)),
                  pl.BlockSpec((nb, D_MODEL, tn), lambda m: (0, 0, 0)),
                  pl.BlockSpec((D_MODEL, 256), fix2),
                  pl.BlockSpec((tm, D_MODEL), row), pl.BlockSpec((tm, D_MODEL), row),
                  pl.BlockSpec((1, D_MODEL), fix2), pl.BlockSpec((1, D_MODEL), fix2),
                  pl.BlockSpec((D_MODEL, D_MODEL), fix2)],
        out_specs=[pl.BlockSpec((tm, D_MODEL), row), pl.BlockSpec((tm, D_MODEL), row),
                   pl.BlockSpec((tm, D_MODEL), row), pl.BlockSpec((1, D_MODEL), fix2),
                   pl.BlockSpec((1, D_MODEL), fix2)],
        out_shape=[SDS((t, D_MODEL), F32), SDS((t, D_MODEL), BF16), SDS((t, D_MODEL), BF16),
                   SDS((1, D_MODEL), F32), SDS((1, D_MODEL), F32)],
        compiler_params=_cparams(),
    )(dqg, dkv, w_in_b, kv_w, h1, dh2, b_gain, kv_gain, w_out_a)


def _layer_a_in_bwd(dqg, dkv, w_in_a, x, dh1, a_gain, chip_sums):
    t = x.shape[0]
    tm = min(t, 256)
    nb, _, tn = w_in_a.shape
    per = D_MODEL // tn

    def body(dqg_ref, dkv_ref, w_ref, x_ref, dh1_ref, ag_ref, sums_ref, dx_ref, dan_ref, land_ref,
             send_sems, recv_sems):
        @pl.when(pl.program_id(0) == 0)
        def _():
            dan_ref[...] = jnp.zeros_like(dan_ref)
            for cp in _chip_copies(sums_ref, land_ref, send_sems, recv_sems):
                cp.start()

        dxn = jnp.zeros((tm, D_MODEL), F32)
        for i in range(nb):
            part = i // per
            src = dqg_ref if part in (0, 3) else dkv_ref
            outer = {0: 0, 3: 1, 1: 0, 2: 1}[part]
            blk = src[outer, :, (i % per) * tn:(i % per + 1) * tn]
            dxn = dxn + _dot_nt(blk, w_ref[i])
        xf = x_ref[...]
        r = _rstd(xf)
        dan_ref[...] += jnp.sum(dxn * (xf * r), axis=0, keepdims=True)
        u = dxn * ag_ref[...]
        dx_ref[...] = dh1_ref[...] + r * u - xf * ((r * r * r) * jnp.mean(u * xf, axis=-1, keepdims=True))

        @pl.when(pl.program_id(0) == t // tm - 1)
        def _():
            for cp in _chip_copies(sums_ref, land_ref, send_sems, recv_sems):
                cp.wait()

    row = lambda m: (m, 0)
    fix2 = lambda m: (0, 0)
    return pl.pallas_call(
        body, name="layer_a_in_bwd", grid=(t // tm,),
        in_specs=[pl.BlockSpec((2, tm, D_MODEL), lambda m: (0, m, 0)),
                  pl.BlockSpec((2, tm, D_MODEL), lambda m: (0, m, 0)),
                  pl.BlockSpec((nb, D_MODEL, tn), lambda m: (0, 0, 0)),
                  pl.BlockSpec((tm, D_MODEL), row), pl.BlockSpec((tm, D_MODEL), row),
                  pl.BlockSpec((1, D_MODEL), fix2), ANY],
        out_specs=[pl.BlockSpec((tm, D_MODEL), row), pl.BlockSpec((1, D_MODEL), fix2), ANY],
        out_shape=[SDS((t, D_MODEL), F32), SDS((1, D_MODEL), F32), SDS(chip_sums.shape, chip_sums.dtype)],
        scratch_shapes=[pltpu.SemaphoreType.DMA((3,)), pltpu.SemaphoreType.DMA((3,))],
        compiler_params=_cparams(),
    )(dqg, dkv, w_in_a, x, dh1, a_gain, chip_sums)


def _lut(s, vals):
    r = jnp.int32(vals[0])
    for i in range(1, len(vals)):
        r = jnp.where(s == i, jnp.int32(vals[i]), r)
    return r


def _held(steps, i):
    seq, cur = [None] * len(steps), None
    for k in range(len(steps) - 1, -1, -1):
        if steps[k][0] == i:
            cur = steps[k][1:3]
        seq[k] = cur
    for k in range(len(steps)):
        cur = seq[k] = seq[k] if seq[k] is not None else cur
    return seq


def _weight_grad_cols(name, my_slot, a, bs, steps, tn):
    t, dw = a.shape
    n_arr = len(bs)
    which = [s[0] for s in steps]
    blks = [s[3] for s in steps]

    def body(slot_ref, a_ref, *rest):
        b_refs, (o_ref, own_ref, at_ref) = rest[:n_arr], rest[n_arr:]
        s = pl.program_id(0)

        @pl.when(s == 0)
        def _():
            at_ref[...] = a_ref[...].T

        for i in range(n_arr):
            @pl.when(_lut(s, which) == i)
            def _(i=i):
                res = _dot(at_ref[...], b_refs[i][0])
                o_ref[0] = res.astype(BF16)

                @pl.when(_lut(s, blks) == slot_ref[0])
                def _():
                    own_ref[...] = res

    def b_spec(i):
        held = _held(steps, i)
        return pl.BlockSpec((1, t, tn), lambda s, slot: (_lut(s, [h[0] for h in held]), 0,
                                                         _lut(s, [h[1] for h in held])))

    return pl.pallas_call(
        body, name=name,
        grid_spec=pltpu.PrefetchScalarGridSpec(
            num_scalar_prefetch=1, grid=(len(steps),),
            in_specs=[pl.BlockSpec((t, dw), lambda s, slot: (0, 0))] + [b_spec(i) for i in range(n_arr)],
            out_specs=[pl.BlockSpec((1, dw, tn), lambda s, slot: (_lut(s, blks), 0, 0)),
                       pl.BlockSpec((dw, tn), lambda s, slot: (0, 0))],
            scratch_shapes=[pltpu.VMEM((dw, t), BF16)]),
        out_shape=[SDS((N_DEV, dw, tn), BF16), SDS((dw, tn), F32)],
        compiler_params=_cparams(),
    )(my_slot, a, *bs)


def _weight_grad_rows(name, my_slot, a, b):
    t, dw = a.shape
    n_o, _, c = b.shape
    rows = dw // N_DEV

    def body(slot_ref, a_ref, b_ref, o_ref, own_ref):
        at = a_ref[...].T
        res = [_dot(at, b_ref[o].astype(BF16)) for o in range(n_o)]
        for o in range(n_o):
            o_ref[0, :, o * c:(o + 1) * c] = res[o].astype(BF16)

        @pl.when(pl.program_id(0) == slot_ref[0])
        def _():
            for o in range(n_o):
                own_ref[:, o * c:(o + 1) * c] = res[o]

    return pl.pallas_call(
        body, name=name,
        grid_spec=pltpu.PrefetchScalarGridSpec(
            num_scalar_prefetch=1, grid=(N_DEV,),
            in_specs=[pl.BlockSpec((t, rows), lambda s, slot: (0, s)),
                      pl.BlockSpec((n_o, t, c), lambda s, slot: (0, 0, 0))],
            out_specs=[pl.BlockSpec((1, rows, n_o * c), lambda s, slot: (s, 0, 0)),
                       pl.BlockSpec((rows, n_o * c), lambda s, slot: (0, 0))]),
        out_shape=[SDS((N_DEV, rows, n_o * c), BF16), SDS((rows, n_o * c), F32)],
        compiler_params=_cparams(),
    )(my_slot, a, b)


def _lane_lo():
    return lax.broadcasted_iota(jnp.int32, (1, 128), 1) < HEAD_DIM


def _offset_sums(gt):
    keys = gt.shape[1]
    gc = gt[0:CHUNK]
    for cc in range(1, QBLK // CHUNK):
        gc = gc + pltpu.roll(gt[cc * CHUNK:(cc + 1) * CHUNK], keys - cc * CHUNK, 1)
    hi = gc.astype(BF16)
    lo = (gc - hi.astype(F32)).astype(BF16)
    flip = (lax.broadcasted_iota(jnp.int32, (CHUNK, CHUNK), 0)
            + lax.broadcasted_iota(jnp.int32, (CHUNK, CHUNK), 1) == CHUNK - 1).astype(BF16)
    gf = _dot(flip, hi) + _dot(flip, lo)
    skew = pltpu.roll(gf, 0, 1, stride=1, stride_axis=0)
    return jnp.sum(skew, axis=0, keepdims=True)


def _band_bias(w_row, band):
    keys = w_row.shape[1]
    base = jnp.broadcast_to(w_row, (CHUNK, keys))
    skew = pltpu.roll(base, 0, 1, stride=1, stride_axis=0)
    skew = pltpu.roll(skew, keys - (CHUNK - 1), 1)
    col = lax.broadcasted_iota(jnp.int32, (CHUNK, keys), 1)
    chunk0 = jnp.where(col < band, skew, NEG)
    return jnp.concatenate(
        [chunk0] + [pltpu.roll(chunk0, cc * CHUNK, 1) for cc in range(1, QBLK // CHUNK)], axis=0)


def _silu_parts(g):
    sg = _sigmoid(g)
    return g * sg, sg * (1.0 + g * (1.0 - sg))


def _a_specs(t):
    nq = t // QBLK
    del nq
    q = pl.BlockSpec((QBLK, 128), lambda p, j: (j, p))
    ks = [pl.BlockSpec((QBLK, 128), lambda p, j, b=b: (jnp.maximum(j - 2 + b, 0), 8 + p)) for b in range(3)]
    vs = [pl.BlockSpec((QBLK, 128), lambda p, j, b=b: (jnp.maximum(j - 2 + b, 0), 16 + p)) for b in range(3)]
    g = pl.BlockSpec((QBLK, 128), lambda p, j: (j, 24 + p))
    bias = pl.BlockSpec((1, 8, A_KEYS), lambda p, j: (p, 0, 0))
    return q, ks, vs, g, bias


def _fill_bias(n, get_row, band, first_valid_col, early, bias_scr, j):
    @pl.when(j == 0)
    def _():
        for h in range(n):
            bias_scr[h] = _band_bias(get_row(h), band)

    @pl.when(j < early)
    def _():
        keys = bias_scr.shape[2]
        col_ok = lax.broadcasted_iota(jnp.int32, (1, keys), 1) >= first_valid_col
        for h in range(n):
            bias_scr[n + h] = jnp.where(col_ok, bias_scr[h], NEG)


def _head_logits(q, k, bias_scr, idx, sel):
    qm = jnp.where(sel, q, jnp.zeros_like(q)) * SCALE
    return qm, _dot_nt(qm, k) + bias_scr[idx]


def _row_sums_everywhere(r, sel):
    return jnp.where(sel, pltpu.roll(r, HEAD_DIM, 1), r)


def _own_everywhere(x, sel):
    return jnp.where(sel, x, pltpu.roll(x, HEAD_DIM, 1))


def _minus_rows(s, row_full):
    return jnp.concatenate([s[:, i:i + 128] - row_full for i in range(0, s.shape[1], 128)], axis=1)


def _attn_a_fwd(qkvg, bias, gather):
    t = qkvg.shape[0]
    nq = t // QBLK
    n_g = len(gather)
    q_spec, k_specs, v_specs, g_spec, bias_spec = _a_specs(t)

    def body(q_ref, k0, k1, k2, v0, v1, v2, g_ref, w_ref, *rest):
        shard_refs, rest = rest[:n_g], rest[n_g:]
        z_ref, o_ref, lse_ref = rest[:3]
        full_refs, (b_ref, *comm) = rest[3:3 + n_g], rest[3 + n_g:]
        p = pl.program_id(0)
        j = pl.program_id(1)
        start, forward, finish = _gather_phases(shard_refs, full_refs, *comm)
        pl.when(jnp.logical_and(p == 0, j == 0))(start)
        pl.when(jnp.logical_and(p == N_HEADS // 4, j == 0))(forward)
        _fill_bias(2, lambda h: w_ref[0, h:h + 1, :], A_BAND, QBLK * (2 - j), 2, b_ref, j)
        early = (j < 2).astype(jnp.int32)
        lane_lo = _lane_lo()
        q = q_ref[...]
        k = jnp.concatenate([k0[...], k1[...], k2[...]], axis=0)
        v = jnp.concatenate([v0[...], v1[...], v2[...]], axis=0)
        outs, lses = [], []
        for hh in range(2):
            sel = lane_lo if hh == 0 else jnp.logical_not(lane_lo)
            _, s = _head_logits(q, k, b_ref, hh + 2 * early, sel)
            mx = jnp.max(s, axis=-1, keepdims=True)
            e = jnp.exp(s - mx).astype(BF16)
            r = _dot(e, jnp.where(sel, v, jnp.ones_like(v)))
            l = _row_sums_everywhere(r, sel)
            outs.append(r / l)
            lses.append(mx + jnp.log(l))
        o = jnp.where(lane_lo, outs[0], outs[1])
        silu, _ = _silu_parts(g_ref[...].astype(F32))
        o_ref[...] = o.astype(BF16)
        z_ref[...] = (o * silu).astype(BF16)
        lse_ref[...] = jnp.where(lane_lo, lses[0], lses[1])
        pl.when(jnp.logical_and(p == N_HEADS // 2 - 1, j == nq - 1))(finish)

    out_spec = pl.BlockSpec((QBLK, 128), lambda p, j: (j, p))
    outs = pl.pallas_call(
        body, name="attn_a_fwd", grid=(N_HEADS // 2, nq),
        in_specs=[q_spec, *k_specs, *v_specs, g_spec, bias_spec] + [ANY] * n_g,
        out_specs=[out_spec, out_spec, out_spec] + [ANY] * n_g,
        out_shape=[SDS((t, D_MODEL), BF16), SDS((t, D_MODEL), BF16), SDS((t, D_MODEL), F32)]
        + [SDS((N_DEV, *s.shape), s.dtype) for s in gather],
        scratch_shapes=[pltpu.VMEM((4, QBLK, A_KEYS), F32)] + _gather_scratch(n_g),
        compiler_params=_cparams(),
    )(qkvg, qkvg, qkvg, qkvg, qkvg, qkvg, qkvg, qkvg, bias, *gather)
    return outs[0], outs[1], outs[2], list(outs[3:])


def _attn_a_bwd(qkvg, bias, out_a, lse, dz, scatter):
    t = qkvg.shape[0]
    nq = t // QBLK
    n_sc = len(scatter)
    q_spec, k_specs, v_specs, g_spec, bias_spec = _a_specs(t)

    def body(q_ref, k0, k1, k2, v0, v1, v2, g_ref, w_ref, o_ref, lse_ref, dz_ref, *rest):
        sc_refs, rest = rest[:n_sc], rest[n_sc:]
        dqg_ref, dkv_ref, dg_ref = rest[:3]
        land_refs, rest = rest[3:3 + n_sc], rest[3 + n_sc:]
        dk_acc, dv_acc, gt_acc, b_ref, send_sems, recv_sems = rest
        j = pl.program_id(1)
        first = jnp.logical_and(pl.program_id(0) == 0, j == 0)
        last = jnp.logical_and(pl.program_id(0) == N_HEADS // 2 - 1, j == nq - 1)

        @pl.when(first)
        def _():
            for cp in _scatter_copies(sc_refs, land_refs, send_sems, recv_sems):
                cp.start()

        _fill_bias(2, lambda h: w_ref[0, h:h + 1, :], A_BAND, QBLK * (2 - j), 2, b_ref, j)

        @pl.when(j == 0)
        def _():
            dk_acc[...] = jnp.zeros_like(dk_acc)
            dv_acc[...] = jnp.zeros_like(dv_acc)
            gt_acc[...] = jnp.zeros_like(gt_acc)

        early = (j < 2).astype(jnp.int32)
        lane_lo = _lane_lo()
        q = q_ref[...]
        k = jnp.concatenate([k0[...], k1[...], k2[...]], axis=0)
        v = jnp.concatenate([v0[...], v1[...], v2[...]], axis=0)
        o = o_ref[...].astype(F32)
        lse_pair = lse_ref[...]
        dzf = dz_ref[...].astype(F32)
        silu, dsilu = _silu_parts(g_ref[...].astype(F32))
        do = dzf * silu
        dqg_ref[1] = (dzf * o * dsilu).astype(BF16)
        doo = do * o
        dqs = []
        dk_blk = jnp.zeros((A_KEYS, 128), F32)
        dv_blk = jnp.zeros((A_KEYS, 128), F32)
        for hh in range(2):
            sel = lane_lo if hh == 0 else jnp.logical_not(lane_lo)
            qm, s = _head_logits(q, k, b_ref, hh + 2 * early, sel)
            p = jnp.exp(_minus_rows(s, _own_everywhere(lse_pair, sel)))
            delta = jnp.sum(jnp.where(sel, doo, 0.0), axis=-1, keepdims=True)
            dom = jnp.where(sel, do, 0.0).astype(BF16)
            dp = _dot_nt(dom, v)
            ds = p * (dp - delta)
            gt_acc[hh] += ds
            dsb = ds.astype(BF16)
            dqs.append(_dot(dsb, k) * SCALE)
            dk_blk = dk_blk + _dot_tn(dsb, qm)
            dv_blk = dv_blk + _dot_tn(p.astype(BF16), dom)
        dqg_ref[0] = jnp.where(lane_lo, dqs[0], dqs[1]).astype(BF16)
        for b in range(3):
            @pl.when(j - 2 + b >= 0)
            def _(b=b):
                rows = pl.ds(pl.multiple_of((j - 2 + b) * QBLK, QBLK), QBLK)
                dk_acc[rows, :] += dk_blk[b * QBLK:(b + 1) * QBLK]
                dv_acc[rows, :] += dv_blk[b * QBLK:(b + 1) * QBLK]

        @pl.when(j == nq - 1)
        def _():
            dkv_ref[0] = dk_acc[...].astype(BF16)
            dkv_ref[1] = dv_acc[...].astype(BF16)
            dg_ref[0] = jnp.concatenate([_offset_sums(gt_acc[0]), _offset_sums(gt_acc[1]),
                                         jnp.zeros((6, A_DIAG), F32)], axis=0)

        @pl.when(last)
        def _():
            for cp in _scatter_copies(sc_refs, land_refs, send_sems, recv_sems):
                cp.wait()

    blk = pl.BlockSpec((QBLK, 128), lambda p, j: (j, p))
    outs = pl.pallas_call(
        body, name="attn_a_bwd", grid=(N_HEADS // 2, nq),
        in_specs=[q_spec, *k_specs, *v_specs, g_spec, bias_spec, blk, blk, blk] + [ANY] * n_sc,
        out_specs=[pl.BlockSpec((2, QBLK, 128), lambda p, j: (0, j, p)),
                   pl.BlockSpec((2, t, 128), lambda p, j: (0, 0, p)),
                   pl.BlockSpec((1, 8, A_DIAG), lambda p, j: (p, 0, 0))] + [ANY] * n_sc,
        out_shape=[SDS((2, t, D_MODEL), BF16), SDS((2, t, D_MODEL), BF16), SDS((N_HEADS // 2, 8, A_DIAG), F32)]
        + [SDS((N_DEV - 1, *g.shape[1:]), g.dtype) for g in scatter],
        scratch_shapes=[pltpu.VMEM((t, 128), F32), pltpu.VMEM((t, 128), F32),
                        pltpu.VMEM((2, QBLK, A_KEYS), F32), pltpu.VMEM((4, QBLK, A_KEYS), F32),
                        pltpu.SemaphoreType.DMA(((N_DEV - 1) * n_sc,)),
                        pltpu.SemaphoreType.DMA(((N_DEV - 1) * n_sc,))],
        compiler_params=_cparams(),
    )(qkvg, qkvg, qkvg, qkvg, qkvg, qkvg, qkvg, qkvg, bias, out_a, lse, dz, *scatter)
    return outs[0], outs[1], outs[2], list(outs[3:])


def _b_specs():
    q = pl.BlockSpec((QBLK, 512), lambda h, j: (j, h))
    g = pl.BlockSpec((QBLK, 512), lambda h, j: (j, 2 + h))
    kp = pl.BlockSpec((128, 128), lambda h, j: (jnp.maximum(2 * j - 1, 0), 0))
    kc = pl.BlockSpec((QBLK, 128), lambda h, j: (j, 0))
    vp = pl.BlockSpec((128, 128), lambda h, j: (jnp.maximum(2 * j - 1, 0), 1))
    vc = pl.BlockSpec((QBLK, 128), lambda h, j: (j, 1))
    bias = pl.BlockSpec((B_GROUP, B_KEYS), lambda h, j: (h, 0))
    sinks = pl.BlockSpec(memory_space=pltpu.SMEM)
    return q, g, kp, kc, vp, vc, bias, sinks


def _b_operands(kp, kc, vp, vc, kvh):
    k = jnp.concatenate([kp[...], kc[...]], axis=0)
    v = jnp.concatenate([vp[...], vc[...]], axis=0)
    kr = pltpu.roll(k, HEAD_DIM, 1)
    vr = pltpu.roll(v, HEAD_DIM, 1)
    first = kvh == 0
    return (jnp.where(first, k, kr), jnp.where(first, kr, k),
            jnp.where(first, v, vr), jnp.where(first, vr, v))


def _attn_b_fwd(qg, kv, bias, sinks):
    t = qg.shape[0]
    q_spec, g_spec, kp_spec, kc_spec, vp_spec, vc_spec, bias_spec, sink_spec = _b_specs()

    def body(q_ref, g_ref, kp, kc, vp, vc, w_ref, sink_ref, z_ref, o_ref, lse_ref, b_ref):
        kvh = pl.program_id(0)
        j = pl.program_id(1)
        _fill_bias(B_GROUP, lambda h: w_ref[h:h + 1, :], B_BAND, 128, 1, b_ref, j)
        early = (j < 1).astype(jnp.int32)
        lane_lo = _lane_lo()
        k_lo, k_hi, v_lo, v_hi = _b_operands(kp, kc, vp, vc, kvh)
        for pp in range(B_GROUP // 2):
            cols = slice(128 * pp, 128 * (pp + 1))
            qp = q_ref[:, cols]
            outs, lses = [], []
            for hh in range(2):
                g = 2 * pp + hh
                sel = lane_lo if hh == 0 else jnp.logical_not(lane_lo)
                sink = sink_ref[kvh * B_GROUP + g]
                vv = v_lo if hh == 0 else v_hi
                _, s = _head_logits(qp, k_lo if hh == 0 else k_hi, b_ref, g + B_GROUP * early, sel)
                mx = jnp.maximum(jnp.max(s, axis=-1, keepdims=True), sink)
                e = jnp.exp(s - mx).astype(BF16)
                r = _dot(e, jnp.where(sel, vv, jnp.ones_like(vv)))
                l = _row_sums_everywhere(r, sel) + jnp.exp(sink - mx)
                outs.append(r / l)
                lses.append(mx + jnp.log(l))
            o = jnp.where(lane_lo, outs[0], outs[1])
            silu, _ = _silu_parts(g_ref[:, cols].astype(F32))
            o_ref[:, cols] = o.astype(BF16)
            z_ref[:, cols] = (o * silu).astype(BF16)
            lse_ref[:, cols] = jnp.where(lane_lo, lses[0], lses[1])

    out_spec = pl.BlockSpec((QBLK, 512), lambda h, j: (j, h))
    return pl.pallas_call(
        body, name="attn_b_fwd", grid=(B_KV_HEADS, t // QBLK),
        in_specs=[q_spec, g_spec, kp_spec, kc_spec, vp_spec, vc_spec, bias_spec, sink_spec],
        out_specs=[out_spec, out_spec, out_spec],
        out_shape=[SDS((t, D_MODEL), BF16), SDS((t, D_MODEL), BF16), SDS((t, D_MODEL), F32)],
        scratch_shapes=[pltpu.VMEM((2 * B_GROUP, QBLK, B_KEYS), F32)],
        compiler_params=_cparams(),
    )(qg, qg, kv, kv, kv, kv, bias, sinks)


def _attn_b_bwd(qg, kv, bias, sinks, out_b, lse, dz, bucket_onehot):
    t = qg.shape[0]
    nq = t // QBLK
    q_spec, g_spec, kp_spec, kc_spec, vp_spec, vc_spec, bias_spec, sink_spec = _b_specs()

    def body(q_ref, g_ref, kp, kc, vp, vc, w_ref, sink_ref, o_ref, lse_ref, dz_ref, oh_ref,
             dqg_ref, dkv_ref, dt5_ref, dsink_ref, gt_acc, b_ref):
        kvh = pl.program_id(0)
        j = pl.program_id(1)
        _fill_bias(B_GROUP, lambda h: w_ref[h:h + 1, :], B_BAND, 128, 1, b_ref, j)

        @pl.when(jnp.logical_and(kvh == 0, j == 0))
        def _():
            dkv_ref[...] = jnp.zeros_like(dkv_ref)

        @pl.when(j == 0)
        def _():
            gt_acc[...] = jnp.zeros_like(gt_acc)
            dsink_ref[...] = jnp.zeros_like(dsink_ref)

        early = (j < 1).astype(jnp.int32)
        lane_lo = _lane_lo()
        k_lo, k_hi, v_lo, v_hi = _b_operands(kp, kc, vp, vc, kvh)
        dk_blk = jnp.zeros((B_KEYS, 128), F32)
        dv_blk = jnp.zeros((B_KEYS, 128), F32)
        for pp in range(B_GROUP // 2):
            cols = slice(128 * pp, 128 * (pp + 1))
            qp = q_ref[:, cols]
            o = o_ref[:, cols].astype(F32)
            lse_pair = lse_ref[:, cols]
            dzf = dz_ref[:, cols].astype(F32)
            silu, dsilu = _silu_parts(g_ref[:, cols].astype(F32))
            do = dzf * silu
            dqg_ref[1, :, cols] = (dzf * o * dsilu).astype(BF16)
            doo = do * o
            dqs = []
            for hh in range(2):
                g = 2 * pp + hh
                sel = lane_lo if hh == 0 else jnp.logical_not(lane_lo)
                sink = sink_ref[kvh * B_GROUP + g]
                kk = k_lo if hh == 0 else k_hi
                vv = v_lo if hh == 0 else v_hi
                qm, s = _head_logits(qp, kk, b_ref, g + B_GROUP * early, sel)
                lse_h = _own_everywhere(lse_pair, sel)
                p = jnp.exp(_minus_rows(s, lse_h))
                delta = jnp.sum(jnp.where(sel, doo, 0.0), axis=-1, keepdims=True)
                dom = jnp.where(sel, do, 0.0).astype(BF16)
                dp = _dot_nt(dom, vv)
                ds = p * (dp - delta)
                gt_acc[g] += ds
                dsink_ref[g:g + 1, :] -= jnp.sum(jnp.exp(sink - lse_h) * delta, axis=0, keepdims=True)
                dsb = ds.astype(BF16)
                dqs.append(_dot(dsb, kk) * SCALE)
                dk_blk = dk_blk + _dot_tn(dsb, qm)
                dv_blk = dv_blk + _dot_tn(p.astype(BF16), dom)
            dqg_ref[0, :, cols] = jnp.where(lane_lo, dqs[0], dqs[1]).astype(BF16)
        mine = lane_lo == (kvh == 0)
        dk_add = jnp.where(mine, dk_blk + pltpu.roll(dk_blk, HEAD_DIM, 1), 0.0)
        dv_add = jnp.where(mine, dv_blk + pltpu.roll(dv_blk, HEAD_DIM, 1), 0.0)

        @pl.when(j >= 1)
        def _():
            rows = pl.ds(pl.multiple_of((2 * j - 1) * 128, 128), 128)
            dkv_ref[0, rows, :] += dk_add[0:128]
            dkv_ref[1, rows, :] += dv_add[0:128]

        rows = pl.ds(pl.multiple_of(j * QBLK, QBLK), QBLK)
        dkv_ref[0, rows, :] += dk_add[128:B_KEYS]
        dkv_ref[1, rows, :] += dv_add[128:B_KEYS]

        @pl.when(j == nq - 1)
        def _():
            dd = jnp.concatenate([_offset_sums(gt_acc[g]) for g in range(B_GROUP)], axis=0)
            hi = dd.astype(BF16)
            lo = (dd - hi.astype(F32)).astype(BF16)
            dt5_ref[...] = _dot(hi, oh_ref[...]) + _dot(lo, oh_ref[...])

    blk = pl.BlockSpec((QBLK, 512), lambda h, j: (j, h))
    return pl.pallas_call(
        body, name="attn_b_bwd", grid=(B_KV_HEADS, nq),
        in_specs=[q_spec, g_spec, kp_spec, kc_spec, vp_spec, vc_spec, bias_spec, sink_spec, blk, blk, blk,
                  pl.BlockSpec((B_DIAG, 128), lambda h, j: (0, 0))],
        out_specs=[pl.BlockSpec((2, QBLK, 512), lambda h, j: (0, j, h)),
                   pl.BlockSpec((2, t, 128), lambda h, j: (0, 0, 0)),
                   pl.BlockSpec((B_GROUP, 128), lambda h, j: (h, 0)),
                   pl.BlockSpec((B_GROUP, 128), lambda h, j: (h, 0))],
        out_shape=[SDS((2, t, D_MODEL), BF16), SDS((2, t, 128), F32),
                   SDS((N_HEADS, 128), F32), SDS((N_HEADS, 128), F32)],
        scratch_shapes=[pltpu.VMEM((B_GROUP, QBLK, B_KEYS), F32), pltpu.VMEM((2 * B_GROUP, QBLK, B_KEYS), F32)],
        compiler_params=_cparams(),
    )(qg, qg, kv, kv, kv, kv, bias, sinks, out_b, lse, dz, bucket_onehot)


def _a_bias_by_offset(rel_bias):
    m = np.arange(A_DIAG)
    idx = np.clip(A_BAND - 1 - m, -A_REL_CLIP, A_REL_CLIP) + A_REL_CLIP
    by_head = rel_bias[idx].T.reshape(N_HEADS // 2, 2, A_DIAG)
    return jnp.concatenate([by_head, jnp.zeros((N_HEADS // 2, 6, A_DIAG), F32)], axis=1)


def _a_bias_grad(offset_sums):
    first = 319
    tail = jnp.sum(offset_sums[:, :first], axis=1)
    body = jnp.flip(offset_sums[:, first:first + 320], axis=1)
    body = body.at[:, -1].add(tail)
    full = jnp.concatenate([jnp.zeros((N_HEADS, 193), F32), body], axis=1)
    return full.T


def _t5_bucket(rel):
    nb = T5_BUCKETS // 2
    max_exact = nb // 2
    ret = jnp.where(rel > 0, nb, 0)
    n = jnp.abs(rel)
    nf = jnp.maximum(n, 1).astype(jnp.float32)
    large = max_exact + (jnp.log(nf / max_exact) / math.log(T5_MAX_DIST / max_exact)
                         * (nb - max_exact)).astype(jnp.int32)
    large = jnp.minimum(large, nb - 1)
    return ret + jnp.where(n < max_exact, n, large)


def _b_offset_buckets():
    return _t5_bucket(jnp.arange(B_DIAG, dtype=jnp.int32) - (B_LEFT_CHUNKS * CHUNK + CHUNK - 1))


def _b_bias_by_offset(t5_table):
    return t5_table[_b_offset_buckets()].T


def _b_bucket_onehot():
    return (_b_offset_buckets()[:, None] == jnp.arange(128)[None, :]).astype(BF16)


def _local_step(my_slot, x, target, a_gain, w_in_a, rel_bias, late_shards, kv_gain, t5_table,
                b_gain, sinks, f_gain):
    a_bias = _a_bias_by_offset(rel_bias)
    b_bias = _b_bias_by_offset(t5_table)
    sinks_flat = sinks.reshape(N_HEADS)

    xn, qkvg = _norm_matmul(x, a_gain, w_in_a)
    z_a, out_a, lse_a, (w_in_b, w_out_a, w_out_b, kv_w) = _attn_a_fwd(qkvg, a_bias, late_shards)
    w_out_a = w_out_a.reshape(D_MODEL, D_MODEL)
    w_out_b = w_out_b.reshape(D_MODEL, D_MODEL)
    kv_w = kv_w.reshape(D_MODEL, 2 * 128)
    h1, kvn, hb, kv, qg = _layer_a_out(x, z_a, w_out_a, kv_gain, b_gain, kv_w, w_in_b)
    z_b, out_b, lse_b = _attn_b_fwd(qg, kv, b_bias, sinks_flat)
    dh2, dh2b, dz_b, loss, d_fn = _layer_b_out_loss(h1, z_b, w_out_b, f_gain, target)

    dqg_b, dkv_b, d_t5, d_sink = _attn_b_bwd(qg, kv, b_bias, sinks_flat, out_b, lse_b, dz_b, _b_bucket_onehot())
    dh1, dh1b, dz_a, d_bn, d_kn = _layer_b_in_bwd(dqg_b, dkv_b, w_in_b, kv_w, h1, dh2, b_gain, kv_gain, w_out_a)
    early = dict(
        b_w_out=_weight_grad_rows("grad_b_w_out", my_slot, z_b, dh2b[None]),
        b_w_in=_weight_grad_cols("grad_b_w_in", my_slot, hb, [dqg_b],
                                 [(0, o, c, 4 * o + c) for o in range(2) for c in range(4)], 256),
        kv_w=_weight_grad_rows("grad_kv_w", my_slot, kvn, dkv_b),
        a_w_out=_weight_grad_rows("grad_a_w_out", my_slot, z_a, dh1b[None]))
    dqg_a, dkv_a, d_rel, landed = _attn_a_bwd(qkvg, a_bias, out_a, lse_a, dz_a, [g[0] for g in early.values()])
    g_w_in_a = _weight_grad_cols(
        "grad_a_w_in", my_slot, xn, [dqg_a, dkv_a],
        [(0, 0, 0, 0), (0, 0, 1, 1), (1, 0, 0, 2), (1, 0, 1, 3), (1, 1, 0, 4), (1, 1, 1, 5), (0, 1, 0, 6), (0, 1, 1, 7)], 512)
    from_sibling, = _exchange_sibling([g_w_in_a[0]])
    x_i, y_i, c_i, chips = _place()
    del x_i, y_i
    forward_slots = jnp.stack([_slot(*chip, c_i) for chip in chips]).astype(jnp.int32)
    chip_sums = _pre_reduce("chip_sum_a_w_in", g_w_in_a[0], from_sibling, forward_slots)
    grad_x, d_an, from_chips = _layer_a_in_bwd(dqg_a, dkv_a, w_in_a, x, dh1, a_gain, chip_sums)

    matrices = {n: (g[1], [(land, 0, N_DEV - 1)]) for (n, g), land in zip(early.items(), landed)}
    matrices["a_w_in"] = (g_w_in_a[1], [(from_sibling, 3, 1), (from_chips, 0, 3)])
    small = dict(
        loss=loss[0, 0], a_norm=d_an, a_rel_bias=_a_bias_grad(d_rel[:, :2].reshape(N_HEADS, A_DIAG)),
        kv_norm=d_kn, t5_bias=d_t5[:, :T5_BUCKETS].T, b_norm=d_bn,
        b_sinks=d_sink[:, 0].reshape(1, N_HEADS), final_norm=d_fn)
    return grad_x, small, matrices


def _place():
    x, y, c = lax.axis_index("x"), lax.axis_index("y"), lax.axis_index("c")
    chips = [(1 - x, y), (x, 1 - y), (1 - x, 1 - y)]
    return x, y, c, chips


def _slot(px, py, pc):
    return 4 * px + 2 * py + pc


ANY = pl.BlockSpec(memory_space=pl.ANY)


def _peer(x, y, c, k):
    return (x ^ (k >> 2), y ^ ((k >> 1) & 1), c ^ (k & 1))


def _scatter_copies(grad_refs, land_refs, send_sems, recv_sems):
    x, y, c, _ = _place()
    copies = []
    for t, (grad, land) in enumerate(zip(grad_refs, land_refs)):
        for k in range(1, N_DEV):
            peer = _peer(x, y, c, k)
            sem = (N_DEV - 1) * t + k - 1
            copies.append(pltpu.make_async_remote_copy(
                src_ref=grad.at[_slot(*peer)], dst_ref=land.at[k - 1],
                send_sem=send_sems.at[sem], recv_sem=recv_sems.at[sem],
                device_id=peer, device_id_type=MESH))
    return copies


def _gather_phases(ins, outs, send_sems, recv_sems, local_sems):
    n = len(ins)
    x, y, c, chips = _place()
    me, sibling = (x, y, c), (x, y, 1 - c)

    def copy(t, k, block, to, src=None):
        dst = outs[t].at[_slot(*block)]
        return pltpu.make_async_remote_copy(
            src_ref=dst if src is None else src, dst_ref=dst,
            send_sem=send_sems.at[7 * t + k], recv_sem=recv_sems.at[7 * t + k],
            device_id=to, device_id_type=MESH)

    def lists():
        mine = [pltpu.make_async_copy(ins[t], outs[t].at[_slot(*me)], local_sems.at[t]) for t in range(n)]
        first = []
        for t in range(n):
            first.append(copy(t, 0, me, sibling, src=ins[t]))
            first += [copy(t, 1 + j, me, (*chip, c), src=ins[t]) for j, chip in enumerate(chips)]
        passed = [copy(t, 4 + j, (*chip, c), sibling) for t in range(n) for j, chip in enumerate(chips)]
        return mine, first, passed

    def start():
        mine, first, _ = lists()
        for cp in mine + first:
            cp.start()

    def forward():
        _, _, passed = lists()
        for t in range(n):
            for j, chip in enumerate(chips):
                copy(t, 1 + j, (*chip, c), me).wait_recv()
                passed[3 * t + j].start()

    def finish():
        mine, first, passed = lists()
        for t in range(n):
            copy(t, 0, sibling, me).wait_recv()
            for j, chip in enumerate(chips):
                copy(t, 4 + j, (*chip, 1 - c), me).wait_recv()
        for cp in first + passed:
            cp.wait_send()
        for cp in mine:
            cp.wait()

    return start, forward, finish


def _gather_scratch(n):
    return [pltpu.SemaphoreType.DMA((7 * n,)), pltpu.SemaphoreType.DMA((7 * n,)), pltpu.SemaphoreType.DMA((n,))]


def _all_gather(shards):
    n = len(shards)

    def body(*refs):
        start, forward, finish = _gather_phases(refs[:n], refs[n:2 * n], *refs[2 * n:])
        start()
        forward()
        finish()

    return pl.pallas_call(
        body, name="all_gather_weights",
        in_specs=[ANY] * n, out_specs=[ANY] * n,
        out_shape=[SDS((N_DEV, *s.shape), s.dtype) for s in shards],
        scratch_shapes=_gather_scratch(n),
    )(*shards)


def _exchange_sibling(grads):
    n = len(grads)

    def body(*refs):
        ins, outs = refs[:n], refs[n:2 * n]
        send_sems, recv_sems = refs[2 * n:]
        x, y, c, chips = _place()
        sibling = (x, y, 1 - c)
        copies = []
        for t in range(n):
            blocks = [(*chip, 1 - c) for chip in chips] + [sibling]
            for k, block in enumerate(blocks):
                copies.append(pltpu.make_async_remote_copy(
                    src_ref=ins[t].at[_slot(*block)], dst_ref=outs[t].at[k],
                    send_sem=send_sems.at[4 * t + k], recv_sem=recv_sems.at[4 * t + k],
                    device_id=sibling, device_id_type=MESH))
        for cp in copies:
            cp.start()
        for cp in copies:
            cp.wait()

    return pl.pallas_call(
        body, name="grads_to_sibling",
        in_specs=[ANY] * n, out_specs=[ANY] * n,
        out_shape=[SDS((4, *g.shape[1:]), g.dtype) for g in grads],
        scratch_shapes=[pltpu.SemaphoreType.DMA((4 * n,)), pltpu.SemaphoreType.DMA((4 * n,))],
    )(*grads)


def _chip_copies(sums_ref, land_ref, send_sems, recv_sems):
    x, y, c, chips = _place()
    del x, y
    return [pltpu.make_async_remote_copy(
        src_ref=sums_ref.at[j], dst_ref=land_ref.at[j], send_sem=send_sems.at[j], recv_sem=recv_sems.at[j],
        device_id=(*chip, c), device_id_type=MESH) for j, chip in enumerate(chips)]


def _row_tile(rows):
    return min(rows, 256)


def _pre_reduce(name, g, from_sibling, slots):
    _, r, c = g.shape
    tr = _row_tile(r)

    def body(slots_ref, g_ref, s_ref, o_ref):
        del slots_ref
        o_ref[...] = (g_ref[...].astype(F32) + s_ref[...].astype(F32)).astype(BF16)

    return pl.pallas_call(
        body, name=name,
        grid_spec=pltpu.PrefetchScalarGridSpec(
            num_scalar_prefetch=1, grid=(3, r // tr),
            in_specs=[pl.BlockSpec((1, tr, c), lambda j, i, s: (s[j], i, 0)),
                      pl.BlockSpec((1, tr, c), lambda j, i, s: (j, i, 0))],
            out_specs=pl.BlockSpec((1, tr, c), lambda j, i, s: (j, i, 0))),
        out_shape=SDS((3, r, c), BF16),
        compiler_params=_cparams(),
    )(slots, g, from_sibling)


def _adamw(w, g, m, v):
    m2 = ADAM_B1 * m + (1.0 - ADAM_B1) * g
    v2 = ADAM_B2 * v + (1.0 - ADAM_B2) * jnp.square(g)
    m_hat = m2 / (1.0 - ADAM_B1 ** ADAM_STEP)
    v_hat = v2 / (1.0 - ADAM_B2 ** ADAM_STEP)
    delta = -ADAM_LR * (m_hat / (jnp.sqrt(v_hat) + ADAM_EPS) + ADAM_WD * w)
    return delta, m2, v2


def _reduce_adamw(name, own, partials, w, m, v):
    r, c = own.shape
    tr = _row_tile(r)
    n_p = len(partials)

    def body(own_ref, *rest):
        p_refs, (w_ref, m_ref, v_ref, grad_ref, d_ref, nm_ref, nv_ref) = rest[:n_p], rest[n_p:]
        grad = own_ref[...]
        for p_ref, (_, _, count) in zip(p_refs, partials):
            for j in range(count):
                grad = grad + p_ref[j].astype(F32)
        grad_ref[...] = grad
        d_ref[...], nm_ref[...], nv_ref[...] = _adamw(w_ref[...], grad, m_ref[...], v_ref[...])

    flat = pl.BlockSpec((tr, c), lambda i: (i, 0))
    return pl.pallas_call(
        body, name=name, grid=(r // tr,),
        in_specs=[flat] + [pl.BlockSpec((count, tr, c), lambda i, first=first, count=count: (first // count, i, 0))
                           for _, first, count in partials] + [flat, flat, flat],
        out_specs=[flat, flat, flat, flat],
        out_shape=[SDS((r, c), F32)] * 4,
        compiler_params=_cparams(),
    )(own, *[p[0] for p in partials], w, m, v)


_SMALL = (("a_norm", 8), ("a_rel_bias", 72), ("kv_norm", 8), ("t5_bias", 8), ("b_norm", 8),
          ("b_sinks", 8), ("final_norm", 8), ("loss", 8))
_SMALL_ROWS = sum(r for _, r in _SMALL)


def _pack_small(parts):
    rows = []
    for name, n_rows in _SMALL:
        flat = parts[name].reshape(-1).astype(F32)
        rows.append(jnp.pad(flat, (0, n_rows * 128 - flat.shape[0])).reshape(n_rows, 128))
    return jnp.concatenate(rows, axis=0)


def _unpack_small(buf, shapes):
    out, at = {}, 0
    for name, n_rows in _SMALL:
        size = int(np.prod(shapes[name])) if shapes[name] else 1
        out[name] = buf[at:at + n_rows].reshape(-1)[:size].reshape(shapes[name])
        at += n_rows
    return out


def _small_allreduce_adamw(gbuf, wbuf, mbuf, vbuf):
    def body(g_ref, w_ref, m_ref, v_ref, sum_ref, d_ref, nm_ref, nv_ref, land_ref, send_sems, recv_sems):
        x, y, c, _ = _place()
        my_slot = _slot(x, y, c)
        land_ref[my_slot] = g_ref[...]
        copies = []
        for k in range(1, N_DEV):
            peer = (x ^ (k >> 2), y ^ ((k >> 1) & 1), c ^ (k & 1))
            copies.append(pltpu.make_async_remote_copy(
                src_ref=g_ref, dst_ref=land_ref.at[my_slot],
                send_sem=send_sems.at[k - 1], recv_sem=recv_sems.at[k - 1],
                device_id=peer, device_id_type=MESH))
        for cp in copies:
            cp.start()
        for k in range(1, N_DEV):
            peer_slot = _slot(x ^ (k >> 2), y ^ ((k >> 1) & 1), c ^ (k & 1))
            pltpu.make_async_remote_copy(
                src_ref=g_ref, dst_ref=land_ref.at[peer_slot],
                send_sem=send_sems.at[k - 1], recv_sem=recv_sems.at[k - 1],
                device_id=(x, y, c), device_id_type=MESH).wait_recv()
        for cp in copies:
            cp.wait_send()
        total = land_ref[0]
        for s in range(1, N_DEV):
            total = total + land_ref[s]
        sum_ref[...] = total
        d_ref[...], nm_ref[...], nv_ref[...] = _adamw(w_ref[...], total, m_ref[...], v_ref[...])

    vm = pl.BlockSpec(memory_space=pltpu.VMEM)
    shape = SDS((_SMALL_ROWS, 128), F32)
    return pl.pallas_call(
        body, name="small_allreduce_adamw",
        in_specs=[vm] * 4, out_specs=[vm] * 4, out_shape=[shape] * 4,
        scratch_shapes=[pltpu.VMEM((N_DEV, _SMALL_ROWS, 128), F32),
                        pltpu.SemaphoreType.DMA((N_DEV - 1,)), pltpu.SemaphoreType.DMA((N_DEV - 1,))],
    )(gbuf, wbuf, mbuf, vbuf)


def kernel(x, a_norm, a_w_in, a_rel_bias, a_w_out, kv_norm, kv_w, t5_bias, b_norm, b_w_in, b_sinks, b_w_out, final_norm, loss_target, m_a_norm, m_a_w_in, m_a_rel_bias, m_a_w_out, m_kv_norm, m_kv_w, m_t5_bias, m_b_norm, m_b_w_in, m_b_sinks, m_b_w_out, m_final_norm, v_a_norm, v_a_w_in, v_a_rel_bias, v_a_w_out, v_kv_norm, v_kv_w, v_t5_bias, v_b_norm, v_b_w_in, v_b_sinks, v_b_w_out, v_final_norm):
    xi, yi, ci = lax.axis_index("x"), lax.axis_index("y"), lax.axis_index("c")
    my_slot = _slot(xi, yi, ci)

    w_in_a, a_gain = _all_gather([a_w_in[0].astype(BF16), a_norm])
    a_gain = a_gain.reshape(1, D_MODEL)

    slot_arr = jnp.reshape(my_slot, (1,)).astype(jnp.int32)
    late_shards = [b_w_in[0].astype(BF16), a_w_out[0].astype(BF16), b_w_out[0].astype(BF16), kv_w.astype(BF16)]
    grad_x, loc, matrices = _local_step(
        slot_arr, x[0], loss_target[0], a_gain, w_in_a, a_rel_bias[0], late_shards,
        kv_norm.reshape(1, D_MODEL), t5_bias, b_norm, b_sinks, final_norm.reshape(1, D_MODEL))

    shard_w = dict(a_w_in=a_w_in[0], b_w_in=b_w_in[0], a_w_out=a_w_out[0], b_w_out=b_w_out[0], kv_w=kv_w)
    shard_m = dict(a_w_in=m_a_w_in[0], b_w_in=m_b_w_in[0], a_w_out=m_a_w_out[0], b_w_out=m_b_w_out[0], kv_w=m_kv_w)
    shard_v = dict(a_w_in=v_a_w_in[0], b_w_in=v_b_w_in[0], a_w_out=v_a_w_out[0], b_w_out=v_b_w_out[0], kv_w=v_kv_w)
    big = {n: _reduce_adamw("adamw_" + n, own, partials, shard_w[n], shard_m[n], shard_v[n])
           for n, (own, partials) in matrices.items()}

    def own_row(vec):
        return lax.dynamic_update_slice(jnp.zeros((N_DEV, 128), F32), vec, (my_slot, 0))

    zero = jnp.zeros((), F32)
    small_w = dict(a_norm=own_row(a_norm), a_rel_bias=a_rel_bias, kv_norm=kv_norm, t5_bias=t5_bias,
                   b_norm=b_norm, b_sinks=b_sinks, final_norm=final_norm, loss=zero)
    small_m = dict(a_norm=own_row(m_a_norm), a_rel_bias=m_a_rel_bias, kv_norm=m_kv_norm, t5_bias=m_t5_bias,
                   b_norm=m_b_norm, b_sinks=m_b_sinks, final_norm=m_final_norm, loss=zero)
    small_v = dict(a_norm=own_row(v_a_norm), a_rel_bias=v_a_rel_bias, kv_norm=v_kv_norm, t5_bias=v_t5_bias,
                   b_norm=v_b_norm, b_sinks=v_b_sinks, final_norm=v_final_norm, loss=zero)
    small = _small_allreduce_adamw(_pack_small(loc), _pack_small(small_w), _pack_small(small_m), _pack_small(small_v))
    shapes = dict(a_norm=(N_DEV, 128), a_rel_bias=a_rel_bias.shape, kv_norm=kv_norm.shape, t5_bias=t5_bias.shape,
                  b_norm=b_norm.shape, b_sinks=b_sinks.shape, final_norm=final_norm.shape, loss=())
    sm = [_unpack_small(buf, shapes) for buf in small]
    for part in sm:
        part["a_norm"] = lax.dynamic_slice(part["a_norm"], (my_slot, 0), (1, 128))

    order = ("a_norm", "a_w_in", "a_rel_bias", "a_w_out", "kv_norm", "kv_w", "t5_bias", "b_norm",
             "b_w_in", "b_sinks", "b_w_out", "final_norm")
    lead = dict(a_w_in=True, b_w_in=True, a_w_out=True, b_w_out=True, kv_w=False)

    def pick(kind, name):
        if name in big:
            val = big[name][kind]
            return val[None] if lead[name] else val
        return sm[kind][name]

    outs = [sm[0]["loss"], grad_x[None]]
    for kind in range(4):
        outs += [pick(kind, n) for n in order]
    return tuple(outs)
```

```python
import functools
import math

import numpy as np
import jax
import jax.numpy as jnp
from jax import lax
from jax.experimental import pallas as pl
from jax.experimental.pallas import tpu as pltpu

F32 = jnp.float32
BF16 = jnp.bfloat16
SDS = jax.ShapeDtypeStruct

D_MODEL = 1024
HEAD_DIM = 64
CHUNK = 64
N_HEADS = 16
RMS_EPS = 1e-6
A_LEFT_CHUNKS = 8
A_BAND = (A_LEFT_CHUNKS + 1) * CHUNK
A_REL_CLIP = 256
B_KV_HEADS = 2
B_GROUP = 8
B_LEFT_CHUNKS = 2
B_BAND = (B_LEFT_CHUNKS + 1) * CHUNK
T5_BUCKETS = 32
T5_MAX_DIST = 128
QBLK = 256
A_KEYS = 3 * QBLK
B_KEYS = QBLK + 128
A_DIAG = A_KEYS
B_DIAG = B_KEYS
NEG = -1e30
SCALE = HEAD_DIM ** -0.5
N_DEV = 8

ADAM_LR = 0.001
ADAM_B1 = 0.9
ADAM_B2 = 0.999
ADAM_EPS = 1e-08
ADAM_WD = 0.01
ADAM_STEP = 10

VMEM_LIMIT_BYTES = 56 * 1024 * 1024
MESH = pl.DeviceIdType.MESH


def _cparams():
    return pltpu.CompilerParams(vmem_limit_bytes=VMEM_LIMIT_BYTES)


def _dot(a, b):
    return jnp.dot(a, b, preferred_element_type=F32)


def _dot_nt(a, b):
    return lax.dot_general(a, b, (((1,), (1,)), ((), ())), preferred_element_type=F32)


def _dot_tn(a, b):
    return lax.dot_general(a, b, (((0,), (0,)), ((), ())), preferred_element_type=F32)


def _rstd(xf):
    return lax.rsqrt(jnp.mean(xf * xf, axis=-1, keepdims=True) + RMS_EPS)


def _sigmoid(x):
    return 1.0 / (1.0 + jnp.exp(-x))


def _norm_matmul(x, gain, w):
    t = x.shape[0]
    nb, _, tn = w.shape
    tm = min(t, 1024)

    def body(x_ref, g_ref, w_ref, xn_ref, o_ref):
        @pl.when(pl.program_id(1) == 0)
        def _():
            xf = x_ref[...]
            xn_ref[...] = ((xf * _rstd(xf)) * g_ref[...]).astype(BF16)

        o_ref[...] = _dot(xn_ref[...], w_ref[0]).astype(BF16)

    return pl.pallas_call(
        body, name="norm_matmul", grid=(t // tm, nb),
        in_specs=[pl.BlockSpec((tm, D_MODEL), lambda m, n: (m, 0)),
                  pl.BlockSpec((1, D_MODEL), lambda m, n: (0, 0)),
                  pl.BlockSpec((1, D_MODEL, tn), lambda m, n: (n, 0, 0))],
        out_specs=[pl.BlockSpec((tm, D_MODEL), lambda m, n: (m, 0)),
                   pl.BlockSpec((tm, tn), lambda m, n: (m, n))],
        out_shape=[SDS((t, D_MODEL), BF16), SDS((t, nb * tn), BF16)],
        compiler_params=_cparams(),
    )(x, gain, w)


def _layer_a_out(x, z, w_out, kv_gain, b_gain, kv_w, w_in_b):
    t = x.shape[0]
    tm = min(t, 512)
    nb, _, tn = w_in_b.shape

    def body(x_ref, z_ref, wo_ref, kvg_ref, bg_ref, kvw_ref, wb_ref,
             h1_ref, kvn_ref, hb_ref, kv_ref, qg_ref):
        h1 = x_ref[...] + _dot(z_ref[...], wo_ref[...])
        h1_ref[...] = h1
        y0 = h1 * _rstd(h1)
        kvn = (y0 * kvg_ref[...]).astype(BF16)
        hb = (y0 * bg_ref[...]).astype(BF16)
        kvn_ref[...] = kvn
        hb_ref[...] = hb
        kv_ref[...] = _dot(kvn, kvw_ref[...]).astype(BF16)
        for i in range(nb):
            qg_ref[:, i * tn:(i + 1) * tn] = _dot(hb, wb_ref[i]).astype(BF16)

    row = lambda m: (m, 0)
    fix2 = lambda m: (0, 0)
    return pl.pallas_call(
        body, name="layer_a_out", grid=(t // tm,),
        in_specs=[pl.BlockSpec((tm, D_MODEL), row), pl.BlockSpec((tm, D_MODEL), row),
                  pl.BlockSpec((D_MODEL, D_MODEL), fix2),
                  pl.BlockSpec((1, D_MODEL), fix2), pl.BlockSpec((1, D_MODEL), fix2),
                  pl.BlockSpec((D_MODEL, 256), fix2),
                  pl.BlockSpec((nb, D_MODEL, tn), lambda m: (0, 0, 0))],
        out_specs=[pl.BlockSpec((tm, D_MODEL), row), pl.BlockSpec((tm, D_MODEL), row),
                   pl.BlockSpec((tm, D_MODEL), row), pl.BlockSpec((tm, 256), row),
                   pl.BlockSpec((tm, nb * tn), row)],
        out_shape=[SDS((t, D_MODEL), F32), SDS((t, D_MODEL), BF16), SDS((t, D_MODEL), BF16),
                   SDS((t, 256), BF16), SDS((t, nb * tn), BF16)],
        compiler_params=_cparams(),
    )(x, z, w_out, kv_gain, b_gain, kv_w, w_in_b)


def _layer_b_out_loss(h1, z, w_out, f_gain, target):
    t = h1.shape[0]
    tm = min(t, 512)

    def body(h1_ref, z_ref, wo_ref, fg_ref, tgt_ref,
             dh2_ref, dh2b_ref, dz_ref, loss_ref, dfn_ref):
        @pl.when(pl.program_id(0) == 0)
        def _():
            loss_ref[...] = jnp.zeros_like(loss_ref)
            dfn_ref[...] = jnp.zeros_like(dfn_ref)

        h2 = h1_ref[...] + _dot(z_ref[...], wo_ref[...])
        r = _rstd(h2)
        yn = h2 * r
        fg = fg_ref[...]
        err = yn * fg - tgt_ref[...]
        loss_ref[...] += (0.5 / D_MODEL) * jnp.sum(err * err)
        dy = err * (1.0 / D_MODEL)
        dfn_ref[...] += jnp.sum(dy * yn, axis=0, keepdims=True)
        u = dy * fg
        dh2 = r * u - h2 * ((r * r * r) * jnp.mean(u * h2, axis=-1, keepdims=True))
        dh2_ref[...] = dh2
        dh2b = dh2.astype(BF16)
        dh2b_ref[...] = dh2b
        dz_ref[...] = _dot_nt(dh2b, wo_ref[...]).astype(BF16)

    row = lambda m: (m, 0)
    fix2 = lambda m: (0, 0)
    return pl.pallas_call(
        body, name="layer_b_out_loss", grid=(t // tm,),
        in_specs=[pl.BlockSpec((tm, D_MODEL), row), pl.BlockSpec((tm, D_MODEL), row),
                  pl.BlockSpec((D_MODEL, D_MODEL), fix2), pl.BlockSpec((1, D_MODEL), fix2),
                  pl.BlockSpec((tm, D_MODEL), row)],
        out_specs=[pl.BlockSpec((tm, D_MODEL), row), pl.BlockSpec((tm, D_MODEL), row),
                   pl.BlockSpec((tm, D_MODEL), row), pl.BlockSpec((1, 128), fix2),
                   pl.BlockSpec((1, D_MODEL), fix2)],
        out_shape=[SDS((t, D_MODEL), F32), SDS((t, D_MODEL), BF16), SDS((t, D_MODEL), BF16),
                   SDS((1, 128), F32), SDS((1, D_MODEL), F32)],
        compiler_params=_cparams(),
    )(h1, z, w_out, f_gain, target)


def _layer_b_in_bwd(dqg, dkv, w_in_b, kv_w, h1, dh2, b_gain, kv_gain, w_out_a):
    t = h1.shape[0]
    tm = min(t, 256)
    nb, _, tn = w_in_b.shape
    per = D_MODEL // tn

    def body(dqg_ref, dkv_ref, wb_ref, kvw_ref, h1_ref, dh2_ref, bg_ref, kvg_ref, wo_ref,
             dh1_ref, dh1b_ref, dz_ref, dbn_ref, dkn_ref):
        @pl.when(pl.program_id(0) == 0)
        def _():
            dbn_ref[...] = jnp.zeros_like(dbn_ref)
            dkn_ref[...] = jnp.zeros_like(dkn_ref)

        dhb = jnp.zeros((tm, D_MODEL), F32)
        for i in range(nb):
            blk = dqg_ref[i // per, :, (i % per) * tn:(i % per + 1) * tn]
            dhb = dhb + _dot_nt(blk, wb_ref[i])
        dkn = (_dot_nt(dkv_ref[0].astype(BF16), kvw_ref[:, 0:128])
               + _dot_nt(dkv_ref[1].astype(BF16), kvw_ref[:, 128:256]))
        h1 = h1_ref[...]
        r = _rstd(h1)
        xr = h1 * r
        dbn_ref[...] += jnp.sum(dhb * xr, axis=0, keepdims=True)
        dkn_ref[...] += jnp.sum(dkn * xr, axis=0, keepdims=True)
        u = dhb * bg_ref[...] + dkn * kvg_ref[...]
        dh1 = dh2_ref[...] + r * u - h1 * ((r * r * r) * jnp.mean(u * h1, axis=-1, keepdims=True))
        dh1_ref[...] = dh1
        dh1b = dh1.astype(BF16)
        dh1b_ref[...] = dh1b
        dz_ref[...] = _dot_nt(dh1b, wo_ref[...]).astype(BF16)

    row = lambda m: (m, 0)
    fix2 = lambda m: (0, 0)
    return pl.pallas_call(
        body, name="layer_b_in_bwd", grid=(t // tm,),
        in_specs=[pl.BlockSpec((2, tm, D_MODEL), lambda m: (0, m, 0)),
                  pl.BlockSpec((2, tm, 128), lambda m: (0, m, 0)),
                  pl.BlockSpec((nb, D_MODEL, tn), lambda m: (0, 0, 0)),
                  pl.BlockSpec((D_MODEL, 256), fix2),
                  pl.BlockSpec((tm, D_MODEL), row), pl.BlockSpec((tm, D_MODEL), row),
                  pl.BlockSpec((1, D_MODEL), fix2), pl.BlockSpec((1, D_MODEL), fix2),
                  pl.BlockSpec((D_MODEL, D_MODEL), fix2)],
        out_specs=[pl.BlockSpec((tm, D_MODEL), row), pl.BlockSpec((tm, D_MODEL), row),
                   pl.BlockSpec((tm, D_MODEL), row), pl.BlockSpec((1, D_MODEL), fix2),
                   pl.BlockSpec((1, D_MODEL), fix2)],
        out_shape=[SDS((t, D_MODEL), F32), SDS((t, D_MODEL), BF16), SDS((t, D_MODEL), BF16),
                   SDS((1, D_MODEL), F32), SDS((1, D_MODEL), F32)],
        compiler_params=_cparams(),
    )(dqg, dkv, w_in_b, kv_w, h1, dh2, b_gain, kv_gain, w_out_a)


def _layer_a_in_bwd(dqg, dkv, w_in_a, x, dh1, a_gain, chip_sums):
    t = x.shape[0]
    tm = min(t, 256)
    nb, _, tn = w_in_a.shape
    per = D_MODEL // tn

    def body(dqg_ref, dkv_ref, w_ref, x_ref, dh1_ref, ag_ref, sums_ref, dx_ref, dan_ref, land_ref,
             send_sems, recv_sems):
        @pl.when(pl.program_id(0) == 0)
        def _():
            dan_ref[...] = jnp.zeros_like(dan_ref)
            for cp in _chip_copies(sums_ref, land_ref, send_sems, recv_sems):
                cp.start()

        dxn = jnp.zeros((tm, D_MODEL), F32)
        for i in range(nb):
            part = i // per
            src = dqg_ref if part in (0, 3) else dkv_ref
            outer = {0: 0, 3: 1, 1: 0, 2: 1}[part]
            blk = src[outer, :, (i % per) * tn:(i % per + 1) * tn]
            dxn = dxn + _dot_nt(blk, w_ref[i])
        xf = x_ref[...]
        r = _rstd(xf)
        dan_ref[...] += jnp.sum(dxn * (xf * r), axis=0, keepdims=True)
        u = dxn * ag_ref[...]
        dx_ref[...] = dh1_ref[...] + r * u - xf * ((r * r * r) * jnp.mean(u * xf, axis=-1, keepdims=True))

        @pl.when(pl.program_id(0) == t // tm - 1)
        def _():
            for cp in _chip_copies(sums_ref, land_ref, send_sems, recv_sems):
                cp.wait()

    row = lambda m: (m, 0)
    fix2 = lambda m: (0, 0)
    return pl.pallas_call(
        body, name="layer_a_in_bwd", grid=(t // tm,),
        in_specs=[pl.BlockSpec((2, tm, D_MODEL), lambda m: (0, m, 0)),
                  pl.BlockSpec((2, tm, D_MODEL), lambda m: (0, m, 0)),
                  pl.BlockSpec((nb, D_MODEL, tn), lambda m: (0, 0, 0)),
                  pl.BlockSpec((tm, D_MODEL), row), pl.BlockSpec((tm, D_MODEL), row),
                  pl.BlockSpec((1, D_MODEL), fix2), ANY],
        out_specs=[pl.BlockSpec((tm, D_MODEL), row), pl.BlockSpec((1, D_MODEL), fix2), ANY],
        out_shape=[SDS((t, D_MODEL), F32), SDS((1, D_MODEL), F32), SDS(chip_sums.shape, chip_sums.dtype)],
        scratch_shapes=[pltpu.SemaphoreType.DMA((3,)), pltpu.SemaphoreType.DMA((3,))],
        compiler_params=_cparams(),
    )(dqg, dkv, w_in_a, x, dh1, a_gain, chip_sums)


def _lut(s, vals):
    r = jnp.int32(vals[0])
    for i in range(1, len(vals)):
        r = jnp.where(s == i, jnp.int32(vals[i]), r)
    return r


def _held(steps, i):
    seq, cur = [None] * len(steps), None
    for k in range(len(steps) - 1, -1, -1):
        if steps[k][0] == i:
            cur = steps[k][1:3]
        seq[k] = cur
    for k in range(len(steps)):
        cur = seq[k] = seq[k] if seq[k] is not None else cur
    return seq


def _weight_grad_cols(name, my_slot, a, bs, steps, tn):
    t, dw = a.shape
    n_arr = len(bs)
    which = [s[0] for s in steps]
    blks = [s[3] for s in steps]

    def body(slot_ref, a_ref, *rest):
        b_refs, (o_ref, own_ref, at_ref) = rest[:n_arr], rest[n_arr:]
        s = pl.program_id(0)

        @pl.when(s == 0)
        def _():
            at_ref[...] = a_ref[...].T

        for i in range(n_arr):
            @pl.when(_lut(s, which) == i)
            def _(i=i):
                res = _dot(at_ref[...], b_refs[i][0])
                o_ref[0] = res.astype(BF16)

                @pl.when(_lut(s, blks) == slot_ref[0])
                def _():
                    own_ref[...] = res

    def b_spec(i):
        held = _held(steps, i)
        return pl.BlockSpec((1, t, tn), lambda s, slot: (_lut(s, [h[0] for h in held]), 0,
                                                         _lut(s, [h[1] for h in held])))

    return pl.pallas_call(
        body, name=name,
        grid_spec=pltpu.PrefetchScalarGridSpec(
            num_scalar_prefetch=1, grid=(len(steps),),
            in_specs=[pl.BlockSpec((t, dw), lambda s, slot: (0, 0))] + [b_spec(i) for i in range(n_arr)],
            out_specs=[pl.BlockSpec((1, dw, tn), lambda s, slot: (_lut(s, blks), 0, 0)),
                       pl.BlockSpec((dw, tn), lambda s, slot: (0, 0))],
            scratch_shapes=[pltpu.VMEM((dw, t), BF16)]),
        out_shape=[SDS((N_DEV, dw, tn), BF16), SDS((dw, tn), F32)],
        compiler_params=_cparams(),
    )(my_slot, a, *bs)


def _weight_grad_rows(name, my_slot, a, b):
    t, dw = a.shape
    n_o, _, c = b.shape
    rows = dw // N_DEV

    def body(slot_ref, a_ref, b_ref, o_ref, own_ref):
        at = a_ref[...].T
        res = [_dot(at, b_ref[o].astype(BF16)) for o in range(n_o)]
        for o in range(n_o):
            o_ref[0, :, o * c:(o + 1) * c] = res[o].astype(BF16)

        @pl.when(pl.program_id(0) == slot_ref[0])
        def _():
            for o in range(n_o):
                own_ref[:, o * c:(o + 1) * c] = res[o]

    return pl.pallas_call(
        body, name=name,
        grid_spec=pltpu.PrefetchScalarGridSpec(
            num_scalar_prefetch=1, grid=(N_DEV,),
            in_specs=[pl.BlockSpec((t, rows), lambda s, slot: (0, s)),
                      pl.BlockSpec((n_o, t, c), lambda s, slot: (0, 0, 0))],
            out_specs=[pl.BlockSpec((1, rows, n_o * c), lambda s, slot: (s, 0, 0)),
                       pl.BlockSpec((rows, n_o * c), lambda s, slot: (0, 0))]),
        out_shape=[SDS((N_DEV, rows, n_o * c), BF16), SDS((rows, n_o * c), F32)],
        compiler_params=_cparams(),
    )(my_slot, a, b)


def _lane_lo():
    return lax.broadcasted_iota(jnp.int32, (1, 128), 1) < HEAD_DIM


def _offset_sums(gt):
    keys = gt.shape[1]
    gc = gt[0:CHUNK]
    for cc in range(1, QBLK // CHUNK):
        gc = gc + pltpu.roll(gt[cc * CHUNK:(cc + 1) * CHUNK], keys - cc * CHUNK, 1)
    hi = gc.astype(BF16)
    lo = (gc - hi.astype(F32)).astype(BF16)
    flip = (lax.broadcasted_iota(jnp.int32, (CHUNK, CHUNK), 0)
            + lax.broadcasted_iota(jnp.int32, (CHUNK, CHUNK), 1) == CHUNK - 1).astype(BF16)
    gf = _dot(flip, hi) + _dot(flip, lo)
    skew = pltpu.roll(gf, 0, 1, stride=1, stride_axis=0)
    return jnp.sum(skew, axis=0, keepdims=True)


def _band_bias(w_row, band):
    keys = w_row.shape[1]
    base = jnp.broadcast_to(w_row, (CHUNK, keys))
    skew = pltpu.roll(base, 0, 1, stride=1, stride_axis=0)
    skew = pltpu.roll(skew, keys - (CHUNK - 1), 1)
    col = lax.broadcasted_iota(jnp.int32, (CHUNK, keys), 1)
    chunk0 = jnp.where(col < band, skew, NEG)
    return jnp.concatenate(
        [chunk0] + [pltpu.roll(chunk0, cc * CHUNK, 1) for cc in range(1, QBLK // CHUNK)], axis=0)


def _silu_parts(g):
    sg = _sigmoid(g)
    return g * sg, sg * (1.0 + g * (1.0 - sg))


A_PAIRS = 2
A_LANES = 128 * A_PAIRS
A_STEPS = D_MODEL // A_LANES


def _a_specs():
    q = pl.BlockSpec((QBLK, A_LANES), lambda p, j: (j, p))
    ks = [pl.BlockSpec((QBLK, A_LANES), lambda p, j, b=b: (jnp.maximum(j - 2 + b, 0), A_STEPS + p)) for b in range(3)]
    vs = [pl.BlockSpec((QBLK, A_LANES), lambda p, j, b=b: (jnp.maximum(j - 2 + b, 0), 2 * A_STEPS + p))
          for b in range(3)]
    g = pl.BlockSpec((QBLK, A_LANES), lambda p, j: (j, 3 * A_STEPS + p))
    bias = pl.BlockSpec((A_PAIRS, 8, A_KEYS), lambda p, j: (p, 0, 0))
    return q, ks, vs, g, bias


def _a_fill_bias(w_ref, b_ref, j):
    _fill_bias(2 * A_PAIRS, lambda h: w_ref[h // 2, h % 2:h % 2 + 1, :], A_BAND, QBLK * (2 - j), 2, b_ref, j)


def _fill_bias(n, get_row, band, first_valid_col, early, bias_scr, j):
    @pl.when(j == 0)
    def _():
        for h in range(n):
            bias_scr[h] = _band_bias(get_row(h), band)

    @pl.when(j < early)
    def _():
        keys = bias_scr.shape[2]
        col_ok = lax.broadcasted_iota(jnp.int32, (1, keys), 1) >= first_valid_col
        for h in range(n):
            bias_scr[n + h] = jnp.where(col_ok, bias_scr[h], NEG)


def _head_logits(q, k, bias_scr, idx, sel):
    qm = jnp.where(sel, q, jnp.zeros_like(q)) * SCALE
    return qm, _dot_nt(qm, k) + bias_scr[idx]


def _row_sums_everywhere(r, sel):
    return jnp.where(sel, pltpu.roll(r, HEAD_DIM, 1), r)


def _own_everywhere(x, sel):
    return jnp.where(sel, x, pltpu.roll(x, HEAD_DIM, 1))


def _minus_rows(s, row_full):
    return jnp.concatenate([s[:, i:i + 128] - row_full for i in range(0, s.shape[1], 128)], axis=1)


def _attn_a_fwd(qkvg, bias, gather):
    t = qkvg.shape[0]
    nq = t // QBLK
    n_g = len(gather)
    q_spec, k_specs, v_specs, g_spec, bias_spec = _a_specs()

    def body(q_ref, k0, k1, k2, v0, v1, v2, g_ref, w_ref, *rest):
        shard_refs, rest = rest[:n_g], rest[n_g:]
        z_ref, o_ref, lse_ref = rest[:3]
        full_refs, (b_ref, *comm) = rest[3:3 + n_g], rest[3 + n_g:]
        p = pl.program_id(0)
        j = pl.program_id(1)
        start, forward, finish = _gather_phases(shard_refs, full_refs, *comm)
        pl.when(jnp.logical_and(p == 0, j == 0))(start)
        pl.when(jnp.logical_and(p == A_STEPS // 2, j == 0))(forward)
        _a_fill_bias(w_ref, b_ref, j)
        early = (j < 2).astype(jnp.int32)
        lane_lo = _lane_lo()
        for pp in range(A_PAIRS):
            cols = slice(128 * pp, 128 * (pp + 1))
            q = q_ref[:, cols]
            k = jnp.concatenate([k0[:, cols], k1[:, cols], k2[:, cols]], axis=0)
            v = jnp.concatenate([v0[:, cols], v1[:, cols], v2[:, cols]], axis=0)
            outs, lses = [], []
            for hh in range(2):
                sel = lane_lo if hh == 0 else jnp.logical_not(lane_lo)
                _, s = _head_logits(q, k, b_ref, 2 * pp + hh + 2 * A_PAIRS * early, sel)
                mx = jnp.max(s, axis=-1, keepdims=True)
                e = jnp.exp(s - mx).astype(BF16)
                r = _dot(e, jnp.where(sel, v, jnp.ones_like(v)))
                l = _row_sums_everywhere(r, sel)
                outs.append(r / l)
                lses.append(mx + jnp.log(l))
            o = jnp.where(lane_lo, outs[0], outs[1])
            silu, _ = _silu_parts(g_ref[:, cols].astype(F32))
            o_ref[:, cols] = o.astype(BF16)
            z_ref[:, cols] = (o * silu).astype(BF16)
            lse_ref[:, cols] = jnp.where(lane_lo, lses[0], lses[1])
        pl.when(jnp.logical_and(p == A_STEPS - 1, j == nq - 1))(finish)

    out_spec = pl.BlockSpec((QBLK, A_LANES), lambda p, j: (j, p))
    outs = pl.pallas_call(
        body, name="attn_a_fwd", grid=(A_STEPS, nq),
        in_specs=[q_spec, *k_specs, *v_specs, g_spec, bias_spec] + [ANY] * n_g,
        out_specs=[out_spec, out_spec, out_spec] + [ANY] * n_g,
        out_shape=[SDS((t, D_MODEL), BF16), SDS((t, D_MODEL), BF16), SDS((t, D_MODEL), F32)]
        + [SDS((N_DEV, *s.shape), s.dtype) for s in gather],
        scratch_shapes=[pltpu.VMEM((4 * A_PAIRS, QBLK, A_KEYS), F32)] + _gather_scratch(n_g),
        compiler_params=_cparams(),
    )(qkvg, qkvg, qkvg, qkvg, qkvg, qkvg, qkvg, qkvg, bias, *gather)
    return outs[0], outs[1], outs[2], list(outs[3:])


def _attn_a_bwd(qkvg, bias, out_a, lse, dz, scatter):
    t = qkvg.shape[0]
    nq = t // QBLK
    n_sc = len(scatter)
    q_spec, k_specs, v_specs, g_spec, bias_spec = _a_specs()

    def body(q_ref, k0, k1, k2, v0, v1, v2, g_ref, w_ref, o_ref, lse_ref, dz_ref, *rest):
        sc_refs, rest = rest[:n_sc], rest[n_sc:]
        dqg_ref, dkv_ref, dg_ref = rest[:3]
        land_refs, rest = rest[3:3 + n_sc], rest[3 + n_sc:]
        dk_acc, dv_acc, gt_acc, b_ref, send_sems, recv_sems = rest
        j = pl.program_id(1)
        first = jnp.logical_and(pl.program_id(0) == 0, j == 0)
        last = jnp.logical_and(pl.program_id(0) == A_STEPS - 1, j == nq - 1)

        @pl.when(first)
        def _():
            for cp in _scatter_copies(sc_refs, land_refs, send_sems, recv_sems):
                cp.start()

        _a_fill_bias(w_ref, b_ref, j)

        @pl.when(j == 0)
        def _():
            dk_acc[...] = jnp.zeros_like(dk_acc)
            dv_acc[...] = jnp.zeros_like(dv_acc)
            gt_acc[...] = jnp.zeros_like(gt_acc)

        early = (j < 2).astype(jnp.int32)
        lane_lo = _lane_lo()
        for pp in range(A_PAIRS):
            cols = slice(128 * pp, 128 * (pp + 1))
            q = q_ref[:, cols]
            k = jnp.concatenate([k0[:, cols], k1[:, cols], k2[:, cols]], axis=0)
            v = jnp.concatenate([v0[:, cols], v1[:, cols], v2[:, cols]], axis=0)
            o = o_ref[:, cols].astype(F32)
            lse_pair = lse_ref[:, cols]
            dzf = dz_ref[:, cols].astype(F32)
            silu, dsilu = _silu_parts(g_ref[:, cols].astype(F32))
            do = dzf * silu
            dqg_ref[1, :, cols] = (dzf * o * dsilu).astype(BF16)
            doo = do * o
            dqs = []
            dk_blk = jnp.zeros((A_KEYS, 128), F32)
            dv_blk = jnp.zeros((A_KEYS, 128), F32)
            for hh in range(2):
                sel = lane_lo if hh == 0 else jnp.logical_not(lane_lo)
                qm, s = _head_logits(q, k, b_ref, 2 * pp + hh + 2 * A_PAIRS * early, sel)
                p = jnp.exp(_minus_rows(s, _own_everywhere(lse_pair, sel)))
                delta = jnp.sum(jnp.where(sel, doo, 0.0), axis=-1, keepdims=True)
                dom = jnp.where(sel, do, 0.0).astype(BF16)
                dp = _dot_nt(dom, v)
                ds = p * (dp - delta)
                gt_acc[2 * pp + hh] += ds
                dsb = ds.astype(BF16)
                dqs.append(_dot(dsb, k) * SCALE)
                dk_blk = dk_blk + _dot_tn(dsb, qm)
                dv_blk = dv_blk + _dot_tn(p.astype(BF16), dom)
            dqg_ref[0, :, cols] = jnp.where(lane_lo, dqs[0], dqs[1]).astype(BF16)
            for b in range(3):
                @pl.when(j - 2 + b >= 0)
                def _(b=b, cols=cols, dk_blk=dk_blk, dv_blk=dv_blk):
                    rows = pl.ds(pl.multiple_of((j - 2 + b) * QBLK, QBLK), QBLK)
                    dk_acc[rows, cols] += dk_blk[b * QBLK:(b + 1) * QBLK]
                    dv_acc[rows, cols] += dv_blk[b * QBLK:(b + 1) * QBLK]

        @pl.when(j == nq - 1)
        def _():
            dkv_ref[0] = dk_acc[...].astype(BF16)
            dkv_ref[1] = dv_acc[...].astype(BF16)
            for pp in range(A_PAIRS):
                dg_ref[pp] = jnp.concatenate([_offset_sums(gt_acc[2 * pp]), _offset_sums(gt_acc[2 * pp + 1]),
                                              jnp.zeros((6, A_DIAG), F32)], axis=0)

        @pl.when(last)
        def _():
            for cp in _scatter_copies(sc_refs, land_refs, send_sems, recv_sems):
                cp.wait()

    blk = pl.BlockSpec((QBLK, A_LANES), lambda p, j: (j, p))
    outs = pl.pallas_call(
        body, name="attn_a_bwd", grid=(A_STEPS, nq),
        in_specs=[q_spec, *k_specs, *v_specs, g_spec, bias_spec, blk, blk, blk] + [ANY] * n_sc,
        out_specs=[pl.BlockSpec((2, QBLK, A_LANES), lambda p, j: (0, j, p)),
                   pl.BlockSpec((2, t, A_LANES), lambda p, j: (0, 0, p)),
                   pl.BlockSpec((A_PAIRS, 8, A_DIAG), lambda p, j: (p, 0, 0))] + [ANY] * n_sc,
        out_shape=[SDS((2, t, D_MODEL), BF16), SDS((2, t, D_MODEL), BF16), SDS((N_HEADS // 2, 8, A_DIAG), F32)]
        + [SDS((N_DEV - 1, *g.shape[1:]), g.dtype) for g in scatter],
        scratch_shapes=[pltpu.VMEM((t, A_LANES), F32), pltpu.VMEM((t, A_LANES), F32),
                        pltpu.VMEM((2 * A_PAIRS, QBLK, A_KEYS), F32), pltpu.VMEM((4 * A_PAIRS, QBLK, A_KEYS), F32),
                        pltpu.SemaphoreType.DMA(((N_DEV - 1) * n_sc,)),
                        pltpu.SemaphoreType.DMA(((N_DEV - 1) * n_sc,))],
        compiler_params=_cparams(),
    )(qkvg, qkvg, qkvg, qkvg, qkvg, qkvg, qkvg, qkvg, bias, out_a, lse, dz, *scatter)
    return outs[0], outs[1], outs[2], list(outs[3:])


def _b_specs():
    q = pl.BlockSpec((QBLK, 512), lambda h, j: (j, h))
    g = pl.BlockSpec((QBLK, 512), lambda h, j: (j, 2 + h))
    kp = pl.BlockSpec((128, 128), lambda h, j: (jnp.maximum(2 * j - 1, 0), 0))
    kc = pl.BlockSpec((QBLK, 128), lambda h, j: (j, 0))
    vp = pl.BlockSpec((128, 128), lambda h, j: (jnp.maximum(2 * j - 1, 0), 1))
    vc = pl.BlockSpec((QBLK, 128), lambda h, j: (j, 1))
    bias = pl.BlockSpec((B_GROUP, B_KEYS), lambda h, j: (h, 0))
    sinks = pl.BlockSpec(memory_space=pltpu.SMEM)
    return q, g, kp, kc, vp, vc, bias, sinks


def _b_operands(kp, kc, vp, vc, kvh):
    k = jnp.concatenate([kp[...], kc[...]], axis=0)
    v = jnp.concatenate([vp[...], vc[...]], axis=0)
    kr = pltpu.roll(k, HEAD_DIM, 1)
    vr = pltpu.roll(v, HEAD_DIM, 1)
    first = kvh == 0
    return (jnp.where(first, k, kr), jnp.where(first, kr, k),
            jnp.where(first, v, vr), jnp.where(first, vr, v))


def _attn_b_fwd(qg, kv, bias, sinks):
    t = qg.shape[0]
    q_spec, g_spec, kp_spec, kc_spec, vp_spec, vc_spec, bias_spec, sink_spec = _b_specs()

    def body(q_ref, g_ref, kp, kc, vp, vc, w_ref, sink_ref, z_ref, o_ref, lse_ref, b_ref):
        kvh = pl.program_id(0)
        j = pl.program_id(1)
        _fill_bias(B_GROUP, lambda h: w_ref[h:h + 1, :], B_BAND, 128, 1, b_ref, j)
        early = (j < 1).astype(jnp.int32)
        lane_lo = _lane_lo()
        k_lo, k_hi, v_lo, v_hi = _b_operands(kp, kc, vp, vc, kvh)
        for pp in range(B_GROUP // 2):
            cols = slice(128 * pp, 128 * (pp + 1))
            qp = q_ref[:, cols]
            outs, lses = [], []
            for hh in range(2):
                g = 2 * pp + hh
                sel = lane_lo if hh == 0 else jnp.logical_not(lane_lo)
                sink = sink_ref[kvh * B_GROUP + g]
                vv = v_lo if hh == 0 else v_hi
                _, s = _head_logits(qp, k_lo if hh == 0 else k_hi, b_ref, g + B_GROUP * early, sel)
                mx = jnp.maximum(jnp.max(s, axis=-1, keepdims=True), sink)
                e = jnp.exp(s - mx).astype(BF16)
                r = _dot(e, jnp.where(sel, vv, jnp.ones_like(vv)))
                l = _row_sums_everywhere(r, sel) + jnp.exp(sink - mx)
                outs.append(r / l)
                lses.append(mx + jnp.log(l))
            o = jnp.where(lane_lo, outs[0], outs[1])
            silu, _ = _silu_parts(g_ref[:, cols].astype(F32))
            o_ref[:, cols] = o.astype(BF16)
            z_ref[:, cols] = (o * silu).astype(BF16)
            lse_ref[:, cols] = jnp.where(lane_lo, lses[0], lses[1])

    out_spec = pl.BlockSpec((QBLK, 512), lambda h, j: (j, h))
    return pl.pallas_call(
        body, name="attn_b_fwd", grid=(B_KV_HEADS, t // QBLK),
        in_specs=[q_spec, g_spec, kp_spec, kc_spec, vp_spec, vc_spec, bias_spec, sink_spec],
        out_specs=[out_spec, out_spec, out_spec],
        out_shape=[SDS((t, D_MODEL), BF16), SDS((t, D_MODEL), BF16), SDS((t, D_MODEL), F32)],
        scratch_shapes=[pltpu.VMEM((2 * B_GROUP, QBLK, B_KEYS), F32)],
        compiler_params=_cparams(),
    )(qg, qg, kv, kv, kv, kv, bias, sinks)


def _attn_b_bwd(qg, kv, bias, sinks, out_b, lse, dz, bucket_onehot):
    t = qg.shape[0]
    nq = t // QBLK
    q_spec, g_spec, kp_spec, kc_spec, vp_spec, vc_spec, bias_spec, sink_spec = _b_specs()

    def body(q_ref, g_ref, kp, kc, vp, vc, w_ref, sink_ref, o_ref, lse_ref, dz_ref, oh_ref,
             dqg_ref, dkv_ref, dt5_ref, dsink_ref, gt_acc, b_ref):
        kvh = pl.program_id(0)
        j = pl.program_id(1)
        _fill_bias(B_GROUP, lambda h: w_ref[h:h + 1, :], B_BAND, 128, 1, b_ref, j)

        @pl.when(jnp.logical_and(kvh == 0, j == 0))
        def _():
            dkv_ref[...] = jnp.zeros_like(dkv_ref)

        @pl.when(j == 0)
        def _():
            gt_acc[...] = jnp.zeros_like(gt_acc)
            dsink_ref[...] = jnp.zeros_like(dsink_ref)

        early = (j < 1).astype(jnp.int32)
        lane_lo = _lane_lo()
        k_lo, k_hi, v_lo, v_hi = _b_operands(kp, kc, vp, vc, kvh)
        dk_blk = jnp.zeros((B_KEYS, 128), F32)
        dv_blk = jnp.zeros((B_KEYS, 128), F32)
        for pp in range(B_GROUP // 2):
            cols = slice(128 * pp, 128 * (pp + 1))
            qp = q_ref[:, cols]
            o = o_ref[:, cols].astype(F32)
            lse_pair = lse_ref[:, cols]
            dzf = dz_ref[:, cols].astype(F32)
            silu, dsilu = _silu_parts(g_ref[:, cols].astype(F32))
            do = dzf * silu
            dqg_ref[1, :, cols] = (dzf * o * dsilu).astype(BF16)
            doo = do * o
            dqs = []
            for hh in range(2):
                g = 2 * pp + hh
                sel = lane_lo if hh == 0 else jnp.logical_not(lane_lo)
                sink = sink_ref[kvh * B_GROUP + g]
                kk = k_lo if hh == 0 else k_hi
                vv = v_lo if hh == 0 else v_hi
                qm, s = _head_logits(qp, kk, b_ref, g + B_GROUP * early, sel)
                lse_h = _own_everywhere(lse_pair, sel)
                p = jnp.exp(_minus_rows(s, lse_h))
                delta = jnp.sum(jnp.where(sel, doo, 0.0), axis=-1, keepdims=True)
                dom = jnp.where(sel, do, 0.0).astype(BF16)
                dp = _dot_nt(dom, vv)
                ds = p * (dp - delta)
                gt_acc[g] += ds
                dsink_ref[g:g + 1, :] -= jnp.sum(jnp.exp(sink - lse_h) * delta, axis=0, keepdims=True)
                dsb = ds.astype(BF16)
                dqs.append(_dot(dsb, kk) * SCALE)
                dk_blk = dk_blk + _dot_tn(dsb, qm)
                dv_blk = dv_blk + _dot_tn(p.astype(BF16), dom)
            dqg_ref[0, :, cols] = jnp.where(lane_lo, dqs[0], dqs[1]).astype(BF16)
        mine = lane_lo == (kvh == 0)
        dk_add = jnp.where(mine, dk_blk + pltpu.roll(dk_blk, HEAD_DIM, 1), 0.0)
        dv_add = jnp.where(mine, dv_blk + pltpu.roll(dv_blk, HEAD_DIM, 1), 0.0)

        @pl.when(j >= 1)
        def _():
            rows = pl.ds(pl.multiple_of((2 * j - 1) * 128, 128), 128)
            dkv_ref[0, rows, :] += dk_add[0:128]
            dkv_ref[1, rows, :] += dv_add[0:128]

        rows = pl.ds(pl.multiple_of(j * QBLK, QBLK), QBLK)
        dkv_ref[0, rows, :] += dk_add[128:B_KEYS]
        dkv_ref[1, rows, :] += dv_add[128:B_KEYS]

        @pl.when(j == nq - 1)
        def _():
            dd = jnp.concatenate([_offset_sums(gt_acc[g]) for g in range(B_GROUP)], axis=0)
            hi = dd.astype(BF16)
            lo = (dd - hi.astype(F32)).astype(BF16)
            dt5_ref[...] = _dot(hi, oh_ref[...]) + _dot(lo, oh_ref[...])

    blk = pl.BlockSpec((QBLK, 512), lambda h, j: (j, h))
    return pl.pallas_call(
        body, name="attn_b_bwd", grid=(B_KV_HEADS, nq),
        in_specs=[q_spec, g_spec, kp_spec, kc_spec, vp_spec, vc_spec, bias_spec, sink_spec, blk, blk, blk,
                  pl.BlockSpec((B_DIAG, 128), lambda h, j: (0, 0))],
        out_specs=[pl.BlockSpec((2, QBLK, 512), lambda h, j: (0, j, h)),
                   pl.BlockSpec((2, t, 128), lambda h, j: (0, 0, 0)),
                   pl.BlockSpec((B_GROUP, 128), lambda h, j: (h, 0)),
                   pl.BlockSpec((B_GROUP, 128), lambda h, j: (h, 0))],
        out_shape=[SDS((2, t, D_MODEL), BF16), SDS((2, t, 128), F32),
                   SDS((N_HEADS, 128), F32), SDS((N_HEADS, 128), F32)],
        scratch_shapes=[pltpu.VMEM((B_GROUP, QBLK, B_KEYS), F32), pltpu.VMEM((2 * B_GROUP, QBLK, B_KEYS), F32)],
        compiler_params=_cparams(),
    )(qg, qg, kv, kv, kv, kv, bias, sinks, out_b, lse, dz, bucket_onehot)


def _a_bias_by_offset(rel_bias):
    m = np.arange(A_DIAG)
    idx = np.clip(A_BAND - 1 - m, -A_REL_CLIP, A_REL_CLIP) + A_REL_CLIP
    by_head = rel_bias[idx].T.reshape(N_HEADS // 2, 2, A_DIAG)
    return jnp.concatenate([by_head, jnp.zeros((N_HEADS // 2, 6, A_DIAG), F32)], axis=1)


def _a_bias_grad(offset_sums):
    first = 319
    tail = jnp.sum(offset_sums[:, :first], axis=1)
    body = jnp.flip(offset_sums[:, first:first + 320], axis=1)
    body = body.at[:, -1].add(tail)
    full = jnp.concatenate([jnp.zeros((N_HEADS, 193), F32), body], axis=1)
    return full.T


def _t5_bucket(rel):
    nb = T5_BUCKETS // 2
    max_exact = nb // 2
    ret = jnp.where(rel > 0, nb, 0)
    n = jnp.abs(rel)
    nf = jnp.maximum(n, 1).astype(jnp.float32)
    large = max_exact + (jnp.log(nf / max_exact) / math.log(T5_MAX_DIST / max_exact)
                         * (nb - max_exact)).astype(jnp.int32)
    large = jnp.minimum(large, nb - 1)
    return ret + jnp.where(n < max_exact, n, large)


def _b_offset_buckets():
    return _t5_bucket(jnp.arange(B_DIAG, dtype=jnp.int32) - (B_LEFT_CHUNKS * CHUNK + CHUNK - 1))


def _b_bias_by_offset(t5_table):
    return t5_table[_b_offset_buckets()].T


def _b_bucket_onehot():
    return (_b_offset_buckets()[:, None] == jnp.arange(128)[None, :]).astype(BF16)


def _local_step(my_slot, x, target, a_gain, w_in_a, rel_bias, late_shards, kv_gain, t5_table,
                b_gain, sinks, f_gain):
    a_bias = _a_bias_by_offset(rel_bias)
    b_bias = _b_bias_by_offset(t5_table)
    sinks_flat = sinks.reshape(N_HEADS)

    xn, qkvg = _norm_matmul(x, a_gain, w_in_a)
    z_a, out_a, lse_a, (w_in_b, w_out_a, w_out_b, kv_w) = _attn_a_fwd(qkvg, a_bias, late_shards)
    w_out_a = w_out_a.reshape(D_MODEL, D_MODEL)
    w_out_b = w_out_b.reshape(D_MODEL, D_MODEL)
    kv_w = kv_w.reshape(D_MODEL, 2 * 128)
    h1, kvn, hb, kv, qg = _layer_a_out(x, z_a, w_out_a, kv_gain, b_gain, kv_w, w_in_b)
    z_b, out_b, lse_b = _attn_b_fwd(qg, kv, b_bias, sinks_flat)
    dh2, dh2b, dz_b, loss, d_fn = _layer_b_out_loss(h1, z_b, w_out_b, f_gain, target)

    dqg_b, dkv_b, d_t5, d_sink = _attn_b_bwd(qg, kv, b_bias, sinks_flat, out_b, lse_b, dz_b, _b_bucket_onehot())
    dh1, dh1b, dz_a, d_bn, d_kn = _layer_b_in_bwd(dqg_b, dkv_b, w_in_b, kv_w, h1, dh2, b_gain, kv_gain, w_out_a)
    early = dict(
        b_w_out=_weight_grad_rows("grad_b_w_out", my_slot, z_b, dh2b[None]),
        b_w_in=_weight_grad_cols("grad_b_w_in", my_slot, hb, [dqg_b],
                                 [(0, o, c, 4 * o + c) for o in range(2) for c in range(4)], 256),
        kv_w=_weight_grad_rows("grad_kv_w", my_slot, kvn, dkv_b),
        a_w_out=_weight_grad_rows("grad_a_w_out", my_slot, z_a, dh1b[None]))
    dqg_a, dkv_a, d_rel, landed = _attn_a_bwd(qkvg, a_bias, out_a, lse_a, dz_a, [g[0] for g in early.values()])
    g_w_in_a = _weight_grad_cols(
        "grad_a_w_in", my_slot, xn, [dqg_a, dkv_a],
        [(0, 0, 0, 0), (0, 0, 1, 1), (1, 0, 0, 2), (1, 0, 1, 3), (1, 1, 0, 4), (1, 1, 1, 5), (0, 1, 0, 6), (0, 1, 1, 7)], 512)
    from_sibling, = _exchange_sibling([g_w_in_a[0]])
    x_i, y_i, c_i, chips = _place()
    del x_i, y_i
    forward_slots = jnp.stack([_slot(*chip, c_i) for chip in chips]).astype(jnp.int32)
    chip_sums = _pre_reduce("chip_sum_a_w_in", g_w_in_a[0], from_sibling, forward_slots)
    grad_x, d_an, from_chips = _layer_a_in_bwd(dqg_a, dkv_a, w_in_a, x, dh1, a_gain, chip_sums)

    matrices = {n: (g[1], [(land, 0, N_DEV - 1)]) for (n, g), land in zip(early.items(), landed)}
    matrices["a_w_in"] = (g_w_in_a[1], [(from_sibling, 3, 1), (from_chips, 0, 3)])
    small = dict(
        loss=loss[0, 0], a_norm=d_an, a_rel_bias=_a_bias_grad(d_rel[:, :2].reshape(N_HEADS, A_DIAG)),
        kv_norm=d_kn, t5_bias=d_t5[:, :T5_BUCKETS].T, b_norm=d_bn,
        b_sinks=d_sink[:, 0].reshape(1, N_HEADS), final_norm=d_fn)
    return grad_x, small, matrices


def _place():
    x, y, c = lax.axis_index("x"), lax.axis_index("y"), lax.axis_index("c")
    chips = [(1 - x, y), (x, 1 - y), (1 - x, 1 - y)]
    return x, y, c, chips


def _slot(px, py, pc):
    return 4 * px + 2 * py + pc


ANY = pl.BlockSpec(memory_space=pl.ANY)


def _peer(x, y, c, k):
    return (x ^ (k >> 2), y ^ ((k >> 1) & 1), c ^ (k & 1))


def _scatter_copies(grad_refs, land_refs, send_sems, recv_sems):
    x, y, c, _ = _place()
    copies = []
    for t, (grad, land) in enumerate(zip(grad_refs, land_refs)):
        for k in range(1, N_DEV):
            peer = _peer(x, y, c, k)
            sem = (N_DEV - 1) * t + k - 1
            copies.append(pltpu.make_async_remote_copy(
                src_ref=grad.at[_slot(*peer)], dst_ref=land.at[k - 1],
                send_sem=send_sems.at[sem], recv_sem=recv_sems.at[sem],
                device_id=peer, device_id_type=MESH))
    return copies


def _gather_phases(ins, outs, send_sems, recv_sems, local_sems):
    n = len(ins)
    x, y, c, chips = _place()
    me, sibling = (x, y, c), (x, y, 1 - c)

    def copy(t, k, block, to, src=None):
        dst = outs[t].at[_slot(*block)]
        return pltpu.make_async_remote_copy(
            src_ref=dst if src is None else src, dst_ref=dst,
            send_sem=send_sems.at[7 * t + k], recv_sem=recv_sems.at[7 * t + k],
            device_id=to, device_id_type=MESH)

    def lists():
        mine = [pltpu.make_async_copy(ins[t], outs[t].at[_slot(*me)], local_sems.at[t]) for t in range(n)]
        first = []
        for t in range(n):
            first.append(copy(t, 0, me, sibling, src=ins[t]))
            first += [copy(t, 1 + j, me, (*chip, c), src=ins[t]) for j, chip in enumerate(chips)]
        passed = [copy(t, 4 + j, (*chip, c), sibling) for t in range(n) for j, chip in enumerate(chips)]
        return mine, first, passed

    def start():
        mine, first, _ = lists()
        for cp in mine + first:
            cp.start()

    def forward():
        _, _, passed = lists()
        for t in range(n):
            for j, chip in enumerate(chips):
                copy(t, 1 + j, (*chip, c), me).wait_recv()
                passed[3 * t + j].start()

    def finish():
        mine, first, passed = lists()
        for t in range(n):
            copy(t, 0, sibling, me).wait_recv()
            for j, chip in enumerate(chips):
                copy(t, 4 + j, (*chip, 1 - c), me).wait_recv()
        for cp in first + passed:
            cp.wait_send()
        for cp in mine:
            cp.wait()

    return start, forward, finish


def _gather_scratch(n):
    return [pltpu.SemaphoreType.DMA((7 * n,)), pltpu.SemaphoreType.DMA((7 * n,)), pltpu.SemaphoreType.DMA((n,))]


def _all_gather(shards):
    n = len(shards)

    def body(*refs):
        start, forward, finish = _gather_phases(refs[:n], refs[n:2 * n], *refs[2 * n:])
        start()
        forward()
        finish()

    return pl.pallas_call(
        body, name="all_gather_weights",
        in_specs=[ANY] * n, out_specs=[ANY] * n,
        out_shape=[SDS((N_DEV, *s.shape), s.dtype) for s in shards],
        scratch_shapes=_gather_scratch(n),
    )(*shards)


def _exchange_sibling(grads):
    n = len(grads)

    def body(*refs):
        ins, outs = refs[:n], refs[n:2 * n]
        send_sems, recv_sems = refs[2 * n:]
        x, y, c, chips = _place()
        sibling = (x, y, 1 - c)
        copies = []
        for t in range(n):
            blocks = [(*chip, 1 - c) for chip in chips] + [sibling]
            for k, block in enumerate(blocks):
                copies.append(pltpu.make_async_remote_copy(
                    src_ref=ins[t].at[_slot(*block)], dst_ref=outs[t].at[k],
                    send_sem=send_sems.at[4 * t + k], recv_sem=recv_sems.at[4 * t + k],
                    device_id=sibling, device_id_type=MESH))
        for cp in copies:
            cp.start()
        for cp in copies:
            cp.wait()

    return pl.pallas_call(
        body, name="grads_to_sibling",
        in_specs=[ANY] * n, out_specs=[ANY] * n,
        out_shape=[SDS((4, *g.shape[1:]), g.dtype) for g in grads],
        scratch_shapes=[pltpu.SemaphoreType.DMA((4 * n,)), pltpu.SemaphoreType.DMA((4 * n,))],
    )(*grads)


def _chip_copies(sums_ref, land_ref, send_sems, recv_sems):
    x, y, c, chips = _place()
    del x, y
    return [pltpu.make_async_remote_copy(
        src_ref=sums_ref.at[j], dst_ref=land_ref.at[j], send_sem=send_sems.at[j], recv_sem=recv_sems.at[j],
        device_id=(*chip, c), device_id_type=MESH) for j, chip in enumerate(chips)]


def _row_tile(rows):
    return min(rows, 256)


def _pre_reduce(name, g, from_sibling, slots):
    _, r, c = g.shape
    tr = _row_tile(r)

    def body(slots_ref, g_ref, s_ref, o_ref):
        del slots_ref
        o_ref[...] = (g_ref[...].astype(F32) + s_ref[...].astype(F32)).astype(BF16)

    return pl.pallas_call(
        body, name=name,
        grid_spec=pltpu.PrefetchScalarGridSpec(
            num_scalar_prefetch=1, grid=(3, r // tr),
            in_specs=[pl.BlockSpec((1, tr, c), lambda j, i, s: (s[j], i, 0)),
                      pl.BlockSpec((1, tr, c), lambda j, i, s: (j, i, 0))],
            out_specs=pl.BlockSpec((1, tr, c), lambda j, i, s: (j, i, 0))),
        out_shape=SDS((3, r, c), BF16),
        compiler_params=_cparams(),
    )(slots, g, from_sibling)


def _adamw(w, g, m, v):
    m2 = ADAM_B1 * m + (1.0 - ADAM_B1) * g
    v2 = ADAM_B2 * v + (1.0 - ADAM_B2) * jnp.square(g)
    m_hat = m2 / (1.0 - ADAM_B1 ** ADAM_STEP)
    v_hat = v2 / (1.0 - ADAM_B2 ** ADAM_STEP)
    delta = -ADAM_LR * (m_hat / (jnp.sqrt(v_hat) + ADAM_EPS) + ADAM_WD * w)
    return delta, m2, v2


def _reduce_adamw(name, own, partials, w, m, v):
    r, c = own.shape
    tr = _row_tile(r)
    n_p = len(partials)

    def body(own_ref, *rest):
        p_refs, (w_ref, m_ref, v_ref, grad_ref, d_ref, nm_ref, nv_ref) = rest[:n_p], rest[n_p:]
        grad = own_ref[...]
        for p_ref, (_, _, count) in zip(p_refs, partials):
            for j in range(count):
                grad = grad + p_ref[j].astype(F32)
        grad_ref[...] = grad
        d_ref[...], nm_ref[...], nv_ref[...] = _adamw(w_ref[...], grad, m_ref[...], v_ref[...])

    flat = pl.BlockSpec((tr, c), lambda i: (i, 0))
    return pl.pallas_call(
        body, name=name, grid=(r // tr,),
        in_specs=[flat] + [pl.BlockSpec((count, tr, c), lambda i, first=first, count=count: (first // count, i, 0))
                           for _, first, count in partials] + [flat, flat, flat],
        out_specs=[flat, flat, flat, flat],
        out_shape=[SDS((r, c), F32)] * 4,
        compiler_params=_cparams(),
    )(own, *[p[0] for p in partials], w, m, v)


_SMALL = (("a_norm", 8), ("a_rel_bias", 72), ("kv_norm", 8), ("t5_bias", 8), ("b_norm", 8),
          ("b_sinks", 8), ("final_norm", 8), ("loss", 8))
_SMALL_ROWS = sum(r for _, r in _SMALL)


def _pack_small(parts):
    rows = []
    for name, n_rows in _SMALL:
        flat = parts[name].reshape(-1).astype(F32)
        rows.append(jnp.pad(flat, (0, n_rows * 128 - flat.shape[0])).reshape(n_rows, 128))
    return jnp.concatenate(rows, axis=0)


def _unpack_small(buf, shapes):
    out, at = {}, 0
    for name, n_rows in _SMALL:
        size = int(np.prod(shapes[name])) if shapes[name] else 1
        out[name] = buf[at:at + n_rows].reshape(-1)[:size].reshape(shapes[name])
        at += n_rows
    return out


def _small_allreduce_adamw(gbuf, wbuf, mbuf, vbuf):
    def body(g_ref, w_ref, m_ref, v_ref, sum_ref, d_ref, nm_ref, nv_ref, land_ref, send_sems, recv_sems):
        x, y, c, _ = _place()
        my_slot = _slot(x, y, c)
        land_ref[my_slot] = g_ref[...]
        copies = []
        for k in range(1, N_DEV):
            peer = (x ^ (k >> 2), y ^ ((k >> 1) & 1), c ^ (k & 1))
            copies.append(pltpu.make_async_remote_copy(
                src_ref=g_ref, dst_ref=land_ref.at[my_slot],
                send_sem=send_sems.at[k - 1], recv_sem=recv_sems.at[k - 1],
                device_id=peer, device_id_type=MESH))
        for cp in copies:
            cp.start()
        for k in range(1, N_DEV):
            peer_slot = _slot(x ^ (k >> 2), y ^ ((k >> 1) & 1), c ^ (k & 1))
            pltpu.make_async_remote_copy(
                src_ref=g_ref, dst_ref=land_ref.at[peer_slot],
                send_sem=send_sems.at[k - 1], recv_sem=recv_sems.at[k - 1],
                device_id=(x, y, c), device_id_type=MESH).wait_recv()
        for cp in copies:
            cp.wait_send()
        total = land_ref[0]
        for s in range(1, N_DEV):
            total = total + land_ref[s]
        sum_ref[...] = total
        d_ref[...], nm_ref[...], nv_ref[...] = _adamw(w_ref[...], total, m_ref[...], v_ref[...])

    vm = pl.BlockSpec(memory_space=pltpu.VMEM)
    shape = SDS((_SMALL_ROWS, 128), F32)
    return pl.pallas_call(
        body, name="small_allreduce_adamw",
        in_specs=[vm] * 4, out_specs=[vm] * 4, out_shape=[shape] * 4,
        scratch_shapes=[pltpu.VMEM((N_DEV, _SMALL_ROWS, 128), F32),
                        pltpu.SemaphoreType.DMA((N_DEV - 1,)), pltpu.SemaphoreType.DMA((N_DEV - 1,))],
    )(gbuf, wbuf, mbuf, vbuf)


def kernel(x, a_norm, a_w_in, a_rel_bias, a_w_out, kv_norm, kv_w, t5_bias, b_norm, b_w_in, b_sinks, b_w_out, final_norm, loss_target, m_a_norm, m_a_w_in, m_a_rel_bias, m_a_w_out, m_kv_norm, m_kv_w, m_t5_bias, m_b_norm, m_b_w_in, m_b_sinks, m_b_w_out, m_final_norm, v_a_norm, v_a_w_in, v_a_rel_bias, v_a_w_out, v_kv_norm, v_kv_w, v_t5_bias, v_b_norm, v_b_w_in, v_b_sinks, v_b_w_out, v_final_norm):
    xi, yi, ci = lax.axis_index("x"), lax.axis_index("y"), lax.axis_index("c")
    my_slot = _slot(xi, yi, ci)

    w_in_a, a_gain = _all_gather([a_w_in[0].astype(BF16), a_norm])
    a_gain = a_gain.reshape(1, D_MODEL)

    slot_arr = jnp.reshape(my_slot, (1,)).astype(jnp.int32)
    late_shards = [b_w_in[0].astype(BF16), a_w_out[0].astype(BF16), b_w_out[0].astype(BF16), kv_w.astype(BF16)]
    grad_x, loc, matrices = _local_step(
        slot_arr, x[0], loss_target[0], a_gain, w_in_a, a_rel_bias[0], late_shards,
        kv_norm.reshape(1, D_MODEL), t5_bias, b_norm, b_sinks, final_norm.reshape(1, D_MODEL))

    shard_w = dict(a_w_in=a_w_in[0], b_w_in=b_w_in[0], a_w_out=a_w_out[0], b_w_out=b_w_out[0], kv_w=kv_w)
    shard_m = dict(a_w_in=m_a_w_in[0], b_w_in=m_b_w_in[0], a_w_out=m_a_w_out[0], b_w_out=m_b_w_out[0], kv_w=m_kv_w)
    shard_v = dict(a_w_in=v_a_w_in[0], b_w_in=v_b_w_in[0], a_w_out=v_a_w_out[0], b_w_out=v_b_w_out[0], kv_w=v_kv_w)
    big = {n: _reduce_adamw("adamw_" + n, own, partials, shard_w[n], shard_m[n], shard_v[n])
           for n, (own, partials) in matrices.items()}

    def own_row(vec):
        return lax.dynamic_update_slice(jnp.zeros((N_DEV, 128), F32), vec, (my_slot, 0))

    zero = jnp.zeros((), F32)
    small_w = dict(a_norm=own_row(a_norm), a_rel_bias=a_rel_bias, kv_norm=kv_norm, t5_bias=t5_bias,
                   b_norm=b_norm, b_sinks=b_sinks, final_norm=final_norm, loss=zero)
    small_m = dict(a_norm=own_row(m_a_norm), a_rel_bias=m_a_rel_bias, kv_norm=m_kv_norm, t5_bias=m_t5_bias,
                   b_norm=m_b_norm, b_sinks=m_b_sinks, final_norm=m_final_norm, loss=zero)
    small_v = dict(a_norm=own_row(v_a_norm), a_rel_bias=v_a_rel_bias, kv_norm=v_kv_norm, t5_bias=v_t5_bias,
                   b_norm=v_b_norm, b_sinks=v_b_sinks, final_norm=v_final_norm, loss=zero)
    small = _small_allreduce_adamw(_pack_small(loc), _pack_small(small_w), _pack_small(small_m), _pack_small(small_v))
    shapes = dict(a_norm=(N_DEV, 128), a_rel_bias=a_rel_bias.shape, kv_norm=kv_norm.shape, t5_bias=t5_bias.shape,
                  b_norm=b_norm.shape, b_sinks=b_sinks.shape, final_norm=final_norm.shape, loss=())
    sm = [_unpack_small(buf, shapes) for buf in small]
    for part in sm:
        part["a_norm"] = lax.dynamic_slice(part["a_norm"], (my_slot, 0), (1, 128))

    order = ("a_norm", "a_w_in", "a_rel_bias", "a_w_out", "kv_norm", "kv_w", "t5_bias", "b_norm",
             "b_w_in", "b_sinks", "b_w_out", "final_norm")
    lead = dict(a_w_in=True, b_w_in=True, a_w_out=True, b_w_out=True, kv_w=False)

    def pick(kind, name):
        if name in big:
            val = big[name][kind]
            return val[None] if lead[name] else val
        return sm[kind][name]

    outs = [sm[0]["loss"], grad_x[None]]
    for kind in range(4):
        outs += [pick(kind, n) for n in order]
    return tuple(outs)
```

```python
import functools
import math

import numpy as np
import jax
import jax.numpy as jnp
from jax import lax
from jax.experimental import pallas as pl
from jax.experimental.pallas import tpu as pltpu

F32 = jnp.float32
BF16 = jnp.bfloat16
SDS = jax.ShapeDtypeStruct

D_MODEL = 1024
HEAD_DIM = 64
CHUNK = 64
N_HEADS = 16
RMS_EPS = 1e-6
A_LEFT_CHUNKS = 8
A_BAND = (A_LEFT_CHUNKS + 1) * CHUNK
A_REL_CLIP = 256
B_KV_HEADS = 2
B_GROUP = 8
B_LEFT_CHUNKS = 2
B_BAND = (B_LEFT_CHUNKS + 1) * CHUNK
T5_BUCKETS = 32
T5_MAX_DIST = 128
QBLK = 256
A_KEYS = 3 * QBLK
B_QBLK = 128
B_KEYS = B_QBLK + 128
A_DIAG = A_KEYS
B_DIAG = B_KEYS
NEG = -1e30
SCALE = HEAD_DIM ** -0.5
N_DEV = 8

ADAM_LR = 0.001
ADAM_B1 = 0.9
ADAM_B2 = 0.999
ADAM_EPS = 1e-08
ADAM_WD = 0.01
ADAM_STEP = 10

VMEM_LIMIT_BYTES = 56 * 1024 * 1024
MESH = pl.DeviceIdType.MESH


def _cparams():
    return pltpu.CompilerParams(vmem_limit_bytes=VMEM_LIMIT_BYTES)


def _dot(a, b):
    return jnp.dot(a, b, preferred_element_type=F32)


def _dot_nt(a, b):
    return lax.dot_general(a, b, (((1,), (1,)), ((), ())), preferred_element_type=F32)


def _dot_tn(a, b):
    return lax.dot_general(a, b, (((0,), (0,)), ((), ())), preferred_element_type=F32)


def _rstd(xf):
    return lax.rsqrt(jnp.mean(xf * xf, axis=-1, keepdims=True) + RMS_EPS)


def _sigmoid(x):
    return 1.0 / (1.0 + jnp.exp(-x))


def _norm_matmul(x, gain, w):
    t = x.shape[0]
    nb, _, tn = w.shape
    tm = min(t, 1024)

    def body(x_ref, g_ref, w_ref, xn_ref, o_ref):
        @pl.when(pl.program_id(1) == 0)
        def _():
            xf = x_ref[...]
            xn_ref[...] = ((xf * _rstd(xf)) * g_ref[...]).astype(BF16)

        o_ref[...] = _dot(xn_ref[...], w_ref[0]).astype(BF16)

    return pl.pallas_call(
        body, name="norm_matmul", grid=(t // tm, nb),
        in_specs=[pl.BlockSpec((tm, D_MODEL), lambda m, n: (m, 0)),
                  pl.BlockSpec((1, D_MODEL), lambda m, n: (0, 0)),
                  pl.BlockSpec((1, D_MODEL, tn), lambda m, n: (n, 0, 0))],
        out_specs=[pl.BlockSpec((tm, D_MODEL), lambda m, n: (m, 0)),
                   pl.BlockSpec((tm, tn), lambda m, n: (m, n))],
        out_shape=[SDS((t, D_MODEL), BF16), SDS((t, nb * tn), BF16)],
        compiler_params=_cparams(),
    )(x, gain, w)


def _layer_a_out(x, z, w_out, kv_gain, b_gain, kv_w, w_in_b):
    t = x.shape[0]
    tm = min(t, 512)
    nb, _, tn = w_in_b.shape

    def body(x_ref, z_ref, wo_ref, kvg_ref, bg_ref, kvw_ref, wb_ref,
             h1_ref, kvn_ref, hb_ref, kv_ref, qg_ref):
        h1 = x_ref[...] + _dot(z_ref[...], wo_ref[...])
        h1_ref[...] = h1
        y0 = h1 * _rstd(h1)
        kvn = (y0 * kvg_ref[...]).astype(BF16)
        hb = (y0 * bg_ref[...]).astype(BF16)
        kvn_ref[...] = kvn
        hb_ref[...] = hb
        kv_ref[...] = _dot(kvn, kvw_ref[...]).astype(BF16)
        for i in range(nb):
            qg_ref[:, i * tn:(i + 1) * tn] = _dot(hb, wb_ref[i]).astype(BF16)

    row = lambda m: (m, 0)
    fix2 = lambda m: (0, 0)
    return pl.pallas_call(
        body, name="layer_a_out", grid=(t // tm,),
        in_specs=[pl.BlockSpec((tm, D_MODEL), row), pl.BlockSpec((tm, D_MODEL), row),
                  pl.BlockSpec((D_MODEL, D_MODEL), fix2),
                  pl.BlockSpec((1, D_MODEL), fix2), pl.BlockSpec((1, D_MODEL), fix2),
                  pl.BlockSpec((D_MODEL, 256), fix2),
                  pl.BlockSpec((nb, D_MODEL, tn), lambda m: (0, 0, 0))],
        out_specs=[pl.BlockSpec((tm, D_MODEL), row), pl.BlockSpec((tm, D_MODEL), row),
                   pl.BlockSpec((tm, D_MODEL), row), pl.BlockSpec((tm, 256), row),
                   pl.BlockSpec((tm, nb * tn), row)],
        out_shape=[SDS((t, D_MODEL), F32), SDS((t, D_MODEL), BF16), SDS((t, D_MODEL), BF16),
                   SDS((t, 256), BF16), SDS((t, nb * tn), BF16)],
        compiler_params=_cparams(),
    )(x, z, w_out, kv_gain, b_gain, kv_w, w_in_b)


def _layer_b_out_loss(h1, z, w_out, f_gain, target):
    t = h1.shape[0]
    tm = min(t, 512)

    def body(h1_ref, z_ref, wo_ref, fg_ref, tgt_ref,
             dh2_ref, dh2b_ref, dz_ref, loss_ref, dfn_ref):
        @pl.when(pl.program_id(0) == 0)
        def _():
            loss_ref[...] = jnp.zeros_like(loss_ref)
            dfn_ref[...] = jnp.zeros_like(dfn_ref)

        h2 = h1_ref[...] + _dot(z_ref[...], wo_ref[...])
        r = _rstd(h2)
        yn = h2 * r
        fg = fg_ref[...]
        err = yn * fg - tgt_ref[...]
        loss_ref[...] += (0.5 / D_MODEL) * jnp.sum(err * err)
        dy = err * (1.0 / D_MODEL)
        dfn_ref[...] += jnp.sum(dy * yn, axis=0, keepdims=True)
        u = dy * fg
        dh2 = r * u - h2 * ((r * r * r) * jnp.mean(u * h2, axis=-1, keepdims=True))
        dh2_ref[...] = dh2
        dh2b = dh2.astype(BF16)
        dh2b_ref[...] = dh2b
        dz_ref[...] = _dot_nt(dh2b, wo_ref[...]).astype(BF16)

    row = lambda m: (m, 0)
    fix2 = lambda m: (0, 0)
    return pl.pallas_call(
        body, name="layer_b_out_loss", grid=(t // tm,),
        in_specs=[pl.BlockSpec((tm, D_MODEL), row), pl.BlockSpec((tm, D_MODEL), row),
                  pl.BlockSpec((D_MODEL, D_MODEL), fix2), pl.BlockSpec((1, D_MODEL), fix2),
                  pl.BlockSpec((tm, D_MODEL), row)],
        out_specs=[pl.BlockSpec((tm, D_MODEL), row), pl.BlockSpec((tm, D_MODEL), row),
                   pl.BlockSpec((tm, D_MODEL), row), pl.BlockSpec((1, 128), fix2),
                   pl.BlockSpec((1, D_MODEL), fix2)],
        out_shape=[SDS((t, D_MODEL), F32), SDS((t, D_MODEL), BF16), SDS((t, D_MODEL), BF16),
                   SDS((1, 128), F32), SDS((1, D_MODEL), F32)],
        compiler_params=_cparams(),
    )(h1, z, w_out, f_gain, target)


def _layer_b_in_bwd(dqg, dkv, w_in_b, kv_w, h1, dh2, b_gain, kv_gain, w_out_a):
    t = h1.shape[0]
    tm = min(t, 256)
    nb, _, tn = w_in_b.shape
    per = D_MODEL // tn

    def body(dqg_ref, dkv_ref, wb_ref, kvw_ref, h1_ref, dh2_ref, bg_ref, kvg_ref, wo_ref,
             dh1_ref, dh1b_ref, dz_ref, dbn_ref, dkn_ref):
        @pl.when(pl.program_id(0) == 0)
        def _():
            dbn_ref[...] = jnp.zeros_like(dbn_ref)
            dkn_ref[...] = jnp.zeros_like(dkn_ref)

        dhb = jnp.zeros((tm, D_MODEL), F32)
        for i in range(nb):
            blk = dqg_ref[i // per, :, (i % per) * tn:(i % per + 1) * tn]
            dhb = dhb + _dot_nt(blk, wb_ref[i])
        dkn = (_dot_nt(dkv_ref[0].astype(BF16), kvw_ref[:, 0:128])
               + _dot_nt(dkv_ref[1].astype(BF16), kvw_ref[:, 128:256]))
        h1 = h1_ref[...]
        r = _rstd(h1)
        xr = h1 * r
        dbn_ref[...] += jnp.sum(dhb * xr, axis=0, keepdims=True)
        dkn_ref[...] += jnp.sum(dkn * xr, axis=0, keepdims=True)
        u = dhb * bg_ref[...] + dkn * kvg_ref[...]
        dh1 = dh2_ref[...] + r * u - h1 * ((r * r * r) * jnp.mean(u * h1, axis=-1, keepdims=True))
        dh1_ref[...] = dh1
        dh1b = dh1.astype(BF16)
        dh1b_ref[...] = dh1b
        dz_ref[...] = _dot_nt(dh1b, wo_ref[...]).astype(BF16)

    row = lambda m: (m, 0)
    fix2 = lambda m: (0, 0)
    return pl.pallas_call(
        body, name="layer_b_in_bwd", grid=(t // tm,),
        in_specs=[pl.BlockSpec((2, tm, D_MODEL), lambda m: (0, m, 0)),
                  pl.BlockSpec((2, tm, 128), lambda m: (0, m, 0)),
                  pl.BlockSpec((nb, D_MODEL, tn), lambda m: (0, 0, 0)),
                  pl.BlockSpec((D_MODEL, 256), fix2),
                  pl.BlockSpec((tm, D_MODEL), row), pl.BlockSpec((tm, D_MODEL), row),
                  pl.BlockSpec((1, D_MODEL), fix2), pl.BlockSpec((1, D_MODEL), fix2),
                  pl.BlockSpec((D_MODEL, D_MODEL), fix2)],
        out_specs=[pl.BlockSpec((tm, D_MODEL), row), pl.BlockSpec((tm, D_MODEL), row),
                   pl.BlockSpec((tm, D_MODEL), row), pl.BlockSpec((1, D_MODEL), fix2),
                   pl.BlockSpec((1, D_MODEL), fix2)],
        out_shape=[SDS((t, D_MODEL), F32), SDS((t, D_MODEL), BF16), SDS((t, D_MODEL), BF16),
                   SDS((1, D_MODEL), F32), SDS((1, D_MODEL), F32)],
        compiler_params=_cparams(),
    )(dqg, dkv, w_in_b, kv_w, h1, dh2, b_gain, kv_gain, w_out_a)


def _layer_a_in_bwd(dqg, dkv, w_in_a, x, dh1, a_gain, chip_sums):
    t = x.shape[0]
    tm = min(t, 256)
    nb, _, tn = w_in_a.shape
    per = D_MODEL // tn

    def body(dqg_ref, dkv_ref, w_ref, x_ref, dh1_ref, ag_ref, sums_ref, dx_ref, dan_ref, land_ref,
             send_sems, recv_sems):
        @pl.when(pl.program_id(0) == 0)
        def _():
            dan_ref[...] = jnp.zeros_like(dan_ref)
            for cp in _chip_copies(sums_ref, land_ref, send_sems, recv_sems):
                cp.start()

        dxn = jnp.zeros((tm, D_MODEL), F32)
        for i in range(nb):
            part = i // per
            src = dqg_ref if part in (0, 3) else dkv_ref
            outer = {0: 0, 3: 1, 1: 0, 2: 1}[part]
            blk = src[outer, :, (i % per) * tn:(i % per + 1) * tn]
            dxn = dxn + _dot_nt(blk, w_ref[i])
        xf = x_ref[...]
        r = _rstd(xf)
        dan_ref[...] += jnp.sum(dxn * (xf * r), axis=0, keepdims=True)
        u = dxn * ag_ref[...]
        dx_ref[...] = dh1_ref[...] + r * u - xf * ((r * r * r) * jnp.mean(u * xf, axis=-1, keepdims=True))

        @pl.when(pl.program_id(0) == t // tm - 1)
        def _():
            for cp in _chip_copies(sums_ref, land_ref, send_sems, recv_sems):
                cp.wait()

    row = lambda m: (m, 0)
    fix2 = lambda m: (0, 0)
    return pl.pallas_call(
        body, name="layer_a_in_bwd", grid=(t // tm,),
        in_specs=[pl.BlockSpec((2, tm, D_MODEL), lambda m: (0, m, 0)),
                  pl.BlockSpec((2, tm, D_MODEL), lambda m: (0, m, 0)),
                  pl.BlockSpec((nb, D_MODEL, tn), lambda m: (0, 0, 0)),
                  pl.BlockSpec((tm, D_MODEL), row), pl.BlockSpec((tm, D_MODEL), row),
                  pl.BlockSpec((1, D_MODEL), fix2), ANY],
        out_specs=[pl.BlockSpec((tm, D_MODEL), row), pl.BlockSpec((1, D_MODEL), fix2), ANY],
        out_shape=[SDS((t, D_MODEL), F32), SDS((1, D_MODEL), F32), SDS(chip_sums.shape, chip_sums.dtype)],
        scratch_shapes=[pltpu.SemaphoreType.DMA((3,)), pltpu.SemaphoreType.DMA((3,))],
        compiler_params=_cparams(),
    )(dqg, dkv, w_in_a, x, dh1, a_gain, chip_sums)


def _lut(s, vals):
    r = jnp.int32(vals[0])
    for i in range(1, len(vals)):
        r = jnp.where(s == i, jnp.int32(vals[i]), r)
    return r


def _held(steps, i):
    seq, cur = [None] * len(steps), None
    for k in range(len(steps) - 1, -1, -1):
        if steps[k][0] == i:
            cur = steps[k][1:3]
        seq[k] = cur
    for k in range(len(steps)):
        cur = seq[k] = seq[k] if seq[k] is not None else cur
    return seq


def _weight_grad_cols(name, my_slot, a, bs, steps, tn):
    t, dw = a.shape
    n_arr = len(bs)
    which = [s[0] for s in steps]
    blks = [s[3] for s in steps]

    def body(slot_ref, a_ref, *rest):
        b_refs, (o_ref, own_ref, at_ref) = rest[:n_arr], rest[n_arr:]
        s = pl.program_id(0)

        @pl.when(s == 0)
        def _():
            at_ref[...] = a_ref[...].T

        for i in range(n_arr):
            @pl.when(_lut(s, which) == i)
            def _(i=i):
                res = _dot(at_ref[...], b_refs[i][0])
                o_ref[0] = res.astype(BF16)

                @pl.when(_lut(s, blks) == slot_ref[0])
                def _():
                    own_ref[...] = res

    def b_spec(i):
        held = _held(steps, i)
        return pl.BlockSpec((1, t, tn), lambda s, slot: (_lut(s, [h[0] for h in held]), 0,
                                                         _lut(s, [h[1] for h in held])))

    return pl.pallas_call(
        body, name=name,
        grid_spec=pltpu.PrefetchScalarGridSpec(
            num_scalar_prefetch=1, grid=(len(steps),),
            in_specs=[pl.BlockSpec((t, dw), lambda s, slot: (0, 0))] + [b_spec(i) for i in range(n_arr)],
            out_specs=[pl.BlockSpec((1, dw, tn), lambda s, slot: (_lut(s, blks), 0, 0)),
                       pl.BlockSpec((dw, tn), lambda s, slot: (0, 0))],
            scratch_shapes=[pltpu.VMEM((dw, t), BF16)]),
        out_shape=[SDS((N_DEV, dw, tn), BF16), SDS((dw, tn), F32)],
        compiler_params=_cparams(),
    )(my_slot, a, *bs)


def _weight_grad_rows(name, my_slot, a, b):
    t, dw = a.shape
    n_o, _, c = b.shape
    rows = dw // N_DEV

    def body(slot_ref, a_ref, b_ref, o_ref, own_ref):
        at = a_ref[...].T
        res = [_dot(at, b_ref[o].astype(BF16)) for o in range(n_o)]
        for o in range(n_o):
            o_ref[0, :, o * c:(o + 1) * c] = res[o].astype(BF16)

        @pl.when(pl.program_id(0) == slot_ref[0])
        def _():
            for o in range(n_o):
                own_ref[:, o * c:(o + 1) * c] = res[o]

    return pl.pallas_call(
        body, name=name,
        grid_spec=pltpu.PrefetchScalarGridSpec(
            num_scalar_prefetch=1, grid=(N_DEV,),
            in_specs=[pl.BlockSpec((t, rows), lambda s, slot: (0, s)),
                      pl.BlockSpec((n_o, t, c), lambda s, slot: (0, 0, 0))],
            out_specs=[pl.BlockSpec((1, rows, n_o * c), lambda s, slot: (s, 0, 0)),
                       pl.BlockSpec((rows, n_o * c), lambda s, slot: (0, 0))]),
        out_shape=[SDS((N_DEV, rows, n_o * c), BF16), SDS((rows, n_o * c), F32)],
        compiler_params=_cparams(),
    )(my_slot, a, b)


def _lane_lo():
    return lax.broadcasted_iota(jnp.int32, (1, 128), 1) < HEAD_DIM


def _offset_sums(gt):
    keys = gt.shape[1]
    gc = gt[0:CHUNK]
    for cc in range(1, gt.shape[0] // CHUNK):
        gc = gc + pltpu.roll(gt[cc * CHUNK:(cc + 1) * CHUNK], keys - cc * CHUNK, 1)
    hi = gc.astype(BF16)
    lo = (gc - hi.astype(F32)).astype(BF16)
    flip = (lax.broadcasted_iota(jnp.int32, (CHUNK, CHUNK), 0)
            + lax.broadcasted_iota(jnp.int32, (CHUNK, CHUNK), 1) == CHUNK - 1).astype(BF16)
    gf = _dot(flip, hi) + _dot(flip, lo)
    skew = pltpu.roll(gf, 0, 1, stride=1, stride_axis=0)
    return jnp.sum(skew, axis=0, keepdims=True)


def _band_bias(w_row, band, rows):
    keys = w_row.shape[1]
    base = jnp.broadcast_to(w_row, (CHUNK, keys))
    skew = pltpu.roll(base, 0, 1, stride=1, stride_axis=0)
    skew = pltpu.roll(skew, keys - (CHUNK - 1), 1)
    col = lax.broadcasted_iota(jnp.int32, (CHUNK, keys), 1)
    chunk0 = jnp.where(col < band, skew, NEG)
    return jnp.concatenate(
        [chunk0] + [pltpu.roll(chunk0, cc * CHUNK, 1) for cc in range(1, rows // CHUNK)], axis=0)


def _silu_parts(g):
    sg = _sigmoid(g)
    return g * sg, sg * (1.0 + g * (1.0 - sg))


A_PAIRS = 2
A_LANES = 128 * A_PAIRS
A_STEPS = D_MODEL // A_LANES


def _a_specs():
    q = pl.BlockSpec((QBLK, A_LANES), lambda p, j: (j, p))
    ks = [pl.BlockSpec((QBLK, A_LANES), lambda p, j, b=b: (jnp.maximum(j - 2 + b, 0), A_STEPS + p)) for b in range(3)]
    vs = [pl.BlockSpec((QBLK, A_LANES), lambda p, j, b=b: (jnp.maximum(j - 2 + b, 0), 2 * A_STEPS + p))
          for b in range(3)]
    g = pl.BlockSpec((QBLK, A_LANES), lambda p, j: (j, 3 * A_STEPS + p))
    bias = pl.BlockSpec((A_PAIRS, 8, A_KEYS), lambda p, j: (p, 0, 0))
    return q, ks, vs, g, bias


def _a_fill_bias(w_ref, b_ref, j):
    _fill_bias(2 * A_PAIRS, lambda h: w_ref[h // 2, h % 2:h % 2 + 1, :], A_BAND, QBLK * (2 - j), 2, b_ref, j)


def _fill_bias(n, get_row, band, first_valid_col, early, bias_scr, j):
    @pl.when(j == 0)
    def _():
        for h in range(n):
            bias_scr[h] = _band_bias(get_row(h), band, bias_scr.shape[1])

    @pl.when(j < early)
    def _():
        keys = bias_scr.shape[2]
        col_ok = lax.broadcasted_iota(jnp.int32, (1, keys), 1) >= first_valid_col
        for h in range(n):
            bias_scr[n + h] = jnp.where(col_ok, bias_scr[h], NEG)


def _head_logits(q, k, bias_scr, idx, sel):
    qm = jnp.where(sel, q, jnp.zeros_like(q)) * SCALE
    return qm, _dot_nt(qm, k) + bias_scr[idx]


def _row_sums_everywhere(r, sel):
    return jnp.where(sel, pltpu.roll(r, HEAD_DIM, 1), r)


def _own_everywhere(x, sel):
    return jnp.where(sel, x, pltpu.roll(x, HEAD_DIM, 1))


def _minus_rows(s, row_full):
    return jnp.concatenate([s[:, i:i + 128] - row_full for i in range(0, s.shape[1], 128)], axis=1)


def _attn_a_fwd(qkvg, bias, gather):
    t = qkvg.shape[0]
    nq = t // QBLK
    n_g = len(gather)
    q_spec, k_specs, v_specs, g_spec, bias_spec = _a_specs()

    def body(q_ref, k0, k1, k2, v0, v1, v2, g_ref, w_ref, *rest):
        shard_refs, rest = rest[:n_g], rest[n_g:]
        z_ref, o_ref, lse_ref = rest[:3]
        full_refs, (b_ref, *comm) = rest[3:3 + n_g], rest[3 + n_g:]
        p = pl.program_id(0)
        j = pl.program_id(1)
        start, forward, finish = _gather_phases(shard_refs, full_refs, *comm)
        pl.when(jnp.logical_and(p == 0, j == 0))(start)
        pl.when(jnp.logical_and(p == A_STEPS // 2, j == 0))(forward)
        _a_fill_bias(w_ref, b_ref, j)
        early = (j < 2).astype(jnp.int32)
        lane_lo = _lane_lo()
        for pp in range(A_PAIRS):
            cols = slice(128 * pp, 128 * (pp + 1))
            q = q_ref[:, cols]
            k = jnp.concatenate([k0[:, cols], k1[:, cols], k2[:, cols]], axis=0)
            v = jnp.concatenate([v0[:, cols], v1[:, cols], v2[:, cols]], axis=0)
            outs, lses = [], []
            for hh in range(2):
                sel = lane_lo if hh == 0 else jnp.logical_not(lane_lo)
                _, s = _head_logits(q, k, b_ref, 2 * pp + hh + 2 * A_PAIRS * early, sel)
                mx = jnp.max(s, axis=-1, keepdims=True)
                e = jnp.exp(s - mx).astype(BF16)
                r = _dot(e, jnp.where(sel, v, jnp.ones_like(v)))
                l = _row_sums_everywhere(r, sel)
                outs.append(r / l)
                lses.append(mx + jnp.log(l))
            o = jnp.where(lane_lo, outs[0], outs[1])
            silu, _ = _silu_parts(g_ref[:, cols].astype(F32))
            o_ref[:, cols] = o.astype(BF16)
            z_ref[:, cols] = (o * silu).astype(BF16)
            lse_ref[:, cols] = jnp.where(lane_lo, lses[0], lses[1])
        pl.when(jnp.logical_and(p == A_STEPS - 1, j == nq - 1))(finish)

    out_spec = pl.BlockSpec((QBLK, A_LANES), lambda p, j: (j, p))
    outs = pl.pallas_call(
        body, name="attn_a_fwd", grid=(A_STEPS, nq),
        in_specs=[q_spec, *k_specs, *v_specs, g_spec, bias_spec] + [ANY] * n_g,
        out_specs=[out_spec, out_spec, out_spec] + [ANY] * n_g,
        out_shape=[SDS((t, D_MODEL), BF16), SDS((t, D_MODEL), BF16), SDS((t, D_MODEL), F32)]
        + [SDS((N_DEV, *s.shape), s.dtype) for s in gather],
        scratch_shapes=[pltpu.VMEM((4 * A_PAIRS, QBLK, A_KEYS), F32)] + _gather_scratch(n_g),
        compiler_params=_cparams(),
    )(qkvg, qkvg, qkvg, qkvg, qkvg, qkvg, qkvg, qkvg, bias, *gather)
    return outs[0], outs[1], outs[2], list(outs[3:])


def _attn_a_bwd(qkvg, bias, out_a, lse, dz, scatter):
    t = qkvg.shape[0]
    nq = t // QBLK
    n_sc = len(scatter)
    q_spec, k_specs, v_specs, g_spec, bias_spec = _a_specs()

    def body(q_ref, k0, k1, k2, v0, v1, v2, g_ref, w_ref, o_ref, lse_ref, dz_ref, *rest):
        sc_refs, rest = rest[:n_sc], rest[n_sc:]
        dqg_ref, dkv_ref, dg_ref = rest[:3]
        land_refs, rest = rest[3:3 + n_sc], rest[3 + n_sc:]
        dk_acc, dv_acc, gt_acc, b_ref, send_sems, recv_sems = rest
        j = pl.program_id(1)
        first = jnp.logical_and(pl.program_id(0) == 0, j == 0)
        last = jnp.logical_and(pl.program_id(0) == A_STEPS - 1, j == nq - 1)

        @pl.when(first)
        def _():
            for cp in _scatter_copies(sc_refs, land_refs, send_sems, recv_sems):
                cp.start()

        _a_fill_bias(w_ref, b_ref, j)

        @pl.when(j == 0)
        def _():
            dk_acc[...] = jnp.zeros_like(dk_acc)
            dv_acc[...] = jnp.zeros_like(dv_acc)
            gt_acc[...] = jnp.zeros_like(gt_acc)

        early = (j < 2).astype(jnp.int32)
        lane_lo = _lane_lo()
        for pp in range(A_PAIRS):
            cols = slice(128 * pp, 128 * (pp + 1))
            q = q_ref[:, cols]
            k = jnp.concatenate([k0[:, cols], k1[:, cols], k2[:, cols]], axis=0)
            v = jnp.concatenate([v0[:, cols], v1[:, cols], v2[:, cols]], axis=0)
            o = o_ref[:, cols].astype(F32)
            lse_pair = lse_ref[:, cols]
            dzf = dz_ref[:, cols].astype(F32)
            silu, dsilu = _silu_parts(g_ref[:, cols].astype(F32))
            do = dzf * silu
            dqg_ref[1, :, cols] = (dzf * o * dsilu).astype(BF16)
            doo = do * o
            dqs = []
            dk_blk = jnp.zeros((A_KEYS, 128), F32)
            dv_blk = jnp.zeros((A_KEYS, 128), F32)
            for hh in range(2):
                sel = lane_lo if hh == 0 else jnp.logical_not(lane_lo)
                qm, s = _head_logits(q, k, b_ref, 2 * pp + hh + 2 * A_PAIRS * early, sel)
                p = jnp.exp(_minus_rows(s, _own_everywhere(lse_pair, sel)))
                delta = jnp.sum(jnp.where(sel, doo, 0.0), axis=-1, keepdims=True)
                dom = jnp.where(sel, do, 0.0).astype(BF16)
                dp = _dot_nt(dom, v)
                ds = p * (dp - delta)
                gt_acc[2 * pp + hh] += ds
                dsb = ds.astype(BF16)
                dqs.append(_dot(dsb, k) * SCALE)
                dk_blk = dk_blk + _dot_tn(dsb, qm)
                dv_blk = dv_blk + _dot_tn(p.astype(BF16), dom)
            dqg_ref[0, :, cols] = jnp.where(lane_lo, dqs[0], dqs[1]).astype(BF16)
            for b in range(3):
                @pl.when(j - 2 + b >= 0)
                def _(b=b, cols=cols, dk_blk=dk_blk, dv_blk=dv_blk):
                    rows = pl.ds(pl.multiple_of((j - 2 + b) * QBLK, QBLK), QBLK)
                    dk_acc[rows, cols] += dk_blk[b * QBLK:(b + 1) * QBLK]
                    dv_acc[rows, cols] += dv_blk[b * QBLK:(b + 1) * QBLK]

        @pl.when(j == nq - 1)
        def _():
            dkv_ref[0] = dk_acc[...].astype(BF16)
            dkv_ref[1] = dv_acc[...].astype(BF16)
            for pp in range(A_PAIRS):
                dg_ref[pp] = jnp.concatenate([_offset_sums(gt_acc[2 * pp]), _offset_sums(gt_acc[2 * pp + 1]),
                                              jnp.zeros((6, A_DIAG), F32)], axis=0)

        @pl.when(last)
        def _():
            for cp in _scatter_copies(sc_refs, land_refs, send_sems, recv_sems):
                cp.wait()

    blk = pl.BlockSpec((QBLK, A_LANES), lambda p, j: (j, p))
    outs = pl.pallas_call(
        body, name="attn_a_bwd", grid=(A_STEPS, nq),
        in_specs=[q_spec, *k_specs, *v_specs, g_spec, bias_spec, blk, blk, blk] + [ANY] * n_sc,
        out_specs=[pl.BlockSpec((2, QBLK, A_LANES), lambda p, j: (0, j, p)),
                   pl.BlockSpec((2, t, A_LANES), lambda p, j: (0, 0, p)),
                   pl.BlockSpec((A_PAIRS, 8, A_DIAG), lambda p, j: (p, 0, 0))] + [ANY] * n_sc,
        out_shape=[SDS((2, t, D_MODEL), BF16), SDS((2, t, D_MODEL), BF16), SDS((N_HEADS // 2, 8, A_DIAG), F32)]
        + [SDS((N_DEV - 1, *g.shape[1:]), g.dtype) for g in scatter],
        scratch_shapes=[pltpu.VMEM((t, A_LANES), F32), pltpu.VMEM((t, A_LANES), F32),
                        pltpu.VMEM((2 * A_PAIRS, QBLK, A_KEYS), F32), pltpu.VMEM((4 * A_PAIRS, QBLK, A_KEYS), F32),
                        pltpu.SemaphoreType.DMA(((N_DEV - 1) * n_sc,)),
                        pltpu.SemaphoreType.DMA(((N_DEV - 1) * n_sc,))],
        compiler_params=_cparams(),
    )(qkvg, qkvg, qkvg, qkvg, qkvg, qkvg, qkvg, qkvg, bias, out_a, lse, dz, *scatter)
    return outs[0], outs[1], outs[2], list(outs[3:])


def _b_specs():
    q = pl.BlockSpec((B_QBLK, 512), lambda h, j: (j, h))
    g = pl.BlockSpec((B_QBLK, 512), lambda h, j: (j, 2 + h))
    kp = pl.BlockSpec((128, 128), lambda h, j: (jnp.maximum(j - 1, 0), 0))
    kc = pl.BlockSpec((B_QBLK, 128), lambda h, j: (j, 0))
    vp = pl.BlockSpec((128, 128), lambda h, j: (jnp.maximum(j - 1, 0), 1))
    vc = pl.BlockSpec((B_QBLK, 128), lambda h, j: (j, 1))
    bias = pl.BlockSpec((B_GROUP, B_KEYS), lambda h, j: (h, 0))
    sinks = pl.BlockSpec(memory_space=pltpu.SMEM)
    return q, g, kp, kc, vp, vc, bias, sinks


def _b_operands(kp, kc, vp, vc, kvh):
    k = jnp.concatenate([kp[...], kc[...]], axis=0)
    v = jnp.concatenate([vp[...], vc[...]], axis=0)
    kr = pltpu.roll(k, HEAD_DIM, 1)
    vr = pltpu.roll(v, HEAD_DIM, 1)
    first = kvh == 0
    return (jnp.where(first, k, kr), jnp.where(first, kr, k),
            jnp.where(first, v, vr), jnp.where(first, vr, v))


def _attn_b_fwd(qg, kv, bias, sinks):
    t = qg.shape[0]
    q_spec, g_spec, kp_spec, kc_spec, vp_spec, vc_spec, bias_spec, sink_spec = _b_specs()

    def body(q_ref, g_ref, kp, kc, vp, vc, w_ref, sink_ref, z_ref, o_ref, lse_ref, b_ref):
        kvh = pl.program_id(0)
        j = pl.program_id(1)
        _fill_bias(B_GROUP, lambda h: w_ref[h:h + 1, :], B_BAND, 128, 1, b_ref, j)
        early = (j < 1).astype(jnp.int32)
        lane_lo = _lane_lo()
        k_lo, k_hi, v_lo, v_hi = _b_operands(kp, kc, vp, vc, kvh)
        for pp in range(B_GROUP // 2):
            cols = slice(128 * pp, 128 * (pp + 1))
            qp = q_ref[:, cols]
            outs, lses = [], []
            for hh in range(2):
                g = 2 * pp + hh
                sel = lane_lo if hh == 0 else jnp.logical_not(lane_lo)
                sink = sink_ref[kvh * B_GROUP + g]
                vv = v_lo if hh == 0 else v_hi
                _, s = _head_logits(qp, k_lo if hh == 0 else k_hi, b_ref, g + B_GROUP * early, sel)
                mx = jnp.maximum(jnp.max(s, axis=-1, keepdims=True), sink)
                e = jnp.exp(s - mx).astype(BF16)
                r = _dot(e, jnp.where(sel, vv, jnp.ones_like(vv)))
                l = _row_sums_everywhere(r, sel) + jnp.exp(sink - mx)
                outs.append(r / l)
                lses.append(mx + jnp.log(l))
            o = jnp.where(lane_lo, outs[0], outs[1])
            silu, _ = _silu_parts(g_ref[:, cols].astype(F32))
            o_ref[:, cols] = o.astype(BF16)
            z_ref[:, cols] = (o * silu).astype(BF16)
            lse_ref[:, cols] = jnp.where(lane_lo, lses[0], lses[1])

    out_spec = pl.BlockSpec((B_QBLK, 512), lambda h, j: (j, h))
    return pl.pallas_call(
        body, name="attn_b_fwd", grid=(B_KV_HEADS, t // B_QBLK),
        in_specs=[q_spec, g_spec, kp_spec, kc_spec, vp_spec, vc_spec, bias_spec, sink_spec],
        out_specs=[out_spec, out_spec, out_spec],
        out_shape=[SDS((t, D_MODEL), BF16), SDS((t, D_MODEL), BF16), SDS((t, D_MODEL), F32)],
        scratch_shapes=[pltpu.VMEM((2 * B_GROUP, B_QBLK, B_KEYS), F32)],
        compiler_params=_cparams(),
    )(qg, qg, kv, kv, kv, kv, bias, sinks)


def _attn_b_bwd(qg, kv, bias, sinks, out_b, lse, dz, bucket_onehot):
    t = qg.shape[0]
    nq = t // B_QBLK
    q_spec, g_spec, kp_spec, kc_spec, vp_spec, vc_spec, bias_spec, sink_spec = _b_specs()

    def body(q_ref, g_ref, kp, kc, vp, vc, w_ref, sink_ref, o_ref, lse_ref, dz_ref, oh_ref,
             dqg_ref, dkv_ref, dt5_ref, dsink_ref, gt_acc, b_ref):
        kvh = pl.program_id(0)
        j = pl.program_id(1)
        _fill_bias(B_GROUP, lambda h: w_ref[h:h + 1, :], B_BAND, 128, 1, b_ref, j)

        @pl.when(jnp.logical_and(kvh == 0, j == 0))
        def _():
            dkv_ref[...] = jnp.zeros_like(dkv_ref)

        @pl.when(j == 0)
        def _():
            gt_acc[...] = jnp.zeros_like(gt_acc)
            dsink_ref[...] = jnp.zeros_like(dsink_ref)

        early = (j < 1).astype(jnp.int32)
        lane_lo = _lane_lo()
        k_lo, k_hi, v_lo, v_hi = _b_operands(kp, kc, vp, vc, kvh)
        dk_blk = jnp.zeros((B_KEYS, 128), F32)
        dv_blk = jnp.zeros((B_KEYS, 128), F32)
        for pp in range(B_GROUP // 2):
            cols = slice(128 * pp, 128 * (pp + 1))
            qp = q_ref[:, cols]
            o = o_ref[:, cols].astype(F32)
            lse_pair = lse_ref[:, cols]
            dzf = dz_ref[:, cols].astype(F32)
            silu, dsilu = _silu_parts(g_ref[:, cols].astype(F32))
            do = dzf * silu
            dqg_ref[1, :, cols] = (dzf * o * dsilu).astype(BF16)
            doo = do * o
            dqs = []
            for hh in range(2):
                g = 2 * pp + hh
                sel = lane_lo if hh == 0 else jnp.logical_not(lane_lo)
                sink = sink_ref[kvh * B_GROUP + g]
                kk = k_lo if hh == 0 else k_hi
                vv = v_lo if hh == 0 else v_hi
                qm, s = _head_logits(qp, kk, b_ref, g + B_GROUP * early, sel)
                lse_h = _own_everywhere(lse_pair, sel)
                p = jnp.exp(_minus_rows(s, lse_h))
                delta = jnp.sum(jnp.where(sel, doo, 0.0), axis=-1, keepdims=True)
                dom = jnp.where(sel, do, 0.0).astype(BF16)
                dp = _dot_nt(dom, vv)
                ds = p * (dp - delta)
                gt_acc[g] += ds
                dsink_ref[g:g + 1, :] -= jnp.sum(jnp.exp(sink - lse_h) * delta, axis=0, keepdims=True)
                dsb = ds.astype(BF16)
                dqs.append(_dot(dsb, kk) * SCALE)
                dk_blk = dk_blk + _dot_tn(dsb, qm)
                dv_blk = dv_blk + _dot_tn(p.astype(BF16), dom)
            dqg_ref[0, :, cols] = jnp.where(lane_lo, dqs[0], dqs[1]).astype(BF16)
        mine = lane_lo == (kvh == 0)
        dk_add = jnp.where(mine, dk_blk + pltpu.roll(dk_blk, HEAD_DIM, 1), 0.0)
        dv_add = jnp.where(mine, dv_blk + pltpu.roll(dv_blk, HEAD_DIM, 1), 0.0)

        @pl.when(j >= 1)
        def _():
            rows = pl.ds(pl.multiple_of((j - 1) * 128, 128), 128)
            dkv_ref[0, rows, :] += dk_add[0:128]
            dkv_ref[1, rows, :] += dv_add[0:128]

        rows = pl.ds(pl.multiple_of(j * B_QBLK, B_QBLK), B_QBLK)
        dkv_ref[0, rows, :] += dk_add[128:B_KEYS]
        dkv_ref[1, rows, :] += dv_add[128:B_KEYS]

        @pl.when(j == nq - 1)
        def _():
            dd = jnp.concatenate([_offset_sums(gt_acc[g]) for g in range(B_GROUP)], axis=0)
            hi = dd.astype(BF16)
            lo = (dd - hi.astype(F32)).astype(BF16)
            dt5_ref[...] = _dot(hi, oh_ref[...]) + _dot(lo, oh_ref[...])

    blk = pl.BlockSpec((B_QBLK, 512), lambda h, j: (j, h))
    return pl.pallas_call(
        body, name="attn_b_bwd", grid=(B_KV_HEADS, nq),
        in_specs=[q_spec, g_spec, kp_spec, kc_spec, vp_spec, vc_spec, bias_spec, sink_spec, blk, blk, blk,
                  pl.BlockSpec((B_DIAG, 128), lambda h, j: (0, 0))],
        out_specs=[pl.BlockSpec((2, B_QBLK, 512), lambda h, j: (0, j, h)),
                   pl.BlockSpec((2, t, 128), lambda h, j: (0, 0, 0)),
                   pl.BlockSpec((B_GROUP, 128), lambda h, j: (h, 0)),
                   pl.BlockSpec((B_GROUP, 128), lambda h, j: (h, 0))],
        out_shape=[SDS((2, t, D_MODEL), BF16), SDS((2, t, 128), F32),
                   SDS((N_HEADS, 128), F32), SDS((N_HEADS, 128), F32)],
        scratch_shapes=[pltpu.VMEM((B_GROUP, B_QBLK, B_KEYS), F32), pltpu.VMEM((2 * B_GROUP, B_QBLK, B_KEYS), F32)],
        compiler_params=_cparams(),
    )(qg, qg, kv, kv, kv, kv, bias, sinks, out_b, lse, dz, bucket_onehot)


def _a_bias_by_offset(rel_bias):
    m = np.arange(A_DIAG)
    idx = np.clip(A_BAND - 1 - m, -A_REL_CLIP, A_REL_CLIP) + A_REL_CLIP
    by_head = rel_bias[idx].T.reshape(N_HEADS // 2, 2, A_DIAG)
    return jnp.concatenate([by_head, jnp.zeros((N_HEADS // 2, 6, A_DIAG), F32)], axis=1)


def _a_bias_grad(offset_sums):
    first = 319
    tail = jnp.sum(offset_sums[:, :first], axis=1)
    body = jnp.flip(offset_sums[:, first:first + 320], axis=1)
    body = body.at[:, -1].add(tail)
    full = jnp.concatenate([jnp.zeros((N_HEADS, 193), F32), body], axis=1)
    return full.T


def _t5_bucket(rel):
    nb = T5_BUCKETS // 2
    max_exact = nb // 2
    ret = jnp.where(rel > 0, nb, 0)
    n = jnp.abs(rel)
    nf = jnp.maximum(n, 1).astype(jnp.float32)
    large = max_exact + (jnp.log(nf / max_exact) / math.log(T5_MAX_DIST / max_exact)
                         * (nb - max_exact)).astype(jnp.int32)
    large = jnp.minimum(large, nb - 1)
    return ret + jnp.where(n < max_exact, n, large)


def _b_offset_buckets():
    return _t5_bucket(jnp.arange(B_DIAG, dtype=jnp.int32) - (B_LEFT_CHUNKS * CHUNK + CHUNK - 1))


def _b_bias_by_offset(t5_table):
    return t5_table[_b_offset_buckets()].T


def _b_bucket_onehot():
    return (_b_offset_buckets()[:, None] == jnp.arange(128)[None, :]).astype(BF16)


def _local_step(my_slot, x, target, a_gain, w_in_a, rel_bias, late_shards, kv_gain, t5_table,
                b_gain, sinks, f_gain):
    a_bias = _a_bias_by_offset(rel_bias)
    b_bias = _b_bias_by_offset(t5_table)
    sinks_flat = sinks.reshape(N_HEADS)

    xn, qkvg = _norm_matmul(x, a_gain, w_in_a)
    z_a, out_a, lse_a, (w_in_b, w_out_a, w_out_b, kv_w) = _attn_a_fwd(qkvg, a_bias, late_shards)
    w_out_a = w_out_a.reshape(D_MODEL, D_MODEL)
    w_out_b = w_out_b.reshape(D_MODEL, D_MODEL)
    kv_w = kv_w.reshape(D_MODEL, 2 * 128)
    h1, kvn, hb, kv, qg = _layer_a_out(x, z_a, w_out_a, kv_gain, b_gain, kv_w, w_in_b)
    z_b, out_b, lse_b = _attn_b_fwd(qg, kv, b_bias, sinks_flat)
    dh2, dh2b, dz_b, loss, d_fn = _layer_b_out_loss(h1, z_b, w_out_b, f_gain, target)

    dqg_b, dkv_b, d_t5, d_sink = _attn_b_bwd(qg, kv, b_bias, sinks_flat, out_b, lse_b, dz_b, _b_bucket_onehot())
    dh1, dh1b, dz_a, d_bn, d_kn = _layer_b_in_bwd(dqg_b, dkv_b, w_in_b, kv_w, h1, dh2, b_gain, kv_gain, w_out_a)
    early = dict(
        b_w_out=_weight_grad_rows("grad_b_w_out", my_slot, z_b, dh2b[None]),
        b_w_in=_weight_grad_cols("grad_b_w_in", my_slot, hb, [dqg_b],
                                 [(0, o, c, 4 * o + c) for o in range(2) for c in range(4)], 256),
        kv_w=_weight_grad_rows("grad_kv_w", my_slot, kvn, dkv_b),
        a_w_out=_weight_grad_rows("grad_a_w_out", my_slot, z_a, dh1b[None]))
    dqg_a, dkv_a, d_rel, landed = _attn_a_bwd(qkvg, a_bias, out_a, lse_a, dz_a, [g[0] for g in early.values()])
    g_w_in_a = _weight_grad_cols(
        "grad_a_w_in", my_slot, xn, [dqg_a, dkv_a],
        [(0, 0, 0, 0), (0, 0, 1, 1), (1, 0, 0, 2), (1, 0, 1, 3), (1, 1, 0, 4), (1, 1, 1, 5), (0, 1, 0, 6), (0, 1, 1, 7)], 512)
    from_sibling, = _exchange_sibling([g_w_in_a[0]])
    x_i, y_i, c_i, chips = _place()
    del x_i, y_i
    forward_slots = jnp.stack([_slot(*chip, c_i) for chip in chips]).astype(jnp.int32)
    chip_sums = _pre_reduce("chip_sum_a_w_in", g_w_in_a[0], from_sibling, forward_slots)
    grad_x, d_an, from_chips = _layer_a_in_bwd(dqg_a, dkv_a, w_in_a, x, dh1, a_gain, chip_sums)

    matrices = {n: (g[1], [(land, 0, N_DEV - 1)]) for (n, g), land in zip(early.items(), landed)}
    matrices["a_w_in"] = (g_w_in_a[1], [(from_sibling, 3, 1), (from_chips, 0, 3)])
    small = dict(
        loss=loss[0, 0], a_norm=d_an, a_rel_bias=_a_bias_grad(d_rel[:, :2].reshape(N_HEADS, A_DIAG)),
        kv_norm=d_kn, t5_bias=d_t5[:, :T5_BUCKETS].T, b_norm=d_bn,
        b_sinks=d_sink[:, 0].reshape(1, N_HEADS), final_norm=d_fn)
    return grad_x, small, matrices


def _place():
    x, y, c = lax.axis_index("x"), lax.axis_index("y"), lax.axis_index("c")
    chips = [(1 - x, y), (x, 1 - y), (1 - x, 1 - y)]
    return x, y, c, chips


def _slot(px, py, pc):
    return 4 * px + 2 * py + pc


ANY = pl.BlockSpec(memory_space=pl.ANY)


def _peer(x, y, c, k):
    return (x ^ (k >> 2), y ^ ((k >> 1) & 1), c ^ (k & 1))


def _scatter_copies(grad_refs, land_refs, send_sems, recv_sems):
    x, y, c, _ = _place()
    copies = []
    for t, (grad, land) in enumerate(zip(grad_refs, land_refs)):
        for k in range(1, N_DEV):
            peer = _peer(x, y, c, k)
            sem = (N_DEV - 1) * t + k - 1
            copies.append(pltpu.make_async_remote_copy(
                src_ref=grad.at[_slot(*peer)], dst_ref=land.at[k - 1],
                send_sem=send_sems.at[sem], recv_sem=recv_sems.at[sem],
                device_id=peer, device_id_type=MESH))
    return copies


def _gather_phases(ins, outs, send_sems, recv_sems, local_sems):
    n = len(ins)
    x, y, c, chips = _place()
    me, sibling = (x, y, c), (x, y, 1 - c)

    def copy(t, k, block, to, src=None):
        dst = outs[t].at[_slot(*block)]
        return pltpu.make_async_remote_copy(
            src_ref=dst if src is None else src, dst_ref=dst,
            send_sem=send_sems.at[7 * t + k], recv_sem=recv_sems.at[7 * t + k],
            device_id=to, device_id_type=MESH)

    def lists():
        mine = [pltpu.make_async_copy(ins[t], outs[t].at[_slot(*me)], local_sems.at[t]) for t in range(n)]
        first = []
        for t in range(n):
            first.append(copy(t, 0, me, sibling, src=ins[t]))
            first += [copy(t, 1 + j, me, (*chip, c), src=ins[t]) for j, chip in enumerate(chips)]
        passed = [copy(t, 4 + j, (*chip, c), sibling) for t in range(n) for j, chip in enumerate(chips)]
        return mine, first, passed

    def start():
        mine, first, _ = lists()
        for cp in mine + first:
            cp.start()

    def forward():
        _, _, passed = lists()
        for t in range(n):
            for j, chip in enumerate(chips):
                copy(t, 1 + j, (*chip, c), me).wait_recv()
                passed[3 * t + j].start()

    def finish():
        mine, first, passed = lists()
        for t in range(n):
            copy(t, 0, sibling, me).wait_recv()
            for j, chip in enumerate(chips):
                copy(t, 4 + j, (*chip, 1 - c), me).wait_recv()
        for cp in first + passed:
            cp.wait_send()
        for cp in mine:
            cp.wait()

    return start, forward, finish


def _gather_scratch(n):
    return [pltpu.SemaphoreType.DMA((7 * n,)), pltpu.SemaphoreType.DMA((7 * n,)), pltpu.SemaphoreType.DMA((n,))]


def _all_gather(shards):
    n = len(shards)

    def body(*refs):
        start, forward, finish = _gather_phases(refs[:n], refs[n:2 * n], *refs[2 * n:])
        start()
        forward()
        finish()

    return pl.pallas_call(
        body, name="all_gather_weights",
        in_specs=[ANY] * n, out_specs=[ANY] * n,
        out_shape=[SDS((N_DEV, *s.shape), s.dtype) for s in shards],
        scratch_shapes=_gather_scratch(n),
    )(*shards)


def _exchange_sibling(grads):
    n = len(grads)

    def body(*refs):
        ins, outs = refs[:n], refs[n:2 * n]
        send_sems, recv_sems = refs[2 * n:]
        x, y, c, chips = _place()
        sibling = (x, y, 1 - c)
        copies = []
        for t in range(n):
            blocks = [(*chip, 1 - c) for chip in chips] + [sibling]
            for k, block in enumerate(blocks):
                copies.append(pltpu.make_async_remote_copy(
                    src_ref=ins[t].at[_slot(*block)], dst_ref=outs[t].at[k],
                    send_sem=send_sems.at[4 * t + k], recv_sem=recv_sems.at[4 * t + k],
                    device_id=sibling, device_id_type=MESH))
        for cp in copies:
            cp.start()
        for cp in copies:
            cp.wait()

    return pl.pallas_call(
        body, name="grads_to_sibling",
        in_specs=[ANY] * n, out_specs=[ANY] * n,
        out_shape=[SDS((4, *g.shape[1:]), g.dtype) for g in grads],
        scratch_shapes=[pltpu.SemaphoreType.DMA((4 * n,)), pltpu.SemaphoreType.DMA((4 * n,))],
    )(*grads)


def _chip_copies(sums_ref, land_ref, send_sems, recv_sems):
    x, y, c, chips = _place()
    del x, y
    return [pltpu.make_async_remote_copy(
        src_ref=sums_ref.at[j], dst_ref=land_ref.at[j], send_sem=send_sems.at[j], recv_sem=recv_sems.at[j],
        device_id=(*chip, c), device_id_type=MESH) for j, chip in enumerate(chips)]


def _row_tile(rows):
    return min(rows, 256)


def _pre_reduce(name, g, from_sibling, slots):
    _, r, c = g.shape
    tr = _row_tile(r)

    def body(slots_ref, g_ref, s_ref, o_ref):
        del slots_ref
        o_ref[...] = (g_ref[...].astype(F32) + s_ref[...].astype(F32)).astype(BF16)

    return pl.pallas_call(
        body, name=name,
        grid_spec=pltpu.PrefetchScalarGridSpec(
            num_scalar_prefetch=1, grid=(3, r // tr),
            in_specs=[pl.BlockSpec((1, tr, c), lambda j, i, s: (s[j], i, 0)),
                      pl.BlockSpec((1, tr, c), lambda j, i, s: (j, i, 0))],
            out_specs=pl.BlockSpec((1, tr, c), lambda j, i, s: (j, i, 0))),
        out_shape=SDS((3, r, c), BF16),
        compiler_params=_cparams(),
    )(slots, g, from_sibling)


def _adamw(w, g, m, v):
    m2 = ADAM_B1 * m + (1.0 - ADAM_B1) * g
    v2 = ADAM_B2 * v + (1.0 - ADAM_B2) * jnp.square(g)
    m_hat = m2 / (1.0 - ADAM_B1 ** ADAM_STEP)
    v_hat = v2 / (1.0 - ADAM_B2 ** ADAM_STEP)
    delta = -ADAM_LR * (m_hat / (jnp.sqrt(v_hat) + ADAM_EPS) + ADAM_WD * w)
    return delta, m2, v2


def _reduce_adamw(name, own, partials, w, m, v):
    r, c = own.shape
    tr = _row_tile(r)
    n_p = len(partials)

    def body(own_ref, *rest):
        p_refs, (w_ref, m_ref, v_ref, grad_ref, d_ref, nm_ref, nv_ref) = rest[:n_p], rest[n_p:]
        grad = own_ref[...]
        for p_ref, (_, _, count) in zip(p_refs, partials):
            for j in range(count):
                grad = grad + p_ref[j].astype(F32)
        grad_ref[...] = grad
        d_ref[...], nm_ref[...], nv_ref[...] = _adamw(w_ref[...], grad, m_ref[...], v_ref[...])

    flat = pl.BlockSpec((tr, c), lambda i: (i, 0))
    return pl.pallas_call(
        body, name=name, grid=(r // tr,),
        in_specs=[flat] + [pl.BlockSpec((count, tr, c), lambda i, first=first, count=count: (first // count, i, 0))
                           for _, first, count in partials] + [flat, flat, flat],
        out_specs=[flat, flat, flat, flat],
        out_shape=[SDS((r, c), F32)] * 4,
        compiler_params=_cparams(),
    )(own, *[p[0] for p in partials], w, m, v)


_SMALL = (("a_norm", 8), ("a_rel_bias", 72), ("kv_norm", 8), ("t5_bias", 8), ("b_norm", 8),
          ("b_sinks", 8), ("final_norm", 8), ("loss", 8))
_SMALL_ROWS = sum(r for _, r in _SMALL)


def _pack_small(parts):
    rows = []
    for name, n_rows in _SMALL:
        flat = parts[name].reshape(-1).astype(F32)
        rows.append(jnp.pad(flat, (0, n_rows * 128 - flat.shape[0])).reshape(n_rows, 128))
    return jnp.concatenate(rows, axis=0)


def _unpack_small(buf, shapes):
    out, at = {}, 0
    for name, n_rows in _SMALL:
        size = int(np.prod(shapes[name])) if shapes[name] else 1
        out[name] = buf[at:at + n_rows].reshape(-1)[:size].reshape(shapes[name])
        at += n_rows
    return out


def _small_allreduce_adamw(gbuf, wbuf, mbuf, vbuf):
    def body(g_ref, w_ref, m_ref, v_ref, sum_ref, d_ref, nm_ref, nv_ref, land_ref, send_sems, recv_sems):
        x, y, c, _ = _place()
        my_slot = _slot(x, y, c)
        land_ref[my_slot] = g_ref[...]
        copies = []
        for k in range(1, N_DEV):
            peer = (x ^ (k >> 2), y ^ ((k >> 1) & 1), c ^ (k & 1))
            copies.append(pltpu.make_async_remote_copy(
                src_ref=g_ref, dst_ref=land_ref.at[my_slot],
                send_sem=send_sems.at[k - 1], recv_sem=recv_sems.at[k - 1],
                device_id=peer, device_id_type=MESH))
        for cp in copies:
            cp.start()
        for k in range(1, N_DEV):
            peer_slot = _slot(x ^ (k >> 2), y ^ ((k >> 1) & 1), c ^ (k & 1))
            pltpu.make_async_remote_copy(
                src_ref=g_ref, dst_ref=land_ref.at[peer_slot],
                send_sem=send_sems.at[k - 1], recv_sem=recv_sems.at[k - 1],
                device_id=(x, y, c), device_id_type=MESH).wait_recv()
        for cp in copies:
            cp.wait_send()
        total = land_ref[0]
        for s in range(1, N_DEV):
            total = total + land_ref[s]
        sum_ref[...] = total
        d_ref[...], nm_ref[...], nv_ref[...] = _adamw(w_ref[...], total, m_ref[...], v_ref[...])

    vm = pl.BlockSpec(memory_space=pltpu.VMEM)
    shape = SDS((_SMALL_ROWS, 128), F32)
    return pl.pallas_call(
        body, name="small_allreduce_adamw",
        in_specs=[vm] * 4, out_specs=[vm] * 4, out_shape=[shape] * 4,
        scratch_shapes=[pltpu.VMEM((N_DEV, _SMALL_ROWS, 128), F32),
                        pltpu.SemaphoreType.DMA((N_DEV - 1,)), pltpu.SemaphoreType.DMA((N_DEV - 1,))],
    )(gbuf, wbuf, mbuf, vbuf)


def kernel(x, a_norm, a_w_in, a_rel_bias, a_w_out, kv_norm, kv_w, t5_bias, b_norm, b_w_in, b_sinks, b_w_out, final_norm, loss_target, m_a_norm, m_a_w_in, m_a_rel_bias, m_a_w_out, m_kv_norm, m_kv_w, m_t5_bias, m_b_norm, m_b_w_in, m_b_sinks, m_b_w_out, m_final_norm, v_a_norm, v_a_w_in, v_a_rel_bias, v_a_w_out, v_kv_norm, v_kv_w, v_t5_bias, v_b_norm, v_b_w_in, v_b_sinks, v_b_w_out, v_final_norm):
    xi, yi, ci = lax.axis_index("x"), lax.axis_index("y"), lax.axis_index("c")
    my_slot = _slot(xi, yi, ci)

    w_in_a, a_gain = _all_gather([a_w_in[0].astype(BF16), a_norm])
    a_gain = a_gain.reshape(1, D_MODEL)

    slot_arr = jnp.reshape(my_slot, (1,)).astype(jnp.int32)
    late_shards = [b_w_in[0].astype(BF16), a_w_out[0].astype(BF16), b_w_out[0].astype(BF16), kv_w.astype(BF16)]
    grad_x, loc, matrices = _local_step(
        slot_arr, x[0], loss_target[0], a_gain, w_in_a, a_rel_bias[0], late_shards,
        kv_norm.reshape(1, D_MODEL), t5_bias, b_norm, b_sinks, final_norm.reshape(1, D_MODEL))

    shard_w = dict(a_w_in=a_w_in[0], b_w_in=b_w_in[0], a_w_out=a_w_out[0], b_w_out=b_w_out[0], kv_w=kv_w)
    shard_m = dict(a_w_in=m_a_w_in[0], b_w_in=m_b_w_in[0], a_w_out=m_a_w_out[0], b_w_out=m_b_w_out[0], kv_w=m_kv_w)
    shard_v = dict(a_w_in=v_a_w_in[0], b_w_in=v_b_w_in[0], a_w_out=v_a_w_out[0], b_w_out=v_b_w_out[0], kv_w=v_kv_w)
    big = {n: _reduce_adamw("adamw_" + n, own, partials, shard_w[n], shard_m[n], shard_v[n])
           for n, (own, partials) in matrices.items()}

    def own_row(vec):
        return lax.dynamic_update_slice(jnp.zeros((N_DEV, 128), F32), vec, (my_slot, 0))

    zero = jnp.zeros((), F32)
    small_w = dict(a_norm=own_row(a_norm), a_rel_bias=a_rel_bias, kv_norm=kv_norm, t5_bias=t5_bias,
                   b_norm=b_norm, b_sinks=b_sinks, final_norm=final_norm, loss=zero)
    small_m = dict(a_norm=own_row(m_a_norm), a_rel_bias=m_a_rel_bias, kv_norm=m_kv_norm, t5_bias=m_t5_bias,
                   b_norm=m_b_norm, b_sinks=m_b_sinks, final_norm=m_final_norm, loss=zero)
    small_v = dict(a_norm=own_row(v_a_norm), a_rel_bias=v_a_rel_bias, kv_norm=v_kv_norm, t5_bias=v_t5_bias,
                   b_norm=v_b_norm, b_sinks=v_b_sinks, final_norm=v_final_norm, loss=zero)
    small = _small_allreduce_adamw(_pack_small(loc), _pack_small(small_w), _pack_small(small_m), _pack_small(small_v))
    shapes = dict(a_norm=(N_DEV, 128), a_rel_bias=a_rel_bias.shape, kv_norm=kv_norm.shape, t5_bias=t5_bias.shape,
                  b_norm=b_norm.shape, b_sinks=b_sinks.shape, final_norm=final_norm.shape, loss=())
    sm = [_unpack_small(buf, shapes) for buf in small]
    for part in sm:
        part["a_norm"] = lax.dynamic_slice(part["a_norm"], (my_slot, 0), (1, 128))

    order = ("a_norm", "a_w_in", "a_rel_bias", "a_w_out", "kv_norm", "kv_w", "t5_bias", "b_norm",
             "b_w_in", "b_sinks", "b_w_out", "final_norm")
    lead = dict(a_w_in=True, b_w_in=True, a_w_out=True, b_w_out=True, kv_w=False)

    def pick(kind, name):
        if name in big:
            val = big[name][kind]
            return val[None] if lead[name] else val
        return sm[kind][name]

    outs = [sm[0]["loss"], grad_x[None]]
    for kind in range(4):
        outs += [pick(kind, n) for n in order]
    return tuple(outs)
```

```python
import functools
import math

import numpy as np
import jax
import jax.numpy as jnp
from jax import lax
from jax.experimental import pallas as pl
from jax.experimental.pallas import tpu as pltpu

F32 = jnp.float32
BF16 = jnp.bfloat16
SDS = jax.ShapeDtypeStruct

D_MODEL = 1024
HEAD_DIM = 64
CHUNK = 64
N_HEADS = 16
RMS_EPS = 1e-6
A_LEFT_CHUNKS = 8
A_BAND = (A_LEFT_CHUNKS + 1) * CHUNK
A_REL_CLIP = 256
B_KV_HEADS = 2
B_GROUP = 8
B_LEFT_CHUNKS = 2
B_BAND = (B_LEFT_CHUNKS + 1) * CHUNK
T5_BUCKETS = 32
T5_MAX_DIST = 128
QBLK = 256
A_KEYS = 3 * QBLK
B_QBLK_FWD = 128
B_QBLK_BWD = 256
B_PREV = 128
A_DIAG = A_KEYS
NEG = -1e30
SCALE = HEAD_DIM ** -0.5
N_DEV = 8

ADAM_LR = 0.001
ADAM_B1 = 0.9
ADAM_B2 = 0.999
ADAM_EPS = 1e-08
ADAM_WD = 0.01
ADAM_STEP = 10

VMEM_LIMIT_BYTES = 56 * 1024 * 1024
MESH = pl.DeviceIdType.MESH


def _cparams():
    return pltpu.CompilerParams(vmem_limit_bytes=VMEM_LIMIT_BYTES)


def _dot(a, b):
    return jnp.dot(a, b, preferred_element_type=F32)


def _dot_nt(a, b):
    return lax.dot_general(a, b, (((1,), (1,)), ((), ())), preferred_element_type=F32)


def _dot_tn(a, b):
    return lax.dot_general(a, b, (((0,), (0,)), ((), ())), preferred_element_type=F32)


def _rstd(xf):
    return lax.rsqrt(jnp.mean(xf * xf, axis=-1, keepdims=True) + RMS_EPS)


def _sigmoid(x):
    return 1.0 / (1.0 + jnp.exp(-x))


def _norm_matmul(x, gain, w):
    t = x.shape[0]
    nb, _, tn = w.shape
    tm = min(t, 1024)

    def body(x_ref, g_ref, w_ref, xn_ref, o_ref):
        @pl.when(pl.program_id(1) == 0)
        def _():
            xf = x_ref[...]
            xn_ref[...] = ((xf * _rstd(xf)) * g_ref[...]).astype(BF16)

        o_ref[...] = _dot(xn_ref[...], w_ref[0]).astype(BF16)

    return pl.pallas_call(
        body, name="norm_matmul", grid=(t // tm, nb),
        in_specs=[pl.BlockSpec((tm, D_MODEL), lambda m, n: (m, 0)),
                  pl.BlockSpec((1, D_MODEL), lambda m, n: (0, 0)),
                  pl.BlockSpec((1, D_MODEL, tn), lambda m, n: (n, 0, 0))],
        out_specs=[pl.BlockSpec((tm, D_MODEL), lambda m, n: (m, 0)),
                   pl.BlockSpec((tm, tn), lambda m, n: (m, n))],
        out_shape=[SDS((t, D_MODEL), BF16), SDS((t, nb * tn), BF16)],
        compiler_params=_cparams(),
    )(x, gain, w)


def _layer_a_out(x, z, w_out, kv_gain, b_gain, kv_w, w_in_b):
    t = x.shape[0]
    tm = min(t, 512)
    nb, _, tn = w_in_b.shape

    def body(x_ref, z_ref, wo_ref, kvg_ref, bg_ref, kvw_ref, wb_ref,
             h1_ref, kvn_ref, hb_ref, kv_ref, qg_ref):
        h1 = x_ref[...] + _dot(z_ref[...], wo_ref[...])
        h1_ref[...] = h1
        y0 = h1 * _rstd(h1)
        kvn = (y0 * kvg_ref[...]).astype(BF16)
        hb = (y0 * bg_ref[...]).astype(BF16)
        kvn_ref[...] = kvn
        hb_ref[...] = hb
        kv_ref[...] = _dot(kvn, kvw_ref[...]).astype(BF16)
        for i in range(nb):
            qg_ref[:, i * tn:(i + 1) * tn] = _dot(hb, wb_ref[i]).astype(BF16)

    row = lambda m: (m, 0)
    fix2 = lambda m: (0, 0)
    return pl.pallas_call(
        body, name="layer_a_out", grid=(t // tm,),
        in_specs=[pl.BlockSpec((tm, D_MODEL), row), pl.BlockSpec((tm, D_MODEL), row),
                  pl.BlockSpec((D_MODEL, D_MODEL), fix2),
                  pl.BlockSpec((1, D_MODEL), fix2), pl.BlockSpec((1, D_MODEL), fix2),
                  pl.BlockSpec((D_MODEL, 256), fix2),
                  pl.BlockSpec((nb, D_MODEL, tn), lambda m: (0, 0, 0))],
        out_specs=[pl.BlockSpec((tm, D_MODEL), row), pl.BlockSpec((tm, D_MODEL), row),
                   pl.BlockSpec((tm, D_MODEL), row), pl.BlockSpec((tm, 256), row),
                   pl.BlockSpec((tm, nb * tn), row)],
        out_shape=[SDS((t, D_MODEL), F32), SDS((t, D_MODEL), BF16), SDS((t, D_MODEL), BF16),
                   SDS((t, 256), BF16), SDS((t, nb * tn), BF16)],
        compiler_params=_cparams(),
    )(x, z, w_out, kv_gain, b_gain, kv_w, w_in_b)


def _layer_b_out_loss(h1, z, w_out, f_gain, target):
    t = h1.shape[0]
    tm = min(t, 512)

    def body(h1_ref, z_ref, wo_ref, fg_ref, tgt_ref,
             dh2_ref, dh2b_ref, dz_ref, loss_ref, dfn_ref):
        @pl.when(pl.program_id(0) == 0)
        def _():
            loss_ref[...] = jnp.zeros_like(loss_ref)
            dfn_ref[...] = jnp.zeros_like(dfn_ref)

        h2 = h1_ref[...] + _dot(z_ref[...], wo_ref[...])
        r = _rstd(h2)
        yn = h2 * r
        fg = fg_ref[...]
        err = yn * fg - tgt_ref[...]
        loss_ref[...] += (0.5 / D_MODEL) * jnp.sum(err * err)
        dy = err * (1.0 / D_MODEL)
        dfn_ref[...] += jnp.sum(dy * yn, axis=0, keepdims=True)
        u = dy * fg
        dh2 = r * u - h2 * ((r * r * r) * jnp.mean(u * h2, axis=-1, keepdims=True))
        dh2_ref[...] = dh2
        dh2b = dh2.astype(BF16)
        dh2b_ref[...] = dh2b
        dz_ref[...] = _dot_nt(dh2b, wo_ref[...]).astype(BF16)

    row = lambda m: (m, 0)
    fix2 = lambda m: (0, 0)
    return pl.pallas_call(
        body, name="layer_b_out_loss", grid=(t // tm,),
        in_specs=[pl.BlockSpec((tm, D_MODEL), row), pl.BlockSpec((tm, D_MODEL), row),
                  pl.BlockSpec((D_MODEL, D_MODEL), fix2), pl.BlockSpec((1, D_MODEL), fix2),
                  pl.BlockSpec((tm, D_MODEL), row)],
        out_specs=[pl.BlockSpec((tm, D_MODEL), row), pl.BlockSpec((tm, D_MODEL), row),
                   pl.BlockSpec((tm, D_MODEL), row), pl.BlockSpec((1, 128), fix2),
                   pl.BlockSpec((1, D_MODEL), fix2)],
        out_shape=[SDS((t, D_MODEL), F32), SDS((t, D_MODEL), BF16), SDS((t, D_MODEL), BF16),
                   SDS((1, 128), F32), SDS((1, D_MODEL), F32)],
        compiler_params=_cparams(),
    )(h1, z, w_out, f_gain, target)


def _layer_b_in_bwd(dqg, dkv, w_in_b, kv_w, h1, dh2, b_gain, kv_gain, w_out_a):
    t = h1.shape[0]
    tm = min(t, 256)
    nb, _, tn = w_in_b.shape
    per = D_MODEL // tn

    def body(dqg_ref, dkv_ref, wb_ref, kvw_ref, h1_ref, dh2_ref, bg_ref, kvg_ref, wo_ref,
             dh1_ref, dh1b_ref, dz_ref, dbn_ref, dkn_ref):
        @pl.when(pl.program_id(0) == 0)
        def _():
            dbn_ref[...] = jnp.zeros_like(dbn_ref)
            dkn_ref[...] = jnp.zeros_like(dkn_ref)

        dhb = jnp.zeros((tm, D_MODEL), F32)
        for i in range(nb):
            blk = dqg_ref[i // per, :, (i % per) * tn:(i % per + 1) * tn]
            dhb = dhb + _dot_nt(blk, wb_ref[i])
        dkn = (_dot_nt(dkv_ref[0].astype(BF16), kvw_ref[:, 0:128])
               + _dot_nt(dkv_ref[1].astype(BF16), kvw_ref[:, 128:256]))
        h1 = h1_ref[...]
        r = _rstd(h1)
        xr = h1 * r
        dbn_ref[...] += jnp.sum(dhb * xr, axis=0, keepdims=True)
        dkn_ref[...] += jnp.sum(dkn * xr, axis=0, keepdims=True)
        u = dhb * bg_ref[...] + dkn * kvg_ref[...]
        dh1 = dh2_ref[...] + r * u - h1 * ((r * r * r) * jnp.mean(u * h1, axis=-1, keepdims=True))
        dh1_ref[...] = dh1
        dh1b = dh1.astype(BF16)
        dh1b_ref[...] = dh1b
        dz_ref[...] = _dot_nt(dh1b, wo_ref[...]).astype(BF16)

    row = lambda m: (m, 0)
    fix2 = lambda m: (0, 0)
    return pl.pallas_call(
        body, name="layer_b_in_bwd", grid=(t // tm,),
        in_specs=[pl.BlockSpec((2, tm, D_MODEL), lambda m: (0, m, 0)),
                  pl.BlockSpec((2, tm, 128), lambda m: (0, m, 0)),
                  pl.BlockSpec((nb, D_MODEL, tn), lambda m: (0, 0, 0)),
                  pl.BlockSpec((D_MODEL, 256), fix2),
                  pl.BlockSpec((tm, D_MODEL), row), pl.BlockSpec((tm, D_MODEL), row),
                  pl.BlockSpec((1, D_MODEL), fix2), pl.BlockSpec((1, D_MODEL), fix2),
                  pl.BlockSpec((D_MODEL, D_MODEL), fix2)],
        out_specs=[pl.BlockSpec((tm, D_MODEL), row), pl.BlockSpec((tm, D_MODEL), row),
                   pl.BlockSpec((tm, D_MODEL), row), pl.BlockSpec((1, D_MODEL), fix2),
                   pl.BlockSpec((1, D_MODEL), fix2)],
        out_shape=[SDS((t, D_MODEL), F32), SDS((t, D_MODEL), BF16), SDS((t, D_MODEL), BF16),
                   SDS((1, D_MODEL), F32), SDS((1, D_MODEL), F32)],
        compiler_params=_cparams(),
    )(dqg, dkv, w_in_b, kv_w, h1, dh2, b_gain, kv_gain, w_out_a)


def _layer_a_in_bwd(dqg, dkv, w_in_a, x, dh1, a_gain, chip_sums):
    t = x.shape[0]
    tm = min(t, 256)
    nb, _, tn = w_in_a.shape
    per = D_MODEL // tn

    def body(dqg_ref, dkv_ref, w_ref, x_ref, dh1_ref, ag_ref, sums_ref, dx_ref, dan_ref, land_ref,
             send_sems, recv_sems):
        @pl.when(pl.program_id(0) == 0)
        def _():
            dan_ref[...] = jnp.zeros_like(dan_ref)
            for cp in _chip_copies(sums_ref, land_ref, send_sems, recv_sems):
                cp.start()

        dxn = jnp.zeros((tm, D_MODEL), F32)
        for i in range(nb):
            part = i // per
            src = dqg_ref if part in (0, 3) else dkv_ref
            outer = {0: 0, 3: 1, 1: 0, 2: 1}[part]
            blk = src[outer, :, (i % per) * tn:(i % per + 1) * tn]
            dxn = dxn + _dot_nt(blk, w_ref[i])
        xf = x_ref[...]
        r = _rstd(xf)
        dan_ref[...] += jnp.sum(dxn * (xf * r), axis=0, keepdims=True)
        u = dxn * ag_ref[...]
        dx_ref[...] = dh1_ref[...] + r * u - xf * ((r * r * r) * jnp.mean(u * xf, axis=-1, keepdims=True))

        @pl.when(pl.program_id(0) == t // tm - 1)
        def _():
            for cp in _chip_copies(sums_ref, land_ref, send_sems, recv_sems):
                cp.wait()

    row = lambda m: (m, 0)
    fix2 = lambda m: (0, 0)
    return pl.pallas_call(
        body, name="layer_a_in_bwd", grid=(t // tm,),
        in_specs=[pl.BlockSpec((2, tm, D_MODEL), lambda m: (0, m, 0)),
                  pl.BlockSpec((2, tm, D_MODEL), lambda m: (0, m, 0)),
                  pl.BlockSpec((nb, D_MODEL, tn), lambda m: (0, 0, 0)),
                  pl.BlockSpec((tm, D_MODEL), row), pl.BlockSpec((tm, D_MODEL), row),
                  pl.BlockSpec((1, D_MODEL), fix2), ANY],
        out_specs=[pl.BlockSpec((tm, D_MODEL), row), pl.BlockSpec((1, D_MODEL), fix2), ANY],
        out_shape=[SDS((t, D_MODEL), F32), SDS((1, D_MODEL), F32), SDS(chip_sums.shape, chip_sums.dtype)],
        scratch_shapes=[pltpu.SemaphoreType.DMA((3,)), pltpu.SemaphoreType.DMA((3,))],
        compiler_params=_cparams(),
    )(dqg, dkv, w_in_a, x, dh1, a_gain, chip_sums)


def _lut(s, vals):
    r = jnp.int32(vals[0])
    for i in range(1, len(vals)):
        r = jnp.where(s == i, jnp.int32(vals[i]), r)
    return r


def _held(steps, i):
    seq, cur = [None] * len(steps), None
    for k in range(len(steps) - 1, -1, -1):
        if steps[k][0] == i:
            cur = steps[k][1:3]
        seq[k] = cur
    for k in range(len(steps)):
        cur = seq[k] = seq[k] if seq[k] is not None else cur
    return seq


def _weight_grad_cols(name, my_slot, a, bs, steps, tn):
    t, dw = a.shape
    n_arr = len(bs)
    which = [s[0] for s in steps]
    blks = [s[3] for s in steps]

    def body(slot_ref, a_ref, *rest):
        b_refs, (o_ref, own_ref, at_ref) = rest[:n_arr], rest[n_arr:]
        s = pl.program_id(0)

        @pl.when(s == 0)
        def _():
            at_ref[...] = a_ref[...].T

        for i in range(n_arr):
            @pl.when(_lut(s, which) == i)
            def _(i=i):
                res = _dot(at_ref[...], b_refs[i][0])
                o_ref[0] = res.astype(BF16)

                @pl.when(_lut(s, blks) == slot_ref[0])
                def _():
                    own_ref[...] = res

    def b_spec(i):
        held = _held(steps, i)
        return pl.BlockSpec((1, t, tn), lambda s, slot: (_lut(s, [h[0] for h in held]), 0,
                                                         _lut(s, [h[1] for h in held])))

    return pl.pallas_call(
        body, name=name,
        grid_spec=pltpu.PrefetchScalarGridSpec(
            num_scalar_prefetch=1, grid=(len(steps),),
            in_specs=[pl.BlockSpec((t, dw), lambda s, slot: (0, 0))] + [b_spec(i) for i in range(n_arr)],
            out_specs=[pl.BlockSpec((1, dw, tn), lambda s, slot: (_lut(s, blks), 0, 0)),
                       pl.BlockSpec((dw, tn), lambda s, slot: (0, 0))],
            scratch_shapes=[pltpu.VMEM((dw, t), BF16)]),
        out_shape=[SDS((N_DEV, dw, tn), BF16), SDS((dw, tn), F32)],
        compiler_params=_cparams(),
    )(my_slot, a, *bs)


def _weight_grad_rows(name, my_slot, a, b):
    t, dw = a.shape
    n_o, _, c = b.shape
    rows = dw // N_DEV

    def body(slot_ref, a_ref, b_ref, o_ref, own_ref):
        at = a_ref[...].T
        res = [_dot(at, b_ref[o].astype(BF16)) for o in range(n_o)]
        for o in range(n_o):
            o_ref[0, :, o * c:(o + 1) * c] = res[o].astype(BF16)

        @pl.when(pl.program_id(0) == slot_ref[0])
        def _():
            for o in range(n_o):
                own_ref[:, o * c:(o + 1) * c] = res[o]

    return pl.pallas_call(
        body, name=name,
        grid_spec=pltpu.PrefetchScalarGridSpec(
            num_scalar_prefetch=1, grid=(N_DEV,),
            in_specs=[pl.BlockSpec((t, rows), lambda s, slot: (0, s)),
                      pl.BlockSpec((n_o, t, c), lambda s, slot: (0, 0, 0))],
            out_specs=[pl.BlockSpec((1, rows, n_o * c), lambda s, slot: (s, 0, 0)),
                       pl.BlockSpec((rows, n_o * c), lambda s, slot: (0, 0))]),
        out_shape=[SDS((N_DEV, rows, n_o * c), BF16), SDS((rows, n_o * c), F32)],
        compiler_params=_cparams(),
    )(my_slot, a, b)


def _lane_lo():
    return lax.broadcasted_iota(jnp.int32, (1, 128), 1) < HEAD_DIM


def _offset_sums(gt):
    keys = gt.shape[1]
    gc = gt[0:CHUNK]
    for cc in range(1, gt.shape[0] // CHUNK):
        gc = gc + pltpu.roll(gt[cc * CHUNK:(cc + 1) * CHUNK], keys - cc * CHUNK, 1)
    hi = gc.astype(BF16)
    lo = (gc - hi.astype(F32)).astype(BF16)
    flip = (lax.broadcasted_iota(jnp.int32, (CHUNK, CHUNK), 0)
            + lax.broadcasted_iota(jnp.int32, (CHUNK, CHUNK), 1) == CHUNK - 1).astype(BF16)
    gf = _dot(flip, hi) + _dot(flip, lo)
    skew = pltpu.roll(gf, 0, 1, stride=1, stride_axis=0)
    return jnp.sum(skew, axis=0, keepdims=True)


def _band_bias(w_row, band, rows):
    keys = w_row.shape[1]
    base = jnp.broadcast_to(w_row, (CHUNK, keys))
    skew = pltpu.roll(base, 0, 1, stride=1, stride_axis=0)
    skew = pltpu.roll(skew, keys - (CHUNK - 1), 1)
    col = lax.broadcasted_iota(jnp.int32, (CHUNK, keys), 1)
    chunk0 = jnp.where(col < band, skew, NEG)
    return jnp.concatenate(
        [chunk0] + [pltpu.roll(chunk0, cc * CHUNK, 1) for cc in range(1, rows // CHUNK)], axis=0)


def _silu_parts(g):
    sg = _sigmoid(g)
    return g * sg, sg * (1.0 + g * (1.0 - sg))


A_PAIRS = 2
A_LANES = 128 * A_PAIRS
A_STEPS = D_MODEL // A_LANES


def _a_specs():
    q = pl.BlockSpec((QBLK, A_LANES), lambda p, j: (j, p))
    ks = [pl.BlockSpec((QBLK, A_LANES), lambda p, j, b=b: (jnp.maximum(j - 2 + b, 0), A_STEPS + p)) for b in range(3)]
    vs = [pl.BlockSpec((QBLK, A_LANES), lambda p, j, b=b: (jnp.maximum(j - 2 + b, 0), 2 * A_STEPS + p))
          for b in range(3)]
    g = pl.BlockSpec((QBLK, A_LANES), lambda p, j: (j, 3 * A_STEPS + p))
    bias = pl.BlockSpec((A_PAIRS, 8, A_KEYS), lambda p, j: (p, 0, 0))
    return q, ks, vs, g, bias


def _a_fill_bias(w_ref, b_ref, j):
    _fill_bias(2 * A_PAIRS, lambda h: w_ref[h // 2, h % 2:h % 2 + 1, :], A_BAND, QBLK * (2 - j), 2, b_ref, j)


def _fill_bias(n, get_row, band, first_valid_col, early, bias_scr, j):
    @pl.when(j == 0)
    def _():
        for h in range(n):
            bias_scr[h] = _band_bias(get_row(h), band, bias_scr.shape[1])

    @pl.when(j < early)
    def _():
        keys = bias_scr.shape[2]
        col_ok = lax.broadcasted_iota(jnp.int32, (1, keys), 1) >= first_valid_col
        for h in range(n):
            bias_scr[n + h] = jnp.where(col_ok, bias_scr[h], NEG)


def _head_logits(q, k, bias_scr, idx, sel):
    qm = jnp.where(sel, q, jnp.zeros_like(q)) * SCALE
    return qm, _dot_nt(qm, k) + bias_scr[idx]


def _row_sums_everywhere(r, sel):
    return jnp.where(sel, pltpu.roll(r, HEAD_DIM, 1), r)


def _own_everywhere(x, sel):
    return jnp.where(sel, x, pltpu.roll(x, HEAD_DIM, 1))


def _minus_rows(s, row_full):
    return jnp.concatenate([s[:, i:i + 128] - row_full for i in range(0, s.shape[1], 128)], axis=1)


def _attn_a_fwd(qkvg, bias, gather):
    t = qkvg.shape[0]
    nq = t // QBLK
    n_g = len(gather)
    q_spec, k_specs, v_specs, g_spec, bias_spec = _a_specs()

    def body(q_ref, k0, k1, k2, v0, v1, v2, g_ref, w_ref, *rest):
        shard_refs, rest = rest[:n_g], rest[n_g:]
        z_ref, o_ref, lse_ref = rest[:3]
        full_refs, (b_ref, *comm) = rest[3:3 + n_g], rest[3 + n_g:]
        p = pl.program_id(0)
        j = pl.program_id(1)
        start, forward, finish = _gather_phases(shard_refs, full_refs, *comm)
        pl.when(jnp.logical_and(p == 0, j == 0))(start)
        pl.when(jnp.logical_and(p == A_STEPS // 2, j == 0))(forward)
        _a_fill_bias(w_ref, b_ref, j)
        early = (j < 2).astype(jnp.int32)
        lane_lo = _lane_lo()
        for pp in range(A_PAIRS):
            cols = slice(128 * pp, 128 * (pp + 1))
            q = q_ref[:, cols]
            k = jnp.concatenate([k0[:, cols], k1[:, cols], k2[:, cols]], axis=0)
            v = jnp.concatenate([v0[:, cols], v1[:, cols], v2[:, cols]], axis=0)
            outs, lses = [], []
            for hh in range(2):
                sel = lane_lo if hh == 0 else jnp.logical_not(lane_lo)
                _, s = _head_logits(q, k, b_ref, 2 * pp + hh + 2 * A_PAIRS * early, sel)
                mx = jnp.max(s, axis=-1, keepdims=True)
                e = jnp.exp(s - mx).astype(BF16)
                r = _dot(e, jnp.where(sel, v, jnp.ones_like(v)))
                l = _row_sums_everywhere(r, sel)
                outs.append(r / l)
                lses.append(mx + jnp.log(l))
            o = jnp.where(lane_lo, outs[0], outs[1])
            silu, _ = _silu_parts(g_ref[:, cols].astype(F32))
            o_ref[:, cols] = o.astype(BF16)
            z_ref[:, cols] = (o * silu).astype(BF16)
            lse_ref[:, cols] = jnp.where(lane_lo, lses[0], lses[1])
        pl.when(jnp.logical_and(p == A_STEPS - 1, j == nq - 1))(finish)

    out_spec = pl.BlockSpec((QBLK, A_LANES), lambda p, j: (j, p))
    outs = pl.pallas_call(
        body, name="attn_a_fwd", grid=(A_STEPS, nq),
        in_specs=[q_spec, *k_specs, *v_specs, g_spec, bias_spec] + [ANY] * n_g,
        out_specs=[out_spec, out_spec, out_spec] + [ANY] * n_g,
        out_shape=[SDS((t, D_MODEL), BF16), SDS((t, D_MODEL), BF16), SDS((t, D_MODEL), F32)]
        + [SDS((N_DEV, *s.shape), s.dtype) for s in gather],
        scratch_shapes=[pltpu.VMEM((4 * A_PAIRS, QBLK, A_KEYS), F32)] + _gather_scratch(n_g),
        compiler_params=_cparams(),
    )(qkvg, qkvg, qkvg, qkvg, qkvg, qkvg, qkvg, qkvg, bias, *gather)
    return outs[0], outs[1], outs[2], list(outs[3:])


def _attn_a_bwd(qkvg, bias, out_a, lse, dz, scatter):
    t = qkvg.shape[0]
    nq = t // QBLK
    n_sc = len(scatter)
    q_spec, k_specs, v_specs, g_spec, bias_spec = _a_specs()

    def body(q_ref, k0, k1, k2, v0, v1, v2, g_ref, w_ref, o_ref, lse_ref, dz_ref, *rest):
        sc_refs, rest = rest[:n_sc], rest[n_sc:]
        dqg_ref, dkv_ref, dg_ref = rest[:3]
        land_refs, rest = rest[3:3 + n_sc], rest[3 + n_sc:]
        dk_acc, dv_acc, gt_acc, b_ref, send_sems, recv_sems = rest
        j = pl.program_id(1)
        first = jnp.logical_and(pl.program_id(0) == 0, j == 0)
        last = jnp.logical_and(pl.program_id(0) == A_STEPS - 1, j == nq - 1)

        @pl.when(first)
        def _():
            for cp in _scatter_copies(sc_refs, land_refs, send_sems, recv_sems):
                cp.start()

        _a_fill_bias(w_ref, b_ref, j)

        @pl.when(j == 0)
        def _():
            dk_acc[...] = jnp.zeros_like(dk_acc)
            dv_acc[...] = jnp.zeros_like(dv_acc)
            gt_acc[...] = jnp.zeros_like(gt_acc)

        early = (j < 2).astype(jnp.int32)
        lane_lo = _lane_lo()
        for pp in range(A_PAIRS):
            cols = slice(128 * pp, 128 * (pp + 1))
            q = q_ref[:, cols]
            k = jnp.concatenate([k0[:, cols], k1[:, cols], k2[:, cols]], axis=0)
            v = jnp.concatenate([v0[:, cols], v1[:, cols], v2[:, cols]], axis=0)
            o = o_ref[:, cols].astype(F32)
            lse_pair = lse_ref[:, cols]
            dzf = dz_ref[:, cols].astype(F32)
            silu, dsilu = _silu_parts(g_ref[:, cols].astype(F32))
            do = dzf * silu
            dqg_ref[1, :, cols] = (dzf * o * dsilu).astype(BF16)
            doo = do * o
            dqs = []
            dk_blk = jnp.zeros((A_KEYS, 128), F32)
            dv_blk = jnp.zeros((A_KEYS, 128), F32)
            for hh in range(2):
                sel = lane_lo if hh == 0 else jnp.logical_not(lane_lo)
                qm, s = _head_logits(q, k, b_ref, 2 * pp + hh + 2 * A_PAIRS * early, sel)
                p = jnp.exp(_minus_rows(s, _own_everywhere(lse_pair, sel)))
                delta = jnp.sum(jnp.where(sel, doo, 0.0), axis=-1, keepdims=True)
                dom = jnp.where(sel, do, 0.0).astype(BF16)
                dp = _dot_nt(dom, v)
                ds = p * (dp - delta)
                gt_acc[2 * pp + hh] += ds
                dsb = ds.astype(BF16)
                dqs.append(_dot(dsb, k) * SCALE)
                dk_blk = dk_blk + _dot_tn(dsb, qm)
                dv_blk = dv_blk + _dot_tn(p.astype(BF16), dom)
            dqg_ref[0, :, cols] = jnp.where(lane_lo, dqs[0], dqs[1]).astype(BF16)
            for b in range(3):
                @pl.when(j - 2 + b >= 0)
                def _(b=b, cols=cols, dk_blk=dk_blk, dv_blk=dv_blk):
                    rows = pl.ds(pl.multiple_of((j - 2 + b) * QBLK, QBLK), QBLK)
                    dk_acc[rows, cols] += dk_blk[b * QBLK:(b + 1) * QBLK]
                    dv_acc[rows, cols] += dv_blk[b * QBLK:(b + 1) * QBLK]

        @pl.when(j == nq - 1)
        def _():
            dkv_ref[0] = dk_acc[...].astype(BF16)
            dkv_ref[1] = dv_acc[...].astype(BF16)
            for pp in range(A_PAIRS):
                dg_ref[pp] = jnp.concatenate([_offset_sums(gt_acc[2 * pp]), _offset_sums(gt_acc[2 * pp + 1]),
                                              jnp.zeros((6, A_DIAG), F32)], axis=0)

        @pl.when(last)
        def _():
            for cp in _scatter_copies(sc_refs, land_refs, send_sems, recv_sems):
                cp.wait()

    blk = pl.BlockSpec((QBLK, A_LANES), lambda p, j: (j, p))
    outs = pl.pallas_call(
        body, name="attn_a_bwd", grid=(A_STEPS, nq),
        in_specs=[q_spec, *k_specs, *v_specs, g_spec, bias_spec, blk, blk, blk] + [ANY] * n_sc,
        out_specs=[pl.BlockSpec((2, QBLK, A_LANES), lambda p, j: (0, j, p)),
                   pl.BlockSpec((2, t, A_LANES), lambda p, j: (0, 0, p)),
                   pl.BlockSpec((A_PAIRS, 8, A_DIAG), lambda p, j: (p, 0, 0))] + [ANY] * n_sc,
        out_shape=[SDS((2, t, D_MODEL), BF16), SDS((2, t, D_MODEL), BF16), SDS((N_HEADS // 2, 8, A_DIAG), F32)]
        + [SDS((N_DEV - 1, *g.shape[1:]), g.dtype) for g in scatter],
        scratch_shapes=[pltpu.VMEM((t, A_LANES), F32), pltpu.VMEM((t, A_LANES), F32),
                        pltpu.VMEM((2 * A_PAIRS, QBLK, A_KEYS), F32), pltpu.VMEM((4 * A_PAIRS, QBLK, A_KEYS), F32),
                        pltpu.SemaphoreType.DMA(((N_DEV - 1) * n_sc,)),
                        pltpu.SemaphoreType.DMA(((N_DEV - 1) * n_sc,))],
        compiler_params=_cparams(),
    )(qkvg, qkvg, qkvg, qkvg, qkvg, qkvg, qkvg, qkvg, bias, out_a, lse, dz, *scatter)
    return outs[0], outs[1], outs[2], list(outs[3:])


def _b_specs(qblk):
    per = qblk // B_PREV
    q = pl.BlockSpec((qblk, 512), lambda h, j: (j, h))
    g = pl.BlockSpec((qblk, 512), lambda h, j: (j, 2 + h))
    kp = pl.BlockSpec((B_PREV, 128), lambda h, j: (jnp.maximum(per * j - 1, 0), 0))
    kc = pl.BlockSpec((qblk, 128), lambda h, j: (j, 0))
    vp = pl.BlockSpec((B_PREV, 128), lambda h, j: (jnp.maximum(per * j - 1, 0), 1))
    vc = pl.BlockSpec((qblk, 128), lambda h, j: (j, 1))
    bias = pl.BlockSpec((B_GROUP, qblk + B_PREV), lambda h, j: (h, 0))
    sinks = pl.BlockSpec(memory_space=pltpu.SMEM)
    return q, g, kp, kc, vp, vc, bias, sinks


def _b_operands(kp, kc, vp, vc, kvh):
    k = jnp.concatenate([kp[...], kc[...]], axis=0)
    v = jnp.concatenate([vp[...], vc[...]], axis=0)
    kr = pltpu.roll(k, HEAD_DIM, 1)
    vr = pltpu.roll(v, HEAD_DIM, 1)
    first = kvh == 0
    return (jnp.where(first, k, kr), jnp.where(first, kr, k),
            jnp.where(first, v, vr), jnp.where(first, vr, v))


def _attn_b_fwd(qg, kv, bias, sinks):
    t = qg.shape[0]
    qblk = B_QBLK_FWD
    q_spec, g_spec, kp_spec, kc_spec, vp_spec, vc_spec, bias_spec, sink_spec = _b_specs(qblk)

    def body(q_ref, g_ref, kp, kc, vp, vc, w_ref, sink_ref, z_ref, o_ref, lse_ref, b_ref):
        kvh = pl.program_id(0)
        j = pl.program_id(1)
        _fill_bias(B_GROUP, lambda h: w_ref[h:h + 1, :], B_BAND, B_PREV, 1, b_ref, j)
        early = (j < 1).astype(jnp.int32)
        lane_lo = _lane_lo()
        k_lo, k_hi, v_lo, v_hi = _b_operands(kp, kc, vp, vc, kvh)
        for pp in range(B_GROUP // 2):
            cols = slice(128 * pp, 128 * (pp + 1))
            qp = q_ref[:, cols]
            outs, lses = [], []
            for hh in range(2):
                g = 2 * pp + hh
                sel = lane_lo if hh == 0 else jnp.logical_not(lane_lo)
                sink = sink_ref[kvh * B_GROUP + g]
                vv = v_lo if hh == 0 else v_hi
                _, s = _head_logits(qp, k_lo if hh == 0 else k_hi, b_ref, g + B_GROUP * early, sel)
                mx = jnp.maximum(jnp.max(s, axis=-1, keepdims=True), sink)
                e = jnp.exp(s - mx).astype(BF16)
                r = _dot(e, jnp.where(sel, vv, jnp.ones_like(vv)))
                l = _row_sums_everywhere(r, sel) + jnp.exp(sink - mx)
                outs.append(r / l)
                lses.append(mx + jnp.log(l))
            o = jnp.where(lane_lo, outs[0], outs[1])
            silu, _ = _silu_parts(g_ref[:, cols].astype(F32))
            o_ref[:, cols] = o.astype(BF16)
            z_ref[:, cols] = (o * silu).astype(BF16)
            lse_ref[:, cols] = jnp.where(lane_lo, lses[0], lses[1])

    out_spec = pl.BlockSpec((qblk, 512), lambda h, j: (j, h))
    return pl.pallas_call(
        body, name="attn_b_fwd", grid=(B_KV_HEADS, t // qblk),
        in_specs=[q_spec, g_spec, kp_spec, kc_spec, vp_spec, vc_spec, bias_spec, sink_spec],
        out_specs=[out_spec, out_spec, out_spec],
        out_shape=[SDS((t, D_MODEL), BF16), SDS((t, D_MODEL), BF16), SDS((t, D_MODEL), F32)],
        scratch_shapes=[pltpu.VMEM((2 * B_GROUP, qblk, qblk + B_PREV), F32)],
        compiler_params=_cparams(),
    )(qg, qg, kv, kv, kv, kv, bias, sinks)


def _attn_b_bwd(qg, kv, bias, sinks, out_b, lse, dz, bucket_onehot):
    t = qg.shape[0]
    qblk = B_QBLK_BWD
    keys = qblk + B_PREV
    nq = t // qblk
    q_spec, g_spec, kp_spec, kc_spec, vp_spec, vc_spec, bias_spec, sink_spec = _b_specs(qblk)

    def body(q_ref, g_ref, kp, kc, vp, vc, w_ref, sink_ref, o_ref, lse_ref, dz_ref, oh_ref,
             dqg_ref, dkv_ref, dt5_ref, dsink_ref, gt_acc, b_ref):
        kvh = pl.program_id(0)
        j = pl.program_id(1)
        _fill_bias(B_GROUP, lambda h: w_ref[h:h + 1, :], B_BAND, B_PREV, 1, b_ref, j)

        @pl.when(jnp.logical_and(kvh == 0, j == 0))
        def _():
            dkv_ref[...] = jnp.zeros_like(dkv_ref)

        @pl.when(j == 0)
        def _():
            gt_acc[...] = jnp.zeros_like(gt_acc)
            dsink_ref[...] = jnp.zeros_like(dsink_ref)

        early = (j < 1).astype(jnp.int32)
        lane_lo = _lane_lo()
        k_lo, k_hi, v_lo, v_hi = _b_operands(kp, kc, vp, vc, kvh)
        dk_blk = jnp.zeros((keys, 128), F32)
        dv_blk = jnp.zeros((keys, 128), F32)
        for pp in range(B_GROUP // 2):
            cols = slice(128 * pp, 128 * (pp + 1))
            qp = q_ref[:, cols]
            o = o_ref[:, cols].astype(F32)
            lse_pair = lse_ref[:, cols]
            dzf = dz_ref[:, cols].astype(F32)
            silu, dsilu = _silu_parts(g_ref[:, cols].astype(F32))
            do = dzf * silu
            dqg_ref[1, :, cols] = (dzf * o * dsilu).astype(BF16)
            doo = do * o
            dqs = []
            for hh in range(2):
                g = 2 * pp + hh
                sel = lane_lo if hh == 0 else jnp.logical_not(lane_lo)
                sink = sink_ref[kvh * B_GROUP + g]
                kk = k_lo if hh == 0 else k_hi
                vv = v_lo if hh == 0 else v_hi
                qm, s = _head_logits(qp, kk, b_ref, g + B_GROUP * early, sel)
                lse_h = _own_everywhere(lse_pair, sel)
                p = jnp.exp(_minus_rows(s, lse_h))
                delta = jnp.sum(jnp.where(sel, doo, 0.0), axis=-1, keepdims=True)
                dom = jnp.where(sel, do, 0.0).astype(BF16)
                dp = _dot_nt(dom, vv)
                ds = p * (dp - delta)
                gt_acc[g] += ds
                dsink_ref[g:g + 1, :] -= jnp.sum(jnp.exp(sink - lse_h) * delta, axis=0, keepdims=True)
                dsb = ds.astype(BF16)
                dqs.append(_dot(dsb, kk) * SCALE)
                dk_blk = dk_blk + _dot_tn(dsb, qm)
                dv_blk = dv_blk + _dot_tn(p.astype(BF16), dom)
            dqg_ref[0, :, cols] = jnp.where(lane_lo, dqs[0], dqs[1]).astype(BF16)
        mine = lane_lo == (kvh == 0)
        dk_add = jnp.where(mine, dk_blk + pltpu.roll(dk_blk, HEAD_DIM, 1), 0.0)
        dv_add = jnp.where(mine, dv_blk + pltpu.roll(dv_blk, HEAD_DIM, 1), 0.0)

        @pl.when(j >= 1)
        def _():
            rows = pl.ds(pl.multiple_of(j * qblk - B_PREV, B_PREV), B_PREV)
            dkv_ref[0, rows, :] += dk_add[0:B_PREV]
            dkv_ref[1, rows, :] += dv_add[0:B_PREV]

        rows = pl.ds(pl.multiple_of(j * qblk, qblk), qblk)
        dkv_ref[0, rows, :] += dk_add[B_PREV:keys]
        dkv_ref[1, rows, :] += dv_add[B_PREV:keys]

        @pl.when(j == nq - 1)
        def _():
            dd = jnp.concatenate([_offset_sums(gt_acc[g]) for g in range(B_GROUP)], axis=0)
            hi = dd.astype(BF16)
            lo = (dd - hi.astype(F32)).astype(BF16)
            dt5_ref[...] = _dot(hi, oh_ref[...]) + _dot(lo, oh_ref[...])

    blk = pl.BlockSpec((qblk, 512), lambda h, j: (j, h))
    return pl.pallas_call(
        body, name="attn_b_bwd", grid=(B_KV_HEADS, nq),
        in_specs=[q_spec, g_spec, kp_spec, kc_spec, vp_spec, vc_spec, bias_spec, sink_spec, blk, blk, blk,
                  pl.BlockSpec((keys, 128), lambda h, j: (0, 0))],
        out_specs=[pl.BlockSpec((2, qblk, 512), lambda h, j: (0, j, h)),
                   pl.BlockSpec((2, t, 128), lambda h, j: (0, 0, 0)),
                   pl.BlockSpec((B_GROUP, 128), lambda h, j: (h, 0)),
                   pl.BlockSpec((B_GROUP, 128), lambda h, j: (h, 0))],
        out_shape=[SDS((2, t, D_MODEL), BF16), SDS((2, t, 128), F32),
                   SDS((N_HEADS, 128), F32), SDS((N_HEADS, 128), F32)],
        scratch_shapes=[pltpu.VMEM((B_GROUP, qblk, keys), F32), pltpu.VMEM((2 * B_GROUP, qblk, keys), F32)],
        compiler_params=_cparams(),
    )(qg, qg, kv, kv, kv, kv, bias, sinks, out_b, lse, dz, bucket_onehot)


def _a_bias_by_offset(rel_bias):
    m = np.arange(A_DIAG)
    idx = np.clip(A_BAND - 1 - m, -A_REL_CLIP, A_REL_CLIP) + A_REL_CLIP
    by_head = rel_bias[idx].T.reshape(N_HEADS // 2, 2, A_DIAG)
    return jnp.concatenate([by_head, jnp.zeros((N_HEADS // 2, 6, A_DIAG), F32)], axis=1)


def _a_bias_grad(offset_sums):
    first = 319
    tail = jnp.sum(offset_sums[:, :first], axis=1)
    body = jnp.flip(offset_sums[:, first:first + 320], axis=1)
    body = body.at[:, -1].add(tail)
    full = jnp.concatenate([jnp.zeros((N_HEADS, 193), F32), body], axis=1)
    return full.T


def _t5_bucket(rel):
    nb = T5_BUCKETS // 2
    max_exact = nb // 2
    ret = jnp.where(rel > 0, nb, 0)
    n = jnp.abs(rel)
    nf = jnp.maximum(n, 1).astype(jnp.float32)
    large = max_exact + (jnp.log(nf / max_exact) / math.log(T5_MAX_DIST / max_exact)
                         * (nb - max_exact)).astype(jnp.int32)
    large = jnp.minimum(large, nb - 1)
    return ret + jnp.where(n < max_exact, n, large)


def _b_offset_buckets(keys):
    return _t5_bucket(jnp.arange(keys, dtype=jnp.int32) - (B_LEFT_CHUNKS * CHUNK + CHUNK - 1))


def _b_bias_by_offset(t5_table, keys):
    return t5_table[_b_offset_buckets(keys)].T


def _b_bucket_onehot(keys):
    return (_b_offset_buckets(keys)[:, None] == jnp.arange(128)[None, :]).astype(BF16)


def _local_step(my_slot, x, target, a_gain, w_in_a, rel_bias, late_shards, kv_gain, t5_table,
                b_gain, sinks, f_gain):
    a_bias = _a_bias_by_offset(rel_bias)
    b_bias_fwd = _b_bias_by_offset(t5_table, B_QBLK_FWD + B_PREV)
    b_bias_bwd = _b_bias_by_offset(t5_table, B_QBLK_BWD + B_PREV)
    sinks_flat = sinks.reshape(N_HEADS)

    xn, qkvg = _norm_matmul(x, a_gain, w_in_a)
    z_a, out_a, lse_a, (w_in_b, w_out_a, w_out_b, kv_w) = _attn_a_fwd(qkvg, a_bias, late_shards)
    w_out_a = w_out_a.reshape(D_MODEL, D_MODEL)
    w_out_b = w_out_b.reshape(D_MODEL, D_MODEL)
    kv_w = kv_w.reshape(D_MODEL, 2 * 128)
    h1, kvn, hb, kv, qg = _layer_a_out(x, z_a, w_out_a, kv_gain, b_gain, kv_w, w_in_b)
    z_b, out_b, lse_b = _attn_b_fwd(qg, kv, b_bias_fwd, sinks_flat)
    dh2, dh2b, dz_b, loss, d_fn = _layer_b_out_loss(h1, z_b, w_out_b, f_gain, target)

    dqg_b, dkv_b, d_t5, d_sink = _attn_b_bwd(qg, kv, b_bias_bwd, sinks_flat, out_b, lse_b, dz_b,
                                             _b_bucket_onehot(B_QBLK_BWD + B_PREV))
    dh1, dh1b, dz_a, d_bn, d_kn = _layer_b_in_bwd(dqg_b, dkv_b, w_in_b, kv_w, h1, dh2, b_gain, kv_gain, w_out_a)
    early = dict(
        b_w_out=_weight_grad_rows("grad_b_w_out", my_slot, z_b, dh2b[None]),
        b_w_in=_weight_grad_cols("grad_b_w_in", my_slot, hb, [dqg_b],
                                 [(0, o, c, 4 * o + c) for o in range(2) for c in range(4)], 256),
        kv_w=_weight_grad_rows("grad_kv_w", my_slot, kvn, dkv_b),
        a_w_out=_weight_grad_rows("grad_a_w_out", my_slot, z_a, dh1b[None]))
    dqg_a, dkv_a, d_rel, landed = _attn_a_bwd(qkvg, a_bias, out_a, lse_a, dz_a, [g[0] for g in early.values()])
    g_w_in_a = _weight_grad_cols(
        "grad_a_w_in", my_slot, xn, [dqg_a, dkv_a],
        [(0, 0, 0, 0), (0, 0, 1, 1), (1, 0, 0, 2), (1, 0, 1, 3), (1, 1, 0, 4), (1, 1, 1, 5), (0, 1, 0, 6), (0, 1, 1, 7)], 512)
    from_sibling, = _exchange_sibling([g_w_in_a[0]])
    x_i, y_i, c_i, chips = _place()
    del x_i, y_i
    forward_slots = jnp.stack([_slot(*chip, c_i) for chip in chips]).astype(jnp.int32)
    chip_sums = _pre_reduce("chip_sum_a_w_in", g_w_in_a[0], from_sibling, forward_slots)
    grad_x, d_an, from_chips = _layer_a_in_bwd(dqg_a, dkv_a, w_in_a, x, dh1, a_gain, chip_sums)

    matrices = {n: (g[1], [(land, 0, N_DEV - 1)]) for (n, g), land in zip(early.items(), landed)}
    matrices["a_w_in"] = (g_w_in_a[1], [(from_sibling, 3, 1), (from_chips, 0, 3)])
    small = dict(
        loss=loss[0, 0], a_norm=d_an, a_rel_bias=_a_bias_grad(d_rel[:, :2].reshape(N_HEADS, A_DIAG)),
        kv_norm=d_kn, t5_bias=d_t5[:, :T5_BUCKETS].T, b_norm=d_bn,
        b_sinks=d_sink[:, 0].reshape(1, N_HEADS), final_norm=d_fn)
    return grad_x, small, matrices


def _place():
    x, y, c = lax.axis_index("x"), lax.axis_index("y"), lax.axis_index("c")
    chips = [(1 - x, y), (x, 1 - y), (1 - x, 1 - y)]
    return x, y, c, chips


def _slot(px, py, pc):
    return 4 * px + 2 * py + pc


ANY = pl.BlockSpec(memory_space=pl.ANY)


def _peer(x, y, c, k):
    return (x ^ (k >> 2), y ^ ((k >> 1) & 1), c ^ (k & 1))


def _scatter_copies(grad_refs, land_refs, send_sems, recv_sems):
    x, y, c, _ = _place()
    copies = []
    for t, (grad, land) in enumerate(zip(grad_refs, land_refs)):
        for k in range(1, N_DEV):
            peer = _peer(x, y, c, k)
            sem = (N_DEV - 1) * t + k - 1
            copies.append(pltpu.make_async_remote_copy(
                src_ref=grad.at[_slot(*peer)], dst_ref=land.at[k - 1],
                send_sem=send_sems.at[sem], recv_sem=recv_sems.at[sem],
                device_id=peer, device_id_type=MESH))
    return copies


def _gather_phases(ins, outs, send_sems, recv_sems, local_sems):
    n = len(ins)
    x, y, c, chips = _place()
    me, sibling = (x, y, c), (x, y, 1 - c)

    def copy(t, k, block, to, src=None):
        dst = outs[t].at[_slot(*block)]
        return pltpu.make_async_remote_copy(
            src_ref=dst if src is None else src, dst_ref=dst,
            send_sem=send_sems.at[7 * t + k], recv_sem=recv_sems.at[7 * t + k],
            device_id=to, device_id_type=MESH)

    def lists():
        mine = [pltpu.make_async_copy(ins[t], outs[t].at[_slot(*me)], local_sems.at[t]) for t in range(n)]
        first = []
        for t in range(n):
            first.append(copy(t, 0, me, sibling, src=ins[t]))
            first += [copy(t, 1 + j, me, (*chip, c), src=ins[t]) for j, chip in enumerate(chips)]
        passed = [copy(t, 4 + j, (*chip, c), sibling) for t in range(n) for j, chip in enumerate(chips)]
        return mine, first, passed

    def start():
        mine, first, _ = lists()
        for cp in mine + first:
            cp.start()

    def forward():
        _, _, passed = lists()
        for t in range(n):
            for j, chip in enumerate(chips):
                copy(t, 1 + j, (*chip, c), me).wait_recv()
                passed[3 * t + j].start()

    def finish():
        mine, first, passed = lists()
        for t in range(n):
            copy(t, 0, sibling, me).wait_recv()
            for j, chip in enumerate(chips):
                copy(t, 4 + j, (*chip, 1 - c), me).wait_recv()
        for cp in first + passed:
            cp.wait_send()
        for cp in mine:
            cp.wait()

    return start, forward, finish


def _gather_scratch(n):
    return [pltpu.SemaphoreType.DMA((7 * n,)), pltpu.SemaphoreType.DMA((7 * n,)), pltpu.SemaphoreType.DMA((n,))]


def _all_gather(shards):
    n = len(shards)

    def body(*refs):
        start, forward, finish = _gather_phases(refs[:n], refs[n:2 * n], *refs[2 * n:])
        start()
        forward()
        finish()

    return pl.pallas_call(
        body, name="all_gather_weights",
        in_specs=[ANY] * n, out_specs=[ANY] * n,
        out_shape=[SDS((N_DEV, *s.shape), s.dtype) for s in shards],
        scratch_shapes=_gather_scratch(n),
    )(*shards)


def _exchange_sibling(grads):
    n = len(grads)

    def body(*refs):
        ins, outs = refs[:n], refs[n:2 * n]
        send_sems, recv_sems = refs[2 * n:]
        x, y, c, chips = _place()
        sibling = (x, y, 1 - c)
        copies = []
        for t in range(n):
            blocks = [(*chip, 1 - c) for chip in chips] + [sibling]
            for k, block in enumerate(blocks):
                copies.append(pltpu.make_async_remote_copy(
                    src_ref=ins[t].at[_slot(*block)], dst_ref=outs[t].at[k],
                    send_sem=send_sems.at[4 * t + k], recv_sem=recv_sems.at[4 * t + k],
                    device_id=sibling, device_id_type=MESH))
        for cp in copies:
            cp.start()
        for cp in copies:
            cp.wait()

    return pl.pallas_call(
        body, name="grads_to_sibling",
        in_specs=[ANY] * n, out_specs=[ANY] * n,
        out_shape=[SDS((4, *g.shape[1:]), g.dtype) for g in grads],
        scratch_shapes=[pltpu.SemaphoreType.DMA((4 * n,)), pltpu.SemaphoreType.DMA((4 * n,))],
    )(*grads)


def _chip_copies(sums_ref, land_ref, send_sems, recv_sems):
    x, y, c, chips = _place()
    del x, y
    return [pltpu.make_async_remote_copy(
        src_ref=sums_ref.at[j], dst_ref=land_ref.at[j], send_sem=send_sems.at[j], recv_sem=recv_sems.at[j],
        device_id=(*chip, c), device_id_type=MESH) for j, chip in enumerate(chips)]


def _row_tile(rows):
    return min(rows, 256)


def _pre_reduce(name, g, from_sibling, slots):
    _, r, c = g.shape
    tr = _row_tile(r)

    def body(slots_ref, g_ref, s_ref, o_ref):
        del slots_ref
        o_ref[...] = (g_ref[...].astype(F32) + s_ref[...].astype(F32)).astype(BF16)

    return pl.pallas_call(
        body, name=name,
        grid_spec=pltpu.PrefetchScalarGridSpec(
            num_scalar_prefetch=1, grid=(3, r // tr),
            in_specs=[pl.BlockSpec((1, tr, c), lambda j, i, s: (s[j], i, 0)),
                      pl.BlockSpec((1, tr, c), lambda j, i, s: (j, i, 0))],
            out_specs=pl.BlockSpec((1, tr, c), lambda j, i, s: (j, i, 0))),
        out_shape=SDS((3, r, c), BF16),
        compiler_params=_cparams(),
    )(slots, g, from_sibling)


def _adamw(w, g, m, v):
    m2 = ADAM_B1 * m + (1.0 - ADAM_B1) * g
    v2 = ADAM_B2 * v + (1.0 - ADAM_B2) * jnp.square(g)
    m_hat = m2 / (1.0 - ADAM_B1 ** ADAM_STEP)
    v_hat = v2 / (1.0 - ADAM_B2 ** ADAM_STEP)
    delta = -ADAM_LR * (m_hat / (jnp.sqrt(v_hat) + ADAM_EPS) + ADAM_WD * w)
    return delta, m2, v2


def _reduce_adamw(name, own, partials, w, m, v):
    r, c = own.shape
    tr = _row_tile(r)
    n_p = len(partials)

    def body(own_ref, *rest):
        p_refs, (w_ref, m_ref, v_ref, grad_ref, d_ref, nm_ref, nv_ref) = rest[:n_p], rest[n_p:]
        grad = own_ref[...]
        for p_ref, (_, _, count) in zip(p_refs, partials):
            for j in range(count):
                grad = grad + p_ref[j].astype(F32)
        grad_ref[...] = grad
        d_ref[...], nm_ref[...], nv_ref[...] = _adamw(w_ref[...], grad, m_ref[...], v_ref[...])

    flat = pl.BlockSpec((tr, c), lambda i: (i, 0))
    return pl.pallas_call(
        body, name=name, grid=(r // tr,),
        in_specs=[flat] + [pl.BlockSpec((count, tr, c), lambda i, first=first, count=count: (first // count, i, 0))
                           for _, first, count in partials] + [flat, flat, flat],
        out_specs=[flat, flat, flat, flat],
        out_shape=[SDS((r, c), F32)] * 4,
        compiler_params=_cparams(),
    )(own, *[p[0] for p in partials], w, m, v)


_SMALL = (("a_norm", 8), ("a_rel_bias", 72), ("kv_norm", 8), ("t5_bias", 8), ("b_norm", 8),
          ("b_sinks", 8), ("final_norm", 8), ("loss", 8))
_SMALL_ROWS = sum(r for _, r in _SMALL)


def _pack_small(parts):
    rows = []
    for name, n_rows in _SMALL:
        flat = parts[name].reshape(-1).astype(F32)
        rows.append(jnp.pad(flat, (0, n_rows * 128 - flat.shape[0])).reshape(n_rows, 128))
    return jnp.concatenate(rows, axis=0)


def _unpack_small(buf, shapes):
    out, at = {}, 0
    for name, n_rows in _SMALL:
        size = int(np.prod(shapes[name])) if shapes[name] else 1
        out[name] = buf[at:at + n_rows].reshape(-1)[:size].reshape(shapes[name])
        at += n_rows
    return out


def _small_allreduce_adamw(gbuf, wbuf, mbuf, vbuf):
    def body(g_ref, w_ref, m_ref, v_ref, sum_ref, d_ref, nm_ref, nv_ref, land_ref, send_sems, recv_sems):
        x, y, c, _ = _place()
        my_slot = _slot(x, y, c)
        land_ref[my_slot] = g_ref[...]
        copies = []
        for k in range(1, N_DEV):
            peer = (x ^ (k >> 2), y ^ ((k >> 1) & 1), c ^ (k & 1))
            copies.append(pltpu.make_async_remote_copy(
                src_ref=g_ref, dst_ref=land_ref.at[my_slot],
                send_sem=send_sems.at[k - 1], recv_sem=recv_sems.at[k - 1],
                device_id=peer, device_id_type=MESH))
        for cp in copies:
            cp.start()
        for k in range(1, N_DEV):
            peer_slot = _slot(x ^ (k >> 2), y ^ ((k >> 1) & 1), c ^ (k & 1))
            pltpu.make_async_remote_copy(
                src_ref=g_ref, dst_ref=land_ref.at[peer_slot],
                send_sem=send_sems.at[k - 1], recv_sem=recv_sems.at[k - 1],
                device_id=(x, y, c), device_id_type=MESH).wait_recv()
        for cp in copies:
            cp.wait_send()
        total = land_ref[0]
        for s in range(1, N_DEV):
            total = total + land_ref[s]
        sum_ref[...] = total
        d_ref[...], nm_ref[...], nv_ref[...] = _adamw(w_ref[...], total, m_ref[...], v_ref[...])

    vm = pl.BlockSpec(memory_space=pltpu.VMEM)
    shape = SDS((_SMALL_ROWS, 128), F32)
    return pl.pallas_call(
        body, name="small_allreduce_adamw",
        in_specs=[vm] * 4, out_specs=[vm] * 4, out_shape=[shape] * 4,
        scratch_shapes=[pltpu.VMEM((N_DEV, _SMALL_ROWS, 128), F32),
                        pltpu.SemaphoreType.DMA((N_DEV - 1,)), pltpu.SemaphoreType.DMA((N_DEV - 1,))],
    )(gbuf, wbuf, mbuf, vbuf)


def kernel(x, a_norm, a_w_in, a_rel_bias, a_w_out, kv_norm, kv_w, t5_bias, b_norm, b_w_in, b_sinks, b_w_out, final_norm, loss_target, m_a_norm, m_a_w_in, m_a_rel_bias, m_a_w_out, m_kv_norm, m_kv_w, m_t5_bias, m_b_norm, m_b_w_in, m_b_sinks, m_b_w_out, m_final_norm, v_a_norm, v_a_w_in, v_a_rel_bias, v_a_w_out, v_kv_norm, v_kv_w, v_t5_bias, v_b_norm, v_b_w_in, v_b_sinks, v_b_w_out, v_final_norm):
    xi, yi, ci = lax.axis_index("x"), lax.axis_index("y"), lax.axis_index("c")
    my_slot = _slot(xi, yi, ci)

    w_in_a, a_gain = _all_gather([a_w_in[0].astype(BF16), a_norm])
    a_gain = a_gain.reshape(1, D_MODEL)

    slot_arr = jnp.reshape(my_slot, (1,)).astype(jnp.int32)
    late_shards = [b_w_in[0].astype(BF16), a_w_out[0].astype(BF16), b_w_out[0].astype(BF16), kv_w.astype(BF16)]
    grad_x, loc, matrices = _local_step(
        slot_arr, x[0], loss_target[0], a_gain, w_in_a, a_rel_bias[0], late_shards,
        kv_norm.reshape(1, D_MODEL), t5_bias, b_norm, b_sinks, final_norm.reshape(1, D_MODEL))

    shard_w = dict(a_w_in=a_w_in[0], b_w_in=b_w_in[0], a_w_out=a_w_out[0], b_w_out=b_w_out[0], kv_w=kv_w)
    shard_m = dict(a_w_in=m_a_w_in[0], b_w_in=m_b_w_in[0], a_w_out=m_a_w_out[0], b_w_out=m_b_w_out[0], kv_w=m_kv_w)
    shard_v = dict(a_w_in=v_a_w_in[0], b_w_in=v_b_w_in[0], a_w_out=v_a_w_out[0], b_w_out=v_b_w_out[0], kv_w=v_kv_w)
    big = {n: _reduce_adamw("adamw_" + n, own, partials, shard_w[n], shard_m[n], shard_v[n])
           for n, (own, partials) in matrices.items()}

    def own_row(vec):
        return lax.dynamic_update_slice(jnp.zeros((N_DEV, 128), F32), vec, (my_slot, 0))

    zero = jnp.zeros((), F32)
    small_w = dict(a_norm=own_row(a_norm), a_rel_bias=a_rel_bias, kv_norm=kv_norm, t5_bias=t5_bias,
                   b_norm=b_norm, b_sinks=b_sinks, final_norm=final_norm, loss=zero)
    small_m = dict(a_norm=own_row(m_a_norm), a_rel_bias=m_a_rel_bias, kv_norm=m_kv_norm, t5_bias=m_t5_bias,
                   b_norm=m_b_norm, b_sinks=m_b_sinks, final_norm=m_final_norm, loss=zero)
    small_v = dict(a_norm=own_row(v_a_norm), a_rel_bias=v_a_rel_bias, kv_norm=v_kv_norm, t5_bias=v_t5_bias,
                   b_norm=v_b_norm, b_sinks=v_b_sinks, final_norm=v_final_norm, loss=zero)
    small = _small_allreduce_adamw(_pack_small(loc), _pack_small(small_w), _pack_small(small_m), _pack_small(small_v))
    shapes = dict(a_norm=(N_DEV, 128), a_rel_bias=a_rel_bias.shape, kv_norm=kv_norm.shape, t5_bias=t5_bias.shape,
                  b_norm=b_norm.shape, b_sinks=b_sinks.shape, final_norm=final_norm.shape, loss=())
    sm = [_unpack_small(buf, shapes) for buf in small]
    for part in sm:
        part["a_norm"] = lax.dynamic_slice(part["a_norm"], (my_slot, 0), (1, 128))

    order = ("a_norm", "a_w_in", "a_rel_bias", "a_w_out", "kv_norm", "kv_w", "t5_bias", "b_norm",
             "b_w_in", "b_sinks", "b_w_out", "final_norm")
    lead = dict(a_w_in=True, b_w_in=True, a_w_out=True, b_w_out=True, kv_w=False)

    def pick(kind, name):
        if name in big:
            val = big[name][kind]
            return val[None] if lead[name] else val
        return sm[kind][name]

    outs = [sm[0]["loss"], grad_x[None]]
    for kind in range(4):
        outs += [pick(kind, n) for n in order]
    return tuple(outs)
```

```python
import functools
import math

import numpy as np
import jax
import jax.numpy as jnp
from jax import lax
from jax.experimental import pallas as pl
from jax.experimental.pallas import tpu as pltpu

F32 = jnp.float32
BF16 = jnp.bfloat16
SDS = jax.ShapeDtypeStruct

D_MODEL = 1024
HEAD_DIM = 64
CHUNK = 64
N_HEADS = 16
RMS_EPS = 1e-6
A_LEFT_CHUNKS = 8
A_BAND = (A_LEFT_CHUNKS + 1) * CHUNK
A_REL_CLIP = 256
B_KV_HEADS = 2
B_GROUP = 8
B_LEFT_CHUNKS = 2
B_BAND = (B_LEFT_CHUNKS + 1) * CHUNK
T5_BUCKETS = 32
T5_MAX_DIST = 128
QBLK = 256
A_KEYS = 3 * QBLK
B_QBLK_FWD = 128
B_QBLK_BWD = 256
B_PREV = 128
A_DIAG = A_KEYS
NEG = -1e30
SCALE = HEAD_DIM ** -0.5
N_DEV = 8

ADAM_LR = 0.001
ADAM_B1 = 0.9
ADAM_B2 = 0.999
ADAM_EPS = 1e-08
ADAM_WD = 0.01
ADAM_STEP = 10

VMEM_LIMIT_BYTES = 56 * 1024 * 1024
MESH = pl.DeviceIdType.MESH


def _cparams():
    return pltpu.CompilerParams(vmem_limit_bytes=VMEM_LIMIT_BYTES)


def _dot(a, b):
    return jnp.dot(a, b, preferred_element_type=F32)


def _dot_nt(a, b):
    return lax.dot_general(a, b, (((1,), (1,)), ((), ())), preferred_element_type=F32)


def _dot_tn(a, b):
    return lax.dot_general(a, b, (((0,), (0,)), ((), ())), preferred_element_type=F32)


def _rstd(xf):
    return lax.rsqrt(jnp.mean(xf * xf, axis=-1, keepdims=True) + RMS_EPS)


def _sigmoid(x):
    return 1.0 / (1.0 + jnp.exp(-x))


def _norm_matmul(x, gain, w):
    t = x.shape[0]
    nb, _, tn = w.shape
    tm = min(t, 1024)

    def body(x_ref, g_ref, w_ref, xn_ref, o_ref):
        @pl.when(pl.program_id(1) == 0)
        def _():
            xf = x_ref[...]
            xn_ref[...] = ((xf * _rstd(xf)) * g_ref[...]).astype(BF16)

        o_ref[...] = _dot(xn_ref[...], w_ref[0]).astype(BF16)

    return pl.pallas_call(
        body, name="norm_matmul", grid=(t // tm, nb),
        in_specs=[pl.BlockSpec((tm, D_MODEL), lambda m, n: (m, 0)),
                  pl.BlockSpec((1, D_MODEL), lambda m, n: (0, 0)),
                  pl.BlockSpec((1, D_MODEL, tn), lambda m, n: (n, 0, 0))],
        out_specs=[pl.BlockSpec((tm, D_MODEL), lambda m, n: (m, 0)),
                   pl.BlockSpec((tm, tn), lambda m, n: (m, n))],
        out_shape=[SDS((t, D_MODEL), BF16), SDS((t, nb * tn), BF16)],
        compiler_params=_cparams(),
    )(x, gain, w)


def _layer_a_out(x, z, w_out, kv_gain, b_gain, kv_w, w_in_b):
    t = x.shape[0]
    tm = min(t, 512)
    nb, _, tn = w_in_b.shape

    def body(x_ref, z_ref, wo_ref, kvg_ref, bg_ref, kvw_ref, wb_ref,
             h1_ref, kvn_ref, hb_ref, kv_ref, qg_ref):
        h1 = x_ref[...] + _dot(z_ref[...], wo_ref[...])
        h1_ref[...] = h1
        y0 = h1 * _rstd(h1)
        kvn = (y0 * kvg_ref[...]).astype(BF16)
        hb = (y0 * bg_ref[...]).astype(BF16)
        kvn_ref[...] = kvn
        hb_ref[...] = hb
        kv_ref[...] = _dot(kvn, kvw_ref[...]).astype(BF16)
        for i in range(nb):
            qg_ref[:, i * tn:(i + 1) * tn] = _dot(hb, wb_ref[i]).astype(BF16)

    row = lambda m: (m, 0)
    fix2 = lambda m: (0, 0)
    return pl.pallas_call(
        body, name="layer_a_out", grid=(t // tm,),
        in_specs=[pl.BlockSpec((tm, D_MODEL), row), pl.BlockSpec((tm, D_MODEL), row),
                  pl.BlockSpec((D_MODEL, D_MODEL), fix2),
                  pl.BlockSpec((1, D_MODEL), fix2), pl.BlockSpec((1, D_MODEL), fix2),
                  pl.BlockSpec((D_MODEL, 256), fix2),
                  pl.BlockSpec((nb, D_MODEL, tn), lambda m: (0, 0, 0))],
        out_specs=[pl.BlockSpec((tm, D_MODEL), row), pl.BlockSpec((tm, D_MODEL), row),
                   pl.BlockSpec((tm, D_MODEL), row), pl.BlockSpec((tm, 256), row),
                   pl.BlockSpec((tm, nb * tn), row)],
        out_shape=[SDS((t, D_MODEL), F32), SDS((t, D_MODEL), BF16), SDS((t, D_MODEL), BF16),
                   SDS((t, 256), BF16), SDS((t, nb * tn), BF16)],
        compiler_params=_cparams(),
    )(x, z, w_out, kv_gain, b_gain, kv_w, w_in_b)


def _layer_b_out_loss(h1, z, w_out, f_gain, target):
    t = h1.shape[0]
    tm = min(t, 512)

    def body(h1_ref, z_ref, wo_ref, fg_ref, tgt_ref,
             dh2_ref, dh2b_ref, dz_ref, loss_ref, dfn_ref):
        @pl.when(pl.program_id(0) == 0)
        def _():
            loss_ref[...] = jnp.zeros_like(loss_ref)
            dfn_ref[...] = jnp.zeros_like(dfn_ref)

        h2 = h1_ref[...] + _dot(z_ref[...], wo_ref[...])
        r = _rstd(h2)
        yn = h2 * r
        fg = fg_ref[...]
        err = yn * fg - tgt_ref[...]
        loss_ref[...] += (0.5 / D_MODEL) * jnp.sum(err * err)
        dy = err * (1.0 / D_MODEL)
        dfn_ref[...] += jnp.sum(dy * yn, axis=0, keepdims=True)
        u = dy * fg
        dh2 = r * u - h2 * ((r * r * r) * jnp.mean(u * h2, axis=-1, keepdims=True))
        dh2_ref[...] = dh2
        dh2b = dh2.astype(BF16)
        dh2b_ref[...] = dh2b
        dz_ref[...] = _dot_nt(dh2b, wo_ref[...]).astype(BF16)

    row = lambda m: (m, 0)
    fix2 = lambda m: (0, 0)
    return pl.pallas_call(
        body, name="layer_b_out_loss", grid=(t // tm,),
        in_specs=[pl.BlockSpec((tm, D_MODEL), row), pl.BlockSpec((tm, D_MODEL), row),
                  pl.BlockSpec((D_MODEL, D_MODEL), fix2), pl.BlockSpec((1, D_MODEL), fix2),
                  pl.BlockSpec((tm, D_MODEL), row)],
        out_specs=[pl.BlockSpec((tm, D_MODEL), row), pl.BlockSpec((tm, D_MODEL), row),
                   pl.BlockSpec((tm, D_MODEL), row), pl.BlockSpec((1, 128), fix2),
                   pl.BlockSpec((1, D_MODEL), fix2)],
        out_shape=[SDS((t, D_MODEL), F32), SDS((t, D_MODEL), BF16), SDS((t, D_MODEL), BF16),
                   SDS((1, 128), F32), SDS((1, D_MODEL), F32)],
        compiler_params=_cparams(),
    )(h1, z, w_out, f_gain, target)


def _layer_b_in_bwd(dqg, dkv, w_in_b, kv_w, h1, dh2, b_gain, kv_gain, w_out_a):
    t = h1.shape[0]
    tm = min(t, 256)
    nb, _, tn = w_in_b.shape
    per = D_MODEL // tn

    def body(dqg_ref, dkv_ref, wb_ref, kvw_ref, h1_ref, dh2_ref, bg_ref, kvg_ref, wo_ref,
             dh1_ref, dh1b_ref, dz_ref, dbn_ref, dkn_ref):
        @pl.when(pl.program_id(0) == 0)
        def _():
            dbn_ref[...] = jnp.zeros_like(dbn_ref)
            dkn_ref[...] = jnp.zeros_like(dkn_ref)

        dhb = jnp.zeros((tm, D_MODEL), F32)
        for i in range(nb):
            blk = dqg_ref[i // per, :, (i % per) * tn:(i % per + 1) * tn]
            dhb = dhb + _dot_nt(blk, wb_ref[i])
        dkn = (_dot_nt(dkv_ref[0].astype(BF16), kvw_ref[:, 0:128])
               + _dot_nt(dkv_ref[1].astype(BF16), kvw_ref[:, 128:256]))
        h1 = h1_ref[...]
        r = _rstd(h1)
        xr = h1 * r
        dbn_ref[...] += jnp.sum(dhb * xr, axis=0, keepdims=True)
        dkn_ref[...] += jnp.sum(dkn * xr, axis=0, keepdims=True)
        u = dhb * bg_ref[...] + dkn * kvg_ref[...]
        dh1 = dh2_ref[...] + r * u - h1 * ((r * r * r) * jnp.mean(u * h1, axis=-1, keepdims=True))
        dh1_ref[...] = dh1
        dh1b = dh1.astype(BF16)
        dh1b_ref[...] = dh1b
        dz_ref[...] = _dot_nt(dh1b, wo_ref[...]).astype(BF16)

    row = lambda m: (m, 0)
    fix2 = lambda m: (0, 0)
    return pl.pallas_call(
        body, name="layer_b_in_bwd", grid=(t // tm,),
        in_specs=[pl.BlockSpec((2, tm, D_MODEL), lambda m: (0, m, 0)),
                  pl.BlockSpec((2, tm, 128), lambda m: (0, m, 0)),
                  pl.BlockSpec((nb, D_MODEL, tn), lambda m: (0, 0, 0)),
                  pl.BlockSpec((D_MODEL, 256), fix2),
                  pl.BlockSpec((tm, D_MODEL), row), pl.BlockSpec((tm, D_MODEL), row),
                  pl.BlockSpec((1, D_MODEL), fix2), pl.BlockSpec((1, D_MODEL), fix2),
                  pl.BlockSpec((D_MODEL, D_MODEL), fix2)],
        out_specs=[pl.BlockSpec((tm, D_MODEL), row), pl.BlockSpec((tm, D_MODEL), row),
                   pl.BlockSpec((tm, D_MODEL), row), pl.BlockSpec((1, D_MODEL), fix2),
                   pl.BlockSpec((1, D_MODEL), fix2)],
        out_shape=[SDS((t, D_MODEL), F32), SDS((t, D_MODEL), BF16), SDS((t, D_MODEL), BF16),
                   SDS((1, D_MODEL), F32), SDS((1, D_MODEL), F32)],
        compiler_params=_cparams(),
    )(dqg, dkv, w_in_b, kv_w, h1, dh2, b_gain, kv_gain, w_out_a)


def _layer_a_in_bwd(dqg, dkv, w_in_a, x, dh1, a_gain, chip_sums):
    t = x.shape[0]
    tm = min(t, 256)
    nb, _, tn = w_in_a.shape
    per = D_MODEL // tn

    def body(dqg_ref, dkv_ref, w_ref, x_ref, dh1_ref, ag_ref, sums_ref, dx_ref, dan_ref, land_ref,
             send_sems, recv_sems):
        @pl.when(pl.program_id(0) == 0)
        def _():
            dan_ref[...] = jnp.zeros_like(dan_ref)
            for cp in _chip_copies(sums_ref, land_ref, send_sems, recv_sems):
                cp.start()

        dxn = jnp.zeros((tm, D_MODEL), F32)
        for i in range(nb):
            part = i // per
            src = dqg_ref if part in (0, 3) else dkv_ref
            outer = {0: 0, 3: 1, 1: 0, 2: 1}[part]
            blk = src[outer, :, (i % per) * tn:(i % per + 1) * tn]
            dxn = dxn + _dot_nt(blk, w_ref[i])
        xf = x_ref[...]
        r = _rstd(xf)
        dan_ref[...] += jnp.sum(dxn * (xf * r), axis=0, keepdims=True)
        u = dxn * ag_ref[...]
        dx_ref[...] = dh1_ref[...] + r * u - xf * ((r * r * r) * jnp.mean(u * xf, axis=-1, keepdims=True))

        @pl.when(pl.program_id(0) == t // tm - 1)
        def _():
            for cp in _chip_copies(sums_ref, land_ref, send_sems, recv_sems):
                cp.wait()

    row = lambda m: (m, 0)
    fix2 = lambda m: (0, 0)
    return pl.pallas_call(
        body, name="layer_a_in_bwd", grid=(t // tm,),
        in_specs=[pl.BlockSpec((2, tm, D_MODEL), lambda m: (0, m, 0)),
                  pl.BlockSpec((2, tm, D_MODEL), lambda m: (0, m, 0)),
                  pl.BlockSpec((nb, D_MODEL, tn), lambda m: (0, 0, 0)),
                  pl.BlockSpec((tm, D_MODEL), row), pl.BlockSpec((tm, D_MODEL), row),
                  pl.BlockSpec((1, D_MODEL), fix2), ANY],
        out_specs=[pl.BlockSpec((tm, D_MODEL), row), pl.BlockSpec((1, D_MODEL), fix2), ANY],
        out_shape=[SDS((t, D_MODEL), F32), SDS((1, D_MODEL), F32), SDS(chip_sums.shape, chip_sums.dtype)],
        scratch_shapes=[pltpu.SemaphoreType.DMA((3,)), pltpu.SemaphoreType.DMA((3,))],
        compiler_params=_cparams(),
    )(dqg, dkv, w_in_a, x, dh1, a_gain, chip_sums)


def _lut(s, vals):
    r = jnp.int32(vals[0])
    for i in range(1, len(vals)):
        r = jnp.where(s == i, jnp.int32(vals[i]), r)
    return r


def _held(steps, i):
    seq, cur = [None] * len(steps), None
    for k in range(len(steps) - 1, -1, -1):
        if steps[k][0] == i:
            cur = steps[k][1:3]
        seq[k] = cur
    for k in range(len(steps)):
        cur = seq[k] = seq[k] if seq[k] is not None else cur
    return seq


def _weight_grad_cols(name, my_slot, a, bs, steps, tn):
    t, dw = a.shape
    n_arr = len(bs)
    which = [s[0] for s in steps]
    blks = [s[3] for s in steps]

    def body(slot_ref, a_ref, *rest):
        b_refs, (o_ref, own_ref, at_ref) = rest[:n_arr], rest[n_arr:]
        s = pl.program_id(0)

        @pl.when(s == 0)
        def _():
            at_ref[...] = a_ref[...].T

        for i in range(n_arr):
            @pl.when(_lut(s, which) == i)
            def _(i=i):
                res = _dot(at_ref[...], b_refs[i][0])
                o_ref[0] = res.astype(BF16)

                @pl.when(_lut(s, blks) == slot_ref[0])
                def _():
                    own_ref[...] = res

    def b_spec(i):
        held = _held(steps, i)
        return pl.BlockSpec((1, t, tn), lambda s, slot: (_lut(s, [h[0] for h in held]), 0,
                                                         _lut(s, [h[1] for h in held])))

    return pl.pallas_call(
        body, name=name,
        grid_spec=pltpu.PrefetchScalarGridSpec(
            num_scalar_prefetch=1, grid=(len(steps),),
            in_specs=[pl.BlockSpec((t, dw), lambda s, slot: (0, 0))] + [b_spec(i) for i in range(n_arr)],
            out_specs=[pl.BlockSpec((1, dw, tn), lambda s, slot: (_lut(s, blks), 0, 0)),
                       pl.BlockSpec((dw, tn), lambda s, slot: (0, 0))],
            scratch_shapes=[pltpu.VMEM((dw, t), BF16)]),
        out_shape=[SDS((N_DEV, dw, tn), BF16), SDS((dw, tn), F32)],
        compiler_params=_cparams(),
    )(my_slot, a, *bs)


def _weight_grad_rows(name, my_slot, a, b):
    t, dw = a.shape
    n_o, _, c = b.shape
    rows = dw // N_DEV

    def body(slot_ref, a_ref, b_ref, o_ref, own_ref):
        at = a_ref[...].T
        res = [_dot(at, b_ref[o].astype(BF16)) for o in range(n_o)]
        for o in range(n_o):
            o_ref[0, :, o * c:(o + 1) * c] = res[o].astype(BF16)

        @pl.when(pl.program_id(0) == slot_ref[0])
        def _():
            for o in range(n_o):
                own_ref[:, o * c:(o + 1) * c] = res[o]

    return pl.pallas_call(
        body, name=name,
        grid_spec=pltpu.PrefetchScalarGridSpec(
            num_scalar_prefetch=1, grid=(N_DEV,),
            in_specs=[pl.BlockSpec((t, rows), lambda s, slot: (0, s)),
                      pl.BlockSpec((n_o, t, c), lambda s, slot: (0, 0, 0))],
            out_specs=[pl.BlockSpec((1, rows, n_o * c), lambda s, slot: (s, 0, 0)),
                       pl.BlockSpec((rows, n_o * c), lambda s, slot: (0, 0))]),
        out_shape=[SDS((N_DEV, rows, n_o * c), BF16), SDS((rows, n_o * c), F32)],
        compiler_params=_cparams(),
    )(my_slot, a, b)


def _lane_lo():
    return lax.broadcasted_iota(jnp.int32, (1, 128), 1) < HEAD_DIM


def _offset_sums(gt):
    keys = gt.shape[1]
    gc = gt[0:CHUNK]
    for cc in range(1, gt.shape[0] // CHUNK):
        gc = gc + pltpu.roll(gt[cc * CHUNK:(cc + 1) * CHUNK], keys - cc * CHUNK, 1)
    hi = gc.astype(BF16)
    lo = (gc - hi.astype(F32)).astype(BF16)
    flip = (lax.broadcasted_iota(jnp.int32, (CHUNK, CHUNK), 0)
            + lax.broadcasted_iota(jnp.int32, (CHUNK, CHUNK), 1) == CHUNK - 1).astype(BF16)
    gf = _dot(flip, hi) + _dot(flip, lo)
    skew = pltpu.roll(gf, 0, 1, stride=1, stride_axis=0)
    return jnp.sum(skew, axis=0, keepdims=True)


def _band_bias(w_row, band, rows):
    keys = w_row.shape[1]
    base = jnp.broadcast_to(w_row, (CHUNK, keys))
    skew = pltpu.roll(base, 0, 1, stride=1, stride_axis=0)
    skew = pltpu.roll(skew, keys - (CHUNK - 1), 1)
    col = lax.broadcasted_iota(jnp.int32, (CHUNK, keys), 1)
    chunk0 = jnp.where(col < band, skew, NEG)
    return jnp.concatenate(
        [chunk0] + [pltpu.roll(chunk0, cc * CHUNK, 1) for cc in range(1, rows // CHUNK)], axis=0)


def _silu_parts(g):
    sg = _sigmoid(g)
    return g * sg, sg * (1.0 + g * (1.0 - sg))


A_PAIRS = 2
A_LANES = 128 * A_PAIRS
A_STEPS = D_MODEL // A_LANES


def _a_specs():
    q = pl.BlockSpec((QBLK, A_LANES), lambda p, j: (j, p))
    ks = [pl.BlockSpec((QBLK, A_LANES), lambda p, j, b=b: (jnp.maximum(j - 2 + b, 0), A_STEPS + p)) for b in range(3)]
    vs = [pl.BlockSpec((QBLK, A_LANES), lambda p, j, b=b: (jnp.maximum(j - 2 + b, 0), 2 * A_STEPS + p))
          for b in range(3)]
    g = pl.BlockSpec((QBLK, A_LANES), lambda p, j: (j, 3 * A_STEPS + p))
    bias = pl.BlockSpec((A_PAIRS, 8, A_KEYS), lambda p, j: (p, 0, 0))
    return q, ks, vs, g, bias


def _a_fill_bias(w_ref, b_ref, j):
    _fill_bias(2 * A_PAIRS, lambda h: w_ref[h // 2, h % 2:h % 2 + 1, :], A_BAND, QBLK * (2 - j), 2, b_ref, j)


def _fill_bias(n, get_row, band, first_valid_col, early, bias_scr, j):
    @pl.when(j == 0)
    def _():
        for h in range(n):
            bias_scr[h] = _band_bias(get_row(h), band, bias_scr.shape[1])

    @pl.when(j < early)
    def _():
        keys = bias_scr.shape[2]
        col_ok = lax.broadcasted_iota(jnp.int32, (1, keys), 1) >= first_valid_col
        for h in range(n):
            bias_scr[n + h] = jnp.where(col_ok, bias_scr[h], NEG)


def _head_logits(q, k, bias_scr, idx, sel):
    qm = jnp.where(sel, q, jnp.zeros_like(q)) * SCALE
    return qm, _dot_nt(qm, k) + bias_scr[idx]


def _row_sums_everywhere(r, sel):
    return jnp.where(sel, pltpu.roll(r, HEAD_DIM, 1), r)


def _own_everywhere(x, sel):
    return jnp.where(sel, x, pltpu.roll(x, HEAD_DIM, 1))


def _minus_rows(s, row_full):
    return jnp.concatenate([s[:, i:i + 128] - row_full for i in range(0, s.shape[1], 128)], axis=1)


def _attn_a_fwd(qkvg, bias, gather):
    t = qkvg.shape[0]
    nq = t // QBLK
    n_g = len(gather)
    q_spec, k_specs, v_specs, g_spec, bias_spec = _a_specs()

    def body(q_ref, k0, k1, k2, v0, v1, v2, g_ref, w_ref, *rest):
        shard_refs, rest = rest[:n_g], rest[n_g:]
        z_ref, o_ref, lse_ref = rest[:3]
        full_refs, (b_ref, *comm) = rest[3:3 + n_g], rest[3 + n_g:]
        p = pl.program_id(0)
        j = pl.program_id(1)
        start, forward, finish = _gather_phases(shard_refs, full_refs, *comm)
        pl.when(jnp.logical_and(p == 0, j == 0))(start)
        pl.when(jnp.logical_and(p == A_STEPS // 2, j == 0))(forward)
        _a_fill_bias(w_ref, b_ref, j)
        early = (j < 2).astype(jnp.int32)
        lane_lo = _lane_lo()
        for pp in range(A_PAIRS):
            cols = slice(128 * pp, 128 * (pp + 1))
            q = q_ref[:, cols]
            k = jnp.concatenate([k0[:, cols], k1[:, cols], k2[:, cols]], axis=0)
            v = jnp.concatenate([v0[:, cols], v1[:, cols], v2[:, cols]], axis=0)
            outs, lses = [], []
            for hh in range(2):
                sel = lane_lo if hh == 0 else jnp.logical_not(lane_lo)
                _, s = _head_logits(q, k, b_ref, 2 * pp + hh + 2 * A_PAIRS * early, sel)
                mx = jnp.max(s, axis=-1, keepdims=True)
                e = jnp.exp(s - mx).astype(BF16)
                r = _dot(e, jnp.where(sel, v, jnp.ones_like(v)))
                l = _row_sums_everywhere(r, sel)
                outs.append(r / l)
                lses.append(mx + jnp.log(l))
            o = jnp.where(lane_lo, outs[0], outs[1])
            silu, _ = _silu_parts(g_ref[:, cols].astype(F32))
            o_ref[:, cols] = o.astype(BF16)
            z_ref[:, cols] = (o * silu).astype(BF16)
            lse_ref[:, cols] = jnp.where(lane_lo, lses[0], lses[1])
        pl.when(jnp.logical_and(p == A_STEPS - 1, j == nq - 1))(finish)

    out_spec = pl.BlockSpec((QBLK, A_LANES), lambda p, j: (j, p))
    outs = pl.pallas_call(
        body, name="attn_a_fwd", grid=(A_STEPS, nq),
        in_specs=[q_spec, *k_specs, *v_specs, g_spec, bias_spec] + [ANY] * n_g,
        out_specs=[out_spec, out_spec, out_spec] + [ANY] * n_g,
        out_shape=[SDS((t, D_MODEL), BF16), SDS((t, D_MODEL), BF16), SDS((t, D_MODEL), F32)]
        + [SDS((N_DEV, *s.shape), s.dtype) for s in gather],
        scratch_shapes=[pltpu.VMEM((4 * A_PAIRS, QBLK, A_KEYS), F32)] + _gather_scratch(n_g),
        compiler_params=_cparams(),
    )(qkvg, qkvg, qkvg, qkvg, qkvg, qkvg, qkvg, qkvg, bias, *gather)
    return outs[0], outs[1], outs[2], list(outs[3:])


def _attn_a_bwd(qkvg, bias, out_a, lse, dz, scatter):
    t = qkvg.shape[0]
    nq = t // QBLK
    n_sc = len(scatter)
    q_spec, k_specs, v_specs, g_spec, bias_spec = _a_specs()

    def body(q_ref, k0, k1, k2, v0, v1, v2, g_ref, w_ref, o_ref, lse_ref, dz_ref, *rest):
        sc_refs, rest = rest[:n_sc], rest[n_sc:]
        dqg_ref, dkv_ref, dg_ref = rest[:3]
        land_refs, rest = rest[3:3 + n_sc], rest[3 + n_sc:]
        dk_acc, dv_acc, gt_acc, b_ref, send_sems, recv_sems = rest
        j = pl.program_id(1)
        first = jnp.logical_and(pl.program_id(0) == 0, j == 0)
        last = jnp.logical_and(pl.program_id(0) == A_STEPS - 1, j == nq - 1)

        @pl.when(first)
        def _():
            for cp in _scatter_copies(sc_refs, land_refs, send_sems, recv_sems):
                cp.start()

        _a_fill_bias(w_ref, b_ref, j)

        @pl.when(j == 0)
        def _():
            dk_acc[...] = jnp.zeros_like(dk_acc)
            dv_acc[...] = jnp.zeros_like(dv_acc)
            gt_acc[...] = jnp.zeros_like(gt_acc)

        early = (j < 2).astype(jnp.int32)
        lane_lo = _lane_lo()
        for pp in range(A_PAIRS):
            cols = slice(128 * pp, 128 * (pp + 1))
            q = q_ref[:, cols]
            k = jnp.concatenate([k0[:, cols], k1[:, cols], k2[:, cols]], axis=0)
            v = jnp.concatenate([v0[:, cols], v1[:, cols], v2[:, cols]], axis=0)
            o = o_ref[:, cols].astype(F32)
            lse_pair = lse_ref[:, cols]
            dzf = dz_ref[:, cols].astype(F32)
            silu, dsilu = _silu_parts(g_ref[:, cols].astype(F32))
            do = dzf * silu
            dqg_ref[1, :, cols] = (dzf * o * dsilu).astype(BF16)
            doo = do * o
            dqs = []
            dk_blk = jnp.zeros((A_KEYS, 128), F32)
            dv_blk = jnp.zeros((A_KEYS, 128), F32)
            for hh in range(2):
                sel = lane_lo if hh == 0 else jnp.logical_not(lane_lo)
                qm, s = _head_logits(q, k, b_ref, 2 * pp + hh + 2 * A_PAIRS * early, sel)
                p = jnp.exp(_minus_rows(s, _own_everywhere(lse_pair, sel)))
                delta = jnp.sum(jnp.where(sel, doo, 0.0), axis=-1, keepdims=True)
                dom = jnp.where(sel, do, 0.0).astype(BF16)
                dp = _dot_nt(dom, v)
                ds = p * (dp - delta)
                gt_acc[2 * pp + hh] += ds
                dsb = ds.astype(BF16)
                dqs.append(_dot(dsb, k) * SCALE)
                dk_blk = dk_blk + _dot_tn(dsb, qm)
                dv_blk = dv_blk + _dot_tn(p.astype(BF16), dom)
            dqg_ref[0, :, cols] = jnp.where(lane_lo, dqs[0], dqs[1]).astype(BF16)
            for b in range(3):
                @pl.when(j - 2 + b >= 0)
                def _(b=b, cols=cols, dk_blk=dk_blk, dv_blk=dv_blk):
                    rows = pl.ds(pl.multiple_of((j - 2 + b) * QBLK, QBLK), QBLK)
                    dk_acc[rows, cols] += dk_blk[b * QBLK:(b + 1) * QBLK]
                    dv_acc[rows, cols] += dv_blk[b * QBLK:(b + 1) * QBLK]

        @pl.when(j == nq - 1)
        def _():
            dkv_ref[0] = dk_acc[...].astype(BF16)
            dkv_ref[1] = dv_acc[...].astype(BF16)
            for pp in range(A_PAIRS):
                dg_ref[pp] = jnp.concatenate([_offset_sums(gt_acc[2 * pp]), _offset_sums(gt_acc[2 * pp + 1]),
                                              jnp.zeros((6, A_DIAG), F32)], axis=0)

        @pl.when(last)
        def _():
            for cp in _scatter_copies(sc_refs, land_refs, send_sems, recv_sems):
                cp.wait()

    blk = pl.BlockSpec((QBLK, A_LANES), lambda p, j: (j, p))
    outs = pl.pallas_call(
        body, name="attn_a_bwd", grid=(A_STEPS, nq),
        in_specs=[q_spec, *k_specs, *v_specs, g_spec, bias_spec, blk, blk, blk] + [ANY] * n_sc,
        out_specs=[pl.BlockSpec((2, QBLK, A_LANES), lambda p, j: (0, j, p)),
                   pl.BlockSpec((2, t, A_LANES), lambda p, j: (0, 0, p)),
                   pl.BlockSpec((A_PAIRS, 8, A_DIAG), lambda p, j: (p, 0, 0))] + [ANY] * n_sc,
        out_shape=[SDS((2, t, D_MODEL), BF16), SDS((2, t, D_MODEL), BF16), SDS((N_HEADS // 2, 8, A_DIAG), F32)]
        + [SDS((N_DEV - 1, *g.shape[1:]), g.dtype) for g in scatter],
        scratch_shapes=[pltpu.VMEM((t, A_LANES), F32), pltpu.VMEM((t, A_LANES), F32),
                        pltpu.VMEM((2 * A_PAIRS, QBLK, A_KEYS), F32), pltpu.VMEM((4 * A_PAIRS, QBLK, A_KEYS), F32),
                        pltpu.SemaphoreType.DMA(((N_DEV - 1) * n_sc,)),
                        pltpu.SemaphoreType.DMA(((N_DEV - 1) * n_sc,))],
        compiler_params=_cparams(),
    )(qkvg, qkvg, qkvg, qkvg, qkvg, qkvg, qkvg, qkvg, bias, out_a, lse, dz, *scatter)
    return outs[0], outs[1], outs[2], list(outs[3:])


def _b_specs(qblk):
    per = qblk // B_PREV
    q = pl.BlockSpec((qblk, 512), lambda h, j: (j, h))
    g = pl.BlockSpec((qblk, 512), lambda h, j: (j, 2 + h))
    kp = pl.BlockSpec((B_PREV, 128), lambda h, j: (jnp.maximum(per * j - 1, 0), 0))
    kc = pl.BlockSpec((qblk, 128), lambda h, j: (j, 0))
    vp = pl.BlockSpec((B_PREV, 128), lambda h, j: (jnp.maximum(per * j - 1, 0), 1))
    vc = pl.BlockSpec((qblk, 128), lambda h, j: (j, 1))
    bias = pl.BlockSpec((B_GROUP, qblk + B_PREV), lambda h, j: (h, 0))
    sinks = pl.BlockSpec(memory_space=pltpu.SMEM)
    return q, g, kp, kc, vp, vc, bias, sinks


def _b_operands(kp, kc, vp, vc, kvh):
    k = jnp.concatenate([kp[...], kc[...]], axis=0)
    v = jnp.concatenate([vp[...], vc[...]], axis=0)
    kr = pltpu.roll(k, HEAD_DIM, 1)
    vr = pltpu.roll(v, HEAD_DIM, 1)
    first = kvh == 0
    return (jnp.where(first, k, kr), jnp.where(first, kr, k),
            jnp.where(first, v, vr), jnp.where(first, vr, v))


def _attn_b_fwd(qg, kv, bias, sinks):
    t = qg.shape[0]
    qblk = B_QBLK_FWD
    q_spec, g_spec, kp_spec, kc_spec, vp_spec, vc_spec, bias_spec, sink_spec = _b_specs(qblk)

    def body(q_ref, g_ref, kp, kc, vp, vc, w_ref, sink_ref, z_ref, o_ref, lse_ref, b_ref):
        kvh = pl.program_id(0)
        j = pl.program_id(1)
        _fill_bias(B_GROUP, lambda h: w_ref[h:h + 1, :], B_BAND, B_PREV, 1, b_ref, j)
        early = (j < 1).astype(jnp.int32)
        lane_lo = _lane_lo()
        k_lo, k_hi, v_lo, v_hi = _b_operands(kp, kc, vp, vc, kvh)
        for pp in range(B_GROUP // 2):
            cols = slice(128 * pp, 128 * (pp + 1))
            qp = q_ref[:, cols]
            outs, lses = [], []
            for hh in range(2):
                g = 2 * pp + hh
                sel = lane_lo if hh == 0 else jnp.logical_not(lane_lo)
                sink = sink_ref[kvh * B_GROUP + g]
                vv = v_lo if hh == 0 else v_hi
                _, s = _head_logits(qp, k_lo if hh == 0 else k_hi, b_ref, g + B_GROUP * early, sel)
                mx = jnp.maximum(jnp.max(s, axis=-1, keepdims=True), sink)
                e = jnp.exp(s - mx).astype(BF16)
                r = _dot(e, jnp.where(sel, vv, jnp.ones_like(vv)))
                l = _row_sums_everywhere(r, sel) + jnp.exp(sink - mx)
                outs.append(r / l)
                lses.append(mx + jnp.log(l))
            o = jnp.where(lane_lo, outs[0], outs[1])
            silu, _ = _silu_parts(g_ref[:, cols].astype(F32))
            o_ref[:, cols] = o.astype(BF16)
            z_ref[:, cols] = (o * silu).astype(BF16)
            lse_ref[:, cols] = jnp.where(lane_lo, lses[0], lses[1])

    out_spec = pl.BlockSpec((qblk, 512), lambda h, j: (j, h))
    return pl.pallas_call(
        body, name="attn_b_fwd", grid=(B_KV_HEADS, t // qblk),
        in_specs=[q_spec, g_spec, kp_spec, kc_spec, vp_spec, vc_spec, bias_spec, sink_spec],
        out_specs=[out_spec, out_spec, out_spec],
        out_shape=[SDS((t, D_MODEL), BF16), SDS((t, D_MODEL), BF16), SDS((t, D_MODEL), F32)],
        scratch_shapes=[pltpu.VMEM((2 * B_GROUP, qblk, qblk + B_PREV), F32)],
        compiler_params=_cparams(),
    )(qg, qg, kv, kv, kv, kv, bias, sinks)


def _attn_b_bwd(qg, kv, bias, sinks, out_b, lse, dz, bucket_onehot):
    t = qg.shape[0]
    qblk = B_QBLK_BWD
    keys = qblk + B_PREV
    nq = t // qblk
    q_spec, g_spec, kp_spec, kc_spec, vp_spec, vc_spec, bias_spec, sink_spec = _b_specs(qblk)

    def body(q_ref, g_ref, kp, kc, vp, vc, w_ref, sink_ref, o_ref, lse_ref, dz_ref, oh_ref,
             dqg_ref, dkv_ref, dt5_ref, dsink_ref, gt_acc, b_ref):
        kvh = pl.program_id(0)
        j = pl.program_id(1)
        _fill_bias(B_GROUP, lambda h: w_ref[h:h + 1, :], B_BAND, B_PREV, 1, b_ref, j)

        @pl.when(jnp.logical_and(kvh == 0, j == 0))
        def _():
            dkv_ref[...] = jnp.zeros_like(dkv_ref)

        @pl.when(j == 0)
        def _():
            gt_acc[...] = jnp.zeros_like(gt_acc)
            dsink_ref[...] = jnp.zeros_like(dsink_ref)

        early = (j < 1).astype(jnp.int32)
        lane_lo = _lane_lo()
        k_lo, k_hi, v_lo, v_hi = _b_operands(kp, kc, vp, vc, kvh)
        dk_blk = jnp.zeros((keys, 128), F32)
        dv_blk = jnp.zeros((keys, 128), F32)
        for pp in range(B_GROUP // 2):
            cols = slice(128 * pp, 128 * (pp + 1))
            qp = q_ref[:, cols]
            o = o_ref[:, cols].astype(F32)
            lse_pair = lse_ref[:, cols]
            dzf = dz_ref[:, cols].astype(F32)
            silu, dsilu = _silu_parts(g_ref[:, cols].astype(F32))
            do = dzf * silu
            dqg_ref[1, :, cols] = (dzf * o * dsilu).astype(BF16)
            doo = do * o
            dqs = []
            for hh in range(2):
                g = 2 * pp + hh
                sel = lane_lo if hh == 0 else jnp.logical_not(lane_lo)
                sink = sink_ref[kvh * B_GROUP + g]
                kk = k_lo if hh == 0 else k_hi
                vv = v_lo if hh == 0 else v_hi
                qm, s = _head_logits(qp, kk, b_ref, g + B_GROUP * early, sel)
                lse_h = _own_everywhere(lse_pair, sel)
                p = jnp.exp(_minus_rows(s, lse_h))
                delta = jnp.sum(jnp.where(sel, doo, 0.0), axis=-1, keepdims=True)
                dom = jnp.where(sel, do, 0.0).astype(BF16)
                dp = _dot_nt(dom, vv)
                ds = p * (dp - delta)
                gt_acc[g] += ds
                dsink_ref[g:g + 1, :] -= jnp.sum(jnp.exp(sink - lse_h) * delta, axis=0, keepdims=True)
                dsb = ds.astype(BF16)
                dqs.append(_dot(dsb, kk) * SCALE)
                dk_blk = dk_blk + _dot_tn(dsb, qm)
                dv_blk = dv_blk + _dot_tn(p.astype(BF16), dom)
            dqg_ref[0, :, cols] = jnp.where(lane_lo, dqs[0], dqs[1]).astype(BF16)
        mine = lane_lo == (kvh == 0)
        dk_add = jnp.where(mine, dk_blk + pltpu.roll(dk_blk, HEAD_DIM, 1), 0.0)
        dv_add = jnp.where(mine, dv_blk + pltpu.roll(dv_blk, HEAD_DIM, 1), 0.0)

        @pl.when(j >= 1)
        def _():
            rows = pl.ds(pl.multiple_of(j * qblk - B_PREV, B_PREV), B_PREV)
            dkv_ref[0, rows, :] += dk_add[0:B_PREV]
            dkv_ref[1, rows, :] += dv_add[0:B_PREV]

        rows = pl.ds(pl.multiple_of(j * qblk, qblk), qblk)
        dkv_ref[0, rows, :] += dk_add[B_PREV:keys]
        dkv_ref[1, rows, :] += dv_add[B_PREV:keys]

        @pl.when(j == nq - 1)
        def _():
            dd = jnp.concatenate([_offset_sums(gt_acc[g]) for g in range(B_GROUP)], axis=0)
            hi = dd.astype(BF16)
            lo = (dd - hi.astype(F32)).astype(BF16)
            dt5_ref[...] = _dot(hi, oh_ref[...]) + _dot(lo, oh_ref[...])

    blk = pl.BlockSpec((qblk, 512), lambda h, j: (j, h))
    return pl.pallas_call(
        body, name="attn_b_bwd", grid=(B_KV_HEADS, nq),
        in_specs=[q_spec, g_spec, kp_spec, kc_spec, vp_spec, vc_spec, bias_spec, sink_spec, blk, blk, blk,
                  pl.BlockSpec((keys, 128), lambda h, j: (0, 0))],
        out_specs=[pl.BlockSpec((2, qblk, 512), lambda h, j: (0, j, h)),
                   pl.BlockSpec((2, t, 128), lambda h, j: (0, 0, 0)),
                   pl.BlockSpec((B_GROUP, 128), lambda h, j: (h, 0)),
                   pl.BlockSpec((B_GROUP, 128), lambda h, j: (h, 0))],
        out_shape=[SDS((2, t, D_MODEL), BF16), SDS((2, t, 128), F32),
                   SDS((N_HEADS, 128), F32), SDS((N_HEADS, 128), F32)],
        scratch_shapes=[pltpu.VMEM((B_GROUP, qblk, keys), F32), pltpu.VMEM((2 * B_GROUP, qblk, keys), F32)],
        compiler_params=_cparams(),
    )(qg, qg, kv, kv, kv, kv, bias, sinks, out_b, lse, dz, bucket_onehot)


def _a_bias_by_offset(rel_bias):
    m = np.arange(A_DIAG)
    idx = np.clip(A_BAND - 1 - m, -A_REL_CLIP, A_REL_CLIP) + A_REL_CLIP
    by_head = rel_bias[idx].T.reshape(N_HEADS // 2, 2, A_DIAG)
    return jnp.concatenate([by_head, jnp.zeros((N_HEADS // 2, 6, A_DIAG), F32)], axis=1)


def _a_bias_grad(offset_sums):
    first = 319
    tail = jnp.sum(offset_sums[:, :first], axis=1)
    body = jnp.flip(offset_sums[:, first:first + 320], axis=1)
    body = body.at[:, -1].add(tail)
    full = jnp.concatenate([jnp.zeros((N_HEADS, 193), F32), body], axis=1)
    return full.T


def _t5_bucket(rel):
    nb = T5_BUCKETS // 2
    max_exact = nb // 2
    ret = jnp.where(rel > 0, nb, 0)
    n = jnp.abs(rel)
    nf = jnp.maximum(n, 1).astype(jnp.float32)
    large = max_exact + (jnp.log(nf / max_exact) / math.log(T5_MAX_DIST / max_exact)
                         * (nb - max_exact)).astype(jnp.int32)
    large = jnp.minimum(large, nb - 1)
    return ret + jnp.where(n < max_exact, n, large)


def _b_offset_buckets(keys):
    return _t5_bucket(jnp.arange(keys, dtype=jnp.int32) - (B_LEFT_CHUNKS * CHUNK + CHUNK - 1))


def _b_bias_by_offset(t5_table, keys):
    return t5_table[_b_offset_buckets(keys)].T


def _b_bucket_onehot(keys):
    return (_b_offset_buckets(keys)[:, None] == jnp.arange(128)[None, :]).astype(BF16)


def _local_step(my_slot, x, target, a_gain, w_in_a, rel_bias, late_shards, kv_gain, t5_table,
                b_gain, sinks, f_gain):
    a_bias = _a_bias_by_offset(rel_bias)
    b_bias_fwd = _b_bias_by_offset(t5_table, B_QBLK_FWD + B_PREV)
    b_bias_bwd = _b_bias_by_offset(t5_table, B_QBLK_BWD + B_PREV)
    sinks_flat = sinks.reshape(N_HEADS)

    xn, qkvg = _norm_matmul(x, a_gain, w_in_a)
    z_a, out_a, lse_a, (w_in_b, w_out_a, w_out_b, kv_w) = _attn_a_fwd(qkvg, a_bias, late_shards)
    w_out_a = w_out_a.reshape(D_MODEL, D_MODEL)
    w_out_b = w_out_b.reshape(D_MODEL, D_MODEL)
    kv_w = kv_w.reshape(D_MODEL, 2 * 128)
    h1, kvn, hb, kv, qg = _layer_a_out(x, z_a, w_out_a, kv_gain, b_gain, kv_w, w_in_b)
    z_b, out_b, lse_b = _attn_b_fwd(qg, kv, b_bias_fwd, sinks_flat)
    dh2, dh2b, dz_b, loss, d_fn = _layer_b_out_loss(h1, z_b, w_out_b, f_gain, target)

    dqg_b, dkv_b, d_t5, d_sink = _attn_b_bwd(qg, kv, b_bias_bwd, sinks_flat, out_b, lse_b, dz_b,
                                             _b_bucket_onehot(B_QBLK_BWD + B_PREV))
    dh1, dh1b, dz_a, d_bn, d_kn = _layer_b_in_bwd(dqg_b, dkv_b, w_in_b, kv_w, h1, dh2, b_gain, kv_gain, w_out_a)
    early = dict(
        b_w_out=_weight_grad_rows("grad_b_w_out", my_slot, z_b, dh2b[None]),
        b_w_in=_weight_grad_cols("grad_b_w_in", my_slot, hb, [dqg_b],
                                 [(0, o, c, 4 * o + c) for o in range(2) for c in range(4)], 256),
        kv_w=_weight_grad_rows("grad_kv_w", my_slot, kvn, dkv_b),
        a_w_out=_weight_grad_rows("grad_a_w_out", my_slot, z_a, dh1b[None]))
    dqg_a, dkv_a, d_rel, landed = _attn_a_bwd(qkvg, a_bias, out_a, lse_a, dz_a, [g[0] for g in early.values()])
    g_w_in_a = _weight_grad_cols(
        "grad_a_w_in", my_slot, xn, [dqg_a, dkv_a],
        [(0, 0, 0, 0), (0, 0, 1, 1), (1, 0, 0, 2), (1, 0, 1, 3), (1, 1, 0, 4), (1, 1, 1, 5), (0, 1, 0, 6), (0, 1, 1, 7)], 512)
    from_sibling, = _exchange_sibling([g_w_in_a[0]])
    x_i, y_i, c_i, chips = _place()
    del x_i, y_i
    forward_slots = jnp.stack([_slot(*chip, c_i) for chip in chips]).astype(jnp.int32)
    chip_sums = _pre_reduce("chip_sum_a_w_in", g_w_in_a[0], from_sibling, forward_slots)
    grad_x, d_an, from_chips = _layer_a_in_bwd(dqg_a, dkv_a, w_in_a, x, dh1, a_gain, chip_sums)

    matrices = {n: (g[1], [(land, 0, N_DEV - 1)]) for (n, g), land in zip(early.items(), landed)}
    matrices["a_w_in"] = (g_w_in_a[1], [(from_sibling, 3, 1), (from_chips, 0, 3)])
    small = dict(
        loss=loss, a_norm=d_an, a_rel_bias=_a_bias_grad(d_rel[:, :2].reshape(N_HEADS, A_DIAG)),
        kv_norm=d_kn, t5_bias=d_t5[:, :T5_BUCKETS].T, b_norm=d_bn,
        b_sinks=d_sink[:, 0].reshape(1, N_HEADS), final_norm=d_fn)
    return grad_x, small, matrices


def _place():
    x, y, c = lax.axis_index("x"), lax.axis_index("y"), lax.axis_index("c")
    chips = [(1 - x, y), (x, 1 - y), (1 - x, 1 - y)]
    return x, y, c, chips


def _slot(px, py, pc):
    return 4 * px + 2 * py + pc


ANY = pl.BlockSpec(memory_space=pl.ANY)


def _peer(x, y, c, k):
    return (x ^ (k >> 2), y ^ ((k >> 1) & 1), c ^ (k & 1))


def _scatter_copies(grad_refs, land_refs, send_sems, recv_sems):
    x, y, c, _ = _place()
    copies = []
    for t, (grad, land) in enumerate(zip(grad_refs, land_refs)):
        for k in range(1, N_DEV):
            peer = _peer(x, y, c, k)
            sem = (N_DEV - 1) * t + k - 1
            copies.append(pltpu.make_async_remote_copy(
                src_ref=grad.at[_slot(*peer)], dst_ref=land.at[k - 1],
                send_sem=send_sems.at[sem], recv_sem=recv_sems.at[sem],
                device_id=peer, device_id_type=MESH))
    return copies


def _gather_phases(ins, outs, send_sems, recv_sems, local_sems):
    n = len(ins)
    x, y, c, chips = _place()
    me, sibling = (x, y, c), (x, y, 1 - c)

    def copy(t, k, block, to, src=None):
        dst = outs[t].at[_slot(*block)]
        return pltpu.make_async_remote_copy(
            src_ref=dst if src is None else src, dst_ref=dst,
            send_sem=send_sems.at[7 * t + k], recv_sem=recv_sems.at[7 * t + k],
            device_id=to, device_id_type=MESH)

    def lists():
        mine = [pltpu.make_async_copy(ins[t], outs[t].at[_slot(*me)], local_sems.at[t]) for t in range(n)]
        first = []
        for t in range(n):
            first.append(copy(t, 0, me, sibling, src=ins[t]))
            first += [copy(t, 1 + j, me, (*chip, c), src=ins[t]) for j, chip in enumerate(chips)]
        passed = [copy(t, 4 + j, (*chip, c), sibling) for t in range(n) for j, chip in enumerate(chips)]
        return mine, first, passed

    def start():
        mine, first, _ = lists()
        for cp in mine + first:
            cp.start()

    def forward():
        _, _, passed = lists()
        for t in range(n):
            for j, chip in enumerate(chips):
                copy(t, 1 + j, (*chip, c), me).wait_recv()
                passed[3 * t + j].start()

    def finish():
        mine, first, passed = lists()
        for t in range(n):
            copy(t, 0, sibling, me).wait_recv()
            for j, chip in enumerate(chips):
                copy(t, 4 + j, (*chip, 1 - c), me).wait_recv()
        for cp in first + passed:
            cp.wait_send()
        for cp in mine:
            cp.wait()

    return start, forward, finish


def _gather_scratch(n):
    return [pltpu.SemaphoreType.DMA((7 * n,)), pltpu.SemaphoreType.DMA((7 * n,)), pltpu.SemaphoreType.DMA((n,))]


def _all_gather(shards):
    n = len(shards)

    def body(*refs):
        start, forward, finish = _gather_phases(refs[:n], refs[n:2 * n], *refs[2 * n:])
        start()
        forward()
        finish()

    return pl.pallas_call(
        body, name="all_gather_weights",
        in_specs=[ANY] * n, out_specs=[ANY] * n,
        out_shape=[SDS((N_DEV, *s.shape), s.dtype) for s in shards],
        scratch_shapes=_gather_scratch(n),
    )(*shards)


def _exchange_sibling(grads):
    n = len(grads)

    def body(*refs):
        ins, outs = refs[:n], refs[n:2 * n]
        send_sems, recv_sems = refs[2 * n:]
        x, y, c, chips = _place()
        sibling = (x, y, 1 - c)
        copies = []
        for t in range(n):
            blocks = [(*chip, 1 - c) for chip in chips] + [sibling]
            for k, block in enumerate(blocks):
                copies.append(pltpu.make_async_remote_copy(
                    src_ref=ins[t].at[_slot(*block)], dst_ref=outs[t].at[k],
                    send_sem=send_sems.at[4 * t + k], recv_sem=recv_sems.at[4 * t + k],
                    device_id=sibling, device_id_type=MESH))
        for cp in copies:
            cp.start()
        for cp in copies:
            cp.wait()

    return pl.pallas_call(
        body, name="grads_to_sibling",
        in_specs=[ANY] * n, out_specs=[ANY] * n,
        out_shape=[SDS((4, *g.shape[1:]), g.dtype) for g in grads],
        scratch_shapes=[pltpu.SemaphoreType.DMA((4 * n,)), pltpu.SemaphoreType.DMA((4 * n,))],
    )(*grads)


def _chip_copies(sums_ref, land_ref, send_sems, recv_sems):
    x, y, c, chips = _place()
    del x, y
    return [pltpu.make_async_remote_copy(
        src_ref=sums_ref.at[j], dst_ref=land_ref.at[j], send_sem=send_sems.at[j], recv_sem=recv_sems.at[j],
        device_id=(*chip, c), device_id_type=MESH) for j, chip in enumerate(chips)]


def _row_tile(rows):
    return min(rows, 256)


def _pre_reduce(name, g, from_sibling, slots):
    _, r, c = g.shape
    tr = _row_tile(r)

    def body(slots_ref, g_ref, s_ref, o_ref):
        del slots_ref
        o_ref[...] = (g_ref[...].astype(F32) + s_ref[...].astype(F32)).astype(BF16)

    return pl.pallas_call(
        body, name=name,
        grid_spec=pltpu.PrefetchScalarGridSpec(
            num_scalar_prefetch=1, grid=(3, r // tr),
            in_specs=[pl.BlockSpec((1, tr, c), lambda j, i, s: (s[j], i, 0)),
                      pl.BlockSpec((1, tr, c), lambda j, i, s: (j, i, 0))],
            out_specs=pl.BlockSpec((1, tr, c), lambda j, i, s: (j, i, 0))),
        out_shape=SDS((3, r, c), BF16),
        compiler_params=_cparams(),
    )(slots, g, from_sibling)


def _adamw(w, g, m, v):
    m2 = ADAM_B1 * m + (1.0 - ADAM_B1) * g
    v2 = ADAM_B2 * v + (1.0 - ADAM_B2) * jnp.square(g)
    m_hat = m2 / (1.0 - ADAM_B1 ** ADAM_STEP)
    v_hat = v2 / (1.0 - ADAM_B2 ** ADAM_STEP)
    delta = -ADAM_LR * (m_hat / (jnp.sqrt(v_hat) + ADAM_EPS) + ADAM_WD * w)
    return delta, m2, v2


def _reduce_adamw(name, own, partials, w, m, v):
    r, c = own.shape
    tr = _row_tile(r)
    n_p = len(partials)

    def body(own_ref, *rest):
        p_refs, (w_ref, m_ref, v_ref, grad_ref, d_ref, nm_ref, nv_ref) = rest[:n_p], rest[n_p:]
        grad = own_ref[...]
        for p_ref, (_, _, count) in zip(p_refs, partials):
            for j in range(count):
                grad = grad + p_ref[j].astype(F32)
        grad_ref[...] = grad
        d_ref[...], nm_ref[...], nv_ref[...] = _adamw(w_ref[...], grad, m_ref[...], v_ref[...])

    flat = pl.BlockSpec((tr, c), lambda i: (i, 0))
    return pl.pallas_call(
        body, name=name, grid=(r // tr,),
        in_specs=[flat] + [pl.BlockSpec((count, tr, c), lambda i, first=first, count=count: (first // count, i, 0))
                           for _, first, count in partials] + [flat, flat, flat],
        out_specs=[flat, flat, flat, flat],
        out_shape=[SDS((r, c), F32)] * 4,
        compiler_params=_cparams(),
    )(own, *[p[0] for p in partials], w, m, v)


VM = pl.BlockSpec(memory_space=pltpu.VMEM)


def _small_allreduce(parts):
    n = len(parts)

    def body(*refs):
        ins, outs, lands = refs[:n], refs[n:2 * n], refs[2 * n:3 * n]
        send_sems, recv_sems = refs[3 * n:]
        x, y, c, _ = _place()
        my_slot = _slot(x, y, c)
        copies = []
        for t in range(n):
            lands[t][my_slot] = ins[t][...]
            for k in range(1, N_DEV):
                sem = (N_DEV - 1) * t + k - 1
                copies.append(pltpu.make_async_remote_copy(
                    src_ref=ins[t], dst_ref=lands[t].at[my_slot],
                    send_sem=send_sems.at[sem], recv_sem=recv_sems.at[sem],
                    device_id=_peer(x, y, c, k), device_id_type=MESH))
        for cp in copies:
            cp.start()
        for t in range(n):
            for k in range(1, N_DEV):
                sem = (N_DEV - 1) * t + k - 1
                pltpu.make_async_remote_copy(
                    src_ref=ins[t], dst_ref=lands[t].at[_slot(*_peer(x, y, c, k))],
                    send_sem=send_sems.at[sem], recv_sem=recv_sems.at[sem],
                    device_id=(x, y, c), device_id_type=MESH).wait_recv()
        for cp in copies:
            cp.wait_send()
        for t in range(n):
            total = lands[t][0]
            for s in range(1, N_DEV):
                total = total + lands[t][s]
            outs[t][...] = total

    n_sems = (N_DEV - 1) * n
    return pl.pallas_call(
        body, name="small_allreduce",
        in_specs=[VM] * n, out_specs=[VM] * n, out_shape=[SDS(p.shape, F32) for p in parts],
        scratch_shapes=[pltpu.VMEM((N_DEV, *p.shape), F32) for p in parts]
        + [pltpu.SemaphoreType.DMA((n_sems,)), pltpu.SemaphoreType.DMA((n_sems,))],
    )(*parts)


def _small_adamw(my_slot, sums, ws, ms, vs):
    n = len(ws)

    def body(slot_ref, *refs):
        sum_refs, refs = refs[:n + 1], refs[n + 1:]
        w_refs, m_refs, v_refs, refs = refs[:n], refs[n:2 * n], refs[2 * n:3 * n], refs[3 * n:]
        g_refs, d_refs, nm_refs, nv_refs = refs[:n + 1], refs[n + 1:2 * n + 1], refs[2 * n + 1:3 * n + 1], refs[3 * n + 1:]
        for t in range(n + 1):
            if t == 0:
                g = sum_refs[0][:, pl.ds(pl.multiple_of(slot_ref[0] * 128, 128), 128)]
            else:
                g = sum_refs[t][...]
            g_refs[t][...] = g
            if t < n:
                d_refs[t][...], nm_refs[t][...], nv_refs[t][...] = _adamw(w_refs[t][...], g, m_refs[t][...], v_refs[t][...])

    shapes = [SDS(w.shape, F32) for w in ws]
    outs = pl.pallas_call(
        body, name="small_adamw",
        in_specs=[pl.BlockSpec(memory_space=pltpu.SMEM)] + [VM] * (4 * n + 1),
        out_specs=[VM] * (4 * n + 1),
        out_shape=shapes + [SDS(sums[-1].shape, F32)] + shapes * 3,
    )(my_slot, *sums, *ws, *ms, *vs)
    return outs[:n + 1], outs[n + 1:2 * n + 1], outs[2 * n + 1:3 * n + 1], outs[3 * n + 1:]


def kernel(x, a_norm, a_w_in, a_rel_bias, a_w_out, kv_norm, kv_w, t5_bias, b_norm, b_w_in, b_sinks, b_w_out, final_norm, loss_target, m_a_norm, m_a_w_in, m_a_rel_bias, m_a_w_out, m_kv_norm, m_kv_w, m_t5_bias, m_b_norm, m_b_w_in, m_b_sinks, m_b_w_out, m_final_norm, v_a_norm, v_a_w_in, v_a_rel_bias, v_a_w_out, v_kv_norm, v_kv_w, v_t5_bias, v_b_norm, v_b_w_in, v_b_sinks, v_b_w_out, v_final_norm):
    xi, yi, ci = lax.axis_index("x"), lax.axis_index("y"), lax.axis_index("c")
    my_slot = _slot(xi, yi, ci)

    w_in_a, a_gain = _all_gather([a_w_in[0].astype(BF16), a_norm])
    a_gain = a_gain.reshape(1, D_MODEL)

    slot_arr = jnp.reshape(my_slot, (1,)).astype(jnp.int32)
    late_shards = [b_w_in[0].astype(BF16), a_w_out[0].astype(BF16), b_w_out[0].astype(BF16), kv_w.astype(BF16)]
    grad_x, loc, matrices = _local_step(
        slot_arr, x[0], loss_target[0], a_gain, w_in_a, a_rel_bias[0], late_shards,
        kv_norm.reshape(1, D_MODEL), t5_bias, b_norm, b_sinks, final_norm.reshape(1, D_MODEL))

    shard_w = dict(a_w_in=a_w_in[0], b_w_in=b_w_in[0], a_w_out=a_w_out[0], b_w_out=b_w_out[0], kv_w=kv_w)
    shard_m = dict(a_w_in=m_a_w_in[0], b_w_in=m_b_w_in[0], a_w_out=m_a_w_out[0], b_w_out=m_b_w_out[0], kv_w=m_kv_w)
    shard_v = dict(a_w_in=v_a_w_in[0], b_w_in=v_b_w_in[0], a_w_out=v_a_w_out[0], b_w_out=v_b_w_out[0], kv_w=v_kv_w)
    big = {n: _reduce_adamw("adamw_" + n, own, partials, shard_w[n], shard_m[n], shard_v[n])
           for n, (own, partials) in matrices.items()}

    names = ("a_norm", "a_rel_bias", "kv_norm", "t5_bias", "b_norm", "b_sinks", "final_norm")
    row = lambda a: a.reshape(1, -1) if a.ndim == 1 else (a[0] if a.ndim == 3 else a)
    small_w = [row(a) for a in (a_norm, a_rel_bias, kv_norm, t5_bias, b_norm, b_sinks, final_norm)]
    small_m = [row(a) for a in (m_a_norm, m_a_rel_bias, m_kv_norm, m_t5_bias, m_b_norm, m_b_sinks, m_final_norm)]
    small_v = [row(a) for a in (v_a_norm, v_a_rel_bias, v_kv_norm, v_t5_bias, v_b_norm, v_b_sinks, v_final_norm)]
    sums = _small_allreduce([loc[n] for n in names] + [loc["loss"]])
    results = _small_adamw(slot_arr, sums, small_w, small_m, small_v)
    like = dict(a_norm=a_norm, a_rel_bias=a_rel_bias, kv_norm=kv_norm, t5_bias=t5_bias, b_norm=b_norm,
                b_sinks=b_sinks, final_norm=final_norm)
    sm = [{n: part[i].reshape(like[n].shape) for i, n in enumerate(names)} for part in results]
    loss = results[0][len(names)][0, 0]

    order = ("a_norm", "a_w_in", "a_rel_bias", "a_w_out", "kv_norm", "kv_w", "t5_bias", "b_norm",
             "b_w_in", "b_sinks", "b_w_out", "final_norm")
    lead = dict(a_w_in=True, b_w_in=True, a_w_out=True, b_w_out=True, kv_w=False)

    def pick(kind, name):
        if name in big:
            val = big[name][kind]
            return val[None] if lead[name] else val
        return sm[kind][name]

    outs = [loss, grad_x[None]]
    for kind in range(4):
        outs += [pick(kind, n) for n in order]
    return tuple(outs)
```

```python
import functools
import math

import numpy as np
import jax
import jax.numpy as jnp
from jax import lax
from jax.experimental import pallas as pl
from jax.experimental.pallas import tpu as pltpu

F32 = jnp.float32
BF16 = jnp.bfloat16
SDS = jax.ShapeDtypeStruct

D_MODEL = 1024
HEAD_DIM = 64
CHUNK = 64
N_HEADS = 16
RMS_EPS = 1e-6
A_LEFT_CHUNKS = 8
A_BAND = (A_LEFT_CHUNKS + 1) * CHUNK
A_REL_CLIP = 256
B_KV_HEADS = 2
B_GROUP = 8
B_LEFT_CHUNKS = 2
B_BAND = (B_LEFT_CHUNKS + 1) * CHUNK
T5_BUCKETS = 32
T5_MAX_DIST = 128
QBLK = 256
A_KEYS = 3 * QBLK
B_QBLK_FWD = 128
B_QBLK_BWD = 256
B_PREV = 128
A_DIAG = A_KEYS
NEG = -1e30
SCALE = HEAD_DIM ** -0.5
N_DEV = 8

ADAM_LR = 0.001
ADAM_B1 = 0.9
ADAM_B2 = 0.999
ADAM_EPS = 1e-08
ADAM_WD = 0.01
ADAM_STEP = 10

VMEM_LIMIT_BYTES = 56 * 1024 * 1024
MESH = pl.DeviceIdType.MESH


def _cparams():
    return pltpu.CompilerParams(vmem_limit_bytes=VMEM_LIMIT_BYTES)


def _dot(a, b):
    return jnp.dot(a, b, preferred_element_type=F32)


def _dot_nt(a, b):
    return lax.dot_general(a, b, (((1,), (1,)), ((), ())), preferred_element_type=F32)


def _dot_tn(a, b):
    return lax.dot_general(a, b, (((0,), (0,)), ((), ())), preferred_element_type=F32)


def _rstd(xf):
    return lax.rsqrt(jnp.mean(xf * xf, axis=-1, keepdims=True) + RMS_EPS)


def _sigmoid(x):
    return 1.0 / (1.0 + jnp.exp(-x))


def _norm_matmul(x, gain, w):
    t = x.shape[0]
    nb, _, tn = w.shape
    tm = min(t, 1024)

    def body(x_ref, g_ref, w_ref, xn_ref, o_ref):
        @pl.when(pl.program_id(1) == 0)
        def _():
            xf = x_ref[...]
            xn_ref[...] = ((xf * _rstd(xf)) * g_ref[...]).astype(BF16)

        o_ref[...] = _dot(xn_ref[...], w_ref[0]).astype(BF16)

    return pl.pallas_call(
        body, name="norm_matmul", grid=(t // tm, nb),
        in_specs=[pl.BlockSpec((tm, D_MODEL), lambda m, n: (m, 0)),
                  pl.BlockSpec((1, D_MODEL), lambda m, n: (0, 0)),
                  pl.BlockSpec((1, D_MODEL, tn), lambda m, n: (n, 0, 0))],
        out_specs=[pl.BlockSpec((tm, D_MODEL), lambda m, n: (m, 0)),
                   pl.BlockSpec((tm, tn), lambda m, n: (m, n))],
        out_shape=[SDS((t, D_MODEL), BF16), SDS((t, nb * tn), BF16)],
        compiler_params=_cparams(),
    )(x, gain, w)


def _layer_a_out(x, z, w_out, kv_gain, b_gain, kv_w, w_in_b):
    t = x.shape[0]
    tm = min(t, 512)
    nb, _, tn = w_in_b.shape

    def body(x_ref, z_ref, wo_ref, kvg_ref, bg_ref, kvw_ref, wb_ref,
             h1_ref, kvn_ref, hb_ref, kv_ref, qg_ref):
        h1 = x_ref[...] + _dot(z_ref[...], wo_ref[...])
        h1_ref[...] = h1
        y0 = h1 * _rstd(h1)
        kvn = (y0 * kvg_ref[...]).astype(BF16)
        hb = (y0 * bg_ref[...]).astype(BF16)
        kvn_ref[...] = kvn
        hb_ref[...] = hb
        kv_ref[...] = _dot(kvn, kvw_ref[...]).astype(BF16)
        for i in range(nb):
            qg_ref[:, i * tn:(i + 1) * tn] = _dot(hb, wb_ref[i]).astype(BF16)

    row = lambda m: (m, 0)
    fix2 = lambda m: (0, 0)
    return pl.pallas_call(
        body, name="layer_a_out", grid=(t // tm,),
        in_specs=[pl.BlockSpec((tm, D_MODEL), row), pl.BlockSpec((tm, D_MODEL), row),
                  pl.BlockSpec((D_MODEL, D_MODEL), fix2),
                  pl.BlockSpec((1, D_MODEL), fix2), pl.BlockSpec((1, D_MODEL), fix2),
                  pl.BlockSpec((D_MODEL, 256), fix2),
                  pl.BlockSpec((nb, D_MODEL, tn), lambda m: (0, 0, 0))],
        out_specs=[pl.BlockSpec((tm, D_MODEL), row), pl.BlockSpec((tm, D_MODEL), row),
                   pl.BlockSpec((tm, D_MODEL), row), pl.BlockSpec((tm, 256), row),
                   pl.BlockSpec((tm, nb * tn), row)],
        out_shape=[SDS((t, D_MODEL), F32), SDS((t, D_MODEL), BF16), SDS((t, D_MODEL), BF16),
                   SDS((t, 256), BF16), SDS((t, nb * tn), BF16)],
        compiler_params=_cparams(),
    )(x, z, w_out, kv_gain, b_gain, kv_w, w_in_b)


def _layer_b_out_loss(h1, z, w_out, f_gain, target):
    t = h1.shape[0]
    tm = min(t, 512)

    def body(h1_ref, z_ref, wo_ref, fg_ref, tgt_ref,
             dh2_ref, dh2b_ref, dz_ref, loss_ref, dfn_ref):
        @pl.when(pl.program_id(0) == 0)
        def _():
            loss_ref[...] = jnp.zeros_like(loss_ref)
            dfn_ref[...] = jnp.zeros_like(dfn_ref)

        h2 = h1_ref[...] + _dot(z_ref[...], wo_ref[...])
        r = _rstd(h2)
        yn = h2 * r
        fg = fg_ref[...]
        err = yn * fg - tgt_ref[...]
        loss_ref[...] += (0.5 / D_MODEL) * jnp.sum(err * err)
        dy = err * (1.0 / D_MODEL)
        dfn_ref[...] += jnp.sum(dy * yn, axis=0, keepdims=True)
        u = dy * fg
        dh2 = r * u - h2 * ((r * r * r) * jnp.mean(u * h2, axis=-1, keepdims=True))
        dh2_ref[...] = dh2
        dh2b = dh2.astype(BF16)
        dh2b_ref[...] = dh2b
        dz_ref[...] = _dot_nt(dh2b, wo_ref[...]).astype(BF16)

    row = lambda m: (m, 0)
    fix2 = lambda m: (0, 0)
    return pl.pallas_call(
        body, name="layer_b_out_loss", grid=(t // tm,),
        in_specs=[pl.BlockSpec((tm, D_MODEL), row), pl.BlockSpec((tm, D_MODEL), row),
                  pl.BlockSpec((D_MODEL, D_MODEL), fix2), pl.BlockSpec((1, D_MODEL), fix2),
                  pl.BlockSpec((tm, D_MODEL), row)],
        out_specs=[pl.BlockSpec((tm, D_MODEL), row), pl.BlockSpec((tm, D_MODEL), row),
                   pl.BlockSpec((tm, D_MODEL), row), pl.BlockSpec((1, 128), fix2),
                   pl.BlockSpec((1, D_MODEL), fix2)],
        out_shape=[SDS((t, D_MODEL), F32), SDS((t, D_MODEL), BF16), SDS((t, D_MODEL), BF16),
                   SDS((1, 128), F32), SDS((1, D_MODEL), F32)],
        compiler_params=_cparams(),
    )(h1, z, w_out, f_gain, target)


def _layer_b_in_bwd(dqg, dkv, w_in_b, kv_w, h1, dh2, b_gain, kv_gain, w_out_a):
    t = h1.shape[0]
    tm = min(t, 256)
    nb, _, tn = w_in_b.shape
    per = D_MODEL // tn

    def body(dqg_ref, dkv_ref, wb_ref, kvw_ref, h1_ref, dh2_ref, bg_ref, kvg_ref, wo_ref,
             dh1_ref, dh1b_ref, dz_ref, dbn_ref, dkn_ref):
        @pl.when(pl.program_id(0) == 0)
        def _():
            dbn_ref[...] = jnp.zeros_like(dbn_ref)
            dkn_ref[...] = jnp.zeros_like(dkn_ref)

        dhb = jnp.zeros((tm, D_MODEL), F32)
        for i in range(nb):
            blk = dqg_ref[i // per, :, (i % per) * tn:(i % per + 1) * tn]
            dhb = dhb + _dot_nt(blk, wb_ref[i])
        dkn = (_dot_nt(dkv_ref[0].astype(BF16), kvw_ref[:, 0:128])
               + _dot_nt(dkv_ref[1].astype(BF16), kvw_ref[:, 128:256]))
        h1 = h1_ref[...]
        r = _rstd(h1)
        xr = h1 * r
        dbn_ref[...] += jnp.sum(dhb * xr, axis=0, keepdims=True)
        dkn_ref[...] += jnp.sum(dkn * xr, axis=0, keepdims=True)
        u = dhb * bg_ref[...] + dkn * kvg_ref[...]
        dh1 = dh2_ref[...] + r * u - h1 * ((r * r * r) * jnp.mean(u * h1, axis=-1, keepdims=True))
        dh1_ref[...] = dh1
        dh1b = dh1.astype(BF16)
        dh1b_ref[...] = dh1b
        dz_ref[...] = _dot_nt(dh1b, wo_ref[...]).astype(BF16)

    row = lambda m: (m, 0)
    fix2 = lambda m: (0, 0)
    return pl.pallas_call(
        body, name="layer_b_in_bwd", grid=(t // tm,),
        in_specs=[pl.BlockSpec((2, tm, D_MODEL), lambda m: (0, m, 0)),
                  pl.BlockSpec((2, tm, 128), lambda m: (0, m, 0)),
                  pl.BlockSpec((nb, D_MODEL, tn), lambda m: (0, 0, 0)),
                  pl.BlockSpec((D_MODEL, 256), fix2),
                  pl.BlockSpec((tm, D_MODEL), row), pl.BlockSpec((tm, D_MODEL), row),
                  pl.BlockSpec((1, D_MODEL), fix2), pl.BlockSpec((1, D_MODEL), fix2),
                  pl.BlockSpec((D_MODEL, D_MODEL), fix2)],
        out_specs=[pl.BlockSpec((tm, D_MODEL), row), pl.BlockSpec((tm, D_MODEL), row),
                   pl.BlockSpec((tm, D_MODEL), row), pl.BlockSpec((1, D_MODEL), fix2),
                   pl.BlockSpec((1, D_MODEL), fix2)],
        out_shape=[SDS((t, D_MODEL), F32), SDS((t, D_MODEL), BF16), SDS((t, D_MODEL), BF16),
                   SDS((1, D_MODEL), F32), SDS((1, D_MODEL), F32)],
        compiler_params=_cparams(),
    )(dqg, dkv, w_in_b, kv_w, h1, dh2, b_gain, kv_gain, w_out_a)


def _layer_a_in_bwd(dqg, dkv, w_in_a, x, dh1, a_gain, chip_sums):
    t = x.shape[0]
    tm = min(t, 256)
    nb, _, tn = w_in_a.shape
    per = D_MODEL // tn

    def body(dqg_ref, dkv_ref, w_ref, x_ref, dh1_ref, ag_ref, sums_ref, dx_ref, dan_ref, land_ref,
             send_sems, recv_sems):
        @pl.when(pl.program_id(0) == 0)
        def _():
            dan_ref[...] = jnp.zeros_like(dan_ref)
            for cp in _chip_copies(sums_ref, land_ref, send_sems, recv_sems):
                cp.start()

        dxn = jnp.zeros((tm, D_MODEL), F32)
        for i in range(nb):
            part = i // per
            src = dqg_ref if part in (0, 3) else dkv_ref
            outer = {0: 0, 3: 1, 1: 0, 2: 1}[part]
            blk = src[outer, :, (i % per) * tn:(i % per + 1) * tn]
            dxn = dxn + _dot_nt(blk, w_ref[i])
        xf = x_ref[...]
        r = _rstd(xf)
        dan_ref[...] += jnp.sum(dxn * (xf * r), axis=0, keepdims=True)
        u = dxn * ag_ref[...]
        dx_ref[...] = dh1_ref[...] + r * u - xf * ((r * r * r) * jnp.mean(u * xf, axis=-1, keepdims=True))

        @pl.when(pl.program_id(0) == t // tm - 1)
        def _():
            for cp in _chip_copies(sums_ref, land_ref, send_sems, recv_sems):
                cp.wait()

    row = lambda m: (m, 0)
    fix2 = lambda m: (0, 0)
    return pl.pallas_call(
        body, name="layer_a_in_bwd", grid=(t // tm,),
        in_specs=[pl.BlockSpec((2, tm, D_MODEL), lambda m: (0, m, 0)),
                  pl.BlockSpec((2, tm, D_MODEL), lambda m: (0, m, 0)),
                  pl.BlockSpec((nb, D_MODEL, tn), lambda m: (0, 0, 0)),
                  pl.BlockSpec((tm, D_MODEL), row), pl.BlockSpec((tm, D_MODEL), row),
                  pl.BlockSpec((1, D_MODEL), fix2), ANY],
        out_specs=[pl.BlockSpec((tm, D_MODEL), row), pl.BlockSpec((1, D_MODEL), fix2), ANY],
        out_shape=[SDS((t, D_MODEL), F32), SDS((1, D_MODEL), F32), SDS(chip_sums.shape, chip_sums.dtype)],
        scratch_shapes=[pltpu.SemaphoreType.DMA((3,)), pltpu.SemaphoreType.DMA((3,))],
        compiler_params=_cparams(),
    )(dqg, dkv, w_in_a, x, dh1, a_gain, chip_sums)


def _lut(s, vals):
    r = jnp.int32(vals[0])
    for i in range(1, len(vals)):
        r = jnp.where(s == i, jnp.int32(vals[i]), r)
    return r


def _held(steps, i):
    seq, cur = [None] * len(steps), None
    for k in range(len(steps) - 1, -1, -1):
        if steps[k][0] == i:
            cur = steps[k][1:3]
        seq[k] = cur
    for k in range(len(steps)):
        cur = seq[k] = seq[k] if seq[k] is not None else cur
    return seq


def _weight_grad_cols(name, my_slot, a, bs, steps, tn):
    t, dw = a.shape
    n_arr = len(bs)
    which = [s[0] for s in steps]
    blks = [s[3] for s in steps]

    def body(slot_ref, a_ref, *rest):
        b_refs, (o_ref, own_ref, at_ref) = rest[:n_arr], rest[n_arr:]
        s = pl.program_id(0)

        @pl.when(s == 0)
        def _():
            at_ref[...] = a_ref[...].T

        for i in range(n_arr):
            @pl.when(_lut(s, which) == i)
            def _(i=i):
                res = _dot(at_ref[...], b_refs[i][0])
                o_ref[0] = res.astype(BF16)

                @pl.when(_lut(s, blks) == slot_ref[0])
                def _():
                    own_ref[...] = res

    def b_spec(i):
        held = _held(steps, i)
        return pl.BlockSpec((1, t, tn), lambda s, slot: (_lut(s, [h[0] for h in held]), 0,
                                                         _lut(s, [h[1] for h in held])))

    return pl.pallas_call(
        body, name=name,
        grid_spec=pltpu.PrefetchScalarGridSpec(
            num_scalar_prefetch=1, grid=(len(steps),),
            in_specs=[pl.BlockSpec((t, dw), lambda s, slot: (0, 0))] + [b_spec(i) for i in range(n_arr)],
            out_specs=[pl.BlockSpec((1, dw, tn), lambda s, slot: (_lut(s, blks), 0, 0)),
                       pl.BlockSpec((dw, tn), lambda s, slot: (0, 0))],
            scratch_shapes=[pltpu.VMEM((dw, t), BF16)]),
        out_shape=[SDS((N_DEV, dw, tn), BF16), SDS((dw, tn), F32)],
        compiler_params=_cparams(),
    )(my_slot, a, *bs)


def _weight_grad_rows(name, my_slot, a, b):
    t, dw = a.shape
    n_o, _, c = b.shape
    rows = dw // N_DEV

    def body(slot_ref, a_ref, b_ref, o_ref, own_ref):
        at = a_ref[...].T
        res = [_dot(at, b_ref[o].astype(BF16)) for o in range(n_o)]
        for o in range(n_o):
            o_ref[0, :, o * c:(o + 1) * c] = res[o].astype(BF16)

        @pl.when(pl.program_id(0) == slot_ref[0])
        def _():
            for o in range(n_o):
                own_ref[:, o * c:(o + 1) * c] = res[o]

    return pl.pallas_call(
        body, name=name,
        grid_spec=pltpu.PrefetchScalarGridSpec(
            num_scalar_prefetch=1, grid=(N_DEV,),
            in_specs=[pl.BlockSpec((t, rows), lambda s, slot: (0, s)),
                      pl.BlockSpec((n_o, t, c), lambda s, slot: (0, 0, 0))],
            out_specs=[pl.BlockSpec((1, rows, n_o * c), lambda s, slot: (s, 0, 0)),
                       pl.BlockSpec((rows, n_o * c), lambda s, slot: (0, 0))]),
        out_shape=[SDS((N_DEV, rows, n_o * c), BF16), SDS((rows, n_o * c), F32)],
        compiler_params=_cparams(),
    )(my_slot, a, b)


def _lane_lo():
    return lax.broadcasted_iota(jnp.int32, (1, 128), 1) < HEAD_DIM


def _offset_sums(gt):
    keys = gt.shape[1]
    gc = gt[0:CHUNK]
    for cc in range(1, gt.shape[0] // CHUNK):
        gc = gc + pltpu.roll(gt[cc * CHUNK:(cc + 1) * CHUNK], keys - cc * CHUNK, 1)
    hi = gc.astype(BF16)
    lo = (gc - hi.astype(F32)).astype(BF16)
    flip = (lax.broadcasted_iota(jnp.int32, (CHUNK, CHUNK), 0)
            + lax.broadcasted_iota(jnp.int32, (CHUNK, CHUNK), 1) == CHUNK - 1).astype(BF16)
    gf = _dot(flip, hi) + _dot(flip, lo)
    skew = pltpu.roll(gf, 0, 1, stride=1, stride_axis=0)
    return jnp.sum(skew, axis=0, keepdims=True)


def _band_bias(w_row, band, rows):
    keys = w_row.shape[1]
    base = jnp.broadcast_to(w_row, (CHUNK, keys))
    skew = pltpu.roll(base, 0, 1, stride=1, stride_axis=0)
    skew = pltpu.roll(skew, keys - (CHUNK - 1), 1)
    col = lax.broadcasted_iota(jnp.int32, (CHUNK, keys), 1)
    chunk0 = jnp.where(col < band, skew, NEG)
    return jnp.concatenate(
        [chunk0] + [pltpu.roll(chunk0, cc * CHUNK, 1) for cc in range(1, rows // CHUNK)], axis=0)


def _silu_parts(g):
    sg = _sigmoid(g)
    return g * sg, sg * (1.0 + g * (1.0 - sg))


A_PAIRS = 2
A_LANES = 128 * A_PAIRS
A_STEPS = D_MODEL // A_LANES


def _a_specs():
    q = pl.BlockSpec((QBLK, A_LANES), lambda p, j: (j, p))
    ks = [pl.BlockSpec((QBLK, A_LANES), lambda p, j, b=b: (jnp.maximum(j - 2 + b, 0), A_STEPS + p)) for b in range(3)]
    vs = [pl.BlockSpec((QBLK, A_LANES), lambda p, j, b=b: (jnp.maximum(j - 2 + b, 0), 2 * A_STEPS + p))
          for b in range(3)]
    g = pl.BlockSpec((QBLK, A_LANES), lambda p, j: (j, 3 * A_STEPS + p))
    bias = pl.BlockSpec((A_PAIRS, 8, A_KEYS), lambda p, j: (p, 0, 0))
    return q, ks, vs, g, bias


def _a_fill_bias(w_ref, b_ref, j):
    _fill_bias(2 * A_PAIRS, lambda h: w_ref[h // 2, h % 2:h % 2 + 1, :], A_BAND, QBLK * (2 - j), 2, b_ref, j)


def _fill_bias(n, get_row, band, first_valid_col, early, bias_scr, j):
    @pl.when(j == 0)
    def _():
        for h in range(n):
            bias_scr[h] = _band_bias(get_row(h), band, bias_scr.shape[1])

    @pl.when(j < early)
    def _():
        keys = bias_scr.shape[2]
        col_ok = lax.broadcasted_iota(jnp.int32, (1, keys), 1) >= first_valid_col
        for h in range(n):
            bias_scr[n + h] = jnp.where(col_ok, bias_scr[h], NEG)


def _head_logits(q, k, bias_scr, idx, sel):
    qm = jnp.where(sel, q, jnp.zeros_like(q)) * SCALE
    return qm, _dot_nt(qm, k) + bias_scr[idx]


def _row_sums_everywhere(r, sel):
    return jnp.where(sel, pltpu.roll(r, HEAD_DIM, 1), r)


def _own_everywhere(x, sel):
    return jnp.where(sel, x, pltpu.roll(x, HEAD_DIM, 1))


def _minus_rows(s, row_full):
    return jnp.concatenate([s[:, i:i + 128] - row_full for i in range(0, s.shape[1], 128)], axis=1)


def _attn_a_fwd(qkvg, bias, gather):
    t = qkvg.shape[0]
    nq = t // QBLK
    n_g = len(gather)
    q_spec, k_specs, v_specs, g_spec, bias_spec = _a_specs()

    def body(q_ref, k0, k1, k2, v0, v1, v2, g_ref, w_ref, *rest):
        shard_refs, rest = rest[:n_g], rest[n_g:]
        z_ref, o_ref, lse_ref = rest[:3]
        full_refs, (b_ref, *comm) = rest[3:3 + n_g], rest[3 + n_g:]
        p = pl.program_id(0)
        j = pl.program_id(1)
        start, forward, finish = _gather_phases(shard_refs, full_refs, *comm)
        pl.when(jnp.logical_and(p == 0, j == 0))(start)
        pl.when(jnp.logical_and(p == A_STEPS // 2, j == 0))(forward)
        _a_fill_bias(w_ref, b_ref, j)
        early = (j < 2).astype(jnp.int32)
        lane_lo = _lane_lo()
        for pp in range(A_PAIRS):
            cols = slice(128 * pp, 128 * (pp + 1))
            q = q_ref[:, cols]
            k = jnp.concatenate([k0[:, cols], k1[:, cols], k2[:, cols]], axis=0)
            v = jnp.concatenate([v0[:, cols], v1[:, cols], v2[:, cols]], axis=0)
            outs, lses = [], []
            for hh in range(2):
                sel = lane_lo if hh == 0 else jnp.logical_not(lane_lo)
                _, s = _head_logits(q, k, b_ref, 2 * pp + hh + 2 * A_PAIRS * early, sel)
                mx = jnp.max(s, axis=-1, keepdims=True)
                e = jnp.exp(s - mx).astype(BF16)
                r = _dot(e, jnp.where(sel, v, jnp.ones_like(v)))
                l = _row_sums_everywhere(r, sel)
                outs.append(r / l)
                lses.append(mx + jnp.log(l))
            o = jnp.where(lane_lo, outs[0], outs[1])
            silu, _ = _silu_parts(g_ref[:, cols].astype(F32))
            o_ref[:, cols] = o.astype(BF16)
            z_ref[:, cols] = (o * silu).astype(BF16)
            lse_ref[:, cols] = jnp.where(lane_lo, lses[0], lses[1])
        pl.when(jnp.logical_and(p == A_STEPS - 1, j == nq - 1))(finish)

    out_spec = pl.BlockSpec((QBLK, A_LANES), lambda p, j: (j, p))
    outs = pl.pallas_call(
        body, name="attn_a_fwd", grid=(A_STEPS, nq),
        in_specs=[q_spec, *k_specs, *v_specs, g_spec, bias_spec] + [ANY] * n_g,
        out_specs=[out_spec, out_spec, out_spec] + [ANY] * n_g,
        out_shape=[SDS((t, D_MODEL), BF16), SDS((t, D_MODEL), BF16), SDS((t, D_MODEL), F32)]
        + [SDS((N_DEV, *s.shape), s.dtype) for s in gather],
        scratch_shapes=[pltpu.VMEM((4 * A_PAIRS, QBLK, A_KEYS), F32)] + _gather_scratch(n_g),
        compiler_params=_cparams(),
    )(qkvg, qkvg, qkvg, qkvg, qkvg, qkvg, qkvg, qkvg, bias, *gather)
    return outs[0], outs[1], outs[2], list(outs[3:])


def _attn_a_bwd(qkvg, bias, out_a, lse, dz, scatter):
    t = qkvg.shape[0]
    nq = t // QBLK
    n_sc = len(scatter)
    q_spec, k_specs, v_specs, g_spec, bias_spec = _a_specs()

    def body(q_ref, k0, k1, k2, v0, v1, v2, g_ref, w_ref, o_ref, lse_ref, dz_ref, *rest):
        sc_refs, rest = rest[:n_sc], rest[n_sc:]
        dqg_ref, dkv_ref, dg_ref = rest[:3]
        land_refs, rest = rest[3:3 + n_sc], rest[3 + n_sc:]
        dk_acc, dv_acc, gt_acc, b_ref, send_sems, recv_sems = rest
        j = pl.program_id(1)
        first = jnp.logical_and(pl.program_id(0) == 0, j == 0)
        last = jnp.logical_and(pl.program_id(0) == A_STEPS - 1, j == nq - 1)

        @pl.when(first)
        def _():
            for cp in _scatter_copies(sc_refs, land_refs, send_sems, recv_sems):
                cp.start()

        _a_fill_bias(w_ref, b_ref, j)

        @pl.when(j == 0)
        def _():
            dk_acc[...] = jnp.zeros_like(dk_acc)
            dv_acc[...] = jnp.zeros_like(dv_acc)
            gt_acc[...] = jnp.zeros_like(gt_acc)

        early = (j < 2).astype(jnp.int32)
        lane_lo = _lane_lo()
        for pp in range(A_PAIRS):
            cols = slice(128 * pp, 128 * (pp + 1))
            q = q_ref[:, cols]
            k = jnp.concatenate([k0[:, cols], k1[:, cols], k2[:, cols]], axis=0)
            v = jnp.concatenate([v0[:, cols], v1[:, cols], v2[:, cols]], axis=0)
            o = o_ref[:, cols].astype(F32)
            lse_pair = lse_ref[:, cols]
            dzf = dz_ref[:, cols].astype(F32)
            silu, dsilu = _silu_parts(g_ref[:, cols].astype(F32))
            do = dzf * silu
            dqg_ref[1, :, cols] = (dzf * o * dsilu).astype(BF16)
            doo = do * o
            dqs = []
            dk_blk = jnp.zeros((A_KEYS, 128), F32)
            dv_blk = jnp.zeros((A_KEYS, 128), F32)
            for hh in range(2):
                sel = lane_lo if hh == 0 else jnp.logical_not(lane_lo)
                qm, s = _head_logits(q, k, b_ref, 2 * pp + hh + 2 * A_PAIRS * early, sel)
                p = jnp.exp(_minus_rows(s, _own_everywhere(lse_pair, sel)))
                delta = jnp.sum(jnp.where(sel, doo, 0.0), axis=-1, keepdims=True)
                dom = jnp.where(sel, do, 0.0).astype(BF16)
                dp = _dot_nt(dom, v)
                ds = p * (dp - delta)
                gt_acc[2 * pp + hh] += ds
                dsb = ds.astype(BF16)
                dqs.append(_dot(dsb, k) * SCALE)
                dk_blk = dk_blk + _dot_tn(dsb, qm)
                dv_blk = dv_blk + _dot_tn(p.astype(BF16), dom)
            dqg_ref[0, :, cols] = jnp.where(lane_lo, dqs[0], dqs[1]).astype(BF16)
            for b in range(3):
                @pl.when(j - 2 + b >= 0)
                def _(b=b, cols=cols, dk_blk=dk_blk, dv_blk=dv_blk):
                    rows = pl.ds(pl.multiple_of((j - 2 + b) * QBLK, QBLK), QBLK)
                    dk_acc[rows, cols] += dk_blk[b * QBLK:(b + 1) * QBLK]
                    dv_acc[rows, cols] += dv_blk[b * QBLK:(b + 1) * QBLK]

        @pl.when(j == nq - 1)
        def _():
            dkv_ref[0] = dk_acc[...].astype(BF16)
            dkv_ref[1] = dv_acc[...].astype(BF16)
            for pp in range(A_PAIRS):
                dg_ref[pp] = jnp.concatenate([_offset_sums(gt_acc[2 * pp]), _offset_sums(gt_acc[2 * pp + 1]),
                                              jnp.zeros((6, A_DIAG), F32)], axis=0)

        @pl.when(last)
        def _():
            for cp in _scatter_copies(sc_refs, land_refs, send_sems, recv_sems):
                cp.wait()

    blk = pl.BlockSpec((QBLK, A_LANES), lambda p, j: (j, p))
    outs = pl.pallas_call(
        body, name="attn_a_bwd", grid=(A_STEPS, nq),
        in_specs=[q_spec, *k_specs, *v_specs, g_spec, bias_spec, blk, blk, blk] + [ANY] * n_sc,
        out_specs=[pl.BlockSpec((2, QBLK, A_LANES), lambda p, j: (0, j, p)),
                   pl.BlockSpec((2, t, A_LANES), lambda p, j: (0, 0, p)),
                   pl.BlockSpec((A_PAIRS, 8, A_DIAG), lambda p, j: (p, 0, 0))] + [ANY] * n_sc,
        out_shape=[SDS((2, t, D_MODEL), BF16), SDS((2, t, D_MODEL), BF16), SDS((N_HEADS // 2, 8, A_DIAG), F32)]
        + [SDS((N_DEV - 1, *g.shape[1:]), g.dtype) for g in scatter],
        scratch_shapes=[pltpu.VMEM((t, A_LANES), F32), pltpu.VMEM((t, A_LANES), F32),
                        pltpu.VMEM((2 * A_PAIRS, QBLK, A_KEYS), F32), pltpu.VMEM((4 * A_PAIRS, QBLK, A_KEYS), F32),
                        pltpu.SemaphoreType.DMA(((N_DEV - 1) * n_sc,)),
                        pltpu.SemaphoreType.DMA(((N_DEV - 1) * n_sc,))],
        compiler_params=_cparams(),
    )(qkvg, qkvg, qkvg, qkvg, qkvg, qkvg, qkvg, qkvg, bias, out_a, lse, dz, *scatter)
    return outs[0], outs[1], outs[2], list(outs[3:])


def _b_specs(qblk):
    per = qblk // B_PREV
    q = pl.BlockSpec((qblk, 512), lambda h, j: (j, h))
    g = pl.BlockSpec((qblk, 512), lambda h, j: (j, 2 + h))
    kp = pl.BlockSpec((B_PREV, 128), lambda h, j: (jnp.maximum(per * j - 1, 0), 0))
    kc = pl.BlockSpec((qblk, 128), lambda h, j: (j, 0))
    vp = pl.BlockSpec((B_PREV, 128), lambda h, j: (jnp.maximum(per * j - 1, 0), 1))
    vc = pl.BlockSpec((qblk, 128), lambda h, j: (j, 1))
    bias = pl.BlockSpec((B_GROUP, qblk + B_PREV), lambda h, j: (h, 0))
    sinks = pl.BlockSpec(memory_space=pltpu.SMEM)
    return q, g, kp, kc, vp, vc, bias, sinks


def _b_operands(kp, kc, vp, vc, kvh):
    k = jnp.concatenate([kp[...], kc[...]], axis=0)
    v = jnp.concatenate([vp[...], vc[...]], axis=0)
    kr = pltpu.roll(k, HEAD_DIM, 1)
    vr = pltpu.roll(v, HEAD_DIM, 1)
    first = kvh == 0
    return (jnp.where(first, k, kr), jnp.where(first, kr, k),
            jnp.where(first, v, vr), jnp.where(first, vr, v))


def _attn_b_fwd(qg, kv, bias, sinks):
    t = qg.shape[0]
    qblk = B_QBLK_FWD
    q_spec, g_spec, kp_spec, kc_spec, vp_spec, vc_spec, bias_spec, sink_spec = _b_specs(qblk)

    def body(q_ref, g_ref, kp, kc, vp, vc, w_ref, sink_ref, z_ref, o_ref, lse_ref, b_ref):
        kvh = pl.program_id(0)
        j = pl.program_id(1)
        _fill_bias(B_GROUP, lambda h: w_ref[h:h + 1, :], B_BAND, B_PREV, 1, b_ref, j)
        early = (j < 1).astype(jnp.int32)
        lane_lo = _lane_lo()
        k_lo, k_hi, v_lo, v_hi = _b_operands(kp, kc, vp, vc, kvh)
        for pp in range(B_GROUP // 2):
            cols = slice(128 * pp, 128 * (pp + 1))
            qp = q_ref[:, cols]
            outs, lses = [], []
            for hh in range(2):
                g = 2 * pp + hh
                sel = lane_lo if hh == 0 else jnp.logical_not(lane_lo)
                sink = sink_ref[kvh * B_GROUP + g]
                vv = v_lo if hh == 0 else v_hi
                _, s = _head_logits(qp, k_lo if hh == 0 else k_hi, b_ref, g + B_GROUP * early, sel)
                mx = jnp.maximum(jnp.max(s, axis=-1, keepdims=True), sink)
                e = jnp.exp(s - mx).astype(BF16)
                r = _dot(e, jnp.where(sel, vv, jnp.ones_like(vv)))
                l = _row_sums_everywhere(r, sel) + jnp.exp(sink - mx)
                outs.append(r / l)
                lses.append(mx + jnp.log(l))
            o = jnp.where(lane_lo, outs[0], outs[1])
            silu, _ = _silu_parts(g_ref[:, cols].astype(F32))
            o_ref[:, cols] = o.astype(BF16)
            z_ref[:, cols] = (o * silu).astype(BF16)
            lse_ref[:, cols] = jnp.where(lane_lo, lses[0], lses[1])

    out_spec = pl.BlockSpec((qblk, 512), lambda h, j: (j, h))
    return pl.pallas_call(
        body, name="attn_b_fwd", grid=(B_KV_HEADS, t // qblk),
        in_specs=[q_spec, g_spec, kp_spec, kc_spec, vp_spec, vc_spec, bias_spec, sink_spec],
        out_specs=[out_spec, out_spec, out_spec],
        out_shape=[SDS((t, D_MODEL), BF16), SDS((t, D_MODEL), BF16), SDS((t, D_MODEL), F32)],
        scratch_shapes=[pltpu.VMEM((2 * B_GROUP, qblk, qblk + B_PREV), F32)],
        compiler_params=_cparams(),
    )(qg, qg, kv, kv, kv, kv, bias, sinks)


def _attn_b_bwd(qg, kv, bias, sinks, out_b, lse, dz, bucket_onehot):
    t = qg.shape[0]
    qblk = B_QBLK_BWD
    keys = qblk + B_PREV
    nq = t // qblk
    q_spec, g_spec, kp_spec, kc_spec, vp_spec, vc_spec, bias_spec, sink_spec = _b_specs(qblk)

    def body(q_ref, g_ref, kp, kc, vp, vc, w_ref, sink_ref, o_ref, lse_ref, dz_ref, oh_ref,
             dqg_ref, dkv_ref, dt5_ref, dsink_ref, gt_acc, b_ref):
        kvh = pl.program_id(0)
        j = pl.program_id(1)
        _fill_bias(B_GROUP, lambda h: w_ref[h:h + 1, :], B_BAND, B_PREV, 1, b_ref, j)

        @pl.when(jnp.logical_and(kvh == 0, j == 0))
        def _():
            dkv_ref[...] = jnp.zeros_like(dkv_ref)

        @pl.when(j == 0)
        def _():
            gt_acc[...] = jnp.zeros_like(gt_acc)
            dsink_ref[...] = jnp.zeros_like(dsink_ref)

        early = (j < 1).astype(jnp.int32)
        lane_lo = _lane_lo()
        k_lo, k_hi, v_lo, v_hi = _b_operands(kp, kc, vp, vc, kvh)
        dk_blk = jnp.zeros((keys, 128), F32)
        dv_blk = jnp.zeros((keys, 128), F32)
        for pp in range(B_GROUP // 2):
            cols = slice(128 * pp, 128 * (pp + 1))
            qp = q_ref[:, cols]
            o = o_ref[:, cols].astype(F32)
            lse_pair = lse_ref[:, cols]
            dzf = dz_ref[:, cols].astype(F32)
            silu, dsilu = _silu_parts(g_ref[:, cols].astype(F32))
            do = dzf * silu
            dqg_ref[1, :, cols] = (dzf * o * dsilu).astype(BF16)
            doo = do * o
            dqs = []
            for hh in range(2):
                g = 2 * pp + hh
                sel = lane_lo if hh == 0 else jnp.logical_not(lane_lo)
                sink = sink_ref[kvh * B_GROUP + g]
                kk = k_lo if hh == 0 else k_hi
                vv = v_lo if hh == 0 else v_hi
                qm, s = _head_logits(qp, kk, b_ref, g + B_GROUP * early, sel)
                lse_h = _own_everywhere(lse_pair, sel)
                p = jnp.exp(_minus_rows(s, lse_h))
                delta = jnp.sum(jnp.where(sel, doo, 0.0), axis=-1, keepdims=True)
                dom = jnp.where(sel, do, 0.0).astype(BF16)
                dp = _dot_nt(dom, vv)
                ds = p * (dp - delta)
                gt_acc[g] += ds
                dsink_ref[g:g + 1, :] -= jnp.sum(jnp.exp(sink - lse_h) * delta, axis=0, keepdims=True)
                dsb = ds.astype(BF16)
                dqs.append(_dot(dsb, kk) * SCALE)
                dk_blk = dk_blk + _dot_tn(dsb, qm)
                dv_blk = dv_blk + _dot_tn(p.astype(BF16), dom)
            dqg_ref[0, :, cols] = jnp.where(lane_lo, dqs[0], dqs[1]).astype(BF16)
        mine = lane_lo == (kvh == 0)
        dk_add = jnp.where(mine, dk_blk + pltpu.roll(dk_blk, HEAD_DIM, 1), 0.0)
        dv_add = jnp.where(mine, dv_blk + pltpu.roll(dv_blk, HEAD_DIM, 1), 0.0)

        @pl.when(j >= 1)
        def _():
            rows = pl.ds(pl.multiple_of(j * qblk - B_PREV, B_PREV), B_PREV)
            dkv_ref[0, rows, :] += dk_add[0:B_PREV]
            dkv_ref[1, rows, :] += dv_add[0:B_PREV]

        rows = pl.ds(pl.multiple_of(j * qblk, qblk), qblk)
        dkv_ref[0, rows, :] += dk_add[B_PREV:keys]
        dkv_ref[1, rows, :] += dv_add[B_PREV:keys]

        @pl.when(j == nq - 1)
        def _():
            dd = jnp.concatenate([_offset_sums(gt_acc[g]) for g in range(B_GROUP)], axis=0)
            hi = dd.astype(BF16)
            lo = (dd - hi.astype(F32)).astype(BF16)
            dt5_ref[...] = _dot(hi, oh_ref[...]) + _dot(lo, oh_ref[...])

    blk = pl.BlockSpec((qblk, 512), lambda h, j: (j, h))
    return pl.pallas_call(
        body, name="attn_b_bwd", grid=(B_KV_HEADS, nq),
        in_specs=[q_spec, g_spec, kp_spec, kc_spec, vp_spec, vc_spec, bias_spec, sink_spec, blk, blk, blk,
                  pl.BlockSpec((keys, 128), lambda h, j: (0, 0))],
        out_specs=[pl.BlockSpec((2, qblk, 512), lambda h, j: (0, j, h)),
                   pl.BlockSpec((2, t, 128), lambda h, j: (0, 0, 0)),
                   pl.BlockSpec((B_GROUP, 128), lambda h, j: (h, 0)),
                   pl.BlockSpec((B_GROUP, 128), lambda h, j: (h, 0))],
        out_shape=[SDS((2, t, D_MODEL), BF16), SDS((2, t, 128), F32),
                   SDS((N_HEADS, 128), F32), SDS((N_HEADS, 128), F32)],
        scratch_shapes=[pltpu.VMEM((B_GROUP, qblk, keys), F32), pltpu.VMEM((2 * B_GROUP, qblk, keys), F32)],
        compiler_params=_cparams(),
    )(qg, qg, kv, kv, kv, kv, bias, sinks, out_b, lse, dz, bucket_onehot)


def _a_bias_by_offset(rel_bias):
    m = np.arange(A_DIAG)
    idx = np.clip(A_BAND - 1 - m, -A_REL_CLIP, A_REL_CLIP) + A_REL_CLIP
    by_head = rel_bias[idx].T.reshape(N_HEADS // 2, 2, A_DIAG)
    return jnp.concatenate([by_head, jnp.zeros((N_HEADS // 2, 6, A_DIAG), F32)], axis=1)


def _a_bias_grad(offset_sums):
    first = 319
    tail = jnp.sum(offset_sums[:, :first], axis=1)
    body = jnp.flip(offset_sums[:, first:first + 320], axis=1)
    body = body.at[:, -1].add(tail)
    full = jnp.concatenate([jnp.zeros((N_HEADS, 193), F32), body], axis=1)
    return full.T


def _t5_bucket(rel):
    nb = T5_BUCKETS // 2
    max_exact = nb // 2
    ret = jnp.where(rel > 0, nb, 0)
    n = jnp.abs(rel)
    nf = jnp.maximum(n, 1).astype(jnp.float32)
    large = max_exact + (jnp.log(nf / max_exact) / math.log(T5_MAX_DIST / max_exact)
                         * (nb - max_exact)).astype(jnp.int32)
    large = jnp.minimum(large, nb - 1)
    return ret + jnp.where(n < max_exact, n, large)


def _b_offset_buckets(keys):
    return _t5_bucket(jnp.arange(keys, dtype=jnp.int32) - (B_LEFT_CHUNKS * CHUNK + CHUNK - 1))


def _b_bias_by_offset(t5_table, keys):
    return t5_table[_b_offset_buckets(keys)].T


def _b_bucket_onehot(keys):
    return (_b_offset_buckets(keys)[:, None] == jnp.arange(128)[None, :]).astype(BF16)


def _local_step(my_slot, x, target, a_gain, w_in_a, rel_bias, late_shards, kv_gain, t5_table,
                b_gain, sinks, f_gain):
    a_bias = _a_bias_by_offset(rel_bias)
    b_bias_fwd = _b_bias_by_offset(t5_table, B_QBLK_FWD + B_PREV)
    b_bias_bwd = _b_bias_by_offset(t5_table, B_QBLK_BWD + B_PREV)
    sinks_flat = sinks.reshape(N_HEADS)

    xn, qkvg = _norm_matmul(x, a_gain, w_in_a)
    z_a, out_a, lse_a, (w_in_b, w_out_a, w_out_b, kv_w) = _attn_a_fwd(qkvg, a_bias, late_shards)
    w_out_a = w_out_a.reshape(D_MODEL, D_MODEL)
    w_out_b = w_out_b.reshape(D_MODEL, D_MODEL)
    kv_w = kv_w.reshape(D_MODEL, 2 * 128)
    h1, kvn, hb, kv, qg = _layer_a_out(x, z_a, w_out_a, kv_gain, b_gain, kv_w, w_in_b)
    z_b, out_b, lse_b = _attn_b_fwd(qg, kv, b_bias_fwd, sinks_flat)
    dh2, dh2b, dz_b, loss, d_fn = _layer_b_out_loss(h1, z_b, w_out_b, f_gain, target)

    dqg_b, dkv_b, d_t5, d_sink = _attn_b_bwd(qg, kv, b_bias_bwd, sinks_flat, out_b, lse_b, dz_b,
                                             _b_bucket_onehot(B_QBLK_BWD + B_PREV))
    dh1, dh1b, dz_a, d_bn, d_kn = _layer_b_in_bwd(dqg_b, dkv_b, w_in_b, kv_w, h1, dh2, b_gain, kv_gain, w_out_a)
    early = dict(
        b_w_out=_weight_grad_rows("grad_b_w_out", my_slot, z_b, dh2b[None]),
        b_w_in=_weight_grad_cols("grad_b_w_in", my_slot, hb, [dqg_b],
                                 [(0, o, c, 4 * o + c) for o in range(2) for c in range(4)], 256),
        kv_w=_weight_grad_rows("grad_kv_w", my_slot, kvn, dkv_b),
        a_w_out=_weight_grad_rows("grad_a_w_out", my_slot, z_a, dh1b[None]))
    dqg_a, dkv_a, d_rel, landed = _attn_a_bwd(qkvg, a_bias, out_a, lse_a, dz_a, [g[0] for g in early.values()])
    g_w_in_a = _weight_grad_cols(
        "grad_a_w_in", my_slot, xn, [dqg_a, dkv_a],
        [(0, 0, 0, 0), (0, 0, 1, 1), (1, 0, 0, 2), (1, 0, 1, 3), (1, 1, 0, 4), (1, 1, 1, 5), (0, 1, 0, 6), (0, 1, 1, 7)], 512)
    from_sibling, = _exchange_sibling([g_w_in_a[0]])
    x_i, y_i, c_i, chips = _place()
    del x_i, y_i
    forward_slots = jnp.stack([_slot(*chip, c_i) for chip in chips]).astype(jnp.int32)
    chip_sums = _pre_reduce("chip_sum_a_w_in", g_w_in_a[0], from_sibling, forward_slots)
    grad_x, d_an, from_chips = _layer_a_in_bwd(dqg_a, dkv_a, w_in_a, x, dh1, a_gain, chip_sums)

    matrices = {n: (g[1], [(land, 0, N_DEV - 1)]) for (n, g), land in zip(early.items(), landed)}
    matrices["a_w_in"] = (g_w_in_a[1], [(from_sibling, 3, 1), (from_chips, 0, 3)])
    small = dict(
        loss=loss, a_norm=d_an, a_rel_bias=d_rel[:, :2].reshape(N_HEADS, A_DIAG),
        kv_norm=d_kn, t5_bias=d_t5, b_norm=d_bn, b_sinks=d_sink, final_norm=d_fn)
    return grad_x, small, matrices


def _place():
    x, y, c = lax.axis_index("x"), lax.axis_index("y"), lax.axis_index("c")
    chips = [(1 - x, y), (x, 1 - y), (1 - x, 1 - y)]
    return x, y, c, chips


def _slot(px, py, pc):
    return 4 * px + 2 * py + pc


ANY = pl.BlockSpec(memory_space=pl.ANY)


def _peer(x, y, c, k):
    return (x ^ (k >> 2), y ^ ((k >> 1) & 1), c ^ (k & 1))


def _scatter_copies(grad_refs, land_refs, send_sems, recv_sems):
    x, y, c, _ = _place()
    copies = []
    for t, (grad, land) in enumerate(zip(grad_refs, land_refs)):
        for k in range(1, N_DEV):
            peer = _peer(x, y, c, k)
            sem = (N_DEV - 1) * t + k - 1
            copies.append(pltpu.make_async_remote_copy(
                src_ref=grad.at[_slot(*peer)], dst_ref=land.at[k - 1],
                send_sem=send_sems.at[sem], recv_sem=recv_sems.at[sem],
                device_id=peer, device_id_type=MESH))
    return copies


def _gather_phases(ins, outs, send_sems, recv_sems, local_sems):
    n = len(ins)
    x, y, c, chips = _place()
    me, sibling = (x, y, c), (x, y, 1 - c)

    def copy(t, k, block, to, src=None):
        dst = outs[t].at[_slot(*block)]
        return pltpu.make_async_remote_copy(
            src_ref=dst if src is None else src, dst_ref=dst,
            send_sem=send_sems.at[7 * t + k], recv_sem=recv_sems.at[7 * t + k],
            device_id=to, device_id_type=MESH)

    def lists():
        mine = [pltpu.make_async_copy(ins[t], outs[t].at[_slot(*me)], local_sems.at[t]) for t in range(n)]
        first = []
        for t in range(n):
            first.append(copy(t, 0, me, sibling, src=ins[t]))
            first += [copy(t, 1 + j, me, (*chip, c), src=ins[t]) for j, chip in enumerate(chips)]
        passed = [copy(t, 4 + j, (*chip, c), sibling) for t in range(n) for j, chip in enumerate(chips)]
        return mine, first, passed

    def start():
        mine, first, _ = lists()
        for cp in mine + first:
            cp.start()

    def forward():
        _, _, passed = lists()
        for t in range(n):
            for j, chip in enumerate(chips):
                copy(t, 1 + j, (*chip, c), me).wait_recv()
                passed[3 * t + j].start()

    def finish():
        mine, first, passed = lists()
        for t in range(n):
            copy(t, 0, sibling, me).wait_recv()
            for j, chip in enumerate(chips):
                copy(t, 4 + j, (*chip, 1 - c), me).wait_recv()
        for cp in first + passed:
            cp.wait_send()
        for cp in mine:
            cp.wait()

    return start, forward, finish


def _gather_scratch(n):
    return [pltpu.SemaphoreType.DMA((7 * n,)), pltpu.SemaphoreType.DMA((7 * n,)), pltpu.SemaphoreType.DMA((n,))]


def _all_gather(shards):
    n = len(shards)

    def body(*refs):
        start, forward, finish = _gather_phases(refs[:n], refs[n:2 * n], *refs[2 * n:])
        start()
        forward()
        finish()

    return pl.pallas_call(
        body, name="all_gather_weights",
        in_specs=[ANY] * n, out_specs=[ANY] * n,
        out_shape=[SDS((N_DEV, *s.shape), s.dtype) for s in shards],
        scratch_shapes=_gather_scratch(n),
    )(*shards)


def _exchange_sibling(grads):
    n = len(grads)

    def body(*refs):
        ins, outs = refs[:n], refs[n:2 * n]
        send_sems, recv_sems = refs[2 * n:]
        x, y, c, chips = _place()
        sibling = (x, y, 1 - c)
        copies = []
        for t in range(n):
            blocks = [(*chip, 1 - c) for chip in chips] + [sibling]
            for k, block in enumerate(blocks):
                copies.append(pltpu.make_async_remote_copy(
                    src_ref=ins[t].at[_slot(*block)], dst_ref=outs[t].at[k],
                    send_sem=send_sems.at[4 * t + k], recv_sem=recv_sems.at[4 * t + k],
                    device_id=sibling, device_id_type=MESH))
        for cp in copies:
            cp.start()
        for cp in copies:
            cp.wait()

    return pl.pallas_call(
        body, name="grads_to_sibling",
        in_specs=[ANY] * n, out_specs=[ANY] * n,
        out_shape=[SDS((4, *g.shape[1:]), g.dtype) for g in grads],
        scratch_shapes=[pltpu.SemaphoreType.DMA((4 * n,)), pltpu.SemaphoreType.DMA((4 * n,))],
    )(*grads)


def _chip_copies(sums_ref, land_ref, send_sems, recv_sems):
    x, y, c, chips = _place()
    del x, y
    return [pltpu.make_async_remote_copy(
        src_ref=sums_ref.at[j], dst_ref=land_ref.at[j], send_sem=send_sems.at[j], recv_sem=recv_sems.at[j],
        device_id=(*chip, c), device_id_type=MESH) for j, chip in enumerate(chips)]


def _row_tile(rows):
    return min(rows, 256)


def _pre_reduce(name, g, from_sibling, slots):
    _, r, c = g.shape
    tr = _row_tile(r)

    def body(slots_ref, g_ref, s_ref, o_ref):
        del slots_ref
        o_ref[...] = (g_ref[...].astype(F32) + s_ref[...].astype(F32)).astype(BF16)

    return pl.pallas_call(
        body, name=name,
        grid_spec=pltpu.PrefetchScalarGridSpec(
            num_scalar_prefetch=1, grid=(3, r // tr),
            in_specs=[pl.BlockSpec((1, tr, c), lambda j, i, s: (s[j], i, 0)),
                      pl.BlockSpec((1, tr, c), lambda j, i, s: (j, i, 0))],
            out_specs=pl.BlockSpec((1, tr, c), lambda j, i, s: (j, i, 0))),
        out_shape=SDS((3, r, c), BF16),
        compiler_params=_cparams(),
    )(slots, g, from_sibling)


def _adamw(w, g, m, v):
    m2 = ADAM_B1 * m + (1.0 - ADAM_B1) * g
    v2 = ADAM_B2 * v + (1.0 - ADAM_B2) * jnp.square(g)
    m_hat = m2 / (1.0 - ADAM_B1 ** ADAM_STEP)
    v_hat = v2 / (1.0 - ADAM_B2 ** ADAM_STEP)
    delta = -ADAM_LR * (m_hat / (jnp.sqrt(v_hat) + ADAM_EPS) + ADAM_WD * w)
    return delta, m2, v2


def _reduce_adamw(name, own, partials, w, m, v):
    r, c = own.shape
    tr = _row_tile(r)
    n_p = len(partials)

    def body(own_ref, *rest):
        p_refs, (w_ref, m_ref, v_ref, grad_ref, d_ref, nm_ref, nv_ref) = rest[:n_p], rest[n_p:]
        grad = own_ref[...]
        for p_ref, (_, _, count) in zip(p_refs, partials):
            for j in range(count):
                grad = grad + p_ref[j].astype(F32)
        grad_ref[...] = grad
        d_ref[...], nm_ref[...], nv_ref[...] = _adamw(w_ref[...], grad, m_ref[...], v_ref[...])

    flat = pl.BlockSpec((tr, c), lambda i: (i, 0))
    return pl.pallas_call(
        body, name=name, grid=(r // tr,),
        in_specs=[flat] + [pl.BlockSpec((count, tr, c), lambda i, first=first, count=count: (first // count, i, 0))
                           for _, first, count in partials] + [flat, flat, flat],
        out_specs=[flat, flat, flat, flat],
        out_shape=[SDS((r, c), F32)] * 4,
        compiler_params=_cparams(),
    )(own, *[p[0] for p in partials], w, m, v)


VM = pl.BlockSpec()


def _small_allreduce(parts):
    n = len(parts)

    def body(*refs):
        ins, outs, lands = refs[:n], refs[n:2 * n], refs[2 * n:3 * n]
        send_sems, recv_sems = refs[3 * n:]
        x, y, c, _ = _place()
        my_slot = _slot(x, y, c)
        copies = []
        for t in range(n):
            lands[t][my_slot] = ins[t][...]
            for k in range(1, N_DEV):
                sem = (N_DEV - 1) * t + k - 1
                copies.append(pltpu.make_async_remote_copy(
                    src_ref=ins[t], dst_ref=lands[t].at[my_slot],
                    send_sem=send_sems.at[sem], recv_sem=recv_sems.at[sem],
                    device_id=_peer(x, y, c, k), device_id_type=MESH))
        for cp in copies:
            cp.start()
        for t in range(n):
            for k in range(1, N_DEV):
                sem = (N_DEV - 1) * t + k - 1
                pltpu.make_async_remote_copy(
                    src_ref=ins[t], dst_ref=lands[t].at[_slot(*_peer(x, y, c, k))],
                    send_sem=send_sems.at[sem], recv_sem=recv_sems.at[sem],
                    device_id=(x, y, c), device_id_type=MESH).wait_recv()
        for cp in copies:
            cp.wait_send()
        for t in range(n):
            total = lands[t][0]
            for s in range(1, N_DEV):
                total = total + lands[t][s]
            outs[t][...] = total

    n_sems = (N_DEV - 1) * n
    return pl.pallas_call(
        body, name="small_allreduce",
        in_specs=[VM] * n, out_specs=[VM] * n, out_shape=[SDS(p.shape, F32) for p in parts],
        scratch_shapes=[pltpu.VMEM((N_DEV, *p.shape), F32) for p in parts]
        + [pltpu.SemaphoreType.DMA((n_sems,)), pltpu.SemaphoreType.DMA((n_sems,))],
    )(*parts)


def _small_adamw(my_slot, sums, ws, ms, vs):
    n = len(ws)

    def body(slot_ref, *refs):
        sum_refs, refs = refs[:n + 1], refs[n + 1:]
        w_refs, m_refs, v_refs, refs = refs[:n], refs[n:2 * n], refs[2 * n:3 * n], refs[3 * n:]
        g_refs, d_refs, nm_refs, nv_refs = refs[:n + 1], refs[n + 1:2 * n + 1], refs[2 * n + 1:3 * n + 1], refs[3 * n + 1:]
        for t in range(n + 1):
            if t == 0:
                g = sum_refs[0][:, pl.ds(pl.multiple_of(slot_ref[0] * 128, 128), 128)]
            else:
                g = sum_refs[t][...]
            g_refs[t][...] = g
            if t < n:
                d_refs[t][...], nm_refs[t][...], nv_refs[t][...] = _adamw(w_refs[t][...], g, m_refs[t][...], v_refs[t][...])

    shapes = [SDS(w.shape, F32) for w in ws]
    outs = pl.pallas_call(
        body, name="small_adamw",
        in_specs=[pl.BlockSpec(memory_space=pltpu.SMEM)] + [VM] * (4 * n + 1),
        out_specs=[VM] * (4 * n + 1),
        out_shape=shapes + [SDS(sums[-1].shape, F32)] + shapes * 3,
    )(my_slot, *sums, *ws, *ms, *vs)
    return outs[:n + 1], outs[n + 1:2 * n + 1], outs[2 * n + 1:3 * n + 1], outs[3 * n + 1:]


def kernel(x, a_norm, a_w_in, a_rel_bias, a_w_out, kv_norm, kv_w, t5_bias, b_norm, b_w_in, b_sinks, b_w_out, final_norm, loss_target, m_a_norm, m_a_w_in, m_a_rel_bias, m_a_w_out, m_kv_norm, m_kv_w, m_t5_bias, m_b_norm, m_b_w_in, m_b_sinks, m_b_w_out, m_final_norm, v_a_norm, v_a_w_in, v_a_rel_bias, v_a_w_out, v_kv_norm, v_kv_w, v_t5_bias, v_b_norm, v_b_w_in, v_b_sinks, v_b_w_out, v_final_norm):
    xi, yi, ci = lax.axis_index("x"), lax.axis_index("y"), lax.axis_index("c")
    my_slot = _slot(xi, yi, ci)

    w_in_a, a_gain = _all_gather([a_w_in[0].astype(BF16), a_norm])
    a_gain = a_gain.reshape(1, D_MODEL)

    slot_arr = jnp.reshape(my_slot, (1,)).astype(jnp.int32)
    late_shards = [b_w_in[0].astype(BF16), a_w_out[0].astype(BF16), b_w_out[0].astype(BF16), kv_w.astype(BF16)]
    grad_x, loc, matrices = _local_step(
        slot_arr, x[0], loss_target[0], a_gain, w_in_a, a_rel_bias[0], late_shards,
        kv_norm.reshape(1, D_MODEL), t5_bias, b_norm, b_sinks, final_norm.reshape(1, D_MODEL))

    shard_w = dict(a_w_in=a_w_in[0], b_w_in=b_w_in[0], a_w_out=a_w_out[0], b_w_out=b_w_out[0], kv_w=kv_w)
    shard_m = dict(a_w_in=m_a_w_in[0], b_w_in=m_b_w_in[0], a_w_out=m_a_w_out[0], b_w_out=m_b_w_out[0], kv_w=m_kv_w)
    shard_v = dict(a_w_in=v_a_w_in[0], b_w_in=v_b_w_in[0], a_w_out=v_a_w_out[0], b_w_out=v_b_w_out[0], kv_w=v_kv_w)
    big = {n: _reduce_adamw("adamw_" + n, own, partials, shard_w[n], shard_m[n], shard_v[n])
           for n, (own, partials) in matrices.items()}

    names = ("a_norm", "a_rel_bias", "kv_norm", "t5_bias", "b_norm", "b_sinks", "final_norm")
    row = lambda a: a.reshape(1, -1) if a.ndim == 1 else (a[0] if a.ndim == 3 else a)
    small_w = [row(a) for a in (a_norm, a_rel_bias, kv_norm, t5_bias, b_norm, b_sinks, final_norm)]
    small_m = [row(a) for a in (m_a_norm, m_a_rel_bias, m_kv_norm, m_t5_bias, m_b_norm, m_b_sinks, m_final_norm)]
    small_v = [row(a) for a in (v_a_norm, v_a_rel_bias, v_kv_norm, v_t5_bias, v_b_norm, v_b_sinks, v_final_norm)]
    sums = dict(zip(names + ("loss",), _small_allreduce([loc[n] for n in names] + [loc["loss"]])))
    sums["a_rel_bias"] = _a_bias_grad(sums["a_rel_bias"])
    sums["t5_bias"] = sums["t5_bias"][:, :T5_BUCKETS].T
    sums["b_sinks"] = sums["b_sinks"][:, 0].reshape(1, N_HEADS)
    results = _small_adamw(slot_arr, [sums[n] for n in names + ("loss",)], small_w, small_m, small_v)
    like = dict(a_norm=a_norm, a_rel_bias=a_rel_bias, kv_norm=kv_norm, t5_bias=t5_bias, b_norm=b_norm,
                b_sinks=b_sinks, final_norm=final_norm)
    sm = [{n: part[i].reshape(like[n].shape) for i, n in enumerate(names)} for part in results]
    loss = results[0][len(names)][0, 0]

    order = ("a_norm", "a_w_in", "a_rel_bias", "a_w_out", "kv_norm", "kv_w", "t5_bias", "b_norm",
             "b_w_in", "b_sinks", "b_w_out", "final_norm")
    lead = dict(a_w_in=True, b_w_in=True, a_w_out=True, b_w_out=True, kv_w=False)

    def pick(kind, name):
        if name in big:
            val = big[name][kind]
            return val[None] if lead[name] else val
        return sm[kind][name]

    outs = [loss, grad_x[None]]
    for kind in range(4):
        outs += [pick(kind, n) for n in order]
    return tuple(outs)
```

```python
import functools
import math

import numpy as np
import jax
import jax.numpy as jnp
from jax import lax
from jax.experimental import pallas as pl
from jax.experimental.pallas import tpu as pltpu

F32 = jnp.float32
BF16 = jnp.bfloat16
SDS = jax.ShapeDtypeStruct

D_MODEL = 1024
HEAD_DIM = 64
CHUNK = 64
N_HEADS = 16
RMS_EPS = 1e-6
A_LEFT_CHUNKS = 8
A_BAND = (A_LEFT_CHUNKS + 1) * CHUNK
A_REL_CLIP = 256
B_KV_HEADS = 2
B_GROUP = 8
B_LEFT_CHUNKS = 2
B_BAND = (B_LEFT_CHUNKS + 1) * CHUNK
T5_BUCKETS = 32
T5_MAX_DIST = 128
QBLK = 256
A_KEYS = 3 * QBLK
B_QBLK_FWD = 128
B_QBLK_BWD = 256
B_PREV = 128
A_DIAG = A_KEYS
NEG = -1e30
SCALE = HEAD_DIM ** -0.5
N_DEV = 8

ADAM_LR = 0.001
ADAM_B1 = 0.9
ADAM_B2 = 0.999
ADAM_EPS = 1e-08
ADAM_WD = 0.01
ADAM_STEP = 10

VMEM_LIMIT_BYTES = 56 * 1024 * 1024
MESH = pl.DeviceIdType.MESH


def _cparams():
    return pltpu.CompilerParams(vmem_limit_bytes=VMEM_LIMIT_BYTES)


def _dot(a, b):
    return jnp.dot(a, b, preferred_element_type=F32)


def _dot_nt(a, b):
    return lax.dot_general(a, b, (((1,), (1,)), ((), ())), preferred_element_type=F32)


def _dot_tn(a, b):
    return lax.dot_general(a, b, (((0,), (0,)), ((), ())), preferred_element_type=F32)


def _rstd(xf):
    return lax.rsqrt(jnp.mean(xf * xf, axis=-1, keepdims=True) + RMS_EPS)


def _sigmoid(x):
    return 1.0 / (1.0 + jnp.exp(-x))


def _norm_matmul_gather(order, x, gain_shard, w_shard):
    t = x.shape[0]
    dw, tn = w_shard.shape
    tm = min(t, 1024)
    n_m = t // tm

    def body(order_ref, x_ref, gs_ref, shard_ref, xn_ref, o_ref, full_ref, gain_ref,
             xn_all, wbuf, gland, send_sems, recv_sems, gsend_sems, grecv_sems, load_sems, own_sem):
        n, m = pl.program_id(0), pl.program_id(1)
        x_i, y_i, c_i, chips = _place()
        me, sibling = (x_i, y_i, c_i), (x_i, y_i, 1 - c_i)

        def send(k, block, to, src=None):
            dst = full_ref.at[_slot(*block)]
            return pltpu.make_async_remote_copy(
                src_ref=dst if src is None else src, dst_ref=dst,
                send_sem=send_sems.at[k], recv_sem=recv_sems.at[k], device_id=to, device_id_type=MESH)

        own = pltpu.make_async_copy(shard_ref, full_ref.at[_slot(*me)], own_sem)
        first = [send(0, me, sibling, src=shard_ref)]
        first += [send(1 + j, me, (*chip, c_i), src=shard_ref) for j, chip in enumerate(chips)]
        forwards = [send(4 + j, (*chip, c_i), sibling) for j, chip in enumerate(chips)]
        arrivals = [send(0, sibling, me)] + [send(1 + j, (*chip, c_i), me) for j, chip in enumerate(chips)]
        arrivals += [send(4 + j, (*chip, 1 - c_i), me) for j, chip in enumerate(chips)]
        gains = [pltpu.make_async_remote_copy(
            src_ref=gs_ref, dst_ref=gland.at[_slot(*me)], send_sem=gsend_sems.at[k - 1],
            recv_sem=grecv_sems.at[k - 1], device_id=_peer(x_i, y_i, c_i, k), device_id_type=MESH)
            for k in range(1, N_DEV)]

        @pl.when(jnp.logical_and(n == 0, m == 0))
        def _():
            own.start()
            for cp in gains + first:
                cp.start()
            pltpu.make_async_copy(shard_ref, wbuf.at[0], load_sems.at[0]).start()
            gland[_slot(*me)] = gs_ref[...]
            for k in range(1, N_DEV):
                pltpu.make_async_remote_copy(
                    src_ref=gs_ref, dst_ref=gland.at[_slot(*_peer(x_i, y_i, c_i, k))],
                    send_sem=gsend_sems.at[k - 1], recv_sem=grecv_sems.at[k - 1],
                    device_id=me, device_id_type=MESH).wait_recv()
            for s in range(N_DEV):
                gain_ref[:, 128 * s:128 * (s + 1)] = gland[s]

        rows = pl.ds(pl.multiple_of(m * tm, tm), tm)

        @pl.when(n == 0)
        def _():
            xf = x_ref[...]
            xn = ((xf * _rstd(xf)) * gain_ref[...]).astype(BF16)
            xn_all[rows, :] = xn
            xn_ref[...] = xn

        @pl.when(m == 0)
        def _():
            pltpu.make_async_copy(full_ref.at[0], wbuf.at[n % 2], load_sems.at[n % 2]).wait()

        o_ref[...] = _dot(xn_all[rows, :], wbuf[n % 2]).astype(BF16)

        for k in range(N_DEV - 1):
            @pl.when(jnp.logical_and(n == k, m == n_m - 1))
            def _(k=k):
                arrivals[k].wait_recv()
                if 1 <= k <= 3:
                    forwards[k - 1].start()
                pltpu.make_async_copy(full_ref.at[order_ref[k + 1]], wbuf.at[(k + 1) % 2],
                                      load_sems.at[(k + 1) % 2]).start()

        @pl.when(jnp.logical_and(n == N_DEV - 1, m == n_m - 1))
        def _():
            for cp in gains + first + forwards:
                cp.wait_send()
            own.wait()

    held = lambda n, m, order: (jnp.where(n == 0, m, n_m - 1), 0)
    return pl.pallas_call(
        body, name="norm_matmul_gather",
        grid_spec=pltpu.PrefetchScalarGridSpec(
            num_scalar_prefetch=1, grid=(N_DEV, n_m),
            in_specs=[pl.BlockSpec((tm, D_MODEL), held),
                      pl.BlockSpec((1, 128), lambda n, m, order: (0, 0)), ANY],
            out_specs=[pl.BlockSpec((tm, D_MODEL), held),
                       pl.BlockSpec((tm, tn), lambda n, m, order: (m, order[n])),
                       ANY, pl.BlockSpec((1, D_MODEL), lambda n, m, order: (0, 0))],
            scratch_shapes=[pltpu.VMEM((t, D_MODEL), BF16), pltpu.VMEM((2, dw, tn), BF16),
                            pltpu.VMEM((N_DEV, 1, 128), F32),
                            pltpu.SemaphoreType.DMA((7,)), pltpu.SemaphoreType.DMA((7,)),
                            pltpu.SemaphoreType.DMA((7,)), pltpu.SemaphoreType.DMA((7,)),
                            pltpu.SemaphoreType.DMA((2,)), pltpu.SemaphoreType.DMA]),
        out_shape=[SDS((t, D_MODEL), BF16), SDS((t, N_DEV * tn), BF16), SDS((N_DEV, dw, tn), BF16),
                   SDS((1, D_MODEL), F32)],
        compiler_params=_cparams(),
    )(order, x, gain_shard, w_shard)


def _layer_a_out(x, z, w_out, kv_gain, b_gain, kv_w, w_in_b):
    t = x.shape[0]
    tm = min(t, 512)
    nb, _, tn = w_in_b.shape

    def body(x_ref, z_ref, wo_ref, kvg_ref, bg_ref, kvw_ref, wb_ref,
             h1_ref, kvn_ref, hb_ref, kv_ref, qg_ref):
        h1 = x_ref[...] + _dot(z_ref[...], wo_ref[...])
        h1_ref[...] = h1
        y0 = h1 * _rstd(h1)
        kvn = (y0 * kvg_ref[...]).astype(BF16)
        hb = (y0 * bg_ref[...]).astype(BF16)
        kvn_ref[...] = kvn
        hb_ref[...] = hb
        kv_ref[...] = _dot(kvn, kvw_ref[...]).astype(BF16)
        for i in range(nb):
            qg_ref[:, i * tn:(i + 1) * tn] = _dot(hb, wb_ref[i]).astype(BF16)

    row = lambda m: (m, 0)
    fix2 = lambda m: (0, 0)
    return pl.pallas_call(
        body, name="layer_a_out", grid=(t // tm,),
        in_specs=[pl.BlockSpec((tm, D_MODEL), row), pl.BlockSpec((tm, D_MODEL), row),
                  pl.BlockSpec((D_MODEL, D_MODEL), fix2),
                  pl.BlockSpec((1, D_MODEL), fix2), pl.BlockSpec((1, D_MODEL), fix2),
                  pl.BlockSpec((D_MODEL, 256), fix2),
                  pl.BlockSpec((nb, D_MODEL, tn), lambda m: (0, 0, 0))],
        out_specs=[pl.BlockSpec((tm, D_MODEL), row), pl.BlockSpec((tm, D_MODEL), row),
                   pl.BlockSpec((tm, D_MODEL), row), pl.BlockSpec((tm, 256), row),
                   pl.BlockSpec((tm, nb * tn), row)],
        out_shape=[SDS((t, D_MODEL), F32), SDS((t, D_MODEL), BF16), SDS((t, D_MODEL), BF16),
                   SDS((t, 256), BF16), SDS((t, nb * tn), BF16)],
        compiler_params=_cparams(),
    )(x, z, w_out, kv_gain, b_gain, kv_w, w_in_b)


def _layer_b_out_loss(h1, z, w_out, f_gain, target):
    t = h1.shape[0]
    tm = min(t, 512)

    def body(h1_ref, z_ref, wo_ref, fg_ref, tgt_ref,
             dh2_ref, dh2b_ref, dz_ref, loss_ref, dfn_ref):
        @pl.when(pl.program_id(0) == 0)
        def _():
            loss_ref[...] = jnp.zeros_like(loss_ref)
            dfn_ref[...] = jnp.zeros_like(dfn_ref)

        h2 = h1_ref[...] + _dot(z_ref[...], wo_ref[...])
        r = _rstd(h2)
        yn = h2 * r
        fg = fg_ref[...]
        err = yn * fg - tgt_ref[...]
        loss_ref[...] += (0.5 / D_MODEL) * jnp.sum(err * err)
        dy = err * (1.0 / D_MODEL)
        dfn_ref[...] += jnp.sum(dy * yn, axis=0, keepdims=True)
        u = dy * fg
        dh2 = r * u - h2 * ((r * r * r) * jnp.mean(u * h2, axis=-1, keepdims=True))
        dh2_ref[...] = dh2
        dh2b = dh2.astype(BF16)
        dh2b_ref[...] = dh2b
        dz_ref[...] = _dot_nt(dh2b, wo_ref[...]).astype(BF16)

    row = lambda m: (m, 0)
    fix2 = lambda m: (0, 0)
    return pl.pallas_call(
        body, name="layer_b_out_loss", grid=(t // tm,),
        in_specs=[pl.BlockSpec((tm, D_MODEL), row), pl.BlockSpec((tm, D_MODEL), row),
                  pl.BlockSpec((D_MODEL, D_MODEL), fix2), pl.BlockSpec((1, D_MODEL), fix2),
                  pl.BlockSpec((tm, D_MODEL), row)],
        out_specs=[pl.BlockSpec((tm, D_MODEL), row), pl.BlockSpec((tm, D_MODEL), row),
                   pl.BlockSpec((tm, D_MODEL), row), pl.BlockSpec((1, 128), fix2),
                   pl.BlockSpec((1, D_MODEL), fix2)],
        out_shape=[SDS((t, D_MODEL), F32), SDS((t, D_MODEL), BF16), SDS((t, D_MODEL), BF16),
                   SDS((1, 128), F32), SDS((1, D_MODEL), F32)],
        compiler_params=_cparams(),
    )(h1, z, w_out, f_gain, target)


def _layer_b_in_bwd(dqg, dkv, w_in_b, kv_w, h1, dh2, b_gain, kv_gain, w_out_a):
    t = h1.shape[0]
    tm = min(t, 256)
    nb, _, tn = w_in_b.shape
    per = D_MODEL // tn

    def body(dqg_ref, dkv_ref, wb_ref, kvw_ref, h1_ref, dh2_ref, bg_ref, kvg_ref, wo_ref,
             dh1_ref, dh1b_ref, dz_ref, dbn_ref, dkn_ref):
        @pl.when(pl.program_id(0) == 0)
        def _():
            dbn_ref[...] = jnp.zeros_like(dbn_ref)
            dkn_ref[...] = jnp.zeros_like(dkn_ref)

        dhb = jnp.zeros((tm, D_MODEL), F32)
        for i in range(nb):
            blk = dqg_ref[i // per, :, (i % per) * tn:(i % per + 1) * tn]
            dhb = dhb + _dot_nt(blk, wb_ref[i])
        dkn = (_dot_nt(dkv_ref[0].astype(BF16), kvw_ref[:, 0:128])
               + _dot_nt(dkv_ref[1].astype(BF16), kvw_ref[:, 128:256]))
        h1 = h1_ref[...]
        r = _rstd(h1)
        xr = h1 * r
        dbn_ref[...] += jnp.sum(dhb * xr, axis=0, keepdims=True)
        dkn_ref[...] += jnp.sum(dkn * xr, axis=0, keepdims=True)
        u = dhb * bg_ref[...] + dkn * kvg_ref[...]
        dh1 = dh2_ref[...] + r * u - h1 * ((r * r * r) * jnp.mean(u * h1, axis=-1, keepdims=True))
        dh1_ref[...] = dh1
        dh1b = dh1.astype(BF16)
        dh1b_ref[...] = dh1b
        dz_ref[...] = _dot_nt(dh1b, wo_ref[...]).astype(BF16)

    row = lambda m: (m, 0)
    fix2 = lambda m: (0, 0)
    return pl.pallas_call(
        body, name="layer_b_in_bwd", grid=(t // tm,),
        in_specs=[pl.BlockSpec((2, tm, D_MODEL), lambda m: (0, m, 0)),
                  pl.BlockSpec((2, tm, 128), lambda m: (0, m, 0)),
                  pl.BlockSpec((nb, D_MODEL, tn), lambda m: (0, 0, 0)),
                  pl.BlockSpec((D_MODEL, 256), fix2),
                  pl.BlockSpec((tm, D_MODEL), row), pl.BlockSpec((tm, D_MODEL), row),
                  pl.BlockSpec((1, D_MODEL), fix2), pl.BlockSpec((1, D_MODEL), fix2),
                  pl.BlockSpec((D_MODEL, D_MODEL), fix2)],
        out_specs=[pl.BlockSpec((tm, D_MODEL), row), pl.BlockSpec((tm, D_MODEL), row),
                   pl.BlockSpec((tm, D_MODEL), row), pl.BlockSpec((1, D_MODEL), fix2),
                   pl.BlockSpec((1, D_MODEL), fix2)],
        out_shape=[SDS((t, D_MODEL), F32), SDS((t, D_MODEL), BF16), SDS((t, D_MODEL), BF16),
                   SDS((1, D_MODEL), F32), SDS((1, D_MODEL), F32)],
        compiler_params=_cparams(),
    )(dqg, dkv, w_in_b, kv_w, h1, dh2, b_gain, kv_gain, w_out_a)


def _layer_a_in_bwd(dqg, dkv, w_in_a, x, dh1, a_gain, chip_sums):
    t = x.shape[0]
    tm = min(t, 256)
    nb, _, tn = w_in_a.shape
    per = D_MODEL // tn

    def body(dqg_ref, dkv_ref, w_ref, x_ref, dh1_ref, ag_ref, sums_ref, dx_ref, dan_ref, land_ref,
             send_sems, recv_sems):
        @pl.when(pl.program_id(0) == 0)
        def _():
            dan_ref[...] = jnp.zeros_like(dan_ref)
            for cp in _chip_copies(sums_ref, land_ref, send_sems, recv_sems):
                cp.start()

        dxn = jnp.zeros((tm, D_MODEL), F32)
        for i in range(nb):
            part = i // per
            src = dqg_ref if part in (0, 3) else dkv_ref
            outer = {0: 0, 3: 1, 1: 0, 2: 1}[part]
            blk = src[outer, :, (i % per) * tn:(i % per + 1) * tn]
            dxn = dxn + _dot_nt(blk, w_ref[i])
        xf = x_ref[...]
        r = _rstd(xf)
        dan_ref[...] += jnp.sum(dxn * (xf * r), axis=0, keepdims=True)
        u = dxn * ag_ref[...]
        dx_ref[...] = dh1_ref[...] + r * u - xf * ((r * r * r) * jnp.mean(u * xf, axis=-1, keepdims=True))

        @pl.when(pl.program_id(0) == t // tm - 1)
        def _():
            for cp in _chip_copies(sums_ref, land_ref, send_sems, recv_sems):
                cp.wait()

    row = lambda m: (m, 0)
    fix2 = lambda m: (0, 0)
    return pl.pallas_call(
        body, name="layer_a_in_bwd", grid=(t // tm,),
        in_specs=[pl.BlockSpec((2, tm, D_MODEL), lambda m: (0, m, 0)),
                  pl.BlockSpec((2, tm, D_MODEL), lambda m: (0, m, 0)),
                  pl.BlockSpec((nb, D_MODEL, tn), lambda m: (0, 0, 0)),
                  pl.BlockSpec((tm, D_MODEL), row), pl.BlockSpec((tm, D_MODEL), row),
                  pl.BlockSpec((1, D_MODEL), fix2), ANY],
        out_specs=[pl.BlockSpec((tm, D_MODEL), row), pl.BlockSpec((1, D_MODEL), fix2), ANY],
        out_shape=[SDS((t, D_MODEL), F32), SDS((1, D_MODEL), F32), SDS(chip_sums.shape, chip_sums.dtype)],
        scratch_shapes=[pltpu.SemaphoreType.DMA((3,)), pltpu.SemaphoreType.DMA((3,))],
        compiler_params=_cparams(),
    )(dqg, dkv, w_in_a, x, dh1, a_gain, chip_sums)


def _lut(s, vals):
    r = jnp.int32(vals[0])
    for i in range(1, len(vals)):
        r = jnp.where(s == i, jnp.int32(vals[i]), r)
    return r


def _held(steps, i):
    seq, cur = [None] * len(steps), None
    for k in range(len(steps) - 1, -1, -1):
        if steps[k][0] == i:
            cur = steps[k][1:3]
        seq[k] = cur
    for k in range(len(steps)):
        cur = seq[k] = seq[k] if seq[k] is not None else cur
    return seq


def _weight_grad_cols(name, my_slot, a, bs, steps, tn):
    t, dw = a.shape
    n_arr = len(bs)
    which = [s[0] for s in steps]
    blks = [s[3] for s in steps]

    def body(slot_ref, a_ref, *rest):
        b_refs, (o_ref, own_ref, at_ref) = rest[:n_arr], rest[n_arr:]
        s = pl.program_id(0)

        @pl.when(s == 0)
        def _():
            at_ref[...] = a_ref[...].T

        for i in range(n_arr):
            @pl.when(_lut(s, which) == i)
            def _(i=i):
                res = _dot(at_ref[...], b_refs[i][0])
                o_ref[0] = res.astype(BF16)

                @pl.when(_lut(s, blks) == slot_ref[0])
                def _():
                    own_ref[...] = res

    def b_spec(i):
        held = _held(steps, i)
        return pl.BlockSpec((1, t, tn), lambda s, slot: (_lut(s, [h[0] for h in held]), 0,
                                                         _lut(s, [h[1] for h in held])))

    return pl.pallas_call(
        body, name=name,
        grid_spec=pltpu.PrefetchScalarGridSpec(
            num_scalar_prefetch=1, grid=(len(steps),),
            in_specs=[pl.BlockSpec((t, dw), lambda s, slot: (0, 0))] + [b_spec(i) for i in range(n_arr)],
            out_specs=[pl.BlockSpec((1, dw, tn), lambda s, slot: (_lut(s, blks), 0, 0)),
                       pl.BlockSpec((dw, tn), lambda s, slot: (0, 0))],
            scratch_shapes=[pltpu.VMEM((dw, t), BF16)]),
        out_shape=[SDS((N_DEV, dw, tn), BF16), SDS((dw, tn), F32)],
        compiler_params=_cparams(),
    )(my_slot, a, *bs)


def _weight_grad_rows(name, my_slot, a, b):
    t, dw = a.shape
    n_o, _, c = b.shape
    rows = dw // N_DEV

    def body(slot_ref, a_ref, b_ref, o_ref, own_ref):
        at = a_ref[...].T
        res = [_dot(at, b_ref[o].astype(BF16)) for o in range(n_o)]
        for o in range(n_o):
            o_ref[0, :, o * c:(o + 1) * c] = res[o].astype(BF16)

        @pl.when(pl.program_id(0) == slot_ref[0])
        def _():
            for o in range(n_o):
                own_ref[:, o * c:(o + 1) * c] = res[o]

    return pl.pallas_call(
        body, name=name,
        grid_spec=pltpu.PrefetchScalarGridSpec(
            num_scalar_prefetch=1, grid=(N_DEV,),
            in_specs=[pl.BlockSpec((t, rows), lambda s, slot: (0, s)),
                      pl.BlockSpec((n_o, t, c), lambda s, slot: (0, 0, 0))],
            out_specs=[pl.BlockSpec((1, rows, n_o * c), lambda s, slot: (s, 0, 0)),
                       pl.BlockSpec((rows, n_o * c), lambda s, slot: (0, 0))]),
        out_shape=[SDS((N_DEV, rows, n_o * c), BF16), SDS((rows, n_o * c), F32)],
        compiler_params=_cparams(),
    )(my_slot, a, b)


def _lane_lo():
    return lax.broadcasted_iota(jnp.int32, (1, 128), 1) < HEAD_DIM


def _offset_sums(gt):
    keys = gt.shape[1]
    gc = gt[0:CHUNK]
    for cc in range(1, gt.shape[0] // CHUNK):
        gc = gc + pltpu.roll(gt[cc * CHUNK:(cc + 1) * CHUNK], keys - cc * CHUNK, 1)
    hi = gc.astype(BF16)
    lo = (gc - hi.astype(F32)).astype(BF16)
    flip = (lax.broadcasted_iota(jnp.int32, (CHUNK, CHUNK), 0)
            + lax.broadcasted_iota(jnp.int32, (CHUNK, CHUNK), 1) == CHUNK - 1).astype(BF16)
    gf = _dot(flip, hi) + _dot(flip, lo)
    skew = pltpu.roll(gf, 0, 1, stride=1, stride_axis=0)
    return jnp.sum(skew, axis=0, keepdims=True)


def _band_bias(w_row, band, rows):
    keys = w_row.shape[1]
    base = jnp.broadcast_to(w_row, (CHUNK, keys))
    skew = pltpu.roll(base, 0, 1, stride=1, stride_axis=0)
    skew = pltpu.roll(skew, keys - (CHUNK - 1), 1)
    col = lax.broadcasted_iota(jnp.int32, (CHUNK, keys), 1)
    chunk0 = jnp.where(col < band, skew, NEG)
    return jnp.concatenate(
        [chunk0] + [pltpu.roll(chunk0, cc * CHUNK, 1) for cc in range(1, rows // CHUNK)], axis=0)


def _silu_parts(g):
    sg = _sigmoid(g)
    return g * sg, sg * (1.0 + g * (1.0 - sg))


A_PAIRS = 2
A_LANES = 128 * A_PAIRS
A_STEPS = D_MODEL // A_LANES


def _a_specs():
    q = pl.BlockSpec((QBLK, A_LANES), lambda p, j: (j, p))
    ks = [pl.BlockSpec((QBLK, A_LANES), lambda p, j, b=b: (jnp.maximum(j - 2 + b, 0), A_STEPS + p)) for b in range(3)]
    vs = [pl.BlockSpec((QBLK, A_LANES), lambda p, j, b=b: (jnp.maximum(j - 2 + b, 0), 2 * A_STEPS + p))
          for b in range(3)]
    g = pl.BlockSpec((QBLK, A_LANES), lambda p, j: (j, 3 * A_STEPS + p))
    bias = pl.BlockSpec((A_PAIRS, 8, A_KEYS), lambda p, j: (p, 0, 0))
    return q, ks, vs, g, bias


def _a_fill_bias(w_ref, b_ref, j):
    _fill_bias(2 * A_PAIRS, lambda h: w_ref[h // 2, h % 2:h % 2 + 1, :], A_BAND, QBLK * (2 - j), 2, b_ref, j)


def _fill_bias(n, get_row, band, first_valid_col, early, bias_scr, j):
    @pl.when(j == 0)
    def _():
        for h in range(n):
            bias_scr[h] = _band_bias(get_row(h), band, bias_scr.shape[1])

    @pl.when(j < early)
    def _():
        keys = bias_scr.shape[2]
        col_ok = lax.broadcasted_iota(jnp.int32, (1, keys), 1) >= first_valid_col
        for h in range(n):
            bias_scr[n + h] = jnp.where(col_ok, bias_scr[h], NEG)


def _head_logits(q, k, bias_scr, idx, sel):
    qm = jnp.where(sel, q, jnp.zeros_like(q)) * SCALE
    return qm, _dot_nt(qm, k) + bias_scr[idx]


def _row_sums_everywhere(r, sel):
    return jnp.where(sel, pltpu.roll(r, HEAD_DIM, 1), r)


def _own_everywhere(x, sel):
    return jnp.where(sel, x, pltpu.roll(x, HEAD_DIM, 1))


def _minus_rows(s, row_full):
    return jnp.concatenate([s[:, i:i + 128] - row_full for i in range(0, s.shape[1], 128)], axis=1)


def _attn_a_fwd(qkvg, bias, gather):
    t = qkvg.shape[0]
    nq = t // QBLK
    n_g = len(gather)
    q_spec, k_specs, v_specs, g_spec, bias_spec = _a_specs()

    def body(q_ref, k0, k1, k2, v0, v1, v2, g_ref, w_ref, *rest):
        shard_refs, rest = rest[:n_g], rest[n_g:]
        z_ref, o_ref, lse_ref = rest[:3]
        full_refs, (b_ref, *comm) = rest[3:3 + n_g], rest[3 + n_g:]
        p = pl.program_id(0)
        j = pl.program_id(1)
        start, forward, finish = _gather_phases(shard_refs, full_refs, *comm)
        pl.when(jnp.logical_and(p == 0, j == 0))(start)
        pl.when(jnp.logical_and(p == A_STEPS // 2, j == 0))(forward)
        _a_fill_bias(w_ref, b_ref, j)
        early = (j < 2).astype(jnp.int32)
        lane_lo = _lane_lo()
        for pp in range(A_PAIRS):
            cols = slice(128 * pp, 128 * (pp + 1))
            q = q_ref[:, cols]
            k = jnp.concatenate([k0[:, cols], k1[:, cols], k2[:, cols]], axis=0)
            v = jnp.concatenate([v0[:, cols], v1[:, cols], v2[:, cols]], axis=0)
            outs, lses = [], []
            for hh in range(2):
                sel = lane_lo if hh == 0 else jnp.logical_not(lane_lo)
                _, s = _head_logits(q, k, b_ref, 2 * pp + hh + 2 * A_PAIRS * early, sel)
                mx = jnp.max(s, axis=-1, keepdims=True)
                e = jnp.exp(s - mx).astype(BF16)
                r = _dot(e, jnp.where(sel, v, jnp.ones_like(v)))
                l = _row_sums_everywhere(r, sel)
                outs.append(r / l)
                lses.append(mx + jnp.log(l))
            o = jnp.where(lane_lo, outs[0], outs[1])
            silu, _ = _silu_parts(g_ref[:, cols].astype(F32))
            o_ref[:, cols] = o.astype(BF16)
            z_ref[:, cols] = (o * silu).astype(BF16)
            lse_ref[:, cols] = jnp.where(lane_lo, lses[0], lses[1])
        pl.when(jnp.logical_and(p == A_STEPS - 1, j == nq - 1))(finish)

    out_spec = pl.BlockSpec((QBLK, A_LANES), lambda p, j: (j, p))
    outs = pl.pallas_call(
        body, name="attn_a_fwd", grid=(A_STEPS, nq),
        in_specs=[q_spec, *k_specs, *v_specs, g_spec, bias_spec] + [ANY] * n_g,
        out_specs=[out_spec, out_spec, out_spec] + [ANY] * n_g,
        out_shape=[SDS((t, D_MODEL), BF16), SDS((t, D_MODEL), BF16), SDS((t, D_MODEL), F32)]
        + [SDS((N_DEV, *s.shape), s.dtype) for s in gather],
        scratch_shapes=[pltpu.VMEM((4 * A_PAIRS, QBLK, A_KEYS), F32)] + _gather_scratch(n_g),
        compiler_params=_cparams(),
    )(qkvg, qkvg, qkvg, qkvg, qkvg, qkvg, qkvg, qkvg, bias, *gather)
    return outs[0], outs[1], outs[2], list(outs[3:])


def _attn_a_bwd(qkvg, bias, out_a, lse, dz, scatter):
    t = qkvg.shape[0]
    nq = t // QBLK
    n_sc = len(scatter)
    q_spec, k_specs, v_specs, g_spec, bias_spec = _a_specs()

    def body(q_ref, k0, k1, k2, v0, v1, v2, g_ref, w_ref, o_ref, lse_ref, dz_ref, *rest):
        sc_refs, rest = rest[:n_sc], rest[n_sc:]
        dqg_ref, dkv_ref, dg_ref = rest[:3]
        land_refs, rest = rest[3:3 + n_sc], rest[3 + n_sc:]
        dk_acc, dv_acc, gt_acc, b_ref, send_sems, recv_sems = rest
        j = pl.program_id(1)
        first = jnp.logical_and(pl.program_id(0) == 0, j == 0)
        last = jnp.logical_and(pl.program_id(0) == A_STEPS - 1, j == nq - 1)

        @pl.when(first)
        def _():
            for cp in _scatter_copies(sc_refs, land_refs, send_sems, recv_sems):
                cp.start()

        _a_fill_bias(w_ref, b_ref, j)

        @pl.when(j == 0)
        def _():
            dk_acc[...] = jnp.zeros_like(dk_acc)
            dv_acc[...] = jnp.zeros_like(dv_acc)
            gt_acc[...] = jnp.zeros_like(gt_acc)

        early = (j < 2).astype(jnp.int32)
        lane_lo = _lane_lo()
        for pp in range(A_PAIRS):
            cols = slice(128 * pp, 128 * (pp + 1))
            q = q_ref[:, cols]
            k = jnp.concatenate([k0[:, cols], k1[:, cols], k2[:, cols]], axis=0)
            v = jnp.concatenate([v0[:, cols], v1[:, cols], v2[:, cols]], axis=0)
            o = o_ref[:, cols].astype(F32)
            lse_pair = lse_ref[:, cols]
            dzf = dz_ref[:, cols].astype(F32)
            silu, dsilu = _silu_parts(g_ref[:, cols].astype(F32))
            do = dzf * silu
            dqg_ref[1, :, cols] = (dzf * o * dsilu).astype(BF16)
            doo = do * o
            dqs = []
            dk_blk = jnp.zeros((A_KEYS, 128), F32)
            dv_blk = jnp.zeros((A_KEYS, 128), F32)
            for hh in range(2):
                sel = lane_lo if hh == 0 else jnp.logical_not(lane_lo)
                qm, s = _head_logits(q, k, b_ref, 2 * pp + hh + 2 * A_PAIRS * early, sel)
                p = jnp.exp(_minus_rows(s, _own_everywhere(lse_pair, sel)))
                delta = jnp.sum(jnp.where(sel, doo, 0.0), axis=-1, keepdims=True)
                dom = jnp.where(sel, do, 0.0).astype(BF16)
                dp = _dot_nt(dom, v)
                ds = p * (dp - delta)
                gt_acc[2 * pp + hh] += ds
                dsb = ds.astype(BF16)
                dqs.append(_dot(dsb, k) * SCALE)
                dk_blk = dk_blk + _dot_tn(dsb, qm)
                dv_blk = dv_blk + _dot_tn(p.astype(BF16), dom)
            dqg_ref[0, :, cols] = jnp.where(lane_lo, dqs[0], dqs[1]).astype(BF16)
            for b in range(3):
                @pl.when(j - 2 + b >= 0)
                def _(b=b, cols=cols, dk_blk=dk_blk, dv_blk=dv_blk):
                    rows = pl.ds(pl.multiple_of((j - 2 + b) * QBLK, QBLK), QBLK)
                    dk_acc[rows, cols] += dk_blk[b * QBLK:(b + 1) * QBLK]
                    dv_acc[rows, cols] += dv_blk[b * QBLK:(b + 1) * QBLK]

        @pl.when(j == nq - 1)
        def _():
            dkv_ref[0] = dk_acc[...].astype(BF16)
            dkv_ref[1] = dv_acc[...].astype(BF16)
            for pp in range(A_PAIRS):
                dg_ref[pp] = jnp.concatenate([_offset_sums(gt_acc[2 * pp]), _offset_sums(gt_acc[2 * pp + 1]),
                                              jnp.zeros((6, A_DIAG), F32)], axis=0)

        @pl.when(last)
        def _():
            for cp in _scatter_copies(sc_refs, land_refs, send_sems, recv_sems):
                cp.wait()

    blk = pl.BlockSpec((QBLK, A_LANES), lambda p, j: (j, p))
    outs = pl.pallas_call(
        body, name="attn_a_bwd", grid=(A_STEPS, nq),
        in_specs=[q_spec, *k_specs, *v_specs, g_spec, bias_spec, blk, blk, blk] + [ANY] * n_sc,
        out_specs=[pl.BlockSpec((2, QBLK, A_LANES), lambda p, j: (0, j, p)),
                   pl.BlockSpec((2, t, A_LANES), lambda p, j: (0, 0, p)),
                   pl.BlockSpec((A_PAIRS, 8, A_DIAG), lambda p, j: (p, 0, 0))] + [ANY] * n_sc,
        out_shape=[SDS((2, t, D_MODEL), BF16), SDS((2, t, D_MODEL), BF16), SDS((N_HEADS // 2, 8, A_DIAG), F32)]
        + [SDS((N_DEV - 1, *g.shape[1:]), g.dtype) for g in scatter],
        scratch_shapes=[pltpu.VMEM((t, A_LANES), F32), pltpu.VMEM((t, A_LANES), F32),
                        pltpu.VMEM((2 * A_PAIRS, QBLK, A_KEYS), F32), pltpu.VMEM((4 * A_PAIRS, QBLK, A_KEYS), F32),
                        pltpu.SemaphoreType.DMA(((N_DEV - 1) * n_sc,)),
                        pltpu.SemaphoreType.DMA(((N_DEV - 1) * n_sc,))],
        compiler_params=_cparams(),
    )(qkvg, qkvg, qkvg, qkvg, qkvg, qkvg, qkvg, qkvg, bias, out_a, lse, dz, *scatter)
    return outs[0], outs[1], outs[2], list(outs[3:])


def _b_specs(qblk):
    per = qblk // B_PREV
    q = pl.BlockSpec((qblk, 512), lambda h, j: (j, h))
    g = pl.BlockSpec((qblk, 512), lambda h, j: (j, 2 + h))
    kp = pl.BlockSpec((B_PREV, 128), lambda h, j: (jnp.maximum(per * j - 1, 0), 0))
    kc = pl.BlockSpec((qblk, 128), lambda h, j: (j, 0))
    vp = pl.BlockSpec((B_PREV, 128), lambda h, j: (jnp.maximum(per * j - 1, 0), 1))
    vc = pl.BlockSpec((qblk, 128), lambda h, j: (j, 1))
    bias = pl.BlockSpec((B_GROUP, qblk + B_PREV), lambda h, j: (h, 0))
    sinks = pl.BlockSpec(memory_space=pltpu.SMEM)
    return q, g, kp, kc, vp, vc, bias, sinks


def _b_operands(kp, kc, vp, vc, kvh):
    k = jnp.concatenate([kp[...], kc[...]], axis=0)
    v = jnp.concatenate([vp[...], vc[...]], axis=0)
    kr = pltpu.roll(k, HEAD_DIM, 1)
    vr = pltpu.roll(v, HEAD_DIM, 1)
    first = kvh == 0
    return (jnp.where(first, k, kr), jnp.where(first, kr, k),
            jnp.where(first, v, vr), jnp.where(first, vr, v))


def _attn_b_fwd(qg, kv, bias, sinks):
    t = qg.shape[0]
    qblk = B_QBLK_FWD
    q_spec, g_spec, kp_spec, kc_spec, vp_spec, vc_spec, bias_spec, sink_spec = _b_specs(qblk)

    def body(q_ref, g_ref, kp, kc, vp, vc, w_ref, sink_ref, z_ref, o_ref, lse_ref, b_ref):
        kvh = pl.program_id(0)
        j = pl.program_id(1)
        _fill_bias(B_GROUP, lambda h: w_ref[h:h + 1, :], B_BAND, B_PREV, 1, b_ref, j)
        early = (j < 1).astype(jnp.int32)
        lane_lo = _lane_lo()
        k_lo, k_hi, v_lo, v_hi = _b_operands(kp, kc, vp, vc, kvh)
        for pp in range(B_GROUP // 2):
            cols = slice(128 * pp, 128 * (pp + 1))
            qp = q_ref[:, cols]
            outs, lses = [], []
            for hh in range(2):
                g = 2 * pp + hh
                sel = lane_lo if hh == 0 else jnp.logical_not(lane_lo)
                sink = sink_ref[kvh * B_GROUP + g]
                vv = v_lo if hh == 0 else v_hi
                _, s = _head_logits(qp, k_lo if hh == 0 else k_hi, b_ref, g + B_GROUP * early, sel)
                mx = jnp.maximum(jnp.max(s, axis=-1, keepdims=True), sink)
                e = jnp.exp(s - mx).astype(BF16)
                r = _dot(e, jnp.where(sel, vv, jnp.ones_like(vv)))
                l = _row_sums_everywhere(r, sel) + jnp.exp(sink - mx)
                outs.append(r / l)
                lses.append(mx + jnp.log(l))
            o = jnp.where(lane_lo, outs[0], outs[1])
            silu, _ = _silu_parts(g_ref[:, cols].astype(F32))
            o_ref[:, cols] = o.astype(BF16)
            z_ref[:, cols] = (o * silu).astype(BF16)
            lse_ref[:, cols] = jnp.where(lane_lo, lses[0], lses[1])

    out_spec = pl.BlockSpec((qblk, 512), lambda h, j: (j, h))
    return pl.pallas_call(
        body, name="attn_b_fwd", grid=(B_KV_HEADS, t // qblk),
        in_specs=[q_spec, g_spec, kp_spec, kc_spec, vp_spec, vc_spec, bias_spec, sink_spec],
        out_specs=[out_spec, out_spec, out_spec],
        out_shape=[SDS((t, D_MODEL), BF16), SDS((t, D_MODEL), BF16), SDS((t, D_MODEL), F32)],
        scratch_shapes=[pltpu.VMEM((2 * B_GROUP, qblk, qblk + B_PREV), F32)],
        compiler_params=_cparams(),
    )(qg, qg, kv, kv, kv, kv, bias, sinks)


def _attn_b_bwd(qg, kv, bias, sinks, out_b, lse, dz, bucket_onehot):
    t = qg.shape[0]
    qblk = B_QBLK_BWD
    keys = qblk + B_PREV
    nq = t // qblk
    q_spec, g_spec, kp_spec, kc_spec, vp_spec, vc_spec, bias_spec, sink_spec = _b_specs(qblk)

    def body(q_ref, g_ref, kp, kc, vp, vc, w_ref, sink_ref, o_ref, lse_ref, dz_ref, oh_ref,
             dqg_ref, dkv_ref, dt5_ref, dsink_ref, gt_acc, b_ref):
        kvh = pl.program_id(0)
        j = pl.program_id(1)
        _fill_bias(B_GROUP, lambda h: w_ref[h:h + 1, :], B_BAND, B_PREV, 1, b_ref, j)

        @pl.when(jnp.logical_and(kvh == 0, j == 0))
        def _():
            dkv_ref[...] = jnp.zeros_like(dkv_ref)

        @pl.when(j == 0)
        def _():
            gt_acc[...] = jnp.zeros_like(gt_acc)
            dsink_ref[...] = jnp.zeros_like(dsink_ref)

        early = (j < 1).astype(jnp.int32)
        lane_lo = _lane_lo()
        k_lo, k_hi, v_lo, v_hi = _b_operands(kp, kc, vp, vc, kvh)
        dk_blk = jnp.zeros((keys, 128), F32)
        dv_blk = jnp.zeros((keys, 128), F32)
        for pp in range(B_GROUP // 2):
            cols = slice(128 * pp, 128 * (pp + 1))
            qp = q_ref[:, cols]
            o = o_ref[:, cols].astype(F32)
            lse_pair = lse_ref[:, cols]
            dzf = dz_ref[:, cols].astype(F32)
            silu, dsilu = _silu_parts(g_ref[:, cols].astype(F32))
            do = dzf * silu
            dqg_ref[1, :, cols] = (dzf * o * dsilu).astype(BF16)
            doo = do * o
            dqs = []
            for hh in range(2):
                g = 2 * pp + hh
                sel = lane_lo if hh == 0 else jnp.logical_not(lane_lo)
                sink = sink_ref[kvh * B_GROUP + g]
                kk = k_lo if hh == 0 else k_hi
                vv = v_lo if hh == 0 else v_hi
                qm, s = _head_logits(qp, kk, b_ref, g + B_GROUP * early, sel)
                lse_h = _own_everywhere(lse_pair, sel)
                p = jnp.exp(_minus_rows(s, lse_h))
                delta = jnp.sum(jnp.where(sel, doo, 0.0), axis=-1, keepdims=True)
                dom = jnp.where(sel, do, 0.0).astype(BF16)
                dp = _dot_nt(dom, vv)
                ds = p * (dp - delta)
                gt_acc[g] += ds
                dsink_ref[g:g + 1, :] -= jnp.sum(jnp.exp(sink - lse_h) * delta, axis=0, keepdims=True)
                dsb = ds.astype(BF16)
                dqs.append(_dot(dsb, kk) * SCALE)
                dk_blk = dk_blk + _dot_tn(dsb, qm)
                dv_blk = dv_blk + _dot_tn(p.astype(BF16), dom)
            dqg_ref[0, :, cols] = jnp.where(lane_lo, dqs[0], dqs[1]).astype(BF16)
        mine = lane_lo == (kvh == 0)
        dk_add = jnp.where(mine, dk_blk + pltpu.roll(dk_blk, HEAD_DIM, 1), 0.0)
        dv_add = jnp.where(mine, dv_blk + pltpu.roll(dv_blk, HEAD_DIM, 1), 0.0)

        @pl.when(j >= 1)
        def _():
            rows = pl.ds(pl.multiple_of(j * qblk - B_PREV, B_PREV), B_PREV)
            dkv_ref[0, rows, :] += dk_add[0:B_PREV]
            dkv_ref[1, rows, :] += dv_add[0:B_PREV]

        rows = pl.ds(pl.multiple_of(j * qblk, qblk), qblk)
        dkv_ref[0, rows, :] += dk_add[B_PREV:keys]
        dkv_ref[1, rows, :] += dv_add[B_PREV:keys]

        @pl.when(j == nq - 1)
        def _():
            dd = jnp.concatenate([_offset_sums(gt_acc[g]) for g in range(B_GROUP)], axis=0)
            hi = dd.astype(BF16)
            lo = (dd - hi.astype(F32)).astype(BF16)
            dt5_ref[...] = _dot(hi, oh_ref[...]) + _dot(lo, oh_ref[...])

    blk = pl.BlockSpec((qblk, 512), lambda h, j: (j, h))
    return pl.pallas_call(
        body, name="attn_b_bwd", grid=(B_KV_HEADS, nq),
        in_specs=[q_spec, g_spec, kp_spec, kc_spec, vp_spec, vc_spec, bias_spec, sink_spec, blk, blk, blk,
                  pl.BlockSpec((keys, 128), lambda h, j: (0, 0))],
        out_specs=[pl.BlockSpec((2, qblk, 512), lambda h, j: (0, j, h)),
                   pl.BlockSpec((2, t, 128), lambda h, j: (0, 0, 0)),
                   pl.BlockSpec((B_GROUP, 128), lambda h, j: (h, 0)),
                   pl.BlockSpec((B_GROUP, 128), lambda h, j: (h, 0))],
        out_shape=[SDS((2, t, D_MODEL), BF16), SDS((2, t, 128), F32),
                   SDS((N_HEADS, 128), F32), SDS((N_HEADS, 128), F32)],
        scratch_shapes=[pltpu.VMEM((B_GROUP, qblk, keys), F32), pltpu.VMEM((2 * B_GROUP, qblk, keys), F32)],
        compiler_params=_cparams(),
    )(qg, qg, kv, kv, kv, kv, bias, sinks, out_b, lse, dz, bucket_onehot)


def _a_bias_by_offset(rel_bias):
    m = np.arange(A_DIAG)
    idx = np.clip(A_BAND - 1 - m, -A_REL_CLIP, A_REL_CLIP) + A_REL_CLIP
    by_head = rel_bias[idx].T.reshape(N_HEADS // 2, 2, A_DIAG)
    return jnp.concatenate([by_head, jnp.zeros((N_HEADS // 2, 6, A_DIAG), F32)], axis=1)


def _a_bias_grad(offset_sums):
    first = 319
    tail = jnp.sum(offset_sums[:, :first], axis=1)
    body = jnp.flip(offset_sums[:, first:first + 320], axis=1)
    body = body.at[:, -1].add(tail)
    full = jnp.concatenate([jnp.zeros((N_HEADS, 193), F32), body], axis=1)
    return full.T


def _t5_bucket(rel):
    nb = T5_BUCKETS // 2
    max_exact = nb // 2
    ret = jnp.where(rel > 0, nb, 0)
    n = jnp.abs(rel)
    nf = jnp.maximum(n, 1).astype(jnp.float32)
    large = max_exact + (jnp.log(nf / max_exact) / math.log(T5_MAX_DIST / max_exact)
                         * (nb - max_exact)).astype(jnp.int32)
    large = jnp.minimum(large, nb - 1)
    return ret + jnp.where(n < max_exact, n, large)


def _b_offset_buckets(keys):
    return _t5_bucket(jnp.arange(keys, dtype=jnp.int32) - (B_LEFT_CHUNKS * CHUNK + CHUNK - 1))


def _b_bias_by_offset(t5_table, keys):
    return t5_table[_b_offset_buckets(keys)].T


def _b_bucket_onehot(keys):
    return (_b_offset_buckets(keys)[:, None] == jnp.arange(128)[None, :]).astype(BF16)


def _local_step(my_slot, order, x, target, a_gain_shard, w_in_a_shard, rel_bias, late_shards, kv_gain,
                t5_table, b_gain, sinks, f_gain):
    a_bias = _a_bias_by_offset(rel_bias)
    b_bias_fwd = _b_bias_by_offset(t5_table, B_QBLK_FWD + B_PREV)
    b_bias_bwd = _b_bias_by_offset(t5_table, B_QBLK_BWD + B_PREV)
    sinks_flat = sinks.reshape(N_HEADS)

    xn, qkvg, w_in_a, a_gain = _norm_matmul_gather(order, x, a_gain_shard, w_in_a_shard)
    z_a, out_a, lse_a, (w_in_b, w_out_a, w_out_b, kv_w) = _attn_a_fwd(qkvg, a_bias, late_shards)
    w_out_a = w_out_a.reshape(D_MODEL, D_MODEL)
    w_out_b = w_out_b.reshape(D_MODEL, D_MODEL)
    kv_w = kv_w.reshape(D_MODEL, 2 * 128)
    h1, kvn, hb, kv, qg = _layer_a_out(x, z_a, w_out_a, kv_gain, b_gain, kv_w, w_in_b)
    z_b, out_b, lse_b = _attn_b_fwd(qg, kv, b_bias_fwd, sinks_flat)
    dh2, dh2b, dz_b, loss, d_fn = _layer_b_out_loss(h1, z_b, w_out_b, f_gain, target)

    dqg_b, dkv_b, d_t5, d_sink = _attn_b_bwd(qg, kv, b_bias_bwd, sinks_flat, out_b, lse_b, dz_b,
                                             _b_bucket_onehot(B_QBLK_BWD + B_PREV))
    dh1, dh1b, dz_a, d_bn, d_kn = _layer_b_in_bwd(dqg_b, dkv_b, w_in_b, kv_w, h1, dh2, b_gain, kv_gain, w_out_a)
    early = dict(
        b_w_out=_weight_grad_rows("grad_b_w_out", my_slot, z_b, dh2b[None]),
        b_w_in=_weight_grad_cols("grad_b_w_in", my_slot, hb, [dqg_b],
                                 [(0, o, c, 4 * o + c) for o in range(2) for c in range(4)], 256),
        kv_w=_weight_grad_rows("grad_kv_w", my_slot, kvn, dkv_b),
        a_w_out=_weight_grad_rows("grad_a_w_out", my_slot, z_a, dh1b[None]))
    dqg_a, dkv_a, d_rel, landed = _attn_a_bwd(qkvg, a_bias, out_a, lse_a, dz_a, [g[0] for g in early.values()])
    g_w_in_a = _weight_grad_cols(
        "grad_a_w_in", my_slot, xn, [dqg_a, dkv_a],
        [(0, 0, 0, 0), (0, 0, 1, 1), (1, 0, 0, 2), (1, 0, 1, 3), (1, 1, 0, 4), (1, 1, 1, 5), (0, 1, 0, 6), (0, 1, 1, 7)], 512)
    from_sibling, = _exchange_sibling([g_w_in_a[0]])
    x_i, y_i, c_i, chips = _place()
    del x_i, y_i
    forward_slots = jnp.stack([_slot(*chip, c_i) for chip in chips]).astype(jnp.int32)
    chip_sums = _pre_reduce("chip_sum_a_w_in", g_w_in_a[0], from_sibling, forward_slots)
    grad_x, d_an, from_chips = _layer_a_in_bwd(dqg_a, dkv_a, w_in_a, x, dh1, a_gain, chip_sums)

    matrices = {n: (g[1], [(land, 0, N_DEV - 1)]) for (n, g), land in zip(early.items(), landed)}
    matrices["a_w_in"] = (g_w_in_a[1], [(from_sibling, 3, 1), (from_chips, 0, 3)])
    small = dict(
        loss=loss, a_norm=d_an, a_rel_bias=d_rel[:, :2].reshape(N_HEADS, A_DIAG),
        kv_norm=d_kn, t5_bias=d_t5, b_norm=d_bn, b_sinks=d_sink, final_norm=d_fn)
    return grad_x, small, matrices


def _place():
    x, y, c = lax.axis_index("x"), lax.axis_index("y"), lax.axis_index("c")
    chips = [(1 - x, y), (x, 1 - y), (1 - x, 1 - y)]
    return x, y, c, chips


def _slot(px, py, pc):
    return 4 * px + 2 * py + pc


ANY = pl.BlockSpec(memory_space=pl.ANY)


def _peer(x, y, c, k):
    return (x ^ (k >> 2), y ^ ((k >> 1) & 1), c ^ (k & 1))


def _scatter_copies(grad_refs, land_refs, send_sems, recv_sems):
    x, y, c, _ = _place()
    copies = []
    for t, (grad, land) in enumerate(zip(grad_refs, land_refs)):
        for k in range(1, N_DEV):
            peer = _peer(x, y, c, k)
            sem = (N_DEV - 1) * t + k - 1
            copies.append(pltpu.make_async_remote_copy(
                src_ref=grad.at[_slot(*peer)], dst_ref=land.at[k - 1],
                send_sem=send_sems.at[sem], recv_sem=recv_sems.at[sem],
                device_id=peer, device_id_type=MESH))
    return copies


def _gather_phases(ins, outs, send_sems, recv_sems, local_sems):
    n = len(ins)
    x, y, c, chips = _place()
    me, sibling = (x, y, c), (x, y, 1 - c)

    def copy(t, k, block, to, src=None):
        dst = outs[t].at[_slot(*block)]
        return pltpu.make_async_remote_copy(
            src_ref=dst if src is None else src, dst_ref=dst,
            send_sem=send_sems.at[7 * t + k], recv_sem=recv_sems.at[7 * t + k],
            device_id=to, device_id_type=MESH)

    def lists():
        mine = [pltpu.make_async_copy(ins[t], outs[t].at[_slot(*me)], local_sems.at[t]) for t in range(n)]
        first = []
        for t in range(n):
            first.append(copy(t, 0, me, sibling, src=ins[t]))
            first += [copy(t, 1 + j, me, (*chip, c), src=ins[t]) for j, chip in enumerate(chips)]
        passed = [copy(t, 4 + j, (*chip, c), sibling) for t in range(n) for j, chip in enumerate(chips)]
        return mine, first, passed

    def start():
        mine, first, _ = lists()
        for cp in mine + first:
            cp.start()

    def forward():
        _, _, passed = lists()
        for t in range(n):
            for j, chip in enumerate(chips):
                copy(t, 1 + j, (*chip, c), me).wait_recv()
                passed[3 * t + j].start()

    def finish():
        mine, first, passed = lists()
        for t in range(n):
            copy(t, 0, sibling, me).wait_recv()
            for j, chip in enumerate(chips):
                copy(t, 4 + j, (*chip, 1 - c), me).wait_recv()
        for cp in first + passed:
            cp.wait_send()
        for cp in mine:
            cp.wait()

    return start, forward, finish


def _gather_scratch(n):
    return [pltpu.SemaphoreType.DMA((7 * n,)), pltpu.SemaphoreType.DMA((7 * n,)), pltpu.SemaphoreType.DMA((n,))]


def _exchange_sibling(grads):
    n = len(grads)

    def body(*refs):
        ins, outs = refs[:n], refs[n:2 * n]
        send_sems, recv_sems = refs[2 * n:]
        x, y, c, chips = _place()
        sibling = (x, y, 1 - c)
        copies = []
        for t in range(n):
            blocks = [(*chip, 1 - c) for chip in chips] + [sibling]
            for k, block in enumerate(blocks):
                copies.append(pltpu.make_async_remote_copy(
                    src_ref=ins[t].at[_slot(*block)], dst_ref=outs[t].at[k],
                    send_sem=send_sems.at[4 * t + k], recv_sem=recv_sems.at[4 * t + k],
                    device_id=sibling, device_id_type=MESH))
        for cp in copies:
            cp.start()
        for cp in copies:
            cp.wait()

    return pl.pallas_call(
        body, name="grads_to_sibling",
        in_specs=[ANY] * n, out_specs=[ANY] * n,
        out_shape=[SDS((4, *g.shape[1:]), g.dtype) for g in grads],
        scratch_shapes=[pltpu.SemaphoreType.DMA((4 * n,)), pltpu.SemaphoreType.DMA((4 * n,))],
    )(*grads)


def _chip_copies(sums_ref, land_ref, send_sems, recv_sems):
    x, y, c, chips = _place()
    del x, y
    return [pltpu.make_async_remote_copy(
        src_ref=sums_ref.at[j], dst_ref=land_ref.at[j], send_sem=send_sems.at[j], recv_sem=recv_sems.at[j],
        device_id=(*chip, c), device_id_type=MESH) for j, chip in enumerate(chips)]


def _row_tile(rows):
    return min(rows, 256)


def _pre_reduce(name, g, from_sibling, slots):
    _, r, c = g.shape
    tr = _row_tile(r)

    def body(slots_ref, g_ref, s_ref, o_ref):
        del slots_ref
        o_ref[...] = (g_ref[...].astype(F32) + s_ref[...].astype(F32)).astype(BF16)

    return pl.pallas_call(
        body, name=name,
        grid_spec=pltpu.PrefetchScalarGridSpec(
            num_scalar_prefetch=1, grid=(3, r // tr),
            in_specs=[pl.BlockSpec((1, tr, c), lambda j, i, s: (s[j], i, 0)),
                      pl.BlockSpec((1, tr, c), lambda j, i, s: (j, i, 0))],
            out_specs=pl.BlockSpec((1, tr, c), lambda j, i, s: (j, i, 0))),
        out_shape=SDS((3, r, c), BF16),
        compiler_params=_cparams(),
    )(slots, g, from_sibling)


def _adamw(w, g, m, v):
    m2 = ADAM_B1 * m + (1.0 - ADAM_B1) * g
    v2 = ADAM_B2 * v + (1.0 - ADAM_B2) * jnp.square(g)
    m_hat = m2 / (1.0 - ADAM_B1 ** ADAM_STEP)
    v_hat = v2 / (1.0 - ADAM_B2 ** ADAM_STEP)
    delta = -ADAM_LR * (m_hat / (jnp.sqrt(v_hat) + ADAM_EPS) + ADAM_WD * w)
    return delta, m2, v2


def _reduce_adamw(name, own, partials, w, m, v):
    r, c = own.shape
    tr = _row_tile(r)
    n_p = len(partials)

    def body(own_ref, *rest):
        p_refs, (w_ref, m_ref, v_ref, grad_ref, d_ref, nm_ref, nv_ref) = rest[:n_p], rest[n_p:]
        grad = own_ref[...]
        for p_ref, (_, _, count) in zip(p_refs, partials):
            for j in range(count):
                grad = grad + p_ref[j].astype(F32)
        grad_ref[...] = grad
        d_ref[...], nm_ref[...], nv_ref[...] = _adamw(w_ref[...], grad, m_ref[...], v_ref[...])

    flat = pl.BlockSpec((tr, c), lambda i: (i, 0))
    return pl.pallas_call(
        body, name=name, grid=(r // tr,),
        in_specs=[flat] + [pl.BlockSpec((count, tr, c), lambda i, first=first, count=count: (first // count, i, 0))
                           for _, first, count in partials] + [flat, flat, flat],
        out_specs=[flat, flat, flat, flat],
        out_shape=[SDS((r, c), F32)] * 4,
        compiler_params=_cparams(),
    )(own, *[p[0] for p in partials], w, m, v)


VM = pl.BlockSpec()


def _small_allreduce(parts):
    n = len(parts)

    def body(*refs):
        ins, outs, lands = refs[:n], refs[n:2 * n], refs[2 * n:3 * n]
        send_sems, recv_sems = refs[3 * n:]
        x, y, c, _ = _place()
        my_slot = _slot(x, y, c)
        copies = []
        for t in range(n):
            lands[t][my_slot] = ins[t][...]
            for k in range(1, N_DEV):
                sem = (N_DEV - 1) * t + k - 1
                copies.append(pltpu.make_async_remote_copy(
                    src_ref=ins[t], dst_ref=lands[t].at[my_slot],
                    send_sem=send_sems.at[sem], recv_sem=recv_sems.at[sem],
                    device_id=_peer(x, y, c, k), device_id_type=MESH))
        for cp in copies:
            cp.start()
        for t in range(n):
            for k in range(1, N_DEV):
                sem = (N_DEV - 1) * t + k - 1
                pltpu.make_async_remote_copy(
                    src_ref=ins[t], dst_ref=lands[t].at[_slot(*_peer(x, y, c, k))],
                    send_sem=send_sems.at[sem], recv_sem=recv_sems.at[sem],
                    device_id=(x, y, c), device_id_type=MESH).wait_recv()
        for cp in copies:
            cp.wait_send()
        for t in range(n):
            total = lands[t][0]
            for s in range(1, N_DEV):
                total = total + lands[t][s]
            outs[t][...] = total

    n_sems = (N_DEV - 1) * n
    return pl.pallas_call(
        body, name="small_allreduce",
        in_specs=[VM] * n, out_specs=[VM] * n, out_shape=[SDS(p.shape, F32) for p in parts],
        scratch_shapes=[pltpu.VMEM((N_DEV, *p.shape), F32) for p in parts]
        + [pltpu.SemaphoreType.DMA((n_sems,)), pltpu.SemaphoreType.DMA((n_sems,))],
    )(*parts)


def _small_adamw(my_slot, sums, ws, ms, vs):
    n = len(ws)

    def body(slot_ref, *refs):
        sum_refs, refs = refs[:n + 1], refs[n + 1:]
        w_refs, m_refs, v_refs, refs = refs[:n], refs[n:2 * n], refs[2 * n:3 * n], refs[3 * n:]
        g_refs, d_refs, nm_refs, nv_refs = refs[:n + 1], refs[n + 1:2 * n + 1], refs[2 * n + 1:3 * n + 1], refs[3 * n + 1:]
        for t in range(n + 1):
            if t == 0:
                g = sum_refs[0][:, pl.ds(pl.multiple_of(slot_ref[0] * 128, 128), 128)]
            else:
                g = sum_refs[t][...]
            g_refs[t][...] = g
            if t < n:
                d_refs[t][...], nm_refs[t][...], nv_refs[t][...] = _adamw(w_refs[t][...], g, m_refs[t][...], v_refs[t][...])

    shapes = [SDS(w.shape, F32) for w in ws]
    outs = pl.pallas_call(
        body, name="small_adamw",
        in_specs=[pl.BlockSpec(memory_space=pltpu.SMEM)] + [VM] * (4 * n + 1),
        out_specs=[VM] * (4 * n + 1),
        out_shape=shapes + [SDS(sums[-1].shape, F32)] + shapes * 3,
    )(my_slot, *sums, *ws, *ms, *vs)
    return outs[:n + 1], outs[n + 1:2 * n + 1], outs[2 * n + 1:3 * n + 1], outs[3 * n + 1:]


def kernel(x, a_norm, a_w_in, a_rel_bias, a_w_out, kv_norm, kv_w, t5_bias, b_norm, b_w_in, b_sinks, b_w_out, final_norm, loss_target, m_a_norm, m_a_w_in, m_a_rel_bias, m_a_w_out, m_kv_norm, m_kv_w, m_t5_bias, m_b_norm, m_b_w_in, m_b_sinks, m_b_w_out, m_final_norm, v_a_norm, v_a_w_in, v_a_rel_bias, v_a_w_out, v_kv_norm, v_kv_w, v_t5_bias, v_b_norm, v_b_w_in, v_b_sinks, v_b_w_out, v_final_norm):
    xi, yi, ci = lax.axis_index("x"), lax.axis_index("y"), lax.axis_index("c")
    my_slot = _slot(xi, yi, ci)

    slot_arr = jnp.reshape(my_slot, (1,)).astype(jnp.int32)
    others = [(1 - xi, yi), (xi, 1 - yi), (1 - xi, 1 - yi)]
    order = jnp.stack([my_slot, _slot(xi, yi, 1 - ci)] + [_slot(*chip, ci) for chip in others]
                      + [_slot(*chip, 1 - ci) for chip in others]).astype(jnp.int32)
    late_shards = [b_w_in[0].astype(BF16), a_w_out[0].astype(BF16), b_w_out[0].astype(BF16), kv_w.astype(BF16)]
    grad_x, loc, matrices = _local_step(
        slot_arr, order, x[0], loss_target[0], a_norm, a_w_in[0].astype(BF16), a_rel_bias[0], late_shards,
        kv_norm.reshape(1, D_MODEL), t5_bias, b_norm, b_sinks, final_norm.reshape(1, D_MODEL))

    shard_w = dict(a_w_in=a_w_in[0], b_w_in=b_w_in[0], a_w_out=a_w_out[0], b_w_out=b_w_out[0], kv_w=kv_w)
    shard_m = dict(a_w_in=m_a_w_in[0], b_w_in=m_b_w_in[0], a_w_out=m_a_w_out[0], b_w_out=m_b_w_out[0], kv_w=m_kv_w)
    shard_v = dict(a_w_in=v_a_w_in[0], b_w_in=v_b_w_in[0], a_w_out=v_a_w_out[0], b_w_out=v_b_w_out[0], kv_w=v_kv_w)
    big = {n: _reduce_adamw("adamw_" + n, own, partials, shard_w[n], shard_m[n], shard_v[n])
           for n, (own, partials) in matrices.items()}

    names = ("a_norm", "a_rel_bias", "kv_norm", "t5_bias", "b_norm", "b_sinks", "final_norm")
    row = lambda a: a.reshape(1, -1) if a.ndim == 1 else (a[0] if a.ndim == 3 else a)
    small_w = [row(a) for a in (a_norm, a_rel_bias, kv_norm, t5_bias, b_norm, b_sinks, final_norm)]
    small_m = [row(a) for a in (m_a_norm, m_a_rel_bias, m_kv_norm, m_t5_bias, m_b_norm, m_b_sinks, m_final_norm)]
    small_v = [row(a) for a in (v_a_norm, v_a_rel_bias, v_kv_norm, v_t5_bias, v_b_norm, v_b_sinks, v_final_norm)]
    sums = dict(zip(names + ("loss",), _small_allreduce([loc[n] for n in names] + [loc["loss"]])))
    sums["a_rel_bias"] = _a_bias_grad(sums["a_rel_bias"])
    sums["t5_bias"] = sums["t5_bias"][:, :T5_BUCKETS].T
    sums["b_sinks"] = sums["b_sinks"][:, 0].reshape(1, N_HEADS)
    results = _small_adamw(slot_arr, [sums[n] for n in names + ("loss",)], small_w, small_m, small_v)
    like = dict(a_norm=a_norm, a_rel_bias=a_rel_bias, kv_norm=kv_norm, t5_bias=t5_bias, b_norm=b_norm,
                b_sinks=b_sinks, final_norm=final_norm)
    sm = [{n: part[i].reshape(like[n].shape) for i, n in enumerate(names)} for part in results]
    loss = results[0][len(names)][0, 0]

    order = ("a_norm", "a_w_in", "a_rel_bias", "a_w_out", "kv_norm", "kv_w", "t5_bias", "b_norm",
             "b_w_in", "b_sinks", "b_w_out", "final_norm")
    lead = dict(a_w_in=True, b_w_in=True, a_w_out=True, b_w_out=True, kv_w=False)

    def pick(kind, name):
        if name in big:
            val = big[name][kind]
            return val[None] if lead[name] else val
        return sm[kind][name]

    outs = [loss, grad_x[None]]
    for kind in range(4):
        outs += [pick(kind, n) for n in order]
    return tuple(outs)
```

```python
import functools
import math

import numpy as np
import jax
import jax.numpy as jnp
from jax import lax
from jax.experimental import pallas as pl
from jax.experimental.pallas import tpu as pltpu

F32 = jnp.float32
BF16 = jnp.bfloat16
SDS = jax.ShapeDtypeStruct

D_MODEL = 1024
HEAD_DIM = 64
CHUNK = 64
N_HEADS = 16
RMS_EPS = 1e-6
A_LEFT_CHUNKS = 8
A_BAND = (A_LEFT_CHUNKS + 1) * CHUNK
A_REL_CLIP = 256
B_KV_HEADS = 2
B_GROUP = 8
B_LEFT_CHUNKS = 2
B_BAND = (B_LEFT_CHUNKS + 1) * CHUNK
T5_BUCKETS = 32
T5_MAX_DIST = 128
QBLK = 256
A_KEYS = 3 * QBLK
B_QBLK_FWD = 128
B_QBLK_BWD = 256
B_PREV = 128
A_DIAG = A_KEYS
NEG = -1e30
SCALE = HEAD_DIM ** -0.5
N_DEV = 8

ADAM_LR = 0.001
ADAM_B1 = 0.9
ADAM_B2 = 0.999
ADAM_EPS = 1e-08
ADAM_WD = 0.01
ADAM_STEP = 10

VMEM_LIMIT_BYTES = 56 * 1024 * 1024
MESH = pl.DeviceIdType.MESH


def _cparams():
    return pltpu.CompilerParams(vmem_limit_bytes=VMEM_LIMIT_BYTES)


def _dot(a, b):
    return jnp.dot(a, b, preferred_element_type=F32)


def _dot_nt(a, b):
    return lax.dot_general(a, b, (((1,), (1,)), ((), ())), preferred_element_type=F32)


def _dot_tn(a, b):
    return lax.dot_general(a, b, (((0,), (0,)), ((), ())), preferred_element_type=F32)


def _rstd(xf):
    return lax.rsqrt(jnp.mean(xf * xf, axis=-1, keepdims=True) + RMS_EPS)


def _sigmoid(x):
    return 1.0 / (1.0 + jnp.exp(-x))


_GATHER_SEQUENCE = ((0, None), (1, 0), (2, 1), (4, None), (5, None), (3, 2), (6, None))


def _gather_order(x, y, c):
    others = [(1 - x, y), (x, 1 - y), (1 - x, 1 - y)]
    arrivals = [_slot(x, y, 1 - c)] + [_slot(*chip, c) for chip in others] + [_slot(*chip, 1 - c) for chip in others]
    return jnp.stack([_slot(x, y, c)] + [arrivals[a] for a, _ in _GATHER_SEQUENCE]).astype(jnp.int32)


def _norm_matmul_gather(order, x, gain_shard, w_shard):
    t = x.shape[0]
    dw, tn = w_shard.shape
    tm = min(t, 1024)
    n_m = t // tm

    def body(order_ref, x_ref, gs_ref, shard_ref, xn_ref, o_ref, full_ref, gain_ref,
             xn_all, wbuf, gland, send_sems, recv_sems, gsend_sems, grecv_sems, load_sems, own_sem):
        n, m = pl.program_id(0), pl.program_id(1)
        x_i, y_i, c_i, chips = _place()
        me, sibling = (x_i, y_i, c_i), (x_i, y_i, 1 - c_i)

        def send(k, block, to, src=None):
            dst = full_ref.at[_slot(*block)]
            return pltpu.make_async_remote_copy(
                src_ref=dst if src is None else src, dst_ref=dst,
                send_sem=send_sems.at[k], recv_sem=recv_sems.at[k], device_id=to, device_id_type=MESH)

        own = pltpu.make_async_copy(shard_ref, full_ref.at[_slot(*me)], own_sem)
        first = [send(0, me, sibling, src=shard_ref)]
        first += [send(1 + j, me, (*chip, c_i), src=shard_ref) for j, chip in enumerate(chips)]
        forwards = [send(4 + j, (*chip, c_i), sibling) for j, chip in enumerate(chips)]
        arrivals = [send(0, sibling, me)] + [send(1 + j, (*chip, c_i), me) for j, chip in enumerate(chips)]
        arrivals += [send(4 + j, (*chip, 1 - c_i), me) for j, chip in enumerate(chips)]
        gains = [pltpu.make_async_remote_copy(
            src_ref=gs_ref, dst_ref=gland.at[_slot(*me)], send_sem=gsend_sems.at[k - 1],
            recv_sem=grecv_sems.at[k - 1], device_id=_peer(x_i, y_i, c_i, k), device_id_type=MESH)
            for k in range(1, N_DEV)]

        @pl.when(jnp.logical_and(n == 0, m == 0))
        def _():
            own.start()
            for cp in gains + first:
                cp.start()
            pltpu.make_async_copy(shard_ref, wbuf.at[0], load_sems.at[0]).start()
            gland[_slot(*me)] = gs_ref[...]
            for k in range(1, N_DEV):
                pltpu.make_async_remote_copy(
                    src_ref=gs_ref, dst_ref=gland.at[_slot(*_peer(x_i, y_i, c_i, k))],
                    send_sem=gsend_sems.at[k - 1], recv_sem=grecv_sems.at[k - 1],
                    device_id=me, device_id_type=MESH).wait_recv()
            for s in range(N_DEV):
                gain_ref[:, 128 * s:128 * (s + 1)] = gland[s]

        rows = pl.ds(pl.multiple_of(m * tm, tm), tm)

        @pl.when(n == 0)
        def _():
            xf = x_ref[...]
            xn = ((xf * _rstd(xf)) * gain_ref[...]).astype(BF16)
            xn_all[rows, :] = xn
            xn_ref[...] = xn

        @pl.when(m == 0)
        def _():
            pltpu.make_async_copy(full_ref.at[0], wbuf.at[n % 2], load_sems.at[n % 2]).wait()

        o_ref[...] = _dot(xn_all[rows, :], wbuf[n % 2]).astype(BF16)

        for k, (arrival, forward) in enumerate(_GATHER_SEQUENCE):
            @pl.when(jnp.logical_and(n == k, m == n_m - 1))
            def _(k=k, arrival=arrival, forward=forward):
                arrivals[arrival].wait_recv()
                if forward is not None:
                    forwards[forward].start()
                pltpu.make_async_copy(full_ref.at[order_ref[k + 1]], wbuf.at[(k + 1) % 2],
                                      load_sems.at[(k + 1) % 2]).start()

        @pl.when(jnp.logical_and(n == N_DEV - 1, m == n_m - 1))
        def _():
            for cp in gains + first + forwards:
                cp.wait_send()
            own.wait()

    held = lambda n, m, order: (jnp.where(n == 0, m, n_m - 1), 0)
    return pl.pallas_call(
        body, name="norm_matmul_gather",
        grid_spec=pltpu.PrefetchScalarGridSpec(
            num_scalar_prefetch=1, grid=(N_DEV, n_m),
            in_specs=[pl.BlockSpec((tm, D_MODEL), held),
                      pl.BlockSpec((1, 128), lambda n, m, order: (0, 0)), ANY],
            out_specs=[pl.BlockSpec((tm, D_MODEL), held),
                       pl.BlockSpec((tm, tn), lambda n, m, order: (m, order[n])),
                       ANY, pl.BlockSpec((1, D_MODEL), lambda n, m, order: (0, 0))],
            scratch_shapes=[pltpu.VMEM((t, D_MODEL), BF16), pltpu.VMEM((2, dw, tn), BF16),
                            pltpu.VMEM((N_DEV, 1, 128), F32),
                            pltpu.SemaphoreType.DMA((7,)), pltpu.SemaphoreType.DMA((7,)),
                            pltpu.SemaphoreType.DMA((7,)), pltpu.SemaphoreType.DMA((7,)),
                            pltpu.SemaphoreType.DMA((2,)), pltpu.SemaphoreType.DMA]),
        out_shape=[SDS((t, D_MODEL), BF16), SDS((t, N_DEV * tn), BF16), SDS((N_DEV, dw, tn), BF16),
                   SDS((1, D_MODEL), F32)],
        compiler_params=_cparams(),
    )(order, x, gain_shard, w_shard)


def _layer_a_out(x, z, w_out, kv_gain, b_gain, kv_w, w_in_b):
    t = x.shape[0]
    tm = min(t, 512)
    nb, _, tn = w_in_b.shape

    def body(x_ref, z_ref, wo_ref, kvg_ref, bg_ref, kvw_ref, wb_ref,
             h1_ref, kvn_ref, hb_ref, kv_ref, qg_ref):
        h1 = x_ref[...] + _dot(z_ref[...], wo_ref[...])
        h1_ref[...] = h1
        y0 = h1 * _rstd(h1)
        kvn = (y0 * kvg_ref[...]).astype(BF16)
        hb = (y0 * bg_ref[...]).astype(BF16)
        kvn_ref[...] = kvn
        hb_ref[...] = hb
        kv_ref[...] = _dot(kvn, kvw_ref[...]).astype(BF16)
        for i in range(nb):
            qg_ref[:, i * tn:(i + 1) * tn] = _dot(hb, wb_ref[i]).astype(BF16)

    row = lambda m: (m, 0)
    fix2 = lambda m: (0, 0)
    return pl.pallas_call(
        body, name="layer_a_out", grid=(t // tm,),
        in_specs=[pl.BlockSpec((tm, D_MODEL), row), pl.BlockSpec((tm, D_MODEL), row),
                  pl.BlockSpec((D_MODEL, D_MODEL), fix2),
                  pl.BlockSpec((1, D_MODEL), fix2), pl.BlockSpec((1, D_MODEL), fix2),
                  pl.BlockSpec((D_MODEL, 256), fix2),
                  pl.BlockSpec((nb, D_MODEL, tn), lambda m: (0, 0, 0))],
        out_specs=[pl.BlockSpec((tm, D_MODEL), row), pl.BlockSpec((tm, D_MODEL), row),
                   pl.BlockSpec((tm, D_MODEL), row), pl.BlockSpec((tm, 256), row),
                   pl.BlockSpec((tm, nb * tn), row)],
        out_shape=[SDS((t, D_MODEL), F32), SDS((t, D_MODEL), BF16), SDS((t, D_MODEL), BF16),
                   SDS((t, 256), BF16), SDS((t, nb * tn), BF16)],
        compiler_params=_cparams(),
    )(x, z, w_out, kv_gain, b_gain, kv_w, w_in_b)


def _layer_b_out_loss(h1, z, w_out, f_gain, target):
    t = h1.shape[0]
    tm = min(t, 512)

    def body(h1_ref, z_ref, wo_ref, fg_ref, tgt_ref,
             dh2_ref, dh2b_ref, dz_ref, loss_ref, dfn_ref):
        @pl.when(pl.program_id(0) == 0)
        def _():
            loss_ref[...] = jnp.zeros_like(loss_ref)
            dfn_ref[...] = jnp.zeros_like(dfn_ref)

        h2 = h1_ref[...] + _dot(z_ref[...], wo_ref[...])
        r = _rstd(h2)
        yn = h2 * r
        fg = fg_ref[...]
        err = yn * fg - tgt_ref[...]
        loss_ref[...] += (0.5 / D_MODEL) * jnp.sum(err * err)
        dy = err * (1.0 / D_MODEL)
        dfn_ref[...] += jnp.sum(dy * yn, axis=0, keepdims=True)
        u = dy * fg
        dh2 = r * u - h2 * ((r * r * r) * jnp.mean(u * h2, axis=-1, keepdims=True))
        dh2_ref[...] = dh2
        dh2b = dh2.astype(BF16)
        dh2b_ref[...] = dh2b
        dz_ref[...] = _dot_nt(dh2b, wo_ref[...]).astype(BF16)

    row = lambda m: (m, 0)
    fix2 = lambda m: (0, 0)
    return pl.pallas_call(
        body, name="layer_b_out_loss", grid=(t // tm,),
        in_specs=[pl.BlockSpec((tm, D_MODEL), row), pl.BlockSpec((tm, D_MODEL), row),
                  pl.BlockSpec((D_MODEL, D_MODEL), fix2), pl.BlockSpec((1, D_MODEL), fix2),
                  pl.BlockSpec((tm, D_MODEL), row)],
        out_specs=[pl.BlockSpec((tm, D_MODEL), row), pl.BlockSpec((tm, D_MODEL), row),
                   pl.BlockSpec((tm, D_MODEL), row), pl.BlockSpec((1, 128), fix2),
                   pl.BlockSpec((1, D_MODEL), fix2)],
        out_shape=[SDS((t, D_MODEL), F32), SDS((t, D_MODEL), BF16), SDS((t, D_MODEL), BF16),
                   SDS((1, 128), F32), SDS((1, D_MODEL), F32)],
        compiler_params=_cparams(),
    )(h1, z, w_out, f_gain, target)


def _layer_b_in_bwd(dqg, dkv, w_in_b, kv_w, h1, dh2, b_gain, kv_gain, w_out_a):
    t = h1.shape[0]
    tm = min(t, 256)
    nb, _, tn = w_in_b.shape
    per = D_MODEL // tn

    def body(dqg_ref, dkv_ref, wb_ref, kvw_ref, h1_ref, dh2_ref, bg_ref, kvg_ref, wo_ref,
             dh1_ref, dh1b_ref, dz_ref, dbn_ref, dkn_ref):
        @pl.when(pl.program_id(0) == 0)
        def _():
            dbn_ref[...] = jnp.zeros_like(dbn_ref)
            dkn_ref[...] = jnp.zeros_like(dkn_ref)

        dhb = jnp.zeros((tm, D_MODEL), F32)
        for i in range(nb):
            blk = dqg_ref[i // per, :, (i % per) * tn:(i % per + 1) * tn]
            dhb = dhb + _dot_nt(blk, wb_ref[i])
        dkn = (_dot_nt(dkv_ref[0].astype(BF16), kvw_ref[:, 0:128])
               + _dot_nt(dkv_ref[1].astype(BF16), kvw_ref[:, 128:256]))
        h1 = h1_ref[...]
        r = _rstd(h1)
        xr = h1 * r
        dbn_ref[...] += jnp.sum(dhb * xr, axis=0, keepdims=True)
        dkn_ref[...] += jnp.sum(dkn * xr, axis=0, keepdims=True)
        u = dhb * bg_ref[...] + dkn * kvg_ref[...]
        dh1 = dh2_ref[...] + r * u - h1 * ((r * r * r) * jnp.mean(u * h1, axis=-1, keepdims=True))
        dh1_ref[...] = dh1
        dh1b = dh1.astype(BF16)
        dh1b_ref[...] = dh1b
        dz_ref[...] = _dot_nt(dh1b, wo_ref[...]).astype(BF16)

    row = lambda m: (m, 0)
    fix2 = lambda m: (0, 0)
    return pl.pallas_call(
        body, name="layer_b_in_bwd", grid=(t // tm,),
        in_specs=[pl.BlockSpec((2, tm, D_MODEL), lambda m: (0, m, 0)),
                  pl.BlockSpec((2, tm, 128), lambda m: (0, m, 0)),
                  pl.BlockSpec((nb, D_MODEL, tn), lambda m: (0, 0, 0)),
                  pl.BlockSpec((D_MODEL, 256), fix2),
                  pl.BlockSpec((tm, D_MODEL), row), pl.BlockSpec((tm, D_MODEL), row),
                  pl.BlockSpec((1, D_MODEL), fix2), pl.BlockSpec((1, D_MODEL), fix2),
                  pl.BlockSpec((D_MODEL, D_MODEL), fix2)],
        out_specs=[pl.BlockSpec((tm, D_MODEL), row), pl.BlockSpec((tm, D_MODEL), row),
                   pl.BlockSpec((tm, D_MODEL), row), pl.BlockSpec((1, D_MODEL), fix2),
                   pl.BlockSpec((1, D_MODEL), fix2)],
        out_shape=[SDS((t, D_MODEL), F32), SDS((t, D_MODEL), BF16), SDS((t, D_MODEL), BF16),
                   SDS((1, D_MODEL), F32), SDS((1, D_MODEL), F32)],
        compiler_params=_cparams(),
    )(dqg, dkv, w_in_b, kv_w, h1, dh2, b_gain, kv_gain, w_out_a)


def _layer_a_in_bwd(dqg, dkv, w_in_a, x, dh1, a_gain, chip_sums):
    t = x.shape[0]
    tm = min(t, 256)
    nb, _, tn = w_in_a.shape
    per = D_MODEL // tn

    def body(dqg_ref, dkv_ref, w_ref, x_ref, dh1_ref, ag_ref, sums_ref, dx_ref, dan_ref, land_ref,
             send_sems, recv_sems):
        @pl.when(pl.program_id(0) == 0)
        def _():
            dan_ref[...] = jnp.zeros_like(dan_ref)
            for cp in _chip_copies(sums_ref, land_ref, send_sems, recv_sems):
                cp.start()

        dxn = jnp.zeros((tm, D_MODEL), F32)
        for i in range(nb):
            part = i // per
            src = dqg_ref if part in (0, 3) else dkv_ref
            outer = {0: 0, 3: 1, 1: 0, 2: 1}[part]
            blk = src[outer, :, (i % per) * tn:(i % per + 1) * tn]
            dxn = dxn + _dot_nt(blk, w_ref[i])
        xf = x_ref[...]
        r = _rstd(xf)
        dan_ref[...] += jnp.sum(dxn * (xf * r), axis=0, keepdims=True)
        u = dxn * ag_ref[...]
        dx_ref[...] = dh1_ref[...] + r * u - xf * ((r * r * r) * jnp.mean(u * xf, axis=-1, keepdims=True))

        @pl.when(pl.program_id(0) == t // tm - 1)
        def _():
            for cp in _chip_copies(sums_ref, land_ref, send_sems, recv_sems):
                cp.wait()

    row = lambda m: (m, 0)
    fix2 = lambda m: (0, 0)
    return pl.pallas_call(
        body, name="layer_a_in_bwd", grid=(t // tm,),
        in_specs=[pl.BlockSpec((2, tm, D_MODEL), lambda m: (0, m, 0)),
                  pl.BlockSpec((2, tm, D_MODEL), lambda m: (0, m, 0)),
                  pl.BlockSpec((nb, D_MODEL, tn), lambda m: (0, 0, 0)),
                  pl.BlockSpec((tm, D_MODEL), row), pl.BlockSpec((tm, D_MODEL), row),
                  pl.BlockSpec((1, D_MODEL), fix2), ANY],
        out_specs=[pl.BlockSpec((tm, D_MODEL), row), pl.BlockSpec((1, D_MODEL), fix2), ANY],
        out_shape=[SDS((t, D_MODEL), F32), SDS((1, D_MODEL), F32), SDS(chip_sums.shape, chip_sums.dtype)],
        scratch_shapes=[pltpu.SemaphoreType.DMA((3,)), pltpu.SemaphoreType.DMA((3,))],
        compiler_params=_cparams(),
    )(dqg, dkv, w_in_a, x, dh1, a_gain, chip_sums)


def _lut(s, vals):
    r = jnp.int32(vals[0])
    for i in range(1, len(vals)):
        r = jnp.where(s == i, jnp.int32(vals[i]), r)
    return r


def _held(steps, i):
    seq, cur = [None] * len(steps), None
    for k in range(len(steps) - 1, -1, -1):
        if steps[k][0] == i:
            cur = steps[k][1:3]
        seq[k] = cur
    for k in range(len(steps)):
        cur = seq[k] = seq[k] if seq[k] is not None else cur
    return seq


def _weight_grad_cols(name, my_slot, a, bs, steps, tn):
    t, dw = a.shape
    n_arr = len(bs)
    which = [s[0] for s in steps]
    blks = [s[3] for s in steps]

    def body(slot_ref, a_ref, *rest):
        b_refs, (o_ref, own_ref, at_ref) = rest[:n_arr], rest[n_arr:]
        s = pl.program_id(0)

        @pl.when(s == 0)
        def _():
            at_ref[...] = a_ref[...].T

        for i in range(n_arr):
            @pl.when(_lut(s, which) == i)
            def _(i=i):
                res = _dot(at_ref[...], b_refs[i][0])
                o_ref[0] = res.astype(BF16)

                @pl.when(_lut(s, blks) == slot_ref[0])
                def _():
                    own_ref[...] = res

    def b_spec(i):
        held = _held(steps, i)
        return pl.BlockSpec((1, t, tn), lambda s, slot: (_lut(s, [h[0] for h in held]), 0,
                                                         _lut(s, [h[1] for h in held])))

    return pl.pallas_call(
        body, name=name,
        grid_spec=pltpu.PrefetchScalarGridSpec(
            num_scalar_prefetch=1, grid=(len(steps),),
            in_specs=[pl.BlockSpec((t, dw), lambda s, slot: (0, 0))] + [b_spec(i) for i in range(n_arr)],
            out_specs=[pl.BlockSpec((1, dw, tn), lambda s, slot: (_lut(s, blks), 0, 0)),
                       pl.BlockSpec((dw, tn), lambda s, slot: (0, 0))],
            scratch_shapes=[pltpu.VMEM((dw, t), BF16)]),
        out_shape=[SDS((N_DEV, dw, tn), BF16), SDS((dw, tn), F32)],
        compiler_params=_cparams(),
    )(my_slot, a, *bs)


def _weight_grad_rows(name, my_slot, a, b):
    t, dw = a.shape
    n_o, _, c = b.shape
    rows = dw // N_DEV

    def body(slot_ref, a_ref, b_ref, o_ref, own_ref):
        at = a_ref[...].T
        res = [_dot(at, b_ref[o].astype(BF16)) for o in range(n_o)]
        for o in range(n_o):
            o_ref[0, :, o * c:(o + 1) * c] = res[o].astype(BF16)

        @pl.when(pl.program_id(0) == slot_ref[0])
        def _():
            for o in range(n_o):
                own_ref[:, o * c:(o + 1) * c] = res[o]

    return pl.pallas_call(
        body, name=name,
        grid_spec=pltpu.PrefetchScalarGridSpec(
            num_scalar_prefetch=1, grid=(N_DEV,),
            in_specs=[pl.BlockSpec((t, rows), lambda s, slot: (0, s)),
                      pl.BlockSpec((n_o, t, c), lambda s, slot: (0, 0, 0))],
            out_specs=[pl.BlockSpec((1, rows, n_o * c), lambda s, slot: (s, 0, 0)),
                       pl.BlockSpec((rows, n_o * c), lambda s, slot: (0, 0))]),
        out_shape=[SDS((N_DEV, rows, n_o * c), BF16), SDS((rows, n_o * c), F32)],
        compiler_params=_cparams(),
    )(my_slot, a, b)


def _lane_lo():
    return lax.broadcasted_iota(jnp.int32, (1, 128), 1) < HEAD_DIM


def _offset_sums(gt):
    keys = gt.shape[1]
    gc = gt[0:CHUNK]
    for cc in range(1, gt.shape[0] // CHUNK):
        gc = gc + pltpu.roll(gt[cc * CHUNK:(cc + 1) * CHUNK], keys - cc * CHUNK, 1)
    hi = gc.astype(BF16)
    lo = (gc - hi.astype(F32)).astype(BF16)
    flip = (lax.broadcasted_iota(jnp.int32, (CHUNK, CHUNK), 0)
            + lax.broadcasted_iota(jnp.int32, (CHUNK, CHUNK), 1) == CHUNK - 1).astype(BF16)
    gf = _dot(flip, hi) + _dot(flip, lo)
    skew = pltpu.roll(gf, 0, 1, stride=1, stride_axis=0)
    return jnp.sum(skew, axis=0, keepdims=True)


def _band_bias(w_row, band, rows):
    keys = w_row.shape[1]
    base = jnp.broadcast_to(w_row, (CHUNK, keys))
    skew = pltpu.roll(base, 0, 1, stride=1, stride_axis=0)
    skew = pltpu.roll(skew, keys - (CHUNK - 1), 1)
    col = lax.broadcasted_iota(jnp.int32, (CHUNK, keys), 1)
    chunk0 = jnp.where(col < band, skew, NEG)
    return jnp.concatenate(
        [chunk0] + [pltpu.roll(chunk0, cc * CHUNK, 1) for cc in range(1, rows // CHUNK)], axis=0)


def _silu_parts(g):
    sg = _sigmoid(g)
    return g * sg, sg * (1.0 + g * (1.0 - sg))


A_PAIRS = 2
A_LANES = 128 * A_PAIRS
A_STEPS = D_MODEL // A_LANES


def _a_specs():
    q = pl.BlockSpec((QBLK, A_LANES), lambda p, j: (j, p))
    ks = [pl.BlockSpec((QBLK, A_LANES), lambda p, j, b=b: (jnp.maximum(j - 2 + b, 0), A_STEPS + p)) for b in range(3)]
    vs = [pl.BlockSpec((QBLK, A_LANES), lambda p, j, b=b: (jnp.maximum(j - 2 + b, 0), 2 * A_STEPS + p))
          for b in range(3)]
    g = pl.BlockSpec((QBLK, A_LANES), lambda p, j: (j, 3 * A_STEPS + p))
    bias = pl.BlockSpec((A_PAIRS, 8, A_KEYS), lambda p, j: (p, 0, 0))
    return q, ks, vs, g, bias


def _a_fill_bias(w_ref, b_ref, j):
    _fill_bias(2 * A_PAIRS, lambda h: w_ref[h // 2, h % 2:h % 2 + 1, :], A_BAND, QBLK * (2 - j), 2, b_ref, j)


def _fill_bias(n, get_row, band, first_valid_col, early, bias_scr, j):
    @pl.when(j == 0)
    def _():
        for h in range(n):
            bias_scr[h] = _band_bias(get_row(h), band, bias_scr.shape[1])

    @pl.when(j < early)
    def _():
        keys = bias_scr.shape[2]
        col_ok = lax.broadcasted_iota(jnp.int32, (1, keys), 1) >= first_valid_col
        for h in range(n):
            bias_scr[n + h] = jnp.where(col_ok, bias_scr[h], NEG)


def _head_logits(q, k, bias_scr, idx, sel):
    qm = jnp.where(sel, q, jnp.zeros_like(q)) * SCALE
    return qm, _dot_nt(qm, k) + bias_scr[idx]


def _row_sums_everywhere(r, sel):
    return jnp.where(sel, pltpu.roll(r, HEAD_DIM, 1), r)


def _own_everywhere(x, sel):
    return jnp.where(sel, x, pltpu.roll(x, HEAD_DIM, 1))


def _minus_rows(s, row_full):
    return jnp.concatenate([s[:, i:i + 128] - row_full for i in range(0, s.shape[1], 128)], axis=1)


def _attn_a_fwd(qkvg, bias, gather):
    t = qkvg.shape[0]
    nq = t // QBLK
    n_g = len(gather)
    q_spec, k_specs, v_specs, g_spec, bias_spec = _a_specs()

    def body(q_ref, k0, k1, k2, v0, v1, v2, g_ref, w_ref, *rest):
        shard_refs, rest = rest[:n_g], rest[n_g:]
        z_ref, o_ref, lse_ref = rest[:3]
        full_refs, (b_ref, *comm) = rest[3:3 + n_g], rest[3 + n_g:]
        p = pl.program_id(0)
        j = pl.program_id(1)
        start, forward, finish = _gather_phases(shard_refs, full_refs, *comm)
        pl.when(jnp.logical_and(p == 0, j == 0))(start)
        pl.when(jnp.logical_and(p == A_STEPS // 2, j == 0))(forward)
        _a_fill_bias(w_ref, b_ref, j)
        early = (j < 2).astype(jnp.int32)
        lane_lo = _lane_lo()
        for pp in range(A_PAIRS):
            cols = slice(128 * pp, 128 * (pp + 1))
            q = q_ref[:, cols]
            k = jnp.concatenate([k0[:, cols], k1[:, cols], k2[:, cols]], axis=0)
            v = jnp.concatenate([v0[:, cols], v1[:, cols], v2[:, cols]], axis=0)
            outs, lses = [], []
            for hh in range(2):
                sel = lane_lo if hh == 0 else jnp.logical_not(lane_lo)
                _, s = _head_logits(q, k, b_ref, 2 * pp + hh + 2 * A_PAIRS * early, sel)
                mx = jnp.max(s, axis=-1, keepdims=True)
                e = jnp.exp(s - mx).astype(BF16)
                r = _dot(e, jnp.where(sel, v, jnp.ones_like(v)))
                l = _row_sums_everywhere(r, sel)
                outs.append(r / l)
                lses.append(mx + jnp.log(l))
            o = jnp.where(lane_lo, outs[0], outs[1])
            silu, _ = _silu_parts(g_ref[:, cols].astype(F32))
            o_ref[:, cols] = o.astype(BF16)
            z_ref[:, cols] = (o * silu).astype(BF16)
            lse_ref[:, cols] = jnp.where(lane_lo, lses[0], lses[1])
        pl.when(jnp.logical_and(p == A_STEPS - 1, j == nq - 1))(finish)

    out_spec = pl.BlockSpec((QBLK, A_LANES), lambda p, j: (j, p))
    outs = pl.pallas_call(
        body, name="attn_a_fwd", grid=(A_STEPS, nq),
        in_specs=[q_spec, *k_specs, *v_specs, g_spec, bias_spec] + [ANY] * n_g,
        out_specs=[out_spec, out_spec, out_spec] + [ANY] * n_g,
        out_shape=[SDS((t, D_MODEL), BF16), SDS((t, D_MODEL), BF16), SDS((t, D_MODEL), F32)]
        + [SDS((N_DEV, *s.shape), s.dtype) for s in gather],
        scratch_shapes=[pltpu.VMEM((4 * A_PAIRS, QBLK, A_KEYS), F32)] + _gather_scratch(n_g),
        compiler_params=_cparams(),
    )(qkvg, qkvg, qkvg, qkvg, qkvg, qkvg, qkvg, qkvg, bias, *gather)
    return outs[0], outs[1], outs[2], list(outs[3:])


def _attn_a_bwd(qkvg, bias, out_a, lse, dz, scatter):
    t = qkvg.shape[0]
    nq = t // QBLK
    n_sc = len(scatter)
    q_spec, k_specs, v_specs, g_spec, bias_spec = _a_specs()

    def body(q_ref, k0, k1, k2, v0, v1, v2, g_ref, w_ref, o_ref, lse_ref, dz_ref, *rest):
        sc_refs, rest = rest[:n_sc], rest[n_sc:]
        dqg_ref, dkv_ref, dg_ref = rest[:3]
        land_refs, rest = rest[3:3 + n_sc], rest[3 + n_sc:]
        dk_acc, dv_acc, gt_acc, b_ref, send_sems, recv_sems = rest
        j = pl.program_id(1)
        first = jnp.logical_and(pl.program_id(0) == 0, j == 0)
        last = jnp.logical_and(pl.program_id(0) == A_STEPS - 1, j == nq - 1)

        @pl.when(first)
        def _():
            for cp in _scatter_copies(sc_refs, land_refs, send_sems, recv_sems):
                cp.start()

        _a_fill_bias(w_ref, b_ref, j)

        @pl.when(j == 0)
        def _():
            dk_acc[...] = jnp.zeros_like(dk_acc)
            dv_acc[...] = jnp.zeros_like(dv_acc)
            gt_acc[...] = jnp.zeros_like(gt_acc)

        early = (j < 2).astype(jnp.int32)
        lane_lo = _lane_lo()
        for pp in range(A_PAIRS):
            cols = slice(128 * pp, 128 * (pp + 1))
            q = q_ref[:, cols]
            k = jnp.concatenate([k0[:, cols], k1[:, cols], k2[:, cols]], axis=0)
            v = jnp.concatenate([v0[:, cols], v1[:, cols], v2[:, cols]], axis=0)
            o = o_ref[:, cols].astype(F32)
            lse_pair = lse_ref[:, cols]
            dzf = dz_ref[:, cols].astype(F32)
            silu, dsilu = _silu_parts(g_ref[:, cols].astype(F32))
            do = dzf * silu
            dqg_ref[1, :, cols] = (dzf * o * dsilu).astype(BF16)
            doo = do * o
            dqs = []
            dk_blk = jnp.zeros((A_KEYS, 128), F32)
            dv_blk = jnp.zeros((A_KEYS, 128), F32)
            for hh in range(2):
                sel = lane_lo if hh == 0 else jnp.logical_not(lane_lo)
                qm, s = _head_logits(q, k, b_ref, 2 * pp + hh + 2 * A_PAIRS * early, sel)
                p = jnp.exp(_minus_rows(s, _own_everywhere(lse_pair, sel)))
                delta = jnp.sum(jnp.where(sel, doo, 0.0), axis=-1, keepdims=True)
                dom = jnp.where(sel, do, 0.0).astype(BF16)
                dp = _dot_nt(dom, v)
                ds = p * (dp - delta)
                gt_acc[2 * pp + hh] += ds
                dsb = ds.astype(BF16)
                dqs.append(_dot(dsb, k) * SCALE)
                dk_blk = dk_blk + _dot_tn(dsb, qm)
                dv_blk = dv_blk + _dot_tn(p.astype(BF16), dom)
            dqg_ref[0, :, cols] = jnp.where(lane_lo, dqs[0], dqs[1]).astype(BF16)
            for b in range(3):
                @pl.when(j - 2 + b >= 0)
                def _(b=b, cols=cols, dk_blk=dk_blk, dv_blk=dv_blk):
                    rows = pl.ds(pl.multiple_of((j - 2 + b) * QBLK, QBLK), QBLK)
                    dk_acc[rows, cols] += dk_blk[b * QBLK:(b + 1) * QBLK]
                    dv_acc[rows, cols] += dv_blk[b * QBLK:(b + 1) * QBLK]

        @pl.when(j == nq - 1)
        def _():
            dkv_ref[0] = dk_acc[...].astype(BF16)
            dkv_ref[1] = dv_acc[...].astype(BF16)
            for pp in range(A_PAIRS):
                dg_ref[pp] = jnp.concatenate([_offset_sums(gt_acc[2 * pp]), _offset_sums(gt_acc[2 * pp + 1]),
                                              jnp.zeros((6, A_DIAG), F32)], axis=0)

        @pl.when(last)
        def _():
            for cp in _scatter_copies(sc_refs, land_refs, send_sems, recv_sems):
                cp.wait()

    blk = pl.BlockSpec((QBLK, A_LANES), lambda p, j: (j, p))
    outs = pl.pallas_call(
        body, name="attn_a_bwd", grid=(A_STEPS, nq),
        in_specs=[q_spec, *k_specs, *v_specs, g_spec, bias_spec, blk, blk, blk] + [ANY] * n_sc,
        out_specs=[pl.BlockSpec((2, QBLK, A_LANES), lambda p, j: (0, j, p)),
                   pl.BlockSpec((2, t, A_LANES), lambda p, j: (0, 0, p)),
                   pl.BlockSpec((A_PAIRS, 8, A_DIAG), lambda p, j: (p, 0, 0))] + [ANY] * n_sc,
        out_shape=[SDS((2, t, D_MODEL), BF16), SDS((2, t, D_MODEL), BF16), SDS((N_HEADS // 2, 8, A_DIAG), F32)]
        + [SDS((N_DEV - 1, *g.shape[1:]), g.dtype) for g in scatter],
        scratch_shapes=[pltpu.VMEM((t, A_LANES), F32), pltpu.VMEM((t, A_LANES), F32),
                        pltpu.VMEM((2 * A_PAIRS, QBLK, A_KEYS), F32), pltpu.VMEM((4 * A_PAIRS, QBLK, A_KEYS), F32),
                        pltpu.SemaphoreType.DMA(((N_DEV - 1) * n_sc,)),
                        pltpu.SemaphoreType.DMA(((N_DEV - 1) * n_sc,))],
        compiler_params=_cparams(),
    )(qkvg, qkvg, qkvg, qkvg, qkvg, qkvg, qkvg, qkvg, bias, out_a, lse, dz, *scatter)
    return outs[0], outs[1], outs[2], list(outs[3:])


def _b_specs(qblk):
    per = qblk // B_PREV
    q = pl.BlockSpec((qblk, 512), lambda h, j: (j, h))
    g = pl.BlockSpec((qblk, 512), lambda h, j: (j, 2 + h))
    kp = pl.BlockSpec((B_PREV, 128), lambda h, j: (jnp.maximum(per * j - 1, 0), 0))
    kc = pl.BlockSpec((qblk, 128), lambda h, j: (j, 0))
    vp = pl.BlockSpec((B_PREV, 128), lambda h, j: (jnp.maximum(per * j - 1, 0), 1))
    vc = pl.BlockSpec((qblk, 128), lambda h, j: (j, 1))
    bias = pl.BlockSpec((B_GROUP, qblk + B_PREV), lambda h, j: (h, 0))
    sinks = pl.BlockSpec(memory_space=pltpu.SMEM)
    return q, g, kp, kc, vp, vc, bias, sinks


def _b_operands(kp, kc, vp, vc, kvh):
    k = jnp.concatenate([kp[...], kc[...]], axis=0)
    v = jnp.concatenate([vp[...], vc[...]], axis=0)
    kr = pltpu.roll(k, HEAD_DIM, 1)
    vr = pltpu.roll(v, HEAD_DIM, 1)
    first = kvh == 0
    return (jnp.where(first, k, kr), jnp.where(first, kr, k),
            jnp.where(first, v, vr), jnp.where(first, vr, v))


def _attn_b_fwd(qg, kv, bias, sinks):
    t = qg.shape[0]
    qblk = B_QBLK_FWD
    q_spec, g_spec, kp_spec, kc_spec, vp_spec, vc_spec, bias_spec, sink_spec = _b_specs(qblk)

    def body(q_ref, g_ref, kp, kc, vp, vc, w_ref, sink_ref, z_ref, o_ref, lse_ref, b_ref):
        kvh = pl.program_id(0)
        j = pl.program_id(1)
        _fill_bias(B_GROUP, lambda h: w_ref[h:h + 1, :], B_BAND, B_PREV, 1, b_ref, j)
        early = (j < 1).astype(jnp.int32)
        lane_lo = _lane_lo()
        k_lo, k_hi, v_lo, v_hi = _b_operands(kp, kc, vp, vc, kvh)
        for pp in range(B_GROUP // 2):
            cols = slice(128 * pp, 128 * (pp + 1))
            qp = q_ref[:, cols]
            outs, lses = [], []
            for hh in range(2):
                g = 2 * pp + hh
                sel = lane_lo if hh == 0 else jnp.logical_not(lane_lo)
                sink = sink_ref[kvh * B_GROUP + g]
                vv = v_lo if hh == 0 else v_hi
                _, s = _head_logits(qp, k_lo if hh == 0 else k_hi, b_ref, g + B_GROUP * early, sel)
                mx = jnp.maximum(jnp.max(s, axis=-1, keepdims=True), sink)
                e = jnp.exp(s - mx).astype(BF16)
                r = _dot(e, jnp.where(sel, vv, jnp.ones_like(vv)))
                l = _row_sums_everywhere(r, sel) + jnp.exp(sink - mx)
                outs.append(r / l)
                lses.append(mx + jnp.log(l))
            o = jnp.where(lane_lo, outs[0], outs[1])
            silu, _ = _silu_parts(g_ref[:, cols].astype(F32))
            o_ref[:, cols] = o.astype(BF16)
            z_ref[:, cols] = (o * silu).astype(BF16)
            lse_ref[:, cols] = jnp.where(lane_lo, lses[0], lses[1])

    out_spec = pl.BlockSpec((qblk, 512), lambda h, j: (j, h))
    return pl.pallas_call(
        body, name="attn_b_fwd", grid=(B_KV_HEADS, t // qblk),
        in_specs=[q_spec, g_spec, kp_spec, kc_spec, vp_spec, vc_spec, bias_spec, sink_spec],
        out_specs=[out_spec, out_spec, out_spec],
        out_shape=[SDS((t, D_MODEL), BF16), SDS((t, D_MODEL), BF16), SDS((t, D_MODEL), F32)],
        scratch_shapes=[pltpu.VMEM((2 * B_GROUP, qblk, qblk + B_PREV), F32)],
        compiler_params=_cparams(),
    )(qg, qg, kv, kv, kv, kv, bias, sinks)


def _attn_b_bwd(qg, kv, bias, sinks, out_b, lse, dz, bucket_onehot):
    t = qg.shape[0]
    qblk = B_QBLK_BWD
    keys = qblk + B_PREV
    nq = t // qblk
    q_spec, g_spec, kp_spec, kc_spec, vp_spec, vc_spec, bias_spec, sink_spec = _b_specs(qblk)

    def body(q_ref, g_ref, kp, kc, vp, vc, w_ref, sink_ref, o_ref, lse_ref, dz_ref, oh_ref,
             dqg_ref, dkv_ref, dt5_ref, dsink_ref, gt_acc, b_ref):
        kvh = pl.program_id(0)
        j = pl.program_id(1)
        _fill_bias(B_GROUP, lambda h: w_ref[h:h + 1, :], B_BAND, B_PREV, 1, b_ref, j)

        @pl.when(jnp.logical_and(kvh == 0, j == 0))
        def _():
            dkv_ref[...] = jnp.zeros_like(dkv_ref)

        @pl.when(j == 0)
        def _():
            gt_acc[...] = jnp.zeros_like(gt_acc)
            dsink_ref[...] = jnp.zeros_like(dsink_ref)

        early = (j < 1).astype(jnp.int32)
        lane_lo = _lane_lo()
        k_lo, k_hi, v_lo, v_hi = _b_operands(kp, kc, vp, vc, kvh)
        dk_blk = jnp.zeros((keys, 128), F32)
        dv_blk = jnp.zeros((keys, 128), F32)
        for pp in range(B_GROUP // 2):
            cols = slice(128 * pp, 128 * (pp + 1))
            qp = q_ref[:, cols]
            o = o_ref[:, cols].astype(F32)
            lse_pair = lse_ref[:, cols]
            dzf = dz_ref[:, cols].astype(F32)
            silu, dsilu = _silu_parts(g_ref[:, cols].astype(F32))
            do = dzf * silu
            dqg_ref[1, :, cols] = (dzf * o * dsilu).astype(BF16)
            doo = do * o
            dqs = []
            for hh in range(2):
                g = 2 * pp + hh
                sel = lane_lo if hh == 0 else jnp.logical_not(lane_lo)
                sink = sink_ref[kvh * B_GROUP + g]
                kk = k_lo if hh == 0 else k_hi
                vv = v_lo if hh == 0 else v_hi
                qm, s = _head_logits(qp, kk, b_ref, g + B_GROUP * early, sel)
                lse_h = _own_everywhere(lse_pair, sel)
                p = jnp.exp(_minus_rows(s, lse_h))
                delta = jnp.sum(jnp.where(sel, doo, 0.0), axis=-1, keepdims=True)
                dom = jnp.where(sel, do, 0.0).astype(BF16)
                dp = _dot_nt(dom, vv)
                ds = p * (dp - delta)
                gt_acc[g] += ds
                dsink_ref[g:g + 1, :] -= jnp.sum(jnp.exp(sink - lse_h) * delta, axis=0, keepdims=True)
                dsb = ds.astype(BF16)
                dqs.append(_dot(dsb, kk) * SCALE)
                dk_blk = dk_blk + _dot_tn(dsb, qm)
                dv_blk = dv_blk + _dot_tn(p.astype(BF16), dom)
            dqg_ref[0, :, cols] = jnp.where(lane_lo, dqs[0], dqs[1]).astype(BF16)
        mine = lane_lo == (kvh == 0)
        dk_add = jnp.where(mine, dk_blk + pltpu.roll(dk_blk, HEAD_DIM, 1), 0.0)
        dv_add = jnp.where(mine, dv_blk + pltpu.roll(dv_blk, HEAD_DIM, 1), 0.0)

        @pl.when(j >= 1)
        def _():
            rows = pl.ds(pl.multiple_of(j * qblk - B_PREV, B_PREV), B_PREV)
            dkv_ref[0, rows, :] += dk_add[0:B_PREV]
            dkv_ref[1, rows, :] += dv_add[0:B_PREV]

        rows = pl.ds(pl.multiple_of(j * qblk, qblk), qblk)
        dkv_ref[0, rows, :] += dk_add[B_PREV:keys]
        dkv_ref[1, rows, :] += dv_add[B_PREV:keys]

        @pl.when(j == nq - 1)
        def _():
            dd = jnp.concatenate([_offset_sums(gt_acc[g]) for g in range(B_GROUP)], axis=0)
            hi = dd.astype(BF16)
            lo = (dd - hi.astype(F32)).astype(BF16)
            dt5_ref[...] = _dot(hi, oh_ref[...]) + _dot(lo, oh_ref[...])

    blk = pl.BlockSpec((qblk, 512), lambda h, j: (j, h))
    return pl.pallas_call(
        body, name="attn_b_bwd", grid=(B_KV_HEADS, nq),
        in_specs=[q_spec, g_spec, kp_spec, kc_spec, vp_spec, vc_spec, bias_spec, sink_spec, blk, blk, blk,
                  pl.BlockSpec((keys, 128), lambda h, j: (0, 0))],
        out_specs=[pl.BlockSpec((2, qblk, 512), lambda h, j: (0, j, h)),
                   pl.BlockSpec((2, t, 128), lambda h, j: (0, 0, 0)),
                   pl.BlockSpec((B_GROUP, 128), lambda h, j: (h, 0)),
                   pl.BlockSpec((B_GROUP, 128), lambda h, j: (h, 0))],
        out_shape=[SDS((2, t, D_MODEL), BF16), SDS((2, t, 128), F32),
                   SDS((N_HEADS, 128), F32), SDS((N_HEADS, 128), F32)],
        scratch_shapes=[pltpu.VMEM((B_GROUP, qblk, keys), F32), pltpu.VMEM((2 * B_GROUP, qblk, keys), F32)],
        compiler_params=_cparams(),
    )(qg, qg, kv, kv, kv, kv, bias, sinks, out_b, lse, dz, bucket_onehot)


def _a_bias_by_offset(rel_bias):
    m = np.arange(A_DIAG)
    idx = np.clip(A_BAND - 1 - m, -A_REL_CLIP, A_REL_CLIP) + A_REL_CLIP
    by_head = rel_bias[idx].T.reshape(N_HEADS // 2, 2, A_DIAG)
    return jnp.concatenate([by_head, jnp.zeros((N_HEADS // 2, 6, A_DIAG), F32)], axis=1)


def _a_bias_grad(offset_sums):
    first = 319
    tail = jnp.sum(offset_sums[:, :first], axis=1)
    body = jnp.flip(offset_sums[:, first:first + 320], axis=1)
    body = body.at[:, -1].add(tail)
    full = jnp.concatenate([jnp.zeros((N_HEADS, 193), F32), body], axis=1)
    return full.T


def _t5_bucket(rel):
    nb = T5_BUCKETS // 2
    max_exact = nb // 2
    ret = jnp.where(rel > 0, nb, 0)
    n = jnp.abs(rel)
    nf = jnp.maximum(n, 1).astype(jnp.float32)
    large = max_exact + (jnp.log(nf / max_exact) / math.log(T5_MAX_DIST / max_exact)
                         * (nb - max_exact)).astype(jnp.int32)
    large = jnp.minimum(large, nb - 1)
    return ret + jnp.where(n < max_exact, n, large)


def _b_offset_buckets(keys):
    return _t5_bucket(jnp.arange(keys, dtype=jnp.int32) - (B_LEFT_CHUNKS * CHUNK + CHUNK - 1))


def _b_bias_by_offset(t5_table, keys):
    return t5_table[_b_offset_buckets(keys)].T


def _b_bucket_onehot(keys):
    return (_b_offset_buckets(keys)[:, None] == jnp.arange(128)[None, :]).astype(BF16)


def _local_step(my_slot, order, x, target, a_gain_shard, w_in_a_shard, rel_bias, late_shards, kv_gain,
                t5_table, b_gain, sinks, f_gain):
    a_bias = _a_bias_by_offset(rel_bias)
    b_bias_fwd = _b_bias_by_offset(t5_table, B_QBLK_FWD + B_PREV)
    b_bias_bwd = _b_bias_by_offset(t5_table, B_QBLK_BWD + B_PREV)
    sinks_flat = sinks.reshape(N_HEADS)

    xn, qkvg, w_in_a, a_gain = _norm_matmul_gather(order, x, a_gain_shard, w_in_a_shard)
    z_a, out_a, lse_a, (w_in_b, w_out_a, w_out_b, kv_w) = _attn_a_fwd(qkvg, a_bias, late_shards)
    w_out_a = w_out_a.reshape(D_MODEL, D_MODEL)
    w_out_b = w_out_b.reshape(D_MODEL, D_MODEL)
    kv_w = kv_w.reshape(D_MODEL, 2 * 128)
    h1, kvn, hb, kv, qg = _layer_a_out(x, z_a, w_out_a, kv_gain, b_gain, kv_w, w_in_b)
    z_b, out_b, lse_b = _attn_b_fwd(qg, kv, b_bias_fwd, sinks_flat)
    dh2, dh2b, dz_b, loss, d_fn = _layer_b_out_loss(h1, z_b, w_out_b, f_gain, target)

    dqg_b, dkv_b, d_t5, d_sink = _attn_b_bwd(qg, kv, b_bias_bwd, sinks_flat, out_b, lse_b, dz_b,
                                             _b_bucket_onehot(B_QBLK_BWD + B_PREV))
    dh1, dh1b, dz_a, d_bn, d_kn = _layer_b_in_bwd(dqg_b, dkv_b, w_in_b, kv_w, h1, dh2, b_gain, kv_gain, w_out_a)
    early = dict(
        b_w_out=_weight_grad_rows("grad_b_w_out", my_slot, z_b, dh2b[None]),
        b_w_in=_weight_grad_cols("grad_b_w_in", my_slot, hb, [dqg_b],
                                 [(0, o, c, 4 * o + c) for o in range(2) for c in range(4)], 256),
        kv_w=_weight_grad_rows("grad_kv_w", my_slot, kvn, dkv_b),
        a_w_out=_weight_grad_rows("grad_a_w_out", my_slot, z_a, dh1b[None]))
    dqg_a, dkv_a, d_rel, landed = _attn_a_bwd(qkvg, a_bias, out_a, lse_a, dz_a, [g[0] for g in early.values()])
    g_w_in_a = _weight_grad_cols(
        "grad_a_w_in", my_slot, xn, [dqg_a, dkv_a],
        [(0, 0, 0, 0), (0, 0, 1, 1), (1, 0, 0, 2), (1, 0, 1, 3), (1, 1, 0, 4), (1, 1, 1, 5), (0, 1, 0, 6), (0, 1, 1, 7)], 512)
    from_sibling, = _exchange_sibling([g_w_in_a[0]])
    x_i, y_i, c_i, chips = _place()
    del x_i, y_i
    forward_slots = jnp.stack([_slot(*chip, c_i) for chip in chips]).astype(jnp.int32)
    chip_sums = _pre_reduce("chip_sum_a_w_in", g_w_in_a[0], from_sibling, forward_slots)
    grad_x, d_an, from_chips = _layer_a_in_bwd(dqg_a, dkv_a, w_in_a, x, dh1, a_gain, chip_sums)

    matrices = {n: (g[1], [(land, 0, N_DEV - 1)]) for (n, g), land in zip(early.items(), landed)}
    matrices["a_w_in"] = (g_w_in_a[1], [(from_sibling, 3, 1), (from_chips, 0, 3)])
    small = dict(
        loss=loss, a_norm=d_an, a_rel_bias=d_rel[:, :2].reshape(N_HEADS, A_DIAG),
        kv_norm=d_kn, t5_bias=d_t5, b_norm=d_bn, b_sinks=d_sink, final_norm=d_fn)
    return grad_x, small, matrices


def _place():
    x, y, c = lax.axis_index("x"), lax.axis_index("y"), lax.axis_index("c")
    chips = [(1 - x, y), (x, 1 - y), (1 - x, 1 - y)]
    return x, y, c, chips


def _slot(px, py, pc):
    return 4 * px + 2 * py + pc


ANY = pl.BlockSpec(memory_space=pl.ANY)


def _peer(x, y, c, k):
    return (x ^ (k >> 2), y ^ ((k >> 1) & 1), c ^ (k & 1))


def _scatter_copies(grad_refs, land_refs, send_sems, recv_sems):
    x, y, c, _ = _place()
    copies = []
    for t, (grad, land) in enumerate(zip(grad_refs, land_refs)):
        for k in range(1, N_DEV):
            peer = _peer(x, y, c, k)
            sem = (N_DEV - 1) * t + k - 1
            copies.append(pltpu.make_async_remote_copy(
                src_ref=grad.at[_slot(*peer)], dst_ref=land.at[k - 1],
                send_sem=send_sems.at[sem], recv_sem=recv_sems.at[sem],
                device_id=peer, device_id_type=MESH))
    return copies


def _gather_phases(ins, outs, send_sems, recv_sems, local_sems):
    n = len(ins)
    x, y, c, chips = _place()
    me, sibling = (x, y, c), (x, y, 1 - c)

    def copy(t, k, block, to, src=None):
        dst = outs[t].at[_slot(*block)]
        return pltpu.make_async_remote_copy(
            src_ref=dst if src is None else src, dst_ref=dst,
            send_sem=send_sems.at[7 * t + k], recv_sem=recv_sems.at[7 * t + k],
            device_id=to, device_id_type=MESH)

    def lists():
        mine = [pltpu.make_async_copy(ins[t], outs[t].at[_slot(*me)], local_sems.at[t]) for t in range(n)]
        first = []
        for t in range(n):
            first.append(copy(t, 0, me, sibling, src=ins[t]))
            first += [copy(t, 1 + j, me, (*chip, c), src=ins[t]) for j, chip in enumerate(chips)]
        passed = [copy(t, 4 + j, (*chip, c), sibling) for t in range(n) for j, chip in enumerate(chips)]
        return mine, first, passed

    def start():
        mine, first, _ = lists()
        for cp in mine + first:
            cp.start()

    def forward():
        _, _, passed = lists()
        for t in range(n):
            for j, chip in enumerate(chips):
                copy(t, 1 + j, (*chip, c), me).wait_recv()
                passed[3 * t + j].start()

    def finish():
        mine, first, passed = lists()
        for t in range(n):
            copy(t, 0, sibling, me).wait_recv()
            for j, chip in enumerate(chips):
                copy(t, 4 + j, (*chip, 1 - c), me).wait_recv()
        for cp in first + passed:
            cp.wait_send()
        for cp in mine:
            cp.wait()

    return start, forward, finish


def _gather_scratch(n):
    return [pltpu.SemaphoreType.DMA((7 * n,)), pltpu.SemaphoreType.DMA((7 * n,)), pltpu.SemaphoreType.DMA((n,))]


def _exchange_sibling(grads):
    n = len(grads)

    def body(*refs):
        ins, outs = refs[:n], refs[n:2 * n]
        send_sems, recv_sems = refs[2 * n:]
        x, y, c, chips = _place()
        sibling = (x, y, 1 - c)
        copies = []
        for t in range(n):
            blocks = [(*chip, 1 - c) for chip in chips] + [sibling]
            for k, block in enumerate(blocks):
                copies.append(pltpu.make_async_remote_copy(
                    src_ref=ins[t].at[_slot(*block)], dst_ref=outs[t].at[k],
                    send_sem=send_sems.at[4 * t + k], recv_sem=recv_sems.at[4 * t + k],
                    device_id=sibling, device_id_type=MESH))
        for cp in copies:
            cp.start()
        for cp in copies:
            cp.wait()

    return pl.pallas_call(
        body, name="grads_to_sibling",
        in_specs=[ANY] * n, out_specs=[ANY] * n,
        out_shape=[SDS((4, *g.shape[1:]), g.dtype) for g in grads],
        scratch_shapes=[pltpu.SemaphoreType.DMA((4 * n,)), pltpu.SemaphoreType.DMA((4 * n,))],
    )(*grads)


def _chip_copies(sums_ref, land_ref, send_sems, recv_sems):
    x, y, c, chips = _place()
    del x, y
    return [pltpu.make_async_remote_copy(
        src_ref=sums_ref.at[j], dst_ref=land_ref.at[j], send_sem=send_sems.at[j], recv_sem=recv_sems.at[j],
        device_id=(*chip, c), device_id_type=MESH) for j, chip in enumerate(chips)]


def _row_tile(rows):
    return min(rows, 256)


def _pre_reduce(name, g, from_sibling, slots):
    _, r, c = g.shape
    tr = _row_tile(r)

    def body(slots_ref, g_ref, s_ref, o_ref):
        del slots_ref
        o_ref[...] = (g_ref[...].astype(F32) + s_ref[...].astype(F32)).astype(BF16)

    return pl.pallas_call(
        body, name=name,
        grid_spec=pltpu.PrefetchScalarGridSpec(
            num_scalar_prefetch=1, grid=(3, r // tr),
            in_specs=[pl.BlockSpec((1, tr, c), lambda j, i, s: (s[j], i, 0)),
                      pl.BlockSpec((1, tr, c), lambda j, i, s: (j, i, 0))],
            out_specs=pl.BlockSpec((1, tr, c), lambda j, i, s: (j, i, 0))),
        out_shape=SDS((3, r, c), BF16),
        compiler_params=_cparams(),
    )(slots, g, from_sibling)


def _adamw(w, g, m, v):
    m2 = ADAM_B1 * m + (1.0 - ADAM_B1) * g
    v2 = ADAM_B2 * v + (1.0 - ADAM_B2) * jnp.square(g)
    m_hat = m2 / (1.0 - ADAM_B1 ** ADAM_STEP)
    v_hat = v2 / (1.0 - ADAM_B2 ** ADAM_STEP)
    delta = -ADAM_LR * (m_hat / (jnp.sqrt(v_hat) + ADAM_EPS) + ADAM_WD * w)
    return delta, m2, v2


def _reduce_adamw(name, own, partials, w, m, v):
    r, c = own.shape
    tr = _row_tile(r)
    n_p = len(partials)

    def body(own_ref, *rest):
        p_refs, (w_ref, m_ref, v_ref, grad_ref, d_ref, nm_ref, nv_ref) = rest[:n_p], rest[n_p:]
        grad = own_ref[...]
        for p_ref, (_, _, count) in zip(p_refs, partials):
            for j in range(count):
                grad = grad + p_ref[j].astype(F32)
        grad_ref[...] = grad
        d_ref[...], nm_ref[...], nv_ref[...] = _adamw(w_ref[...], grad, m_ref[...], v_ref[...])

    flat = pl.BlockSpec((tr, c), lambda i: (i, 0))
    return pl.pallas_call(
        body, name=name, grid=(r // tr,),
        in_specs=[flat] + [pl.BlockSpec((count, tr, c), lambda i, first=first, count=count: (first // count, i, 0))
                           for _, first, count in partials] + [flat, flat, flat],
        out_specs=[flat, flat, flat, flat],
        out_shape=[SDS((r, c), F32)] * 4,
        compiler_params=_cparams(),
    )(own, *[p[0] for p in partials], w, m, v)


VM = pl.BlockSpec()


def _small_allreduce(parts):
    n = len(parts)

    def body(*refs):
        ins, outs, lands = refs[:n], refs[n:2 * n], refs[2 * n:3 * n]
        send_sems, recv_sems = refs[3 * n:]
        x, y, c, _ = _place()
        my_slot = _slot(x, y, c)
        copies = []
        for t in range(n):
            lands[t][my_slot] = ins[t][...]
            for k in range(1, N_DEV):
                sem = (N_DEV - 1) * t + k - 1
                copies.append(pltpu.make_async_remote_copy(
                    src_ref=ins[t], dst_ref=lands[t].at[my_slot],
                    send_sem=send_sems.at[sem], recv_sem=recv_sems.at[sem],
                    device_id=_peer(x, y, c, k), device_id_type=MESH))
        for cp in copies:
            cp.start()
        for t in range(n):
            for k in range(1, N_DEV):
                sem = (N_DEV - 1) * t + k - 1
                pltpu.make_async_remote_copy(
                    src_ref=ins[t], dst_ref=lands[t].at[_slot(*_peer(x, y, c, k))],
                    send_sem=send_sems.at[sem], recv_sem=recv_sems.at[sem],
                    device_id=(x, y, c), device_id_type=MESH).wait_recv()
        for cp in copies:
            cp.wait_send()
        for t in range(n):
            total = lands[t][0]
            for s in range(1, N_DEV):
                total = total + lands[t][s]
            outs[t][...] = total

    n_sems = (N_DEV - 1) * n
    return pl.pallas_call(
        body, name="small_allreduce",
        in_specs=[VM] * n, out_specs=[VM] * n, out_shape=[SDS(p.shape, F32) for p in parts],
        scratch_shapes=[pltpu.VMEM((N_DEV, *p.shape), F32) for p in parts]
        + [pltpu.SemaphoreType.DMA((n_sems,)), pltpu.SemaphoreType.DMA((n_sems,))],
    )(*parts)


def _small_adamw(my_slot, sums, ws, ms, vs):
    n = len(ws)

    def body(slot_ref, *refs):
        sum_refs, refs = refs[:n + 1], refs[n + 1:]
        w_refs, m_refs, v_refs, refs = refs[:n], refs[n:2 * n], refs[2 * n:3 * n], refs[3 * n:]
        g_refs, d_refs, nm_refs, nv_refs = refs[:n + 1], refs[n + 1:2 * n + 1], refs[2 * n + 1:3 * n + 1], refs[3 * n + 1:]
        for t in range(n + 1):
            if t == 0:
                g = sum_refs[0][:, pl.ds(pl.multiple_of(slot_ref[0] * 128, 128), 128)]
            else:
                g = sum_refs[t][...]
            g_refs[t][...] = g
            if t < n:
                d_refs[t][...], nm_refs[t][...], nv_refs[t][...] = _adamw(w_refs[t][...], g, m_refs[t][...], v_refs[t][...])

    shapes = [SDS(w.shape, F32) for w in ws]
    outs = pl.pallas_call(
        body, name="small_adamw",
        in_specs=[pl.BlockSpec(memory_space=pltpu.SMEM)] + [VM] * (4 * n + 1),
        out_specs=[VM] * (4 * n + 1),
        out_shape=shapes + [SDS(sums[-1].shape, F32)] + shapes * 3,
    )(my_slot, *sums, *ws, *ms, *vs)
    return outs[:n + 1], outs[n + 1:2 * n + 1], outs[2 * n + 1:3 * n + 1], outs[3 * n + 1:]


def kernel(x, a_norm, a_w_in, a_rel_bias, a_w_out, kv_norm, kv_w, t5_bias, b_norm, b_w_in, b_sinks, b_w_out, final_norm, loss_target, m_a_norm, m_a_w_in, m_a_rel_bias, m_a_w_out, m_kv_norm, m_kv_w, m_t5_bias, m_b_norm, m_b_w_in, m_b_sinks, m_b_w_out, m_final_norm, v_a_norm, v_a_w_in, v_a_rel_bias, v_a_w_out, v_kv_norm, v_kv_w, v_t5_bias, v_b_norm, v_b_w_in, v_b_sinks, v_b_w_out, v_final_norm):
    xi, yi, ci = lax.axis_index("x"), lax.axis_index("y"), lax.axis_index("c")
    my_slot = _slot(xi, yi, ci)

    slot_arr = jnp.reshape(my_slot, (1,)).astype(jnp.int32)
    order = _gather_order(xi, yi, ci)
    late_shards = [b_w_in[0].astype(BF16), a_w_out[0].astype(BF16), b_w_out[0].astype(BF16), kv_w.astype(BF16)]
    grad_x, loc, matrices = _local_step(
        slot_arr, order, x[0], loss_target[0], a_norm, a_w_in[0].astype(BF16), a_rel_bias[0], late_shards,
        kv_norm.reshape(1, D_MODEL), t5_bias, b_norm, b_sinks, final_norm.reshape(1, D_MODEL))

    shard_w = dict(a_w_in=a_w_in[0], b_w_in=b_w_in[0], a_w_out=a_w_out[0], b_w_out=b_w_out[0], kv_w=kv_w)
    shard_m = dict(a_w_in=m_a_w_in[0], b_w_in=m_b_w_in[0], a_w_out=m_a_w_out[0], b_w_out=m_b_w_out[0], kv_w=m_kv_w)
    shard_v = dict(a_w_in=v_a_w_in[0], b_w_in=v_b_w_in[0], a_w_out=v_a_w_out[0], b_w_out=v_b_w_out[0], kv_w=v_kv_w)
    big = {n: _reduce_adamw("adamw_" + n, own, partials, shard_w[n], shard_m[n], shard_v[n])
           for n, (own, partials) in matrices.items()}

    names = ("a_norm", "a_rel_bias", "kv_norm", "t5_bias", "b_norm", "b_sinks", "final_norm")
    row = lambda a: a.reshape(1, -1) if a.ndim == 1 else (a[0] if a.ndim == 3 else a)
    small_w = [row(a) for a in (a_norm, a_rel_bias, kv_norm, t5_bias, b_norm, b_sinks, final_norm)]
    small_m = [row(a) for a in (m_a_norm, m_a_rel_bias, m_kv_norm, m_t5_bias, m_b_norm, m_b_sinks, m_final_norm)]
    small_v = [row(a) for a in (v_a_norm, v_a_rel_bias, v_kv_norm, v_t5_bias, v_b_norm, v_b_sinks, v_final_norm)]
    sums = dict(zip(names + ("loss",), _small_allreduce([loc[n] for n in names] + [loc["loss"]])))
    sums["a_rel_bias"] = _a_bias_grad(sums["a_rel_bias"])
    sums["t5_bias"] = sums["t5_bias"][:, :T5_BUCKETS].T
    sums["b_sinks"] = sums["b_sinks"][:, 0].reshape(1, N_HEADS)
    results = _small_adamw(slot_arr, [sums[n] for n in names + ("loss",)], small_w, small_m, small_v)
    like = dict(a_norm=a_norm, a_rel_bias=a_rel_bias, kv_norm=kv_norm, t5_bias=t5_bias, b_norm=b_norm,
                b_sinks=b_sinks, final_norm=final_norm)
    sm = [{n: part[i].reshape(like[n].shape) for i, n in enumerate(names)} for part in results]
    loss = results[0][len(names)][0, 0]

    order = ("a_norm", "a_w_in", "a_rel_bias", "a_w_out", "kv_norm", "kv_w", "t5_bias", "b_norm",
             "b_w_in", "b_sinks", "b_w_out", "final_norm")
    lead = dict(a_w_in=True, b_w_in=True, a_w_out=True, b_w_out=True, kv_w=False)

    def pick(kind, name):
        if name in big:
            val = big[name][kind]
            return val[None] if lead[name] else val
        return sm[kind][name]

    outs = [loss, grad_x[None]]
    for kind in range(4):
        outs += [pick(kind, n) for n in order]
    return tuple(outs)
```

```python
import functools
import math

import numpy as np
import jax
import jax.numpy as jnp
from jax import lax
from jax.experimental import pallas as pl
from jax.experimental.pallas import tpu as pltpu

F32 = jnp.float32
BF16 = jnp.bfloat16
SDS = jax.ShapeDtypeStruct

D_MODEL = 1024
HEAD_DIM = 64
CHUNK = 64
N_HEADS = 16
RMS_EPS = 1e-6
A_LEFT_CHUNKS = 8
A_BAND = (A_LEFT_CHUNKS + 1) * CHUNK
A_REL_CLIP = 256
B_KV_HEADS = 2
B_GROUP = 8
B_LEFT_CHUNKS = 2
B_BAND = (B_LEFT_CHUNKS + 1) * CHUNK
T5_BUCKETS = 32
T5_MAX_DIST = 128
QBLK = 256
A_KEYS = 3 * QBLK
B_QBLK_FWD = 128
B_QBLK_BWD = 256
B_PREV = 128
A_DIAG = A_KEYS
NEG = -1e30
SCALE = HEAD_DIM ** -0.5
N_DEV = 8

ADAM_LR = 0.001
ADAM_B1 = 0.9
ADAM_B2 = 0.999
ADAM_EPS = 1e-08
ADAM_WD = 0.01
ADAM_STEP = 10

VMEM_LIMIT_BYTES = 56 * 1024 * 1024
MESH = pl.DeviceIdType.MESH


def _cparams():
    return pltpu.CompilerParams(vmem_limit_bytes=VMEM_LIMIT_BYTES)


def _dot(a, b):
    return jnp.dot(a, b, preferred_element_type=F32)


def _dot_nt(a, b):
    return lax.dot_general(a, b, (((1,), (1,)), ((), ())), preferred_element_type=F32)


def _dot_tn(a, b):
    return lax.dot_general(a, b, (((0,), (0,)), ((), ())), preferred_element_type=F32)


def _rstd(xf):
    return lax.rsqrt(jnp.mean(xf * xf, axis=-1, keepdims=True) + RMS_EPS)


def _sigmoid(x):
    return 1.0 / (1.0 + jnp.exp(-x))


_GATHER_SEQUENCE = ((0, None), (1, 0), (2, 1), (4, None), (5, None), (3, 2), (6, None))


def _gather_order(x, y, c):
    others = [(1 - x, y), (x, 1 - y), (1 - x, 1 - y)]
    arrivals = [_slot(x, y, 1 - c)] + [_slot(*chip, c) for chip in others] + [_slot(*chip, 1 - c) for chip in others]
    return jnp.stack([_slot(x, y, c)] + [arrivals[a] for a, _ in _GATHER_SEQUENCE]).astype(jnp.int32)


def _norm_matmul_gather(order, x, gain_shard, w_shard):
    t = x.shape[0]
    dw, tn = w_shard.shape
    tm = min(t, 1024)
    n_m = t // tm

    def body(order_ref, x_ref, gs_ref, shard_ref, xn_ref, o_ref, full_ref, gain_ref,
             xn_all, wbuf, gland, send_sems, recv_sems, gsend_sems, grecv_sems, load_sems, own_sem):
        n, m = pl.program_id(0), pl.program_id(1)
        x_i, y_i, c_i, chips = _place()
        me, sibling = (x_i, y_i, c_i), (x_i, y_i, 1 - c_i)

        def send(k, block, to, src=None):
            dst = full_ref.at[_slot(*block)]
            return pltpu.make_async_remote_copy(
                src_ref=dst if src is None else src, dst_ref=dst,
                send_sem=send_sems.at[k], recv_sem=recv_sems.at[k], device_id=to, device_id_type=MESH)

        own = pltpu.make_async_copy(shard_ref, full_ref.at[_slot(*me)], own_sem)
        first = [send(0, me, sibling, src=shard_ref)]
        first += [send(1 + j, me, (*chip, c_i), src=shard_ref) for j, chip in enumerate(chips)]
        forwards = [send(4 + j, (*chip, c_i), sibling) for j, chip in enumerate(chips)]
        arrivals = [send(0, sibling, me)] + [send(1 + j, (*chip, c_i), me) for j, chip in enumerate(chips)]
        arrivals += [send(4 + j, (*chip, 1 - c_i), me) for j, chip in enumerate(chips)]
        gains = [pltpu.make_async_remote_copy(
            src_ref=gs_ref, dst_ref=gland.at[_slot(*me)], send_sem=gsend_sems.at[k - 1],
            recv_sem=grecv_sems.at[k - 1], device_id=_peer(x_i, y_i, c_i, k), device_id_type=MESH)
            for k in range(1, N_DEV)]

        @pl.when(jnp.logical_and(n == 0, m == 0))
        def _():
            own.start()
            for cp in gains + first:
                cp.start()
            pltpu.make_async_copy(shard_ref, wbuf.at[0], load_sems.at[0]).start()
            gland[_slot(*me)] = gs_ref[...]
            for k in range(1, N_DEV):
                pltpu.make_async_remote_copy(
                    src_ref=gs_ref, dst_ref=gland.at[_slot(*_peer(x_i, y_i, c_i, k))],
                    send_sem=gsend_sems.at[k - 1], recv_sem=grecv_sems.at[k - 1],
                    device_id=me, device_id_type=MESH).wait_recv()
            for s in range(N_DEV):
                gain_ref[:, 128 * s:128 * (s + 1)] = gland[s]

        rows = pl.ds(pl.multiple_of(m * tm, tm), tm)

        @pl.when(n == 0)
        def _():
            xf = x_ref[...]
            xn = ((xf * _rstd(xf)) * gain_ref[...]).astype(BF16)
            xn_all[rows, :] = xn
            xn_ref[...] = xn

        @pl.when(m == 0)
        def _():
            pltpu.make_async_copy(full_ref.at[0], wbuf.at[n % 2], load_sems.at[n % 2]).wait()

        o_ref[...] = _dot(xn_all[rows, :], wbuf[n % 2]).astype(BF16)

        for k, (arrival, forward) in enumerate(_GATHER_SEQUENCE):
            @pl.when(jnp.logical_and(n == k, m == n_m - 1))
            def _(k=k, arrival=arrival, forward=forward):
                arrivals[arrival].wait_recv()
                if forward is not None:
                    forwards[forward].start()
                pltpu.make_async_copy(full_ref.at[order_ref[k + 1]], wbuf.at[(k + 1) % 2],
                                      load_sems.at[(k + 1) % 2]).start()

        @pl.when(jnp.logical_and(n == N_DEV - 1, m == n_m - 1))
        def _():
            for cp in gains + first + forwards:
                cp.wait_send()
            own.wait()

    held = lambda n, m, order: (jnp.where(n == 0, m, n_m - 1), 0)
    return pl.pallas_call(
        body, name="norm_matmul_gather",
        grid_spec=pltpu.PrefetchScalarGridSpec(
            num_scalar_prefetch=1, grid=(N_DEV, n_m),
            in_specs=[pl.BlockSpec((tm, D_MODEL), held),
                      pl.BlockSpec((1, 128), lambda n, m, order: (0, 0)), ANY],
            out_specs=[pl.BlockSpec((tm, D_MODEL), held),
                       pl.BlockSpec((tm, tn), lambda n, m, order: (m, order[n])),
                       ANY, pl.BlockSpec((1, D_MODEL), lambda n, m, order: (0, 0))],
            scratch_shapes=[pltpu.VMEM((t, D_MODEL), BF16), pltpu.VMEM((2, dw, tn), BF16),
                            pltpu.VMEM((N_DEV, 1, 128), F32),
                            pltpu.SemaphoreType.DMA((7,)), pltpu.SemaphoreType.DMA((7,)),
                            pltpu.SemaphoreType.DMA((7,)), pltpu.SemaphoreType.DMA((7,)),
                            pltpu.SemaphoreType.DMA((2,)), pltpu.SemaphoreType.DMA]),
        out_shape=[SDS((t, D_MODEL), BF16), SDS((t, N_DEV * tn), BF16), SDS((N_DEV, dw, tn), BF16),
                   SDS((1, D_MODEL), F32)],
        compiler_params=_cparams(),
    )(order, x, gain_shard, w_shard)


def _layer_a_out(x, z, w_out, kv_gain, b_gain, kv_w, w_in_b):
    t = x.shape[0]
    tm = min(t, 512)
    nb, _, tn = w_in_b.shape

    def body(x_ref, z_ref, wo_ref, kvg_ref, bg_ref, kvw_ref, wb_ref,
             h1_ref, kvn_ref, hb_ref, kv_ref, qg_ref):
        h1 = x_ref[...] + _dot(z_ref[...], wo_ref[...])
        h1_ref[...] = h1
        y0 = h1 * _rstd(h1)
        kvn = (y0 * kvg_ref[...]).astype(BF16)
        hb = (y0 * bg_ref[...]).astype(BF16)
        kvn_ref[...] = kvn
        hb_ref[...] = hb
        kv_ref[...] = _dot(kvn, kvw_ref[...]).astype(BF16)
        for i in range(nb):
            qg_ref[:, i * tn:(i + 1) * tn] = _dot(hb, wb_ref[i]).astype(BF16)

    row = lambda m: (m, 0)
    fix2 = lambda m: (0, 0)
    return pl.pallas_call(
        body, name="layer_a_out", grid=(t // tm,),
        in_specs=[pl.BlockSpec((tm, D_MODEL), row), pl.BlockSpec((tm, D_MODEL), row),
                  pl.BlockSpec((D_MODEL, D_MODEL), fix2),
                  pl.BlockSpec((1, D_MODEL), fix2), pl.BlockSpec((1, D_MODEL), fix2),
                  pl.BlockSpec((D_MODEL, 256), fix2),
                  pl.BlockSpec((nb, D_MODEL, tn), lambda m: (0, 0, 0))],
        out_specs=[pl.BlockSpec((tm, D_MODEL), row), pl.BlockSpec((tm, D_MODEL), row),
                   pl.BlockSpec((tm, D_MODEL), row), pl.BlockSpec((tm, 256), row),
                   pl.BlockSpec((tm, nb * tn), row)],
        out_shape=[SDS((t, D_MODEL), F32), SDS((t, D_MODEL), BF16), SDS((t, D_MODEL), BF16),
                   SDS((t, 256), BF16), SDS((t, nb * tn), BF16)],
        compiler_params=_cparams(),
    )(x, z, w_out, kv_gain, b_gain, kv_w, w_in_b)


def _layer_b_out_loss(h1, z, w_out, f_gain, target):
    t = h1.shape[0]
    tm = min(t, 512)

    def body(h1_ref, z_ref, wo_ref, fg_ref, tgt_ref,
             dh2_ref, dh2b_ref, dz_ref, loss_ref, dfn_ref):
        @pl.when(pl.program_id(0) == 0)
        def _():
            loss_ref[...] = jnp.zeros_like(loss_ref)
            dfn_ref[...] = jnp.zeros_like(dfn_ref)

        h2 = h1_ref[...] + _dot(z_ref[...], wo_ref[...])
        r = _rstd(h2)
        yn = h2 * r
        fg = fg_ref[...]
        err = yn * fg - tgt_ref[...]
        loss_ref[...] += (0.5 / D_MODEL) * jnp.sum(err * err)
        dy = err * (1.0 / D_MODEL)
        dfn_ref[...] += jnp.sum(dy * yn, axis=0, keepdims=True)
        u = dy * fg
        dh2 = r * u - h2 * ((r * r * r) * jnp.mean(u * h2, axis=-1, keepdims=True))
        dh2_ref[...] = dh2
        dh2b = dh2.astype(BF16)
        dh2b_ref[...] = dh2b
        dz_ref[...] = _dot_nt(dh2b, wo_ref[...]).astype(BF16)

    row = lambda m: (m, 0)
    fix2 = lambda m: (0, 0)
    return pl.pallas_call(
        body, name="layer_b_out_loss", grid=(t // tm,),
        in_specs=[pl.BlockSpec((tm, D_MODEL), row), pl.BlockSpec((tm, D_MODEL), row),
                  pl.BlockSpec((D_MODEL, D_MODEL), fix2), pl.BlockSpec((1, D_MODEL), fix2),
                  pl.BlockSpec((tm, D_MODEL), row)],
        out_specs=[pl.BlockSpec((tm, D_MODEL), row), pl.BlockSpec((tm, D_MODEL), row),
                   pl.BlockSpec((tm, D_MODEL), row), pl.BlockSpec((1, 128), fix2),
                   pl.BlockSpec((1, D_MODEL), fix2)],
        out_shape=[SDS((t, D_MODEL), F32), SDS((t, D_MODEL), BF16), SDS((t, D_MODEL), BF16),
                   SDS((1, 128), F32), SDS((1, D_MODEL), F32)],
        compiler_params=_cparams(),
    )(h1, z, w_out, f_gain, target)


def _layer_b_in_bwd(dqg, dkv, w_in_b, kv_w, h1, dh2, b_gain, kv_gain, w_out_a):
    t = h1.shape[0]
    tm = min(t, 256)
    nb, _, tn = w_in_b.shape
    per = D_MODEL // tn

    def body(dqg_ref, dkv_ref, wb_ref, kvw_ref, h1_ref, dh2_ref, bg_ref, kvg_ref, wo_ref,
             dh1_ref, dh1b_ref, dz_ref, dbn_ref, dkn_ref):
        @pl.when(pl.program_id(0) == 0)
        def _():
            dbn_ref[...] = jnp.zeros_like(dbn_ref)
            dkn_ref[...] = jnp.zeros_like(dkn_ref)

        dhb = jnp.zeros((tm, D_MODEL), F32)
        for i in range(nb):
            blk = dqg_ref[i // per, :, (i % per) * tn:(i % per + 1) * tn]
            dhb = dhb + _dot_nt(blk, wb_ref[i])
        dkn = (_dot_nt(dkv_ref[0].astype(BF16), kvw_ref[:, 0:128])
               + _dot_nt(dkv_ref[1].astype(BF16), kvw_ref[:, 128:256]))
        h1 = h1_ref[...]
        r = _rstd(h1)
        xr = h1 * r
        dbn_ref[...] += jnp.sum(dhb * xr, axis=0, keepdims=True)
        dkn_ref[...] += jnp.sum(dkn * xr, axis=0, keepdims=True)
        u = dhb * bg_ref[...] + dkn * kvg_ref[...]
        dh1 = dh2_ref[...] + r * u - h1 * ((r * r * r) * jnp.mean(u * h1, axis=-1, keepdims=True))
        dh1_ref[...] = dh1
        dh1b = dh1.astype(BF16)
        dh1b_ref[...] = dh1b
        dz_ref[...] = _dot_nt(dh1b, wo_ref[...]).astype(BF16)

    row = lambda m: (m, 0)
    fix2 = lambda m: (0, 0)
    return pl.pallas_call(
        body, name="layer_b_in_bwd", grid=(t // tm,),
        in_specs=[pl.BlockSpec((2, tm, D_MODEL), lambda m: (0, m, 0)),
                  pl.BlockSpec((2, tm, 128), lambda m: (0, m, 0)),
                  pl.BlockSpec((nb, D_MODEL, tn), lambda m: (0, 0, 0)),
                  pl.BlockSpec((D_MODEL, 256), fix2),
                  pl.BlockSpec((tm, D_MODEL), row), pl.BlockSpec((tm, D_MODEL), row),
                  pl.BlockSpec((1, D_MODEL), fix2), pl.BlockSpec((1, D_MODEL), fix2),
                  pl.BlockSpec((D_MODEL, D_MODEL), fix2)],
        out_specs=[pl.BlockSpec((tm, D_MODEL), row), pl.BlockSpec((tm, D_MODEL), row),
                   pl.BlockSpec((tm, D_MODEL), row), pl.BlockSpec((1, D_MODEL), fix2),
                   pl.BlockSpec((1, D_MODEL), fix2)],
        out_shape=[SDS((t, D_MODEL), F32), SDS((t, D_MODEL), BF16), SDS((t, D_MODEL), BF16),
                   SDS((1, D_MODEL), F32), SDS((1, D_MODEL), F32)],
        compiler_params=_cparams(),
    )(dqg, dkv, w_in_b, kv_w, h1, dh2, b_gain, kv_gain, w_out_a)


def _layer_a_in_bwd(dqg, dkv, w_in_a, x, dh1, a_gain, chip_sums):
    t = x.shape[0]
    tm = min(t, 256)
    nb, _, tn = w_in_a.shape
    per = D_MODEL // tn

    def body(dqg_ref, dkv_ref, w_ref, x_ref, dh1_ref, ag_ref, sums_ref, dx_ref, dan_ref, land_ref,
             send_sems, recv_sems):
        @pl.when(pl.program_id(0) == 0)
        def _():
            dan_ref[...] = jnp.zeros_like(dan_ref)
            for cp in _chip_copies(sums_ref, land_ref, send_sems, recv_sems):
                cp.start()

        dxn = jnp.zeros((tm, D_MODEL), F32)
        for i in range(nb):
            part = i // per
            src = dqg_ref if part in (0, 3) else dkv_ref
            outer = {0: 0, 3: 1, 1: 0, 2: 1}[part]
            blk = src[outer, :, (i % per) * tn:(i % per + 1) * tn]
            dxn = dxn + _dot_nt(blk, w_ref[i])
        xf = x_ref[...]
        r = _rstd(xf)
        dan_ref[...] += jnp.sum(dxn * (xf * r), axis=0, keepdims=True)
        u = dxn * ag_ref[...]
        dx_ref[...] = dh1_ref[...] + r * u - xf * ((r * r * r) * jnp.mean(u * xf, axis=-1, keepdims=True))

        @pl.when(pl.program_id(0) == t // tm - 1)
        def _():
            for cp in _chip_copies(sums_ref, land_ref, send_sems, recv_sems):
                cp.wait()

    row = lambda m: (m, 0)
    fix2 = lambda m: (0, 0)
    return pl.pallas_call(
        body, name="layer_a_in_bwd", grid=(t // tm,),
        in_specs=[pl.BlockSpec((2, tm, D_MODEL), lambda m: (0, m, 0)),
                  pl.BlockSpec((2, tm, D_MODEL), lambda m: (0, m, 0)),
                  pl.BlockSpec((nb, D_MODEL, tn), lambda m: (0, 0, 0)),
                  pl.BlockSpec((tm, D_MODEL), row), pl.BlockSpec((tm, D_MODEL), row),
                  pl.BlockSpec((1, D_MODEL), fix2), ANY],
        out_specs=[pl.BlockSpec((tm, D_MODEL), row), pl.BlockSpec((1, D_MODEL), fix2), ANY],
        out_shape=[SDS((t, D_MODEL), F32), SDS((1, D_MODEL), F32), SDS(chip_sums.shape, chip_sums.dtype)],
        scratch_shapes=[pltpu.SemaphoreType.DMA((3,)), pltpu.SemaphoreType.DMA((3,))],
        compiler_params=_cparams(),
    )(dqg, dkv, w_in_a, x, dh1, a_gain, chip_sums)


def _lut(s, vals):
    r = jnp.int32(vals[0])
    for i in range(1, len(vals)):
        r = jnp.where(s == i, jnp.int32(vals[i]), r)
    return r


def _held(steps, i):
    seq, cur = [None] * len(steps), None
    for k in range(len(steps) - 1, -1, -1):
        if steps[k][0] == i:
            cur = steps[k][1:3]
        seq[k] = cur
    for k in range(len(steps)):
        cur = seq[k] = seq[k] if seq[k] is not None else cur
    return seq


def _weight_grad_cols(name, my_slot, a, bs, steps, tn):
    t, dw = a.shape
    n_arr = len(bs)
    which = [s[0] for s in steps]
    blks = [s[3] for s in steps]

    def body(slot_ref, a_ref, *rest):
        b_refs, (o_ref, own_ref, at_ref) = rest[:n_arr], rest[n_arr:]
        s = pl.program_id(0)

        @pl.when(s == 0)
        def _():
            at_ref[...] = a_ref[...].T

        for i in range(n_arr):
            @pl.when(_lut(s, which) == i)
            def _(i=i):
                res = _dot(at_ref[...], b_refs[i][0])
                o_ref[0] = res.astype(BF16)

                @pl.when(_lut(s, blks) == slot_ref[0])
                def _():
                    own_ref[...] = res

    def b_spec(i):
        held = _held(steps, i)
        return pl.BlockSpec((1, t, tn), lambda s, slot: (_lut(s, [h[0] for h in held]), 0,
                                                         _lut(s, [h[1] for h in held])))

    return pl.pallas_call(
        body, name=name,
        grid_spec=pltpu.PrefetchScalarGridSpec(
            num_scalar_prefetch=1, grid=(len(steps),),
            in_specs=[pl.BlockSpec((t, dw), lambda s, slot: (0, 0))] + [b_spec(i) for i in range(n_arr)],
            out_specs=[pl.BlockSpec((1, dw, tn), lambda s, slot: (_lut(s, blks), 0, 0)),
                       pl.BlockSpec((dw, tn), lambda s, slot: (0, 0))],
            scratch_shapes=[pltpu.VMEM((dw, t), BF16)]),
        out_shape=[SDS((N_DEV, dw, tn), BF16), SDS((dw, tn), F32)],
        compiler_params=_cparams(),
    )(my_slot, a, *bs)


def _weight_grad_rows(name, my_slot, a, b):
    t, dw = a.shape
    n_o, _, c = b.shape
    rows = dw // N_DEV

    def body(slot_ref, a_ref, b_ref, o_ref, own_ref):
        at = a_ref[...].T
        res = [_dot(at, b_ref[o].astype(BF16)) for o in range(n_o)]
        for o in range(n_o):
            o_ref[0, :, o * c:(o + 1) * c] = res[o].astype(BF16)

        @pl.when(pl.program_id(0) == slot_ref[0])
        def _():
            for o in range(n_o):
                own_ref[:, o * c:(o + 1) * c] = res[o]

    return pl.pallas_call(
        body, name=name,
        grid_spec=pltpu.PrefetchScalarGridSpec(
            num_scalar_prefetch=1, grid=(N_DEV,),
            in_specs=[pl.BlockSpec((t, rows), lambda s, slot: (0, s)),
                      pl.BlockSpec((n_o, t, c), lambda s, slot: (0, 0, 0))],
            out_specs=[pl.BlockSpec((1, rows, n_o * c), lambda s, slot: (s, 0, 0)),
                       pl.BlockSpec((rows, n_o * c), lambda s, slot: (0, 0))]),
        out_shape=[SDS((N_DEV, rows, n_o * c), BF16), SDS((rows, n_o * c), F32)],
        compiler_params=_cparams(),
    )(my_slot, a, b)


def _lane_lo():
    return lax.broadcasted_iota(jnp.int32, (1, 128), 1) < HEAD_DIM


def _offset_sums(gt):
    keys = gt.shape[1]
    gc = gt[0:CHUNK]
    for cc in range(1, gt.shape[0] // CHUNK):
        gc = gc + pltpu.roll(gt[cc * CHUNK:(cc + 1) * CHUNK], keys - cc * CHUNK, 1)
    hi = gc.astype(BF16)
    lo = (gc - hi.astype(F32)).astype(BF16)
    flip = (lax.broadcasted_iota(jnp.int32, (CHUNK, CHUNK), 0)
            + lax.broadcasted_iota(jnp.int32, (CHUNK, CHUNK), 1) == CHUNK - 1).astype(BF16)
    gf = _dot(flip, hi) + _dot(flip, lo)
    skew = pltpu.roll(gf, 0, 1, stride=1, stride_axis=0)
    return jnp.sum(skew, axis=0, keepdims=True)


def _band_bias(w_row, band, rows):
    keys = w_row.shape[1]
    base = jnp.broadcast_to(w_row, (CHUNK, keys))
    skew = pltpu.roll(base, 0, 1, stride=1, stride_axis=0)
    skew = pltpu.roll(skew, keys - (CHUNK - 1), 1)
    col = lax.broadcasted_iota(jnp.int32, (CHUNK, keys), 1)
    chunk0 = jnp.where(col < band, skew, NEG)
    return jnp.concatenate(
        [chunk0] + [pltpu.roll(chunk0, cc * CHUNK, 1) for cc in range(1, rows // CHUNK)], axis=0)


def _silu_parts(g):
    sg = _sigmoid(g)
    return g * sg, sg * (1.0 + g * (1.0 - sg))


A_PAIRS = 2
A_ROWS = QBLK
A_LANES = 128 * A_PAIRS
A_STEPS = D_MODEL // A_LANES


def _a_specs():
    q = pl.BlockSpec((QBLK, A_LANES), lambda p, j: (j, p))
    ks = [pl.BlockSpec((QBLK, A_LANES), lambda p, j, b=b: (jnp.maximum(j - 2 + b, 0), A_STEPS + p)) for b in range(3)]
    vs = [pl.BlockSpec((QBLK, A_LANES), lambda p, j, b=b: (jnp.maximum(j - 2 + b, 0), 2 * A_STEPS + p))
          for b in range(3)]
    g = pl.BlockSpec((QBLK, A_LANES), lambda p, j: (j, 3 * A_STEPS + p))
    bias = pl.BlockSpec((A_PAIRS, 8, A_KEYS), lambda p, j: (p, 0, 0))
    return q, ks, vs, g, bias


def _a_fill_bias(w_ref, b_ref, j):
    _fill_bias(2 * A_PAIRS, lambda h: w_ref[h // 2, h % 2:h % 2 + 1, :], A_BAND, QBLK * (2 - j), 2, b_ref, j)


def _fill_bias(n, get_row, band, first_valid_col, early, bias_scr, j):
    @pl.when(j == 0)
    def _():
        for h in range(n):
            bias_scr[h] = _band_bias(get_row(h), band, bias_scr.shape[1])

    @pl.when(j < early)
    def _():
        keys = bias_scr.shape[2]
        col_ok = lax.broadcasted_iota(jnp.int32, (1, keys), 1) >= first_valid_col
        for h in range(n):
            bias_scr[n + h] = jnp.where(col_ok, bias_scr[h], NEG)


def _head_logits(q, k, bias_scr, idx, sel):
    qm = jnp.where(sel, q, jnp.zeros_like(q)) * SCALE
    return qm, _dot_nt(qm, k) + bias_scr[idx]


def _row_sums_everywhere(r, sel):
    return jnp.where(sel, pltpu.roll(r, HEAD_DIM, 1), r)


def _own_everywhere(x, sel):
    return jnp.where(sel, x, pltpu.roll(x, HEAD_DIM, 1))


def _minus_rows(s, row_full):
    return jnp.concatenate([s[:, i:i + 128] - row_full for i in range(0, s.shape[1], 128)], axis=1)


def _attn_a_fwd(qkvg, bias, gather):
    t = qkvg.shape[0]
    nq = t // QBLK
    n_g = len(gather)
    q_spec, k_specs, v_specs, g_spec, bias_spec = _a_specs()

    def body(q_ref, k0, k1, k2, v0, v1, v2, g_ref, w_ref, *rest):
        shard_refs, rest = rest[:n_g], rest[n_g:]
        z_ref, o_ref, lse_ref = rest[:3]
        full_refs, (b_ref, *comm) = rest[3:3 + n_g], rest[3 + n_g:]
        p = pl.program_id(0)
        j = pl.program_id(1)
        start, forward, finish = _gather_phases(shard_refs, full_refs, *comm)
        pl.when(jnp.logical_and(p == 0, j == 0))(start)
        pl.when(jnp.logical_and(p == A_STEPS // 2, j == 0))(forward)
        _a_fill_bias(w_ref, b_ref, j)
        early = (j < 2).astype(jnp.int32)
        lane_lo = _lane_lo()
        sels = (lane_lo, jnp.logical_not(lane_lo))
        for pp in range(A_PAIRS):
            cols = slice(128 * pp, 128 * (pp + 1))
            k = jnp.concatenate([k0[:, cols], k1[:, cols], k2[:, cols]], axis=0)
            v = jnp.concatenate([v0[:, cols], v1[:, cols], v2[:, cols]], axis=0)
            v1s = [jnp.where(sel, v, jnp.ones_like(v)) for sel in sels]
            for rb in range(QBLK // A_ROWS):
                rows = slice(rb * A_ROWS, (rb + 1) * A_ROWS)
                q = q_ref[rows, cols]
                outs, lses = [], []
                qm2 = jnp.concatenate([jnp.where(sel, q, jnp.zeros_like(q)) for sel in sels], axis=0) * SCALE
                s2 = _dot_nt(qm2, k)
                for hh, sel in enumerate(sels):
                    s = s2[hh * A_ROWS:(hh + 1) * A_ROWS] + b_ref[2 * pp + hh + 2 * A_PAIRS * early, rows, :]
                    mx = jnp.max(s, axis=-1, keepdims=True)
                    e = jnp.exp(s - mx).astype(BF16)
                    r = _dot(e, v1s[hh])
                    l = _row_sums_everywhere(r, sel)
                    outs.append(r / l)
                    lses.append(mx + jnp.log(l))
                o = jnp.where(lane_lo, outs[0], outs[1])
                silu, _ = _silu_parts(g_ref[rows, cols].astype(F32))
                o_ref[rows, cols] = o.astype(BF16)
                z_ref[rows, cols] = (o * silu).astype(BF16)
                lse_ref[rows, cols] = jnp.where(lane_lo, lses[0], lses[1])
        pl.when(jnp.logical_and(p == A_STEPS - 1, j == nq - 1))(finish)

    out_spec = pl.BlockSpec((QBLK, A_LANES), lambda p, j: (j, p))
    outs = pl.pallas_call(
        body, name="attn_a_fwd", grid=(A_STEPS, nq),
        in_specs=[q_spec, *k_specs, *v_specs, g_spec, bias_spec] + [ANY] * n_g,
        out_specs=[out_spec, out_spec, out_spec] + [ANY] * n_g,
        out_shape=[SDS((t, D_MODEL), BF16), SDS((t, D_MODEL), BF16), SDS((t, D_MODEL), F32)]
        + [SDS((N_DEV, *s.shape), s.dtype) for s in gather],
        scratch_shapes=[pltpu.VMEM((4 * A_PAIRS, QBLK, A_KEYS), F32)] + _gather_scratch(n_g),
        compiler_params=_cparams(),
    )(qkvg, qkvg, qkvg, qkvg, qkvg, qkvg, qkvg, qkvg, bias, *gather)
    return outs[0], outs[1], outs[2], list(outs[3:])


def _attn_a_bwd(qkvg, bias, out_a, lse, dz, scatter):
    t = qkvg.shape[0]
    nq = t // QBLK
    n_sc = len(scatter)
    q_spec, k_specs, v_specs, g_spec, bias_spec = _a_specs()

    def body(q_ref, k0, k1, k2, v0, v1, v2, g_ref, w_ref, o_ref, lse_ref, dz_ref, *rest):
        sc_refs, rest = rest[:n_sc], rest[n_sc:]
        dqg_ref, dkv_ref, dg_ref = rest[:3]
        land_refs, rest = rest[3:3 + n_sc], rest[3 + n_sc:]
        dk_acc, dv_acc, gt_acc, b_ref, send_sems, recv_sems = rest
        j = pl.program_id(1)
        first = jnp.logical_and(pl.program_id(0) == 0, j == 0)
        last = jnp.logical_and(pl.program_id(0) == A_STEPS - 1, j == nq - 1)

        @pl.when(first)
        def _():
            for cp in _scatter_copies(sc_refs, land_refs, send_sems, recv_sems):
                cp.start()

        _a_fill_bias(w_ref, b_ref, j)

        @pl.when(j == 0)
        def _():
            dk_acc[...] = jnp.zeros_like(dk_acc)
            dv_acc[...] = jnp.zeros_like(dv_acc)
            gt_acc[...] = jnp.zeros_like(gt_acc)

        early = (j < 2).astype(jnp.int32)
        lane_lo = _lane_lo()
        for pp in range(A_PAIRS):
            cols = slice(128 * pp, 128 * (pp + 1))
            q = q_ref[:, cols]
            k = jnp.concatenate([k0[:, cols], k1[:, cols], k2[:, cols]], axis=0)
            v = jnp.concatenate([v0[:, cols], v1[:, cols], v2[:, cols]], axis=0)
            o = o_ref[:, cols].astype(F32)
            lse_pair = lse_ref[:, cols]
            dzf = dz_ref[:, cols].astype(F32)
            silu, dsilu = _silu_parts(g_ref[:, cols].astype(F32))
            do = dzf * silu
            dqg_ref[1, :, cols] = (dzf * o * dsilu).astype(BF16)
            doo = do * o
            sels = (lane_lo, jnp.logical_not(lane_lo))
            qm2 = jnp.concatenate([jnp.where(sel, q, jnp.zeros_like(q)) for sel in sels], axis=0) * SCALE
            dom2 = jnp.concatenate([jnp.where(sel, do, 0.0) for sel in sels], axis=0).astype(BF16)
            s2 = _dot_nt(qm2, k)
            dp2 = _dot_nt(dom2, v)
            ps, dss = [], []
            for hh, sel in enumerate(sels):
                rows = slice(hh * QBLK, (hh + 1) * QBLK)
                s = s2[rows] + b_ref[2 * pp + hh + 2 * A_PAIRS * early]
                p = jnp.exp(_minus_rows(s, _own_everywhere(lse_pair, sel)))
                delta = jnp.sum(jnp.where(sel, doo, 0.0), axis=-1, keepdims=True)
                ds = p * (dp2[rows] - delta)
                gt_acc[2 * pp + hh] += ds
                ps.append(p.astype(BF16))
                dss.append(ds.astype(BF16))
            dsb2 = jnp.concatenate(dss, axis=0)
            dq2 = _dot(dsb2, k) * SCALE
            dk_blk = _dot_tn(dsb2, qm2)
            dv_blk = _dot_tn(jnp.concatenate(ps, axis=0), dom2)
            dqg_ref[0, :, cols] = jnp.where(lane_lo, dq2[0:QBLK], dq2[QBLK:2 * QBLK]).astype(BF16)
            for b in range(3):
                @pl.when(j - 2 + b >= 0)
                def _(b=b, cols=cols, dk_blk=dk_blk, dv_blk=dv_blk):
                    rows = pl.ds(pl.multiple_of((j - 2 + b) * QBLK, QBLK), QBLK)
                    dk_acc[rows, cols] += dk_blk[b * QBLK:(b + 1) * QBLK]
                    dv_acc[rows, cols] += dv_blk[b * QBLK:(b + 1) * QBLK]

        @pl.when(j == nq - 1)
        def _():
            dkv_ref[0] = dk_acc[...].astype(BF16)
            dkv_ref[1] = dv_acc[...].astype(BF16)
            for pp in range(A_PAIRS):
                dg_ref[pp] = jnp.concatenate([_offset_sums(gt_acc[2 * pp]), _offset_sums(gt_acc[2 * pp + 1]),
                                              jnp.zeros((6, A_DIAG), F32)], axis=0)

        @pl.when(last)
        def _():
            for cp in _scatter_copies(sc_refs, land_refs, send_sems, recv_sems):
                cp.wait()

    blk = pl.BlockSpec((QBLK, A_LANES), lambda p, j: (j, p))
    outs = pl.pallas_call(
        body, name="attn_a_bwd", grid=(A_STEPS, nq),
        in_specs=[q_spec, *k_specs, *v_specs, g_spec, bias_spec, blk, blk, blk] + [ANY] * n_sc,
        out_specs=[pl.BlockSpec((2, QBLK, A_LANES), lambda p, j: (0, j, p)),
                   pl.BlockSpec((2, t, A_LANES), lambda p, j: (0, 0, p)),
                   pl.BlockSpec((A_PAIRS, 8, A_DIAG), lambda p, j: (p, 0, 0))] + [ANY] * n_sc,
        out_shape=[SDS((2, t, D_MODEL), BF16), SDS((2, t, D_MODEL), BF16), SDS((N_HEADS // 2, 8, A_DIAG), F32)]
        + [SDS((N_DEV - 1, *g.shape[1:]), g.dtype) for g in scatter],
        scratch_shapes=[pltpu.VMEM((t, A_LANES), F32), pltpu.VMEM((t, A_LANES), F32),
                        pltpu.VMEM((2 * A_PAIRS, QBLK, A_KEYS), F32), pltpu.VMEM((4 * A_PAIRS, QBLK, A_KEYS), F32),
                        pltpu.SemaphoreType.DMA(((N_DEV - 1) * n_sc,)),
                        pltpu.SemaphoreType.DMA(((N_DEV - 1) * n_sc,))],
        compiler_params=_cparams(),
    )(qkvg, qkvg, qkvg, qkvg, qkvg, qkvg, qkvg, qkvg, bias, out_a, lse, dz, *scatter)
    return outs[0], outs[1], outs[2], list(outs[3:])


def _b_specs(qblk):
    per = qblk // B_PREV
    q = pl.BlockSpec((qblk, 512), lambda h, j: (j, h))
    g = pl.BlockSpec((qblk, 512), lambda h, j: (j, 2 + h))
    kp = pl.BlockSpec((B_PREV, 128), lambda h, j: (jnp.maximum(per * j - 1, 0), 0))
    kc = pl.BlockSpec((qblk, 128), lambda h, j: (j, 0))
    vp = pl.BlockSpec((B_PREV, 128), lambda h, j: (jnp.maximum(per * j - 1, 0), 1))
    vc = pl.BlockSpec((qblk, 128), lambda h, j: (j, 1))
    bias = pl.BlockSpec((B_GROUP, qblk + B_PREV), lambda h, j: (h, 0))
    sinks = pl.BlockSpec(memory_space=pltpu.SMEM)
    return q, g, kp, kc, vp, vc, bias, sinks


def _b_operands(kp, kc, vp, vc, kvh):
    k = jnp.concatenate([kp[...], kc[...]], axis=0)
    v = jnp.concatenate([vp[...], vc[...]], axis=0)
    kr = pltpu.roll(k, HEAD_DIM, 1)
    vr = pltpu.roll(v, HEAD_DIM, 1)
    first = kvh == 0
    return (jnp.where(first, k, kr), jnp.where(first, kr, k),
            jnp.where(first, v, vr), jnp.where(first, vr, v))


def _attn_b_fwd(qg, kv, bias, sinks):
    t = qg.shape[0]
    qblk = B_QBLK_FWD
    q_spec, g_spec, kp_spec, kc_spec, vp_spec, vc_spec, bias_spec, sink_spec = _b_specs(qblk)

    def body(q_ref, g_ref, kp, kc, vp, vc, w_ref, sink_ref, z_ref, o_ref, lse_ref, b_ref):
        kvh = pl.program_id(0)
        j = pl.program_id(1)
        _fill_bias(B_GROUP, lambda h: w_ref[h:h + 1, :], B_BAND, B_PREV, 1, b_ref, j)
        early = (j < 1).astype(jnp.int32)
        lane_lo = _lane_lo()
        k_lo, k_hi, v_lo, v_hi = _b_operands(kp, kc, vp, vc, kvh)
        for pp in range(B_GROUP // 2):
            cols = slice(128 * pp, 128 * (pp + 1))
            qp = q_ref[:, cols]
            outs, lses = [], []
            for hh in range(2):
                g = 2 * pp + hh
                sel = lane_lo if hh == 0 else jnp.logical_not(lane_lo)
                sink = sink_ref[kvh * B_GROUP + g]
                vv = v_lo if hh == 0 else v_hi
                _, s = _head_logits(qp, k_lo if hh == 0 else k_hi, b_ref, g + B_GROUP * early, sel)
                mx = jnp.maximum(jnp.max(s, axis=-1, keepdims=True), sink)
                e = jnp.exp(s - mx).astype(BF16)
                r = _dot(e, jnp.where(sel, vv, jnp.ones_like(vv)))
                l = _row_sums_everywhere(r, sel) + jnp.exp(sink - mx)
                outs.append(r / l)
                lses.append(mx + jnp.log(l))
            o = jnp.where(lane_lo, outs[0], outs[1])
            silu, _ = _silu_parts(g_ref[:, cols].astype(F32))
            o_ref[:, cols] = o.astype(BF16)
            z_ref[:, cols] = (o * silu).astype(BF16)
            lse_ref[:, cols] = jnp.where(lane_lo, lses[0], lses[1])

    out_spec = pl.BlockSpec((qblk, 512), lambda h, j: (j, h))
    return pl.pallas_call(
        body, name="attn_b_fwd", grid=(B_KV_HEADS, t // qblk),
        in_specs=[q_spec, g_spec, kp_spec, kc_spec, vp_spec, vc_spec, bias_spec, sink_spec],
        out_specs=[out_spec, out_spec, out_spec],
        out_shape=[SDS((t, D_MODEL), BF16), SDS((t, D_MODEL), BF16), SDS((t, D_MODEL), F32)],
        scratch_shapes=[pltpu.VMEM((2 * B_GROUP, qblk, qblk + B_PREV), F32)],
        compiler_params=_cparams(),
    )(qg, qg, kv, kv, kv, kv, bias, sinks)


def _attn_b_bwd(qg, kv, bias, sinks, out_b, lse, dz, bucket_onehot):
    t = qg.shape[0]
    qblk = B_QBLK_BWD
    keys = qblk + B_PREV
    nq = t // qblk
    q_spec, g_spec, kp_spec, kc_spec, vp_spec, vc_spec, bias_spec, sink_spec = _b_specs(qblk)

    def body(q_ref, g_ref, kp, kc, vp, vc, w_ref, sink_ref, o_ref, lse_ref, dz_ref, oh_ref,
             dqg_ref, dkv_ref, dt5_ref, dsink_ref, gt_acc, b_ref):
        kvh = pl.program_id(0)
        j = pl.program_id(1)
        _fill_bias(B_GROUP, lambda h: w_ref[h:h + 1, :], B_BAND, B_PREV, 1, b_ref, j)

        @pl.when(jnp.logical_and(kvh == 0, j == 0))
        def _():
            dkv_ref[...] = jnp.zeros_like(dkv_ref)

        @pl.when(j == 0)
        def _():
            gt_acc[...] = jnp.zeros_like(gt_acc)
            dsink_ref[...] = jnp.zeros_like(dsink_ref)

        early = (j < 1).astype(jnp.int32)
        lane_lo = _lane_lo()
        k_lo, k_hi, v_lo, v_hi = _b_operands(kp, kc, vp, vc, kvh)
        dk_blk = jnp.zeros((keys, 128), F32)
        dv_blk = jnp.zeros((keys, 128), F32)
        for pp in range(B_GROUP // 2):
            cols = slice(128 * pp, 128 * (pp + 1))
            qp = q_ref[:, cols]
            o = o_ref[:, cols].astype(F32)
            lse_pair = lse_ref[:, cols]
            dzf = dz_ref[:, cols].astype(F32)
            silu, dsilu = _silu_parts(g_ref[:, cols].astype(F32))
            do = dzf * silu
            dqg_ref[1, :, cols] = (dzf * o * dsilu).astype(BF16)
            doo = do * o
            dqs = []
            for hh in range(2):
                g = 2 * pp + hh
                sel = lane_lo if hh == 0 else jnp.logical_not(lane_lo)
                sink = sink_ref[kvh * B_GROUP + g]
                kk = k_lo if hh == 0 else k_hi
                vv = v_lo if hh == 0 else v_hi
                qm, s = _head_logits(qp, kk, b_ref, g + B_GROUP * early, sel)
                lse_h = _own_everywhere(lse_pair, sel)
                p = jnp.exp(_minus_rows(s, lse_h))
                delta = jnp.sum(jnp.where(sel, doo, 0.0), axis=-1, keepdims=True)
                dom = jnp.where(sel, do, 0.0).astype(BF16)
                dp = _dot_nt(dom, vv)
                ds = p * (dp - delta)
                gt_acc[g] += ds
                dsink_ref[g:g + 1, :] -= jnp.sum(jnp.exp(sink - lse_h) * delta, axis=0, keepdims=True)
                dsb = ds.astype(BF16)
                dqs.append(_dot(dsb, kk) * SCALE)
                dk_blk = dk_blk + _dot_tn(dsb, qm)
                dv_blk = dv_blk + _dot_tn(p.astype(BF16), dom)
            dqg_ref[0, :, cols] = jnp.where(lane_lo, dqs[0], dqs[1]).astype(BF16)
        mine = lane_lo == (kvh == 0)
        dk_add = jnp.where(mine, dk_blk + pltpu.roll(dk_blk, HEAD_DIM, 1), 0.0)
        dv_add = jnp.where(mine, dv_blk + pltpu.roll(dv_blk, HEAD_DIM, 1), 0.0)

        @pl.when(j >= 1)
        def _():
            rows = pl.ds(pl.multiple_of(j * qblk - B_PREV, B_PREV), B_PREV)
            dkv_ref[0, rows, :] += dk_add[0:B_PREV]
            dkv_ref[1, rows, :] += dv_add[0:B_PREV]

        rows = pl.ds(pl.multiple_of(j * qblk, qblk), qblk)
        dkv_ref[0, rows, :] += dk_add[B_PREV:keys]
        dkv_ref[1, rows, :] += dv_add[B_PREV:keys]

        @pl.when(j == nq - 1)
        def _():
            dd = jnp.concatenate([_offset_sums(gt_acc[g]) for g in range(B_GROUP)], axis=0)
            hi = dd.astype(BF16)
            lo = (dd - hi.astype(F32)).astype(BF16)
            dt5_ref[...] = _dot(hi, oh_ref[...]) + _dot(lo, oh_ref[...])

    blk = pl.BlockSpec((qblk, 512), lambda h, j: (j, h))
    return pl.pallas_call(
        body, name="attn_b_bwd", grid=(B_KV_HEADS, nq),
        in_specs=[q_spec, g_spec, kp_spec, kc_spec, vp_spec, vc_spec, bias_spec, sink_spec, blk, blk, blk,
                  pl.BlockSpec((keys, 128), lambda h, j: (0, 0))],
        out_specs=[pl.BlockSpec((2, qblk, 512), lambda h, j: (0, j, h)),
                   pl.BlockSpec((2, t, 128), lambda h, j: (0, 0, 0)),
                   pl.BlockSpec((B_GROUP, 128), lambda h, j: (h, 0)),
                   pl.BlockSpec((B_GROUP, 128), lambda h, j: (h, 0))],
        out_shape=[SDS((2, t, D_MODEL), BF16), SDS((2, t, 128), F32),
                   SDS((N_HEADS, 128), F32), SDS((N_HEADS, 128), F32)],
        scratch_shapes=[pltpu.VMEM((B_GROUP, qblk, keys), F32), pltpu.VMEM((2 * B_GROUP, qblk, keys), F32)],
        compiler_params=_cparams(),
    )(qg, qg, kv, kv, kv, kv, bias, sinks, out_b, lse, dz, bucket_onehot)


def _a_bias_by_offset(rel_bias):
    m = np.arange(A_DIAG)
    idx = np.clip(A_BAND - 1 - m, -A_REL_CLIP, A_REL_CLIP) + A_REL_CLIP
    by_head = rel_bias[idx].T.reshape(N_HEADS // 2, 2, A_DIAG)
    return jnp.concatenate([by_head, jnp.zeros((N_HEADS // 2, 6, A_DIAG), F32)], axis=1)


def _a_bias_grad(offset_sums):
    first = 319
    tail = jnp.sum(offset_sums[:, :first], axis=1)
    body = jnp.flip(offset_sums[:, first:first + 320], axis=1)
    body = body.at[:, -1].add(tail)
    full = jnp.concatenate([jnp.zeros((N_HEADS, 193), F32), body], axis=1)
    return full.T


def _t5_bucket(rel):
    nb = T5_BUCKETS // 2
    max_exact = nb // 2
    ret = jnp.where(rel > 0, nb, 0)
    n = jnp.abs(rel)
    nf = jnp.maximum(n, 1).astype(jnp.float32)
    large = max_exact + (jnp.log(nf / max_exact) / math.log(T5_MAX_DIST / max_exact)
                         * (nb - max_exact)).astype(jnp.int32)
    large = jnp.minimum(large, nb - 1)
    return ret + jnp.where(n < max_exact, n, large)


def _b_offset_buckets(keys):
    return _t5_bucket(jnp.arange(keys, dtype=jnp.int32) - (B_LEFT_CHUNKS * CHUNK + CHUNK - 1))


def _b_bias_by_offset(t5_table, keys):
    return t5_table[_b_offset_buckets(keys)].T


def _b_bucket_onehot(keys):
    return (_b_offset_buckets(keys)[:, None] == jnp.arange(128)[None, :]).astype(BF16)


def _local_step(my_slot, order, x, target, a_gain_shard, w_in_a_shard, rel_bias, late_shards, kv_gain,
                t5_table, b_gain, sinks, f_gain):
    a_bias = _a_bias_by_offset(rel_bias)
    b_bias_fwd = _b_bias_by_offset(t5_table, B_QBLK_FWD + B_PREV)
    b_bias_bwd = _b_bias_by_offset(t5_table, B_QBLK_BWD + B_PREV)
    sinks_flat = sinks.reshape(N_HEADS)

    xn, qkvg, w_in_a, a_gain = _norm_matmul_gather(order, x, a_gain_shard, w_in_a_shard)
    z_a, out_a, lse_a, (w_in_b, w_out_a, w_out_b, kv_w) = _attn_a_fwd(qkvg, a_bias, late_shards)
    w_out_a = w_out_a.reshape(D_MODEL, D_MODEL)
    w_out_b = w_out_b.reshape(D_MODEL, D_MODEL)
    kv_w = kv_w.reshape(D_MODEL, 2 * 128)
    h1, kvn, hb, kv, qg = _layer_a_out(x, z_a, w_out_a, kv_gain, b_gain, kv_w, w_in_b)
    z_b, out_b, lse_b = _attn_b_fwd(qg, kv, b_bias_fwd, sinks_flat)
    dh2, dh2b, dz_b, loss, d_fn = _layer_b_out_loss(h1, z_b, w_out_b, f_gain, target)

    dqg_b, dkv_b, d_t5, d_sink = _attn_b_bwd(qg, kv, b_bias_bwd, sinks_flat, out_b, lse_b, dz_b,
                                             _b_bucket_onehot(B_QBLK_BWD + B_PREV))
    dh1, dh1b, dz_a, d_bn, d_kn = _layer_b_in_bwd(dqg_b, dkv_b, w_in_b, kv_w, h1, dh2, b_gain, kv_gain, w_out_a)
    early = dict(
        b_w_out=_weight_grad_rows("grad_b_w_out", my_slot, z_b, dh2b[None]),
        b_w_in=_weight_grad_cols("grad_b_w_in", my_slot, hb, [dqg_b],
                                 [(0, o, c, 4 * o + c) for o in range(2) for c in range(4)], 256),
        kv_w=_weight_grad_rows("grad_kv_w", my_slot, kvn, dkv_b),
        a_w_out=_weight_grad_rows("grad_a_w_out", my_slot, z_a, dh1b[None]))
    dqg_a, dkv_a, d_rel, landed = _attn_a_bwd(qkvg, a_bias, out_a, lse_a, dz_a, [g[0] for g in early.values()])
    g_w_in_a = _weight_grad_cols(
        "grad_a_w_in", my_slot, xn, [dqg_a, dkv_a],
        [(0, 0, 0, 0), (0, 0, 1, 1), (1, 0, 0, 2), (1, 0, 1, 3), (1, 1, 0, 4), (1, 1, 1, 5), (0, 1, 0, 6), (0, 1, 1, 7)], 512)
    from_sibling, = _exchange_sibling([g_w_in_a[0]])
    x_i, y_i, c_i, chips = _place()
    del x_i, y_i
    forward_slots = jnp.stack([_slot(*chip, c_i) for chip in chips]).astype(jnp.int32)
    chip_sums = _pre_reduce("chip_sum_a_w_in", g_w_in_a[0], from_sibling, forward_slots)
    grad_x, d_an, from_chips = _layer_a_in_bwd(dqg_a, dkv_a, w_in_a, x, dh1, a_gain, chip_sums)

    matrices = {n: (g[1], [(land, 0, N_DEV - 1)]) for (n, g), land in zip(early.items(), landed)}
    matrices["a_w_in"] = (g_w_in_a[1], [(from_sibling, 3, 1), (from_chips, 0, 3)])
    small = dict(
        loss=loss, a_norm=d_an, a_rel_bias=d_rel[:, :2].reshape(N_HEADS, A_DIAG),
        kv_norm=d_kn, t5_bias=d_t5, b_norm=d_bn, b_sinks=d_sink, final_norm=d_fn)
    return grad_x, small, matrices


def _place():
    x, y, c = lax.axis_index("x"), lax.axis_index("y"), lax.axis_index("c")
    chips = [(1 - x, y), (x, 1 - y), (1 - x, 1 - y)]
    return x, y, c, chips


def _slot(px, py, pc):
    return 4 * px + 2 * py + pc


ANY = pl.BlockSpec(memory_space=pl.ANY)


def _peer(x, y, c, k):
    return (x ^ (k >> 2), y ^ ((k >> 1) & 1), c ^ (k & 1))


def _scatter_copies(grad_refs, land_refs, send_sems, recv_sems):
    x, y, c, _ = _place()
    copies = []
    for t, (grad, land) in enumerate(zip(grad_refs, land_refs)):
        for k in range(1, N_DEV):
            peer = _peer(x, y, c, k)
            sem = (N_DEV - 1) * t + k - 1
            copies.append(pltpu.make_async_remote_copy(
                src_ref=grad.at[_slot(*peer)], dst_ref=land.at[k - 1],
                send_sem=send_sems.at[sem], recv_sem=recv_sems.at[sem],
                device_id=peer, device_id_type=MESH))
    return copies


def _gather_phases(ins, outs, send_sems, recv_sems, local_sems):
    n = len(ins)
    x, y, c, chips = _place()
    me, sibling = (x, y, c), (x, y, 1 - c)

    def copy(t, k, block, to, src=None):
        dst = outs[t].at[_slot(*block)]
        return pltpu.make_async_remote_copy(
            src_ref=dst if src is None else src, dst_ref=dst,
            send_sem=send_sems.at[7 * t + k], recv_sem=recv_sems.at[7 * t + k],
            device_id=to, device_id_type=MESH)

    def lists():
        mine = [pltpu.make_async_copy(ins[t], outs[t].at[_slot(*me)], local_sems.at[t]) for t in range(n)]
        first = []
        for t in range(n):
            first.append(copy(t, 0, me, sibling, src=ins[t]))
            first += [copy(t, 1 + j, me, (*chip, c), src=ins[t]) for j, chip in enumerate(chips)]
        passed = [copy(t, 4 + j, (*chip, c), sibling) for t in range(n) for j, chip in enumerate(chips)]
        return mine, first, passed

    def start():
        mine, first, _ = lists()
        for cp in mine + first:
            cp.start()

    def forward():
        _, _, passed = lists()
        for t in range(n):
            for j, chip in enumerate(chips):
                copy(t, 1 + j, (*chip, c), me).wait_recv()
                passed[3 * t + j].start()

    def finish():
        mine, first, passed = lists()
        for t in range(n):
            copy(t, 0, sibling, me).wait_recv()
            for j, chip in enumerate(chips):
                copy(t, 4 + j, (*chip, 1 - c), me).wait_recv()
        for cp in first + passed:
            cp.wait_send()
        for cp in mine:
            cp.wait()

    return start, forward, finish


def _gather_scratch(n):
    return [pltpu.SemaphoreType.DMA((7 * n,)), pltpu.SemaphoreType.DMA((7 * n,)), pltpu.SemaphoreType.DMA((n,))]


def _exchange_sibling(grads):
    n = len(grads)

    def body(*refs):
        ins, outs = refs[:n], refs[n:2 * n]
        send_sems, recv_sems = refs[2 * n:]
        x, y, c, chips = _place()
        sibling = (x, y, 1 - c)
        copies = []
        for t in range(n):
            blocks = [(*chip, 1 - c) for chip in chips] + [sibling]
            for k, block in enumerate(blocks):
                copies.append(pltpu.make_async_remote_copy(
                    src_ref=ins[t].at[_slot(*block)], dst_ref=outs[t].at[k],
                    send_sem=send_sems.at[4 * t + k], recv_sem=recv_sems.at[4 * t + k],
                    device_id=sibling, device_id_type=MESH))
        for cp in copies:
            cp.start()
        for cp in copies:
            cp.wait()

    return pl.pallas_call(
        body, name="grads_to_sibling",
        in_specs=[ANY] * n, out_specs=[ANY] * n,
        out_shape=[SDS((4, *g.shape[1:]), g.dtype) for g in grads],
        scratch_shapes=[pltpu.SemaphoreType.DMA((4 * n,)), pltpu.SemaphoreType.DMA((4 * n,))],
    )(*grads)


def _chip_copies(sums_ref, land_ref, send_sems, recv_sems):
    x, y, c, chips = _place()
    del x, y
    return [pltpu.make_async_remote_copy(
        src_ref=sums_ref.at[j], dst_ref=land_ref.at[j], send_sem=send_sems.at[j], recv_sem=recv_sems.at[j],
        device_id=(*chip, c), device_id_type=MESH) for j, chip in enumerate(chips)]


def _row_tile(rows):
    return min(rows, 256)


def _pre_reduce(name, g, from_sibling, slots):
    _, r, c = g.shape
    tr = _row_tile(r)

    def body(slots_ref, g_ref, s_ref, o_ref):
        del slots_ref
        o_ref[...] = (g_ref[...].astype(F32) + s_ref[...].astype(F32)).astype(BF16)

    return pl.pallas_call(
        body, name=name,
        grid_spec=pltpu.PrefetchScalarGridSpec(
            num_scalar_prefetch=1, grid=(3, r // tr),
            in_specs=[pl.BlockSpec((1, tr, c), lambda j, i, s: (s[j], i, 0)),
                      pl.BlockSpec((1, tr, c), lambda j, i, s: (j, i, 0))],
            out_specs=pl.BlockSpec((1, tr, c), lambda j, i, s: (j, i, 0))),
        out_shape=SDS((3, r, c), BF16),
        compiler_params=_cparams(),
    )(slots, g, from_sibling)


def _adamw(w, g, m, v):
    m2 = ADAM_B1 * m + (1.0 - ADAM_B1) * g
    v2 = ADAM_B2 * v + (1.0 - ADAM_B2) * jnp.square(g)
    m_hat = m2 / (1.0 - ADAM_B1 ** ADAM_STEP)
    v_hat = v2 / (1.0 - ADAM_B2 ** ADAM_STEP)
    delta = -ADAM_LR * (m_hat / (jnp.sqrt(v_hat) + ADAM_EPS) + ADAM_WD * w)
    return delta, m2, v2


def _reduce_adamw(name, own, partials, w, m, v):
    r, c = own.shape
    tr = _row_tile(r)
    n_p = len(partials)

    def body(own_ref, *rest):
        p_refs, (w_ref, m_ref, v_ref, grad_ref, d_ref, nm_ref, nv_ref) = rest[:n_p], rest[n_p:]
        grad = own_ref[...]
        for p_ref, (_, _, count) in zip(p_refs, partials):
            for j in range(count):
                grad = grad + p_ref[j].astype(F32)
        grad_ref[...] = grad
        d_ref[...], nm_ref[...], nv_ref[...] = _adamw(w_ref[...], grad, m_ref[...], v_ref[...])

    flat = pl.BlockSpec((tr, c), lambda i: (i, 0))
    return pl.pallas_call(
        body, name=name, grid=(r // tr,),
        in_specs=[flat] + [pl.BlockSpec((count, tr, c), lambda i, first=first, count=count: (first // count, i, 0))
                           for _, first, count in partials] + [flat, flat, flat],
        out_specs=[flat, flat, flat, flat],
        out_shape=[SDS((r, c), F32)] * 4,
        compiler_params=_cparams(),
    )(own, *[p[0] for p in partials], w, m, v)


VM = pl.BlockSpec()


def _small_allreduce(parts):
    n = len(parts)

    def body(*refs):
        ins, outs, lands = refs[:n], refs[n:2 * n], refs[2 * n:3 * n]
        send_sems, recv_sems = refs[3 * n:]
        x, y, c, _ = _place()
        my_slot = _slot(x, y, c)
        copies = []
        for t in range(n):
            lands[t][my_slot] = ins[t][...]
            for k in range(1, N_DEV):
                sem = (N_DEV - 1) * t + k - 1
                copies.append(pltpu.make_async_remote_copy(
                    src_ref=ins[t], dst_ref=lands[t].at[my_slot],
                    send_sem=send_sems.at[sem], recv_sem=recv_sems.at[sem],
                    device_id=_peer(x, y, c, k), device_id_type=MESH))
        for cp in copies:
            cp.start()
        for t in range(n):
            for k in range(1, N_DEV):
                sem = (N_DEV - 1) * t + k - 1
                pltpu.make_async_remote_copy(
                    src_ref=ins[t], dst_ref=lands[t].at[_slot(*_peer(x, y, c, k))],
                    send_sem=send_sems.at[sem], recv_sem=recv_sems.at[sem],
                    device_id=(x, y, c), device_id_type=MESH).wait_recv()
        for cp in copies:
            cp.wait_send()
        for t in range(n):
            total = lands[t][0]
            for s in range(1, N_DEV):
                total = total + lands[t][s]
            outs[t][...] = total

    n_sems = (N_DEV - 1) * n
    return pl.pallas_call(
        body, name="small_allreduce",
        in_specs=[VM] * n, out_specs=[VM] * n, out_shape=[SDS(p.shape, F32) for p in parts],
        scratch_shapes=[pltpu.VMEM((N_DEV, *p.shape), F32) for p in parts]
        + [pltpu.SemaphoreType.DMA((n_sems,)), pltpu.SemaphoreType.DMA((n_sems,))],
    )(*parts)


def _small_adamw(my_slot, sums, ws, ms, vs):
    n = len(ws)

    def body(slot_ref, *refs):
        sum_refs, refs = refs[:n + 1], refs[n + 1:]
        w_refs, m_refs, v_refs, refs = refs[:n], refs[n:2 * n], refs[2 * n:3 * n], refs[3 * n:]
        g_refs, d_refs, nm_refs, nv_refs = refs[:n + 1], refs[n + 1:2 * n + 1], refs[2 * n + 1:3 * n + 1], refs[3 * n + 1:]
        for t in range(n + 1):
            if t == 0:
                g = sum_refs[0][:, pl.ds(pl.multiple_of(slot_ref[0] * 128, 128), 128)]
            else:
                g = sum_refs[t][...]
            g_refs[t][...] = g
            if t < n:
                d_refs[t][...], nm_refs[t][...], nv_refs[t][...] = _adamw(w_refs[t][...], g, m_refs[t][...], v_refs[t][...])

    shapes = [SDS(w.shape, F32) for w in ws]
    outs = pl.pallas_call(
        body, name="small_adamw",
        in_specs=[pl.BlockSpec(memory_space=pltpu.SMEM)] + [VM] * (4 * n + 1),
        out_specs=[VM] * (4 * n + 1),
        out_shape=shapes + [SDS(sums[-1].shape, F32)] + shapes * 3,
    )(my_slot, *sums, *ws, *ms, *vs)
    return outs[:n + 1], outs[n + 1:2 * n + 1], outs[2 * n + 1:3 * n + 1], outs[3 * n + 1:]


def kernel(x, a_norm, a_w_in, a_rel_bias, a_w_out, kv_norm, kv_w, t5_bias, b_norm, b_w_in, b_sinks, b_w_out, final_norm, loss_target, m_a_norm, m_a_w_in, m_a_rel_bias, m_a_w_out, m_kv_norm, m_kv_w, m_t5_bias, m_b_norm, m_b_w_in, m_b_sinks, m_b_w_out, m_final_norm, v_a_norm, v_a_w_in, v_a_rel_bias, v_a_w_out, v_kv_norm, v_kv_w, v_t5_bias, v_b_norm, v_b_w_in, v_b_sinks, v_b_w_out, v_final_norm):
    xi, yi, ci = lax.axis_index("x"), lax.axis_index("y"), lax.axis_index("c")
    my_slot = _slot(xi, yi, ci)

    slot_arr = jnp.reshape(my_slot, (1,)).astype(jnp.int32)
    order = _gather_order(xi, yi, ci)
    late_shards = [b_w_in[0].astype(BF16), a_w_out[0].astype(BF16), b_w_out[0].astype(BF16), kv_w.astype(BF16)]
    grad_x, loc, matrices = _local_step(
        slot_arr, order, x[0], loss_target[0], a_norm, a_w_in[0].astype(BF16), a_rel_bias[0], late_shards,
        kv_norm.reshape(1, D_MODEL), t5_bias, b_norm, b_sinks, final_norm.reshape(1, D_MODEL))

    shard_w = dict(a_w_in=a_w_in[0], b_w_in=b_w_in[0], a_w_out=a_w_out[0], b_w_out=b_w_out[0], kv_w=kv_w)
    shard_m = dict(a_w_in=m_a_w_in[0], b_w_in=m_b_w_in[0], a_w_out=m_a_w_out[0], b_w_out=m_b_w_out[0], kv_w=m_kv_w)
    shard_v = dict(a_w_in=v_a_w_in[0], b_w_in=v_b_w_in[0], a_w_out=v_a_w_out[0], b_w_out=v_b_w_out[0], kv_w=v_kv_w)
    big = {n: _reduce_adamw("adamw_" + n, own, partials, shard_w[n], shard_m[n], shard_v[n])
           for n, (own, partials) in matrices.items()}

    names = ("a_norm", "a_rel_bias", "kv_norm", "t5_bias", "b_norm", "b_sinks", "final_norm")
    row = lambda a: a.reshape(1, -1) if a.ndim == 1 else (a[0] if a.ndim == 3 else a)
    small_w = [row(a) for a in (a_norm, a_rel_bias, kv_norm, t5_bias, b_norm, b_sinks, final_norm)]
    small_m = [row(a) for a in (m_a_norm, m_a_rel_bias, m_kv_norm, m_t5_bias, m_b_norm, m_b_sinks, m_final_norm)]
    small_v = [row(a) for a in (v_a_norm, v_a_rel_bias, v_kv_norm, v_t5_bias, v_b_norm, v_b_sinks, v_final_norm)]
    sums = dict(zip(names + ("loss",), _small_allreduce([loc[n] for n in names] + [loc["loss"]])))
    sums["a_rel_bias"] = _a_bias_grad(sums["a_rel_bias"])
    sums["t5_bias"] = sums["t5_bias"][:, :T5_BUCKETS].T
    sums["b_sinks"] = sums["b_sinks"][:, 0].reshape(1, N_HEADS)
    results = _small_adamw(slot_arr, [sums[n] for n in names + ("loss",)], small_w, small_m, small_v)
    like = dict(a_norm=a_norm, a_rel_bias=a_rel_bias, kv_norm=kv_norm, t5_bias=t5_bias, b_norm=b_norm,
                b_sinks=b_sinks, final_norm=final_norm)
    sm = [{n: part[i].reshape(like[n].shape) for i, n in enumerate(names)} for part in results]
    loss = results[0][len(names)][0, 0]

    order = ("a_norm", "a_w_in", "a_rel_bias", "a_w_out", "kv_norm", "kv_w", "t5_bias", "b_norm",
             "b_w_in", "b_sinks", "b_w_out", "final_norm")
    lead = dict(a_w_in=True, b_w_in=True, a_w_out=True, b_w_out=True, kv_w=False)

    def pick(kind, name):
        if name in big:
            val = big[name][kind]
            return val[None] if lead[name] else val
        return sm[kind][name]

    outs = [loss, grad_x[None]]
    for kind in range(4):
        outs += [pick(kind, n) for n in order]
    return tuple(outs)
```

```python
import functools
import math

import numpy as np
import jax
import jax.numpy as jnp
from jax import lax
from jax.experimental import pallas as pl
from jax.experimental.pallas import tpu as pltpu

F32 = jnp.float32
BF16 = jnp.bfloat16
SDS = jax.ShapeDtypeStruct

D_MODEL = 1024
HEAD_DIM = 64
CHUNK = 64
N_HEADS = 16
RMS_EPS = 1e-6
A_LEFT_CHUNKS = 8
A_BAND = (A_LEFT_CHUNKS + 1) * CHUNK
A_REL_CLIP = 256
B_KV_HEADS = 2
B_GROUP = 8
B_LEFT_CHUNKS = 2
B_BAND = (B_LEFT_CHUNKS + 1) * CHUNK
T5_BUCKETS = 32
T5_MAX_DIST = 128
QBLK = 256
A_KEYS = 3 * QBLK
B_QBLK_FWD = 128
B_QBLK_BWD = 256
B_PREV = 128
A_DIAG = A_KEYS
NEG = -1e30
SCALE = HEAD_DIM ** -0.5
N_DEV = 8

ADAM_LR = 0.001
ADAM_B1 = 0.9
ADAM_B2 = 0.999
ADAM_EPS = 1e-08
ADAM_WD = 0.01
ADAM_STEP = 10

VMEM_LIMIT_BYTES = 56 * 1024 * 1024
MESH = pl.DeviceIdType.MESH


def _cparams():
    return pltpu.CompilerParams(vmem_limit_bytes=VMEM_LIMIT_BYTES)


def _dot(a, b):
    return jnp.dot(a, b, preferred_element_type=F32)


def _dot_nt(a, b):
    return lax.dot_general(a, b, (((1,), (1,)), ((), ())), preferred_element_type=F32)


def _dot_tn(a, b):
    return lax.dot_general(a, b, (((0,), (0,)), ((), ())), preferred_element_type=F32)


def _rstd(xf):
    return lax.rsqrt(jnp.mean(xf * xf, axis=-1, keepdims=True) + RMS_EPS)


def _sigmoid(x):
    return 1.0 / (1.0 + jnp.exp(-x))


_GATHER_SEQUENCE = ((0, None), (1, 0), (2, 1), (4, None), (5, None), (3, 2), (6, None))


def _gather_order(x, y, c):
    others = [(1 - x, y), (x, 1 - y), (1 - x, 1 - y)]
    arrivals = [_slot(x, y, 1 - c)] + [_slot(*chip, c) for chip in others] + [_slot(*chip, 1 - c) for chip in others]
    return jnp.stack([_slot(x, y, c)] + [arrivals[a] for a, _ in _GATHER_SEQUENCE]).astype(jnp.int32)


def _norm_matmul_gather(order, x, gain_shard, w_shard):
    t = x.shape[0]
    dw, tn = w_shard.shape
    tm = min(t, 1024)
    n_m = t // tm

    def body(order_ref, x_ref, gs_ref, shard_ref, xn_ref, o_ref, full_ref, gain_ref,
             xn_all, wbuf, gland, send_sems, recv_sems, gsend_sems, grecv_sems, load_sems, own_sem):
        n, m = pl.program_id(0), pl.program_id(1)
        x_i, y_i, c_i, chips = _place()
        me, sibling = (x_i, y_i, c_i), (x_i, y_i, 1 - c_i)

        def send(k, block, to, src=None):
            dst = full_ref.at[_slot(*block)]
            return pltpu.make_async_remote_copy(
                src_ref=dst if src is None else src, dst_ref=dst,
                send_sem=send_sems.at[k], recv_sem=recv_sems.at[k], device_id=to, device_id_type=MESH)

        own = pltpu.make_async_copy(shard_ref, full_ref.at[_slot(*me)], own_sem)
        first = [send(0, me, sibling, src=shard_ref)]
        first += [send(1 + j, me, (*chip, c_i), src=shard_ref) for j, chip in enumerate(chips)]
        forwards = [send(4 + j, (*chip, c_i), sibling) for j, chip in enumerate(chips)]
        arrivals = [send(0, sibling, me)] + [send(1 + j, (*chip, c_i), me) for j, chip in enumerate(chips)]
        arrivals += [send(4 + j, (*chip, 1 - c_i), me) for j, chip in enumerate(chips)]
        gains = [pltpu.make_async_remote_copy(
            src_ref=gs_ref, dst_ref=gland.at[_slot(*me)], send_sem=gsend_sems.at[k - 1],
            recv_sem=grecv_sems.at[k - 1], device_id=_peer(x_i, y_i, c_i, k), device_id_type=MESH)
            for k in range(1, N_DEV)]

        @pl.when(jnp.logical_and(n == 0, m == 0))
        def _():
            own.start()
            for cp in gains + first:
                cp.start()
            pltpu.make_async_copy(shard_ref, wbuf.at[0], load_sems.at[0]).start()
            gland[_slot(*me)] = gs_ref[...]
            for k in range(1, N_DEV):
                pltpu.make_async_remote_copy(
                    src_ref=gs_ref, dst_ref=gland.at[_slot(*_peer(x_i, y_i, c_i, k))],
                    send_sem=gsend_sems.at[k - 1], recv_sem=grecv_sems.at[k - 1],
                    device_id=me, device_id_type=MESH).wait_recv()
            for s in range(N_DEV):
                gain_ref[:, 128 * s:128 * (s + 1)] = gland[s]

        rows = pl.ds(pl.multiple_of(m * tm, tm), tm)

        @pl.when(n == 0)
        def _():
            xf = x_ref[...]
            xn = ((xf * _rstd(xf)) * gain_ref[...]).astype(BF16)
            xn_all[rows, :] = xn
            xn_ref[...] = xn

        @pl.when(m == 0)
        def _():
            pltpu.make_async_copy(full_ref.at[0], wbuf.at[n % 2], load_sems.at[n % 2]).wait()

        o_ref[...] = _dot(xn_all[rows, :], wbuf[n % 2]).astype(BF16)

        for k, (arrival, forward) in enumerate(_GATHER_SEQUENCE):
            @pl.when(jnp.logical_and(n == k, m == n_m - 1))
            def _(k=k, arrival=arrival, forward=forward):
                arrivals[arrival].wait_recv()
                if forward is not None:
                    forwards[forward].start()
                pltpu.make_async_copy(full_ref.at[order_ref[k + 1]], wbuf.at[(k + 1) % 2],
                                      load_sems.at[(k + 1) % 2]).start()

        @pl.when(jnp.logical_and(n == N_DEV - 1, m == n_m - 1))
        def _():
            for cp in gains + first + forwards:
                cp.wait_send()
            own.wait()

    held = lambda n, m, order: (jnp.where(n == 0, m, n_m - 1), 0)
    return pl.pallas_call(
        body, name="norm_matmul_gather",
        grid_spec=pltpu.PrefetchScalarGridSpec(
            num_scalar_prefetch=1, grid=(N_DEV, n_m),
            in_specs=[pl.BlockSpec((tm, D_MODEL), held),
                      pl.BlockSpec((1, 128), lambda n, m, order: (0, 0)), ANY],
            out_specs=[pl.BlockSpec((tm, D_MODEL), held),
                       pl.BlockSpec((tm, tn), lambda n, m, order: (m, order[n])),
                       ANY, pl.BlockSpec((1, D_MODEL), lambda n, m, order: (0, 0))],
            scratch_shapes=[pltpu.VMEM((t, D_MODEL), BF16), pltpu.VMEM((2, dw, tn), BF16),
                            pltpu.VMEM((N_DEV, 1, 128), F32),
                            pltpu.SemaphoreType.DMA((7,)), pltpu.SemaphoreType.DMA((7,)),
                            pltpu.SemaphoreType.DMA((7,)), pltpu.SemaphoreType.DMA((7,)),
                            pltpu.SemaphoreType.DMA((2,)), pltpu.SemaphoreType.DMA]),
        out_shape=[SDS((t, D_MODEL), BF16), SDS((t, N_DEV * tn), BF16), SDS((N_DEV, dw, tn), BF16),
                   SDS((1, D_MODEL), F32)],
        compiler_params=_cparams(),
    )(order, x, gain_shard, w_shard)


def _layer_a_out(x, z, w_out, kv_gain, b_gain, kv_w, w_in_b):
    t = x.shape[0]
    tm = min(t, 512)
    nb, _, tn = w_in_b.shape

    def body(x_ref, z_ref, wo_ref, kvg_ref, bg_ref, kvw_ref, wb_ref,
             h1_ref, kvn_ref, hb_ref, kv_ref, qg_ref):
        h1 = x_ref[...] + _dot(z_ref[...], wo_ref[...])
        h1_ref[...] = h1
        y0 = h1 * _rstd(h1)
        kvn = (y0 * kvg_ref[...]).astype(BF16)
        hb = (y0 * bg_ref[...]).astype(BF16)
        kvn_ref[...] = kvn
        hb_ref[...] = hb
        kv_ref[...] = _dot(kvn, kvw_ref[...]).astype(BF16)
        for i in range(nb):
            qg_ref[:, i * tn:(i + 1) * tn] = _dot(hb, wb_ref[i]).astype(BF16)

    row = lambda m: (m, 0)
    fix2 = lambda m: (0, 0)
    return pl.pallas_call(
        body, name="layer_a_out", grid=(t // tm,),
        in_specs=[pl.BlockSpec((tm, D_MODEL), row), pl.BlockSpec((tm, D_MODEL), row),
                  pl.BlockSpec((D_MODEL, D_MODEL), fix2),
                  pl.BlockSpec((1, D_MODEL), fix2), pl.BlockSpec((1, D_MODEL), fix2),
                  pl.BlockSpec((D_MODEL, 256), fix2),
                  pl.BlockSpec((nb, D_MODEL, tn), lambda m: (0, 0, 0))],
        out_specs=[pl.BlockSpec((tm, D_MODEL), row), pl.BlockSpec((tm, D_MODEL), row),
                   pl.BlockSpec((tm, D_MODEL), row), pl.BlockSpec((tm, 256), row),
                   pl.BlockSpec((tm, nb * tn), row)],
        out_shape=[SDS((t, D_MODEL), F32), SDS((t, D_MODEL), BF16), SDS((t, D_MODEL), BF16),
                   SDS((t, 256), BF16), SDS((t, nb * tn), BF16)],
        compiler_params=_cparams(),
    )(x, z, w_out, kv_gain, b_gain, kv_w, w_in_b)


def _layer_b_out_loss(h1, z, w_out, f_gain, target):
    t = h1.shape[0]
    tm = min(t, 512)

    def body(h1_ref, z_ref, wo_ref, fg_ref, tgt_ref,
             dh2_ref, dh2b_ref, dz_ref, loss_ref, dfn_ref):
        @pl.when(pl.program_id(0) == 0)
        def _():
            loss_ref[...] = jnp.zeros_like(loss_ref)
            dfn_ref[...] = jnp.zeros_like(dfn_ref)

        h2 = h1_ref[...] + _dot(z_ref[...], wo_ref[...])
        r = _rstd(h2)
        yn = h2 * r
        fg = fg_ref[...]
        err = yn * fg - tgt_ref[...]
        loss_ref[...] += (0.5 / D_MODEL) * jnp.sum(err * err)
        dy = err * (1.0 / D_MODEL)
        dfn_ref[...] += jnp.sum(dy * yn, axis=0, keepdims=True)
        u = dy * fg
        dh2 = r * u - h2 * ((r * r * r) * jnp.mean(u * h2, axis=-1, keepdims=True))
        dh2_ref[...] = dh2
        dh2b = dh2.astype(BF16)
        dh2b_ref[...] = dh2b
        dz_ref[...] = _dot_nt(dh2b, wo_ref[...]).astype(BF16)

    row = lambda m: (m, 0)
    fix2 = lambda m: (0, 0)
    return pl.pallas_call(
        body, name="layer_b_out_loss", grid=(t // tm,),
        in_specs=[pl.BlockSpec((tm, D_MODEL), row), pl.BlockSpec((tm, D_MODEL), row),
                  pl.BlockSpec((D_MODEL, D_MODEL), fix2), pl.BlockSpec((1, D_MODEL), fix2),
                  pl.BlockSpec((tm, D_MODEL), row)],
        out_specs=[pl.BlockSpec((tm, D_MODEL), row), pl.BlockSpec((tm, D_MODEL), row),
                   pl.BlockSpec((tm, D_MODEL), row), pl.BlockSpec((1, 128), fix2),
                   pl.BlockSpec((1, D_MODEL), fix2)],
        out_shape=[SDS((t, D_MODEL), F32), SDS((t, D_MODEL), BF16), SDS((t, D_MODEL), BF16),
                   SDS((1, 128), F32), SDS((1, D_MODEL), F32)],
        compiler_params=_cparams(),
    )(h1, z, w_out, f_gain, target)


def _layer_b_in_bwd(dqg, dkv, w_in_b, kv_w, h1, dh2, b_gain, kv_gain, w_out_a):
    t = h1.shape[0]
    tm = min(t, 256)
    nb, _, tn = w_in_b.shape
    per = D_MODEL // tn

    def body(dqg_ref, dkv_ref, wb_ref, kvw_ref, h1_ref, dh2_ref, bg_ref, kvg_ref, wo_ref,
             dh1_ref, dh1b_ref, dz_ref, dbn_ref, dkn_ref):
        @pl.when(pl.program_id(0) == 0)
        def _():
            dbn_ref[...] = jnp.zeros_like(dbn_ref)
            dkn_ref[...] = jnp.zeros_like(dkn_ref)

        dhb = jnp.zeros((tm, D_MODEL), F32)
        for i in range(nb):
            blk = dqg_ref[i // per, :, (i % per) * tn:(i % per + 1) * tn]
            dhb = dhb + _dot_nt(blk, wb_ref[i])
        dkn = (_dot_nt(dkv_ref[0].astype(BF16), kvw_ref[:, 0:128])
               + _dot_nt(dkv_ref[1].astype(BF16), kvw_ref[:, 128:256]))
        h1 = h1_ref[...]
        r = _rstd(h1)
        xr = h1 * r
        dbn_ref[...] += jnp.sum(dhb * xr, axis=0, keepdims=True)
        dkn_ref[...] += jnp.sum(dkn * xr, axis=0, keepdims=True)
        u = dhb * bg_ref[...] + dkn * kvg_ref[...]
        dh1 = dh2_ref[...] + r * u - h1 * ((r * r * r) * jnp.mean(u * h1, axis=-1, keepdims=True))
        dh1_ref[...] = dh1
        dh1b = dh1.astype(BF16)
        dh1b_ref[...] = dh1b
        dz_ref[...] = _dot_nt(dh1b, wo_ref[...]).astype(BF16)

    row = lambda m: (m, 0)
    fix2 = lambda m: (0, 0)
    return pl.pallas_call(
        body, name="layer_b_in_bwd", grid=(t // tm,),
        in_specs=[pl.BlockSpec((2, tm, D_MODEL), lambda m: (0, m, 0)),
                  pl.BlockSpec((2, tm, 128), lambda m: (0, m, 0)),
                  pl.BlockSpec((nb, D_MODEL, tn), lambda m: (0, 0, 0)),
                  pl.BlockSpec((D_MODEL, 256), fix2),
                  pl.BlockSpec((tm, D_MODEL), row), pl.BlockSpec((tm, D_MODEL), row),
                  pl.BlockSpec((1, D_MODEL), fix2), pl.BlockSpec((1, D_MODEL), fix2),
                  pl.BlockSpec((D_MODEL, D_MODEL), fix2)],
        out_specs=[pl.BlockSpec((tm, D_MODEL), row), pl.BlockSpec((tm, D_MODEL), row),
                   pl.BlockSpec((tm, D_MODEL), row), pl.BlockSpec((1, D_MODEL), fix2),
                   pl.BlockSpec((1, D_MODEL), fix2)],
        out_shape=[SDS((t, D_MODEL), F32), SDS((t, D_MODEL), BF16), SDS((t, D_MODEL), BF16),
                   SDS((1, D_MODEL), F32), SDS((1, D_MODEL), F32)],
        compiler_params=_cparams(),
    )(dqg, dkv, w_in_b, kv_w, h1, dh2, b_gain, kv_gain, w_out_a)


def _layer_a_in_bwd(dqg, dkv, w_in_a, x, dh1, a_gain, chip_sums):
    t = x.shape[0]
    tm = min(t, 256)
    nb, _, tn = w_in_a.shape
    per = D_MODEL // tn

    def body(dqg_ref, dkv_ref, w_ref, x_ref, dh1_ref, ag_ref, sums_ref, dx_ref, dan_ref, land_ref,
             send_sems, recv_sems):
        @pl.when(pl.program_id(0) == 0)
        def _():
            dan_ref[...] = jnp.zeros_like(dan_ref)
            for cp in _chip_copies(sums_ref, land_ref, send_sems, recv_sems):
                cp.start()

        dxn = jnp.zeros((tm, D_MODEL), F32)
        for i in range(nb):
            part = i // per
            src = dqg_ref if part in (0, 3) else dkv_ref
            outer = {0: 0, 3: 1, 1: 0, 2: 1}[part]
            blk = src[outer, :, (i % per) * tn:(i % per + 1) * tn]
            dxn = dxn + _dot_nt(blk, w_ref[i])
        xf = x_ref[...]
        r = _rstd(xf)
        dan_ref[...] += jnp.sum(dxn * (xf * r), axis=0, keepdims=True)
        u = dxn * ag_ref[...]
        dx_ref[...] = dh1_ref[...] + r * u - xf * ((r * r * r) * jnp.mean(u * xf, axis=-1, keepdims=True))

        @pl.when(pl.program_id(0) == t // tm - 1)
        def _():
            for cp in _chip_copies(sums_ref, land_ref, send_sems, recv_sems):
                cp.wait()

    row = lambda m: (m, 0)
    fix2 = lambda m: (0, 0)
    return pl.pallas_call(
        body, name="layer_a_in_bwd", grid=(t // tm,),
        in_specs=[pl.BlockSpec((2, tm, D_MODEL), lambda m: (0, m, 0)),
                  pl.BlockSpec((2, tm, D_MODEL), lambda m: (0, m, 0)),
                  pl.BlockSpec((nb, D_MODEL, tn), lambda m: (0, 0, 0)),
                  pl.BlockSpec((tm, D_MODEL), row), pl.BlockSpec((tm, D_MODEL), row),
                  pl.BlockSpec((1, D_MODEL), fix2), ANY],
        out_specs=[pl.BlockSpec((tm, D_MODEL), row), pl.BlockSpec((1, D_MODEL), fix2), ANY],
        out_shape=[SDS((t, D_MODEL), F32), SDS((1, D_MODEL), F32), SDS(chip_sums.shape, chip_sums.dtype)],
        scratch_shapes=[pltpu.SemaphoreType.DMA((3,)), pltpu.SemaphoreType.DMA((3,))],
        compiler_params=_cparams(),
    )(dqg, dkv, w_in_a, x, dh1, a_gain, chip_sums)


def _lut(s, vals):
    r = jnp.int32(vals[0])
    for i in range(1, len(vals)):
        r = jnp.where(s == i, jnp.int32(vals[i]), r)
    return r


def _held(steps, i):
    seq, cur = [None] * len(steps), None
    for k in range(len(steps) - 1, -1, -1):
        if steps[k][0] == i:
            cur = steps[k][1:3]
        seq[k] = cur
    for k in range(len(steps)):
        cur = seq[k] = seq[k] if seq[k] is not None else cur
    return seq


def _weight_grad_cols(name, my_slot, a, bs, steps, tn):
    t, dw = a.shape
    n_arr = len(bs)
    which = [s[0] for s in steps]
    blks = [s[3] for s in steps]

    def body(slot_ref, a_ref, *rest):
        b_refs, (o_ref, own_ref, at_ref) = rest[:n_arr], rest[n_arr:]
        s = pl.program_id(0)

        @pl.when(s == 0)
        def _():
            at_ref[...] = a_ref[...].T

        for i in range(n_arr):
            @pl.when(_lut(s, which) == i)
            def _(i=i):
                res = _dot(at_ref[...], b_refs[i][0])
                o_ref[0] = res.astype(BF16)

                @pl.when(_lut(s, blks) == slot_ref[0])
                def _():
                    own_ref[...] = res

    def b_spec(i):
        held = _held(steps, i)
        return pl.BlockSpec((1, t, tn), lambda s, slot: (_lut(s, [h[0] for h in held]), 0,
                                                         _lut(s, [h[1] for h in held])))

    return pl.pallas_call(
        body, name=name,
        grid_spec=pltpu.PrefetchScalarGridSpec(
            num_scalar_prefetch=1, grid=(len(steps),),
            in_specs=[pl.BlockSpec((t, dw), lambda s, slot: (0, 0))] + [b_spec(i) for i in range(n_arr)],
            out_specs=[pl.BlockSpec((1, dw, tn), lambda s, slot: (_lut(s, blks), 0, 0)),
                       pl.BlockSpec((dw, tn), lambda s, slot: (0, 0))],
            scratch_shapes=[pltpu.VMEM((dw, t), BF16)]),
        out_shape=[SDS((N_DEV, dw, tn), BF16), SDS((dw, tn), F32)],
        compiler_params=_cparams(),
    )(my_slot, a, *bs)


def _weight_grad_rows(name, my_slot, a, b):
    t, dw = a.shape
    n_o, _, c = b.shape
    rows = dw // N_DEV

    def body(slot_ref, a_ref, b_ref, o_ref, own_ref):
        at = a_ref[...].T
        res = [_dot(at, b_ref[o].astype(BF16)) for o in range(n_o)]
        for o in range(n_o):
            o_ref[0, :, o * c:(o + 1) * c] = res[o].astype(BF16)

        @pl.when(pl.program_id(0) == slot_ref[0])
        def _():
            for o in range(n_o):
                own_ref[:, o * c:(o + 1) * c] = res[o]

    return pl.pallas_call(
        body, name=name,
        grid_spec=pltpu.PrefetchScalarGridSpec(
            num_scalar_prefetch=1, grid=(N_DEV,),
            in_specs=[pl.BlockSpec((t, rows), lambda s, slot: (0, s)),
                      pl.BlockSpec((n_o, t, c), lambda s, slot: (0, 0, 0))],
            out_specs=[pl.BlockSpec((1, rows, n_o * c), lambda s, slot: (s, 0, 0)),
                       pl.BlockSpec((rows, n_o * c), lambda s, slot: (0, 0))]),
        out_shape=[SDS((N_DEV, rows, n_o * c), BF16), SDS((rows, n_o * c), F32)],
        compiler_params=_cparams(),
    )(my_slot, a, b)


def _lane_lo():
    return lax.broadcasted_iota(jnp.int32, (1, 128), 1) < HEAD_DIM


def _offset_sums(gt):
    keys = gt.shape[1]
    gc = gt[0:CHUNK]
    for cc in range(1, gt.shape[0] // CHUNK):
        gc = gc + pltpu.roll(gt[cc * CHUNK:(cc + 1) * CHUNK], keys - cc * CHUNK, 1)
    hi = gc.astype(BF16)
    lo = (gc - hi.astype(F32)).astype(BF16)
    flip = (lax.broadcasted_iota(jnp.int32, (CHUNK, CHUNK), 0)
            + lax.broadcasted_iota(jnp.int32, (CHUNK, CHUNK), 1) == CHUNK - 1).astype(BF16)
    gf = _dot(flip, hi) + _dot(flip, lo)
    skew = pltpu.roll(gf, 0, 1, stride=1, stride_axis=0)
    return jnp.sum(skew, axis=0, keepdims=True)


def _band_bias(w_row, band, rows):
    keys = w_row.shape[1]
    base = jnp.broadcast_to(w_row, (CHUNK, keys))
    skew = pltpu.roll(base, 0, 1, stride=1, stride_axis=0)
    skew = pltpu.roll(skew, keys - (CHUNK - 1), 1)
    col = lax.broadcasted_iota(jnp.int32, (CHUNK, keys), 1)
    chunk0 = jnp.where(col < band, skew, NEG)
    return jnp.concatenate(
        [chunk0] + [pltpu.roll(chunk0, cc * CHUNK, 1) for cc in range(1, rows // CHUNK)], axis=0)


def _silu_parts(g):
    sg = _sigmoid(g)
    return g * sg, sg * (1.0 + g * (1.0 - sg))


A_PAIRS = 2
A_ROWS = QBLK
A_LANES = 128 * A_PAIRS
A_STEPS = D_MODEL // A_LANES


def _a_specs():
    q = pl.BlockSpec((QBLK, A_LANES), lambda p, j: (j, p))
    ks = [pl.BlockSpec((QBLK, A_LANES), lambda p, j, b=b: (jnp.maximum(j - 2 + b, 0), A_STEPS + p)) for b in range(3)]
    vs = [pl.BlockSpec((QBLK, A_LANES), lambda p, j, b=b: (jnp.maximum(j - 2 + b, 0), 2 * A_STEPS + p))
          for b in range(3)]
    g = pl.BlockSpec((QBLK, A_LANES), lambda p, j: (j, 3 * A_STEPS + p))
    bias = pl.BlockSpec((A_PAIRS, 8, A_KEYS), lambda p, j: (p, 0, 0))
    return q, ks, vs, g, bias


def _a_fill_bias(w_ref, b_ref, j):
    _fill_bias(2 * A_PAIRS, lambda h: w_ref[h // 2, h % 2:h % 2 + 1, :], A_BAND, QBLK * (2 - j), 2, b_ref, j)


def _fill_bias(n, get_row, band, first_valid_col, early, bias_scr, j):
    @pl.when(j == 0)
    def _():
        for h in range(n):
            bias_scr[h] = _band_bias(get_row(h), band, bias_scr.shape[1])

    @pl.when(j < early)
    def _():
        keys = bias_scr.shape[2]
        col_ok = lax.broadcasted_iota(jnp.int32, (1, keys), 1) >= first_valid_col
        for h in range(n):
            bias_scr[n + h] = jnp.where(col_ok, bias_scr[h], NEG)


def _head_logits(q, k, bias_scr, idx, sel):
    qm = jnp.where(sel, q, jnp.zeros_like(q)) * SCALE
    return qm, _dot_nt(qm, k) + bias_scr[idx]


def _row_sums_everywhere(r, sel):
    return jnp.where(sel, pltpu.roll(r, HEAD_DIM, 1), r)


def _own_everywhere(x, sel):
    return jnp.where(sel, x, pltpu.roll(x, HEAD_DIM, 1))


def _minus_rows(s, row_full):
    return jnp.concatenate([s[:, i:i + 128] - row_full for i in range(0, s.shape[1], 128)], axis=1)


def _attn_a_fwd(qkvg, bias, gather):
    t = qkvg.shape[0]
    nq = t // QBLK
    n_g = len(gather)
    q_spec, k_specs, v_specs, g_spec, bias_spec = _a_specs()

    def body(q_ref, k0, k1, k2, v0, v1, v2, g_ref, w_ref, *rest):
        shard_refs, rest = rest[:n_g], rest[n_g:]
        z_ref, o_ref, lse_ref = rest[:3]
        full_refs, (b_ref, *comm) = rest[3:3 + n_g], rest[3 + n_g:]
        p = pl.program_id(0)
        j = pl.program_id(1)
        start, forward, finish = _gather_phases(shard_refs, full_refs, *comm)
        pl.when(jnp.logical_and(p == 0, j == 0))(start)
        pl.when(jnp.logical_and(p == A_STEPS // 2, j == 0))(forward)
        _a_fill_bias(w_ref, b_ref, j)
        early = (j < 2).astype(jnp.int32)
        lane_lo = _lane_lo()
        sels = (lane_lo, jnp.logical_not(lane_lo))
        for pp in range(A_PAIRS):
            cols = slice(128 * pp, 128 * (pp + 1))
            k = jnp.concatenate([k0[:, cols], k1[:, cols], k2[:, cols]], axis=0)
            v = jnp.concatenate([v0[:, cols], v1[:, cols], v2[:, cols]], axis=0)
            v1s = [jnp.where(sel, v, jnp.ones_like(v)) for sel in sels]
            for rb in range(QBLK // A_ROWS):
                rows = slice(rb * A_ROWS, (rb + 1) * A_ROWS)
                q = q_ref[rows, cols]
                outs, lses = [], []
                qm2 = jnp.concatenate([jnp.where(sel, q, jnp.zeros_like(q)) for sel in sels], axis=0) * SCALE
                s2 = _dot_nt(qm2, k)
                for hh, sel in enumerate(sels):
                    s = s2[hh * A_ROWS:(hh + 1) * A_ROWS] + b_ref[2 * pp + hh + 2 * A_PAIRS * early, rows, :]
                    mx = jnp.max(s, axis=-1, keepdims=True)
                    e = jnp.exp(s - mx).astype(BF16)
                    r = _dot(e, v1s[hh])
                    l = _row_sums_everywhere(r, sel)
                    outs.append(r / l)
                    lses.append(mx + jnp.log(l))
                o = jnp.where(lane_lo, outs[0], outs[1])
                silu, _ = _silu_parts(g_ref[rows, cols].astype(F32))
                o_ref[rows, cols] = o.astype(BF16)
                z_ref[rows, cols] = (o * silu).astype(BF16)
                lse_ref[rows, cols] = jnp.where(lane_lo, lses[0], lses[1])
        pl.when(jnp.logical_and(p == A_STEPS - 1, j == nq - 1))(finish)

    out_spec = pl.BlockSpec((QBLK, A_LANES), lambda p, j: (j, p))
    outs = pl.pallas_call(
        body, name="attn_a_fwd", grid=(A_STEPS, nq),
        in_specs=[q_spec, *k_specs, *v_specs, g_spec, bias_spec] + [ANY] * n_g,
        out_specs=[out_spec, out_spec, out_spec] + [ANY] * n_g,
        out_shape=[SDS((t, D_MODEL), BF16), SDS((t, D_MODEL), BF16), SDS((t, D_MODEL), F32)]
        + [SDS((N_DEV, *s.shape), s.dtype) for s in gather],
        scratch_shapes=[pltpu.VMEM((4 * A_PAIRS, QBLK, A_KEYS), F32)] + _gather_scratch(n_g),
        compiler_params=_cparams(),
    )(qkvg, qkvg, qkvg, qkvg, qkvg, qkvg, qkvg, qkvg, bias, *gather)
    return outs[0], outs[1], outs[2], list(outs[3:])


def _attn_a_bwd(qkvg, bias, out_a, lse, dz, scatter):
    t = qkvg.shape[0]
    nq = t // QBLK
    n_sc = len(scatter)
    q_spec, k_specs, v_specs, g_spec, bias_spec = _a_specs()

    def body(q_ref, k0, k1, k2, v0, v1, v2, g_ref, w_ref, o_ref, lse_ref, dz_ref, *rest):
        sc_refs, rest = rest[:n_sc], rest[n_sc:]
        dqg_ref, dkv_ref, dg_ref = rest[:3]
        land_refs, rest = rest[3:3 + n_sc], rest[3 + n_sc:]
        dk_acc, dv_acc, gt_acc, b_ref, send_sems, recv_sems = rest
        j = pl.program_id(1)
        first = jnp.logical_and(pl.program_id(0) == 0, j == 0)
        last = jnp.logical_and(pl.program_id(0) == A_STEPS - 1, j == nq - 1)

        @pl.when(first)
        def _():
            for cp in _scatter_copies(sc_refs, land_refs, send_sems, recv_sems):
                cp.start()

        _a_fill_bias(w_ref, b_ref, j)

        @pl.when(j == 0)
        def _():
            dk_acc[...] = jnp.zeros_like(dk_acc)
            dv_acc[...] = jnp.zeros_like(dv_acc)
            gt_acc[...] = jnp.zeros_like(gt_acc)

        early = (j < 2).astype(jnp.int32)
        lane_lo = _lane_lo()
        for pp in range(A_PAIRS):
            cols = slice(128 * pp, 128 * (pp + 1))
            q = q_ref[:, cols]
            k = jnp.concatenate([k0[:, cols], k1[:, cols], k2[:, cols]], axis=0)
            v = jnp.concatenate([v0[:, cols], v1[:, cols], v2[:, cols]], axis=0)
            o = o_ref[:, cols].astype(F32)
            lse_pair = lse_ref[:, cols]
            dzf = dz_ref[:, cols].astype(F32)
            silu, dsilu = _silu_parts(g_ref[:, cols].astype(F32))
            do = dzf * silu
            dqg_ref[1, :, cols] = (dzf * o * dsilu).astype(BF16)
            doo = do * o
            sels = (lane_lo, jnp.logical_not(lane_lo))
            qm2 = jnp.concatenate([jnp.where(sel, q, jnp.zeros_like(q)) for sel in sels], axis=0) * SCALE
            dom2 = jnp.concatenate([jnp.where(sel, do, 0.0) for sel in sels], axis=0).astype(BF16)
            s2 = _dot_nt(qm2, k)
            dp2 = _dot_nt(dom2, v)
            ps, dss = [], []
            for hh, sel in enumerate(sels):
                rows = slice(hh * QBLK, (hh + 1) * QBLK)
                s = s2[rows] + b_ref[2 * pp + hh + 2 * A_PAIRS * early]
                p = jnp.exp(_minus_rows(s, _own_everywhere(lse_pair, sel)))
                delta = jnp.sum(jnp.where(sel, doo, 0.0), axis=-1, keepdims=True)
                ds = p * (dp2[rows] - delta)
                gt_acc[2 * pp + hh] += ds
                ps.append(p.astype(BF16))
                dss.append(ds.astype(BF16))
            dsb2 = jnp.concatenate(dss, axis=0)
            dq2 = _dot(dsb2, k) * SCALE
            dk_blk = _dot_tn(dsb2, qm2)
            dv_blk = _dot_tn(jnp.concatenate(ps, axis=0), dom2)
            dqg_ref[0, :, cols] = jnp.where(lane_lo, dq2[0:QBLK], dq2[QBLK:2 * QBLK]).astype(BF16)
            for b in range(3):
                @pl.when(j - 2 + b >= 0)
                def _(b=b, cols=cols, dk_blk=dk_blk, dv_blk=dv_blk):
                    rows = pl.ds(pl.multiple_of((j - 2 + b) * QBLK, QBLK), QBLK)
                    dk_acc[rows, cols] += dk_blk[b * QBLK:(b + 1) * QBLK]
                    dv_acc[rows, cols] += dv_blk[b * QBLK:(b + 1) * QBLK]

        @pl.when(j == nq - 1)
        def _():
            dkv_ref[0] = dk_acc[...].astype(BF16)
            dkv_ref[1] = dv_acc[...].astype(BF16)
            for pp in range(A_PAIRS):
                dg_ref[pp] = jnp.concatenate([_offset_sums(gt_acc[2 * pp]), _offset_sums(gt_acc[2 * pp + 1]),
                                              jnp.zeros((6, A_DIAG), F32)], axis=0)

        @pl.when(last)
        def _():
            for cp in _scatter_copies(sc_refs, land_refs, send_sems, recv_sems):
                cp.wait()

    blk = pl.BlockSpec((QBLK, A_LANES), lambda p, j: (j, p))
    outs = pl.pallas_call(
        body, name="attn_a_bwd", grid=(A_STEPS, nq),
        in_specs=[q_spec, *k_specs, *v_specs, g_spec, bias_spec, blk, blk, blk] + [ANY] * n_sc,
        out_specs=[pl.BlockSpec((2, QBLK, A_LANES), lambda p, j: (0, j, p)),
                   pl.BlockSpec((2, t, A_LANES), lambda p, j: (0, 0, p)),
                   pl.BlockSpec((A_PAIRS, 8, A_DIAG), lambda p, j: (p, 0, 0))] + [ANY] * n_sc,
        out_shape=[SDS((2, t, D_MODEL), BF16), SDS((2, t, D_MODEL), BF16), SDS((N_HEADS // 2, 8, A_DIAG), F32)]
        + [SDS((N_DEV - 1, *g.shape[1:]), g.dtype) for g in scatter],
        scratch_shapes=[pltpu.VMEM((t, A_LANES), F32), pltpu.VMEM((t, A_LANES), F32),
                        pltpu.VMEM((2 * A_PAIRS, QBLK, A_KEYS), F32), pltpu.VMEM((4 * A_PAIRS, QBLK, A_KEYS), F32),
                        pltpu.SemaphoreType.DMA(((N_DEV - 1) * n_sc,)),
                        pltpu.SemaphoreType.DMA(((N_DEV - 1) * n_sc,))],
        compiler_params=_cparams(),
    )(qkvg, qkvg, qkvg, qkvg, qkvg, qkvg, qkvg, qkvg, bias, out_a, lse, dz, *scatter)
    return outs[0], outs[1], outs[2], list(outs[3:])


def _b_specs(qblk):
    per = qblk // B_PREV
    q = pl.BlockSpec((qblk, 512), lambda h, j: (j, h))
    g = pl.BlockSpec((qblk, 512), lambda h, j: (j, 2 + h))
    kp = pl.BlockSpec((B_PREV, 128), lambda h, j: (jnp.maximum(per * j - 1, 0), 0))
    kc = pl.BlockSpec((qblk, 128), lambda h, j: (j, 0))
    vp = pl.BlockSpec((B_PREV, 128), lambda h, j: (jnp.maximum(per * j - 1, 0), 1))
    vc = pl.BlockSpec((qblk, 128), lambda h, j: (j, 1))
    bias = pl.BlockSpec((B_GROUP, qblk + B_PREV), lambda h, j: (h, 0))
    sinks = pl.BlockSpec(memory_space=pltpu.SMEM)
    return q, g, kp, kc, vp, vc, bias, sinks


def _b_operands(kp, kc, vp, vc, kvh):
    k = jnp.concatenate([kp[...], kc[...]], axis=0)
    v = jnp.concatenate([vp[...], vc[...]], axis=0)
    kr = pltpu.roll(k, HEAD_DIM, 1)
    vr = pltpu.roll(v, HEAD_DIM, 1)
    first = kvh == 0
    return (jnp.where(first, k, kr), jnp.where(first, kr, k),
            jnp.where(first, v, vr), jnp.where(first, vr, v))


def _attn_b_fwd(qg, kv, bias, sinks):
    t = qg.shape[0]
    qblk = B_QBLK_FWD
    q_spec, g_spec, kp_spec, kc_spec, vp_spec, vc_spec, bias_spec, sink_spec = _b_specs(qblk)

    def body(q_ref, g_ref, kp, kc, vp, vc, w_ref, sink_ref, z_ref, o_ref, lse_ref, b_ref):
        kvh = pl.program_id(0)
        j = pl.program_id(1)
        _fill_bias(B_GROUP, lambda h: w_ref[h:h + 1, :], B_BAND, B_PREV, 1, b_ref, j)
        early = (j < 1).astype(jnp.int32)
        lane_lo = _lane_lo()
        k_lo, k_hi, v_lo, v_hi = _b_operands(kp, kc, vp, vc, kvh)
        n_pairs = B_GROUP // 2
        halves = []
        for hh, sel in enumerate((lane_lo, jnp.logical_not(lane_lo))):
            kk = k_lo if hh == 0 else k_hi
            vv = v_lo if hh == 0 else v_hi
            qm4 = jnp.concatenate([jnp.where(sel, q_ref[:, 128 * pp:128 * (pp + 1)], jnp.zeros((qblk, 128), BF16))
                                   for pp in range(n_pairs)], axis=0) * SCALE
            s4 = _dot_nt(qm4, kk)
            es, mxs = [], []
            for pp in range(n_pairs):
                g = 2 * pp + hh
                s = s4[pp * qblk:(pp + 1) * qblk] + b_ref[g + B_GROUP * early]
                mxs.append(jnp.maximum(jnp.max(s, axis=-1, keepdims=True), sink_ref[kvh * B_GROUP + g]))
                es.append(jnp.exp(s - mxs[pp]).astype(BF16))
            r4 = _dot(jnp.concatenate(es, axis=0), jnp.where(sel, vv, jnp.ones_like(vv)))
            outs, lses = [], []
            for pp in range(n_pairs):
                r = r4[pp * qblk:(pp + 1) * qblk]
                l = _row_sums_everywhere(r, sel) + jnp.exp(sink_ref[kvh * B_GROUP + 2 * pp + hh] - mxs[pp])
                outs.append(r / l)
                lses.append(mxs[pp] + jnp.log(l))
            halves.append((outs, lses))
        for pp in range(n_pairs):
            cols = slice(128 * pp, 128 * (pp + 1))
            o = jnp.where(lane_lo, halves[0][0][pp], halves[1][0][pp])
            silu, _ = _silu_parts(g_ref[:, cols].astype(F32))
            o_ref[:, cols] = o.astype(BF16)
            z_ref[:, cols] = (o * silu).astype(BF16)
            lse_ref[:, cols] = jnp.where(lane_lo, halves[0][1][pp], halves[1][1][pp])

    out_spec = pl.BlockSpec((qblk, 512), lambda h, j: (j, h))
    return pl.pallas_call(
        body, name="attn_b_fwd", grid=(B_KV_HEADS, t // qblk),
        in_specs=[q_spec, g_spec, kp_spec, kc_spec, vp_spec, vc_spec, bias_spec, sink_spec],
        out_specs=[out_spec, out_spec, out_spec],
        out_shape=[SDS((t, D_MODEL), BF16), SDS((t, D_MODEL), BF16), SDS((t, D_MODEL), F32)],
        scratch_shapes=[pltpu.VMEM((2 * B_GROUP, qblk, qblk + B_PREV), F32)],
        compiler_params=_cparams(),
    )(qg, qg, kv, kv, kv, kv, bias, sinks)


def _attn_b_bwd(qg, kv, bias, sinks, out_b, lse, dz, bucket_onehot):
    t = qg.shape[0]
    qblk = B_QBLK_BWD
    keys = qblk + B_PREV
    nq = t // qblk
    q_spec, g_spec, kp_spec, kc_spec, vp_spec, vc_spec, bias_spec, sink_spec = _b_specs(qblk)

    def body(q_ref, g_ref, kp, kc, vp, vc, w_ref, sink_ref, o_ref, lse_ref, dz_ref, oh_ref,
             dqg_ref, dkv_ref, dt5_ref, dsink_ref, gt_acc, b_ref):
        kvh = pl.program_id(0)
        j = pl.program_id(1)
        _fill_bias(B_GROUP, lambda h: w_ref[h:h + 1, :], B_BAND, B_PREV, 1, b_ref, j)

        @pl.when(jnp.logical_and(kvh == 0, j == 0))
        def _():
            dkv_ref[...] = jnp.zeros_like(dkv_ref)

        @pl.when(j == 0)
        def _():
            gt_acc[...] = jnp.zeros_like(gt_acc)
            dsink_ref[...] = jnp.zeros_like(dsink_ref)

        early = (j < 1).astype(jnp.int32)
        lane_lo = _lane_lo()
        k_lo, k_hi, v_lo, v_hi = _b_operands(kp, kc, vp, vc, kvh)
        dk_blk = jnp.zeros((keys, 128), F32)
        dv_blk = jnp.zeros((keys, 128), F32)
        for pp in range(B_GROUP // 2):
            cols = slice(128 * pp, 128 * (pp + 1))
            qp = q_ref[:, cols]
            o = o_ref[:, cols].astype(F32)
            lse_pair = lse_ref[:, cols]
            dzf = dz_ref[:, cols].astype(F32)
            silu, dsilu = _silu_parts(g_ref[:, cols].astype(F32))
            do = dzf * silu
            dqg_ref[1, :, cols] = (dzf * o * dsilu).astype(BF16)
            doo = do * o
            dqs = []
            for hh in range(2):
                g = 2 * pp + hh
                sel = lane_lo if hh == 0 else jnp.logical_not(lane_lo)
                sink = sink_ref[kvh * B_GROUP + g]
                kk = k_lo if hh == 0 else k_hi
                vv = v_lo if hh == 0 else v_hi
                qm, s = _head_logits(qp, kk, b_ref, g + B_GROUP * early, sel)
                lse_h = _own_everywhere(lse_pair, sel)
                p = jnp.exp(_minus_rows(s, lse_h))
                delta = jnp.sum(jnp.where(sel, doo, 0.0), axis=-1, keepdims=True)
                dom = jnp.where(sel, do, 0.0).astype(BF16)
                dp = _dot_nt(dom, vv)
                ds = p * (dp - delta)
                gt_acc[g] += ds
                dsink_ref[g:g + 1, :] -= jnp.sum(jnp.exp(sink - lse_h) * delta, axis=0, keepdims=True)
                dsb = ds.astype(BF16)
                dqs.append(_dot(dsb, kk) * SCALE)
                dk_blk = dk_blk + _dot_tn(dsb, qm)
                dv_blk = dv_blk + _dot_tn(p.astype(BF16), dom)
            dqg_ref[0, :, cols] = jnp.where(lane_lo, dqs[0], dqs[1]).astype(BF16)
        mine = lane_lo == (kvh == 0)
        dk_add = jnp.where(mine, dk_blk + pltpu.roll(dk_blk, HEAD_DIM, 1), 0.0)
        dv_add = jnp.where(mine, dv_blk + pltpu.roll(dv_blk, HEAD_DIM, 1), 0.0)

        @pl.when(j >= 1)
        def _():
            rows = pl.ds(pl.multiple_of(j * qblk - B_PREV, B_PREV), B_PREV)
            dkv_ref[0, rows, :] += dk_add[0:B_PREV]
            dkv_ref[1, rows, :] += dv_add[0:B_PREV]

        rows = pl.ds(pl.multiple_of(j * qblk, qblk), qblk)
        dkv_ref[0, rows, :] += dk_add[B_PREV:keys]
        dkv_ref[1, rows, :] += dv_add[B_PREV:keys]

        @pl.when(j == nq - 1)
        def _():
            dd = jnp.concatenate([_offset_sums(gt_acc[g]) for g in range(B_GROUP)], axis=0)
            hi = dd.astype(BF16)
            lo = (dd - hi.astype(F32)).astype(BF16)
            dt5_ref[...] = _dot(hi, oh_ref[...]) + _dot(lo, oh_ref[...])

    blk = pl.BlockSpec((qblk, 512), lambda h, j: (j, h))
    return pl.pallas_call(
        body, name="attn_b_bwd", grid=(B_KV_HEADS, nq),
        in_specs=[q_spec, g_spec, kp_spec, kc_spec, vp_spec, vc_spec, bias_spec, sink_spec, blk, blk, blk,
                  pl.BlockSpec((keys, 128), lambda h, j: (0, 0))],
        out_specs=[pl.BlockSpec((2, qblk, 512), lambda h, j: (0, j, h)),
                   pl.BlockSpec((2, t, 128), lambda h, j: (0, 0, 0)),
                   pl.BlockSpec((B_GROUP, 128), lambda h, j: (h, 0)),
                   pl.BlockSpec((B_GROUP, 128), lambda h, j: (h, 0))],
        out_shape=[SDS((2, t, D_MODEL), BF16), SDS((2, t, 128), F32),
                   SDS((N_HEADS, 128), F32), SDS((N_HEADS, 128), F32)],
        scratch_shapes=[pltpu.VMEM((B_GROUP, qblk, keys), F32), pltpu.VMEM((2 * B_GROUP, qblk, keys), F32)],
        compiler_params=_cparams(),
    )(qg, qg, kv, kv, kv, kv, bias, sinks, out_b, lse, dz, bucket_onehot)


def _a_bias_by_offset(rel_bias):
    m = np.arange(A_DIAG)
    idx = np.clip(A_BAND - 1 - m, -A_REL_CLIP, A_REL_CLIP) + A_REL_CLIP
    by_head = rel_bias[idx].T.reshape(N_HEADS // 2, 2, A_DIAG)
    return jnp.concatenate([by_head, jnp.zeros((N_HEADS // 2, 6, A_DIAG), F32)], axis=1)


def _a_bias_grad(offset_sums):
    first = 319
    tail = jnp.sum(offset_sums[:, :first], axis=1)
    body = jnp.flip(offset_sums[:, first:first + 320], axis=1)
    body = body.at[:, -1].add(tail)
    full = jnp.concatenate([jnp.zeros((N_HEADS, 193), F32), body], axis=1)
    return full.T


def _t5_bucket(rel):
    nb = T5_BUCKETS // 2
    max_exact = nb // 2
    ret = jnp.where(rel > 0, nb, 0)
    n = jnp.abs(rel)
    nf = jnp.maximum(n, 1).astype(jnp.float32)
    large = max_exact + (jnp.log(nf / max_exact) / math.log(T5_MAX_DIST / max_exact)
                         * (nb - max_exact)).astype(jnp.int32)
    large = jnp.minimum(large, nb - 1)
    return ret + jnp.where(n < max_exact, n, large)


def _b_offset_buckets(keys):
    return _t5_bucket(jnp.arange(keys, dtype=jnp.int32) - (B_LEFT_CHUNKS * CHUNK + CHUNK - 1))


def _b_bias_by_offset(t5_table, keys):
    return t5_table[_b_offset_buckets(keys)].T


def _b_bucket_onehot(keys):
    return (_b_offset_buckets(keys)[:, None] == jnp.arange(128)[None, :]).astype(BF16)


def _local_step(my_slot, order, x, target, a_gain_shard, w_in_a_shard, rel_bias, late_shards, kv_gain,
                t5_table, b_gain, sinks, f_gain):
    a_bias = _a_bias_by_offset(rel_bias)
    b_bias_fwd = _b_bias_by_offset(t5_table, B_QBLK_FWD + B_PREV)
    b_bias_bwd = _b_bias_by_offset(t5_table, B_QBLK_BWD + B_PREV)
    sinks_flat = sinks.reshape(N_HEADS)

    xn, qkvg, w_in_a, a_gain = _norm_matmul_gather(order, x, a_gain_shard, w_in_a_shard)
    z_a, out_a, lse_a, (w_in_b, w_out_a, w_out_b, kv_w) = _attn_a_fwd(qkvg, a_bias, late_shards)
    w_out_a = w_out_a.reshape(D_MODEL, D_MODEL)
    w_out_b = w_out_b.reshape(D_MODEL, D_MODEL)
    kv_w = kv_w.reshape(D_MODEL, 2 * 128)
    h1, kvn, hb, kv, qg = _layer_a_out(x, z_a, w_out_a, kv_gain, b_gain, kv_w, w_in_b)
    z_b, out_b, lse_b = _attn_b_fwd(qg, kv, b_bias_fwd, sinks_flat)
    dh2, dh2b, dz_b, loss, d_fn = _layer_b_out_loss(h1, z_b, w_out_b, f_gain, target)

    dqg_b, dkv_b, d_t5, d_sink = _attn_b_bwd(qg, kv, b_bias_bwd, sinks_flat, out_b, lse_b, dz_b,
                                             _b_bucket_onehot(B_QBLK_BWD + B_PREV))
    dh1, dh1b, dz_a, d_bn, d_kn = _layer_b_in_bwd(dqg_b, dkv_b, w_in_b, kv_w, h1, dh2, b_gain, kv_gain, w_out_a)
    early = dict(
        b_w_out=_weight_grad_rows("grad_b_w_out", my_slot, z_b, dh2b[None]),
        b_w_in=_weight_grad_cols("grad_b_w_in", my_slot, hb, [dqg_b],
                                 [(0, o, c, 4 * o + c) for o in range(2) for c in range(4)], 256),
        kv_w=_weight_grad_rows("grad_kv_w", my_slot, kvn, dkv_b),
        a_w_out=_weight_grad_rows("grad_a_w_out", my_slot, z_a, dh1b[None]))
    dqg_a, dkv_a, d_rel, landed = _attn_a_bwd(qkvg, a_bias, out_a, lse_a, dz_a, [g[0] for g in early.values()])
    g_w_in_a = _weight_grad_cols(
        "grad_a_w_in", my_slot, xn, [dqg_a, dkv_a],
        [(0, 0, 0, 0), (0, 0, 1, 1), (1, 0, 0, 2), (1, 0, 1, 3), (1, 1, 0, 4), (1, 1, 1, 5), (0, 1, 0, 6), (0, 1, 1, 7)], 512)
    from_sibling, = _exchange_sibling([g_w_in_a[0]])
    x_i, y_i, c_i, chips = _place()
    del x_i, y_i
    forward_slots = jnp.stack([_slot(*chip, c_i) for chip in chips]).astype(jnp.int32)
    chip_sums = _pre_reduce("chip_sum_a_w_in", g_w_in_a[0], from_sibling, forward_slots)
    grad_x, d_an, from_chips = _layer_a_in_bwd(dqg_a, dkv_a, w_in_a, x, dh1, a_gain, chip_sums)

    matrices = {n: (g[1], [(land, 0, N_DEV - 1)]) for (n, g), land in zip(early.items(), landed)}
    matrices["a_w_in"] = (g_w_in_a[1], [(from_sibling, 3, 1), (from_chips, 0, 3)])
    small = dict(
        loss=loss, a_norm=d_an, a_rel_bias=d_rel[:, :2].reshape(N_HEADS, A_DIAG),
        kv_norm=d_kn, t5_bias=d_t5, b_norm=d_bn, b_sinks=d_sink, final_norm=d_fn)
    return grad_x, small, matrices


def _place():
    x, y, c = lax.axis_index("x"), lax.axis_index("y"), lax.axis_index("c")
    chips = [(1 - x, y), (x, 1 - y), (1 - x, 1 - y)]
    return x, y, c, chips


def _slot(px, py, pc):
    return 4 * px + 2 * py + pc


ANY = pl.BlockSpec(memory_space=pl.ANY)


def _peer(x, y, c, k):
    return (x ^ (k >> 2), y ^ ((k >> 1) & 1), c ^ (k & 1))


def _scatter_copies(grad_refs, land_refs, send_sems, recv_sems):
    x, y, c, _ = _place()
    copies = []
    for t, (grad, land) in enumerate(zip(grad_refs, land_refs)):
        for k in range(1, N_DEV):
            peer = _peer(x, y, c, k)
            sem = (N_DEV - 1) * t + k - 1
            copies.append(pltpu.make_async_remote_copy(
                src_ref=grad.at[_slot(*peer)], dst_ref=land.at[k - 1],
                send_sem=send_sems.at[sem], recv_sem=recv_sems.at[sem],
                device_id=peer, device_id_type=MESH))
    return copies


def _gather_phases(ins, outs, send_sems, recv_sems, local_sems):
    n = len(ins)
    x, y, c, chips = _place()
    me, sibling = (x, y, c), (x, y, 1 - c)

    def copy(t, k, block, to, src=None):
        dst = outs[t].at[_slot(*block)]
        return pltpu.make_async_remote_copy(
            src_ref=dst if src is None else src, dst_ref=dst,
            send_sem=send_sems.at[7 * t + k], recv_sem=recv_sems.at[7 * t + k],
            device_id=to, device_id_type=MESH)

    def lists():
        mine = [pltpu.make_async_copy(ins[t], outs[t].at[_slot(*me)], local_sems.at[t]) for t in range(n)]
        first = []
        for t in range(n):
            first.append(copy(t, 0, me, sibling, src=ins[t]))
            first += [copy(t, 1 + j, me, (*chip, c), src=ins[t]) for j, chip in enumerate(chips)]
        passed = [copy(t, 4 + j, (*chip, c), sibling) for t in range(n) for j, chip in enumerate(chips)]
        return mine, first, passed

    def start():
        mine, first, _ = lists()
        for cp in mine + first:
            cp.start()

    def forward():
        _, _, passed = lists()
        for t in range(n):
            for j, chip in enumerate(chips):
                copy(t, 1 + j, (*chip, c), me).wait_recv()
                passed[3 * t + j].start()

    def finish():
        mine, first, passed = lists()
        for t in range(n):
            copy(t, 0, sibling, me).wait_recv()
            for j, chip in enumerate(chips):
                copy(t, 4 + j, (*chip, 1 - c), me).wait_recv()
        for cp in first + passed:
            cp.wait_send()
        for cp in mine:
            cp.wait()

    return start, forward, finish


def _gather_scratch(n):
    return [pltpu.SemaphoreType.DMA((7 * n,)), pltpu.SemaphoreType.DMA((7 * n,)), pltpu.SemaphoreType.DMA((n,))]


def _exchange_sibling(grads):
    n = len(grads)

    def body(*refs):
        ins, outs = refs[:n], refs[n:2 * n]
        send_sems, recv_sems = refs[2 * n:]
        x, y, c, chips = _place()
        sibling = (x, y, 1 - c)
        copies = []
        for t in range(n):
            blocks = [(*chip, 1 - c) for chip in chips] + [sibling]
            for k, block in enumerate(blocks):
                copies.append(pltpu.make_async_remote_copy(
                    src_ref=ins[t].at[_slot(*block)], dst_ref=outs[t].at[k],
                    send_sem=send_sems.at[4 * t + k], recv_sem=recv_sems.at[4 * t + k],
                    device_id=sibling, device_id_type=MESH))
        for cp in copies:
            cp.start()
        for cp in copies:
            cp.wait()

    return pl.pallas_call(
        body, name="grads_to_sibling",
        in_specs=[ANY] * n, out_specs=[ANY] * n,
        out_shape=[SDS((4, *g.shape[1:]), g.dtype) for g in grads],
        scratch_shapes=[pltpu.SemaphoreType.DMA((4 * n,)), pltpu.SemaphoreType.DMA((4 * n,))],
    )(*grads)


def _chip_copies(sums_ref, land_ref, send_sems, recv_sems):
    x, y, c, chips = _place()
    del x, y
    return [pltpu.make_async_remote_copy(
        src_ref=sums_ref.at[j], dst_ref=land_ref.at[j], send_sem=send_sems.at[j], recv_sem=recv_sems.at[j],
        device_id=(*chip, c), device_id_type=MESH) for j, chip in enumerate(chips)]


def _row_tile(rows):
    return min(rows, 256)


def _pre_reduce(name, g, from_sibling, slots):
    _, r, c = g.shape
    tr = _row_tile(r)

    def body(slots_ref, g_ref, s_ref, o_ref):
        del slots_ref
        o_ref[...] = (g_ref[...].astype(F32) + s_ref[...].astype(F32)).astype(BF16)

    return pl.pallas_call(
        body, name=name,
        grid_spec=pltpu.PrefetchScalarGridSpec(
            num_scalar_prefetch=1, grid=(3, r // tr),
            in_specs=[pl.BlockSpec((1, tr, c), lambda j, i, s: (s[j], i, 0)),
                      pl.BlockSpec((1, tr, c), lambda j, i, s: (j, i, 0))],
            out_specs=pl.BlockSpec((1, tr, c), lambda j, i, s: (j, i, 0))),
        out_shape=SDS((3, r, c), BF16),
        compiler_params=_cparams(),
    )(slots, g, from_sibling)


def _adamw(w, g, m, v):
    m2 = ADAM_B1 * m + (1.0 - ADAM_B1) * g
    v2 = ADAM_B2 * v + (1.0 - ADAM_B2) * jnp.square(g)
    m_hat = m2 / (1.0 - ADAM_B1 ** ADAM_STEP)
    v_hat = v2 / (1.0 - ADAM_B2 ** ADAM_STEP)
    delta = -ADAM_LR * (m_hat / (jnp.sqrt(v_hat) + ADAM_EPS) + ADAM_WD * w)
    return delta, m2, v2


def _reduce_adamw(name, own, partials, w, m, v):
    r, c = own.shape
    tr = _row_tile(r)
    n_p = len(partials)

    def body(own_ref, *rest):
        p_refs, (w_ref, m_ref, v_ref, grad_ref, d_ref, nm_ref, nv_ref) = rest[:n_p], rest[n_p:]
        grad = own_ref[...]
        for p_ref, (_, _, count) in zip(p_refs, partials):
            for j in range(count):
                grad = grad + p_ref[j].astype(F32)
        grad_ref[...] = grad
        d_ref[...], nm_ref[...], nv_ref[...] = _adamw(w_ref[...], grad, m_ref[...], v_ref[...])

    flat = pl.BlockSpec((tr, c), lambda i: (i, 0))
    return pl.pallas_call(
        body, name=name, grid=(r // tr,),
        in_specs=[flat] + [pl.BlockSpec((count, tr, c), lambda i, first=first, count=count: (first // count, i, 0))
                           for _, first, count in partials] + [flat, flat, flat],
        out_specs=[flat, flat, flat, flat],
        out_shape=[SDS((r, c), F32)] * 4,
        compiler_params=_cparams(),
    )(own, *[p[0] for p in partials], w, m, v)


VM = pl.BlockSpec()


def _small_allreduce(parts):
    n = len(parts)

    def body(*refs):
        ins, outs, lands = refs[:n], refs[n:2 * n], refs[2 * n:3 * n]
        send_sems, recv_sems = refs[3 * n:]
        x, y, c, _ = _place()
        my_slot = _slot(x, y, c)
        copies = []
        for t in range(n):
            lands[t][my_slot] = ins[t][...]
            for k in range(1, N_DEV):
                sem = (N_DEV - 1) * t + k - 1
                copies.append(pltpu.make_async_remote_copy(
                    src_ref=ins[t], dst_ref=lands[t].at[my_slot],
                    send_sem=send_sems.at[sem], recv_sem=recv_sems.at[sem],
                    device_id=_peer(x, y, c, k), device_id_type=MESH))
        for cp in copies:
            cp.start()
        for t in range(n):
            for k in range(1, N_DEV):
                sem = (N_DEV - 1) * t + k - 1
                pltpu.make_async_remote_copy(
                    src_ref=ins[t], dst_ref=lands[t].at[_slot(*_peer(x, y, c, k))],
                    send_sem=send_sems.at[sem], recv_sem=recv_sems.at[sem],
                    device_id=(x, y, c), device_id_type=MESH).wait_recv()
        for cp in copies:
            cp.wait_send()
        for t in range(n):
            total = lands[t][0]
            for s in range(1, N_DEV):
                total = total + lands[t][s]
            outs[t][...] = total

    n_sems = (N_DEV - 1) * n
    return pl.pallas_call(
        body, name="small_allreduce",
        in_specs=[VM] * n, out_specs=[VM] * n, out_shape=[SDS(p.shape, F32) for p in parts],
        scratch_shapes=[pltpu.VMEM((N_DEV, *p.shape), F32) for p in parts]
        + [pltpu.SemaphoreType.DMA((n_sems,)), pltpu.SemaphoreType.DMA((n_sems,))],
    )(*parts)


def _small_adamw(my_slot, sums, ws, ms, vs):
    n = len(ws)

    def body(slot_ref, *refs):
        sum_refs, refs = refs[:n + 1], refs[n + 1:]
        w_refs, m_refs, v_refs, refs = refs[:n], refs[n:2 * n], refs[2 * n:3 * n], refs[3 * n:]
        g_refs, d_refs, nm_refs, nv_refs = refs[:n + 1], refs[n + 1:2 * n + 1], refs[2 * n + 1:3 * n + 1], refs[3 * n + 1:]
        for t in range(n + 1):
            if t == 0:
                g = sum_refs[0][:, pl.ds(pl.multiple_of(slot_ref[0] * 128, 128), 128)]
            else:
                g = sum_refs[t][...]
            g_refs[t][...] = g
            if t < n:
                d_refs[t][...], nm_refs[t][...], nv_refs[t][...] = _adamw(w_refs[t][...], g, m_refs[t][...], v_refs[t][...])

    shapes = [SDS(w.shape, F32) for w in ws]
    outs = pl.pallas_call(
        body, name="small_adamw",
        in_specs=[pl.BlockSpec(memory_space=pltpu.SMEM)] + [VM] * (4 * n + 1),
        out_specs=[VM] * (4 * n + 1),
        out_shape=shapes + [SDS(sums[-1].shape, F32)] + shapes * 3,
    )(my_slot, *sums, *ws, *ms, *vs)
    return outs[:n + 1], outs[n + 1:2 * n + 1], outs[2 * n + 1:3 * n + 1], outs[3 * n + 1:]


def kernel(x, a_norm, a_w_in, a_rel_bias, a_w_out, kv_norm, kv_w, t5_bias, b_norm, b_w_in, b_sinks, b_w_out, final_norm, loss_target, m_a_norm, m_a_w_in, m_a_rel_bias, m_a_w_out, m_kv_norm, m_kv_w, m_t5_bias, m_b_norm, m_b_w_in, m_b_sinks, m_b_w_out, m_final_norm, v_a_norm, v_a_w_in, v_a_rel_bias, v_a_w_out, v_kv_norm, v_kv_w, v_t5_bias, v_b_norm, v_b_w_in, v_b_sinks, v_b_w_out, v_final_norm):
    xi, yi, ci = lax.axis_index("x"), lax.axis_index("y"), lax.axis_index("c")
    my_slot = _slot(xi, yi, ci)

    slot_arr = jnp.reshape(my_slot, (1,)).astype(jnp.int32)
    order = _gather_order(xi, yi, ci)
    late_shards = [b_w_in[0].astype(BF16), a_w_out[0].astype(BF16), b_w_out[0].astype(BF16), kv_w.astype(BF16)]
    grad_x, loc, matrices = _local_step(
        slot_arr, order, x[0], loss_target[0], a_norm, a_w_in[0].astype(BF16), a_rel_bias[0], late_shards,
        kv_norm.reshape(1, D_MODEL), t5_bias, b_norm, b_sinks, final_norm.reshape(1, D_MODEL))

    shard_w = dict(a_w_in=a_w_in[0], b_w_in=b_w_in[0], a_w_out=a_w_out[0], b_w_out=b_w_out[0], kv_w=kv_w)
    shard_m = dict(a_w_in=m_a_w_in[0], b_w_in=m_b_w_in[0], a_w_out=m_a_w_out[0], b_w_out=m_b_w_out[0], kv_w=m_kv_w)
    shard_v = dict(a_w_in=v_a_w_in[0], b_w_in=v_b_w_in[0], a_w_out=v_a_w_out[0], b_w_out=v_b_w_out[0], kv_w=v_kv_w)
    big = {n: _reduce_adamw("adamw_" + n, own, partials, shard_w[n], shard_m[n], shard_v[n])
           for n, (own, partials) in matrices.items()}

    names = ("a_norm", "a_rel_bias", "kv_norm", "t5_bias", "b_norm", "b_sinks", "final_norm")
    row = lambda a: a.reshape(1, -1) if a.ndim == 1 else (a[0] if a.ndim == 3 else a)
    small_w = [row(a) for a in (a_norm, a_rel_bias, kv_norm, t5_bias, b_norm, b_sinks, final_norm)]
    small_m = [row(a) for a in (m_a_norm, m_a_rel_bias, m_kv_norm, m_t5_bias, m_b_norm, m_b_sinks, m_final_norm)]
    small_v = [row(a) for a in (v_a_norm, v_a_rel_bias, v_kv_norm, v_t5_bias, v_b_norm, v_b_sinks, v_final_norm)]
    sums = dict(zip(names + ("loss",), _small_allreduce([loc[n] for n in names] + [loc["loss"]])))
    sums["a_rel_bias"] = _a_bias_grad(sums["a_rel_bias"])
    sums["t5_bias"] = sums["t5_bias"][:, :T5_BUCKETS].T
    sums["b_sinks"] = sums["b_sinks"][:, 0].reshape(1, N_HEADS)
    results = _small_adamw(slot_arr, [sums[n] for n in names + ("loss",)], small_w, small_m, small_v)
    like = dict(a_norm=a_norm, a_rel_bias=a_rel_bias, kv_norm=kv_norm, t5_bias=t5_bias, b_norm=b_norm,
                b_sinks=b_sinks, final_norm=final_norm)
    sm = [{n: part[i].reshape(like[n].shape) for i, n in enumerate(names)} for part in results]
    loss = results[0][len(names)][0, 0]

    order = ("a_norm", "a_w_in", "a_rel_bias", "a_w_out", "kv_norm", "kv_w", "t5_bias", "b_norm",
             "b_w_in", "b_sinks", "b_w_out", "final_norm")
    lead = dict(a_w_in=True, b_w_in=True, a_w_out=True, b_w_out=True, kv_w=False)

    def pick(kind, name):
        if name in big:
            val = big[name][kind]
            return val[None] if lead[name] else val
        return sm[kind][name]

    outs = [loss, grad_x[None]]
    for kind in range(4):
        outs += [pick(kind, n) for n in order]
    return tuple(outs)
```

```python
import functools
import math

import numpy as np
import jax
import jax.numpy as jnp
from jax import lax
from jax.experimental import pallas as pl
from jax.experimental.pallas import tpu as pltpu

F32 = jnp.float32
BF16 = jnp.bfloat16
SDS = jax.ShapeDtypeStruct

D_MODEL = 1024
HEAD_DIM = 64
CHUNK = 64
N_HEADS = 16
RMS_EPS = 1e-6
A_LEFT_CHUNKS = 8
A_BAND = (A_LEFT_CHUNKS + 1) * CHUNK
A_REL_CLIP = 256
B_KV_HEADS = 2
B_GROUP = 8
B_LEFT_CHUNKS = 2
B_BAND = (B_LEFT_CHUNKS + 1) * CHUNK
T5_BUCKETS = 32
T5_MAX_DIST = 128
QBLK = 256
A_KEYS = 3 * QBLK
B_QBLK_FWD = 128
B_QBLK_BWD = 256
B_PREV = 128
A_DIAG = A_KEYS
NEG = -1e30
SCALE = HEAD_DIM ** -0.5
N_DEV = 8

ADAM_LR = 0.001
ADAM_B1 = 0.9
ADAM_B2 = 0.999
ADAM_EPS = 1e-08
ADAM_WD = 0.01
ADAM_STEP = 10

VMEM_LIMIT_BYTES = 56 * 1024 * 1024
MESH = pl.DeviceIdType.MESH


def _cparams():
    return pltpu.CompilerParams(vmem_limit_bytes=VMEM_LIMIT_BYTES)


def _dot(a, b):
    return jnp.dot(a, b, preferred_element_type=F32)


def _dot_nt(a, b):
    return lax.dot_general(a, b, (((1,), (1,)), ((), ())), preferred_element_type=F32)


def _dot_tn(a, b):
    return lax.dot_general(a, b, (((0,), (0,)), ((), ())), preferred_element_type=F32)


def _rstd(xf):
    return lax.rsqrt(jnp.mean(xf * xf, axis=-1, keepdims=True) + RMS_EPS)


def _sigmoid(x):
    return 1.0 / (1.0 + jnp.exp(-x))


_GATHER_SEQUENCE = ((0, None), (1, 0), (2, 1), (4, None), (5, None), (3, 2), (6, None))


def _gather_order(x, y, c):
    others = [(1 - x, y), (x, 1 - y), (1 - x, 1 - y)]
    arrivals = [_slot(x, y, 1 - c)] + [_slot(*chip, c) for chip in others] + [_slot(*chip, 1 - c) for chip in others]
    return jnp.stack([_slot(x, y, c)] + [arrivals[a] for a, _ in _GATHER_SEQUENCE]).astype(jnp.int32)


def _norm_matmul_gather(order, x, gain_shard, w_shard):
    t = x.shape[0]
    dw, tn = w_shard.shape
    tm = min(t, 1024)
    n_m = t // tm

    def body(order_ref, x_ref, gs_ref, shard_ref, xn_ref, o_ref, full_ref, gain_ref,
             xn_all, wbuf, gland, send_sems, recv_sems, gsend_sems, grecv_sems, load_sems, own_sem):
        n, m = pl.program_id(0), pl.program_id(1)
        x_i, y_i, c_i, chips = _place()
        me, sibling = (x_i, y_i, c_i), (x_i, y_i, 1 - c_i)

        def send(k, block, to, src=None):
            dst = full_ref.at[_slot(*block)]
            return pltpu.make_async_remote_copy(
                src_ref=dst if src is None else src, dst_ref=dst,
                send_sem=send_sems.at[k], recv_sem=recv_sems.at[k], device_id=to, device_id_type=MESH)

        own = pltpu.make_async_copy(shard_ref, full_ref.at[_slot(*me)], own_sem)
        first = [send(0, me, sibling, src=shard_ref)]
        first += [send(1 + j, me, (*chip, c_i), src=shard_ref) for j, chip in enumerate(chips)]
        forwards = [send(4 + j, (*chip, c_i), sibling) for j, chip in enumerate(chips)]
        arrivals = [send(0, sibling, me)] + [send(1 + j, (*chip, c_i), me) for j, chip in enumerate(chips)]
        arrivals += [send(4 + j, (*chip, 1 - c_i), me) for j, chip in enumerate(chips)]
        gains = [pltpu.make_async_remote_copy(
            src_ref=gs_ref, dst_ref=gland.at[_slot(*me)], send_sem=gsend_sems.at[k - 1],
            recv_sem=grecv_sems.at[k - 1], device_id=_peer(x_i, y_i, c_i, k), device_id_type=MESH)
            for k in range(1, N_DEV)]

        @pl.when(jnp.logical_and(n == 0, m == 0))
        def _():
            own.start()
            for cp in gains + first:
                cp.start()
            pltpu.make_async_copy(shard_ref, wbuf.at[0], load_sems.at[0]).start()
            gland[_slot(*me)] = gs_ref[...]
            for k in range(1, N_DEV):
                pltpu.make_async_remote_copy(
                    src_ref=gs_ref, dst_ref=gland.at[_slot(*_peer(x_i, y_i, c_i, k))],
                    send_sem=gsend_sems.at[k - 1], recv_sem=grecv_sems.at[k - 1],
                    device_id=me, device_id_type=MESH).wait_recv()
            for s in range(N_DEV):
                gain_ref[:, 128 * s:128 * (s + 1)] = gland[s]

        rows = pl.ds(pl.multiple_of(m * tm, tm), tm)

        @pl.when(n == 0)
        def _():
            xf = x_ref[...]
            xn = ((xf * _rstd(xf)) * gain_ref[...]).astype(BF16)
            xn_all[rows, :] = xn
            xn_ref[...] = xn

        @pl.when(m == 0)
        def _():
            pltpu.make_async_copy(full_ref.at[0], wbuf.at[n % 2], load_sems.at[n % 2]).wait()

        o_ref[...] = _dot(xn_all[rows, :], wbuf[n % 2]).astype(BF16)

        for k, (arrival, forward) in enumerate(_GATHER_SEQUENCE):
            @pl.when(jnp.logical_and(n == k, m == n_m - 1))
            def _(k=k, arrival=arrival, forward=forward):
                arrivals[arrival].wait_recv()
                if forward is not None:
                    forwards[forward].start()
                pltpu.make_async_copy(full_ref.at[order_ref[k + 1]], wbuf.at[(k + 1) % 2],
                                      load_sems.at[(k + 1) % 2]).start()

        @pl.when(jnp.logical_and(n == N_DEV - 1, m == n_m - 1))
        def _():
            for cp in gains + first + forwards:
                cp.wait_send()
            own.wait()

    held = lambda n, m, order: (jnp.where(n == 0, m, n_m - 1), 0)
    return pl.pallas_call(
        body, name="norm_matmul_gather",
        grid_spec=pltpu.PrefetchScalarGridSpec(
            num_scalar_prefetch=1, grid=(N_DEV, n_m),
            in_specs=[pl.BlockSpec((tm, D_MODEL), held),
                      pl.BlockSpec((1, 128), lambda n, m, order: (0, 0)), ANY],
            out_specs=[pl.BlockSpec((tm, D_MODEL), held),
                       pl.BlockSpec((tm, tn), lambda n, m, order: (m, order[n])),
                       ANY, pl.BlockSpec((1, D_MODEL), lambda n, m, order: (0, 0))],
            scratch_shapes=[pltpu.VMEM((t, D_MODEL), BF16), pltpu.VMEM((2, dw, tn), BF16),
                            pltpu.VMEM((N_DEV, 1, 128), F32),
                            pltpu.SemaphoreType.DMA((7,)), pltpu.SemaphoreType.DMA((7,)),
                            pltpu.SemaphoreType.DMA((7,)), pltpu.SemaphoreType.DMA((7,)),
                            pltpu.SemaphoreType.DMA((2,)), pltpu.SemaphoreType.DMA]),
        out_shape=[SDS((t, D_MODEL), BF16), SDS((t, N_DEV * tn), BF16), SDS((N_DEV, dw, tn), BF16),
                   SDS((1, D_MODEL), F32)],
        compiler_params=_cparams(),
    )(order, x, gain_shard, w_shard)


def _layer_a_out(x, z, w_out, kv_gain, b_gain, kv_w, w_in_b):
    t = x.shape[0]
    tm = min(t, 512)
    nb, _, tn = w_in_b.shape

    def body(x_ref, z_ref, wo_ref, kvg_ref, bg_ref, kvw_ref, wb_ref,
             h1_ref, kvn_ref, hb_ref, kv_ref, qg_ref):
        h1 = x_ref[...] + _dot(z_ref[...], wo_ref[...])
        h1_ref[...] = h1
        y0 = h1 * _rstd(h1)
        kvn = (y0 * kvg_ref[...]).astype(BF16)
        hb = (y0 * bg_ref[...]).astype(BF16)
        kvn_ref[...] = kvn
        hb_ref[...] = hb
        kv_ref[...] = _dot(kvn, kvw_ref[...]).astype(BF16)
        for i in range(nb):
            qg_ref[:, i * tn:(i + 1) * tn] = _dot(hb, wb_ref[i]).astype(BF16)

    row = lambda m: (m, 0)
    fix2 = lambda m: (0, 0)
    return pl.pallas_call(
        body, name="layer_a_out", grid=(t // tm,),
        in_specs=[pl.BlockSpec((tm, D_MODEL), row), pl.BlockSpec((tm, D_MODEL), row),
                  pl.BlockSpec((D_MODEL, D_MODEL), fix2),
                  pl.BlockSpec((1, D_MODEL), fix2), pl.BlockSpec((1, D_MODEL), fix2),
                  pl.BlockSpec((D_MODEL, 256), fix2),
                  pl.BlockSpec((nb, D_MODEL, tn), lambda m: (0, 0, 0))],
        out_specs=[pl.BlockSpec((tm, D_MODEL), row), pl.BlockSpec((tm, D_MODEL), row),
                   pl.BlockSpec((tm, D_MODEL), row), pl.BlockSpec((tm, 256), row),
                   pl.BlockSpec((tm, nb * tn), row)],
        out_shape=[SDS((t, D_MODEL), F32), SDS((t, D_MODEL), BF16), SDS((t, D_MODEL), BF16),
                   SDS((t, 256), BF16), SDS((t, nb * tn), BF16)],
        compiler_params=_cparams(),
    )(x, z, w_out, kv_gain, b_gain, kv_w, w_in_b)


def _layer_b_out_loss(h1, z, w_out, f_gain, target):
    t = h1.shape[0]
    tm = min(t, 512)

    def body(h1_ref, z_ref, wo_ref, fg_ref, tgt_ref,
             dh2_ref, dh2b_ref, dz_ref, loss_ref, dfn_ref):
        @pl.when(pl.program_id(0) == 0)
        def _():
            loss_ref[...] = jnp.zeros_like(loss_ref)
            dfn_ref[...] = jnp.zeros_like(dfn_ref)

        h2 = h1_ref[...] + _dot(z_ref[...], wo_ref[...])
        r = _rstd(h2)
        yn = h2 * r
        fg = fg_ref[...]
        err = yn * fg - tgt_ref[...]
        loss_ref[...] += (0.5 / D_MODEL) * jnp.sum(err * err)
        dy = err * (1.0 / D_MODEL)
        dfn_ref[...] += jnp.sum(dy * yn, axis=0, keepdims=True)
        u = dy * fg
        dh2 = r * u - h2 * ((r * r * r) * jnp.mean(u * h2, axis=-1, keepdims=True))
        dh2_ref[...] = dh2
        dh2b = dh2.astype(BF16)
        dh2b_ref[...] = dh2b
        dz_ref[...] = _dot_nt(dh2b, wo_ref[...]).astype(BF16)

    row = lambda m: (m, 0)
    fix2 = lambda m: (0, 0)
    return pl.pallas_call(
        body, name="layer_b_out_loss", grid=(t // tm,),
        in_specs=[pl.BlockSpec((tm, D_MODEL), row), pl.BlockSpec((tm, D_MODEL), row),
                  pl.BlockSpec((D_MODEL, D_MODEL), fix2), pl.BlockSpec((1, D_MODEL), fix2),
                  pl.BlockSpec((tm, D_MODEL), row)],
        out_specs=[pl.BlockSpec((tm, D_MODEL), row), pl.BlockSpec((tm, D_MODEL), row),
                   pl.BlockSpec((tm, D_MODEL), row), pl.BlockSpec((1, 128), fix2),
                   pl.BlockSpec((1, D_MODEL), fix2)],
        out_shape=[SDS((t, D_MODEL), F32), SDS((t, D_MODEL), BF16), SDS((t, D_MODEL), BF16),
                   SDS((1, 128), F32), SDS((1, D_MODEL), F32)],
        compiler_params=_cparams(),
    )(h1, z, w_out, f_gain, target)


def _layer_b_in_bwd(dqg, dkv, w_in_b, kv_w, h1, dh2, b_gain, kv_gain, w_out_a):
    t = h1.shape[0]
    tm = min(t, 512)
    nb, _, tn = w_in_b.shape
    per = D_MODEL // tn

    def body(dqg_ref, dkv_ref, wb_ref, kvw_ref, h1_ref, dh2_ref, bg_ref, kvg_ref, wo_ref,
             dh1_ref, dh1b_ref, dz_ref, dbn_ref, dkn_ref):
        @pl.when(pl.program_id(0) == 0)
        def _():
            dbn_ref[...] = jnp.zeros_like(dbn_ref)
            dkn_ref[...] = jnp.zeros_like(dkn_ref)

        dhb = jnp.zeros((tm, D_MODEL), F32)
        for i in range(nb):
            blk = dqg_ref[i // per, :, (i % per) * tn:(i % per + 1) * tn]
            dhb = dhb + _dot_nt(blk, wb_ref[i])
        dkn = (_dot_nt(dkv_ref[0].astype(BF16), kvw_ref[:, 0:128])
               + _dot_nt(dkv_ref[1].astype(BF16), kvw_ref[:, 128:256]))
        h1 = h1_ref[...]
        r = _rstd(h1)
        xr = h1 * r
        dbn_ref[...] += jnp.sum(dhb * xr, axis=0, keepdims=True)
        dkn_ref[...] += jnp.sum(dkn * xr, axis=0, keepdims=True)
        u = dhb * bg_ref[...] + dkn * kvg_ref[...]
        dh1 = dh2_ref[...] + r * u - h1 * ((r * r * r) * jnp.mean(u * h1, axis=-1, keepdims=True))
        dh1_ref[...] = dh1
        dh1b = dh1.astype(BF16)
        dh1b_ref[...] = dh1b
        dz_ref[...] = _dot_nt(dh1b, wo_ref[...]).astype(BF16)

    row = lambda m: (m, 0)
    fix2 = lambda m: (0, 0)
    return pl.pallas_call(
        body, name="layer_b_in_bwd", grid=(t // tm,),
        in_specs=[pl.BlockSpec((2, tm, D_MODEL), lambda m: (0, m, 0)),
                  pl.BlockSpec((2, tm, 128), lambda m: (0, m, 0)),
                  pl.BlockSpec((nb, D_MODEL, tn), lambda m: (0, 0, 0)),
                  pl.BlockSpec((D_MODEL, 256), fix2),
                  pl.BlockSpec((tm, D_MODEL), row), pl.BlockSpec((tm, D_MODEL), row),
                  pl.BlockSpec((1, D_MODEL), fix2), pl.BlockSpec((1, D_MODEL), fix2),
                  pl.BlockSpec((D_MODEL, D_MODEL), fix2)],
        out_specs=[pl.BlockSpec((tm, D_MODEL), row), pl.BlockSpec((tm, D_MODEL), row),
                   pl.BlockSpec((tm, D_MODEL), row), pl.BlockSpec((1, D_MODEL), fix2),
                   pl.BlockSpec((1, D_MODEL), fix2)],
        out_shape=[SDS((t, D_MODEL), F32), SDS((t, D_MODEL), BF16), SDS((t, D_MODEL), BF16),
                   SDS((1, D_MODEL), F32), SDS((1, D_MODEL), F32)],
        compiler_params=_cparams(),
    )(dqg, dkv, w_in_b, kv_w, h1, dh2, b_gain, kv_gain, w_out_a)


def _layer_a_in_bwd(dqg, dkv, w_in_a, x, dh1, a_gain, chip_sums):
    t = x.shape[0]
    tm = min(t, 512)
    nb, _, tn = w_in_a.shape
    per = D_MODEL // tn

    def body(dqg_ref, dkv_ref, w_ref, x_ref, dh1_ref, ag_ref, sums_ref, dx_ref, dan_ref, land_ref,
             send_sems, recv_sems):
        @pl.when(pl.program_id(0) == 0)
        def _():
            dan_ref[...] = jnp.zeros_like(dan_ref)
            for cp in _chip_copies(sums_ref, land_ref, send_sems, recv_sems):
                cp.start()

        dxn = jnp.zeros((tm, D_MODEL), F32)
        for i in range(nb):
            part = i // per
            src = dqg_ref if part in (0, 3) else dkv_ref
            outer = {0: 0, 3: 1, 1: 0, 2: 1}[part]
            blk = src[outer, :, (i % per) * tn:(i % per + 1) * tn]
            dxn = dxn + _dot_nt(blk, w_ref[i])
        xf = x_ref[...]
        r = _rstd(xf)
        dan_ref[...] += jnp.sum(dxn * (xf * r), axis=0, keepdims=True)
        u = dxn * ag_ref[...]
        dx_ref[...] = dh1_ref[...] + r * u - xf * ((r * r * r) * jnp.mean(u * xf, axis=-1, keepdims=True))

        @pl.when(pl.program_id(0) == t // tm - 1)
        def _():
            for cp in _chip_copies(sums_ref, land_ref, send_sems, recv_sems):
                cp.wait()

    row = lambda m: (m, 0)
    fix2 = lambda m: (0, 0)
    return pl.pallas_call(
        body, name="layer_a_in_bwd", grid=(t // tm,),
        in_specs=[pl.BlockSpec((2, tm, D_MODEL), lambda m: (0, m, 0)),
                  pl.BlockSpec((2, tm, D_MODEL), lambda m: (0, m, 0)),
                  pl.BlockSpec((nb, D_MODEL, tn), lambda m: (0, 0, 0)),
                  pl.BlockSpec((tm, D_MODEL), row), pl.BlockSpec((tm, D_MODEL), row),
                  pl.BlockSpec((1, D_MODEL), fix2), ANY],
        out_specs=[pl.BlockSpec((tm, D_MODEL), row), pl.BlockSpec((1, D_MODEL), fix2), ANY],
        out_shape=[SDS((t, D_MODEL), F32), SDS((1, D_MODEL), F32), SDS(chip_sums.shape, chip_sums.dtype)],
        scratch_shapes=[pltpu.SemaphoreType.DMA((3,)), pltpu.SemaphoreType.DMA((3,))],
        compiler_params=_cparams(),
    )(dqg, dkv, w_in_a, x, dh1, a_gain, chip_sums)


def _lut(s, vals):
    r = jnp.int32(vals[0])
    for i in range(1, len(vals)):
        r = jnp.where(s == i, jnp.int32(vals[i]), r)
    return r


def _held(steps, i):
    seq, cur = [None] * len(steps), None
    for k in range(len(steps) - 1, -1, -1):
        if steps[k][0] == i:
            cur = steps[k][1:3]
        seq[k] = cur
    for k in range(len(steps)):
        cur = seq[k] = seq[k] if seq[k] is not None else cur
    return seq


def _weight_grad_cols(name, my_slot, a, bs, steps, tn):
    t, dw = a.shape
    n_arr = len(bs)
    which = [s[0] for s in steps]
    blks = [s[3] for s in steps]

    def body(slot_ref, a_ref, *rest):
        b_refs, (o_ref, own_ref, at_ref) = rest[:n_arr], rest[n_arr:]
        s = pl.program_id(0)

        @pl.when(s == 0)
        def _():
            at_ref[...] = a_ref[...].T

        for i in range(n_arr):
            @pl.when(_lut(s, which) == i)
            def _(i=i):
                res = _dot(at_ref[...], b_refs[i][0])
                o_ref[0] = res.astype(BF16)

                @pl.when(_lut(s, blks) == slot_ref[0])
                def _():
                    own_ref[...] = res

    def b_spec(i):
        held = _held(steps, i)
        return pl.BlockSpec((1, t, tn), lambda s, slot: (_lut(s, [h[0] for h in held]), 0,
                                                         _lut(s, [h[1] for h in held])))

    return pl.pallas_call(
        body, name=name,
        grid_spec=pltpu.PrefetchScalarGridSpec(
            num_scalar_prefetch=1, grid=(len(steps),),
            in_specs=[pl.BlockSpec((t, dw), lambda s, slot: (0, 0))] + [b_spec(i) for i in range(n_arr)],
            out_specs=[pl.BlockSpec((1, dw, tn), lambda s, slot: (_lut(s, blks), 0, 0)),
                       pl.BlockSpec((dw, tn), lambda s, slot: (0, 0))],
            scratch_shapes=[pltpu.VMEM((dw, t), BF16)]),
        out_shape=[SDS((N_DEV, dw, tn), BF16), SDS((dw, tn), F32)],
        compiler_params=_cparams(),
    )(my_slot, a, *bs)


def _weight_grad_rows(name, my_slot, a, b):
    t, dw = a.shape
    n_o, _, c = b.shape
    rows = dw // N_DEV
    tn = min(c, 256)
    per = c // tn

    def body(slot_ref, a_ref, b_ref, o_ref, own_ref, at_ref, res_ref):
        @pl.when(pl.program_id(0) == 0)
        def _():
            at_ref[...] = a_ref[...].T

        res_ref[...] = _dot(at_ref[...], b_ref[0].astype(BF16))
        o_ref[...] = res_ref[...].astype(BF16)
        own_ref[...] = res_ref[pl.ds(pl.multiple_of(slot_ref[0] * rows, rows), rows), :]

    all_rows, own = pl.pallas_call(
        body, name=name,
        grid_spec=pltpu.PrefetchScalarGridSpec(
            num_scalar_prefetch=1, grid=(n_o * per,),
            in_specs=[pl.BlockSpec((t, dw), lambda s, slot: (0, 0)),
                      pl.BlockSpec((1, t, tn), lambda s, slot: (s // per, 0, s % per))],
            out_specs=[pl.BlockSpec((dw, tn), lambda s, slot: (0, s)),
                       pl.BlockSpec((rows, tn), lambda s, slot: (0, s))],
            scratch_shapes=[pltpu.VMEM((dw, t), BF16), pltpu.VMEM((dw, tn), F32)]),
        out_shape=[SDS((dw, n_o * c), BF16), SDS((rows, n_o * c), F32)],
        compiler_params=_cparams(),
    )(my_slot, a, b)
    return all_rows.reshape(N_DEV, rows, n_o * c), own


def _lane_lo():
    return lax.broadcasted_iota(jnp.int32, (1, 128), 1) < HEAD_DIM


def _offset_sums(gt):
    keys = gt.shape[1]
    gc = gt[0:CHUNK]
    for cc in range(1, gt.shape[0] // CHUNK):
        gc = gc + pltpu.roll(gt[cc * CHUNK:(cc + 1) * CHUNK], keys - cc * CHUNK, 1)
    hi = gc.astype(BF16)
    lo = (gc - hi.astype(F32)).astype(BF16)
    flip = (lax.broadcasted_iota(jnp.int32, (CHUNK, CHUNK), 0)
            + lax.broadcasted_iota(jnp.int32, (CHUNK, CHUNK), 1) == CHUNK - 1).astype(BF16)
    gf = _dot(flip, hi) + _dot(flip, lo)
    skew = pltpu.roll(gf, 0, 1, stride=1, stride_axis=0)
    return jnp.sum(skew, axis=0, keepdims=True)


def _band_bias(w_row, band, rows):
    keys = w_row.shape[1]
    base = jnp.broadcast_to(w_row, (CHUNK, keys))
    skew = pltpu.roll(base, 0, 1, stride=1, stride_axis=0)
    skew = pltpu.roll(skew, keys - (CHUNK - 1), 1)
    col = lax.broadcasted_iota(jnp.int32, (CHUNK, keys), 1)
    chunk0 = jnp.where(col < band, skew, NEG)
    return jnp.concatenate(
        [chunk0] + [pltpu.roll(chunk0, cc * CHUNK, 1) for cc in range(1, rows // CHUNK)], axis=0)


def _silu_parts(g):
    sg = _sigmoid(g)
    return g * sg, sg * (1.0 + g * (1.0 - sg))


A_PAIRS = 2
A_ROWS = QBLK
A_LANES = 128 * A_PAIRS
A_STEPS = D_MODEL // A_LANES


def _a_specs():
    q = pl.BlockSpec((QBLK, A_LANES), lambda p, j: (j, p))
    ks = [pl.BlockSpec((QBLK, A_LANES), lambda p, j, b=b: (jnp.maximum(j - 2 + b, 0), A_STEPS + p)) for b in range(3)]
    vs = [pl.BlockSpec((QBLK, A_LANES), lambda p, j, b=b: (jnp.maximum(j - 2 + b, 0), 2 * A_STEPS + p))
          for b in range(3)]
    g = pl.BlockSpec((QBLK, A_LANES), lambda p, j: (j, 3 * A_STEPS + p))
    bias = pl.BlockSpec((A_PAIRS, 8, A_KEYS), lambda p, j: (p, 0, 0))
    return q, ks, vs, g, bias


def _a_fill_bias(w_ref, b_ref, j):
    _fill_bias(2 * A_PAIRS, lambda h: w_ref[h // 2, h % 2:h % 2 + 1, :], A_BAND, QBLK * (2 - j), 2, b_ref, j)


def _fill_bias(n, get_row, band, first_valid_col, early, bias_scr, j):
    @pl.when(j == 0)
    def _():
        for h in range(n):
            bias_scr[h] = _band_bias(get_row(h), band, bias_scr.shape[1])

    @pl.when(j < early)
    def _():
        keys = bias_scr.shape[2]
        col_ok = lax.broadcasted_iota(jnp.int32, (1, keys), 1) >= first_valid_col
        for h in range(n):
            bias_scr[n + h] = jnp.where(col_ok, bias_scr[h], NEG)


def _head_logits(q, k, bias_scr, idx, sel):
    qm = jnp.where(sel, q, jnp.zeros_like(q)) * SCALE
    return qm, _dot_nt(qm, k) + bias_scr[idx]


def _row_sums_everywhere(r, sel):
    return jnp.where(sel, pltpu.roll(r, HEAD_DIM, 1), r)


def _own_everywhere(x, sel):
    return jnp.where(sel, x, pltpu.roll(x, HEAD_DIM, 1))


def _minus_rows(s, row_full):
    return jnp.concatenate([s[:, i:i + 128] - row_full for i in range(0, s.shape[1], 128)], axis=1)


def _attn_a_fwd(qkvg, bias, gather):
    t = qkvg.shape[0]
    nq = t // QBLK
    n_g = len(gather)
    q_spec, k_specs, v_specs, g_spec, bias_spec = _a_specs()

    def body(q_ref, k0, k1, k2, v0, v1, v2, g_ref, w_ref, *rest):
        shard_refs, rest = rest[:n_g], rest[n_g:]
        z_ref, o_ref, lse_ref = rest[:3]
        full_refs, (b_ref, *comm) = rest[3:3 + n_g], rest[3 + n_g:]
        p = pl.program_id(0)
        j = pl.program_id(1)
        start, forward, finish = _gather_phases(shard_refs, full_refs, *comm)
        pl.when(jnp.logical_and(p == 0, j == 0))(start)
        pl.when(jnp.logical_and(p == A_STEPS // 2, j == 0))(forward)
        _a_fill_bias(w_ref, b_ref, j)
        early = (j < 2).astype(jnp.int32)
        lane_lo = _lane_lo()
        sels = (lane_lo, jnp.logical_not(lane_lo))
        for pp in range(A_PAIRS):
            cols = slice(128 * pp, 128 * (pp + 1))
            k = jnp.concatenate([k0[:, cols], k1[:, cols], k2[:, cols]], axis=0)
            v = jnp.concatenate([v0[:, cols], v1[:, cols], v2[:, cols]], axis=0)
            v1s = [jnp.where(sel, v, jnp.ones_like(v)) for sel in sels]
            for rb in range(QBLK // A_ROWS):
                rows = slice(rb * A_ROWS, (rb + 1) * A_ROWS)
                q = q_ref[rows, cols]
                outs, lses = [], []
                qm2 = jnp.concatenate([jnp.where(sel, q, jnp.zeros_like(q)) for sel in sels], axis=0) * SCALE
                s2 = _dot_nt(qm2, k)
                for hh, sel in enumerate(sels):
                    s = s2[hh * A_ROWS:(hh + 1) * A_ROWS] + b_ref[2 * pp + hh + 2 * A_PAIRS * early, rows, :]
                    mx = jnp.max(s, axis=-1, keepdims=True)
                    e = jnp.exp(s - mx).astype(BF16)
                    r = _dot(e, v1s[hh])
                    l = _row_sums_everywhere(r, sel)
                    outs.append(r / l)
                    lses.append(mx + jnp.log(l))
                o = jnp.where(lane_lo, outs[0], outs[1])
                silu, _ = _silu_parts(g_ref[rows, cols].astype(F32))
                o_ref[rows, cols] = o.astype(BF16)
                z_ref[rows, cols] = (o * silu).astype(BF16)
                lse_ref[rows, cols] = jnp.where(lane_lo, lses[0], lses[1])
        pl.when(jnp.logical_and(p == A_STEPS - 1, j == nq - 1))(finish)

    out_spec = pl.BlockSpec((QBLK, A_LANES), lambda p, j: (j, p))
    outs = pl.pallas_call(
        body, name="attn_a_fwd", grid=(A_STEPS, nq),
        in_specs=[q_spec, *k_specs, *v_specs, g_spec, bias_spec] + [ANY] * n_g,
        out_specs=[out_spec, out_spec, out_spec] + [ANY] * n_g,
        out_shape=[SDS((t, D_MODEL), BF16), SDS((t, D_MODEL), BF16), SDS((t, D_MODEL), F32)]
        + [SDS((N_DEV, *s.shape), s.dtype) for s in gather],
        scratch_shapes=[pltpu.VMEM((4 * A_PAIRS, QBLK, A_KEYS), F32)] + _gather_scratch(n_g),
        compiler_params=_cparams(),
    )(qkvg, qkvg, qkvg, qkvg, qkvg, qkvg, qkvg, qkvg, bias, *gather)
    return outs[0], outs[1], outs[2], list(outs[3:])


def _attn_a_bwd(qkvg, bias, out_a, lse, dz, scatter):
    t = qkvg.shape[0]
    nq = t // QBLK
    n_sc = len(scatter)
    q_spec, k_specs, v_specs, g_spec, bias_spec = _a_specs()

    def body(q_ref, k0, k1, k2, v0, v1, v2, g_ref, w_ref, o_ref, lse_ref, dz_ref, *rest):
        sc_refs, rest = rest[:n_sc], rest[n_sc:]
        dqg_ref, dkv_ref, dg_ref = rest[:3]
        land_refs, rest = rest[3:3 + n_sc], rest[3 + n_sc:]
        dk_acc, dv_acc, gt_acc, b_ref, send_sems, recv_sems = rest
        j = pl.program_id(1)
        first = jnp.logical_and(pl.program_id(0) == 0, j == 0)
        last = jnp.logical_and(pl.program_id(0) == A_STEPS - 1, j == nq - 1)

        @pl.when(first)
        def _():
            for cp in _scatter_copies(sc_refs, land_refs, send_sems, recv_sems):
                cp.start()

        _a_fill_bias(w_ref, b_ref, j)

        @pl.when(j == 0)
        def _():
            dk_acc[...] = jnp.zeros_like(dk_acc)
            dv_acc[...] = jnp.zeros_like(dv_acc)
            gt_acc[...] = jnp.zeros_like(gt_acc)

        early = (j < 2).astype(jnp.int32)
        lane_lo = _lane_lo()
        for pp in range(A_PAIRS):
            cols = slice(128 * pp, 128 * (pp + 1))
            q = q_ref[:, cols]
            k = jnp.concatenate([k0[:, cols], k1[:, cols], k2[:, cols]], axis=0)
            v = jnp.concatenate([v0[:, cols], v1[:, cols], v2[:, cols]], axis=0)
            o = o_ref[:, cols].astype(F32)
            lse_pair = lse_ref[:, cols]
            dzf = dz_ref[:, cols].astype(F32)
            silu, dsilu = _silu_parts(g_ref[:, cols].astype(F32))
            do = dzf * silu
            dqg_ref[1, :, cols] = (dzf * o * dsilu).astype(BF16)
            doo = do * o
            sels = (lane_lo, jnp.logical_not(lane_lo))
            qm2 = jnp.concatenate([jnp.where(sel, q, jnp.zeros_like(q)) for sel in sels], axis=0) * SCALE
            dom2 = jnp.concatenate([jnp.where(sel, do, 0.0) for sel in sels], axis=0).astype(BF16)
            s2 = _dot_nt(qm2, k)
            dp2 = _dot_nt(dom2, v)
            ps, dss = [], []
            for hh, sel in enumerate(sels):
                rows = slice(hh * QBLK, (hh + 1) * QBLK)
                s = s2[rows] + b_ref[2 * pp + hh + 2 * A_PAIRS * early]
                p = jnp.exp(_minus_rows(s, _own_everywhere(lse_pair, sel)))
                delta = jnp.sum(jnp.where(sel, doo, 0.0), axis=-1, keepdims=True)
                ds = p * (dp2[rows] - delta)
                gt_acc[2 * pp + hh] += ds
                ps.append(p.astype(BF16))
                dss.append(ds.astype(BF16))
            dsb2 = jnp.concatenate(dss, axis=0)
            dq2 = _dot(dsb2, k) * SCALE
            dk_blk = _dot_tn(dsb2, qm2)
            dv_blk = _dot_tn(jnp.concatenate(ps, axis=0), dom2)
            dqg_ref[0, :, cols] = jnp.where(lane_lo, dq2[0:QBLK], dq2[QBLK:2 * QBLK]).astype(BF16)
            for b in range(3):
                @pl.when(j - 2 + b >= 0)
                def _(b=b, cols=cols, dk_blk=dk_blk, dv_blk=dv_blk):
                    rows = pl.ds(pl.multiple_of((j - 2 + b) * QBLK, QBLK), QBLK)
                    dk_acc[rows, cols] += dk_blk[b * QBLK:(b + 1) * QBLK]
                    dv_acc[rows, cols] += dv_blk[b * QBLK:(b + 1) * QBLK]

        @pl.when(j == nq - 1)
        def _():
            dkv_ref[0] = dk_acc[...].astype(BF16)
            dkv_ref[1] = dv_acc[...].astype(BF16)
            for pp in range(A_PAIRS):
                dg_ref[pp] = jnp.concatenate([_offset_sums(gt_acc[2 * pp]), _offset_sums(gt_acc[2 * pp + 1]),
                                              jnp.zeros((6, A_DIAG), F32)], axis=0)

        @pl.when(last)
        def _():
            for cp in _scatter_copies(sc_refs, land_refs, send_sems, recv_sems):
                cp.wait()

    blk = pl.BlockSpec((QBLK, A_LANES), lambda p, j: (j, p))
    outs = pl.pallas_call(
        body, name="attn_a_bwd", grid=(A_STEPS, nq),
        in_specs=[q_spec, *k_specs, *v_specs, g_spec, bias_spec, blk, blk, blk] + [ANY] * n_sc,
        out_specs=[pl.BlockSpec((2, QBLK, A_LANES), lambda p, j: (0, j, p)),
                   pl.BlockSpec((2, t, A_LANES), lambda p, j: (0, 0, p)),
                   pl.BlockSpec((A_PAIRS, 8, A_DIAG), lambda p, j: (p, 0, 0))] + [ANY] * n_sc,
        out_shape=[SDS((2, t, D_MODEL), BF16), SDS((2, t, D_MODEL), BF16), SDS((N_HEADS // 2, 8, A_DIAG), F32)]
        + [SDS((N_DEV - 1, *g.shape[1:]), g.dtype) for g in scatter],
        scratch_shapes=[pltpu.VMEM((t, A_LANES), F32), pltpu.VMEM((t, A_LANES), F32),
                        pltpu.VMEM((2 * A_PAIRS, QBLK, A_KEYS), F32), pltpu.VMEM((4 * A_PAIRS, QBLK, A_KEYS), F32),
                        pltpu.SemaphoreType.DMA(((N_DEV - 1) * n_sc,)),
                        pltpu.SemaphoreType.DMA(((N_DEV - 1) * n_sc,))],
        compiler_params=_cparams(),
    )(qkvg, qkvg, qkvg, qkvg, qkvg, qkvg, qkvg, qkvg, bias, out_a, lse, dz, *scatter)
    return outs[0], outs[1], outs[2], list(outs[3:])


def _b_specs(qblk):
    per = qblk // B_PREV
    q = pl.BlockSpec((qblk, 512), lambda h, j: (j, h))
    g = pl.BlockSpec((qblk, 512), lambda h, j: (j, 2 + h))
    kp = pl.BlockSpec((B_PREV, 128), lambda h, j: (jnp.maximum(per * j - 1, 0), 0))
    kc = pl.BlockSpec((qblk, 128), lambda h, j: (j, 0))
    vp = pl.BlockSpec((B_PREV, 128), lambda h, j: (jnp.maximum(per * j - 1, 0), 1))
    vc = pl.BlockSpec((qblk, 128), lambda h, j: (j, 1))
    bias = pl.BlockSpec((B_GROUP, qblk + B_PREV), lambda h, j: (h, 0))
    sinks = pl.BlockSpec(memory_space=pltpu.SMEM)
    return q, g, kp, kc, vp, vc, bias, sinks


def _b_operands(kp, kc, vp, vc, kvh):
    k = jnp.concatenate([kp[...], kc[...]], axis=0)
    v = jnp.concatenate([vp[...], vc[...]], axis=0)
    kr = pltpu.roll(k, HEAD_DIM, 1)
    vr = pltpu.roll(v, HEAD_DIM, 1)
    first = kvh == 0
    return (jnp.where(first, k, kr), jnp.where(first, kr, k),
            jnp.where(first, v, vr), jnp.where(first, vr, v))


def _attn_b_fwd(qg, kv, bias, sinks):
    t = qg.shape[0]
    qblk = B_QBLK_FWD
    q_spec, g_spec, kp_spec, kc_spec, vp_spec, vc_spec, bias_spec, sink_spec = _b_specs(qblk)

    def body(q_ref, g_ref, kp, kc, vp, vc, w_ref, sink_ref, z_ref, o_ref, lse_ref, b_ref):
        kvh = pl.program_id(0)
        j = pl.program_id(1)
        _fill_bias(B_GROUP, lambda h: w_ref[h:h + 1, :], B_BAND, B_PREV, 1, b_ref, j)
        early = (j < 1).astype(jnp.int32)
        lane_lo = _lane_lo()
        k_lo, k_hi, v_lo, v_hi = _b_operands(kp, kc, vp, vc, kvh)
        n_pairs = B_GROUP // 2
        halves = []
        for hh, sel in enumerate((lane_lo, jnp.logical_not(lane_lo))):
            kk = k_lo if hh == 0 else k_hi
            vv = v_lo if hh == 0 else v_hi
            qm4 = jnp.concatenate([jnp.where(sel, q_ref[:, 128 * pp:128 * (pp + 1)], jnp.zeros((qblk, 128), BF16))
                                   for pp in range(n_pairs)], axis=0) * SCALE
            s4 = _dot_nt(qm4, kk)
            es, mxs = [], []
            for pp in range(n_pairs):
                g = 2 * pp + hh
                s = s4[pp * qblk:(pp + 1) * qblk] + b_ref[g + B_GROUP * early]
                mxs.append(jnp.maximum(jnp.max(s, axis=-1, keepdims=True), sink_ref[kvh * B_GROUP + g]))
                es.append(jnp.exp(s - mxs[pp]).astype(BF16))
            r4 = _dot(jnp.concatenate(es, axis=0), jnp.where(sel, vv, jnp.ones_like(vv)))
            outs, lses = [], []
            for pp in range(n_pairs):
                r = r4[pp * qblk:(pp + 1) * qblk]
                l = _row_sums_everywhere(r, sel) + jnp.exp(sink_ref[kvh * B_GROUP + 2 * pp + hh] - mxs[pp])
                outs.append(r / l)
                lses.append(mxs[pp] + jnp.log(l))
            halves.append((outs, lses))
        for pp in range(n_pairs):
            cols = slice(128 * pp, 128 * (pp + 1))
            o = jnp.where(lane_lo, halves[0][0][pp], halves[1][0][pp])
            silu, _ = _silu_parts(g_ref[:, cols].astype(F32))
            o_ref[:, cols] = o.astype(BF16)
            z_ref[:, cols] = (o * silu).astype(BF16)
            lse_ref[:, cols] = jnp.where(lane_lo, halves[0][1][pp], halves[1][1][pp])

    out_spec = pl.BlockSpec((qblk, 512), lambda h, j: (j, h))
    return pl.pallas_call(
        body, name="attn_b_fwd", grid=(B_KV_HEADS, t // qblk),
        in_specs=[q_spec, g_spec, kp_spec, kc_spec, vp_spec, vc_spec, bias_spec, sink_spec],
        out_specs=[out_spec, out_spec, out_spec],
        out_shape=[SDS((t, D_MODEL), BF16), SDS((t, D_MODEL), BF16), SDS((t, D_MODEL), F32)],
        scratch_shapes=[pltpu.VMEM((2 * B_GROUP, qblk, qblk + B_PREV), F32)],
        compiler_params=_cparams(),
    )(qg, qg, kv, kv, kv, kv, bias, sinks)


def _attn_b_bwd(qg, kv, bias, sinks, out_b, lse, dz, bucket_onehot):
    t = qg.shape[0]
    qblk = B_QBLK_BWD
    keys = qblk + B_PREV
    nq = t // qblk
    q_spec, g_spec, kp_spec, kc_spec, vp_spec, vc_spec, bias_spec, sink_spec = _b_specs(qblk)

    def body(q_ref, g_ref, kp, kc, vp, vc, w_ref, sink_ref, o_ref, lse_ref, dz_ref, oh_ref,
             dqg_ref, dkv_ref, dt5_ref, dsink_ref, gt_acc, b_ref):
        kvh = pl.program_id(0)
        j = pl.program_id(1)
        _fill_bias(B_GROUP, lambda h: w_ref[h:h + 1, :], B_BAND, B_PREV, 1, b_ref, j)

        @pl.when(jnp.logical_and(kvh == 0, j == 0))
        def _():
            dkv_ref[...] = jnp.zeros_like(dkv_ref)

        @pl.when(j == 0)
        def _():
            gt_acc[...] = jnp.zeros_like(gt_acc)
            dsink_ref[...] = jnp.zeros_like(dsink_ref)

        early = (j < 1).astype(jnp.int32)
        lane_lo = _lane_lo()
        k_lo, k_hi, v_lo, v_hi = _b_operands(kp, kc, vp, vc, kvh)
        dk_blk = jnp.zeros((keys, 128), F32)
        dv_blk = jnp.zeros((keys, 128), F32)
        for pp in range(B_GROUP // 2):
            cols = slice(128 * pp, 128 * (pp + 1))
            qp = q_ref[:, cols]
            o = o_ref[:, cols].astype(F32)
            lse_pair = lse_ref[:, cols]
            dzf = dz_ref[:, cols].astype(F32)
            silu, dsilu = _silu_parts(g_ref[:, cols].astype(F32))
            do = dzf * silu
            dqg_ref[1, :, cols] = (dzf * o * dsilu).astype(BF16)
            doo = do * o
            dqs = []
            for hh in range(2):
                g = 2 * pp + hh
                sel = lane_lo if hh == 0 else jnp.logical_not(lane_lo)
                sink = sink_ref[kvh * B_GROUP + g]
                kk = k_lo if hh == 0 else k_hi
                vv = v_lo if hh == 0 else v_hi
                qm, s = _head_logits(qp, kk, b_ref, g + B_GROUP * early, sel)
                lse_h = _own_everywhere(lse_pair, sel)
                p = jnp.exp(_minus_rows(s, lse_h))
                delta = jnp.sum(jnp.where(sel, doo, 0.0), axis=-1, keepdims=True)
                dom = jnp.where(sel, do, 0.0).astype(BF16)
                dp = _dot_nt(dom, vv)
                ds = p * (dp - delta)
                gt_acc[g] += ds
                dsink_ref[g:g + 1, :] -= jnp.sum(jnp.exp(sink - lse_h) * delta, axis=0, keepdims=True)
                dsb = ds.astype(BF16)
                dqs.append(_dot(dsb, kk) * SCALE)
                dk_blk = dk_blk + _dot_tn(dsb, qm)
                dv_blk = dv_blk + _dot_tn(p.astype(BF16), dom)
            dqg_ref[0, :, cols] = jnp.where(lane_lo, dqs[0], dqs[1]).astype(BF16)
        mine = lane_lo == (kvh == 0)
        dk_add = jnp.where(mine, dk_blk + pltpu.roll(dk_blk, HEAD_DIM, 1), 0.0)
        dv_add = jnp.where(mine, dv_blk + pltpu.roll(dv_blk, HEAD_DIM, 1), 0.0)

        @pl.when(j >= 1)
        def _():
            rows = pl.ds(pl.multiple_of(j * qblk - B_PREV, B_PREV), B_PREV)
            dkv_ref[0, rows, :] += dk_add[0:B_PREV]
            dkv_ref[1, rows, :] += dv_add[0:B_PREV]

        rows = pl.ds(pl.multiple_of(j * qblk, qblk), qblk)
        dkv_ref[0, rows, :] += dk_add[B_PREV:keys]
        dkv_ref[1, rows, :] += dv_add[B_PREV:keys]

        @pl.when(j == nq - 1)
        def _():
            dd = jnp.concatenate([_offset_sums(gt_acc[g]) for g in range(B_GROUP)], axis=0)
            hi = dd.astype(BF16)
            lo = (dd - hi.astype(F32)).astype(BF16)
            dt5_ref[...] = _dot(hi, oh_ref[...]) + _dot(lo, oh_ref[...])

    blk = pl.BlockSpec((qblk, 512), lambda h, j: (j, h))
    return pl.pallas_call(
        body, name="attn_b_bwd", grid=(B_KV_HEADS, nq),
        in_specs=[q_spec, g_spec, kp_spec, kc_spec, vp_spec, vc_spec, bias_spec, sink_spec, blk, blk, blk,
                  pl.BlockSpec((keys, 128), lambda h, j: (0, 0))],
        out_specs=[pl.BlockSpec((2, qblk, 512), lambda h, j: (0, j, h)),
                   pl.BlockSpec((2, t, 128), lambda h, j: (0, 0, 0)),
                   pl.BlockSpec((B_GROUP, 128), lambda h, j: (h, 0)),
                   pl.BlockSpec((B_GROUP, 128), lambda h, j: (h, 0))],
        out_shape=[SDS((2, t, D_MODEL), BF16), SDS((2, t, 128), F32),
                   SDS((N_HEADS, 128), F32), SDS((N_HEADS, 128), F32)],
        scratch_shapes=[pltpu.VMEM((B_GROUP, qblk, keys), F32), pltpu.VMEM((2 * B_GROUP, qblk, keys), F32)],
        compiler_params=_cparams(),
    )(qg, qg, kv, kv, kv, kv, bias, sinks, out_b, lse, dz, bucket_onehot)


def _a_bias_by_offset(rel_bias):
    m = np.arange(A_DIAG)
    idx = np.clip(A_BAND - 1 - m, -A_REL_CLIP, A_REL_CLIP) + A_REL_CLIP
    by_head = rel_bias[idx].T.reshape(N_HEADS // 2, 2, A_DIAG)
    return jnp.concatenate([by_head, jnp.zeros((N_HEADS // 2, 6, A_DIAG), F32)], axis=1)


def _a_bias_grad(offset_sums):
    first = 319
    tail = jnp.sum(offset_sums[:, :first], axis=1)
    body = jnp.flip(offset_sums[:, first:first + 320], axis=1)
    body = body.at[:, -1].add(tail)
    full = jnp.concatenate([jnp.zeros((N_HEADS, 193), F32), body], axis=1)
    return full.T


def _t5_bucket(rel):
    nb = T5_BUCKETS // 2
    max_exact = nb // 2
    ret = jnp.where(rel > 0, nb, 0)
    n = jnp.abs(rel)
    nf = jnp.maximum(n, 1).astype(jnp.float32)
    large = max_exact + (jnp.log(nf / max_exact) / math.log(T5_MAX_DIST / max_exact)
                         * (nb - max_exact)).astype(jnp.int32)
    large = jnp.minimum(large, nb - 1)
    return ret + jnp.where(n < max_exact, n, large)


def _b_offset_buckets(keys):
    return _t5_bucket(jnp.arange(keys, dtype=jnp.int32) - (B_LEFT_CHUNKS * CHUNK + CHUNK - 1))


def _b_bias_by_offset(t5_table, keys):
    return t5_table[_b_offset_buckets(keys)].T


def _b_bucket_onehot(keys):
    return (_b_offset_buckets(keys)[:, None] == jnp.arange(128)[None, :]).astype(BF16)


def _local_step(my_slot, order, x, target, a_gain_shard, w_in_a_shard, rel_bias, late_shards, kv_gain,
                t5_table, b_gain, sinks, f_gain):
    a_bias = _a_bias_by_offset(rel_bias)
    b_bias_fwd = _b_bias_by_offset(t5_table, B_QBLK_FWD + B_PREV)
    b_bias_bwd = _b_bias_by_offset(t5_table, B_QBLK_BWD + B_PREV)
    sinks_flat = sinks.reshape(N_HEADS)

    xn, qkvg, w_in_a, a_gain = _norm_matmul_gather(order, x, a_gain_shard, w_in_a_shard)
    z_a, out_a, lse_a, (w_in_b, w_out_a, w_out_b, kv_w) = _attn_a_fwd(qkvg, a_bias, late_shards)
    w_out_a = w_out_a.reshape(D_MODEL, D_MODEL)
    w_out_b = w_out_b.reshape(D_MODEL, D_MODEL)
    kv_w = kv_w.reshape(D_MODEL, 2 * 128)
    h1, kvn, hb, kv, qg = _layer_a_out(x, z_a, w_out_a, kv_gain, b_gain, kv_w, w_in_b)
    z_b, out_b, lse_b = _attn_b_fwd(qg, kv, b_bias_fwd, sinks_flat)
    dh2, dh2b, dz_b, loss, d_fn = _layer_b_out_loss(h1, z_b, w_out_b, f_gain, target)

    dqg_b, dkv_b, d_t5, d_sink = _attn_b_bwd(qg, kv, b_bias_bwd, sinks_flat, out_b, lse_b, dz_b,
                                             _b_bucket_onehot(B_QBLK_BWD + B_PREV))
    dh1, dh1b, dz_a, d_bn, d_kn = _layer_b_in_bwd(dqg_b, dkv_b, w_in_b, kv_w, h1, dh2, b_gain, kv_gain, w_out_a)
    early = dict(
        b_w_out=_weight_grad_rows("grad_b_w_out", my_slot, z_b, dh2b[None]),
        b_w_in=_weight_grad_cols("grad_b_w_in", my_slot, hb, [dqg_b],
                                 [(0, o, c, 4 * o + c) for o in range(2) for c in range(4)], 256),
        kv_w=_weight_grad_rows("grad_kv_w", my_slot, kvn, dkv_b),
        a_w_out=_weight_grad_rows("grad_a_w_out", my_slot, z_a, dh1b[None]))
    dqg_a, dkv_a, d_rel, landed = _attn_a_bwd(qkvg, a_bias, out_a, lse_a, dz_a, [g[0] for g in early.values()])
    g_w_in_a = _weight_grad_cols(
        "grad_a_w_in", my_slot, xn, [dqg_a, dkv_a],
        [(0, 0, 0, 0), (0, 0, 1, 1), (1, 0, 0, 2), (1, 0, 1, 3), (1, 1, 0, 4), (1, 1, 1, 5), (0, 1, 0, 6), (0, 1, 1, 7)], 512)
    from_sibling, = _exchange_sibling([g_w_in_a[0]])
    x_i, y_i, c_i, chips = _place()
    del x_i, y_i
    forward_slots = jnp.stack([_slot(*chip, c_i) for chip in chips]).astype(jnp.int32)
    chip_sums = _pre_reduce("chip_sum_a_w_in", g_w_in_a[0], from_sibling, forward_slots)
    grad_x, d_an, from_chips = _layer_a_in_bwd(dqg_a, dkv_a, w_in_a, x, dh1, a_gain, chip_sums)

    matrices = {n: (g[1], [(land, 0, N_DEV - 1)]) for (n, g), land in zip(early.items(), landed)}
    matrices["a_w_in"] = (g_w_in_a[1], [(from_sibling, 3, 1), (from_chips, 0, 3)])
    small = dict(
        loss=loss, a_norm=d_an, a_rel_bias=d_rel[:, :2].reshape(N_HEADS, A_DIAG),
        kv_norm=d_kn, t5_bias=d_t5, b_norm=d_bn, b_sinks=d_sink, final_norm=d_fn)
    return grad_x, small, matrices


def _place():
    x, y, c = lax.axis_index("x"), lax.axis_index("y"), lax.axis_index("c")
    chips = [(1 - x, y), (x, 1 - y), (1 - x, 1 - y)]
    return x, y, c, chips


def _slot(px, py, pc):
    return 4 * px + 2 * py + pc


ANY = pl.BlockSpec(memory_space=pl.ANY)


def _peer(x, y, c, k):
    return (x ^ (k >> 2), y ^ ((k >> 1) & 1), c ^ (k & 1))


def _scatter_copies(grad_refs, land_refs, send_sems, recv_sems):
    x, y, c, _ = _place()
    copies = []
    for t, (grad, land) in enumerate(zip(grad_refs, land_refs)):
        for k in range(1, N_DEV):
            peer = _peer(x, y, c, k)
            sem = (N_DEV - 1) * t + k - 1
            copies.append(pltpu.make_async_remote_copy(
                src_ref=grad.at[_slot(*peer)], dst_ref=land.at[k - 1],
                send_sem=send_sems.at[sem], recv_sem=recv_sems.at[sem],
                device_id=peer, device_id_type=MESH))
    return copies


def _gather_phases(ins, outs, send_sems, recv_sems, local_sems):
    n = len(ins)
    x, y, c, chips = _place()
    me, sibling = (x, y, c), (x, y, 1 - c)

    def copy(t, k, block, to, src=None):
        dst = outs[t].at[_slot(*block)]
        return pltpu.make_async_remote_copy(
            src_ref=dst if src is None else src, dst_ref=dst,
            send_sem=send_sems.at[7 * t + k], recv_sem=recv_sems.at[7 * t + k],
            device_id=to, device_id_type=MESH)

    def lists():
        mine = [pltpu.make_async_copy(ins[t], outs[t].at[_slot(*me)], local_sems.at[t]) for t in range(n)]
        first = []
        for t in range(n):
            first.append(copy(t, 0, me, sibling, src=ins[t]))
            first += [copy(t, 1 + j, me, (*chip, c), src=ins[t]) for j, chip in enumerate(chips)]
        passed = [copy(t, 4 + j, (*chip, c), sibling) for t in range(n) for j, chip in enumerate(chips)]
        return mine, first, passed

    def start():
        mine, first, _ = lists()
        for cp in mine + first:
            cp.start()

    def forward():
        _, _, passed = lists()
        for t in range(n):
            for j, chip in enumerate(chips):
                copy(t, 1 + j, (*chip, c), me).wait_recv()
                passed[3 * t + j].start()

    def finish():
        mine, first, passed = lists()
        for t in range(n):
            copy(t, 0, sibling, me).wait_recv()
            for j, chip in enumerate(chips):
                copy(t, 4 + j, (*chip, 1 - c), me).wait_recv()
        for cp in first + passed:
            cp.wait_send()
        for cp in mine:
            cp.wait()

    return start, forward, finish


def _gather_scratch(n):
    return [pltpu.SemaphoreType.DMA((7 * n,)), pltpu.SemaphoreType.DMA((7 * n,)), pltpu.SemaphoreType.DMA((n,))]


def _exchange_sibling(grads):
    n = len(grads)

    def body(*refs):
        ins, outs = refs[:n], refs[n:2 * n]
        send_sems, recv_sems = refs[2 * n:]
        x, y, c, chips = _place()
        sibling = (x, y, 1 - c)
        copies = []
        for t in range(n):
            blocks = [(*chip, 1 - c) for chip in chips] + [sibling]
            for k, block in enumerate(blocks):
                copies.append(pltpu.make_async_remote_copy(
                    src_ref=ins[t].at[_slot(*block)], dst_ref=outs[t].at[k],
                    send_sem=send_sems.at[4 * t + k], recv_sem=recv_sems.at[4 * t + k],
                    device_id=sibling, device_id_type=MESH))
        for cp in copies:
            cp.start()
        for cp in copies:
            cp.wait()

    return pl.pallas_call(
        body, name="grads_to_sibling",
        in_specs=[ANY] * n, out_specs=[ANY] * n,
        out_shape=[SDS((4, *g.shape[1:]), g.dtype) for g in grads],
        scratch_shapes=[pltpu.SemaphoreType.DMA((4 * n,)), pltpu.SemaphoreType.DMA((4 * n,))],
    )(*grads)


def _chip_copies(sums_ref, land_ref, send_sems, recv_sems):
    x, y, c, chips = _place()
    del x, y
    return [pltpu.make_async_remote_copy(
        src_ref=sums_ref.at[j], dst_ref=land_ref.at[j], send_sem=send_sems.at[j], recv_sem=recv_sems.at[j],
        device_id=(*chip, c), device_id_type=MESH) for j, chip in enumerate(chips)]


def _row_tile(rows):
    return min(rows, 256)


def _pre_reduce(name, g, from_sibling, slots):
    _, r, c = g.shape
    tr = _row_tile(r)

    def body(slots_ref, g_ref, s_ref, o_ref):
        del slots_ref
        o_ref[...] = (g_ref[...].astype(F32) + s_ref[...].astype(F32)).astype(BF16)

    return pl.pallas_call(
        body, name=name,
        grid_spec=pltpu.PrefetchScalarGridSpec(
            num_scalar_prefetch=1, grid=(3, r // tr),
            in_specs=[pl.BlockSpec((1, tr, c), lambda j, i, s: (s[j], i, 0)),
                      pl.BlockSpec((1, tr, c), lambda j, i, s: (j, i, 0))],
            out_specs=pl.BlockSpec((1, tr, c), lambda j, i, s: (j, i, 0))),
        out_shape=SDS((3, r, c), BF16),
        compiler_params=_cparams(),
    )(slots, g, from_sibling)


def _adamw(w, g, m, v):
    m2 = ADAM_B1 * m + (1.0 - ADAM_B1) * g
    v2 = ADAM_B2 * v + (1.0 - ADAM_B2) * jnp.square(g)
    m_hat = m2 / (1.0 - ADAM_B1 ** ADAM_STEP)
    v_hat = v2 / (1.0 - ADAM_B2 ** ADAM_STEP)
    delta = -ADAM_LR * (m_hat / (jnp.sqrt(v_hat) + ADAM_EPS) + ADAM_WD * w)
    return delta, m2, v2


def _reduce_adamw(name, own, partials, w, m, v):
    r, c = own.shape
    tr = _row_tile(r)
    n_p = len(partials)

    def body(own_ref, *rest):
        p_refs, (w_ref, m_ref, v_ref, grad_ref, d_ref, nm_ref, nv_ref) = rest[:n_p], rest[n_p:]
        grad = own_ref[...]
        for p_ref, (_, _, count) in zip(p_refs, partials):
            for j in range(count):
                grad = grad + p_ref[j].astype(F32)
        grad_ref[...] = grad
        d_ref[...], nm_ref[...], nv_ref[...] = _adamw(w_ref[...], grad, m_ref[...], v_ref[...])

    flat = pl.BlockSpec((tr, c), lambda i: (i, 0))
    return pl.pallas_call(
        body, name=name, grid=(r // tr,),
        in_specs=[flat] + [pl.BlockSpec((count, tr, c), lambda i, first=first, count=count: (first // count, i, 0))
                           for _, first, count in partials] + [flat, flat, flat],
        out_specs=[flat, flat, flat, flat],
        out_shape=[SDS((r, c), F32)] * 4,
        compiler_params=_cparams(),
    )(own, *[p[0] for p in partials], w, m, v)


VM = pl.BlockSpec()


def _small_allreduce(parts):
    n = len(parts)

    def body(*refs):
        ins, outs, lands = refs[:n], refs[n:2 * n], refs[2 * n:3 * n]
        send_sems, recv_sems = refs[3 * n:]
        x, y, c, _ = _place()
        my_slot = _slot(x, y, c)
        copies = []
        for t in range(n):
            lands[t][my_slot] = ins[t][...]
            for k in range(1, N_DEV):
                sem = (N_DEV - 1) * t + k - 1
                copies.append(pltpu.make_async_remote_copy(
                    src_ref=ins[t], dst_ref=lands[t].at[my_slot],
                    send_sem=send_sems.at[sem], recv_sem=recv_sems.at[sem],
                    device_id=_peer(x, y, c, k), device_id_type=MESH))
        for cp in copies:
            cp.start()
        for t in range(n):
            for k in range(1, N_DEV):
                sem = (N_DEV - 1) * t + k - 1
                pltpu.make_async_remote_copy(
                    src_ref=ins[t], dst_ref=lands[t].at[_slot(*_peer(x, y, c, k))],
                    send_sem=send_sems.at[sem], recv_sem=recv_sems.at[sem],
                    device_id=(x, y, c), device_id_type=MESH).wait_recv()
        for cp in copies:
            cp.wait_send()
        for t in range(n):
            total = lands[t][0]
            for s in range(1, N_DEV):
                total = total + lands[t][s]
            outs[t][...] = total

    n_sems = (N_DEV - 1) * n
    return pl.pallas_call(
        body, name="small_allreduce",
        in_specs=[VM] * n, out_specs=[VM] * n, out_shape=[SDS(p.shape, F32) for p in parts],
        scratch_shapes=[pltpu.VMEM((N_DEV, *p.shape), F32) for p in parts]
        + [pltpu.SemaphoreType.DMA((n_sems,)), pltpu.SemaphoreType.DMA((n_sems,))],
    )(*parts)


def _small_adamw(my_slot, sums, ws, ms, vs):
    n = len(ws)

    def body(slot_ref, *refs):
        sum_refs, refs = refs[:n + 1], refs[n + 1:]
        w_refs, m_refs, v_refs, refs = refs[:n], refs[n:2 * n], refs[2 * n:3 * n], refs[3 * n:]
        g_refs, d_refs, nm_refs, nv_refs = refs[:n + 1], refs[n + 1:2 * n + 1], refs[2 * n + 1:3 * n + 1], refs[3 * n + 1:]
        for t in range(n + 1):
            if t == 0:
                g = sum_refs[0][:, pl.ds(pl.multiple_of(slot_ref[0] * 128, 128), 128)]
            else:
                g = sum_refs[t][...]
            g_refs[t][...] = g
            if t < n:
                d_refs[t][...], nm_refs[t][...], nv_refs[t][...] = _adamw(w_refs[t][...], g, m_refs[t][...], v_refs[t][...])

    shapes = [SDS(w.shape, F32) for w in ws]
    outs = pl.pallas_call(
        body, name="small_adamw",
        in_specs=[pl.BlockSpec(memory_space=pltpu.SMEM)] + [VM] * (4 * n + 1),
        out_specs=[VM] * (4 * n + 1),
        out_shape=shapes + [SDS(sums[-1].shape, F32)] + shapes * 3,
    )(my_slot, *sums, *ws, *ms, *vs)
    return outs[:n + 1], outs[n + 1:2 * n + 1], outs[2 * n + 1:3 * n + 1], outs[3 * n + 1:]


def kernel(x, a_norm, a_w_in, a_rel_bias, a_w_out, kv_norm, kv_w, t5_bias, b_norm, b_w_in, b_sinks, b_w_out, final_norm, loss_target, m_a_norm, m_a_w_in, m_a_rel_bias, m_a_w_out, m_kv_norm, m_kv_w, m_t5_bias, m_b_norm, m_b_w_in, m_b_sinks, m_b_w_out, m_final_norm, v_a_norm, v_a_w_in, v_a_rel_bias, v_a_w_out, v_kv_norm, v_kv_w, v_t5_bias, v_b_norm, v_b_w_in, v_b_sinks, v_b_w_out, v_final_norm):
    xi, yi, ci = lax.axis_index("x"), lax.axis_index("y"), lax.axis_index("c")
    my_slot = _slot(xi, yi, ci)

    slot_arr = jnp.reshape(my_slot, (1,)).astype(jnp.int32)
    order = _gather_order(xi, yi, ci)
    late_shards = [b_w_in[0].astype(BF16), a_w_out[0].astype(BF16), b_w_out[0].astype(BF16), kv_w.astype(BF16)]
    grad_x, loc, matrices = _local_step(
        slot_arr, order, x[0], loss_target[0], a_norm, a_w_in[0].astype(BF16), a_rel_bias[0], late_shards,
        kv_norm.reshape(1, D_MODEL), t5_bias, b_norm, b_sinks, final_norm.reshape(1, D_MODEL))

    shard_w = dict(a_w_in=a_w_in[0], b_w_in=b_w_in[0], a_w_out=a_w_out[0], b_w_out=b_w_out[0], kv_w=kv_w)
    shard_m = dict(a_w_in=m_a_w_in[0], b_w_in=m_b_w_in[0], a_w_out=m_a_w_out[0], b_w_out=m_b_w_out[0], kv_w=m_kv_w)
    shard_v = dict(a_w_in=v_a_w_in[0], b_w_in=v_b_w_in[0], a_w_out=v_a_w_out[0], b_w_out=v_b_w_out[0], kv_w=v_kv_w)
    big = {n: _reduce_adamw("adamw_" + n, own, partials, shard_w[n], shard_m[n], shard_v[n])
           for n, (own, partials) in matrices.items()}

    names = ("a_norm", "a_rel_bias", "kv_norm", "t5_bias", "b_norm", "b_sinks", "final_norm")
    row = lambda a: a.reshape(1, -1) if a.ndim == 1 else (a[0] if a.ndim == 3 else a)
    small_w = [row(a) for a in (a_norm, a_rel_bias, kv_norm, t5_bias, b_norm, b_sinks, final_norm)]
    small_m = [row(a) for a in (m_a_norm, m_a_rel_bias, m_kv_norm, m_t5_bias, m_b_norm, m_b_sinks, m_final_norm)]
    small_v = [row(a) for a in (v_a_norm, v_a_rel_bias, v_kv_norm, v_t5_bias, v_b_norm, v_b_sinks, v_final_norm)]
    sums = dict(zip(names + ("loss",), _small_allreduce([loc[n] for n in names] + [loc["loss"]])))
    sums["a_rel_bias"] = _a_bias_grad(sums["a_rel_bias"])
    sums["t5_bias"] = sums["t5_bias"][:, :T5_BUCKETS].T
    sums["b_sinks"] = sums["b_sinks"][:, 0].reshape(1, N_HEADS)
    results = _small_adamw(slot_arr, [sums[n] for n in names + ("loss",)], small_w, small_m, small_v)
    like = dict(a_norm=a_norm, a_rel_bias=a_rel_bias, kv_norm=kv_norm, t5_bias=t5_bias, b_norm=b_norm,
                b_sinks=b_sinks, final_norm=final_norm)
    sm = [{n: part[i].reshape(like[n].shape) for i, n in enumerate(names)} for part in results]
    loss = results[0][len(names)][0, 0]

    order = ("a_norm", "a_w_in", "a_rel_bias", "a_w_out", "kv_norm", "kv_w", "t5_bias", "b_norm",
             "b_w_in", "b_sinks", "b_w_out", "final_norm")
    lead = dict(a_w_in=True, b_w_in=True, a_w_out=True, b_w_out=True, kv_w=False)

    def pick(kind, name):
        if name in big:
            val = big[name][kind]
            return val[None] if lead[name] else val
        return sm[kind][name]

    outs = [loss, grad_x[None]]
    for kind in range(4):
        outs += [pick(kind, n) for n in order]
    return tuple(outs)
```

```python
import functools
import math

import numpy as np
import jax
import jax.numpy as jnp
from jax import lax
from jax.experimental import pallas as pl
from jax.experimental.pallas import tpu as pltpu

F32 = jnp.float32
BF16 = jnp.bfloat16
SDS = jax.ShapeDtypeStruct

D_MODEL = 1024
HEAD_DIM = 64
CHUNK = 64
N_HEADS = 16
RMS_EPS = 1e-6
A_LEFT_CHUNKS = 8
A_BAND = (A_LEFT_CHUNKS + 1) * CHUNK
A_REL_CLIP = 256
B_KV_HEADS = 2
B_GROUP = 8
B_LEFT_CHUNKS = 2
B_BAND = (B_LEFT_CHUNKS + 1) * CHUNK
T5_BUCKETS = 32
T5_MAX_DIST = 128
QBLK = 256
A_KEYS = 3 * QBLK
B_QBLK_FWD = 128
B_QBLK_BWD = 256
B_PREV = 128
A_DIAG = A_KEYS
NEG = -1e30
SCALE = HEAD_DIM ** -0.5
N_DEV = 8

ADAM_LR = 0.001
ADAM_B1 = 0.9
ADAM_B2 = 0.999
ADAM_EPS = 1e-08
ADAM_WD = 0.01
ADAM_STEP = 10

VMEM_LIMIT_BYTES = 56 * 1024 * 1024
MESH = pl.DeviceIdType.MESH


def _cparams():
    return pltpu.CompilerParams(vmem_limit_bytes=VMEM_LIMIT_BYTES)


def _dot(a, b):
    return jnp.dot(a, b, preferred_element_type=F32)


def _dot_nt(a, b):
    return lax.dot_general(a, b, (((1,), (1,)), ((), ())), preferred_element_type=F32)


def _dot_tn(a, b):
    return lax.dot_general(a, b, (((0,), (0,)), ((), ())), preferred_element_type=F32)


def _rstd(xf):
    return lax.rsqrt(jnp.mean(xf * xf, axis=-1, keepdims=True) + RMS_EPS)


def _sigmoid(x):
    return 1.0 / (1.0 + jnp.exp(-x))


_GATHER_SEQUENCE = ((0, None), (1, 0), (2, 1), (4, None), (5, None), (3, 2), (6, None))


def _gather_order(x, y, c):
    others = [(1 - x, y), (x, 1 - y), (1 - x, 1 - y)]
    arrivals = [_slot(x, y, 1 - c)] + [_slot(*chip, c) for chip in others] + [_slot(*chip, 1 - c) for chip in others]
    return jnp.stack([_slot(x, y, c)] + [arrivals[a] for a, _ in _GATHER_SEQUENCE]).astype(jnp.int32)


def _norm_matmul_gather(order, x, gain_shard, w_shard):
    t = x.shape[0]
    dw, tn = w_shard.shape
    tm = min(t, 1024)
    n_m = t // tm

    def body(order_ref, x_ref, gs_ref, shard_ref, xn_ref, o_ref, full_ref, gain_ref,
             xn_all, wbuf, gland, send_sems, recv_sems, gsend_sems, grecv_sems, load_sems, own_sem):
        n, m = pl.program_id(0), pl.program_id(1)
        x_i, y_i, c_i, chips = _place()
        me, sibling = (x_i, y_i, c_i), (x_i, y_i, 1 - c_i)

        def send(k, block, to, src=None):
            dst = full_ref.at[_slot(*block)]
            return pltpu.make_async_remote_copy(
                src_ref=dst if src is None else src, dst_ref=dst,
                send_sem=send_sems.at[k], recv_sem=recv_sems.at[k], device_id=to, device_id_type=MESH)

        own = pltpu.make_async_copy(shard_ref, full_ref.at[_slot(*me)], own_sem)
        first = [send(0, me, sibling, src=shard_ref)]
        first += [send(1 + j, me, (*chip, c_i), src=shard_ref) for j, chip in enumerate(chips)]
        forwards = [send(4 + j, (*chip, c_i), sibling) for j, chip in enumerate(chips)]
        arrivals = [send(0, sibling, me)] + [send(1 + j, (*chip, c_i), me) for j, chip in enumerate(chips)]
        arrivals += [send(4 + j, (*chip, 1 - c_i), me) for j, chip in enumerate(chips)]
        gains = [pltpu.make_async_remote_copy(
            src_ref=gs_ref, dst_ref=gland.at[_slot(*me)], send_sem=gsend_sems.at[k - 1],
            recv_sem=grecv_sems.at[k - 1], device_id=_peer(x_i, y_i, c_i, k), device_id_type=MESH)
            for k in range(1, N_DEV)]

        @pl.when(jnp.logical_and(n == 0, m == 0))
        def _():
            own.start()
            for cp in gains + first:
                cp.start()
            pltpu.make_async_copy(shard_ref, wbuf.at[0], load_sems.at[0]).start()
            gland[_slot(*me)] = gs_ref[...]
            for k in range(1, N_DEV):
                pltpu.make_async_remote_copy(
                    src_ref=gs_ref, dst_ref=gland.at[_slot(*_peer(x_i, y_i, c_i, k))],
                    send_sem=gsend_sems.at[k - 1], recv_sem=grecv_sems.at[k - 1],
                    device_id=me, device_id_type=MESH).wait_recv()
            for s in range(N_DEV):
                gain_ref[:, 128 * s:128 * (s + 1)] = gland[s]

        rows = pl.ds(pl.multiple_of(m * tm, tm), tm)

        @pl.when(n == 0)
        def _():
            xf = x_ref[...]
            xn = ((xf * _rstd(xf)) * gain_ref[...]).astype(BF16)
            xn_all[rows, :] = xn
            xn_ref[...] = xn

        @pl.when(m == 0)
        def _():
            pltpu.make_async_copy(full_ref.at[0], wbuf.at[n % 2], load_sems.at[n % 2]).wait()

        o_ref[...] = _dot(xn_all[rows, :], wbuf[n % 2]).astype(BF16)

        for k, (arrival, forward) in enumerate(_GATHER_SEQUENCE):
            @pl.when(jnp.logical_and(n == k, m == n_m - 1))
            def _(k=k, arrival=arrival, forward=forward):
                arrivals[arrival].wait_recv()
                if forward is not None:
                    forwards[forward].start()
                pltpu.make_async_copy(full_ref.at[order_ref[k + 1]], wbuf.at[(k + 1) % 2],
                                      load_sems.at[(k + 1) % 2]).start()

        @pl.when(jnp.logical_and(n == N_DEV - 1, m == n_m - 1))
        def _():
            for cp in gains + first + forwards:
                cp.wait_send()
            own.wait()

    held = lambda n, m, order: (jnp.where(n == 0, m, n_m - 1), 0)
    return pl.pallas_call(
        body, name="norm_matmul_gather",
        grid_spec=pltpu.PrefetchScalarGridSpec(
            num_scalar_prefetch=1, grid=(N_DEV, n_m),
            in_specs=[pl.BlockSpec((tm, D_MODEL), held),
                      pl.BlockSpec((1, 128), lambda n, m, order: (0, 0)), ANY],
            out_specs=[pl.BlockSpec((tm, D_MODEL), held),
                       pl.BlockSpec((tm, tn), lambda n, m, order: (m, order[n])),
                       ANY, pl.BlockSpec((1, D_MODEL), lambda n, m, order: (0, 0))],
            scratch_shapes=[pltpu.VMEM((t, D_MODEL), BF16), pltpu.VMEM((2, dw, tn), BF16),
                            pltpu.VMEM((N_DEV, 1, 128), F32),
                            pltpu.SemaphoreType.DMA((7,)), pltpu.SemaphoreType.DMA((7,)),
                            pltpu.SemaphoreType.DMA((7,)), pltpu.SemaphoreType.DMA((7,)),
                            pltpu.SemaphoreType.DMA((2,)), pltpu.SemaphoreType.DMA]),
        out_shape=[SDS((t, D_MODEL), BF16), SDS((t, N_DEV * tn), BF16), SDS((N_DEV, dw, tn), BF16),
                   SDS((1, D_MODEL), F32)],
        compiler_params=_cparams(),
    )(order, x, gain_shard, w_shard)


def _layer_a_out(x, z, w_out, kv_gain, b_gain, kv_w, w_in_b):
    t = x.shape[0]
    tm = min(t, 512)
    nb, _, tn = w_in_b.shape

    def body(x_ref, z_ref, wo_ref, kvg_ref, bg_ref, kvw_ref, wb_ref,
             h1_ref, kvn_ref, hb_ref, kv_ref, qg_ref):
        h1 = x_ref[...] + _dot(z_ref[...], wo_ref[...])
        h1_ref[...] = h1
        y0 = h1 * _rstd(h1)
        kvn = (y0 * kvg_ref[...]).astype(BF16)
        hb = (y0 * bg_ref[...]).astype(BF16)
        kvn_ref[...] = kvn
        hb_ref[...] = hb
        kv_ref[...] = _dot(kvn, kvw_ref[...]).astype(BF16)
        for i in range(nb):
            qg_ref[:, i * tn:(i + 1) * tn] = _dot(hb, wb_ref[i]).astype(BF16)

    row = lambda m: (m, 0)
    fix2 = lambda m: (0, 0)
    return pl.pallas_call(
        body, name="layer_a_out", grid=(t // tm,),
        in_specs=[pl.BlockSpec((tm, D_MODEL), row), pl.BlockSpec((tm, D_MODEL), row),
                  pl.BlockSpec((D_MODEL, D_MODEL), fix2),
                  pl.BlockSpec((1, D_MODEL), fix2), pl.BlockSpec((1, D_MODEL), fix2),
                  pl.BlockSpec((D_MODEL, 256), fix2),
                  pl.BlockSpec((nb, D_MODEL, tn), lambda m: (0, 0, 0))],
        out_specs=[pl.BlockSpec((tm, D_MODEL), row), pl.BlockSpec((tm, D_MODEL), row),
                   pl.BlockSpec((tm, D_MODEL), row), pl.BlockSpec((tm, 256), row),
                   pl.BlockSpec((tm, nb * tn), row)],
        out_shape=[SDS((t, D_MODEL), F32), SDS((t, D_MODEL), BF16), SDS((t, D_MODEL), BF16),
                   SDS((t, 256), BF16), SDS((t, nb * tn), BF16)],
        compiler_params=_cparams(),
    )(x, z, w_out, kv_gain, b_gain, kv_w, w_in_b)


def _layer_b_out_loss(h1, z, w_out, f_gain, target):
    t = h1.shape[0]
    tm = min(t, 512)

    def body(h1_ref, z_ref, wo_ref, fg_ref, tgt_ref,
             dh2_ref, dh2b_ref, dz_ref, loss_ref, dfn_ref):
        @pl.when(pl.program_id(0) == 0)
        def _():
            loss_ref[...] = jnp.zeros_like(loss_ref)
            dfn_ref[...] = jnp.zeros_like(dfn_ref)

        h2 = h1_ref[...] + _dot(z_ref[...], wo_ref[...])
        r = _rstd(h2)
        yn = h2 * r
        fg = fg_ref[...]
        err = yn * fg - tgt_ref[...]
        loss_ref[...] += (0.5 / D_MODEL) * jnp.sum(err * err)
        dy = err * (1.0 / D_MODEL)
        dfn_ref[...] += jnp.sum(dy * yn, axis=0, keepdims=True)
        u = dy * fg
        dh2 = r * u - h2 * ((r * r * r) * jnp.mean(u * h2, axis=-1, keepdims=True))
        dh2_ref[...] = dh2
        dh2b = dh2.astype(BF16)
        dh2b_ref[...] = dh2b
        dz_ref[...] = _dot_nt(dh2b, wo_ref[...]).astype(BF16)

    row = lambda m: (m, 0)
    fix2 = lambda m: (0, 0)
    return pl.pallas_call(
        body, name="layer_b_out_loss", grid=(t // tm,),
        in_specs=[pl.BlockSpec((tm, D_MODEL), row), pl.BlockSpec((tm, D_MODEL), row),
                  pl.BlockSpec((D_MODEL, D_MODEL), fix2), pl.BlockSpec((1, D_MODEL), fix2),
                  pl.BlockSpec((tm, D_MODEL), row)],
        out_specs=[pl.BlockSpec((tm, D_MODEL), row), pl.BlockSpec((tm, D_MODEL), row),
                   pl.BlockSpec((tm, D_MODEL), row), pl.BlockSpec((1, 128), fix2),
                   pl.BlockSpec((1, D_MODEL), fix2)],
        out_shape=[SDS((t, D_MODEL), F32), SDS((t, D_MODEL), BF16), SDS((t, D_MODEL), BF16),
                   SDS((1, 128), F32), SDS((1, D_MODEL), F32)],
        compiler_params=_cparams(),
    )(h1, z, w_out, f_gain, target)


def _layer_b_in_bwd(dqg, dkv, w_in_b, kv_w, h1, dh2, b_gain, kv_gain, w_out_a):
    t = h1.shape[0]
    tm = min(t, 512)
    nb, _, tn = w_in_b.shape
    per = D_MODEL // tn

    def body(dqg_ref, dkv_ref, wb_ref, kvw_ref, h1_ref, dh2_ref, bg_ref, kvg_ref, wo_ref,
             dh1_ref, dh1b_ref, dz_ref, dbn_ref, dkn_ref):
        @pl.when(pl.program_id(0) == 0)
        def _():
            dbn_ref[...] = jnp.zeros_like(dbn_ref)
            dkn_ref[...] = jnp.zeros_like(dkn_ref)

        dhb = jnp.zeros((tm, D_MODEL), F32)
        for i in range(nb):
            blk = dqg_ref[i // per, :, (i % per) * tn:(i % per + 1) * tn]
            dhb = dhb + _dot_nt(blk, wb_ref[i])
        dkn = (_dot_nt(dkv_ref[0].astype(BF16), kvw_ref[:, 0:128])
               + _dot_nt(dkv_ref[1].astype(BF16), kvw_ref[:, 128:256]))
        h1 = h1_ref[...]
        r = _rstd(h1)
        xr = h1 * r
        dbn_ref[...] += jnp.sum(dhb * xr, axis=0, keepdims=True)
        dkn_ref[...] += jnp.sum(dkn * xr, axis=0, keepdims=True)
        u = dhb * bg_ref[...] + dkn * kvg_ref[...]
        dh1 = dh2_ref[...] + r * u - h1 * ((r * r * r) * jnp.mean(u * h1, axis=-1, keepdims=True))
        dh1_ref[...] = dh1
        dh1b = dh1.astype(BF16)
        dh1b_ref[...] = dh1b
        dz_ref[...] = _dot_nt(dh1b, wo_ref[...]).astype(BF16)

    row = lambda m: (m, 0)
    fix2 = lambda m: (0, 0)
    return pl.pallas_call(
        body, name="layer_b_in_bwd", grid=(t // tm,),
        in_specs=[pl.BlockSpec((2, tm, D_MODEL), lambda m: (0, m, 0)),
                  pl.BlockSpec((2, tm, 128), lambda m: (0, m, 0)),
                  pl.BlockSpec((nb, D_MODEL, tn), lambda m: (0, 0, 0)),
                  pl.BlockSpec((D_MODEL, 256), fix2),
                  pl.BlockSpec((tm, D_MODEL), row), pl.BlockSpec((tm, D_MODEL), row),
                  pl.BlockSpec((1, D_MODEL), fix2), pl.BlockSpec((1, D_MODEL), fix2),
                  pl.BlockSpec((D_MODEL, D_MODEL), fix2)],
        out_specs=[pl.BlockSpec((tm, D_MODEL), row), pl.BlockSpec((tm, D_MODEL), row),
                   pl.BlockSpec((tm, D_MODEL), row), pl.BlockSpec((1, D_MODEL), fix2),
                   pl.BlockSpec((1, D_MODEL), fix2)],
        out_shape=[SDS((t, D_MODEL), F32), SDS((t, D_MODEL), BF16), SDS((t, D_MODEL), BF16),
                   SDS((1, D_MODEL), F32), SDS((1, D_MODEL), F32)],
        compiler_params=_cparams(),
    )(dqg, dkv, w_in_b, kv_w, h1, dh2, b_gain, kv_gain, w_out_a)


def _layer_a_in_bwd(dqg, dkv, w_in_a, x, dh1, a_gain, chip_sums):
    t = x.shape[0]
    tm = min(t, 512)
    nb, _, tn = w_in_a.shape
    per = D_MODEL // tn

    def body(dqg_ref, dkv_ref, w_ref, x_ref, dh1_ref, ag_ref, sums_ref, dx_ref, dan_ref, land_ref,
             send_sems, recv_sems):
        @pl.when(pl.program_id(0) == 0)
        def _():
            dan_ref[...] = jnp.zeros_like(dan_ref)
            for cp in _chip_copies(sums_ref, land_ref, send_sems, recv_sems):
                cp.start()

        dxn = jnp.zeros((tm, D_MODEL), F32)
        for i in range(nb):
            part = i // per
            src = dqg_ref if part in (0, 3) else dkv_ref
            outer = {0: 0, 3: 1, 1: 0, 2: 1}[part]
            blk = src[outer, :, (i % per) * tn:(i % per + 1) * tn]
            dxn = dxn + _dot_nt(blk, w_ref[i])
        xf = x_ref[...]
        r = _rstd(xf)
        dan_ref[...] += jnp.sum(dxn * (xf * r), axis=0, keepdims=True)
        u = dxn * ag_ref[...]
        dx_ref[...] = dh1_ref[...] + r * u - xf * ((r * r * r) * jnp.mean(u * xf, axis=-1, keepdims=True))

        @pl.when(pl.program_id(0) == t // tm - 1)
        def _():
            for cp in _chip_copies(sums_ref, land_ref, send_sems, recv_sems):
                cp.wait()

    row = lambda m: (m, 0)
    fix2 = lambda m: (0, 0)
    return pl.pallas_call(
        body, name="layer_a_in_bwd", grid=(t // tm,),
        in_specs=[pl.BlockSpec((2, tm, D_MODEL), lambda m: (0, m, 0)),
                  pl.BlockSpec((2, tm, D_MODEL), lambda m: (0, m, 0)),
                  pl.BlockSpec((nb, D_MODEL, tn), lambda m: (0, 0, 0)),
                  pl.BlockSpec((tm, D_MODEL), row), pl.BlockSpec((tm, D_MODEL), row),
                  pl.BlockSpec((1, D_MODEL), fix2), ANY],
        out_specs=[pl.BlockSpec((tm, D_MODEL), row), pl.BlockSpec((1, D_MODEL), fix2), ANY],
        out_shape=[SDS((t, D_MODEL), F32), SDS((1, D_MODEL), F32), SDS(chip_sums.shape, chip_sums.dtype)],
        scratch_shapes=[pltpu.SemaphoreType.DMA((3,)), pltpu.SemaphoreType.DMA((3,))],
        compiler_params=_cparams(),
    )(dqg, dkv, w_in_a, x, dh1, a_gain, chip_sums)


def _lut(s, vals):
    r = jnp.int32(vals[0])
    for i in range(1, len(vals)):
        r = jnp.where(s == i, jnp.int32(vals[i]), r)
    return r


def _held(steps, i):
    seq, cur = [None] * len(steps), None
    for k in range(len(steps) - 1, -1, -1):
        if steps[k][0] == i:
            cur = steps[k][1:3]
        seq[k] = cur
    for k in range(len(steps)):
        cur = seq[k] = seq[k] if seq[k] is not None else cur
    return seq


def _weight_grad_cols(name, my_slot, a, bs, steps, tn):
    t, dw = a.shape
    n_arr = len(bs)
    which = [s[0] for s in steps]
    blks = [s[3] for s in steps]

    def body(slot_ref, a_ref, *rest):
        b_refs, (o_ref, own_ref, at_ref) = rest[:n_arr], rest[n_arr:]
        s = pl.program_id(0)

        @pl.when(s == 0)
        def _():
            at_ref[...] = a_ref[...].T

        for i in range(n_arr):
            @pl.when(_lut(s, which) == i)
            def _(i=i):
                res = _dot(at_ref[...], b_refs[i][0])
                o_ref[0] = res.astype(BF16)

                @pl.when(_lut(s, blks) == slot_ref[0])
                def _():
                    own_ref[...] = res

    def b_spec(i):
        held = _held(steps, i)
        return pl.BlockSpec((1, t, tn), lambda s, slot: (_lut(s, [h[0] for h in held]), 0,
                                                         _lut(s, [h[1] for h in held])))

    return pl.pallas_call(
        body, name=name,
        grid_spec=pltpu.PrefetchScalarGridSpec(
            num_scalar_prefetch=1, grid=(len(steps),),
            in_specs=[pl.BlockSpec((t, dw), lambda s, slot: (0, 0))] + [b_spec(i) for i in range(n_arr)],
            out_specs=[pl.BlockSpec((1, dw, tn), lambda s, slot: (_lut(s, blks), 0, 0)),
                       pl.BlockSpec((dw, tn), lambda s, slot: (0, 0))],
            scratch_shapes=[pltpu.VMEM((dw, t), BF16)]),
        out_shape=[SDS((N_DEV, dw, tn), BF16), SDS((dw, tn), F32)],
        compiler_params=_cparams(),
    )(my_slot, a, *bs)


def _weight_grad_rows(name, my_slot, a, b):
    t, dw = a.shape
    n_o, _, c = b.shape
    rows = dw // N_DEV
    tn = min(c, 256)
    per = c // tn

    def body(slot_ref, a_ref, b_ref, o_ref, own_ref, at_ref, res_ref):
        @pl.when(pl.program_id(0) == 0)
        def _():
            at_ref[...] = a_ref[...].T

        res_ref[...] = _dot(at_ref[...], b_ref[0].astype(BF16))
        o_ref[...] = res_ref[...].astype(BF16)
        own_ref[...] = res_ref[pl.ds(pl.multiple_of(slot_ref[0] * rows, rows), rows), :]

    all_rows, own = pl.pallas_call(
        body, name=name,
        grid_spec=pltpu.PrefetchScalarGridSpec(
            num_scalar_prefetch=1, grid=(n_o * per,),
            in_specs=[pl.BlockSpec((t, dw), lambda s, slot: (0, 0)),
                      pl.BlockSpec((1, t, tn), lambda s, slot: (s // per, 0, s % per))],
            out_specs=[pl.BlockSpec((dw, tn), lambda s, slot: (0, s)),
                       pl.BlockSpec((rows, tn), lambda s, slot: (0, s))],
            scratch_shapes=[pltpu.VMEM((dw, t), BF16), pltpu.VMEM((dw, tn), F32)]),
        out_shape=[SDS((dw, n_o * c), BF16), SDS((rows, n_o * c), F32)],
        compiler_params=_cparams(),
    )(my_slot, a, b)
    return all_rows.reshape(N_DEV, rows, n_o * c), own


def _lane_lo():
    return lax.broadcasted_iota(jnp.int32, (1, 128), 1) < HEAD_DIM


def _offset_sums(gt):
    keys = gt.shape[1]
    gc = gt[0:CHUNK]
    for cc in range(1, gt.shape[0] // CHUNK):
        gc = gc + pltpu.roll(gt[cc * CHUNK:(cc + 1) * CHUNK], keys - cc * CHUNK, 1)
    hi = gc.astype(BF16)
    lo = (gc - hi.astype(F32)).astype(BF16)
    flip = (lax.broadcasted_iota(jnp.int32, (CHUNK, CHUNK), 0)
            + lax.broadcasted_iota(jnp.int32, (CHUNK, CHUNK), 1) == CHUNK - 1).astype(BF16)
    gf = _dot(flip, hi) + _dot(flip, lo)
    skew = pltpu.roll(gf, 0, 1, stride=1, stride_axis=0)
    return jnp.sum(skew, axis=0, keepdims=True)


def _band_bias(w_row, band, rows):
    keys = w_row.shape[1]
    base = jnp.broadcast_to(w_row, (CHUNK, keys))
    skew = pltpu.roll(base, 0, 1, stride=1, stride_axis=0)
    skew = pltpu.roll(skew, keys - (CHUNK - 1), 1)
    col = lax.broadcasted_iota(jnp.int32, (CHUNK, keys), 1)
    chunk0 = jnp.where(col < band, skew, NEG)
    return jnp.concatenate(
        [chunk0] + [pltpu.roll(chunk0, cc * CHUNK, 1) for cc in range(1, rows // CHUNK)], axis=0)


def _silu_parts(g):
    sg = _sigmoid(g)
    return g * sg, sg * (1.0 + g * (1.0 - sg))


A_PAIRS = 2
A_ROWS = QBLK
A_LANES = 128 * A_PAIRS
A_STEPS = D_MODEL // A_LANES


def _a_specs():
    q = pl.BlockSpec((QBLK, A_LANES), lambda p, j: (j, p))
    ks = [pl.BlockSpec((QBLK, A_LANES), lambda p, j, b=b: (jnp.maximum(j - 2 + b, 0), A_STEPS + p)) for b in range(3)]
    vs = [pl.BlockSpec((QBLK, A_LANES), lambda p, j, b=b: (jnp.maximum(j - 2 + b, 0), 2 * A_STEPS + p))
          for b in range(3)]
    g = pl.BlockSpec((QBLK, A_LANES), lambda p, j: (j, 3 * A_STEPS + p))
    bias = pl.BlockSpec((A_PAIRS, 8, A_KEYS), lambda p, j: (p, 0, 0))
    return q, ks, vs, g, bias


def _a_fill_bias(w_ref, b_ref, j):
    _fill_bias(2 * A_PAIRS, lambda h: w_ref[h // 2, h % 2:h % 2 + 1, :], A_BAND, QBLK * (2 - j), 2, b_ref, j)


def _fill_bias(n, get_row, band, first_valid_col, early, bias_scr, j):
    @pl.when(j == 0)
    def _():
        for h in range(n):
            bias_scr[h] = _band_bias(get_row(h), band, bias_scr.shape[1])

    @pl.when(j < early)
    def _():
        keys = bias_scr.shape[2]
        col_ok = lax.broadcasted_iota(jnp.int32, (1, keys), 1) >= first_valid_col
        for h in range(n):
            bias_scr[n + h] = jnp.where(col_ok, bias_scr[h], NEG)


def _head_logits(q, k, bias_scr, idx, sel):
    qm = jnp.where(sel, q, jnp.zeros_like(q)) * SCALE
    return qm, _dot_nt(qm, k) + bias_scr[idx]


def _row_sums_everywhere(r, sel):
    return jnp.where(sel, pltpu.roll(r, HEAD_DIM, 1), r)


def _own_everywhere(x, sel):
    return jnp.where(sel, x, pltpu.roll(x, HEAD_DIM, 1))


def _minus_rows(s, row_full):
    return jnp.concatenate([s[:, i:i + 128] - row_full for i in range(0, s.shape[1], 128)], axis=1)


def _attn_a_fwd(qkvg, bias, gather):
    t = qkvg.shape[0]
    nq = t // QBLK
    n_g = len(gather)
    q_spec, k_specs, v_specs, g_spec, bias_spec = _a_specs()

    def body(q_ref, k0, k1, k2, v0, v1, v2, g_ref, w_ref, *rest):
        shard_refs, rest = rest[:n_g], rest[n_g:]
        z_ref, o_ref, lse_ref = rest[:3]
        full_refs, (b_ref, *comm) = rest[3:3 + n_g], rest[3 + n_g:]
        p = pl.program_id(0)
        j = pl.program_id(1)
        start, forward, finish = _gather_phases(shard_refs, full_refs, *comm)
        pl.when(jnp.logical_and(p == 0, j == 0))(start)
        pl.when(jnp.logical_and(p == A_STEPS // 2, j == 0))(forward)
        _a_fill_bias(w_ref, b_ref, j)
        early = (j < 2).astype(jnp.int32)
        lane_lo = _lane_lo()
        sels = (lane_lo, jnp.logical_not(lane_lo))
        for pp in range(A_PAIRS):
            cols = slice(128 * pp, 128 * (pp + 1))
            k = jnp.concatenate([k0[:, cols], k1[:, cols], k2[:, cols]], axis=0)
            v = jnp.concatenate([v0[:, cols], v1[:, cols], v2[:, cols]], axis=0)
            v1s = [jnp.where(sel, v, jnp.ones_like(v)) for sel in sels]
            for rb in range(QBLK // A_ROWS):
                rows = slice(rb * A_ROWS, (rb + 1) * A_ROWS)
                q = q_ref[rows, cols]
                outs, lses = [], []
                qm2 = jnp.concatenate([jnp.where(sel, q, jnp.zeros_like(q)) for sel in sels], axis=0) * SCALE
                s2 = _dot_nt(qm2, k)
                for hh, sel in enumerate(sels):
                    s = s2[hh * A_ROWS:(hh + 1) * A_ROWS] + b_ref[2 * pp + hh + 2 * A_PAIRS * early, rows, :]
                    mx = jnp.max(s, axis=-1, keepdims=True)
                    e = jnp.exp(s - mx).astype(BF16)
                    r = _dot(e, v1s[hh])
                    l = _row_sums_everywhere(r, sel)
                    outs.append(r / l)
                    lses.append(mx + jnp.log(l))
                o = jnp.where(lane_lo, outs[0], outs[1])
                silu, _ = _silu_parts(g_ref[rows, cols].astype(F32))
                o_ref[rows, cols] = o.astype(BF16)
                z_ref[rows, cols] = (o * silu).astype(BF16)
                lse_ref[rows, cols] = jnp.where(lane_lo, lses[0], lses[1])
        pl.when(jnp.logical_and(p == A_STEPS - 1, j == nq - 1))(finish)

    out_spec = pl.BlockSpec((QBLK, A_LANES), lambda p, j: (j, p))
    outs = pl.pallas_call(
        body, name="attn_a_fwd", grid=(A_STEPS, nq),
        in_specs=[q_spec, *k_specs, *v_specs, g_spec, bias_spec] + [ANY] * n_g,
        out_specs=[out_spec, out_spec, out_spec] + [ANY] * n_g,
        out_shape=[SDS((t, D_MODEL), BF16), SDS((t, D_MODEL), BF16), SDS((t, D_MODEL), F32)]
        + [SDS((N_DEV, *s.shape), s.dtype) for s in gather],
        scratch_shapes=[pltpu.VMEM((4 * A_PAIRS, QBLK, A_KEYS), F32)] + _gather_scratch(n_g),
        compiler_params=_cparams(),
    )(qkvg, qkvg, qkvg, qkvg, qkvg, qkvg, qkvg, qkvg, bias, *gather)
    return outs[0], outs[1], outs[2], list(outs[3:])


def _attn_a_bwd(qkvg, bias, out_a, lse, dz, scatter):
    t = qkvg.shape[0]
    nq = t // QBLK
    n_sc = len(scatter)
    q_spec, k_specs, v_specs, g_spec, bias_spec = _a_specs()

    def body(q_ref, k0, k1, k2, v0, v1, v2, g_ref, w_ref, o_ref, lse_ref, dz_ref, *rest):
        sc_refs, rest = rest[:n_sc], rest[n_sc:]
        dqg_ref, dkv_ref, dg_ref = rest[:3]
        land_refs, rest = rest[3:3 + n_sc], rest[3 + n_sc:]
        dk_acc, dv_acc, gt_acc, b_ref, send_sems, recv_sems = rest
        j = pl.program_id(1)
        first = jnp.logical_and(pl.program_id(0) == 0, j == 0)
        last = jnp.logical_and(pl.program_id(0) == A_STEPS - 1, j == nq - 1)

        @pl.when(first)
        def _():
            for cp in _scatter_copies(sc_refs, land_refs, send_sems, recv_sems):
                cp.start()

        _a_fill_bias(w_ref, b_ref, j)

        @pl.when(j == 0)
        def _():
            dk_acc[...] = jnp.zeros_like(dk_acc)
            dv_acc[...] = jnp.zeros_like(dv_acc)
            gt_acc[...] = jnp.zeros_like(gt_acc)

        early = (j < 2).astype(jnp.int32)
        lane_lo = _lane_lo()
        for pp in range(A_PAIRS):
            cols = slice(128 * pp, 128 * (pp + 1))
            q = q_ref[:, cols]
            k = jnp.concatenate([k0[:, cols], k1[:, cols], k2[:, cols]], axis=0)
            v = jnp.concatenate([v0[:, cols], v1[:, cols], v2[:, cols]], axis=0)
            o = o_ref[:, cols].astype(F32)
            lse_pair = lse_ref[:, cols]
            dzf = dz_ref[:, cols].astype(F32)
            silu, dsilu = _silu_parts(g_ref[:, cols].astype(F32))
            do = dzf * silu
            dqg_ref[1, :, cols] = (dzf * o * dsilu).astype(BF16)
            doo = do * o
            sels = (lane_lo, jnp.logical_not(lane_lo))
            qm2 = jnp.concatenate([jnp.where(sel, q, jnp.zeros_like(q)) for sel in sels], axis=0) * SCALE
            dom2 = jnp.concatenate([jnp.where(sel, do, 0.0) for sel in sels], axis=0).astype(BF16)
            s2 = _dot_nt(qm2, k)
            dp2 = _dot_nt(dom2, v)
            ps, dss = [], []
            for hh, sel in enumerate(sels):
                rows = slice(hh * QBLK, (hh + 1) * QBLK)
                s = s2[rows] + b_ref[2 * pp + hh + 2 * A_PAIRS * early]
                p = jnp.exp(_minus_rows(s, _own_everywhere(lse_pair, sel)))
                delta = jnp.sum(jnp.where(sel, doo, 0.0), axis=-1, keepdims=True)
                ds = p * (dp2[rows] - delta)
                gt_acc[2 * pp + hh] += ds
                ps.append(p.astype(BF16))
                dss.append(ds.astype(BF16))
            dsb2 = jnp.concatenate(dss, axis=0)
            dq2 = _dot(dsb2, k) * SCALE
            dk_blk = _dot_tn(dsb2, qm2)
            dv_blk = _dot_tn(jnp.concatenate(ps, axis=0), dom2)
            dqg_ref[0, :, cols] = jnp.where(lane_lo, dq2[0:QBLK], dq2[QBLK:2 * QBLK]).astype(BF16)
            for b in range(3):
                @pl.when(j - 2 + b >= 0)
                def _(b=b, cols=cols, dk_blk=dk_blk, dv_blk=dv_blk):
                    rows = pl.ds(pl.multiple_of((j - 2 + b) * QBLK, QBLK), QBLK)
                    dk_acc[rows, cols] += dk_blk[b * QBLK:(b + 1) * QBLK]
                    dv_acc[rows, cols] += dv_blk[b * QBLK:(b + 1) * QBLK]

        @pl.when(j == nq - 1)
        def _():
            dkv_ref[0] = dk_acc[...].astype(BF16)
            dkv_ref[1] = dv_acc[...].astype(BF16)
            for pp in range(A_PAIRS):
                dg_ref[pp] = jnp.concatenate([_offset_sums(gt_acc[2 * pp]), _offset_sums(gt_acc[2 * pp + 1]),
                                              jnp.zeros((6, A_DIAG), F32)], axis=0)

        @pl.when(last)
        def _():
            for cp in _scatter_copies(sc_refs, land_refs, send_sems, recv_sems):
                cp.wait()

    blk = pl.BlockSpec((QBLK, A_LANES), lambda p, j: (j, p))
    outs = pl.pallas_call(
        body, name="attn_a_bwd", grid=(A_STEPS, nq),
        in_specs=[q_spec, *k_specs, *v_specs, g_spec, bias_spec, blk, blk, blk] + [ANY] * n_sc,
        out_specs=[pl.BlockSpec((2, QBLK, A_LANES), lambda p, j: (0, j, p)),
                   pl.BlockSpec((2, t, A_LANES), lambda p, j: (0, 0, p)),
                   pl.BlockSpec((A_PAIRS, 8, A_DIAG), lambda p, j: (p, 0, 0))] + [ANY] * n_sc,
        out_shape=[SDS((2, t, D_MODEL), BF16), SDS((2, t, D_MODEL), BF16), SDS((N_HEADS // 2, 8, A_DIAG), F32)]
        + [SDS((N_DEV - 1, *g.shape[1:]), g.dtype) for g in scatter],
        scratch_shapes=[pltpu.VMEM((t, A_LANES), F32), pltpu.VMEM((t, A_LANES), F32),
                        pltpu.VMEM((2 * A_PAIRS, QBLK, A_KEYS), F32), pltpu.VMEM((4 * A_PAIRS, QBLK, A_KEYS), F32),
                        pltpu.SemaphoreType.DMA(((N_DEV - 1) * n_sc,)),
                        pltpu.SemaphoreType.DMA(((N_DEV - 1) * n_sc,))],
        compiler_params=_cparams(),
    )(qkvg, qkvg, qkvg, qkvg, qkvg, qkvg, qkvg, qkvg, bias, out_a, lse, dz, *scatter)
    return outs[0], outs[1], outs[2], list(outs[3:])


def _b_specs(qblk):
    per = qblk // B_PREV
    q = pl.BlockSpec((qblk, 512), lambda h, j: (j, h))
    g = pl.BlockSpec((qblk, 512), lambda h, j: (j, 2 + h))
    kp = pl.BlockSpec((B_PREV, 128), lambda h, j: (jnp.maximum(per * j - 1, 0), 0))
    kc = pl.BlockSpec((qblk, 128), lambda h, j: (j, 0))
    vp = pl.BlockSpec((B_PREV, 128), lambda h, j: (jnp.maximum(per * j - 1, 0), 1))
    vc = pl.BlockSpec((qblk, 128), lambda h, j: (j, 1))
    bias = pl.BlockSpec((B_GROUP, qblk + B_PREV), lambda h, j: (h, 0))
    sinks = pl.BlockSpec(memory_space=pltpu.SMEM)
    return q, g, kp, kc, vp, vc, bias, sinks


def _b_operands(kp, kc, vp, vc, kvh):
    k = jnp.concatenate([kp[...], kc[...]], axis=0)
    v = jnp.concatenate([vp[...], vc[...]], axis=0)
    kr = pltpu.roll(k, HEAD_DIM, 1)
    vr = pltpu.roll(v, HEAD_DIM, 1)
    first = kvh == 0
    return (jnp.where(first, k, kr), jnp.where(first, kr, k),
            jnp.where(first, v, vr), jnp.where(first, vr, v))


def _attn_b_fwd(qg, kv, bias, sinks):
    t = qg.shape[0]
    qblk = B_QBLK_FWD
    q_spec, g_spec, kp_spec, kc_spec, vp_spec, vc_spec, bias_spec, sink_spec = _b_specs(qblk)

    def body(q_ref, g_ref, kp, kc, vp, vc, w_ref, sink_ref, z_ref, o_ref, lse_ref, b_ref):
        kvh = pl.program_id(0)
        j = pl.program_id(1)
        _fill_bias(B_GROUP, lambda h: w_ref[h:h + 1, :], B_BAND, B_PREV, 1, b_ref, j)
        early = (j < 1).astype(jnp.int32)
        lane_lo = _lane_lo()
        k_lo, k_hi, v_lo, v_hi = _b_operands(kp, kc, vp, vc, kvh)
        n_pairs = B_GROUP // 2
        halves = []
        for hh, sel in enumerate((lane_lo, jnp.logical_not(lane_lo))):
            kk = k_lo if hh == 0 else k_hi
            vv = v_lo if hh == 0 else v_hi
            qm4 = jnp.concatenate([jnp.where(sel, q_ref[:, 128 * pp:128 * (pp + 1)], jnp.zeros((qblk, 128), BF16))
                                   for pp in range(n_pairs)], axis=0) * SCALE
            s4 = _dot_nt(qm4, kk)
            es, mxs = [], []
            for pp in range(n_pairs):
                g = 2 * pp + hh
                s = s4[pp * qblk:(pp + 1) * qblk] + b_ref[g + B_GROUP * early]
                mxs.append(jnp.maximum(jnp.max(s, axis=-1, keepdims=True), sink_ref[kvh * B_GROUP + g]))
                es.append(jnp.exp(s - mxs[pp]).astype(BF16))
            r4 = _dot(jnp.concatenate(es, axis=0), jnp.where(sel, vv, jnp.ones_like(vv)))
            outs, lses = [], []
            for pp in range(n_pairs):
                r = r4[pp * qblk:(pp + 1) * qblk]
                l = _row_sums_everywhere(r, sel) + jnp.exp(sink_ref[kvh * B_GROUP + 2 * pp + hh] - mxs[pp])
                outs.append(r / l)
                lses.append(mxs[pp] + jnp.log(l))
            halves.append((outs, lses))
        for pp in range(n_pairs):
            cols = slice(128 * pp, 128 * (pp + 1))
            o = jnp.where(lane_lo, halves[0][0][pp], halves[1][0][pp])
            silu, _ = _silu_parts(g_ref[:, cols].astype(F32))
            o_ref[:, cols] = o.astype(BF16)
            z_ref[:, cols] = (o * silu).astype(BF16)
            lse_ref[:, cols] = jnp.where(lane_lo, halves[0][1][pp], halves[1][1][pp])

    out_spec = pl.BlockSpec((qblk, 512), lambda h, j: (j, h))
    return pl.pallas_call(
        body, name="attn_b_fwd", grid=(B_KV_HEADS, t // qblk),
        in_specs=[q_spec, g_spec, kp_spec, kc_spec, vp_spec, vc_spec, bias_spec, sink_spec],
        out_specs=[out_spec, out_spec, out_spec],
        out_shape=[SDS((t, D_MODEL), BF16), SDS((t, D_MODEL), BF16), SDS((t, D_MODEL), F32)],
        scratch_shapes=[pltpu.VMEM((2 * B_GROUP, qblk, qblk + B_PREV), F32)],
        compiler_params=_cparams(),
    )(qg, qg, kv, kv, kv, kv, bias, sinks)


def _attn_b_bwd(qg, kv, bias, sinks, out_b, lse, dz, bucket_onehot):
    t = qg.shape[0]
    qblk = B_QBLK_BWD
    keys = qblk + B_PREV
    nq = t // qblk
    q_spec, g_spec, kp_spec, kc_spec, vp_spec, vc_spec, bias_spec, sink_spec = _b_specs(qblk)

    def body(q_ref, g_ref, kp, kc, vp, vc, w_ref, sink_ref, o_ref, lse_ref, dz_ref, oh_ref,
             dqg_ref, dkv_ref, dt5_ref, dsink_ref, gt_acc, b_ref):
        kvh = pl.program_id(0)
        j = pl.program_id(1)
        _fill_bias(B_GROUP, lambda h: w_ref[h:h + 1, :], B_BAND, B_PREV, 1, b_ref, j)

        @pl.when(jnp.logical_and(kvh == 0, j == 0))
        def _():
            dkv_ref[...] = jnp.zeros_like(dkv_ref)

        @pl.when(j == 0)
        def _():
            gt_acc[...] = jnp.zeros_like(gt_acc)
            dsink_ref[...] = jnp.zeros_like(dsink_ref)

        early = (j < 1).astype(jnp.int32)
        lane_lo = _lane_lo()
        k_lo, k_hi, v_lo, v_hi = _b_operands(kp, kc, vp, vc, kvh)
        dk_blk = jnp.zeros((keys, 128), F32)
        dv_blk = jnp.zeros((keys, 128), F32)
        for pp in range(B_GROUP // 2):
            cols = slice(128 * pp, 128 * (pp + 1))
            qp = q_ref[:, cols]
            o = o_ref[:, cols].astype(F32)
            lse_pair = lse_ref[:, cols]
            dzf = dz_ref[:, cols].astype(F32)
            silu, dsilu = _silu_parts(g_ref[:, cols].astype(F32))
            do = dzf * silu
            dqg_ref[1, :, cols] = (dzf * o * dsilu).astype(BF16)
            doo = do * o
            dqs = []
            for hh in range(2):
                g = 2 * pp + hh
                sel = lane_lo if hh == 0 else jnp.logical_not(lane_lo)
                sink = sink_ref[kvh * B_GROUP + g]
                kk = k_lo if hh == 0 else k_hi
                vv = v_lo if hh == 0 else v_hi
                qm, s = _head_logits(qp, kk, b_ref, g + B_GROUP * early, sel)
                lse_h = _own_everywhere(lse_pair, sel)
                p = jnp.exp(_minus_rows(s, lse_h))
                delta = jnp.sum(jnp.where(sel, doo, 0.0), axis=-1, keepdims=True)
                dom = jnp.where(sel, do, 0.0).astype(BF16)
                dp = _dot_nt(dom, vv)
                ds = p * (dp - delta)
                gt_acc[g] += ds
                dsink_ref[g:g + 1, :] -= jnp.sum(jnp.exp(sink - lse_h) * delta, axis=0, keepdims=True)
                dsb = ds.astype(BF16)
                dqs.append(_dot(dsb, kk) * SCALE)
                dk_blk = dk_blk + _dot_tn(dsb, qm)
                dv_blk = dv_blk + _dot_tn(p.astype(BF16), dom)
            dqg_ref[0, :, cols] = jnp.where(lane_lo, dqs[0], dqs[1]).astype(BF16)
        mine = lane_lo == (kvh == 0)
        dk_add = jnp.where(mine, dk_blk + pltpu.roll(dk_blk, HEAD_DIM, 1), 0.0)
        dv_add = jnp.where(mine, dv_blk + pltpu.roll(dv_blk, HEAD_DIM, 1), 0.0)

        @pl.when(j >= 1)
        def _():
            rows = pl.ds(pl.multiple_of(j * qblk - B_PREV, B_PREV), B_PREV)
            dkv_ref[0, rows, :] += dk_add[0:B_PREV]
            dkv_ref[1, rows, :] += dv_add[0:B_PREV]

        rows = pl.ds(pl.multiple_of(j * qblk, qblk), qblk)
        dkv_ref[0, rows, :] += dk_add[B_PREV:keys]
        dkv_ref[1, rows, :] += dv_add[B_PREV:keys]

        @pl.when(j == nq - 1)
        def _():
            dd = jnp.concatenate([_offset_sums(gt_acc[g]) for g in range(B_GROUP)], axis=0)
            hi = dd.astype(BF16)
            lo = (dd - hi.astype(F32)).astype(BF16)
            dt5_ref[...] = _dot(hi, oh_ref[...]) + _dot(lo, oh_ref[...])

    blk = pl.BlockSpec((qblk, 512), lambda h, j: (j, h))
    return pl.pallas_call(
        body, name="attn_b_bwd", grid=(B_KV_HEADS, nq),
        in_specs=[q_spec, g_spec, kp_spec, kc_spec, vp_spec, vc_spec, bias_spec, sink_spec, blk, blk, blk,
                  pl.BlockSpec((keys, 128), lambda h, j: (0, 0))],
        out_specs=[pl.BlockSpec((2, qblk, 512), lambda h, j: (0, j, h)),
                   pl.BlockSpec((2, t, 128), lambda h, j: (0, 0, 0)),
                   pl.BlockSpec((B_GROUP, 128), lambda h, j: (h, 0)),
                   pl.BlockSpec((B_GROUP, 128), lambda h, j: (h, 0))],
        out_shape=[SDS((2, t, D_MODEL), BF16), SDS((2, t, 128), F32),
                   SDS((N_HEADS, 128), F32), SDS((N_HEADS, 128), F32)],
        scratch_shapes=[pltpu.VMEM((B_GROUP, qblk, keys), F32), pltpu.VMEM((2 * B_GROUP, qblk, keys), F32)],
        compiler_params=_cparams(),
    )(qg, qg, kv, kv, kv, kv, bias, sinks, out_b, lse, dz, bucket_onehot)


def _a_bias_by_offset(rel_bias):
    m = np.arange(A_DIAG)
    idx = np.clip(A_BAND - 1 - m, -A_REL_CLIP, A_REL_CLIP) + A_REL_CLIP
    by_head = rel_bias[idx].T.reshape(N_HEADS // 2, 2, A_DIAG)
    return jnp.concatenate([by_head, jnp.zeros((N_HEADS // 2, 6, A_DIAG), F32)], axis=1)


def _a_bias_grad(offset_sums):
    first = 319
    tail = jnp.sum(offset_sums[:, :first], axis=1)
    body = jnp.flip(offset_sums[:, first:first + 320], axis=1)
    body = body.at[:, -1].add(tail)
    full = jnp.concatenate([jnp.zeros((N_HEADS, 193), F32), body], axis=1)
    return full


def _t5_bucket(rel):
    nb = T5_BUCKETS // 2
    max_exact = nb // 2
    ret = jnp.where(rel > 0, nb, 0)
    n = jnp.abs(rel)
    nf = jnp.maximum(n, 1).astype(jnp.float32)
    large = max_exact + (jnp.log(nf / max_exact) / math.log(T5_MAX_DIST / max_exact)
                         * (nb - max_exact)).astype(jnp.int32)
    large = jnp.minimum(large, nb - 1)
    return ret + jnp.where(n < max_exact, n, large)


def _b_offset_buckets(keys):
    return _t5_bucket(jnp.arange(keys, dtype=jnp.int32) - (B_LEFT_CHUNKS * CHUNK + CHUNK - 1))


def _b_bias_by_offset(t5_table, keys):
    return t5_table[_b_offset_buckets(keys)].T


def _b_bucket_onehot(keys):
    return (_b_offset_buckets(keys)[:, None] == jnp.arange(128)[None, :]).astype(BF16)


def _local_step(my_slot, order, x, target, a_gain_shard, w_in_a_shard, rel_bias, late_shards, kv_gain,
                t5_table, b_gain, sinks, f_gain):
    a_bias = _a_bias_by_offset(rel_bias)
    b_bias_fwd = _b_bias_by_offset(t5_table, B_QBLK_FWD + B_PREV)
    b_bias_bwd = _b_bias_by_offset(t5_table, B_QBLK_BWD + B_PREV)
    sinks_flat = sinks.reshape(N_HEADS)

    xn, qkvg, w_in_a, a_gain = _norm_matmul_gather(order, x, a_gain_shard, w_in_a_shard)
    z_a, out_a, lse_a, (w_in_b, w_out_a, w_out_b, kv_w) = _attn_a_fwd(qkvg, a_bias, late_shards)
    w_out_a = w_out_a.reshape(D_MODEL, D_MODEL)
    w_out_b = w_out_b.reshape(D_MODEL, D_MODEL)
    kv_w = kv_w.reshape(D_MODEL, 2 * 128)
    h1, kvn, hb, kv, qg = _layer_a_out(x, z_a, w_out_a, kv_gain, b_gain, kv_w, w_in_b)
    z_b, out_b, lse_b = _attn_b_fwd(qg, kv, b_bias_fwd, sinks_flat)
    dh2, dh2b, dz_b, loss, d_fn = _layer_b_out_loss(h1, z_b, w_out_b, f_gain, target)

    dqg_b, dkv_b, d_t5, d_sink = _attn_b_bwd(qg, kv, b_bias_bwd, sinks_flat, out_b, lse_b, dz_b,
                                             _b_bucket_onehot(B_QBLK_BWD + B_PREV))
    dh1, dh1b, dz_a, d_bn, d_kn = _layer_b_in_bwd(dqg_b, dkv_b, w_in_b, kv_w, h1, dh2, b_gain, kv_gain, w_out_a)
    early = dict(
        b_w_out=_weight_grad_rows("grad_b_w_out", my_slot, z_b, dh2b[None]),
        b_w_in=_weight_grad_cols("grad_b_w_in", my_slot, hb, [dqg_b],
                                 [(0, o, c, 4 * o + c) for o in range(2) for c in range(4)], 256),
        kv_w=_weight_grad_rows("grad_kv_w", my_slot, kvn, dkv_b),
        a_w_out=_weight_grad_rows("grad_a_w_out", my_slot, z_a, dh1b[None]))
    dqg_a, dkv_a, d_rel, landed = _attn_a_bwd(qkvg, a_bias, out_a, lse_a, dz_a, [g[0] for g in early.values()])
    g_w_in_a = _weight_grad_cols(
        "grad_a_w_in", my_slot, xn, [dqg_a, dkv_a],
        [(0, 0, 0, 0), (0, 0, 1, 1), (1, 0, 0, 2), (1, 0, 1, 3), (1, 1, 0, 4), (1, 1, 1, 5), (0, 1, 0, 6), (0, 1, 1, 7)], 512)
    from_sibling, = _exchange_sibling([g_w_in_a[0]])
    x_i, y_i, c_i, chips = _place()
    del x_i, y_i
    forward_slots = jnp.stack([_slot(*chip, c_i) for chip in chips]).astype(jnp.int32)
    chip_sums = _pre_reduce("chip_sum_a_w_in", g_w_in_a[0], from_sibling, forward_slots)
    grad_x, d_an, from_chips = _layer_a_in_bwd(dqg_a, dkv_a, w_in_a, x, dh1, a_gain, chip_sums)

    matrices = {n: (g[1], [(land, 0, N_DEV - 1)]) for (n, g), land in zip(early.items(), landed)}
    matrices["a_w_in"] = (g_w_in_a[1], [(from_sibling, 3, 1), (from_chips, 0, 3)])
    small = dict(
        loss=loss, a_norm=d_an, a_rel_bias=d_rel[:, :2].reshape(N_HEADS, A_DIAG),
        kv_norm=d_kn, t5_bias=d_t5, b_norm=d_bn, b_sinks=d_sink, final_norm=d_fn)
    return grad_x, small, matrices


def _place():
    x, y, c = lax.axis_index("x"), lax.axis_index("y"), lax.axis_index("c")
    chips = [(1 - x, y), (x, 1 - y), (1 - x, 1 - y)]
    return x, y, c, chips


def _slot(px, py, pc):
    return 4 * px + 2 * py + pc


ANY = pl.BlockSpec(memory_space=pl.ANY)


def _peer(x, y, c, k):
    return (x ^ (k >> 2), y ^ ((k >> 1) & 1), c ^ (k & 1))


def _scatter_copies(grad_refs, land_refs, send_sems, recv_sems):
    x, y, c, _ = _place()
    copies = []
    for t, (grad, land) in enumerate(zip(grad_refs, land_refs)):
        for k in range(1, N_DEV):
            peer = _peer(x, y, c, k)
            sem = (N_DEV - 1) * t + k - 1
            copies.append(pltpu.make_async_remote_copy(
                src_ref=grad.at[_slot(*peer)], dst_ref=land.at[k - 1],
                send_sem=send_sems.at[sem], recv_sem=recv_sems.at[sem],
                device_id=peer, device_id_type=MESH))
    return copies


def _gather_phases(ins, outs, send_sems, recv_sems, local_sems):
    n = len(ins)
    x, y, c, chips = _place()
    me, sibling = (x, y, c), (x, y, 1 - c)

    def copy(t, k, block, to, src=None):
        dst = outs[t].at[_slot(*block)]
        return pltpu.make_async_remote_copy(
            src_ref=dst if src is None else src, dst_ref=dst,
            send_sem=send_sems.at[7 * t + k], recv_sem=recv_sems.at[7 * t + k],
            device_id=to, device_id_type=MESH)

    def lists():
        mine = [pltpu.make_async_copy(ins[t], outs[t].at[_slot(*me)], local_sems.at[t]) for t in range(n)]
        first = []
        for t in range(n):
            first.append(copy(t, 0, me, sibling, src=ins[t]))
            first += [copy(t, 1 + j, me, (*chip, c), src=ins[t]) for j, chip in enumerate(chips)]
        passed = [copy(t, 4 + j, (*chip, c), sibling) for t in range(n) for j, chip in enumerate(chips)]
        return mine, first, passed

    def start():
        mine, first, _ = lists()
        for cp in mine + first:
            cp.start()

    def forward():
        _, _, passed = lists()
        for t in range(n):
            for j, chip in enumerate(chips):
                copy(t, 1 + j, (*chip, c), me).wait_recv()
                passed[3 * t + j].start()

    def finish():
        mine, first, passed = lists()
        for t in range(n):
            copy(t, 0, sibling, me).wait_recv()
            for j, chip in enumerate(chips):
                copy(t, 4 + j, (*chip, 1 - c), me).wait_recv()
        for cp in first + passed:
            cp.wait_send()
        for cp in mine:
            cp.wait()

    return start, forward, finish


def _gather_scratch(n):
    return [pltpu.SemaphoreType.DMA((7 * n,)), pltpu.SemaphoreType.DMA((7 * n,)), pltpu.SemaphoreType.DMA((n,))]


def _exchange_sibling(grads):
    n = len(grads)

    def body(*refs):
        ins, outs = refs[:n], refs[n:2 * n]
        send_sems, recv_sems = refs[2 * n:]
        x, y, c, chips = _place()
        sibling = (x, y, 1 - c)
        copies = []
        for t in range(n):
            blocks = [(*chip, 1 - c) for chip in chips] + [sibling]
            for k, block in enumerate(blocks):
                copies.append(pltpu.make_async_remote_copy(
                    src_ref=ins[t].at[_slot(*block)], dst_ref=outs[t].at[k],
                    send_sem=send_sems.at[4 * t + k], recv_sem=recv_sems.at[4 * t + k],
                    device_id=sibling, device_id_type=MESH))
        for cp in copies:
            cp.start()
        for cp in copies:
            cp.wait()

    return pl.pallas_call(
        body, name="grads_to_sibling",
        in_specs=[ANY] * n, out_specs=[ANY] * n,
        out_shape=[SDS((4, *g.shape[1:]), g.dtype) for g in grads],
        scratch_shapes=[pltpu.SemaphoreType.DMA((4 * n,)), pltpu.SemaphoreType.DMA((4 * n,))],
    )(*grads)


def _chip_copies(sums_ref, land_ref, send_sems, recv_sems):
    x, y, c, chips = _place()
    del x, y
    return [pltpu.make_async_remote_copy(
        src_ref=sums_ref.at[j], dst_ref=land_ref.at[j], send_sem=send_sems.at[j], recv_sem=recv_sems.at[j],
        device_id=(*chip, c), device_id_type=MESH) for j, chip in enumerate(chips)]


def _row_tile(rows):
    return min(rows, 256)


def _pre_reduce(name, g, from_sibling, slots):
    _, r, c = g.shape
    tr = _row_tile(r)

    def body(slots_ref, g_ref, s_ref, o_ref):
        del slots_ref
        o_ref[...] = (g_ref[...].astype(F32) + s_ref[...].astype(F32)).astype(BF16)

    return pl.pallas_call(
        body, name=name,
        grid_spec=pltpu.PrefetchScalarGridSpec(
            num_scalar_prefetch=1, grid=(3, r // tr),
            in_specs=[pl.BlockSpec((1, tr, c), lambda j, i, s: (s[j], i, 0)),
                      pl.BlockSpec((1, tr, c), lambda j, i, s: (j, i, 0))],
            out_specs=pl.BlockSpec((1, tr, c), lambda j, i, s: (j, i, 0))),
        out_shape=SDS((3, r, c), BF16),
        compiler_params=_cparams(),
    )(slots, g, from_sibling)


def _adamw(w, g, m, v):
    m2 = ADAM_B1 * m + (1.0 - ADAM_B1) * g
    v2 = ADAM_B2 * v + (1.0 - ADAM_B2) * jnp.square(g)
    m_hat = m2 / (1.0 - ADAM_B1 ** ADAM_STEP)
    v_hat = v2 / (1.0 - ADAM_B2 ** ADAM_STEP)
    delta = -ADAM_LR * (m_hat / (jnp.sqrt(v_hat) + ADAM_EPS) + ADAM_WD * w)
    return delta, m2, v2


def _reduce_adamw(name, own, partials, w, m, v):
    r, c = own.shape
    tr = _row_tile(r)
    n_p = len(partials)

    def body(own_ref, *rest):
        p_refs, (w_ref, m_ref, v_ref, grad_ref, d_ref, nm_ref, nv_ref) = rest[:n_p], rest[n_p:]
        grad = own_ref[...]
        for p_ref, (_, _, count) in zip(p_refs, partials):
            for j in range(count):
                grad = grad + p_ref[j].astype(F32)
        grad_ref[...] = grad
        d_ref[...], nm_ref[...], nv_ref[...] = _adamw(w_ref[...], grad, m_ref[...], v_ref[...])

    flat = pl.BlockSpec((tr, c), lambda i: (i, 0))
    return pl.pallas_call(
        body, name=name, grid=(r // tr,),
        in_specs=[flat] + [pl.BlockSpec((count, tr, c), lambda i, first=first, count=count: (first // count, i, 0))
                           for _, first, count in partials] + [flat, flat, flat],
        out_specs=[flat, flat, flat, flat],
        out_shape=[SDS((r, c), F32)] * 4,
        compiler_params=_cparams(),
    )(own, *[p[0] for p in partials], w, m, v)


VM = pl.BlockSpec()


def _small_allreduce(parts):
    n = len(parts)

    def body(*refs):
        ins, outs, lands = refs[:n], refs[n:2 * n], refs[2 * n:3 * n]
        send_sems, recv_sems = refs[3 * n:]
        x, y, c, _ = _place()
        my_slot = _slot(x, y, c)
        copies = []
        for t in range(n):
            lands[t][my_slot] = ins[t][...]
            for k in range(1, N_DEV):
                sem = (N_DEV - 1) * t + k - 1
                copies.append(pltpu.make_async_remote_copy(
                    src_ref=ins[t], dst_ref=lands[t].at[my_slot],
                    send_sem=send_sems.at[sem], recv_sem=recv_sems.at[sem],
                    device_id=_peer(x, y, c, k), device_id_type=MESH))
        for cp in copies:
            cp.start()
        for t in range(n):
            for k in range(1, N_DEV):
                sem = (N_DEV - 1) * t + k - 1
                pltpu.make_async_remote_copy(
                    src_ref=ins[t], dst_ref=lands[t].at[_slot(*_peer(x, y, c, k))],
                    send_sem=send_sems.at[sem], recv_sem=recv_sems.at[sem],
                    device_id=(x, y, c), device_id_type=MESH).wait_recv()
        for cp in copies:
            cp.wait_send()
        for t in range(n):
            total = lands[t][0]
            for s in range(1, N_DEV):
                total = total + lands[t][s]
            outs[t][...] = total

    n_sems = (N_DEV - 1) * n
    return pl.pallas_call(
        body, name="small_allreduce",
        in_specs=[VM] * n, out_specs=[VM] * n, out_shape=[SDS(p.shape, F32) for p in parts],
        scratch_shapes=[pltpu.VMEM((N_DEV, *p.shape), F32) for p in parts]
        + [pltpu.SemaphoreType.DMA((n_sems,)), pltpu.SemaphoreType.DMA((n_sems,))],
    )(*parts)


def _small_adamw(my_slot, sums, ws, ms, vs):
    n = len(ws)

    def body(slot_ref, *refs):
        sum_refs, refs = refs[:n + 1], refs[n + 1:]
        w_refs, m_refs, v_refs, refs = refs[:n], refs[n:2 * n], refs[2 * n:3 * n], refs[3 * n:]
        g_refs, d_refs, nm_refs, nv_refs = refs[:n + 1], refs[n + 1:2 * n + 1], refs[2 * n + 1:3 * n + 1], refs[3 * n + 1:]
        for t in range(n + 1):
            if t == 0:
                g = sum_refs[0][:, pl.ds(pl.multiple_of(slot_ref[0] * 128, 128), 128)]
            else:
                g = sum_refs[t][...]
            g_refs[t][...] = g
            if t < n:
                d_refs[t][...], nm_refs[t][...], nv_refs[t][...] = _adamw(w_refs[t][...], g, m_refs[t][...], v_refs[t][...])

    shapes = [SDS(w.shape, F32) for w in ws]
    outs = pl.pallas_call(
        body, name="small_adamw",
        in_specs=[pl.BlockSpec(memory_space=pltpu.SMEM)] + [VM] * (4 * n + 1),
        out_specs=[VM] * (4 * n + 1),
        out_shape=shapes + [SDS(sums[-1].shape, F32)] + shapes * 3,
    )(my_slot, *sums, *ws, *ms, *vs)
    return outs[:n + 1], outs[n + 1:2 * n + 1], outs[2 * n + 1:3 * n + 1], outs[3 * n + 1:]


def kernel(x, a_norm, a_w_in, a_rel_bias, a_w_out, kv_norm, kv_w, t5_bias, b_norm, b_w_in, b_sinks, b_w_out, final_norm, loss_target, m_a_norm, m_a_w_in, m_a_rel_bias, m_a_w_out, m_kv_norm, m_kv_w, m_t5_bias, m_b_norm, m_b_w_in, m_b_sinks, m_b_w_out, m_final_norm, v_a_norm, v_a_w_in, v_a_rel_bias, v_a_w_out, v_kv_norm, v_kv_w, v_t5_bias, v_b_norm, v_b_w_in, v_b_sinks, v_b_w_out, v_final_norm):
    xi, yi, ci = lax.axis_index("x"), lax.axis_index("y"), lax.axis_index("c")
    my_slot = _slot(xi, yi, ci)

    slot_arr = jnp.reshape(my_slot, (1,)).astype(jnp.int32)
    order = _gather_order(xi, yi, ci)
    late_shards = [b_w_in[0].astype(BF16), a_w_out[0].astype(BF16), b_w_out[0].astype(BF16), kv_w.astype(BF16)]
    grad_x, loc, matrices = _local_step(
        slot_arr, order, x[0], loss_target[0], a_norm, a_w_in[0].astype(BF16), a_rel_bias[0], late_shards,
        kv_norm.reshape(1, D_MODEL), t5_bias, b_norm, b_sinks, final_norm.reshape(1, D_MODEL))

    shard_w = dict(a_w_in=a_w_in[0], b_w_in=b_w_in[0], a_w_out=a_w_out[0], b_w_out=b_w_out[0], kv_w=kv_w)
    shard_m = dict(a_w_in=m_a_w_in[0], b_w_in=m_b_w_in[0], a_w_out=m_a_w_out[0], b_w_out=m_b_w_out[0], kv_w=m_kv_w)
    shard_v = dict(a_w_in=v_a_w_in[0], b_w_in=v_b_w_in[0], a_w_out=v_a_w_out[0], b_w_out=v_b_w_out[0], kv_w=v_kv_w)
    big = {n: _reduce_adamw("adamw_" + n, own, partials, shard_w[n], shard_m[n], shard_v[n])
           for n, (own, partials) in matrices.items()}

    names = ("a_norm", "a_rel_bias", "kv_norm", "t5_bias", "b_norm", "b_sinks", "final_norm")
    tables = ("a_rel_bias", "t5_bias")

    def row(n, a):
        return a.reshape(-1, a.shape[-1]).T if n in tables else a.reshape(1, -1)

    small_w = [row(n, a) for n, a in zip(names, (a_norm, a_rel_bias, kv_norm, t5_bias, b_norm, b_sinks, final_norm))]
    small_m = [row(n, a) for n, a in zip(names, (m_a_norm, m_a_rel_bias, m_kv_norm, m_t5_bias, m_b_norm, m_b_sinks,
                                                 m_final_norm))]
    small_v = [row(n, a) for n, a in zip(names, (v_a_norm, v_a_rel_bias, v_kv_norm, v_t5_bias, v_b_norm, v_b_sinks,
                                                 v_final_norm))]
    sums = dict(zip(names + ("loss",), _small_allreduce([loc[n] for n in names] + [loc["loss"]])))
    sums["a_rel_bias"] = _a_bias_grad(sums["a_rel_bias"])
    sums["t5_bias"] = sums["t5_bias"][:, :T5_BUCKETS]
    sums["b_sinks"] = sums["b_sinks"][:, 0].reshape(1, N_HEADS)
    results = _small_adamw(slot_arr, [sums[n] for n in names + ("loss",)], small_w, small_m, small_v)
    like = dict(a_norm=a_norm, a_rel_bias=a_rel_bias, kv_norm=kv_norm, t5_bias=t5_bias, b_norm=b_norm,
                b_sinks=b_sinks, final_norm=final_norm)
    sm = [{n: (part[i].T if n in tables else part[i]).reshape(like[n].shape) for i, n in enumerate(names)}
          for part in results]
    loss = results[0][len(names)][0, 0]

    order = ("a_norm", "a_w_in", "a_rel_bias", "a_w_out", "kv_norm", "kv_w", "t5_bias", "b_norm",
             "b_w_in", "b_sinks", "b_w_out", "final_norm")
    lead = dict(a_w_in=True, b_w_in=True, a_w_out=True, b_w_out=True, kv_w=False)

    def pick(kind, name):
        if name in big:
            val = big[name][kind]
            return val[None] if lead[name] else val
        return sm[kind][name]

    outs = [loss, grad_x[None]]
    for kind in range(4):
        outs += [pick(kind, n) for n in order]
    return tuple(outs)
```

```python
import functools
import math

import numpy as np
import jax
import jax.numpy as jnp
from jax import lax
from jax.experimental import pallas as pl
from jax.experimental.pallas import tpu as pltpu

F32 = jnp.float32
BF16 = jnp.bfloat16
SDS = jax.ShapeDtypeStruct

D_MODEL = 1024
HEAD_DIM = 64
CHUNK = 64
N_HEADS = 16
RMS_EPS = 1e-6
A_LEFT_CHUNKS = 8
A_BAND = (A_LEFT_CHUNKS + 1) * CHUNK
A_REL_CLIP = 256
B_KV_HEADS = 2
B_GROUP = 8
B_LEFT_CHUNKS = 2
B_BAND = (B_LEFT_CHUNKS + 1) * CHUNK
T5_BUCKETS = 32
T5_MAX_DIST = 128
QBLK = 256
A_KEYS = 3 * QBLK
B_QBLK_FWD = 128
B_QBLK_BWD = 256
B_PREV = 128
A_DIAG = A_KEYS
NEG = -1e30
SCALE = HEAD_DIM ** -0.5
N_DEV = 8

ADAM_LR = 0.001
ADAM_B1 = 0.9
ADAM_B2 = 0.999
ADAM_EPS = 1e-08
ADAM_WD = 0.01
ADAM_STEP = 10

VMEM_LIMIT_BYTES = 56 * 1024 * 1024
MESH = pl.DeviceIdType.MESH


def _cparams():
    return pltpu.CompilerParams(vmem_limit_bytes=VMEM_LIMIT_BYTES)


def _dot(a, b):
    return jnp.dot(a, b, preferred_element_type=F32)


def _dot_nt(a, b):
    return lax.dot_general(a, b, (((1,), (1,)), ((), ())), preferred_element_type=F32)


def _dot_tn(a, b):
    return lax.dot_general(a, b, (((0,), (0,)), ((), ())), preferred_element_type=F32)


def _rstd(xf):
    return lax.rsqrt(jnp.mean(xf * xf, axis=-1, keepdims=True) + RMS_EPS)


def _sigmoid(x):
    return 1.0 / (1.0 + jnp.exp(-x))


_GATHER_SEQUENCE = ((0, None), (1, 0), (2, 1), (4, None), (5, None), (3, 2), (6, None))


def _gather_order(x, y, c):
    others = [(1 - x, y), (x, 1 - y), (1 - x, 1 - y)]
    arrivals = [_slot(x, y, 1 - c)] + [_slot(*chip, c) for chip in others] + [_slot(*chip, 1 - c) for chip in others]
    return jnp.stack([_slot(x, y, c)] + [arrivals[a] for a, _ in _GATHER_SEQUENCE]).astype(jnp.int32)


def _norm_matmul_gather(order, x, gain_shard, w_shard):
    t = x.shape[0]
    dw, tn = w_shard.shape
    tm = min(t, 1024)
    n_m = t // tm

    def body(order_ref, x_ref, gs_ref, shard_ref, xn_ref, o_ref, full_ref, gain_ref,
             xn_all, wbuf, gland, send_sems, recv_sems, gsend_sems, grecv_sems, load_sems, own_sem):
        n, m = pl.program_id(0), pl.program_id(1)
        x_i, y_i, c_i, chips = _place()
        me, sibling = (x_i, y_i, c_i), (x_i, y_i, 1 - c_i)

        def send(k, block, to, src=None):
            dst = full_ref.at[_slot(*block)]
            return pltpu.make_async_remote_copy(
                src_ref=dst if src is None else src, dst_ref=dst,
                send_sem=send_sems.at[k], recv_sem=recv_sems.at[k], device_id=to, device_id_type=MESH)

        own = pltpu.make_async_copy(shard_ref, full_ref.at[_slot(*me)], own_sem)
        first = [send(0, me, sibling, src=shard_ref)]
        first += [send(1 + j, me, (*chip, c_i), src=shard_ref) for j, chip in enumerate(chips)]
        forwards = [send(4 + j, (*chip, c_i), sibling) for j, chip in enumerate(chips)]
        arrivals = [send(0, sibling, me)] + [send(1 + j, (*chip, c_i), me) for j, chip in enumerate(chips)]
        arrivals += [send(4 + j, (*chip, 1 - c_i), me) for j, chip in enumerate(chips)]
        gains = [pltpu.make_async_remote_copy(
            src_ref=gs_ref, dst_ref=gland.at[_slot(*me)], send_sem=gsend_sems.at[k - 1],
            recv_sem=grecv_sems.at[k - 1], device_id=_peer(x_i, y_i, c_i, k), device_id_type=MESH)
            for k in range(1, N_DEV)]

        @pl.when(jnp.logical_and(n == 0, m == 0))
        def _():
            own.start()
            for cp in gains + first:
                cp.start()
            pltpu.make_async_copy(shard_ref, wbuf.at[0], load_sems.at[0]).start()
            gland[_slot(*me)] = gs_ref[...]
            for k in range(1, N_DEV):
                pltpu.make_async_remote_copy(
                    src_ref=gs_ref, dst_ref=gland.at[_slot(*_peer(x_i, y_i, c_i, k))],
                    send_sem=gsend_sems.at[k - 1], recv_sem=grecv_sems.at[k - 1],
                    device_id=me, device_id_type=MESH).wait_recv()
            for s in range(N_DEV):
                gain_ref[:, 128 * s:128 * (s + 1)] = gland[s]

        rows = pl.ds(pl.multiple_of(m * tm, tm), tm)

        @pl.when(n == 0)
        def _():
            xf = x_ref[...]
            xn = ((xf * _rstd(xf)) * gain_ref[...]).astype(BF16)
            xn_all[rows, :] = xn
            xn_ref[...] = xn

        @pl.when(m == 0)
        def _():
            pltpu.make_async_copy(full_ref.at[0], wbuf.at[n % 2], load_sems.at[n % 2]).wait()

        o_ref[...] = _dot(xn_all[rows, :], wbuf[n % 2]).astype(BF16)

        for k, (arrival, forward) in enumerate(_GATHER_SEQUENCE):
            @pl.when(jnp.logical_and(n == k, m == n_m - 1))
            def _(k=k, arrival=arrival, forward=forward):
                arrivals[arrival].wait_recv()
                if forward is not None:
                    forwards[forward].start()
                pltpu.make_async_copy(full_ref.at[order_ref[k + 1]], wbuf.at[(k + 1) % 2],
                                      load_sems.at[(k + 1) % 2]).start()

        @pl.when(jnp.logical_and(n == N_DEV - 1, m == n_m - 1))
        def _():
            for cp in gains + first + forwards:
                cp.wait_send()
            own.wait()

    held = lambda n, m, order: (jnp.where(n == 0, m, n_m - 1), 0)
    return pl.pallas_call(
        body, name="norm_matmul_gather",
        grid_spec=pltpu.PrefetchScalarGridSpec(
            num_scalar_prefetch=1, grid=(N_DEV, n_m),
            in_specs=[pl.BlockSpec((tm, D_MODEL), held),
                      pl.BlockSpec((1, 128), lambda n, m, order: (0, 0)), ANY],
            out_specs=[pl.BlockSpec((tm, D_MODEL), held),
                       pl.BlockSpec((tm, tn), lambda n, m, order: (m, order[n])),
                       ANY, pl.BlockSpec((1, D_MODEL), lambda n, m, order: (0, 0))],
            scratch_shapes=[pltpu.VMEM((t, D_MODEL), BF16), pltpu.VMEM((2, dw, tn), BF16),
                            pltpu.VMEM((N_DEV, 1, 128), F32),
                            pltpu.SemaphoreType.DMA((7,)), pltpu.SemaphoreType.DMA((7,)),
                            pltpu.SemaphoreType.DMA((7,)), pltpu.SemaphoreType.DMA((7,)),
                            pltpu.SemaphoreType.DMA((2,)), pltpu.SemaphoreType.DMA]),
        out_shape=[SDS((t, D_MODEL), BF16), SDS((t, N_DEV * tn), BF16), SDS((N_DEV, dw, tn), BF16),
                   SDS((1, D_MODEL), F32)],
        compiler_params=_cparams(),
    )(order, x, gain_shard, w_shard)


def _layer_a_out(x, z, w_out, kv_gain, b_gain, kv_w, w_in_b):
    t = x.shape[0]
    tm = min(t, 512)
    nb, _, tn = w_in_b.shape

    def body(x_ref, z_ref, wo_ref, kvg_ref, bg_ref, kvw_ref, wb_ref,
             h1_ref, kvn_ref, hb_ref, kv_ref, qg_ref):
        h1 = x_ref[...] + _dot(z_ref[...], wo_ref[...])
        h1_ref[...] = h1
        y0 = h1 * _rstd(h1)
        kvn = (y0 * kvg_ref[...]).astype(BF16)
        hb = (y0 * bg_ref[...]).astype(BF16)
        kvn_ref[...] = kvn
        hb_ref[...] = hb
        kv_ref[...] = _dot(kvn, kvw_ref[...]).astype(BF16)
        for i in range(nb):
            qg_ref[:, i * tn:(i + 1) * tn] = _dot(hb, wb_ref[i]).astype(BF16)

    row = lambda m: (m, 0)
    fix2 = lambda m: (0, 0)
    return pl.pallas_call(
        body, name="layer_a_out", grid=(t // tm,),
        in_specs=[pl.BlockSpec((tm, D_MODEL), row), pl.BlockSpec((tm, D_MODEL), row),
                  pl.BlockSpec((D_MODEL, D_MODEL), fix2),
                  pl.BlockSpec((1, D_MODEL), fix2), pl.BlockSpec((1, D_MODEL), fix2),
                  pl.BlockSpec((D_MODEL, 256), fix2),
                  pl.BlockSpec((nb, D_MODEL, tn), lambda m: (0, 0, 0))],
        out_specs=[pl.BlockSpec((tm, D_MODEL), row), pl.BlockSpec((tm, D_MODEL), row),
                   pl.BlockSpec((tm, D_MODEL), row), pl.BlockSpec((tm, 256), row),
                   pl.BlockSpec((tm, nb * tn), row)],
        out_shape=[SDS((t, D_MODEL), F32), SDS((t, D_MODEL), BF16), SDS((t, D_MODEL), BF16),
                   SDS((t, 256), BF16), SDS((t, nb * tn), BF16)],
        compiler_params=_cparams(),
    )(x, z, w_out, kv_gain, b_gain, kv_w, w_in_b)


def _layer_b_out_loss(h1, z, w_out, f_gain, target):
    t = h1.shape[0]
    tm = min(t, 512)

    def body(h1_ref, z_ref, wo_ref, fg_ref, tgt_ref,
             dh2_ref, dh2b_ref, dz_ref, loss_ref, dfn_ref):
        @pl.when(pl.program_id(0) == 0)
        def _():
            loss_ref[...] = jnp.zeros_like(loss_ref)
            dfn_ref[...] = jnp.zeros_like(dfn_ref)

        h2 = h1_ref[...] + _dot(z_ref[...], wo_ref[...])
        r = _rstd(h2)
        yn = h2 * r
        fg = fg_ref[...]
        err = yn * fg - tgt_ref[...]
        loss_ref[...] += (0.5 / D_MODEL) * jnp.sum(err * err)
        dy = err * (1.0 / D_MODEL)
        dfn_ref[...] += jnp.sum(dy * yn, axis=0, keepdims=True)
        u = dy * fg
        dh2 = r * u - h2 * ((r * r * r) * jnp.mean(u * h2, axis=-1, keepdims=True))
        dh2_ref[...] = dh2
        dh2b = dh2.astype(BF16)
        dh2b_ref[...] = dh2b
        dz_ref[...] = _dot_nt(dh2b, wo_ref[...]).astype(BF16)

    row = lambda m: (m, 0)
    fix2 = lambda m: (0, 0)
    return pl.pallas_call(
        body, name="layer_b_out_loss", grid=(t // tm,),
        in_specs=[pl.BlockSpec((tm, D_MODEL), row), pl.BlockSpec((tm, D_MODEL), row),
                  pl.BlockSpec((D_MODEL, D_MODEL), fix2), pl.BlockSpec((1, D_MODEL), fix2),
                  pl.BlockSpec((tm, D_MODEL), row)],
        out_specs=[pl.BlockSpec((tm, D_MODEL), row), pl.BlockSpec((tm, D_MODEL), row),
                   pl.BlockSpec((tm, D_MODEL), row), pl.BlockSpec((1, 128), fix2),
                   pl.BlockSpec((1, D_MODEL), fix2)],
        out_shape=[SDS((t, D_MODEL), F32), SDS((t, D_MODEL), BF16), SDS((t, D_MODEL), BF16),
                   SDS((1, 128), F32), SDS((1, D_MODEL), F32)],
        compiler_params=_cparams(),
    )(h1, z, w_out, f_gain, target)


def _layer_b_in_bwd(dqg, dkv, w_in_b, kv_w, h1, dh2, b_gain, kv_gain, w_out_a):
    t = h1.shape[0]
    tm = min(t, 512)
    nb, _, tn = w_in_b.shape
    per = D_MODEL // tn

    def body(dqg_ref, dkv_ref, wb_ref, kvw_ref, h1_ref, dh2_ref, bg_ref, kvg_ref, wo_ref,
             dh1_ref, dh1b_ref, dz_ref, dbn_ref, dkn_ref):
        @pl.when(pl.program_id(0) == 0)
        def _():
            dbn_ref[...] = jnp.zeros_like(dbn_ref)
            dkn_ref[...] = jnp.zeros_like(dkn_ref)

        dhb = jnp.zeros((tm, D_MODEL), F32)
        for i in range(nb):
            blk = dqg_ref[i // per, :, (i % per) * tn:(i % per + 1) * tn]
            dhb = dhb + _dot_nt(blk, wb_ref[i])
        dkn = (_dot_nt(dkv_ref[0].astype(BF16), kvw_ref[:, 0:128])
               + _dot_nt(dkv_ref[1].astype(BF16), kvw_ref[:, 128:256]))
        h1 = h1_ref[...]
        r = _rstd(h1)
        xr = h1 * r
        dbn_ref[...] += jnp.sum(dhb * xr, axis=0, keepdims=True)
        dkn_ref[...] += jnp.sum(dkn * xr, axis=0, keepdims=True)
        u = dhb * bg_ref[...] + dkn * kvg_ref[...]
        dh1 = dh2_ref[...] + r * u - h1 * ((r * r * r) * jnp.mean(u * h1, axis=-1, keepdims=True))
        dh1_ref[...] = dh1
        dh1b = dh1.astype(BF16)
        dh1b_ref[...] = dh1b
        dz_ref[...] = _dot_nt(dh1b, wo_ref[...]).astype(BF16)

    row = lambda m: (m, 0)
    fix2 = lambda m: (0, 0)
    return pl.pallas_call(
        body, name="layer_b_in_bwd", grid=(t // tm,),
        in_specs=[pl.BlockSpec((2, tm, D_MODEL), lambda m: (0, m, 0)),
                  pl.BlockSpec((2, tm, 128), lambda m: (0, m, 0)),
                  pl.BlockSpec((nb, D_MODEL, tn), lambda m: (0, 0, 0)),
                  pl.BlockSpec((D_MODEL, 256), fix2),
                  pl.BlockSpec((tm, D_MODEL), row), pl.BlockSpec((tm, D_MODEL), row),
                  pl.BlockSpec((1, D_MODEL), fix2), pl.BlockSpec((1, D_MODEL), fix2),
                  pl.BlockSpec((D_MODEL, D_MODEL), fix2)],
        out_specs=[pl.BlockSpec((tm, D_MODEL), row), pl.BlockSpec((tm, D_MODEL), row),
                   pl.BlockSpec((tm, D_MODEL), row), pl.BlockSpec((1, D_MODEL), fix2),
                   pl.BlockSpec((1, D_MODEL), fix2)],
        out_shape=[SDS((t, D_MODEL), F32), SDS((t, D_MODEL), BF16), SDS((t, D_MODEL), BF16),
                   SDS((1, D_MODEL), F32), SDS((1, D_MODEL), F32)],
        compiler_params=_cparams(),
    )(dqg, dkv, w_in_b, kv_w, h1, dh2, b_gain, kv_gain, w_out_a)


def _layer_a_in_bwd(dqg, dkv, w_in_a, x, dh1, a_gain, chip_sums):
    t = x.shape[0]
    tm = min(t, 512)
    nb, _, tn = w_in_a.shape
    per = D_MODEL // tn

    def body(dqg_ref, dkv_ref, w_ref, x_ref, dh1_ref, ag_ref, sums_ref, dx_ref, dan_ref, land_ref,
             send_sems, recv_sems):
        @pl.when(pl.program_id(0) == 0)
        def _():
            dan_ref[...] = jnp.zeros_like(dan_ref)
            for cp in _chip_copies(sums_ref, land_ref, send_sems, recv_sems):
                cp.start()

        dxn = jnp.zeros((tm, D_MODEL), F32)
        for i in range(nb):
            part = i // per
            src = dqg_ref if part in (0, 3) else dkv_ref
            outer = {0: 0, 3: 1, 1: 0, 2: 1}[part]
            blk = src[outer, :, (i % per) * tn:(i % per + 1) * tn]
            dxn = dxn + _dot_nt(blk, w_ref[i])
        xf = x_ref[...]
        r = _rstd(xf)
        dan_ref[...] += jnp.sum(dxn * (xf * r), axis=0, keepdims=True)
        u = dxn * ag_ref[...]
        dx_ref[...] = dh1_ref[...] + r * u - xf * ((r * r * r) * jnp.mean(u * xf, axis=-1, keepdims=True))

        @pl.when(pl.program_id(0) == t // tm - 1)
        def _():
            for cp in _chip_copies(sums_ref, land_ref, send_sems, recv_sems):
                cp.wait()

    row = lambda m: (m, 0)
    fix2 = lambda m: (0, 0)
    return pl.pallas_call(
        body, name="layer_a_in_bwd", grid=(t // tm,),
        in_specs=[pl.BlockSpec((2, tm, D_MODEL), lambda m: (0, m, 0)),
                  pl.BlockSpec((2, tm, D_MODEL), lambda m: (0, m, 0)),
                  pl.BlockSpec((nb, D_MODEL, tn), lambda m: (0, 0, 0)),
                  pl.BlockSpec((tm, D_MODEL), row), pl.BlockSpec((tm, D_MODEL), row),
                  pl.BlockSpec((1, D_MODEL), fix2), ANY],
        out_specs=[pl.BlockSpec((tm, D_MODEL), row), pl.BlockSpec((1, D_MODEL), fix2), ANY],
        out_shape=[SDS((t, D_MODEL), F32), SDS((1, D_MODEL), F32), SDS(chip_sums.shape, chip_sums.dtype)],
        scratch_shapes=[pltpu.SemaphoreType.DMA((3,)), pltpu.SemaphoreType.DMA((3,))],
        compiler_params=_cparams(),
    )(dqg, dkv, w_in_a, x, dh1, a_gain, chip_sums)


def _lut(s, vals):
    r = jnp.int32(vals[0])
    for i in range(1, len(vals)):
        r = jnp.where(s == i, jnp.int32(vals[i]), r)
    return r


def _held(steps, i):
    seq, cur = [None] * len(steps), None
    for k in range(len(steps) - 1, -1, -1):
        if steps[k][0] == i:
            cur = steps[k][1:3]
        seq[k] = cur
    for k in range(len(steps)):
        cur = seq[k] = seq[k] if seq[k] is not None else cur
    return seq


def _weight_grad_cols(name, my_slot, a, bs, steps, tn):
    t, dw = a.shape
    n_arr = len(bs)
    which = [s[0] for s in steps]
    blks = [s[3] for s in steps]

    def body(slot_ref, a_ref, *rest):
        b_refs, (o_ref, own_ref, at_ref) = rest[:n_arr], rest[n_arr:]
        s = pl.program_id(0)

        @pl.when(s == 0)
        def _():
            at_ref[...] = a_ref[...].T

        for i in range(n_arr):
            @pl.when(_lut(s, which) == i)
            def _(i=i):
                res = _dot(at_ref[...], b_refs[i][0])
                o_ref[0] = res.astype(BF16)

                @pl.when(_lut(s, blks) == slot_ref[0])
                def _():
                    own_ref[...] = res

    def b_spec(i):
        held = _held(steps, i)
        return pl.BlockSpec((1, t, tn), lambda s, slot: (_lut(s, [h[0] for h in held]), 0,
                                                         _lut(s, [h[1] for h in held])))

    return pl.pallas_call(
        body, name=name,
        grid_spec=pltpu.PrefetchScalarGridSpec(
            num_scalar_prefetch=1, grid=(len(steps),),
            in_specs=[pl.BlockSpec((t, dw), lambda s, slot: (0, 0))] + [b_spec(i) for i in range(n_arr)],
            out_specs=[pl.BlockSpec((1, dw, tn), lambda s, slot: (_lut(s, blks), 0, 0)),
                       pl.BlockSpec((dw, tn), lambda s, slot: (0, 0))],
            scratch_shapes=[pltpu.VMEM((dw, t), BF16)]),
        out_shape=[SDS((N_DEV, dw, tn), BF16), SDS((dw, tn), F32)],
        compiler_params=_cparams(),
    )(my_slot, a, *bs)


def _weight_grad_rows(name, my_slot, a, b):
    t, dw = a.shape
    n_o, _, c = b.shape
    rows = dw // N_DEV
    tn = min(c, 256)
    per = c // tn

    def body(slot_ref, a_ref, b_ref, o_ref, own_ref, at_ref, res_ref):
        @pl.when(pl.program_id(0) == 0)
        def _():
            at_ref[...] = a_ref[...].T

        res_ref[...] = _dot(at_ref[...], b_ref[0].astype(BF16))
        o_ref[...] = res_ref[...].astype(BF16)
        own_ref[...] = res_ref[pl.ds(pl.multiple_of(slot_ref[0] * rows, rows), rows), :]

    all_rows, own = pl.pallas_call(
        body, name=name,
        grid_spec=pltpu.PrefetchScalarGridSpec(
            num_scalar_prefetch=1, grid=(n_o * per,),
            in_specs=[pl.BlockSpec((t, dw), lambda s, slot: (0, 0)),
                      pl.BlockSpec((1, t, tn), lambda s, slot: (s // per, 0, s % per))],
            out_specs=[pl.BlockSpec((dw, tn), lambda s, slot: (0, s)),
                       pl.BlockSpec((rows, tn), lambda s, slot: (0, s))],
            scratch_shapes=[pltpu.VMEM((dw, t), BF16), pltpu.VMEM((dw, tn), F32)]),
        out_shape=[SDS((dw, n_o * c), BF16), SDS((rows, n_o * c), F32)],
        compiler_params=_cparams(),
    )(my_slot, a, b)
    return all_rows.reshape(N_DEV, rows, n_o * c), own


def _lane_lo():
    return lax.broadcasted_iota(jnp.int32, (1, 128), 1) < HEAD_DIM


def _offset_sums(gt):
    keys = gt.shape[1]
    gc = gt[0:CHUNK]
    for cc in range(1, gt.shape[0] // CHUNK):
        gc = gc + pltpu.roll(gt[cc * CHUNK:(cc + 1) * CHUNK], keys - cc * CHUNK, 1)
    hi = gc.astype(BF16)
    lo = (gc - hi.astype(F32)).astype(BF16)
    flip = (lax.broadcasted_iota(jnp.int32, (CHUNK, CHUNK), 0)
            + lax.broadcasted_iota(jnp.int32, (CHUNK, CHUNK), 1) == CHUNK - 1).astype(BF16)
    gf = _dot(flip, hi) + _dot(flip, lo)
    skew = pltpu.roll(gf, 0, 1, stride=1, stride_axis=0)
    return jnp.sum(skew, axis=0, keepdims=True)


def _band_bias(w_row, band, rows):
    keys = w_row.shape[1]
    base = jnp.broadcast_to(w_row, (CHUNK, keys))
    skew = pltpu.roll(base, 0, 1, stride=1, stride_axis=0)
    skew = pltpu.roll(skew, keys - (CHUNK - 1), 1)
    col = lax.broadcasted_iota(jnp.int32, (CHUNK, keys), 1)
    chunk0 = jnp.where(col < band, skew, NEG)
    return jnp.concatenate(
        [chunk0] + [pltpu.roll(chunk0, cc * CHUNK, 1) for cc in range(1, rows // CHUNK)], axis=0)


def _silu_parts(g):
    sg = _sigmoid(g)
    return g * sg, sg * (1.0 + g * (1.0 - sg))


A_PAIRS = 2
A_LANES = 128 * A_PAIRS
A_STEPS = D_MODEL // A_LANES


def _a_specs():
    q = pl.BlockSpec((QBLK, A_LANES), lambda p, j: (j, p))
    ks = [pl.BlockSpec((QBLK, A_LANES), lambda p, j, b=b: (jnp.maximum(j - 2 + b, 0), A_STEPS + p)) for b in range(3)]
    vs = [pl.BlockSpec((QBLK, A_LANES), lambda p, j, b=b: (jnp.maximum(j - 2 + b, 0), 2 * A_STEPS + p))
          for b in range(3)]
    g = pl.BlockSpec((QBLK, A_LANES), lambda p, j: (j, 3 * A_STEPS + p))
    bias = pl.BlockSpec((A_PAIRS, 8, A_KEYS), lambda p, j: (p, 0, 0))
    return q, ks, vs, g, bias


def _a_fill_bias(w_ref, b_ref, j):
    _fill_bias(2 * A_PAIRS, lambda h: w_ref[h // 2, h % 2:h % 2 + 1, :], A_BAND, 0, 0, b_ref, j)


def _by_valid_key_blocks(j, fn):
    pl.when(j == 0)(functools.partial(fn, 1))
    pl.when(j == 1)(functools.partial(fn, 2))
    pl.when(j >= 2)(functools.partial(fn, 3))


def _fill_bias(n, get_row, band, first_valid_col, early, bias_scr, j):
    @pl.when(j == 0)
    def _():
        for h in range(n):
            bias_scr[h] = _band_bias(get_row(h), band, bias_scr.shape[1])

    if early:
        @pl.when(j < early)
        def _():
            keys = bias_scr.shape[2]
            col_ok = lax.broadcasted_iota(jnp.int32, (1, keys), 1) >= first_valid_col
            for h in range(n):
                bias_scr[n + h] = jnp.where(col_ok, bias_scr[h], NEG)


def _head_logits(q, k, bias_scr, idx, sel):
    qm = jnp.where(sel, q, jnp.zeros_like(q)) * SCALE
    return qm, _dot_nt(qm, k) + bias_scr[idx]


def _row_sums_everywhere(r, sel):
    return jnp.where(sel, pltpu.roll(r, HEAD_DIM, 1), r)


def _own_everywhere(x, sel):
    return jnp.where(sel, x, pltpu.roll(x, HEAD_DIM, 1))


def _minus_rows(s, row_full):
    return jnp.concatenate([s[:, i:i + 128] - row_full for i in range(0, s.shape[1], 128)], axis=1)


def _attn_a_fwd(qkvg, bias, gather):
    t = qkvg.shape[0]
    nq = t // QBLK
    n_g = len(gather)
    q_spec, k_specs, v_specs, g_spec, bias_spec = _a_specs()

    def body(q_ref, k0, k1, k2, v0, v1, v2, g_ref, w_ref, *rest):
        shard_refs, rest = rest[:n_g], rest[n_g:]
        z_ref, o_ref, lse_ref = rest[:3]
        full_refs, (b_ref, *comm) = rest[3:3 + n_g], rest[3 + n_g:]
        p = pl.program_id(0)
        j = pl.program_id(1)
        start, forward, finish = _gather_phases(shard_refs, full_refs, *comm)
        pl.when(jnp.logical_and(p == 0, j == 0))(start)
        pl.when(jnp.logical_and(p == A_STEPS // 2, j == 0))(forward)
        _a_fill_bias(w_ref, b_ref, j)
        lane_lo = _lane_lo()
        sels = (lane_lo, jnp.logical_not(lane_lo))

        def attend(n_blocks):
            first_col = (3 - n_blocks) * QBLK
            for pp in range(A_PAIRS):
                cols = slice(128 * pp, 128 * (pp + 1))
                k = jnp.concatenate([r[:, cols] for r in (k0, k1, k2)[3 - n_blocks:]], axis=0)
                v = jnp.concatenate([r[:, cols] for r in (v0, v1, v2)[3 - n_blocks:]], axis=0)
                q = q_ref[:, cols]
                qm2 = jnp.concatenate([jnp.where(sel, q, jnp.zeros_like(q)) for sel in sels], axis=0) * SCALE
                s2 = _dot_nt(qm2, k)
                outs, lses = [], []
                for hh, sel in enumerate(sels):
                    s = s2[hh * QBLK:(hh + 1) * QBLK] + b_ref[2 * pp + hh, :, first_col:]
                    mx = jnp.max(s, axis=-1, keepdims=True)
                    e = jnp.exp(s - mx).astype(BF16)
                    r = _dot(e, jnp.where(sel, v, jnp.ones_like(v)))
                    l = _row_sums_everywhere(r, sel)
                    outs.append(r / l)
                    lses.append(mx + jnp.log(l))
                o = jnp.where(lane_lo, outs[0], outs[1])
                silu, _ = _silu_parts(g_ref[:, cols].astype(F32))
                o_ref[:, cols] = o.astype(BF16)
                z_ref[:, cols] = (o * silu).astype(BF16)
                lse_ref[:, cols] = jnp.where(lane_lo, lses[0], lses[1])

        _by_valid_key_blocks(j, attend)
        pl.when(jnp.logical_and(p == A_STEPS - 1, j == nq - 1))(finish)

    out_spec = pl.BlockSpec((QBLK, A_LANES), lambda p, j: (j, p))
    outs = pl.pallas_call(
        body, name="attn_a_fwd", grid=(A_STEPS, nq),
        in_specs=[q_spec, *k_specs, *v_specs, g_spec, bias_spec] + [ANY] * n_g,
        out_specs=[out_spec, out_spec, out_spec] + [ANY] * n_g,
        out_shape=[SDS((t, D_MODEL), BF16), SDS((t, D_MODEL), BF16), SDS((t, D_MODEL), F32)]
        + [SDS((N_DEV, *s.shape), s.dtype) for s in gather],
        scratch_shapes=[pltpu.VMEM((2 * A_PAIRS, QBLK, A_KEYS), F32)] + _gather_scratch(n_g),
        compiler_params=_cparams(),
    )(qkvg, qkvg, qkvg, qkvg, qkvg, qkvg, qkvg, qkvg, bias, *gather)
    return outs[0], outs[1], outs[2], list(outs[3:])


def _attn_a_bwd(qkvg, bias, out_a, lse, dz, scatter):
    t = qkvg.shape[0]
    nq = t // QBLK
    n_sc = len(scatter)
    q_spec, k_specs, v_specs, g_spec, bias_spec = _a_specs()

    def body(q_ref, k0, k1, k2, v0, v1, v2, g_ref, w_ref, o_ref, lse_ref, dz_ref, *rest):
        sc_refs, rest = rest[:n_sc], rest[n_sc:]
        dqg_ref, dkv_ref, dg_ref = rest[:3]
        land_refs, rest = rest[3:3 + n_sc], rest[3 + n_sc:]
        dk_acc, dv_acc, gt_acc, b_ref, send_sems, recv_sems = rest
        j = pl.program_id(1)
        first = jnp.logical_and(pl.program_id(0) == 0, j == 0)
        last = jnp.logical_and(pl.program_id(0) == A_STEPS - 1, j == nq - 1)

        @pl.when(first)
        def _():
            for cp in _scatter_copies(sc_refs, land_refs, send_sems, recv_sems):
                cp.start()

        _a_fill_bias(w_ref, b_ref, j)

        @pl.when(j == 0)
        def _():
            dk_acc[...] = jnp.zeros_like(dk_acc)
            dv_acc[...] = jnp.zeros_like(dv_acc)
            gt_acc[...] = jnp.zeros_like(gt_acc)

        lane_lo = _lane_lo()
        sels = (lane_lo, jnp.logical_not(lane_lo))

        def attend(n_blocks):
            first_col = (3 - n_blocks) * QBLK
            for pp in range(A_PAIRS):
                cols = slice(128 * pp, 128 * (pp + 1))
                q = q_ref[:, cols]
                k = jnp.concatenate([r[:, cols] for r in (k0, k1, k2)[3 - n_blocks:]], axis=0)
                v = jnp.concatenate([r[:, cols] for r in (v0, v1, v2)[3 - n_blocks:]], axis=0)
                o = o_ref[:, cols].astype(F32)
                lse_pair = lse_ref[:, cols]
                dzf = dz_ref[:, cols].astype(F32)
                silu, dsilu = _silu_parts(g_ref[:, cols].astype(F32))
                do = dzf * silu
                dqg_ref[1, :, cols] = (dzf * o * dsilu).astype(BF16)
                doo = do * o
                qm2 = jnp.concatenate([jnp.where(sel, q, jnp.zeros_like(q)) for sel in sels], axis=0) * SCALE
                dom2 = jnp.concatenate([jnp.where(sel, do, 0.0) for sel in sels], axis=0).astype(BF16)
                s2 = _dot_nt(qm2, k)
                dp2 = _dot_nt(dom2, v)
                ps, dss = [], []
                for hh, sel in enumerate(sels):
                    rows = slice(hh * QBLK, (hh + 1) * QBLK)
                    s = s2[rows] + b_ref[2 * pp + hh, :, first_col:]
                    p = jnp.exp(_minus_rows(s, _own_everywhere(lse_pair, sel)))
                    delta = jnp.sum(jnp.where(sel, doo, 0.0), axis=-1, keepdims=True)
                    ds = p * (dp2[rows] - delta)
                    gt_acc[2 * pp + hh, :, first_col:] += ds
                    ps.append(p.astype(BF16))
                    dss.append(ds.astype(BF16))
                dsb2 = jnp.concatenate(dss, axis=0)
                dq2 = _dot(dsb2, k) * SCALE
                dk_blk = _dot_tn(dsb2, qm2)
                dv_blk = _dot_tn(jnp.concatenate(ps, axis=0), dom2)
                dqg_ref[0, :, cols] = jnp.where(lane_lo, dq2[0:QBLK], dq2[QBLK:2 * QBLK]).astype(BF16)
                for b in range(n_blocks):
                    rows = pl.ds(pl.multiple_of((j - n_blocks + 1 + b) * QBLK, QBLK), QBLK)
                    dk_acc[rows, cols] += dk_blk[b * QBLK:(b + 1) * QBLK]
                    dv_acc[rows, cols] += dv_blk[b * QBLK:(b + 1) * QBLK]

        _by_valid_key_blocks(j, attend)

        @pl.when(j == nq - 1)
        def _():
            dkv_ref[0] = dk_acc[...].astype(BF16)
            dkv_ref[1] = dv_acc[...].astype(BF16)
            for pp in range(A_PAIRS):
                dg_ref[pp] = jnp.concatenate([_offset_sums(gt_acc[2 * pp]), _offset_sums(gt_acc[2 * pp + 1]),
                                              jnp.zeros((6, A_DIAG), F32)], axis=0)

        @pl.when(last)
        def _():
            for cp in _scatter_copies(sc_refs, land_refs, send_sems, recv_sems):
                cp.wait()

    blk = pl.BlockSpec((QBLK, A_LANES), lambda p, j: (j, p))
    outs = pl.pallas_call(
        body, name="attn_a_bwd", grid=(A_STEPS, nq),
        in_specs=[q_spec, *k_specs, *v_specs, g_spec, bias_spec, blk, blk, blk] + [ANY] * n_sc,
        out_specs=[pl.BlockSpec((2, QBLK, A_LANES), lambda p, j: (0, j, p)),
                   pl.BlockSpec((2, t, A_LANES), lambda p, j: (0, 0, p)),
                   pl.BlockSpec((A_PAIRS, 8, A_DIAG), lambda p, j: (p, 0, 0))] + [ANY] * n_sc,
        out_shape=[SDS((2, t, D_MODEL), BF16), SDS((2, t, D_MODEL), BF16), SDS((N_HEADS // 2, 8, A_DIAG), F32)]
        + [SDS((N_DEV - 1, *g.shape[1:]), g.dtype) for g in scatter],
        scratch_shapes=[pltpu.VMEM((t, A_LANES), F32), pltpu.VMEM((t, A_LANES), F32),
                        pltpu.VMEM((2 * A_PAIRS, QBLK, A_KEYS), F32), pltpu.VMEM((2 * A_PAIRS, QBLK, A_KEYS), F32),
                        pltpu.SemaphoreType.DMA(((N_DEV - 1) * n_sc,)),
                        pltpu.SemaphoreType.DMA(((N_DEV - 1) * n_sc,))],
        compiler_params=_cparams(),
    )(qkvg, qkvg, qkvg, qkvg, qkvg, qkvg, qkvg, qkvg, bias, out_a, lse, dz, *scatter)
    return outs[0], outs[1], outs[2], list(outs[3:])


def _b_specs(qblk):
    per = qblk // B_PREV
    q = pl.BlockSpec((qblk, 512), lambda h, j: (j, h))
    g = pl.BlockSpec((qblk, 512), lambda h, j: (j, 2 + h))
    kp = pl.BlockSpec((B_PREV, 128), lambda h, j: (jnp.maximum(per * j - 1, 0), 0))
    kc = pl.BlockSpec((qblk, 128), lambda h, j: (j, 0))
    vp = pl.BlockSpec((B_PREV, 128), lambda h, j: (jnp.maximum(per * j - 1, 0), 1))
    vc = pl.BlockSpec((qblk, 128), lambda h, j: (j, 1))
    bias = pl.BlockSpec((B_GROUP, qblk + B_PREV), lambda h, j: (h, 0))
    sinks = pl.BlockSpec(memory_space=pltpu.SMEM)
    return q, g, kp, kc, vp, vc, bias, sinks


def _b_operands(kp, kc, vp, vc, kvh):
    k = jnp.concatenate([kp[...], kc[...]], axis=0)
    v = jnp.concatenate([vp[...], vc[...]], axis=0)
    kr = pltpu.roll(k, HEAD_DIM, 1)
    vr = pltpu.roll(v, HEAD_DIM, 1)
    first = kvh == 0
    return (jnp.where(first, k, kr), jnp.where(first, kr, k),
            jnp.where(first, v, vr), jnp.where(first, vr, v))


def _attn_b_fwd(qg, kv, bias, sinks):
    t = qg.shape[0]
    qblk = B_QBLK_FWD
    q_spec, g_spec, kp_spec, kc_spec, vp_spec, vc_spec, bias_spec, sink_spec = _b_specs(qblk)

    def body(q_ref, g_ref, kp, kc, vp, vc, w_ref, sink_ref, z_ref, o_ref, lse_ref, b_ref):
        kvh = pl.program_id(0)
        j = pl.program_id(1)
        _fill_bias(B_GROUP, lambda h: w_ref[h:h + 1, :], B_BAND, B_PREV, 1, b_ref, j)
        early = (j < 1).astype(jnp.int32)
        lane_lo = _lane_lo()
        k_lo, k_hi, v_lo, v_hi = _b_operands(kp, kc, vp, vc, kvh)
        n_pairs = B_GROUP // 2
        halves = []
        for hh, sel in enumerate((lane_lo, jnp.logical_not(lane_lo))):
            kk = k_lo if hh == 0 else k_hi
            vv = v_lo if hh == 0 else v_hi
            qm4 = jnp.concatenate([jnp.where(sel, q_ref[:, 128 * pp:128 * (pp + 1)], jnp.zeros((qblk, 128), BF16))
                                   for pp in range(n_pairs)], axis=0) * SCALE
            s4 = _dot_nt(qm4, kk)
            es, mxs = [], []
            for pp in range(n_pairs):
                g = 2 * pp + hh
                s = s4[pp * qblk:(pp + 1) * qblk] + b_ref[g + B_GROUP * early]
                mxs.append(jnp.maximum(jnp.max(s, axis=-1, keepdims=True), sink_ref[kvh * B_GROUP + g]))
                es.append(jnp.exp(s - mxs[pp]).astype(BF16))
            r4 = _dot(jnp.concatenate(es, axis=0), jnp.where(sel, vv, jnp.ones_like(vv)))
            outs, lses = [], []
            for pp in range(n_pairs):
                r = r4[pp * qblk:(pp + 1) * qblk]
                l = _row_sums_everywhere(r, sel) + jnp.exp(sink_ref[kvh * B_GROUP + 2 * pp + hh] - mxs[pp])
                outs.append(r / l)
                lses.append(mxs[pp] + jnp.log(l))
            halves.append((outs, lses))
        for pp in range(n_pairs):
            cols = slice(128 * pp, 128 * (pp + 1))
            o = jnp.where(lane_lo, halves[0][0][pp], halves[1][0][pp])
            silu, _ = _silu_parts(g_ref[:, cols].astype(F32))
            o_ref[:, cols] = o.astype(BF16)
            z_ref[:, cols] = (o * silu).astype(BF16)
            lse_ref[:, cols] = jnp.where(lane_lo, halves[0][1][pp], halves[1][1][pp])

    out_spec = pl.BlockSpec((qblk, 512), lambda h, j: (j, h))
    return pl.pallas_call(
        body, name="attn_b_fwd", grid=(B_KV_HEADS, t // qblk),
        in_specs=[q_spec, g_spec, kp_spec, kc_spec, vp_spec, vc_spec, bias_spec, sink_spec],
        out_specs=[out_spec, out_spec, out_spec],
        out_shape=[SDS((t, D_MODEL), BF16), SDS((t, D_MODEL), BF16), SDS((t, D_MODEL), F32)],
        scratch_shapes=[pltpu.VMEM((2 * B_GROUP, qblk, qblk + B_PREV), F32)],
        compiler_params=_cparams(),
    )(qg, qg, kv, kv, kv, kv, bias, sinks)


def _attn_b_bwd(qg, kv, bias, sinks, out_b, lse, dz, bucket_onehot):
    t = qg.shape[0]
    qblk = B_QBLK_BWD
    keys = qblk + B_PREV
    nq = t // qblk
    q_spec, g_spec, kp_spec, kc_spec, vp_spec, vc_spec, bias_spec, sink_spec = _b_specs(qblk)

    def body(q_ref, g_ref, kp, kc, vp, vc, w_ref, sink_ref, o_ref, lse_ref, dz_ref, oh_ref,
             dqg_ref, dkv_ref, dt5_ref, dsink_ref, gt_acc, b_ref):
        kvh = pl.program_id(0)
        j = pl.program_id(1)
        _fill_bias(B_GROUP, lambda h: w_ref[h:h + 1, :], B_BAND, B_PREV, 1, b_ref, j)

        @pl.when(jnp.logical_and(kvh == 0, j == 0))
        def _():
            dkv_ref[...] = jnp.zeros_like(dkv_ref)

        @pl.when(j == 0)
        def _():
            gt_acc[...] = jnp.zeros_like(gt_acc)
            dsink_ref[...] = jnp.zeros_like(dsink_ref)

        early = (j < 1).astype(jnp.int32)
        lane_lo = _lane_lo()
        k_lo, k_hi, v_lo, v_hi = _b_operands(kp, kc, vp, vc, kvh)
        dk_blk = jnp.zeros((keys, 128), F32)
        dv_blk = jnp.zeros((keys, 128), F32)
        for pp in range(B_GROUP // 2):
            cols = slice(128 * pp, 128 * (pp + 1))
            qp = q_ref[:, cols]
            o = o_ref[:, cols].astype(F32)
            lse_pair = lse_ref[:, cols]
            dzf = dz_ref[:, cols].astype(F32)
            silu, dsilu = _silu_parts(g_ref[:, cols].astype(F32))
            do = dzf * silu
            dqg_ref[1, :, cols] = (dzf * o * dsilu).astype(BF16)
            doo = do * o
            dqs = []
            for hh in range(2):
                g = 2 * pp + hh
                sel = lane_lo if hh == 0 else jnp.logical_not(lane_lo)
                sink = sink_ref[kvh * B_GROUP + g]
                kk = k_lo if hh == 0 else k_hi
                vv = v_lo if hh == 0 else v_hi
                qm, s = _head_logits(qp, kk, b_ref, g + B_GROUP * early, sel)
                lse_h = _own_everywhere(lse_pair, sel)
                p = jnp.exp(_minus_rows(s, lse_h))
                delta = jnp.sum(jnp.where(sel, doo, 0.0), axis=-1, keepdims=True)
                dom = jnp.where(sel, do, 0.0).astype(BF16)
                dp = _dot_nt(dom, vv)
                ds = p * (dp - delta)
                gt_acc[g] += ds
                dsink_ref[g:g + 1, :] -= jnp.sum(jnp.exp(sink - lse_h) * delta, axis=0, keepdims=True)
                dsb = ds.astype(BF16)
                dqs.append(_dot(dsb, kk) * SCALE)
                dk_blk = dk_blk + _dot_tn(dsb, qm)
                dv_blk = dv_blk + _dot_tn(p.astype(BF16), dom)
            dqg_ref[0, :, cols] = jnp.where(lane_lo, dqs[0], dqs[1]).astype(BF16)
        mine = lane_lo == (kvh == 0)
        dk_add = jnp.where(mine, dk_blk + pltpu.roll(dk_blk, HEAD_DIM, 1), 0.0)
        dv_add = jnp.where(mine, dv_blk + pltpu.roll(dv_blk, HEAD_DIM, 1), 0.0)

        @pl.when(j >= 1)
        def _():
            rows = pl.ds(pl.multiple_of(j * qblk - B_PREV, B_PREV), B_PREV)
            dkv_ref[0, rows, :] += dk_add[0:B_PREV]
            dkv_ref[1, rows, :] += dv_add[0:B_PREV]

        rows = pl.ds(pl.multiple_of(j * qblk, qblk), qblk)
        dkv_ref[0, rows, :] += dk_add[B_PREV:keys]
        dkv_ref[1, rows, :] += dv_add[B_PREV:keys]

        @pl.when(j == nq - 1)
        def _():
            dd = jnp.concatenate([_offset_sums(gt_acc[g]) for g in range(B_GROUP)], axis=0)
            hi = dd.astype(BF16)
            lo = (dd - hi.astype(F32)).astype(BF16)
            dt5_ref[...] = _dot(hi, oh_ref[...]) + _dot(lo, oh_ref[...])

    blk = pl.BlockSpec((qblk, 512), lambda h, j: (j, h))
    return pl.pallas_call(
        body, name="attn_b_bwd", grid=(B_KV_HEADS, nq),
        in_specs=[q_spec, g_spec, kp_spec, kc_spec, vp_spec, vc_spec, bias_spec, sink_spec, blk, blk, blk,
                  pl.BlockSpec((keys, 128), lambda h, j: (0, 0))],
        out_specs=[pl.BlockSpec((2, qblk, 512), lambda h, j: (0, j, h)),
                   pl.BlockSpec((2, t, 128), lambda h, j: (0, 0, 0)),
                   pl.BlockSpec((B_GROUP, 128), lambda h, j: (h, 0)),
                   pl.BlockSpec((B_GROUP, 128), lambda h, j: (h, 0))],
        out_shape=[SDS((2, t, D_MODEL), BF16), SDS((2, t, 128), F32),
                   SDS((N_HEADS, 128), F32), SDS((N_HEADS, 128), F32)],
        scratch_shapes=[pltpu.VMEM((B_GROUP, qblk, keys), F32), pltpu.VMEM((2 * B_GROUP, qblk, keys), F32)],
        compiler_params=_cparams(),
    )(qg, qg, kv, kv, kv, kv, bias, sinks, out_b, lse, dz, bucket_onehot)


def _a_bias_by_offset(rel_bias):
    m = np.arange(A_DIAG)
    idx = np.clip(A_BAND - 1 - m, -A_REL_CLIP, A_REL_CLIP) + A_REL_CLIP
    by_head = rel_bias[idx].T.reshape(N_HEADS // 2, 2, A_DIAG)
    return jnp.concatenate([by_head, jnp.zeros((N_HEADS // 2, 6, A_DIAG), F32)], axis=1)


def _a_bias_grad(offset_sums):
    first = 319
    tail = jnp.sum(offset_sums[:, :first], axis=1)
    body = jnp.flip(offset_sums[:, first:first + 320], axis=1)
    body = body.at[:, -1].add(tail)
    full = jnp.concatenate([jnp.zeros((N_HEADS, 193), F32), body], axis=1)
    return full


def _t5_bucket(rel):
    nb = T5_BUCKETS // 2
    max_exact = nb // 2
    ret = jnp.where(rel > 0, nb, 0)
    n = jnp.abs(rel)
    nf = jnp.maximum(n, 1).astype(jnp.float32)
    large = max_exact + (jnp.log(nf / max_exact) / math.log(T5_MAX_DIST / max_exact)
                         * (nb - max_exact)).astype(jnp.int32)
    large = jnp.minimum(large, nb - 1)
    return ret + jnp.where(n < max_exact, n, large)


def _b_offset_buckets(keys):
    return _t5_bucket(jnp.arange(keys, dtype=jnp.int32) - (B_LEFT_CHUNKS * CHUNK + CHUNK - 1))


def _b_bias_by_offset(t5_table, keys):
    return t5_table[_b_offset_buckets(keys)].T


def _b_bucket_onehot(keys):
    return (_b_offset_buckets(keys)[:, None] == jnp.arange(128)[None, :]).astype(BF16)


def _local_step(my_slot, order, x, target, a_gain_shard, w_in_a_shard, rel_bias, late_shards, kv_gain,
                t5_table, b_gain, sinks, f_gain):
    a_bias = _a_bias_by_offset(rel_bias)
    b_bias_fwd = _b_bias_by_offset(t5_table, B_QBLK_FWD + B_PREV)
    b_bias_bwd = _b_bias_by_offset(t5_table, B_QBLK_BWD + B_PREV)
    sinks_flat = sinks.reshape(N_HEADS)

    xn, qkvg, w_in_a, a_gain = _norm_matmul_gather(order, x, a_gain_shard, w_in_a_shard)
    z_a, out_a, lse_a, (w_in_b, w_out_a, w_out_b, kv_w) = _attn_a_fwd(qkvg, a_bias, late_shards)
    w_out_a = w_out_a.reshape(D_MODEL, D_MODEL)
    w_out_b = w_out_b.reshape(D_MODEL, D_MODEL)
    kv_w = kv_w.reshape(D_MODEL, 2 * 128)
    h1, kvn, hb, kv, qg = _layer_a_out(x, z_a, w_out_a, kv_gain, b_gain, kv_w, w_in_b)
    z_b, out_b, lse_b = _attn_b_fwd(qg, kv, b_bias_fwd, sinks_flat)
    dh2, dh2b, dz_b, loss, d_fn = _layer_b_out_loss(h1, z_b, w_out_b, f_gain, target)

    dqg_b, dkv_b, d_t5, d_sink = _attn_b_bwd(qg, kv, b_bias_bwd, sinks_flat, out_b, lse_b, dz_b,
                                             _b_bucket_onehot(B_QBLK_BWD + B_PREV))
    dh1, dh1b, dz_a, d_bn, d_kn = _layer_b_in_bwd(dqg_b, dkv_b, w_in_b, kv_w, h1, dh2, b_gain, kv_gain, w_out_a)
    early = dict(
        b_w_out=_weight_grad_rows("grad_b_w_out", my_slot, z_b, dh2b[None]),
        b_w_in=_weight_grad_cols("grad_b_w_in", my_slot, hb, [dqg_b],
                                 [(0, o, c, 4 * o + c) for o in range(2) for c in range(4)], 256),
        kv_w=_weight_grad_rows("grad_kv_w", my_slot, kvn, dkv_b),
        a_w_out=_weight_grad_rows("grad_a_w_out", my_slot, z_a, dh1b[None]))
    dqg_a, dkv_a, d_rel, landed = _attn_a_bwd(qkvg, a_bias, out_a, lse_a, dz_a, [g[0] for g in early.values()])
    g_w_in_a = _weight_grad_cols(
        "grad_a_w_in", my_slot, xn, [dqg_a, dkv_a],
        [(0, 0, 0, 0), (0, 0, 1, 1), (1, 0, 0, 2), (1, 0, 1, 3), (1, 1, 0, 4), (1, 1, 1, 5), (0, 1, 0, 6), (0, 1, 1, 7)], 512)
    from_sibling, = _exchange_sibling([g_w_in_a[0]])
    x_i, y_i, c_i, chips = _place()
    del x_i, y_i
    forward_slots = jnp.stack([_slot(*chip, c_i) for chip in chips]).astype(jnp.int32)
    chip_sums = _pre_reduce("chip_sum_a_w_in", g_w_in_a[0], from_sibling, forward_slots)
    grad_x, d_an, from_chips = _layer_a_in_bwd(dqg_a, dkv_a, w_in_a, x, dh1, a_gain, chip_sums)

    matrices = {n: (g[1], [(land, 0, N_DEV - 1)]) for (n, g), land in zip(early.items(), landed)}
    matrices["a_w_in"] = (g_w_in_a[1], [(from_sibling, 3, 1), (from_chips, 0, 3)])
    small = dict(
        loss=loss, a_norm=d_an, a_rel_bias=d_rel[:, :2].reshape(N_HEADS, A_DIAG),
        kv_norm=d_kn, t5_bias=d_t5, b_norm=d_bn, b_sinks=d_sink, final_norm=d_fn)
    return grad_x, small, matrices


def _place():
    x, y, c = lax.axis_index("x"), lax.axis_index("y"), lax.axis_index("c")
    chips = [(1 - x, y), (x, 1 - y), (1 - x, 1 - y)]
    return x, y, c, chips


def _slot(px, py, pc):
    return 4 * px + 2 * py + pc


ANY = pl.BlockSpec(memory_space=pl.ANY)


def _peer(x, y, c, k):
    return (x ^ (k >> 2), y ^ ((k >> 1) & 1), c ^ (k & 1))


def _scatter_copies(grad_refs, land_refs, send_sems, recv_sems):
    x, y, c, _ = _place()
    copies = []
    for t, (grad, land) in enumerate(zip(grad_refs, land_refs)):
        for k in range(1, N_DEV):
            peer = _peer(x, y, c, k)
            sem = (N_DEV - 1) * t + k - 1
            copies.append(pltpu.make_async_remote_copy(
                src_ref=grad.at[_slot(*peer)], dst_ref=land.at[k - 1],
                send_sem=send_sems.at[sem], recv_sem=recv_sems.at[sem],
                device_id=peer, device_id_type=MESH))
    return copies


def _gather_phases(ins, outs, send_sems, recv_sems, local_sems):
    n = len(ins)
    x, y, c, chips = _place()
    me, sibling = (x, y, c), (x, y, 1 - c)

    def copy(t, k, block, to, src=None):
        dst = outs[t].at[_slot(*block)]
        return pltpu.make_async_remote_copy(
            src_ref=dst if src is None else src, dst_ref=dst,
            send_sem=send_sems.at[7 * t + k], recv_sem=recv_sems.at[7 * t + k],
            device_id=to, device_id_type=MESH)

    def lists():
        mine = [pltpu.make_async_copy(ins[t], outs[t].at[_slot(*me)], local_sems.at[t]) for t in range(n)]
        first = []
        for t in range(n):
            first.append(copy(t, 0, me, sibling, src=ins[t]))
            first += [copy(t, 1 + j, me, (*chip, c), src=ins[t]) for j, chip in enumerate(chips)]
        passed = [copy(t, 4 + j, (*chip, c), sibling) for t in range(n) for j, chip in enumerate(chips)]
        return mine, first, passed

    def start():
        mine, first, _ = lists()
        for cp in mine + first:
            cp.start()

    def forward():
        _, _, passed = lists()
        for t in range(n):
            for j, chip in enumerate(chips):
                copy(t, 1 + j, (*chip, c), me).wait_recv()
                passed[3 * t + j].start()

    def finish():
        mine, first, passed = lists()
        for t in range(n):
            copy(t, 0, sibling, me).wait_recv()
            for j, chip in enumerate(chips):
                copy(t, 4 + j, (*chip, 1 - c), me).wait_recv()
        for cp in first + passed:
            cp.wait_send()
        for cp in mine:
            cp.wait()

    return start, forward, finish


def _gather_scratch(n):
    return [pltpu.SemaphoreType.DMA((7 * n,)), pltpu.SemaphoreType.DMA((7 * n,)), pltpu.SemaphoreType.DMA((n,))]


def _exchange_sibling(grads):
    n = len(grads)

    def body(*refs):
        ins, outs = refs[:n], refs[n:2 * n]
        send_sems, recv_sems = refs[2 * n:]
        x, y, c, chips = _place()
        sibling = (x, y, 1 - c)
        copies = []
        for t in range(n):
            blocks = [(*chip, 1 - c) for chip in chips] + [sibling]
            for k, block in enumerate(blocks):
                copies.append(pltpu.make_async_remote_copy(
                    src_ref=ins[t].at[_slot(*block)], dst_ref=outs[t].at[k],
                    send_sem=send_sems.at[4 * t + k], recv_sem=recv_sems.at[4 * t + k],
                    device_id=sibling, device_id_type=MESH))
        for cp in copies:
            cp.start()
        for cp in copies:
            cp.wait()

    return pl.pallas_call(
        body, name="grads_to_sibling",
        in_specs=[ANY] * n, out_specs=[ANY] * n,
        out_shape=[SDS((4, *g.shape[1:]), g.dtype) for g in grads],
        scratch_shapes=[pltpu.SemaphoreType.DMA((4 * n,)), pltpu.SemaphoreType.DMA((4 * n,))],
    )(*grads)


def _chip_copies(sums_ref, land_ref, send_sems, recv_sems):
    x, y, c, chips = _place()
    del x, y
    return [pltpu.make_async_remote_copy(
        src_ref=sums_ref.at[j], dst_ref=land_ref.at[j], send_sem=send_sems.at[j], recv_sem=recv_sems.at[j],
        device_id=(*chip, c), device_id_type=MESH) for j, chip in enumerate(chips)]


def _row_tile(rows):
    return min(rows, 256)


def _pre_reduce(name, g, from_sibling, slots):
    _, r, c = g.shape
    tr = _row_tile(r)

    def body(slots_ref, g_ref, s_ref, o_ref):
        del slots_ref
        o_ref[...] = (g_ref[...].astype(F32) + s_ref[...].astype(F32)).astype(BF16)

    return pl.pallas_call(
        body, name=name,
        grid_spec=pltpu.PrefetchScalarGridSpec(
            num_scalar_prefetch=1, grid=(3, r // tr),
            in_specs=[pl.BlockSpec((1, tr, c), lambda j, i, s: (s[j], i, 0)),
                      pl.BlockSpec((1, tr, c), lambda j, i, s: (j, i, 0))],
            out_specs=pl.BlockSpec((1, tr, c), lambda j, i, s: (j, i, 0))),
        out_shape=SDS((3, r, c), BF16),
        compiler_params=_cparams(),
    )(slots, g, from_sibling)


def _adamw(w, g, m, v):
    m2 = ADAM_B1 * m + (1.0 - ADAM_B1) * g
    v2 = ADAM_B2 * v + (1.0 - ADAM_B2) * jnp.square(g)
    m_hat = m2 / (1.0 - ADAM_B1 ** ADAM_STEP)
    v_hat = v2 / (1.0 - ADAM_B2 ** ADAM_STEP)
    delta = -ADAM_LR * (m_hat / (jnp.sqrt(v_hat) + ADAM_EPS) + ADAM_WD * w)
    return delta, m2, v2


def _reduce_adamw(name, own, partials, w, m, v):
    r, c = own.shape
    tr = _row_tile(r)
    n_p = len(partials)

    def body(own_ref, *rest):
        p_refs, (w_ref, m_ref, v_ref, grad_ref, d_ref, nm_ref, nv_ref) = rest[:n_p], rest[n_p:]
        grad = own_ref[...]
        for p_ref, (_, _, count) in zip(p_refs, partials):
            for j in range(count):
                grad = grad + p_ref[j].astype(F32)
        grad_ref[...] = grad
        d_ref[...], nm_ref[...], nv_ref[...] = _adamw(w_ref[...], grad, m_ref[...], v_ref[...])

    flat = pl.BlockSpec((tr, c), lambda i: (i, 0))
    return pl.pallas_call(
        body, name=name, grid=(r // tr,),
        in_specs=[flat] + [pl.BlockSpec((count, tr, c), lambda i, first=first, count=count: (first // count, i, 0))
                           for _, first, count in partials] + [flat, flat, flat],
        out_specs=[flat, flat, flat, flat],
        out_shape=[SDS((r, c), F32)] * 4,
        compiler_params=_cparams(),
    )(own, *[p[0] for p in partials], w, m, v)


VM = pl.BlockSpec()


def _small_allreduce(parts):
    n = len(parts)

    def body(*refs):
        ins, outs, lands = refs[:n], refs[n:2 * n], refs[2 * n:3 * n]
        send_sems, recv_sems = refs[3 * n:]
        x, y, c, _ = _place()
        my_slot = _slot(x, y, c)
        copies = []
        for t in range(n):
            lands[t][my_slot] = ins[t][...]
            for k in range(1, N_DEV):
                sem = (N_DEV - 1) * t + k - 1
                copies.append(pltpu.make_async_remote_copy(
                    src_ref=ins[t], dst_ref=lands[t].at[my_slot],
                    send_sem=send_sems.at[sem], recv_sem=recv_sems.at[sem],
                    device_id=_peer(x, y, c, k), device_id_type=MESH))
        for cp in copies:
            cp.start()
        for t in range(n):
            for k in range(1, N_DEV):
                sem = (N_DEV - 1) * t + k - 1
                pltpu.make_async_remote_copy(
                    src_ref=ins[t], dst_ref=lands[t].at[_slot(*_peer(x, y, c, k))],
                    send_sem=send_sems.at[sem], recv_sem=recv_sems.at[sem],
                    device_id=(x, y, c), device_id_type=MESH).wait_recv()
        for cp in copies:
            cp.wait_send()
        for t in range(n):
            total = lands[t][0]
            for s in range(1, N_DEV):
                total = total + lands[t][s]
            outs[t][...] = total

    n_sems = (N_DEV - 1) * n
    return pl.pallas_call(
        body, name="small_allreduce",
        in_specs=[VM] * n, out_specs=[VM] * n, out_shape=[SDS(p.shape, F32) for p in parts],
        scratch_shapes=[pltpu.VMEM((N_DEV, *p.shape), F32) for p in parts]
        + [pltpu.SemaphoreType.DMA((n_sems,)), pltpu.SemaphoreType.DMA((n_sems,))],
    )(*parts)


def _small_adamw(my_slot, sums, ws, ms, vs):
    n = len(ws)

    def body(slot_ref, *refs):
        sum_refs, refs = refs[:n + 1], refs[n + 1:]
        w_refs, m_refs, v_refs, refs = refs[:n], refs[n:2 * n], refs[2 * n:3 * n], refs[3 * n:]
        g_refs, d_refs, nm_refs, nv_refs = refs[:n + 1], refs[n + 1:2 * n + 1], refs[2 * n + 1:3 * n + 1], refs[3 * n + 1:]
        for t in range(n + 1):
            if t == 0:
                g = sum_refs[0][:, pl.ds(pl.multiple_of(slot_ref[0] * 128, 128), 128)]
            else:
                g = sum_refs[t][...]
            g_refs[t][...] = g
            if t < n:
                d_refs[t][...], nm_refs[t][...], nv_refs[t][...] = _adamw(w_refs[t][...], g, m_refs[t][...], v_refs[t][...])

    shapes = [SDS(w.shape, F32) for w in ws]
    outs = pl.pallas_call(
        body, name="small_adamw",
        in_specs=[pl.BlockSpec(memory_space=pltpu.SMEM)] + [VM] * (4 * n + 1),
        out_specs=[VM] * (4 * n + 1),
        out_shape=shapes + [SDS(sums[-1].shape, F32)] + shapes * 3,
    )(my_slot, *sums, *ws, *ms, *vs)
    return outs[:n + 1], outs[n + 1:2 * n + 1], outs[2 * n + 1:3 * n + 1], outs[3 * n + 1:]


def kernel(x, a_norm, a_w_in, a_rel_bias, a_w_out, kv_norm, kv_w, t5_bias, b_norm, b_w_in, b_sinks, b_w_out, final_norm, loss_target, m_a_norm, m_a_w_in, m_a_rel_bias, m_a_w_out, m_kv_norm, m_kv_w, m_t5_bias, m_b_norm, m_b_w_in, m_b_sinks, m_b_w_out, m_final_norm, v_a_norm, v_a_w_in, v_a_rel_bias, v_a_w_out, v_kv_norm, v_kv_w, v_t5_bias, v_b_norm, v_b_w_in, v_b_sinks, v_b_w_out, v_final_norm):
    xi, yi, ci = lax.axis_index("x"), lax.axis_index("y"), lax.axis_index("c")
    my_slot = _slot(xi, yi, ci)

    slot_arr = jnp.reshape(my_slot, (1,)).astype(jnp.int32)
    order = _gather_order(xi, yi, ci)
    late_shards = [b_w_in[0].astype(BF16), a_w_out[0].astype(BF16), b_w_out[0].astype(BF16), kv_w.astype(BF16)]
    grad_x, loc, matrices = _local_step(
        slot_arr, order, x[0], loss_target[0], a_norm, a_w_in[0].astype(BF16), a_rel_bias[0], late_shards,
        kv_norm.reshape(1, D_MODEL), t5_bias, b_norm, b_sinks, final_norm.reshape(1, D_MODEL))

    shard_w = dict(a_w_in=a_w_in[0], b_w_in=b_w_in[0], a_w_out=a_w_out[0], b_w_out=b_w_out[0], kv_w=kv_w)
    shard_m = dict(a_w_in=m_a_w_in[0], b_w_in=m_b_w_in[0], a_w_out=m_a_w_out[0], b_w_out=m_b_w_out[0], kv_w=m_kv_w)
    shard_v = dict(a_w_in=v_a_w_in[0], b_w_in=v_b_w_in[0], a_w_out=v_a_w_out[0], b_w_out=v_b_w_out[0], kv_w=v_kv_w)
    big = {n: _reduce_adamw("adamw_" + n, own, partials, shard_w[n], shard_m[n], shard_v[n])
           for n, (own, partials) in matrices.items()}

    names = ("a_norm", "a_rel_bias", "kv_norm", "t5_bias", "b_norm", "b_sinks", "final_norm")
    tables = ("a_rel_bias", "t5_bias")

    def row(n, a):
        return a.reshape(-1, a.shape[-1]).T if n in tables else a.reshape(1, -1)

    small_w = [row(n, a) for n, a in zip(names, (a_norm, a_rel_bias, kv_norm, t5_bias, b_norm, b_sinks, final_norm))]
    small_m = [row(n, a) for n, a in zip(names, (m_a_norm, m_a_rel_bias, m_kv_norm, m_t5_bias, m_b_norm, m_b_sinks,
                                                 m_final_norm))]
    small_v = [row(n, a) for n, a in zip(names, (v_a_norm, v_a_rel_bias, v_kv_norm, v_t5_bias, v_b_norm, v_b_sinks,
                                                 v_final_norm))]
    sums = dict(zip(names + ("loss",), _small_allreduce([loc[n] for n in names] + [loc["loss"]])))
    sums["a_rel_bias"] = _a_bias_grad(sums["a_rel_bias"])
    sums["t5_bias"] = sums["t5_bias"][:, :T5_BUCKETS]
    sums["b_sinks"] = sums["b_sinks"][:, 0].reshape(1, N_HEADS)
    results = _small_adamw(slot_arr, [sums[n] for n in names + ("loss",)], small_w, small_m, small_v)
    like = dict(a_norm=a_norm, a_rel_bias=a_rel_bias, kv_norm=kv_norm, t5_bias=t5_bias, b_norm=b_norm,
                b_sinks=b_sinks, final_norm=final_norm)
    sm = [{n: (part[i].T if n in tables else part[i]).reshape(like[n].shape) for i, n in enumerate(names)}
          for part in results]
    loss = results[0][len(names)][0, 0]

    order = ("a_norm", "a_w_in", "a_rel_bias", "a_w_out", "kv_norm", "kv_w", "t5_bias", "b_norm",
             "b_w_in", "b_sinks", "b_w_out", "final_norm")
    lead = dict(a_w_in=True, b_w_in=True, a_w_out=True, b_w_out=True, kv_w=False)

    def pick(kind, name):
        if name in big:
            val = big[name][kind]
            return val[None] if lead[name] else val
        return sm[kind][name]

    outs = [loss, grad_x[None]]
    for kind in range(4):
        outs += [pick(kind, n) for n in order]
    return tuple(outs)
```

```python
import functools
import math

import numpy as np
import jax
import jax.numpy as jnp
from jax import lax
from jax.experimental import pallas as pl
from jax.experimental.pallas import tpu as pltpu

F32 = jnp.float32
BF16 = jnp.bfloat16
SDS = jax.ShapeDtypeStruct

D_MODEL = 1024
HEAD_DIM = 64
CHUNK = 64
N_HEADS = 16
RMS_EPS = 1e-6
A_LEFT_CHUNKS = 8
A_BAND = (A_LEFT_CHUNKS + 1) * CHUNK
A_REL_CLIP = 256
B_KV_HEADS = 2
B_GROUP = 8
B_LEFT_CHUNKS = 2
B_BAND = (B_LEFT_CHUNKS + 1) * CHUNK
T5_BUCKETS = 32
T5_MAX_DIST = 128
QBLK = 256
A_KEYS = 3 * QBLK
B_QBLK_FWD = 128
B_QBLK_BWD = 256
B_PREV = 128
A_DIAG = A_KEYS
NEG = -1e30
SCALE = HEAD_DIM ** -0.5
N_DEV = 8

ADAM_LR = 0.001
ADAM_B1 = 0.9
ADAM_B2 = 0.999
ADAM_EPS = 1e-08
ADAM_WD = 0.01
ADAM_STEP = 10

VMEM_LIMIT_BYTES = 56 * 1024 * 1024
MESH = pl.DeviceIdType.MESH


def _cparams():
    return pltpu.CompilerParams(vmem_limit_bytes=VMEM_LIMIT_BYTES)


def _dot(a, b):
    return jnp.dot(a, b, preferred_element_type=F32)


def _dot_nt(a, b):
    return lax.dot_general(a, b, (((1,), (1,)), ((), ())), preferred_element_type=F32)


def _dot_tn(a, b):
    return lax.dot_general(a, b, (((0,), (0,)), ((), ())), preferred_element_type=F32)


def _rstd(xf):
    return lax.rsqrt(jnp.mean(xf * xf, axis=-1, keepdims=True) + RMS_EPS)


def _sigmoid(x):
    return 1.0 / (1.0 + jnp.exp(-x))


_GATHER_SEQUENCE = ((0, None), (1, 0), (2, 1), (4, None), (5, None), (3, 2), (6, None))


def _gather_order(x, y, c):
    others = [(1 - x, y), (x, 1 - y), (1 - x, 1 - y)]
    arrivals = [_slot(x, y, 1 - c)] + [_slot(*chip, c) for chip in others] + [_slot(*chip, 1 - c) for chip in others]
    return jnp.stack([_slot(x, y, c)] + [arrivals[a] for a, _ in _GATHER_SEQUENCE]).astype(jnp.int32)


def _norm_matmul_gather(order, x, gain_shard, w_shard):
    t = x.shape[0]
    dw, tn = w_shard.shape
    tm = min(t, 1024)
    n_m = t // tm

    def body(order_ref, x_ref, gs_ref, shard_ref, xn_ref, o_ref, full_ref, gain_ref,
             xn_all, wbuf, gland, send_sems, recv_sems, gsend_sems, grecv_sems, load_sems, own_sem):
        n, m = pl.program_id(0), pl.program_id(1)
        x_i, y_i, c_i, chips = _place()
        me, sibling = (x_i, y_i, c_i), (x_i, y_i, 1 - c_i)

        def send(k, block, to, src=None):
            dst = full_ref.at[_slot(*block)]
            return pltpu.make_async_remote_copy(
                src_ref=dst if src is None else src, dst_ref=dst,
                send_sem=send_sems.at[k], recv_sem=recv_sems.at[k], device_id=to, device_id_type=MESH)

        own = pltpu.make_async_copy(shard_ref, full_ref.at[_slot(*me)], own_sem)
        first = [send(0, me, sibling, src=shard_ref)]
        first += [send(1 + j, me, (*chip, c_i), src=shard_ref) for j, chip in enumerate(chips)]
        forwards = [send(4 + j, (*chip, c_i), sibling) for j, chip in enumerate(chips)]
        arrivals = [send(0, sibling, me)] + [send(1 + j, (*chip, c_i), me) for j, chip in enumerate(chips)]
        arrivals += [send(4 + j, (*chip, 1 - c_i), me) for j, chip in enumerate(chips)]
        gains = [pltpu.make_async_remote_copy(
            src_ref=gs_ref, dst_ref=gland.at[_slot(*me)], send_sem=gsend_sems.at[k - 1],
            recv_sem=grecv_sems.at[k - 1], device_id=_peer(x_i, y_i, c_i, k), device_id_type=MESH)
            for k in range(1, N_DEV)]

        @pl.when(jnp.logical_and(n == 0, m == 0))
        def _():
            own.start()
            for cp in gains + first:
                cp.start()
            pltpu.make_async_copy(shard_ref, wbuf.at[0], load_sems.at[0]).start()
            gland[_slot(*me)] = gs_ref[...]
            for k in range(1, N_DEV):
                pltpu.make_async_remote_copy(
                    src_ref=gs_ref, dst_ref=gland.at[_slot(*_peer(x_i, y_i, c_i, k))],
                    send_sem=gsend_sems.at[k - 1], recv_sem=grecv_sems.at[k - 1],
                    device_id=me, device_id_type=MESH).wait_recv()
            for s in range(N_DEV):
                gain_ref[:, 128 * s:128 * (s + 1)] = gland[s]

        rows = pl.ds(pl.multiple_of(m * tm, tm), tm)

        @pl.when(n == 0)
        def _():
            xf = x_ref[...]
            xn = ((xf * _rstd(xf)) * gain_ref[...]).astype(BF16)
            xn_all[rows, :] = xn
            xn_ref[...] = xn

        @pl.when(m == 0)
        def _():
            pltpu.make_async_copy(full_ref.at[0], wbuf.at[n % 2], load_sems.at[n % 2]).wait()

        o_ref[...] = _dot(xn_all[rows, :], wbuf[n % 2]).astype(BF16)

        for k, (arrival, forward) in enumerate(_GATHER_SEQUENCE):
            @pl.when(jnp.logical_and(n == k, m == n_m - 1))
            def _(k=k, arrival=arrival, forward=forward):
                arrivals[arrival].wait_recv()
                if forward is not None:
                    forwards[forward].start()
                pltpu.make_async_copy(full_ref.at[order_ref[k + 1]], wbuf.at[(k + 1) % 2],
                                      load_sems.at[(k + 1) % 2]).start()

        @pl.when(jnp.logical_and(n == N_DEV - 1, m == n_m - 1))
        def _():
            for cp in gains + first + forwards:
                cp.wait_send()
            own.wait()

    held = lambda n, m, order: (jnp.where(n == 0, m, n_m - 1), 0)
    return pl.pallas_call(
        body, name="norm_matmul_gather",
        grid_spec=pltpu.PrefetchScalarGridSpec(
            num_scalar_prefetch=1, grid=(N_DEV, n_m),
            in_specs=[pl.BlockSpec((tm, D_MODEL), held),
                      pl.BlockSpec((1, 128), lambda n, m, order: (0, 0)), ANY],
            out_specs=[pl.BlockSpec((tm, D_MODEL), held),
                       pl.BlockSpec((tm, tn), lambda n, m, order: (m, order[n])),
                       ANY, pl.BlockSpec((1, D_MODEL), lambda n, m, order: (0, 0))],
            scratch_shapes=[pltpu.VMEM((t, D_MODEL), BF16), pltpu.VMEM((2, dw, tn), BF16),
                            pltpu.VMEM((N_DEV, 1, 128), F32),
                            pltpu.SemaphoreType.DMA((7,)), pltpu.SemaphoreType.DMA((7,)),
                            pltpu.SemaphoreType.DMA((7,)), pltpu.SemaphoreType.DMA((7,)),
                            pltpu.SemaphoreType.DMA((2,)), pltpu.SemaphoreType.DMA]),
        out_shape=[SDS((t, D_MODEL), BF16), SDS((t, N_DEV * tn), BF16), SDS((N_DEV, dw, tn), BF16),
                   SDS((1, D_MODEL), F32)],
        compiler_params=_cparams(),
    )(order, x, gain_shard, w_shard)


def _layer_a_out(x, z, w_out, kv_gain, b_gain, kv_w, w_in_b):
    t = x.shape[0]
    tm = min(t, 512)
    nb, _, tn = w_in_b.shape

    def body(x_ref, z_ref, wo_ref, kvg_ref, bg_ref, kvw_ref, wb_ref,
             h1_ref, kvn_ref, hb_ref, kv_ref, qg_ref):
        h1 = x_ref[...] + _dot(z_ref[...], wo_ref[...])
        h1_ref[...] = h1
        y0 = h1 * _rstd(h1)
        kvn = (y0 * kvg_ref[...]).astype(BF16)
        hb = (y0 * bg_ref[...]).astype(BF16)
        kvn_ref[...] = kvn
        hb_ref[...] = hb
        kv_ref[...] = _dot(kvn, kvw_ref[...]).astype(BF16)
        for i in range(nb):
            qg_ref[:, i * tn:(i + 1) * tn] = _dot(hb, wb_ref[i]).astype(BF16)

    row = lambda m: (m, 0)
    fix2 = lambda m: (0, 0)
    return pl.pallas_call(
        body, name="layer_a_out", grid=(t // tm,),
        in_specs=[pl.BlockSpec((tm, D_MODEL), row), pl.BlockSpec((tm, D_MODEL), row),
                  pl.BlockSpec((D_MODEL, D_MODEL), fix2),
                  pl.BlockSpec((1, D_MODEL), fix2), pl.BlockSpec((1, D_MODEL), fix2),
                  pl.BlockSpec((D_MODEL, 256), fix2),
                  pl.BlockSpec((nb, D_MODEL, tn), lambda m: (0, 0, 0))],
        out_specs=[pl.BlockSpec((tm, D_MODEL), row), pl.BlockSpec((tm, D_MODEL), row),
                   pl.BlockSpec((tm, D_MODEL), row), pl.BlockSpec((tm, 256), row),
                   pl.BlockSpec((tm, nb * tn), row)],
        out_shape=[SDS((t, D_MODEL), F32), SDS((t, D_MODEL), BF16), SDS((t, D_MODEL), BF16),
                   SDS((t, 256), BF16), SDS((t, nb * tn), BF16)],
        compiler_params=_cparams(),
    )(x, z, w_out, kv_gain, b_gain, kv_w, w_in_b)


def _layer_b_out_loss(h1, z, w_out, f_gain, target):
    t = h1.shape[0]
    tm = min(t, 512)

    def body(h1_ref, z_ref, wo_ref, fg_ref, tgt_ref,
             dh2_ref, dh2b_ref, dz_ref, loss_ref, dfn_ref):
        @pl.when(pl.program_id(0) == 0)
        def _():
            loss_ref[...] = jnp.zeros_like(loss_ref)
            dfn_ref[...] = jnp.zeros_like(dfn_ref)

        h2 = h1_ref[...] + _dot(z_ref[...], wo_ref[...])
        r = _rstd(h2)
        yn = h2 * r
        fg = fg_ref[...]
        err = yn * fg - tgt_ref[...]
        loss_ref[...] += (0.5 / D_MODEL) * jnp.sum(err * err)
        dy = err * (1.0 / D_MODEL)
        dfn_ref[...] += jnp.sum(dy * yn, axis=0, keepdims=True)
        u = dy * fg
        dh2 = r * u - h2 * ((r * r * r) * jnp.mean(u * h2, axis=-1, keepdims=True))
        dh2_ref[...] = dh2
        dh2b = dh2.astype(BF16)
        dh2b_ref[...] = dh2b
        dz_ref[...] = _dot_nt(dh2b, wo_ref[...]).astype(BF16)

    row = lambda m: (m, 0)
    fix2 = lambda m: (0, 0)
    return pl.pallas_call(
        body, name="layer_b_out_loss", grid=(t // tm,),
        in_specs=[pl.BlockSpec((tm, D_MODEL), row), pl.BlockSpec((tm, D_MODEL), row),
                  pl.BlockSpec((D_MODEL, D_MODEL), fix2), pl.BlockSpec((1, D_MODEL), fix2),
                  pl.BlockSpec((tm, D_MODEL), row)],
        out_specs=[pl.BlockSpec((tm, D_MODEL), row), pl.BlockSpec((tm, D_MODEL), row),
                   pl.BlockSpec((tm, D_MODEL), row), pl.BlockSpec((1, 128), fix2),
                   pl.BlockSpec((1, D_MODEL), fix2)],
        out_shape=[SDS((t, D_MODEL), F32), SDS((t, D_MODEL), BF16), SDS((t, D_MODEL), BF16),
                   SDS((1, 128), F32), SDS((1, D_MODEL), F32)],
        compiler_params=_cparams(),
    )(h1, z, w_out, f_gain, target)


def _layer_b_in_bwd(dqg, dkv, w_in_b, kv_w, h1, dh2, b_gain, kv_gain, w_out_a):
    t = h1.shape[0]
    tm = min(t, 512)
    nb, _, tn = w_in_b.shape
    per = D_MODEL // tn

    def body(dqg_ref, dkv_ref, wb_ref, kvw_ref, h1_ref, dh2_ref, bg_ref, kvg_ref, wo_ref,
             dh1_ref, dh1b_ref, dz_ref, dbn_ref, dkn_ref):
        @pl.when(pl.program_id(0) == 0)
        def _():
            dbn_ref[...] = jnp.zeros_like(dbn_ref)
            dkn_ref[...] = jnp.zeros_like(dkn_ref)

        dhb = jnp.zeros((tm, D_MODEL), F32)
        for i in range(nb):
            blk = dqg_ref[i // per, :, (i % per) * tn:(i % per + 1) * tn]
            dhb = dhb + _dot_nt(blk, wb_ref[i])
        dkn = (_dot_nt(dkv_ref[0].astype(BF16), kvw_ref[:, 0:128])
               + _dot_nt(dkv_ref[1].astype(BF16), kvw_ref[:, 128:256]))
        h1 = h1_ref[...]
        r = _rstd(h1)
        xr = h1 * r
        dbn_ref[...] += jnp.sum(dhb * xr, axis=0, keepdims=True)
        dkn_ref[...] += jnp.sum(dkn * xr, axis=0, keepdims=True)
        u = dhb * bg_ref[...] + dkn * kvg_ref[...]
        dh1 = dh2_ref[...] + r * u - h1 * ((r * r * r) * jnp.mean(u * h1, axis=-1, keepdims=True))
        dh1_ref[...] = dh1
        dh1b = dh1.astype(BF16)
        dh1b_ref[...] = dh1b
        dz_ref[...] = _dot_nt(dh1b, wo_ref[...]).astype(BF16)

    row = lambda m: (m, 0)
    fix2 = lambda m: (0, 0)
    return pl.pallas_call(
        body, name="layer_b_in_bwd", grid=(t // tm,),
        in_specs=[pl.BlockSpec((2, tm, D_MODEL), lambda m: (0, m, 0)),
                  pl.BlockSpec((2, tm, 128), lambda m: (0, m, 0)),
                  pl.BlockSpec((nb, D_MODEL, tn), lambda m: (0, 0, 0)),
                  pl.BlockSpec((D_MODEL, 256), fix2),
                  pl.BlockSpec((tm, D_MODEL), row), pl.BlockSpec((tm, D_MODEL), row),
                  pl.BlockSpec((1, D_MODEL), fix2), pl.BlockSpec((1, D_MODEL), fix2),
                  pl.BlockSpec((D_MODEL, D_MODEL), fix2)],
        out_specs=[pl.BlockSpec((tm, D_MODEL), row), pl.BlockSpec((tm, D_MODEL), row),
                   pl.BlockSpec((tm, D_MODEL), row), pl.BlockSpec((1, D_MODEL), fix2),
                   pl.BlockSpec((1, D_MODEL), fix2)],
        out_shape=[SDS((t, D_MODEL), F32), SDS((t, D_MODEL), BF16), SDS((t, D_MODEL), BF16),
                   SDS((1, D_MODEL), F32), SDS((1, D_MODEL), F32)],
        compiler_params=_cparams(),
    )(dqg, dkv, w_in_b, kv_w, h1, dh2, b_gain, kv_gain, w_out_a)


def _layer_a_in_bwd(dqg, dkv, w_in_a, x, dh1, a_gain, chip_sums):
    t = x.shape[0]
    tm = min(t, 512)
    nb, _, tn = w_in_a.shape
    per = D_MODEL // tn

    def body(dqg_ref, dkv_ref, w_ref, x_ref, dh1_ref, ag_ref, sums_ref, dx_ref, dan_ref, land_ref,
             send_sems, recv_sems):
        @pl.when(pl.program_id(0) == 0)
        def _():
            dan_ref[...] = jnp.zeros_like(dan_ref)
            for cp in _chip_copies(sums_ref, land_ref, send_sems, recv_sems):
                cp.start()

        dxn = jnp.zeros((tm, D_MODEL), F32)
        for i in range(nb):
            part = i // per
            src = dqg_ref if part in (0, 3) else dkv_ref
            outer = {0: 0, 3: 1, 1: 0, 2: 1}[part]
            blk = src[outer, :, (i % per) * tn:(i % per + 1) * tn]
            dxn = dxn + _dot_nt(blk, w_ref[i])
        xf = x_ref[...]
        r = _rstd(xf)
        dan_ref[...] += jnp.sum(dxn * (xf * r), axis=0, keepdims=True)
        u = dxn * ag_ref[...]
        dx_ref[...] = dh1_ref[...] + r * u - xf * ((r * r * r) * jnp.mean(u * xf, axis=-1, keepdims=True))

        @pl.when(pl.program_id(0) == t // tm - 1)
        def _():
            for cp in _chip_copies(sums_ref, land_ref, send_sems, recv_sems):
                cp.wait()

    row = lambda m: (m, 0)
    fix2 = lambda m: (0, 0)
    return pl.pallas_call(
        body, name="layer_a_in_bwd", grid=(t // tm,),
        in_specs=[pl.BlockSpec((2, tm, D_MODEL), lambda m: (0, m, 0)),
                  pl.BlockSpec((2, tm, D_MODEL), lambda m: (0, m, 0)),
                  pl.BlockSpec((nb, D_MODEL, tn), lambda m: (0, 0, 0)),
                  pl.BlockSpec((tm, D_MODEL), row), pl.BlockSpec((tm, D_MODEL), row),
                  pl.BlockSpec((1, D_MODEL), fix2), ANY],
        out_specs=[pl.BlockSpec((tm, D_MODEL), row), pl.BlockSpec((1, D_MODEL), fix2), ANY],
        out_shape=[SDS((t, D_MODEL), F32), SDS((1, D_MODEL), F32), SDS(chip_sums.shape, chip_sums.dtype)],
        scratch_shapes=[pltpu.SemaphoreType.DMA((3,)), pltpu.SemaphoreType.DMA((3,))],
        compiler_params=_cparams(),
    )(dqg, dkv, w_in_a, x, dh1, a_gain, chip_sums)


def _lut(s, vals):
    r = jnp.int32(vals[0])
    for i in range(1, len(vals)):
        r = jnp.where(s == i, jnp.int32(vals[i]), r)
    return r


def _held(steps, i):
    seq, cur = [None] * len(steps), None
    for k in range(len(steps) - 1, -1, -1):
        if steps[k][0] == i:
            cur = steps[k][1:3]
        seq[k] = cur
    for k in range(len(steps)):
        cur = seq[k] = seq[k] if seq[k] is not None else cur
    return seq


def _weight_grad_cols(name, my_slot, a, bs, steps, tn):
    t, dw = a.shape
    n_arr = len(bs)
    which = [s[0] for s in steps]
    blks = [s[3] for s in steps]

    def body(slot_ref, a_ref, *rest):
        b_refs, (o_ref, own_ref, at_ref) = rest[:n_arr], rest[n_arr:]
        s = pl.program_id(0)

        @pl.when(s == 0)
        def _():
            at_ref[...] = a_ref[...].T

        for i in range(n_arr):
            @pl.when(_lut(s, which) == i)
            def _(i=i):
                res = _dot(at_ref[...], b_refs[i][0])
                o_ref[0] = res.astype(BF16)

                @pl.when(_lut(s, blks) == slot_ref[0])
                def _():
                    own_ref[...] = res

    def b_spec(i):
        held = _held(steps, i)
        return pl.BlockSpec((1, t, tn), lambda s, slot: (_lut(s, [h[0] for h in held]), 0,
                                                         _lut(s, [h[1] for h in held])))

    return pl.pallas_call(
        body, name=name,
        grid_spec=pltpu.PrefetchScalarGridSpec(
            num_scalar_prefetch=1, grid=(len(steps),),
            in_specs=[pl.BlockSpec((t, dw), lambda s, slot: (0, 0))] + [b_spec(i) for i in range(n_arr)],
            out_specs=[pl.BlockSpec((1, dw, tn), lambda s, slot: (_lut(s, blks), 0, 0)),
                       pl.BlockSpec((dw, tn), lambda s, slot: (0, 0))],
            scratch_shapes=[pltpu.VMEM((dw, t), BF16)]),
        out_shape=[SDS((N_DEV, dw, tn), BF16), SDS((dw, tn), F32)],
        compiler_params=_cparams(),
    )(my_slot, a, *bs)


def _weight_grad_rows(name, my_slot, a, b):
    t, dw = a.shape
    n_o, _, c = b.shape
    rows = dw // N_DEV
    tn = min(c, 256)
    per = c // tn

    def body(slot_ref, a_ref, b_ref, o_ref, own_ref, at_ref, res_ref):
        @pl.when(pl.program_id(0) == 0)
        def _():
            at_ref[...] = a_ref[...].T

        res_ref[...] = _dot(at_ref[...], b_ref[0].astype(BF16))
        o_ref[...] = res_ref[...].astype(BF16)
        own_ref[...] = res_ref[pl.ds(pl.multiple_of(slot_ref[0] * rows, rows), rows), :]

    all_rows, own = pl.pallas_call(
        body, name=name,
        grid_spec=pltpu.PrefetchScalarGridSpec(
            num_scalar_prefetch=1, grid=(n_o * per,),
            in_specs=[pl.BlockSpec((t, dw), lambda s, slot: (0, 0)),
                      pl.BlockSpec((1, t, tn), lambda s, slot: (s // per, 0, s % per))],
            out_specs=[pl.BlockSpec((dw, tn), lambda s, slot: (0, s)),
                       pl.BlockSpec((rows, tn), lambda s, slot: (0, s))],
            scratch_shapes=[pltpu.VMEM((dw, t), BF16), pltpu.VMEM((dw, tn), F32)]),
        out_shape=[SDS((dw, n_o * c), BF16), SDS((rows, n_o * c), F32)],
        compiler_params=_cparams(),
    )(my_slot, a, b)
    return all_rows.reshape(N_DEV, rows, n_o * c), own


def _lane_lo():
    return lax.broadcasted_iota(jnp.int32, (1, 128), 1) < HEAD_DIM


def _offset_sums(gt):
    keys = gt.shape[1]
    gc = gt[0:CHUNK]
    for cc in range(1, gt.shape[0] // CHUNK):
        gc = gc + pltpu.roll(gt[cc * CHUNK:(cc + 1) * CHUNK], keys - cc * CHUNK, 1)
    hi = gc.astype(BF16)
    lo = (gc - hi.astype(F32)).astype(BF16)
    flip = (lax.broadcasted_iota(jnp.int32, (CHUNK, CHUNK), 0)
            + lax.broadcasted_iota(jnp.int32, (CHUNK, CHUNK), 1) == CHUNK - 1).astype(BF16)
    gf = _dot(flip, hi) + _dot(flip, lo)
    skew = pltpu.roll(gf, 0, 1, stride=1, stride_axis=0)
    return jnp.sum(skew, axis=0, keepdims=True)


def _band_bias(w_row, band, rows):
    keys = w_row.shape[1]
    base = jnp.broadcast_to(w_row, (CHUNK, keys))
    skew = pltpu.roll(base, 0, 1, stride=1, stride_axis=0)
    skew = pltpu.roll(skew, keys - (CHUNK - 1), 1)
    col = lax.broadcasted_iota(jnp.int32, (CHUNK, keys), 1)
    chunk0 = jnp.where(col < band, skew, NEG)
    return jnp.concatenate(
        [chunk0] + [pltpu.roll(chunk0, cc * CHUNK, 1) for cc in range(1, rows // CHUNK)], axis=0)


def _silu_parts(g):
    sg = _sigmoid(g)
    return g * sg, sg * (1.0 + g * (1.0 - sg))


A_PAIRS = 2
A_LANES = 128 * A_PAIRS
A_STEPS = D_MODEL // A_LANES


def _a_specs():
    q = pl.BlockSpec((QBLK, A_LANES), lambda p, j: (j, p))
    ks = [pl.BlockSpec((QBLK, A_LANES), lambda p, j, b=b: (jnp.maximum(j - 2 + b, 0), A_STEPS + p)) for b in range(3)]
    vs = [pl.BlockSpec((QBLK, A_LANES), lambda p, j, b=b: (jnp.maximum(j - 2 + b, 0), 2 * A_STEPS + p))
          for b in range(3)]
    g = pl.BlockSpec((QBLK, A_LANES), lambda p, j: (j, 3 * A_STEPS + p))
    bias = pl.BlockSpec((A_PAIRS, 8, A_KEYS), lambda p, j: (p, 0, 0))
    return q, ks, vs, g, bias


def _a_fill_bias(w_ref, b_ref, j):
    _fill_bias(2 * A_PAIRS, lambda h: w_ref[h // 2, h % 2:h % 2 + 1, :], A_BAND, b_ref, j)


def _by_valid_key_blocks(j, fn):
    pl.when(j == 0)(functools.partial(fn, 1))
    pl.when(j == 1)(functools.partial(fn, 2))
    pl.when(j >= 2)(functools.partial(fn, 3))


def _fill_bias(n, get_row, band, bias_scr, j):
    @pl.when(j == 0)
    def _():
        for h in range(n):
            bias_scr[h] = _band_bias(get_row(h), band, bias_scr.shape[1])


def _row_sums_everywhere(r, sel):
    return jnp.where(sel, pltpu.roll(r, HEAD_DIM, 1), r)


def _own_everywhere(x, sel):
    return jnp.where(sel, x, pltpu.roll(x, HEAD_DIM, 1))


def _minus_rows(s, row_full):
    return jnp.concatenate([s[:, i:i + 128] - row_full for i in range(0, s.shape[1], 128)], axis=1)


def _attn_a_fwd(qkvg, bias, gather):
    t = qkvg.shape[0]
    nq = t // QBLK
    n_g = len(gather)
    q_spec, k_specs, v_specs, g_spec, bias_spec = _a_specs()

    def body(q_ref, k0, k1, k2, v0, v1, v2, g_ref, w_ref, *rest):
        shard_refs, rest = rest[:n_g], rest[n_g:]
        z_ref, o_ref, lse_ref = rest[:3]
        full_refs, (b_ref, *comm) = rest[3:3 + n_g], rest[3 + n_g:]
        p = pl.program_id(0)
        j = pl.program_id(1)
        start, forward, finish = _gather_phases(shard_refs, full_refs, *comm)
        pl.when(jnp.logical_and(p == 0, j == 0))(start)
        pl.when(jnp.logical_and(p == A_STEPS // 2, j == 0))(forward)
        _a_fill_bias(w_ref, b_ref, j)
        lane_lo = _lane_lo()
        sels = (lane_lo, jnp.logical_not(lane_lo))

        def attend(n_blocks):
            first_col = (3 - n_blocks) * QBLK
            for pp in range(A_PAIRS):
                cols = slice(128 * pp, 128 * (pp + 1))
                k = jnp.concatenate([r[:, cols] for r in (k0, k1, k2)[3 - n_blocks:]], axis=0)
                v = jnp.concatenate([r[:, cols] for r in (v0, v1, v2)[3 - n_blocks:]], axis=0)
                q = q_ref[:, cols]
                qm2 = jnp.concatenate([jnp.where(sel, q, jnp.zeros_like(q)) for sel in sels], axis=0) * SCALE
                s2 = _dot_nt(qm2, k)
                outs, lses = [], []
                for hh, sel in enumerate(sels):
                    s = s2[hh * QBLK:(hh + 1) * QBLK] + b_ref[2 * pp + hh, :, first_col:]
                    mx = jnp.max(s, axis=-1, keepdims=True)
                    e = jnp.exp(s - mx).astype(BF16)
                    r = _dot(e, jnp.where(sel, v, jnp.ones_like(v)))
                    l = _row_sums_everywhere(r, sel)
                    outs.append(r / l)
                    lses.append(mx + jnp.log(l))
                o = jnp.where(lane_lo, outs[0], outs[1])
                silu, _ = _silu_parts(g_ref[:, cols].astype(F32))
                o_ref[:, cols] = o.astype(BF16)
                z_ref[:, cols] = (o * silu).astype(BF16)
                lse_ref[:, cols] = jnp.where(lane_lo, lses[0], lses[1])

        _by_valid_key_blocks(j, attend)
        pl.when(jnp.logical_and(p == A_STEPS - 1, j == nq - 1))(finish)

    out_spec = pl.BlockSpec((QBLK, A_LANES), lambda p, j: (j, p))
    outs = pl.pallas_call(
        body, name="attn_a_fwd", grid=(A_STEPS, nq),
        in_specs=[q_spec, *k_specs, *v_specs, g_spec, bias_spec] + [ANY] * n_g,
        out_specs=[out_spec, out_spec, out_spec] + [ANY] * n_g,
        out_shape=[SDS((t, D_MODEL), BF16), SDS((t, D_MODEL), BF16), SDS((t, D_MODEL), F32)]
        + [SDS((N_DEV, *s.shape), s.dtype) for s in gather],
        scratch_shapes=[pltpu.VMEM((2 * A_PAIRS, QBLK, A_KEYS), F32)] + _gather_scratch(n_g),
        compiler_params=_cparams(),
    )(qkvg, qkvg, qkvg, qkvg, qkvg, qkvg, qkvg, qkvg, bias, *gather)
    return outs[0], outs[1], outs[2], list(outs[3:])


def _attn_a_bwd(qkvg, bias, out_a, lse, dz, scatter):
    t = qkvg.shape[0]
    nq = t // QBLK
    n_sc = len(scatter)
    q_spec, k_specs, v_specs, g_spec, bias_spec = _a_specs()

    def body(q_ref, k0, k1, k2, v0, v1, v2, g_ref, w_ref, o_ref, lse_ref, dz_ref, *rest):
        sc_refs, rest = rest[:n_sc], rest[n_sc:]
        dqg_ref, dkv_ref, dg_ref = rest[:3]
        land_refs, rest = rest[3:3 + n_sc], rest[3 + n_sc:]
        dk_acc, dv_acc, gt_acc, b_ref, send_sems, recv_sems = rest
        j = pl.program_id(1)
        first = jnp.logical_and(pl.program_id(0) == 0, j == 0)
        last = jnp.logical_and(pl.program_id(0) == A_STEPS - 1, j == nq - 1)

        @pl.when(first)
        def _():
            for cp in _scatter_copies(sc_refs, land_refs, send_sems, recv_sems):
                cp.start()

        _a_fill_bias(w_ref, b_ref, j)

        @pl.when(j == 0)
        def _():
            dk_acc[...] = jnp.zeros_like(dk_acc)
            dv_acc[...] = jnp.zeros_like(dv_acc)
            gt_acc[...] = jnp.zeros_like(gt_acc)

        lane_lo = _lane_lo()
        sels = (lane_lo, jnp.logical_not(lane_lo))

        def attend(n_blocks):
            first_col = (3 - n_blocks) * QBLK
            for pp in range(A_PAIRS):
                cols = slice(128 * pp, 128 * (pp + 1))
                q = q_ref[:, cols]
                k = jnp.concatenate([r[:, cols] for r in (k0, k1, k2)[3 - n_blocks:]], axis=0)
                v = jnp.concatenate([r[:, cols] for r in (v0, v1, v2)[3 - n_blocks:]], axis=0)
                o = o_ref[:, cols].astype(F32)
                lse_pair = lse_ref[:, cols]
                dzf = dz_ref[:, cols].astype(F32)
                silu, dsilu = _silu_parts(g_ref[:, cols].astype(F32))
                do = dzf * silu
                dqg_ref[1, :, cols] = (dzf * o * dsilu).astype(BF16)
                doo = do * o
                qm2 = jnp.concatenate([jnp.where(sel, q, jnp.zeros_like(q)) for sel in sels], axis=0) * SCALE
                dom2 = jnp.concatenate([jnp.where(sel, do, 0.0) for sel in sels], axis=0).astype(BF16)
                s2 = _dot_nt(qm2, k)
                dp2 = _dot_nt(dom2, v)
                ps, dss = [], []
                for hh, sel in enumerate(sels):
                    rows = slice(hh * QBLK, (hh + 1) * QBLK)
                    s = s2[rows] + b_ref[2 * pp + hh, :, first_col:]
                    p = jnp.exp(_minus_rows(s, _own_everywhere(lse_pair, sel)))
                    delta = jnp.sum(jnp.where(sel, doo, 0.0), axis=-1, keepdims=True)
                    ds = p * (dp2[rows] - delta)
                    gt_acc[2 * pp + hh, :, first_col:] += ds
                    ps.append(p.astype(BF16))
                    dss.append(ds.astype(BF16))
                dsb2 = jnp.concatenate(dss, axis=0)
                dq2 = _dot(dsb2, k) * SCALE
                dk_blk = _dot_tn(dsb2, qm2)
                dv_blk = _dot_tn(jnp.concatenate(ps, axis=0), dom2)
                dqg_ref[0, :, cols] = jnp.where(lane_lo, dq2[0:QBLK], dq2[QBLK:2 * QBLK]).astype(BF16)
                for b in range(n_blocks):
                    rows = pl.ds(pl.multiple_of((j - n_blocks + 1 + b) * QBLK, QBLK), QBLK)
                    dk_acc[rows, cols] += dk_blk[b * QBLK:(b + 1) * QBLK]
                    dv_acc[rows, cols] += dv_blk[b * QBLK:(b + 1) * QBLK]

        _by_valid_key_blocks(j, attend)

        @pl.when(j == nq - 1)
        def _():
            dkv_ref[0] = dk_acc[...].astype(BF16)
            dkv_ref[1] = dv_acc[...].astype(BF16)
            for pp in range(A_PAIRS):
                dg_ref[pp] = jnp.concatenate([_offset_sums(gt_acc[2 * pp]), _offset_sums(gt_acc[2 * pp + 1]),
                                              jnp.zeros((6, A_DIAG), F32)], axis=0)

        @pl.when(last)
        def _():
            for cp in _scatter_copies(sc_refs, land_refs, send_sems, recv_sems):
                cp.wait()

    blk = pl.BlockSpec((QBLK, A_LANES), lambda p, j: (j, p))
    outs = pl.pallas_call(
        body, name="attn_a_bwd", grid=(A_STEPS, nq),
        in_specs=[q_spec, *k_specs, *v_specs, g_spec, bias_spec, blk, blk, blk] + [ANY] * n_sc,
        out_specs=[pl.BlockSpec((2, QBLK, A_LANES), lambda p, j: (0, j, p)),
                   pl.BlockSpec((2, t, A_LANES), lambda p, j: (0, 0, p)),
                   pl.BlockSpec((A_PAIRS, 8, A_DIAG), lambda p, j: (p, 0, 0))] + [ANY] * n_sc,
        out_shape=[SDS((2, t, D_MODEL), BF16), SDS((2, t, D_MODEL), BF16), SDS((N_HEADS // 2, 8, A_DIAG), F32)]
        + [SDS((N_DEV - 1, *g.shape[1:]), g.dtype) for g in scatter],
        scratch_shapes=[pltpu.VMEM((t, A_LANES), F32), pltpu.VMEM((t, A_LANES), F32),
                        pltpu.VMEM((2 * A_PAIRS, QBLK, A_KEYS), F32), pltpu.VMEM((2 * A_PAIRS, QBLK, A_KEYS), F32),
                        pltpu.SemaphoreType.DMA(((N_DEV - 1) * n_sc,)),
                        pltpu.SemaphoreType.DMA(((N_DEV - 1) * n_sc,))],
        compiler_params=_cparams(),
    )(qkvg, qkvg, qkvg, qkvg, qkvg, qkvg, qkvg, qkvg, bias, out_a, lse, dz, *scatter)
    return outs[0], outs[1], outs[2], list(outs[3:])


def _b_specs(qblk):
    per = qblk // B_PREV
    q = pl.BlockSpec((qblk, 512), lambda h, j: (j, h))
    g = pl.BlockSpec((qblk, 512), lambda h, j: (j, 2 + h))
    kp = pl.BlockSpec((B_PREV, 128), lambda h, j: (jnp.maximum(per * j - 1, 0), 0))
    kc = pl.BlockSpec((qblk, 128), lambda h, j: (j, 0))
    vp = pl.BlockSpec((B_PREV, 128), lambda h, j: (jnp.maximum(per * j - 1, 0), 1))
    vc = pl.BlockSpec((qblk, 128), lambda h, j: (j, 1))
    bias = pl.BlockSpec((B_GROUP, qblk + B_PREV), lambda h, j: (h, 0))
    sinks = pl.BlockSpec(memory_space=pltpu.SMEM)
    return q, g, kp, kc, vp, vc, bias, sinks


def _b_operands(kp, kc, vp, vc, kvh, with_prev):
    k = jnp.concatenate([kp[...], kc[...]], axis=0) if with_prev else kc[...]
    v = jnp.concatenate([vp[...], vc[...]], axis=0) if with_prev else vc[...]
    kr = pltpu.roll(k, HEAD_DIM, 1)
    vr = pltpu.roll(v, HEAD_DIM, 1)
    first = kvh == 0
    return (jnp.where(first, k, kr), jnp.where(first, kr, k),
            jnp.where(first, v, vr), jnp.where(first, vr, v))


def _attn_b_fwd(qg, kv, bias, sinks):
    t = qg.shape[0]
    qblk = B_QBLK_FWD
    q_spec, g_spec, kp_spec, kc_spec, vp_spec, vc_spec, bias_spec, sink_spec = _b_specs(qblk)

    def body(q_ref, g_ref, kp, kc, vp, vc, w_ref, sink_ref, z_ref, o_ref, lse_ref, b_ref):
        kvh = pl.program_id(0)
        j = pl.program_id(1)
        _fill_bias(B_GROUP, lambda h: w_ref[h:h + 1, :], B_BAND, b_ref, j)
        lane_lo = _lane_lo()
        n_pairs = B_GROUP // 2

        def attend(with_prev):
            first_col = 0 if with_prev else B_PREV
            k_lo, k_hi, v_lo, v_hi = _b_operands(kp, kc, vp, vc, kvh, with_prev)
            halves = []
            for hh, sel in enumerate((lane_lo, jnp.logical_not(lane_lo))):
                kk = k_lo if hh == 0 else k_hi
                vv = v_lo if hh == 0 else v_hi
                qm4 = jnp.concatenate(
                    [jnp.where(sel, q_ref[:, 128 * pp:128 * (pp + 1)], jnp.zeros((qblk, 128), BF16))
                     for pp in range(n_pairs)], axis=0) * SCALE
                s4 = _dot_nt(qm4, kk)
                es, mxs = [], []
                for pp in range(n_pairs):
                    g = 2 * pp + hh
                    s = s4[pp * qblk:(pp + 1) * qblk] + b_ref[g, :, first_col:]
                    mxs.append(jnp.maximum(jnp.max(s, axis=-1, keepdims=True), sink_ref[kvh * B_GROUP + g]))
                    es.append(jnp.exp(s - mxs[pp]).astype(BF16))
                r4 = _dot(jnp.concatenate(es, axis=0), jnp.where(sel, vv, jnp.ones_like(vv)))
                outs, lses = [], []
                for pp in range(n_pairs):
                    r = r4[pp * qblk:(pp + 1) * qblk]
                    l = _row_sums_everywhere(r, sel) + jnp.exp(sink_ref[kvh * B_GROUP + 2 * pp + hh] - mxs[pp])
                    outs.append(r / l)
                    lses.append(mxs[pp] + jnp.log(l))
                halves.append((outs, lses))
            for pp in range(n_pairs):
                cols = slice(128 * pp, 128 * (pp + 1))
                o = jnp.where(lane_lo, halves[0][0][pp], halves[1][0][pp])
                silu, _ = _silu_parts(g_ref[:, cols].astype(F32))
                o_ref[:, cols] = o.astype(BF16)
                z_ref[:, cols] = (o * silu).astype(BF16)
                lse_ref[:, cols] = jnp.where(lane_lo, halves[0][1][pp], halves[1][1][pp])

        pl.when(j == 0)(functools.partial(attend, False))
        pl.when(j >= 1)(functools.partial(attend, True))

    out_spec = pl.BlockSpec((qblk, 512), lambda h, j: (j, h))
    return pl.pallas_call(
        body, name="attn_b_fwd", grid=(B_KV_HEADS, t // qblk),
        in_specs=[q_spec, g_spec, kp_spec, kc_spec, vp_spec, vc_spec, bias_spec, sink_spec],
        out_specs=[out_spec, out_spec, out_spec],
        out_shape=[SDS((t, D_MODEL), BF16), SDS((t, D_MODEL), BF16), SDS((t, D_MODEL), F32)],
        scratch_shapes=[pltpu.VMEM((B_GROUP, qblk, qblk + B_PREV), F32)],
        compiler_params=_cparams(),
    )(qg, qg, kv, kv, kv, kv, bias, sinks)


def _attn_b_bwd(qg, kv, bias, sinks, out_b, lse, dz, bucket_onehot):
    t = qg.shape[0]
    qblk = B_QBLK_BWD
    keys = qblk + B_PREV
    nq = t // qblk
    q_spec, g_spec, kp_spec, kc_spec, vp_spec, vc_spec, bias_spec, sink_spec = _b_specs(qblk)

    def body(q_ref, g_ref, kp, kc, vp, vc, w_ref, sink_ref, o_ref, lse_ref, dz_ref, oh_ref,
             dqg_ref, dkv_ref, dt5_ref, dsink_ref, gt_acc, b_ref):
        kvh = pl.program_id(0)
        j = pl.program_id(1)
        _fill_bias(B_GROUP, lambda h: w_ref[h:h + 1, :], B_BAND, b_ref, j)

        @pl.when(jnp.logical_and(kvh == 0, j == 0))
        def _():
            dkv_ref[...] = jnp.zeros_like(dkv_ref)

        @pl.when(j == 0)
        def _():
            gt_acc[...] = jnp.zeros_like(gt_acc)
            dsink_ref[...] = jnp.zeros_like(dsink_ref)

        lane_lo = _lane_lo()

        def attend(with_prev):
            first_col = 0 if with_prev else B_PREV
            k_lo, k_hi, v_lo, v_hi = _b_operands(kp, kc, vp, vc, kvh, with_prev)
            dk_blk = jnp.zeros((keys - first_col, 128), F32)
            dv_blk = jnp.zeros((keys - first_col, 128), F32)
            for pp in range(B_GROUP // 2):
                cols = slice(128 * pp, 128 * (pp + 1))
                qp = q_ref[:, cols]
                o = o_ref[:, cols].astype(F32)
                lse_pair = lse_ref[:, cols]
                dzf = dz_ref[:, cols].astype(F32)
                silu, dsilu = _silu_parts(g_ref[:, cols].astype(F32))
                do = dzf * silu
                dqg_ref[1, :, cols] = (dzf * o * dsilu).astype(BF16)
                doo = do * o
                dqs = []
                for hh in range(2):
                    g = 2 * pp + hh
                    sel = lane_lo if hh == 0 else jnp.logical_not(lane_lo)
                    sink = sink_ref[kvh * B_GROUP + g]
                    kk = k_lo if hh == 0 else k_hi
                    vv = v_lo if hh == 0 else v_hi
                    qm = jnp.where(sel, qp, jnp.zeros_like(qp)) * SCALE
                    s = _dot_nt(qm, kk) + b_ref[g, :, first_col:]
                    lse_h = _own_everywhere(lse_pair, sel)
                    p = jnp.exp(_minus_rows(s, lse_h))
                    delta = jnp.sum(jnp.where(sel, doo, 0.0), axis=-1, keepdims=True)
                    dom = jnp.where(sel, do, 0.0).astype(BF16)
                    dp = _dot_nt(dom, vv)
                    ds = p * (dp - delta)
                    gt_acc[g, :, first_col:] += ds
                    dsink_ref[g:g + 1, :] -= jnp.sum(jnp.exp(sink - lse_h) * delta, axis=0, keepdims=True)
                    dsb = ds.astype(BF16)
                    dqs.append(_dot(dsb, kk) * SCALE)
                    dk_blk = dk_blk + _dot_tn(dsb, qm)
                    dv_blk = dv_blk + _dot_tn(p.astype(BF16), dom)
                dqg_ref[0, :, cols] = jnp.where(lane_lo, dqs[0], dqs[1]).astype(BF16)
            mine = lane_lo == (kvh == 0)
            dk_add = jnp.where(mine, dk_blk + pltpu.roll(dk_blk, HEAD_DIM, 1), 0.0)
            dv_add = jnp.where(mine, dv_blk + pltpu.roll(dv_blk, HEAD_DIM, 1), 0.0)
            first_key = B_PREV if with_prev else 0
            if with_prev:
                rows = pl.ds(pl.multiple_of(j * qblk - B_PREV, B_PREV), B_PREV)
                dkv_ref[0, rows, :] += dk_add[0:B_PREV]
                dkv_ref[1, rows, :] += dv_add[0:B_PREV]
            rows = pl.ds(pl.multiple_of(j * qblk, qblk), qblk)
            dkv_ref[0, rows, :] += dk_add[first_key:first_key + qblk]
            dkv_ref[1, rows, :] += dv_add[first_key:first_key + qblk]

        pl.when(j == 0)(functools.partial(attend, False))
        pl.when(j >= 1)(functools.partial(attend, True))

        @pl.when(j == nq - 1)
        def _():
            dd = jnp.concatenate([_offset_sums(gt_acc[g]) for g in range(B_GROUP)], axis=0)
            hi = dd.astype(BF16)
            lo = (dd - hi.astype(F32)).astype(BF16)
            dt5_ref[...] = _dot(hi, oh_ref[...]) + _dot(lo, oh_ref[...])

    blk = pl.BlockSpec((qblk, 512), lambda h, j: (j, h))
    return pl.pallas_call(
        body, name="attn_b_bwd", grid=(B_KV_HEADS, nq),
        in_specs=[q_spec, g_spec, kp_spec, kc_spec, vp_spec, vc_spec, bias_spec, sink_spec, blk, blk, blk,
                  pl.BlockSpec((keys, 128), lambda h, j: (0, 0))],
        out_specs=[pl.BlockSpec((2, qblk, 512), lambda h, j: (0, j, h)),
                   pl.BlockSpec((2, t, 128), lambda h, j: (0, 0, 0)),
                   pl.BlockSpec((B_GROUP, 128), lambda h, j: (h, 0)),
                   pl.BlockSpec((B_GROUP, 128), lambda h, j: (h, 0))],
        out_shape=[SDS((2, t, D_MODEL), BF16), SDS((2, t, 128), F32),
                   SDS((N_HEADS, 128), F32), SDS((N_HEADS, 128), F32)],
        scratch_shapes=[pltpu.VMEM((B_GROUP, qblk, keys), F32), pltpu.VMEM((B_GROUP, qblk, keys), F32)],
        compiler_params=_cparams(),
    )(qg, qg, kv, kv, kv, kv, bias, sinks, out_b, lse, dz, bucket_onehot)


def _a_bias_by_offset(rel_bias):
    m = np.arange(A_DIAG)
    idx = np.clip(A_BAND - 1 - m, -A_REL_CLIP, A_REL_CLIP) + A_REL_CLIP
    by_head = rel_bias[idx].T.reshape(N_HEADS // 2, 2, A_DIAG)
    return jnp.concatenate([by_head, jnp.zeros((N_HEADS // 2, 6, A_DIAG), F32)], axis=1)


def _a_bias_grad(offset_sums):
    first = 319
    tail = jnp.sum(offset_sums[:, :first], axis=1)
    body = jnp.flip(offset_sums[:, first:first + 320], axis=1)
    body = body.at[:, -1].add(tail)
    full = jnp.concatenate([jnp.zeros((N_HEADS, 193), F32), body], axis=1)
    return full


def _t5_bucket(rel):
    nb = T5_BUCKETS // 2
    max_exact = nb // 2
    ret = jnp.where(rel > 0, nb, 0)
    n = jnp.abs(rel)
    nf = jnp.maximum(n, 1).astype(jnp.float32)
    large = max_exact + (jnp.log(nf / max_exact) / math.log(T5_MAX_DIST / max_exact)
                         * (nb - max_exact)).astype(jnp.int32)
    large = jnp.minimum(large, nb - 1)
    return ret + jnp.where(n < max_exact, n, large)


def _b_offset_buckets(keys):
    return _t5_bucket(jnp.arange(keys, dtype=jnp.int32) - (B_LEFT_CHUNKS * CHUNK + CHUNK - 1))


def _b_bias_by_offset(t5_table, keys):
    return t5_table[_b_offset_buckets(keys)].T


def _b_bucket_onehot(keys):
    return (_b_offset_buckets(keys)[:, None] == jnp.arange(128)[None, :]).astype(BF16)


def _local_step(my_slot, order, x, target, a_gain_shard, w_in_a_shard, rel_bias, late_shards, kv_gain,
                t5_table, b_gain, sinks, f_gain):
    a_bias = _a_bias_by_offset(rel_bias)
    b_bias_fwd = _b_bias_by_offset(t5_table, B_QBLK_FWD + B_PREV)
    b_bias_bwd = _b_bias_by_offset(t5_table, B_QBLK_BWD + B_PREV)
    sinks_flat = sinks.reshape(N_HEADS)

    xn, qkvg, w_in_a, a_gain = _norm_matmul_gather(order, x, a_gain_shard, w_in_a_shard)
    z_a, out_a, lse_a, (w_in_b, w_out_a, w_out_b, kv_w) = _attn_a_fwd(qkvg, a_bias, late_shards)
    w_out_a = w_out_a.reshape(D_MODEL, D_MODEL)
    w_out_b = w_out_b.reshape(D_MODEL, D_MODEL)
    kv_w = kv_w.reshape(D_MODEL, 2 * 128)
    h1, kvn, hb, kv, qg = _layer_a_out(x, z_a, w_out_a, kv_gain, b_gain, kv_w, w_in_b)
    z_b, out_b, lse_b = _attn_b_fwd(qg, kv, b_bias_fwd, sinks_flat)
    dh2, dh2b, dz_b, loss, d_fn = _layer_b_out_loss(h1, z_b, w_out_b, f_gain, target)

    dqg_b, dkv_b, d_t5, d_sink = _attn_b_bwd(qg, kv, b_bias_bwd, sinks_flat, out_b, lse_b, dz_b,
                                             _b_bucket_onehot(B_QBLK_BWD + B_PREV))
    dh1, dh1b, dz_a, d_bn, d_kn = _layer_b_in_bwd(dqg_b, dkv_b, w_in_b, kv_w, h1, dh2, b_gain, kv_gain, w_out_a)
    early = dict(
        b_w_out=_weight_grad_rows("grad_b_w_out", my_slot, z_b, dh2b[None]),
        b_w_in=_weight_grad_cols("grad_b_w_in", my_slot, hb, [dqg_b],
                                 [(0, o, c, 4 * o + c) for o in range(2) for c in range(4)], 256),
        kv_w=_weight_grad_rows("grad_kv_w", my_slot, kvn, dkv_b),
        a_w_out=_weight_grad_rows("grad_a_w_out", my_slot, z_a, dh1b[None]))
    dqg_a, dkv_a, d_rel, landed = _attn_a_bwd(qkvg, a_bias, out_a, lse_a, dz_a, [g[0] for g in early.values()])
    g_w_in_a = _weight_grad_cols(
        "grad_a_w_in", my_slot, xn, [dqg_a, dkv_a],
        [(0, 0, 0, 0), (0, 0, 1, 1), (1, 0, 0, 2), (1, 0, 1, 3), (1, 1, 0, 4), (1, 1, 1, 5), (0, 1, 0, 6), (0, 1, 1, 7)], 512)
    from_sibling, = _exchange_sibling([g_w_in_a[0]])
    x_i, y_i, c_i, chips = _place()
    del x_i, y_i
    forward_slots = jnp.stack([_slot(*chip, c_i) for chip in chips]).astype(jnp.int32)
    chip_sums = _pre_reduce("chip_sum_a_w_in", g_w_in_a[0], from_sibling, forward_slots)
    grad_x, d_an, from_chips = _layer_a_in_bwd(dqg_a, dkv_a, w_in_a, x, dh1, a_gain, chip_sums)

    matrices = {n: (g[1], [(land, 0, N_DEV - 1)]) for (n, g), land in zip(early.items(), landed)}
    matrices["a_w_in"] = (g_w_in_a[1], [(from_sibling, 3, 1), (from_chips, 0, 3)])
    small = dict(
        loss=loss, a_norm=d_an, a_rel_bias=d_rel[:, :2].reshape(N_HEADS, A_DIAG),
        kv_norm=d_kn, t5_bias=d_t5, b_norm=d_bn, b_sinks=d_sink, final_norm=d_fn)
    return grad_x, small, matrices


def _place():
    x, y, c = lax.axis_index("x"), lax.axis_index("y"), lax.axis_index("c")
    chips = [(1 - x, y), (x, 1 - y), (1 - x, 1 - y)]
    return x, y, c, chips


def _slot(px, py, pc):
    return 4 * px + 2 * py + pc


ANY = pl.BlockSpec(memory_space=pl.ANY)


def _peer(x, y, c, k):
    return (x ^ (k >> 2), y ^ ((k >> 1) & 1), c ^ (k & 1))


def _scatter_copies(grad_refs, land_refs, send_sems, recv_sems):
    x, y, c, _ = _place()
    copies = []
    for t, (grad, land) in enumerate(zip(grad_refs, land_refs)):
        for k in range(1, N_DEV):
            peer = _peer(x, y, c, k)
            sem = (N_DEV - 1) * t + k - 1
            copies.append(pltpu.make_async_remote_copy(
                src_ref=grad.at[_slot(*peer)], dst_ref=land.at[k - 1],
                send_sem=send_sems.at[sem], recv_sem=recv_sems.at[sem],
                device_id=peer, device_id_type=MESH))
    return copies


def _gather_phases(ins, outs, send_sems, recv_sems, local_sems):
    n = len(ins)
    x, y, c, chips = _place()
    me, sibling = (x, y, c), (x, y, 1 - c)

    def copy(t, k, block, to, src=None):
        dst = outs[t].at[_slot(*block)]
        return pltpu.make_async_remote_copy(
            src_ref=dst if src is None else src, dst_ref=dst,
            send_sem=send_sems.at[7 * t + k], recv_sem=recv_sems.at[7 * t + k],
            device_id=to, device_id_type=MESH)

    def lists():
        mine = [pltpu.make_async_copy(ins[t], outs[t].at[_slot(*me)], local_sems.at[t]) for t in range(n)]
        first = []
        for t in range(n):
            first.append(copy(t, 0, me, sibling, src=ins[t]))
            first += [copy(t, 1 + j, me, (*chip, c), src=ins[t]) for j, chip in enumerate(chips)]
        passed = [copy(t, 4 + j, (*chip, c), sibling) for t in range(n) for j, chip in enumerate(chips)]
        return mine, first, passed

    def start():
        mine, first, _ = lists()
        for cp in mine + first:
            cp.start()

    def forward():
        _, _, passed = lists()
        for t in range(n):
            for j, chip in enumerate(chips):
                copy(t, 1 + j, (*chip, c), me).wait_recv()
                passed[3 * t + j].start()

    def finish():
        mine, first, passed = lists()
        for t in range(n):
            copy(t, 0, sibling, me).wait_recv()
            for j, chip in enumerate(chips):
                copy(t, 4 + j, (*chip, 1 - c), me).wait_recv()
        for cp in first + passed:
            cp.wait_send()
        for cp in mine:
            cp.wait()

    return start, forward, finish


def _gather_scratch(n):
    return [pltpu.SemaphoreType.DMA((7 * n,)), pltpu.SemaphoreType.DMA((7 * n,)), pltpu.SemaphoreType.DMA((n,))]


def _exchange_sibling(grads):
    n = len(grads)

    def body(*refs):
        ins, outs = refs[:n], refs[n:2 * n]
        send_sems, recv_sems = refs[2 * n:]
        x, y, c, chips = _place()
        sibling = (x, y, 1 - c)
        copies = []
        for t in range(n):
            blocks = [(*chip, 1 - c) for chip in chips] + [sibling]
            for k, block in enumerate(blocks):
                copies.append(pltpu.make_async_remote_copy(
                    src_ref=ins[t].at[_slot(*block)], dst_ref=outs[t].at[k],
                    send_sem=send_sems.at[4 * t + k], recv_sem=recv_sems.at[4 * t + k],
                    device_id=sibling, device_id_type=MESH))
        for cp in copies:
            cp.start()
        for cp in copies:
            cp.wait()

    return pl.pallas_call(
        body, name="grads_to_sibling",
        in_specs=[ANY] * n, out_specs=[ANY] * n,
        out_shape=[SDS((4, *g.shape[1:]), g.dtype) for g in grads],
        scratch_shapes=[pltpu.SemaphoreType.DMA((4 * n,)), pltpu.SemaphoreType.DMA((4 * n,))],
    )(*grads)


def _chip_copies(sums_ref, land_ref, send_sems, recv_sems):
    x, y, c, chips = _place()
    del x, y
    return [pltpu.make_async_remote_copy(
        src_ref=sums_ref.at[j], dst_ref=land_ref.at[j], send_sem=send_sems.at[j], recv_sem=recv_sems.at[j],
        device_id=(*chip, c), device_id_type=MESH) for j, chip in enumerate(chips)]


def _row_tile(rows):
    return min(rows, 256)


def _pre_reduce(name, g, from_sibling, slots):
    _, r, c = g.shape
    tr = _row_tile(r)

    def body(slots_ref, g_ref, s_ref, o_ref):
        del slots_ref
        o_ref[...] = (g_ref[...].astype(F32) + s_ref[...].astype(F32)).astype(BF16)

    return pl.pallas_call(
        body, name=name,
        grid_spec=pltpu.PrefetchScalarGridSpec(
            num_scalar_prefetch=1, grid=(3, r // tr),
            in_specs=[pl.BlockSpec((1, tr, c), lambda j, i, s: (s[j], i, 0)),
                      pl.BlockSpec((1, tr, c), lambda j, i, s: (j, i, 0))],
            out_specs=pl.BlockSpec((1, tr, c), lambda j, i, s: (j, i, 0))),
        out_shape=SDS((3, r, c), BF16),
        compiler_params=_cparams(),
    )(slots, g, from_sibling)


def _adamw(w, g, m, v):
    m2 = ADAM_B1 * m + (1.0 - ADAM_B1) * g
    v2 = ADAM_B2 * v + (1.0 - ADAM_B2) * jnp.square(g)
    m_hat = m2 / (1.0 - ADAM_B1 ** ADAM_STEP)
    v_hat = v2 / (1.0 - ADAM_B2 ** ADAM_STEP)
    delta = -ADAM_LR * (m_hat / (jnp.sqrt(v_hat) + ADAM_EPS) + ADAM_WD * w)
    return delta, m2, v2


def _reduce_adamw(name, own, partials, w, m, v):
    r, c = own.shape
    tr = _row_tile(r)
    n_p = len(partials)

    def body(own_ref, *rest):
        p_refs, (w_ref, m_ref, v_ref, grad_ref, d_ref, nm_ref, nv_ref) = rest[:n_p], rest[n_p:]
        grad = own_ref[...]
        for p_ref, (_, _, count) in zip(p_refs, partials):
            for j in range(count):
                grad = grad + p_ref[j].astype(F32)
        grad_ref[...] = grad
        d_ref[...], nm_ref[...], nv_ref[...] = _adamw(w_ref[...], grad, m_ref[...], v_ref[...])

    flat = pl.BlockSpec((tr, c), lambda i: (i, 0))
    return pl.pallas_call(
        body, name=name, grid=(r // tr,),
        in_specs=[flat] + [pl.BlockSpec((count, tr, c), lambda i, first=first, count=count: (first // count, i, 0))
                           for _, first, count in partials] + [flat, flat, flat],
        out_specs=[flat, flat, flat, flat],
        out_shape=[SDS((r, c), F32)] * 4,
        compiler_params=_cparams(),
    )(own, *[p[0] for p in partials], w, m, v)


VM = pl.BlockSpec()


def _small_allreduce(parts):
    n = len(parts)

    def body(*refs):
        ins, outs, lands = refs[:n], refs[n:2 * n], refs[2 * n:3 * n]
        send_sems, recv_sems = refs[3 * n:]
        x, y, c, _ = _place()
        my_slot = _slot(x, y, c)
        copies = []
        for t in range(n):
            lands[t][my_slot] = ins[t][...]
            for k in range(1, N_DEV):
                sem = (N_DEV - 1) * t + k - 1
                copies.append(pltpu.make_async_remote_copy(
                    src_ref=ins[t], dst_ref=lands[t].at[my_slot],
                    send_sem=send_sems.at[sem], recv_sem=recv_sems.at[sem],
                    device_id=_peer(x, y, c, k), device_id_type=MESH))
        for cp in copies:
            cp.start()
        for t in range(n):
            for k in range(1, N_DEV):
                sem = (N_DEV - 1) * t + k - 1
                pltpu.make_async_remote_copy(
                    src_ref=ins[t], dst_ref=lands[t].at[_slot(*_peer(x, y, c, k))],
                    send_sem=send_sems.at[sem], recv_sem=recv_sems.at[sem],
                    device_id=(x, y, c), device_id_type=MESH).wait_recv()
        for cp in copies:
            cp.wait_send()
        for t in range(n):
            total = lands[t][0]
            for s in range(1, N_DEV):
                total = total + lands[t][s]
            outs[t][...] = total

    n_sems = (N_DEV - 1) * n
    return pl.pallas_call(
        body, name="small_allreduce",
        in_specs=[VM] * n, out_specs=[VM] * n, out_shape=[SDS(p.shape, F32) for p in parts],
        scratch_shapes=[pltpu.VMEM((N_DEV, *p.shape), F32) for p in parts]
        + [pltpu.SemaphoreType.DMA((n_sems,)), pltpu.SemaphoreType.DMA((n_sems,))],
    )(*parts)


def _small_adamw(my_slot, sums, ws, ms, vs):
    n = len(ws)

    def body(slot_ref, *refs):
        sum_refs, refs = refs[:n + 1], refs[n + 1:]
        w_refs, m_refs, v_refs, refs = refs[:n], refs[n:2 * n], refs[2 * n:3 * n], refs[3 * n:]
        g_refs, d_refs, nm_refs, nv_refs = refs[:n + 1], refs[n + 1:2 * n + 1], refs[2 * n + 1:3 * n + 1], refs[3 * n + 1:]
        for t in range(n + 1):
            if t == 0:
                g = sum_refs[0][:, pl.ds(pl.multiple_of(slot_ref[0] * 128, 128), 128)]
            else:
                g = sum_refs[t][...]
            g_refs[t][...] = g
            if t < n:
                d_refs[t][...], nm_refs[t][...], nv_refs[t][...] = _adamw(w_refs[t][...], g, m_refs[t][...], v_refs[t][...])

    shapes = [SDS(w.shape, F32) for w in ws]
    outs = pl.pallas_call(
        body, name="small_adamw",
        in_specs=[pl.BlockSpec(memory_space=pltpu.SMEM)] + [VM] * (4 * n + 1),
        out_specs=[VM] * (4 * n + 1),
        out_shape=shapes + [SDS(sums[-1].shape, F32)] + shapes * 3,
    )(my_slot, *sums, *ws, *ms, *vs)
    return outs[:n + 1], outs[n + 1:2 * n + 1], outs[2 * n + 1:3 * n + 1], outs[3 * n + 1:]


def kernel(x, a_norm, a_w_in, a_rel_bias, a_w_out, kv_norm, kv_w, t5_bias, b_norm, b_w_in, b_sinks, b_w_out, final_norm, loss_target, m_a_norm, m_a_w_in, m_a_rel_bias, m_a_w_out, m_kv_norm, m_kv_w, m_t5_bias, m_b_norm, m_b_w_in, m_b_sinks, m_b_w_out, m_final_norm, v_a_norm, v_a_w_in, v_a_rel_bias, v_a_w_out, v_kv_norm, v_kv_w, v_t5_bias, v_b_norm, v_b_w_in, v_b_sinks, v_b_w_out, v_final_norm):
    xi, yi, ci = lax.axis_index("x"), lax.axis_index("y"), lax.axis_index("c")
    my_slot = _slot(xi, yi, ci)

    slot_arr = jnp.reshape(my_slot, (1,)).astype(jnp.int32)
    order = _gather_order(xi, yi, ci)
    late_shards = [b_w_in[0].astype(BF16), a_w_out[0].astype(BF16), b_w_out[0].astype(BF16), kv_w.astype(BF16)]
    grad_x, loc, matrices = _local_step(
        slot_arr, order, x[0], loss_target[0], a_norm, a_w_in[0].astype(BF16), a_rel_bias[0], late_shards,
        kv_norm.reshape(1, D_MODEL), t5_bias, b_norm, b_sinks, final_norm.reshape(1, D_MODEL))

    shard_w = dict(a_w_in=a_w_in[0], b_w_in=b_w_in[0], a_w_out=a_w_out[0], b_w_out=b_w_out[0], kv_w=kv_w)
    shard_m = dict(a_w_in=m_a_w_in[0], b_w_in=m_b_w_in[0], a_w_out=m_a_w_out[0], b_w_out=m_b_w_out[0], kv_w=m_kv_w)
    shard_v = dict(a_w_in=v_a_w_in[0], b_w_in=v_b_w_in[0], a_w_out=v_a_w_out[0], b_w_out=v_b_w_out[0], kv_w=v_kv_w)
    big = {n: _reduce_adamw("adamw_" + n, own, partials, shard_w[n], shard_m[n], shard_v[n])
           for n, (own, partials) in matrices.items()}

    names = ("a_norm", "a_rel_bias", "kv_norm", "t5_bias", "b_norm", "b_sinks", "final_norm")
    tables = ("a_rel_bias", "t5_bias")

    def row(n, a):
        return a.reshape(-1, a.shape[-1]).T if n in tables else a.reshape(1, -1)

    small_w = [row(n, a) for n, a in zip(names, (a_norm, a_rel_bias, kv_norm, t5_bias, b_norm, b_sinks, final_norm))]
    small_m = [row(n, a) for n, a in zip(names, (m_a_norm, m_a_rel_bias, m_kv_norm, m_t5_bias, m_b_norm, m_b_sinks,
                                                 m_final_norm))]
    small_v = [row(n, a) for n, a in zip(names, (v_a_norm, v_a_rel_bias, v_kv_norm, v_t5_bias, v_b_norm, v_b_sinks,
                                                 v_final_norm))]
    sums = dict(zip(names + ("loss",), _small_allreduce([loc[n] for n in names] + [loc["loss"]])))
    sums["a_rel_bias"] = _a_bias_grad(sums["a_rel_bias"])
    sums["t5_bias"] = sums["t5_bias"][:, :T5_BUCKETS]
    sums["b_sinks"] = sums["b_sinks"][:, 0].reshape(1, N_HEADS)
    results = _small_adamw(slot_arr, [sums[n] for n in names + ("loss",)], small_w, small_m, small_v)
    like = dict(a_norm=a_norm, a_rel_bias=a_rel_bias, kv_norm=kv_norm, t5_bias=t5_bias, b_norm=b_norm,
                b_sinks=b_sinks, final_norm=final_norm)
    sm = [{n: (part[i].T if n in tables else part[i]).reshape(like[n].shape) for i, n in enumerate(names)}
          for part in results]
    loss = results[0][len(names)][0, 0]

    order = ("a_norm", "a_w_in", "a_rel_bias", "a_w_out", "kv_norm", "kv_w", "t5_bias", "b_norm",
             "b_w_in", "b_sinks", "b_w_out", "final_norm")
    lead = dict(a_w_in=True, b_w_in=True, a_w_out=True, b_w_out=True, kv_w=False)

    def pick(kind, name):
        if name in big:
            val = big[name][kind]
            return val[None] if lead[name] else val
        return sm[kind][name]

    outs = [loss, grad_x[None]]
    for kind in range(4):
        outs += [pick(kind, n) for n in order]
    return tuple(outs)
```

```python
import functools
import math

import numpy as np
import jax
import jax.numpy as jnp
from jax import lax
from jax.experimental import pallas as pl
from jax.experimental.pallas import tpu as pltpu

F32 = jnp.float32
BF16 = jnp.bfloat16
SDS = jax.ShapeDtypeStruct

D_MODEL = 1024
HEAD_DIM = 64
CHUNK = 64
N_HEADS = 16
RMS_EPS = 1e-6
A_LEFT_CHUNKS = 8
A_BAND = (A_LEFT_CHUNKS + 1) * CHUNK
A_REL_CLIP = 256
B_KV_HEADS = 2
B_GROUP = 8
B_LEFT_CHUNKS = 2
B_BAND = (B_LEFT_CHUNKS + 1) * CHUNK
T5_BUCKETS = 32
T5_MAX_DIST = 128
QBLK = 256
A_KEYS = 3 * QBLK
B_QBLK_FWD = 128
B_QBLK_BWD = 256
B_PREV = 128
A_DIAG = A_KEYS
NEG = -1e30
SCALE = HEAD_DIM ** -0.5
N_DEV = 8

ADAM_LR = 0.001
ADAM_B1 = 0.9
ADAM_B2 = 0.999
ADAM_EPS = 1e-08
ADAM_WD = 0.01
ADAM_STEP = 10

VMEM_LIMIT_BYTES = 56 * 1024 * 1024
MESH = pl.DeviceIdType.MESH


def _cparams():
    return pltpu.CompilerParams(vmem_limit_bytes=VMEM_LIMIT_BYTES)


def _dot(a, b):
    return jnp.dot(a, b, preferred_element_type=F32)


def _dot_nt(a, b):
    return lax.dot_general(a, b, (((1,), (1,)), ((), ())), preferred_element_type=F32)


def _dot_tn(a, b):
    return lax.dot_general(a, b, (((0,), (0,)), ((), ())), preferred_element_type=F32)


def _rstd(xf):
    return lax.rsqrt(jnp.mean(xf * xf, axis=-1, keepdims=True) + RMS_EPS)


def _sigmoid(x):
    return 1.0 / (1.0 + jnp.exp(-x))


_GATHER_SEQUENCE = ((0, None), (1, 0), (2, 1), (4, None), (5, None), (3, 2), (6, None))


def _gather_order(x, y, c):
    others = [(1 - x, y), (x, 1 - y), (1 - x, 1 - y)]
    arrivals = [_slot(x, y, 1 - c)] + [_slot(*chip, c) for chip in others] + [_slot(*chip, 1 - c) for chip in others]
    return jnp.stack([_slot(x, y, c)] + [arrivals[a] for a, _ in _GATHER_SEQUENCE]).astype(jnp.int32)


def _norm_matmul_gather(order, x, gain_shard, w_shard):
    t = x.shape[0]
    dw, tn = w_shard.shape
    tm = min(t, 1024)
    n_m = t // tm

    def body(order_ref, x_ref, gs_ref, shard_ref, xn_ref, o_ref, full_ref, gain_ref,
             xn_all, wbuf, gland, send_sems, recv_sems, gsend_sems, grecv_sems, load_sems, own_sem):
        n, m = pl.program_id(0), pl.program_id(1)
        x_i, y_i, c_i, chips = _place()
        me, sibling = (x_i, y_i, c_i), (x_i, y_i, 1 - c_i)

        def send(k, block, to, src=None):
            dst = full_ref.at[_slot(*block)]
            return pltpu.make_async_remote_copy(
                src_ref=dst if src is None else src, dst_ref=dst,
                send_sem=send_sems.at[k], recv_sem=recv_sems.at[k], device_id=to, device_id_type=MESH)

        own = pltpu.make_async_copy(shard_ref, full_ref.at[_slot(*me)], own_sem)
        first = [send(0, me, sibling, src=shard_ref)]
        first += [send(1 + j, me, (*chip, c_i), src=shard_ref) for j, chip in enumerate(chips)]
        forwards = [send(4 + j, (*chip, c_i), sibling) for j, chip in enumerate(chips)]
        arrivals = [send(0, sibling, me)] + [send(1 + j, (*chip, c_i), me) for j, chip in enumerate(chips)]
        arrivals += [send(4 + j, (*chip, 1 - c_i), me) for j, chip in enumerate(chips)]
        gains = [pltpu.make_async_remote_copy(
            src_ref=gs_ref, dst_ref=gland.at[_slot(*me)], send_sem=gsend_sems.at[k - 1],
            recv_sem=grecv_sems.at[k - 1], device_id=_peer(x_i, y_i, c_i, k), device_id_type=MESH)
            for k in range(1, N_DEV)]

        @pl.when(jnp.logical_and(n == 0, m == 0))
        def _():
            own.start()
            for cp in gains + first:
                cp.start()
            pltpu.make_async_copy(shard_ref, wbuf.at[0], load_sems.at[0]).start()
            gland[_slot(*me)] = gs_ref[...]
            for k in range(1, N_DEV):
                pltpu.make_async_remote_copy(
                    src_ref=gs_ref, dst_ref=gland.at[_slot(*_peer(x_i, y_i, c_i, k))],
                    send_sem=gsend_sems.at[k - 1], recv_sem=grecv_sems.at[k - 1],
                    device_id=me, device_id_type=MESH).wait_recv()
            for s in range(N_DEV):
                gain_ref[:, 128 * s:128 * (s + 1)] = gland[s]

        rows = pl.ds(pl.multiple_of(m * tm, tm), tm)

        @pl.when(n == 0)
        def _():
            xf = x_ref[...]
            xn = ((xf * _rstd(xf)) * gain_ref[...]).astype(BF16)
            xn_all[rows, :] = xn
            xn_ref[...] = xn

        @pl.when(m == 0)
        def _():
            pltpu.make_async_copy(full_ref.at[0], wbuf.at[n % 2], load_sems.at[n % 2]).wait()

        o_ref[...] = _dot(xn_all[rows, :], wbuf[n % 2]).astype(BF16)

        for k, (arrival, forward) in enumerate(_GATHER_SEQUENCE):
            @pl.when(jnp.logical_and(n == k, m == n_m - 1))
            def _(k=k, arrival=arrival, forward=forward):
                arrivals[arrival].wait_recv()
                if forward is not None:
                    forwards[forward].start()
                pltpu.make_async_copy(full_ref.at[order_ref[k + 1]], wbuf.at[(k + 1) % 2],
                                      load_sems.at[(k + 1) % 2]).start()

        @pl.when(jnp.logical_and(n == N_DEV - 1, m == n_m - 1))
        def _():
            for cp in gains + first + forwards:
                cp.wait_send()
            own.wait()

    held = lambda n, m, order: (jnp.where(n == 0, m, n_m - 1), 0)
    return pl.pallas_call(
        body, name="norm_matmul_gather",
        grid_spec=pltpu.PrefetchScalarGridSpec(
            num_scalar_prefetch=1, grid=(N_DEV, n_m),
            in_specs=[pl.BlockSpec((tm, D_MODEL), held),
                      pl.BlockSpec((1, 128), lambda n, m, order: (0, 0)), ANY],
            out_specs=[pl.BlockSpec((tm, D_MODEL), held),
                       pl.BlockSpec((tm, tn), lambda n, m, order: (m, order[n])),
                       ANY, pl.BlockSpec((1, D_MODEL), lambda n, m, order: (0, 0))],
            scratch_shapes=[pltpu.VMEM((t, D_MODEL), BF16), pltpu.VMEM((2, dw, tn), BF16),
                            pltpu.VMEM((N_DEV, 1, 128), F32),
                            pltpu.SemaphoreType.DMA((7,)), pltpu.SemaphoreType.DMA((7,)),
                            pltpu.SemaphoreType.DMA((7,)), pltpu.SemaphoreType.DMA((7,)),
                            pltpu.SemaphoreType.DMA((2,)), pltpu.SemaphoreType.DMA]),
        out_shape=[SDS((t, D_MODEL), BF16), SDS((t, N_DEV * tn), BF16), SDS((N_DEV, dw, tn), BF16),
                   SDS((1, D_MODEL), F32)],
        compiler_params=_cparams(),
    )(order, x, gain_shard, w_shard)


def _layer_a_out(x, z, w_out, kv_gain, b_gain, kv_w, w_in_b):
    t = x.shape[0]
    tm = min(t, 512)
    nb, _, tn = w_in_b.shape

    def body(x_ref, z_ref, wo_ref, kvg_ref, bg_ref, kvw_ref, wb_ref,
             h1_ref, kvn_ref, hb_ref, kv_ref, qg_ref):
        h1 = x_ref[...] + _dot(z_ref[...], wo_ref[...])
        h1_ref[...] = h1
        y0 = h1 * _rstd(h1)
        kvn = (y0 * kvg_ref[...]).astype(BF16)
        hb = (y0 * bg_ref[...]).astype(BF16)
        kvn_ref[...] = kvn
        hb_ref[...] = hb
        kv_ref[...] = _dot(kvn, kvw_ref[...]).astype(BF16)
        for i in range(nb):
            qg_ref[:, i * tn:(i + 1) * tn] = _dot(hb, wb_ref[i]).astype(BF16)

    row = lambda m: (m, 0)
    fix2 = lambda m: (0, 0)
    return pl.pallas_call(
        body, name="layer_a_out", grid=(t // tm,),
        in_specs=[pl.BlockSpec((tm, D_MODEL), row), pl.BlockSpec((tm, D_MODEL), row),
                  pl.BlockSpec((D_MODEL, D_MODEL), fix2),
                  pl.BlockSpec((1, D_MODEL), fix2), pl.BlockSpec((1, D_MODEL), fix2),
                  pl.BlockSpec((D_MODEL, 256), fix2),
                  pl.BlockSpec((nb, D_MODEL, tn), lambda m: (0, 0, 0))],
        out_specs=[pl.BlockSpec((tm, D_MODEL), row), pl.BlockSpec((tm, D_MODEL), row),
                   pl.BlockSpec((tm, D_MODEL), row), pl.BlockSpec((tm, 256), row),
                   pl.BlockSpec((tm, nb * tn), row)],
        out_shape=[SDS((t, D_MODEL), F32), SDS((t, D_MODEL), BF16), SDS((t, D_MODEL), BF16),
                   SDS((t, 256), BF16), SDS((t, nb * tn), BF16)],
        compiler_params=_cparams(),
    )(x, z, w_out, kv_gain, b_gain, kv_w, w_in_b)


def _layer_b_out_loss(h1, z, w_out, f_gain, target):
    t = h1.shape[0]
    tm = min(t, 512)

    def body(h1_ref, z_ref, wo_ref, fg_ref, tgt_ref,
             dh2_ref, dh2b_ref, dz_ref, loss_ref, dfn_ref):
        @pl.when(pl.program_id(0) == 0)
        def _():
            loss_ref[...] = jnp.zeros_like(loss_ref)
            dfn_ref[...] = jnp.zeros_like(dfn_ref)

        h2 = h1_ref[...] + _dot(z_ref[...], wo_ref[...])
        r = _rstd(h2)
        yn = h2 * r
        fg = fg_ref[...]
        err = yn * fg - tgt_ref[...]
        loss_ref[...] += (0.5 / D_MODEL) * jnp.sum(err * err)
        dy = err * (1.0 / D_MODEL)
        dfn_ref[...] += jnp.sum(dy * yn, axis=0, keepdims=True)
        u = dy * fg
        dh2 = r * u - h2 * ((r * r * r) * jnp.mean(u * h2, axis=-1, keepdims=True))
        dh2_ref[...] = dh2
        dh2b = dh2.astype(BF16)
        dh2b_ref[...] = dh2b
        dz_ref[...] = _dot_nt(dh2b, wo_ref[...]).astype(BF16)

    row = lambda m: (m, 0)
    fix2 = lambda m: (0, 0)
    return pl.pallas_call(
        body, name="layer_b_out_loss", grid=(t // tm,),
        in_specs=[pl.BlockSpec((tm, D_MODEL), row), pl.BlockSpec((tm, D_MODEL), row),
                  pl.BlockSpec((D_MODEL, D_MODEL), fix2), pl.BlockSpec((1, D_MODEL), fix2),
                  pl.BlockSpec((tm, D_MODEL), row)],
        out_specs=[pl.BlockSpec((tm, D_MODEL), row), pl.BlockSpec((tm, D_MODEL), row),
                   pl.BlockSpec((tm, D_MODEL), row), pl.BlockSpec((1, 128), fix2),
                   pl.BlockSpec((1, D_MODEL), fix2)],
        out_shape=[SDS((t, D_MODEL), F32), SDS((t, D_MODEL), BF16), SDS((t, D_MODEL), BF16),
                   SDS((1, 128), F32), SDS((1, D_MODEL), F32)],
        compiler_params=_cparams(),
    )(h1, z, w_out, f_gain, target)


def _layer_b_in_bwd(dqg, dkv, w_in_b, kv_w, h1, dh2, b_gain, kv_gain, w_out_a):
    t = h1.shape[0]
    tm = min(t, 512)
    nb, _, tn = w_in_b.shape
    per = D_MODEL // tn

    def body(dqg_ref, dkv_ref, wb_ref, kvw_ref, h1_ref, dh2_ref, bg_ref, kvg_ref, wo_ref,
             dh1_ref, dh1b_ref, dz_ref, dbn_ref, dkn_ref):
        @pl.when(pl.program_id(0) == 0)
        def _():
            dbn_ref[...] = jnp.zeros_like(dbn_ref)
            dkn_ref[...] = jnp.zeros_like(dkn_ref)

        dhb = jnp.zeros((tm, D_MODEL), F32)
        for i in range(nb):
            blk = dqg_ref[i // per, :, (i % per) * tn:(i % per + 1) * tn]
            dhb = dhb + _dot_nt(blk, wb_ref[i])
        dkn = (_dot_nt(dkv_ref[0].astype(BF16), kvw_ref[:, 0:128])
               + _dot_nt(dkv_ref[1].astype(BF16), kvw_ref[:, 128:256]))
        h1 = h1_ref[...]
        r = _rstd(h1)
        xr = h1 * r
        dbn_ref[...] += jnp.sum(dhb * xr, axis=0, keepdims=True)
        dkn_ref[...] += jnp.sum(dkn * xr, axis=0, keepdims=True)
        u = dhb * bg_ref[...] + dkn * kvg_ref[...]
        dh1 = dh2_ref[...] + r * u - h1 * ((r * r * r) * jnp.mean(u * h1, axis=-1, keepdims=True))
        dh1_ref[...] = dh1
        dh1b = dh1.astype(BF16)
        dh1b_ref[...] = dh1b
        dz_ref[...] = _dot_nt(dh1b, wo_ref[...]).astype(BF16)

    row = lambda m: (m, 0)
    fix2 = lambda m: (0, 0)
    return pl.pallas_call(
        body, name="layer_b_in_bwd", grid=(t // tm,),
        in_specs=[pl.BlockSpec((2, tm, D_MODEL), lambda m: (0, m, 0)),
                  pl.BlockSpec((2, tm, 128), lambda m: (0, m, 0)),
                  pl.BlockSpec((nb, D_MODEL, tn), lambda m: (0, 0, 0)),
                  pl.BlockSpec((D_MODEL, 256), fix2),
                  pl.BlockSpec((tm, D_MODEL), row), pl.BlockSpec((tm, D_MODEL), row),
                  pl.BlockSpec((1, D_MODEL), fix2), pl.BlockSpec((1, D_MODEL), fix2),
                  pl.BlockSpec((D_MODEL, D_MODEL), fix2)],
        out_specs=[pl.BlockSpec((tm, D_MODEL), row), pl.BlockSpec((tm, D_MODEL), row),
                   pl.BlockSpec((tm, D_MODEL), row), pl.BlockSpec((1, D_MODEL), fix2),
                   pl.BlockSpec((1, D_MODEL), fix2)],
        out_shape=[SDS((t, D_MODEL), F32), SDS((t, D_MODEL), BF16), SDS((t, D_MODEL), BF16),
                   SDS((1, D_MODEL), F32), SDS((1, D_MODEL), F32)],
        compiler_params=_cparams(),
    )(dqg, dkv, w_in_b, kv_w, h1, dh2, b_gain, kv_gain, w_out_a)


def _layer_a_in_bwd(dqg, dkv, w_in_a, x, dh1, a_gain, chip_sums):
    t = x.shape[0]
    tm = min(t, 512)
    nb, _, tn = w_in_a.shape
    per = D_MODEL // tn

    def body(dqg_ref, dkv_ref, w_ref, x_ref, dh1_ref, ag_ref, sums_ref, dx_ref, dan_ref, land_ref,
             send_sems, recv_sems):
        @pl.when(pl.program_id(0) == 0)
        def _():
            dan_ref[...] = jnp.zeros_like(dan_ref)
            for cp in _chip_copies(sums_ref, land_ref, send_sems, recv_sems):
                cp.start()

        dxn = jnp.zeros((tm, D_MODEL), F32)
        for i in range(nb):
            part = i // per
            src = dqg_ref if part in (0, 3) else dkv_ref
            outer = {0: 0, 3: 1, 1: 0, 2: 1}[part]
            blk = src[outer, :, (i % per) * tn:(i % per + 1) * tn]
            dxn = dxn + _dot_nt(blk, w_ref[i])
        xf = x_ref[...]
        r = _rstd(xf)
        dan_ref[...] += jnp.sum(dxn * (xf * r), axis=0, keepdims=True)
        u = dxn * ag_ref[...]
        dx_ref[...] = dh1_ref[...] + r * u - xf * ((r * r * r) * jnp.mean(u * xf, axis=-1, keepdims=True))

        @pl.when(pl.program_id(0) == t // tm - 1)
        def _():
            for cp in _chip_copies(sums_ref, land_ref, send_sems, recv_sems):
                cp.wait()

    row = lambda m: (m, 0)
    fix2 = lambda m: (0, 0)
    return pl.pallas_call(
        body, name="layer_a_in_bwd", grid=(t // tm,),
        in_specs=[pl.BlockSpec((2, tm, D_MODEL), lambda m: (0, m, 0)),
                  pl.BlockSpec((2, tm, D_MODEL), lambda m: (0, m, 0)),
                  pl.BlockSpec((nb, D_MODEL, tn), lambda m: (0, 0, 0)),
                  pl.BlockSpec((tm, D_MODEL), row), pl.BlockSpec((tm, D_MODEL), row),
                  pl.BlockSpec((1, D_MODEL), fix2), ANY],
        out_specs=[pl.BlockSpec((tm, D_MODEL), row), pl.BlockSpec((1, D_MODEL), fix2), ANY],
        out_shape=[SDS((t, D_MODEL), F32), SDS((1, D_MODEL), F32), SDS(chip_sums.shape, chip_sums.dtype)],
        scratch_shapes=[pltpu.SemaphoreType.DMA((3,)), pltpu.SemaphoreType.DMA((3,))],
        compiler_params=_cparams(),
    )(dqg, dkv, w_in_a, x, dh1, a_gain, chip_sums)


def _lut(s, vals):
    r = jnp.int32(vals[0])
    for i in range(1, len(vals)):
        r = jnp.where(s == i, jnp.int32(vals[i]), r)
    return r


def _held(steps, i):
    seq, cur = [None] * len(steps), None
    for k in range(len(steps) - 1, -1, -1):
        if steps[k][0] == i:
            cur = steps[k][1:3]
        seq[k] = cur
    for k in range(len(steps)):
        cur = seq[k] = seq[k] if seq[k] is not None else cur
    return seq


def _weight_grad_cols(name, my_slot, a, bs, steps, tn):
    t, dw = a.shape
    n_arr = len(bs)
    which = [s[0] for s in steps]
    blks = [s[3] for s in steps]

    def body(slot_ref, a_ref, *rest):
        b_refs, (o_ref, own_ref, at_ref) = rest[:n_arr], rest[n_arr:]
        s = pl.program_id(0)

        @pl.when(s == 0)
        def _():
            at_ref[...] = a_ref[...].T

        for i in range(n_arr):
            @pl.when(_lut(s, which) == i)
            def _(i=i):
                res = _dot(at_ref[...], b_refs[i][0])
                o_ref[0] = res.astype(BF16)

                @pl.when(_lut(s, blks) == slot_ref[0])
                def _():
                    own_ref[...] = res

    def b_spec(i):
        held = _held(steps, i)
        return pl.BlockSpec((1, t, tn), lambda s, slot: (_lut(s, [h[0] for h in held]), 0,
                                                         _lut(s, [h[1] for h in held])))

    return pl.pallas_call(
        body, name=name,
        grid_spec=pltpu.PrefetchScalarGridSpec(
            num_scalar_prefetch=1, grid=(len(steps),),
            in_specs=[pl.BlockSpec((t, dw), lambda s, slot: (0, 0))] + [b_spec(i) for i in range(n_arr)],
            out_specs=[pl.BlockSpec((1, dw, tn), lambda s, slot: (_lut(s, blks), 0, 0)),
                       pl.BlockSpec((dw, tn), lambda s, slot: (0, 0))],
            scratch_shapes=[pltpu.VMEM((dw, t), BF16)]),
        out_shape=[SDS((N_DEV, dw, tn), BF16), SDS((dw, tn), F32)],
        compiler_params=_cparams(),
    )(my_slot, a, *bs)


def _weight_grad_rows(name, my_slot, a, b):
    t, dw = a.shape
    n_o, _, c = b.shape
    rows = dw // N_DEV
    tn = min(c, 256)
    per = c // tn

    def body(slot_ref, a_ref, b_ref, o_ref, own_ref, at_ref, res_ref):
        @pl.when(pl.program_id(0) == 0)
        def _():
            at_ref[...] = a_ref[...].T

        res_ref[...] = _dot(at_ref[...], b_ref[0].astype(BF16))
        o_ref[...] = res_ref[...].astype(BF16)
        own_ref[...] = res_ref[pl.ds(pl.multiple_of(slot_ref[0] * rows, rows), rows), :]

    all_rows, own = pl.pallas_call(
        body, name=name,
        grid_spec=pltpu.PrefetchScalarGridSpec(
            num_scalar_prefetch=1, grid=(n_o * per,),
            in_specs=[pl.BlockSpec((t, dw), lambda s, slot: (0, 0)),
                      pl.BlockSpec((1, t, tn), lambda s, slot: (s // per, 0, s % per))],
            out_specs=[pl.BlockSpec((dw, tn), lambda s, slot: (0, s)),
                       pl.BlockSpec((rows, tn), lambda s, slot: (0, s))],
            scratch_shapes=[pltpu.VMEM((dw, t), BF16), pltpu.VMEM((dw, tn), F32)]),
        out_shape=[SDS((dw, n_o * c), BF16), SDS((rows, n_o * c), F32)],
        compiler_params=_cparams(),
    )(my_slot, a, b)
    return all_rows.reshape(N_DEV, rows, n_o * c), own


def _lane_lo():
    return lax.broadcasted_iota(jnp.int32, (1, 128), 1) < HEAD_DIM


def _collapse_chunks(ds, keys):
    if ds.shape[1] < keys:
        ds = jnp.concatenate([jnp.zeros((ds.shape[0], keys - ds.shape[1]), F32), ds], axis=1)
    gc = ds[0:CHUNK]
    for cc in range(1, ds.shape[0] // CHUNK):
        gc = gc + pltpu.roll(ds[cc * CHUNK:(cc + 1) * CHUNK], keys - cc * CHUNK, 1)
    return gc


def _offset_sums(gc):
    hi = gc.astype(BF16)
    lo = (gc - hi.astype(F32)).astype(BF16)
    flip = (lax.broadcasted_iota(jnp.int32, (CHUNK, CHUNK), 0)
            + lax.broadcasted_iota(jnp.int32, (CHUNK, CHUNK), 1) == CHUNK - 1).astype(BF16)
    gf = _dot(flip, hi) + _dot(flip, lo)
    skew = pltpu.roll(gf, 0, 1, stride=1, stride_axis=0)
    return jnp.sum(skew, axis=0, keepdims=True)


def _band_bias(w_row, band, rows):
    keys = w_row.shape[1]
    base = jnp.broadcast_to(w_row, (CHUNK, keys))
    skew = pltpu.roll(base, 0, 1, stride=1, stride_axis=0)
    skew = pltpu.roll(skew, keys - (CHUNK - 1), 1)
    col = lax.broadcasted_iota(jnp.int32, (CHUNK, keys), 1)
    chunk0 = jnp.where(col < band, skew, NEG)
    return jnp.concatenate(
        [chunk0] + [pltpu.roll(chunk0, cc * CHUNK, 1) for cc in range(1, rows // CHUNK)], axis=0)


def _silu_parts(g):
    sg = _sigmoid(g)
    return g * sg, sg * (1.0 + g * (1.0 - sg))


A_PAIRS = 2
A_LANES = 128 * A_PAIRS
A_STEPS = D_MODEL // A_LANES


def _a_specs():
    q = pl.BlockSpec((QBLK, A_LANES), lambda p, j: (j, p))
    ks = [pl.BlockSpec((QBLK, A_LANES), lambda p, j, b=b: (jnp.maximum(j - 2 + b, 0), A_STEPS + p)) for b in range(3)]
    vs = [pl.BlockSpec((QBLK, A_LANES), lambda p, j, b=b: (jnp.maximum(j - 2 + b, 0), 2 * A_STEPS + p))
          for b in range(3)]
    g = pl.BlockSpec((QBLK, A_LANES), lambda p, j: (j, 3 * A_STEPS + p))
    bias = pl.BlockSpec((A_PAIRS, 8, A_KEYS), lambda p, j: (p, 0, 0))
    return q, ks, vs, g, bias


def _a_fill_bias(w_ref, b_ref, j):
    _fill_bias(2 * A_PAIRS, lambda h: w_ref[h // 2, h % 2:h % 2 + 1, :], A_BAND, b_ref, j)


def _by_valid_key_blocks(j, fn):
    pl.when(j == 0)(functools.partial(fn, 1))
    pl.when(j == 1)(functools.partial(fn, 2))
    pl.when(j >= 2)(functools.partial(fn, 3))


def _fill_bias(n, get_row, band, bias_scr, j):
    @pl.when(j == 0)
    def _():
        for h in range(n):
            bias_scr[h] = _band_bias(get_row(h), band, bias_scr.shape[1])


def _row_sums_everywhere(r, sel):
    return jnp.where(sel, pltpu.roll(r, HEAD_DIM, 1), r)


def _own_everywhere(x, sel):
    return jnp.where(sel, x, pltpu.roll(x, HEAD_DIM, 1))


def _minus_rows(s, row_full):
    return jnp.concatenate([s[:, i:i + 128] - row_full for i in range(0, s.shape[1], 128)], axis=1)


def _attn_a_fwd(qkvg, bias, gather):
    t = qkvg.shape[0]
    nq = t // QBLK
    n_g = len(gather)
    q_spec, k_specs, v_specs, g_spec, bias_spec = _a_specs()

    def body(q_ref, k0, k1, k2, v0, v1, v2, g_ref, w_ref, *rest):
        shard_refs, rest = rest[:n_g], rest[n_g:]
        z_ref, o_ref, lse_ref = rest[:3]
        full_refs, (b_ref, *comm) = rest[3:3 + n_g], rest[3 + n_g:]
        p = pl.program_id(0)
        j = pl.program_id(1)
        start, forward, finish = _gather_phases(shard_refs, full_refs, *comm)
        pl.when(jnp.logical_and(p == 0, j == 0))(start)
        pl.when(jnp.logical_and(p == A_STEPS // 2, j == 0))(forward)
        _a_fill_bias(w_ref, b_ref, j)
        lane_lo = _lane_lo()
        sels = (lane_lo, jnp.logical_not(lane_lo))

        def attend(n_blocks):
            first_col = (3 - n_blocks) * QBLK
            for pp in range(A_PAIRS):
                cols = slice(128 * pp, 128 * (pp + 1))
                k = jnp.concatenate([r[:, cols] for r in (k0, k1, k2)[3 - n_blocks:]], axis=0)
                v = jnp.concatenate([r[:, cols] for r in (v0, v1, v2)[3 - n_blocks:]], axis=0)
                q = q_ref[:, cols]
                qm2 = jnp.concatenate([jnp.where(sel, q, jnp.zeros_like(q)) for sel in sels], axis=0) * SCALE
                s2 = _dot_nt(qm2, k)
                outs, lses = [], []
                for hh, sel in enumerate(sels):
                    s = s2[hh * QBLK:(hh + 1) * QBLK] + b_ref[2 * pp + hh, :, first_col:]
                    mx = jnp.max(s, axis=-1, keepdims=True)
                    e = jnp.exp(s - mx).astype(BF16)
                    r = _dot(e, jnp.where(sel, v, jnp.ones_like(v)))
                    l = _row_sums_everywhere(r, sel)
                    outs.append(r / l)
                    lses.append(mx + jnp.log(l))
                o = jnp.where(lane_lo, outs[0], outs[1])
                silu, _ = _silu_parts(g_ref[:, cols].astype(F32))
                o_ref[:, cols] = o.astype(BF16)
                z_ref[:, cols] = (o * silu).astype(BF16)
                lse_ref[:, cols] = jnp.where(lane_lo, lses[0], lses[1])

        _by_valid_key_blocks(j, attend)
        pl.when(jnp.logical_and(p == A_STEPS - 1, j == nq - 1))(finish)

    out_spec = pl.BlockSpec((QBLK, A_LANES), lambda p, j: (j, p))
    outs = pl.pallas_call(
        body, name="attn_a_fwd", grid=(A_STEPS, nq),
        in_specs=[q_spec, *k_specs, *v_specs, g_spec, bias_spec] + [ANY] * n_g,
        out_specs=[out_spec, out_spec, out_spec] + [ANY] * n_g,
        out_shape=[SDS((t, D_MODEL), BF16), SDS((t, D_MODEL), BF16), SDS((t, D_MODEL), F32)]
        + [SDS((N_DEV, *s.shape), s.dtype) for s in gather],
        scratch_shapes=[pltpu.VMEM((2 * A_PAIRS, QBLK, A_KEYS), F32)] + _gather_scratch(n_g),
        compiler_params=_cparams(),
    )(qkvg, qkvg, qkvg, qkvg, qkvg, qkvg, qkvg, qkvg, bias, *gather)
    return outs[0], outs[1], outs[2], list(outs[3:])


def _attn_a_bwd(qkvg, bias, out_a, lse, dz, scatter):
    t = qkvg.shape[0]
    nq = t // QBLK
    n_sc = len(scatter)
    q_spec, k_specs, v_specs, g_spec, bias_spec = _a_specs()

    def body(q_ref, k0, k1, k2, v0, v1, v2, g_ref, w_ref, o_ref, lse_ref, dz_ref, *rest):
        sc_refs, rest = rest[:n_sc], rest[n_sc:]
        dqg_ref, dkv_ref, dg_ref = rest[:3]
        land_refs, rest = rest[3:3 + n_sc], rest[3 + n_sc:]
        dk_acc, dv_acc, gt_acc, b_ref, send_sems, recv_sems = rest
        j = pl.program_id(1)
        first = jnp.logical_and(pl.program_id(0) == 0, j == 0)
        last = jnp.logical_and(pl.program_id(0) == A_STEPS - 1, j == nq - 1)

        @pl.when(first)
        def _():
            for cp in _scatter_copies(sc_refs, land_refs, send_sems, recv_sems):
                cp.start()

        _a_fill_bias(w_ref, b_ref, j)

        @pl.when(j == 0)
        def _():
            dk_acc[...] = jnp.zeros_like(dk_acc)
            dv_acc[...] = jnp.zeros_like(dv_acc)
            gt_acc[...] = jnp.zeros_like(gt_acc)

        lane_lo = _lane_lo()
        sels = (lane_lo, jnp.logical_not(lane_lo))

        def attend(n_blocks):
            first_col = (3 - n_blocks) * QBLK
            for pp in range(A_PAIRS):
                cols = slice(128 * pp, 128 * (pp + 1))
                q = q_ref[:, cols]
                k = jnp.concatenate([r[:, cols] for r in (k0, k1, k2)[3 - n_blocks:]], axis=0)
                v = jnp.concatenate([r[:, cols] for r in (v0, v1, v2)[3 - n_blocks:]], axis=0)
                o = o_ref[:, cols].astype(F32)
                lse_pair = lse_ref[:, cols]
                dzf = dz_ref[:, cols].astype(F32)
                silu, dsilu = _silu_parts(g_ref[:, cols].astype(F32))
                do = dzf * silu
                dqg_ref[1, :, cols] = (dzf * o * dsilu).astype(BF16)
                doo = do * o
                qm2 = jnp.concatenate([jnp.where(sel, q, jnp.zeros_like(q)) for sel in sels], axis=0) * SCALE
                dom2 = jnp.concatenate([jnp.where(sel, do, 0.0) for sel in sels], axis=0).astype(BF16)
                s2 = _dot_nt(qm2, k)
                dp2 = _dot_nt(dom2, v)
                ps, dss = [], []
                for hh, sel in enumerate(sels):
                    rows = slice(hh * QBLK, (hh + 1) * QBLK)
                    s = s2[rows] + b_ref[2 * pp + hh, :, first_col:]
                    p = jnp.exp(_minus_rows(s, _own_everywhere(lse_pair, sel)))
                    delta = jnp.sum(jnp.where(sel, doo, 0.0), axis=-1, keepdims=True)
                    ds = p * (dp2[rows] - delta)
                    gt_acc[2 * pp + hh] += _collapse_chunks(ds, A_KEYS)
                    ps.append(p.astype(BF16))
                    dss.append(ds.astype(BF16))
                dsb2 = jnp.concatenate(dss, axis=0)
                dq2 = _dot(dsb2, k) * SCALE
                dk_blk = _dot_tn(dsb2, qm2)
                dv_blk = _dot_tn(jnp.concatenate(ps, axis=0), dom2)
                dqg_ref[0, :, cols] = jnp.where(lane_lo, dq2[0:QBLK], dq2[QBLK:2 * QBLK]).astype(BF16)
                for b in range(n_blocks):
                    rows = pl.ds(pl.multiple_of((j - n_blocks + 1 + b) * QBLK, QBLK), QBLK)
                    dk_acc[rows, cols] += dk_blk[b * QBLK:(b + 1) * QBLK]
                    dv_acc[rows, cols] += dv_blk[b * QBLK:(b + 1) * QBLK]

        _by_valid_key_blocks(j, attend)

        @pl.when(j == nq - 1)
        def _():
            dkv_ref[0] = dk_acc[...].astype(BF16)
            dkv_ref[1] = dv_acc[...].astype(BF16)
            for pp in range(A_PAIRS):
                dg_ref[pp] = jnp.concatenate([_offset_sums(gt_acc[2 * pp]), _offset_sums(gt_acc[2 * pp + 1]),
                                              jnp.zeros((6, A_DIAG), F32)], axis=0)

        @pl.when(last)
        def _():
            for cp in _scatter_copies(sc_refs, land_refs, send_sems, recv_sems):
                cp.wait()

    blk = pl.BlockSpec((QBLK, A_LANES), lambda p, j: (j, p))
    outs = pl.pallas_call(
        body, name="attn_a_bwd", grid=(A_STEPS, nq),
        in_specs=[q_spec, *k_specs, *v_specs, g_spec, bias_spec, blk, blk, blk] + [ANY] * n_sc,
        out_specs=[pl.BlockSpec((2, QBLK, A_LANES), lambda p, j: (0, j, p)),
                   pl.BlockSpec((2, t, A_LANES), lambda p, j: (0, 0, p)),
                   pl.BlockSpec((A_PAIRS, 8, A_DIAG), lambda p, j: (p, 0, 0))] + [ANY] * n_sc,
        out_shape=[SDS((2, t, D_MODEL), BF16), SDS((2, t, D_MODEL), BF16), SDS((N_HEADS // 2, 8, A_DIAG), F32)]
        + [SDS((N_DEV - 1, *g.shape[1:]), g.dtype) for g in scatter],
        scratch_shapes=[pltpu.VMEM((t, A_LANES), F32), pltpu.VMEM((t, A_LANES), F32),
                        pltpu.VMEM((2 * A_PAIRS, CHUNK, A_KEYS), F32), pltpu.VMEM((2 * A_PAIRS, QBLK, A_KEYS), F32),
                        pltpu.SemaphoreType.DMA(((N_DEV - 1) * n_sc,)),
                        pltpu.SemaphoreType.DMA(((N_DEV - 1) * n_sc,))],
        compiler_params=_cparams(),
    )(qkvg, qkvg, qkvg, qkvg, qkvg, qkvg, qkvg, qkvg, bias, out_a, lse, dz, *scatter)
    return outs[0], outs[1], outs[2], list(outs[3:])


def _b_specs(qblk):
    per = qblk // B_PREV
    q = pl.BlockSpec((qblk, 512), lambda h, j: (j, h))
    g = pl.BlockSpec((qblk, 512), lambda h, j: (j, 2 + h))
    kp = pl.BlockSpec((B_PREV, 128), lambda h, j: (jnp.maximum(per * j - 1, 0), 0))
    kc = pl.BlockSpec((qblk, 128), lambda h, j: (j, 0))
    vp = pl.BlockSpec((B_PREV, 128), lambda h, j: (jnp.maximum(per * j - 1, 0), 1))
    vc = pl.BlockSpec((qblk, 128), lambda h, j: (j, 1))
    bias = pl.BlockSpec((B_GROUP, qblk + B_PREV), lambda h, j: (h, 0))
    sinks = pl.BlockSpec(memory_space=pltpu.SMEM)
    return q, g, kp, kc, vp, vc, bias, sinks


def _b_operands(kp, kc, vp, vc, kvh, with_prev):
    k = jnp.concatenate([kp[...], kc[...]], axis=0) if with_prev else kc[...]
    v = jnp.concatenate([vp[...], vc[...]], axis=0) if with_prev else vc[...]
    kr = pltpu.roll(k, HEAD_DIM, 1)
    vr = pltpu.roll(v, HEAD_DIM, 1)
    first = kvh == 0
    return (jnp.where(first, k, kr), jnp.where(first, kr, k),
            jnp.where(first, v, vr), jnp.where(first, vr, v))


def _attn_b_fwd(qg, kv, bias, sinks):
    t = qg.shape[0]
    qblk = B_QBLK_FWD
    q_spec, g_spec, kp_spec, kc_spec, vp_spec, vc_spec, bias_spec, sink_spec = _b_specs(qblk)

    def body(q_ref, g_ref, kp, kc, vp, vc, w_ref, sink_ref, z_ref, o_ref, lse_ref, b_ref):
        kvh = pl.program_id(0)
        j = pl.program_id(1)
        _fill_bias(B_GROUP, lambda h: w_ref[h:h + 1, :], B_BAND, b_ref, j)
        lane_lo = _lane_lo()
        n_pairs = B_GROUP // 2

        def attend(with_prev):
            first_col = 0 if with_prev else B_PREV
            k_lo, k_hi, v_lo, v_hi = _b_operands(kp, kc, vp, vc, kvh, with_prev)
            halves = []
            for hh, sel in enumerate((lane_lo, jnp.logical_not(lane_lo))):
                kk = k_lo if hh == 0 else k_hi
                vv = v_lo if hh == 0 else v_hi
                qm4 = jnp.concatenate(
                    [jnp.where(sel, q_ref[:, 128 * pp:128 * (pp + 1)], jnp.zeros((qblk, 128), BF16))
                     for pp in range(n_pairs)], axis=0) * SCALE
                s4 = _dot_nt(qm4, kk)
                es, mxs = [], []
                for pp in range(n_pairs):
                    g = 2 * pp + hh
                    s = s4[pp * qblk:(pp + 1) * qblk] + b_ref[g, :, first_col:]
                    mxs.append(jnp.maximum(jnp.max(s, axis=-1, keepdims=True), sink_ref[kvh * B_GROUP + g]))
                    es.append(jnp.exp(s - mxs[pp]).astype(BF16))
                r4 = _dot(jnp.concatenate(es, axis=0), jnp.where(sel, vv, jnp.ones_like(vv)))
                outs, lses = [], []
                for pp in range(n_pairs):
                    r = r4[pp * qblk:(pp + 1) * qblk]
                    l = _row_sums_everywhere(r, sel) + jnp.exp(sink_ref[kvh * B_GROUP + 2 * pp + hh] - mxs[pp])
                    outs.append(r / l)
                    lses.append(mxs[pp] + jnp.log(l))
                halves.append((outs, lses))
            for pp in range(n_pairs):
                cols = slice(128 * pp, 128 * (pp + 1))
                o = jnp.where(lane_lo, halves[0][0][pp], halves[1][0][pp])
                silu, _ = _silu_parts(g_ref[:, cols].astype(F32))
                o_ref[:, cols] = o.astype(BF16)
                z_ref[:, cols] = (o * silu).astype(BF16)
                lse_ref[:, cols] = jnp.where(lane_lo, halves[0][1][pp], halves[1][1][pp])

        pl.when(j == 0)(functools.partial(attend, False))
        pl.when(j >= 1)(functools.partial(attend, True))

    out_spec = pl.BlockSpec((qblk, 512), lambda h, j: (j, h))
    return pl.pallas_call(
        body, name="attn_b_fwd", grid=(B_KV_HEADS, t // qblk),
        in_specs=[q_spec, g_spec, kp_spec, kc_spec, vp_spec, vc_spec, bias_spec, sink_spec],
        out_specs=[out_spec, out_spec, out_spec],
        out_shape=[SDS((t, D_MODEL), BF16), SDS((t, D_MODEL), BF16), SDS((t, D_MODEL), F32)],
        scratch_shapes=[pltpu.VMEM((B_GROUP, qblk, qblk + B_PREV), F32)],
        compiler_params=_cparams(),
    )(qg, qg, kv, kv, kv, kv, bias, sinks)


def _attn_b_bwd(qg, kv, bias, sinks, out_b, lse, dz, bucket_onehot):
    t = qg.shape[0]
    qblk = B_QBLK_BWD
    keys = qblk + B_PREV
    nq = t // qblk
    q_spec, g_spec, kp_spec, kc_spec, vp_spec, vc_spec, bias_spec, sink_spec = _b_specs(qblk)

    def body(q_ref, g_ref, kp, kc, vp, vc, w_ref, sink_ref, o_ref, lse_ref, dz_ref, oh_ref,
             dqg_ref, dkv_ref, dt5_ref, dsink_ref, gt_acc, b_ref):
        kvh = pl.program_id(0)
        j = pl.program_id(1)
        _fill_bias(B_GROUP, lambda h: w_ref[h:h + 1, :], B_BAND, b_ref, j)

        @pl.when(jnp.logical_and(kvh == 0, j == 0))
        def _():
            dkv_ref[...] = jnp.zeros_like(dkv_ref)

        @pl.when(j == 0)
        def _():
            gt_acc[...] = jnp.zeros_like(gt_acc)
            dsink_ref[...] = jnp.zeros_like(dsink_ref)

        lane_lo = _lane_lo()

        def attend(with_prev):
            first_col = 0 if with_prev else B_PREV
            k_lo, k_hi, v_lo, v_hi = _b_operands(kp, kc, vp, vc, kvh, with_prev)
            dk_blk = jnp.zeros((keys - first_col, 128), F32)
            dv_blk = jnp.zeros((keys - first_col, 128), F32)
            for pp in range(B_GROUP // 2):
                cols = slice(128 * pp, 128 * (pp + 1))
                qp = q_ref[:, cols]
                o = o_ref[:, cols].astype(F32)
                lse_pair = lse_ref[:, cols]
                dzf = dz_ref[:, cols].astype(F32)
                silu, dsilu = _silu_parts(g_ref[:, cols].astype(F32))
                do = dzf * silu
                dqg_ref[1, :, cols] = (dzf * o * dsilu).astype(BF16)
                doo = do * o
                dqs = []
                for hh in range(2):
                    g = 2 * pp + hh
                    sel = lane_lo if hh == 0 else jnp.logical_not(lane_lo)
                    sink = sink_ref[kvh * B_GROUP + g]
                    kk = k_lo if hh == 0 else k_hi
                    vv = v_lo if hh == 0 else v_hi
                    qm = jnp.where(sel, qp, jnp.zeros_like(qp)) * SCALE
                    s = _dot_nt(qm, kk) + b_ref[g, :, first_col:]
                    lse_h = _own_everywhere(lse_pair, sel)
                    p = jnp.exp(_minus_rows(s, lse_h))
                    delta = jnp.sum(jnp.where(sel, doo, 0.0), axis=-1, keepdims=True)
                    dom = jnp.where(sel, do, 0.0).astype(BF16)
                    dp = _dot_nt(dom, vv)
                    ds = p * (dp - delta)
                    gt_acc[g] += _collapse_chunks(ds, keys)
                    dsink_ref[g:g + 1, :] -= jnp.sum(jnp.exp(sink - lse_h) * delta, axis=0, keepdims=True)
                    dsb = ds.astype(BF16)
                    dqs.append(_dot(dsb, kk) * SCALE)
                    dk_blk = dk_blk + _dot_tn(dsb, qm)
                    dv_blk = dv_blk + _dot_tn(p.astype(BF16), dom)
                dqg_ref[0, :, cols] = jnp.where(lane_lo, dqs[0], dqs[1]).astype(BF16)
            mine = lane_lo == (kvh == 0)
            dk_add = jnp.where(mine, dk_blk + pltpu.roll(dk_blk, HEAD_DIM, 1), 0.0)
            dv_add = jnp.where(mine, dv_blk + pltpu.roll(dv_blk, HEAD_DIM, 1), 0.0)
            first_key = B_PREV if with_prev else 0
            if with_prev:
                rows = pl.ds(pl.multiple_of(j * qblk - B_PREV, B_PREV), B_PREV)
                dkv_ref[0, rows, :] += dk_add[0:B_PREV]
                dkv_ref[1, rows, :] += dv_add[0:B_PREV]
            rows = pl.ds(pl.multiple_of(j * qblk, qblk), qblk)
            dkv_ref[0, rows, :] += dk_add[first_key:first_key + qblk]
            dkv_ref[1, rows, :] += dv_add[first_key:first_key + qblk]

        pl.when(j == 0)(functools.partial(attend, False))
        pl.when(j >= 1)(functools.partial(attend, True))

        @pl.when(j == nq - 1)
        def _():
            dd = jnp.concatenate([_offset_sums(gt_acc[g]) for g in range(B_GROUP)], axis=0)
            hi = dd.astype(BF16)
            lo = (dd - hi.astype(F32)).astype(BF16)
            dt5_ref[...] = _dot(hi, oh_ref[...]) + _dot(lo, oh_ref[...])

    blk = pl.BlockSpec((qblk, 512), lambda h, j: (j, h))
    return pl.pallas_call(
        body, name="attn_b_bwd", grid=(B_KV_HEADS, nq),
        in_specs=[q_spec, g_spec, kp_spec, kc_spec, vp_spec, vc_spec, bias_spec, sink_spec, blk, blk, blk,
                  pl.BlockSpec((keys, 128), lambda h, j: (0, 0))],
        out_specs=[pl.BlockSpec((2, qblk, 512), lambda h, j: (0, j, h)),
                   pl.BlockSpec((2, t, 128), lambda h, j: (0, 0, 0)),
                   pl.BlockSpec((B_GROUP, 128), lambda h, j: (h, 0)),
                   pl.BlockSpec((B_GROUP, 128), lambda h, j: (h, 0))],
        out_shape=[SDS((2, t, D_MODEL), BF16), SDS((2, t, 128), F32),
                   SDS((N_HEADS, 128), F32), SDS((N_HEADS, 128), F32)],
        scratch_shapes=[pltpu.VMEM((B_GROUP, CHUNK, keys), F32), pltpu.VMEM((B_GROUP, qblk, keys), F32)],
        compiler_params=_cparams(),
    )(qg, qg, kv, kv, kv, kv, bias, sinks, out_b, lse, dz, bucket_onehot)


def _a_bias_by_offset(rel_bias):
    m = np.arange(A_DIAG)
    idx = np.clip(A_BAND - 1 - m, -A_REL_CLIP, A_REL_CLIP) + A_REL_CLIP
    by_head = rel_bias[idx].T.reshape(N_HEADS // 2, 2, A_DIAG)
    return jnp.concatenate([by_head, jnp.zeros((N_HEADS // 2, 6, A_DIAG), F32)], axis=1)


def _a_bias_grad(offset_sums):
    first = 319
    tail = jnp.sum(offset_sums[:, :first], axis=1)
    body = jnp.flip(offset_sums[:, first:first + 320], axis=1)
    body = body.at[:, -1].add(tail)
    full = jnp.concatenate([jnp.zeros((N_HEADS, 193), F32), body], axis=1)
    return full


def _t5_bucket(rel):
    nb = T5_BUCKETS // 2
    max_exact = nb // 2
    ret = jnp.where(rel > 0, nb, 0)
    n = jnp.abs(rel)
    nf = jnp.maximum(n, 1).astype(jnp.float32)
    large = max_exact + (jnp.log(nf / max_exact) / math.log(T5_MAX_DIST / max_exact)
                         * (nb - max_exact)).astype(jnp.int32)
    large = jnp.minimum(large, nb - 1)
    return ret + jnp.where(n < max_exact, n, large)


def _b_offset_buckets(keys):
    return _t5_bucket(jnp.arange(keys, dtype=jnp.int32) - (B_LEFT_CHUNKS * CHUNK + CHUNK - 1))


def _b_bias_by_offset(t5_table, keys):
    return t5_table[_b_offset_buckets(keys)].T


def _b_bucket_onehot(keys):
    return (_b_offset_buckets(keys)[:, None] == jnp.arange(128)[None, :]).astype(BF16)


def _local_step(my_slot, order, x, target, a_gain_shard, w_in_a_shard, rel_bias, late_shards, kv_gain,
                t5_table, b_gain, sinks, f_gain):
    a_bias = _a_bias_by_offset(rel_bias)
    b_bias_fwd = _b_bias_by_offset(t5_table, B_QBLK_FWD + B_PREV)
    b_bias_bwd = _b_bias_by_offset(t5_table, B_QBLK_BWD + B_PREV)
    sinks_flat = sinks.reshape(N_HEADS)

    xn, qkvg, w_in_a, a_gain = _norm_matmul_gather(order, x, a_gain_shard, w_in_a_shard)
    z_a, out_a, lse_a, (w_in_b, w_out_a, w_out_b, kv_w) = _attn_a_fwd(qkvg, a_bias, late_shards)
    w_out_a = w_out_a.reshape(D_MODEL, D_MODEL)
    w_out_b = w_out_b.reshape(D_MODEL, D_MODEL)
    kv_w = kv_w.reshape(D_MODEL, 2 * 128)
    h1, kvn, hb, kv, qg = _layer_a_out(x, z_a, w_out_a, kv_gain, b_gain, kv_w, w_in_b)
    z_b, out_b, lse_b = _attn_b_fwd(qg, kv, b_bias_fwd, sinks_flat)
    dh2, dh2b, dz_b, loss, d_fn = _layer_b_out_loss(h1, z_b, w_out_b, f_gain, target)

    dqg_b, dkv_b, d_t5, d_sink = _attn_b_bwd(qg, kv, b_bias_bwd, sinks_flat, out_b, lse_b, dz_b,
                                             _b_bucket_onehot(B_QBLK_BWD + B_PREV))
    dh1, dh1b, dz_a, d_bn, d_kn = _layer_b_in_bwd(dqg_b, dkv_b, w_in_b, kv_w, h1, dh2, b_gain, kv_gain, w_out_a)
    early = dict(
        b_w_out=_weight_grad_rows("grad_b_w_out", my_slot, z_b, dh2b[None]),
        b_w_in=_weight_grad_cols("grad_b_w_in", my_slot, hb, [dqg_b],
                                 [(0, o, c, 4 * o + c) for o in range(2) for c in range(4)], 256),
        kv_w=_weight_grad_rows("grad_kv_w", my_slot, kvn, dkv_b),
        a_w_out=_weight_grad_rows("grad_a_w_out", my_slot, z_a, dh1b[None]))
    dqg_a, dkv_a, d_rel, landed = _attn_a_bwd(qkvg, a_bias, out_a, lse_a, dz_a, [g[0] for g in early.values()])
    g_w_in_a = _weight_grad_cols(
        "grad_a_w_in", my_slot, xn, [dqg_a, dkv_a],
        [(0, 0, 0, 0), (0, 0, 1, 1), (1, 0, 0, 2), (1, 0, 1, 3), (1, 1, 0, 4), (1, 1, 1, 5), (0, 1, 0, 6), (0, 1, 1, 7)], 512)
    from_sibling, = _exchange_sibling([g_w_in_a[0]])
    x_i, y_i, c_i, chips = _place()
    del x_i, y_i
    forward_slots = jnp.stack([_slot(*chip, c_i) for chip in chips]).astype(jnp.int32)
    chip_sums = _pre_reduce("chip_sum_a_w_in", g_w_in_a[0], from_sibling, forward_slots)
    grad_x, d_an, from_chips = _layer_a_in_bwd(dqg_a, dkv_a, w_in_a, x, dh1, a_gain, chip_sums)

    matrices = {n: (g[1], [(land, 0, N_DEV - 1)]) for (n, g), land in zip(early.items(), landed)}
    matrices["a_w_in"] = (g_w_in_a[1], [(from_sibling, 3, 1), (from_chips, 0, 3)])
    small = dict(
        loss=loss, a_norm=d_an, a_rel_bias=d_rel[:, :2].reshape(N_HEADS, A_DIAG),
        kv_norm=d_kn, t5_bias=d_t5, b_norm=d_bn, b_sinks=d_sink, final_norm=d_fn)
    return grad_x, small, matrices


def _place():
    x, y, c = lax.axis_index("x"), lax.axis_index("y"), lax.axis_index("c")
    chips = [(1 - x, y), (x, 1 - y), (1 - x, 1 - y)]
    return x, y, c, chips


def _slot(px, py, pc):
    return 4 * px + 2 * py + pc


ANY = pl.BlockSpec(memory_space=pl.ANY)


def _peer(x, y, c, k):
    return (x ^ (k >> 2), y ^ ((k >> 1) & 1), c ^ (k & 1))


def _scatter_copies(grad_refs, land_refs, send_sems, recv_sems):
    x, y, c, _ = _place()
    copies = []
    for t, (grad, land) in enumerate(zip(grad_refs, land_refs)):
        for k in range(1, N_DEV):
            peer = _peer(x, y, c, k)
            sem = (N_DEV - 1) * t + k - 1
            copies.append(pltpu.make_async_remote_copy(
                src_ref=grad.at[_slot(*peer)], dst_ref=land.at[k - 1],
                send_sem=send_sems.at[sem], recv_sem=recv_sems.at[sem],
                device_id=peer, device_id_type=MESH))
    return copies


def _gather_phases(ins, outs, send_sems, recv_sems, local_sems):
    n = len(ins)
    x, y, c, chips = _place()
    me, sibling = (x, y, c), (x, y, 1 - c)

    def copy(t, k, block, to, src=None):
        dst = outs[t].at[_slot(*block)]
        return pltpu.make_async_remote_copy(
            src_ref=dst if src is None else src, dst_ref=dst,
            send_sem=send_sems.at[7 * t + k], recv_sem=recv_sems.at[7 * t + k],
            device_id=to, device_id_type=MESH)

    def lists():
        mine = [pltpu.make_async_copy(ins[t], outs[t].at[_slot(*me)], local_sems.at[t]) for t in range(n)]
        first = []
        for t in range(n):
            first.append(copy(t, 0, me, sibling, src=ins[t]))
            first += [copy(t, 1 + j, me, (*chip, c), src=ins[t]) for j, chip in enumerate(chips)]
        passed = [copy(t, 4 + j, (*chip, c), sibling) for t in range(n) for j, chip in enumerate(chips)]
        return mine, first, passed

    def start():
        mine, first, _ = lists()
        for cp in mine + first:
            cp.start()

    def forward():
        _, _, passed = lists()
        for t in range(n):
            for j, chip in enumerate(chips):
                copy(t, 1 + j, (*chip, c), me).wait_recv()
                passed[3 * t + j].start()

    def finish():
        mine, first, passed = lists()
        for t in range(n):
            copy(t, 0, sibling, me).wait_recv()
            for j, chip in enumerate(chips):
                copy(t, 4 + j, (*chip, 1 - c), me).wait_recv()
        for cp in first + passed:
            cp.wait_send()
        for cp in mine:
            cp.wait()

    return start, forward, finish


def _gather_scratch(n):
    return [pltpu.SemaphoreType.DMA((7 * n,)), pltpu.SemaphoreType.DMA((7 * n,)), pltpu.SemaphoreType.DMA((n,))]


def _exchange_sibling(grads):
    n = len(grads)

    def body(*refs):
        ins, outs = refs[:n], refs[n:2 * n]
        send_sems, recv_sems = refs[2 * n:]
        x, y, c, chips = _place()
        sibling = (x, y, 1 - c)
        copies = []
        for t in range(n):
            blocks = [(*chip, 1 - c) for chip in chips] + [sibling]
            for k, block in enumerate(blocks):
                copies.append(pltpu.make_async_remote_copy(
                    src_ref=ins[t].at[_slot(*block)], dst_ref=outs[t].at[k],
                    send_sem=send_sems.at[4 * t + k], recv_sem=recv_sems.at[4 * t + k],
                    device_id=sibling, device_id_type=MESH))
        for cp in copies:
            cp.start()
        for cp in copies:
            cp.wait()

    return pl.pallas_call(
        body, name="grads_to_sibling",
        in_specs=[ANY] * n, out_specs=[ANY] * n,
        out_shape=[SDS((4, *g.shape[1:]), g.dtype) for g in grads],
        scratch_shapes=[pltpu.SemaphoreType.DMA((4 * n,)), pltpu.SemaphoreType.DMA((4 * n,))],
    )(*grads)


def _chip_copies(sums_ref, land_ref, send_sems, recv_sems):
    x, y, c, chips = _place()
    del x, y
    return [pltpu.make_async_remote_copy(
        src_ref=sums_ref.at[j], dst_ref=land_ref.at[j], send_sem=send_sems.at[j], recv_sem=recv_sems.at[j],
        device_id=(*chip, c), device_id_type=MESH) for j, chip in enumerate(chips)]


def _row_tile(rows):
    return min(rows, 256)


def _pre_reduce(name, g, from_sibling, slots):
    _, r, c = g.shape
    tr = _row_tile(r)

    def body(slots_ref, g_ref, s_ref, o_ref):
        del slots_ref
        o_ref[...] = (g_ref[...].astype(F32) + s_ref[...].astype(F32)).astype(BF16)

    return pl.pallas_call(
        body, name=name,
        grid_spec=pltpu.PrefetchScalarGridSpec(
            num_scalar_prefetch=1, grid=(3, r // tr),
            in_specs=[pl.BlockSpec((1, tr, c), lambda j, i, s: (s[j], i, 0)),
                      pl.BlockSpec((1, tr, c), lambda j, i, s: (j, i, 0))],
            out_specs=pl.BlockSpec((1, tr, c), lambda j, i, s: (j, i, 0))),
        out_shape=SDS((3, r, c), BF16),
        compiler_params=_cparams(),
    )(slots, g, from_sibling)


def _adamw(w, g, m, v):
    m2 = ADAM_B1 * m + (1.0 - ADAM_B1) * g
    v2 = ADAM_B2 * v + (1.0 - ADAM_B2) * jnp.square(g)
    m_hat = m2 / (1.0 - ADAM_B1 ** ADAM_STEP)
    v_hat = v2 / (1.0 - ADAM_B2 ** ADAM_STEP)
    delta = -ADAM_LR * (m_hat / (jnp.sqrt(v_hat) + ADAM_EPS) + ADAM_WD * w)
    return delta, m2, v2


def _reduce_adamw(name, own, partials, w, m, v):
    r, c = own.shape
    tr = _row_tile(r)
    n_p = len(partials)

    def body(own_ref, *rest):
        p_refs, (w_ref, m_ref, v_ref, grad_ref, d_ref, nm_ref, nv_ref) = rest[:n_p], rest[n_p:]
        grad = own_ref[...]
        for p_ref, (_, _, count) in zip(p_refs, partials):
            for j in range(count):
                grad = grad + p_ref[j].astype(F32)
        grad_ref[...] = grad
        d_ref[...], nm_ref[...], nv_ref[...] = _adamw(w_ref[...], grad, m_ref[...], v_ref[...])

    flat = pl.BlockSpec((tr, c), lambda i: (i, 0))
    return pl.pallas_call(
        body, name=name, grid=(r // tr,),
        in_specs=[flat] + [pl.BlockSpec((count, tr, c), lambda i, first=first, count=count: (first // count, i, 0))
                           for _, first, count in partials] + [flat, flat, flat],
        out_specs=[flat, flat, flat, flat],
        out_shape=[SDS((r, c), F32)] * 4,
        compiler_params=_cparams(),
    )(own, *[p[0] for p in partials], w, m, v)


VM = pl.BlockSpec()


def _small_allreduce(parts):
    n = len(parts)

    def body(*refs):
        ins, outs, lands = refs[:n], refs[n:2 * n], refs[2 * n:3 * n]
        send_sems, recv_sems = refs[3 * n:]
        x, y, c, _ = _place()
        my_slot = _slot(x, y, c)
        copies = []
        for t in range(n):
            lands[t][my_slot] = ins[t][...]
            for k in range(1, N_DEV):
                sem = (N_DEV - 1) * t + k - 1
                copies.append(pltpu.make_async_remote_copy(
                    src_ref=ins[t], dst_ref=lands[t].at[my_slot],
                    send_sem=send_sems.at[sem], recv_sem=recv_sems.at[sem],
                    device_id=_peer(x, y, c, k), device_id_type=MESH))
        for cp in copies:
            cp.start()
        for t in range(n):
            for k in range(1, N_DEV):
                sem = (N_DEV - 1) * t + k - 1
                pltpu.make_async_remote_copy(
                    src_ref=ins[t], dst_ref=lands[t].at[_slot(*_peer(x, y, c, k))],
                    send_sem=send_sems.at[sem], recv_sem=recv_sems.at[sem],
                    device_id=(x, y, c), device_id_type=MESH).wait_recv()
        for cp in copies:
            cp.wait_send()
        for t in range(n):
            total = lands[t][0]
            for s in range(1, N_DEV):
                total = total + lands[t][s]
            outs[t][...] = total

    n_sems = (N_DEV - 1) * n
    return pl.pallas_call(
        body, name="small_allreduce",
        in_specs=[VM] * n, out_specs=[VM] * n, out_shape=[SDS(p.shape, F32) for p in parts],
        scratch_shapes=[pltpu.VMEM((N_DEV, *p.shape), F32) for p in parts]
        + [pltpu.SemaphoreType.DMA((n_sems,)), pltpu.SemaphoreType.DMA((n_sems,))],
    )(*parts)


def _small_adamw(my_slot, sums, ws, ms, vs):
    n = len(ws)

    def body(slot_ref, *refs):
        sum_refs, refs = refs[:n + 1], refs[n + 1:]
        w_refs, m_refs, v_refs, refs = refs[:n], refs[n:2 * n], refs[2 * n:3 * n], refs[3 * n:]
        g_refs, d_refs, nm_refs, nv_refs = refs[:n + 1], refs[n + 1:2 * n + 1], refs[2 * n + 1:3 * n + 1], refs[3 * n + 1:]
        for t in range(n + 1):
            if t == 0:
                g = sum_refs[0][:, pl.ds(pl.multiple_of(slot_ref[0] * 128, 128), 128)]
            else:
                g = sum_refs[t][...]
            g_refs[t][...] = g
            if t < n:
                d_refs[t][...], nm_refs[t][...], nv_refs[t][...] = _adamw(w_refs[t][...], g, m_refs[t][...], v_refs[t][...])

    shapes = [SDS(w.shape, F32) for w in ws]
    outs = pl.pallas_call(
        body, name="small_adamw",
        in_specs=[pl.BlockSpec(memory_space=pltpu.SMEM)] + [VM] * (4 * n + 1),
        out_specs=[VM] * (4 * n + 1),
        out_shape=shapes + [SDS(sums[-1].shape, F32)] + shapes * 3,
    )(my_slot, *sums, *ws, *ms, *vs)
    return outs[:n + 1], outs[n + 1:2 * n + 1], outs[2 * n + 1:3 * n + 1], outs[3 * n + 1:]


def kernel(x, a_norm, a_w_in, a_rel_bias, a_w_out, kv_norm, kv_w, t5_bias, b_norm, b_w_in, b_sinks, b_w_out, final_norm, loss_target, m_a_norm, m_a_w_in, m_a_rel_bias, m_a_w_out, m_kv_norm, m_kv_w, m_t5_bias, m_b_norm, m_b_w_in, m_b_sinks, m_b_w_out, m_final_norm, v_a_norm, v_a_w_in, v_a_rel_bias, v_a_w_out, v_kv_norm, v_kv_w, v_t5_bias, v_b_norm, v_b_w_in, v_b_sinks, v_b_w_out, v_final_norm):
    xi, yi, ci = lax.axis_index("x"), lax.axis_index("y"), lax.axis_index("c")
    my_slot = _slot(xi, yi, ci)

    slot_arr = jnp.reshape(my_slot, (1,)).astype(jnp.int32)
    order = _gather_order(xi, yi, ci)
    late_shards = [b_w_in[0].astype(BF16), a_w_out[0].astype(BF16), b_w_out[0].astype(BF16), kv_w.astype(BF16)]
    grad_x, loc, matrices = _local_step(
        slot_arr, order, x[0], loss_target[0], a_norm, a_w_in[0].astype(BF16), a_rel_bias[0], late_shards,
        kv_norm.reshape(1, D_MODEL), t5_bias, b_norm, b_sinks, final_norm.reshape(1, D_MODEL))

    shard_w = dict(a_w_in=a_w_in[0], b_w_in=b_w_in[0], a_w_out=a_w_out[0], b_w_out=b_w_out[0], kv_w=kv_w)
    shard_m = dict(a_w_in=m_a_w_in[0], b_w_in=m_b_w_in[0], a_w_out=m_a_w_out[0], b_w_out=m_b_w_out[0], kv_w=m_kv_w)
    shard_v = dict(a_w_in=v_a_w_in[0], b_w_in=v_b_w_in[0], a_w_out=v_a_w_out[0], b_w_out=v_b_w_out[0], kv_w=v_kv_w)
    big = {n: _reduce_adamw("adamw_" + n, own, partials, shard_w[n], shard_m[n], shard_v[n])
           for n, (own, partials) in matrices.items()}

    names = ("a_norm", "a_rel_bias", "kv_norm", "t5_bias", "b_norm", "b_sinks", "final_norm")
    tables = ("a_rel_bias", "t5_bias")

    def row(n, a):
        return a.reshape(-1, a.shape[-1]).T if n in tables else a.reshape(1, -1)

    small_w = [row(n, a) for n, a in zip(names, (a_norm, a_rel_bias, kv_norm, t5_bias, b_norm, b_sinks, final_norm))]
    small_m = [row(n, a) for n, a in zip(names, (m_a_norm, m_a_rel_bias, m_kv_norm, m_t5_bias, m_b_norm, m_b_sinks,
                                                 m_final_norm))]
    small_v = [row(n, a) for n, a in zip(names, (v_a_norm, v_a_rel_bias, v_kv_norm, v_t5_bias, v_b_norm, v_b_sinks,
                                                 v_final_norm))]
    sums = dict(zip(names + ("loss",), _small_allreduce([loc[n] for n in names] + [loc["loss"]])))
    sums["a_rel_bias"] = _a_bias_grad(sums["a_rel_bias"])
    sums["t5_bias"] = sums["t5_bias"][:, :T5_BUCKETS]
    sums["b_sinks"] = sums["b_sinks"][:, 0].reshape(1, N_HEADS)
    results = _small_adamw(slot_arr, [sums[n] for n in names + ("loss",)], small_w, small_m, small_v)
    like = dict(a_norm=a_norm, a_rel_bias=a_rel_bias, kv_norm=kv_norm, t5_bias=t5_bias, b_norm=b_norm,
                b_sinks=b_sinks, final_norm=final_norm)
    sm = [{n: (part[i].T if n in tables else part[i]).reshape(like[n].shape) for i, n in enumerate(names)}
          for part in results]
    loss = results[0][len(names)][0, 0]

    order = ("a_norm", "a_w_in", "a_rel_bias", "a_w_out", "kv_norm", "kv_w", "t5_bias", "b_norm",
             "b_w_in", "b_sinks", "b_w_out", "final_norm")
    lead = dict(a_w_in=True, b_w_in=True, a_w_out=True, b_w_out=True, kv_w=False)

    def pick(kind, name):
        if name in big:
            val = big[name][kind]
            return val[None] if lead[name] else val
        return sm[kind][name]

    outs = [loss, grad_x[None]]
    for kind in range(4):
        outs += [pick(kind, n) for n in order]
    return tuple(outs)
```

```python
import functools
import math

import numpy as np
import jax
import jax.numpy as jnp
from jax import lax
from jax.experimental import pallas as pl
from jax.experimental.pallas import tpu as pltpu

F32 = jnp.float32
BF16 = jnp.bfloat16
SDS = jax.ShapeDtypeStruct

D_MODEL = 1024
HEAD_DIM = 64
CHUNK = 64
N_HEADS = 16
RMS_EPS = 1e-6
A_LEFT_CHUNKS = 8
A_BAND = (A_LEFT_CHUNKS + 1) * CHUNK
A_REL_CLIP = 256
B_KV_HEADS = 2
B_GROUP = 8
B_LEFT_CHUNKS = 2
B_BAND = (B_LEFT_CHUNKS + 1) * CHUNK
T5_BUCKETS = 32
T5_MAX_DIST = 128
QBLK = 256
A_KEYS = 3 * QBLK
B_QBLK_FWD = 128
B_QBLK_BWD = 256
B_PREV = 128
A_DIAG = A_KEYS
NEG = -1e30
SCALE = HEAD_DIM ** -0.5
N_DEV = 8

ADAM_LR = 0.001
ADAM_B1 = 0.9
ADAM_B2 = 0.999
ADAM_EPS = 1e-08
ADAM_WD = 0.01
ADAM_STEP = 10

VMEM_LIMIT_BYTES = 56 * 1024 * 1024
MESH = pl.DeviceIdType.MESH


def _cparams():
    return pltpu.CompilerParams(vmem_limit_bytes=VMEM_LIMIT_BYTES)


def _dot(a, b):
    return jnp.dot(a, b, preferred_element_type=F32)


def _dot_nt(a, b):
    return lax.dot_general(a, b, (((1,), (1,)), ((), ())), preferred_element_type=F32)


def _dot_tn(a, b):
    return lax.dot_general(a, b, (((0,), (0,)), ((), ())), preferred_element_type=F32)


def _rstd(xf):
    return lax.rsqrt(jnp.mean(xf * xf, axis=-1, keepdims=True) + RMS_EPS)


def _sigmoid(x):
    return 1.0 / (1.0 + jnp.exp(-x))


_GATHER_SEQUENCE = ((0, None), (1, 0), (2, 1), (4, None), (5, None), (3, 2), (6, None))


def _gather_order(x, y, c):
    others = [(1 - x, y), (x, 1 - y), (1 - x, 1 - y)]
    arrivals = [_slot(x, y, 1 - c)] + [_slot(*chip, c) for chip in others] + [_slot(*chip, 1 - c) for chip in others]
    return jnp.stack([_slot(x, y, c)] + [arrivals[a] for a, _ in _GATHER_SEQUENCE]).astype(jnp.int32)


def _norm_matmul_gather(order, x, gain_shard, w_shard):
    t = x.shape[0]
    dw, tn = w_shard.shape
    tm = min(t, 1024)
    n_m = t // tm

    def body(order_ref, x_ref, gs_ref, shard_ref, xn_ref, o_ref, full_ref, gain_ref,
             xn_all, wbuf, gland, send_sems, recv_sems, gsend_sems, grecv_sems, load_sems, own_sem):
        n, m = pl.program_id(0), pl.program_id(1)
        x_i, y_i, c_i, chips = _place()
        me, sibling = (x_i, y_i, c_i), (x_i, y_i, 1 - c_i)

        def send(k, block, to, src=None):
            dst = full_ref.at[_slot(*block)]
            return pltpu.make_async_remote_copy(
                src_ref=dst if src is None else src, dst_ref=dst,
                send_sem=send_sems.at[k], recv_sem=recv_sems.at[k], device_id=to, device_id_type=MESH)

        own = pltpu.make_async_copy(shard_ref, full_ref.at[_slot(*me)], own_sem)
        first = [send(0, me, sibling, src=shard_ref)]
        first += [send(1 + j, me, (*chip, c_i), src=shard_ref) for j, chip in enumerate(chips)]
        forwards = [send(4 + j, (*chip, c_i), sibling) for j, chip in enumerate(chips)]
        arrivals = [send(0, sibling, me)] + [send(1 + j, (*chip, c_i), me) for j, chip in enumerate(chips)]
        arrivals += [send(4 + j, (*chip, 1 - c_i), me) for j, chip in enumerate(chips)]
        gains = [pltpu.make_async_remote_copy(
            src_ref=gs_ref, dst_ref=gland.at[_slot(*me)], send_sem=gsend_sems.at[k - 1],
            recv_sem=grecv_sems.at[k - 1], device_id=_peer(x_i, y_i, c_i, k), device_id_type=MESH)
            for k in range(1, N_DEV)]

        @pl.when(jnp.logical_and(n == 0, m == 0))
        def _():
            own.start()
            for cp in gains + first:
                cp.start()
            pltpu.make_async_copy(shard_ref, wbuf.at[0], load_sems.at[0]).start()
            gland[_slot(*me)] = gs_ref[...]
            for k in range(1, N_DEV):
                pltpu.make_async_remote_copy(
                    src_ref=gs_ref, dst_ref=gland.at[_slot(*_peer(x_i, y_i, c_i, k))],
                    send_sem=gsend_sems.at[k - 1], recv_sem=grecv_sems.at[k - 1],
                    device_id=me, device_id_type=MESH).wait_recv()
            for s in range(N_DEV):
                gain_ref[:, 128 * s:128 * (s + 1)] = gland[s]

        rows = pl.ds(pl.multiple_of(m * tm, tm), tm)

        @pl.when(n == 0)
        def _():
            xf = x_ref[...]
            xn = ((xf * _rstd(xf)) * gain_ref[...]).astype(BF16)
            xn_all[rows, :] = xn
            xn_ref[...] = xn

        @pl.when(m == 0)
        def _():
            pltpu.make_async_copy(full_ref.at[0], wbuf.at[n % 2], load_sems.at[n % 2]).wait()

        o_ref[...] = _dot(xn_all[rows, :], wbuf[n % 2]).astype(BF16)

        for k, (arrival, forward) in enumerate(_GATHER_SEQUENCE):
            @pl.when(jnp.logical_and(n == k, m == n_m - 1))
            def _(k=k, arrival=arrival, forward=forward):
                arrivals[arrival].wait_recv()
                if forward is not None:
                    forwards[forward].start()
                pltpu.make_async_copy(full_ref.at[order_ref[k + 1]], wbuf.at[(k + 1) % 2],
                                      load_sems.at[(k + 1) % 2]).start()

        @pl.when(jnp.logical_and(n == N_DEV - 1, m == n_m - 1))
        def _():
            for cp in gains + first + forwards:
                cp.wait_send()
            own.wait()

    held = lambda n, m, order: (jnp.where(n == 0, m, n_m - 1), 0)
    return pl.pallas_call(
        body, name="norm_matmul_gather",
        grid_spec=pltpu.PrefetchScalarGridSpec(
            num_scalar_prefetch=1, grid=(N_DEV, n_m),
            in_specs=[pl.BlockSpec((tm, D_MODEL), held),
                      pl.BlockSpec((1, 128), lambda n, m, order: (0, 0)), ANY],
            out_specs=[pl.BlockSpec((tm, D_MODEL), held),
                       pl.BlockSpec((tm, tn), lambda n, m, order: (m, order[n])),
                       ANY, pl.BlockSpec((1, D_MODEL), lambda n, m, order: (0, 0))],
            scratch_shapes=[pltpu.VMEM((t, D_MODEL), BF16), pltpu.VMEM((2, dw, tn), BF16),
                            pltpu.VMEM((N_DEV, 1, 128), F32),
                            pltpu.SemaphoreType.DMA((7,)), pltpu.SemaphoreType.DMA((7,)),
                            pltpu.SemaphoreType.DMA((7,)), pltpu.SemaphoreType.DMA((7,)),
                            pltpu.SemaphoreType.DMA((2,)), pltpu.SemaphoreType.DMA]),
        out_shape=[SDS((t, D_MODEL), BF16), SDS((t, N_DEV * tn), BF16), SDS((N_DEV, dw, tn), BF16),
                   SDS((1, D_MODEL), F32)],
        compiler_params=_cparams(),
    )(order, x, gain_shard, w_shard)


def _layer_a_out(x, z, w_out, kv_gain, b_gain, kv_w, w_in_b):
    t = x.shape[0]
    tm = min(t, 512)
    nb, _, tn = w_in_b.shape

    def body(x_ref, z_ref, wo_ref, kvg_ref, bg_ref, kvw_ref, wb_ref,
             h1_ref, kvn_ref, hb_ref, kv_ref, qg_ref):
        h1 = x_ref[...] + _dot(z_ref[...], wo_ref[...])
        h1_ref[...] = h1
        y0 = h1 * _rstd(h1)
        kvn = (y0 * kvg_ref[...]).astype(BF16)
        hb = (y0 * bg_ref[...]).astype(BF16)
        kvn_ref[...] = kvn
        hb_ref[...] = hb
        kv_ref[...] = _dot(kvn, kvw_ref[...]).astype(BF16)
        for i in range(nb):
            qg_ref[:, i * tn:(i + 1) * tn] = _dot(hb, wb_ref[i]).astype(BF16)

    row = lambda m: (m, 0)
    fix2 = lambda m: (0, 0)
    return pl.pallas_call(
        body, name="layer_a_out", grid=(t // tm,),
        in_specs=[pl.BlockSpec((tm, D_MODEL), row), pl.BlockSpec((tm, D_MODEL), row),
                  pl.BlockSpec((D_MODEL, D_MODEL), fix2),
                  pl.BlockSpec((1, D_MODEL), fix2), pl.BlockSpec((1, D_MODEL), fix2),
                  pl.BlockSpec((D_MODEL, 256), fix2),
                  pl.BlockSpec((nb, D_MODEL, tn), lambda m: (0, 0, 0))],
        out_specs=[pl.BlockSpec((tm, D_MODEL), row), pl.BlockSpec((tm, D_MODEL), row),
                   pl.BlockSpec((tm, D_MODEL), row), pl.BlockSpec((tm, 256), row),
                   pl.BlockSpec((tm, nb * tn), row)],
        out_shape=[SDS((t, D_MODEL), F32), SDS((t, D_MODEL), BF16), SDS((t, D_MODEL), BF16),
                   SDS((t, 256), BF16), SDS((t, nb * tn), BF16)],
        compiler_params=_cparams(),
    )(x, z, w_out, kv_gain, b_gain, kv_w, w_in_b)


def _layer_b_out_loss(h1, z, w_out, f_gain, target):
    t = h1.shape[0]
    tm = min(t, 512)

    def body(h1_ref, z_ref, wo_ref, fg_ref, tgt_ref,
             dh2_ref, dh2b_ref, dz_ref, loss_ref, dfn_ref):
        @pl.when(pl.program_id(0) == 0)
        def _():
            loss_ref[...] = jnp.zeros_like(loss_ref)
            dfn_ref[...] = jnp.zeros_like(dfn_ref)

        h2 = h1_ref[...] + _dot(z_ref[...], wo_ref[...])
        r = _rstd(h2)
        yn = h2 * r
        fg = fg_ref[...]
        err = yn * fg - tgt_ref[...]
        loss_ref[...] += (0.5 / D_MODEL) * jnp.sum(err * err)
        dy = err * (1.0 / D_MODEL)
        dfn_ref[...] += jnp.sum(dy * yn, axis=0, keepdims=True)
        u = dy * fg
        dh2 = r * u - h2 * ((r * r * r) * jnp.mean(u * h2, axis=-1, keepdims=True))
        dh2_ref[...] = dh2
        dh2b = dh2.astype(BF16)
        dh2b_ref[...] = dh2b
        dz_ref[...] = _dot_nt(dh2b, wo_ref[...]).astype(BF16)

    row = lambda m: (m, 0)
    fix2 = lambda m: (0, 0)
    return pl.pallas_call(
        body, name="layer_b_out_loss", grid=(t // tm,),
        in_specs=[pl.BlockSpec((tm, D_MODEL), row), pl.BlockSpec((tm, D_MODEL), row),
                  pl.BlockSpec((D_MODEL, D_MODEL), fix2), pl.BlockSpec((1, D_MODEL), fix2),
                  pl.BlockSpec((tm, D_MODEL), row)],
        out_specs=[pl.BlockSpec((tm, D_MODEL), row), pl.BlockSpec((tm, D_MODEL), row),
                   pl.BlockSpec((tm, D_MODEL), row), pl.BlockSpec((1, 128), fix2),
                   pl.BlockSpec((1, D_MODEL), fix2)],
        out_shape=[SDS((t, D_MODEL), F32), SDS((t, D_MODEL), BF16), SDS((t, D_MODEL), BF16),
                   SDS((1, 128), F32), SDS((1, D_MODEL), F32)],
        compiler_params=_cparams(),
    )(h1, z, w_out, f_gain, target)


def _layer_b_in_bwd(dqg, dkv, w_in_b, kv_w, h1, dh2, b_gain, kv_gain, w_out_a):
    t = h1.shape[0]
    tm = min(t, 512)
    nb, _, tn = w_in_b.shape
    per = D_MODEL // tn

    def body(dqg_ref, dkv_ref, wb_ref, kvw_ref, h1_ref, dh2_ref, bg_ref, kvg_ref, wo_ref,
             dh1_ref, dh1b_ref, dz_ref, dbn_ref, dkn_ref):
        @pl.when(pl.program_id(0) == 0)
        def _():
            dbn_ref[...] = jnp.zeros_like(dbn_ref)
            dkn_ref[...] = jnp.zeros_like(dkn_ref)

        dhb = jnp.zeros((tm, D_MODEL), F32)
        for i in range(nb):
            blk = dqg_ref[i // per, :, (i % per) * tn:(i % per + 1) * tn]
            dhb = dhb + _dot_nt(blk, wb_ref[i])
        dkn = (_dot_nt(dkv_ref[0].astype(BF16), kvw_ref[:, 0:128])
               + _dot_nt(dkv_ref[1].astype(BF16), kvw_ref[:, 128:256]))
        h1 = h1_ref[...]
        r = _rstd(h1)
        xr = h1 * r
        dbn_ref[...] += jnp.sum(dhb * xr, axis=0, keepdims=True)
        dkn_ref[...] += jnp.sum(dkn * xr, axis=0, keepdims=True)
        u = dhb * bg_ref[...] + dkn * kvg_ref[...]
        dh1 = dh2_ref[...] + r * u - h1 * ((r * r * r) * jnp.mean(u * h1, axis=-1, keepdims=True))
        dh1_ref[...] = dh1
        dh1b = dh1.astype(BF16)
        dh1b_ref[...] = dh1b
        dz_ref[...] = _dot_nt(dh1b, wo_ref[...]).astype(BF16)

    row = lambda m: (m, 0)
    fix2 = lambda m: (0, 0)
    return pl.pallas_call(
        body, name="layer_b_in_bwd", grid=(t // tm,),
        in_specs=[pl.BlockSpec((2, tm, D_MODEL), lambda m: (0, m, 0)),
                  pl.BlockSpec((2, tm, 128), lambda m: (0, m, 0)),
                  pl.BlockSpec((nb, D_MODEL, tn), lambda m: (0, 0, 0)),
                  pl.BlockSpec((D_MODEL, 256), fix2),
                  pl.BlockSpec((tm, D_MODEL), row), pl.BlockSpec((tm, D_MODEL), row),
                  pl.BlockSpec((1, D_MODEL), fix2), pl.BlockSpec((1, D_MODEL), fix2),
                  pl.BlockSpec((D_MODEL, D_MODEL), fix2)],
        out_specs=[pl.BlockSpec((tm, D_MODEL), row), pl.BlockSpec((tm, D_MODEL), row),
                   pl.BlockSpec((tm, D_MODEL), row), pl.BlockSpec((1, D_MODEL), fix2),
                   pl.BlockSpec((1, D_MODEL), fix2)],
        out_shape=[SDS((t, D_MODEL), F32), SDS((t, D_MODEL), BF16), SDS((t, D_MODEL), BF16),
                   SDS((1, D_MODEL), F32), SDS((1, D_MODEL), F32)],
        compiler_params=_cparams(),
    )(dqg, dkv, w_in_b, kv_w, h1, dh2, b_gain, kv_gain, w_out_a)


def _layer_a_in_bwd(dqg, dkv, w_in_a, x, dh1, a_gain, chip_sums):
    t = x.shape[0]
    tm = min(t, 512)
    nb, _, tn = w_in_a.shape
    per = D_MODEL // tn

    def body(dqg_ref, dkv_ref, w_ref, x_ref, dh1_ref, ag_ref, sums_ref, dx_ref, dan_ref, land_ref,
             send_sems, recv_sems):
        @pl.when(pl.program_id(0) == 0)
        def _():
            dan_ref[...] = jnp.zeros_like(dan_ref)
            for cp in _chip_copies(sums_ref, land_ref, send_sems, recv_sems):
                cp.start()

        dxn = jnp.zeros((tm, D_MODEL), F32)
        for i in range(nb):
            part = i // per
            src = dqg_ref if part in (0, 3) else dkv_ref
            outer = {0: 0, 3: 1, 1: 0, 2: 1}[part]
            blk = src[outer, :, (i % per) * tn:(i % per + 1) * tn]
            dxn = dxn + _dot_nt(blk, w_ref[i])
        xf = x_ref[...]
        r = _rstd(xf)
        dan_ref[...] += jnp.sum(dxn * (xf * r), axis=0, keepdims=True)
        u = dxn * ag_ref[...]
        dx_ref[...] = dh1_ref[...] + r * u - xf * ((r * r * r) * jnp.mean(u * xf, axis=-1, keepdims=True))

        @pl.when(pl.program_id(0) == t // tm - 1)
        def _():
            for cp in _chip_copies(sums_ref, land_ref, send_sems, recv_sems):
                cp.wait()

    row = lambda m: (m, 0)
    fix2 = lambda m: (0, 0)
    return pl.pallas_call(
        body, name="layer_a_in_bwd", grid=(t // tm,),
        in_specs=[pl.BlockSpec((2, tm, D_MODEL), lambda m: (0, m, 0)),
                  pl.BlockSpec((2, tm, D_MODEL), lambda m: (0, m, 0)),
                  pl.BlockSpec((nb, D_MODEL, tn), lambda m: (0, 0, 0)),
                  pl.BlockSpec((tm, D_MODEL), row), pl.BlockSpec((tm, D_MODEL), row),
                  pl.BlockSpec((1, D_MODEL), fix2), ANY],
        out_specs=[pl.BlockSpec((tm, D_MODEL), row), pl.BlockSpec((1, D_MODEL), fix2), ANY],
        out_shape=[SDS((t, D_MODEL), F32), SDS((1, D_MODEL), F32), SDS(chip_sums.shape, chip_sums.dtype)],
        scratch_shapes=[pltpu.SemaphoreType.DMA((3,)), pltpu.SemaphoreType.DMA((3,))],
        compiler_params=_cparams(),
    )(dqg, dkv, w_in_a, x, dh1, a_gain, chip_sums)


def _lut(s, vals):
    r = jnp.int32(vals[0])
    for i in range(1, len(vals)):
        r = jnp.where(s == i, jnp.int32(vals[i]), r)
    return r


def _held(steps, i):
    seq, cur = [None] * len(steps), None
    for k in range(len(steps) - 1, -1, -1):
        if steps[k][0] == i:
            cur = steps[k][1:3]
        seq[k] = cur
    for k in range(len(steps)):
        cur = seq[k] = seq[k] if seq[k] is not None else cur
    return seq


def _weight_grad_cols(name, my_slot, a, bs, steps, tn):
    t, dw = a.shape
    n_arr = len(bs)
    which = [s[0] for s in steps]
    blks = [s[3] for s in steps]

    def body(slot_ref, a_ref, *rest):
        b_refs, (o_ref, own_ref, at_ref) = rest[:n_arr], rest[n_arr:]
        s = pl.program_id(0)

        @pl.when(s == 0)
        def _():
            at_ref[...] = a_ref[...].T

        for i in range(n_arr):
            @pl.when(_lut(s, which) == i)
            def _(i=i):
                res = _dot(at_ref[...], b_refs[i][0])
                o_ref[0] = res.astype(BF16)

                @pl.when(_lut(s, blks) == slot_ref[0])
                def _():
                    own_ref[...] = res

    def b_spec(i):
        held = _held(steps, i)
        return pl.BlockSpec((1, t, tn), lambda s, slot: (_lut(s, [h[0] for h in held]), 0,
                                                         _lut(s, [h[1] for h in held])))

    return pl.pallas_call(
        body, name=name,
        grid_spec=pltpu.PrefetchScalarGridSpec(
            num_scalar_prefetch=1, grid=(len(steps),),
            in_specs=[pl.BlockSpec((t, dw), lambda s, slot: (0, 0))] + [b_spec(i) for i in range(n_arr)],
            out_specs=[pl.BlockSpec((1, dw, tn), lambda s, slot: (_lut(s, blks), 0, 0)),
                       pl.BlockSpec((dw, tn), lambda s, slot: (0, 0))],
            scratch_shapes=[pltpu.VMEM((dw, t), BF16)]),
        out_shape=[SDS((N_DEV, dw, tn), BF16), SDS((dw, tn), F32)],
        compiler_params=_cparams(),
    )(my_slot, a, *bs)


def _weight_grad_rows(name, my_slot, a, b):
    t, dw = a.shape
    n_o, _, c = b.shape
    rows = dw // N_DEV
    tn = min(c, 256)
    per = c // tn

    def body(slot_ref, a_ref, b_ref, o_ref, own_ref, at_ref, res_ref):
        @pl.when(pl.program_id(0) == 0)
        def _():
            at_ref[...] = a_ref[...].T

        res_ref[...] = _dot(at_ref[...], b_ref[0].astype(BF16))
        o_ref[...] = res_ref[...].astype(BF16)
        own_ref[...] = res_ref[pl.ds(pl.multiple_of(slot_ref[0] * rows, rows), rows), :]

    all_rows, own = pl.pallas_call(
        body, name=name,
        grid_spec=pltpu.PrefetchScalarGridSpec(
            num_scalar_prefetch=1, grid=(n_o * per,),
            in_specs=[pl.BlockSpec((t, dw), lambda s, slot: (0, 0)),
                      pl.BlockSpec((1, t, tn), lambda s, slot: (s // per, 0, s % per))],
            out_specs=[pl.BlockSpec((dw, tn), lambda s, slot: (0, s)),
                       pl.BlockSpec((rows, tn), lambda s, slot: (0, s))],
            scratch_shapes=[pltpu.VMEM((dw, t), BF16), pltpu.VMEM((dw, tn), F32)]),
        out_shape=[SDS((dw, n_o * c), BF16), SDS((rows, n_o * c), F32)],
        compiler_params=_cparams(),
    )(my_slot, a, b)
    return all_rows.reshape(N_DEV, rows, n_o * c), own


def _lane_lo():
    return lax.broadcasted_iota(jnp.int32, (1, 128), 1) < HEAD_DIM


def _collapse_chunks(ds, keys):
    if ds.shape[1] < keys:
        ds = jnp.concatenate([jnp.zeros((ds.shape[0], keys - ds.shape[1]), F32), ds], axis=1)
    gc = ds[0:CHUNK]
    for cc in range(1, ds.shape[0] // CHUNK):
        gc = gc + pltpu.roll(ds[cc * CHUNK:(cc + 1) * CHUNK], keys - cc * CHUNK, 1)
    return gc


def _offset_sums(gc):
    hi = gc.astype(BF16)
    lo = (gc - hi.astype(F32)).astype(BF16)
    flip = (lax.broadcasted_iota(jnp.int32, (CHUNK, CHUNK), 0)
            + lax.broadcasted_iota(jnp.int32, (CHUNK, CHUNK), 1) == CHUNK - 1).astype(BF16)
    gf = _dot(flip, hi) + _dot(flip, lo)
    skew = pltpu.roll(gf, 0, 1, stride=1, stride_axis=0)
    return jnp.sum(skew, axis=0, keepdims=True)


def _band_bias(w_row, band, rows):
    keys = w_row.shape[1]
    base = jnp.broadcast_to(w_row, (CHUNK, keys))
    skew = pltpu.roll(base, 0, 1, stride=1, stride_axis=0)
    skew = pltpu.roll(skew, keys - (CHUNK - 1), 1)
    col = lax.broadcasted_iota(jnp.int32, (CHUNK, keys), 1)
    chunk0 = jnp.where(col < band, skew, NEG)
    return jnp.concatenate(
        [chunk0] + [pltpu.roll(chunk0, cc * CHUNK, 1) for cc in range(1, rows // CHUNK)], axis=0)


def _silu_parts(g):
    sg = _sigmoid(g)
    return g * sg, sg * (1.0 + g * (1.0 - sg))


A_PAIRS = 2
A_LANES = 128 * A_PAIRS
A_STEPS = D_MODEL // A_LANES


def _a_specs():
    q = pl.BlockSpec((QBLK, A_LANES), lambda p, j: (j, p))
    ks = [pl.BlockSpec((QBLK, A_LANES), lambda p, j, b=b: (jnp.maximum(j - 2 + b, 0), A_STEPS + p)) for b in range(3)]
    vs = [pl.BlockSpec((QBLK, A_LANES), lambda p, j, b=b: (jnp.maximum(j - 2 + b, 0), 2 * A_STEPS + p))
          for b in range(3)]
    g = pl.BlockSpec((QBLK, A_LANES), lambda p, j: (j, 3 * A_STEPS + p))
    bias = pl.BlockSpec((A_PAIRS, 8, A_KEYS), lambda p, j: (p, 0, 0))
    return q, ks, vs, g, bias


def _a_fill_bias(w_ref, b_ref, j):
    _fill_bias(2 * A_PAIRS, lambda h: w_ref[h // 2, h % 2:h % 2 + 1, :], A_BAND, b_ref, j)


def _by_valid_key_blocks(j, fn):
    pl.when(j == 0)(functools.partial(fn, 1))
    pl.when(j == 1)(functools.partial(fn, 2))
    pl.when(j >= 2)(functools.partial(fn, 3))


def _fill_bias(n, get_row, band, bias_scr, j):
    @pl.when(j == 0)
    def _():
        for h in range(n):
            bias_scr[h] = _band_bias(get_row(h), band, bias_scr.shape[1])


def _row_sums_everywhere(r, sel):
    return jnp.where(sel, pltpu.roll(r, HEAD_DIM, 1), r)


def _own_everywhere(x, sel):
    return jnp.where(sel, x, pltpu.roll(x, HEAD_DIM, 1))


def _minus_rows(s, row_full):
    return jnp.concatenate([s[:, i:i + 128] - row_full for i in range(0, s.shape[1], 128)], axis=1)


def _attn_a_fwd(qkvg, bias, gather):
    t = qkvg.shape[0]
    nq = t // QBLK
    n_g = len(gather)
    q_spec, k_specs, v_specs, g_spec, bias_spec = _a_specs()

    def body(q_ref, k0, k1, k2, v0, v1, v2, g_ref, w_ref, *rest):
        shard_refs, rest = rest[:n_g], rest[n_g:]
        z_ref, o_ref, lse_ref = rest[:3]
        full_refs, (b_ref, *comm) = rest[3:3 + n_g], rest[3 + n_g:]
        p = pl.program_id(0)
        j = pl.program_id(1)
        start, forward, finish = _gather_phases(shard_refs, full_refs, *comm)
        pl.when(jnp.logical_and(p == 0, j == 0))(start)
        pl.when(jnp.logical_and(p == A_STEPS // 2, j == 0))(forward)
        _a_fill_bias(w_ref, b_ref, j)
        lane_lo = _lane_lo()
        sels = (lane_lo, jnp.logical_not(lane_lo))

        def attend(n_blocks):
            first_col = (3 - n_blocks) * QBLK
            for pp in range(A_PAIRS):
                cols = slice(128 * pp, 128 * (pp + 1))
                k = jnp.concatenate([r[:, cols] for r in (k0, k1, k2)[3 - n_blocks:]], axis=0)
                v = jnp.concatenate([r[:, cols] for r in (v0, v1, v2)[3 - n_blocks:]], axis=0)
                q = q_ref[:, cols]
                qm2 = jnp.concatenate([jnp.where(sel, q, jnp.zeros_like(q)) for sel in sels], axis=0) * SCALE
                s2 = _dot_nt(qm2, k)
                outs, lses = [], []
                for hh, sel in enumerate(sels):
                    s = s2[hh * QBLK:(hh + 1) * QBLK] + b_ref[2 * pp + hh, :, first_col:]
                    mx = jnp.max(s, axis=-1, keepdims=True)
                    e = jnp.exp(s - mx).astype(BF16)
                    r = _dot(e, jnp.where(sel, v, jnp.ones_like(v)))
                    l = _row_sums_everywhere(r, sel)
                    outs.append(r / l)
                    lses.append(mx + jnp.log(l))
                o = jnp.where(lane_lo, outs[0], outs[1])
                silu, _ = _silu_parts(g_ref[:, cols].astype(F32))
                o_ref[:, cols] = o.astype(BF16)
                z_ref[:, cols] = (o * silu).astype(BF16)
                lse_ref[:, cols] = jnp.where(lane_lo, lses[0], lses[1])

        _by_valid_key_blocks(j, attend)
        pl.when(jnp.logical_and(p == A_STEPS - 1, j == nq - 1))(finish)

    out_spec = pl.BlockSpec((QBLK, A_LANES), lambda p, j: (j, p))
    outs = pl.pallas_call(
        body, name="attn_a_fwd", grid=(A_STEPS, nq),
        in_specs=[q_spec, *k_specs, *v_specs, g_spec, bias_spec] + [ANY] * n_g,
        out_specs=[out_spec, out_spec, out_spec] + [ANY] * n_g,
        out_shape=[SDS((t, D_MODEL), BF16), SDS((t, D_MODEL), BF16), SDS((t, D_MODEL), F32)]
        + [SDS((N_DEV, *s.shape), s.dtype) for s in gather],
        scratch_shapes=[pltpu.VMEM((2 * A_PAIRS, QBLK, A_KEYS), F32)] + _gather_scratch(n_g),
        compiler_params=_cparams(),
    )(qkvg, qkvg, qkvg, qkvg, qkvg, qkvg, qkvg, qkvg, bias, *gather)
    return outs[0], outs[1], outs[2], list(outs[3:])


def _attn_a_bwd(qkvg, bias, out_a, lse, dz, scatter):
    t = qkvg.shape[0]
    nq = t // QBLK
    n_sc = len(scatter)
    q_spec, k_specs, v_specs, g_spec, bias_spec = _a_specs()

    def body(q_ref, k0, k1, k2, v0, v1, v2, g_ref, w_ref, o_ref, lse_ref, dz_ref, *rest):
        sc_refs, rest = rest[:n_sc], rest[n_sc:]
        dqg_ref, dkv_ref, dg_ref = rest[:3]
        land_refs, rest = rest[3:3 + n_sc], rest[3 + n_sc:]
        dk_acc, dv_acc, gt_acc, b_ref, send_sems, recv_sems = rest
        j = pl.program_id(1)
        first = jnp.logical_and(pl.program_id(0) == 0, j == 0)
        last = jnp.logical_and(pl.program_id(0) == A_STEPS - 1, j == nq - 1)

        @pl.when(first)
        def _():
            for cp in _scatter_copies(sc_refs, land_refs, send_sems, recv_sems):
                cp.start()

        _a_fill_bias(w_ref, b_ref, j)

        @pl.when(j == 0)
        def _():
            dk_acc[...] = jnp.zeros_like(dk_acc)
            dv_acc[...] = jnp.zeros_like(dv_acc)
            gt_acc[...] = jnp.zeros_like(gt_acc)

        lane_lo = _lane_lo()
        sels = (lane_lo, jnp.logical_not(lane_lo))

        def attend(n_blocks):
            first_col = (3 - n_blocks) * QBLK
            for pp in range(A_PAIRS):
                cols = slice(128 * pp, 128 * (pp + 1))
                q = q_ref[:, cols]
                k = jnp.concatenate([r[:, cols] for r in (k0, k1, k2)[3 - n_blocks:]], axis=0)
                v = jnp.concatenate([r[:, cols] for r in (v0, v1, v2)[3 - n_blocks:]], axis=0)
                o = o_ref[:, cols].astype(F32)
                lse_pair = lse_ref[:, cols]
                dzf = dz_ref[:, cols].astype(F32)
                silu, dsilu = _silu_parts(g_ref[:, cols].astype(F32))
                do = dzf * silu
                dqg_ref[1, :, cols] = (dzf * o * dsilu).astype(BF16)
                doo = do * o
                qm2 = jnp.concatenate([jnp.where(sel, q, jnp.zeros_like(q)) for sel in sels], axis=0) * SCALE
                dom2 = jnp.concatenate([jnp.where(sel, do, 0.0) for sel in sels], axis=0).astype(BF16)
                s2 = _dot_nt(qm2, k)
                dp2 = _dot_nt(dom2, v)
                ps, dss = [], []
                for hh, sel in enumerate(sels):
                    rows = slice(hh * QBLK, (hh + 1) * QBLK)
                    s = s2[rows] + b_ref[2 * pp + hh, :, first_col:]
                    p = jnp.exp(_minus_rows(s, _own_everywhere(lse_pair, sel)))
                    delta = jnp.sum(jnp.where(sel, doo, 0.0), axis=-1, keepdims=True)
                    ds = p * (dp2[rows] - delta)
                    gt_acc[2 * pp + hh] += _collapse_chunks(ds, A_KEYS)
                    ps.append(p.astype(BF16))
                    dss.append(ds.astype(BF16))
                dsb2 = jnp.concatenate(dss, axis=0)
                dq2 = _dot(dsb2, k) * SCALE
                dk_blk = _dot_tn(dsb2, qm2)
                dv_blk = _dot_tn(jnp.concatenate(ps, axis=0), dom2)
                dqg_ref[0, :, cols] = jnp.where(lane_lo, dq2[0:QBLK], dq2[QBLK:2 * QBLK]).astype(BF16)
                for b in range(n_blocks):
                    rows = pl.ds(pl.multiple_of((j - n_blocks + 1 + b) * QBLK, QBLK), QBLK)
                    dk_acc[rows, cols] += dk_blk[b * QBLK:(b + 1) * QBLK]
                    dv_acc[rows, cols] += dv_blk[b * QBLK:(b + 1) * QBLK]

        _by_valid_key_blocks(j, attend)

        @pl.when(j == nq - 1)
        def _():
            dkv_ref[0] = dk_acc[...].astype(BF16)
            dkv_ref[1] = dv_acc[...].astype(BF16)
            for pp in range(A_PAIRS):
                dg_ref[pp] = jnp.concatenate([_offset_sums(gt_acc[2 * pp]), _offset_sums(gt_acc[2 * pp + 1]),
                                              jnp.zeros((6, A_DIAG), F32)], axis=0)

        @pl.when(last)
        def _():
            for cp in _scatter_copies(sc_refs, land_refs, send_sems, recv_sems):
                cp.wait()

    blk = pl.BlockSpec((QBLK, A_LANES), lambda p, j: (j, p))
    outs = pl.pallas_call(
        body, name="attn_a_bwd", grid=(A_STEPS, nq),
        in_specs=[q_spec, *k_specs, *v_specs, g_spec, bias_spec, blk, blk, blk] + [ANY] * n_sc,
        out_specs=[pl.BlockSpec((2, QBLK, A_LANES), lambda p, j: (0, j, p)),
                   pl.BlockSpec((2, t, A_LANES), lambda p, j: (0, 0, p)),
                   pl.BlockSpec((A_PAIRS, 8, A_DIAG), lambda p, j: (p, 0, 0))] + [ANY] * n_sc,
        out_shape=[SDS((2, t, D_MODEL), BF16), SDS((2, t, D_MODEL), BF16), SDS((N_HEADS // 2, 8, A_DIAG), F32)]
        + [SDS((N_DEV - 1, *g.shape[1:]), g.dtype) for g in scatter],
        scratch_shapes=[pltpu.VMEM((t, A_LANES), F32), pltpu.VMEM((t, A_LANES), F32),
                        pltpu.VMEM((2 * A_PAIRS, CHUNK, A_KEYS), F32), pltpu.VMEM((2 * A_PAIRS, QBLK, A_KEYS), F32),
                        pltpu.SemaphoreType.DMA(((N_DEV - 1) * n_sc,)),
                        pltpu.SemaphoreType.DMA(((N_DEV - 1) * n_sc,))],
        compiler_params=_cparams(),
    )(qkvg, qkvg, qkvg, qkvg, qkvg, qkvg, qkvg, qkvg, bias, out_a, lse, dz, *scatter)
    return outs[0], outs[1], outs[2], list(outs[3:])


def _b_specs(qblk):
    per = qblk // B_PREV
    q = pl.BlockSpec((qblk, 512), lambda h, j: (j, h))
    g = pl.BlockSpec((qblk, 512), lambda h, j: (j, 2 + h))
    kp = pl.BlockSpec((B_PREV, 128), lambda h, j: (jnp.maximum(per * j - 1, 0), 0))
    kc = pl.BlockSpec((qblk, 128), lambda h, j: (j, 0))
    vp = pl.BlockSpec((B_PREV, 128), lambda h, j: (jnp.maximum(per * j - 1, 0), 1))
    vc = pl.BlockSpec((qblk, 128), lambda h, j: (j, 1))
    bias = pl.BlockSpec((B_GROUP, qblk + B_PREV), lambda h, j: (h, 0))
    sinks = pl.BlockSpec(memory_space=pltpu.SMEM)
    return q, g, kp, kc, vp, vc, bias, sinks


def _b_operands(kp, kc, vp, vc, kvh, with_prev):
    k = jnp.concatenate([kp[...], kc[...]], axis=0) if with_prev else kc[...]
    v = jnp.concatenate([vp[...], vc[...]], axis=0) if with_prev else vc[...]
    kr = pltpu.roll(k, HEAD_DIM, 1)
    vr = pltpu.roll(v, HEAD_DIM, 1)
    first = kvh == 0
    return (jnp.where(first, k, kr), jnp.where(first, kr, k),
            jnp.where(first, v, vr), jnp.where(first, vr, v))


def _attn_b_fwd(qg, kv, bias, sinks):
    t = qg.shape[0]
    qblk = B_QBLK_FWD
    q_spec, g_spec, kp_spec, kc_spec, vp_spec, vc_spec, bias_spec, sink_spec = _b_specs(qblk)

    def body(q_ref, g_ref, kp, kc, vp, vc, w_ref, sink_ref, z_ref, o_ref, lse_ref, b_ref):
        kvh = pl.program_id(0)
        j = pl.program_id(1)
        _fill_bias(B_GROUP, lambda h: w_ref[h:h + 1, :], B_BAND, b_ref, j)
        lane_lo = _lane_lo()
        n_pairs = B_GROUP // 2

        def attend(with_prev):
            first_col = 0 if with_prev else B_PREV
            k_lo, k_hi, v_lo, v_hi = _b_operands(kp, kc, vp, vc, kvh, with_prev)
            halves = []
            for hh, sel in enumerate((lane_lo, jnp.logical_not(lane_lo))):
                kk = k_lo if hh == 0 else k_hi
                vv = v_lo if hh == 0 else v_hi
                qm4 = jnp.concatenate(
                    [jnp.where(sel, q_ref[:, 128 * pp:128 * (pp + 1)], jnp.zeros((qblk, 128), BF16))
                     for pp in range(n_pairs)], axis=0) * SCALE
                s4 = _dot_nt(qm4, kk)
                es, mxs = [], []
                for pp in range(n_pairs):
                    g = 2 * pp + hh
                    s = s4[pp * qblk:(pp + 1) * qblk] + b_ref[g, :, first_col:]
                    mxs.append(jnp.maximum(jnp.max(s, axis=-1, keepdims=True), sink_ref[kvh * B_GROUP + g]))
                    es.append(jnp.exp(s - mxs[pp]).astype(BF16))
                r4 = _dot(jnp.concatenate(es, axis=0), jnp.where(sel, vv, jnp.ones_like(vv)))
                outs, lses = [], []
                for pp in range(n_pairs):
                    r = r4[pp * qblk:(pp + 1) * qblk]
                    l = _row_sums_everywhere(r, sel) + jnp.exp(sink_ref[kvh * B_GROUP + 2 * pp + hh] - mxs[pp])
                    outs.append(r / l)
                    lses.append(mxs[pp] + jnp.log(l))
                halves.append((outs, lses))
            for pp in range(n_pairs):
                cols = slice(128 * pp, 128 * (pp + 1))
                o = jnp.where(lane_lo, halves[0][0][pp], halves[1][0][pp])
                silu, _ = _silu_parts(g_ref[:, cols].astype(F32))
                o_ref[:, cols] = o.astype(BF16)
                z_ref[:, cols] = (o * silu).astype(BF16)
                lse_ref[:, cols] = jnp.where(lane_lo, halves[0][1][pp], halves[1][1][pp])

        pl.when(j == 0)(functools.partial(attend, False))
        pl.when(j >= 1)(functools.partial(attend, True))

    out_spec = pl.BlockSpec((qblk, 512), lambda h, j: (j, h))
    return pl.pallas_call(
        body, name="attn_b_fwd", grid=(B_KV_HEADS, t // qblk),
        in_specs=[q_spec, g_spec, kp_spec, kc_spec, vp_spec, vc_spec, bias_spec, sink_spec],
        out_specs=[out_spec, out_spec, out_spec],
        out_shape=[SDS((t, D_MODEL), BF16), SDS((t, D_MODEL), BF16), SDS((t, D_MODEL), F32)],
        scratch_shapes=[pltpu.VMEM((B_GROUP, qblk, qblk + B_PREV), F32)],
        compiler_params=_cparams(),
    )(qg, qg, kv, kv, kv, kv, bias, sinks)


def _attn_b_bwd(qg, kv, bias, sinks, out_b, lse, dz, bucket_onehot):
    t = qg.shape[0]
    qblk = B_QBLK_BWD
    keys = qblk + B_PREV
    nq = t // qblk
    q_spec, g_spec, kp_spec, kc_spec, vp_spec, vc_spec, bias_spec, sink_spec = _b_specs(qblk)

    def body(q_ref, g_ref, kp, kc, vp, vc, w_ref, sink_ref, o_ref, lse_ref, dz_ref, oh_ref,
             dqg_ref, dkv_ref, dt5_ref, dsink_ref, gt_acc, b_ref):
        kvh = pl.program_id(0)
        j = pl.program_id(1)
        _fill_bias(B_GROUP, lambda h: w_ref[h:h + 1, :], B_BAND, b_ref, j)

        @pl.when(jnp.logical_and(kvh == 0, j == 0))
        def _():
            dkv_ref[...] = jnp.zeros_like(dkv_ref)

        @pl.when(j == 0)
        def _():
            gt_acc[...] = jnp.zeros_like(gt_acc)
            dsink_ref[...] = jnp.zeros_like(dsink_ref)

        lane_lo = _lane_lo()

        def attend(with_prev):
            first_col = 0 if with_prev else B_PREV
            k_lo, k_hi, v_lo, v_hi = _b_operands(kp, kc, vp, vc, kvh, with_prev)
            dk_blk = jnp.zeros((keys - first_col, 128), F32)
            dv_blk = jnp.zeros((keys - first_col, 128), F32)
            for pp in range(B_GROUP // 2):
                cols = slice(128 * pp, 128 * (pp + 1))
                qp = q_ref[:, cols]
                o = o_ref[:, cols].astype(F32)
                lse_pair = lse_ref[:, cols]
                dzf = dz_ref[:, cols].astype(F32)
                silu, dsilu = _silu_parts(g_ref[:, cols].astype(F32))
                do = dzf * silu
                dqg_ref[1, :, cols] = (dzf * o * dsilu).astype(BF16)
                doo = do * o
                dqs = []
                for hh in range(2):
                    g = 2 * pp + hh
                    sel = lane_lo if hh == 0 else jnp.logical_not(lane_lo)
                    sink = sink_ref[kvh * B_GROUP + g]
                    kk = k_lo if hh == 0 else k_hi
                    vv = v_lo if hh == 0 else v_hi
                    qm = jnp.where(sel, qp, jnp.zeros_like(qp)) * SCALE
                    s = _dot_nt(qm, kk) + b_ref[g, :, first_col:]
                    lse_h = _own_everywhere(lse_pair, sel)
                    p = jnp.exp(_minus_rows(s, lse_h))
                    delta = jnp.sum(jnp.where(sel, doo, 0.0), axis=-1, keepdims=True)
                    dom = jnp.where(sel, do, 0.0).astype(BF16)
                    dp = _dot_nt(dom, vv)
                    ds = p * (dp - delta)
                    gt_acc[g, :, first_col:] += ds
                    dsink_ref[g:g + 1, :] -= jnp.sum(jnp.exp(sink - lse_h) * delta, axis=0, keepdims=True)
                    dsb = ds.astype(BF16)
                    dqs.append(_dot(dsb, kk) * SCALE)
                    dk_blk = dk_blk + _dot_tn(dsb, qm)
                    dv_blk = dv_blk + _dot_tn(p.astype(BF16), dom)
                dqg_ref[0, :, cols] = jnp.where(lane_lo, dqs[0], dqs[1]).astype(BF16)
            mine = lane_lo == (kvh == 0)
            dk_add = jnp.where(mine, dk_blk + pltpu.roll(dk_blk, HEAD_DIM, 1), 0.0)
            dv_add = jnp.where(mine, dv_blk + pltpu.roll(dv_blk, HEAD_DIM, 1), 0.0)
            first_key = B_PREV if with_prev else 0
            if with_prev:
                rows = pl.ds(pl.multiple_of(j * qblk - B_PREV, B_PREV), B_PREV)
                dkv_ref[0, rows, :] += dk_add[0:B_PREV]
                dkv_ref[1, rows, :] += dv_add[0:B_PREV]
            rows = pl.ds(pl.multiple_of(j * qblk, qblk), qblk)
            dkv_ref[0, rows, :] += dk_add[first_key:first_key + qblk]
            dkv_ref[1, rows, :] += dv_add[first_key:first_key + qblk]

        pl.when(j == 0)(functools.partial(attend, False))
        pl.when(j >= 1)(functools.partial(attend, True))

        @pl.when(j == nq - 1)
        def _():
            dd = jnp.concatenate([_offset_sums(_collapse_chunks(gt_acc[g], keys)) for g in range(B_GROUP)], axis=0)
            hi = dd.astype(BF16)
            lo = (dd - hi.astype(F32)).astype(BF16)
            dt5_ref[...] = _dot(hi, oh_ref[...]) + _dot(lo, oh_ref[...])

    blk = pl.BlockSpec((qblk, 512), lambda h, j: (j, h))
    return pl.pallas_call(
        body, name="attn_b_bwd", grid=(B_KV_HEADS, nq),
        in_specs=[q_spec, g_spec, kp_spec, kc_spec, vp_spec, vc_spec, bias_spec, sink_spec, blk, blk, blk,
                  pl.BlockSpec((keys, 128), lambda h, j: (0, 0))],
        out_specs=[pl.BlockSpec((2, qblk, 512), lambda h, j: (0, j, h)),
                   pl.BlockSpec((2, t, 128), lambda h, j: (0, 0, 0)),
                   pl.BlockSpec((B_GROUP, 128), lambda h, j: (h, 0)),
                   pl.BlockSpec((B_GROUP, 128), lambda h, j: (h, 0))],
        out_shape=[SDS((2, t, D_MODEL), BF16), SDS((2, t, 128), F32),
                   SDS((N_HEADS, 128), F32), SDS((N_HEADS, 128), F32)],
        scratch_shapes=[pltpu.VMEM((B_GROUP, qblk, keys), F32), pltpu.VMEM((B_GROUP, qblk, keys), F32)],
        compiler_params=_cparams(),
    )(qg, qg, kv, kv, kv, kv, bias, sinks, out_b, lse, dz, bucket_onehot)


def _a_bias_by_offset(rel_bias):
    m = np.arange(A_DIAG)
    idx = np.clip(A_BAND - 1 - m, -A_REL_CLIP, A_REL_CLIP) + A_REL_CLIP
    by_head = rel_bias[idx].T.reshape(N_HEADS // 2, 2, A_DIAG)
    return jnp.concatenate([by_head, jnp.zeros((N_HEADS // 2, 6, A_DIAG), F32)], axis=1)


def _a_bias_grad(offset_sums):
    first = 319
    tail = jnp.sum(offset_sums[:, :first], axis=1)
    body = jnp.flip(offset_sums[:, first:first + 320], axis=1)
    body = body.at[:, -1].add(tail)
    full = jnp.concatenate([jnp.zeros((N_HEADS, 193), F32), body], axis=1)
    return full


def _t5_bucket(rel):
    nb = T5_BUCKETS // 2
    max_exact = nb // 2
    ret = jnp.where(rel > 0, nb, 0)
    n = jnp.abs(rel)
    nf = jnp.maximum(n, 1).astype(jnp.float32)
    large = max_exact + (jnp.log(nf / max_exact) / math.log(T5_MAX_DIST / max_exact)
                         * (nb - max_exact)).astype(jnp.int32)
    large = jnp.minimum(large, nb - 1)
    return ret + jnp.where(n < max_exact, n, large)


def _b_offset_buckets(keys):
    return _t5_bucket(jnp.arange(keys, dtype=jnp.int32) - (B_LEFT_CHUNKS * CHUNK + CHUNK - 1))


def _b_bias_by_offset(t5_table, keys):
    return t5_table[_b_offset_buckets(keys)].T


def _b_bucket_onehot(keys):
    return (_b_offset_buckets(keys)[:, None] == jnp.arange(128)[None, :]).astype(BF16)


def _local_step(my_slot, order, x, target, a_gain_shard, w_in_a_shard, rel_bias, late_shards, kv_gain,
                t5_table, b_gain, sinks, f_gain):
    a_bias = _a_bias_by_offset(rel_bias)
    b_bias_fwd = _b_bias_by_offset(t5_table, B_QBLK_FWD + B_PREV)
    b_bias_bwd = _b_bias_by_offset(t5_table, B_QBLK_BWD + B_PREV)
    sinks_flat = sinks.reshape(N_HEADS)

    xn, qkvg, w_in_a, a_gain = _norm_matmul_gather(order, x, a_gain_shard, w_in_a_shard)
    z_a, out_a, lse_a, (w_in_b, w_out_a, w_out_b, kv_w) = _attn_a_fwd(qkvg, a_bias, late_shards)
    w_out_a = w_out_a.reshape(D_MODEL, D_MODEL)
    w_out_b = w_out_b.reshape(D_MODEL, D_MODEL)
    kv_w = kv_w.reshape(D_MODEL, 2 * 128)
    h1, kvn, hb, kv, qg = _layer_a_out(x, z_a, w_out_a, kv_gain, b_gain, kv_w, w_in_b)
    z_b, out_b, lse_b = _attn_b_fwd(qg, kv, b_bias_fwd, sinks_flat)
    dh2, dh2b, dz_b, loss, d_fn = _layer_b_out_loss(h1, z_b, w_out_b, f_gain, target)

    dqg_b, dkv_b, d_t5, d_sink = _attn_b_bwd(qg, kv, b_bias_bwd, sinks_flat, out_b, lse_b, dz_b,
                                             _b_bucket_onehot(B_QBLK_BWD + B_PREV))
    dh1, dh1b, dz_a, d_bn, d_kn = _layer_b_in_bwd(dqg_b, dkv_b, w_in_b, kv_w, h1, dh2, b_gain, kv_gain, w_out_a)
    early = dict(
        b_w_out=_weight_grad_rows("grad_b_w_out", my_slot, z_b, dh2b[None]),
        b_w_in=_weight_grad_cols("grad_b_w_in", my_slot, hb, [dqg_b],
                                 [(0, o, c, 4 * o + c) for o in range(2) for c in range(4)], 256),
        kv_w=_weight_grad_rows("grad_kv_w", my_slot, kvn, dkv_b),
        a_w_out=_weight_grad_rows("grad_a_w_out", my_slot, z_a, dh1b[None]))
    dqg_a, dkv_a, d_rel, landed = _attn_a_bwd(qkvg, a_bias, out_a, lse_a, dz_a, [g[0] for g in early.values()])
    g_w_in_a = _weight_grad_cols(
        "grad_a_w_in", my_slot, xn, [dqg_a, dkv_a],
        [(0, 0, 0, 0), (0, 0, 1, 1), (1, 0, 0, 2), (1, 0, 1, 3), (1, 1, 0, 4), (1, 1, 1, 5), (0, 1, 0, 6), (0, 1, 1, 7)], 512)
    from_sibling, = _exchange_sibling([g_w_in_a[0]])
    x_i, y_i, c_i, chips = _place()
    del x_i, y_i
    forward_slots = jnp.stack([_slot(*chip, c_i) for chip in chips]).astype(jnp.int32)
    chip_sums = _pre_reduce("chip_sum_a_w_in", g_w_in_a[0], from_sibling, forward_slots)
    grad_x, d_an, from_chips = _layer_a_in_bwd(dqg_a, dkv_a, w_in_a, x, dh1, a_gain, chip_sums)

    matrices = {n: (g[1], [(land, 0, N_DEV - 1)]) for (n, g), land in zip(early.items(), landed)}
    matrices["a_w_in"] = (g_w_in_a[1], [(from_sibling, 3, 1), (from_chips, 0, 3)])
    small = dict(
        loss=loss, a_norm=d_an, a_rel_bias=d_rel[:, :2].reshape(N_HEADS, A_DIAG),
        kv_norm=d_kn, t5_bias=d_t5, b_norm=d_bn, b_sinks=d_sink, final_norm=d_fn)
    return grad_x, small, matrices


def _place():
    x, y, c = lax.axis_index("x"), lax.axis_index("y"), lax.axis_index("c")
    chips = [(1 - x, y), (x, 1 - y), (1 - x, 1 - y)]
    return x, y, c, chips


def _slot(px, py, pc):
    return 4 * px + 2 * py + pc


ANY = pl.BlockSpec(memory_space=pl.ANY)


def _peer(x, y, c, k):
    return (x ^ (k >> 2), y ^ ((k >> 1) & 1), c ^ (k & 1))


def _scatter_copies(grad_refs, land_refs, send_sems, recv_sems):
    x, y, c, _ = _place()
    copies = []
    for t, (grad, land) in enumerate(zip(grad_refs, land_refs)):
        for k in range(1, N_DEV):
            peer = _peer(x, y, c, k)
            sem = (N_DEV - 1) * t + k - 1
            copies.append(pltpu.make_async_remote_copy(
                src_ref=grad.at[_slot(*peer)], dst_ref=land.at[k - 1],
                send_sem=send_sems.at[sem], recv_sem=recv_sems.at[sem],
                device_id=peer, device_id_type=MESH))
    return copies


def _gather_phases(ins, outs, send_sems, recv_sems, local_sems):
    n = len(ins)
    x, y, c, chips = _place()
    me, sibling = (x, y, c), (x, y, 1 - c)

    def copy(t, k, block, to, src=None):
        dst = outs[t].at[_slot(*block)]
        return pltpu.make_async_remote_copy(
            src_ref=dst if src is None else src, dst_ref=dst,
            send_sem=send_sems.at[7 * t + k], recv_sem=recv_sems.at[7 * t + k],
            device_id=to, device_id_type=MESH)

    def lists():
        mine = [pltpu.make_async_copy(ins[t], outs[t].at[_slot(*me)], local_sems.at[t]) for t in range(n)]
        first = []
        for t in range(n):
            first.append(copy(t, 0, me, sibling, src=ins[t]))
            first += [copy(t, 1 + j, me, (*chip, c), src=ins[t]) for j, chip in enumerate(chips)]
        passed = [copy(t, 4 + j, (*chip, c), sibling) for t in range(n) for j, chip in enumerate(chips)]
        return mine, first, passed

    def start():
        mine, first, _ = lists()
        for cp in mine + first:
            cp.start()

    def forward():
        _, _, passed = lists()
        for t in range(n):
            for j, chip in enumerate(chips):
                copy(t, 1 + j, (*chip, c), me).wait_recv()
                passed[3 * t + j].start()

    def finish():
        mine, first, passed = lists()
        for t in range(n):
            copy(t, 0, sibling, me).wait_recv()
            for j, chip in enumerate(chips):
                copy(t, 4 + j, (*chip, 1 - c), me).wait_recv()
        for cp in first + passed:
            cp.wait_send()
        for cp in mine:
            cp.wait()

    return start, forward, finish


def _gather_scratch(n):
    return [pltpu.SemaphoreType.DMA((7 * n,)), pltpu.SemaphoreType.DMA((7 * n,)), pltpu.SemaphoreType.DMA((n,))]


def _exchange_sibling(grads):
    n = len(grads)

    def body(*refs):
        ins, outs = refs[:n], refs[n:2 * n]
        send_sems, recv_sems = refs[2 * n:]
        x, y, c, chips = _place()
        sibling = (x, y, 1 - c)
        copies = []
        for t in range(n):
            blocks = [(*chip, 1 - c) for chip in chips] + [sibling]
            for k, block in enumerate(blocks):
                copies.append(pltpu.make_async_remote_copy(
                    src_ref=ins[t].at[_slot(*block)], dst_ref=outs[t].at[k],
                    send_sem=send_sems.at[4 * t + k], recv_sem=recv_sems.at[4 * t + k],
                    device_id=sibling, device_id_type=MESH))
        for cp in copies:
            cp.start()
        for cp in copies:
            cp.wait()

    return pl.pallas_call(
        body, name="grads_to_sibling",
        in_specs=[ANY] * n, out_specs=[ANY] * n,
        out_shape=[SDS((4, *g.shape[1:]), g.dtype) for g in grads],
        scratch_shapes=[pltpu.SemaphoreType.DMA((4 * n,)), pltpu.SemaphoreType.DMA((4 * n,))],
    )(*grads)


def _chip_copies(sums_ref, land_ref, send_sems, recv_sems):
    x, y, c, chips = _place()
    del x, y
    return [pltpu.make_async_remote_copy(
        src_ref=sums_ref.at[j], dst_ref=land_ref.at[j], send_sem=send_sems.at[j], recv_sem=recv_sems.at[j],
        device_id=(*chip, c), device_id_type=MESH) for j, chip in enumerate(chips)]


def _row_tile(rows):
    return min(rows, 256)


def _pre_reduce(name, g, from_sibling, slots):
    _, r, c = g.shape
    tr = _row_tile(r)

    def body(slots_ref, g_ref, s_ref, o_ref):
        del slots_ref
        o_ref[...] = (g_ref[...].astype(F32) + s_ref[...].astype(F32)).astype(BF16)

    return pl.pallas_call(
        body, name=name,
        grid_spec=pltpu.PrefetchScalarGridSpec(
            num_scalar_prefetch=1, grid=(3, r // tr),
            in_specs=[pl.BlockSpec((1, tr, c), lambda j, i, s: (s[j], i, 0)),
                      pl.BlockSpec((1, tr, c), lambda j, i, s: (j, i, 0))],
            out_specs=pl.BlockSpec((1, tr, c), lambda j, i, s: (j, i, 0))),
        out_shape=SDS((3, r, c), BF16),
        compiler_params=_cparams(),
    )(slots, g, from_sibling)


def _adamw(w, g, m, v):
    m2 = ADAM_B1 * m + (1.0 - ADAM_B1) * g
    v2 = ADAM_B2 * v + (1.0 - ADAM_B2) * jnp.square(g)
    m_hat = m2 / (1.0 - ADAM_B1 ** ADAM_STEP)
    v_hat = v2 / (1.0 - ADAM_B2 ** ADAM_STEP)
    delta = -ADAM_LR * (m_hat / (jnp.sqrt(v_hat) + ADAM_EPS) + ADAM_WD * w)
    return delta, m2, v2


def _reduce_adamw(name, own, partials, w, m, v):
    r, c = own.shape
    tr = _row_tile(r)
    n_p = len(partials)

    def body(own_ref, *rest):
        p_refs, (w_ref, m_ref, v_ref, grad_ref, d_ref, nm_ref, nv_ref) = rest[:n_p], rest[n_p:]
        grad = own_ref[...]
        for p_ref, (_, _, count) in zip(p_refs, partials):
            for j in range(count):
                grad = grad + p_ref[j].astype(F32)
        grad_ref[...] = grad
        d_ref[...], nm_ref[...], nv_ref[...] = _adamw(w_ref[...], grad, m_ref[...], v_ref[...])

    flat = pl.BlockSpec((tr, c), lambda i: (i, 0))
    return pl.pallas_call(
        body, name=name, grid=(r // tr,),
        in_specs=[flat] + [pl.BlockSpec((count, tr, c), lambda i, first=first, count=count: (first // count, i, 0))
                           for _, first, count in partials] + [flat, flat, flat],
        out_specs=[flat, flat, flat, flat],
        out_shape=[SDS((r, c), F32)] * 4,
        compiler_params=_cparams(),
    )(own, *[p[0] for p in partials], w, m, v)


VM = pl.BlockSpec()


def _small_allreduce(parts):
    n = len(parts)

    def body(*refs):
        ins, outs, lands = refs[:n], refs[n:2 * n], refs[2 * n:3 * n]
        send_sems, recv_sems = refs[3 * n:]
        x, y, c, _ = _place()
        my_slot = _slot(x, y, c)
        copies = []
        for t in range(n):
            lands[t][my_slot] = ins[t][...]
            for k in range(1, N_DEV):
                sem = (N_DEV - 1) * t + k - 1
                copies.append(pltpu.make_async_remote_copy(
                    src_ref=ins[t], dst_ref=lands[t].at[my_slot],
                    send_sem=send_sems.at[sem], recv_sem=recv_sems.at[sem],
                    device_id=_peer(x, y, c, k), device_id_type=MESH))
        for cp in copies:
            cp.start()
        for t in range(n):
            for k in range(1, N_DEV):
                sem = (N_DEV - 1) * t + k - 1
                pltpu.make_async_remote_copy(
                    src_ref=ins[t], dst_ref=lands[t].at[_slot(*_peer(x, y, c, k))],
                    send_sem=send_sems.at[sem], recv_sem=recv_sems.at[sem],
                    device_id=(x, y, c), device_id_type=MESH).wait_recv()
        for cp in copies:
            cp.wait_send()
        for t in range(n):
            total = lands[t][0]
            for s in range(1, N_DEV):
                total = total + lands[t][s]
            outs[t][...] = total

    n_sems = (N_DEV - 1) * n
    return pl.pallas_call(
        body, name="small_allreduce",
        in_specs=[VM] * n, out_specs=[VM] * n, out_shape=[SDS(p.shape, F32) for p in parts],
        scratch_shapes=[pltpu.VMEM((N_DEV, *p.shape), F32) for p in parts]
        + [pltpu.SemaphoreType.DMA((n_sems,)), pltpu.SemaphoreType.DMA((n_sems,))],
    )(*parts)


def _small_adamw(my_slot, sums, ws, ms, vs):
    n = len(ws)

    def body(slot_ref, *refs):
        sum_refs, refs = refs[:n + 1], refs[n + 1:]
        w_refs, m_refs, v_refs, refs = refs[:n], refs[n:2 * n], refs[2 * n:3 * n], refs[3 * n:]
        g_refs, d_refs, nm_refs, nv_refs = refs[:n + 1], refs[n + 1:2 * n + 1], refs[2 * n + 1:3 * n + 1], refs[3 * n + 1:]
        for t in range(n + 1):
            if t == 0:
                g = sum_refs[0][:, pl.ds(pl.multiple_of(slot_ref[0] * 128, 128), 128)]
            else:
                g = sum_refs[t][...]
            g_refs[t][...] = g
            if t < n:
                d_refs[t][...], nm_refs[t][...], nv_refs[t][...] = _adamw(w_refs[t][...], g, m_refs[t][...], v_refs[t][...])

    shapes = [SDS(w.shape, F32) for w in ws]
    outs = pl.pallas_call(
        body, name="small_adamw",
        in_specs=[pl.BlockSpec(memory_space=pltpu.SMEM)] + [VM] * (4 * n + 1),
        out_specs=[VM] * (4 * n + 1),
        out_shape=shapes + [SDS(sums[-1].shape, F32)] + shapes * 3,
    )(my_slot, *sums, *ws, *ms, *vs)
    return outs[:n + 1], outs[n + 1:2 * n + 1], outs[2 * n + 1:3 * n + 1], outs[3 * n + 1:]


def kernel(x, a_norm, a_w_in, a_rel_bias, a_w_out, kv_norm, kv_w, t5_bias, b_norm, b_w_in, b_sinks, b_w_out, final_norm, loss_target, m_a_norm, m_a_w_in, m_a_rel_bias, m_a_w_out, m_kv_norm, m_kv_w, m_t5_bias, m_b_norm, m_b_w_in, m_b_sinks, m_b_w_out, m_final_norm, v_a_norm, v_a_w_in, v_a_rel_bias, v_a_w_out, v_kv_norm, v_kv_w, v_t5_bias, v_b_norm, v_b_w_in, v_b_sinks, v_b_w_out, v_final_norm):
    xi, yi, ci = lax.axis_index("x"), lax.axis_index("y"), lax.axis_index("c")
    my_slot = _slot(xi, yi, ci)

    slot_arr = jnp.reshape(my_slot, (1,)).astype(jnp.int32)
    order = _gather_order(xi, yi, ci)
    late_shards = [b_w_in[0].astype(BF16), a_w_out[0].astype(BF16), b_w_out[0].astype(BF16), kv_w.astype(BF16)]
    grad_x, loc, matrices = _local_step(
        slot_arr, order, x[0], loss_target[0], a_norm, a_w_in[0].astype(BF16), a_rel_bias[0], late_shards,
        kv_norm.reshape(1, D_MODEL), t5_bias, b_norm, b_sinks, final_norm.reshape(1, D_MODEL))

    shard_w = dict(a_w_in=a_w_in[0], b_w_in=b_w_in[0], a_w_out=a_w_out[0], b_w_out=b_w_out[0], kv_w=kv_w)
    shard_m = dict(a_w_in=m_a_w_in[0], b_w_in=m_b_w_in[0], a_w_out=m_a_w_out[0], b_w_out=m_b_w_out[0], kv_w=m_kv_w)
    shard_v = dict(a_w_in=v_a_w_in[0], b_w_in=v_b_w_in[0], a_w_out=v_a_w_out[0], b_w_out=v_b_w_out[0], kv_w=v_kv_w)
    big = {n: _reduce_adamw("adamw_" + n, own, partials, shard_w[n], shard_m[n], shard_v[n])
           for n, (own, partials) in matrices.items()}

    names = ("a_norm", "a_rel_bias", "kv_norm", "t5_bias", "b_norm", "b_sinks", "final_norm")
    tables = ("a_rel_bias", "t5_bias")

    def row(n, a):
        return a.reshape(-1, a.shape[-1]).T if n in tables else a.reshape(1, -1)

    small_w = [row(n, a) for n, a in zip(names, (a_norm, a_rel_bias, kv_norm, t5_bias, b_norm, b_sinks, final_norm))]
    small_m = [row(n, a) for n, a in zip(names, (m_a_norm, m_a_rel_bias, m_kv_norm, m_t5_bias, m_b_norm, m_b_sinks,
                                                 m_final_norm))]
    small_v = [row(n, a) for n, a in zip(names, (v_a_norm, v_a_rel_bias, v_kv_norm, v_t5_bias, v_b_norm, v_b_sinks,
                                                 v_final_norm))]
    sums = dict(zip(names + ("loss",), _small_allreduce([loc[n] for n in names] + [loc["loss"]])))
    sums["a_rel_bias"] = _a_bias_grad(sums["a_rel_bias"])
    sums["t5_bias"] = sums["t5_bias"][:, :T5_BUCKETS]
    sums["b_sinks"] = sums["b_sinks"][:, 0].reshape(1, N_HEADS)
    results = _small_adamw(slot_arr, [sums[n] for n in names + ("loss",)], small_w, small_m, small_v)
    like = dict(a_norm=a_norm, a_rel_bias=a_rel_bias, kv_norm=kv_norm, t5_bias=t5_bias, b_norm=b_norm,
                b_sinks=b_sinks, final_norm=final_norm)
    sm = [{n: (part[i].T if n in tables else part[i]).reshape(like[n].shape) for i, n in enumerate(names)}
          for part in results]
    loss = results[0][len(names)][0, 0]

    order = ("a_norm", "a_w_in", "a_rel_bias", "a_w_out", "kv_norm", "kv_w", "t5_bias", "b_norm",
             "b_w_in", "b_sinks", "b_w_out", "final_norm")
    lead = dict(a_w_in=True, b_w_in=True, a_w_out=True, b_w_out=True, kv_w=False)

    def pick(kind, name):
        if name in big:
            val = big[name][kind]
            return val[None] if lead[name] else val
        return sm[kind][name]

    outs = [loss, grad_x[None]]
    for kind in range(4):
        outs += [pick(kind, n) for n in order]
    return tuple(outs)
```

```python
import functools
import math

import numpy as np
import jax
import jax.numpy as jnp
from jax import lax
from jax.experimental import pallas as pl
from jax.experimental.pallas import tpu as pltpu

F32 = jnp.float32
BF16 = jnp.bfloat16
SDS = jax.ShapeDtypeStruct

D_MODEL = 1024
HEAD_DIM = 64
CHUNK = 64
N_HEADS = 16
RMS_EPS = 1e-6
A_LEFT_CHUNKS = 8
A_BAND = (A_LEFT_CHUNKS + 1) * CHUNK
A_REL_CLIP = 256
B_KV_HEADS = 2
B_GROUP = 8
B_LEFT_CHUNKS = 2
B_BAND = (B_LEFT_CHUNKS + 1) * CHUNK
T5_BUCKETS = 32
T5_MAX_DIST = 128
QBLK = 256
A_KEYS = 3 * QBLK
B_QBLK_FWD = 128
B_QBLK_BWD = 256
B_PREV = 128
A_DIAG = A_KEYS
NEG = -1e30
SCALE = HEAD_DIM ** -0.5
N_DEV = 8

ADAM_LR = 0.001
ADAM_B1 = 0.9
ADAM_B2 = 0.999
ADAM_EPS = 1e-08
ADAM_WD = 0.01
ADAM_STEP = 10

VMEM_LIMIT_BYTES = 56 * 1024 * 1024
MESH = pl.DeviceIdType.MESH


def _cparams():
    return pltpu.CompilerParams(vmem_limit_bytes=VMEM_LIMIT_BYTES)


def _dot(a, b):
    return jnp.dot(a, b, preferred_element_type=F32)


def _dot_nt(a, b):
    return lax.dot_general(a, b, (((1,), (1,)), ((), ())), preferred_element_type=F32)


def _dot_tn(a, b):
    return lax.dot_general(a, b, (((0,), (0,)), ((), ())), preferred_element_type=F32)


def _rstd(xf):
    return lax.rsqrt(jnp.mean(xf * xf, axis=-1, keepdims=True) + RMS_EPS)


def _sigmoid(x):
    return 1.0 / (1.0 + jnp.exp(-x))


_GATHER_SEQUENCE = ((0, None), (1, 0), (2, 1), (4, None), (5, None), (3, 2), (6, None))


def _gather_order(x, y, c):
    others = [(1 - x, y), (x, 1 - y), (1 - x, 1 - y)]
    arrivals = [_slot(x, y, 1 - c)] + [_slot(*chip, c) for chip in others] + [_slot(*chip, 1 - c) for chip in others]
    return jnp.stack([_slot(x, y, c)] + [arrivals[a] for a, _ in _GATHER_SEQUENCE]).astype(jnp.int32)


def _norm_matmul_gather(order, x, gain_shard, w_shard):
    t = x.shape[0]
    dw, tn = w_shard.shape
    tm = min(t, 1024)
    n_m = t // tm

    def body(order_ref, x_ref, gs_ref, shard_ref, xn_ref, o_ref, full_ref, gain_ref,
             xn_all, wbuf, gland, send_sems, recv_sems, gsend_sems, grecv_sems, load_sems, own_sem):
        n, m = pl.program_id(0), pl.program_id(1)
        x_i, y_i, c_i, chips = _place()
        me, sibling = (x_i, y_i, c_i), (x_i, y_i, 1 - c_i)

        def send(k, block, to, src=None):
            dst = full_ref.at[_slot(*block)]
            return pltpu.make_async_remote_copy(
                src_ref=dst if src is None else src, dst_ref=dst,
                send_sem=send_sems.at[k], recv_sem=recv_sems.at[k], device_id=to, device_id_type=MESH)

        own = pltpu.make_async_copy(shard_ref, full_ref.at[_slot(*me)], own_sem)
        first = [send(0, me, sibling, src=shard_ref)]
        first += [send(1 + j, me, (*chip, c_i), src=shard_ref) for j, chip in enumerate(chips)]
        forwards = [send(4 + j, (*chip, c_i), sibling) for j, chip in enumerate(chips)]
        arrivals = [send(0, sibling, me)] + [send(1 + j, (*chip, c_i), me) for j, chip in enumerate(chips)]
        arrivals += [send(4 + j, (*chip, 1 - c_i), me) for j, chip in enumerate(chips)]
        gains = [pltpu.make_async_remote_copy(
            src_ref=gs_ref, dst_ref=gland.at[_slot(*me)], send_sem=gsend_sems.at[k - 1],
            recv_sem=grecv_sems.at[k - 1], device_id=_peer(x_i, y_i, c_i, k), device_id_type=MESH)
            for k in range(1, N_DEV)]

        @pl.when(jnp.logical_and(n == 0, m == 0))
        def _():
            own.start()
            for cp in gains + first:
                cp.start()
            pltpu.make_async_copy(shard_ref, wbuf.at[0], load_sems.at[0]).start()
            gland[_slot(*me)] = gs_ref[...]
            for k in range(1, N_DEV):
                pltpu.make_async_remote_copy(
                    src_ref=gs_ref, dst_ref=gland.at[_slot(*_peer(x_i, y_i, c_i, k))],
                    send_sem=gsend_sems.at[k - 1], recv_sem=grecv_sems.at[k - 1],
                    device_id=me, device_id_type=MESH).wait_recv()
            for s in range(N_DEV):
                gain_ref[:, 128 * s:128 * (s + 1)] = gland[s]

        rows = pl.ds(pl.multiple_of(m * tm, tm), tm)

        @pl.when(n == 0)
        def _():
            xf = x_ref[...]
            xn = ((xf * _rstd(xf)) * gain_ref[...]).astype(BF16)
            xn_all[rows, :] = xn
            xn_ref[...] = xn

        @pl.when(m == 0)
        def _():
            pltpu.make_async_copy(full_ref.at[0], wbuf.at[n % 2], load_sems.at[n % 2]).wait()

        o_ref[...] = _dot(xn_all[rows, :], wbuf[n % 2]).astype(BF16)

        for k, (arrival, forward) in enumerate(_GATHER_SEQUENCE):
            @pl.when(jnp.logical_and(n == k, m == n_m - 1))
            def _(k=k, arrival=arrival, forward=forward):
                arrivals[arrival].wait_recv()
                if forward is not None:
                    forwards[forward].start()
                pltpu.make_async_copy(full_ref.at[order_ref[k + 1]], wbuf.at[(k + 1) % 2],
                                      load_sems.at[(k + 1) % 2]).start()

        @pl.when(jnp.logical_and(n == N_DEV - 1, m == n_m - 1))
        def _():
            for cp in gains + first + forwards:
                cp.wait_send()
            own.wait()

    held = lambda n, m, order: (jnp.where(n == 0, m, n_m - 1), 0)
    return pl.pallas_call(
        body, name="norm_matmul_gather",
        grid_spec=pltpu.PrefetchScalarGridSpec(
            num_scalar_prefetch=1, grid=(N_DEV, n_m),
            in_specs=[pl.BlockSpec((tm, D_MODEL), held),
                      pl.BlockSpec((1, 128), lambda n, m, order: (0, 0)), ANY],
            out_specs=[pl.BlockSpec((tm, D_MODEL), held),
                       pl.BlockSpec((tm, tn), lambda n, m, order: (m, order[n])),
                       ANY, pl.BlockSpec((1, D_MODEL), lambda n, m, order: (0, 0))],
            scratch_shapes=[pltpu.VMEM((t, D_MODEL), BF16), pltpu.VMEM((2, dw, tn), BF16),
                            pltpu.VMEM((N_DEV, 1, 128), F32),
                            pltpu.SemaphoreType.DMA((7,)), pltpu.SemaphoreType.DMA((7,)),
                            pltpu.SemaphoreType.DMA((7,)), pltpu.SemaphoreType.DMA((7,)),
                            pltpu.SemaphoreType.DMA((2,)), pltpu.SemaphoreType.DMA]),
        out_shape=[SDS((t, D_MODEL), BF16), SDS((t, N_DEV * tn), BF16), SDS((N_DEV, dw, tn), BF16),
                   SDS((1, D_MODEL), F32)],
        compiler_params=_cparams(),
    )(order, x, gain_shard, w_shard)


def _layer_a_out(x, z, w_out, kv_gain, b_gain, kv_w, w_in_b):
    t = x.shape[0]
    tm = min(t, 512)
    nb, _, tn = w_in_b.shape

    def body(x_ref, z_ref, wo_ref, kvg_ref, bg_ref, kvw_ref, wb_ref,
             h1_ref, kvn_ref, hb_ref, kv_ref, qg_ref):
        h1 = x_ref[...] + _dot(z_ref[...], wo_ref[...])
        h1_ref[...] = h1
        y0 = h1 * _rstd(h1)
        kvn = (y0 * kvg_ref[...]).astype(BF16)
        hb = (y0 * bg_ref[...]).astype(BF16)
        kvn_ref[...] = kvn
        hb_ref[...] = hb
        kv_ref[...] = _dot(kvn, kvw_ref[...]).astype(BF16)
        for i in range(nb):
            qg_ref[:, i * tn:(i + 1) * tn] = _dot(hb, wb_ref[i]).astype(BF16)

    row = lambda m: (m, 0)
    fix2 = lambda m: (0, 0)
    return pl.pallas_call(
        body, name="layer_a_out", grid=(t // tm,),
        in_specs=[pl.BlockSpec((tm, D_MODEL), row), pl.BlockSpec((tm, D_MODEL), row),
                  pl.BlockSpec((D_MODEL, D_MODEL), fix2),
                  pl.BlockSpec((1, D_MODEL), fix2), pl.BlockSpec((1, D_MODEL), fix2),
                  pl.BlockSpec((D_MODEL, 256), fix2),
                  pl.BlockSpec((nb, D_MODEL, tn), lambda m: (0, 0, 0))],
        out_specs=[pl.BlockSpec((tm, D_MODEL), row), pl.BlockSpec((tm, D_MODEL), row),
                   pl.BlockSpec((tm, D_MODEL), row), pl.BlockSpec((tm, 256), row),
                   pl.BlockSpec((tm, nb * tn), row)],
        out_shape=[SDS((t, D_MODEL), F32), SDS((t, D_MODEL), BF16), SDS((t, D_MODEL), BF16),
                   SDS((t, 256), BF16), SDS((t, nb * tn), BF16)],
        compiler_params=_cparams(),
    )(x, z, w_out, kv_gain, b_gain, kv_w, w_in_b)


def _layer_b_out_loss(h1, z, w_out, f_gain, target):
    t = h1.shape[0]
    tm = min(t, 512)

    def body(h1_ref, z_ref, wo_ref, fg_ref, tgt_ref,
             dh2_ref, dh2b_ref, dz_ref, loss_ref, dfn_ref):
        @pl.when(pl.program_id(0) == 0)
        def _():
            loss_ref[...] = jnp.zeros_like(loss_ref)
            dfn_ref[...] = jnp.zeros_like(dfn_ref)

        h2 = h1_ref[...] + _dot(z_ref[...], wo_ref[...])
        r = _rstd(h2)
        yn = h2 * r
        fg = fg_ref[...]
        err = yn * fg - tgt_ref[...]
        loss_ref[...] += (0.5 / D_MODEL) * jnp.sum(err * err)
        dy = err * (1.0 / D_MODEL)
        dfn_ref[...] += jnp.sum(dy * yn, axis=0, keepdims=True)
        u = dy * fg
        dh2 = r * u - h2 * ((r * r * r) * jnp.mean(u * h2, axis=-1, keepdims=True))
        dh2_ref[...] = dh2
        dh2b = dh2.astype(BF16)
        dh2b_ref[...] = dh2b
        dz_ref[...] = _dot_nt(dh2b, wo_ref[...]).astype(BF16)

    row = lambda m: (m, 0)
    fix2 = lambda m: (0, 0)
    return pl.pallas_call(
        body, name="layer_b_out_loss", grid=(t // tm,),
        in_specs=[pl.BlockSpec((tm, D_MODEL), row), pl.BlockSpec((tm, D_MODEL), row),
                  pl.BlockSpec((D_MODEL, D_MODEL), fix2), pl.BlockSpec((1, D_MODEL), fix2),
                  pl.BlockSpec((tm, D_MODEL), row)],
        out_specs=[pl.BlockSpec((tm, D_MODEL), row), pl.BlockSpec((tm, D_MODEL), row),
                   pl.BlockSpec((tm, D_MODEL), row), pl.BlockSpec((1, 128), fix2),
                   pl.BlockSpec((1, D_MODEL), fix2)],
        out_shape=[SDS((t, D_MODEL), F32), SDS((t, D_MODEL), BF16), SDS((t, D_MODEL), BF16),
                   SDS((1, 128), F32), SDS((1, D_MODEL), F32)],
        compiler_params=_cparams(),
    )(h1, z, w_out, f_gain, target)


def _layer_b_in_bwd(dqg, dkv, w_in_b, kv_w, h1, dh2, b_gain, kv_gain, w_out_a):
    t = h1.shape[0]
    tm = min(t, 512)
    nb, _, tn = w_in_b.shape
    per = D_MODEL // tn

    def body(dqg_ref, dkv_ref, wb_ref, kvw_ref, h1_ref, dh2_ref, bg_ref, kvg_ref, wo_ref,
             dh1_ref, dh1b_ref, dz_ref, dbn_ref, dkn_ref):
        @pl.when(pl.program_id(0) == 0)
        def _():
            dbn_ref[...] = jnp.zeros_like(dbn_ref)
            dkn_ref[...] = jnp.zeros_like(dkn_ref)

        dhb = jnp.zeros((tm, D_MODEL), F32)
        for i in range(nb):
            blk = dqg_ref[i // per, :, (i % per) * tn:(i % per + 1) * tn]
            dhb = dhb + _dot_nt(blk, wb_ref[i])
        dkn = (_dot_nt(dkv_ref[0].astype(BF16), kvw_ref[:, 0:128])
               + _dot_nt(dkv_ref[1].astype(BF16), kvw_ref[:, 128:256]))
        h1 = h1_ref[...]
        r = _rstd(h1)
        xr = h1 * r
        dbn_ref[...] += jnp.sum(dhb * xr, axis=0, keepdims=True)
        dkn_ref[...] += jnp.sum(dkn * xr, axis=0, keepdims=True)
        u = dhb * bg_ref[...] + dkn * kvg_ref[...]
        dh1 = dh2_ref[...] + r * u - h1 * ((r * r * r) * jnp.mean(u * h1, axis=-1, keepdims=True))
        dh1_ref[...] = dh1
        dh1b = dh1.astype(BF16)
        dh1b_ref[...] = dh1b
        dz_ref[...] = _dot_nt(dh1b, wo_ref[...]).astype(BF16)

    row = lambda m: (m, 0)
    fix2 = lambda m: (0, 0)
    return pl.pallas_call(
        body, name="layer_b_in_bwd", grid=(t // tm,),
        in_specs=[pl.BlockSpec((2, tm, D_MODEL), lambda m: (0, m, 0)),
                  pl.BlockSpec((2, tm, 128), lambda m: (0, m, 0)),
                  pl.BlockSpec((nb, D_MODEL, tn), lambda m: (0, 0, 0)),
                  pl.BlockSpec((D_MODEL, 256), fix2),
                  pl.BlockSpec((tm, D_MODEL), row), pl.BlockSpec((tm, D_MODEL), row),
                  pl.BlockSpec((1, D_MODEL), fix2), pl.BlockSpec((1, D_MODEL), fix2),
                  pl.BlockSpec((D_MODEL, D_MODEL), fix2)],
        out_specs=[pl.BlockSpec((tm, D_MODEL), row), pl.BlockSpec((tm, D_MODEL), row),
                   pl.BlockSpec((tm, D_MODEL), row), pl.BlockSpec((1, D_MODEL), fix2),
                   pl.BlockSpec((1, D_MODEL), fix2)],
        out_shape=[SDS((t, D_MODEL), F32), SDS((t, D_MODEL), BF16), SDS((t, D_MODEL), BF16),
                   SDS((1, D_MODEL), F32), SDS((1, D_MODEL), F32)],
        compiler_params=_cparams(),
    )(dqg, dkv, w_in_b, kv_w, h1, dh2, b_gain, kv_gain, w_out_a)


def _layer_a_in_bwd(dqg, dkv, w_in_a, x, dh1, a_gain, chip_sums):
    t = x.shape[0]
    tm = min(t, 512)
    nb, _, tn = w_in_a.shape
    per = D_MODEL // tn

    def body(dqg_ref, dkv_ref, w_ref, x_ref, dh1_ref, ag_ref, sums_ref, dx_ref, dan_ref, land_ref,
             send_sems, recv_sems):
        @pl.when(pl.program_id(0) == 0)
        def _():
            dan_ref[...] = jnp.zeros_like(dan_ref)
            for cp in _chip_copies(sums_ref, land_ref, send_sems, recv_sems):
                cp.start()

        dxn = jnp.zeros((tm, D_MODEL), F32)
        for i in range(nb):
            part = i // per
            src = dqg_ref if part in (0, 3) else dkv_ref
            outer = {0: 0, 3: 1, 1: 0, 2: 1}[part]
            blk = src[outer, :, (i % per) * tn:(i % per + 1) * tn]
            dxn = dxn + _dot_nt(blk, w_ref[i])
        xf = x_ref[...]
        r = _rstd(xf)
        dan_ref[...] += jnp.sum(dxn * (xf * r), axis=0, keepdims=True)
        u = dxn * ag_ref[...]
        dx_ref[...] = dh1_ref[...] + r * u - xf * ((r * r * r) * jnp.mean(u * xf, axis=-1, keepdims=True))

        @pl.when(pl.program_id(0) == t // tm - 1)
        def _():
            for cp in _chip_copies(sums_ref, land_ref, send_sems, recv_sems):
                cp.wait()

    row = lambda m: (m, 0)
    fix2 = lambda m: (0, 0)
    return pl.pallas_call(
        body, name="layer_a_in_bwd", grid=(t // tm,),
        in_specs=[pl.BlockSpec((2, tm, D_MODEL), lambda m: (0, m, 0)),
                  pl.BlockSpec((2, tm, D_MODEL), lambda m: (0, m, 0)),
                  pl.BlockSpec((nb, D_MODEL, tn), lambda m: (0, 0, 0)),
                  pl.BlockSpec((tm, D_MODEL), row), pl.BlockSpec((tm, D_MODEL), row),
                  pl.BlockSpec((1, D_MODEL), fix2), ANY],
        out_specs=[pl.BlockSpec((tm, D_MODEL), row), pl.BlockSpec((1, D_MODEL), fix2), ANY],
        out_shape=[SDS((t, D_MODEL), F32), SDS((1, D_MODEL), F32), SDS(chip_sums.shape, chip_sums.dtype)],
        scratch_shapes=[pltpu.SemaphoreType.DMA((3,)), pltpu.SemaphoreType.DMA((3,))],
        compiler_params=_cparams(),
    )(dqg, dkv, w_in_a, x, dh1, a_gain, chip_sums)


def _lut(s, vals):
    r = jnp.int32(vals[0])
    for i in range(1, len(vals)):
        r = jnp.where(s == i, jnp.int32(vals[i]), r)
    return r


def _held(steps, i):
    seq, cur = [None] * len(steps), None
    for k in range(len(steps) - 1, -1, -1):
        if steps[k][0] == i:
            cur = steps[k][1:3]
        seq[k] = cur
    for k in range(len(steps)):
        cur = seq[k] = seq[k] if seq[k] is not None else cur
    return seq


def _weight_grad_cols(name, my_slot, a, bs, steps, tn):
    t, dw = a.shape
    n_arr = len(bs)
    which = [s[0] for s in steps]
    blks = [s[3] for s in steps]

    def body(slot_ref, a_ref, *rest):
        b_refs, (o_ref, own_ref, at_ref) = rest[:n_arr], rest[n_arr:]
        s = pl.program_id(0)

        @pl.when(s == 0)
        def _():
            at_ref[...] = a_ref[...].T

        for i in range(n_arr):
            @pl.when(_lut(s, which) == i)
            def _(i=i):
                res = _dot(at_ref[...], b_refs[i][0])
                o_ref[0] = res.astype(BF16)

                @pl.when(_lut(s, blks) == slot_ref[0])
                def _():
                    own_ref[...] = res

    def b_spec(i):
        held = _held(steps, i)
        return pl.BlockSpec((1, t, tn), lambda s, slot: (_lut(s, [h[0] for h in held]), 0,
                                                         _lut(s, [h[1] for h in held])))

    return pl.pallas_call(
        body, name=name,
        grid_spec=pltpu.PrefetchScalarGridSpec(
            num_scalar_prefetch=1, grid=(len(steps),),
            in_specs=[pl.BlockSpec((t, dw), lambda s, slot: (0, 0))] + [b_spec(i) for i in range(n_arr)],
            out_specs=[pl.BlockSpec((1, dw, tn), lambda s, slot: (_lut(s, blks), 0, 0)),
                       pl.BlockSpec((dw, tn), lambda s, slot: (0, 0))],
            scratch_shapes=[pltpu.VMEM((dw, t), BF16)]),
        out_shape=[SDS((N_DEV, dw, tn), BF16), SDS((dw, tn), F32)],
        compiler_params=_cparams(),
    )(my_slot, a, *bs)


def _weight_grad_rows(name, my_slot, a, b):
    t, dw = a.shape
    n_o, _, c = b.shape
    rows = dw // N_DEV
    tn = min(c, 256)
    per = c // tn

    def body(slot_ref, a_ref, b_ref, o_ref, own_ref, at_ref, res_ref):
        @pl.when(pl.program_id(0) == 0)
        def _():
            at_ref[...] = a_ref[...].T

        res_ref[...] = _dot(at_ref[...], b_ref[0].astype(BF16))
        o_ref[...] = res_ref[...].astype(BF16)
        own_ref[...] = res_ref[pl.ds(pl.multiple_of(slot_ref[0] * rows, rows), rows), :]

    all_rows, own = pl.pallas_call(
        body, name=name,
        grid_spec=pltpu.PrefetchScalarGridSpec(
            num_scalar_prefetch=1, grid=(n_o * per,),
            in_specs=[pl.BlockSpec((t, dw), lambda s, slot: (0, 0)),
                      pl.BlockSpec((1, t, tn), lambda s, slot: (s // per, 0, s % per))],
            out_specs=[pl.BlockSpec((dw, tn), lambda s, slot: (0, s)),
                       pl.BlockSpec((rows, tn), lambda s, slot: (0, s))],
            scratch_shapes=[pltpu.VMEM((dw, t), BF16), pltpu.VMEM((dw, tn), F32)]),
        out_shape=[SDS((dw, n_o * c), BF16), SDS((rows, n_o * c), F32)],
        compiler_params=_cparams(),
    )(my_slot, a, b)
    return all_rows.reshape(N_DEV, rows, n_o * c), own


def _lane_lo():
    return lax.broadcasted_iota(jnp.int32, (1, 128), 1) < HEAD_DIM


def _collapse_chunks(ds, keys):
    if ds.shape[1] < keys:
        ds = jnp.concatenate([jnp.zeros((ds.shape[0], keys - ds.shape[1]), F32), ds], axis=1)
    gc = ds[0:CHUNK]
    for cc in range(1, ds.shape[0] // CHUNK):
        gc = gc + pltpu.roll(ds[cc * CHUNK:(cc + 1) * CHUNK], keys - cc * CHUNK, 1)
    return gc


def _offset_sums(gc):
    hi = gc.astype(BF16)
    lo = (gc - hi.astype(F32)).astype(BF16)
    flip = (lax.broadcasted_iota(jnp.int32, (CHUNK, CHUNK), 0)
            + lax.broadcasted_iota(jnp.int32, (CHUNK, CHUNK), 1) == CHUNK - 1).astype(BF16)
    gf = _dot(flip, hi) + _dot(flip, lo)
    skew = pltpu.roll(gf, 0, 1, stride=1, stride_axis=0)
    return jnp.sum(skew, axis=0, keepdims=True)


def _band_bias(w_row, band, rows):
    keys = w_row.shape[1]
    base = jnp.broadcast_to(w_row, (CHUNK, keys))
    skew = pltpu.roll(base, 0, 1, stride=1, stride_axis=0)
    skew = pltpu.roll(skew, keys - (CHUNK - 1), 1)
    col = lax.broadcasted_iota(jnp.int32, (CHUNK, keys), 1)
    chunk0 = jnp.where(col < band, skew, NEG)
    return jnp.concatenate(
        [chunk0] + [pltpu.roll(chunk0, cc * CHUNK, 1) for cc in range(1, rows // CHUNK)], axis=0)


def _silu_parts(g):
    sg = _sigmoid(g)
    return g * sg, sg * (1.0 + g * (1.0 - sg))


A_PAIRS = 2
A_LANES = 128 * A_PAIRS
A_STEPS = D_MODEL // A_LANES


def _a_specs():
    q = pl.BlockSpec((QBLK, A_LANES), lambda p, j: (j, p))
    ks = [pl.BlockSpec((QBLK, A_LANES), lambda p, j, b=b: (jnp.maximum(j - 2 + b, 0), A_STEPS + p)) for b in range(3)]
    vs = [pl.BlockSpec((QBLK, A_LANES), lambda p, j, b=b: (jnp.maximum(j - 2 + b, 0), 2 * A_STEPS + p))
          for b in range(3)]
    g = pl.BlockSpec((QBLK, A_LANES), lambda p, j: (j, 3 * A_STEPS + p))
    bias = pl.BlockSpec((A_PAIRS, 8, A_KEYS), lambda p, j: (p, 0, 0))
    return q, ks, vs, g, bias


def _a_fill_bias(w_ref, b_ref, j):
    _fill_bias(2 * A_PAIRS, lambda h: w_ref[h // 2, h % 2:h % 2 + 1, :], A_BAND, b_ref, j)


def _by_valid_key_blocks(j, fn):
    pl.when(j == 0)(functools.partial(fn, 1))
    pl.when(j == 1)(functools.partial(fn, 2))
    pl.when(j >= 2)(functools.partial(fn, 3))


def _fill_bias(n, get_row, band, bias_scr, j):
    @pl.when(j == 0)
    def _():
        for h in range(n):
            bias_scr[h] = _band_bias(get_row(h), band, bias_scr.shape[1])


def _row_sums_everywhere(r, sel):
    return jnp.where(sel, pltpu.roll(r, HEAD_DIM, 1), r)


def _own_everywhere(x, sel):
    return jnp.where(sel, x, pltpu.roll(x, HEAD_DIM, 1))


def _minus_rows(s, row_full):
    return jnp.concatenate([s[:, i:i + 128] - row_full for i in range(0, s.shape[1], 128)], axis=1)


def _attn_a_fwd(qkvg, bias, gather):
    t = qkvg.shape[0]
    nq = t // QBLK
    n_g = len(gather)
    q_spec, k_specs, v_specs, g_spec, bias_spec = _a_specs()

    def body(q_ref, k0, k1, k2, v0, v1, v2, g_ref, w_ref, *rest):
        shard_refs, rest = rest[:n_g], rest[n_g:]
        z_ref, o_ref, lse_ref = rest[:3]
        full_refs, (b_ref, *comm) = rest[3:3 + n_g], rest[3 + n_g:]
        p = pl.program_id(0)
        j = pl.program_id(1)
        start, forward, finish = _gather_phases(shard_refs, full_refs, *comm)
        pl.when(jnp.logical_and(p == 0, j == 0))(start)
        pl.when(jnp.logical_and(p == A_STEPS // 2, j == 0))(forward)
        _a_fill_bias(w_ref, b_ref, j)
        lane_lo = _lane_lo()
        sels = (lane_lo, jnp.logical_not(lane_lo))

        def attend(n_blocks):
            first_col = (3 - n_blocks) * QBLK
            for pp in range(A_PAIRS):
                cols = slice(128 * pp, 128 * (pp + 1))
                k = jnp.concatenate([r[:, cols] for r in (k0, k1, k2)[3 - n_blocks:]], axis=0)
                v = jnp.concatenate([r[:, cols] for r in (v0, v1, v2)[3 - n_blocks:]], axis=0)
                q = q_ref[:, cols]
                qm2 = jnp.concatenate([jnp.where(sel, q, jnp.zeros_like(q)) for sel in sels], axis=0) * SCALE
                s2 = _dot_nt(qm2, k)
                outs, lses = [], []
                for hh, sel in enumerate(sels):
                    s = s2[hh * QBLK:(hh + 1) * QBLK] + b_ref[2 * pp + hh, :, first_col:]
                    mx = jnp.max(s, axis=-1, keepdims=True)
                    e = jnp.exp(s - mx).astype(BF16)
                    r = _dot(e, jnp.where(sel, v, jnp.ones_like(v)))
                    l = _row_sums_everywhere(r, sel)
                    outs.append(r / l)
                    lses.append(mx + jnp.log(l))
                o = jnp.where(lane_lo, outs[0], outs[1])
                silu, _ = _silu_parts(g_ref[:, cols].astype(F32))
                o_ref[:, cols] = o.astype(BF16)
                z_ref[:, cols] = (o * silu).astype(BF16)
                lse_ref[:, cols] = jnp.where(lane_lo, lses[0], lses[1])

        _by_valid_key_blocks(j, attend)
        pl.when(jnp.logical_and(p == A_STEPS - 1, j == nq - 1))(finish)

    out_spec = pl.BlockSpec((QBLK, A_LANES), lambda p, j: (j, p))
    outs = pl.pallas_call(
        body, name="attn_a_fwd", grid=(A_STEPS, nq),
        in_specs=[q_spec, *k_specs, *v_specs, g_spec, bias_spec] + [ANY] * n_g,
        out_specs=[out_spec, out_spec, out_spec] + [ANY] * n_g,
        out_shape=[SDS((t, D_MODEL), BF16), SDS((t, D_MODEL), BF16), SDS((t, D_MODEL), F32)]
        + [SDS((N_DEV, *s.shape), s.dtype) for s in gather],
        scratch_shapes=[pltpu.VMEM((2 * A_PAIRS, QBLK, A_KEYS), F32)] + _gather_scratch(n_g),
        compiler_params=_cparams(),
    )(qkvg, qkvg, qkvg, qkvg, qkvg, qkvg, qkvg, qkvg, bias, *gather)
    return outs[0], outs[1], outs[2], list(outs[3:])


def _attn_a_bwd(qkvg, bias, out_a, lse, dz, scatter):
    t = qkvg.shape[0]
    nq = t // QBLK
    n_sc = len(scatter)
    q_spec, k_specs, v_specs, g_spec, bias_spec = _a_specs()

    def body(q_ref, k0, k1, k2, v0, v1, v2, g_ref, w_ref, o_ref, lse_ref, dz_ref, *rest):
        sc_refs, rest = rest[:n_sc], rest[n_sc:]
        dqg_ref, dkv_ref, dg_ref = rest[:3]
        land_refs, rest = rest[3:3 + n_sc], rest[3 + n_sc:]
        dk_acc, dv_acc, gt_acc, b_ref, send_sems, recv_sems = rest
        j = pl.program_id(1)
        first = jnp.logical_and(pl.program_id(0) == 0, j == 0)
        last = jnp.logical_and(pl.program_id(0) == A_STEPS - 1, j == nq - 1)

        @pl.when(first)
        def _():
            for cp in _scatter_copies(sc_refs, land_refs, send_sems, recv_sems):
                cp.start()

        _a_fill_bias(w_ref, b_ref, j)

        @pl.when(j == 0)
        def _():
            dk_acc[...] = jnp.zeros_like(dk_acc)
            dv_acc[...] = jnp.zeros_like(dv_acc)
            gt_acc[...] = jnp.zeros_like(gt_acc)

        lane_lo = _lane_lo()
        sels = (lane_lo, jnp.logical_not(lane_lo))

        def attend(n_blocks):
            first_col = (3 - n_blocks) * QBLK
            for pp in range(A_PAIRS):
                cols = slice(128 * pp, 128 * (pp + 1))
                q = q_ref[:, cols]
                k = jnp.concatenate([r[:, cols] for r in (k0, k1, k2)[3 - n_blocks:]], axis=0)
                v = jnp.concatenate([r[:, cols] for r in (v0, v1, v2)[3 - n_blocks:]], axis=0)
                o = o_ref[:, cols].astype(F32)
                lse_pair = lse_ref[:, cols]
                dzf = dz_ref[:, cols].astype(F32)
                silu, dsilu = _silu_parts(g_ref[:, cols].astype(F32))
                do = dzf * silu
                dqg_ref[1, :, cols] = (dzf * o * dsilu).astype(BF16)
                doo = do * o
                qm2 = jnp.concatenate([jnp.where(sel, q, jnp.zeros_like(q)) for sel in sels], axis=0) * SCALE
                dom2 = jnp.concatenate([jnp.where(sel, do, 0.0) for sel in sels], axis=0).astype(BF16)
                s2 = _dot_nt(qm2, k)
                dp2 = _dot_nt(dom2, v)
                ps, dss = [], []
                for hh, sel in enumerate(sels):
                    rows = slice(hh * QBLK, (hh + 1) * QBLK)
                    s = s2[rows] + b_ref[2 * pp + hh, :, first_col:]
                    p = jnp.exp(_minus_rows(s, _own_everywhere(lse_pair, sel)))
                    delta = jnp.sum(jnp.where(sel, doo, 0.0), axis=-1, keepdims=True)
                    ds = p * (dp2[rows] - delta)
                    gt_acc[2 * pp + hh] += _collapse_chunks(ds, A_KEYS)
                    ps.append(p.astype(BF16))
                    dss.append(ds.astype(BF16))
                dsb2 = jnp.concatenate(dss, axis=0)
                dq2 = _dot(dsb2, k) * SCALE
                dk_blk = _dot_tn(dsb2, qm2)
                dv_blk = _dot_tn(jnp.concatenate(ps, axis=0), dom2)
                dqg_ref[0, :, cols] = jnp.where(lane_lo, dq2[0:QBLK], dq2[QBLK:2 * QBLK]).astype(BF16)
                for b in range(n_blocks):
                    rows = pl.ds(pl.multiple_of((j - n_blocks + 1 + b) * QBLK, QBLK), QBLK)
                    dk_acc[rows, cols] += dk_blk[b * QBLK:(b + 1) * QBLK]
                    dv_acc[rows, cols] += dv_blk[b * QBLK:(b + 1) * QBLK]

        _by_valid_key_blocks(j, attend)

        @pl.when(j == nq - 1)
        def _():
            dkv_ref[0] = dk_acc[...].astype(BF16)
            dkv_ref[1] = dv_acc[...].astype(BF16)
            for pp in range(A_PAIRS):
                dg_ref[pp] = jnp.concatenate([_offset_sums(gt_acc[2 * pp]), _offset_sums(gt_acc[2 * pp + 1]),
                                              jnp.zeros((6, A_DIAG), F32)], axis=0)

        @pl.when(last)
        def _():
            for cp in _scatter_copies(sc_refs, land_refs, send_sems, recv_sems):
                cp.wait()

    blk = pl.BlockSpec((QBLK, A_LANES), lambda p, j: (j, p))
    outs = pl.pallas_call(
        body, name="attn_a_bwd", grid=(A_STEPS, nq),
        in_specs=[q_spec, *k_specs, *v_specs, g_spec, bias_spec, blk, blk, blk] + [ANY] * n_sc,
        out_specs=[pl.BlockSpec((2, QBLK, A_LANES), lambda p, j: (0, j, p)),
                   pl.BlockSpec((2, t, A_LANES), lambda p, j: (0, 0, p)),
                   pl.BlockSpec((A_PAIRS, 8, A_DIAG), lambda p, j: (p, 0, 0))] + [ANY] * n_sc,
        out_shape=[SDS((2, t, D_MODEL), BF16), SDS((2, t, D_MODEL), BF16), SDS((N_HEADS // 2, 8, A_DIAG), F32)]
        + [SDS((N_DEV - 1, *g.shape[1:]), g.dtype) for g in scatter],
        scratch_shapes=[pltpu.VMEM((t, A_LANES), F32), pltpu.VMEM((t, A_LANES), F32),
                        pltpu.VMEM((2 * A_PAIRS, CHUNK, A_KEYS), F32), pltpu.VMEM((2 * A_PAIRS, QBLK, A_KEYS), F32),
                        pltpu.SemaphoreType.DMA(((N_DEV - 1) * n_sc,)),
                        pltpu.SemaphoreType.DMA(((N_DEV - 1) * n_sc,))],
        compiler_params=_cparams(),
    )(qkvg, qkvg, qkvg, qkvg, qkvg, qkvg, qkvg, qkvg, bias, out_a, lse, dz, *scatter)
    return outs[0], outs[1], outs[2], list(outs[3:])


def _b_specs(qblk):
    per = qblk // B_PREV
    q = pl.BlockSpec((qblk, 512), lambda h, j: (j, h))
    g = pl.BlockSpec((qblk, 512), lambda h, j: (j, 2 + h))
    kp = pl.BlockSpec((B_PREV, 128), lambda h, j: (jnp.maximum(per * j - 1, 0), 0))
    kc = pl.BlockSpec((qblk, 128), lambda h, j: (j, 0))
    vp = pl.BlockSpec((B_PREV, 128), lambda h, j: (jnp.maximum(per * j - 1, 0), 1))
    vc = pl.BlockSpec((qblk, 128), lambda h, j: (j, 1))
    bias = pl.BlockSpec((B_GROUP, qblk + B_PREV), lambda h, j: (h, 0))
    sinks = pl.BlockSpec(memory_space=pltpu.SMEM)
    return q, g, kp, kc, vp, vc, bias, sinks


def _b_operands(kp, kc, vp, vc, kvh, with_prev):
    k = jnp.concatenate([kp[...], kc[...]], axis=0) if with_prev else kc[...]
    v = jnp.concatenate([vp[...], vc[...]], axis=0) if with_prev else vc[...]
    kr = pltpu.roll(k, HEAD_DIM, 1)
    vr = pltpu.roll(v, HEAD_DIM, 1)
    first = kvh == 0
    return (jnp.where(first, k, kr), jnp.where(first, kr, k),
            jnp.where(first, v, vr), jnp.where(first, vr, v))


def _attn_b_fwd(qg, kv, bias, sinks):
    t = qg.shape[0]
    qblk = B_QBLK_FWD
    q_spec, g_spec, kp_spec, kc_spec, vp_spec, vc_spec, bias_spec, sink_spec = _b_specs(qblk)

    def body(q_ref, g_ref, kp, kc, vp, vc, w_ref, sink_ref, z_ref, o_ref, lse_ref, b_ref):
        kvh = pl.program_id(0)
        j = pl.program_id(1)
        _fill_bias(B_GROUP, lambda h: w_ref[h:h + 1, :], B_BAND, b_ref, j)
        lane_lo = _lane_lo()
        n_pairs = B_GROUP // 2

        def attend(with_prev):
            first_col = 0 if with_prev else B_PREV
            k_lo, k_hi, v_lo, v_hi = _b_operands(kp, kc, vp, vc, kvh, with_prev)
            halves = []
            for hh, sel in enumerate((lane_lo, jnp.logical_not(lane_lo))):
                kk = k_lo if hh == 0 else k_hi
                vv = v_lo if hh == 0 else v_hi
                qm4 = jnp.concatenate(
                    [jnp.where(sel, q_ref[:, 128 * pp:128 * (pp + 1)], jnp.zeros((qblk, 128), BF16))
                     for pp in range(n_pairs)], axis=0) * SCALE
                s4 = _dot_nt(qm4, kk)
                es, mxs = [], []
                for pp in range(n_pairs):
                    g = 2 * pp + hh
                    s = s4[pp * qblk:(pp + 1) * qblk] + b_ref[g, :, first_col:]
                    mxs.append(jnp.maximum(jnp.max(s, axis=-1, keepdims=True), sink_ref[kvh * B_GROUP + g]))
                    es.append(jnp.exp(s - mxs[pp]).astype(BF16))
                r4 = _dot(jnp.concatenate(es, axis=0), jnp.where(sel, vv, jnp.ones_like(vv)))
                outs, lses = [], []
                for pp in range(n_pairs):
                    r = r4[pp * qblk:(pp + 1) * qblk]
                    l = _row_sums_everywhere(r, sel) + jnp.exp(sink_ref[kvh * B_GROUP + 2 * pp + hh] - mxs[pp])
                    outs.append(r / l)
                    lses.append(mxs[pp] + jnp.log(l))
                halves.append((outs, lses))
            for pp in range(n_pairs):
                cols = slice(128 * pp, 128 * (pp + 1))
                o = jnp.where(lane_lo, halves[0][0][pp], halves[1][0][pp])
                silu, _ = _silu_parts(g_ref[:, cols].astype(F32))
                o_ref[:, cols] = o.astype(BF16)
                z_ref[:, cols] = (o * silu).astype(BF16)
                lse_ref[:, cols] = jnp.where(lane_lo, halves[0][1][pp], halves[1][1][pp])

        pl.when(j == 0)(functools.partial(attend, False))
        pl.when(j >= 1)(functools.partial(attend, True))

    out_spec = pl.BlockSpec((qblk, 512), lambda h, j: (j, h))
    return pl.pallas_call(
        body, name="attn_b_fwd", grid=(B_KV_HEADS, t // qblk),
        in_specs=[q_spec, g_spec, kp_spec, kc_spec, vp_spec, vc_spec, bias_spec, sink_spec],
        out_specs=[out_spec, out_spec, out_spec],
        out_shape=[SDS((t, D_MODEL), BF16), SDS((t, D_MODEL), BF16), SDS((t, D_MODEL), F32)],
        scratch_shapes=[pltpu.VMEM((B_GROUP, qblk, qblk + B_PREV), F32)],
        compiler_params=_cparams(),
    )(qg, qg, kv, kv, kv, kv, bias, sinks)


def _attn_b_bwd(qg, kv, bias, sinks, out_b, lse, dz, bucket_onehot):
    t = qg.shape[0]
    qblk = B_QBLK_BWD
    keys = qblk + B_PREV
    nq = t // qblk
    q_spec, g_spec, kp_spec, kc_spec, vp_spec, vc_spec, bias_spec, sink_spec = _b_specs(qblk)

    def body(q_ref, g_ref, kp, kc, vp, vc, w_ref, sink_ref, o_ref, lse_ref, dz_ref, oh_ref,
             dqg_ref, dkv_ref, dt5_ref, dsink_ref, gt_acc, b_ref):
        kvh = pl.program_id(0)
        j = pl.program_id(1)
        _fill_bias(B_GROUP, lambda h: w_ref[h:h + 1, :], B_BAND, b_ref, j)

        @pl.when(jnp.logical_and(kvh == 0, j == 0))
        def _():
            dkv_ref[...] = jnp.zeros_like(dkv_ref)

        @pl.when(j == 0)
        def _():
            gt_acc[...] = jnp.zeros_like(gt_acc)
            dsink_ref[...] = jnp.zeros_like(dsink_ref)

        lane_lo = _lane_lo()

        def attend(with_prev):
            first_col = 0 if with_prev else B_PREV
            k_lo, k_hi, v_lo, v_hi = _b_operands(kp, kc, vp, vc, kvh, with_prev)
            dk_blk = jnp.zeros((keys - first_col, 128), F32)
            dv_blk = jnp.zeros((keys - first_col, 128), F32)
            for pp in range(B_GROUP // 2):
                cols = slice(128 * pp, 128 * (pp + 1))
                qp = q_ref[:, cols]
                o = o_ref[:, cols].astype(F32)
                lse_pair = lse_ref[:, cols]
                dzf = dz_ref[:, cols].astype(F32)
                silu, dsilu = _silu_parts(g_ref[:, cols].astype(F32))
                do = dzf * silu
                dqg_ref[1, :, cols] = (dzf * o * dsilu).astype(BF16)
                doo = do * o
                dqs = []
                for hh in range(2):
                    g = 2 * pp + hh
                    sel = lane_lo if hh == 0 else jnp.logical_not(lane_lo)
                    sink = sink_ref[kvh * B_GROUP + g]
                    kk = k_lo if hh == 0 else k_hi
                    vv = v_lo if hh == 0 else v_hi
                    qm = jnp.where(sel, qp, jnp.zeros_like(qp)) * SCALE
                    s = _dot_nt(qm, kk) + b_ref[g, :, first_col:]
                    lse_h = _own_everywhere(lse_pair, sel)
                    p = jnp.exp(_minus_rows(s, lse_h))
                    delta = jnp.sum(jnp.where(sel, doo, 0.0), axis=-1, keepdims=True)
                    dom = jnp.where(sel, do, 0.0).astype(BF16)
                    dp = _dot_nt(dom, vv)
                    ds = p * (dp - delta)
                    gt_acc[g, :, first_col:] += ds
                    dsink_ref[g:g + 1, :] -= jnp.sum(jnp.exp(sink - lse_h) * delta, axis=0, keepdims=True)
                    dsb = ds.astype(BF16)
                    dqs.append(_dot(dsb, kk) * SCALE)
                    dk_blk = dk_blk + _dot_tn(dsb, qm)
                    dv_blk = dv_blk + _dot_tn(p.astype(BF16), dom)
                dqg_ref[0, :, cols] = jnp.where(lane_lo, dqs[0], dqs[1]).astype(BF16)
            mine = lane_lo == (kvh == 0)
            dk_add = jnp.where(mine, dk_blk + pltpu.roll(dk_blk, HEAD_DIM, 1), 0.0)
            dv_add = jnp.where(mine, dv_blk + pltpu.roll(dv_blk, HEAD_DIM, 1), 0.0)
            first_key = B_PREV if with_prev else 0
            if with_prev:
                rows = pl.ds(pl.multiple_of(j * qblk - B_PREV, B_PREV), B_PREV)
                dkv_ref[0, rows, :] += dk_add[0:B_PREV]
                dkv_ref[1, rows, :] += dv_add[0:B_PREV]
            rows = pl.ds(pl.multiple_of(j * qblk, qblk), qblk)
            dkv_ref[0, rows, :] += dk_add[first_key:first_key + qblk]
            dkv_ref[1, rows, :] += dv_add[first_key:first_key + qblk]

        pl.when(j == 0)(functools.partial(attend, False))
        pl.when(j >= 1)(functools.partial(attend, True))

        @pl.when(j == nq - 1)
        def _():
            dd = jnp.concatenate([_offset_sums(_collapse_chunks(gt_acc[g], keys)) for g in range(B_GROUP)], axis=0)
            hi = dd.astype(BF16)
            lo = (dd - hi.astype(F32)).astype(BF16)
            dt5_ref[...] = _dot(hi, oh_ref[...]) + _dot(lo, oh_ref[...])

    blk = pl.BlockSpec((qblk, 512), lambda h, j: (j, h))
    return pl.pallas_call(
        body, name="attn_b_bwd", grid=(B_KV_HEADS, nq),
        in_specs=[q_spec, g_spec, kp_spec, kc_spec, vp_spec, vc_spec, bias_spec, sink_spec, blk, blk, blk,
                  pl.BlockSpec((keys, 128), lambda h, j: (0, 0))],
        out_specs=[pl.BlockSpec((2, qblk, 512), lambda h, j: (0, j, h)),
                   pl.BlockSpec((2, t, 128), lambda h, j: (0, 0, 0)),
                   pl.BlockSpec((B_GROUP, 128), lambda h, j: (h, 0)),
                   pl.BlockSpec((B_GROUP, 128), lambda h, j: (h, 0))],
        out_shape=[SDS((2, t, D_MODEL), BF16), SDS((2, t, 128), F32),
                   SDS((N_HEADS, 128), F32), SDS((N_HEADS, 128), F32)],
        scratch_shapes=[pltpu.VMEM((B_GROUP, qblk, keys), F32), pltpu.VMEM((B_GROUP, qblk, keys), F32)],
        compiler_params=_cparams(),
    )(qg, qg, kv, kv, kv, kv, bias, sinks, out_b, lse, dz, bucket_onehot)


def _a_bias_by_offset(rel_bias):
    m = np.arange(A_DIAG)
    idx = np.clip(A_BAND - 1 - m, -A_REL_CLIP, A_REL_CLIP) + A_REL_CLIP
    by_head = rel_bias[idx].T.reshape(N_HEADS // 2, 2, A_DIAG)
    return jnp.concatenate([by_head, jnp.zeros((N_HEADS // 2, 6, A_DIAG), F32)], axis=1)


def _a_bias_grad(offset_sums):
    first = 319
    tail = jnp.sum(offset_sums[:, :first], axis=1)
    body = jnp.flip(offset_sums[:, first:first + 320], axis=1)
    body = body.at[:, -1].add(tail)
    full = jnp.concatenate([jnp.zeros((N_HEADS, 193), F32), body], axis=1)
    return full


def _t5_bucket(rel):
    nb = T5_BUCKETS // 2
    max_exact = nb // 2
    ret = jnp.where(rel > 0, nb, 0)
    n = jnp.abs(rel)
    nf = jnp.maximum(n, 1).astype(jnp.float32)
    large = max_exact + (jnp.log(nf / max_exact) / math.log(T5_MAX_DIST / max_exact)
                         * (nb - max_exact)).astype(jnp.int32)
    large = jnp.minimum(large, nb - 1)
    return ret + jnp.where(n < max_exact, n, large)


def _b_offset_buckets(keys):
    return _t5_bucket(jnp.arange(keys, dtype=jnp.int32) - (B_LEFT_CHUNKS * CHUNK + CHUNK - 1))


def _b_bias_by_offset(t5_table, keys):
    return t5_table[_b_offset_buckets(keys)].T


def _b_bucket_onehot(keys):
    return (_b_offset_buckets(keys)[:, None] == jnp.arange(128)[None, :]).astype(BF16)


def _local_step(my_slot, order, x, target, a_gain_shard, w_in_a_shard, rel_bias, late_shards, kv_gain,
                t5_table, b_gain, sinks, f_gain):
    a_bias = _a_bias_by_offset(rel_bias)
    b_bias_fwd = _b_bias_by_offset(t5_table, B_QBLK_FWD + B_PREV)
    b_bias_bwd = _b_bias_by_offset(t5_table, B_QBLK_BWD + B_PREV)
    sinks_flat = sinks.reshape(N_HEADS)

    xn, qkvg, w_in_a, a_gain = _norm_matmul_gather(order, x, a_gain_shard, w_in_a_shard)
    z_a, out_a, lse_a, (w_in_b, w_out_a, w_out_b, kv_w) = _attn_a_fwd(qkvg, a_bias, late_shards)
    w_out_a = w_out_a.reshape(D_MODEL, D_MODEL)
    w_out_b = w_out_b.reshape(D_MODEL, D_MODEL)
    kv_w = kv_w.reshape(D_MODEL, 2 * 128)
    h1, kvn, hb, kv, qg = _layer_a_out(x, z_a, w_out_a, kv_gain, b_gain, kv_w, w_in_b)
    z_b, out_b, lse_b = _attn_b_fwd(qg, kv, b_bias_fwd, sinks_flat)
    dh2, dh2b, dz_b, loss, d_fn = _layer_b_out_loss(h1, z_b, w_out_b, f_gain, target)

    dqg_b, dkv_b, d_t5, d_sink = _attn_b_bwd(qg, kv, b_bias_bwd, sinks_flat, out_b, lse_b, dz_b,
                                             _b_bucket_onehot(B_QBLK_BWD + B_PREV))
    dh1, dh1b, dz_a, d_bn, d_kn = _layer_b_in_bwd(dqg_b, dkv_b, w_in_b, kv_w, h1, dh2, b_gain, kv_gain, w_out_a)
    early = dict(
        b_w_out=_weight_grad_rows("grad_b_w_out", my_slot, z_b, dh2b[None]),
        b_w_in=_weight_grad_cols("grad_b_w_in", my_slot, hb, [dqg_b],
                                 [(0, o, c, 4 * o + c) for o in range(2) for c in range(4)], 256),
        kv_w=_weight_grad_rows("grad_kv_w", my_slot, kvn, dkv_b),
        a_w_out=_weight_grad_rows("grad_a_w_out", my_slot, z_a, dh1b[None]))
    dqg_a, dkv_a, d_rel, landed = _attn_a_bwd(qkvg, a_bias, out_a, lse_a, dz_a, [g[0] for g in early.values()])
    g_w_in_a = _weight_grad_cols(
        "grad_a_w_in", my_slot, xn, [dqg_a, dkv_a],
        [(0, 0, 0, 0), (0, 0, 1, 1), (1, 0, 0, 2), (1, 0, 1, 3), (1, 1, 0, 4), (1, 1, 1, 5), (0, 1, 0, 6), (0, 1, 1, 7)], 512)
    chip_sums, from_sibling = _chip_sums(g_w_in_a[0])
    grad_x, d_an, from_chips = _layer_a_in_bwd(dqg_a, dkv_a, w_in_a, x, dh1, a_gain, chip_sums)

    matrices = {n: (g[1], [(land, 0, N_DEV - 1)]) for (n, g), land in zip(early.items(), landed)}
    matrices["a_w_in"] = (g_w_in_a[1], [(from_sibling, 0, 1), (from_chips, 0, 3)])
    small = dict(
        loss=loss, a_norm=d_an, a_rel_bias=d_rel[:, :2].reshape(N_HEADS, A_DIAG),
        kv_norm=d_kn, t5_bias=d_t5, b_norm=d_bn, b_sinks=d_sink, final_norm=d_fn)
    return grad_x, small, matrices


def _place():
    x, y, c = lax.axis_index("x"), lax.axis_index("y"), lax.axis_index("c")
    chips = [(1 - x, y), (x, 1 - y), (1 - x, 1 - y)]
    return x, y, c, chips


def _slot(px, py, pc):
    return 4 * px + 2 * py + pc


ANY = pl.BlockSpec(memory_space=pl.ANY)


def _peer(x, y, c, k):
    return (x ^ (k >> 2), y ^ ((k >> 1) & 1), c ^ (k & 1))


def _scatter_copies(grad_refs, land_refs, send_sems, recv_sems):
    x, y, c, _ = _place()
    copies = []
    for t, (grad, land) in enumerate(zip(grad_refs, land_refs)):
        for k in range(1, N_DEV):
            peer = _peer(x, y, c, k)
            sem = (N_DEV - 1) * t + k - 1
            copies.append(pltpu.make_async_remote_copy(
                src_ref=grad.at[_slot(*peer)], dst_ref=land.at[k - 1],
                send_sem=send_sems.at[sem], recv_sem=recv_sems.at[sem],
                device_id=peer, device_id_type=MESH))
    return copies


def _gather_phases(ins, outs, send_sems, recv_sems, local_sems):
    n = len(ins)
    x, y, c, chips = _place()
    me, sibling = (x, y, c), (x, y, 1 - c)

    def copy(t, k, block, to, src=None):
        dst = outs[t].at[_slot(*block)]
        return pltpu.make_async_remote_copy(
            src_ref=dst if src is None else src, dst_ref=dst,
            send_sem=send_sems.at[7 * t + k], recv_sem=recv_sems.at[7 * t + k],
            device_id=to, device_id_type=MESH)

    def lists():
        mine = [pltpu.make_async_copy(ins[t], outs[t].at[_slot(*me)], local_sems.at[t]) for t in range(n)]
        first = []
        for t in range(n):
            first.append(copy(t, 0, me, sibling, src=ins[t]))
            first += [copy(t, 1 + j, me, (*chip, c), src=ins[t]) for j, chip in enumerate(chips)]
        passed = [copy(t, 4 + j, (*chip, c), sibling) for t in range(n) for j, chip in enumerate(chips)]
        return mine, first, passed

    def start():
        mine, first, _ = lists()
        for cp in mine + first:
            cp.start()

    def forward():
        _, _, passed = lists()
        for t in range(n):
            for j, chip in enumerate(chips):
                copy(t, 1 + j, (*chip, c), me).wait_recv()
                passed[3 * t + j].start()

    def finish():
        mine, first, passed = lists()
        for t in range(n):
            copy(t, 0, sibling, me).wait_recv()
            for j, chip in enumerate(chips):
                copy(t, 4 + j, (*chip, 1 - c), me).wait_recv()
        for cp in first + passed:
            cp.wait_send()
        for cp in mine:
            cp.wait()

    return start, forward, finish


def _gather_scratch(n):
    return [pltpu.SemaphoreType.DMA((7 * n,)), pltpu.SemaphoreType.DMA((7 * n,)), pltpu.SemaphoreType.DMA((n,))]


def _chip_sums(g):
    _, r, c = g.shape

    def body(g_ref, sums_ref, mine_ref, land, own, send_sems, recv_sems, load_sems):
        x, y, c_i, chips = _place()
        sibling = (x, y, 1 - c_i)
        blocks = [(*chip, 1 - c_i) for chip in chips] + [sibling]
        sends = [pltpu.make_async_remote_copy(
            src_ref=g_ref.at[_slot(*block)], dst_ref=land.at[k], send_sem=send_sems.at[k],
            recv_sem=recv_sems.at[k], device_id=sibling, device_id_type=MESH) for k, block in enumerate(blocks)]
        loads = [pltpu.make_async_copy(g_ref.at[_slot(*chip, c_i)], own.at[j], load_sems.at[j])
                 for j, chip in enumerate(chips)]
        for cp in sends + loads:
            cp.start()
        for cp in sends + loads:
            cp.wait()
        for j in range(3):
            sums_ref[j] = (own[j].astype(F32) + land[j].astype(F32)).astype(BF16)
        mine_ref[0] = land[3]

    return pl.pallas_call(
        body, name="chip_sums",
        in_specs=[ANY], out_specs=[VM, VM],
        out_shape=[SDS((3, r, c), BF16), SDS((1, r, c), BF16)],
        scratch_shapes=[pltpu.VMEM((4, r, c), BF16), pltpu.VMEM((3, r, c), BF16),
                        pltpu.SemaphoreType.DMA((4,)), pltpu.SemaphoreType.DMA((4,)), pltpu.SemaphoreType.DMA((3,))],
        compiler_params=_cparams(),
    )(g)


def _chip_copies(sums_ref, land_ref, send_sems, recv_sems):
    x, y, c, chips = _place()
    del x, y
    return [pltpu.make_async_remote_copy(
        src_ref=sums_ref.at[j], dst_ref=land_ref.at[j], send_sem=send_sems.at[j], recv_sem=recv_sems.at[j],
        device_id=(*chip, c), device_id_type=MESH) for j, chip in enumerate(chips)]


def _row_tile(rows):
    return min(rows, 256)


def _adamw(w, g, m, v):
    m2 = ADAM_B1 * m + (1.0 - ADAM_B1) * g
    v2 = ADAM_B2 * v + (1.0 - ADAM_B2) * jnp.square(g)
    m_hat = m2 / (1.0 - ADAM_B1 ** ADAM_STEP)
    v_hat = v2 / (1.0 - ADAM_B2 ** ADAM_STEP)
    delta = -ADAM_LR * (m_hat / (jnp.sqrt(v_hat) + ADAM_EPS) + ADAM_WD * w)
    return delta, m2, v2


def _reduce_adamw(name, own, partials, w, m, v):
    r, c = own.shape
    tr = _row_tile(r)
    n_p = len(partials)

    def body(own_ref, *rest):
        p_refs, (w_ref, m_ref, v_ref, grad_ref, d_ref, nm_ref, nv_ref) = rest[:n_p], rest[n_p:]
        grad = own_ref[...]
        for p_ref, (_, _, count) in zip(p_refs, partials):
            for j in range(count):
                grad = grad + p_ref[j].astype(F32)
        grad_ref[...] = grad
        d_ref[...], nm_ref[...], nv_ref[...] = _adamw(w_ref[...], grad, m_ref[...], v_ref[...])

    flat = pl.BlockSpec((tr, c), lambda i: (i, 0))
    return pl.pallas_call(
        body, name=name, grid=(r // tr,),
        in_specs=[flat] + [pl.BlockSpec((count, tr, c), lambda i, first=first, count=count: (first // count, i, 0))
                           for _, first, count in partials] + [flat, flat, flat],
        out_specs=[flat, flat, flat, flat],
        out_shape=[SDS((r, c), F32)] * 4,
        compiler_params=_cparams(),
    )(own, *[p[0] for p in partials], w, m, v)


VM = pl.BlockSpec()


def _small_allreduce(parts):
    n = len(parts)

    def body(*refs):
        ins, outs, lands = refs[:n], refs[n:2 * n], refs[2 * n:3 * n]
        send_sems, recv_sems = refs[3 * n:]
        x, y, c, _ = _place()
        my_slot = _slot(x, y, c)
        copies = []
        for t in range(n):
            lands[t][my_slot] = ins[t][...]
            for k in range(1, N_DEV):
                sem = (N_DEV - 1) * t + k - 1
                copies.append(pltpu.make_async_remote_copy(
                    src_ref=ins[t], dst_ref=lands[t].at[my_slot],
                    send_sem=send_sems.at[sem], recv_sem=recv_sems.at[sem],
                    device_id=_peer(x, y, c, k), device_id_type=MESH))
        for cp in copies:
            cp.start()
        for t in range(n):
            for k in range(1, N_DEV):
                sem = (N_DEV - 1) * t + k - 1
                pltpu.make_async_remote_copy(
                    src_ref=ins[t], dst_ref=lands[t].at[_slot(*_peer(x, y, c, k))],
                    send_sem=send_sems.at[sem], recv_sem=recv_sems.at[sem],
                    device_id=(x, y, c), device_id_type=MESH).wait_recv()
        for cp in copies:
            cp.wait_send()
        for t in range(n):
            total = lands[t][0]
            for s in range(1, N_DEV):
                total = total + lands[t][s]
            outs[t][...] = total

    n_sems = (N_DEV - 1) * n
    return pl.pallas_call(
        body, name="small_allreduce",
        in_specs=[VM] * n, out_specs=[VM] * n, out_shape=[SDS(p.shape, F32) for p in parts],
        scratch_shapes=[pltpu.VMEM((N_DEV, *p.shape), F32) for p in parts]
        + [pltpu.SemaphoreType.DMA((n_sems,)), pltpu.SemaphoreType.DMA((n_sems,))],
    )(*parts)


def _small_adamw(my_slot, sums, ws, ms, vs):
    n = len(ws)

    def body(slot_ref, *refs):
        sum_refs, refs = refs[:n + 1], refs[n + 1:]
        w_refs, m_refs, v_refs, refs = refs[:n], refs[n:2 * n], refs[2 * n:3 * n], refs[3 * n:]
        g_refs, d_refs, nm_refs, nv_refs = refs[:n + 1], refs[n + 1:2 * n + 1], refs[2 * n + 1:3 * n + 1], refs[3 * n + 1:]
        for t in range(n + 1):
            if t == 0:
                g = sum_refs[0][:, pl.ds(pl.multiple_of(slot_ref[0] * 128, 128), 128)]
            else:
                g = sum_refs[t][...]
            g_refs[t][...] = g
            if t < n:
                d_refs[t][...], nm_refs[t][...], nv_refs[t][...] = _adamw(w_refs[t][...], g, m_refs[t][...], v_refs[t][...])

    shapes = [SDS(w.shape, F32) for w in ws]
    outs = pl.pallas_call(
        body, name="small_adamw",
        in_specs=[pl.BlockSpec(memory_space=pltpu.SMEM)] + [VM] * (4 * n + 1),
        out_specs=[VM] * (4 * n + 1),
        out_shape=shapes + [SDS(sums[-1].shape, F32)] + shapes * 3,
    )(my_slot, *sums, *ws, *ms, *vs)
    return outs[:n + 1], outs[n + 1:2 * n + 1], outs[2 * n + 1:3 * n + 1], outs[3 * n + 1:]


def kernel(x, a_norm, a_w_in, a_rel_bias, a_w_out, kv_norm, kv_w, t5_bias, b_norm, b_w_in, b_sinks, b_w_out, final_norm, loss_target, m_a_norm, m_a_w_in, m_a_rel_bias, m_a_w_out, m_kv_norm, m_kv_w, m_t5_bias, m_b_norm, m_b_w_in, m_b_sinks, m_b_w_out, m_final_norm, v_a_norm, v_a_w_in, v_a_rel_bias, v_a_w_out, v_kv_norm, v_kv_w, v_t5_bias, v_b_norm, v_b_w_in, v_b_sinks, v_b_w_out, v_final_norm):
    xi, yi, ci = lax.axis_index("x"), lax.axis_index("y"), lax.axis_index("c")
    my_slot = _slot(xi, yi, ci)

    slot_arr = jnp.reshape(my_slot, (1,)).astype(jnp.int32)
    order = _gather_order(xi, yi, ci)
    late_shards = [b_w_in[0].astype(BF16), a_w_out[0].astype(BF16), b_w_out[0].astype(BF16), kv_w.astype(BF16)]
    grad_x, loc, matrices = _local_step(
        slot_arr, order, x[0], loss_target[0], a_norm, a_w_in[0].astype(BF16), a_rel_bias[0], late_shards,
        kv_norm.reshape(1, D_MODEL), t5_bias, b_norm, b_sinks, final_norm.reshape(1, D_MODEL))

    shard_w = dict(a_w_in=a_w_in[0], b_w_in=b_w_in[0], a_w_out=a_w_out[0], b_w_out=b_w_out[0], kv_w=kv_w)
    shard_m = dict(a_w_in=m_a_w_in[0], b_w_in=m_b_w_in[0], a_w_out=m_a_w_out[0], b_w_out=m_b_w_out[0], kv_w=m_kv_w)
    shard_v = dict(a_w_in=v_a_w_in[0], b_w_in=v_b_w_in[0], a_w_out=v_a_w_out[0], b_w_out=v_b_w_out[0], kv_w=v_kv_w)
    big = {n: _reduce_adamw("adamw_" + n, own, partials, shard_w[n], shard_m[n], shard_v[n])
           for n, (own, partials) in matrices.items()}

    names = ("a_norm", "a_rel_bias", "kv_norm", "t5_bias", "b_norm", "b_sinks", "final_norm")
    tables = ("a_rel_bias", "t5_bias")

    def row(n, a):
        return a.reshape(-1, a.shape[-1]).T if n in tables else a.reshape(1, -1)

    small_w = [row(n, a) for n, a in zip(names, (a_norm, a_rel_bias, kv_norm, t5_bias, b_norm, b_sinks, final_norm))]
    small_m = [row(n, a) for n, a in zip(names, (m_a_norm, m_a_rel_bias, m_kv_norm, m_t5_bias, m_b_norm, m_b_sinks,
                                                 m_final_norm))]
    small_v = [row(n, a) for n, a in zip(names, (v_a_norm, v_a_rel_bias, v_kv_norm, v_t5_bias, v_b_norm, v_b_sinks,
                                                 v_final_norm))]
    sums = dict(zip(names + ("loss",), _small_allreduce([loc[n] for n in names] + [loc["loss"]])))
    sums["a_rel_bias"] = _a_bias_grad(sums["a_rel_bias"])
    sums["t5_bias"] = sums["t5_bias"][:, :T5_BUCKETS]
    sums["b_sinks"] = sums["b_sinks"][:, 0].reshape(1, N_HEADS)
    results = _small_adamw(slot_arr, [sums[n] for n in names + ("loss",)], small_w, small_m, small_v)
    like = dict(a_norm=a_norm, a_rel_bias=a_rel_bias, kv_norm=kv_norm, t5_bias=t5_bias, b_norm=b_norm,
                b_sinks=b_sinks, final_norm=final_norm)
    sm = [{n: (part[i].T if n in tables else part[i]).reshape(like[n].shape) for i, n in enumerate(names)}
          for part in results]
    loss = results[0][len(names)][0, 0]

    order = ("a_norm", "a_w_in", "a_rel_bias", "a_w_out", "kv_norm", "kv_w", "t5_bias", "b_norm",
             "b_w_in", "b_sinks", "b_w_out", "final_norm")
    lead = dict(a_w_in=True, b_w_in=True, a_w_out=True, b_w_out=True, kv_w=False)

    def pick(kind, name):
        if name in big:
            val = big[name][kind]
            return val[None] if lead[name] else val
        return sm[kind][name]

    outs = [loss, grad_x[None]]
    for kind in range(4):
        outs += [pick(kind, n) for n in order]
    return tuple(outs)
```

```python
import functools
import math

import numpy as np
import jax
import jax.numpy as jnp
from jax import lax
from jax.experimental import pallas as pl
from jax.experimental.pallas import tpu as pltpu

F32 = jnp.float32
BF16 = jnp.bfloat16
SDS = jax.ShapeDtypeStruct

D_MODEL = 1024
HEAD_DIM = 64
CHUNK = 64
N_HEADS = 16
RMS_EPS = 1e-6
A_LEFT_CHUNKS = 8
A_BAND = (A_LEFT_CHUNKS + 1) * CHUNK
A_REL_CLIP = 256
B_KV_HEADS = 2
B_GROUP = 8
B_LEFT_CHUNKS = 2
B_BAND = (B_LEFT_CHUNKS + 1) * CHUNK
T5_BUCKETS = 32
T5_MAX_DIST = 128
QBLK = 256
A_KEYS = 3 * QBLK
B_QBLK_FWD = 128
B_QBLK_BWD = 256
B_PREV = 128
A_DIAG = A_KEYS
NEG = -1e30
SCALE = HEAD_DIM ** -0.5
N_DEV = 8

ADAM_LR = 0.001
ADAM_B1 = 0.9
ADAM_B2 = 0.999
ADAM_EPS = 1e-08
ADAM_WD = 0.01
ADAM_STEP = 10

VMEM_LIMIT_BYTES = 56 * 1024 * 1024
MESH = pl.DeviceIdType.MESH


def _cparams():
    return pltpu.CompilerParams(vmem_limit_bytes=VMEM_LIMIT_BYTES)


def _dot(a, b):
    return jnp.dot(a, b, preferred_element_type=F32)


def _dot_nt(a, b):
    return lax.dot_general(a, b, (((1,), (1,)), ((), ())), preferred_element_type=F32)


def _dot_tn(a, b):
    return lax.dot_general(a, b, (((0,), (0,)), ((), ())), preferred_element_type=F32)


def _rstd(xf):
    return lax.rsqrt(jnp.mean(xf * xf, axis=-1, keepdims=True) + RMS_EPS)


def _sigmoid(x):
    return 1.0 / (1.0 + jnp.exp(-x))


_GATHER_SEQUENCE = ((0, None), (1, 0), (2, 1), (4, None), (5, None), (3, 2), (6, None))


def _gather_order(x, y, c):
    others = [(1 - x, y), (x, 1 - y), (1 - x, 1 - y)]
    arrivals = [_slot(x, y, 1 - c)] + [_slot(*chip, c) for chip in others] + [_slot(*chip, 1 - c) for chip in others]
    return jnp.stack([_slot(x, y, c)] + [arrivals[a] for a, _ in _GATHER_SEQUENCE]).astype(jnp.int32)


def _norm_matmul_gather(order, x, gain_shard, w_shard):
    t = x.shape[0]
    dw, tn = w_shard.shape
    tm = min(t, 1024)
    n_m = t // tm

    def body(order_ref, x_ref, gs_ref, shard_ref, xn_ref, o_ref, full_ref, gain_ref,
             xn_all, wbuf, gland, send_sems, recv_sems, gsend_sems, grecv_sems, load_sems, own_sem):
        n, m = pl.program_id(0), pl.program_id(1)
        x_i, y_i, c_i, chips = _place()
        me, sibling = (x_i, y_i, c_i), (x_i, y_i, 1 - c_i)

        def send(k, block, to, src=None):
            dst = full_ref.at[_slot(*block)]
            return pltpu.make_async_remote_copy(
                src_ref=dst if src is None else src, dst_ref=dst,
                send_sem=send_sems.at[k], recv_sem=recv_sems.at[k], device_id=to, device_id_type=MESH)

        own = pltpu.make_async_copy(shard_ref, full_ref.at[_slot(*me)], own_sem)
        first = [send(0, me, sibling, src=shard_ref)]
        first += [send(1 + j, me, (*chip, c_i), src=shard_ref) for j, chip in enumerate(chips)]
        forwards = [send(4 + j, (*chip, c_i), sibling) for j, chip in enumerate(chips)]
        arrivals = [send(0, sibling, me)] + [send(1 + j, (*chip, c_i), me) for j, chip in enumerate(chips)]
        arrivals += [send(4 + j, (*chip, 1 - c_i), me) for j, chip in enumerate(chips)]
        gains = [pltpu.make_async_remote_copy(
            src_ref=gs_ref, dst_ref=gland.at[_slot(*me)], send_sem=gsend_sems.at[k - 1],
            recv_sem=grecv_sems.at[k - 1], device_id=_peer(x_i, y_i, c_i, k), device_id_type=MESH)
            for k in range(1, N_DEV)]

        @pl.when(jnp.logical_and(n == 0, m == 0))
        def _():
            own.start()
            for cp in gains + first:
                cp.start()
            pltpu.make_async_copy(shard_ref, wbuf.at[0], load_sems.at[0]).start()
            gland[_slot(*me)] = gs_ref[...]
            for k in range(1, N_DEV):
                pltpu.make_async_remote_copy(
                    src_ref=gs_ref, dst_ref=gland.at[_slot(*_peer(x_i, y_i, c_i, k))],
                    send_sem=gsend_sems.at[k - 1], recv_sem=grecv_sems.at[k - 1],
                    device_id=me, device_id_type=MESH).wait_recv()
            for s in range(N_DEV):
                gain_ref[:, 128 * s:128 * (s + 1)] = gland[s]

        rows = pl.ds(pl.multiple_of(m * tm, tm), tm)

        @pl.when(n == 0)
        def _():
            xf = x_ref[...]
            xn = ((xf * _rstd(xf)) * gain_ref[...]).astype(BF16)
            xn_all[rows, :] = xn
            xn_ref[...] = xn

        @pl.when(m == 0)
        def _():
            pltpu.make_async_copy(full_ref.at[0], wbuf.at[n % 2], load_sems.at[n % 2]).wait()

        o_ref[...] = _dot(xn_all[rows, :], wbuf[n % 2]).astype(BF16)

        for k, (arrival, forward) in enumerate(_GATHER_SEQUENCE):
            @pl.when(jnp.logical_and(n == k, m == n_m - 1))
            def _(k=k, arrival=arrival, forward=forward):
                arrivals[arrival].wait_recv()
                if forward is not None:
                    forwards[forward].start()
                pltpu.make_async_copy(full_ref.at[order_ref[k + 1]], wbuf.at[(k + 1) % 2],
                                      load_sems.at[(k + 1) % 2]).start()

        @pl.when(jnp.logical_and(n == N_DEV - 1, m == n_m - 1))
        def _():
            for cp in gains + first + forwards:
                cp.wait_send()
            own.wait()

    held = lambda n, m, order: (jnp.where(n == 0, m, n_m - 1), 0)
    return pl.pallas_call(
        body, name="norm_matmul_gather",
        grid_spec=pltpu.PrefetchScalarGridSpec(
            num_scalar_prefetch=1, grid=(N_DEV, n_m),
            in_specs=[pl.BlockSpec((tm, D_MODEL), held),
                      pl.BlockSpec((1, 128), lambda n, m, order: (0, 0)), ANY],
            out_specs=[pl.BlockSpec((tm, D_MODEL), held),
                       pl.BlockSpec((tm, tn), lambda n, m, order: (m, order[n])),
                       ANY, pl.BlockSpec((1, D_MODEL), lambda n, m, order: (0, 0))],
            scratch_shapes=[pltpu.VMEM((t, D_MODEL), BF16), pltpu.VMEM((2, dw, tn), BF16),
                            pltpu.VMEM((N_DEV, 1, 128), F32),
                            pltpu.SemaphoreType.DMA((7,)), pltpu.SemaphoreType.DMA((7,)),
                            pltpu.SemaphoreType.DMA((7,)), pltpu.SemaphoreType.DMA((7,)),
                            pltpu.SemaphoreType.DMA((2,)), pltpu.SemaphoreType.DMA]),
        out_shape=[SDS((t, D_MODEL), BF16), SDS((t, N_DEV * tn), BF16), SDS((N_DEV, dw, tn), BF16),
                   SDS((1, D_MODEL), F32)],
        compiler_params=_cparams(),
    )(order, x, gain_shard, w_shard)


def _layer_a_out(x, z, w_out, kv_gain, b_gain, kv_w, w_in_b):
    t = x.shape[0]
    tm = min(t, 512)
    nb, _, tn = w_in_b.shape

    def body(x_ref, z_ref, wo_ref, kvg_ref, bg_ref, kvw_ref, wb_ref,
             h1_ref, kvn_ref, hb_ref, kv_ref, qg_ref):
        h1 = x_ref[...] + _dot(z_ref[...], wo_ref[...])
        h1_ref[...] = h1
        y0 = h1 * _rstd(h1)
        kvn = (y0 * kvg_ref[...]).astype(BF16)
        hb = (y0 * bg_ref[...]).astype(BF16)
        kvn_ref[...] = kvn
        hb_ref[...] = hb
        kv_ref[...] = _dot(kvn, kvw_ref[...]).astype(BF16)
        for i in range(nb):
            qg_ref[:, i * tn:(i + 1) * tn] = _dot(hb, wb_ref[i]).astype(BF16)

    row = lambda m: (m, 0)
    fix2 = lambda m: (0, 0)
    return pl.pallas_call(
        body, name="layer_a_out", grid=(t // tm,),
        in_specs=[pl.BlockSpec((tm, D_MODEL), row), pl.BlockSpec((tm, D_MODEL), row),
                  pl.BlockSpec((D_MODEL, D_MODEL), fix2),
                  pl.BlockSpec((1, D_MODEL), fix2), pl.BlockSpec((1, D_MODEL), fix2),
                  pl.BlockSpec((D_MODEL, 256), fix2),
                  pl.BlockSpec((nb, D_MODEL, tn), lambda m: (0, 0, 0))],
        out_specs=[pl.BlockSpec((tm, D_MODEL), row), pl.BlockSpec((tm, D_MODEL), row),
                   pl.BlockSpec((tm, D_MODEL), row), pl.BlockSpec((tm, 256), row),
                   pl.BlockSpec((tm, nb * tn), row)],
        out_shape=[SDS((t, D_MODEL), F32), SDS((t, D_MODEL), BF16), SDS((t, D_MODEL), BF16),
                   SDS((t, 256), BF16), SDS((t, nb * tn), BF16)],
        compiler_params=_cparams(),
    )(x, z, w_out, kv_gain, b_gain, kv_w, w_in_b)


def _layer_b_out_loss(h1, z, w_out, f_gain, target):
    t = h1.shape[0]
    tm = min(t, 512)

    def body(h1_ref, z_ref, wo_ref, fg_ref, tgt_ref,
             dh2_ref, dh2b_ref, dz_ref, loss_ref, dfn_ref):
        @pl.when(pl.program_id(0) == 0)
        def _():
            loss_ref[...] = jnp.zeros_like(loss_ref)
            dfn_ref[...] = jnp.zeros_like(dfn_ref)

        h2 = h1_ref[...] + _dot(z_ref[...], wo_ref[...])
        r = _rstd(h2)
        yn = h2 * r
        fg = fg_ref[...]
        err = yn * fg - tgt_ref[...]
        loss_ref[...] += (0.5 / D_MODEL) * jnp.sum(err * err)
        dy = err * (1.0 / D_MODEL)
        dfn_ref[...] += jnp.sum(dy * yn, axis=0, keepdims=True)
        u = dy * fg
        dh2 = r * u - h2 * ((r * r * r) * jnp.mean(u * h2, axis=-1, keepdims=True))
        dh2_ref[...] = dh2
        dh2b = dh2.astype(BF16)
        dh2b_ref[...] = dh2b
        dz_ref[...] = _dot_nt(dh2b, wo_ref[...]).astype(BF16)

    row = lambda m: (m, 0)
    fix2 = lambda m: (0, 0)
    return pl.pallas_call(
        body, name="layer_b_out_loss", grid=(t // tm,),
        in_specs=[pl.BlockSpec((tm, D_MODEL), row), pl.BlockSpec((tm, D_MODEL), row),
                  pl.BlockSpec((D_MODEL, D_MODEL), fix2), pl.BlockSpec((1, D_MODEL), fix2),
                  pl.BlockSpec((tm, D_MODEL), row)],
        out_specs=[pl.BlockSpec((tm, D_MODEL), row), pl.BlockSpec((tm, D_MODEL), row),
                   pl.BlockSpec((tm, D_MODEL), row), pl.BlockSpec((1, 128), fix2),
                   pl.BlockSpec((1, D_MODEL), fix2)],
        out_shape=[SDS((t, D_MODEL), F32), SDS((t, D_MODEL), BF16), SDS((t, D_MODEL), BF16),
                   SDS((1, 128), F32), SDS((1, D_MODEL), F32)],
        compiler_params=_cparams(),
    )(h1, z, w_out, f_gain, target)


def _layer_b_in_bwd(dqg, dkv, w_in_b, kv_w, h1, dh2, b_gain, kv_gain, w_out_a):
    t = h1.shape[0]
    tm = min(t, 512)
    nb, _, tn = w_in_b.shape
    per = D_MODEL // tn

    def body(dqg_ref, dkv_ref, wb_ref, kvw_ref, h1_ref, dh2_ref, bg_ref, kvg_ref, wo_ref,
             dh1_ref, dh1b_ref, dz_ref, dbn_ref, dkn_ref):
        @pl.when(pl.program_id(0) == 0)
        def _():
            dbn_ref[...] = jnp.zeros_like(dbn_ref)
            dkn_ref[...] = jnp.zeros_like(dkn_ref)

        dhb = jnp.zeros((tm, D_MODEL), F32)
        for i in range(nb):
            blk = dqg_ref[i // per, :, (i % per) * tn:(i % per + 1) * tn]
            dhb = dhb + _dot_nt(blk, wb_ref[i])
        dkn = (_dot_nt(dkv_ref[0].astype(BF16), kvw_ref[:, 0:128])
               + _dot_nt(dkv_ref[1].astype(BF16), kvw_ref[:, 128:256]))
        h1 = h1_ref[...]
        r = _rstd(h1)
        xr = h1 * r
        dbn_ref[...] += jnp.sum(dhb * xr, axis=0, keepdims=True)
        dkn_ref[...] += jnp.sum(dkn * xr, axis=0, keepdims=True)
        u = dhb * bg_ref[...] + dkn * kvg_ref[...]
        dh1 = dh2_ref[...] + r * u - h1 * ((r * r * r) * jnp.mean(u * h1, axis=-1, keepdims=True))
        dh1_ref[...] = dh1
        dh1b = dh1.astype(BF16)
        dh1b_ref[...] = dh1b
        dz_ref[...] = _dot_nt(dh1b, wo_ref[...]).astype(BF16)

    row = lambda m: (m, 0)
    fix2 = lambda m: (0, 0)
    return pl.pallas_call(
        body, name="layer_b_in_bwd", grid=(t // tm,),
        in_specs=[pl.BlockSpec((2, tm, D_MODEL), lambda m: (0, m, 0)),
                  pl.BlockSpec((2, tm, 128), lambda m: (0, m, 0)),
                  pl.BlockSpec((nb, D_MODEL, tn), lambda m: (0, 0, 0)),
                  pl.BlockSpec((D_MODEL, 256), fix2),
                  pl.BlockSpec((tm, D_MODEL), row), pl.BlockSpec((tm, D_MODEL), row),
                  pl.BlockSpec((1, D_MODEL), fix2), pl.BlockSpec((1, D_MODEL), fix2),
                  pl.BlockSpec((D_MODEL, D_MODEL), fix2)],
        out_specs=[pl.BlockSpec((tm, D_MODEL), row), pl.BlockSpec((tm, D_MODEL), row),
                   pl.BlockSpec((tm, D_MODEL), row), pl.BlockSpec((1, D_MODEL), fix2),
                   pl.BlockSpec((1, D_MODEL), fix2)],
        out_shape=[SDS((t, D_MODEL), F32), SDS((t, D_MODEL), BF16), SDS((t, D_MODEL), BF16),
                   SDS((1, D_MODEL), F32), SDS((1, D_MODEL), F32)],
        compiler_params=_cparams(),
    )(dqg, dkv, w_in_b, kv_w, h1, dh2, b_gain, kv_gain, w_out_a)


def _layer_a_in_bwd(dqg, dkv, w_in_a, x, dh1, a_gain, chip_sums):
    t = x.shape[0]
    tm = min(t, 512)
    nb, _, tn = w_in_a.shape
    per = D_MODEL // tn

    def body(dqg_ref, dkv_ref, w_ref, x_ref, dh1_ref, ag_ref, sums_ref, dx_ref, dan_ref, land_ref,
             send_sems, recv_sems):
        @pl.when(pl.program_id(0) == 0)
        def _():
            dan_ref[...] = jnp.zeros_like(dan_ref)
            for cp in _chip_copies(sums_ref, land_ref, send_sems, recv_sems):
                cp.start()

        dxn = jnp.zeros((tm, D_MODEL), F32)
        for i in range(nb):
            part = i // per
            src = dqg_ref if part in (0, 3) else dkv_ref
            outer = {0: 0, 3: 1, 1: 0, 2: 1}[part]
            blk = src[outer, :, (i % per) * tn:(i % per + 1) * tn]
            dxn = dxn + _dot_nt(blk, w_ref[i])
        xf = x_ref[...]
        r = _rstd(xf)
        dan_ref[...] += jnp.sum(dxn * (xf * r), axis=0, keepdims=True)
        u = dxn * ag_ref[...]
        dx_ref[...] = dh1_ref[...] + r * u - xf * ((r * r * r) * jnp.mean(u * xf, axis=-1, keepdims=True))

        @pl.when(pl.program_id(0) == t // tm - 1)
        def _():
            for cp in _chip_copies(sums_ref, land_ref, send_sems, recv_sems):
                cp.wait()

    row = lambda m: (m, 0)
    fix2 = lambda m: (0, 0)
    return pl.pallas_call(
        body, name="layer_a_in_bwd", grid=(t // tm,),
        in_specs=[pl.BlockSpec((2, tm, D_MODEL), lambda m: (0, m, 0)),
                  pl.BlockSpec((2, tm, D_MODEL), lambda m: (0, m, 0)),
                  pl.BlockSpec((nb, D_MODEL, tn), lambda m: (0, 0, 0)),
                  pl.BlockSpec((tm, D_MODEL), row), pl.BlockSpec((tm, D_MODEL), row),
                  pl.BlockSpec((1, D_MODEL), fix2), ANY],
        out_specs=[pl.BlockSpec((tm, D_MODEL), row), pl.BlockSpec((1, D_MODEL), fix2), ANY],
        out_shape=[SDS((t, D_MODEL), F32), SDS((1, D_MODEL), F32), SDS(chip_sums.shape, chip_sums.dtype)],
        scratch_shapes=[pltpu.SemaphoreType.DMA((3,)), pltpu.SemaphoreType.DMA((3,))],
        compiler_params=_cparams(),
    )(dqg, dkv, w_in_a, x, dh1, a_gain, chip_sums)


def _lut(s, vals):
    r = jnp.int32(vals[0])
    for i in range(1, len(vals)):
        r = jnp.where(s == i, jnp.int32(vals[i]), r)
    return r


def _held(steps, i):
    seq, cur = [None] * len(steps), None
    for k in range(len(steps) - 1, -1, -1):
        if steps[k][0] == i:
            cur = steps[k][1:3]
        seq[k] = cur
    for k in range(len(steps)):
        cur = seq[k] = seq[k] if seq[k] is not None else cur
    return seq


def _weight_grad_cols(name, my_slot, a, bs, steps, tn):
    t, dw = a.shape
    n_arr = len(bs)
    which = [s[0] for s in steps]
    blks = [s[3] for s in steps]

    def body(slot_ref, a_ref, *rest):
        b_refs, (o_ref, own_ref, at_ref) = rest[:n_arr], rest[n_arr:]
        s = pl.program_id(0)

        @pl.when(s == 0)
        def _():
            at_ref[...] = a_ref[...].T

        for i in range(n_arr):
            @pl.when(_lut(s, which) == i)
            def _(i=i):
                res = _dot(at_ref[...], b_refs[i][0])
                o_ref[0] = res.astype(BF16)

                @pl.when(_lut(s, blks) == slot_ref[0])
                def _():
                    own_ref[...] = res

    def b_spec(i):
        held = _held(steps, i)
        return pl.BlockSpec((1, t, tn), lambda s, slot: (_lut(s, [h[0] for h in held]), 0,
                                                         _lut(s, [h[1] for h in held])))

    return pl.pallas_call(
        body, name=name,
        grid_spec=pltpu.PrefetchScalarGridSpec(
            num_scalar_prefetch=1, grid=(len(steps),),
            in_specs=[pl.BlockSpec((t, dw), lambda s, slot: (0, 0))] + [b_spec(i) for i in range(n_arr)],
            out_specs=[pl.BlockSpec((1, dw, tn), lambda s, slot: (_lut(s, blks), 0, 0)),
                       pl.BlockSpec((dw, tn), lambda s, slot: (0, 0))],
            scratch_shapes=[pltpu.VMEM((dw, t), BF16)]),
        out_shape=[SDS((N_DEV, dw, tn), BF16), SDS((dw, tn), F32)],
        compiler_params=_cparams(),
    )(my_slot, a, *bs)


def _weight_grad_rows(name, my_slot, a, b):
    t, dw = a.shape
    n_o, _, c = b.shape
    rows = dw // N_DEV
    tn = min(c, 256)
    per = c // tn

    def body(slot_ref, a_ref, b_ref, o_ref, own_ref, at_ref, res_ref):
        @pl.when(pl.program_id(0) == 0)
        def _():
            at_ref[...] = a_ref[...].T

        res_ref[...] = _dot(at_ref[...], b_ref[0].astype(BF16))
        o_ref[...] = res_ref[...].astype(BF16)
        own_ref[...] = res_ref[pl.ds(pl.multiple_of(slot_ref[0] * rows, rows), rows), :]

    all_rows, own = pl.pallas_call(
        body, name=name,
        grid_spec=pltpu.PrefetchScalarGridSpec(
            num_scalar_prefetch=1, grid=(n_o * per,),
            in_specs=[pl.BlockSpec((t, dw), lambda s, slot: (0, 0)),
                      pl.BlockSpec((1, t, tn), lambda s, slot: (s // per, 0, s % per))],
            out_specs=[pl.BlockSpec((dw, tn), lambda s, slot: (0, s)),
                       pl.BlockSpec((rows, tn), lambda s, slot: (0, s))],
            scratch_shapes=[pltpu.VMEM((dw, t), BF16), pltpu.VMEM((dw, tn), F32)]),
        out_shape=[SDS((dw, n_o * c), BF16), SDS((rows, n_o * c), F32)],
        compiler_params=_cparams(),
    )(my_slot, a, b)
    return all_rows.reshape(N_DEV, rows, n_o * c), own


def _lane_lo():
    return lax.broadcasted_iota(jnp.int32, (1, 128), 1) < HEAD_DIM


def _collapse_chunks(ds, keys):
    if ds.shape[1] < keys:
        ds = jnp.concatenate([jnp.zeros((ds.shape[0], keys - ds.shape[1]), F32), ds], axis=1)
    gc = ds[0:CHUNK]
    for cc in range(1, ds.shape[0] // CHUNK):
        gc = gc + pltpu.roll(ds[cc * CHUNK:(cc + 1) * CHUNK], keys - cc * CHUNK, 1)
    return gc


def _offset_sums(gc):
    hi = gc.astype(BF16)
    lo = (gc - hi.astype(F32)).astype(BF16)
    flip = (lax.broadcasted_iota(jnp.int32, (CHUNK, CHUNK), 0)
            + lax.broadcasted_iota(jnp.int32, (CHUNK, CHUNK), 1) == CHUNK - 1).astype(BF16)
    gf = _dot(flip, hi) + _dot(flip, lo)
    skew = pltpu.roll(gf, 0, 1, stride=1, stride_axis=0)
    return jnp.sum(skew, axis=0, keepdims=True)


def _band_bias(w_row, band, rows):
    keys = w_row.shape[1]
    base = jnp.broadcast_to(w_row, (CHUNK, keys))
    skew = pltpu.roll(base, 0, 1, stride=1, stride_axis=0)
    skew = pltpu.roll(skew, keys - (CHUNK - 1), 1)
    col = lax.broadcasted_iota(jnp.int32, (CHUNK, keys), 1)
    chunk0 = jnp.where(col < band, skew, NEG)
    return jnp.concatenate(
        [chunk0] + [pltpu.roll(chunk0, cc * CHUNK, 1) for cc in range(1, rows // CHUNK)], axis=0)


def _silu_parts(g):
    sg = _sigmoid(g)
    return g * sg, sg * (1.0 + g * (1.0 - sg))


A_PAIRS = 2
A_LANES = 128 * A_PAIRS
A_STEPS = D_MODEL // A_LANES


def _a_specs():
    q = pl.BlockSpec((QBLK, A_LANES), lambda p, j: (j, p))
    ks = [pl.BlockSpec((QBLK, A_LANES), lambda p, j, b=b: (jnp.maximum(j - 2 + b, 0), A_STEPS + p)) for b in range(3)]
    vs = [pl.BlockSpec((QBLK, A_LANES), lambda p, j, b=b: (jnp.maximum(j - 2 + b, 0), 2 * A_STEPS + p))
          for b in range(3)]
    g = pl.BlockSpec((QBLK, A_LANES), lambda p, j: (j, 3 * A_STEPS + p))
    bias = pl.BlockSpec((A_PAIRS, 8, A_KEYS), lambda p, j: (p, 0, 0))
    return q, ks, vs, g, bias


def _a_fill_bias(w_ref, b_ref, j):
    _fill_bias(2 * A_PAIRS, lambda h: w_ref[h // 2, h % 2:h % 2 + 1, :], A_BAND, b_ref, j)


def _by_valid_key_blocks(j, fn):
    pl.when(j == 0)(functools.partial(fn, 1))
    pl.when(j == 1)(functools.partial(fn, 2))
    pl.when(j >= 2)(functools.partial(fn, 3))


def _fill_bias(n, get_row, band, bias_scr, j):
    @pl.when(j == 0)
    def _():
        for h in range(n):
            bias_scr[h] = _band_bias(get_row(h), band, bias_scr.shape[1])


def _normalise_pair(rs, mxs, lane_lo, extra=None):
    num = jnp.where(lane_lo, rs[0], rs[1])
    den = pltpu.roll(jnp.where(lane_lo, rs[1], rs[0]), HEAD_DIM, 1)
    if extra is not None:
        den = den + jnp.where(lane_lo, extra[0], extra[1])
    return num / den, jnp.where(lane_lo, mxs[0], mxs[1]) + jnp.log(den)


def _own_everywhere(x, sel):
    return jnp.where(sel, x, pltpu.roll(x, HEAD_DIM, 1))


def _minus_rows(s, row_full):
    return jnp.concatenate([s[:, i:i + 128] - row_full for i in range(0, s.shape[1], 128)], axis=1)


def _attn_a_fwd(qkvg, bias, gather):
    t = qkvg.shape[0]
    nq = t // QBLK
    n_g = len(gather)
    q_spec, k_specs, v_specs, g_spec, bias_spec = _a_specs()

    def body(q_ref, k0, k1, k2, v0, v1, v2, g_ref, w_ref, *rest):
        shard_refs, rest = rest[:n_g], rest[n_g:]
        z_ref, o_ref, lse_ref = rest[:3]
        full_refs, (b_ref, *comm) = rest[3:3 + n_g], rest[3 + n_g:]
        p = pl.program_id(0)
        j = pl.program_id(1)
        start, forward, finish = _gather_phases(shard_refs, full_refs, *comm)
        pl.when(jnp.logical_and(p == 0, j == 0))(start)
        pl.when(jnp.logical_and(p == A_STEPS // 2, j == 0))(forward)
        _a_fill_bias(w_ref, b_ref, j)
        lane_lo = _lane_lo()
        sels = (lane_lo, jnp.logical_not(lane_lo))

        def attend(n_blocks):
            first_col = (3 - n_blocks) * QBLK
            for pp in range(A_PAIRS):
                cols = slice(128 * pp, 128 * (pp + 1))
                k = jnp.concatenate([r[:, cols] for r in (k0, k1, k2)[3 - n_blocks:]], axis=0)
                v = jnp.concatenate([r[:, cols] for r in (v0, v1, v2)[3 - n_blocks:]], axis=0)
                q = q_ref[:, cols]
                qm2 = jnp.concatenate([jnp.where(sel, q, jnp.zeros_like(q)) for sel in sels], axis=0) * SCALE
                s2 = _dot_nt(qm2, k)
                rs, mxs = [], []
                for hh, sel in enumerate(sels):
                    s = s2[hh * QBLK:(hh + 1) * QBLK] + b_ref[2 * pp + hh, :, first_col:]
                    mxs.append(jnp.max(s, axis=-1, keepdims=True))
                    e = jnp.exp(s - mxs[hh]).astype(BF16)
                    rs.append(_dot(e, jnp.where(sel, v, jnp.ones_like(v))))
                o, lse = _normalise_pair(rs, mxs, lane_lo)
                silu, _ = _silu_parts(g_ref[:, cols].astype(F32))
                o_ref[:, cols] = o.astype(BF16)
                z_ref[:, cols] = (o * silu).astype(BF16)
                lse_ref[:, cols] = lse

        _by_valid_key_blocks(j, attend)
        pl.when(jnp.logical_and(p == A_STEPS - 1, j == nq - 1))(finish)

    out_spec = pl.BlockSpec((QBLK, A_LANES), lambda p, j: (j, p))
    outs = pl.pallas_call(
        body, name="attn_a_fwd", grid=(A_STEPS, nq),
        in_specs=[q_spec, *k_specs, *v_specs, g_spec, bias_spec] + [ANY] * n_g,
        out_specs=[out_spec, out_spec, out_spec] + [ANY] * n_g,
        out_shape=[SDS((t, D_MODEL), BF16), SDS((t, D_MODEL), BF16), SDS((t, D_MODEL), F32)]
        + [SDS((N_DEV, *s.shape), s.dtype) for s in gather],
        scratch_shapes=[pltpu.VMEM((2 * A_PAIRS, QBLK, A_KEYS), F32)] + _gather_scratch(n_g),
        compiler_params=_cparams(),
    )(qkvg, qkvg, qkvg, qkvg, qkvg, qkvg, qkvg, qkvg, bias, *gather)
    return outs[0], outs[1], outs[2], list(outs[3:])


def _attn_a_bwd(qkvg, bias, out_a, lse, dz, scatter):
    t = qkvg.shape[0]
    nq = t // QBLK
    n_sc = len(scatter)
    q_spec, k_specs, v_specs, g_spec, bias_spec = _a_specs()

    def body(q_ref, k0, k1, k2, v0, v1, v2, g_ref, w_ref, o_ref, lse_ref, dz_ref, *rest):
        sc_refs, rest = rest[:n_sc], rest[n_sc:]
        dqg_ref, dkv_ref, dg_ref = rest[:3]
        land_refs, rest = rest[3:3 + n_sc], rest[3 + n_sc:]
        dk_acc, dv_acc, gt_acc, b_ref, send_sems, recv_sems = rest
        j = pl.program_id(1)
        first = jnp.logical_and(pl.program_id(0) == 0, j == 0)
        last = jnp.logical_and(pl.program_id(0) == A_STEPS - 1, j == nq - 1)

        @pl.when(first)
        def _():
            for cp in _scatter_copies(sc_refs, land_refs, send_sems, recv_sems):
                cp.start()

        _a_fill_bias(w_ref, b_ref, j)

        @pl.when(j == 0)
        def _():
            dk_acc[...] = jnp.zeros_like(dk_acc)
            dv_acc[...] = jnp.zeros_like(dv_acc)
            gt_acc[...] = jnp.zeros_like(gt_acc)

        lane_lo = _lane_lo()
        sels = (lane_lo, jnp.logical_not(lane_lo))

        def attend(n_blocks):
            first_col = (3 - n_blocks) * QBLK
            for pp in range(A_PAIRS):
                cols = slice(128 * pp, 128 * (pp + 1))
                q = q_ref[:, cols]
                k = jnp.concatenate([r[:, cols] for r in (k0, k1, k2)[3 - n_blocks:]], axis=0)
                v = jnp.concatenate([r[:, cols] for r in (v0, v1, v2)[3 - n_blocks:]], axis=0)
                o = o_ref[:, cols].astype(F32)
                lse_pair = lse_ref[:, cols]
                dzf = dz_ref[:, cols].astype(F32)
                silu, dsilu = _silu_parts(g_ref[:, cols].astype(F32))
                do = dzf * silu
                dqg_ref[1, :, cols] = (dzf * o * dsilu).astype(BF16)
                doo = do * o
                qm2 = jnp.concatenate([jnp.where(sel, q, jnp.zeros_like(q)) for sel in sels], axis=0) * SCALE
                dom2 = jnp.concatenate([jnp.where(sel, do, 0.0) for sel in sels], axis=0).astype(BF16)
                s2 = _dot_nt(qm2, k)
                dp2 = _dot_nt(dom2, v)
                ps, dss = [], []
                for hh, sel in enumerate(sels):
                    rows = slice(hh * QBLK, (hh + 1) * QBLK)
                    s = s2[rows] + b_ref[2 * pp + hh, :, first_col:]
                    p = jnp.exp(_minus_rows(s, _own_everywhere(lse_pair, sel)))
                    delta = jnp.sum(jnp.where(sel, doo, 0.0), axis=-1, keepdims=True)
                    ds = p * (dp2[rows] - delta)
                    gt_acc[2 * pp + hh] += _collapse_chunks(ds, A_KEYS)
                    ps.append(p.astype(BF16))
                    dss.append(ds.astype(BF16))
                dsb2 = jnp.concatenate(dss, axis=0)
                dq2 = _dot(dsb2, k) * SCALE
                dk_blk = _dot_tn(dsb2, qm2)
                dv_blk = _dot_tn(jnp.concatenate(ps, axis=0), dom2)
                dqg_ref[0, :, cols] = jnp.where(lane_lo, dq2[0:QBLK], dq2[QBLK:2 * QBLK]).astype(BF16)
                for b in range(n_blocks):
                    rows = pl.ds(pl.multiple_of((j - n_blocks + 1 + b) * QBLK, QBLK), QBLK)
                    dk_acc[rows, cols] += dk_blk[b * QBLK:(b + 1) * QBLK]
                    dv_acc[rows, cols] += dv_blk[b * QBLK:(b + 1) * QBLK]

        _by_valid_key_blocks(j, attend)

        @pl.when(j == nq - 1)
        def _():
            dkv_ref[0] = dk_acc[...].astype(BF16)
            dkv_ref[1] = dv_acc[...].astype(BF16)
            for pp in range(A_PAIRS):
                dg_ref[pp] = jnp.concatenate([_offset_sums(gt_acc[2 * pp]), _offset_sums(gt_acc[2 * pp + 1]),
                                              jnp.zeros((6, A_DIAG), F32)], axis=0)

        @pl.when(last)
        def _():
            for cp in _scatter_copies(sc_refs, land_refs, send_sems, recv_sems):
                cp.wait()

    blk = pl.BlockSpec((QBLK, A_LANES), lambda p, j: (j, p))
    outs = pl.pallas_call(
        body, name="attn_a_bwd", grid=(A_STEPS, nq),
        in_specs=[q_spec, *k_specs, *v_specs, g_spec, bias_spec, blk, blk, blk] + [ANY] * n_sc,
        out_specs=[pl.BlockSpec((2, QBLK, A_LANES), lambda p, j: (0, j, p)),
                   pl.BlockSpec((2, t, A_LANES), lambda p, j: (0, 0, p)),
                   pl.BlockSpec((A_PAIRS, 8, A_DIAG), lambda p, j: (p, 0, 0))] + [ANY] * n_sc,
        out_shape=[SDS((2, t, D_MODEL), BF16), SDS((2, t, D_MODEL), BF16), SDS((N_HEADS // 2, 8, A_DIAG), F32)]
        + [SDS((N_DEV - 1, *g.shape[1:]), g.dtype) for g in scatter],
        scratch_shapes=[pltpu.VMEM((t, A_LANES), F32), pltpu.VMEM((t, A_LANES), F32),
                        pltpu.VMEM((2 * A_PAIRS, CHUNK, A_KEYS), F32), pltpu.VMEM((2 * A_PAIRS, QBLK, A_KEYS), F32),
                        pltpu.SemaphoreType.DMA(((N_DEV - 1) * n_sc,)),
                        pltpu.SemaphoreType.DMA(((N_DEV - 1) * n_sc,))],
        compiler_params=_cparams(),
    )(qkvg, qkvg, qkvg, qkvg, qkvg, qkvg, qkvg, qkvg, bias, out_a, lse, dz, *scatter)
    return outs[0], outs[1], outs[2], list(outs[3:])


def _b_specs(qblk):
    per = qblk // B_PREV
    q = pl.BlockSpec((qblk, 512), lambda h, j: (j, h))
    g = pl.BlockSpec((qblk, 512), lambda h, j: (j, 2 + h))
    kp = pl.BlockSpec((B_PREV, 128), lambda h, j: (jnp.maximum(per * j - 1, 0), 0))
    kc = pl.BlockSpec((qblk, 128), lambda h, j: (j, 0))
    vp = pl.BlockSpec((B_PREV, 128), lambda h, j: (jnp.maximum(per * j - 1, 0), 1))
    vc = pl.BlockSpec((qblk, 128), lambda h, j: (j, 1))
    bias = pl.BlockSpec((B_GROUP, qblk + B_PREV), lambda h, j: (h, 0))
    sinks = pl.BlockSpec(memory_space=pltpu.SMEM)
    return q, g, kp, kc, vp, vc, bias, sinks


def _b_operands(kp, kc, vp, vc, kvh, with_prev):
    k = jnp.concatenate([kp[...], kc[...]], axis=0) if with_prev else kc[...]
    v = jnp.concatenate([vp[...], vc[...]], axis=0) if with_prev else vc[...]
    kr = pltpu.roll(k, HEAD_DIM, 1)
    vr = pltpu.roll(v, HEAD_DIM, 1)
    first = kvh == 0
    return (jnp.where(first, k, kr), jnp.where(first, kr, k),
            jnp.where(first, v, vr), jnp.where(first, vr, v))


def _attn_b_fwd(qg, kv, bias, sinks):
    t = qg.shape[0]
    qblk = B_QBLK_FWD
    q_spec, g_spec, kp_spec, kc_spec, vp_spec, vc_spec, bias_spec, sink_spec = _b_specs(qblk)

    def body(q_ref, g_ref, kp, kc, vp, vc, w_ref, sink_ref, z_ref, o_ref, lse_ref, b_ref):
        kvh = pl.program_id(0)
        j = pl.program_id(1)
        _fill_bias(B_GROUP, lambda h: w_ref[h:h + 1, :], B_BAND, b_ref, j)
        lane_lo = _lane_lo()
        n_pairs = B_GROUP // 2

        def attend(with_prev):
            first_col = 0 if with_prev else B_PREV
            k_lo, k_hi, v_lo, v_hi = _b_operands(kp, kc, vp, vc, kvh, with_prev)
            halves = []
            for hh, sel in enumerate((lane_lo, jnp.logical_not(lane_lo))):
                kk = k_lo if hh == 0 else k_hi
                vv = v_lo if hh == 0 else v_hi
                qm4 = jnp.concatenate(
                    [jnp.where(sel, q_ref[:, 128 * pp:128 * (pp + 1)], jnp.zeros((qblk, 128), BF16))
                     for pp in range(n_pairs)], axis=0) * SCALE
                s4 = _dot_nt(qm4, kk)
                es, mxs = [], []
                for pp in range(n_pairs):
                    g = 2 * pp + hh
                    s = s4[pp * qblk:(pp + 1) * qblk] + b_ref[g, :, first_col:]
                    mxs.append(jnp.maximum(jnp.max(s, axis=-1, keepdims=True), sink_ref[kvh * B_GROUP + g]))
                    es.append(jnp.exp(s - mxs[pp]).astype(BF16))
                r4 = _dot(jnp.concatenate(es, axis=0), jnp.where(sel, vv, jnp.ones_like(vv)))
                halves.append((r4, mxs))
            for pp in range(n_pairs):
                cols = slice(128 * pp, 128 * (pp + 1))
                rows = slice(pp * qblk, (pp + 1) * qblk)
                mxs = [halves[hh][1][pp] for hh in range(2)]
                sink_terms = [jnp.exp(sink_ref[kvh * B_GROUP + 2 * pp + hh] - mxs[hh]) for hh in range(2)]
                o, lse = _normalise_pair([halves[hh][0][rows] for hh in range(2)], mxs, lane_lo, sink_terms)
                silu, _ = _silu_parts(g_ref[:, cols].astype(F32))
                o_ref[:, cols] = o.astype(BF16)
                z_ref[:, cols] = (o * silu).astype(BF16)
                lse_ref[:, cols] = lse

        pl.when(j == 0)(functools.partial(attend, False))
        pl.when(j >= 1)(functools.partial(attend, True))

    out_spec = pl.BlockSpec((qblk, 512), lambda h, j: (j, h))
    return pl.pallas_call(
        body, name="attn_b_fwd", grid=(B_KV_HEADS, t // qblk),
        in_specs=[q_spec, g_spec, kp_spec, kc_spec, vp_spec, vc_spec, bias_spec, sink_spec],
        out_specs=[out_spec, out_spec, out_spec],
        out_shape=[SDS((t, D_MODEL), BF16), SDS((t, D_MODEL), BF16), SDS((t, D_MODEL), F32)],
        scratch_shapes=[pltpu.VMEM((B_GROUP, qblk, qblk + B_PREV), F32)],
        compiler_params=_cparams(),
    )(qg, qg, kv, kv, kv, kv, bias, sinks)


def _attn_b_bwd(qg, kv, bias, sinks, out_b, lse, dz, bucket_onehot):
    t = qg.shape[0]
    qblk = B_QBLK_BWD
    keys = qblk + B_PREV
    nq = t // qblk
    q_spec, g_spec, kp_spec, kc_spec, vp_spec, vc_spec, bias_spec, sink_spec = _b_specs(qblk)

    def body(q_ref, g_ref, kp, kc, vp, vc, w_ref, sink_ref, o_ref, lse_ref, dz_ref, oh_ref,
             dqg_ref, dkv_ref, dt5_ref, dsink_ref, gt_acc, b_ref):
        kvh = pl.program_id(0)
        j = pl.program_id(1)
        _fill_bias(B_GROUP, lambda h: w_ref[h:h + 1, :], B_BAND, b_ref, j)

        @pl.when(jnp.logical_and(kvh == 0, j == 0))
        def _():
            dkv_ref[...] = jnp.zeros_like(dkv_ref)

        @pl.when(j == 0)
        def _():
            gt_acc[...] = jnp.zeros_like(gt_acc)
            dsink_ref[...] = jnp.zeros_like(dsink_ref)

        lane_lo = _lane_lo()

        def attend(with_prev):
            first_col = 0 if with_prev else B_PREV
            k_lo, k_hi, v_lo, v_hi = _b_operands(kp, kc, vp, vc, kvh, with_prev)
            dk_blk = jnp.zeros((keys - first_col, 128), F32)
            dv_blk = jnp.zeros((keys - first_col, 128), F32)
            for pp in range(B_GROUP // 2):
                cols = slice(128 * pp, 128 * (pp + 1))
                qp = q_ref[:, cols]
                o = o_ref[:, cols].astype(F32)
                lse_pair = lse_ref[:, cols]
                dzf = dz_ref[:, cols].astype(F32)
                silu, dsilu = _silu_parts(g_ref[:, cols].astype(F32))
                do = dzf * silu
                dqg_ref[1, :, cols] = (dzf * o * dsilu).astype(BF16)
                doo = do * o
                dqs = []
                for hh in range(2):
                    g = 2 * pp + hh
                    sel = lane_lo if hh == 0 else jnp.logical_not(lane_lo)
                    sink = sink_ref[kvh * B_GROUP + g]
                    kk = k_lo if hh == 0 else k_hi
                    vv = v_lo if hh == 0 else v_hi
                    qm = jnp.where(sel, qp, jnp.zeros_like(qp)) * SCALE
                    s = _dot_nt(qm, kk) + b_ref[g, :, first_col:]
                    lse_h = _own_everywhere(lse_pair, sel)
                    p = jnp.exp(_minus_rows(s, lse_h))
                    delta = jnp.sum(jnp.where(sel, doo, 0.0), axis=-1, keepdims=True)
                    dom = jnp.where(sel, do, 0.0).astype(BF16)
                    dp = _dot_nt(dom, vv)
                    ds = p * (dp - delta)
                    gt_acc[g, :, first_col:] += ds
                    dsink_ref[g:g + 1, :] -= jnp.sum(jnp.exp(sink - lse_h) * delta, axis=0, keepdims=True)
                    dsb = ds.astype(BF16)
                    dqs.append(_dot(dsb, kk) * SCALE)
                    dk_blk = dk_blk + _dot_tn(dsb, qm)
                    dv_blk = dv_blk + _dot_tn(p.astype(BF16), dom)
                dqg_ref[0, :, cols] = jnp.where(lane_lo, dqs[0], dqs[1]).astype(BF16)
            mine = lane_lo == (kvh == 0)
            dk_add = jnp.where(mine, dk_blk + pltpu.roll(dk_blk, HEAD_DIM, 1), 0.0)
            dv_add = jnp.where(mine, dv_blk + pltpu.roll(dv_blk, HEAD_DIM, 1), 0.0)
            first_key = B_PREV if with_prev else 0
            if with_prev:
                rows = pl.ds(pl.multiple_of(j * qblk - B_PREV, B_PREV), B_PREV)
                dkv_ref[0, rows, :] += dk_add[0:B_PREV]
                dkv_ref[1, rows, :] += dv_add[0:B_PREV]
            rows = pl.ds(pl.multiple_of(j * qblk, qblk), qblk)
            dkv_ref[0, rows, :] += dk_add[first_key:first_key + qblk]
            dkv_ref[1, rows, :] += dv_add[first_key:first_key + qblk]

        pl.when(j == 0)(functools.partial(attend, False))
        pl.when(j >= 1)(functools.partial(attend, True))

        @pl.when(j == nq - 1)
        def _():
            dd = jnp.concatenate([_offset_sums(_collapse_chunks(gt_acc[g], keys)) for g in range(B_GROUP)], axis=0)
            hi = dd.astype(BF16)
            lo = (dd - hi.astype(F32)).astype(BF16)
            dt5_ref[...] = _dot(hi, oh_ref[...]) + _dot(lo, oh_ref[...])

    blk = pl.BlockSpec((qblk, 512), lambda h, j: (j, h))
    return pl.pallas_call(
        body, name="attn_b_bwd", grid=(B_KV_HEADS, nq),
        in_specs=[q_spec, g_spec, kp_spec, kc_spec, vp_spec, vc_spec, bias_spec, sink_spec, blk, blk, blk,
                  pl.BlockSpec((keys, 128), lambda h, j: (0, 0))],
        out_specs=[pl.BlockSpec((2, qblk, 512), lambda h, j: (0, j, h)),
                   pl.BlockSpec((2, t, 128), lambda h, j: (0, 0, 0)),
                   pl.BlockSpec((B_GROUP, 128), lambda h, j: (h, 0)),
                   pl.BlockSpec((B_GROUP, 128), lambda h, j: (h, 0))],
        out_shape=[SDS((2, t, D_MODEL), BF16), SDS((2, t, 128), F32),
                   SDS((N_HEADS, 128), F32), SDS((N_HEADS, 128), F32)],
        scratch_shapes=[pltpu.VMEM((B_GROUP, qblk, keys), F32), pltpu.VMEM((B_GROUP, qblk, keys), F32)],
        compiler_params=_cparams(),
    )(qg, qg, kv, kv, kv, kv, bias, sinks, out_b, lse, dz, bucket_onehot)


def _a_bias_by_offset(rel_bias):
    m = np.arange(A_DIAG)
    idx = np.clip(A_BAND - 1 - m, -A_REL_CLIP, A_REL_CLIP) + A_REL_CLIP
    by_head = rel_bias[idx].T.reshape(N_HEADS // 2, 2, A_DIAG)
    return jnp.concatenate([by_head, jnp.zeros((N_HEADS // 2, 6, A_DIAG), F32)], axis=1)


def _a_bias_grad(offset_sums):
    first = 319
    tail = jnp.sum(offset_sums[:, :first], axis=1)
    body = jnp.flip(offset_sums[:, first:first + 320], axis=1)
    body = body.at[:, -1].add(tail)
    full = jnp.concatenate([jnp.zeros((N_HEADS, 193), F32), body], axis=1)
    return full


def _t5_bucket(rel):
    nb = T5_BUCKETS // 2
    max_exact = nb // 2
    ret = jnp.where(rel > 0, nb, 0)
    n = jnp.abs(rel)
    nf = jnp.maximum(n, 1).astype(jnp.float32)
    large = max_exact + (jnp.log(nf / max_exact) / math.log(T5_MAX_DIST / max_exact)
                         * (nb - max_exact)).astype(jnp.int32)
    large = jnp.minimum(large, nb - 1)
    return ret + jnp.where(n < max_exact, n, large)


def _b_offset_buckets(keys):
    return _t5_bucket(jnp.arange(keys, dtype=jnp.int32) - (B_LEFT_CHUNKS * CHUNK + CHUNK - 1))


def _b_bias_by_offset(t5_table, keys):
    return t5_table[_b_offset_buckets(keys)].T


def _b_bucket_onehot(keys):
    return (_b_offset_buckets(keys)[:, None] == jnp.arange(128)[None, :]).astype(BF16)


def _local_step(my_slot, order, x, target, a_gain_shard, w_in_a_shard, rel_bias, late_shards, kv_gain,
                t5_table, b_gain, sinks, f_gain):
    a_bias = _a_bias_by_offset(rel_bias)
    b_bias_fwd = _b_bias_by_offset(t5_table, B_QBLK_FWD + B_PREV)
    b_bias_bwd = _b_bias_by_offset(t5_table, B_QBLK_BWD + B_PREV)
    sinks_flat = sinks.reshape(N_HEADS)

    xn, qkvg, w_in_a, a_gain = _norm_matmul_gather(order, x, a_gain_shard, w_in_a_shard)
    z_a, out_a, lse_a, (w_in_b, w_out_a, w_out_b, kv_w) = _attn_a_fwd(qkvg, a_bias, late_shards)
    w_out_a = w_out_a.reshape(D_MODEL, D_MODEL)
    w_out_b = w_out_b.reshape(D_MODEL, D_MODEL)
    kv_w = kv_w.reshape(D_MODEL, 2 * 128)
    h1, kvn, hb, kv, qg = _layer_a_out(x, z_a, w_out_a, kv_gain, b_gain, kv_w, w_in_b)
    z_b, out_b, lse_b = _attn_b_fwd(qg, kv, b_bias_fwd, sinks_flat)
    dh2, dh2b, dz_b, loss, d_fn = _layer_b_out_loss(h1, z_b, w_out_b, f_gain, target)

    dqg_b, dkv_b, d_t5, d_sink = _attn_b_bwd(qg, kv, b_bias_bwd, sinks_flat, out_b, lse_b, dz_b,
                                             _b_bucket_onehot(B_QBLK_BWD + B_PREV))
    dh1, dh1b, dz_a, d_bn, d_kn = _layer_b_in_bwd(dqg_b, dkv_b, w_in_b, kv_w, h1, dh2, b_gain, kv_gain, w_out_a)
    early = dict(
        b_w_out=_weight_grad_rows("grad_b_w_out", my_slot, z_b, dh2b[None]),
        b_w_in=_weight_grad_cols("grad_b_w_in", my_slot, hb, [dqg_b],
                                 [(0, o, c, 4 * o + c) for o in range(2) for c in range(4)], 256),
        kv_w=_weight_grad_rows("grad_kv_w", my_slot, kvn, dkv_b),
        a_w_out=_weight_grad_rows("grad_a_w_out", my_slot, z_a, dh1b[None]))
    dqg_a, dkv_a, d_rel, landed = _attn_a_bwd(qkvg, a_bias, out_a, lse_a, dz_a, [g[0] for g in early.values()])
    g_w_in_a = _weight_grad_cols(
        "grad_a_w_in", my_slot, xn, [dqg_a, dkv_a],
        [(0, 0, 0, 0), (0, 0, 1, 1), (1, 0, 0, 2), (1, 0, 1, 3), (1, 1, 0, 4), (1, 1, 1, 5), (0, 1, 0, 6), (0, 1, 1, 7)], 512)
    chip_sums, from_sibling = _chip_sums(g_w_in_a[0])
    grad_x, d_an, from_chips = _layer_a_in_bwd(dqg_a, dkv_a, w_in_a, x, dh1, a_gain, chip_sums)

    matrices = {n: (g[1], [(land, 0, N_DEV - 1)]) for (n, g), land in zip(early.items(), landed)}
    matrices["a_w_in"] = (g_w_in_a[1], [(from_sibling, 0, 1), (from_chips, 0, 3)])
    small = dict(
        loss=loss, a_norm=d_an, a_rel_bias=d_rel[:, :2].reshape(N_HEADS, A_DIAG),
        kv_norm=d_kn, t5_bias=d_t5, b_norm=d_bn, b_sinks=d_sink, final_norm=d_fn)
    return grad_x, small, matrices


def _place():
    x, y, c = lax.axis_index("x"), lax.axis_index("y"), lax.axis_index("c")
    chips = [(1 - x, y), (x, 1 - y), (1 - x, 1 - y)]
    return x, y, c, chips


def _slot(px, py, pc):
    return 4 * px + 2 * py + pc


ANY = pl.BlockSpec(memory_space=pl.ANY)


def _peer(x, y, c, k):
    return (x ^ (k >> 2), y ^ ((k >> 1) & 1), c ^ (k & 1))


def _scatter_copies(grad_refs, land_refs, send_sems, recv_sems):
    x, y, c, _ = _place()
    copies = []
    for t, (grad, land) in enumerate(zip(grad_refs, land_refs)):
        for k in range(1, N_DEV):
            peer = _peer(x, y, c, k)
            sem = (N_DEV - 1) * t + k - 1
            copies.append(pltpu.make_async_remote_copy(
                src_ref=grad.at[_slot(*peer)], dst_ref=land.at[k - 1],
                send_sem=send_sems.at[sem], recv_sem=recv_sems.at[sem],
                device_id=peer, device_id_type=MESH))
    return copies


def _gather_phases(ins, outs, send_sems, recv_sems, local_sems):
    n = len(ins)
    x, y, c, chips = _place()
    me, sibling = (x, y, c), (x, y, 1 - c)

    def copy(t, k, block, to, src=None):
        dst = outs[t].at[_slot(*block)]
        return pltpu.make_async_remote_copy(
            src_ref=dst if src is None else src, dst_ref=dst,
            send_sem=send_sems.at[7 * t + k], recv_sem=recv_sems.at[7 * t + k],
            device_id=to, device_id_type=MESH)

    def lists():
        mine = [pltpu.make_async_copy(ins[t], outs[t].at[_slot(*me)], local_sems.at[t]) for t in range(n)]
        first = []
        for t in range(n):
            first.append(copy(t, 0, me, sibling, src=ins[t]))
            first += [copy(t, 1 + j, me, (*chip, c), src=ins[t]) for j, chip in enumerate(chips)]
        passed = [copy(t, 4 + j, (*chip, c), sibling) for t in range(n) for j, chip in enumerate(chips)]
        return mine, first, passed

    def start():
        mine, first, _ = lists()
        for cp in mine + first:
            cp.start()

    def forward():
        _, _, passed = lists()
        for t in range(n):
            for j, chip in enumerate(chips):
                copy(t, 1 + j, (*chip, c), me).wait_recv()
                passed[3 * t + j].start()

    def finish():
        mine, first, passed = lists()
        for t in range(n):
            copy(t, 0, sibling, me).wait_recv()
            for j, chip in enumerate(chips):
                copy(t, 4 + j, (*chip, 1 - c), me).wait_recv()
        for cp in first + passed:
            cp.wait_send()
        for cp in mine:
            cp.wait()

    return start, forward, finish


def _gather_scratch(n):
    return [pltpu.SemaphoreType.DMA((7 * n,)), pltpu.SemaphoreType.DMA((7 * n,)), pltpu.SemaphoreType.DMA((n,))]


def _chip_sums(g):
    _, r, c = g.shape

    def body(g_ref, sums_ref, mine_ref, land, own, send_sems, recv_sems, load_sems):
        x, y, c_i, chips = _place()
        sibling = (x, y, 1 - c_i)
        blocks = [(*chip, 1 - c_i) for chip in chips] + [sibling]
        sends = [pltpu.make_async_remote_copy(
            src_ref=g_ref.at[_slot(*block)], dst_ref=land.at[k], send_sem=send_sems.at[k],
            recv_sem=recv_sems.at[k], device_id=sibling, device_id_type=MESH) for k, block in enumerate(blocks)]
        loads = [pltpu.make_async_copy(g_ref.at[_slot(*chip, c_i)], own.at[j], load_sems.at[j])
                 for j, chip in enumerate(chips)]
        for cp in sends + loads:
            cp.start()
        for cp in sends + loads:
            cp.wait()
        for j in range(3):
            sums_ref[j] = (own[j].astype(F32) + land[j].astype(F32)).astype(BF16)
        mine_ref[0] = land[3]

    return pl.pallas_call(
        body, name="chip_sums",
        in_specs=[ANY], out_specs=[VM, VM],
        out_shape=[SDS((3, r, c), BF16), SDS((1, r, c), BF16)],
        scratch_shapes=[pltpu.VMEM((4, r, c), BF16), pltpu.VMEM((3, r, c), BF16),
                        pltpu.SemaphoreType.DMA((4,)), pltpu.SemaphoreType.DMA((4,)), pltpu.SemaphoreType.DMA((3,))],
        compiler_params=_cparams(),
    )(g)


def _chip_copies(sums_ref, land_ref, send_sems, recv_sems):
    x, y, c, chips = _place()
    del x, y
    return [pltpu.make_async_remote_copy(
        src_ref=sums_ref.at[j], dst_ref=land_ref.at[j], send_sem=send_sems.at[j], recv_sem=recv_sems.at[j],
        device_id=(*chip, c), device_id_type=MESH) for j, chip in enumerate(chips)]


def _row_tile(rows):
    return min(rows, 256)


def _adamw(w, g, m, v):
    m2 = ADAM_B1 * m + (1.0 - ADAM_B1) * g
    v2 = ADAM_B2 * v + (1.0 - ADAM_B2) * jnp.square(g)
    m_hat = m2 / (1.0 - ADAM_B1 ** ADAM_STEP)
    v_hat = v2 / (1.0 - ADAM_B2 ** ADAM_STEP)
    delta = -ADAM_LR * (m_hat / (jnp.sqrt(v_hat) + ADAM_EPS) + ADAM_WD * w)
    return delta, m2, v2


def _reduce_adamw(name, own, partials, w, m, v):
    r, c = own.shape
    tr = _row_tile(r)
    n_p = len(partials)

    def body(own_ref, *rest):
        p_refs, (w_ref, m_ref, v_ref, grad_ref, d_ref, nm_ref, nv_ref) = rest[:n_p], rest[n_p:]
        grad = own_ref[...]
        for p_ref, (_, _, count) in zip(p_refs, partials):
            for j in range(count):
                grad = grad + p_ref[j].astype(F32)
        grad_ref[...] = grad
        d_ref[...], nm_ref[...], nv_ref[...] = _adamw(w_ref[...], grad, m_ref[...], v_ref[...])

    flat = pl.BlockSpec((tr, c), lambda i: (i, 0))
    return pl.pallas_call(
        body, name=name, grid=(r // tr,),
        in_specs=[flat] + [pl.BlockSpec((count, tr, c), lambda i, first=first, count=count: (first // count, i, 0))
                           for _, first, count in partials] + [flat, flat, flat],
        out_specs=[flat, flat, flat, flat],
        out_shape=[SDS((r, c), F32)] * 4,
        compiler_params=_cparams(),
    )(own, *[p[0] for p in partials], w, m, v)


VM = pl.BlockSpec()


def _small_allreduce(parts):
    n = len(parts)

    def body(*refs):
        ins, outs, lands = refs[:n], refs[n:2 * n], refs[2 * n:3 * n]
        send_sems, recv_sems = refs[3 * n:]
        x, y, c, _ = _place()
        my_slot = _slot(x, y, c)
        copies = []
        for t in range(n):
            lands[t][my_slot] = ins[t][...]
            for k in range(1, N_DEV):
                sem = (N_DEV - 1) * t + k - 1
                copies.append(pltpu.make_async_remote_copy(
                    src_ref=ins[t], dst_ref=lands[t].at[my_slot],
                    send_sem=send_sems.at[sem], recv_sem=recv_sems.at[sem],
                    device_id=_peer(x, y, c, k), device_id_type=MESH))
        for cp in copies:
            cp.start()
        for t in range(n):
            for k in range(1, N_DEV):
                sem = (N_DEV - 1) * t + k - 1
                pltpu.make_async_remote_copy(
                    src_ref=ins[t], dst_ref=lands[t].at[_slot(*_peer(x, y, c, k))],
                    send_sem=send_sems.at[sem], recv_sem=recv_sems.at[sem],
                    device_id=(x, y, c), device_id_type=MESH).wait_recv()
        for cp in copies:
            cp.wait_send()
        for t in range(n):
            total = lands[t][0]
            for s in range(1, N_DEV):
                total = total + lands[t][s]
            outs[t][...] = total

    n_sems = (N_DEV - 1) * n
    return pl.pallas_call(
        body, name="small_allreduce",
        in_specs=[VM] * n, out_specs=[VM] * n, out_shape=[SDS(p.shape, F32) for p in parts],
        scratch_shapes=[pltpu.VMEM((N_DEV, *p.shape), F32) for p in parts]
        + [pltpu.SemaphoreType.DMA((n_sems,)), pltpu.SemaphoreType.DMA((n_sems,))],
    )(*parts)


def _small_adamw(my_slot, sums, ws, ms, vs):
    n = len(ws)

    def body(slot_ref, *refs):
        sum_refs, refs = refs[:n + 1], refs[n + 1:]
        w_refs, m_refs, v_refs, refs = refs[:n], refs[n:2 * n], refs[2 * n:3 * n], refs[3 * n:]
        g_refs, d_refs, nm_refs, nv_refs = refs[:n + 1], refs[n + 1:2 * n + 1], refs[2 * n + 1:3 * n + 1], refs[3 * n + 1:]
        for t in range(n + 1):
            if t == 0:
                g = sum_refs[0][:, pl.ds(pl.multiple_of(slot_ref[0] * 128, 128), 128)]
            else:
                g = sum_refs[t][...]
            g_refs[t][...] = g
            if t < n:
                d_refs[t][...], nm_refs[t][...], nv_refs[t][...] = _adamw(w_refs[t][...], g, m_refs[t][...], v_refs[t][...])

    shapes = [SDS(w.shape, F32) for w in ws]
    outs = pl.pallas_call(
        body, name="small_adamw",
        in_specs=[pl.BlockSpec(memory_space=pltpu.SMEM)] + [VM] * (4 * n + 1),
        out_specs=[VM] * (4 * n + 1),
        out_shape=shapes + [SDS(sums[-1].shape, F32)] + shapes * 3,
    )(my_slot, *sums, *ws, *ms, *vs)
    return outs[:n + 1], outs[n + 1:2 * n + 1], outs[2 * n + 1:3 * n + 1], outs[3 * n + 1:]


def kernel(x, a_norm, a_w_in, a_rel_bias, a_w_out, kv_norm, kv_w, t5_bias, b_norm, b_w_in, b_sinks, b_w_out, final_norm, loss_target, m_a_norm, m_a_w_in, m_a_rel_bias, m_a_w_out, m_kv_norm, m_kv_w, m_t5_bias, m_b_norm, m_b_w_in, m_b_sinks, m_b_w_out, m_final_norm, v_a_norm, v_a_w_in, v_a_rel_bias, v_a_w_out, v_kv_norm, v_kv_w, v_t5_bias, v_b_norm, v_b_w_in, v_b_sinks, v_b_w_out, v_final_norm):
    xi, yi, ci = lax.axis_index("x"), lax.axis_index("y"), lax.axis_index("c")
    my_slot = _slot(xi, yi, ci)

    slot_arr = jnp.reshape(my_slot, (1,)).astype(jnp.int32)
    order = _gather_order(xi, yi, ci)
    late_shards = [b_w_in[0].astype(BF16), a_w_out[0].astype(BF16), b_w_out[0].astype(BF16), kv_w.astype(BF16)]
    grad_x, loc, matrices = _local_step(
        slot_arr, order, x[0], loss_target[0], a_norm, a_w_in[0].astype(BF16), a_rel_bias[0], late_shards,
        kv_norm.reshape(1, D_MODEL), t5_bias, b_norm, b_sinks, final_norm.reshape(1, D_MODEL))

    shard_w = dict(a_w_in=a_w_in[0], b_w_in=b_w_in[0], a_w_out=a_w_out[0], b_w_out=b_w_out[0], kv_w=kv_w)
    shard_m = dict(a_w_in=m_a_w_in[0], b_w_in=m_b_w_in[0], a_w_out=m_a_w_out[0], b_w_out=m_b_w_out[0], kv_w=m_kv_w)
    shard_v = dict(a_w_in=v_a_w_in[0], b_w_in=v_b_w_in[0], a_w_out=v_a_w_out[0], b_w_out=v_b_w_out[0], kv_w=v_kv_w)
    big = {n: _reduce_adamw("adamw_" + n, own, partials, shard_w[n], shard_m[n], shard_v[n])
           for n, (own, partials) in matrices.items()}

    names = ("a_norm", "a_rel_bias", "kv_norm", "t5_bias", "b_norm", "b_sinks", "final_norm")
    tables = ("a_rel_bias", "t5_bias")

    def row(n, a):
        return a.reshape(-1, a.shape[-1]).T if n in tables else a.reshape(1, -1)

    small_w = [row(n, a) for n, a in zip(names, (a_norm, a_rel_bias, kv_norm, t5_bias, b_norm, b_sinks, final_norm))]
    small_m = [row(n, a) for n, a in zip(names, (m_a_norm, m_a_rel_bias, m_kv_norm, m_t5_bias, m_b_norm, m_b_sinks,
                                                 m_final_norm))]
    small_v = [row(n, a) for n, a in zip(names, (v_a_norm, v_a_rel_bias, v_kv_norm, v_t5_bias, v_b_norm, v_b_sinks,
                                                 v_final_norm))]
    sums = dict(zip(names + ("loss",), _small_allreduce([loc[n] for n in names] + [loc["loss"]])))
    sums["a_rel_bias"] = _a_bias_grad(sums["a_rel_bias"])
    sums["t5_bias"] = sums["t5_bias"][:, :T5_BUCKETS]
    sums["b_sinks"] = sums["b_sinks"][:, 0].reshape(1, N_HEADS)
    results = _small_adamw(slot_arr, [sums[n] for n in names + ("loss",)], small_w, small_m, small_v)
    like = dict(a_norm=a_norm, a_rel_bias=a_rel_bias, kv_norm=kv_norm, t5_bias=t5_bias, b_norm=b_norm,
                b_sinks=b_sinks, final_norm=final_norm)
    sm = [{n: (part[i].T if n in tables else part[i]).reshape(like[n].shape) for i, n in enumerate(names)}
          for part in results]
    loss = results[0][len(names)][0, 0]

    order = ("a_norm", "a_w_in", "a_rel_bias", "a_w_out", "kv_norm", "kv_w", "t5_bias", "b_norm",
             "b_w_in", "b_sinks", "b_w_out", "final_norm")
    lead = dict(a_w_in=True, b_w_in=True, a_w_out=True, b_w_out=True, kv_w=False)

    def pick(kind, name):
        if name in big:
            val = big[name][kind]
            return val[None] if lead[name] else val
        return sm[kind][name]

    outs = [loss, grad_x[None]]
    for kind in range(4):
        outs += [pick(kind, n) for n in order]
    return tuple(outs)
```

```python
import functools
import math

import numpy as np
import jax
import jax.numpy as jnp
from jax import lax
from jax.experimental import pallas as pl
from jax.experimental.pallas import tpu as pltpu

F32 = jnp.float32
BF16 = jnp.bfloat16
SDS = jax.ShapeDtypeStruct

D_MODEL = 1024
HEAD_DIM = 64
CHUNK = 64
N_HEADS = 16
RMS_EPS = 1e-6
A_LEFT_CHUNKS = 8
A_BAND = (A_LEFT_CHUNKS + 1) * CHUNK
A_REL_CLIP = 256
B_KV_HEADS = 2
B_GROUP = 8
B_LEFT_CHUNKS = 2
B_BAND = (B_LEFT_CHUNKS + 1) * CHUNK
T5_BUCKETS = 32
T5_MAX_DIST = 128
QBLK = 256
A_KEYS = 3 * QBLK
B_QBLK_FWD = 128
B_QBLK_BWD = 256
B_PREV = 128
A_DIAG = A_KEYS
NEG = -1e30
SCALE = HEAD_DIM ** -0.5
N_DEV = 8

ADAM_LR = 0.001
ADAM_B1 = 0.9
ADAM_B2 = 0.999
ADAM_EPS = 1e-08
ADAM_WD = 0.01
ADAM_STEP = 10

VMEM_LIMIT_BYTES = 56 * 1024 * 1024
MESH = pl.DeviceIdType.MESH


def _cparams():
    return pltpu.CompilerParams(vmem_limit_bytes=VMEM_LIMIT_BYTES)


def _dot(a, b):
    return jnp.dot(a, b, preferred_element_type=F32)


def _dot_nt(a, b):
    return lax.dot_general(a, b, (((1,), (1,)), ((), ())), preferred_element_type=F32)


def _dot_tn(a, b):
    return lax.dot_general(a, b, (((0,), (0,)), ((), ())), preferred_element_type=F32)


def _rstd(xf):
    return lax.rsqrt(jnp.mean(xf * xf, axis=-1, keepdims=True) + RMS_EPS)


def _sigmoid(x):
    return 1.0 / (1.0 + jnp.exp(-x))


_GATHER_SEQUENCE = ((0, None), (1, 0), (2, 1), (4, None), (5, None), (3, 2), (6, None))


def _gather_order(x, y, c):
    others = [(1 - x, y), (x, 1 - y), (1 - x, 1 - y)]
    arrivals = [_slot(x, y, 1 - c)] + [_slot(*chip, c) for chip in others] + [_slot(*chip, 1 - c) for chip in others]
    return jnp.stack([_slot(x, y, c)] + [arrivals[a] for a, _ in _GATHER_SEQUENCE]).astype(jnp.int32)


def _norm_matmul_gather(order, x, gain_shard, w_shard):
    t = x.shape[0]
    dw, tn = w_shard.shape
    tm = min(t, 1024)
    n_m = t // tm

    def body(order_ref, x_ref, gs_ref, shard_ref, xn_ref, o_ref, full_ref, gain_ref,
             xn_all, wbuf, gland, send_sems, recv_sems, gsend_sems, grecv_sems, load_sems, own_sem):
        n, m = pl.program_id(0), pl.program_id(1)
        x_i, y_i, c_i, chips = _place()
        me, sibling = (x_i, y_i, c_i), (x_i, y_i, 1 - c_i)

        def send(k, block, to, src=None):
            dst = full_ref.at[_slot(*block)]
            return pltpu.make_async_remote_copy(
                src_ref=dst if src is None else src, dst_ref=dst,
                send_sem=send_sems.at[k], recv_sem=recv_sems.at[k], device_id=to, device_id_type=MESH)

        own = pltpu.make_async_copy(shard_ref, full_ref.at[_slot(*me)], own_sem)
        first = [send(0, me, sibling, src=shard_ref)]
        first += [send(1 + j, me, (*chip, c_i), src=shard_ref) for j, chip in enumerate(chips)]
        forwards = [send(4 + j, (*chip, c_i), sibling) for j, chip in enumerate(chips)]
        arrivals = [send(0, sibling, me)] + [send(1 + j, (*chip, c_i), me) for j, chip in enumerate(chips)]
        arrivals += [send(4 + j, (*chip, 1 - c_i), me) for j, chip in enumerate(chips)]
        gains = [pltpu.make_async_remote_copy(
            src_ref=gs_ref, dst_ref=gland.at[_slot(*me)], send_sem=gsend_sems.at[k - 1],
            recv_sem=grecv_sems.at[k - 1], device_id=_peer(x_i, y_i, c_i, k), device_id_type=MESH)
            for k in range(1, N_DEV)]

        @pl.when(jnp.logical_and(n == 0, m == 0))
        def _():
            own.start()
            for cp in gains + first:
                cp.start()
            pltpu.make_async_copy(shard_ref, wbuf.at[0], load_sems.at[0]).start()
            gland[_slot(*me)] = gs_ref[...]
            for k in range(1, N_DEV):
                pltpu.make_async_remote_copy(
                    src_ref=gs_ref, dst_ref=gland.at[_slot(*_peer(x_i, y_i, c_i, k))],
                    send_sem=gsend_sems.at[k - 1], recv_sem=grecv_sems.at[k - 1],
                    device_id=me, device_id_type=MESH).wait_recv()
            for s in range(N_DEV):
                gain_ref[:, 128 * s:128 * (s + 1)] = gland[s]

        rows = pl.ds(pl.multiple_of(m * tm, tm), tm)

        @pl.when(n == 0)
        def _():
            xf = x_ref[...]
            xn = ((xf * _rstd(xf)) * gain_ref[...]).astype(BF16)
            xn_all[rows, :] = xn
            xn_ref[...] = xn

        @pl.when(m == 0)
        def _():
            pltpu.make_async_copy(full_ref.at[0], wbuf.at[n % 2], load_sems.at[n % 2]).wait()

        o_ref[...] = _dot(xn_all[rows, :], wbuf[n % 2]).astype(BF16)

        for k, (arrival, forward) in enumerate(_GATHER_SEQUENCE):
            @pl.when(jnp.logical_and(n == k, m == n_m - 1))
            def _(k=k, arrival=arrival, forward=forward):
                arrivals[arrival].wait_recv()
                if forward is not None:
                    forwards[forward].start()
                pltpu.make_async_copy(full_ref.at[order_ref[k + 1]], wbuf.at[(k + 1) % 2],
                                      load_sems.at[(k + 1) % 2]).start()

        @pl.when(jnp.logical_and(n == N_DEV - 1, m == n_m - 1))
        def _():
            for cp in gains + first + forwards:
                cp.wait_send()
            own.wait()

    held = lambda n, m, order: (jnp.where(n == 0, m, n_m - 1), 0)
    return pl.pallas_call(
        body, name="norm_matmul_gather",
        grid_spec=pltpu.PrefetchScalarGridSpec(
            num_scalar_prefetch=1, grid=(N_DEV, n_m),
            in_specs=[pl.BlockSpec((tm, D_MODEL), held),
                      pl.BlockSpec((1, 128), lambda n, m, order: (0, 0)), ANY],
            out_specs=[pl.BlockSpec((tm, D_MODEL), held),
                       pl.BlockSpec((tm, tn), lambda n, m, order: (m, order[n])),
                       ANY, pl.BlockSpec((1, D_MODEL), lambda n, m, order: (0, 0))],
            scratch_shapes=[pltpu.VMEM((t, D_MODEL), BF16), pltpu.VMEM((2, dw, tn), BF16),
                            pltpu.VMEM((N_DEV, 1, 128), F32),
                            pltpu.SemaphoreType.DMA((7,)), pltpu.SemaphoreType.DMA((7,)),
                            pltpu.SemaphoreType.DMA((7,)), pltpu.SemaphoreType.DMA((7,)),
                            pltpu.SemaphoreType.DMA((2,)), pltpu.SemaphoreType.DMA]),
        out_shape=[SDS((t, D_MODEL), BF16), SDS((t, N_DEV * tn), BF16), SDS((N_DEV, dw, tn), BF16),
                   SDS((1, D_MODEL), F32)],
        compiler_params=_cparams(),
    )(order, x, gain_shard, w_shard)


def _layer_a_out(x, z, w_out, kv_gain, b_gain, kv_w, w_in_b):
    t = x.shape[0]
    tm = min(t, 512)
    nb, _, tn = w_in_b.shape

    def body(x_ref, z_ref, wo_ref, kvg_ref, bg_ref, kvw_ref, wb_ref,
             h1_ref, kvn_ref, hb_ref, kv_ref, qg_ref):
        h1 = x_ref[...] + _dot(z_ref[...], wo_ref[...])
        h1_ref[...] = h1
        y0 = h1 * _rstd(h1)
        kvn = (y0 * kvg_ref[...]).astype(BF16)
        hb = (y0 * bg_ref[...]).astype(BF16)
        kvn_ref[...] = kvn
        hb_ref[...] = hb
        kv_ref[...] = _dot(kvn, kvw_ref[...]).astype(BF16)
        for i in range(nb):
            qg_ref[:, i * tn:(i + 1) * tn] = _dot(hb, wb_ref[i]).astype(BF16)

    row = lambda m: (m, 0)
    fix2 = lambda m: (0, 0)
    return pl.pallas_call(
        body, name="layer_a_out", grid=(t // tm,),
        in_specs=[pl.BlockSpec((tm, D_MODEL), row), pl.BlockSpec((tm, D_MODEL), row),
                  pl.BlockSpec((D_MODEL, D_MODEL), fix2),
                  pl.BlockSpec((1, D_MODEL), fix2), pl.BlockSpec((1, D_MODEL), fix2),
                  pl.BlockSpec((D_MODEL, 256), fix2),
                  pl.BlockSpec((nb, D_MODEL, tn), lambda m: (0, 0, 0))],
        out_specs=[pl.BlockSpec((tm, D_MODEL), row), pl.BlockSpec((tm, D_MODEL), row),
                   pl.BlockSpec((tm, D_MODEL), row), pl.BlockSpec((tm, 256), row),
                   pl.BlockSpec((tm, nb * tn), row)],
        out_shape=[SDS((t, D_MODEL), F32), SDS((t, D_MODEL), BF16), SDS((t, D_MODEL), BF16),
                   SDS((t, 256), BF16), SDS((t, nb * tn), BF16)],
        compiler_params=_cparams(),
    )(x, z, w_out, kv_gain, b_gain, kv_w, w_in_b)


def _layer_b_out_loss(h1, z, w_out, f_gain, target):
    t = h1.shape[0]
    tm = min(t, 512)

    def body(h1_ref, z_ref, wo_ref, fg_ref, tgt_ref,
             dh2_ref, dh2b_ref, dz_ref, loss_ref, dfn_ref):
        @pl.when(pl.program_id(0) == 0)
        def _():
            loss_ref[...] = jnp.zeros_like(loss_ref)
            dfn_ref[...] = jnp.zeros_like(dfn_ref)

        h2 = h1_ref[...] + _dot(z_ref[...], wo_ref[...])
        r = _rstd(h2)
        yn = h2 * r
        fg = fg_ref[...]
        err = yn * fg - tgt_ref[...]
        loss_ref[...] += (0.5 / D_MODEL) * jnp.sum(err * err)
        dy = err * (1.0 / D_MODEL)
        dfn_ref[...] += jnp.sum(dy * yn, axis=0, keepdims=True)
        u = dy * fg
        dh2 = r * u - h2 * ((r * r * r) * jnp.mean(u * h2, axis=-1, keepdims=True))
        dh2_ref[...] = dh2
        dh2b = dh2.astype(BF16)
        dh2b_ref[...] = dh2b
        dz_ref[...] = _dot_nt(dh2b, wo_ref[...]).astype(BF16)

    row = lambda m: (m, 0)
    fix2 = lambda m: (0, 0)
    return pl.pallas_call(
        body, name="layer_b_out_loss", grid=(t // tm,),
        in_specs=[pl.BlockSpec((tm, D_MODEL), row), pl.BlockSpec((tm, D_MODEL), row),
                  pl.BlockSpec((D_MODEL, D_MODEL), fix2), pl.BlockSpec((1, D_MODEL), fix2),
                  pl.BlockSpec((tm, D_MODEL), row)],
        out_specs=[pl.BlockSpec((tm, D_MODEL), row), pl.BlockSpec((tm, D_MODEL), row),
                   pl.BlockSpec((tm, D_MODEL), row), pl.BlockSpec((1, 128), fix2),
                   pl.BlockSpec((1, D_MODEL), fix2)],
        out_shape=[SDS((t, D_MODEL), F32), SDS((t, D_MODEL), BF16), SDS((t, D_MODEL), BF16),
                   SDS((1, 128), F32), SDS((1, D_MODEL), F32)],
        compiler_params=_cparams(),
    )(h1, z, w_out, f_gain, target)


def _layer_b_in_bwd(dqg, dkv, w_in_b, kv_w, h1, dh2, b_gain, kv_gain, w_out_a):
    t = h1.shape[0]
    tm = min(t, 512)
    nb, _, tn = w_in_b.shape
    per = D_MODEL // tn

    def body(dqg_ref, dkv_ref, wb_ref, kvw_ref, h1_ref, dh2_ref, bg_ref, kvg_ref, wo_ref,
             dh1_ref, dh1b_ref, dz_ref, dbn_ref, dkn_ref):
        @pl.when(pl.program_id(0) == 0)
        def _():
            dbn_ref[...] = jnp.zeros_like(dbn_ref)
            dkn_ref[...] = jnp.zeros_like(dkn_ref)

        dhb = jnp.zeros((tm, D_MODEL), F32)
        for i in range(nb):
            blk = dqg_ref[i // per, :, (i % per) * tn:(i % per + 1) * tn]
            dhb = dhb + _dot_nt(blk, wb_ref[i])
        dkn = (_dot_nt(dkv_ref[0].astype(BF16), kvw_ref[:, 0:128])
               + _dot_nt(dkv_ref[1].astype(BF16), kvw_ref[:, 128:256]))
        h1 = h1_ref[...]
        r = _rstd(h1)
        xr = h1 * r
        dbn_ref[...] += jnp.sum(dhb * xr, axis=0, keepdims=True)
        dkn_ref[...] += jnp.sum(dkn * xr, axis=0, keepdims=True)
        u = dhb * bg_ref[...] + dkn * kvg_ref[...]
        dh1 = dh2_ref[...] + r * u - h1 * ((r * r * r) * jnp.mean(u * h1, axis=-1, keepdims=True))
        dh1_ref[...] = dh1
        dh1b = dh1.astype(BF16)
        dh1b_ref[...] = dh1b
        dz_ref[...] = _dot_nt(dh1b, wo_ref[...]).astype(BF16)

    row = lambda m: (m, 0)
    fix2 = lambda m: (0, 0)
    return pl.pallas_call(
        body, name="layer_b_in_bwd", grid=(t // tm,),
        in_specs=[pl.BlockSpec((2, tm, D_MODEL), lambda m: (0, m, 0)),
                  pl.BlockSpec((2, tm, 128), lambda m: (0, m, 0)),
                  pl.BlockSpec((nb, D_MODEL, tn), lambda m: (0, 0, 0)),
                  pl.BlockSpec((D_MODEL, 256), fix2),
                  pl.BlockSpec((tm, D_MODEL), row), pl.BlockSpec((tm, D_MODEL), row),
                  pl.BlockSpec((1, D_MODEL), fix2), pl.BlockSpec((1, D_MODEL), fix2),
                  pl.BlockSpec((D_MODEL, D_MODEL), fix2)],
        out_specs=[pl.BlockSpec((tm, D_MODEL), row), pl.BlockSpec((tm, D_MODEL), row),
                   pl.BlockSpec((tm, D_MODEL), row), pl.BlockSpec((1, D_MODEL), fix2),
                   pl.BlockSpec((1, D_MODEL), fix2)],
        out_shape=[SDS((t, D_MODEL), F32), SDS((t, D_MODEL), BF16), SDS((t, D_MODEL), BF16),
                   SDS((1, D_MODEL), F32), SDS((1, D_MODEL), F32)],
        compiler_params=_cparams(),
    )(dqg, dkv, w_in_b, kv_w, h1, dh2, b_gain, kv_gain, w_out_a)


def _layer_a_in_bwd(dqg, dkv, w_in_a, x, dh1, a_gain, chip_sums):
    t = x.shape[0]
    tm = min(t, 512)
    nb, _, tn = w_in_a.shape
    per = D_MODEL // tn

    def body(dqg_ref, dkv_ref, w_ref, x_ref, dh1_ref, ag_ref, sums_ref, dx_ref, dan_ref, land_ref,
             send_sems, recv_sems):
        @pl.when(pl.program_id(0) == 0)
        def _():
            dan_ref[...] = jnp.zeros_like(dan_ref)
            for cp in _chip_copies(sums_ref, land_ref, send_sems, recv_sems):
                cp.start()

        dxn = jnp.zeros((tm, D_MODEL), F32)
        for i in range(nb):
            part = i // per
            src = dqg_ref if part in (0, 3) else dkv_ref
            outer = {0: 0, 3: 1, 1: 0, 2: 1}[part]
            blk = src[outer, :, (i % per) * tn:(i % per + 1) * tn]
            dxn = dxn + _dot_nt(blk, w_ref[i])
        xf = x_ref[...]
        r = _rstd(xf)
        dan_ref[...] += jnp.sum(dxn * (xf * r), axis=0, keepdims=True)
        u = dxn * ag_ref[...]
        dx_ref[...] = dh1_ref[...] + r * u - xf * ((r * r * r) * jnp.mean(u * xf, axis=-1, keepdims=True))

        @pl.when(pl.program_id(0) == t // tm - 1)
        def _():
            for cp in _chip_copies(sums_ref, land_ref, send_sems, recv_sems):
                cp.wait()

    row = lambda m: (m, 0)
    fix2 = lambda m: (0, 0)
    return pl.pallas_call(
        body, name="layer_a_in_bwd", grid=(t // tm,),
        in_specs=[pl.BlockSpec((2, tm, D_MODEL), lambda m: (0, m, 0)),
                  pl.BlockSpec((2, tm, D_MODEL), lambda m: (0, m, 0)),
                  pl.BlockSpec((nb, D_MODEL, tn), lambda m: (0, 0, 0)),
                  pl.BlockSpec((tm, D_MODEL), row), pl.BlockSpec((tm, D_MODEL), row),
                  pl.BlockSpec((1, D_MODEL), fix2), ANY],
        out_specs=[pl.BlockSpec((tm, D_MODEL), row), pl.BlockSpec((1, D_MODEL), fix2), ANY],
        out_shape=[SDS((t, D_MODEL), F32), SDS((1, D_MODEL), F32), SDS(chip_sums.shape, chip_sums.dtype)],
        scratch_shapes=[pltpu.SemaphoreType.DMA((3,)), pltpu.SemaphoreType.DMA((3,))],
        compiler_params=_cparams(),
    )(dqg, dkv, w_in_a, x, dh1, a_gain, chip_sums)


def _lut(s, vals):
    r = jnp.int32(vals[0])
    for i in range(1, len(vals)):
        r = jnp.where(s == i, jnp.int32(vals[i]), r)
    return r


def _held(steps, i):
    seq, cur = [None] * len(steps), None
    for k in range(len(steps) - 1, -1, -1):
        if steps[k][0] == i:
            cur = steps[k][1:3]
        seq[k] = cur
    for k in range(len(steps)):
        cur = seq[k] = seq[k] if seq[k] is not None else cur
    return seq


def _weight_grad_cols(name, my_slot, a, bs, steps, tn):
    t, dw = a.shape
    n_arr = len(bs)
    which = [s[0] for s in steps]
    blks = [s[3] for s in steps]

    def body(slot_ref, a_ref, *rest):
        b_refs, (o_ref, own_ref, at_ref) = rest[:n_arr], rest[n_arr:]
        s = pl.program_id(0)

        @pl.when(s == 0)
        def _():
            at_ref[...] = a_ref[...].T

        for i in range(n_arr):
            @pl.when(_lut(s, which) == i)
            def _(i=i):
                res = _dot(at_ref[...], b_refs[i][0])
                o_ref[0] = res.astype(BF16)

                @pl.when(_lut(s, blks) == slot_ref[0])
                def _():
                    own_ref[...] = res

    def b_spec(i):
        held = _held(steps, i)
        return pl.BlockSpec((1, t, tn), lambda s, slot: (_lut(s, [h[0] for h in held]), 0,
                                                         _lut(s, [h[1] for h in held])))

    return pl.pallas_call(
        body, name=name,
        grid_spec=pltpu.PrefetchScalarGridSpec(
            num_scalar_prefetch=1, grid=(len(steps),),
            in_specs=[pl.BlockSpec((t, dw), lambda s, slot: (0, 0))] + [b_spec(i) for i in range(n_arr)],
            out_specs=[pl.BlockSpec((1, dw, tn), lambda s, slot: (_lut(s, blks), 0, 0)),
                       pl.BlockSpec((dw, tn), lambda s, slot: (0, 0))],
            scratch_shapes=[pltpu.VMEM((dw, t), BF16)]),
        out_shape=[SDS((N_DEV, dw, tn), BF16), SDS((dw, tn), F32)],
        compiler_params=_cparams(),
    )(my_slot, a, *bs)


def _weight_grad_rows(name, my_slot, a, b):
    t, dw = a.shape
    n_o, _, c = b.shape
    rows = dw // N_DEV
    tn = min(c, 256)
    per = c // tn

    def body(slot_ref, a_ref, b_ref, o_ref, own_ref, at_ref, res_ref):
        @pl.when(pl.program_id(0) == 0)
        def _():
            at_ref[...] = a_ref[...].T

        res_ref[...] = _dot(at_ref[...], b_ref[0].astype(BF16))
        o_ref[...] = res_ref[...].astype(BF16)
        own_ref[...] = res_ref[pl.ds(pl.multiple_of(slot_ref[0] * rows, rows), rows), :]

    all_rows, own = pl.pallas_call(
        body, name=name,
        grid_spec=pltpu.PrefetchScalarGridSpec(
            num_scalar_prefetch=1, grid=(n_o * per,),
            in_specs=[pl.BlockSpec((t, dw), lambda s, slot: (0, 0)),
                      pl.BlockSpec((1, t, tn), lambda s, slot: (s // per, 0, s % per))],
            out_specs=[pl.BlockSpec((dw, tn), lambda s, slot: (0, s)),
                       pl.BlockSpec((rows, tn), lambda s, slot: (0, s))],
            scratch_shapes=[pltpu.VMEM((dw, t), BF16), pltpu.VMEM((dw, tn), F32)]),
        out_shape=[SDS((dw, n_o * c), BF16), SDS((rows, n_o * c), F32)],
        compiler_params=_cparams(),
    )(my_slot, a, b)
    return all_rows.reshape(N_DEV, rows, n_o * c), own


def _lane_lo():
    return lax.broadcasted_iota(jnp.int32, (1, 128), 1) < HEAD_DIM


def _collapse_chunks(ds, keys):
    if ds.shape[1] < keys:
        ds = jnp.concatenate([jnp.zeros((ds.shape[0], keys - ds.shape[1]), F32), ds], axis=1)
    gc = ds[0:CHUNK]
    for cc in range(1, ds.shape[0] // CHUNK):
        gc = gc + pltpu.roll(ds[cc * CHUNK:(cc + 1) * CHUNK], keys - cc * CHUNK, 1)
    return gc


def _offset_sums(gc):
    hi = gc.astype(BF16)
    lo = (gc - hi.astype(F32)).astype(BF16)
    flip = (lax.broadcasted_iota(jnp.int32, (CHUNK, CHUNK), 0)
            + lax.broadcasted_iota(jnp.int32, (CHUNK, CHUNK), 1) == CHUNK - 1).astype(BF16)
    gf = _dot(flip, hi) + _dot(flip, lo)
    skew = pltpu.roll(gf, 0, 1, stride=1, stride_axis=0)
    return jnp.sum(skew, axis=0, keepdims=True)


def _band_bias(w_row, band, rows):
    keys = w_row.shape[1]
    base = jnp.broadcast_to(w_row, (CHUNK, keys))
    skew = pltpu.roll(base, 0, 1, stride=1, stride_axis=0)
    skew = pltpu.roll(skew, keys - (CHUNK - 1), 1)
    col = lax.broadcasted_iota(jnp.int32, (CHUNK, keys), 1)
    chunk0 = jnp.where(col < band, skew, NEG)
    return jnp.concatenate(
        [chunk0] + [pltpu.roll(chunk0, cc * CHUNK, 1) for cc in range(1, rows // CHUNK)], axis=0)


def _silu_parts(g):
    sg = _sigmoid(g)
    return g * sg, sg * (1.0 + g * (1.0 - sg))


A_PAIRS_FWD = 4
A_PAIRS_BWD = 2


def _a_specs(pairs):
    lanes = 128 * pairs
    steps = D_MODEL // lanes
    q = pl.BlockSpec((QBLK, lanes), lambda p, j: (j, p))
    ks = [pl.BlockSpec((QBLK, lanes), lambda p, j, b=b: (jnp.maximum(j - 2 + b, 0), steps + p)) for b in range(3)]
    vs = [pl.BlockSpec((QBLK, lanes), lambda p, j, b=b: (jnp.maximum(j - 2 + b, 0), 2 * steps + p))
          for b in range(3)]
    g = pl.BlockSpec((QBLK, lanes), lambda p, j: (j, 3 * steps + p))
    bias = pl.BlockSpec((pairs, 8, A_KEYS), lambda p, j: (p, 0, 0))
    return q, ks, vs, g, bias


def _a_fill_bias(w_ref, b_ref, j, pairs):
    _fill_bias(2 * pairs, lambda h: w_ref[h // 2, h % 2:h % 2 + 1, :], A_BAND, b_ref, j)


def _by_valid_key_blocks(j, fn):
    pl.when(j == 0)(functools.partial(fn, 1))
    pl.when(j == 1)(functools.partial(fn, 2))
    pl.when(j >= 2)(functools.partial(fn, 3))


def _fill_bias(n, get_row, band, bias_scr, j):
    @pl.when(j == 0)
    def _():
        for h in range(n):
            bias_scr[h] = _band_bias(get_row(h), band, bias_scr.shape[1])


def _normalise_pair(rs, mxs, lane_lo, extra=None):
    num = jnp.where(lane_lo, rs[0], rs[1])
    den = pltpu.roll(jnp.where(lane_lo, rs[1], rs[0]), HEAD_DIM, 1)
    if extra is not None:
        den = den + jnp.where(lane_lo, extra[0], extra[1])
    return num / den, jnp.where(lane_lo, mxs[0], mxs[1]) + jnp.log(den)


def _own_everywhere(x, sel):
    return jnp.where(sel, x, pltpu.roll(x, HEAD_DIM, 1))


def _minus_rows(s, row_full):
    return jnp.concatenate([s[:, i:i + 128] - row_full for i in range(0, s.shape[1], 128)], axis=1)


def _attn_a_fwd(qkvg, bias, gather):
    t = qkvg.shape[0]
    nq = t // QBLK
    n_g = len(gather)
    pairs = A_PAIRS_FWD
    lanes = 128 * pairs
    steps = D_MODEL // lanes
    q_spec, k_specs, v_specs, g_spec, bias_spec = _a_specs(pairs)

    def body(q_ref, k0, k1, k2, v0, v1, v2, g_ref, w_ref, *rest):
        shard_refs, rest = rest[:n_g], rest[n_g:]
        z_ref, o_ref, lse_ref = rest[:3]
        full_refs, (b_ref, *comm) = rest[3:3 + n_g], rest[3 + n_g:]
        p = pl.program_id(0)
        j = pl.program_id(1)
        start, forward, finish = _gather_phases(shard_refs, full_refs, *comm)
        pl.when(jnp.logical_and(p == 0, j == 0))(start)
        pl.when(jnp.logical_and(p == steps // 2, j == 0))(forward)
        _a_fill_bias(w_ref, b_ref, j, pairs)
        lane_lo = _lane_lo()
        sels = (lane_lo, jnp.logical_not(lane_lo))

        def attend(n_blocks):
            first_col = (3 - n_blocks) * QBLK
            for pp in range(pairs):
                cols = slice(128 * pp, 128 * (pp + 1))
                k = jnp.concatenate([r[:, cols] for r in (k0, k1, k2)[3 - n_blocks:]], axis=0)
                v = jnp.concatenate([r[:, cols] for r in (v0, v1, v2)[3 - n_blocks:]], axis=0)
                q = q_ref[:, cols]
                qm2 = jnp.concatenate([jnp.where(sel, q, jnp.zeros_like(q)) for sel in sels], axis=0) * SCALE
                s2 = _dot_nt(qm2, k)
                rs, mxs = [], []
                for hh, sel in enumerate(sels):
                    s = s2[hh * QBLK:(hh + 1) * QBLK] + b_ref[2 * pp + hh, :, first_col:]
                    mxs.append(jnp.max(s, axis=-1, keepdims=True))
                    e = jnp.exp(s - mxs[hh]).astype(BF16)
                    rs.append(_dot(e, jnp.where(sel, v, jnp.ones_like(v))))
                o, lse = _normalise_pair(rs, mxs, lane_lo)
                silu, _ = _silu_parts(g_ref[:, cols].astype(F32))
                o_ref[:, cols] = o.astype(BF16)
                z_ref[:, cols] = (o * silu).astype(BF16)
                lse_ref[:, cols] = lse

        _by_valid_key_blocks(j, attend)
        pl.when(jnp.logical_and(p == steps - 1, j == nq - 1))(finish)

    out_spec = pl.BlockSpec((QBLK, lanes), lambda p, j: (j, p))
    outs = pl.pallas_call(
        body, name="attn_a_fwd", grid=(steps, nq),
        in_specs=[q_spec, *k_specs, *v_specs, g_spec, bias_spec] + [ANY] * n_g,
        out_specs=[out_spec, out_spec, out_spec] + [ANY] * n_g,
        out_shape=[SDS((t, D_MODEL), BF16), SDS((t, D_MODEL), BF16), SDS((t, D_MODEL), F32)]
        + [SDS((N_DEV, *s.shape), s.dtype) for s in gather],
        scratch_shapes=[pltpu.VMEM((2 * pairs, QBLK, A_KEYS), F32)] + _gather_scratch(n_g),
        compiler_params=_cparams(),
    )(qkvg, qkvg, qkvg, qkvg, qkvg, qkvg, qkvg, qkvg, bias, *gather)
    return outs[0], outs[1], outs[2], list(outs[3:])


def _attn_a_bwd(qkvg, bias, out_a, lse, dz, scatter):
    t = qkvg.shape[0]
    nq = t // QBLK
    n_sc = len(scatter)
    pairs = A_PAIRS_BWD
    lanes = 128 * pairs
    steps = D_MODEL // lanes
    q_spec, k_specs, v_specs, g_spec, bias_spec = _a_specs(pairs)

    def body(q_ref, k0, k1, k2, v0, v1, v2, g_ref, w_ref, o_ref, lse_ref, dz_ref, *rest):
        sc_refs, rest = rest[:n_sc], rest[n_sc:]
        dqg_ref, dkv_ref, dg_ref = rest[:3]
        land_refs, rest = rest[3:3 + n_sc], rest[3 + n_sc:]
        dk_acc, dv_acc, gt_acc, b_ref, send_sems, recv_sems = rest
        j = pl.program_id(1)
        first = jnp.logical_and(pl.program_id(0) == 0, j == 0)
        last = jnp.logical_and(pl.program_id(0) == steps - 1, j == nq - 1)

        @pl.when(first)
        def _():
            for cp in _scatter_copies(sc_refs, land_refs, send_sems, recv_sems):
                cp.start()

        _a_fill_bias(w_ref, b_ref, j, pairs)

        @pl.when(j == 0)
        def _():
            dk_acc[...] = jnp.zeros_like(dk_acc)
            dv_acc[...] = jnp.zeros_like(dv_acc)
            gt_acc[...] = jnp.zeros_like(gt_acc)

        lane_lo = _lane_lo()
        sels = (lane_lo, jnp.logical_not(lane_lo))

        def attend(n_blocks):
            first_col = (3 - n_blocks) * QBLK
            for pp in range(pairs):
                cols = slice(128 * pp, 128 * (pp + 1))
                q = q_ref[:, cols]
                k = jnp.concatenate([r[:, cols] for r in (k0, k1, k2)[3 - n_blocks:]], axis=0)
                v = jnp.concatenate([r[:, cols] for r in (v0, v1, v2)[3 - n_blocks:]], axis=0)
                o = o_ref[:, cols].astype(F32)
                lse_pair = lse_ref[:, cols]
                dzf = dz_ref[:, cols].astype(F32)
                silu, dsilu = _silu_parts(g_ref[:, cols].astype(F32))
                do = dzf * silu
                dqg_ref[1, :, cols] = (dzf * o * dsilu).astype(BF16)
                doo = do * o
                qm2 = jnp.concatenate([jnp.where(sel, q, jnp.zeros_like(q)) for sel in sels], axis=0) * SCALE
                dom2 = jnp.concatenate([jnp.where(sel, do, 0.0) for sel in sels], axis=0).astype(BF16)
                s2 = _dot_nt(qm2, k)
                dp2 = _dot_nt(dom2, v)
                ps, dss = [], []
                for hh, sel in enumerate(sels):
                    rows = slice(hh * QBLK, (hh + 1) * QBLK)
                    s = s2[rows] + b_ref[2 * pp + hh, :, first_col:]
                    p = jnp.exp(_minus_rows(s, _own_everywhere(lse_pair, sel)))
                    delta = jnp.sum(jnp.where(sel, doo, 0.0), axis=-1, keepdims=True)
                    ds = p * (dp2[rows] - delta)
                    gt_acc[2 * pp + hh] += _collapse_chunks(ds, A_KEYS)
                    ps.append(p.astype(BF16))
                    dss.append(ds.astype(BF16))
                dsb2 = jnp.concatenate(dss, axis=0)
                dq2 = _dot(dsb2, k) * SCALE
                dk_blk = _dot_tn(dsb2, qm2)
                dv_blk = _dot_tn(jnp.concatenate(ps, axis=0), dom2)
                dqg_ref[0, :, cols] = jnp.where(lane_lo, dq2[0:QBLK], dq2[QBLK:2 * QBLK]).astype(BF16)
                for b in range(n_blocks):
                    rows = pl.ds(pl.multiple_of((j - n_blocks + 1 + b) * QBLK, QBLK), QBLK)
                    dk_acc[rows, cols] += dk_blk[b * QBLK:(b + 1) * QBLK]
                    dv_acc[rows, cols] += dv_blk[b * QBLK:(b + 1) * QBLK]

        _by_valid_key_blocks(j, attend)

        @pl.when(j == nq - 1)
        def _():
            dkv_ref[0] = dk_acc[...].astype(BF16)
            dkv_ref[1] = dv_acc[...].astype(BF16)
            for pp in range(pairs):
                dg_ref[pp] = jnp.concatenate([_offset_sums(gt_acc[2 * pp]), _offset_sums(gt_acc[2 * pp + 1]),
                                              jnp.zeros((6, A_DIAG), F32)], axis=0)

        @pl.when(last)
        def _():
            for cp in _scatter_copies(sc_refs, land_refs, send_sems, recv_sems):
                cp.wait()

    blk = pl.BlockSpec((QBLK, lanes), lambda p, j: (j, p))
    outs = pl.pallas_call(
        body, name="attn_a_bwd", grid=(steps, nq),
        in_specs=[q_spec, *k_specs, *v_specs, g_spec, bias_spec, blk, blk, blk] + [ANY] * n_sc,
        out_specs=[pl.BlockSpec((2, QBLK, lanes), lambda p, j: (0, j, p)),
                   pl.BlockSpec((2, t, lanes), lambda p, j: (0, 0, p)),
                   pl.BlockSpec((pairs, 8, A_DIAG), lambda p, j: (p, 0, 0))] + [ANY] * n_sc,
        out_shape=[SDS((2, t, D_MODEL), BF16), SDS((2, t, D_MODEL), BF16), SDS((N_HEADS // 2, 8, A_DIAG), F32)]
        + [SDS((N_DEV - 1, *g.shape[1:]), g.dtype) for g in scatter],
        scratch_shapes=[pltpu.VMEM((t, lanes), F32), pltpu.VMEM((t, lanes), F32),
                        pltpu.VMEM((2 * pairs, CHUNK, A_KEYS), F32), pltpu.VMEM((2 * pairs, QBLK, A_KEYS), F32),
                        pltpu.SemaphoreType.DMA(((N_DEV - 1) * n_sc,)),
                        pltpu.SemaphoreType.DMA(((N_DEV - 1) * n_sc,))],
        compiler_params=_cparams(),
    )(qkvg, qkvg, qkvg, qkvg, qkvg, qkvg, qkvg, qkvg, bias, out_a, lse, dz, *scatter)
    return outs[0], outs[1], outs[2], list(outs[3:])


def _b_specs(qblk):
    per = qblk // B_PREV
    q = pl.BlockSpec((qblk, 512), lambda h, j: (j, h))
    g = pl.BlockSpec((qblk, 512), lambda h, j: (j, 2 + h))
    kp = pl.BlockSpec((B_PREV, 128), lambda h, j: (jnp.maximum(per * j - 1, 0), 0))
    kc = pl.BlockSpec((qblk, 128), lambda h, j: (j, 0))
    vp = pl.BlockSpec((B_PREV, 128), lambda h, j: (jnp.maximum(per * j - 1, 0), 1))
    vc = pl.BlockSpec((qblk, 128), lambda h, j: (j, 1))
    bias = pl.BlockSpec((B_GROUP, qblk + B_PREV), lambda h, j: (h, 0))
    sinks = pl.BlockSpec(memory_space=pltpu.SMEM)
    return q, g, kp, kc, vp, vc, bias, sinks


def _b_operands(kp, kc, vp, vc, kvh, with_prev):
    k = jnp.concatenate([kp[...], kc[...]], axis=0) if with_prev else kc[...]
    v = jnp.concatenate([vp[...], vc[...]], axis=0) if with_prev else vc[...]
    kr = pltpu.roll(k, HEAD_DIM, 1)
    vr = pltpu.roll(v, HEAD_DIM, 1)
    first = kvh == 0
    return (jnp.where(first, k, kr), jnp.where(first, kr, k),
            jnp.where(first, v, vr), jnp.where(first, vr, v))


def _attn_b_fwd(qg, kv, bias, sinks):
    t = qg.shape[0]
    qblk = B_QBLK_FWD
    q_spec, g_spec, kp_spec, kc_spec, vp_spec, vc_spec, bias_spec, sink_spec = _b_specs(qblk)

    def body(q_ref, g_ref, kp, kc, vp, vc, w_ref, sink_ref, z_ref, o_ref, lse_ref, b_ref):
        kvh = pl.program_id(0)
        j = pl.program_id(1)
        _fill_bias(B_GROUP, lambda h: w_ref[h:h + 1, :], B_BAND, b_ref, j)
        lane_lo = _lane_lo()
        n_pairs = B_GROUP // 2

        def attend(with_prev):
            first_col = 0 if with_prev else B_PREV
            k_lo, k_hi, v_lo, v_hi = _b_operands(kp, kc, vp, vc, kvh, with_prev)
            halves = []
            for hh, sel in enumerate((lane_lo, jnp.logical_not(lane_lo))):
                kk = k_lo if hh == 0 else k_hi
                vv = v_lo if hh == 0 else v_hi
                qm4 = jnp.concatenate(
                    [jnp.where(sel, q_ref[:, 128 * pp:128 * (pp + 1)], jnp.zeros((qblk, 128), BF16))
                     for pp in range(n_pairs)], axis=0) * SCALE
                s4 = _dot_nt(qm4, kk)
                es, mxs = [], []
                for pp in range(n_pairs):
                    g = 2 * pp + hh
                    s = s4[pp * qblk:(pp + 1) * qblk] + b_ref[g, :, first_col:]
                    mxs.append(jnp.maximum(jnp.max(s, axis=-1, keepdims=True), sink_ref[kvh * B_GROUP + g]))
                    es.append(jnp.exp(s - mxs[pp]).astype(BF16))
                r4 = _dot(jnp.concatenate(es, axis=0), jnp.where(sel, vv, jnp.ones_like(vv)))
                halves.append((r4, mxs))
            for pp in range(n_pairs):
                cols = slice(128 * pp, 128 * (pp + 1))
                rows = slice(pp * qblk, (pp + 1) * qblk)
                mxs = [halves[hh][1][pp] for hh in range(2)]
                sink_terms = [jnp.exp(sink_ref[kvh * B_GROUP + 2 * pp + hh] - mxs[hh]) for hh in range(2)]
                o, lse = _normalise_pair([halves[hh][0][rows] for hh in range(2)], mxs, lane_lo, sink_terms)
                silu, _ = _silu_parts(g_ref[:, cols].astype(F32))
                o_ref[:, cols] = o.astype(BF16)
                z_ref[:, cols] = (o * silu).astype(BF16)
                lse_ref[:, cols] = lse

        pl.when(j == 0)(functools.partial(attend, False))
        pl.when(j >= 1)(functools.partial(attend, True))

    out_spec = pl.BlockSpec((qblk, 512), lambda h, j: (j, h))
    return pl.pallas_call(
        body, name="attn_b_fwd", grid=(B_KV_HEADS, t // qblk),
        in_specs=[q_spec, g_spec, kp_spec, kc_spec, vp_spec, vc_spec, bias_spec, sink_spec],
        out_specs=[out_spec, out_spec, out_spec],
        out_shape=[SDS((t, D_MODEL), BF16), SDS((t, D_MODEL), BF16), SDS((t, D_MODEL), F32)],
        scratch_shapes=[pltpu.VMEM((B_GROUP, qblk, qblk + B_PREV), F32)],
        compiler_params=_cparams(),
    )(qg, qg, kv, kv, kv, kv, bias, sinks)


def _attn_b_bwd(qg, kv, bias, sinks, out_b, lse, dz, bucket_onehot):
    t = qg.shape[0]
    qblk = B_QBLK_BWD
    keys = qblk + B_PREV
    nq = t // qblk
    q_spec, g_spec, kp_spec, kc_spec, vp_spec, vc_spec, bias_spec, sink_spec = _b_specs(qblk)

    def body(q_ref, g_ref, kp, kc, vp, vc, w_ref, sink_ref, o_ref, lse_ref, dz_ref, oh_ref,
             dqg_ref, dkv_ref, dt5_ref, dsink_ref, gt_acc, b_ref):
        kvh = pl.program_id(0)
        j = pl.program_id(1)
        _fill_bias(B_GROUP, lambda h: w_ref[h:h + 1, :], B_BAND, b_ref, j)

        @pl.when(jnp.logical_and(kvh == 0, j == 0))
        def _():
            dkv_ref[...] = jnp.zeros_like(dkv_ref)

        @pl.when(j == 0)
        def _():
            gt_acc[...] = jnp.zeros_like(gt_acc)
            dsink_ref[...] = jnp.zeros_like(dsink_ref)

        lane_lo = _lane_lo()

        def attend(with_prev):
            first_col = 0 if with_prev else B_PREV
            k_lo, k_hi, v_lo, v_hi = _b_operands(kp, kc, vp, vc, kvh, with_prev)
            dk_blk = jnp.zeros((keys - first_col, 128), F32)
            dv_blk = jnp.zeros((keys - first_col, 128), F32)
            for pp in range(B_GROUP // 2):
                cols = slice(128 * pp, 128 * (pp + 1))
                qp = q_ref[:, cols]
                o = o_ref[:, cols].astype(F32)
                lse_pair = lse_ref[:, cols]
                dzf = dz_ref[:, cols].astype(F32)
                silu, dsilu = _silu_parts(g_ref[:, cols].astype(F32))
                do = dzf * silu
                dqg_ref[1, :, cols] = (dzf * o * dsilu).astype(BF16)
                doo = do * o
                dqs = []
                for hh in range(2):
                    g = 2 * pp + hh
                    sel = lane_lo if hh == 0 else jnp.logical_not(lane_lo)
                    sink = sink_ref[kvh * B_GROUP + g]
                    kk = k_lo if hh == 0 else k_hi
                    vv = v_lo if hh == 0 else v_hi
                    qm = jnp.where(sel, qp, jnp.zeros_like(qp)) * SCALE
                    s = _dot_nt(qm, kk) + b_ref[g, :, first_col:]
                    lse_h = _own_everywhere(lse_pair, sel)
                    p = jnp.exp(_minus_rows(s, lse_h))
                    delta = jnp.sum(jnp.where(sel, doo, 0.0), axis=-1, keepdims=True)
                    dom = jnp.where(sel, do, 0.0).astype(BF16)
                    dp = _dot_nt(dom, vv)
                    ds = p * (dp - delta)
                    gt_acc[g, :, first_col:] += ds
                    dsink_ref[g:g + 1, :] -= jnp.sum(jnp.exp(sink - lse_h) * delta, axis=0, keepdims=True)
                    dsb = ds.astype(BF16)
                    dqs.append(_dot(dsb, kk) * SCALE)
                    dk_blk = dk_blk + _dot_tn(dsb, qm)
                    dv_blk = dv_blk + _dot_tn(p.astype(BF16), dom)
                dqg_ref[0, :, cols] = jnp.where(lane_lo, dqs[0], dqs[1]).astype(BF16)
            mine = lane_lo == (kvh == 0)
            dk_add = jnp.where(mine, dk_blk + pltpu.roll(dk_blk, HEAD_DIM, 1), 0.0)
            dv_add = jnp.where(mine, dv_blk + pltpu.roll(dv_blk, HEAD_DIM, 1), 0.0)
            first_key = B_PREV if with_prev else 0
            if with_prev:
                rows = pl.ds(pl.multiple_of(j * qblk - B_PREV, B_PREV), B_PREV)
                dkv_ref[0, rows, :] += dk_add[0:B_PREV]
                dkv_ref[1, rows, :] += dv_add[0:B_PREV]
            rows = pl.ds(pl.multiple_of(j * qblk, qblk), qblk)
            dkv_ref[0, rows, :] += dk_add[first_key:first_key + qblk]
            dkv_ref[1, rows, :] += dv_add[first_key:first_key + qblk]

        pl.when(j == 0)(functools.partial(attend, False))
        pl.when(j >= 1)(functools.partial(attend, True))

        @pl.when(j == nq - 1)
        def _():
            dd = jnp.concatenate([_offset_sums(_collapse_chunks(gt_acc[g], keys)) for g in range(B_GROUP)], axis=0)
            hi = dd.astype(BF16)
            lo = (dd - hi.astype(F32)).astype(BF16)
            dt5_ref[...] = _dot(hi, oh_ref[...]) + _dot(lo, oh_ref[...])

    blk = pl.BlockSpec((qblk, 512), lambda h, j: (j, h))
    return pl.pallas_call(
        body, name="attn_b_bwd", grid=(B_KV_HEADS, nq),
        in_specs=[q_spec, g_spec, kp_spec, kc_spec, vp_spec, vc_spec, bias_spec, sink_spec, blk, blk, blk,
                  pl.BlockSpec((keys, 128), lambda h, j: (0, 0))],
        out_specs=[pl.BlockSpec((2, qblk, 512), lambda h, j: (0, j, h)),
                   pl.BlockSpec((2, t, 128), lambda h, j: (0, 0, 0)),
                   pl.BlockSpec((B_GROUP, 128), lambda h, j: (h, 0)),
                   pl.BlockSpec((B_GROUP, 128), lambda h, j: (h, 0))],
        out_shape=[SDS((2, t, D_MODEL), BF16), SDS((2, t, 128), F32),
                   SDS((N_HEADS, 128), F32), SDS((N_HEADS, 128), F32)],
        scratch_shapes=[pltpu.VMEM((B_GROUP, qblk, keys), F32), pltpu.VMEM((B_GROUP, qblk, keys), F32)],
        compiler_params=_cparams(),
    )(qg, qg, kv, kv, kv, kv, bias, sinks, out_b, lse, dz, bucket_onehot)


def _a_bias_by_offset(rel_bias):
    m = np.arange(A_DIAG)
    idx = np.clip(A_BAND - 1 - m, -A_REL_CLIP, A_REL_CLIP) + A_REL_CLIP
    by_head = rel_bias[idx].T.reshape(N_HEADS // 2, 2, A_DIAG)
    return jnp.concatenate([by_head, jnp.zeros((N_HEADS // 2, 6, A_DIAG), F32)], axis=1)


def _a_bias_grad(offset_sums):
    first = 319
    tail = jnp.sum(offset_sums[:, :first], axis=1)
    body = jnp.flip(offset_sums[:, first:first + 320], axis=1)
    body = body.at[:, -1].add(tail)
    full = jnp.concatenate([jnp.zeros((N_HEADS, 193), F32), body], axis=1)
    return full


def _t5_bucket(rel):
    nb = T5_BUCKETS // 2
    max_exact = nb // 2
    ret = jnp.where(rel > 0, nb, 0)
    n = jnp.abs(rel)
    nf = jnp.maximum(n, 1).astype(jnp.float32)
    large = max_exact + (jnp.log(nf / max_exact) / math.log(T5_MAX_DIST / max_exact)
                         * (nb - max_exact)).astype(jnp.int32)
    large = jnp.minimum(large, nb - 1)
    return ret + jnp.where(n < max_exact, n, large)


def _b_offset_buckets(keys):
    return _t5_bucket(jnp.arange(keys, dtype=jnp.int32) - (B_LEFT_CHUNKS * CHUNK + CHUNK - 1))


def _b_bias_by_offset(t5_table, keys):
    return t5_table[_b_offset_buckets(keys)].T


def _b_bucket_onehot(keys):
    return (_b_offset_buckets(keys)[:, None] == jnp.arange(128)[None, :]).astype(BF16)


def _local_step(my_slot, order, x, target, a_gain_shard, w_in_a_shard, rel_bias, late_shards, kv_gain,
                t5_table, b_gain, sinks, f_gain):
    a_bias = _a_bias_by_offset(rel_bias)
    b_bias_fwd = _b_bias_by_offset(t5_table, B_QBLK_FWD + B_PREV)
    b_bias_bwd = _b_bias_by_offset(t5_table, B_QBLK_BWD + B_PREV)
    sinks_flat = sinks.reshape(N_HEADS)

    xn, qkvg, w_in_a, a_gain = _norm_matmul_gather(order, x, a_gain_shard, w_in_a_shard)
    z_a, out_a, lse_a, (w_in_b, w_out_a, w_out_b, kv_w) = _attn_a_fwd(qkvg, a_bias, late_shards)
    w_out_a = w_out_a.reshape(D_MODEL, D_MODEL)
    w_out_b = w_out_b.reshape(D_MODEL, D_MODEL)
    kv_w = kv_w.reshape(D_MODEL, 2 * 128)
    h1, kvn, hb, kv, qg = _layer_a_out(x, z_a, w_out_a, kv_gain, b_gain, kv_w, w_in_b)
    z_b, out_b, lse_b = _attn_b_fwd(qg, kv, b_bias_fwd, sinks_flat)
    dh2, dh2b, dz_b, loss, d_fn = _layer_b_out_loss(h1, z_b, w_out_b, f_gain, target)

    dqg_b, dkv_b, d_t5, d_sink = _attn_b_bwd(qg, kv, b_bias_bwd, sinks_flat, out_b, lse_b, dz_b,
                                             _b_bucket_onehot(B_QBLK_BWD + B_PREV))
    dh1, dh1b, dz_a, d_bn, d_kn = _layer_b_in_bwd(dqg_b, dkv_b, w_in_b, kv_w, h1, dh2, b_gain, kv_gain, w_out_a)
    early = dict(
        b_w_out=_weight_grad_rows("grad_b_w_out", my_slot, z_b, dh2b[None]),
        b_w_in=_weight_grad_cols("grad_b_w_in", my_slot, hb, [dqg_b],
                                 [(0, o, c, 4 * o + c) for o in range(2) for c in range(4)], 256),
        kv_w=_weight_grad_rows("grad_kv_w", my_slot, kvn, dkv_b),
        a_w_out=_weight_grad_rows("grad_a_w_out", my_slot, z_a, dh1b[None]))
    dqg_a, dkv_a, d_rel, landed = _attn_a_bwd(qkvg, a_bias, out_a, lse_a, dz_a, [g[0] for g in early.values()])
    g_w_in_a = _weight_grad_cols(
        "grad_a_w_in", my_slot, xn, [dqg_a, dkv_a],
        [(0, 0, 0, 0), (0, 0, 1, 1), (1, 0, 0, 2), (1, 0, 1, 3), (1, 1, 0, 4), (1, 1, 1, 5), (0, 1, 0, 6), (0, 1, 1, 7)], 512)
    chip_sums, from_sibling = _chip_sums(g_w_in_a[0])
    grad_x, d_an, from_chips = _layer_a_in_bwd(dqg_a, dkv_a, w_in_a, x, dh1, a_gain, chip_sums)

    matrices = {n: (g[1], [(land, 0, N_DEV - 1)]) for (n, g), land in zip(early.items(), landed)}
    matrices["a_w_in"] = (g_w_in_a[1], [(from_sibling, 0, 1), (from_chips, 0, 3)])
    small = dict(
        loss=loss, a_norm=d_an, a_rel_bias=d_rel[:, :2].reshape(N_HEADS, A_DIAG),
        kv_norm=d_kn, t5_bias=d_t5, b_norm=d_bn, b_sinks=d_sink, final_norm=d_fn)
    return grad_x, small, matrices


def _place():
    x, y, c = lax.axis_index("x"), lax.axis_index("y"), lax.axis_index("c")
    chips = [(1 - x, y), (x, 1 - y), (1 - x, 1 - y)]
    return x, y, c, chips


def _slot(px, py, pc):
    return 4 * px + 2 * py + pc


ANY = pl.BlockSpec(memory_space=pl.ANY)


def _peer(x, y, c, k):
    return (x ^ (k >> 2), y ^ ((k >> 1) & 1), c ^ (k & 1))


def _scatter_copies(grad_refs, land_refs, send_sems, recv_sems):
    x, y, c, _ = _place()
    copies = []
    for t, (grad, land) in enumerate(zip(grad_refs, land_refs)):
        for k in range(1, N_DEV):
            peer = _peer(x, y, c, k)
            sem = (N_DEV - 1) * t + k - 1
            copies.append(pltpu.make_async_remote_copy(
                src_ref=grad.at[_slot(*peer)], dst_ref=land.at[k - 1],
                send_sem=send_sems.at[sem], recv_sem=recv_sems.at[sem],
                device_id=peer, device_id_type=MESH))
    return copies


def _gather_phases(ins, outs, send_sems, recv_sems, local_sems):
    n = len(ins)
    x, y, c, chips = _place()
    me, sibling = (x, y, c), (x, y, 1 - c)

    def copy(t, k, block, to, src=None):
        dst = outs[t].at[_slot(*block)]
        return pltpu.make_async_remote_copy(
            src_ref=dst if src is None else src, dst_ref=dst,
            send_sem=send_sems.at[7 * t + k], recv_sem=recv_sems.at[7 * t + k],
            device_id=to, device_id_type=MESH)

    def lists():
        mine = [pltpu.make_async_copy(ins[t], outs[t].at[_slot(*me)], local_sems.at[t]) for t in range(n)]
        first = []
        for t in range(n):
            first.append(copy(t, 0, me, sibling, src=ins[t]))
            first += [copy(t, 1 + j, me, (*chip, c), src=ins[t]) for j, chip in enumerate(chips)]
        passed = [copy(t, 4 + j, (*chip, c), sibling) for t in range(n) for j, chip in enumerate(chips)]
        return mine, first, passed

    def start():
        mine, first, _ = lists()
        for cp in mine + first:
            cp.start()

    def forward():
        _, _, passed = lists()
        for t in range(n):
            for j, chip in enumerate(chips):
                copy(t, 1 + j, (*chip, c), me).wait_recv()
                passed[3 * t + j].start()

    def finish():
        mine, first, passed = lists()
        for t in range(n):
            copy(t, 0, sibling, me).wait_recv()
            for j, chip in enumerate(chips):
                copy(t, 4 + j, (*chip, 1 - c), me).wait_recv()
        for cp in first + passed:
            cp.wait_send()
        for cp in mine:
            cp.wait()

    return start, forward, finish


def _gather_scratch(n):
    return [pltpu.SemaphoreType.DMA((7 * n,)), pltpu.SemaphoreType.DMA((7 * n,)), pltpu.SemaphoreType.DMA((n,))]


def _chip_sums(g):
    _, r, c = g.shape

    def body(g_ref, sums_ref, mine_ref, land, own, send_sems, recv_sems, load_sems):
        x, y, c_i, chips = _place()
        sibling = (x, y, 1 - c_i)
        blocks = [(*chip, 1 - c_i) for chip in chips] + [sibling]
        sends = [pltpu.make_async_remote_copy(
            src_ref=g_ref.at[_slot(*block)], dst_ref=land.at[k], send_sem=send_sems.at[k],
            recv_sem=recv_sems.at[k], device_id=sibling, device_id_type=MESH) for k, block in enumerate(blocks)]
        loads = [pltpu.make_async_copy(g_ref.at[_slot(*chip, c_i)], own.at[j], load_sems.at[j])
                 for j, chip in enumerate(chips)]
        for cp in sends + loads:
            cp.start()
        for cp in sends + loads:
            cp.wait()
        for j in range(3):
            sums_ref[j] = (own[j].astype(F32) + land[j].astype(F32)).astype(BF16)
        mine_ref[0] = land[3]

    return pl.pallas_call(
        body, name="chip_sums",
        in_specs=[ANY], out_specs=[VM, VM],
        out_shape=[SDS((3, r, c), BF16), SDS((1, r, c), BF16)],
        scratch_shapes=[pltpu.VMEM((4, r, c), BF16), pltpu.VMEM((3, r, c), BF16),
                        pltpu.SemaphoreType.DMA((4,)), pltpu.SemaphoreType.DMA((4,)), pltpu.SemaphoreType.DMA((3,))],
        compiler_params=_cparams(),
    )(g)


def _chip_copies(sums_ref, land_ref, send_sems, recv_sems):
    x, y, c, chips = _place()
    del x, y
    return [pltpu.make_async_remote_copy(
        src_ref=sums_ref.at[j], dst_ref=land_ref.at[j], send_sem=send_sems.at[j], recv_sem=recv_sems.at[j],
        device_id=(*chip, c), device_id_type=MESH) for j, chip in enumerate(chips)]


def _row_tile(rows):
    return min(rows, 256)


def _adamw(w, g, m, v):
    m2 = ADAM_B1 * m + (1.0 - ADAM_B1) * g
    v2 = ADAM_B2 * v + (1.0 - ADAM_B2) * jnp.square(g)
    m_hat = m2 / (1.0 - ADAM_B1 ** ADAM_STEP)
    v_hat = v2 / (1.0 - ADAM_B2 ** ADAM_STEP)
    delta = -ADAM_LR * (m_hat / (jnp.sqrt(v_hat) + ADAM_EPS) + ADAM_WD * w)
    return delta, m2, v2


def _reduce_adamw(name, own, partials, w, m, v):
    r, c = own.shape
    tr = _row_tile(r)
    n_p = len(partials)

    def body(own_ref, *rest):
        p_refs, (w_ref, m_ref, v_ref, grad_ref, d_ref, nm_ref, nv_ref) = rest[:n_p], rest[n_p:]
        grad = own_ref[...]
        for p_ref, (_, _, count) in zip(p_refs, partials):
            for j in range(count):
                grad = grad + p_ref[j].astype(F32)
        grad_ref[...] = grad
        d_ref[...], nm_ref[...], nv_ref[...] = _adamw(w_ref[...], grad, m_ref[...], v_ref[...])

    flat = pl.BlockSpec((tr, c), lambda i: (i, 0))
    return pl.pallas_call(
        body, name=name, grid=(r // tr,),
        in_specs=[flat] + [pl.BlockSpec((count, tr, c), lambda i, first=first, count=count: (first // count, i, 0))
                           for _, first, count in partials] + [flat, flat, flat],
        out_specs=[flat, flat, flat, flat],
        out_shape=[SDS((r, c), F32)] * 4,
        compiler_params=_cparams(),
    )(own, *[p[0] for p in partials], w, m, v)


VM = pl.BlockSpec()


def _small_allreduce(parts):
    n = len(parts)

    def body(*refs):
        ins, outs, lands = refs[:n], refs[n:2 * n], refs[2 * n:3 * n]
        send_sems, recv_sems = refs[3 * n:]
        x, y, c, _ = _place()
        my_slot = _slot(x, y, c)
        copies = []
        for t in range(n):
            lands[t][my_slot] = ins[t][...]
            for k in range(1, N_DEV):
                sem = (N_DEV - 1) * t + k - 1
                copies.append(pltpu.make_async_remote_copy(
                    src_ref=ins[t], dst_ref=lands[t].at[my_slot],
                    send_sem=send_sems.at[sem], recv_sem=recv_sems.at[sem],
                    device_id=_peer(x, y, c, k), device_id_type=MESH))
        for cp in copies:
            cp.start()
        for t in range(n):
            for k in range(1, N_DEV):
                sem = (N_DEV - 1) * t + k - 1
                pltpu.make_async_remote_copy(
                    src_ref=ins[t], dst_ref=lands[t].at[_slot(*_peer(x, y, c, k))],
                    send_sem=send_sems.at[sem], recv_sem=recv_sems.at[sem],
                    device_id=(x, y, c), device_id_type=MESH).wait_recv()
        for cp in copies:
            cp.wait_send()
        for t in range(n):
            total = lands[t][0]
            for s in range(1, N_DEV):
                total = total + lands[t][s]
            outs[t][...] = total

    n_sems = (N_DEV - 1) * n
    return pl.pallas_call(
        body, name="small_allreduce",
        in_specs=[VM] * n, out_specs=[VM] * n, out_shape=[SDS(p.shape, F32) for p in parts],
        scratch_shapes=[pltpu.VMEM((N_DEV, *p.shape), F32) for p in parts]
        + [pltpu.SemaphoreType.DMA((n_sems,)), pltpu.SemaphoreType.DMA((n_sems,))],
    )(*parts)


def _small_adamw(my_slot, sums, ws, ms, vs):
    n = len(ws)

    def body(slot_ref, *refs):
        sum_refs, refs = refs[:n + 1], refs[n + 1:]
        w_refs, m_refs, v_refs, refs = refs[:n], refs[n:2 * n], refs[2 * n:3 * n], refs[3 * n:]
        g_refs, d_refs, nm_refs, nv_refs = refs[:n + 1], refs[n + 1:2 * n + 1], refs[2 * n + 1:3 * n + 1], refs[3 * n + 1:]
        for t in range(n + 1):
            if t == 0:
                g = sum_refs[0][:, pl.ds(pl.multiple_of(slot_ref[0] * 128, 128), 128)]
            else:
                g = sum_refs[t][...]
            g_refs[t][...] = g
            if t < n:
                d_refs[t][...], nm_refs[t][...], nv_refs[t][...] = _adamw(w_refs[t][...], g, m_refs[t][...], v_refs[t][...])

    shapes = [SDS(w.shape, F32) for w in ws]
    outs = pl.pallas_call(
        body, name="small_adamw",
        in_specs=[pl.BlockSpec(memory_space=pltpu.SMEM)] + [VM] * (4 * n + 1),
        out_specs=[VM] * (4 * n + 1),
        out_shape=shapes + [SDS(sums[-1].shape, F32)] + shapes * 3,
    )(my_slot, *sums, *ws, *ms, *vs)
    return outs[:n + 1], outs[n + 1:2 * n + 1], outs[2 * n + 1:3 * n + 1], outs[3 * n + 1:]


def kernel(x, a_norm, a_w_in, a_rel_bias, a_w_out, kv_norm, kv_w, t5_bias, b_norm, b_w_in, b_sinks, b_w_out, final_norm, loss_target, m_a_norm, m_a_w_in, m_a_rel_bias, m_a_w_out, m_kv_norm, m_kv_w, m_t5_bias, m_b_norm, m_b_w_in, m_b_sinks, m_b_w_out, m_final_norm, v_a_norm, v_a_w_in, v_a_rel_bias, v_a_w_out, v_kv_norm, v_kv_w, v_t5_bias, v_b_norm, v_b_w_in, v_b_sinks, v_b_w_out, v_final_norm):
    xi, yi, ci = lax.axis_index("x"), lax.axis_index("y"), lax.axis_index("c")
    my_slot = _slot(xi, yi, ci)

    slot_arr = jnp.reshape(my_slot, (1,)).astype(jnp.int32)
    order = _gather_order(xi, yi, ci)
    late_shards = [b_w_in[0].astype(BF16), a_w_out[0].astype(BF16), b_w_out[0].astype(BF16), kv_w.astype(BF16)]
    grad_x, loc, matrices = _local_step(
        slot_arr, order, x[0], loss_target[0], a_norm, a_w_in[0].astype(BF16), a_rel_bias[0], late_shards,
        kv_norm.reshape(1, D_MODEL), t5_bias, b_norm, b_sinks, final_norm.reshape(1, D_MODEL))

    shard_w = dict(a_w_in=a_w_in[0], b_w_in=b_w_in[0], a_w_out=a_w_out[0], b_w_out=b_w_out[0], kv_w=kv_w)
    shard_m = dict(a_w_in=m_a_w_in[0], b_w_in=m_b_w_in[0], a_w_out=m_a_w_out[0], b_w_out=m_b_w_out[0], kv_w=m_kv_w)
    shard_v = dict(a_w_in=v_a_w_in[0], b_w_in=v_b_w_in[0], a_w_out=v_a_w_out[0], b_w_out=v_b_w_out[0], kv_w=v_kv_w)
    big = {n: _reduce_adamw("adamw_" + n, own, partials, shard_w[n], shard_m[n], shard_v[n])
           for n, (own, partials) in matrices.items()}

    names = ("a_norm", "a_rel_bias", "kv_norm", "t5_bias", "b_norm", "b_sinks", "final_norm")
    tables = ("a_rel_bias", "t5_bias")

    def row(n, a):
        return a.reshape(-1, a.shape[-1]).T if n in tables else a.reshape(1, -1)

    small_w = [row(n, a) for n, a in zip(names, (a_norm, a_rel_bias, kv_norm, t5_bias, b_norm, b_sinks, final_norm))]
    small_m = [row(n, a) for n, a in zip(names, (m_a_norm, m_a_rel_bias, m_kv_norm, m_t5_bias, m_b_norm, m_b_sinks,
                                                 m_final_norm))]
    small_v = [row(n, a) for n, a in zip(names, (v_a_norm, v_a_rel_bias, v_kv_norm, v_t5_bias, v_b_norm, v_b_sinks,
                                                 v_final_norm))]
    sums = dict(zip(names + ("loss",), _small_allreduce([loc[n] for n in names] + [loc["loss"]])))
    sums["a_rel_bias"] = _a_bias_grad(sums["a_rel_bias"])
    sums["t5_bias"] = sums["t5_bias"][:, :T5_BUCKETS]
    sums["b_sinks"] = sums["b_sinks"][:, 0].reshape(1, N_HEADS)
    results = _small_adamw(slot_arr, [sums[n] for n in names + ("loss",)], small_w, small_m, small_v)
    like = dict(a_norm=a_norm, a_rel_bias=a_rel_bias, kv_norm=kv_norm, t5_bias=t5_bias, b_norm=b_norm,
                b_sinks=b_sinks, final_norm=final_norm)
    sm = [{n: (part[i].T if n in tables else part[i]).reshape(like[n].shape) for i, n in enumerate(names)}
          for part in results]
    loss = results[0][len(names)][0, 0]

    order = ("a_norm", "a_w_in", "a_rel_bias", "a_w_out", "kv_norm", "kv_w", "t5_bias", "b_norm",
             "b_w_in", "b_sinks", "b_w_out", "final_norm")
    lead = dict(a_w_in=True, b_w_in=True, a_w_out=True, b_w_out=True, kv_w=False)

    def pick(kind, name):
        if name in big:
            val = big[name][kind]
            return val[None] if lead[name] else val
        return sm[kind][name]

    outs = [loss, grad_x[None]]
    for kind in range(4):
        outs += [pick(kind, n) for n in order]
    return tuple(outs)
```

```python
import functools
import math

import numpy as np
import jax
import jax.numpy as jnp
from jax import lax
from jax.experimental import pallas as pl
from jax.experimental.pallas import tpu as pltpu

F32 = jnp.float32
BF16 = jnp.bfloat16
SDS = jax.ShapeDtypeStruct

D_MODEL = 1024
HEAD_DIM = 64
CHUNK = 64
N_HEADS = 16
RMS_EPS = 1e-6
A_LEFT_CHUNKS = 8
A_BAND = (A_LEFT_CHUNKS + 1) * CHUNK
A_REL_CLIP = 256
B_KV_HEADS = 2
B_GROUP = 8
B_LEFT_CHUNKS = 2
B_BAND = (B_LEFT_CHUNKS + 1) * CHUNK
T5_BUCKETS = 32
T5_MAX_DIST = 128
QBLK = 256
A_KEYS = 3 * QBLK
B_QBLK_FWD = 128
B_QBLK_BWD = 256
B_PREV = 128
A_DIAG = A_KEYS
NEG = -1e30
SCALE = HEAD_DIM ** -0.5
N_DEV = 8

ADAM_LR = 0.001
ADAM_B1 = 0.9
ADAM_B2 = 0.999
ADAM_EPS = 1e-08
ADAM_WD = 0.01
ADAM_STEP = 10

VMEM_LIMIT_BYTES = 56 * 1024 * 1024
MESH = pl.DeviceIdType.MESH


def _cparams():
    return pltpu.CompilerParams(vmem_limit_bytes=VMEM_LIMIT_BYTES)


def _dot(a, b):
    return jnp.dot(a, b, preferred_element_type=F32)


def _dot_nt(a, b):
    return lax.dot_general(a, b, (((1,), (1,)), ((), ())), preferred_element_type=F32)


def _dot_tn(a, b):
    return lax.dot_general(a, b, (((0,), (0,)), ((), ())), preferred_element_type=F32)


def _rstd(xf):
    return lax.rsqrt(jnp.mean(xf * xf, axis=-1, keepdims=True) + RMS_EPS)


def _sigmoid(x):
    return 1.0 / (1.0 + jnp.exp(-x))


_GATHER_SEQUENCE = ((0, None), (1, 0), (2, 1), (4, None), (5, None), (3, 2), (6, None))


def _gather_order(x, y, c):
    others = [(1 - x, y), (x, 1 - y), (1 - x, 1 - y)]
    arrivals = [_slot(x, y, 1 - c)] + [_slot(*chip, c) for chip in others] + [_slot(*chip, 1 - c) for chip in others]
    return jnp.stack([_slot(x, y, c)] + [arrivals[a] for a, _ in _GATHER_SEQUENCE]).astype(jnp.int32)


def _norm_matmul_gather(order, x, gain_shard, w_shard):
    t = x.shape[0]
    dw, tn = w_shard.shape
    tm = min(t, 1024)
    n_m = t // tm

    def body(order_ref, x_ref, gs_ref, shard_ref, xn_ref, o_ref, full_ref, gain_ref,
             xn_all, wbuf, gland, send_sems, recv_sems, gsend_sems, grecv_sems, load_sems, own_sem):
        n, m = pl.program_id(0), pl.program_id(1)
        x_i, y_i, c_i, chips = _place()
        me, sibling = (x_i, y_i, c_i), (x_i, y_i, 1 - c_i)

        def send(k, block, to, src=None):
            dst = full_ref.at[_slot(*block)]
            return pltpu.make_async_remote_copy(
                src_ref=dst if src is None else src, dst_ref=dst,
                send_sem=send_sems.at[k], recv_sem=recv_sems.at[k], device_id=to, device_id_type=MESH)

        own = pltpu.make_async_copy(shard_ref, full_ref.at[_slot(*me)], own_sem)
        first = [send(0, me, sibling, src=shard_ref)]
        first += [send(1 + j, me, (*chip, c_i), src=shard_ref) for j, chip in enumerate(chips)]
        forwards = [send(4 + j, (*chip, c_i), sibling) for j, chip in enumerate(chips)]
        arrivals = [send(0, sibling, me)] + [send(1 + j, (*chip, c_i), me) for j, chip in enumerate(chips)]
        arrivals += [send(4 + j, (*chip, 1 - c_i), me) for j, chip in enumerate(chips)]
        gains = [pltpu.make_async_remote_copy(
            src_ref=gs_ref, dst_ref=gland.at[_slot(*me)], send_sem=gsend_sems.at[k - 1],
            recv_sem=grecv_sems.at[k - 1], device_id=_peer(x_i, y_i, c_i, k), device_id_type=MESH)
            for k in range(1, N_DEV)]

        @pl.when(jnp.logical_and(n == 0, m == 0))
        def _():
            own.start()
            for cp in gains + first:
                cp.start()
            pltpu.make_async_copy(shard_ref, wbuf.at[0], load_sems.at[0]).start()
            gland[_slot(*me)] = gs_ref[...]
            for k in range(1, N_DEV):
                pltpu.make_async_remote_copy(
                    src_ref=gs_ref, dst_ref=gland.at[_slot(*_peer(x_i, y_i, c_i, k))],
                    send_sem=gsend_sems.at[k - 1], recv_sem=grecv_sems.at[k - 1],
                    device_id=me, device_id_type=MESH).wait_recv()
            for s in range(N_DEV):
                gain_ref[:, 128 * s:128 * (s + 1)] = gland[s]

        rows = pl.ds(pl.multiple_of(m * tm, tm), tm)

        @pl.when(n == 0)
        def _():
            xf = x_ref[...]
            xn = ((xf * _rstd(xf)) * gain_ref[...]).astype(BF16)
            xn_all[rows, :] = xn
            xn_ref[...] = xn

        @pl.when(m == 0)
        def _():
            pltpu.make_async_copy(full_ref.at[0], wbuf.at[n % 2], load_sems.at[n % 2]).wait()

        o_ref[...] = _dot(xn_all[rows, :], wbuf[n % 2]).astype(BF16)

        for k, (arrival, forward) in enumerate(_GATHER_SEQUENCE):
            @pl.when(jnp.logical_and(n == k, m == n_m - 1))
            def _(k=k, arrival=arrival, forward=forward):
                arrivals[arrival].wait_recv()
                if forward is not None:
                    forwards[forward].start()
                pltpu.make_async_copy(full_ref.at[order_ref[k + 1]], wbuf.at[(k + 1) % 2],
                                      load_sems.at[(k + 1) % 2]).start()

        @pl.when(jnp.logical_and(n == N_DEV - 1, m == n_m - 1))
        def _():
            for cp in gains + first + forwards:
                cp.wait_send()
            own.wait()

    held = lambda n, m, order: (jnp.where(n == 0, m, n_m - 1), 0)
    return pl.pallas_call(
        body, name="norm_matmul_gather",
        grid_spec=pltpu.PrefetchScalarGridSpec(
            num_scalar_prefetch=1, grid=(N_DEV, n_m),
            in_specs=[pl.BlockSpec((tm, D_MODEL), held),
                      pl.BlockSpec((1, 128), lambda n, m, order: (0, 0)), ANY],
            out_specs=[pl.BlockSpec((tm, D_MODEL), held),
                       pl.BlockSpec((tm, tn), lambda n, m, order: (m, order[n])),
                       ANY, pl.BlockSpec((1, D_MODEL), lambda n, m, order: (0, 0))],
            scratch_shapes=[pltpu.VMEM((t, D_MODEL), BF16), pltpu.VMEM((2, dw, tn), BF16),
                            pltpu.VMEM((N_DEV, 1, 128), F32),
                            pltpu.SemaphoreType.DMA((7,)), pltpu.SemaphoreType.DMA((7,)),
                            pltpu.SemaphoreType.DMA((7,)), pltpu.SemaphoreType.DMA((7,)),
                            pltpu.SemaphoreType.DMA((2,)), pltpu.SemaphoreType.DMA]),
        out_shape=[SDS((t, D_MODEL), BF16), SDS((t, N_DEV * tn), BF16), SDS((N_DEV, dw, tn), BF16),
                   SDS((1, D_MODEL), F32)],
        compiler_params=_cparams(),
    )(order, x, gain_shard, w_shard)


def _layer_a_out(x, z, w_out, kv_gain, b_gain, kv_w, w_in_b):
    t = x.shape[0]
    tm = min(t, 512)
    nb, _, tn = w_in_b.shape

    def body(x_ref, z_ref, wo_ref, kvg_ref, bg_ref, kvw_ref, wb_ref,
             h1_ref, kvn_ref, hb_ref, kv_ref, qg_ref):
        h1 = x_ref[...] + _dot(z_ref[...], wo_ref[...])
        h1_ref[...] = h1
        y0 = h1 * _rstd(h1)
        kvn = (y0 * kvg_ref[...]).astype(BF16)
        hb = (y0 * bg_ref[...]).astype(BF16)
        kvn_ref[...] = kvn
        hb_ref[...] = hb
        kv_ref[...] = _dot(kvn, kvw_ref[...]).astype(BF16)
        for i in range(nb):
            qg_ref[:, i * tn:(i + 1) * tn] = _dot(hb, wb_ref[i]).astype(BF16)

    row = lambda m: (m, 0)
    fix2 = lambda m: (0, 0)
    return pl.pallas_call(
        body, name="layer_a_out", grid=(t // tm,),
        in_specs=[pl.BlockSpec((tm, D_MODEL), row), pl.BlockSpec((tm, D_MODEL), row),
                  pl.BlockSpec((D_MODEL, D_MODEL), fix2),
                  pl.BlockSpec((1, D_MODEL), fix2), pl.BlockSpec((1, D_MODEL), fix2),
                  pl.BlockSpec((D_MODEL, 256), fix2),
                  pl.BlockSpec((nb, D_MODEL, tn), lambda m: (0, 0, 0))],
        out_specs=[pl.BlockSpec((tm, D_MODEL), row), pl.BlockSpec((tm, D_MODEL), row),
                   pl.BlockSpec((tm, D_MODEL), row), pl.BlockSpec((tm, 256), row),
                   pl.BlockSpec((tm, nb * tn), row)],
        out_shape=[SDS((t, D_MODEL), F32), SDS((t, D_MODEL), BF16), SDS((t, D_MODEL), BF16),
                   SDS((t, 256), BF16), SDS((t, nb * tn), BF16)],
        compiler_params=_cparams(),
    )(x, z, w_out, kv_gain, b_gain, kv_w, w_in_b)


def _layer_b_out_loss(h1, z, w_out, f_gain, target):
    t = h1.shape[0]
    tm = min(t, 512)

    def body(h1_ref, z_ref, wo_ref, fg_ref, tgt_ref,
             dh2_ref, dh2b_ref, dz_ref, loss_ref, dfn_ref):
        @pl.when(pl.program_id(0) == 0)
        def _():
            loss_ref[...] = jnp.zeros_like(loss_ref)
            dfn_ref[...] = jnp.zeros_like(dfn_ref)

        h2 = h1_ref[...] + _dot(z_ref[...], wo_ref[...])
        r = _rstd(h2)
        yn = h2 * r
        fg = fg_ref[...]
        err = yn * fg - tgt_ref[...]
        loss_ref[...] += (0.5 / D_MODEL) * jnp.sum(err * err)
        dy = err * (1.0 / D_MODEL)
        dfn_ref[...] += jnp.sum(dy * yn, axis=0, keepdims=True)
        u = dy * fg
        dh2 = r * u - h2 * ((r * r * r) * jnp.mean(u * h2, axis=-1, keepdims=True))
        dh2_ref[...] = dh2
        dh2b = dh2.astype(BF16)
        dh2b_ref[...] = dh2b
        dz_ref[...] = _dot_nt(dh2b, wo_ref[...]).astype(BF16)

    row = lambda m: (m, 0)
    fix2 = lambda m: (0, 0)
    return pl.pallas_call(
        body, name="layer_b_out_loss", grid=(t // tm,),
        in_specs=[pl.BlockSpec((tm, D_MODEL), row), pl.BlockSpec((tm, D_MODEL), row),
                  pl.BlockSpec((D_MODEL, D_MODEL), fix2), pl.BlockSpec((1, D_MODEL), fix2),
                  pl.BlockSpec((tm, D_MODEL), row)],
        out_specs=[pl.BlockSpec((tm, D_MODEL), row), pl.BlockSpec((tm, D_MODEL), row),
                   pl.BlockSpec((tm, D_MODEL), row), pl.BlockSpec((1, 128), fix2),
                   pl.BlockSpec((1, D_MODEL), fix2)],
        out_shape=[SDS((t, D_MODEL), F32), SDS((t, D_MODEL), BF16), SDS((t, D_MODEL), BF16),
                   SDS((1, 128), F32), SDS((1, D_MODEL), F32)],
        compiler_params=_cparams(),
    )(h1, z, w_out, f_gain, target)


def _layer_b_in_bwd(dqg, dkv, w_in_b, kv_w, h1, dh2, b_gain, kv_gain, w_out_a):
    t = h1.shape[0]
    tm = min(t, 512)
    nb, _, tn = w_in_b.shape
    per = D_MODEL // tn

    def body(dqg_ref, dkv_ref, wb_ref, kvw_ref, h1_ref, dh2_ref, bg_ref, kvg_ref, wo_ref,
             dh1_ref, dh1b_ref, dz_ref, dbn_ref, dkn_ref):
        @pl.when(pl.program_id(0) == 0)
        def _():
            dbn_ref[...] = jnp.zeros_like(dbn_ref)
            dkn_ref[...] = jnp.zeros_like(dkn_ref)

        dhb = jnp.zeros((tm, D_MODEL), F32)
        for i in range(nb):
            blk = dqg_ref[i // per, :, (i % per) * tn:(i % per + 1) * tn]
            dhb = dhb + _dot_nt(blk, wb_ref[i])
        dkn = (_dot_nt(dkv_ref[0].astype(BF16), kvw_ref[:, 0:128])
               + _dot_nt(dkv_ref[1].astype(BF16), kvw_ref[:, 128:256]))
        h1 = h1_ref[...]
        r = _rstd(h1)
        xr = h1 * r
        dbn_ref[...] += jnp.sum(dhb * xr, axis=0, keepdims=True)
        dkn_ref[...] += jnp.sum(dkn * xr, axis=0, keepdims=True)
        u = dhb * bg_ref[...] + dkn * kvg_ref[...]
        dh1 = dh2_ref[...] + r * u - h1 * ((r * r * r) * jnp.mean(u * h1, axis=-1, keepdims=True))
        dh1_ref[...] = dh1
        dh1b = dh1.astype(BF16)
        dh1b_ref[...] = dh1b
        dz_ref[...] = _dot_nt(dh1b, wo_ref[...]).astype(BF16)

    row = lambda m: (m, 0)
    fix2 = lambda m: (0, 0)
    return pl.pallas_call(
        body, name="layer_b_in_bwd", grid=(t // tm,),
        in_specs=[pl.BlockSpec((2, tm, D_MODEL), lambda m: (0, m, 0)),
                  pl.BlockSpec((2, tm, 128), lambda m: (0, m, 0)),
                  pl.BlockSpec((nb, D_MODEL, tn), lambda m: (0, 0, 0)),
                  pl.BlockSpec((D_MODEL, 256), fix2),
                  pl.BlockSpec((tm, D_MODEL), row), pl.BlockSpec((tm, D_MODEL), row),
                  pl.BlockSpec((1, D_MODEL), fix2), pl.BlockSpec((1, D_MODEL), fix2),
                  pl.BlockSpec((D_MODEL, D_MODEL), fix2)],
        out_specs=[pl.BlockSpec((tm, D_MODEL), row), pl.BlockSpec((tm, D_MODEL), row),
                   pl.BlockSpec((tm, D_MODEL), row), pl.BlockSpec((1, D_MODEL), fix2),
                   pl.BlockSpec((1, D_MODEL), fix2)],
        out_shape=[SDS((t, D_MODEL), F32), SDS((t, D_MODEL), BF16), SDS((t, D_MODEL), BF16),
                   SDS((1, D_MODEL), F32), SDS((1, D_MODEL), F32)],
        compiler_params=_cparams(),
    )(dqg, dkv, w_in_b, kv_w, h1, dh2, b_gain, kv_gain, w_out_a)


def _layer_a_in_bwd(dqg, dkv, w_in_a, x, dh1, a_gain, chip_sums):
    t = x.shape[0]
    tm = min(t, 512)
    nb, _, tn = w_in_a.shape
    per = D_MODEL // tn

    def body(dqg_ref, dkv_ref, w_ref, x_ref, dh1_ref, ag_ref, sums_ref, dx_ref, dan_ref, land_ref,
             send_sems, recv_sems):
        @pl.when(pl.program_id(0) == 0)
        def _():
            dan_ref[...] = jnp.zeros_like(dan_ref)
            for cp in _chip_copies(sums_ref, land_ref, send_sems, recv_sems):
                cp.start()

        dxn = jnp.zeros((tm, D_MODEL), F32)
        for i in range(nb):
            part = i // per
            src = dqg_ref if part in (0, 3) else dkv_ref
            outer = {0: 0, 3: 1, 1: 0, 2: 1}[part]
            blk = src[outer, :, (i % per) * tn:(i % per + 1) * tn]
            dxn = dxn + _dot_nt(blk, w_ref[i])
        xf = x_ref[...]
        r = _rstd(xf)
        dan_ref[...] += jnp.sum(dxn * (xf * r), axis=0, keepdims=True)
        u = dxn * ag_ref[...]
        dx_ref[...] = dh1_ref[...] + r * u - xf * ((r * r * r) * jnp.mean(u * xf, axis=-1, keepdims=True))

        @pl.when(pl.program_id(0) == t // tm - 1)
        def _():
            for cp in _chip_copies(sums_ref, land_ref, send_sems, recv_sems):
                cp.wait()

    row = lambda m: (m, 0)
    fix2 = lambda m: (0, 0)
    return pl.pallas_call(
        body, name="layer_a_in_bwd", grid=(t // tm,),
        in_specs=[pl.BlockSpec((2, tm, D_MODEL), lambda m: (0, m, 0)),
                  pl.BlockSpec((2, tm, D_MODEL), lambda m: (0, m, 0)),
                  pl.BlockSpec((nb, D_MODEL, tn), lambda m: (0, 0, 0)),
                  pl.BlockSpec((tm, D_MODEL), row), pl.BlockSpec((tm, D_MODEL), row),
                  pl.BlockSpec((1, D_MODEL), fix2), ANY],
        out_specs=[pl.BlockSpec((tm, D_MODEL), row), pl.BlockSpec((1, D_MODEL), fix2), ANY],
        out_shape=[SDS((t, D_MODEL), F32), SDS((1, D_MODEL), F32), SDS(chip_sums.shape, chip_sums.dtype)],
        scratch_shapes=[pltpu.SemaphoreType.DMA((3,)), pltpu.SemaphoreType.DMA((3,))],
        compiler_params=_cparams(),
    )(dqg, dkv, w_in_a, x, dh1, a_gain, chip_sums)


def _lut(s, vals):
    r = jnp.int32(vals[0])
    for i in range(1, len(vals)):
        r = jnp.where(s == i, jnp.int32(vals[i]), r)
    return r


def _held(steps, i):
    seq, cur = [None] * len(steps), None
    for k in range(len(steps) - 1, -1, -1):
        if steps[k][0] == i:
            cur = steps[k][1:3]
        seq[k] = cur
    for k in range(len(steps)):
        cur = seq[k] = seq[k] if seq[k] is not None else cur
    return seq


def _weight_grad_cols(name, my_slot, a, bs, steps, tn):
    t, dw = a.shape
    n_arr = len(bs)
    which = [s[0] for s in steps]
    blks = [s[3] for s in steps]

    def body(slot_ref, a_ref, *rest):
        b_refs, (o_ref, own_ref, at_ref) = rest[:n_arr], rest[n_arr:]
        s = pl.program_id(0)

        @pl.when(s == 0)
        def _():
            at_ref[...] = a_ref[...].T

        for i in range(n_arr):
            @pl.when(_lut(s, which) == i)
            def _(i=i):
                res = _dot(at_ref[...], b_refs[i][0])
                o_ref[0] = res.astype(BF16)

                @pl.when(_lut(s, blks) == slot_ref[0])
                def _():
                    own_ref[...] = res

    def b_spec(i):
        held = _held(steps, i)
        return pl.BlockSpec((1, t, tn), lambda s, slot: (_lut(s, [h[0] for h in held]), 0,
                                                         _lut(s, [h[1] for h in held])))

    return pl.pallas_call(
        body, name=name,
        grid_spec=pltpu.PrefetchScalarGridSpec(
            num_scalar_prefetch=1, grid=(len(steps),),
            in_specs=[pl.BlockSpec((t, dw), lambda s, slot: (0, 0))] + [b_spec(i) for i in range(n_arr)],
            out_specs=[pl.BlockSpec((1, dw, tn), lambda s, slot: (_lut(s, blks), 0, 0)),
                       pl.BlockSpec((dw, tn), lambda s, slot: (0, 0))],
            scratch_shapes=[pltpu.VMEM((dw, t), BF16)]),
        out_shape=[SDS((N_DEV, dw, tn), BF16), SDS((dw, tn), F32)],
        compiler_params=_cparams(),
    )(my_slot, a, *bs)


def _weight_grad_rows(name, my_slot, a, b):
    t, dw = a.shape
    n_o, _, c = b.shape
    rows = dw // N_DEV
    tn = min(c, 256)
    per = c // tn

    def body(slot_ref, a_ref, b_ref, o_ref, own_ref, at_ref, res_ref):
        @pl.when(pl.program_id(0) == 0)
        def _():
            at_ref[...] = a_ref[...].T

        res_ref[...] = _dot(at_ref[...], b_ref[0].astype(BF16))
        o_ref[...] = res_ref[...].astype(BF16)
        own_ref[...] = res_ref[pl.ds(pl.multiple_of(slot_ref[0] * rows, rows), rows), :]

    all_rows, own = pl.pallas_call(
        body, name=name,
        grid_spec=pltpu.PrefetchScalarGridSpec(
            num_scalar_prefetch=1, grid=(n_o * per,),
            in_specs=[pl.BlockSpec((t, dw), lambda s, slot: (0, 0)),
                      pl.BlockSpec((1, t, tn), lambda s, slot: (s // per, 0, s % per))],
            out_specs=[pl.BlockSpec((dw, tn), lambda s, slot: (0, s)),
                       pl.BlockSpec((rows, tn), lambda s, slot: (0, s))],
            scratch_shapes=[pltpu.VMEM((dw, t), BF16), pltpu.VMEM((dw, tn), F32)]),
        out_shape=[SDS((dw, n_o * c), BF16), SDS((rows, n_o * c), F32)],
        compiler_params=_cparams(),
    )(my_slot, a, b)
    return all_rows.reshape(N_DEV, rows, n_o * c), own


def _lane_lo():
    return lax.broadcasted_iota(jnp.int32, (1, 128), 1) < HEAD_DIM


def _collapse_chunks(ds, keys):
    if ds.shape[1] < keys:
        ds = jnp.concatenate([jnp.zeros((ds.shape[0], keys - ds.shape[1]), F32), ds], axis=1)
    gc = ds[0:CHUNK]
    for cc in range(1, ds.shape[0] // CHUNK):
        gc = gc + pltpu.roll(ds[cc * CHUNK:(cc + 1) * CHUNK], keys - cc * CHUNK, 1)
    return gc


def _offset_sums(gc):
    hi = gc.astype(BF16)
    lo = (gc - hi.astype(F32)).astype(BF16)
    flip = (lax.broadcasted_iota(jnp.int32, (CHUNK, CHUNK), 0)
            + lax.broadcasted_iota(jnp.int32, (CHUNK, CHUNK), 1) == CHUNK - 1).astype(BF16)
    gf = _dot(flip, hi) + _dot(flip, lo)
    skew = pltpu.roll(gf, 0, 1, stride=1, stride_axis=0)
    return jnp.sum(skew, axis=0, keepdims=True)


def _band_bias(w_row, band, rows):
    keys = w_row.shape[1]
    base = jnp.broadcast_to(w_row, (CHUNK, keys))
    skew = pltpu.roll(base, 0, 1, stride=1, stride_axis=0)
    skew = pltpu.roll(skew, keys - (CHUNK - 1), 1)
    col = lax.broadcasted_iota(jnp.int32, (CHUNK, keys), 1)
    chunk0 = jnp.where(col < band, skew, NEG)
    return jnp.concatenate(
        [chunk0] + [pltpu.roll(chunk0, cc * CHUNK, 1) for cc in range(1, rows // CHUNK)], axis=0)


def _silu_parts(g):
    sg = _sigmoid(g)
    return g * sg, sg * (1.0 + g * (1.0 - sg))


A_PAIRS_FWD = 4
A_PAIRS_BWD = 4


def _a_specs(pairs):
    lanes = 128 * pairs
    steps = D_MODEL // lanes
    q = pl.BlockSpec((QBLK, lanes), lambda p, j: (j, p))
    ks = [pl.BlockSpec((QBLK, lanes), lambda p, j, b=b: (jnp.maximum(j - 2 + b, 0), steps + p)) for b in range(3)]
    vs = [pl.BlockSpec((QBLK, lanes), lambda p, j, b=b: (jnp.maximum(j - 2 + b, 0), 2 * steps + p))
          for b in range(3)]
    g = pl.BlockSpec((QBLK, lanes), lambda p, j: (j, 3 * steps + p))
    bias = pl.BlockSpec((pairs, 8, A_KEYS), lambda p, j: (p, 0, 0))
    return q, ks, vs, g, bias


def _a_fill_bias(w_ref, b_ref, j, pairs):
    _fill_bias(2 * pairs, lambda h: w_ref[h // 2, h % 2:h % 2 + 1, :], A_BAND, b_ref, j)


def _by_valid_key_blocks(j, fn):
    pl.when(j == 0)(functools.partial(fn, 1))
    pl.when(j == 1)(functools.partial(fn, 2))
    pl.when(j >= 2)(functools.partial(fn, 3))


def _fill_bias(n, get_row, band, bias_scr, j):
    @pl.when(j == 0)
    def _():
        for h in range(n):
            bias_scr[h] = _band_bias(get_row(h), band, bias_scr.shape[1])


def _normalise_pair(rs, mxs, lane_lo, extra=None):
    num = jnp.where(lane_lo, rs[0], rs[1])
    den = pltpu.roll(jnp.where(lane_lo, rs[1], rs[0]), HEAD_DIM, 1)
    if extra is not None:
        den = den + jnp.where(lane_lo, extra[0], extra[1])
    return num / den, jnp.where(lane_lo, mxs[0], mxs[1]) + jnp.log(den)


def _own_everywhere(x, sel):
    return jnp.where(sel, x, pltpu.roll(x, HEAD_DIM, 1))


def _minus_rows(s, row_full):
    return jnp.concatenate([s[:, i:i + 128] - row_full for i in range(0, s.shape[1], 128)], axis=1)


def _attn_a_fwd(qkvg, bias, gather):
    t = qkvg.shape[0]
    nq = t // QBLK
    n_g = len(gather)
    pairs = A_PAIRS_FWD
    lanes = 128 * pairs
    steps = D_MODEL // lanes
    q_spec, k_specs, v_specs, g_spec, bias_spec = _a_specs(pairs)

    def body(q_ref, k0, k1, k2, v0, v1, v2, g_ref, w_ref, *rest):
        shard_refs, rest = rest[:n_g], rest[n_g:]
        z_ref, o_ref, lse_ref = rest[:3]
        full_refs, (b_ref, *comm) = rest[3:3 + n_g], rest[3 + n_g:]
        p = pl.program_id(0)
        j = pl.program_id(1)
        start, forward, finish = _gather_phases(shard_refs, full_refs, *comm)
        pl.when(jnp.logical_and(p == 0, j == 0))(start)
        pl.when(jnp.logical_and(p == steps // 2, j == 0))(forward)
        _a_fill_bias(w_ref, b_ref, j, pairs)
        lane_lo = _lane_lo()
        sels = (lane_lo, jnp.logical_not(lane_lo))

        def attend(n_blocks):
            first_col = (3 - n_blocks) * QBLK
            for pp in range(pairs):
                cols = slice(128 * pp, 128 * (pp + 1))
                k = jnp.concatenate([r[:, cols] for r in (k0, k1, k2)[3 - n_blocks:]], axis=0)
                v = jnp.concatenate([r[:, cols] for r in (v0, v1, v2)[3 - n_blocks:]], axis=0)
                q = q_ref[:, cols]
                qm2 = jnp.concatenate([jnp.where(sel, q, jnp.zeros_like(q)) for sel in sels], axis=0) * SCALE
                s2 = _dot_nt(qm2, k)
                rs, mxs = [], []
                for hh, sel in enumerate(sels):
                    s = s2[hh * QBLK:(hh + 1) * QBLK] + b_ref[2 * pp + hh, :, first_col:]
                    mxs.append(jnp.max(s, axis=-1, keepdims=True))
                    e = jnp.exp(s - mxs[hh]).astype(BF16)
                    rs.append(_dot(e, jnp.where(sel, v, jnp.ones_like(v))))
                o, lse = _normalise_pair(rs, mxs, lane_lo)
                silu, _ = _silu_parts(g_ref[:, cols].astype(F32))
                o_ref[:, cols] = o.astype(BF16)
                z_ref[:, cols] = (o * silu).astype(BF16)
                lse_ref[:, cols] = lse

        _by_valid_key_blocks(j, attend)
        pl.when(jnp.logical_and(p == steps - 1, j == nq - 1))(finish)

    out_spec = pl.BlockSpec((QBLK, lanes), lambda p, j: (j, p))
    outs = pl.pallas_call(
        body, name="attn_a_fwd", grid=(steps, nq),
        in_specs=[q_spec, *k_specs, *v_specs, g_spec, bias_spec] + [ANY] * n_g,
        out_specs=[out_spec, out_spec, out_spec] + [ANY] * n_g,
        out_shape=[SDS((t, D_MODEL), BF16), SDS((t, D_MODEL), BF16), SDS((t, D_MODEL), F32)]
        + [SDS((N_DEV, *s.shape), s.dtype) for s in gather],
        scratch_shapes=[pltpu.VMEM((2 * pairs, QBLK, A_KEYS), F32)] + _gather_scratch(n_g),
        compiler_params=_cparams(),
    )(qkvg, qkvg, qkvg, qkvg, qkvg, qkvg, qkvg, qkvg, bias, *gather)
    return outs[0], outs[1], outs[2], list(outs[3:])


def _attn_a_bwd(qkvg, bias, out_a, lse, dz, scatter):
    t = qkvg.shape[0]
    nq = t // QBLK
    n_sc = len(scatter)
    pairs = A_PAIRS_BWD
    lanes = 128 * pairs
    steps = D_MODEL // lanes
    q_spec, k_specs, v_specs, g_spec, bias_spec = _a_specs(pairs)

    def body(q_ref, k0, k1, k2, v0, v1, v2, g_ref, w_ref, o_ref, lse_ref, dz_ref, *rest):
        sc_refs, rest = rest[:n_sc], rest[n_sc:]
        dqg_ref, dkv_ref, dg_ref = rest[:3]
        land_refs, rest = rest[3:3 + n_sc], rest[3 + n_sc:]
        dk_acc, dv_acc, gt_acc, b_ref, send_sems, recv_sems = rest
        j = pl.program_id(1)
        first = jnp.logical_and(pl.program_id(0) == 0, j == 0)
        last = jnp.logical_and(pl.program_id(0) == steps - 1, j == nq - 1)

        @pl.when(first)
        def _():
            for cp in _scatter_copies(sc_refs, land_refs, send_sems, recv_sems):
                cp.start()

        _a_fill_bias(w_ref, b_ref, j, pairs)

        @pl.when(j == 0)
        def _():
            dk_acc[...] = jnp.zeros_like(dk_acc)
            dv_acc[...] = jnp.zeros_like(dv_acc)
            gt_acc[...] = jnp.zeros_like(gt_acc)

        lane_lo = _lane_lo()
        sels = (lane_lo, jnp.logical_not(lane_lo))

        def attend(n_blocks):
            first_col = (3 - n_blocks) * QBLK
            for pp in range(pairs):
                cols = slice(128 * pp, 128 * (pp + 1))
                q = q_ref[:, cols]
                k = jnp.concatenate([r[:, cols] for r in (k0, k1, k2)[3 - n_blocks:]], axis=0)
                v = jnp.concatenate([r[:, cols] for r in (v0, v1, v2)[3 - n_blocks:]], axis=0)
                o = o_ref[:, cols].astype(F32)
                lse_pair = lse_ref[:, cols]
                dzf = dz_ref[:, cols].astype(F32)
                silu, dsilu = _silu_parts(g_ref[:, cols].astype(F32))
                do = dzf * silu
                dqg_ref[1, :, cols] = (dzf * o * dsilu).astype(BF16)
                doo = do * o
                qm2 = jnp.concatenate([jnp.where(sel, q, jnp.zeros_like(q)) for sel in sels], axis=0) * SCALE
                dom2 = jnp.concatenate([jnp.where(sel, do, 0.0) for sel in sels], axis=0).astype(BF16)
                s2 = _dot_nt(qm2, k)
                dp2 = _dot_nt(dom2, v)
                ps, dss = [], []
                for hh, sel in enumerate(sels):
                    rows = slice(hh * QBLK, (hh + 1) * QBLK)
                    s = s2[rows] + b_ref[2 * pp + hh, :, first_col:]
                    p = jnp.exp(_minus_rows(s, _own_everywhere(lse_pair, sel)))
                    delta = jnp.sum(jnp.where(sel, doo, 0.0), axis=-1, keepdims=True)
                    ds = p * (dp2[rows] - delta)
                    gt_acc[2 * pp + hh] += _collapse_chunks(ds, A_KEYS)
                    ps.append(p.astype(BF16))
                    dss.append(ds.astype(BF16))
                dsb2 = jnp.concatenate(dss, axis=0)
                dq2 = _dot(dsb2, k) * SCALE
                dk_blk = _dot_tn(dsb2, qm2)
                dv_blk = _dot_tn(jnp.concatenate(ps, axis=0), dom2)
                dqg_ref[0, :, cols] = jnp.where(lane_lo, dq2[0:QBLK], dq2[QBLK:2 * QBLK]).astype(BF16)
                for b in range(n_blocks):
                    rows = pl.ds(pl.multiple_of((j - n_blocks + 1 + b) * QBLK, QBLK), QBLK)
                    dk_acc[rows, cols] += dk_blk[b * QBLK:(b + 1) * QBLK]
                    dv_acc[rows, cols] += dv_blk[b * QBLK:(b + 1) * QBLK]

        _by_valid_key_blocks(j, attend)

        @pl.when(j == nq - 1)
        def _():
            dkv_ref[0] = dk_acc[...].astype(BF16)
            dkv_ref[1] = dv_acc[...].astype(BF16)
            for pp in range(pairs):
                dg_ref[pp] = jnp.concatenate([_offset_sums(gt_acc[2 * pp]), _offset_sums(gt_acc[2 * pp + 1]),
                                              jnp.zeros((6, A_DIAG), F32)], axis=0)

        @pl.when(last)
        def _():
            for cp in _scatter_copies(sc_refs, land_refs, send_sems, recv_sems):
                cp.wait()

    blk = pl.BlockSpec((QBLK, lanes), lambda p, j: (j, p))
    outs = pl.pallas_call(
        body, name="attn_a_bwd", grid=(steps, nq),
        in_specs=[q_spec, *k_specs, *v_specs, g_spec, bias_spec, blk, blk, blk] + [ANY] * n_sc,
        out_specs=[pl.BlockSpec((2, QBLK, lanes), lambda p, j: (0, j, p)),
                   pl.BlockSpec((2, t, lanes), lambda p, j: (0, 0, p)),
                   pl.BlockSpec((pairs, 8, A_DIAG), lambda p, j: (p, 0, 0))] + [ANY] * n_sc,
        out_shape=[SDS((2, t, D_MODEL), BF16), SDS((2, t, D_MODEL), BF16), SDS((N_HEADS // 2, 8, A_DIAG), F32)]
        + [SDS((N_DEV - 1, *g.shape[1:]), g.dtype) for g in scatter],
        scratch_shapes=[pltpu.VMEM((t, lanes), F32), pltpu.VMEM((t, lanes), F32),
                        pltpu.VMEM((2 * pairs, CHUNK, A_KEYS), F32), pltpu.VMEM((2 * pairs, QBLK, A_KEYS), F32),
                        pltpu.SemaphoreType.DMA(((N_DEV - 1) * n_sc,)),
                        pltpu.SemaphoreType.DMA(((N_DEV - 1) * n_sc,))],
        compiler_params=_cparams(),
    )(qkvg, qkvg, qkvg, qkvg, qkvg, qkvg, qkvg, qkvg, bias, out_a, lse, dz, *scatter)
    return outs[0], outs[1], outs[2], list(outs[3:])


def _b_specs(qblk):
    per = qblk // B_PREV
    q = pl.BlockSpec((qblk, 512), lambda h, j: (j, h))
    g = pl.BlockSpec((qblk, 512), lambda h, j: (j, 2 + h))
    kp = pl.BlockSpec((B_PREV, 128), lambda h, j: (jnp.maximum(per * j - 1, 0), 0))
    kc = pl.BlockSpec((qblk, 128), lambda h, j: (j, 0))
    vp = pl.BlockSpec((B_PREV, 128), lambda h, j: (jnp.maximum(per * j - 1, 0), 1))
    vc = pl.BlockSpec((qblk, 128), lambda h, j: (j, 1))
    bias = pl.BlockSpec((B_GROUP, qblk + B_PREV), lambda h, j: (h, 0))
    sinks = pl.BlockSpec(memory_space=pltpu.SMEM)
    return q, g, kp, kc, vp, vc, bias, sinks


def _b_operands(kp, kc, vp, vc, kvh, with_prev):
    k = jnp.concatenate([kp[...], kc[...]], axis=0) if with_prev else kc[...]
    v = jnp.concatenate([vp[...], vc[...]], axis=0) if with_prev else vc[...]
    kr = pltpu.roll(k, HEAD_DIM, 1)
    vr = pltpu.roll(v, HEAD_DIM, 1)
    first = kvh == 0
    return (jnp.where(first, k, kr), jnp.where(first, kr, k),
            jnp.where(first, v, vr), jnp.where(first, vr, v))


def _attn_b_fwd(qg, kv, bias, sinks):
    t = qg.shape[0]
    qblk = B_QBLK_FWD
    q_spec, g_spec, kp_spec, kc_spec, vp_spec, vc_spec, bias_spec, sink_spec = _b_specs(qblk)

    def body(q_ref, g_ref, kp, kc, vp, vc, w_ref, sink_ref, z_ref, o_ref, lse_ref, b_ref):
        kvh = pl.program_id(0)
        j = pl.program_id(1)
        _fill_bias(B_GROUP, lambda h: w_ref[h:h + 1, :], B_BAND, b_ref, j)
        lane_lo = _lane_lo()
        n_pairs = B_GROUP // 2

        def attend(with_prev):
            first_col = 0 if with_prev else B_PREV
            k_lo, k_hi, v_lo, v_hi = _b_operands(kp, kc, vp, vc, kvh, with_prev)
            halves = []
            for hh, sel in enumerate((lane_lo, jnp.logical_not(lane_lo))):
                kk = k_lo if hh == 0 else k_hi
                vv = v_lo if hh == 0 else v_hi
                qm4 = jnp.concatenate(
                    [jnp.where(sel, q_ref[:, 128 * pp:128 * (pp + 1)], jnp.zeros((qblk, 128), BF16))
                     for pp in range(n_pairs)], axis=0) * SCALE
                s4 = _dot_nt(qm4, kk)
                es, mxs = [], []
                for pp in range(n_pairs):
                    g = 2 * pp + hh
                    s = s4[pp * qblk:(pp + 1) * qblk] + b_ref[g, :, first_col:]
                    mxs.append(jnp.maximum(jnp.max(s, axis=-1, keepdims=True), sink_ref[kvh * B_GROUP + g]))
                    es.append(jnp.exp(s - mxs[pp]).astype(BF16))
                r4 = _dot(jnp.concatenate(es, axis=0), jnp.where(sel, vv, jnp.ones_like(vv)))
                halves.append((r4, mxs))
            for pp in range(n_pairs):
                cols = slice(128 * pp, 128 * (pp + 1))
                rows = slice(pp * qblk, (pp + 1) * qblk)
                mxs = [halves[hh][1][pp] for hh in range(2)]
                sink_terms = [jnp.exp(sink_ref[kvh * B_GROUP + 2 * pp + hh] - mxs[hh]) for hh in range(2)]
                o, lse = _normalise_pair([halves[hh][0][rows] for hh in range(2)], mxs, lane_lo, sink_terms)
                silu, _ = _silu_parts(g_ref[:, cols].astype(F32))
                o_ref[:, cols] = o.astype(BF16)
                z_ref[:, cols] = (o * silu).astype(BF16)
                lse_ref[:, cols] = lse

        pl.when(j == 0)(functools.partial(attend, False))
        pl.when(j >= 1)(functools.partial(attend, True))

    out_spec = pl.BlockSpec((qblk, 512), lambda h, j: (j, h))
    return pl.pallas_call(
        body, name="attn_b_fwd", grid=(B_KV_HEADS, t // qblk),
        in_specs=[q_spec, g_spec, kp_spec, kc_spec, vp_spec, vc_spec, bias_spec, sink_spec],
        out_specs=[out_spec, out_spec, out_spec],
        out_shape=[SDS((t, D_MODEL), BF16), SDS((t, D_MODEL), BF16), SDS((t, D_MODEL), F32)],
        scratch_shapes=[pltpu.VMEM((B_GROUP, qblk, qblk + B_PREV), F32)],
        compiler_params=_cparams(),
    )(qg, qg, kv, kv, kv, kv, bias, sinks)


def _attn_b_bwd(qg, kv, bias, sinks, out_b, lse, dz, bucket_onehot):
    t = qg.shape[0]
    qblk = B_QBLK_BWD
    keys = qblk + B_PREV
    nq = t // qblk
    q_spec, g_spec, kp_spec, kc_spec, vp_spec, vc_spec, bias_spec, sink_spec = _b_specs(qblk)

    def body(q_ref, g_ref, kp, kc, vp, vc, w_ref, sink_ref, o_ref, lse_ref, dz_ref, oh_ref,
             dqg_ref, dkv_ref, dt5_ref, dsink_ref, gt_acc, b_ref):
        kvh = pl.program_id(0)
        j = pl.program_id(1)
        _fill_bias(B_GROUP, lambda h: w_ref[h:h + 1, :], B_BAND, b_ref, j)

        @pl.when(jnp.logical_and(kvh == 0, j == 0))
        def _():
            dkv_ref[...] = jnp.zeros_like(dkv_ref)

        @pl.when(j == 0)
        def _():
            gt_acc[...] = jnp.zeros_like(gt_acc)
            dsink_ref[...] = jnp.zeros_like(dsink_ref)

        lane_lo = _lane_lo()

        def attend(with_prev):
            first_col = 0 if with_prev else B_PREV
            k_lo, k_hi, v_lo, v_hi = _b_operands(kp, kc, vp, vc, kvh, with_prev)
            dk_blk = jnp.zeros((keys - first_col, 128), F32)
            dv_blk = jnp.zeros((keys - first_col, 128), F32)
            for pp in range(B_GROUP // 2):
                cols = slice(128 * pp, 128 * (pp + 1))
                qp = q_ref[:, cols]
                o = o_ref[:, cols].astype(F32)
                lse_pair = lse_ref[:, cols]
                dzf = dz_ref[:, cols].astype(F32)
                silu, dsilu = _silu_parts(g_ref[:, cols].astype(F32))
                do = dzf * silu
                dqg_ref[1, :, cols] = (dzf * o * dsilu).astype(BF16)
                doo = do * o
                dqs = []
                for hh in range(2):
                    g = 2 * pp + hh
                    sel = lane_lo if hh == 0 else jnp.logical_not(lane_lo)
                    sink = sink_ref[kvh * B_GROUP + g]
                    kk = k_lo if hh == 0 else k_hi
                    vv = v_lo if hh == 0 else v_hi
                    qm = jnp.where(sel, qp, jnp.zeros_like(qp)) * SCALE
                    s = _dot_nt(qm, kk) + b_ref[g, :, first_col:]
                    lse_h = _own_everywhere(lse_pair, sel)
                    p = jnp.exp(_minus_rows(s, lse_h))
                    delta = jnp.sum(jnp.where(sel, doo, 0.0), axis=-1, keepdims=True)
                    dom = jnp.where(sel, do, 0.0).astype(BF16)
                    dp = _dot_nt(dom, vv)
                    ds = p * (dp - delta)
                    gt_acc[g, :, first_col:] += ds
                    dsink_ref[g:g + 1, :] -= jnp.sum(jnp.exp(sink - lse_h) * delta, axis=0, keepdims=True)
                    dsb = ds.astype(BF16)
                    dqs.append(_dot(dsb, kk) * SCALE)
                    dk_blk = dk_blk + _dot_tn(dsb, qm)
                    dv_blk = dv_blk + _dot_tn(p.astype(BF16), dom)
                dqg_ref[0, :, cols] = jnp.where(lane_lo, dqs[0], dqs[1]).astype(BF16)
            mine = lane_lo == (kvh == 0)
            dk_add = jnp.where(mine, dk_blk + pltpu.roll(dk_blk, HEAD_DIM, 1), 0.0)
            dv_add = jnp.where(mine, dv_blk + pltpu.roll(dv_blk, HEAD_DIM, 1), 0.0)
            first_key = B_PREV if with_prev else 0
            if with_prev:
                rows = pl.ds(pl.multiple_of(j * qblk - B_PREV, B_PREV), B_PREV)
                dkv_ref[0, rows, :] += dk_add[0:B_PREV]
                dkv_ref[1, rows, :] += dv_add[0:B_PREV]
            rows = pl.ds(pl.multiple_of(j * qblk, qblk), qblk)
            dkv_ref[0, rows, :] += dk_add[first_key:first_key + qblk]
            dkv_ref[1, rows, :] += dv_add[first_key:first_key + qblk]

        pl.when(j == 0)(functools.partial(attend, False))
        pl.when(j >= 1)(functools.partial(attend, True))

        @pl.when(j == nq - 1)
        def _():
            dd = jnp.concatenate([_offset_sums(_collapse_chunks(gt_acc[g], keys)) for g in range(B_GROUP)], axis=0)
            hi = dd.astype(BF16)
            lo = (dd - hi.astype(F32)).astype(BF16)
            dt5_ref[...] = _dot(hi, oh_ref[...]) + _dot(lo, oh_ref[...])

    blk = pl.BlockSpec((qblk, 512), lambda h, j: (j, h))
    return pl.pallas_call(
        body, name="attn_b_bwd", grid=(B_KV_HEADS, nq),
        in_specs=[q_spec, g_spec, kp_spec, kc_spec, vp_spec, vc_spec, bias_spec, sink_spec, blk, blk, blk,
                  pl.BlockSpec((keys, 128), lambda h, j: (0, 0))],
        out_specs=[pl.BlockSpec((2, qblk, 512), lambda h, j: (0, j, h)),
                   pl.BlockSpec((2, t, 128), lambda h, j: (0, 0, 0)),
                   pl.BlockSpec((B_GROUP, 128), lambda h, j: (h, 0)),
                   pl.BlockSpec((B_GROUP, 128), lambda h, j: (h, 0))],
        out_shape=[SDS((2, t, D_MODEL), BF16), SDS((2, t, 128), F32),
                   SDS((N_HEADS, 128), F32), SDS((N_HEADS, 128), F32)],
        scratch_shapes=[pltpu.VMEM((B_GROUP, qblk, keys), F32), pltpu.VMEM((B_GROUP, qblk, keys), F32)],
        compiler_params=_cparams(),
    )(qg, qg, kv, kv, kv, kv, bias, sinks, out_b, lse, dz, bucket_onehot)


def _a_bias_by_offset(rel_bias):
    m = np.arange(A_DIAG)
    idx = np.clip(A_BAND - 1 - m, -A_REL_CLIP, A_REL_CLIP) + A_REL_CLIP
    by_head = rel_bias[idx].T.reshape(N_HEADS // 2, 2, A_DIAG)
    return jnp.concatenate([by_head, jnp.zeros((N_HEADS // 2, 6, A_DIAG), F32)], axis=1)


def _a_bias_grad(offset_sums):
    first = 319
    tail = jnp.sum(offset_sums[:, :first], axis=1)
    body = jnp.flip(offset_sums[:, first:first + 320], axis=1)
    body = body.at[:, -1].add(tail)
    full = jnp.concatenate([jnp.zeros((N_HEADS, 193), F32), body], axis=1)
    return full


def _t5_bucket(rel):
    nb = T5_BUCKETS // 2
    max_exact = nb // 2
    ret = jnp.where(rel > 0, nb, 0)
    n = jnp.abs(rel)
    nf = jnp.maximum(n, 1).astype(jnp.float32)
    large = max_exact + (jnp.log(nf / max_exact) / math.log(T5_MAX_DIST / max_exact)
                         * (nb - max_exact)).astype(jnp.int32)
    large = jnp.minimum(large, nb - 1)
    return ret + jnp.where(n < max_exact, n, large)


def _b_offset_buckets(keys):
    return _t5_bucket(jnp.arange(keys, dtype=jnp.int32) - (B_LEFT_CHUNKS * CHUNK + CHUNK - 1))


def _b_bias_by_offset(t5_table, keys):
    return t5_table[_b_offset_buckets(keys)].T


def _b_bucket_onehot(keys):
    return (_b_offset_buckets(keys)[:, None] == jnp.arange(128)[None, :]).astype(BF16)


def _local_step(my_slot, order, x, target, a_gain_shard, w_in_a_shard, rel_bias, late_shards, kv_gain,
                t5_table, b_gain, sinks, f_gain):
    a_bias = _a_bias_by_offset(rel_bias)
    b_bias_fwd = _b_bias_by_offset(t5_table, B_QBLK_FWD + B_PREV)
    b_bias_bwd = _b_bias_by_offset(t5_table, B_QBLK_BWD + B_PREV)
    sinks_flat = sinks.reshape(N_HEADS)

    xn, qkvg, w_in_a, a_gain = _norm_matmul_gather(order, x, a_gain_shard, w_in_a_shard)
    z_a, out_a, lse_a, (w_in_b, w_out_a, w_out_b, kv_w) = _attn_a_fwd(qkvg, a_bias, late_shards)
    w_out_a = w_out_a.reshape(D_MODEL, D_MODEL)
    w_out_b = w_out_b.reshape(D_MODEL, D_MODEL)
    kv_w = kv_w.reshape(D_MODEL, 2 * 128)
    h1, kvn, hb, kv, qg = _layer_a_out(x, z_a, w_out_a, kv_gain, b_gain, kv_w, w_in_b)
    z_b, out_b, lse_b = _attn_b_fwd(qg, kv, b_bias_fwd, sinks_flat)
    dh2, dh2b, dz_b, loss, d_fn = _layer_b_out_loss(h1, z_b, w_out_b, f_gain, target)

    dqg_b, dkv_b, d_t5, d_sink = _attn_b_bwd(qg, kv, b_bias_bwd, sinks_flat, out_b, lse_b, dz_b,
                                             _b_bucket_onehot(B_QBLK_BWD + B_PREV))
    dh1, dh1b, dz_a, d_bn, d_kn = _layer_b_in_bwd(dqg_b, dkv_b, w_in_b, kv_w, h1, dh2, b_gain, kv_gain, w_out_a)
    early = dict(
        b_w_out=_weight_grad_rows("grad_b_w_out", my_slot, z_b, dh2b[None]),
        b_w_in=_weight_grad_cols("grad_b_w_in", my_slot, hb, [dqg_b],
                                 [(0, o, c, 4 * o + c) for o in range(2) for c in range(4)], 256),
        kv_w=_weight_grad_rows("grad_kv_w", my_slot, kvn, dkv_b),
        a_w_out=_weight_grad_rows("grad_a_w_out", my_slot, z_a, dh1b[None]))
    dqg_a, dkv_a, d_rel, landed = _attn_a_bwd(qkvg, a_bias, out_a, lse_a, dz_a, [g[0] for g in early.values()])
    g_w_in_a = _weight_grad_cols(
        "grad_a_w_in", my_slot, xn, [dqg_a, dkv_a],
        [(0, 0, 0, 0), (0, 0, 1, 1), (1, 0, 0, 2), (1, 0, 1, 3), (1, 1, 0, 4), (1, 1, 1, 5), (0, 1, 0, 6), (0, 1, 1, 7)], 512)
    chip_sums, from_sibling = _chip_sums(g_w_in_a[0])
    grad_x, d_an, from_chips = _layer_a_in_bwd(dqg_a, dkv_a, w_in_a, x, dh1, a_gain, chip_sums)

    matrices = {n: (g[1], [(land, 0, N_DEV - 1)]) for (n, g), land in zip(early.items(), landed)}
    matrices["a_w_in"] = (g_w_in_a[1], [(from_sibling, 0, 1), (from_chips, 0, 3)])
    small = dict(
        loss=loss, a_norm=d_an, a_rel_bias=d_rel[:, :2].reshape(N_HEADS, A_DIAG),
        kv_norm=d_kn, t5_bias=d_t5, b_norm=d_bn, b_sinks=d_sink, final_norm=d_fn)
    return grad_x, small, matrices


def _place():
    x, y, c = lax.axis_index("x"), lax.axis_index("y"), lax.axis_index("c")
    chips = [(1 - x, y), (x, 1 - y), (1 - x, 1 - y)]
    return x, y, c, chips


def _slot(px, py, pc):
    return 4 * px + 2 * py + pc


ANY = pl.BlockSpec(memory_space=pl.ANY)


def _peer(x, y, c, k):
    return (x ^ (k >> 2), y ^ ((k >> 1) & 1), c ^ (k & 1))


def _scatter_copies(grad_refs, land_refs, send_sems, recv_sems):
    x, y, c, _ = _place()
    copies = []
    for t, (grad, land) in enumerate(zip(grad_refs, land_refs)):
        for k in range(1, N_DEV):
            peer = _peer(x, y, c, k)
            sem = (N_DEV - 1) * t + k - 1
            copies.append(pltpu.make_async_remote_copy(
                src_ref=grad.at[_slot(*peer)], dst_ref=land.at[k - 1],
                send_sem=send_sems.at[sem], recv_sem=recv_sems.at[sem],
                device_id=peer, device_id_type=MESH))
    return copies


def _gather_phases(ins, outs, send_sems, recv_sems, local_sems):
    n = len(ins)
    x, y, c, chips = _place()
    me, sibling = (x, y, c), (x, y, 1 - c)

    def copy(t, k, block, to, src=None):
        dst = outs[t].at[_slot(*block)]
        return pltpu.make_async_remote_copy(
            src_ref=dst if src is None else src, dst_ref=dst,
            send_sem=send_sems.at[7 * t + k], recv_sem=recv_sems.at[7 * t + k],
            device_id=to, device_id_type=MESH)

    def lists():
        mine = [pltpu.make_async_copy(ins[t], outs[t].at[_slot(*me)], local_sems.at[t]) for t in range(n)]
        first = []
        for t in range(n):
            first.append(copy(t, 0, me, sibling, src=ins[t]))
            first += [copy(t, 1 + j, me, (*chip, c), src=ins[t]) for j, chip in enumerate(chips)]
        passed = [copy(t, 4 + j, (*chip, c), sibling) for t in range(n) for j, chip in enumerate(chips)]
        return mine, first, passed

    def start():
        mine, first, _ = lists()
        for cp in mine + first:
            cp.start()

    def forward():
        _, _, passed = lists()
        for t in range(n):
            for j, chip in enumerate(chips):
                copy(t, 1 + j, (*chip, c), me).wait_recv()
                passed[3 * t + j].start()

    def finish():
        mine, first, passed = lists()
        for t in range(n):
            copy(t, 0, sibling, me).wait_recv()
            for j, chip in enumerate(chips):
                copy(t, 4 + j, (*chip, 1 - c), me).wait_recv()
        for cp in first + passed:
            cp.wait_send()
        for cp in mine:
            cp.wait()

    return start, forward, finish


def _gather_scratch(n):
    return [pltpu.SemaphoreType.DMA((7 * n,)), pltpu.SemaphoreType.DMA((7 * n,)), pltpu.SemaphoreType.DMA((n,))]


def _chip_sums(g):
    _, r, c = g.shape

    def body(g_ref, sums_ref, mine_ref, land, own, send_sems, recv_sems, load_sems):
        x, y, c_i, chips = _place()
        sibling = (x, y, 1 - c_i)
        blocks = [(*chip, 1 - c_i) for chip in chips] + [sibling]
        sends = [pltpu.make_async_remote_copy(
            src_ref=g_ref.at[_slot(*block)], dst_ref=land.at[k], send_sem=send_sems.at[k],
            recv_sem=recv_sems.at[k], device_id=sibling, device_id_type=MESH) for k, block in enumerate(blocks)]
        loads = [pltpu.make_async_copy(g_ref.at[_slot(*chip, c_i)], own.at[j], load_sems.at[j])
                 for j, chip in enumerate(chips)]
        for cp in sends + loads:
            cp.start()
        for cp in sends + loads:
            cp.wait()
        for j in range(3):
            sums_ref[j] = (own[j].astype(F32) + land[j].astype(F32)).astype(BF16)
        mine_ref[0] = land[3]

    return pl.pallas_call(
        body, name="chip_sums",
        in_specs=[ANY], out_specs=[VM, VM],
        out_shape=[SDS((3, r, c), BF16), SDS((1, r, c), BF16)],
        scratch_shapes=[pltpu.VMEM((4, r, c), BF16), pltpu.VMEM((3, r, c), BF16),
                        pltpu.SemaphoreType.DMA((4,)), pltpu.SemaphoreType.DMA((4,)), pltpu.SemaphoreType.DMA((3,))],
        compiler_params=_cparams(),
    )(g)


def _chip_copies(sums_ref, land_ref, send_sems, recv_sems):
    x, y, c, chips = _place()
    del x, y
    return [pltpu.make_async_remote_copy(
        src_ref=sums_ref.at[j], dst_ref=land_ref.at[j], send_sem=send_sems.at[j], recv_sem=recv_sems.at[j],
        device_id=(*chip, c), device_id_type=MESH) for j, chip in enumerate(chips)]


def _row_tile(rows):
    return min(rows, 256)


def _adamw(w, g, m, v):
    m2 = ADAM_B1 * m + (1.0 - ADAM_B1) * g
    v2 = ADAM_B2 * v + (1.0 - ADAM_B2) * jnp.square(g)
    m_hat = m2 / (1.0 - ADAM_B1 ** ADAM_STEP)
    v_hat = v2 / (1.0 - ADAM_B2 ** ADAM_STEP)
    delta = -ADAM_LR * (m_hat / (jnp.sqrt(v_hat) + ADAM_EPS) + ADAM_WD * w)
    return delta, m2, v2


def _reduce_adamw(name, own, partials, w, m, v):
    r, c = own.shape
    tr = _row_tile(r)
    n_p = len(partials)

    def body(own_ref, *rest):
        p_refs, (w_ref, m_ref, v_ref, grad_ref, d_ref, nm_ref, nv_ref) = rest[:n_p], rest[n_p:]
        grad = own_ref[...]
        for p_ref, (_, _, count) in zip(p_refs, partials):
            for j in range(count):
                grad = grad + p_ref[j].astype(F32)
        grad_ref[...] = grad
        d_ref[...], nm_ref[...], nv_ref[...] = _adamw(w_ref[...], grad, m_ref[...], v_ref[...])

    flat = pl.BlockSpec((tr, c), lambda i: (i, 0))
    return pl.pallas_call(
        body, name=name, grid=(r // tr,),
        in_specs=[flat] + [pl.BlockSpec((count, tr, c), lambda i, first=first, count=count: (first // count, i, 0))
                           for _, first, count in partials] + [flat, flat, flat],
        out_specs=[flat, flat, flat, flat],
        out_shape=[SDS((r, c), F32)] * 4,
        compiler_params=_cparams(),
    )(own, *[p[0] for p in partials], w, m, v)


VM = pl.BlockSpec()


def _small_allreduce(parts):
    n = len(parts)

    def body(*refs):
        ins, outs, lands = refs[:n], refs[n:2 * n], refs[2 * n:3 * n]
        send_sems, recv_sems = refs[3 * n:]
        x, y, c, _ = _place()
        my_slot = _slot(x, y, c)
        copies = []
        for t in range(n):
            lands[t][my_slot] = ins[t][...]
            for k in range(1, N_DEV):
                sem = (N_DEV - 1) * t + k - 1
                copies.append(pltpu.make_async_remote_copy(
                    src_ref=ins[t], dst_ref=lands[t].at[my_slot],
                    send_sem=send_sems.at[sem], recv_sem=recv_sems.at[sem],
                    device_id=_peer(x, y, c, k), device_id_type=MESH))
        for cp in copies:
            cp.start()
        for t in range(n):
            for k in range(1, N_DEV):
                sem = (N_DEV - 1) * t + k - 1
                pltpu.make_async_remote_copy(
                    src_ref=ins[t], dst_ref=lands[t].at[_slot(*_peer(x, y, c, k))],
                    send_sem=send_sems.at[sem], recv_sem=recv_sems.at[sem],
                    device_id=(x, y, c), device_id_type=MESH).wait_recv()
        for cp in copies:
            cp.wait_send()
        for t in range(n):
            total = lands[t][0]
            for s in range(1, N_DEV):
                total = total + lands[t][s]
            outs[t][...] = total

    n_sems = (N_DEV - 1) * n
    return pl.pallas_call(
        body, name="small_allreduce",
        in_specs=[VM] * n, out_specs=[VM] * n, out_shape=[SDS(p.shape, F32) for p in parts],
        scratch_shapes=[pltpu.VMEM((N_DEV, *p.shape), F32) for p in parts]
        + [pltpu.SemaphoreType.DMA((n_sems,)), pltpu.SemaphoreType.DMA((n_sems,))],
    )(*parts)


def _small_adamw(my_slot, sums, ws, ms, vs):
    n = len(ws)

    def body(slot_ref, *refs):
        sum_refs, refs = refs[:n + 1], refs[n + 1:]
        w_refs, m_refs, v_refs, refs = refs[:n], refs[n:2 * n], refs[2 * n:3 * n], refs[3 * n:]
        g_refs, d_refs, nm_refs, nv_refs = refs[:n + 1], refs[n + 1:2 * n + 1], refs[2 * n + 1:3 * n + 1], refs[3 * n + 1:]
        for t in range(n + 1):
            if t == 0:
                g = sum_refs[0][:, pl.ds(pl.multiple_of(slot_ref[0] * 128, 128), 128)]
            else:
                g = sum_refs[t][...]
            g_refs[t][...] = g
            if t < n:
                d_refs[t][...], nm_refs[t][...], nv_refs[t][...] = _adamw(w_refs[t][...], g, m_refs[t][...], v_refs[t][...])

    shapes = [SDS(w.shape, F32) for w in ws]
    outs = pl.pallas_call(
        body, name="small_adamw",
        in_specs=[pl.BlockSpec(memory_space=pltpu.SMEM)] + [VM] * (4 * n + 1),
        out_specs=[VM] * (4 * n + 1),
        out_shape=shapes + [SDS(sums[-1].shape, F32)] + shapes * 3,
    )(my_slot, *sums, *ws, *ms, *vs)
    return outs[:n + 1], outs[n + 1:2 * n + 1], outs[2 * n + 1:3 * n + 1], outs[3 * n + 1:]


def kernel(x, a_norm, a_w_in, a_rel_bias, a_w_out, kv_norm, kv_w, t5_bias, b_norm, b_w_in, b_sinks, b_w_out, final_norm, loss_target, m_a_norm, m_a_w_in, m_a_rel_bias, m_a_w_out, m_kv_norm, m_kv_w, m_t5_bias, m_b_norm, m_b_w_in, m_b_sinks, m_b_w_out, m_final_norm, v_a_norm, v_a_w_in, v_a_rel_bias, v_a_w_out, v_kv_norm, v_kv_w, v_t5_bias, v_b_norm, v_b_w_in, v_b_sinks, v_b_w_out, v_final_norm):
    xi, yi, ci = lax.axis_index("x"), lax.axis_index("y"), lax.axis_index("c")
    my_slot = _slot(xi, yi, ci)

    slot_arr = jnp.reshape(my_slot, (1,)).astype(jnp.int32)
    order = _gather_order(xi, yi, ci)
    late_shards = [b_w_in[0].astype(BF16), a_w_out[0].astype(BF16), b_w_out[0].astype(BF16), kv_w.astype(BF16)]
    grad_x, loc, matrices = _local_step(
        slot_arr, order, x[0], loss_target[0], a_norm, a_w_in[0].astype(BF16), a_rel_bias[0], late_shards,
        kv_norm.reshape(1, D_MODEL), t5_bias, b_norm, b_sinks, final_norm.reshape(1, D_MODEL))

    shard_w = dict(a_w_in=a_w_in[0], b_w_in=b_w_in[0], a_w_out=a_w_out[0], b_w_out=b_w_out[0], kv_w=kv_w)
    shard_m = dict(a_w_in=m_a_w_in[0], b_w_in=m_b_w_in[0], a_w_out=m_a_w_out[0], b_w_out=m_b_w_out[0], kv_w=m_kv_w)
    shard_v = dict(a_w_in=v_a_w_in[0], b_w_in=v_b_w_in[0], a_w_out=v_a_w_out[0], b_w_out=v_b_w_out[0], kv_w=v_kv_w)
    big = {n: _reduce_adamw("adamw_" + n, own, partials, shard_w[n], shard_m[n], shard_v[n])
           for n, (own, partials) in matrices.items()}

    names = ("a_norm", "a_rel_bias", "kv_norm", "t5_bias", "b_norm", "b_sinks", "final_norm")
    tables = ("a_rel_bias", "t5_bias")

    def row(n, a):
        return a.reshape(-1, a.shape[-1]).T if n in tables else a.reshape(1, -1)

    small_w = [row(n, a) for n, a in zip(names, (a_norm, a_rel_bias, kv_norm, t5_bias, b_norm, b_sinks, final_norm))]
    small_m = [row(n, a) for n, a in zip(names, (m_a_norm, m_a_rel_bias, m_kv_norm, m_t5_bias, m_b_norm, m_b_sinks,
                                                 m_final_norm))]
    small_v = [row(n, a) for n, a in zip(names, (v_a_norm, v_a_rel_bias, v_kv_norm, v_t5_bias, v_b_norm, v_b_sinks,
                                                 v_final_norm))]
    sums = dict(zip(names + ("loss",), _small_allreduce([loc[n] for n in names] + [loc["loss"]])))
    sums["a_rel_bias"] = _a_bias_grad(sums["a_rel_bias"])
    sums["t5_bias"] = sums["t5_bias"][:, :T5_BUCKETS]
    sums["b_sinks"] = sums["b_sinks"][:, 0].reshape(1, N_HEADS)
    results = _small_adamw(slot_arr, [sums[n] for n in names + ("loss",)], small_w, small_m, small_v)
    like = dict(a_norm=a_norm, a_rel_bias=a_rel_bias, kv_norm=kv_norm, t5_bias=t5_bias, b_norm=b_norm,
                b_sinks=b_sinks, final_norm=final_norm)
    sm = [{n: (part[i].T if n in tables else part[i]).reshape(like[n].shape) for i, n in enumerate(names)}
          for part in results]
    loss = results[0][len(names)][0, 0]

    order = ("a_norm", "a_w_in", "a_rel_bias", "a_w_out", "kv_norm", "kv_w", "t5_bias", "b_norm",
             "b_w_in", "b_sinks", "b_w_out", "final_norm")
    lead = dict(a_w_in=True, b_w_in=True, a_w_out=True, b_w_out=True, kv_w=False)

    def pick(kind, name):
        if name in big:
            val = big[name][kind]
            return val[None] if lead[name] else val
        return sm[kind][name]

    outs = [loss, grad_x[None]]
    for kind in range(4):
        outs += [pick(kind, n) for n in order]
    return tuple(outs)
```

```python
import functools
import math

import numpy as np
import jax
import jax.numpy as jnp
from jax import lax
from jax.experimental import pallas as pl
from jax.experimental.pallas import tpu as pltpu

F32 = jnp.float32
BF16 = jnp.bfloat16
SDS = jax.ShapeDtypeStruct

D_MODEL = 1024
HEAD_DIM = 64
CHUNK = 64
N_HEADS = 16
RMS_EPS = 1e-6
A_LEFT_CHUNKS = 8
A_BAND = (A_LEFT_CHUNKS + 1) * CHUNK
A_REL_CLIP = 256
B_KV_HEADS = 2
B_GROUP = 8
B_LEFT_CHUNKS = 2
B_BAND = (B_LEFT_CHUNKS + 1) * CHUNK
T5_BUCKETS = 32
T5_MAX_DIST = 128
QBLK = 256
A_KEYS = 3 * QBLK
B_QBLK_FWD = 128
B_QBLK_BWD = 256
B_PREV = 128
A_DIAG = A_KEYS
NEG = -1e30
SCALE = HEAD_DIM ** -0.5
N_DEV = 8

ADAM_LR = 0.001
ADAM_B1 = 0.9
ADAM_B2 = 0.999
ADAM_EPS = 1e-08
ADAM_WD = 0.01
ADAM_STEP = 10

VMEM_LIMIT_BYTES = 56 * 1024 * 1024
MESH = pl.DeviceIdType.MESH


def _cparams():
    return pltpu.CompilerParams(vmem_limit_bytes=VMEM_LIMIT_BYTES)


def _dot(a, b):
    return jnp.dot(a, b, preferred_element_type=F32)


def _dot_nt(a, b):
    return lax.dot_general(a, b, (((1,), (1,)), ((), ())), preferred_element_type=F32)


def _dot_tn(a, b):
    return lax.dot_general(a, b, (((0,), (0,)), ((), ())), preferred_element_type=F32)


def _rstd(xf):
    return lax.rsqrt(jnp.mean(xf * xf, axis=-1, keepdims=True) + RMS_EPS)


def _sigmoid(x):
    return 1.0 / (1.0 + jnp.exp(-x))


_GATHER_SEQUENCE = ((0, None), (1, 0), (2, 1), (4, None), (5, None), (3, 2), (6, None))


def _gather_order(x, y, c):
    others = [(1 - x, y), (x, 1 - y), (1 - x, 1 - y)]
    arrivals = [_slot(x, y, 1 - c)] + [_slot(*chip, c) for chip in others] + [_slot(*chip, 1 - c) for chip in others]
    return jnp.stack([_slot(x, y, c)] + [arrivals[a] for a, _ in _GATHER_SEQUENCE]).astype(jnp.int32)


def _norm_matmul_gather(order, x, gain_shard, w_shard):
    t = x.shape[0]
    dw, tn = w_shard.shape
    tm = min(t, 1024)
    n_m = t // tm

    def body(order_ref, x_ref, gs_ref, shard_ref, xn_ref, o_ref, full_ref, gain_ref,
             xn_all, wbuf, gland, send_sems, recv_sems, gsend_sems, grecv_sems, load_sems, own_sem):
        n, m = pl.program_id(0), pl.program_id(1)
        x_i, y_i, c_i, chips = _place()
        me, sibling = (x_i, y_i, c_i), (x_i, y_i, 1 - c_i)

        def send(k, block, to, src=None):
            dst = full_ref.at[_slot(*block)]
            return pltpu.make_async_remote_copy(
                src_ref=dst if src is None else src, dst_ref=dst,
                send_sem=send_sems.at[k], recv_sem=recv_sems.at[k], device_id=to, device_id_type=MESH)

        own = pltpu.make_async_copy(shard_ref, full_ref.at[_slot(*me)], own_sem)
        first = [send(0, me, sibling, src=shard_ref)]
        first += [send(1 + j, me, (*chip, c_i), src=shard_ref) for j, chip in enumerate(chips)]
        forwards = [send(4 + j, (*chip, c_i), sibling) for j, chip in enumerate(chips)]
        arrivals = [send(0, sibling, me)] + [send(1 + j, (*chip, c_i), me) for j, chip in enumerate(chips)]
        arrivals += [send(4 + j, (*chip, 1 - c_i), me) for j, chip in enumerate(chips)]
        gains = [pltpu.make_async_remote_copy(
            src_ref=gs_ref, dst_ref=gland.at[_slot(*me)], send_sem=gsend_sems.at[k - 1],
            recv_sem=grecv_sems.at[k - 1], device_id=_peer(x_i, y_i, c_i, k), device_id_type=MESH)
            for k in range(1, N_DEV)]

        @pl.when(jnp.logical_and(n == 0, m == 0))
        def _():
            own.start()
            for cp in gains + first:
                cp.start()
            pltpu.make_async_copy(shard_ref, wbuf.at[0], load_sems.at[0]).start()
            gland[_slot(*me)] = gs_ref[...]
            for k in range(1, N_DEV):
                pltpu.make_async_remote_copy(
                    src_ref=gs_ref, dst_ref=gland.at[_slot(*_peer(x_i, y_i, c_i, k))],
                    send_sem=gsend_sems.at[k - 1], recv_sem=grecv_sems.at[k - 1],
                    device_id=me, device_id_type=MESH).wait_recv()
            for s in range(N_DEV):
                gain_ref[:, 128 * s:128 * (s + 1)] = gland[s]

        rows = pl.ds(pl.multiple_of(m * tm, tm), tm)

        @pl.when(n == 0)
        def _():
            xf = x_ref[...]
            xn = ((xf * _rstd(xf)) * gain_ref[...]).astype(BF16)
            xn_all[rows, :] = xn
            xn_ref[...] = xn

        @pl.when(m == 0)
        def _():
            pltpu.make_async_copy(full_ref.at[0], wbuf.at[n % 2], load_sems.at[n % 2]).wait()

        o_ref[...] = _dot(xn_all[rows, :], wbuf[n % 2]).astype(BF16)

        for k, (arrival, forward) in enumerate(_GATHER_SEQUENCE):
            @pl.when(jnp.logical_and(n == k, m == n_m - 1))
            def _(k=k, arrival=arrival, forward=forward):
                arrivals[arrival].wait_recv()
                if forward is not None:
                    forwards[forward].start()
                pltpu.make_async_copy(full_ref.at[order_ref[k + 1]], wbuf.at[(k + 1) % 2],
                                      load_sems.at[(k + 1) % 2]).start()

        @pl.when(jnp.logical_and(n == N_DEV - 1, m == n_m - 1))
        def _():
            for cp in gains + first + forwards:
                cp.wait_send()
            own.wait()

    held = lambda n, m, order: (jnp.where(n == 0, m, n_m - 1), 0)
    return pl.pallas_call(
        body, name="norm_matmul_gather",
        grid_spec=pltpu.PrefetchScalarGridSpec(
            num_scalar_prefetch=1, grid=(N_DEV, n_m),
            in_specs=[pl.BlockSpec((tm, D_MODEL), held),
                      pl.BlockSpec((1, 128), lambda n, m, order: (0, 0)), ANY],
            out_specs=[pl.BlockSpec((tm, D_MODEL), held),
                       pl.BlockSpec((tm, tn), lambda n, m, order: (m, order[n])),
                       ANY, pl.BlockSpec((1, D_MODEL), lambda n, m, order: (0, 0))],
            scratch_shapes=[pltpu.VMEM((t, D_MODEL), BF16), pltpu.VMEM((2, dw, tn), BF16),
                            pltpu.VMEM((N_DEV, 1, 128), F32),
                            pltpu.SemaphoreType.DMA((7,)), pltpu.SemaphoreType.DMA((7,)),
                            pltpu.SemaphoreType.DMA((7,)), pltpu.SemaphoreType.DMA((7,)),
                            pltpu.SemaphoreType.DMA((2,)), pltpu.SemaphoreType.DMA]),
        out_shape=[SDS((t, D_MODEL), BF16), SDS((t, N_DEV * tn), BF16), SDS((N_DEV, dw, tn), BF16),
                   SDS((1, D_MODEL), F32)],
        compiler_params=_cparams(),
    )(order, x, gain_shard, w_shard)


def _layer_a_out(x, z, w_out, kv_gain, b_gain, kv_w, w_in_b):
    t = x.shape[0]
    tm = min(t, 512)
    nb, _, tn = w_in_b.shape

    def body(x_ref, z_ref, wo_ref, kvg_ref, bg_ref, kvw_ref, wb_ref,
             h1_ref, kvn_ref, hb_ref, kv_ref, qg_ref):
        h1 = x_ref[...] + _dot(z_ref[...], wo_ref[...])
        h1_ref[...] = h1
        y0 = h1 * _rstd(h1)
        kvn = (y0 * kvg_ref[...]).astype(BF16)
        hb = (y0 * bg_ref[...]).astype(BF16)
        kvn_ref[...] = kvn
        hb_ref[...] = hb
        kv_ref[...] = _dot(kvn, kvw_ref[...]).astype(BF16)
        for i in range(nb):
            qg_ref[:, i * tn:(i + 1) * tn] = _dot(hb, wb_ref[i]).astype(BF16)

    row = lambda m: (m, 0)
    fix2 = lambda m: (0, 0)
    return pl.pallas_call(
        body, name="layer_a_out", grid=(t // tm,),
        in_specs=[pl.BlockSpec((tm, D_MODEL), row), pl.BlockSpec((tm, D_MODEL), row),
                  pl.BlockSpec((D_MODEL, D_MODEL), fix2),
                  pl.BlockSpec((1, D_MODEL), fix2), pl.BlockSpec((1, D_MODEL), fix2),
                  pl.BlockSpec((D_MODEL, 256), fix2),
                  pl.BlockSpec((nb, D_MODEL, tn), lambda m: (0, 0, 0))],
        out_specs=[pl.BlockSpec((tm, D_MODEL), row), pl.BlockSpec((tm, D_MODEL), row),
                   pl.BlockSpec((tm, D_MODEL), row), pl.BlockSpec((tm, 256), row),
                   pl.BlockSpec((tm, nb * tn), row)],
        out_shape=[SDS((t, D_MODEL), F32), SDS((t, D_MODEL), BF16), SDS((t, D_MODEL), BF16),
                   SDS((t, 256), BF16), SDS((t, nb * tn), BF16)],
        compiler_params=_cparams(),
    )(x, z, w_out, kv_gain, b_gain, kv_w, w_in_b)


def _layer_b_out_loss(h1, z, w_out, f_gain, target):
    t = h1.shape[0]
    tm = min(t, 512)

    def body(h1_ref, z_ref, wo_ref, fg_ref, tgt_ref,
             dh2_ref, dh2b_ref, dz_ref, loss_ref, dfn_ref):
        @pl.when(pl.program_id(0) == 0)
        def _():
            loss_ref[...] = jnp.zeros_like(loss_ref)
            dfn_ref[...] = jnp.zeros_like(dfn_ref)

        h2 = h1_ref[...] + _dot(z_ref[...], wo_ref[...])
        r = _rstd(h2)
        yn = h2 * r
        fg = fg_ref[...]
        err = yn * fg - tgt_ref[...]
        loss_ref[...] += (0.5 / D_MODEL) * jnp.sum(err * err)
        dy = err * (1.0 / D_MODEL)
        dfn_ref[...] += jnp.sum(dy * yn, axis=0, keepdims=True)
        u = dy * fg
        dh2 = r * u - h2 * ((r * r * r) * jnp.mean(u * h2, axis=-1, keepdims=True))
        dh2_ref[...] = dh2
        dh2b = dh2.astype(BF16)
        dh2b_ref[...] = dh2b
        dz_ref[...] = _dot_nt(dh2b, wo_ref[...]).astype(BF16)

    row = lambda m: (m, 0)
    fix2 = lambda m: (0, 0)
    return pl.pallas_call(
        body, name="layer_b_out_loss", grid=(t // tm,),
        in_specs=[pl.BlockSpec((tm, D_MODEL), row), pl.BlockSpec((tm, D_MODEL), row),
                  pl.BlockSpec((D_MODEL, D_MODEL), fix2), pl.BlockSpec((1, D_MODEL), fix2),
                  pl.BlockSpec((tm, D_MODEL), row)],
        out_specs=[pl.BlockSpec((tm, D_MODEL), row), pl.BlockSpec((tm, D_MODEL), row),
                   pl.BlockSpec((tm, D_MODEL), row), pl.BlockSpec((1, 128), fix2),
                   pl.BlockSpec((1, D_MODEL), fix2)],
        out_shape=[SDS((t, D_MODEL), F32), SDS((t, D_MODEL), BF16), SDS((t, D_MODEL), BF16),
                   SDS((1, 128), F32), SDS((1, D_MODEL), F32)],
        compiler_params=_cparams(),
    )(h1, z, w_out, f_gain, target)


def _layer_b_in_bwd(dqg, dkv, w_in_b, kv_w, h1, dh2, b_gain, kv_gain, w_out_a):
    t = h1.shape[0]
    tm = min(t, 512)
    nb, _, tn = w_in_b.shape
    per = D_MODEL // tn

    def body(dqg_ref, dkv_ref, wb_ref, kvw_ref, h1_ref, dh2_ref, bg_ref, kvg_ref, wo_ref,
             dh1_ref, dh1b_ref, dz_ref, dbn_ref, dkn_ref):
        @pl.when(pl.program_id(0) == 0)
        def _():
            dbn_ref[...] = jnp.zeros_like(dbn_ref)
            dkn_ref[...] = jnp.zeros_like(dkn_ref)

        dhb = jnp.zeros((tm, D_MODEL), F32)
        for i in range(nb):
            blk = dqg_ref[i // per, :, (i % per) * tn:(i % per + 1) * tn]
            dhb = dhb + _dot_nt(blk, wb_ref[i])
        dkn = (_dot_nt(dkv_ref[0].astype(BF16), kvw_ref[:, 0:128])
               + _dot_nt(dkv_ref[1].astype(BF16), kvw_ref[:, 128:256]))
        h1 = h1_ref[...]
        r = _rstd(h1)
        xr = h1 * r
        dbn_ref[...] += jnp.sum(dhb * xr, axis=0, keepdims=True)
        dkn_ref[...] += jnp.sum(dkn * xr, axis=0, keepdims=True)
        u = dhb * bg_ref[...] + dkn * kvg_ref[...]
        dh1 = dh2_ref[...] + r * u - h1 * ((r * r * r) * jnp.mean(u * h1, axis=-1, keepdims=True))
        dh1_ref[...] = dh1
        dh1b = dh1.astype(BF16)
        dh1b_ref[...] = dh1b
        dz_ref[...] = _dot_nt(dh1b, wo_ref[...]).astype(BF16)

    row = lambda m: (m, 0)
    fix2 = lambda m: (0, 0)
    return pl.pallas_call(
        body, name="layer_b_in_bwd", grid=(t // tm,),
        in_specs=[pl.BlockSpec((2, tm, D_MODEL), lambda m: (0, m, 0)),
                  pl.BlockSpec((2, tm, 128), lambda m: (0, m, 0)),
                  pl.BlockSpec((nb, D_MODEL, tn), lambda m: (0, 0, 0)),
                  pl.BlockSpec((D_MODEL, 256), fix2),
                  pl.BlockSpec((tm, D_MODEL), row), pl.BlockSpec((tm, D_MODEL), row),
                  pl.BlockSpec((1, D_MODEL), fix2), pl.BlockSpec((1, D_MODEL), fix2),
                  pl.BlockSpec((D_MODEL, D_MODEL), fix2)],
        out_specs=[pl.BlockSpec((tm, D_MODEL), row), pl.BlockSpec((tm, D_MODEL), row),
                   pl.BlockSpec((tm, D_MODEL), row), pl.BlockSpec((1, D_MODEL), fix2),
                   pl.BlockSpec((1, D_MODEL), fix2)],
        out_shape=[SDS((t, D_MODEL), F32), SDS((t, D_MODEL), BF16), SDS((t, D_MODEL), BF16),
                   SDS((1, D_MODEL), F32), SDS((1, D_MODEL), F32)],
        compiler_params=_cparams(),
    )(dqg, dkv, w_in_b, kv_w, h1, dh2, b_gain, kv_gain, w_out_a)


def _layer_a_in_bwd(dqg, dkv, w_in_a, x, dh1, a_gain, chip_sums):
    t = x.shape[0]
    tm = min(t, 512)
    nb, _, tn = w_in_a.shape
    per = D_MODEL // tn

    def body(dqg_ref, dkv_ref, w_ref, x_ref, dh1_ref, ag_ref, sums_ref, dx_ref, dan_ref, land_ref,
             send_sems, recv_sems):
        @pl.when(pl.program_id(0) == 0)
        def _():
            dan_ref[...] = jnp.zeros_like(dan_ref)
            for cp in _chip_copies(sums_ref, land_ref, send_sems, recv_sems):
                cp.start()

        dxn = jnp.zeros((tm, D_MODEL), F32)
        for i in range(nb):
            part = i // per
            src = dqg_ref if part in (0, 3) else dkv_ref
            outer = {0: 0, 3: 1, 1: 0, 2: 1}[part]
            blk = src[outer, :, (i % per) * tn:(i % per + 1) * tn]
            dxn = dxn + _dot_nt(blk, w_ref[i])
        xf = x_ref[...]
        r = _rstd(xf)
        dan_ref[...] += jnp.sum(dxn * (xf * r), axis=0, keepdims=True)
        u = dxn * ag_ref[...]
        dx_ref[...] = dh1_ref[...] + r * u - xf * ((r * r * r) * jnp.mean(u * xf, axis=-1, keepdims=True))

        @pl.when(pl.program_id(0) == t // tm - 1)
        def _():
            for cp in _chip_copies(sums_ref, land_ref, send_sems, recv_sems):
                cp.wait()

    row = lambda m: (m, 0)
    fix2 = lambda m: (0, 0)
    return pl.pallas_call(
        body, name="layer_a_in_bwd", grid=(t // tm,),
        in_specs=[pl.BlockSpec((2, tm, D_MODEL), lambda m: (0, m, 0)),
                  pl.BlockSpec((2, tm, D_MODEL), lambda m: (0, m, 0)),
                  pl.BlockSpec((nb, D_MODEL, tn), lambda m: (0, 0, 0)),
                  pl.BlockSpec((tm, D_MODEL), row), pl.BlockSpec((tm, D_MODEL), row),
                  pl.BlockSpec((1, D_MODEL), fix2), ANY],
        out_specs=[pl.BlockSpec((tm, D_MODEL), row), pl.BlockSpec((1, D_MODEL), fix2), ANY],
        out_shape=[SDS((t, D_MODEL), F32), SDS((1, D_MODEL), F32), SDS(chip_sums.shape, chip_sums.dtype)],
        scratch_shapes=[pltpu.SemaphoreType.DMA((3,)), pltpu.SemaphoreType.DMA((3,))],
        compiler_params=_cparams(),
    )(dqg, dkv, w_in_a, x, dh1, a_gain, chip_sums)


def _lut(s, vals):
    r = jnp.int32(vals[0])
    for i in range(1, len(vals)):
        r = jnp.where(s == i, jnp.int32(vals[i]), r)
    return r


def _held(steps, i):
    seq, cur = [None] * len(steps), None
    for k in range(len(steps) - 1, -1, -1):
        if steps[k][0] == i:
            cur = steps[k][1:3]
        seq[k] = cur
    for k in range(len(steps)):
        cur = seq[k] = seq[k] if seq[k] is not None else cur
    return seq


def _weight_grad_cols(name, my_slot, a, bs, steps, tn):
    t, dw = a.shape
    n_arr = len(bs)
    which = [s[0] for s in steps]
    blks = [s[3] for s in steps]

    def body(slot_ref, a_ref, *rest):
        b_refs, (o_ref, own_ref, at_ref) = rest[:n_arr], rest[n_arr:]
        s = pl.program_id(0)

        @pl.when(s == 0)
        def _():
            at_ref[...] = a_ref[...].T

        for i in range(n_arr):
            @pl.when(_lut(s, which) == i)
            def _(i=i):
                res = _dot(at_ref[...], b_refs[i][0])
                o_ref[0] = res.astype(BF16)

                @pl.when(_lut(s, blks) == slot_ref[0])
                def _():
                    own_ref[...] = res

    def b_spec(i):
        held = _held(steps, i)
        return pl.BlockSpec((1, t, tn), lambda s, slot: (_lut(s, [h[0] for h in held]), 0,
                                                         _lut(s, [h[1] for h in held])))

    return pl.pallas_call(
        body, name=name,
        grid_spec=pltpu.PrefetchScalarGridSpec(
            num_scalar_prefetch=1, grid=(len(steps),),
            in_specs=[pl.BlockSpec((t, dw), lambda s, slot: (0, 0))] + [b_spec(i) for i in range(n_arr)],
            out_specs=[pl.BlockSpec((1, dw, tn), lambda s, slot: (_lut(s, blks), 0, 0)),
                       pl.BlockSpec((dw, tn), lambda s, slot: (0, 0))],
            scratch_shapes=[pltpu.VMEM((dw, t), BF16)]),
        out_shape=[SDS((N_DEV, dw, tn), BF16), SDS((dw, tn), F32)],
        compiler_params=_cparams(),
    )(my_slot, a, *bs)


def _weight_grad_rows(name, my_slot, a, b):
    t, dw = a.shape
    n_o, _, c = b.shape
    rows = dw // N_DEV
    tn = min(c, 256)
    per = c // tn

    def body(slot_ref, a_ref, b_ref, o_ref, own_ref, at_ref, res_ref):
        @pl.when(pl.program_id(0) == 0)
        def _():
            at_ref[...] = a_ref[...].T

        res_ref[...] = _dot(at_ref[...], b_ref[0].astype(BF16))
        o_ref[...] = res_ref[...].astype(BF16)
        own_ref[...] = res_ref[pl.ds(pl.multiple_of(slot_ref[0] * rows, rows), rows), :]

    all_rows, own = pl.pallas_call(
        body, name=name,
        grid_spec=pltpu.PrefetchScalarGridSpec(
            num_scalar_prefetch=1, grid=(n_o * per,),
            in_specs=[pl.BlockSpec((t, dw), lambda s, slot: (0, 0)),
                      pl.BlockSpec((1, t, tn), lambda s, slot: (s // per, 0, s % per))],
            out_specs=[pl.BlockSpec((dw, tn), lambda s, slot: (0, s)),
                       pl.BlockSpec((rows, tn), lambda s, slot: (0, s))],
            scratch_shapes=[pltpu.VMEM((dw, t), BF16), pltpu.VMEM((dw, tn), F32)]),
        out_shape=[SDS((dw, n_o * c), BF16), SDS((rows, n_o * c), F32)],
        compiler_params=_cparams(),
    )(my_slot, a, b)
    return all_rows.reshape(N_DEV, rows, n_o * c), own


def _lane_lo():
    return lax.broadcasted_iota(jnp.int32, (1, 128), 1) < HEAD_DIM


def _collapse_chunks(ds, keys):
    if ds.shape[1] < keys:
        ds = jnp.concatenate([jnp.zeros((ds.shape[0], keys - ds.shape[1]), F32), ds], axis=1)
    gc = ds[0:CHUNK]
    for cc in range(1, ds.shape[0] // CHUNK):
        gc = gc + pltpu.roll(ds[cc * CHUNK:(cc + 1) * CHUNK], keys - cc * CHUNK, 1)
    return gc


def _offset_sums(gc):
    hi = gc.astype(BF16)
    lo = (gc - hi.astype(F32)).astype(BF16)
    flip = (lax.broadcasted_iota(jnp.int32, (CHUNK, CHUNK), 0)
            + lax.broadcasted_iota(jnp.int32, (CHUNK, CHUNK), 1) == CHUNK - 1).astype(BF16)
    gf = _dot(flip, hi) + _dot(flip, lo)
    skew = pltpu.roll(gf, 0, 1, stride=1, stride_axis=0)
    return jnp.sum(skew, axis=0, keepdims=True)


def _band_bias(w_row, band, rows):
    keys = w_row.shape[1]
    base = jnp.broadcast_to(w_row, (CHUNK, keys))
    skew = pltpu.roll(base, 0, 1, stride=1, stride_axis=0)
    skew = pltpu.roll(skew, keys - (CHUNK - 1), 1)
    col = lax.broadcasted_iota(jnp.int32, (CHUNK, keys), 1)
    chunk0 = jnp.where(col < band, skew, NEG)
    return jnp.concatenate(
        [chunk0] + [pltpu.roll(chunk0, cc * CHUNK, 1) for cc in range(1, rows // CHUNK)], axis=0)


def _silu_parts(g):
    sg = _sigmoid(g)
    return g * sg, sg * (1.0 + g * (1.0 - sg))


A_PAIRS_FWD = 4
A_PAIRS_BWD = 4


def _a_specs(pairs):
    lanes = 128 * pairs
    steps = D_MODEL // lanes
    q = pl.BlockSpec((QBLK, lanes), lambda p, j: (j, p))
    ks = [pl.BlockSpec((QBLK, lanes), lambda p, j, b=b: (jnp.maximum(j - 2 + b, 0), steps + p)) for b in range(3)]
    vs = [pl.BlockSpec((QBLK, lanes), lambda p, j, b=b: (jnp.maximum(j - 2 + b, 0), 2 * steps + p))
          for b in range(3)]
    g = pl.BlockSpec((QBLK, lanes), lambda p, j: (j, 3 * steps + p))
    bias = pl.BlockSpec((pairs, 8, A_KEYS), lambda p, j: (p, 0, 0))
    return q, ks, vs, g, bias


def _a_fill_bias(w_ref, b_ref, j, pairs):
    _fill_bias(2 * pairs, lambda h: w_ref[h // 2, h % 2:h % 2 + 1, :], A_BAND, b_ref, j)


def _by_valid_key_blocks(j, fn):
    pl.when(j == 0)(functools.partial(fn, 1))
    pl.when(j == 1)(functools.partial(fn, 2))
    pl.when(j >= 2)(functools.partial(fn, 3))


def _fill_bias(n, get_row, band, bias_scr, j):
    @pl.when(j == 0)
    def _():
        for h in range(n):
            bias_scr[h] = _band_bias(get_row(h), band, bias_scr.shape[1])


def _normalise_pair(rs, mxs, lane_lo, extra=None):
    num = jnp.where(lane_lo, rs[0], rs[1])
    den = pltpu.roll(jnp.where(lane_lo, rs[1], rs[0]), HEAD_DIM, 1)
    if extra is not None:
        den = den + jnp.where(lane_lo, extra[0], extra[1])
    return num / den, jnp.where(lane_lo, mxs[0], mxs[1]) + jnp.log(den)


def _own_everywhere(x, sel):
    return jnp.where(sel, x, pltpu.roll(x, HEAD_DIM, 1))


def _minus_rows(s, row_full):
    return jnp.concatenate([s[:, i:i + 128] - row_full for i in range(0, s.shape[1], 128)], axis=1)


def _attn_a_fwd(qkvg, bias, gather):
    t = qkvg.shape[0]
    nq = t // QBLK
    n_g = len(gather)
    pairs = A_PAIRS_FWD
    lanes = 128 * pairs
    steps = D_MODEL // lanes
    q_spec, k_specs, v_specs, g_spec, bias_spec = _a_specs(pairs)

    def body(q_ref, k0, k1, k2, v0, v1, v2, g_ref, w_ref, *rest):
        shard_refs, rest = rest[:n_g], rest[n_g:]
        z_ref, o_ref, lse_ref = rest[:3]
        full_refs, (b_ref, *comm) = rest[3:3 + n_g], rest[3 + n_g:]
        p = pl.program_id(0)
        j = pl.program_id(1)
        start, forward, finish = _gather_phases(shard_refs, full_refs, *comm)
        pl.when(jnp.logical_and(p == 0, j == 0))(start)
        pl.when(jnp.logical_and(p == steps // 2, j == 0))(forward)
        _a_fill_bias(w_ref, b_ref, j, pairs)
        lane_lo = _lane_lo()
        sels = (lane_lo, jnp.logical_not(lane_lo))

        def attend(n_blocks):
            first_col = (3 - n_blocks) * QBLK
            for pp in range(pairs):
                cols = slice(128 * pp, 128 * (pp + 1))
                k = jnp.concatenate([r[:, cols] for r in (k0, k1, k2)[3 - n_blocks:]], axis=0)
                v = jnp.concatenate([r[:, cols] for r in (v0, v1, v2)[3 - n_blocks:]], axis=0)
                q = q_ref[:, cols]
                qm2 = jnp.concatenate([jnp.where(sel, q, jnp.zeros_like(q)) for sel in sels], axis=0) * SCALE
                s2 = _dot_nt(qm2, k)
                rs, mxs = [], []
                for hh, sel in enumerate(sels):
                    s = s2[hh * QBLK:(hh + 1) * QBLK] + b_ref[2 * pp + hh, :, first_col:]
                    mxs.append(jnp.max(s, axis=-1, keepdims=True))
                    e = jnp.exp(s - mxs[hh]).astype(BF16)
                    rs.append(_dot(e, jnp.where(sel, v, jnp.ones_like(v))))
                o, lse = _normalise_pair(rs, mxs, lane_lo)
                silu, _ = _silu_parts(g_ref[:, cols].astype(F32))
                o_ref[:, cols] = o.astype(BF16)
                z_ref[:, cols] = (o * silu).astype(BF16)
                lse_ref[:, cols] = lse

        _by_valid_key_blocks(j, attend)
        pl.when(jnp.logical_and(p == steps - 1, j == nq - 1))(finish)

    out_spec = pl.BlockSpec((QBLK, lanes), lambda p, j: (j, p))
    outs = pl.pallas_call(
        body, name="attn_a_fwd", grid=(steps, nq),
        in_specs=[q_spec, *k_specs, *v_specs, g_spec, bias_spec] + [ANY] * n_g,
        out_specs=[out_spec, out_spec, out_spec] + [ANY] * n_g,
        out_shape=[SDS((t, D_MODEL), BF16), SDS((t, D_MODEL), BF16), SDS((t, D_MODEL), F32)]
        + [SDS((N_DEV, *s.shape), s.dtype) for s in gather],
        scratch_shapes=[pltpu.VMEM((2 * pairs, QBLK, A_KEYS), F32)] + _gather_scratch(n_g),
        compiler_params=_cparams(),
    )(qkvg, qkvg, qkvg, qkvg, qkvg, qkvg, qkvg, qkvg, bias, *gather)
    return outs[0], outs[1], outs[2], list(outs[3:])


def _attn_a_bwd(qkvg, bias, out_a, lse, dz, scatter):
    t = qkvg.shape[0]
    nq = t // QBLK
    n_sc = len(scatter)
    pairs = A_PAIRS_BWD
    lanes = 128 * pairs
    steps = D_MODEL // lanes
    q_spec, k_specs, v_specs, g_spec, bias_spec = _a_specs(pairs)

    def body(q_ref, k0, k1, k2, v0, v1, v2, g_ref, w_ref, o_ref, lse_ref, dz_ref, *rest):
        sc_refs, rest = rest[:n_sc], rest[n_sc:]
        dqg_ref, dkv_ref, dg_ref = rest[:3]
        land_refs, rest = rest[3:3 + n_sc], rest[3 + n_sc:]
        dk_acc, dv_acc, gt_acc, b_ref, send_sems, recv_sems = rest
        j = pl.program_id(1)
        first = jnp.logical_and(pl.program_id(0) == 0, j == 0)
        last = jnp.logical_and(pl.program_id(0) == steps - 1, j == nq - 1)

        @pl.when(first)
        def _():
            for cp in _scatter_copies(sc_refs, land_refs, send_sems, recv_sems):
                cp.start()

        _a_fill_bias(w_ref, b_ref, j, pairs)

        @pl.when(j == 0)
        def _():
            dk_acc[...] = jnp.zeros_like(dk_acc)
            dv_acc[...] = jnp.zeros_like(dv_acc)
            gt_acc[...] = jnp.zeros_like(gt_acc)

        lane_lo = _lane_lo()
        sels = (lane_lo, jnp.logical_not(lane_lo))

        def attend(n_blocks):
            first_col = (3 - n_blocks) * QBLK
            for pp in range(pairs):
                cols = slice(128 * pp, 128 * (pp + 1))
                q = q_ref[:, cols]
                k = jnp.concatenate([r[:, cols] for r in (k0, k1, k2)[3 - n_blocks:]], axis=0)
                v = jnp.concatenate([r[:, cols] for r in (v0, v1, v2)[3 - n_blocks:]], axis=0)
                o = o_ref[:, cols].astype(F32)
                lse_pair = lse_ref[:, cols]
                dzf = dz_ref[:, cols].astype(F32)
                silu, dsilu = _silu_parts(g_ref[:, cols].astype(F32))
                do = dzf * silu
                dqg_ref[1, :, cols] = (dzf * o * dsilu).astype(BF16)
                doo = do * o
                qm2 = jnp.concatenate([jnp.where(sel, q, jnp.zeros_like(q)) for sel in sels], axis=0) * SCALE
                dom2 = jnp.concatenate([jnp.where(sel, do, 0.0) for sel in sels], axis=0).astype(BF16)
                s2 = _dot_nt(qm2, k)
                dp2 = _dot_nt(dom2, v)
                ps, dss = [], []
                for hh, sel in enumerate(sels):
                    rows = slice(hh * QBLK, (hh + 1) * QBLK)
                    s = s2[rows] + b_ref[2 * pp + hh, :, first_col:]
                    p = jnp.exp(_minus_rows(s, _own_everywhere(lse_pair, sel)))
                    delta = jnp.sum(jnp.where(sel, doo, 0.0), axis=-1, keepdims=True)
                    ds = p * (dp2[rows] - delta)
                    gt_acc[2 * pp + hh] += _collapse_chunks(ds, A_KEYS)
                    ps.append(p.astype(BF16))
                    dss.append(ds.astype(BF16))
                dsb2 = jnp.concatenate(dss, axis=0)
                dq2 = _dot(dsb2, k) * SCALE
                dk_blk = _dot_tn(dsb2, qm2)
                dv_blk = _dot_tn(jnp.concatenate(ps, axis=0), dom2)
                dqg_ref[0, :, cols] = jnp.where(lane_lo, dq2[0:QBLK], dq2[QBLK:2 * QBLK]).astype(BF16)
                for b in range(n_blocks):
                    rows = pl.ds(pl.multiple_of((j - n_blocks + 1 + b) * QBLK, QBLK), QBLK)
                    dk_acc[rows, cols] += dk_blk[b * QBLK:(b + 1) * QBLK]
                    dv_acc[rows, cols] += dv_blk[b * QBLK:(b + 1) * QBLK]

        _by_valid_key_blocks(j, attend)

        @pl.when(j == nq - 1)
        def _():
            dkv_ref[0] = dk_acc[...].astype(BF16)
            dkv_ref[1] = dv_acc[...].astype(BF16)
            for pp in range(pairs):
                dg_ref[pp] = jnp.concatenate([_offset_sums(gt_acc[2 * pp]), _offset_sums(gt_acc[2 * pp + 1]),
                                              jnp.zeros((6, A_DIAG), F32)], axis=0)

        @pl.when(last)
        def _():
            for cp in _scatter_copies(sc_refs, land_refs, send_sems, recv_sems):
                cp.wait()

    blk = pl.BlockSpec((QBLK, lanes), lambda p, j: (j, p))
    outs = pl.pallas_call(
        body, name="attn_a_bwd", grid=(steps, nq),
        in_specs=[q_spec, *k_specs, *v_specs, g_spec, bias_spec, blk, blk, blk] + [ANY] * n_sc,
        out_specs=[pl.BlockSpec((2, QBLK, lanes), lambda p, j: (0, j, p)),
                   pl.BlockSpec((2, t, lanes), lambda p, j: (0, 0, p)),
                   pl.BlockSpec((pairs, 8, A_DIAG), lambda p, j: (p, 0, 0))] + [ANY] * n_sc,
        out_shape=[SDS((2, t, D_MODEL), BF16), SDS((2, t, D_MODEL), BF16), SDS((N_HEADS // 2, 8, A_DIAG), F32)]
        + [SDS((N_DEV - 1, *g.shape[1:]), g.dtype) for g in scatter],
        scratch_shapes=[pltpu.VMEM((t, lanes), F32), pltpu.VMEM((t, lanes), F32),
                        pltpu.VMEM((2 * pairs, CHUNK, A_KEYS), F32), pltpu.VMEM((2 * pairs, QBLK, A_KEYS), F32),
                        pltpu.SemaphoreType.DMA(((N_DEV - 1) * n_sc,)),
                        pltpu.SemaphoreType.DMA(((N_DEV - 1) * n_sc,))],
        compiler_params=_cparams(),
    )(qkvg, qkvg, qkvg, qkvg, qkvg, qkvg, qkvg, qkvg, bias, out_a, lse, dz, *scatter)
    return outs[0], outs[1], outs[2], list(outs[3:])


def _b_specs(qblk):
    per = qblk // B_PREV
    q = pl.BlockSpec((qblk, 512), lambda h, j: (j, h))
    g = pl.BlockSpec((qblk, 512), lambda h, j: (j, 2 + h))
    kp = pl.BlockSpec((B_PREV, 128), lambda h, j: (jnp.maximum(per * j - 1, 0), 0))
    kc = pl.BlockSpec((qblk, 128), lambda h, j: (j, 0))
    vp = pl.BlockSpec((B_PREV, 128), lambda h, j: (jnp.maximum(per * j - 1, 0), 1))
    vc = pl.BlockSpec((qblk, 128), lambda h, j: (j, 1))
    bias = pl.BlockSpec((B_GROUP, qblk + B_PREV), lambda h, j: (h, 0))
    sinks = pl.BlockSpec(memory_space=pltpu.SMEM)
    return q, g, kp, kc, vp, vc, bias, sinks


def _b_operands(kp, kc, vp, vc, kvh, with_prev):
    k = jnp.concatenate([kp[...], kc[...]], axis=0) if with_prev else kc[...]
    v = jnp.concatenate([vp[...], vc[...]], axis=0) if with_prev else vc[...]
    kr = pltpu.roll(k, HEAD_DIM, 1)
    vr = pltpu.roll(v, HEAD_DIM, 1)
    first = kvh == 0
    return (jnp.where(first, k, kr), jnp.where(first, kr, k),
            jnp.where(first, v, vr), jnp.where(first, vr, v))


def _attn_b_fwd(qg, kv, bias, sinks):
    t = qg.shape[0]
    qblk = B_QBLK_FWD
    per_step = 2
    step = per_step * qblk
    q_spec, g_spec, kp_spec, kc_spec, vp_spec, vc_spec, _, sink_spec = _b_specs(step)
    bias_spec = pl.BlockSpec((B_GROUP, qblk + B_PREV), lambda h, j: (h, 0))

    def body(q_ref, g_ref, kp, kc, vp, vc, w_ref, sink_ref, z_ref, o_ref, lse_ref, b_ref):
        kvh = pl.program_id(0)
        j = pl.program_id(1)
        _fill_bias(B_GROUP, lambda h: w_ref[h:h + 1, :], B_BAND, b_ref, j)
        lane_lo = _lane_lo()
        n_pairs = B_GROUP // 2

        def attend(first):
            k_lo, k_hi, v_lo, v_hi = _b_operands(kp, kc, vp, vc, kvh, True)
            for sb in range(per_step):
                no_prev = first and sb == 0
                first_col = B_PREV if no_prev else 0
                keys = slice(sb * qblk + first_col, (sb + 1) * qblk + B_PREV)
                qrows = slice(sb * qblk, (sb + 1) * qblk)
                halves = []
                for hh, sel in enumerate((lane_lo, jnp.logical_not(lane_lo))):
                    kk = (k_lo if hh == 0 else k_hi)[keys]
                    vv = (v_lo if hh == 0 else v_hi)[keys]
                    qm4 = jnp.concatenate(
                        [jnp.where(sel, q_ref[qrows, 128 * pp:128 * (pp + 1)], jnp.zeros((qblk, 128), BF16))
                         for pp in range(n_pairs)], axis=0) * SCALE
                    s4 = _dot_nt(qm4, kk)
                    es, mxs = [], []
                    for pp in range(n_pairs):
                        g = 2 * pp + hh
                        s = s4[pp * qblk:(pp + 1) * qblk] + b_ref[g, :, first_col:]
                        mxs.append(jnp.maximum(jnp.max(s, axis=-1, keepdims=True), sink_ref[kvh * B_GROUP + g]))
                        es.append(jnp.exp(s - mxs[pp]).astype(BF16))
                    r4 = _dot(jnp.concatenate(es, axis=0), jnp.where(sel, vv, jnp.ones_like(vv)))
                    halves.append((r4, mxs))
                for pp in range(n_pairs):
                    cols = slice(128 * pp, 128 * (pp + 1))
                    rows = slice(pp * qblk, (pp + 1) * qblk)
                    mxs = [halves[hh][1][pp] for hh in range(2)]
                    sink_terms = [jnp.exp(sink_ref[kvh * B_GROUP + 2 * pp + hh] - mxs[hh]) for hh in range(2)]
                    o, lse = _normalise_pair([halves[hh][0][rows] for hh in range(2)], mxs, lane_lo, sink_terms)
                    silu, _ = _silu_parts(g_ref[qrows, cols].astype(F32))
                    o_ref[qrows, cols] = o.astype(BF16)
                    z_ref[qrows, cols] = (o * silu).astype(BF16)
                    lse_ref[qrows, cols] = lse

        pl.when(j == 0)(functools.partial(attend, True))
        pl.when(j >= 1)(functools.partial(attend, False))

    out_spec = pl.BlockSpec((step, 512), lambda h, j: (j, h))
    return pl.pallas_call(
        body, name="attn_b_fwd", grid=(B_KV_HEADS, t // step),
        in_specs=[q_spec, g_spec, kp_spec, kc_spec, vp_spec, vc_spec, bias_spec, sink_spec],
        out_specs=[out_spec, out_spec, out_spec],
        out_shape=[SDS((t, D_MODEL), BF16), SDS((t, D_MODEL), BF16), SDS((t, D_MODEL), F32)],
        scratch_shapes=[pltpu.VMEM((B_GROUP, qblk, qblk + B_PREV), F32)],
        compiler_params=_cparams(),
    )(qg, qg, kv, kv, kv, kv, bias, sinks)


def _attn_b_bwd(qg, kv, bias, sinks, out_b, lse, dz, bucket_onehot):
    t = qg.shape[0]
    qblk = B_QBLK_BWD
    keys = qblk + B_PREV
    nq = t // qblk
    q_spec, g_spec, kp_spec, kc_spec, vp_spec, vc_spec, bias_spec, sink_spec = _b_specs(qblk)

    def body(q_ref, g_ref, kp, kc, vp, vc, w_ref, sink_ref, o_ref, lse_ref, dz_ref, oh_ref,
             dqg_ref, dkv_ref, dt5_ref, dsink_ref, gt_acc, b_ref):
        kvh = pl.program_id(0)
        j = pl.program_id(1)
        _fill_bias(B_GROUP, lambda h: w_ref[h:h + 1, :], B_BAND, b_ref, j)

        @pl.when(jnp.logical_and(kvh == 0, j == 0))
        def _():
            dkv_ref[...] = jnp.zeros_like(dkv_ref)

        @pl.when(j == 0)
        def _():
            gt_acc[...] = jnp.zeros_like(gt_acc)
            dsink_ref[...] = jnp.zeros_like(dsink_ref)

        lane_lo = _lane_lo()

        def attend(with_prev):
            first_col = 0 if with_prev else B_PREV
            k_lo, k_hi, v_lo, v_hi = _b_operands(kp, kc, vp, vc, kvh, with_prev)
            dk_blk = jnp.zeros((keys - first_col, 128), F32)
            dv_blk = jnp.zeros((keys - first_col, 128), F32)
            for pp in range(B_GROUP // 2):
                cols = slice(128 * pp, 128 * (pp + 1))
                qp = q_ref[:, cols]
                o = o_ref[:, cols].astype(F32)
                lse_pair = lse_ref[:, cols]
                dzf = dz_ref[:, cols].astype(F32)
                silu, dsilu = _silu_parts(g_ref[:, cols].astype(F32))
                do = dzf * silu
                dqg_ref[1, :, cols] = (dzf * o * dsilu).astype(BF16)
                doo = do * o
                dqs = []
                for hh in range(2):
                    g = 2 * pp + hh
                    sel = lane_lo if hh == 0 else jnp.logical_not(lane_lo)
                    sink = sink_ref[kvh * B_GROUP + g]
                    kk = k_lo if hh == 0 else k_hi
                    vv = v_lo if hh == 0 else v_hi
                    qm = jnp.where(sel, qp, jnp.zeros_like(qp)) * SCALE
                    s = _dot_nt(qm, kk) + b_ref[g, :, first_col:]
                    lse_h = _own_everywhere(lse_pair, sel)
                    p = jnp.exp(_minus_rows(s, lse_h))
                    delta = jnp.sum(jnp.where(sel, doo, 0.0), axis=-1, keepdims=True)
                    dom = jnp.where(sel, do, 0.0).astype(BF16)
                    dp = _dot_nt(dom, vv)
                    ds = p * (dp - delta)
                    gt_acc[g, :, first_col:] += ds
                    dsink_ref[g:g + 1, :] -= jnp.sum(jnp.exp(sink - lse_h) * delta, axis=0, keepdims=True)
                    dsb = ds.astype(BF16)
                    dqs.append(_dot(dsb, kk) * SCALE)
                    dk_blk = dk_blk + _dot_tn(dsb, qm)
                    dv_blk = dv_blk + _dot_tn(p.astype(BF16), dom)
                dqg_ref[0, :, cols] = jnp.where(lane_lo, dqs[0], dqs[1]).astype(BF16)
            mine = lane_lo == (kvh == 0)
            dk_add = jnp.where(mine, dk_blk + pltpu.roll(dk_blk, HEAD_DIM, 1), 0.0)
            dv_add = jnp.where(mine, dv_blk + pltpu.roll(dv_blk, HEAD_DIM, 1), 0.0)
            first_key = B_PREV if with_prev else 0
            if with_prev:
                rows = pl.ds(pl.multiple_of(j * qblk - B_PREV, B_PREV), B_PREV)
                dkv_ref[0, rows, :] += dk_add[0:B_PREV]
                dkv_ref[1, rows, :] += dv_add[0:B_PREV]
            rows = pl.ds(pl.multiple_of(j * qblk, qblk), qblk)
            dkv_ref[0, rows, :] += dk_add[first_key:first_key + qblk]
            dkv_ref[1, rows, :] += dv_add[first_key:first_key + qblk]

        pl.when(j == 0)(functools.partial(attend, False))
        pl.when(j >= 1)(functools.partial(attend, True))

        @pl.when(j == nq - 1)
        def _():
            dd = jnp.concatenate([_offset_sums(_collapse_chunks(gt_acc[g], keys)) for g in range(B_GROUP)], axis=0)
            hi = dd.astype(BF16)
            lo = (dd - hi.astype(F32)).astype(BF16)
            dt5_ref[...] = _dot(hi, oh_ref[...]) + _dot(lo, oh_ref[...])

    blk = pl.BlockSpec((qblk, 512), lambda h, j: (j, h))
    return pl.pallas_call(
        body, name="attn_b_bwd", grid=(B_KV_HEADS, nq),
        in_specs=[q_spec, g_spec, kp_spec, kc_spec, vp_spec, vc_spec, bias_spec, sink_spec, blk, blk, blk,
                  pl.BlockSpec((keys, 128), lambda h, j: (0, 0))],
        out_specs=[pl.BlockSpec((2, qblk, 512), lambda h, j: (0, j, h)),
                   pl.BlockSpec((2, t, 128), lambda h, j: (0, 0, 0)),
                   pl.BlockSpec((B_GROUP, 128), lambda h, j: (h, 0)),
                   pl.BlockSpec((B_GROUP, 128), lambda h, j: (h, 0))],
        out_shape=[SDS((2, t, D_MODEL), BF16), SDS((2, t, 128), F32),
                   SDS((N_HEADS, 128), F32), SDS((N_HEADS, 128), F32)],
        scratch_shapes=[pltpu.VMEM((B_GROUP, qblk, keys), F32), pltpu.VMEM((B_GROUP, qblk, keys), F32)],
        compiler_params=_cparams(),
    )(qg, qg, kv, kv, kv, kv, bias, sinks, out_b, lse, dz, bucket_onehot)


def _a_bias_by_offset(rel_bias):
    m = np.arange(A_DIAG)
    idx = np.clip(A_BAND - 1 - m, -A_REL_CLIP, A_REL_CLIP) + A_REL_CLIP
    by_head = rel_bias[idx].T.reshape(N_HEADS // 2, 2, A_DIAG)
    return jnp.concatenate([by_head, jnp.zeros((N_HEADS // 2, 6, A_DIAG), F32)], axis=1)


def _a_bias_grad(offset_sums):
    first = 319
    tail = jnp.sum(offset_sums[:, :first], axis=1)
    body = jnp.flip(offset_sums[:, first:first + 320], axis=1)
    body = body.at[:, -1].add(tail)
    full = jnp.concatenate([jnp.zeros((N_HEADS, 193), F32), body], axis=1)
    return full


def _t5_bucket(rel):
    nb = T5_BUCKETS // 2
    max_exact = nb // 2
    ret = jnp.where(rel > 0, nb, 0)
    n = jnp.abs(rel)
    nf = jnp.maximum(n, 1).astype(jnp.float32)
    large = max_exact + (jnp.log(nf / max_exact) / math.log(T5_MAX_DIST / max_exact)
                         * (nb - max_exact)).astype(jnp.int32)
    large = jnp.minimum(large, nb - 1)
    return ret + jnp.where(n < max_exact, n, large)


def _b_offset_buckets(keys):
    return _t5_bucket(jnp.arange(keys, dtype=jnp.int32) - (B_LEFT_CHUNKS * CHUNK + CHUNK - 1))


def _b_bias_by_offset(t5_table, keys):
    return t5_table[_b_offset_buckets(keys)].T


def _b_bucket_onehot(keys):
    return (_b_offset_buckets(keys)[:, None] == jnp.arange(128)[None, :]).astype(BF16)


def _local_step(my_slot, order, x, target, a_gain_shard, w_in_a_shard, rel_bias, late_shards, kv_gain,
                t5_table, b_gain, sinks, f_gain):
    a_bias = _a_bias_by_offset(rel_bias)
    b_bias_fwd = _b_bias_by_offset(t5_table, B_QBLK_FWD + B_PREV)
    b_bias_bwd = _b_bias_by_offset(t5_table, B_QBLK_BWD + B_PREV)
    sinks_flat = sinks.reshape(N_HEADS)

    xn, qkvg, w_in_a, a_gain = _norm_matmul_gather(order, x, a_gain_shard, w_in_a_shard)
    z_a, out_a, lse_a, (w_in_b, w_out_a, w_out_b, kv_w) = _attn_a_fwd(qkvg, a_bias, late_shards)
    w_out_a = w_out_a.reshape(D_MODEL, D_MODEL)
    w_out_b = w_out_b.reshape(D_MODEL, D_MODEL)
    kv_w = kv_w.reshape(D_MODEL, 2 * 128)
    h1, kvn, hb, kv, qg = _layer_a_out(x, z_a, w_out_a, kv_gain, b_gain, kv_w, w_in_b)
    z_b, out_b, lse_b = _attn_b_fwd(qg, kv, b_bias_fwd, sinks_flat)
    dh2, dh2b, dz_b, loss, d_fn = _layer_b_out_loss(h1, z_b, w_out_b, f_gain, target)

    dqg_b, dkv_b, d_t5, d_sink = _attn_b_bwd(qg, kv, b_bias_bwd, sinks_flat, out_b, lse_b, dz_b,
                                             _b_bucket_onehot(B_QBLK_BWD + B_PREV))
    dh1, dh1b, dz_a, d_bn, d_kn = _layer_b_in_bwd(dqg_b, dkv_b, w_in_b, kv_w, h1, dh2, b_gain, kv_gain, w_out_a)
    early = dict(
        b_w_out=_weight_grad_rows("grad_b_w_out", my_slot, z_b, dh2b[None]),
        b_w_in=_weight_grad_cols("grad_b_w_in", my_slot, hb, [dqg_b],
                                 [(0, o, c, 4 * o + c) for o in range(2) for c in range(4)], 256),
        kv_w=_weight_grad_rows("grad_kv_w", my_slot, kvn, dkv_b),
        a_w_out=_weight_grad_rows("grad_a_w_out", my_slot, z_a, dh1b[None]))
    dqg_a, dkv_a, d_rel, landed = _attn_a_bwd(qkvg, a_bias, out_a, lse_a, dz_a, [g[0] for g in early.values()])
    g_w_in_a = _weight_grad_cols(
        "grad_a_w_in", my_slot, xn, [dqg_a, dkv_a],
        [(0, 0, 0, 0), (0, 0, 1, 1), (1, 0, 0, 2), (1, 0, 1, 3), (1, 1, 0, 4), (1, 1, 1, 5), (0, 1, 0, 6), (0, 1, 1, 7)], 512)
    chip_sums, from_sibling = _chip_sums(g_w_in_a[0])
    grad_x, d_an, from_chips = _layer_a_in_bwd(dqg_a, dkv_a, w_in_a, x, dh1, a_gain, chip_sums)

    matrices = {n: (g[1], [(land, 0, N_DEV - 1)]) for (n, g), land in zip(early.items(), landed)}
    matrices["a_w_in"] = (g_w_in_a[1], [(from_sibling, 0, 1), (from_chips, 0, 3)])
    small = dict(
        loss=loss, a_norm=d_an, a_rel_bias=d_rel[:, :2].reshape(N_HEADS, A_DIAG),
        kv_norm=d_kn, t5_bias=d_t5, b_norm=d_bn, b_sinks=d_sink, final_norm=d_fn)
    return grad_x, small, matrices


def _place():
    x, y, c = lax.axis_index("x"), lax.axis_index("y"), lax.axis_index("c")
    chips = [(1 - x, y), (x, 1 - y), (1 - x, 1 - y)]
    return x, y, c, chips


def _slot(px, py, pc):
    return 4 * px + 2 * py + pc


ANY = pl.BlockSpec(memory_space=pl.ANY)


def _peer(x, y, c, k):
    return (x ^ (k >> 2), y ^ ((k >> 1) & 1), c ^ (k & 1))


def _scatter_copies(grad_refs, land_refs, send_sems, recv_sems):
    x, y, c, _ = _place()
    copies = []
    for t, (grad, land) in enumerate(zip(grad_refs, land_refs)):
        for k in range(1, N_DEV):
            peer = _peer(x, y, c, k)
            sem = (N_DEV - 1) * t + k - 1
            copies.append(pltpu.make_async_remote_copy(
                src_ref=grad.at[_slot(*peer)], dst_ref=land.at[k - 1],
                send_sem=send_sems.at[sem], recv_sem=recv_sems.at[sem],
                device_id=peer, device_id_type=MESH))
    return copies


def _gather_phases(ins, outs, send_sems, recv_sems, local_sems):
    n = len(ins)
    x, y, c, chips = _place()
    me, sibling = (x, y, c), (x, y, 1 - c)

    def copy(t, k, block, to, src=None):
        dst = outs[t].at[_slot(*block)]
        return pltpu.make_async_remote_copy(
            src_ref=dst if src is None else src, dst_ref=dst,
            send_sem=send_sems.at[7 * t + k], recv_sem=recv_sems.at[7 * t + k],
            device_id=to, device_id_type=MESH)

    def lists():
        mine = [pltpu.make_async_copy(ins[t], outs[t].at[_slot(*me)], local_sems.at[t]) for t in range(n)]
        first = []
        for t in range(n):
            first.append(copy(t, 0, me, sibling, src=ins[t]))
            first += [copy(t, 1 + j, me, (*chip, c), src=ins[t]) for j, chip in enumerate(chips)]
        passed = [copy(t, 4 + j, (*chip, c), sibling) for t in range(n) for j, chip in enumerate(chips)]
        return mine, first, passed

    def start():
        mine, first, _ = lists()
        for cp in mine + first:
            cp.start()

    def forward():
        _, _, passed = lists()
        for t in range(n):
            for j, chip in enumerate(chips):
                copy(t, 1 + j, (*chip, c), me).wait_recv()
                passed[3 * t + j].start()

    def finish():
        mine, first, passed = lists()
        for t in range(n):
            copy(t, 0, sibling, me).wait_recv()
            for j, chip in enumerate(chips):
                copy(t, 4 + j, (*chip, 1 - c), me).wait_recv()
        for cp in first + passed:
            cp.wait_send()
        for cp in mine:
            cp.wait()

    return start, forward, finish


def _gather_scratch(n):
    return [pltpu.SemaphoreType.DMA((7 * n,)), pltpu.SemaphoreType.DMA((7 * n,)), pltpu.SemaphoreType.DMA((n,))]


def _chip_sums(g):
    _, r, c = g.shape

    def body(g_ref, sums_ref, mine_ref, land, own, send_sems, recv_sems, load_sems):
        x, y, c_i, chips = _place()
        sibling = (x, y, 1 - c_i)
        blocks = [(*chip, 1 - c_i) for chip in chips] + [sibling]
        sends = [pltpu.make_async_remote_copy(
            src_ref=g_ref.at[_slot(*block)], dst_ref=land.at[k], send_sem=send_sems.at[k],
            recv_sem=recv_sems.at[k], device_id=sibling, device_id_type=MESH) for k, block in enumerate(blocks)]
        loads = [pltpu.make_async_copy(g_ref.at[_slot(*chip, c_i)], own.at[j], load_sems.at[j])
                 for j, chip in enumerate(chips)]
        for cp in sends + loads:
            cp.start()
        for cp in sends + loads:
            cp.wait()
        for j in range(3):
            sums_ref[j] = (own[j].astype(F32) + land[j].astype(F32)).astype(BF16)
        mine_ref[0] = land[3]

    return pl.pallas_call(
        body, name="chip_sums",
        in_specs=[ANY], out_specs=[VM, VM],
        out_shape=[SDS((3, r, c), BF16), SDS((1, r, c), BF16)],
        scratch_shapes=[pltpu.VMEM((4, r, c), BF16), pltpu.VMEM((3, r, c), BF16),
                        pltpu.SemaphoreType.DMA((4,)), pltpu.SemaphoreType.DMA((4,)), pltpu.SemaphoreType.DMA((3,))],
        compiler_params=_cparams(),
    )(g)


def _chip_copies(sums_ref, land_ref, send_sems, recv_sems):
    x, y, c, chips = _place()
    del x, y
    return [pltpu.make_async_remote_copy(
        src_ref=sums_ref.at[j], dst_ref=land_ref.at[j], send_sem=send_sems.at[j], recv_sem=recv_sems.at[j],
        device_id=(*chip, c), device_id_type=MESH) for j, chip in enumerate(chips)]


def _row_tile(rows):
    return min(rows, 256)


def _adamw(w, g, m, v):
    m2 = ADAM_B1 * m + (1.0 - ADAM_B1) * g
    v2 = ADAM_B2 * v + (1.0 - ADAM_B2) * jnp.square(g)
    m_hat = m2 / (1.0 - ADAM_B1 ** ADAM_STEP)
    v_hat = v2 / (1.0 - ADAM_B2 ** ADAM_STEP)
    delta = -ADAM_LR * (m_hat / (jnp.sqrt(v_hat) + ADAM_EPS) + ADAM_WD * w)
    return delta, m2, v2


def _reduce_adamw(name, own, partials, w, m, v):
    r, c = own.shape
    tr = _row_tile(r)
    n_p = len(partials)

    def body(own_ref, *rest):
        p_refs, (w_ref, m_ref, v_ref, grad_ref, d_ref, nm_ref, nv_ref) = rest[:n_p], rest[n_p:]
        grad = own_ref[...]
        for p_ref, (_, _, count) in zip(p_refs, partials):
            for j in range(count):
                grad = grad + p_ref[j].astype(F32)
        grad_ref[...] = grad
        d_ref[...], nm_ref[...], nv_ref[...] = _adamw(w_ref[...], grad, m_ref[...], v_ref[...])

    flat = pl.BlockSpec((tr, c), lambda i: (i, 0))
    return pl.pallas_call(
        body, name=name, grid=(r // tr,),
        in_specs=[flat] + [pl.BlockSpec((count, tr, c), lambda i, first=first, count=count: (first // count, i, 0))
                           for _, first, count in partials] + [flat, flat, flat],
        out_specs=[flat, flat, flat, flat],
        out_shape=[SDS((r, c), F32)] * 4,
        compiler_params=_cparams(),
    )(own, *[p[0] for p in partials], w, m, v)


VM = pl.BlockSpec()


def _small_allreduce(parts):
    n = len(parts)

    def body(*refs):
        ins, outs, lands = refs[:n], refs[n:2 * n], refs[2 * n:3 * n]
        send_sems, recv_sems = refs[3 * n:]
        x, y, c, _ = _place()
        my_slot = _slot(x, y, c)
        copies = []
        for t in range(n):
            lands[t][my_slot] = ins[t][...]
            for k in range(1, N_DEV):
                sem = (N_DEV - 1) * t + k - 1
                copies.append(pltpu.make_async_remote_copy(
                    src_ref=ins[t], dst_ref=lands[t].at[my_slot],
                    send_sem=send_sems.at[sem], recv_sem=recv_sems.at[sem],
                    device_id=_peer(x, y, c, k), device_id_type=MESH))
        for cp in copies:
            cp.start()
        for t in range(n):
            for k in range(1, N_DEV):
                sem = (N_DEV - 1) * t + k - 1
                pltpu.make_async_remote_copy(
                    src_ref=ins[t], dst_ref=lands[t].at[_slot(*_peer(x, y, c, k))],
                    send_sem=send_sems.at[sem], recv_sem=recv_sems.at[sem],
                    device_id=(x, y, c), device_id_type=MESH).wait_recv()
        for cp in copies:
            cp.wait_send()
        for t in range(n):
            total = lands[t][0]
            for s in range(1, N_DEV):
                total = total + lands[t][s]
            outs[t][...] = total

    n_sems = (N_DEV - 1) * n
    return pl.pallas_call(
        body, name="small_allreduce",
        in_specs=[VM] * n, out_specs=[VM] * n, out_shape=[SDS(p.shape, F32) for p in parts],
        scratch_shapes=[pltpu.VMEM((N_DEV, *p.shape), F32) for p in parts]
        + [pltpu.SemaphoreType.DMA((n_sems,)), pltpu.SemaphoreType.DMA((n_sems,))],
    )(*parts)


def _small_adamw(my_slot, sums, ws, ms, vs):
    n = len(ws)

    def body(slot_ref, *refs):
        sum_refs, refs = refs[:n + 1], refs[n + 1:]
        w_refs, m_refs, v_refs, refs = refs[:n], refs[n:2 * n], refs[2 * n:3 * n], refs[3 * n:]
        g_refs, d_refs, nm_refs, nv_refs = refs[:n + 1], refs[n + 1:2 * n + 1], refs[2 * n + 1:3 * n + 1], refs[3 * n + 1:]
        for t in range(n + 1):
            if t == 0:
                g = sum_refs[0][:, pl.ds(pl.multiple_of(slot_ref[0] * 128, 128), 128)]
            else:
                g = sum_refs[t][...]
            g_refs[t][...] = g
            if t < n:
                d_refs[t][...], nm_refs[t][...], nv_refs[t][...] = _adamw(w_refs[t][...], g, m_refs[t][...], v_refs[t][...])

    shapes = [SDS(w.shape, F32) for w in ws]
    outs = pl.pallas_call(
        body, name="small_adamw",
        in_specs=[pl.BlockSpec(memory_space=pltpu.SMEM)] + [VM] * (4 * n + 1),
        out_specs=[VM] * (4 * n + 1),
        out_shape=shapes + [SDS(sums[-1].shape, F32)] + shapes * 3,
    )(my_slot, *sums, *ws, *ms, *vs)
    return outs[:n + 1], outs[n + 1:2 * n + 1], outs[2 * n + 1:3 * n + 1], outs[3 * n + 1:]


def kernel(x, a_norm, a_w_in, a_rel_bias, a_w_out, kv_norm, kv_w, t5_bias, b_norm, b_w_in, b_sinks, b_w_out, final_norm, loss_target, m_a_norm, m_a_w_in, m_a_rel_bias, m_a_w_out, m_kv_norm, m_kv_w, m_t5_bias, m_b_norm, m_b_w_in, m_b_sinks, m_b_w_out, m_final_norm, v_a_norm, v_a_w_in, v_a_rel_bias, v_a_w_out, v_kv_norm, v_kv_w, v_t5_bias, v_b_norm, v_b_w_in, v_b_sinks, v_b_w_out, v_final_norm):
    xi, yi, ci = lax.axis_index("x"), lax.axis_index("y"), lax.axis_index("c")
    my_slot = _slot(xi, yi, ci)

    slot_arr = jnp.reshape(my_slot, (1,)).astype(jnp.int32)
    order = _gather_order(xi, yi, ci)
    late_shards = [b_w_in[0].astype(BF16), a_w_out[0].astype(BF16), b_w_out[0].astype(BF16), kv_w.astype(BF16)]
    grad_x, loc, matrices = _local_step(
        slot_arr, order, x[0], loss_target[0], a_norm, a_w_in[0].astype(BF16), a_rel_bias[0], late_shards,
        kv_norm.reshape(1, D_MODEL), t5_bias, b_norm, b_sinks, final_norm.reshape(1, D_MODEL))

    shard_w = dict(a_w_in=a_w_in[0], b_w_in=b_w_in[0], a_w_out=a_w_out[0], b_w_out=b_w_out[0], kv_w=kv_w)
    shard_m = dict(a_w_in=m_a_w_in[0], b_w_in=m_b_w_in[0], a_w_out=m_a_w_out[0], b_w_out=m_b_w_out[0], kv_w=m_kv_w)
    shard_v = dict(a_w_in=v_a_w_in[0], b_w_in=v_b_w_in[0], a_w_out=v_a_w_out[0], b_w_out=v_b_w_out[0], kv_w=v_kv_w)
    big = {n: _reduce_adamw("adamw_" + n, own, partials, shard_w[n], shard_m[n], shard_v[n])
           for n, (own, partials) in matrices.items()}

    names = ("a_norm", "a_rel_bias", "kv_norm", "t5_bias", "b_norm", "b_sinks", "final_norm")
    tables = ("a_rel_bias", "t5_bias")

    def row(n, a):
        return a.reshape(-1, a.shape[-1]).T if n in tables else a.reshape(1, -1)

    small_w = [row(n, a) for n, a in zip(names, (a_norm, a_rel_bias, kv_norm, t5_bias, b_norm, b_sinks, final_norm))]
    small_m = [row(n, a) for n, a in zip(names, (m_a_norm, m_a_rel_bias, m_kv_norm, m_t5_bias, m_b_norm, m_b_sinks,
                                                 m_final_norm))]
    small_v = [row(n, a) for n, a in zip(names, (v_a_norm, v_a_rel_bias, v_kv_norm, v_t5_bias, v_b_norm, v_b_sinks,
                                                 v_final_norm))]
    sums = dict(zip(names + ("loss",), _small_allreduce([loc[n] for n in names] + [loc["loss"]])))
    sums["a_rel_bias"] = _a_bias_grad(sums["a_rel_bias"])
    sums["t5_bias"] = sums["t5_bias"][:, :T5_BUCKETS]
    sums["b_sinks"] = sums["b_sinks"][:, 0].reshape(1, N_HEADS)
    results = _small_adamw(slot_arr, [sums[n] for n in names + ("loss",)], small_w, small_m, small_v)
    like = dict(a_norm=a_norm, a_rel_bias=a_rel_bias, kv_norm=kv_norm, t5_bias=t5_bias, b_norm=b_norm,
                b_sinks=b_sinks, final_norm=final_norm)
    sm = [{n: (part[i].T if n in tables else part[i]).reshape(like[n].shape) for i, n in enumerate(names)}
          for part in results]
    loss = results[0][len(names)][0, 0]

    order = ("a_norm", "a_w_in", "a_rel_bias", "a_w_out", "kv_norm", "kv_w", "t5_bias", "b_norm",
             "b_w_in", "b_sinks", "b_w_out", "final_norm")
    lead = dict(a_w_in=True, b_w_in=True, a_w_out=True, b_w_out=True, kv_w=False)

    def pick(kind, name):
        if name in big:
            val = big[name][kind]
            return val[None] if lead[name] else val
        return sm[kind][name]

    outs = [loss, grad_x[None]]
    for kind in range(4):
        outs += [pick(kind, n) for n in order]
    return tuple(outs)
```

```python
import functools
import math

import numpy as np
import jax
import jax.numpy as jnp
from jax import lax
from jax.experimental import pallas as pl
from jax.experimental.pallas import tpu as pltpu

F32 = jnp.float32
BF16 = jnp.bfloat16
SDS = jax.ShapeDtypeStruct

D_MODEL = 1024
HEAD_DIM = 64
CHUNK = 64
N_HEADS = 16
RMS_EPS = 1e-6
A_LEFT_CHUNKS = 8
A_BAND = (A_LEFT_CHUNKS + 1) * CHUNK
A_REL_CLIP = 256
B_KV_HEADS = 2
B_GROUP = 8
B_LEFT_CHUNKS = 2
B_BAND = (B_LEFT_CHUNKS + 1) * CHUNK
T5_BUCKETS = 32
T5_MAX_DIST = 128
QBLK = 256
A_KEYS = 3 * QBLK
B_QBLK_FWD = 128
B_QBLK_BWD = 256
B_PREV = 128
A_DIAG = A_KEYS
NEG = -1e30
SCALE = HEAD_DIM ** -0.5
N_DEV = 8

ADAM_LR = 0.001
ADAM_B1 = 0.9
ADAM_B2 = 0.999
ADAM_EPS = 1e-08
ADAM_WD = 0.01
ADAM_STEP = 10

VMEM_LIMIT_BYTES = 56 * 1024 * 1024
MESH = pl.DeviceIdType.MESH


def _cparams():
    return pltpu.CompilerParams(vmem_limit_bytes=VMEM_LIMIT_BYTES)


def _dot(a, b):
    return jnp.dot(a, b, preferred_element_type=F32)


def _dot_nt(a, b):
    return lax.dot_general(a, b, (((1,), (1,)), ((), ())), preferred_element_type=F32)


def _dot_tn(a, b):
    return lax.dot_general(a, b, (((0,), (0,)), ((), ())), preferred_element_type=F32)


def _rstd(xf):
    return lax.rsqrt(jnp.mean(xf * xf, axis=-1, keepdims=True) + RMS_EPS)


def _sigmoid(x):
    return 1.0 / (1.0 + jnp.exp(-x))


_GATHER_SEQUENCE = ((0, None), (1, 0), (2, 1), (4, None), (5, None), (3, 2), (6, None))


def _gather_order(x, y, c):
    others = [(1 - x, y), (x, 1 - y), (1 - x, 1 - y)]
    arrivals = [_slot(x, y, 1 - c)] + [_slot(*chip, c) for chip in others] + [_slot(*chip, 1 - c) for chip in others]
    return jnp.stack([_slot(x, y, c)] + [arrivals[a] for a, _ in _GATHER_SEQUENCE]).astype(jnp.int32)


def _norm_matmul_gather(order, x, gain_shard, w_shard):
    t = x.shape[0]
    dw, tn = w_shard.shape
    tm = min(t, 1024)
    n_m = t // tm

    def body(order_ref, x_ref, gs_ref, shard_ref, xn_ref, o_ref, full_ref, gain_ref,
             xn_all, wbuf, gland, send_sems, recv_sems, gsend_sems, grecv_sems, load_sems, own_sem):
        n, m = pl.program_id(0), pl.program_id(1)
        x_i, y_i, c_i, chips = _place()
        me, sibling = (x_i, y_i, c_i), (x_i, y_i, 1 - c_i)

        def send(k, block, to, src=None):
            dst = full_ref.at[_slot(*block)]
            return pltpu.make_async_remote_copy(
                src_ref=dst if src is None else src, dst_ref=dst,
                send_sem=send_sems.at[k], recv_sem=recv_sems.at[k], device_id=to, device_id_type=MESH)

        own = pltpu.make_async_copy(shard_ref, full_ref.at[_slot(*me)], own_sem)
        first = [send(0, me, sibling, src=shard_ref)]
        first += [send(1 + j, me, (*chip, c_i), src=shard_ref) for j, chip in enumerate(chips)]
        forwards = [send(4 + j, (*chip, c_i), sibling) for j, chip in enumerate(chips)]
        arrivals = [send(0, sibling, me)] + [send(1 + j, (*chip, c_i), me) for j, chip in enumerate(chips)]
        arrivals += [send(4 + j, (*chip, 1 - c_i), me) for j, chip in enumerate(chips)]
        gains = [pltpu.make_async_remote_copy(
            src_ref=gs_ref, dst_ref=gland.at[_slot(*me)], send_sem=gsend_sems.at[k - 1],
            recv_sem=grecv_sems.at[k - 1], device_id=_peer(x_i, y_i, c_i, k), device_id_type=MESH)
            for k in range(1, N_DEV)]

        @pl.when(jnp.logical_and(n == 0, m == 0))
        def _():
            own.start()
            for cp in gains + first:
                cp.start()
            pltpu.make_async_copy(shard_ref, wbuf.at[0], load_sems.at[0]).start()
            gland[_slot(*me)] = gs_ref[...]
            for k in range(1, N_DEV):
                pltpu.make_async_remote_copy(
                    src_ref=gs_ref, dst_ref=gland.at[_slot(*_peer(x_i, y_i, c_i, k))],
                    send_sem=gsend_sems.at[k - 1], recv_sem=grecv_sems.at[k - 1],
                    device_id=me, device_id_type=MESH).wait_recv()
            for s in range(N_DEV):
                gain_ref[:, 128 * s:128 * (s + 1)] = gland[s]

        rows = pl.ds(pl.multiple_of(m * tm, tm), tm)

        @pl.when(n == 0)
        def _():
            xf = x_ref[...]
            xn = ((xf * _rstd(xf)) * gain_ref[...]).astype(BF16)
            xn_all[rows, :] = xn
            xn_ref[...] = xn

        @pl.when(m == 0)
        def _():
            pltpu.make_async_copy(full_ref.at[0], wbuf.at[n % 2], load_sems.at[n % 2]).wait()

        o_ref[...] = _dot(xn_all[rows, :], wbuf[n % 2]).astype(BF16)

        for k, (arrival, forward) in enumerate(_GATHER_SEQUENCE):
            @pl.when(jnp.logical_and(n == k, m == n_m - 1))
            def _(k=k, arrival=arrival, forward=forward):
                arrivals[arrival].wait_recv()
                if forward is not None:
                    forwards[forward].start()
                pltpu.make_async_copy(full_ref.at[order_ref[k + 1]], wbuf.at[(k + 1) % 2],
                                      load_sems.at[(k + 1) % 2]).start()

        @pl.when(jnp.logical_and(n == N_DEV - 1, m == n_m - 1))
        def _():
            for cp in gains + first + forwards:
                cp.wait_send()
            own.wait()

    held = lambda n, m, order: (jnp.where(n == 0, m, n_m - 1), 0)
    return pl.pallas_call(
        body, name="norm_matmul_gather",
        grid_spec=pltpu.PrefetchScalarGridSpec(
            num_scalar_prefetch=1, grid=(N_DEV, n_m),
            in_specs=[pl.BlockSpec((tm, D_MODEL), held),
                      pl.BlockSpec((1, 128), lambda n, m, order: (0, 0)), ANY],
            out_specs=[pl.BlockSpec((tm, D_MODEL), held),
                       pl.BlockSpec((tm, tn), lambda n, m, order: (m, order[n])),
                       ANY, pl.BlockSpec((1, D_MODEL), lambda n, m, order: (0, 0))],
            scratch_shapes=[pltpu.VMEM((t, D_MODEL), BF16), pltpu.VMEM((2, dw, tn), BF16),
                            pltpu.VMEM((N_DEV, 1, 128), F32),
                            pltpu.SemaphoreType.DMA((7,)), pltpu.SemaphoreType.DMA((7,)),
                            pltpu.SemaphoreType.DMA((7,)), pltpu.SemaphoreType.DMA((7,)),
                            pltpu.SemaphoreType.DMA((2,)), pltpu.SemaphoreType.DMA]),
        out_shape=[SDS((t, D_MODEL), BF16), SDS((t, N_DEV * tn), BF16), SDS((N_DEV, dw, tn), BF16),
                   SDS((1, D_MODEL), F32)],
        compiler_params=_cparams(),
    )(order, x, gain_shard, w_shard)


def _layer_a_out(x, z, w_out, kv_gain, b_gain, kv_w, w_in_b):
    t = x.shape[0]
    tm = min(t, 512)
    nb, _, tn = w_in_b.shape

    def body(x_ref, z_ref, wo_ref, kvg_ref, bg_ref, kvw_ref, wb_ref,
             h1_ref, kvn_ref, hb_ref, kv_ref, qg_ref):
        h1 = x_ref[...] + _dot(z_ref[...], wo_ref[...])
        h1_ref[...] = h1
        y0 = h1 * _rstd(h1)
        kvn = (y0 * kvg_ref[...]).astype(BF16)
        hb = (y0 * bg_ref[...]).astype(BF16)
        kvn_ref[...] = kvn
        hb_ref[...] = hb
        kv_ref[...] = _dot(kvn, kvw_ref[...]).astype(BF16)
        for i in range(nb):
            qg_ref[:, i * tn:(i + 1) * tn] = _dot(hb, wb_ref[i]).astype(BF16)

    row = lambda m: (m, 0)
    fix2 = lambda m: (0, 0)
    return pl.pallas_call(
        body, name="layer_a_out", grid=(t // tm,),
        in_specs=[pl.BlockSpec((tm, D_MODEL), row), pl.BlockSpec((tm, D_MODEL), row),
                  pl.BlockSpec((D_MODEL, D_MODEL), fix2),
                  pl.BlockSpec((1, D_MODEL), fix2), pl.BlockSpec((1, D_MODEL), fix2),
                  pl.BlockSpec((D_MODEL, 256), fix2),
                  pl.BlockSpec((nb, D_MODEL, tn), lambda m: (0, 0, 0))],
        out_specs=[pl.BlockSpec((tm, D_MODEL), row), pl.BlockSpec((tm, D_MODEL), row),
                   pl.BlockSpec((tm, D_MODEL), row), pl.BlockSpec((tm, 256), row),
                   pl.BlockSpec((tm, nb * tn), row)],
        out_shape=[SDS((t, D_MODEL), F32), SDS((t, D_MODEL), BF16), SDS((t, D_MODEL), BF16),
                   SDS((t, 256), BF16), SDS((t, nb * tn), BF16)],
        compiler_params=_cparams(),
    )(x, z, w_out, kv_gain, b_gain, kv_w, w_in_b)


def _layer_b_out_loss(h1, z, w_out, f_gain, target):
    t = h1.shape[0]
    tm = min(t, 512)

    def body(h1_ref, z_ref, wo_ref, fg_ref, tgt_ref,
             dh2_ref, dh2b_ref, dz_ref, loss_ref, dfn_ref):
        @pl.when(pl.program_id(0) == 0)
        def _():
            loss_ref[...] = jnp.zeros_like(loss_ref)
            dfn_ref[...] = jnp.zeros_like(dfn_ref)

        h2 = h1_ref[...] + _dot(z_ref[...], wo_ref[...])
        r = _rstd(h2)
        yn = h2 * r
        fg = fg_ref[...]
        err = yn * fg - tgt_ref[...]
        loss_ref[...] += (0.5 / D_MODEL) * jnp.sum(err * err)
        dy = err * (1.0 / D_MODEL)
        dfn_ref[...] += jnp.sum(dy * yn, axis=0, keepdims=True)
        u = dy * fg
        dh2 = r * u - h2 * ((r * r * r) * jnp.mean(u * h2, axis=-1, keepdims=True))
        dh2_ref[...] = dh2
        dh2b = dh2.astype(BF16)
        dh2b_ref[...] = dh2b
        dz_ref[...] = _dot_nt(dh2b, wo_ref[...]).astype(BF16)

    row = lambda m: (m, 0)
    fix2 = lambda m: (0, 0)
    return pl.pallas_call(
        body, name="layer_b_out_loss", grid=(t // tm,),
        in_specs=[pl.BlockSpec((tm, D_MODEL), row), pl.BlockSpec((tm, D_MODEL), row),
                  pl.BlockSpec((D_MODEL, D_MODEL), fix2), pl.BlockSpec((1, D_MODEL), fix2),
                  pl.BlockSpec((tm, D_MODEL), row)],
        out_specs=[pl.BlockSpec((tm, D_MODEL), row), pl.BlockSpec((tm, D_MODEL), row),
                   pl.BlockSpec((tm, D_MODEL), row), pl.BlockSpec((1, 128), fix2),
                   pl.BlockSpec((1, D_MODEL), fix2)],
        out_shape=[SDS((t, D_MODEL), F32), SDS((t, D_MODEL), BF16), SDS((t, D_MODEL), BF16),
                   SDS((1, 128), F32), SDS((1, D_MODEL), F32)],
        compiler_params=_cparams(),
    )(h1, z, w_out, f_gain, target)


def _layer_b_in_bwd(dqg, dkv, w_in_b, kv_w, h1, dh2, b_gain, kv_gain, w_out_a):
    t = h1.shape[0]
    tm = min(t, 512)
    nb, _, tn = w_in_b.shape
    per = D_MODEL // tn

    def body(dqg_ref, dkv_ref, wb_ref, kvw_ref, h1_ref, dh2_ref, bg_ref, kvg_ref, wo_ref,
             dh1_ref, dh1b_ref, dz_ref, dbn_ref, dkn_ref):
        @pl.when(pl.program_id(0) == 0)
        def _():
            dbn_ref[...] = jnp.zeros_like(dbn_ref)
            dkn_ref[...] = jnp.zeros_like(dkn_ref)

        dhb = jnp.zeros((tm, D_MODEL), F32)
        for i in range(nb):
            blk = dqg_ref[i // per, :, (i % per) * tn:(i % per + 1) * tn]
            dhb = dhb + _dot_nt(blk, wb_ref[i])
        dkn = (_dot_nt(dkv_ref[0].astype(BF16), kvw_ref[:, 0:128])
               + _dot_nt(dkv_ref[1].astype(BF16), kvw_ref[:, 128:256]))
        h1 = h1_ref[...]
        r = _rstd(h1)
        xr = h1 * r
        dbn_ref[...] += jnp.sum(dhb * xr, axis=0, keepdims=True)
        dkn_ref[...] += jnp.sum(dkn * xr, axis=0, keepdims=True)
        u = dhb * bg_ref[...] + dkn * kvg_ref[...]
        dh1 = dh2_ref[...] + r * u - h1 * ((r * r * r) * jnp.mean(u * h1, axis=-1, keepdims=True))
        dh1_ref[...] = dh1
        dh1b = dh1.astype(BF16)
        dh1b_ref[...] = dh1b
        dz_ref[...] = _dot_nt(dh1b, wo_ref[...]).astype(BF16)

    row = lambda m: (m, 0)
    fix2 = lambda m: (0, 0)
    return pl.pallas_call(
        body, name="layer_b_in_bwd", grid=(t // tm,),
        in_specs=[pl.BlockSpec((2, tm, D_MODEL), lambda m: (0, m, 0)),
                  pl.BlockSpec((2, tm, 128), lambda m: (0, m, 0)),
                  pl.BlockSpec((nb, D_MODEL, tn), lambda m: (0, 0, 0)),
                  pl.BlockSpec((D_MODEL, 256), fix2),
                  pl.BlockSpec((tm, D_MODEL), row), pl.BlockSpec((tm, D_MODEL), row),
                  pl.BlockSpec((1, D_MODEL), fix2), pl.BlockSpec((1, D_MODEL), fix2),
                  pl.BlockSpec((D_MODEL, D_MODEL), fix2)],
        out_specs=[pl.BlockSpec((tm, D_MODEL), row), pl.BlockSpec((tm, D_MODEL), row),
                   pl.BlockSpec((tm, D_MODEL), row), pl.BlockSpec((1, D_MODEL), fix2),
                   pl.BlockSpec((1, D_MODEL), fix2)],
        out_shape=[SDS((t, D_MODEL), F32), SDS((t, D_MODEL), BF16), SDS((t, D_MODEL), BF16),
                   SDS((1, D_MODEL), F32), SDS((1, D_MODEL), F32)],
        compiler_params=_cparams(),
    )(dqg, dkv, w_in_b, kv_w, h1, dh2, b_gain, kv_gain, w_out_a)


def _layer_a_in_bwd(dqg, dkv, w_in_a, x, dh1, a_gain, chip_sums):
    t = x.shape[0]
    tm = min(t, 512)
    nb, _, tn = w_in_a.shape
    per = D_MODEL // tn

    def body(dqg_ref, dkv_ref, w_ref, x_ref, dh1_ref, ag_ref, sums_ref, dx_ref, dan_ref, land_ref,
             send_sems, recv_sems):
        @pl.when(pl.program_id(0) == 0)
        def _():
            dan_ref[...] = jnp.zeros_like(dan_ref)
            for cp in _chip_copies(sums_ref, land_ref, send_sems, recv_sems):
                cp.start()

        dxn = jnp.zeros((tm, D_MODEL), F32)
        for i in range(nb):
            part = i // per
            src = dqg_ref if part in (0, 3) else dkv_ref
            outer = {0: 0, 3: 1, 1: 0, 2: 1}[part]
            blk = src[outer, :, (i % per) * tn:(i % per + 1) * tn]
            dxn = dxn + _dot_nt(blk, w_ref[i])
        xf = x_ref[...]
        r = _rstd(xf)
        dan_ref[...] += jnp.sum(dxn * (xf * r), axis=0, keepdims=True)
        u = dxn * ag_ref[...]
        dx_ref[...] = dh1_ref[...] + r * u - xf * ((r * r * r) * jnp.mean(u * xf, axis=-1, keepdims=True))

        @pl.when(pl.program_id(0) == t // tm - 1)
        def _():
            for cp in _chip_copies(sums_ref, land_ref, send_sems, recv_sems):
                cp.wait()

    row = lambda m: (m, 0)
    fix2 = lambda m: (0, 0)
    return pl.pallas_call(
        body, name="layer_a_in_bwd", grid=(t // tm,),
        in_specs=[pl.BlockSpec((2, tm, D_MODEL), lambda m: (0, m, 0)),
                  pl.BlockSpec((2, tm, D_MODEL), lambda m: (0, m, 0)),
                  pl.BlockSpec((nb, D_MODEL, tn), lambda m: (0, 0, 0)),
                  pl.BlockSpec((tm, D_MODEL), row), pl.BlockSpec((tm, D_MODEL), row),
                  pl.BlockSpec((1, D_MODEL), fix2), ANY],
        out_specs=[pl.BlockSpec((tm, D_MODEL), row), pl.BlockSpec((1, D_MODEL), fix2), ANY],
        out_shape=[SDS((t, D_MODEL), F32), SDS((1, D_MODEL), F32), SDS(chip_sums.shape, chip_sums.dtype)],
        scratch_shapes=[pltpu.SemaphoreType.DMA((3,)), pltpu.SemaphoreType.DMA((3,))],
        compiler_params=_cparams(),
    )(dqg, dkv, w_in_a, x, dh1, a_gain, chip_sums)


def _lut(s, vals):
    r = jnp.int32(vals[0])
    for i in range(1, len(vals)):
        r = jnp.where(s == i, jnp.int32(vals[i]), r)
    return r


def _held(steps, i):
    seq, cur = [None] * len(steps), None
    for k in range(len(steps) - 1, -1, -1):
        if steps[k][0] == i:
            cur = steps[k][1:3]
        seq[k] = cur
    for k in range(len(steps)):
        cur = seq[k] = seq[k] if seq[k] is not None else cur
    return seq


def _weight_grad_cols(name, my_slot, a, bs, steps, tn):
    t, dw = a.shape
    n_arr = len(bs)
    which = [s[0] for s in steps]
    blks = [s[3] for s in steps]

    def body(slot_ref, a_ref, *rest):
        b_refs, (o_ref, own_ref, at_ref) = rest[:n_arr], rest[n_arr:]
        s = pl.program_id(0)

        @pl.when(s == 0)
        def _():
            at_ref[...] = a_ref[...].T

        for i in range(n_arr):
            @pl.when(_lut(s, which) == i)
            def _(i=i):
                res = _dot(at_ref[...], b_refs[i][0])
                o_ref[0] = res.astype(BF16)

                @pl.when(_lut(s, blks) == slot_ref[0])
                def _():
                    own_ref[...] = res

    def b_spec(i):
        held = _held(steps, i)
        return pl.BlockSpec((1, t, tn), lambda s, slot: (_lut(s, [h[0] for h in held]), 0,
                                                         _lut(s, [h[1] for h in held])))

    return pl.pallas_call(
        body, name=name,
        grid_spec=pltpu.PrefetchScalarGridSpec(
            num_scalar_prefetch=1, grid=(len(steps),),
            in_specs=[pl.BlockSpec((t, dw), lambda s, slot: (0, 0))] + [b_spec(i) for i in range(n_arr)],
            out_specs=[pl.BlockSpec((1, dw, tn), lambda s, slot: (_lut(s, blks), 0, 0)),
                       pl.BlockSpec((dw, tn), lambda s, slot: (0, 0))],
            scratch_shapes=[pltpu.VMEM((dw, t), BF16)]),
        out_shape=[SDS((N_DEV, dw, tn), BF16), SDS((dw, tn), F32)],
        compiler_params=_cparams(),
    )(my_slot, a, *bs)


def _weight_grad_rows(name, my_slot, a, b):
    t, dw = a.shape
    n_o, _, c = b.shape
    rows = dw // N_DEV
    tn = min(c, 256)
    per = c // tn

    def body(slot_ref, a_ref, b_ref, o_ref, own_ref, at_ref, res_ref):
        @pl.when(pl.program_id(0) == 0)
        def _():
            at_ref[...] = a_ref[...].T

        res_ref[...] = _dot(at_ref[...], b_ref[0].astype(BF16))
        o_ref[...] = res_ref[...].astype(BF16)
        own_ref[...] = res_ref[pl.ds(pl.multiple_of(slot_ref[0] * rows, rows), rows), :]

    all_rows, own = pl.pallas_call(
        body, name=name,
        grid_spec=pltpu.PrefetchScalarGridSpec(
            num_scalar_prefetch=1, grid=(n_o * per,),
            in_specs=[pl.BlockSpec((t, dw), lambda s, slot: (0, 0)),
                      pl.BlockSpec((1, t, tn), lambda s, slot: (s // per, 0, s % per))],
            out_specs=[pl.BlockSpec((dw, tn), lambda s, slot: (0, s)),
                       pl.BlockSpec((rows, tn), lambda s, slot: (0, s))],
            scratch_shapes=[pltpu.VMEM((dw, t), BF16), pltpu.VMEM((dw, tn), F32)]),
        out_shape=[SDS((dw, n_o * c), BF16), SDS((rows, n_o * c), F32)],
        compiler_params=_cparams(),
    )(my_slot, a, b)
    return all_rows.reshape(N_DEV, rows, n_o * c), own


def _lane_lo():
    return lax.broadcasted_iota(jnp.int32, (1, 128), 1) < HEAD_DIM


def _collapse_chunks(ds, keys):
    if ds.shape[1] < keys:
        ds = jnp.concatenate([jnp.zeros((ds.shape[0], keys - ds.shape[1]), F32), ds], axis=1)
    gc = ds[0:CHUNK]
    for cc in range(1, ds.shape[0] // CHUNK):
        gc = gc + pltpu.roll(ds[cc * CHUNK:(cc + 1) * CHUNK], keys - cc * CHUNK, 1)
    return gc


def _offset_sums(gc):
    hi = gc.astype(BF16)
    lo = (gc - hi.astype(F32)).astype(BF16)
    flip = (lax.broadcasted_iota(jnp.int32, (CHUNK, CHUNK), 0)
            + lax.broadcasted_iota(jnp.int32, (CHUNK, CHUNK), 1) == CHUNK - 1).astype(BF16)
    gf = _dot(flip, hi) + _dot(flip, lo)
    skew = pltpu.roll(gf, 0, 1, stride=1, stride_axis=0)
    return jnp.sum(skew, axis=0, keepdims=True)


def _band_bias(w_row, band, rows):
    keys = w_row.shape[1]
    base = jnp.broadcast_to(w_row, (CHUNK, keys))
    skew = pltpu.roll(base, 0, 1, stride=1, stride_axis=0)
    skew = pltpu.roll(skew, keys - (CHUNK - 1), 1)
    col = lax.broadcasted_iota(jnp.int32, (CHUNK, keys), 1)
    chunk0 = jnp.where(col < band, skew, NEG)
    return jnp.concatenate(
        [chunk0] + [pltpu.roll(chunk0, cc * CHUNK, 1) for cc in range(1, rows // CHUNK)], axis=0)


def _silu_parts(g):
    sg = _sigmoid(g)
    return g * sg, sg * (1.0 + g * (1.0 - sg))


A_PAIRS_FWD = 8
A_PAIRS_BWD = 4


def _a_specs(pairs):
    lanes = 128 * pairs
    steps = D_MODEL // lanes
    q = pl.BlockSpec((QBLK, lanes), lambda p, j: (j, p))
    ks = [pl.BlockSpec((QBLK, lanes), lambda p, j, b=b: (jnp.maximum(j - 2 + b, 0), steps + p)) for b in range(3)]
    vs = [pl.BlockSpec((QBLK, lanes), lambda p, j, b=b: (jnp.maximum(j - 2 + b, 0), 2 * steps + p))
          for b in range(3)]
    g = pl.BlockSpec((QBLK, lanes), lambda p, j: (j, 3 * steps + p))
    bias = pl.BlockSpec((pairs, 8, A_KEYS), lambda p, j: (p, 0, 0))
    return q, ks, vs, g, bias


def _a_fill_bias(w_ref, b_ref, j, pairs):
    _fill_bias(2 * pairs, lambda h: w_ref[h // 2, h % 2:h % 2 + 1, :], A_BAND, b_ref, j)


def _by_valid_key_blocks(j, fn):
    pl.when(j == 0)(functools.partial(fn, 1))
    pl.when(j == 1)(functools.partial(fn, 2))
    pl.when(j >= 2)(functools.partial(fn, 3))


def _fill_bias(n, get_row, band, bias_scr, j):
    @pl.when(j == 0)
    def _():
        for h in range(n):
            bias_scr[h] = _band_bias(get_row(h), band, bias_scr.shape[1])


def _normalise_pair(rs, mxs, lane_lo, extra=None):
    num = jnp.where(lane_lo, rs[0], rs[1])
    den = pltpu.roll(jnp.where(lane_lo, rs[1], rs[0]), HEAD_DIM, 1)
    if extra is not None:
        den = den + jnp.where(lane_lo, extra[0], extra[1])
    return num / den, jnp.where(lane_lo, mxs[0], mxs[1]) + jnp.log(den)


def _own_everywhere(x, sel):
    return jnp.where(sel, x, pltpu.roll(x, HEAD_DIM, 1))


def _minus_rows(s, row_full):
    return jnp.concatenate([s[:, i:i + 128] - row_full for i in range(0, s.shape[1], 128)], axis=1)


def _attn_a_fwd(qkvg, bias, gather):
    t = qkvg.shape[0]
    nq = t // QBLK
    n_g = len(gather)
    pairs = A_PAIRS_FWD
    lanes = 128 * pairs
    steps = D_MODEL // lanes
    q_spec, k_specs, v_specs, g_spec, bias_spec = _a_specs(pairs)

    def body(q_ref, k0, k1, k2, v0, v1, v2, g_ref, w_ref, *rest):
        shard_refs, rest = rest[:n_g], rest[n_g:]
        z_ref, o_ref, lse_ref = rest[:3]
        full_refs, (b_ref, *comm) = rest[3:3 + n_g], rest[3 + n_g:]
        p = pl.program_id(0)
        j = pl.program_id(1)
        start, forward, finish = _gather_phases(shard_refs, full_refs, *comm)
        at = p * nq + j
        pl.when(at == 0)(start)
        pl.when(at == steps * nq // 2)(forward)
        _a_fill_bias(w_ref, b_ref, j, pairs)
        lane_lo = _lane_lo()
        sels = (lane_lo, jnp.logical_not(lane_lo))

        def attend(n_blocks):
            first_col = (3 - n_blocks) * QBLK
            for pp in range(pairs):
                cols = slice(128 * pp, 128 * (pp + 1))
                k = jnp.concatenate([r[:, cols] for r in (k0, k1, k2)[3 - n_blocks:]], axis=0)
                v = jnp.concatenate([r[:, cols] for r in (v0, v1, v2)[3 - n_blocks:]], axis=0)
                q = q_ref[:, cols]
                qm2 = jnp.concatenate([jnp.where(sel, q, jnp.zeros_like(q)) for sel in sels], axis=0) * SCALE
                s2 = _dot_nt(qm2, k)
                rs, mxs = [], []
                for hh, sel in enumerate(sels):
                    s = s2[hh * QBLK:(hh + 1) * QBLK] + b_ref[2 * pp + hh, :, first_col:]
                    mxs.append(jnp.max(s, axis=-1, keepdims=True))
                    e = jnp.exp(s - mxs[hh]).astype(BF16)
                    rs.append(_dot(e, jnp.where(sel, v, jnp.ones_like(v))))
                o, lse = _normalise_pair(rs, mxs, lane_lo)
                silu, _ = _silu_parts(g_ref[:, cols].astype(F32))
                o_ref[:, cols] = o.astype(BF16)
                z_ref[:, cols] = (o * silu).astype(BF16)
                lse_ref[:, cols] = lse

        _by_valid_key_blocks(j, attend)
        pl.when(at == steps * nq - 1)(finish)

    out_spec = pl.BlockSpec((QBLK, lanes), lambda p, j: (j, p))
    outs = pl.pallas_call(
        body, name="attn_a_fwd", grid=(steps, nq),
        in_specs=[q_spec, *k_specs, *v_specs, g_spec, bias_spec] + [ANY] * n_g,
        out_specs=[out_spec, out_spec, out_spec] + [ANY] * n_g,
        out_shape=[SDS((t, D_MODEL), BF16), SDS((t, D_MODEL), BF16), SDS((t, D_MODEL), F32)]
        + [SDS((N_DEV, *s.shape), s.dtype) for s in gather],
        scratch_shapes=[pltpu.VMEM((2 * pairs, QBLK, A_KEYS), F32)] + _gather_scratch(n_g),
        compiler_params=_cparams(),
    )(qkvg, qkvg, qkvg, qkvg, qkvg, qkvg, qkvg, qkvg, bias, *gather)
    return outs[0], outs[1], outs[2], list(outs[3:])


def _attn_a_bwd(qkvg, bias, out_a, lse, dz, scatter):
    t = qkvg.shape[0]
    nq = t // QBLK
    n_sc = len(scatter)
    pairs = A_PAIRS_BWD
    lanes = 128 * pairs
    steps = D_MODEL // lanes
    q_spec, k_specs, v_specs, g_spec, bias_spec = _a_specs(pairs)

    def body(q_ref, k0, k1, k2, v0, v1, v2, g_ref, w_ref, o_ref, lse_ref, dz_ref, *rest):
        sc_refs, rest = rest[:n_sc], rest[n_sc:]
        dqg_ref, dkv_ref, dg_ref = rest[:3]
        land_refs, rest = rest[3:3 + n_sc], rest[3 + n_sc:]
        dk_acc, dv_acc, gt_acc, b_ref, send_sems, recv_sems = rest
        j = pl.program_id(1)
        first = jnp.logical_and(pl.program_id(0) == 0, j == 0)
        last = jnp.logical_and(pl.program_id(0) == steps - 1, j == nq - 1)

        @pl.when(first)
        def _():
            for cp in _scatter_copies(sc_refs, land_refs, send_sems, recv_sems):
                cp.start()

        _a_fill_bias(w_ref, b_ref, j, pairs)

        @pl.when(j == 0)
        def _():
            dk_acc[...] = jnp.zeros_like(dk_acc)
            dv_acc[...] = jnp.zeros_like(dv_acc)
            gt_acc[...] = jnp.zeros_like(gt_acc)

        lane_lo = _lane_lo()
        sels = (lane_lo, jnp.logical_not(lane_lo))

        def attend(n_blocks):
            first_col = (3 - n_blocks) * QBLK
            for pp in range(pairs):
                cols = slice(128 * pp, 128 * (pp + 1))
                q = q_ref[:, cols]
                k = jnp.concatenate([r[:, cols] for r in (k0, k1, k2)[3 - n_blocks:]], axis=0)
                v = jnp.concatenate([r[:, cols] for r in (v0, v1, v2)[3 - n_blocks:]], axis=0)
                o = o_ref[:, cols].astype(F32)
                lse_pair = lse_ref[:, cols]
                dzf = dz_ref[:, cols].astype(F32)
                silu, dsilu = _silu_parts(g_ref[:, cols].astype(F32))
                do = dzf * silu
                dqg_ref[1, :, cols] = (dzf * o * dsilu).astype(BF16)
                doo = do * o
                qm2 = jnp.concatenate([jnp.where(sel, q, jnp.zeros_like(q)) for sel in sels], axis=0) * SCALE
                dom2 = jnp.concatenate([jnp.where(sel, do, 0.0) for sel in sels], axis=0).astype(BF16)
                s2 = _dot_nt(qm2, k)
                dp2 = _dot_nt(dom2, v)
                ps, dss = [], []
                for hh, sel in enumerate(sels):
                    rows = slice(hh * QBLK, (hh + 1) * QBLK)
                    s = s2[rows] + b_ref[2 * pp + hh, :, first_col:]
                    p = jnp.exp(_minus_rows(s, _own_everywhere(lse_pair, sel)))
                    delta = jnp.sum(jnp.where(sel, doo, 0.0), axis=-1, keepdims=True)
                    ds = p * (dp2[rows] - delta)
                    gt_acc[2 * pp + hh] += _collapse_chunks(ds, A_KEYS)
                    ps.append(p.astype(BF16))
                    dss.append(ds.astype(BF16))
                dsb2 = jnp.concatenate(dss, axis=0)
                dq2 = _dot(dsb2, k) * SCALE
                dk_blk = _dot_tn(dsb2, qm2)
                dv_blk = _dot_tn(jnp.concatenate(ps, axis=0), dom2)
                dqg_ref[0, :, cols] = jnp.where(lane_lo, dq2[0:QBLK], dq2[QBLK:2 * QBLK]).astype(BF16)
                for b in range(n_blocks):
                    rows = pl.ds(pl.multiple_of((j - n_blocks + 1 + b) * QBLK, QBLK), QBLK)
                    dk_acc[rows, cols] += dk_blk[b * QBLK:(b + 1) * QBLK]
                    dv_acc[rows, cols] += dv_blk[b * QBLK:(b + 1) * QBLK]

        _by_valid_key_blocks(j, attend)

        @pl.when(j == nq - 1)
        def _():
            dkv_ref[0] = dk_acc[...].astype(BF16)
            dkv_ref[1] = dv_acc[...].astype(BF16)
            for pp in range(pairs):
                dg_ref[pp] = jnp.concatenate([_offset_sums(gt_acc[2 * pp]), _offset_sums(gt_acc[2 * pp + 1]),
                                              jnp.zeros((6, A_DIAG), F32)], axis=0)

        @pl.when(last)
        def _():
            for cp in _scatter_copies(sc_refs, land_refs, send_sems, recv_sems):
                cp.wait()

    blk = pl.BlockSpec((QBLK, lanes), lambda p, j: (j, p))
    outs = pl.pallas_call(
        body, name="attn_a_bwd", grid=(steps, nq),
        in_specs=[q_spec, *k_specs, *v_specs, g_spec, bias_spec, blk, blk, blk] + [ANY] * n_sc,
        out_specs=[pl.BlockSpec((2, QBLK, lanes), lambda p, j: (0, j, p)),
                   pl.BlockSpec((2, t, lanes), lambda p, j: (0, 0, p)),
                   pl.BlockSpec((pairs, 8, A_DIAG), lambda p, j: (p, 0, 0))] + [ANY] * n_sc,
        out_shape=[SDS((2, t, D_MODEL), BF16), SDS((2, t, D_MODEL), BF16), SDS((N_HEADS // 2, 8, A_DIAG), F32)]
        + [SDS((N_DEV - 1, *g.shape[1:]), g.dtype) for g in scatter],
        scratch_shapes=[pltpu.VMEM((t, lanes), F32), pltpu.VMEM((t, lanes), F32),
                        pltpu.VMEM((2 * pairs, CHUNK, A_KEYS), F32), pltpu.VMEM((2 * pairs, QBLK, A_KEYS), F32),
                        pltpu.SemaphoreType.DMA(((N_DEV - 1) * n_sc,)),
                        pltpu.SemaphoreType.DMA(((N_DEV - 1) * n_sc,))],
        compiler_params=_cparams(),
    )(qkvg, qkvg, qkvg, qkvg, qkvg, qkvg, qkvg, qkvg, bias, out_a, lse, dz, *scatter)
    return outs[0], outs[1], outs[2], list(outs[3:])


def _b_specs(qblk):
    per = qblk // B_PREV
    q = pl.BlockSpec((qblk, 512), lambda h, j: (j, h))
    g = pl.BlockSpec((qblk, 512), lambda h, j: (j, 2 + h))
    kp = pl.BlockSpec((B_PREV, 128), lambda h, j: (jnp.maximum(per * j - 1, 0), 0))
    kc = pl.BlockSpec((qblk, 128), lambda h, j: (j, 0))
    vp = pl.BlockSpec((B_PREV, 128), lambda h, j: (jnp.maximum(per * j - 1, 0), 1))
    vc = pl.BlockSpec((qblk, 128), lambda h, j: (j, 1))
    bias = pl.BlockSpec((B_GROUP, qblk + B_PREV), lambda h, j: (h, 0))
    sinks = pl.BlockSpec(memory_space=pltpu.SMEM)
    return q, g, kp, kc, vp, vc, bias, sinks


def _b_operands(kp, kc, vp, vc, kvh, with_prev):
    k = jnp.concatenate([kp[...], kc[...]], axis=0) if with_prev else kc[...]
    v = jnp.concatenate([vp[...], vc[...]], axis=0) if with_prev else vc[...]
    kr = pltpu.roll(k, HEAD_DIM, 1)
    vr = pltpu.roll(v, HEAD_DIM, 1)
    first = kvh == 0
    return (jnp.where(first, k, kr), jnp.where(first, kr, k),
            jnp.where(first, v, vr), jnp.where(first, vr, v))


def _attn_b_fwd(qg, kv, bias, sinks):
    t = qg.shape[0]
    qblk = B_QBLK_FWD
    per_step = 4
    step = per_step * qblk
    q_spec, g_spec, kp_spec, kc_spec, vp_spec, vc_spec, _, sink_spec = _b_specs(step)
    bias_spec = pl.BlockSpec((B_GROUP, qblk + B_PREV), lambda h, j: (h, 0))

    def body(q_ref, g_ref, kp, kc, vp, vc, w_ref, sink_ref, z_ref, o_ref, lse_ref, b_ref):
        kvh = pl.program_id(0)
        j = pl.program_id(1)
        _fill_bias(B_GROUP, lambda h: w_ref[h:h + 1, :], B_BAND, b_ref, j)
        lane_lo = _lane_lo()
        n_pairs = B_GROUP // 2

        def attend(first):
            k_lo, k_hi, v_lo, v_hi = _b_operands(kp, kc, vp, vc, kvh, True)
            for sb in range(per_step):
                no_prev = first and sb == 0
                first_col = B_PREV if no_prev else 0
                keys = slice(sb * qblk + first_col, (sb + 1) * qblk + B_PREV)
                qrows = slice(sb * qblk, (sb + 1) * qblk)
                halves = []
                for hh, sel in enumerate((lane_lo, jnp.logical_not(lane_lo))):
                    kk = (k_lo if hh == 0 else k_hi)[keys]
                    vv = (v_lo if hh == 0 else v_hi)[keys]
                    qm4 = jnp.concatenate(
                        [jnp.where(sel, q_ref[qrows, 128 * pp:128 * (pp + 1)], jnp.zeros((qblk, 128), BF16))
                         for pp in range(n_pairs)], axis=0) * SCALE
                    s4 = _dot_nt(qm4, kk)
                    es, mxs = [], []
                    for pp in range(n_pairs):
                        g = 2 * pp + hh
                        s = s4[pp * qblk:(pp + 1) * qblk] + b_ref[g, :, first_col:]
                        mxs.append(jnp.maximum(jnp.max(s, axis=-1, keepdims=True), sink_ref[kvh * B_GROUP + g]))
                        es.append(jnp.exp(s - mxs[pp]).astype(BF16))
                    r4 = _dot(jnp.concatenate(es, axis=0), jnp.where(sel, vv, jnp.ones_like(vv)))
                    halves.append((r4, mxs))
                for pp in range(n_pairs):
                    cols = slice(128 * pp, 128 * (pp + 1))
                    rows = slice(pp * qblk, (pp + 1) * qblk)
                    mxs = [halves[hh][1][pp] for hh in range(2)]
                    sink_terms = [jnp.exp(sink_ref[kvh * B_GROUP + 2 * pp + hh] - mxs[hh]) for hh in range(2)]
                    o, lse = _normalise_pair([halves[hh][0][rows] for hh in range(2)], mxs, lane_lo, sink_terms)
                    silu, _ = _silu_parts(g_ref[qrows, cols].astype(F32))
                    o_ref[qrows, cols] = o.astype(BF16)
                    z_ref[qrows, cols] = (o * silu).astype(BF16)
                    lse_ref[qrows, cols] = lse

        pl.when(j == 0)(functools.partial(attend, True))
        pl.when(j >= 1)(functools.partial(attend, False))

    out_spec = pl.BlockSpec((step, 512), lambda h, j: (j, h))
    return pl.pallas_call(
        body, name="attn_b_fwd", grid=(B_KV_HEADS, t // step),
        in_specs=[q_spec, g_spec, kp_spec, kc_spec, vp_spec, vc_spec, bias_spec, sink_spec],
        out_specs=[out_spec, out_spec, out_spec],
        out_shape=[SDS((t, D_MODEL), BF16), SDS((t, D_MODEL), BF16), SDS((t, D_MODEL), F32)],
        scratch_shapes=[pltpu.VMEM((B_GROUP, qblk, qblk + B_PREV), F32)],
        compiler_params=_cparams(),
    )(qg, qg, kv, kv, kv, kv, bias, sinks)


def _attn_b_bwd(qg, kv, bias, sinks, out_b, lse, dz, bucket_onehot):
    t = qg.shape[0]
    qblk = B_QBLK_BWD
    keys = qblk + B_PREV
    nq = t // qblk
    q_spec, g_spec, kp_spec, kc_spec, vp_spec, vc_spec, bias_spec, sink_spec = _b_specs(qblk)

    def body(q_ref, g_ref, kp, kc, vp, vc, w_ref, sink_ref, o_ref, lse_ref, dz_ref, oh_ref,
             dqg_ref, dkv_ref, dt5_ref, dsink_ref, gt_acc, b_ref):
        kvh = pl.program_id(0)
        j = pl.program_id(1)
        _fill_bias(B_GROUP, lambda h: w_ref[h:h + 1, :], B_BAND, b_ref, j)

        @pl.when(jnp.logical_and(kvh == 0, j == 0))
        def _():
            dkv_ref[...] = jnp.zeros_like(dkv_ref)

        @pl.when(j == 0)
        def _():
            gt_acc[...] = jnp.zeros_like(gt_acc)
            dsink_ref[...] = jnp.zeros_like(dsink_ref)

        lane_lo = _lane_lo()

        def attend(with_prev):
            first_col = 0 if with_prev else B_PREV
            k_lo, k_hi, v_lo, v_hi = _b_operands(kp, kc, vp, vc, kvh, with_prev)
            dk_blk = jnp.zeros((keys - first_col, 128), F32)
            dv_blk = jnp.zeros((keys - first_col, 128), F32)
            for pp in range(B_GROUP // 2):
                cols = slice(128 * pp, 128 * (pp + 1))
                qp = q_ref[:, cols]
                o = o_ref[:, cols].astype(F32)
                lse_pair = lse_ref[:, cols]
                dzf = dz_ref[:, cols].astype(F32)
                silu, dsilu = _silu_parts(g_ref[:, cols].astype(F32))
                do = dzf * silu
                dqg_ref[1, :, cols] = (dzf * o * dsilu).astype(BF16)
                doo = do * o
                dqs = []
                for hh in range(2):
                    g = 2 * pp + hh
                    sel = lane_lo if hh == 0 else jnp.logical_not(lane_lo)
                    sink = sink_ref[kvh * B_GROUP + g]
                    kk = k_lo if hh == 0 else k_hi
                    vv = v_lo if hh == 0 else v_hi
                    qm = jnp.where(sel, qp, jnp.zeros_like(qp)) * SCALE
                    s = _dot_nt(qm, kk) + b_ref[g, :, first_col:]
                    lse_h = _own_everywhere(lse_pair, sel)
                    p = jnp.exp(_minus_rows(s, lse_h))
                    delta = jnp.sum(jnp.where(sel, doo, 0.0), axis=-1, keepdims=True)
                    dom = jnp.where(sel, do, 0.0).astype(BF16)
                    dp = _dot_nt(dom, vv)
                    ds = p * (dp - delta)
                    gt_acc[g, :, first_col:] += ds
                    dsink_ref[g:g + 1, :] -= jnp.sum(jnp.exp(sink - lse_h) * delta, axis=0, keepdims=True)
                    dsb = ds.astype(BF16)
                    dqs.append(_dot(dsb, kk) * SCALE)
                    dk_blk = dk_blk + _dot_tn(dsb, qm)
                    dv_blk = dv_blk + _dot_tn(p.astype(BF16), dom)
                dqg_ref[0, :, cols] = jnp.where(lane_lo, dqs[0], dqs[1]).astype(BF16)
            mine = lane_lo == (kvh == 0)
            dk_add = jnp.where(mine, dk_blk + pltpu.roll(dk_blk, HEAD_DIM, 1), 0.0)
            dv_add = jnp.where(mine, dv_blk + pltpu.roll(dv_blk, HEAD_DIM, 1), 0.0)
            first_key = B_PREV if with_prev else 0
            if with_prev:
                rows = pl.ds(pl.multiple_of(j * qblk - B_PREV, B_PREV), B_PREV)
                dkv_ref[0, rows, :] += dk_add[0:B_PREV]
                dkv_ref[1, rows, :] += dv_add[0:B_PREV]
            rows = pl.ds(pl.multiple_of(j * qblk, qblk), qblk)
            dkv_ref[0, rows, :] += dk_add[first_key:first_key + qblk]
            dkv_ref[1, rows, :] += dv_add[first_key:first_key + qblk]

        pl.when(j == 0)(functools.partial(attend, False))
        pl.when(j >= 1)(functools.partial(attend, True))

        @pl.when(j == nq - 1)
        def _():
            dd = jnp.concatenate([_offset_sums(_collapse_chunks(gt_acc[g], keys)) for g in range(B_GROUP)], axis=0)
            hi = dd.astype(BF16)
            lo = (dd - hi.astype(F32)).astype(BF16)
            dt5_ref[...] = _dot(hi, oh_ref[...]) + _dot(lo, oh_ref[...])

    blk = pl.BlockSpec((qblk, 512), lambda h, j: (j, h))
    return pl.pallas_call(
        body, name="attn_b_bwd", grid=(B_KV_HEADS, nq),
        in_specs=[q_spec, g_spec, kp_spec, kc_spec, vp_spec, vc_spec, bias_spec, sink_spec, blk, blk, blk,
                  pl.BlockSpec((keys, 128), lambda h, j: (0, 0))],
        out_specs=[pl.BlockSpec((2, qblk, 512), lambda h, j: (0, j, h)),
                   pl.BlockSpec((2, t, 128), lambda h, j: (0, 0, 0)),
                   pl.BlockSpec((B_GROUP, 128), lambda h, j: (h, 0)),
                   pl.BlockSpec((B_GROUP, 128), lambda h, j: (h, 0))],
        out_shape=[SDS((2, t, D_MODEL), BF16), SDS((2, t, 128), F32),
                   SDS((N_HEADS, 128), F32), SDS((N_HEADS, 128), F32)],
        scratch_shapes=[pltpu.VMEM((B_GROUP, qblk, keys), F32), pltpu.VMEM((B_GROUP, qblk, keys), F32)],
        compiler_params=_cparams(),
    )(qg, qg, kv, kv, kv, kv, bias, sinks, out_b, lse, dz, bucket_onehot)


def _a_bias_by_offset(rel_bias):
    m = np.arange(A_DIAG)
    idx = np.clip(A_BAND - 1 - m, -A_REL_CLIP, A_REL_CLIP) + A_REL_CLIP
    by_head = rel_bias[idx].T.reshape(N_HEADS // 2, 2, A_DIAG)
    return jnp.concatenate([by_head, jnp.zeros((N_HEADS // 2, 6, A_DIAG), F32)], axis=1)


def _a_bias_grad(offset_sums):
    first = 319
    tail = jnp.sum(offset_sums[:, :first], axis=1)
    body = jnp.flip(offset_sums[:, first:first + 320], axis=1)
    body = body.at[:, -1].add(tail)
    full = jnp.concatenate([jnp.zeros((N_HEADS, 193), F32), body], axis=1)
    return full


def _t5_bucket(rel):
    nb = T5_BUCKETS // 2
    max_exact = nb // 2
    ret = jnp.where(rel > 0, nb, 0)
    n = jnp.abs(rel)
    nf = jnp.maximum(n, 1).astype(jnp.float32)
    large = max_exact + (jnp.log(nf / max_exact) / math.log(T5_MAX_DIST / max_exact)
                         * (nb - max_exact)).astype(jnp.int32)
    large = jnp.minimum(large, nb - 1)
    return ret + jnp.where(n < max_exact, n, large)


def _b_offset_buckets(keys):
    return _t5_bucket(jnp.arange(keys, dtype=jnp.int32) - (B_LEFT_CHUNKS * CHUNK + CHUNK - 1))


def _b_bias_by_offset(t5_table, keys):
    return t5_table[_b_offset_buckets(keys)].T


def _b_bucket_onehot(keys):
    return (_b_offset_buckets(keys)[:, None] == jnp.arange(128)[None, :]).astype(BF16)


def _local_step(my_slot, order, x, target, a_gain_shard, w_in_a_shard, rel_bias, late_shards, kv_gain,
                t5_table, b_gain, sinks, f_gain):
    a_bias = _a_bias_by_offset(rel_bias)
    b_bias_fwd = _b_bias_by_offset(t5_table, B_QBLK_FWD + B_PREV)
    b_bias_bwd = _b_bias_by_offset(t5_table, B_QBLK_BWD + B_PREV)
    sinks_flat = sinks.reshape(N_HEADS)

    xn, qkvg, w_in_a, a_gain = _norm_matmul_gather(order, x, a_gain_shard, w_in_a_shard)
    z_a, out_a, lse_a, (w_in_b, w_out_a, w_out_b, kv_w) = _attn_a_fwd(qkvg, a_bias, late_shards)
    w_out_a = w_out_a.reshape(D_MODEL, D_MODEL)
    w_out_b = w_out_b.reshape(D_MODEL, D_MODEL)
    kv_w = kv_w.reshape(D_MODEL, 2 * 128)
    h1, kvn, hb, kv, qg = _layer_a_out(x, z_a, w_out_a, kv_gain, b_gain, kv_w, w_in_b)
    z_b, out_b, lse_b = _attn_b_fwd(qg, kv, b_bias_fwd, sinks_flat)
    dh2, dh2b, dz_b, loss, d_fn = _layer_b_out_loss(h1, z_b, w_out_b, f_gain, target)

    dqg_b, dkv_b, d_t5, d_sink = _attn_b_bwd(qg, kv, b_bias_bwd, sinks_flat, out_b, lse_b, dz_b,
                                             _b_bucket_onehot(B_QBLK_BWD + B_PREV))
    dh1, dh1b, dz_a, d_bn, d_kn = _layer_b_in_bwd(dqg_b, dkv_b, w_in_b, kv_w, h1, dh2, b_gain, kv_gain, w_out_a)
    early = dict(
        b_w_out=_weight_grad_rows("grad_b_w_out", my_slot, z_b, dh2b[None]),
        b_w_in=_weight_grad_cols("grad_b_w_in", my_slot, hb, [dqg_b],
                                 [(0, o, c, 4 * o + c) for o in range(2) for c in range(4)], 256),
        kv_w=_weight_grad_rows("grad_kv_w", my_slot, kvn, dkv_b),
        a_w_out=_weight_grad_rows("grad_a_w_out", my_slot, z_a, dh1b[None]))
    dqg_a, dkv_a, d_rel, landed = _attn_a_bwd(qkvg, a_bias, out_a, lse_a, dz_a, [g[0] for g in early.values()])
    g_w_in_a = _weight_grad_cols(
        "grad_a_w_in", my_slot, xn, [dqg_a, dkv_a],
        [(0, 0, 0, 0), (0, 0, 1, 1), (1, 0, 0, 2), (1, 0, 1, 3), (1, 1, 0, 4), (1, 1, 1, 5), (0, 1, 0, 6), (0, 1, 1, 7)], 512)
    chip_sums, from_sibling = _chip_sums(g_w_in_a[0])
    grad_x, d_an, from_chips = _layer_a_in_bwd(dqg_a, dkv_a, w_in_a, x, dh1, a_gain, chip_sums)

    matrices = {n: (g[1], [(land, 0, N_DEV - 1)]) for (n, g), land in zip(early.items(), landed)}
    matrices["a_w_in"] = (g_w_in_a[1], [(from_sibling, 0, 1), (from_chips, 0, 3)])
    small = dict(
        loss=loss, a_norm=d_an, a_rel_bias=d_rel[:, :2].reshape(N_HEADS, A_DIAG),
        kv_norm=d_kn, t5_bias=d_t5, b_norm=d_bn, b_sinks=d_sink, final_norm=d_fn)
    return grad_x, small, matrices


def _place():
    x, y, c = lax.axis_index("x"), lax.axis_index("y"), lax.axis_index("c")
    chips = [(1 - x, y), (x, 1 - y), (1 - x, 1 - y)]
    return x, y, c, chips


def _slot(px, py, pc):
    return 4 * px + 2 * py + pc


ANY = pl.BlockSpec(memory_space=pl.ANY)


def _peer(x, y, c, k):
    return (x ^ (k >> 2), y ^ ((k >> 1) & 1), c ^ (k & 1))


def _scatter_copies(grad_refs, land_refs, send_sems, recv_sems):
    x, y, c, _ = _place()
    copies = []
    for t, (grad, land) in enumerate(zip(grad_refs, land_refs)):
        for k in range(1, N_DEV):
            peer = _peer(x, y, c, k)
            sem = (N_DEV - 1) * t + k - 1
            copies.append(pltpu.make_async_remote_copy(
                src_ref=grad.at[_slot(*peer)], dst_ref=land.at[k - 1],
                send_sem=send_sems.at[sem], recv_sem=recv_sems.at[sem],
                device_id=peer, device_id_type=MESH))
    return copies


def _gather_phases(ins, outs, send_sems, recv_sems, local_sems):
    n = len(ins)
    x, y, c, chips = _place()
    me, sibling = (x, y, c), (x, y, 1 - c)

    def copy(t, k, block, to, src=None):
        dst = outs[t].at[_slot(*block)]
        return pltpu.make_async_remote_copy(
            src_ref=dst if src is None else src, dst_ref=dst,
            send_sem=send_sems.at[7 * t + k], recv_sem=recv_sems.at[7 * t + k],
            device_id=to, device_id_type=MESH)

    def lists():
        mine = [pltpu.make_async_copy(ins[t], outs[t].at[_slot(*me)], local_sems.at[t]) for t in range(n)]
        first = []
        for t in range(n):
            first.append(copy(t, 0, me, sibling, src=ins[t]))
            first += [copy(t, 1 + j, me, (*chip, c), src=ins[t]) for j, chip in enumerate(chips)]
        passed = [copy(t, 4 + j, (*chip, c), sibling) for t in range(n) for j, chip in enumerate(chips)]
        return mine, first, passed

    def start():
        mine, first, _ = lists()
        for cp in mine + first:
            cp.start()

    def forward():
        _, _, passed = lists()
        for t in range(n):
            for j, chip in enumerate(chips):
                copy(t, 1 + j, (*chip, c), me).wait_recv()
                passed[3 * t + j].start()

    def finish():
        mine, first, passed = lists()
        for t in range(n):
            copy(t, 0, sibling, me).wait_recv()
            for j, chip in enumerate(chips):
                copy(t, 4 + j, (*chip, 1 - c), me).wait_recv()
        for cp in first + passed:
            cp.wait_send()
        for cp in mine:
            cp.wait()

    return start, forward, finish


def _gather_scratch(n):
    return [pltpu.SemaphoreType.DMA((7 * n,)), pltpu.SemaphoreType.DMA((7 * n,)), pltpu.SemaphoreType.DMA((n,))]


def _chip_sums(g):
    _, r, c = g.shape

    def body(g_ref, sums_ref, mine_ref, land, own, send_sems, recv_sems, load_sems):
        x, y, c_i, chips = _place()
        sibling = (x, y, 1 - c_i)
        blocks = [(*chip, 1 - c_i) for chip in chips] + [sibling]
        sends = [pltpu.make_async_remote_copy(
            src_ref=g_ref.at[_slot(*block)], dst_ref=land.at[k], send_sem=send_sems.at[k],
            recv_sem=recv_sems.at[k], device_id=sibling, device_id_type=MESH) for k, block in enumerate(blocks)]
        loads = [pltpu.make_async_copy(g_ref.at[_slot(*chip, c_i)], own.at[j], load_sems.at[j])
                 for j, chip in enumerate(chips)]
        for cp in sends + loads:
            cp.start()
        for cp in sends + loads:
            cp.wait()
        for j in range(3):
            sums_ref[j] = (own[j].astype(F32) + land[j].astype(F32)).astype(BF16)
        mine_ref[0] = land[3]

    return pl.pallas_call(
        body, name="chip_sums",
        in_specs=[ANY], out_specs=[VM, VM],
        out_shape=[SDS((3, r, c), BF16), SDS((1, r, c), BF16)],
        scratch_shapes=[pltpu.VMEM((4, r, c), BF16), pltpu.VMEM((3, r, c), BF16),
                        pltpu.SemaphoreType.DMA((4,)), pltpu.SemaphoreType.DMA((4,)), pltpu.SemaphoreType.DMA((3,))],
        compiler_params=_cparams(),
    )(g)


def _chip_copies(sums_ref, land_ref, send_sems, recv_sems):
    x, y, c, chips = _place()
    del x, y
    return [pltpu.make_async_remote_copy(
        src_ref=sums_ref.at[j], dst_ref=land_ref.at[j], send_sem=send_sems.at[j], recv_sem=recv_sems.at[j],
        device_id=(*chip, c), device_id_type=MESH) for j, chip in enumerate(chips)]


def _row_tile(rows):
    return min(rows, 256)


def _adamw(w, g, m, v):
    m2 = ADAM_B1 * m + (1.0 - ADAM_B1) * g
    v2 = ADAM_B2 * v + (1.0 - ADAM_B2) * jnp.square(g)
    m_hat = m2 / (1.0 - ADAM_B1 ** ADAM_STEP)
    v_hat = v2 / (1.0 - ADAM_B2 ** ADAM_STEP)
    delta = -ADAM_LR * (m_hat / (jnp.sqrt(v_hat) + ADAM_EPS) + ADAM_WD * w)
    return delta, m2, v2


def _reduce_adamw(name, own, partials, w, m, v):
    r, c = own.shape
    tr = _row_tile(r)
    n_p = len(partials)

    def body(own_ref, *rest):
        p_refs, (w_ref, m_ref, v_ref, grad_ref, d_ref, nm_ref, nv_ref) = rest[:n_p], rest[n_p:]
        grad = own_ref[...]
        for p_ref, (_, _, count) in zip(p_refs, partials):
            for j in range(count):
                grad = grad + p_ref[j].astype(F32)
        grad_ref[...] = grad
        d_ref[...], nm_ref[...], nv_ref[...] = _adamw(w_ref[...], grad, m_ref[...], v_ref[...])

    flat = pl.BlockSpec((tr, c), lambda i: (i, 0))
    return pl.pallas_call(
        body, name=name, grid=(r // tr,),
        in_specs=[flat] + [pl.BlockSpec((count, tr, c), lambda i, first=first, count=count: (first // count, i, 0))
                           for _, first, count in partials] + [flat, flat, flat],
        out_specs=[flat, flat, flat, flat],
        out_shape=[SDS((r, c), F32)] * 4,
        compiler_params=_cparams(),
    )(own, *[p[0] for p in partials], w, m, v)


VM = pl.BlockSpec()


def _small_allreduce(parts):
    n = len(parts)

    def body(*refs):
        ins, outs, lands = refs[:n], refs[n:2 * n], refs[2 * n:3 * n]
        send_sems, recv_sems = refs[3 * n:]
        x, y, c, _ = _place()
        my_slot = _slot(x, y, c)
        copies = []
        for t in range(n):
            lands[t][my_slot] = ins[t][...]
            for k in range(1, N_DEV):
                sem = (N_DEV - 1) * t + k - 1
                copies.append(pltpu.make_async_remote_copy(
                    src_ref=ins[t], dst_ref=lands[t].at[my_slot],
                    send_sem=send_sems.at[sem], recv_sem=recv_sems.at[sem],
                    device_id=_peer(x, y, c, k), device_id_type=MESH))
        for cp in copies:
            cp.start()
        for t in range(n):
            for k in range(1, N_DEV):
                sem = (N_DEV - 1) * t + k - 1
                pltpu.make_async_remote_copy(
                    src_ref=ins[t], dst_ref=lands[t].at[_slot(*_peer(x, y, c, k))],
                    send_sem=send_sems.at[sem], recv_sem=recv_sems.at[sem],
                    device_id=(x, y, c), device_id_type=MESH).wait_recv()
        for cp in copies:
            cp.wait_send()
        for t in range(n):
            total = lands[t][0]
            for s in range(1, N_DEV):
                total = total + lands[t][s]
            outs[t][...] = total

    n_sems = (N_DEV - 1) * n
    return pl.pallas_call(
        body, name="small_allreduce",
        in_specs=[VM] * n, out_specs=[VM] * n, out_shape=[SDS(p.shape, F32) for p in parts],
        scratch_shapes=[pltpu.VMEM((N_DEV, *p.shape), F32) for p in parts]
        + [pltpu.SemaphoreType.DMA((n_sems,)), pltpu.SemaphoreType.DMA((n_sems,))],
    )(*parts)


def _small_adamw(my_slot, sums, ws, ms, vs):
    n = len(ws)

    def body(slot_ref, *refs):
        sum_refs, refs = refs[:n + 1], refs[n + 1:]
        w_refs, m_refs, v_refs, refs = refs[:n], refs[n:2 * n], refs[2 * n:3 * n], refs[3 * n:]
        g_refs, d_refs, nm_refs, nv_refs = refs[:n + 1], refs[n + 1:2 * n + 1], refs[2 * n + 1:3 * n + 1], refs[3 * n + 1:]
        for t in range(n + 1):
            if t == 0:
                g = sum_refs[0][:, pl.ds(pl.multiple_of(slot_ref[0] * 128, 128), 128)]
            else:
                g = sum_refs[t][...]
            g_refs[t][...] = g
            if t < n:
                d_refs[t][...], nm_refs[t][...], nv_refs[t][...] = _adamw(w_refs[t][...], g, m_refs[t][...], v_refs[t][...])

    shapes = [SDS(w.shape, F32) for w in ws]
    outs = pl.pallas_call(
        body, name="small_adamw",
        in_specs=[pl.BlockSpec(memory_space=pltpu.SMEM)] + [VM] * (4 * n + 1),
        out_specs=[VM] * (4 * n + 1),
        out_shape=shapes + [SDS(sums[-1].shape, F32)] + shapes * 3,
    )(my_slot, *sums, *ws, *ms, *vs)
    return outs[:n + 1], outs[n + 1:2 * n + 1], outs[2 * n + 1:3 * n + 1], outs[3 * n + 1:]


def kernel(x, a_norm, a_w_in, a_rel_bias, a_w_out, kv_norm, kv_w, t5_bias, b_norm, b_w_in, b_sinks, b_w_out, final_norm, loss_target, m_a_norm, m_a_w_in, m_a_rel_bias, m_a_w_out, m_kv_norm, m_kv_w, m_t5_bias, m_b_norm, m_b_w_in, m_b_sinks, m_b_w_out, m_final_norm, v_a_norm, v_a_w_in, v_a_rel_bias, v_a_w_out, v_kv_norm, v_kv_w, v_t5_bias, v_b_norm, v_b_w_in, v_b_sinks, v_b_w_out, v_final_norm):
    xi, yi, ci = lax.axis_index("x"), lax.axis_index("y"), lax.axis_index("c")
    my_slot = _slot(xi, yi, ci)

    slot_arr = jnp.reshape(my_slot, (1,)).astype(jnp.int32)
    order = _gather_order(xi, yi, ci)
    late_shards = [b_w_in[0].astype(BF16), a_w_out[0].astype(BF16), b_w_out[0].astype(BF16), kv_w.astype(BF16)]
    grad_x, loc, matrices = _local_step(
        slot_arr, order, x[0], loss_target[0], a_norm, a_w_in[0].astype(BF16), a_rel_bias[0], late_shards,
        kv_norm.reshape(1, D_MODEL), t5_bias, b_norm, b_sinks, final_norm.reshape(1, D_MODEL))

    shard_w = dict(a_w_in=a_w_in[0], b_w_in=b_w_in[0], a_w_out=a_w_out[0], b_w_out=b_w_out[0], kv_w=kv_w)
    shard_m = dict(a_w_in=m_a_w_in[0], b_w_in=m_b_w_in[0], a_w_out=m_a_w_out[0], b_w_out=m_b_w_out[0], kv_w=m_kv_w)
    shard_v = dict(a_w_in=v_a_w_in[0], b_w_in=v_b_w_in[0], a_w_out=v_a_w_out[0], b_w_out=v_b_w_out[0], kv_w=v_kv_w)
    big = {n: _reduce_adamw("adamw_" + n, own, partials, shard_w[n], shard_m[n], shard_v[n])
           for n, (own, partials) in matrices.items()}

    names = ("a_norm", "a_rel_bias", "kv_norm", "t5_bias", "b_norm", "b_sinks", "final_norm")
    tables = ("a_rel_bias", "t5_bias")

    def row(n, a):
        return a.reshape(-1, a.shape[-1]).T if n in tables else a.reshape(1, -1)

    small_w = [row(n, a) for n, a in zip(names, (a_norm, a_rel_bias, kv_norm, t5_bias, b_norm, b_sinks, final_norm))]
    small_m = [row(n, a) for n, a in zip(names, (m_a_norm, m_a_rel_bias, m_kv_norm, m_t5_bias, m_b_norm, m_b_sinks,
                                                 m_final_norm))]
    small_v = [row(n, a) for n, a in zip(names, (v_a_norm, v_a_rel_bias, v_kv_norm, v_t5_bias, v_b_norm, v_b_sinks,
                                                 v_final_norm))]
    sums = dict(zip(names + ("loss",), _small_allreduce([loc[n] for n in names] + [loc["loss"]])))
    sums["a_rel_bias"] = _a_bias_grad(sums["a_rel_bias"])
    sums["t5_bias"] = sums["t5_bias"][:, :T5_BUCKETS]
    sums["b_sinks"] = sums["b_sinks"][:, 0].reshape(1, N_HEADS)
    results = _small_adamw(slot_arr, [sums[n] for n in names + ("loss",)], small_w, small_m, small_v)
    like = dict(a_norm=a_norm, a_rel_bias=a_rel_bias, kv_norm=kv_norm, t5_bias=t5_bias, b_norm=b_norm,
                b_sinks=b_sinks, final_norm=final_norm)
    sm = [{n: (part[i].T if n in tables else part[i]).reshape(like[n].shape) for i, n in enumerate(names)}
          for part in results]
    loss = results[0][len(names)][0, 0]

    order = ("a_norm", "a_w_in", "a_rel_bias", "a_w_out", "kv_norm", "kv_w", "t5_bias", "b_norm",
             "b_w_in", "b_sinks", "b_w_out", "final_norm")
    lead = dict(a_w_in=True, b_w_in=True, a_w_out=True, b_w_out=True, kv_w=False)

    def pick(kind, name):
        if name in big:
            val = big[name][kind]
            return val[None] if lead[name] else val
        return sm[kind][name]

    outs = [loss, grad_x[None]]
    for kind in range(4):
        outs += [pick(kind, n) for n in order]
    return tuple(outs)
```

```python
import functools
import math

import numpy as np
import jax
import jax.numpy as jnp
from jax import lax
from jax.experimental import pallas as pl
from jax.experimental.pallas import tpu as pltpu

F32 = jnp.float32
BF16 = jnp.bfloat16
SDS = jax.ShapeDtypeStruct

D_MODEL = 1024
HEAD_DIM = 64
CHUNK = 64
N_HEADS = 16
RMS_EPS = 1e-6
A_LEFT_CHUNKS = 8
A_BAND = (A_LEFT_CHUNKS + 1) * CHUNK
A_REL_CLIP = 256
B_KV_HEADS = 2
B_GROUP = 8
B_LEFT_CHUNKS = 2
B_BAND = (B_LEFT_CHUNKS + 1) * CHUNK
T5_BUCKETS = 32
T5_MAX_DIST = 128
QBLK = 256
A_KEYS = 3 * QBLK
B_QBLK_FWD = 128
B_QBLK_BWD = 256
B_PREV = 128
A_DIAG = A_KEYS
NEG = -1e30
SCALE = HEAD_DIM ** -0.5
N_DEV = 8

ADAM_LR = 0.001
ADAM_B1 = 0.9
ADAM_B2 = 0.999
ADAM_EPS = 1e-08
ADAM_WD = 0.01
ADAM_STEP = 10

VMEM_LIMIT_BYTES = 56 * 1024 * 1024
MESH = pl.DeviceIdType.MESH


def _cparams():
    return pltpu.CompilerParams(vmem_limit_bytes=VMEM_LIMIT_BYTES)


def _dot(a, b):
    return jnp.dot(a, b, preferred_element_type=F32)


def _dot_nt(a, b):
    return lax.dot_general(a, b, (((1,), (1,)), ((), ())), preferred_element_type=F32)


def _dot_tn(a, b):
    return lax.dot_general(a, b, (((0,), (0,)), ((), ())), preferred_element_type=F32)


def _rstd(xf):
    return lax.rsqrt(jnp.mean(xf * xf, axis=-1, keepdims=True) + RMS_EPS)


def _sigmoid(x):
    return 1.0 / (1.0 + jnp.exp(-x))


_GATHER_SEQUENCE = ((0, None), (1, 0), (2, 1), (4, None), (5, None), (3, 2), (6, None))


def _gather_order(x, y, c):
    others = [(1 - x, y), (x, 1 - y), (1 - x, 1 - y)]
    arrivals = [_slot(x, y, 1 - c)] + [_slot(*chip, c) for chip in others] + [_slot(*chip, 1 - c) for chip in others]
    return jnp.stack([_slot(x, y, c)] + [arrivals[a] for a, _ in _GATHER_SEQUENCE]).astype(jnp.int32)


def _norm_matmul_gather(order, x, gain_shard, w_shard):
    t = x.shape[0]
    dw, tn = w_shard.shape
    tm = min(t, 2048)
    n_m = t // tm

    def body(order_ref, x_ref, gs_ref, shard_ref, xn_ref, o_ref, full_ref, gain_ref,
             xn_all, wbuf, gland, send_sems, recv_sems, gsend_sems, grecv_sems, load_sems, own_sem):
        n, m = pl.program_id(0), pl.program_id(1)
        x_i, y_i, c_i, chips = _place()
        me, sibling = (x_i, y_i, c_i), (x_i, y_i, 1 - c_i)

        def send(k, block, to, src=None):
            dst = full_ref.at[_slot(*block)]
            return pltpu.make_async_remote_copy(
                src_ref=dst if src is None else src, dst_ref=dst,
                send_sem=send_sems.at[k], recv_sem=recv_sems.at[k], device_id=to, device_id_type=MESH)

        own = pltpu.make_async_copy(shard_ref, full_ref.at[_slot(*me)], own_sem)
        first = [send(0, me, sibling, src=shard_ref)]
        first += [send(1 + j, me, (*chip, c_i), src=shard_ref) for j, chip in enumerate(chips)]
        forwards = [send(4 + j, (*chip, c_i), sibling) for j, chip in enumerate(chips)]
        arrivals = [send(0, sibling, me)] + [send(1 + j, (*chip, c_i), me) for j, chip in enumerate(chips)]
        arrivals += [send(4 + j, (*chip, 1 - c_i), me) for j, chip in enumerate(chips)]
        gains = [pltpu.make_async_remote_copy(
            src_ref=gs_ref, dst_ref=gland.at[_slot(*me)], send_sem=gsend_sems.at[k - 1],
            recv_sem=grecv_sems.at[k - 1], device_id=_peer(x_i, y_i, c_i, k), device_id_type=MESH)
            for k in range(1, N_DEV)]

        @pl.when(jnp.logical_and(n == 0, m == 0))
        def _():
            own.start()
            for cp in gains + first:
                cp.start()
            pltpu.make_async_copy(shard_ref, wbuf.at[0], load_sems.at[0]).start()
            gland[_slot(*me)] = gs_ref[...]
            for k in range(1, N_DEV):
                pltpu.make_async_remote_copy(
                    src_ref=gs_ref, dst_ref=gland.at[_slot(*_peer(x_i, y_i, c_i, k))],
                    send_sem=gsend_sems.at[k - 1], recv_sem=grecv_sems.at[k - 1],
                    device_id=me, device_id_type=MESH).wait_recv()
            for s in range(N_DEV):
                gain_ref[:, 128 * s:128 * (s + 1)] = gland[s]

        rows = pl.ds(pl.multiple_of(m * tm, tm), tm)

        @pl.when(n == 0)
        def _():
            xf = x_ref[...]
            xn = ((xf * _rstd(xf)) * gain_ref[...]).astype(BF16)
            xn_all[rows, :] = xn
            xn_ref[...] = xn

        @pl.when(m == 0)
        def _():
            pltpu.make_async_copy(full_ref.at[0], wbuf.at[n % 2], load_sems.at[n % 2]).wait()

        o_ref[...] = _dot(xn_all[rows, :], wbuf[n % 2]).astype(BF16)

        for k, (arrival, forward) in enumerate(_GATHER_SEQUENCE):
            @pl.when(jnp.logical_and(n == k, m == n_m - 1))
            def _(k=k, arrival=arrival, forward=forward):
                arrivals[arrival].wait_recv()
                if forward is not None:
                    forwards[forward].start()
                pltpu.make_async_copy(full_ref.at[order_ref[k + 1]], wbuf.at[(k + 1) % 2],
                                      load_sems.at[(k + 1) % 2]).start()

        @pl.when(jnp.logical_and(n == N_DEV - 1, m == n_m - 1))
        def _():
            for cp in gains + first + forwards:
                cp.wait_send()
            own.wait()

    held = lambda n, m, order: (jnp.where(n == 0, m, n_m - 1), 0)
    return pl.pallas_call(
        body, name="norm_matmul_gather",
        grid_spec=pltpu.PrefetchScalarGridSpec(
            num_scalar_prefetch=1, grid=(N_DEV, n_m),
            in_specs=[pl.BlockSpec((tm, D_MODEL), held),
                      pl.BlockSpec((1, 128), lambda n, m, order: (0, 0)), ANY],
            out_specs=[pl.BlockSpec((tm, D_MODEL), held),
                       pl.BlockSpec((tm, tn), lambda n, m, order: (m, order[n])),
                       ANY, pl.BlockSpec((1, D_MODEL), lambda n, m, order: (0, 0))],
            scratch_shapes=[pltpu.VMEM((t, D_MODEL), BF16), pltpu.VMEM((2, dw, tn), BF16),
                            pltpu.VMEM((N_DEV, 1, 128), F32),
                            pltpu.SemaphoreType.DMA((7,)), pltpu.SemaphoreType.DMA((7,)),
                            pltpu.SemaphoreType.DMA((7,)), pltpu.SemaphoreType.DMA((7,)),
                            pltpu.SemaphoreType.DMA((2,)), pltpu.SemaphoreType.DMA]),
        out_shape=[SDS((t, D_MODEL), BF16), SDS((t, N_DEV * tn), BF16), SDS((N_DEV, dw, tn), BF16),
                   SDS((1, D_MODEL), F32)],
        compiler_params=_cparams(),
    )(order, x, gain_shard, w_shard)


def _layer_a_out(x, z, w_out, kv_gain, b_gain, kv_w, w_in_b):
    t = x.shape[0]
    tm = min(t, 512)
    nb, _, tn = w_in_b.shape

    def body(x_ref, z_ref, wo_ref, kvg_ref, bg_ref, kvw_ref, wb_ref,
             h1_ref, kvn_ref, hb_ref, kv_ref, qg_ref):
        h1 = x_ref[...] + _dot(z_ref[...], wo_ref[...])
        h1_ref[...] = h1
        y0 = h1 * _rstd(h1)
        kvn = (y0 * kvg_ref[...]).astype(BF16)
        hb = (y0 * bg_ref[...]).astype(BF16)
        kvn_ref[...] = kvn
        hb_ref[...] = hb
        kv_ref[...] = _dot(kvn, kvw_ref[...]).astype(BF16)
        for i in range(nb):
            qg_ref[:, i * tn:(i + 1) * tn] = _dot(hb, wb_ref[i]).astype(BF16)

    row = lambda m: (m, 0)
    fix2 = lambda m: (0, 0)
    return pl.pallas_call(
        body, name="layer_a_out", grid=(t // tm,),
        in_specs=[pl.BlockSpec((tm, D_MODEL), row), pl.BlockSpec((tm, D_MODEL), row),
                  pl.BlockSpec((D_MODEL, D_MODEL), fix2),
                  pl.BlockSpec((1, D_MODEL), fix2), pl.BlockSpec((1, D_MODEL), fix2),
                  pl.BlockSpec((D_MODEL, 256), fix2),
                  pl.BlockSpec((nb, D_MODEL, tn), lambda m: (0, 0, 0))],
        out_specs=[pl.BlockSpec((tm, D_MODEL), row), pl.BlockSpec((tm, D_MODEL), row),
                   pl.BlockSpec((tm, D_MODEL), row), pl.BlockSpec((tm, 256), row),
                   pl.BlockSpec((tm, nb * tn), row)],
        out_shape=[SDS((t, D_MODEL), F32), SDS((t, D_MODEL), BF16), SDS((t, D_MODEL), BF16),
                   SDS((t, 256), BF16), SDS((t, nb * tn), BF16)],
        compiler_params=_cparams(),
    )(x, z, w_out, kv_gain, b_gain, kv_w, w_in_b)


def _layer_b_out_loss(h1, z, w_out, f_gain, target):
    t = h1.shape[0]
    tm = min(t, 1024)

    def body(h1_ref, z_ref, wo_ref, fg_ref, tgt_ref,
             dh2_ref, dh2b_ref, dz_ref, loss_ref, dfn_ref):
        @pl.when(pl.program_id(0) == 0)
        def _():
            loss_ref[...] = jnp.zeros_like(loss_ref)
            dfn_ref[...] = jnp.zeros_like(dfn_ref)

        h2 = h1_ref[...] + _dot(z_ref[...], wo_ref[...])
        r = _rstd(h2)
        yn = h2 * r
        fg = fg_ref[...]
        err = yn * fg - tgt_ref[...]
        loss_ref[...] += (0.5 / D_MODEL) * jnp.sum(err * err)
        dy = err * (1.0 / D_MODEL)
        dfn_ref[...] += jnp.sum(dy * yn, axis=0, keepdims=True)
        u = dy * fg
        dh2 = r * u - h2 * ((r * r * r) * jnp.mean(u * h2, axis=-1, keepdims=True))
        dh2_ref[...] = dh2
        dh2b = dh2.astype(BF16)
        dh2b_ref[...] = dh2b
        dz_ref[...] = _dot_nt(dh2b, wo_ref[...]).astype(BF16)

    row = lambda m: (m, 0)
    fix2 = lambda m: (0, 0)
    return pl.pallas_call(
        body, name="layer_b_out_loss", grid=(t // tm,),
        in_specs=[pl.BlockSpec((tm, D_MODEL), row), pl.BlockSpec((tm, D_MODEL), row),
                  pl.BlockSpec((D_MODEL, D_MODEL), fix2), pl.BlockSpec((1, D_MODEL), fix2),
                  pl.BlockSpec((tm, D_MODEL), row)],
        out_specs=[pl.BlockSpec((tm, D_MODEL), row), pl.BlockSpec((tm, D_MODEL), row),
                   pl.BlockSpec((tm, D_MODEL), row), pl.BlockSpec((1, 128), fix2),
                   pl.BlockSpec((1, D_MODEL), fix2)],
        out_shape=[SDS((t, D_MODEL), F32), SDS((t, D_MODEL), BF16), SDS((t, D_MODEL), BF16),
                   SDS((1, 128), F32), SDS((1, D_MODEL), F32)],
        compiler_params=_cparams(),
    )(h1, z, w_out, f_gain, target)


def _layer_b_in_bwd(dqg, dkv, w_in_b, kv_w, h1, dh2, b_gain, kv_gain, w_out_a):
    t = h1.shape[0]
    tm = min(t, 512)
    nb, _, tn = w_in_b.shape
    per = D_MODEL // tn

    def body(dqg_ref, dkv_ref, wb_ref, kvw_ref, h1_ref, dh2_ref, bg_ref, kvg_ref, wo_ref,
             dh1_ref, dh1b_ref, dz_ref, dbn_ref, dkn_ref):
        @pl.when(pl.program_id(0) == 0)
        def _():
            dbn_ref[...] = jnp.zeros_like(dbn_ref)
            dkn_ref[...] = jnp.zeros_like(dkn_ref)

        dhb = jnp.zeros((tm, D_MODEL), F32)
        for i in range(nb):
            blk = dqg_ref[i // per, :, (i % per) * tn:(i % per + 1) * tn]
            dhb = dhb + _dot_nt(blk, wb_ref[i])
        dkn = (_dot_nt(dkv_ref[0].astype(BF16), kvw_ref[:, 0:128])
               + _dot_nt(dkv_ref[1].astype(BF16), kvw_ref[:, 128:256]))
        h1 = h1_ref[...]
        r = _rstd(h1)
        xr = h1 * r
        dbn_ref[...] += jnp.sum(dhb * xr, axis=0, keepdims=True)
        dkn_ref[...] += jnp.sum(dkn * xr, axis=0, keepdims=True)
        u = dhb * bg_ref[...] + dkn * kvg_ref[...]
        dh1 = dh2_ref[...] + r * u - h1 * ((r * r * r) * jnp.mean(u * h1, axis=-1, keepdims=True))
        dh1_ref[...] = dh1
        dh1b = dh1.astype(BF16)
        dh1b_ref[...] = dh1b
        dz_ref[...] = _dot_nt(dh1b, wo_ref[...]).astype(BF16)

    row = lambda m: (m, 0)
    fix2 = lambda m: (0, 0)
    return pl.pallas_call(
        body, name="layer_b_in_bwd", grid=(t // tm,),
        in_specs=[pl.BlockSpec((2, tm, D_MODEL), lambda m: (0, m, 0)),
                  pl.BlockSpec((2, tm, 128), lambda m: (0, m, 0)),
                  pl.BlockSpec((nb, D_MODEL, tn), lambda m: (0, 0, 0)),
                  pl.BlockSpec((D_MODEL, 256), fix2),
                  pl.BlockSpec((tm, D_MODEL), row), pl.BlockSpec((tm, D_MODEL), row),
                  pl.BlockSpec((1, D_MODEL), fix2), pl.BlockSpec((1, D_MODEL), fix2),
                  pl.BlockSpec((D_MODEL, D_MODEL), fix2)],
        out_specs=[pl.BlockSpec((tm, D_MODEL), row), pl.BlockSpec((tm, D_MODEL), row),
                   pl.BlockSpec((tm, D_MODEL), row), pl.BlockSpec((1, D_MODEL), fix2),
                   pl.BlockSpec((1, D_MODEL), fix2)],
        out_shape=[SDS((t, D_MODEL), F32), SDS((t, D_MODEL), BF16), SDS((t, D_MODEL), BF16),
                   SDS((1, D_MODEL), F32), SDS((1, D_MODEL), F32)],
        compiler_params=_cparams(),
    )(dqg, dkv, w_in_b, kv_w, h1, dh2, b_gain, kv_gain, w_out_a)


def _layer_a_in_bwd(dqg, dkv, w_in_a, x, dh1, a_gain, chip_sums):
    t = x.shape[0]
    tm = min(t, 512)
    nb, _, tn = w_in_a.shape
    per = D_MODEL // tn

    def body(dqg_ref, dkv_ref, w_ref, x_ref, dh1_ref, ag_ref, sums_ref, dx_ref, dan_ref, land_ref,
             send_sems, recv_sems):
        @pl.when(pl.program_id(0) == 0)
        def _():
            dan_ref[...] = jnp.zeros_like(dan_ref)
            for cp in _chip_copies(sums_ref, land_ref, send_sems, recv_sems):
                cp.start()

        dxn = jnp.zeros((tm, D_MODEL), F32)
        for i in range(nb):
            part = i // per
            src = dqg_ref if part in (0, 3) else dkv_ref
            outer = {0: 0, 3: 1, 1: 0, 2: 1}[part]
            blk = src[outer, :, (i % per) * tn:(i % per + 1) * tn]
            dxn = dxn + _dot_nt(blk, w_ref[i])
        xf = x_ref[...]
        r = _rstd(xf)
        dan_ref[...] += jnp.sum(dxn * (xf * r), axis=0, keepdims=True)
        u = dxn * ag_ref[...]
        dx_ref[...] = dh1_ref[...] + r * u - xf * ((r * r * r) * jnp.mean(u * xf, axis=-1, keepdims=True))

        @pl.when(pl.program_id(0) == t // tm - 1)
        def _():
            for cp in _chip_copies(sums_ref, land_ref, send_sems, recv_sems):
                cp.wait()

    row = lambda m: (m, 0)
    fix2 = lambda m: (0, 0)
    return pl.pallas_call(
        body, name="layer_a_in_bwd", grid=(t // tm,),
        in_specs=[pl.BlockSpec((2, tm, D_MODEL), lambda m: (0, m, 0)),
                  pl.BlockSpec((2, tm, D_MODEL), lambda m: (0, m, 0)),
                  pl.BlockSpec((nb, D_MODEL, tn), lambda m: (0, 0, 0)),
                  pl.BlockSpec((tm, D_MODEL), row), pl.BlockSpec((tm, D_MODEL), row),
                  pl.BlockSpec((1, D_MODEL), fix2), ANY],
        out_specs=[pl.BlockSpec((tm, D_MODEL), row), pl.BlockSpec((1, D_MODEL), fix2), ANY],
        out_shape=[SDS((t, D_MODEL), F32), SDS((1, D_MODEL), F32), SDS(chip_sums.shape, chip_sums.dtype)],
        scratch_shapes=[pltpu.SemaphoreType.DMA((3,)), pltpu.SemaphoreType.DMA((3,))],
        compiler_params=_cparams(),
    )(dqg, dkv, w_in_a, x, dh1, a_gain, chip_sums)


def _lut(s, vals):
    r = jnp.int32(vals[0])
    for i in range(1, len(vals)):
        r = jnp.where(s == i, jnp.int32(vals[i]), r)
    return r


def _held(steps, i):
    seq, cur = [None] * len(steps), None
    for k in range(len(steps) - 1, -1, -1):
        if steps[k][0] == i:
            cur = steps[k][1:3]
        seq[k] = cur
    for k in range(len(steps)):
        cur = seq[k] = seq[k] if seq[k] is not None else cur
    return seq


def _weight_grad_cols(name, my_slot, a, bs, steps, tn):
    t, dw = a.shape
    n_arr = len(bs)
    which = [s[0] for s in steps]
    blks = [s[3] for s in steps]

    def body(slot_ref, a_ref, *rest):
        b_refs, (o_ref, own_ref, at_ref) = rest[:n_arr], rest[n_arr:]
        s = pl.program_id(0)

        @pl.when(s == 0)
        def _():
            at_ref[...] = a_ref[...].T

        for i in range(n_arr):
            @pl.when(_lut(s, which) == i)
            def _(i=i):
                res = _dot(at_ref[...], b_refs[i][0])
                o_ref[0] = res.astype(BF16)

                @pl.when(_lut(s, blks) == slot_ref[0])
                def _():
                    own_ref[...] = res

    def b_spec(i):
        held = _held(steps, i)
        return pl.BlockSpec((1, t, tn), lambda s, slot: (_lut(s, [h[0] for h in held]), 0,
                                                         _lut(s, [h[1] for h in held])))

    return pl.pallas_call(
        body, name=name,
        grid_spec=pltpu.PrefetchScalarGridSpec(
            num_scalar_prefetch=1, grid=(len(steps),),
            in_specs=[pl.BlockSpec((t, dw), lambda s, slot: (0, 0))] + [b_spec(i) for i in range(n_arr)],
            out_specs=[pl.BlockSpec((1, dw, tn), lambda s, slot: (_lut(s, blks), 0, 0)),
                       pl.BlockSpec((dw, tn), lambda s, slot: (0, 0))],
            scratch_shapes=[pltpu.VMEM((dw, t), BF16)]),
        out_shape=[SDS((N_DEV, dw, tn), BF16), SDS((dw, tn), F32)],
        compiler_params=_cparams(),
    )(my_slot, a, *bs)


def _weight_grad_rows(name, my_slot, a, b):
    t, dw = a.shape
    n_o, _, c = b.shape
    rows = dw // N_DEV
    tn = min(c, 256)
    per = c // tn

    def body(slot_ref, a_ref, b_ref, o_ref, own_ref, at_ref, res_ref):
        @pl.when(pl.program_id(0) == 0)
        def _():
            at_ref[...] = a_ref[...].T

        res_ref[...] = _dot(at_ref[...], b_ref[0].astype(BF16))
        o_ref[...] = res_ref[...].astype(BF16)
        own_ref[...] = res_ref[pl.ds(pl.multiple_of(slot_ref[0] * rows, rows), rows), :]

    all_rows, own = pl.pallas_call(
        body, name=name,
        grid_spec=pltpu.PrefetchScalarGridSpec(
            num_scalar_prefetch=1, grid=(n_o * per,),
            in_specs=[pl.BlockSpec((t, dw), lambda s, slot: (0, 0)),
                      pl.BlockSpec((1, t, tn), lambda s, slot: (s // per, 0, s % per))],
            out_specs=[pl.BlockSpec((dw, tn), lambda s, slot: (0, s)),
                       pl.BlockSpec((rows, tn), lambda s, slot: (0, s))],
            scratch_shapes=[pltpu.VMEM((dw, t), BF16), pltpu.VMEM((dw, tn), F32)]),
        out_shape=[SDS((dw, n_o * c), BF16), SDS((rows, n_o * c), F32)],
        compiler_params=_cparams(),
    )(my_slot, a, b)
    return all_rows.reshape(N_DEV, rows, n_o * c), own


def _lane_lo():
    return lax.broadcasted_iota(jnp.int32, (1, 128), 1) < HEAD_DIM


def _collapse_chunks(ds, keys):
    if ds.shape[1] < keys:
        ds = jnp.concatenate([jnp.zeros((ds.shape[0], keys - ds.shape[1]), F32), ds], axis=1)
    gc = ds[0:CHUNK]
    for cc in range(1, ds.shape[0] // CHUNK):
        gc = gc + pltpu.roll(ds[cc * CHUNK:(cc + 1) * CHUNK], keys - cc * CHUNK, 1)
    return gc


def _offset_sums(gc):
    hi = gc.astype(BF16)
    lo = (gc - hi.astype(F32)).astype(BF16)
    flip = (lax.broadcasted_iota(jnp.int32, (CHUNK, CHUNK), 0)
            + lax.broadcasted_iota(jnp.int32, (CHUNK, CHUNK), 1) == CHUNK - 1).astype(BF16)
    gf = _dot(flip, hi) + _dot(flip, lo)
    skew = pltpu.roll(gf, 0, 1, stride=1, stride_axis=0)
    return jnp.sum(skew, axis=0, keepdims=True)


def _band_bias(w_row, band, rows):
    keys = w_row.shape[1]
    base = jnp.broadcast_to(w_row, (CHUNK, keys))
    skew = pltpu.roll(base, 0, 1, stride=1, stride_axis=0)
    skew = pltpu.roll(skew, keys - (CHUNK - 1), 1)
    col = lax.broadcasted_iota(jnp.int32, (CHUNK, keys), 1)
    chunk0 = jnp.where(col < band, skew, NEG)
    return jnp.concatenate(
        [chunk0] + [pltpu.roll(chunk0, cc * CHUNK, 1) for cc in range(1, rows // CHUNK)], axis=0)


def _silu_parts(g):
    sg = _sigmoid(g)
    return g * sg, sg * (1.0 + g * (1.0 - sg))


A_PAIRS_FWD = 8
A_PAIRS_BWD = 4


def _a_specs(pairs):
    lanes = 128 * pairs
    steps = D_MODEL // lanes
    q = pl.BlockSpec((QBLK, lanes), lambda p, j: (j, p))
    ks = [pl.BlockSpec((QBLK, lanes), lambda p, j, b=b: (jnp.maximum(j - 2 + b, 0), steps + p)) for b in range(3)]
    vs = [pl.BlockSpec((QBLK, lanes), lambda p, j, b=b: (jnp.maximum(j - 2 + b, 0), 2 * steps + p))
          for b in range(3)]
    g = pl.BlockSpec((QBLK, lanes), lambda p, j: (j, 3 * steps + p))
    bias = pl.BlockSpec((pairs, 8, A_KEYS), lambda p, j: (p, 0, 0))
    return q, ks, vs, g, bias


def _a_fill_bias(w_ref, b_ref, j, pairs):
    _fill_bias(2 * pairs, lambda h: w_ref[h // 2, h % 2:h % 2 + 1, :], A_BAND, b_ref, j)


def _by_valid_key_blocks(j, fn):
    pl.when(j == 0)(functools.partial(fn, 1))
    pl.when(j == 1)(functools.partial(fn, 2))
    pl.when(j >= 2)(functools.partial(fn, 3))


def _fill_bias(n, get_row, band, bias_scr, j):
    @pl.when(j == 0)
    def _():
        for h in range(n):
            bias_scr[h] = _band_bias(get_row(h), band, bias_scr.shape[1])


def _normalise_pair(rs, mxs, lane_lo, extra=None):
    num = jnp.where(lane_lo, rs[0], rs[1])
    den = pltpu.roll(jnp.where(lane_lo, rs[1], rs[0]), HEAD_DIM, 1)
    if extra is not None:
        den = den + jnp.where(lane_lo, extra[0], extra[1])
    return num / den, jnp.where(lane_lo, mxs[0], mxs[1]) + jnp.log(den)


def _own_everywhere(x, sel):
    return jnp.where(sel, x, pltpu.roll(x, HEAD_DIM, 1))


def _minus_rows(s, row_full):
    return jnp.concatenate([s[:, i:i + 128] - row_full for i in range(0, s.shape[1], 128)], axis=1)


def _attn_a_fwd(qkvg, bias, gather):
    t = qkvg.shape[0]
    nq = t // QBLK
    n_g = len(gather)
    pairs = A_PAIRS_FWD
    lanes = 128 * pairs
    steps = D_MODEL // lanes
    q_spec, k_specs, v_specs, g_spec, bias_spec = _a_specs(pairs)

    def body(q_ref, k0, k1, k2, v0, v1, v2, g_ref, w_ref, *rest):
        shard_refs, rest = rest[:n_g], rest[n_g:]
        z_ref, o_ref, lse_ref = rest[:3]
        full_refs, (b_ref, *comm) = rest[3:3 + n_g], rest[3 + n_g:]
        p = pl.program_id(0)
        j = pl.program_id(1)
        start, forward, finish = _gather_phases(shard_refs, full_refs, *comm)
        at = p * nq + j
        pl.when(at == 0)(start)
        pl.when(at == steps * nq // 2)(forward)
        _a_fill_bias(w_ref, b_ref, j, pairs)
        lane_lo = _lane_lo()
        sels = (lane_lo, jnp.logical_not(lane_lo))

        def attend(n_blocks):
            first_col = (3 - n_blocks) * QBLK
            for pp in range(pairs):
                cols = slice(128 * pp, 128 * (pp + 1))
                k = jnp.concatenate([r[:, cols] for r in (k0, k1, k2)[3 - n_blocks:]], axis=0)
                v = jnp.concatenate([r[:, cols] for r in (v0, v1, v2)[3 - n_blocks:]], axis=0)
                q = q_ref[:, cols]
                qm2 = jnp.concatenate([jnp.where(sel, q, jnp.zeros_like(q)) for sel in sels], axis=0) * SCALE
                s2 = _dot_nt(qm2, k)
                rs, mxs = [], []
                for hh, sel in enumerate(sels):
                    s = s2[hh * QBLK:(hh + 1) * QBLK] + b_ref[2 * pp + hh, :, first_col:]
                    mxs.append(jnp.max(s, axis=-1, keepdims=True))
                    e = jnp.exp(s - mxs[hh]).astype(BF16)
                    rs.append(_dot(e, jnp.where(sel, v, jnp.ones_like(v))))
                o, lse = _normalise_pair(rs, mxs, lane_lo)
                silu, _ = _silu_parts(g_ref[:, cols].astype(F32))
                o_ref[:, cols] = o.astype(BF16)
                z_ref[:, cols] = (o * silu).astype(BF16)
                lse_ref[:, cols] = lse

        _by_valid_key_blocks(j, attend)
        pl.when(at == steps * nq - 1)(finish)

    out_spec = pl.BlockSpec((QBLK, lanes), lambda p, j: (j, p))
    outs = pl.pallas_call(
        body, name="attn_a_fwd", grid=(steps, nq),
        in_specs=[q_spec, *k_specs, *v_specs, g_spec, bias_spec] + [ANY] * n_g,
        out_specs=[out_spec, out_spec, out_spec] + [ANY] * n_g,
        out_shape=[SDS((t, D_MODEL), BF16), SDS((t, D_MODEL), BF16), SDS((t, D_MODEL), F32)]
        + [SDS((N_DEV, *s.shape), s.dtype) for s in gather],
        scratch_shapes=[pltpu.VMEM((2 * pairs, QBLK, A_KEYS), F32)] + _gather_scratch(n_g),
        compiler_params=_cparams(),
    )(qkvg, qkvg, qkvg, qkvg, qkvg, qkvg, qkvg, qkvg, bias, *gather)
    return outs[0], outs[1], outs[2], list(outs[3:])


def _attn_a_bwd(qkvg, bias, out_a, lse, dz, scatter):
    t = qkvg.shape[0]
    nq = t // QBLK
    n_sc = len(scatter)
    pairs = A_PAIRS_BWD
    lanes = 128 * pairs
    steps = D_MODEL // lanes
    q_spec, k_specs, v_specs, g_spec, bias_spec = _a_specs(pairs)

    def body(q_ref, k0, k1, k2, v0, v1, v2, g_ref, w_ref, o_ref, lse_ref, dz_ref, *rest):
        sc_refs, rest = rest[:n_sc], rest[n_sc:]
        dqg_ref, dkv_ref, dg_ref = rest[:3]
        land_refs, rest = rest[3:3 + n_sc], rest[3 + n_sc:]
        dk_acc, dv_acc, gt_acc, b_ref, send_sems, recv_sems = rest
        j = pl.program_id(1)
        first = jnp.logical_and(pl.program_id(0) == 0, j == 0)
        last = jnp.logical_and(pl.program_id(0) == steps - 1, j == nq - 1)

        @pl.when(first)
        def _():
            for cp in _scatter_copies(sc_refs, land_refs, send_sems, recv_sems):
                cp.start()

        _a_fill_bias(w_ref, b_ref, j, pairs)

        @pl.when(j == 0)
        def _():
            dk_acc[...] = jnp.zeros_like(dk_acc)
            dv_acc[...] = jnp.zeros_like(dv_acc)
            gt_acc[...] = jnp.zeros_like(gt_acc)

        lane_lo = _lane_lo()
        sels = (lane_lo, jnp.logical_not(lane_lo))

        def attend(n_blocks):
            first_col = (3 - n_blocks) * QBLK
            for pp in range(pairs):
                cols = slice(128 * pp, 128 * (pp + 1))
                q = q_ref[:, cols]
                k = jnp.concatenate([r[:, cols] for r in (k0, k1, k2)[3 - n_blocks:]], axis=0)
                v = jnp.concatenate([r[:, cols] for r in (v0, v1, v2)[3 - n_blocks:]], axis=0)
                o = o_ref[:, cols].astype(F32)
                lse_pair = lse_ref[:, cols]
                dzf = dz_ref[:, cols].astype(F32)
                silu, dsilu = _silu_parts(g_ref[:, cols].astype(F32))
                do = dzf * silu
                dqg_ref[1, :, cols] = (dzf * o * dsilu).astype(BF16)
                doo = do * o
                qm2 = jnp.concatenate([jnp.where(sel, q, jnp.zeros_like(q)) for sel in sels], axis=0) * SCALE
                dom2 = jnp.concatenate([jnp.where(sel, do, 0.0) for sel in sels], axis=0).astype(BF16)
                s2 = _dot_nt(qm2, k)
                dp2 = _dot_nt(dom2, v)
                ps, dss = [], []
                for hh, sel in enumerate(sels):
                    rows = slice(hh * QBLK, (hh + 1) * QBLK)
                    s = s2[rows] + b_ref[2 * pp + hh, :, first_col:]
                    p = jnp.exp(_minus_rows(s, _own_everywhere(lse_pair, sel)))
                    delta = jnp.sum(jnp.where(sel, doo, 0.0), axis=-1, keepdims=True)
                    ds = p * (dp2[rows] - delta)
                    gt_acc[2 * pp + hh] += _collapse_chunks(ds, A_KEYS)
                    ps.append(p.astype(BF16))
                    dss.append(ds.astype(BF16))
                dsb2 = jnp.concatenate(dss, axis=0)
                dq2 = _dot(dsb2, k) * SCALE
                dk_blk = _dot_tn(dsb2, qm2)
                dv_blk = _dot_tn(jnp.concatenate(ps, axis=0), dom2)
                dqg_ref[0, :, cols] = jnp.where(lane_lo, dq2[0:QBLK], dq2[QBLK:2 * QBLK]).astype(BF16)
                for b in range(n_blocks):
                    rows = pl.ds(pl.multiple_of((j - n_blocks + 1 + b) * QBLK, QBLK), QBLK)
                    dk_acc[rows, cols] += dk_blk[b * QBLK:(b + 1) * QBLK]
                    dv_acc[rows, cols] += dv_blk[b * QBLK:(b + 1) * QBLK]

        _by_valid_key_blocks(j, attend)

        @pl.when(j == nq - 1)
        def _():
            dkv_ref[0] = dk_acc[...].astype(BF16)
            dkv_ref[1] = dv_acc[...].astype(BF16)
            for pp in range(pairs):
                dg_ref[pp] = jnp.concatenate([_offset_sums(gt_acc[2 * pp]), _offset_sums(gt_acc[2 * pp + 1]),
                                              jnp.zeros((6, A_DIAG), F32)], axis=0)

        @pl.when(last)
        def _():
            for cp in _scatter_copies(sc_refs, land_refs, send_sems, recv_sems):
                cp.wait()

    blk = pl.BlockSpec((QBLK, lanes), lambda p, j: (j, p))
    outs = pl.pallas_call(
        body, name="attn_a_bwd", grid=(steps, nq),
        in_specs=[q_spec, *k_specs, *v_specs, g_spec, bias_spec, blk, blk, blk] + [ANY] * n_sc,
        out_specs=[pl.BlockSpec((2, QBLK, lanes), lambda p, j: (0, j, p)),
                   pl.BlockSpec((2, t, lanes), lambda p, j: (0, 0, p)),
                   pl.BlockSpec((pairs, 8, A_DIAG), lambda p, j: (p, 0, 0))] + [ANY] * n_sc,
        out_shape=[SDS((2, t, D_MODEL), BF16), SDS((2, t, D_MODEL), BF16), SDS((N_HEADS // 2, 8, A_DIAG), F32)]
        + [SDS((N_DEV - 1, *g.shape[1:]), g.dtype) for g in scatter],
        scratch_shapes=[pltpu.VMEM((t, lanes), F32), pltpu.VMEM((t, lanes), F32),
                        pltpu.VMEM((2 * pairs, CHUNK, A_KEYS), F32), pltpu.VMEM((2 * pairs, QBLK, A_KEYS), F32),
                        pltpu.SemaphoreType.DMA(((N_DEV - 1) * n_sc,)),
                        pltpu.SemaphoreType.DMA(((N_DEV - 1) * n_sc,))],
        compiler_params=_cparams(),
    )(qkvg, qkvg, qkvg, qkvg, qkvg, qkvg, qkvg, qkvg, bias, out_a, lse, dz, *scatter)
    return outs[0], outs[1], outs[2], list(outs[3:])


def _b_specs(qblk):
    per = qblk // B_PREV
    q = pl.BlockSpec((qblk, 512), lambda h, j: (j, h))
    g = pl.BlockSpec((qblk, 512), lambda h, j: (j, 2 + h))
    kp = pl.BlockSpec((B_PREV, 128), lambda h, j: (jnp.maximum(per * j - 1, 0), 0))
    kc = pl.BlockSpec((qblk, 128), lambda h, j: (j, 0))
    vp = pl.BlockSpec((B_PREV, 128), lambda h, j: (jnp.maximum(per * j - 1, 0), 1))
    vc = pl.BlockSpec((qblk, 128), lambda h, j: (j, 1))
    bias = pl.BlockSpec((B_GROUP, qblk + B_PREV), lambda h, j: (h, 0))
    sinks = pl.BlockSpec(memory_space=pltpu.SMEM)
    return q, g, kp, kc, vp, vc, bias, sinks


def _b_operands(kp, kc, vp, vc, kvh, with_prev):
    k = jnp.concatenate([kp[...], kc[...]], axis=0) if with_prev else kc[...]
    v = jnp.concatenate([vp[...], vc[...]], axis=0) if with_prev else vc[...]
    kr = pltpu.roll(k, HEAD_DIM, 1)
    vr = pltpu.roll(v, HEAD_DIM, 1)
    first = kvh == 0
    return (jnp.where(first, k, kr), jnp.where(first, kr, k),
            jnp.where(first, v, vr), jnp.where(first, vr, v))


def _attn_b_fwd(qg, kv, bias, sinks):
    t = qg.shape[0]
    qblk = B_QBLK_FWD
    per_step = 4
    step = per_step * qblk
    q_spec, g_spec, kp_spec, kc_spec, vp_spec, vc_spec, _, sink_spec = _b_specs(step)
    bias_spec = pl.BlockSpec((B_GROUP, qblk + B_PREV), lambda h, j: (h, 0))

    def body(q_ref, g_ref, kp, kc, vp, vc, w_ref, sink_ref, z_ref, o_ref, lse_ref, b_ref):
        kvh = pl.program_id(0)
        j = pl.program_id(1)
        _fill_bias(B_GROUP, lambda h: w_ref[h:h + 1, :], B_BAND, b_ref, j)
        lane_lo = _lane_lo()
        n_pairs = B_GROUP // 2

        def attend(first):
            k_lo, k_hi, v_lo, v_hi = _b_operands(kp, kc, vp, vc, kvh, True)
            for sb in range(per_step):
                no_prev = first and sb == 0
                first_col = B_PREV if no_prev else 0
                keys = slice(sb * qblk + first_col, (sb + 1) * qblk + B_PREV)
                qrows = slice(sb * qblk, (sb + 1) * qblk)
                halves = []
                for hh, sel in enumerate((lane_lo, jnp.logical_not(lane_lo))):
                    kk = (k_lo if hh == 0 else k_hi)[keys]
                    vv = (v_lo if hh == 0 else v_hi)[keys]
                    qm4 = jnp.concatenate(
                        [jnp.where(sel, q_ref[qrows, 128 * pp:128 * (pp + 1)], jnp.zeros((qblk, 128), BF16))
                         for pp in range(n_pairs)], axis=0) * SCALE
                    s4 = _dot_nt(qm4, kk)
                    es, mxs = [], []
                    for pp in range(n_pairs):
                        g = 2 * pp + hh
                        s = s4[pp * qblk:(pp + 1) * qblk] + b_ref[g, :, first_col:]
                        mxs.append(jnp.maximum(jnp.max(s, axis=-1, keepdims=True), sink_ref[kvh * B_GROUP + g]))
                        es.append(jnp.exp(s - mxs[pp]).astype(BF16))
                    r4 = _dot(jnp.concatenate(es, axis=0), jnp.where(sel, vv, jnp.ones_like(vv)))
                    halves.append((r4, mxs))
                for pp in range(n_pairs):
                    cols = slice(128 * pp, 128 * (pp + 1))
                    rows = slice(pp * qblk, (pp + 1) * qblk)
                    mxs = [halves[hh][1][pp] for hh in range(2)]
                    sink_terms = [jnp.exp(sink_ref[kvh * B_GROUP + 2 * pp + hh] - mxs[hh]) for hh in range(2)]
                    o, lse = _normalise_pair([halves[hh][0][rows] for hh in range(2)], mxs, lane_lo, sink_terms)
                    silu, _ = _silu_parts(g_ref[qrows, cols].astype(F32))
                    o_ref[qrows, cols] = o.astype(BF16)
                    z_ref[qrows, cols] = (o * silu).astype(BF16)
                    lse_ref[qrows, cols] = lse

        pl.when(j == 0)(functools.partial(attend, True))
        pl.when(j >= 1)(functools.partial(attend, False))

    out_spec = pl.BlockSpec((step, 512), lambda h, j: (j, h))
    return pl.pallas_call(
        body, name="attn_b_fwd", grid=(B_KV_HEADS, t // step),
        in_specs=[q_spec, g_spec, kp_spec, kc_spec, vp_spec, vc_spec, bias_spec, sink_spec],
        out_specs=[out_spec, out_spec, out_spec],
        out_shape=[SDS((t, D_MODEL), BF16), SDS((t, D_MODEL), BF16), SDS((t, D_MODEL), F32)],
        scratch_shapes=[pltpu.VMEM((B_GROUP, qblk, qblk + B_PREV), F32)],
        compiler_params=_cparams(),
    )(qg, qg, kv, kv, kv, kv, bias, sinks)


def _attn_b_bwd(qg, kv, bias, sinks, out_b, lse, dz, bucket_onehot):
    t = qg.shape[0]
    qblk = B_QBLK_BWD
    keys = qblk + B_PREV
    nq = t // qblk
    q_spec, g_spec, kp_spec, kc_spec, vp_spec, vc_spec, bias_spec, sink_spec = _b_specs(qblk)

    def body(q_ref, g_ref, kp, kc, vp, vc, w_ref, sink_ref, o_ref, lse_ref, dz_ref, oh_ref,
             dqg_ref, dkv_ref, dt5_ref, dsink_ref, gt_acc, b_ref):
        kvh = pl.program_id(0)
        j = pl.program_id(1)
        _fill_bias(B_GROUP, lambda h: w_ref[h:h + 1, :], B_BAND, b_ref, j)

        @pl.when(jnp.logical_and(kvh == 0, j == 0))
        def _():
            dkv_ref[...] = jnp.zeros_like(dkv_ref)

        @pl.when(j == 0)
        def _():
            gt_acc[...] = jnp.zeros_like(gt_acc)
            dsink_ref[...] = jnp.zeros_like(dsink_ref)

        lane_lo = _lane_lo()

        def attend(with_prev):
            first_col = 0 if with_prev else B_PREV
            k_lo, k_hi, v_lo, v_hi = _b_operands(kp, kc, vp, vc, kvh, with_prev)
            dk_blk = jnp.zeros((keys - first_col, 128), F32)
            dv_blk = jnp.zeros((keys - first_col, 128), F32)
            for pp in range(B_GROUP // 2):
                cols = slice(128 * pp, 128 * (pp + 1))
                qp = q_ref[:, cols]
                o = o_ref[:, cols].astype(F32)
                lse_pair = lse_ref[:, cols]
                dzf = dz_ref[:, cols].astype(F32)
                silu, dsilu = _silu_parts(g_ref[:, cols].astype(F32))
                do = dzf * silu
                dqg_ref[1, :, cols] = (dzf * o * dsilu).astype(BF16)
                doo = do * o
                dqs = []
                for hh in range(2):
                    g = 2 * pp + hh
                    sel = lane_lo if hh == 0 else jnp.logical_not(lane_lo)
                    sink = sink_ref[kvh * B_GROUP + g]
                    kk = k_lo if hh == 0 else k_hi
                    vv = v_lo if hh == 0 else v_hi
                    qm = jnp.where(sel, qp, jnp.zeros_like(qp)) * SCALE
                    s = _dot_nt(qm, kk) + b_ref[g, :, first_col:]
                    lse_h = _own_everywhere(lse_pair, sel)
                    p = jnp.exp(_minus_rows(s, lse_h))
                    delta = jnp.sum(jnp.where(sel, doo, 0.0), axis=-1, keepdims=True)
                    dom = jnp.where(sel, do, 0.0).astype(BF16)
                    dp = _dot_nt(dom, vv)
                    ds = p * (dp - delta)
                    gt_acc[g, :, first_col:] += ds
                    dsink_ref[g:g + 1, :] -= jnp.sum(jnp.exp(sink - lse_h) * delta, axis=0, keepdims=True)
                    dsb = ds.astype(BF16)
                    dqs.append(_dot(dsb, kk) * SCALE)
                    dk_blk = dk_blk + _dot_tn(dsb, qm)
                    dv_blk = dv_blk + _dot_tn(p.astype(BF16), dom)
                dqg_ref[0, :, cols] = jnp.where(lane_lo, dqs[0], dqs[1]).astype(BF16)
            mine = lane_lo == (kvh == 0)
            dk_add = jnp.where(mine, dk_blk + pltpu.roll(dk_blk, HEAD_DIM, 1), 0.0)
            dv_add = jnp.where(mine, dv_blk + pltpu.roll(dv_blk, HEAD_DIM, 1), 0.0)
            first_key = B_PREV if with_prev else 0
            if with_prev:
                rows = pl.ds(pl.multiple_of(j * qblk - B_PREV, B_PREV), B_PREV)
                dkv_ref[0, rows, :] += dk_add[0:B_PREV]
                dkv_ref[1, rows, :] += dv_add[0:B_PREV]
            rows = pl.ds(pl.multiple_of(j * qblk, qblk), qblk)
            dkv_ref[0, rows, :] += dk_add[first_key:first_key + qblk]
            dkv_ref[1, rows, :] += dv_add[first_key:first_key + qblk]

        pl.when(j == 0)(functools.partial(attend, False))
        pl.when(j >= 1)(functools.partial(attend, True))

        @pl.when(j == nq - 1)
        def _():
            dd = jnp.concatenate([_offset_sums(_collapse_chunks(gt_acc[g], keys)) for g in range(B_GROUP)], axis=0)
            hi = dd.astype(BF16)
            lo = (dd - hi.astype(F32)).astype(BF16)
            dt5_ref[...] = _dot(hi, oh_ref[...]) + _dot(lo, oh_ref[...])

    blk = pl.BlockSpec((qblk, 512), lambda h, j: (j, h))
    return pl.pallas_call(
        body, name="attn_b_bwd", grid=(B_KV_HEADS, nq),
        in_specs=[q_spec, g_spec, kp_spec, kc_spec, vp_spec, vc_spec, bias_spec, sink_spec, blk, blk, blk,
                  pl.BlockSpec((keys, 128), lambda h, j: (0, 0))],
        out_specs=[pl.BlockSpec((2, qblk, 512), lambda h, j: (0, j, h)),
                   pl.BlockSpec((2, t, 128), lambda h, j: (0, 0, 0)),
                   pl.BlockSpec((B_GROUP, 128), lambda h, j: (h, 0)),
                   pl.BlockSpec((B_GROUP, 128), lambda h, j: (h, 0))],
        out_shape=[SDS((2, t, D_MODEL), BF16), SDS((2, t, 128), F32),
                   SDS((N_HEADS, 128), F32), SDS((N_HEADS, 128), F32)],
        scratch_shapes=[pltpu.VMEM((B_GROUP, qblk, keys), F32), pltpu.VMEM((B_GROUP, qblk, keys), F32)],
        compiler_params=_cparams(),
    )(qg, qg, kv, kv, kv, kv, bias, sinks, out_b, lse, dz, bucket_onehot)


def _a_bias_by_offset(rel_bias):
    m = np.arange(A_DIAG)
    idx = np.clip(A_BAND - 1 - m, -A_REL_CLIP, A_REL_CLIP) + A_REL_CLIP
    by_head = rel_bias[idx].T.reshape(N_HEADS // 2, 2, A_DIAG)
    return jnp.concatenate([by_head, jnp.zeros((N_HEADS // 2, 6, A_DIAG), F32)], axis=1)


def _a_bias_grad(offset_sums):
    first = 319
    tail = jnp.sum(offset_sums[:, :first], axis=1)
    body = jnp.flip(offset_sums[:, first:first + 320], axis=1)
    body = body.at[:, -1].add(tail)
    full = jnp.concatenate([jnp.zeros((N_HEADS, 193), F32), body], axis=1)
    return full


def _t5_bucket(rel):
    nb = T5_BUCKETS // 2
    max_exact = nb // 2
    ret = jnp.where(rel > 0, nb, 0)
    n = jnp.abs(rel)
    nf = jnp.maximum(n, 1).astype(jnp.float32)
    large = max_exact + (jnp.log(nf / max_exact) / math.log(T5_MAX_DIST / max_exact)
                         * (nb - max_exact)).astype(jnp.int32)
    large = jnp.minimum(large, nb - 1)
    return ret + jnp.where(n < max_exact, n, large)


def _b_offset_buckets(keys):
    return _t5_bucket(jnp.arange(keys, dtype=jnp.int32) - (B_LEFT_CHUNKS * CHUNK + CHUNK - 1))


def _b_bias_by_offset(t5_table, keys):
    return t5_table[_b_offset_buckets(keys)].T


def _b_bucket_onehot(keys):
    return (_b_offset_buckets(keys)[:, None] == jnp.arange(128)[None, :]).astype(BF16)


def _local_step(my_slot, order, x, target, a_gain_shard, w_in_a_shard, rel_bias, late_shards, kv_gain,
                t5_table, b_gain, sinks, f_gain):
    a_bias = _a_bias_by_offset(rel_bias)
    b_bias_fwd = _b_bias_by_offset(t5_table, B_QBLK_FWD + B_PREV)
    b_bias_bwd = _b_bias_by_offset(t5_table, B_QBLK_BWD + B_PREV)
    sinks_flat = sinks.reshape(N_HEADS)

    xn, qkvg, w_in_a, a_gain = _norm_matmul_gather(order, x, a_gain_shard, w_in_a_shard)
    z_a, out_a, lse_a, (w_in_b, w_out_a, w_out_b, kv_w) = _attn_a_fwd(qkvg, a_bias, late_shards)
    w_out_a = w_out_a.reshape(D_MODEL, D_MODEL)
    w_out_b = w_out_b.reshape(D_MODEL, D_MODEL)
    kv_w = kv_w.reshape(D_MODEL, 2 * 128)
    h1, kvn, hb, kv, qg = _layer_a_out(x, z_a, w_out_a, kv_gain, b_gain, kv_w, w_in_b)
    z_b, out_b, lse_b = _attn_b_fwd(qg, kv, b_bias_fwd, sinks_flat)
    dh2, dh2b, dz_b, loss, d_fn = _layer_b_out_loss(h1, z_b, w_out_b, f_gain, target)

    dqg_b, dkv_b, d_t5, d_sink = _attn_b_bwd(qg, kv, b_bias_bwd, sinks_flat, out_b, lse_b, dz_b,
                                             _b_bucket_onehot(B_QBLK_BWD + B_PREV))
    dh1, dh1b, dz_a, d_bn, d_kn = _layer_b_in_bwd(dqg_b, dkv_b, w_in_b, kv_w, h1, dh2, b_gain, kv_gain, w_out_a)
    early = dict(
        b_w_out=_weight_grad_rows("grad_b_w_out", my_slot, z_b, dh2b[None]),
        b_w_in=_weight_grad_cols("grad_b_w_in", my_slot, hb, [dqg_b],
                                 [(0, o, c, 4 * o + c) for o in range(2) for c in range(4)], 256),
        kv_w=_weight_grad_rows("grad_kv_w", my_slot, kvn, dkv_b),
        a_w_out=_weight_grad_rows("grad_a_w_out", my_slot, z_a, dh1b[None]))
    dqg_a, dkv_a, d_rel, landed = _attn_a_bwd(qkvg, a_bias, out_a, lse_a, dz_a, [g[0] for g in early.values()])
    g_w_in_a = _weight_grad_cols(
        "grad_a_w_in", my_slot, xn, [dqg_a, dkv_a],
        [(0, 0, 0, 0), (0, 0, 1, 1), (1, 0, 0, 2), (1, 0, 1, 3), (1, 1, 0, 4), (1, 1, 1, 5), (0, 1, 0, 6), (0, 1, 1, 7)], 512)
    chip_sums, from_sibling = _chip_sums(g_w_in_a[0])
    grad_x, d_an, from_chips = _layer_a_in_bwd(dqg_a, dkv_a, w_in_a, x, dh1, a_gain, chip_sums)

    matrices = {n: (g[1], [(land, 0, N_DEV - 1)]) for (n, g), land in zip(early.items(), landed)}
    matrices["a_w_in"] = (g_w_in_a[1], [(from_sibling, 0, 1), (from_chips, 0, 3)])
    small = dict(
        loss=loss, a_norm=d_an, a_rel_bias=d_rel[:, :2].reshape(N_HEADS, A_DIAG),
        kv_norm=d_kn, t5_bias=d_t5, b_norm=d_bn, b_sinks=d_sink, final_norm=d_fn)
    return grad_x, small, matrices


def _place():
    x, y, c = lax.axis_index("x"), lax.axis_index("y"), lax.axis_index("c")
    chips = [(1 - x, y), (x, 1 - y), (1 - x, 1 - y)]
    return x, y, c, chips


def _slot(px, py, pc):
    return 4 * px + 2 * py + pc


ANY = pl.BlockSpec(memory_space=pl.ANY)


def _peer(x, y, c, k):
    return (x ^ (k >> 2), y ^ ((k >> 1) & 1), c ^ (k & 1))


def _scatter_copies(grad_refs, land_refs, send_sems, recv_sems):
    x, y, c, _ = _place()
    copies = []
    for t, (grad, land) in enumerate(zip(grad_refs, land_refs)):
        for k in range(1, N_DEV):
            peer = _peer(x, y, c, k)
            sem = (N_DEV - 1) * t + k - 1
            copies.append(pltpu.make_async_remote_copy(
                src_ref=grad.at[_slot(*peer)], dst_ref=land.at[k - 1],
                send_sem=send_sems.at[sem], recv_sem=recv_sems.at[sem],
                device_id=peer, device_id_type=MESH))
    return copies


def _gather_phases(ins, outs, send_sems, recv_sems, local_sems):
    n = len(ins)
    x, y, c, chips = _place()
    me, sibling = (x, y, c), (x, y, 1 - c)

    def copy(t, k, block, to, src=None):
        dst = outs[t].at[_slot(*block)]
        return pltpu.make_async_remote_copy(
            src_ref=dst if src is None else src, dst_ref=dst,
            send_sem=send_sems.at[7 * t + k], recv_sem=recv_sems.at[7 * t + k],
            device_id=to, device_id_type=MESH)

    def lists():
        mine = [pltpu.make_async_copy(ins[t], outs[t].at[_slot(*me)], local_sems.at[t]) for t in range(n)]
        first = []
        for t in range(n):
            first.append(copy(t, 0, me, sibling, src=ins[t]))
            first += [copy(t, 1 + j, me, (*chip, c), src=ins[t]) for j, chip in enumerate(chips)]
        passed = [copy(t, 4 + j, (*chip, c), sibling) for t in range(n) for j, chip in enumerate(chips)]
        return mine, first, passed

    def start():
        mine, first, _ = lists()
        for cp in mine + first:
            cp.start()

    def forward():
        _, _, passed = lists()
        for t in range(n):
            for j, chip in enumerate(chips):
                copy(t, 1 + j, (*chip, c), me).wait_recv()
                passed[3 * t + j].start()

    def finish():
        mine, first, passed = lists()
        for t in range(n):
            copy(t, 0, sibling, me).wait_recv()
            for j, chip in enumerate(chips):
                copy(t, 4 + j, (*chip, 1 - c), me).wait_recv()
        for cp in first + passed:
            cp.wait_send()
        for cp in mine:
            cp.wait()

    return start, forward, finish


def _gather_scratch(n):
    return [pltpu.SemaphoreType.DMA((7 * n,)), pltpu.SemaphoreType.DMA((7 * n,)), pltpu.SemaphoreType.DMA((n,))]


def _chip_sums(g):
    _, r, c = g.shape

    def body(g_ref, sums_ref, mine_ref, land, own, send_sems, recv_sems, load_sems):
        x, y, c_i, chips = _place()
        sibling = (x, y, 1 - c_i)
        blocks = [(*chip, 1 - c_i) for chip in chips] + [sibling]
        sends = [pltpu.make_async_remote_copy(
            src_ref=g_ref.at[_slot(*block)], dst_ref=land.at[k], send_sem=send_sems.at[k],
            recv_sem=recv_sems.at[k], device_id=sibling, device_id_type=MESH) for k, block in enumerate(blocks)]
        loads = [pltpu.make_async_copy(g_ref.at[_slot(*chip, c_i)], own.at[j], load_sems.at[j])
                 for j, chip in enumerate(chips)]
        for cp in sends + loads:
            cp.start()
        for cp in sends + loads:
            cp.wait()
        for j in range(3):
            sums_ref[j] = (own[j].astype(F32) + land[j].astype(F32)).astype(BF16)
        mine_ref[0] = land[3]

    return pl.pallas_call(
        body, name="chip_sums",
        in_specs=[ANY], out_specs=[VM, VM],
        out_shape=[SDS((3, r, c), BF16), SDS((1, r, c), BF16)],
        scratch_shapes=[pltpu.VMEM((4, r, c), BF16), pltpu.VMEM((3, r, c), BF16),
                        pltpu.SemaphoreType.DMA((4,)), pltpu.SemaphoreType.DMA((4,)), pltpu.SemaphoreType.DMA((3,))],
        compiler_params=_cparams(),
    )(g)


def _chip_copies(sums_ref, land_ref, send_sems, recv_sems):
    x, y, c, chips = _place()
    del x, y
    return [pltpu.make_async_remote_copy(
        src_ref=sums_ref.at[j], dst_ref=land_ref.at[j], send_sem=send_sems.at[j], recv_sem=recv_sems.at[j],
        device_id=(*chip, c), device_id_type=MESH) for j, chip in enumerate(chips)]


def _row_tile(rows):
    return min(rows, 256)


def _adamw(w, g, m, v):
    m2 = ADAM_B1 * m + (1.0 - ADAM_B1) * g
    v2 = ADAM_B2 * v + (1.0 - ADAM_B2) * jnp.square(g)
    m_hat = m2 / (1.0 - ADAM_B1 ** ADAM_STEP)
    v_hat = v2 / (1.0 - ADAM_B2 ** ADAM_STEP)
    delta = -ADAM_LR * (m_hat / (jnp.sqrt(v_hat) + ADAM_EPS) + ADAM_WD * w)
    return delta, m2, v2


def _reduce_adamw(name, own, partials, w, m, v):
    r, c = own.shape
    tr = _row_tile(r)
    n_p = len(partials)

    def body(own_ref, *rest):
        p_refs, (w_ref, m_ref, v_ref, grad_ref, d_ref, nm_ref, nv_ref) = rest[:n_p], rest[n_p:]
        grad = own_ref[...]
        for p_ref, (_, _, count) in zip(p_refs, partials):
            for j in range(count):
                grad = grad + p_ref[j].astype(F32)
        grad_ref[...] = grad
        d_ref[...], nm_ref[...], nv_ref[...] = _adamw(w_ref[...], grad, m_ref[...], v_ref[...])

    flat = pl.BlockSpec((tr, c), lambda i: (i, 0))
    return pl.pallas_call(
        body, name=name, grid=(r // tr,),
        in_specs=[flat] + [pl.BlockSpec((count, tr, c), lambda i, first=first, count=count: (first // count, i, 0))
                           for _, first, count in partials] + [flat, flat, flat],
        out_specs=[flat, flat, flat, flat],
        out_shape=[SDS((r, c), F32)] * 4,
        compiler_params=_cparams(),
    )(own, *[p[0] for p in partials], w, m, v)


VM = pl.BlockSpec()


def _small_allreduce(parts):
    n = len(parts)

    def body(*refs):
        ins, outs, lands = refs[:n], refs[n:2 * n], refs[2 * n:3 * n]
        send_sems, recv_sems = refs[3 * n:]
        x, y, c, _ = _place()
        my_slot = _slot(x, y, c)
        copies = []
        for t in range(n):
            lands[t][my_slot] = ins[t][...]
            for k in range(1, N_DEV):
                sem = (N_DEV - 1) * t + k - 1
                copies.append(pltpu.make_async_remote_copy(
                    src_ref=ins[t], dst_ref=lands[t].at[my_slot],
                    send_sem=send_sems.at[sem], recv_sem=recv_sems.at[sem],
                    device_id=_peer(x, y, c, k), device_id_type=MESH))
        for cp in copies:
            cp.start()
        for t in range(n):
            for k in range(1, N_DEV):
                sem = (N_DEV - 1) * t + k - 1
                pltpu.make_async_remote_copy(
                    src_ref=ins[t], dst_ref=lands[t].at[_slot(*_peer(x, y, c, k))],
                    send_sem=send_sems.at[sem], recv_sem=recv_sems.at[sem],
                    device_id=(x, y, c), device_id_type=MESH).wait_recv()
        for cp in copies:
            cp.wait_send()
        for t in range(n):
            total = lands[t][0]
            for s in range(1, N_DEV):
                total = total + lands[t][s]
            outs[t][...] = total

    n_sems = (N_DEV - 1) * n
    return pl.pallas_call(
        body, name="small_allreduce",
        in_specs=[VM] * n, out_specs=[VM] * n, out_shape=[SDS(p.shape, F32) for p in parts],
        scratch_shapes=[pltpu.VMEM((N_DEV, *p.shape), F32) for p in parts]
        + [pltpu.SemaphoreType.DMA((n_sems,)), pltpu.SemaphoreType.DMA((n_sems,))],
    )(*parts)


def _small_adamw(my_slot, sums, ws, ms, vs):
    n = len(ws)

    def body(slot_ref, *refs):
        sum_refs, refs = refs[:n + 1], refs[n + 1:]
        w_refs, m_refs, v_refs, refs = refs[:n], refs[n:2 * n], refs[2 * n:3 * n], refs[3 * n:]
        g_refs, d_refs, nm_refs, nv_refs = refs[:n + 1], refs[n + 1:2 * n + 1], refs[2 * n + 1:3 * n + 1], refs[3 * n + 1:]
        for t in range(n + 1):
            if t == 0:
                g = sum_refs[0][:, pl.ds(pl.multiple_of(slot_ref[0] * 128, 128), 128)]
            else:
                g = sum_refs[t][...]
            g_refs[t][...] = g
            if t < n:
                d_refs[t][...], nm_refs[t][...], nv_refs[t][...] = _adamw(w_refs[t][...], g, m_refs[t][...], v_refs[t][...])

    shapes = [SDS(w.shape, F32) for w in ws]
    outs = pl.pallas_call(
        body, name="small_adamw",
        in_specs=[pl.BlockSpec(memory_space=pltpu.SMEM)] + [VM] * (4 * n + 1),
        out_specs=[VM] * (4 * n + 1),
        out_shape=shapes + [SDS(sums[-1].shape, F32)] + shapes * 3,
    )(my_slot, *sums, *ws, *ms, *vs)
    return outs[:n + 1], outs[n + 1:2 * n + 1], outs[2 * n + 1:3 * n + 1], outs[3 * n + 1:]


def kernel(x, a_norm, a_w_in, a_rel_bias, a_w_out, kv_norm, kv_w, t5_bias, b_norm, b_w_in, b_sinks, b_w_out, final_norm, loss_target, m_a_norm, m_a_w_in, m_a_rel_bias, m_a_w_out, m_kv_norm, m_kv_w, m_t5_bias, m_b_norm, m_b_w_in, m_b_sinks, m_b_w_out, m_final_norm, v_a_norm, v_a_w_in, v_a_rel_bias, v_a_w_out, v_kv_norm, v_kv_w, v_t5_bias, v_b_norm, v_b_w_in, v_b_sinks, v_b_w_out, v_final_norm):
    xi, yi, ci = lax.axis_index("x"), lax.axis_index("y"), lax.axis_index("c")
    my_slot = _slot(xi, yi, ci)

    slot_arr = jnp.reshape(my_slot, (1,)).astype(jnp.int32)
    order = _gather_order(xi, yi, ci)
    late_shards = [b_w_in[0].astype(BF16), a_w_out[0].astype(BF16), b_w_out[0].astype(BF16), kv_w.astype(BF16)]
    grad_x, loc, matrices = _local_step(
        slot_arr, order, x[0], loss_target[0], a_norm, a_w_in[0].astype(BF16), a_rel_bias[0], late_shards,
        kv_norm.reshape(1, D_MODEL), t5_bias, b_norm, b_sinks, final_norm.reshape(1, D_MODEL))

    shard_w = dict(a_w_in=a_w_in[0], b_w_in=b_w_in[0], a_w_out=a_w_out[0], b_w_out=b_w_out[0], kv_w=kv_w)
    shard_m = dict(a_w_in=m_a_w_in[0], b_w_in=m_b_w_in[0], a_w_out=m_a_w_out[0], b_w_out=m_b_w_out[0], kv_w=m_kv_w)
    shard_v = dict(a_w_in=v_a_w_in[0], b_w_in=v_b_w_in[0], a_w_out=v_a_w_out[0], b_w_out=v_b_w_out[0], kv_w=v_kv_w)
    big = {n: _reduce_adamw("adamw_" + n, own, partials, shard_w[n], shard_m[n], shard_v[n])
           for n, (own, partials) in matrices.items()}

    names = ("a_norm", "a_rel_bias", "kv_norm", "t5_bias", "b_norm", "b_sinks", "final_norm")
    tables = ("a_rel_bias", "t5_bias")

    def row(n, a):
        return a.reshape(-1, a.shape[-1]).T if n in tables else a.reshape(1, -1)

    small_w = [row(n, a) for n, a in zip(names, (a_norm, a_rel_bias, kv_norm, t5_bias, b_norm, b_sinks, final_norm))]
    small_m = [row(n, a) for n, a in zip(names, (m_a_norm, m_a_rel_bias, m_kv_norm, m_t5_bias, m_b_norm, m_b_sinks,
                                                 m_final_norm))]
    small_v = [row(n, a) for n, a in zip(names, (v_a_norm, v_a_rel_bias, v_kv_norm, v_t5_bias, v_b_norm, v_b_sinks,
                                                 v_final_norm))]
    sums = dict(zip(names + ("loss",), _small_allreduce([loc[n] for n in names] + [loc["loss"]])))
    sums["a_rel_bias"] = _a_bias_grad(sums["a_rel_bias"])
    sums["t5_bias"] = sums["t5_bias"][:, :T5_BUCKETS]
    sums["b_sinks"] = sums["b_sinks"][:, 0].reshape(1, N_HEADS)
    results = _small_adamw(slot_arr, [sums[n] for n in names + ("loss",)], small_w, small_m, small_v)
    like = dict(a_norm=a_norm, a_rel_bias=a_rel_bias, kv_norm=kv_norm, t5_bias=t5_bias, b_norm=b_norm,
                b_sinks=b_sinks, final_norm=final_norm)
    sm = [{n: (part[i].T if n in tables else part[i]).reshape(like[n].shape) for i, n in enumerate(names)}
          for part in results]
    loss = results[0][len(names)][0, 0]

    order = ("a_norm", "a_w_in", "a_rel_bias", "a_w_out", "kv_norm", "kv_w", "t5_bias", "b_norm",
             "b_w_in", "b_sinks", "b_w_out", "final_norm")
    lead = dict(a_w_in=True, b_w_in=True, a_w_out=True, b_w_out=True, kv_w=False)

    def pick(kind, name):
        if name in big:
            val = big[name][kind]
            return val[None] if lead[name] else val
        return sm[kind][name]

    outs = [loss, grad_x[None]]
    for kind in range(4):
        outs += [pick(kind, n) for n in order]
    return tuple(outs)
```

```python
import functools
import math

import numpy as np
import jax
import jax.numpy as jnp
from jax import lax
from jax.experimental import pallas as pl
from jax.experimental.pallas import tpu as pltpu

F32 = jnp.float32
BF16 = jnp.bfloat16
SDS = jax.ShapeDtypeStruct

D_MODEL = 1024
HEAD_DIM = 64
CHUNK = 64
N_HEADS = 16
RMS_EPS = 1e-6
A_LEFT_CHUNKS = 8
A_BAND = (A_LEFT_CHUNKS + 1) * CHUNK
A_REL_CLIP = 256
B_KV_HEADS = 2
B_GROUP = 8
B_LEFT_CHUNKS = 2
B_BAND = (B_LEFT_CHUNKS + 1) * CHUNK
T5_BUCKETS = 32
T5_MAX_DIST = 128
QBLK = 256
A_KEYS = 3 * QBLK
B_QBLK_FWD = 128
B_QBLK_BWD = 256
B_PREV = 128
A_DIAG = A_KEYS
NEG = -1e30
SCALE = HEAD_DIM ** -0.5
N_DEV = 8

ADAM_LR = 0.001
ADAM_B1 = 0.9
ADAM_B2 = 0.999
ADAM_EPS = 1e-08
ADAM_WD = 0.01
ADAM_STEP = 10

VMEM_LIMIT_BYTES = 56 * 1024 * 1024
MESH = pl.DeviceIdType.MESH


def _cparams():
    return pltpu.CompilerParams(vmem_limit_bytes=VMEM_LIMIT_BYTES)


def _dot(a, b):
    return jnp.dot(a, b, preferred_element_type=F32)


def _dot_nt(a, b):
    return lax.dot_general(a, b, (((1,), (1,)), ((), ())), preferred_element_type=F32)


def _dot_tn(a, b):
    return lax.dot_general(a, b, (((0,), (0,)), ((), ())), preferred_element_type=F32)


def _rstd(xf):
    return lax.rsqrt(jnp.mean(xf * xf, axis=-1, keepdims=True) + RMS_EPS)


def _sigmoid(x):
    return 1.0 / (1.0 + jnp.exp(-x))


_GATHER_SEQUENCE = ((0, None), (1, 0), (2, 1), (4, None), (5, None), (3, 2), (6, None))


def _gather_order(x, y, c):
    others = [(1 - x, y), (x, 1 - y), (1 - x, 1 - y)]
    arrivals = [_slot(x, y, 1 - c)] + [_slot(*chip, c) for chip in others] + [_slot(*chip, 1 - c) for chip in others]
    return jnp.stack([_slot(x, y, c)] + [arrivals[a] for a, _ in _GATHER_SEQUENCE]).astype(jnp.int32)


def _norm_matmul_gather(order, x, gain_shard, w_shard):
    t = x.shape[0]
    dw, tn = w_shard.shape
    tm = min(t, 2048)
    n_m = t // tm

    def body(order_ref, x_ref, gs_ref, shard_ref, xn_ref, o_ref, full_ref, gain_ref,
             xn_all, wbuf, gland, send_sems, recv_sems, gsend_sems, grecv_sems, load_sems, own_sem):
        n, m = pl.program_id(0), pl.program_id(1)
        x_i, y_i, c_i, chips = _place()
        me, sibling = (x_i, y_i, c_i), (x_i, y_i, 1 - c_i)

        def send(k, block, to, src=None):
            dst = full_ref.at[_slot(*block)]
            return pltpu.make_async_remote_copy(
                src_ref=dst if src is None else src, dst_ref=dst,
                send_sem=send_sems.at[k], recv_sem=recv_sems.at[k], device_id=to, device_id_type=MESH)

        own = pltpu.make_async_copy(shard_ref, full_ref.at[_slot(*me)], own_sem)
        first = [send(0, me, sibling, src=shard_ref)]
        first += [send(1 + j, me, (*chip, c_i), src=shard_ref) for j, chip in enumerate(chips)]
        forwards = [send(4 + j, (*chip, c_i), sibling) for j, chip in enumerate(chips)]
        arrivals = [send(0, sibling, me)] + [send(1 + j, (*chip, c_i), me) for j, chip in enumerate(chips)]
        arrivals += [send(4 + j, (*chip, 1 - c_i), me) for j, chip in enumerate(chips)]
        gains = [pltpu.make_async_remote_copy(
            src_ref=gs_ref, dst_ref=gland.at[_slot(*me)], send_sem=gsend_sems.at[k - 1],
            recv_sem=grecv_sems.at[k - 1], device_id=_peer(x_i, y_i, c_i, k), device_id_type=MESH)
            for k in range(1, N_DEV)]

        @pl.when(jnp.logical_and(n == 0, m == 0))
        def _():
            own.start()
            for cp in gains + first:
                cp.start()
            pltpu.make_async_copy(shard_ref, wbuf.at[0], load_sems.at[0]).start()
            gland[_slot(*me)] = gs_ref[...]
            for k in range(1, N_DEV):
                pltpu.make_async_remote_copy(
                    src_ref=gs_ref, dst_ref=gland.at[_slot(*_peer(x_i, y_i, c_i, k))],
                    send_sem=gsend_sems.at[k - 1], recv_sem=grecv_sems.at[k - 1],
                    device_id=me, device_id_type=MESH).wait_recv()
            for s in range(N_DEV):
                gain_ref[:, 128 * s:128 * (s + 1)] = gland[s]

        rows = pl.ds(pl.multiple_of(m * tm, tm), tm)

        @pl.when(n == 0)
        def _():
            xf = x_ref[...]
            xn = ((xf * _rstd(xf)) * gain_ref[...]).astype(BF16)
            xn_all[rows, :] = xn
            xn_ref[...] = xn

        @pl.when(m == 0)
        def _():
            pltpu.make_async_copy(full_ref.at[0], wbuf.at[n % 2], load_sems.at[n % 2]).wait()

        o_ref[...] = _dot(xn_all[rows, :], wbuf[n % 2]).astype(BF16)

        for k, (arrival, forward) in enumerate(_GATHER_SEQUENCE):
            @pl.when(jnp.logical_and(n == k, m == n_m - 1))
            def _(k=k, arrival=arrival, forward=forward):
                arrivals[arrival].wait_recv()
                if forward is not None:
                    forwards[forward].start()
                pltpu.make_async_copy(full_ref.at[order_ref[k + 1]], wbuf.at[(k + 1) % 2],
                                      load_sems.at[(k + 1) % 2]).start()

        @pl.when(jnp.logical_and(n == N_DEV - 1, m == n_m - 1))
        def _():
            for cp in gains + first + forwards:
                cp.wait_send()
            own.wait()

    held = lambda n, m, order: (jnp.where(n == 0, m, n_m - 1), 0)
    return pl.pallas_call(
        body, name="norm_matmul_gather",
        grid_spec=pltpu.PrefetchScalarGridSpec(
            num_scalar_prefetch=1, grid=(N_DEV, n_m),
            in_specs=[pl.BlockSpec((tm, D_MODEL), held),
                      pl.BlockSpec((1, 128), lambda n, m, order: (0, 0)), ANY],
            out_specs=[pl.BlockSpec((tm, D_MODEL), held),
                       pl.BlockSpec((tm, tn), lambda n, m, order: (m, order[n])),
                       ANY, pl.BlockSpec((1, D_MODEL), lambda n, m, order: (0, 0))],
            scratch_shapes=[pltpu.VMEM((t, D_MODEL), BF16), pltpu.VMEM((2, dw, tn), BF16),
                            pltpu.VMEM((N_DEV, 1, 128), F32),
                            pltpu.SemaphoreType.DMA((7,)), pltpu.SemaphoreType.DMA((7,)),
                            pltpu.SemaphoreType.DMA((7,)), pltpu.SemaphoreType.DMA((7,)),
                            pltpu.SemaphoreType.DMA((2,)), pltpu.SemaphoreType.DMA]),
        out_shape=[SDS((t, D_MODEL), BF16), SDS((t, N_DEV * tn), BF16), SDS((N_DEV, dw, tn), BF16),
                   SDS((1, D_MODEL), F32)],
        compiler_params=_cparams(),
    )(order, x, gain_shard, w_shard)


def _layer_a_out(x, z, w_out, kv_gain, b_gain, kv_w, w_in_b):
    t = x.shape[0]
    tm = min(t, 512)
    nb, _, tn = w_in_b.shape

    def body(x_ref, z_ref, wo_ref, kvg_ref, bg_ref, kvw_ref, wb_ref,
             h1_ref, kvn_ref, hb_ref, kv_ref, qg_ref):
        h1 = x_ref[...] + _dot(z_ref[...], wo_ref[...])
        h1_ref[...] = h1
        y0 = h1 * _rstd(h1)
        kvn = (y0 * kvg_ref[...]).astype(BF16)
        hb = (y0 * bg_ref[...]).astype(BF16)
        kvn_ref[...] = kvn
        hb_ref[...] = hb
        kv_ref[...] = _dot(kvn, kvw_ref[...]).astype(BF16)
        for i in range(nb):
            qg_ref[:, i * tn:(i + 1) * tn] = _dot(hb, wb_ref[i]).astype(BF16)

    row = lambda m: (m, 0)
    fix2 = lambda m: (0, 0)
    return pl.pallas_call(
        body, name="layer_a_out", grid=(t // tm,),
        in_specs=[pl.BlockSpec((tm, D_MODEL), row), pl.BlockSpec((tm, D_MODEL), row),
                  pl.BlockSpec((D_MODEL, D_MODEL), fix2),
                  pl.BlockSpec((1, D_MODEL), fix2), pl.BlockSpec((1, D_MODEL), fix2),
                  pl.BlockSpec((D_MODEL, 256), fix2),
                  pl.BlockSpec((nb, D_MODEL, tn), lambda m: (0, 0, 0))],
        out_specs=[pl.BlockSpec((tm, D_MODEL), row), pl.BlockSpec((tm, D_MODEL), row),
                   pl.BlockSpec((tm, D_MODEL), row), pl.BlockSpec((tm, 256), row),
                   pl.BlockSpec((tm, nb * tn), row)],
        out_shape=[SDS((t, D_MODEL), F32), SDS((t, D_MODEL), BF16), SDS((t, D_MODEL), BF16),
                   SDS((t, 256), BF16), SDS((t, nb * tn), BF16)],
        compiler_params=_cparams(),
    )(x, z, w_out, kv_gain, b_gain, kv_w, w_in_b)


def _layer_b_out_loss(h1, z, w_out, f_gain, target):
    t = h1.shape[0]
    tm = min(t, 1024)

    def body(h1_ref, z_ref, wo_ref, fg_ref, tgt_ref,
             dh2_ref, dh2b_ref, dz_ref, loss_ref, dfn_ref):
        @pl.when(pl.program_id(0) == 0)
        def _():
            loss_ref[...] = jnp.zeros_like(loss_ref)
            dfn_ref[...] = jnp.zeros_like(dfn_ref)

        h2 = h1_ref[...] + _dot(z_ref[...], wo_ref[...])
        r = _rstd(h2)
        yn = h2 * r
        fg = fg_ref[...]
        err = yn * fg - tgt_ref[...]
        loss_ref[...] += (0.5 / D_MODEL) * jnp.sum(err * err)
        dy = err * (1.0 / D_MODEL)
        dfn_ref[...] += jnp.sum(dy * yn, axis=0, keepdims=True)
        u = dy * fg
        dh2 = r * u - h2 * ((r * r * r) * jnp.mean(u * h2, axis=-1, keepdims=True))
        dh2_ref[...] = dh2
        dh2b = dh2.astype(BF16)
        dh2b_ref[...] = dh2b
        dz_ref[...] = _dot_nt(dh2b, wo_ref[...]).astype(BF16)

    row = lambda m: (m, 0)
    fix2 = lambda m: (0, 0)
    return pl.pallas_call(
        body, name="layer_b_out_loss", grid=(t // tm,),
        in_specs=[pl.BlockSpec((tm, D_MODEL), row), pl.BlockSpec((tm, D_MODEL), row),
                  pl.BlockSpec((D_MODEL, D_MODEL), fix2), pl.BlockSpec((1, D_MODEL), fix2),
                  pl.BlockSpec((tm, D_MODEL), row)],
        out_specs=[pl.BlockSpec((tm, D_MODEL), row), pl.BlockSpec((tm, D_MODEL), row),
                   pl.BlockSpec((tm, D_MODEL), row), pl.BlockSpec((1, 128), fix2),
                   pl.BlockSpec((1, D_MODEL), fix2)],
        out_shape=[SDS((t, D_MODEL), F32), SDS((t, D_MODEL), BF16), SDS((t, D_MODEL), BF16),
                   SDS((1, 128), F32), SDS((1, D_MODEL), F32)],
        compiler_params=_cparams(),
    )(h1, z, w_out, f_gain, target)


def _layer_b_in_bwd(dqg, dkv, w_in_b, kv_w, h1, dh2, b_gain, kv_gain, w_out_a):
    t = h1.shape[0]
    tm = min(t, 512)
    nb, _, tn = w_in_b.shape
    per = D_MODEL // tn

    def body(dqg_ref, dkv_ref, wb_ref, kvw_ref, h1_ref, dh2_ref, bg_ref, kvg_ref, wo_ref,
             dh1_ref, dh1b_ref, dz_ref, dbn_ref, dkn_ref):
        @pl.when(pl.program_id(0) == 0)
        def _():
            dbn_ref[...] = jnp.zeros_like(dbn_ref)
            dkn_ref[...] = jnp.zeros_like(dkn_ref)

        dhb = jnp.zeros((tm, D_MODEL), F32)
        for i in range(nb):
            blk = dqg_ref[i // per, :, (i % per) * tn:(i % per + 1) * tn]
            dhb = dhb + _dot_nt(blk, wb_ref[i])
        dkn = (_dot_nt(dkv_ref[0].astype(BF16), kvw_ref[:, 0:128])
               + _dot_nt(dkv_ref[1].astype(BF16), kvw_ref[:, 128:256]))
        h1 = h1_ref[...]
        r = _rstd(h1)
        xr = h1 * r
        dbn_ref[...] += jnp.sum(dhb * xr, axis=0, keepdims=True)
        dkn_ref[...] += jnp.sum(dkn * xr, axis=0, keepdims=True)
        u = dhb * bg_ref[...] + dkn * kvg_ref[...]
        dh1 = dh2_ref[...] + r * u - h1 * ((r * r * r) * jnp.mean(u * h1, axis=-1, keepdims=True))
        dh1_ref[...] = dh1
        dh1b = dh1.astype(BF16)
        dh1b_ref[...] = dh1b
        dz_ref[...] = _dot_nt(dh1b, wo_ref[...]).astype(BF16)

    row = lambda m: (m, 0)
    fix2 = lambda m: (0, 0)
    return pl.pallas_call(
        body, name="layer_b_in_bwd", grid=(t // tm,),
        in_specs=[pl.BlockSpec((2, tm, D_MODEL), lambda m: (0, m, 0)),
                  pl.BlockSpec((2, tm, 128), lambda m: (0, m, 0)),
                  pl.BlockSpec((nb, D_MODEL, tn), lambda m: (0, 0, 0)),
                  pl.BlockSpec((D_MODEL, 256), fix2),
                  pl.BlockSpec((tm, D_MODEL), row), pl.BlockSpec((tm, D_MODEL), row),
                  pl.BlockSpec((1, D_MODEL), fix2), pl.BlockSpec((1, D_MODEL), fix2),
                  pl.BlockSpec((D_MODEL, D_MODEL), fix2)],
        out_specs=[pl.BlockSpec((tm, D_MODEL), row), pl.BlockSpec((tm, D_MODEL), row),
                   pl.BlockSpec((tm, D_MODEL), row), pl.BlockSpec((1, D_MODEL), fix2),
                   pl.BlockSpec((1, D_MODEL), fix2)],
        out_shape=[SDS((t, D_MODEL), F32), SDS((t, D_MODEL), BF16), SDS((t, D_MODEL), BF16),
                   SDS((1, D_MODEL), F32), SDS((1, D_MODEL), F32)],
        compiler_params=_cparams(),
    )(dqg, dkv, w_in_b, kv_w, h1, dh2, b_gain, kv_gain, w_out_a)


def _layer_a_in_bwd(dqg, dkv, w_in_a, x, dh1, a_gain, chip_sums):
    t = x.shape[0]
    tm = min(t, 512)
    nb, _, tn = w_in_a.shape
    per = D_MODEL // tn

    def body(dqg_ref, dkv_ref, w_ref, x_ref, dh1_ref, ag_ref, sums_ref, dx_ref, dan_ref, land_ref,
             send_sems, recv_sems):
        @pl.when(pl.program_id(0) == 0)
        def _():
            dan_ref[...] = jnp.zeros_like(dan_ref)
            for cp in _chip_copies(sums_ref, land_ref, send_sems, recv_sems):
                cp.start()

        dxn = jnp.zeros((tm, D_MODEL), F32)
        for i in range(nb):
            part = i // per
            src = dqg_ref if part in (0, 3) else dkv_ref
            outer = {0: 0, 3: 1, 1: 0, 2: 1}[part]
            blk = src[outer, :, (i % per) * tn:(i % per + 1) * tn]
            dxn = dxn + _dot_nt(blk, w_ref[i])
        xf = x_ref[...]
        r = _rstd(xf)
        dan_ref[...] += jnp.sum(dxn * (xf * r), axis=0, keepdims=True)
        u = dxn * ag_ref[...]
        dx_ref[...] = dh1_ref[...] + r * u - xf * ((r * r * r) * jnp.mean(u * xf, axis=-1, keepdims=True))

        @pl.when(pl.program_id(0) == t // tm - 1)
        def _():
            for cp in _chip_copies(sums_ref, land_ref, send_sems, recv_sems):
                cp.wait()

    row = lambda m: (m, 0)
    fix2 = lambda m: (0, 0)
    return pl.pallas_call(
        body, name="layer_a_in_bwd", grid=(t // tm,),
        in_specs=[pl.BlockSpec((2, tm, D_MODEL), lambda m: (0, m, 0)),
                  pl.BlockSpec((2, tm, D_MODEL), lambda m: (0, m, 0)),
                  pl.BlockSpec((nb, D_MODEL, tn), lambda m: (0, 0, 0)),
                  pl.BlockSpec((tm, D_MODEL), row), pl.BlockSpec((tm, D_MODEL), row),
                  pl.BlockSpec((1, D_MODEL), fix2), ANY],
        out_specs=[pl.BlockSpec((tm, D_MODEL), row), pl.BlockSpec((1, D_MODEL), fix2), ANY],
        out_shape=[SDS((t, D_MODEL), F32), SDS((1, D_MODEL), F32), SDS(chip_sums.shape, chip_sums.dtype)],
        scratch_shapes=[pltpu.SemaphoreType.DMA((3,)), pltpu.SemaphoreType.DMA((3,))],
        compiler_params=_cparams(),
    )(dqg, dkv, w_in_a, x, dh1, a_gain, chip_sums)


def _lut(s, vals):
    r = jnp.int32(vals[0])
    for i in range(1, len(vals)):
        r = jnp.where(s == i, jnp.int32(vals[i]), r)
    return r


def _held(steps, i):
    seq, cur = [None] * len(steps), None
    for k in range(len(steps) - 1, -1, -1):
        if steps[k][0] == i:
            cur = steps[k][1:3]
        seq[k] = cur
    for k in range(len(steps)):
        cur = seq[k] = seq[k] if seq[k] is not None else cur
    return seq


def _weight_grad_cols(name, my_slot, a, bs, steps, tn):
    t, dw = a.shape
    n_arr = len(bs)
    which = [s[0] for s in steps]
    blks = [s[3] for s in steps]

    def body(slot_ref, a_ref, *rest):
        b_refs, (o_ref, own_ref, at_ref) = rest[:n_arr], rest[n_arr:]
        s = pl.program_id(0)

        @pl.when(s == 0)
        def _():
            at_ref[...] = a_ref[...].T

        for i in range(n_arr):
            @pl.when(_lut(s, which) == i)
            def _(i=i):
                res = _dot(at_ref[...], b_refs[i][0])
                o_ref[0] = res.astype(BF16)

                @pl.when(_lut(s, blks) == slot_ref[0])
                def _():
                    own_ref[...] = res

    def b_spec(i):
        held = _held(steps, i)
        return pl.BlockSpec((1, t, tn), lambda s, slot: (_lut(s, [h[0] for h in held]), 0,
                                                         _lut(s, [h[1] for h in held])))

    return pl.pallas_call(
        body, name=name,
        grid_spec=pltpu.PrefetchScalarGridSpec(
            num_scalar_prefetch=1, grid=(len(steps),),
            in_specs=[pl.BlockSpec((t, dw), lambda s, slot: (0, 0))] + [b_spec(i) for i in range(n_arr)],
            out_specs=[pl.BlockSpec((1, dw, tn), lambda s, slot: (_lut(s, blks), 0, 0)),
                       pl.BlockSpec((dw, tn), lambda s, slot: (0, 0))],
            scratch_shapes=[pltpu.VMEM((dw, t), BF16)]),
        out_shape=[SDS((N_DEV, dw, tn), BF16), SDS((dw, tn), F32)],
        compiler_params=_cparams(),
    )(my_slot, a, *bs)


def _weight_grad_rows(name, my_slot, a, b):
    t, dw = a.shape
    n_o, _, c = b.shape
    rows = dw // N_DEV
    tn = min(c, 256)
    per = c // tn

    def body(slot_ref, a_ref, b_ref, o_ref, own_ref, at_ref, res_ref):
        @pl.when(pl.program_id(0) == 0)
        def _():
            at_ref[...] = a_ref[...].T

        res_ref[...] = _dot(at_ref[...], b_ref[0].astype(BF16))
        o_ref[...] = res_ref[...].astype(BF16)
        own_ref[...] = res_ref[pl.ds(pl.multiple_of(slot_ref[0] * rows, rows), rows), :]

    all_rows, own = pl.pallas_call(
        body, name=name,
        grid_spec=pltpu.PrefetchScalarGridSpec(
            num_scalar_prefetch=1, grid=(n_o * per,),
            in_specs=[pl.BlockSpec((t, dw), lambda s, slot: (0, 0)),
                      pl.BlockSpec((1, t, tn), lambda s, slot: (s // per, 0, s % per))],
            out_specs=[pl.BlockSpec((dw, tn), lambda s, slot: (0, s)),
                       pl.BlockSpec((rows, tn), lambda s, slot: (0, s))],
            scratch_shapes=[pltpu.VMEM((dw, t), BF16), pltpu.VMEM((dw, tn), F32)]),
        out_shape=[SDS((dw, n_o * c), BF16), SDS((rows, n_o * c), F32)],
        compiler_params=_cparams(),
    )(my_slot, a, b)
    return all_rows.reshape(N_DEV, rows, n_o * c), own


def _lane_lo():
    return lax.broadcasted_iota(jnp.int32, (1, 128), 1) < HEAD_DIM


def _collapse_chunks(ds, keys):
    if ds.shape[1] < keys:
        ds = jnp.concatenate([jnp.zeros((ds.shape[0], keys - ds.shape[1]), F32), ds], axis=1)
    gc = ds[0:CHUNK]
    for cc in range(1, ds.shape[0] // CHUNK):
        gc = gc + pltpu.roll(ds[cc * CHUNK:(cc + 1) * CHUNK], keys - cc * CHUNK, 1)
    return gc


def _offset_sums(gc):
    hi = gc.astype(BF16)
    lo = (gc - hi.astype(F32)).astype(BF16)
    flip = (lax.broadcasted_iota(jnp.int32, (CHUNK, CHUNK), 0)
            + lax.broadcasted_iota(jnp.int32, (CHUNK, CHUNK), 1) == CHUNK - 1).astype(BF16)
    gf = _dot(flip, hi) + _dot(flip, lo)
    skew = pltpu.roll(gf, 0, 1, stride=1, stride_axis=0)
    return jnp.sum(skew, axis=0, keepdims=True)


def _band_bias(w_row, band, rows):
    keys = w_row.shape[1]
    base = jnp.broadcast_to(w_row, (CHUNK, keys))
    skew = pltpu.roll(base, 0, 1, stride=1, stride_axis=0)
    skew = pltpu.roll(skew, keys - (CHUNK - 1), 1)
    col = lax.broadcasted_iota(jnp.int32, (CHUNK, keys), 1)
    chunk0 = jnp.where(col < band, skew, NEG)
    return jnp.concatenate(
        [chunk0] + [pltpu.roll(chunk0, cc * CHUNK, 1) for cc in range(1, rows // CHUNK)], axis=0)


def _silu_parts(g):
    sg = _sigmoid(g)
    return g * sg, sg * (1.0 + g * (1.0 - sg))


A_PAIRS_FWD = 8
A_PAIRS_BWD = 4


def _a_specs(pairs):
    lanes = 128 * pairs
    steps = D_MODEL // lanes
    q = pl.BlockSpec((QBLK, lanes), lambda p, j: (j, p))
    ks = [pl.BlockSpec((QBLK, lanes), lambda p, j, b=b: (jnp.maximum(j - 2 + b, 0), steps + p)) for b in range(3)]
    vs = [pl.BlockSpec((QBLK, lanes), lambda p, j, b=b: (jnp.maximum(j - 2 + b, 0), 2 * steps + p))
          for b in range(3)]
    g = pl.BlockSpec((QBLK, lanes), lambda p, j: (j, 3 * steps + p))
    bias = pl.BlockSpec((pairs, 8, A_KEYS), lambda p, j: (p, 0, 0))
    return q, ks, vs, g, bias


def _a_fill_bias(w_ref, b_ref, j, pairs):
    _fill_bias(2 * pairs, lambda h: w_ref[h // 2, h % 2:h % 2 + 1, :], A_BAND, b_ref, j)


def _by_valid_key_blocks(j, fn):
    pl.when(j == 0)(functools.partial(fn, 1))
    pl.when(j == 1)(functools.partial(fn, 2))
    pl.when(j >= 2)(functools.partial(fn, 3))


def _fill_bias(n, get_row, band, bias_scr, j):
    @pl.when(j == 0)
    def _():
        for h in range(n):
            bias_scr[h] = _band_bias(get_row(h), band, bias_scr.shape[1])


def _normalise_pair(rs, mxs, lane_lo, extra=None):
    num = jnp.where(lane_lo, rs[0], rs[1])
    den = pltpu.roll(jnp.where(lane_lo, rs[1], rs[0]), HEAD_DIM, 1)
    if extra is not None:
        den = den + jnp.where(lane_lo, extra[0], extra[1])
    return num / den, jnp.where(lane_lo, mxs[0], mxs[1]) + jnp.log(den)


def _own_everywhere(x, sel):
    return jnp.where(sel, x, pltpu.roll(x, HEAD_DIM, 1))


def _minus_rows(s, row_full):
    return jnp.concatenate([s[:, i:i + 128] - row_full for i in range(0, s.shape[1], 128)], axis=1)


def _attn_a_fwd(qkvg, bias, gather):
    t = qkvg.shape[0]
    nq = t // QBLK
    n_g = len(gather)
    pairs = A_PAIRS_FWD
    lanes = 128 * pairs
    steps = D_MODEL // lanes
    q_spec, k_specs, v_specs, g_spec, bias_spec = _a_specs(pairs)

    def body(q_ref, k0, k1, k2, v0, v1, v2, g_ref, w_ref, *rest):
        shard_refs, rest = rest[:n_g], rest[n_g:]
        z_ref, o_ref, lse_ref = rest[:3]
        full_refs, (b_ref, *comm) = rest[3:3 + n_g], rest[3 + n_g:]
        p = pl.program_id(0)
        j = pl.program_id(1)
        start, forward, finish = _gather_phases(shard_refs, full_refs, *comm)
        at = p * nq + j
        pl.when(at == 0)(start)
        pl.when(at == steps * nq // 2)(forward)
        _a_fill_bias(w_ref, b_ref, j, pairs)
        lane_lo = _lane_lo()
        sels = (lane_lo, jnp.logical_not(lane_lo))

        def attend(n_blocks):
            first_col = (3 - n_blocks) * QBLK
            for pp in range(pairs):
                cols = slice(128 * pp, 128 * (pp + 1))
                k = jnp.concatenate([r[:, cols] for r in (k0, k1, k2)[3 - n_blocks:]], axis=0)
                v = jnp.concatenate([r[:, cols] for r in (v0, v1, v2)[3 - n_blocks:]], axis=0)
                q = q_ref[:, cols]
                qm2 = jnp.concatenate([jnp.where(sel, q, jnp.zeros_like(q)) for sel in sels], axis=0) * SCALE
                s2 = _dot_nt(qm2, k)
                rs, mxs = [], []
                for hh, sel in enumerate(sels):
                    s = s2[hh * QBLK:(hh + 1) * QBLK] + b_ref[2 * pp + hh, :, first_col:]
                    mxs.append(jnp.max(s, axis=-1, keepdims=True))
                    e = jnp.exp(s - mxs[hh]).astype(BF16)
                    rs.append(_dot(e, jnp.where(sel, v, jnp.ones_like(v))))
                o, lse = _normalise_pair(rs, mxs, lane_lo)
                silu, _ = _silu_parts(g_ref[:, cols].astype(F32))
                o_ref[:, cols] = o.astype(BF16)
                z_ref[:, cols] = (o * silu).astype(BF16)
                lse_ref[:, cols] = lse

        _by_valid_key_blocks(j, attend)
        pl.when(at == steps * nq - 1)(finish)

    out_spec = pl.BlockSpec((QBLK, lanes), lambda p, j: (j, p))
    outs = pl.pallas_call(
        body, name="attn_a_fwd", grid=(steps, nq),
        in_specs=[q_spec, *k_specs, *v_specs, g_spec, bias_spec] + [ANY] * n_g,
        out_specs=[out_spec, out_spec, out_spec] + [ANY] * n_g,
        out_shape=[SDS((t, D_MODEL), BF16), SDS((t, D_MODEL), BF16), SDS((t, D_MODEL), F32)]
        + [SDS((N_DEV, *s.shape), s.dtype) for s in gather],
        scratch_shapes=[pltpu.VMEM((2 * pairs, QBLK, A_KEYS), F32)] + _gather_scratch(n_g),
        compiler_params=_cparams(),
    )(qkvg, qkvg, qkvg, qkvg, qkvg, qkvg, qkvg, qkvg, bias, *gather)
    return outs[0], outs[1], outs[2], list(outs[3:])


def _attn_a_bwd(qkvg, bias, out_a, lse, dz, scatter):
    t = qkvg.shape[0]
    nq = t // QBLK
    n_sc = len(scatter)
    pairs = A_PAIRS_BWD
    lanes = 128 * pairs
    steps = D_MODEL // lanes
    q_spec, k_specs, v_specs, g_spec, bias_spec = _a_specs(pairs)

    def body(q_ref, k0, k1, k2, v0, v1, v2, g_ref, w_ref, o_ref, lse_ref, dz_ref, *rest):
        sc_refs, rest = rest[:n_sc], rest[n_sc:]
        dqg_ref, dkv_ref, dg_ref = rest[:3]
        land_refs, rest = rest[3:3 + n_sc], rest[3 + n_sc:]
        dk_acc, dv_acc, gt_acc, b_ref, send_sems, recv_sems = rest
        j = pl.program_id(1)
        first = jnp.logical_and(pl.program_id(0) == 0, j == 0)
        last = jnp.logical_and(pl.program_id(0) == steps - 1, j == nq - 1)

        @pl.when(first)
        def _():
            for cp in _scatter_copies(sc_refs, land_refs, send_sems, recv_sems):
                cp.start()

        _a_fill_bias(w_ref, b_ref, j, pairs)

        @pl.when(j == 0)
        def _():
            dk_acc[...] = jnp.zeros_like(dk_acc)
            dv_acc[...] = jnp.zeros_like(dv_acc)
            gt_acc[...] = jnp.zeros_like(gt_acc)

        lane_lo = _lane_lo()
        sels = (lane_lo, jnp.logical_not(lane_lo))

        def attend(n_blocks):
            first_col = (3 - n_blocks) * QBLK
            for pp in range(pairs):
                cols = slice(128 * pp, 128 * (pp + 1))
                q = q_ref[:, cols]
                k = jnp.concatenate([r[:, cols] for r in (k0, k1, k2)[3 - n_blocks:]], axis=0)
                v = jnp.concatenate([r[:, cols] for r in (v0, v1, v2)[3 - n_blocks:]], axis=0)
                o = o_ref[:, cols].astype(F32)
                lse_pair = lse_ref[:, cols]
                dzf = dz_ref[:, cols].astype(F32)
                silu, dsilu = _silu_parts(g_ref[:, cols].astype(F32))
                do = dzf * silu
                dqg_ref[1, :, cols] = (dzf * o * dsilu).astype(BF16)
                doo = do * o
                qm2 = jnp.concatenate([jnp.where(sel, q, jnp.zeros_like(q)) for sel in sels], axis=0) * SCALE
                dom2 = jnp.concatenate([jnp.where(sel, do, 0.0) for sel in sels], axis=0).astype(BF16)
                s2 = _dot_nt(qm2, k)
                dp2 = _dot_nt(dom2, v)
                ps, dss = [], []
                for hh, sel in enumerate(sels):
                    rows = slice(hh * QBLK, (hh + 1) * QBLK)
                    s = s2[rows] + b_ref[2 * pp + hh, :, first_col:]
                    p = jnp.exp(_minus_rows(s, _own_everywhere(lse_pair, sel)))
                    delta = jnp.sum(jnp.where(sel, doo, 0.0), axis=-1, keepdims=True)
                    ds = p * (dp2[rows] - delta)
                    gt_acc[2 * pp + hh] += _collapse_chunks(ds, A_KEYS)
                    ps.append(p.astype(BF16))
                    dss.append(ds.astype(BF16))
                dsb2 = jnp.concatenate(dss, axis=0)
                dq2 = _dot(dsb2, k) * SCALE
                dk_blk = _dot_tn(dsb2, qm2)
                dv_blk = _dot_tn(jnp.concatenate(ps, axis=0), dom2)
                dqg_ref[0, :, cols] = jnp.where(lane_lo, dq2[0:QBLK], dq2[QBLK:2 * QBLK]).astype(BF16)
                for b in range(n_blocks):
                    rows = pl.ds(pl.multiple_of((j - n_blocks + 1 + b) * QBLK, QBLK), QBLK)
                    dk_acc[rows, cols] += dk_blk[b * QBLK:(b + 1) * QBLK]
                    dv_acc[rows, cols] += dv_blk[b * QBLK:(b + 1) * QBLK]

        _by_valid_key_blocks(j, attend)

        @pl.when(j == nq - 1)
        def _():
            dkv_ref[0] = dk_acc[...].astype(BF16)
            dkv_ref[1] = dv_acc[...].astype(BF16)
            for pp in range(pairs):
                dg_ref[pp] = jnp.concatenate([_offset_sums(gt_acc[2 * pp]), _offset_sums(gt_acc[2 * pp + 1]),
                                              jnp.zeros((6, A_DIAG), F32)], axis=0)

        @pl.when(last)
        def _():
            for cp in _scatter_copies(sc_refs, land_refs, send_sems, recv_sems):
                cp.wait()

    blk = pl.BlockSpec((QBLK, lanes), lambda p, j: (j, p))
    outs = pl.pallas_call(
        body, name="attn_a_bwd", grid=(steps, nq),
        in_specs=[q_spec, *k_specs, *v_specs, g_spec, bias_spec, blk, blk, blk] + [ANY] * n_sc,
        out_specs=[pl.BlockSpec((2, QBLK, lanes), lambda p, j: (0, j, p)),
                   pl.BlockSpec((2, t, lanes), lambda p, j: (0, 0, p)),
                   pl.BlockSpec((pairs, 8, A_DIAG), lambda p, j: (p, 0, 0))] + [ANY] * n_sc,
        out_shape=[SDS((2, t, D_MODEL), BF16), SDS((2, t, D_MODEL), BF16), SDS((N_HEADS // 2, 8, A_DIAG), F32)]
        + [SDS((N_DEV - 1, *g.shape[1:]), g.dtype) for g in scatter],
        scratch_shapes=[pltpu.VMEM((t, lanes), F32), pltpu.VMEM((t, lanes), F32),
                        pltpu.VMEM((2 * pairs, CHUNK, A_KEYS), F32), pltpu.VMEM((2 * pairs, QBLK, A_KEYS), F32),
                        pltpu.SemaphoreType.DMA(((N_DEV - 1) * n_sc,)),
                        pltpu.SemaphoreType.DMA(((N_DEV - 1) * n_sc,))],
        compiler_params=_cparams(),
    )(qkvg, qkvg, qkvg, qkvg, qkvg, qkvg, qkvg, qkvg, bias, out_a, lse, dz, *scatter)
    return outs[0], outs[1], outs[2], list(outs[3:])


def _b_specs(qblk):
    per = qblk // B_PREV
    q = pl.BlockSpec((qblk, 512), lambda h, j: (j, h))
    g = pl.BlockSpec((qblk, 512), lambda h, j: (j, 2 + h))
    kp = pl.BlockSpec((B_PREV, 128), lambda h, j: (jnp.maximum(per * j - 1, 0), 0))
    kc = pl.BlockSpec((qblk, 128), lambda h, j: (j, 0))
    vp = pl.BlockSpec((B_PREV, 128), lambda h, j: (jnp.maximum(per * j - 1, 0), 1))
    vc = pl.BlockSpec((qblk, 128), lambda h, j: (j, 1))
    bias = pl.BlockSpec((B_GROUP, qblk + B_PREV), lambda h, j: (h, 0))
    sinks = pl.BlockSpec(memory_space=pltpu.SMEM)
    return q, g, kp, kc, vp, vc, bias, sinks


def _b_operands(kp, kc, vp, vc, kvh, with_prev):
    k = jnp.concatenate([kp[...], kc[...]], axis=0) if with_prev else kc[...]
    v = jnp.concatenate([vp[...], vc[...]], axis=0) if with_prev else vc[...]
    kr = pltpu.roll(k, HEAD_DIM, 1)
    vr = pltpu.roll(v, HEAD_DIM, 1)
    first = kvh == 0
    return (jnp.where(first, k, kr), jnp.where(first, kr, k),
            jnp.where(first, v, vr), jnp.where(first, vr, v))


def _attn_b_fwd(qg, kv, bias, sinks):
    t = qg.shape[0]
    qblk = B_QBLK_FWD
    per_step = 4
    step = per_step * qblk
    q_spec, g_spec, kp_spec, kc_spec, vp_spec, vc_spec, _, sink_spec = _b_specs(step)
    bias_spec = pl.BlockSpec((B_GROUP, qblk + B_PREV), lambda h, j: (h, 0))

    def body(q_ref, g_ref, kp, kc, vp, vc, w_ref, sink_ref, z_ref, o_ref, lse_ref, b_ref):
        kvh = pl.program_id(0)
        j = pl.program_id(1)
        _fill_bias(B_GROUP, lambda h: w_ref[h:h + 1, :], B_BAND, b_ref, j)
        lane_lo = _lane_lo()
        n_pairs = B_GROUP // 2

        def attend(first):
            k_lo, k_hi, v_lo, v_hi = _b_operands(kp, kc, vp, vc, kvh, True)
            for sb in range(per_step):
                no_prev = first and sb == 0
                first_col = B_PREV if no_prev else 0
                keys = slice(sb * qblk + first_col, (sb + 1) * qblk + B_PREV)
                qrows = slice(sb * qblk, (sb + 1) * qblk)
                halves = []
                for hh, sel in enumerate((lane_lo, jnp.logical_not(lane_lo))):
                    kk = (k_lo if hh == 0 else k_hi)[keys]
                    vv = (v_lo if hh == 0 else v_hi)[keys]
                    qm4 = jnp.concatenate(
                        [jnp.where(sel, q_ref[qrows, 128 * pp:128 * (pp + 1)], jnp.zeros((qblk, 128), BF16))
                         for pp in range(n_pairs)], axis=0) * SCALE
                    s4 = _dot_nt(qm4, kk)
                    es, mxs = [], []
                    for pp in range(n_pairs):
                        g = 2 * pp + hh
                        s = s4[pp * qblk:(pp + 1) * qblk] + b_ref[g, :, first_col:]
                        mxs.append(jnp.maximum(jnp.max(s, axis=-1, keepdims=True), sink_ref[kvh * B_GROUP + g]))
                        es.append(jnp.exp(s - mxs[pp]).astype(BF16))
                    r4 = _dot(jnp.concatenate(es, axis=0), jnp.where(sel, vv, jnp.ones_like(vv)))
                    halves.append((r4, mxs))
                for pp in range(n_pairs):
                    cols = slice(128 * pp, 128 * (pp + 1))
                    rows = slice(pp * qblk, (pp + 1) * qblk)
                    mxs = [halves[hh][1][pp] for hh in range(2)]
                    sink_terms = [jnp.exp(sink_ref[kvh * B_GROUP + 2 * pp + hh] - mxs[hh]) for hh in range(2)]
                    o, lse = _normalise_pair([halves[hh][0][rows] for hh in range(2)], mxs, lane_lo, sink_terms)
                    silu, _ = _silu_parts(g_ref[qrows, cols].astype(F32))
                    o_ref[qrows, cols] = o.astype(BF16)
                    z_ref[qrows, cols] = (o * silu).astype(BF16)
                    lse_ref[qrows, cols] = lse

        pl.when(j == 0)(functools.partial(attend, True))
        pl.when(j >= 1)(functools.partial(attend, False))

    out_spec = pl.BlockSpec((step, 512), lambda h, j: (j, h))
    return pl.pallas_call(
        body, name="attn_b_fwd", grid=(B_KV_HEADS, t // step),
        in_specs=[q_spec, g_spec, kp_spec, kc_spec, vp_spec, vc_spec, bias_spec, sink_spec],
        out_specs=[out_spec, out_spec, out_spec],
        out_shape=[SDS((t, D_MODEL), BF16), SDS((t, D_MODEL), BF16), SDS((t, D_MODEL), F32)],
        scratch_shapes=[pltpu.VMEM((B_GROUP, qblk, qblk + B_PREV), F32)],
        compiler_params=_cparams(),
    )(qg, qg, kv, kv, kv, kv, bias, sinks)


def _attn_b_bwd(qg, kv, bias, sinks, out_b, lse, dz, bucket_onehot):
    t = qg.shape[0]
    qblk = B_QBLK_BWD
    keys = qblk + B_PREV
    nq = t // qblk
    per = qblk // B_PREV

    def body(q_ref, g_ref, kp, kc, vp, vc, w_ref, sink_ref, o_ref, lse_ref, dz_ref, oh_ref,
             dqg_ref, dkv_ref, dt5_ref, dsink_ref, gt_acc, b_ref):
        j = pl.program_id(0)
        _fill_bias(N_HEADS, lambda h: w_ref[h:h + 1, :], B_BAND, b_ref, j)

        @pl.when(j == 0)
        def _():
            dkv_ref[...] = jnp.zeros_like(dkv_ref)
            gt_acc[...] = jnp.zeros_like(gt_acc)
            dsink_ref[...] = jnp.zeros_like(dsink_ref)

        lane_lo = _lane_lo()

        def attend(with_prev):
            first_col = 0 if with_prev else B_PREV
            dk_add = jnp.zeros((keys - first_col, 128), F32)
            dv_add = jnp.zeros((keys - first_col, 128), F32)
            for kvh in range(B_KV_HEADS):
                k_lo, k_hi, v_lo, v_hi = _b_operands(kp, kc, vp, vc, kvh, with_prev)
                dk_blk = jnp.zeros((keys - first_col, 128), F32)
                dv_blk = jnp.zeros((keys - first_col, 128), F32)
                for pp in range(B_GROUP // 2):
                    cols = slice(512 * kvh + 128 * pp, 512 * kvh + 128 * (pp + 1))
                    qp = q_ref[:, cols]
                    o = o_ref[:, cols].astype(F32)
                    lse_pair = lse_ref[:, cols]
                    dzf = dz_ref[:, cols].astype(F32)
                    silu, dsilu = _silu_parts(g_ref[:, cols].astype(F32))
                    do = dzf * silu
                    dqg_ref[1, :, cols] = (dzf * o * dsilu).astype(BF16)
                    doo = do * o
                    dqs = []
                    for hh in range(2):
                        g = kvh * B_GROUP + 2 * pp + hh
                        sel = lane_lo if hh == 0 else jnp.logical_not(lane_lo)
                        kk = k_lo if hh == 0 else k_hi
                        vv = v_lo if hh == 0 else v_hi
                        qm = jnp.where(sel, qp, jnp.zeros_like(qp)) * SCALE
                        s = _dot_nt(qm, kk) + b_ref[g, :, first_col:]
                        lse_h = _own_everywhere(lse_pair, sel)
                        p = jnp.exp(_minus_rows(s, lse_h))
                        delta = jnp.sum(jnp.where(sel, doo, 0.0), axis=-1, keepdims=True)
                        dom = jnp.where(sel, do, 0.0).astype(BF16)
                        dp = _dot_nt(dom, vv)
                        ds = p * (dp - delta)
                        gt_acc[g, :, first_col:] += ds
                        dsink_ref[g:g + 1, :] -= jnp.sum(jnp.exp(sink_ref[g] - lse_h) * delta, axis=0, keepdims=True)
                        dsb = ds.astype(BF16)
                        dqs.append(_dot(dsb, kk) * SCALE)
                        dk_blk = dk_blk + _dot_tn(dsb, qm)
                        dv_blk = dv_blk + _dot_tn(p.astype(BF16), dom)
                    dqg_ref[0, :, cols] = jnp.where(lane_lo, dqs[0], dqs[1]).astype(BF16)
                mine = lane_lo if kvh == 0 else jnp.logical_not(lane_lo)
                dk_add = dk_add + jnp.where(mine, dk_blk + pltpu.roll(dk_blk, HEAD_DIM, 1), 0.0)
                dv_add = dv_add + jnp.where(mine, dv_blk + pltpu.roll(dv_blk, HEAD_DIM, 1), 0.0)
            first_key = B_PREV if with_prev else 0
            if with_prev:
                rows = pl.ds(pl.multiple_of(j * qblk - B_PREV, B_PREV), B_PREV)
                dkv_ref[0, rows, :] += dk_add[0:B_PREV]
                dkv_ref[1, rows, :] += dv_add[0:B_PREV]
            rows = pl.ds(pl.multiple_of(j * qblk, qblk), qblk)
            dkv_ref[0, rows, :] += dk_add[first_key:first_key + qblk]
            dkv_ref[1, rows, :] += dv_add[first_key:first_key + qblk]

        pl.when(j == 0)(functools.partial(attend, False))
        pl.when(j >= 1)(functools.partial(attend, True))

        @pl.when(j == nq - 1)
        def _():
            dd = jnp.concatenate([_offset_sums(_collapse_chunks(gt_acc[g], keys)) for g in range(N_HEADS)], axis=0)
            hi = dd.astype(BF16)
            lo = (dd - hi.astype(F32)).astype(BF16)
            dt5_ref[...] = _dot(hi, oh_ref[...]) + _dot(lo, oh_ref[...])

    wide = lambda col: pl.BlockSpec((qblk, D_MODEL), lambda j, col=col: (j, col))
    prev = lambda col: pl.BlockSpec((B_PREV, 128), lambda j, col=col: (jnp.maximum(per * j - 1, 0), col))
    cur = lambda col: pl.BlockSpec((qblk, 128), lambda j, col=col: (j, col))
    fixed = lambda shape: pl.BlockSpec(shape, lambda j: (0,) * len(shape))
    return pl.pallas_call(
        body, name="attn_b_bwd", grid=(nq,),
        in_specs=[wide(0), wide(1), prev(0), cur(0), prev(1), cur(1), fixed((N_HEADS, keys)),
                  pl.BlockSpec(memory_space=pltpu.SMEM), wide(0), wide(0), wide(0), fixed((keys, 128))],
        out_specs=[pl.BlockSpec((2, qblk, D_MODEL), lambda j: (0, j, 0)), fixed((2, t, 128)),
                   fixed((N_HEADS, 128)), fixed((N_HEADS, 128))],
        out_shape=[SDS((2, t, D_MODEL), BF16), SDS((2, t, 128), F32),
                   SDS((N_HEADS, 128), F32), SDS((N_HEADS, 128), F32)],
        scratch_shapes=[pltpu.VMEM((N_HEADS, qblk, keys), F32), pltpu.VMEM((N_HEADS, qblk, keys), F32)],
        compiler_params=_cparams(),
    )(qg, qg, kv, kv, kv, kv, bias, sinks, out_b, lse, dz, bucket_onehot)


def _a_bias_by_offset(rel_bias):
    m = np.arange(A_DIAG)
    idx = np.clip(A_BAND - 1 - m, -A_REL_CLIP, A_REL_CLIP) + A_REL_CLIP
    by_head = rel_bias[idx].T.reshape(N_HEADS // 2, 2, A_DIAG)
    return jnp.concatenate([by_head, jnp.zeros((N_HEADS // 2, 6, A_DIAG), F32)], axis=1)


def _a_bias_grad(offset_sums):
    first = 319
    tail = jnp.sum(offset_sums[:, :first], axis=1)
    body = jnp.flip(offset_sums[:, first:first + 320], axis=1)
    body = body.at[:, -1].add(tail)
    full = jnp.concatenate([jnp.zeros((N_HEADS, 193), F32), body], axis=1)
    return full


def _t5_bucket(rel):
    nb = T5_BUCKETS // 2
    max_exact = nb // 2
    ret = jnp.where(rel > 0, nb, 0)
    n = jnp.abs(rel)
    nf = jnp.maximum(n, 1).astype(jnp.float32)
    large = max_exact + (jnp.log(nf / max_exact) / math.log(T5_MAX_DIST / max_exact)
                         * (nb - max_exact)).astype(jnp.int32)
    large = jnp.minimum(large, nb - 1)
    return ret + jnp.where(n < max_exact, n, large)


def _b_offset_buckets(keys):
    return _t5_bucket(jnp.arange(keys, dtype=jnp.int32) - (B_LEFT_CHUNKS * CHUNK + CHUNK - 1))


def _b_bias_by_offset(t5_table, keys):
    return t5_table[_b_offset_buckets(keys)].T


def _b_bucket_onehot(keys):
    return (_b_offset_buckets(keys)[:, None] == jnp.arange(128)[None, :]).astype(BF16)


def _local_step(my_slot, order, x, target, a_gain_shard, w_in_a_shard, rel_bias, late_shards, kv_gain,
                t5_table, b_gain, sinks, f_gain):
    a_bias = _a_bias_by_offset(rel_bias)
    b_bias_fwd = _b_bias_by_offset(t5_table, B_QBLK_FWD + B_PREV)
    b_bias_bwd = _b_bias_by_offset(t5_table, B_QBLK_BWD + B_PREV)
    sinks_flat = sinks.reshape(N_HEADS)

    xn, qkvg, w_in_a, a_gain = _norm_matmul_gather(order, x, a_gain_shard, w_in_a_shard)
    z_a, out_a, lse_a, (w_in_b, w_out_a, w_out_b, kv_w) = _attn_a_fwd(qkvg, a_bias, late_shards)
    w_out_a = w_out_a.reshape(D_MODEL, D_MODEL)
    w_out_b = w_out_b.reshape(D_MODEL, D_MODEL)
    kv_w = kv_w.reshape(D_MODEL, 2 * 128)
    h1, kvn, hb, kv, qg = _layer_a_out(x, z_a, w_out_a, kv_gain, b_gain, kv_w, w_in_b)
    z_b, out_b, lse_b = _attn_b_fwd(qg, kv, b_bias_fwd, sinks_flat)
    dh2, dh2b, dz_b, loss, d_fn = _layer_b_out_loss(h1, z_b, w_out_b, f_gain, target)

    dqg_b, dkv_b, d_t5, d_sink = _attn_b_bwd(qg, kv, b_bias_bwd, sinks_flat, out_b, lse_b, dz_b,
                                             _b_bucket_onehot(B_QBLK_BWD + B_PREV))
    dh1, dh1b, dz_a, d_bn, d_kn = _layer_b_in_bwd(dqg_b, dkv_b, w_in_b, kv_w, h1, dh2, b_gain, kv_gain, w_out_a)
    early = dict(
        b_w_out=_weight_grad_rows("grad_b_w_out", my_slot, z_b, dh2b[None]),
        b_w_in=_weight_grad_cols("grad_b_w_in", my_slot, hb, [dqg_b],
                                 [(0, o, c, 4 * o + c) for o in range(2) for c in range(4)], 256),
        kv_w=_weight_grad_rows("grad_kv_w", my_slot, kvn, dkv_b),
        a_w_out=_weight_grad_rows("grad_a_w_out", my_slot, z_a, dh1b[None]))
    dqg_a, dkv_a, d_rel, landed = _attn_a_bwd(qkvg, a_bias, out_a, lse_a, dz_a, [g[0] for g in early.values()])
    g_w_in_a = _weight_grad_cols(
        "grad_a_w_in", my_slot, xn, [dqg_a, dkv_a],
        [(0, 0, 0, 0), (0, 0, 1, 1), (1, 0, 0, 2), (1, 0, 1, 3), (1, 1, 0, 4), (1, 1, 1, 5), (0, 1, 0, 6), (0, 1, 1, 7)], 512)
    chip_sums, from_sibling = _chip_sums(g_w_in_a[0])
    grad_x, d_an, from_chips = _layer_a_in_bwd(dqg_a, dkv_a, w_in_a, x, dh1, a_gain, chip_sums)

    matrices = {n: (g[1], [(land, 0, N_DEV - 1)]) for (n, g), land in zip(early.items(), landed)}
    matrices["a_w_in"] = (g_w_in_a[1], [(from_sibling, 0, 1), (from_chips, 0, 3)])
    small = dict(
        loss=loss, a_norm=d_an, a_rel_bias=d_rel[:, :2].reshape(N_HEADS, A_DIAG),
        kv_norm=d_kn, t5_bias=d_t5, b_norm=d_bn, b_sinks=d_sink, final_norm=d_fn)
    return grad_x, small, matrices


def _place():
    x, y, c = lax.axis_index("x"), lax.axis_index("y"), lax.axis_index("c")
    chips = [(1 - x, y), (x, 1 - y), (1 - x, 1 - y)]
    return x, y, c, chips


def _slot(px, py, pc):
    return 4 * px + 2 * py + pc


ANY = pl.BlockSpec(memory_space=pl.ANY)


def _peer(x, y, c, k):
    return (x ^ (k >> 2), y ^ ((k >> 1) & 1), c ^ (k & 1))


def _scatter_copies(grad_refs, land_refs, send_sems, recv_sems):
    x, y, c, _ = _place()
    copies = []
    for t, (grad, land) in enumerate(zip(grad_refs, land_refs)):
        for k in range(1, N_DEV):
            peer = _peer(x, y, c, k)
            sem = (N_DEV - 1) * t + k - 1
            copies.append(pltpu.make_async_remote_copy(
                src_ref=grad.at[_slot(*peer)], dst_ref=land.at[k - 1],
                send_sem=send_sems.at[sem], recv_sem=recv_sems.at[sem],
                device_id=peer, device_id_type=MESH))
    return copies


def _gather_phases(ins, outs, send_sems, recv_sems, local_sems):
    n = len(ins)
    x, y, c, chips = _place()
    me, sibling = (x, y, c), (x, y, 1 - c)

    def copy(t, k, block, to, src=None):
        dst = outs[t].at[_slot(*block)]
        return pltpu.make_async_remote_copy(
            src_ref=dst if src is None else src, dst_ref=dst,
            send_sem=send_sems.at[7 * t + k], recv_sem=recv_sems.at[7 * t + k],
            device_id=to, device_id_type=MESH)

    def lists():
        mine = [pltpu.make_async_copy(ins[t], outs[t].at[_slot(*me)], local_sems.at[t]) for t in range(n)]
        first = []
        for t in range(n):
            first.append(copy(t, 0, me, sibling, src=ins[t]))
            first += [copy(t, 1 + j, me, (*chip, c), src=ins[t]) for j, chip in enumerate(chips)]
        passed = [copy(t, 4 + j, (*chip, c), sibling) for t in range(n) for j, chip in enumerate(chips)]
        return mine, first, passed

    def start():
        mine, first, _ = lists()
        for cp in mine + first:
            cp.start()

    def forward():
        _, _, passed = lists()
        for t in range(n):
            for j, chip in enumerate(chips):
                copy(t, 1 + j, (*chip, c), me).wait_recv()
                passed[3 * t + j].start()

    def finish():
        mine, first, passed = lists()
        for t in range(n):
            copy(t, 0, sibling, me).wait_recv()
            for j, chip in enumerate(chips):
                copy(t, 4 + j, (*chip, 1 - c), me).wait_recv()
        for cp in first + passed:
            cp.wait_send()
        for cp in mine:
            cp.wait()

    return start, forward, finish


def _gather_scratch(n):
    return [pltpu.SemaphoreType.DMA((7 * n,)), pltpu.SemaphoreType.DMA((7 * n,)), pltpu.SemaphoreType.DMA((n,))]


def _chip_sums(g):
    _, r, c = g.shape

    def body(g_ref, sums_ref, mine_ref, land, own, send_sems, recv_sems, load_sems):
        x, y, c_i, chips = _place()
        sibling = (x, y, 1 - c_i)
        blocks = [(*chip, 1 - c_i) for chip in chips] + [sibling]
        sends = [pltpu.make_async_remote_copy(
            src_ref=g_ref.at[_slot(*block)], dst_ref=land.at[k], send_sem=send_sems.at[k],
            recv_sem=recv_sems.at[k], device_id=sibling, device_id_type=MESH) for k, block in enumerate(blocks)]
        loads = [pltpu.make_async_copy(g_ref.at[_slot(*chip, c_i)], own.at[j], load_sems.at[j])
                 for j, chip in enumerate(chips)]
        for cp in sends + loads:
            cp.start()
        for cp in sends + loads:
            cp.wait()
        for j in range(3):
            sums_ref[j] = (own[j].astype(F32) + land[j].astype(F32)).astype(BF16)
        mine_ref[0] = land[3]

    return pl.pallas_call(
        body, name="chip_sums",
        in_specs=[ANY], out_specs=[VM, VM],
        out_shape=[SDS((3, r, c), BF16), SDS((1, r, c), BF16)],
        scratch_shapes=[pltpu.VMEM((4, r, c), BF16), pltpu.VMEM((3, r, c), BF16),
                        pltpu.SemaphoreType.DMA((4,)), pltpu.SemaphoreType.DMA((4,)), pltpu.SemaphoreType.DMA((3,))],
        compiler_params=_cparams(),
    )(g)


def _chip_copies(sums_ref, land_ref, send_sems, recv_sems):
    x, y, c, chips = _place()
    del x, y
    return [pltpu.make_async_remote_copy(
        src_ref=sums_ref.at[j], dst_ref=land_ref.at[j], send_sem=send_sems.at[j], recv_sem=recv_sems.at[j],
        device_id=(*chip, c), device_id_type=MESH) for j, chip in enumerate(chips)]


def _row_tile(rows):
    return min(rows, 256)


def _adamw(w, g, m, v):
    m2 = ADAM_B1 * m + (1.0 - ADAM_B1) * g
    v2 = ADAM_B2 * v + (1.0 - ADAM_B2) * jnp.square(g)
    m_hat = m2 / (1.0 - ADAM_B1 ** ADAM_STEP)
    v_hat = v2 / (1.0 - ADAM_B2 ** ADAM_STEP)
    delta = -ADAM_LR * (m_hat / (jnp.sqrt(v_hat) + ADAM_EPS) + ADAM_WD * w)
    return delta, m2, v2


def _reduce_adamw(name, own, partials, w, m, v):
    r, c = own.shape
    tr = _row_tile(r)
    n_p = len(partials)

    def body(own_ref, *rest):
        p_refs, (w_ref, m_ref, v_ref, grad_ref, d_ref, nm_ref, nv_ref) = rest[:n_p], rest[n_p:]
        grad = own_ref[...]
        for p_ref, (_, _, count) in zip(p_refs, partials):
            for j in range(count):
                grad = grad + p_ref[j].astype(F32)
        grad_ref[...] = grad
        d_ref[...], nm_ref[...], nv_ref[...] = _adamw(w_ref[...], grad, m_ref[...], v_ref[...])

    flat = pl.BlockSpec((tr, c), lambda i: (i, 0))
    return pl.pallas_call(
        body, name=name, grid=(r // tr,),
        in_specs=[flat] + [pl.BlockSpec((count, tr, c), lambda i, first=first, count=count: (first // count, i, 0))
                           for _, first, count in partials] + [flat, flat, flat],
        out_specs=[flat, flat, flat, flat],
        out_shape=[SDS((r, c), F32)] * 4,
        compiler_params=_cparams(),
    )(own, *[p[0] for p in partials], w, m, v)


VM = pl.BlockSpec()


def _small_allreduce(parts):
    n = len(parts)

    def body(*refs):
        ins, outs, lands = refs[:n], refs[n:2 * n], refs[2 * n:3 * n]
        send_sems, recv_sems = refs[3 * n:]
        x, y, c, _ = _place()
        my_slot = _slot(x, y, c)
        copies = []
        for t in range(n):
            lands[t][my_slot] = ins[t][...]
            for k in range(1, N_DEV):
                sem = (N_DEV - 1) * t + k - 1
                copies.append(pltpu.make_async_remote_copy(
                    src_ref=ins[t], dst_ref=lands[t].at[my_slot],
                    send_sem=send_sems.at[sem], recv_sem=recv_sems.at[sem],
                    device_id=_peer(x, y, c, k), device_id_type=MESH))
        for cp in copies:
            cp.start()
        for t in range(n):
            for k in range(1, N_DEV):
                sem = (N_DEV - 1) * t + k - 1
                pltpu.make_async_remote_copy(
                    src_ref=ins[t], dst_ref=lands[t].at[_slot(*_peer(x, y, c, k))],
                    send_sem=send_sems.at[sem], recv_sem=recv_sems.at[sem],
                    device_id=(x, y, c), device_id_type=MESH).wait_recv()
        for cp in copies:
            cp.wait_send()
        for t in range(n):
            total = lands[t][0]
            for s in range(1, N_DEV):
                total = total + lands[t][s]
            outs[t][...] = total

    n_sems = (N_DEV - 1) * n
    return pl.pallas_call(
        body, name="small_allreduce",
        in_specs=[VM] * n, out_specs=[VM] * n, out_shape=[SDS(p.shape, F32) for p in parts],
        scratch_shapes=[pltpu.VMEM((N_DEV, *p.shape), F32) for p in parts]
        + [pltpu.SemaphoreType.DMA((n_sems,)), pltpu.SemaphoreType.DMA((n_sems,))],
    )(*parts)


def _small_adamw(my_slot, sums, ws, ms, vs):
    n = len(ws)

    def body(slot_ref, *refs):
        sum_refs, refs = refs[:n + 1], refs[n + 1:]
        w_refs, m_refs, v_refs, refs = refs[:n], refs[n:2 * n], refs[2 * n:3 * n], refs[3 * n:]
        g_refs, d_refs, nm_refs, nv_refs = refs[:n + 1], refs[n + 1:2 * n + 1], refs[2 * n + 1:3 * n + 1], refs[3 * n + 1:]
        for t in range(n + 1):
            if t == 0:
                g = sum_refs[0][:, pl.ds(pl.multiple_of(slot_ref[0] * 128, 128), 128)]
            else:
                g = sum_refs[t][...]
            g_refs[t][...] = g
            if t < n:
                d_refs[t][...], nm_refs[t][...], nv_refs[t][...] = _adamw(w_refs[t][...], g, m_refs[t][...], v_refs[t][...])

    shapes = [SDS(w.shape, F32) for w in ws]
    outs = pl.pallas_call(
        body, name="small_adamw",
        in_specs=[pl.BlockSpec(memory_space=pltpu.SMEM)] + [VM] * (4 * n + 1),
        out_specs=[VM] * (4 * n + 1),
        out_shape=shapes + [SDS(sums[-1].shape, F32)] + shapes * 3,
    )(my_slot, *sums, *ws, *ms, *vs)
    return outs[:n + 1], outs[n + 1:2 * n + 1], outs[2 * n + 1:3 * n + 1], outs[3 * n + 1:]


def kernel(x, a_norm, a_w_in, a_rel_bias, a_w_out, kv_norm, kv_w, t5_bias, b_norm, b_w_in, b_sinks, b_w_out, final_norm, loss_target, m_a_norm, m_a_w_in, m_a_rel_bias, m_a_w_out, m_kv_norm, m_kv_w, m_t5_bias, m_b_norm, m_b_w_in, m_b_sinks, m_b_w_out, m_final_norm, v_a_norm, v_a_w_in, v_a_rel_bias, v_a_w_out, v_kv_norm, v_kv_w, v_t5_bias, v_b_norm, v_b_w_in, v_b_sinks, v_b_w_out, v_final_norm):
    xi, yi, ci = lax.axis_index("x"), lax.axis_index("y"), lax.axis_index("c")
    my_slot = _slot(xi, yi, ci)

    slot_arr = jnp.reshape(my_slot, (1,)).astype(jnp.int32)
    order = _gather_order(xi, yi, ci)
    late_shards = [b_w_in[0].astype(BF16), a_w_out[0].astype(BF16), b_w_out[0].astype(BF16), kv_w.astype(BF16)]
    grad_x, loc, matrices = _local_step(
        slot_arr, order, x[0], loss_target[0], a_norm, a_w_in[0].astype(BF16), a_rel_bias[0], late_shards,
        kv_norm.reshape(1, D_MODEL), t5_bias, b_norm, b_sinks, final_norm.reshape(1, D_MODEL))

    shard_w = dict(a_w_in=a_w_in[0], b_w_in=b_w_in[0], a_w_out=a_w_out[0], b_w_out=b_w_out[0], kv_w=kv_w)
    shard_m = dict(a_w_in=m_a_w_in[0], b_w_in=m_b_w_in[0], a_w_out=m_a_w_out[0], b_w_out=m_b_w_out[0], kv_w=m_kv_w)
    shard_v = dict(a_w_in=v_a_w_in[0], b_w_in=v_b_w_in[0], a_w_out=v_a_w_out[0], b_w_out=v_b_w_out[0], kv_w=v_kv_w)
    big = {n: _reduce_adamw("adamw_" + n, own, partials, shard_w[n], shard_m[n], shard_v[n])
           for n, (own, partials) in matrices.items()}

    names = ("a_norm", "a_rel_bias", "kv_norm", "t5_bias", "b_norm", "b_sinks", "final_norm")
    tables = ("a_rel_bias", "t5_bias")

    def row(n, a):
        return a.reshape(-1, a.shape[-1]).T if n in tables else a.reshape(1, -1)

    small_w = [row(n, a) for n, a in zip(names, (a_norm, a_rel_bias, kv_norm, t5_bias, b_norm, b_sinks, final_norm))]
    small_m = [row(n, a) for n, a in zip(names, (m_a_norm, m_a_rel_bias, m_kv_norm, m_t5_bias, m_b_norm, m_b_sinks,
                                                 m_final_norm))]
    small_v = [row(n, a) for n, a in zip(names, (v_a_norm, v_a_rel_bias, v_kv_norm, v_t5_bias, v_b_norm, v_b_sinks,
                                                 v_final_norm))]
    sums = dict(zip(names + ("loss",), _small_allreduce([loc[n] for n in names] + [loc["loss"]])))
    sums["a_rel_bias"] = _a_bias_grad(sums["a_rel_bias"])
    sums["t5_bias"] = sums["t5_bias"][:, :T5_BUCKETS]
    sums["b_sinks"] = sums["b_sinks"][:, 0].reshape(1, N_HEADS)
    results = _small_adamw(slot_arr, [sums[n] for n in names + ("loss",)], small_w, small_m, small_v)
    like = dict(a_norm=a_norm, a_rel_bias=a_rel_bias, kv_norm=kv_norm, t5_bias=t5_bias, b_norm=b_norm,
                b_sinks=b_sinks, final_norm=final_norm)
    sm = [{n: (part[i].T if n in tables else part[i]).reshape(like[n].shape) for i, n in enumerate(names)}
          for part in results]
    loss = results[0][len(names)][0, 0]

    order = ("a_norm", "a_w_in", "a_rel_bias", "a_w_out", "kv_norm", "kv_w", "t5_bias", "b_norm",
             "b_w_in", "b_sinks", "b_w_out", "final_norm")
    lead = dict(a_w_in=True, b_w_in=True, a_w_out=True, b_w_out=True, kv_w=False)

    def pick(kind, name):
        if name in big:
            val = big[name][kind]
            return val[None] if lead[name] else val
        return sm[kind][name]

    outs = [loss, grad_x[None]]
    for kind in range(4):
        outs += [pick(kind, n) for n in order]
    return tuple(outs)
```

```python
import functools
import math

import numpy as np
import jax
import jax.numpy as jnp
from jax import lax
from jax.experimental import pallas as pl
from jax.experimental.pallas import tpu as pltpu

F32 = jnp.float32
BF16 = jnp.bfloat16
SDS = jax.ShapeDtypeStruct

D_MODEL = 1024
HEAD_DIM = 64
CHUNK = 64
N_HEADS = 16
RMS_EPS = 1e-6
A_LEFT_CHUNKS = 8
A_BAND = (A_LEFT_CHUNKS + 1) * CHUNK
A_REL_CLIP = 256
B_KV_HEADS = 2
B_GROUP = 8
B_LEFT_CHUNKS = 2
B_BAND = (B_LEFT_CHUNKS + 1) * CHUNK
T5_BUCKETS = 32
T5_MAX_DIST = 128
QBLK = 256
A_KEYS = 3 * QBLK
B_QBLK_FWD = 128
B_QBLK_BWD = 256
B_PREV = 128
A_DIAG = A_KEYS
NEG = -1e30
SCALE = HEAD_DIM ** -0.5
N_DEV = 8

ADAM_LR = 0.001
ADAM_B1 = 0.9
ADAM_B2 = 0.999
ADAM_EPS = 1e-08
ADAM_WD = 0.01
ADAM_STEP = 10

VMEM_LIMIT_BYTES = 56 * 1024 * 1024
MESH = pl.DeviceIdType.MESH


def _cparams():
    return pltpu.CompilerParams(vmem_limit_bytes=VMEM_LIMIT_BYTES)


def _dot(a, b):
    return jnp.dot(a, b, preferred_element_type=F32)


def _dot_nt(a, b):
    return lax.dot_general(a, b, (((1,), (1,)), ((), ())), preferred_element_type=F32)


def _dot_tn(a, b):
    return lax.dot_general(a, b, (((0,), (0,)), ((), ())), preferred_element_type=F32)


def _rstd(xf):
    return lax.rsqrt(jnp.mean(xf * xf, axis=-1, keepdims=True) + RMS_EPS)


def _sigmoid(x):
    return 1.0 / (1.0 + jnp.exp(-x))


_GATHER_SEQUENCE = ((0, None), (1, 0), (2, 1), (4, None), (5, None), (3, 2), (6, None))


def _gather_order(x, y, c):
    others = [(1 - x, y), (x, 1 - y), (1 - x, 1 - y)]
    arrivals = [_slot(x, y, 1 - c)] + [_slot(*chip, c) for chip in others] + [_slot(*chip, 1 - c) for chip in others]
    return jnp.stack([_slot(x, y, c)] + [arrivals[a] for a, _ in _GATHER_SEQUENCE]).astype(jnp.int32)


def _norm_matmul_gather(order, x, gain_shard, w_shard):
    t = x.shape[0]
    dw, tn = w_shard.shape
    tm = min(t, 2048)
    n_m = t // tm

    def body(order_ref, x_ref, gs_ref, shard_ref, xn_ref, o_ref, full_ref, gain_ref,
             xn_all, wbuf, gland, send_sems, recv_sems, gsend_sems, grecv_sems, load_sems, own_sem):
        n, m = pl.program_id(0), pl.program_id(1)
        x_i, y_i, c_i, chips = _place()
        me, sibling = (x_i, y_i, c_i), (x_i, y_i, 1 - c_i)

        def send(k, block, to, src=None):
            dst = full_ref.at[_slot(*block)]
            return pltpu.make_async_remote_copy(
                src_ref=dst if src is None else src, dst_ref=dst,
                send_sem=send_sems.at[k], recv_sem=recv_sems.at[k], device_id=to, device_id_type=MESH)

        own = pltpu.make_async_copy(shard_ref, full_ref.at[_slot(*me)], own_sem)
        first = [send(0, me, sibling, src=shard_ref)]
        first += [send(1 + j, me, (*chip, c_i), src=shard_ref) for j, chip in enumerate(chips)]
        forwards = [send(4 + j, (*chip, c_i), sibling) for j, chip in enumerate(chips)]
        arrivals = [send(0, sibling, me)] + [send(1 + j, (*chip, c_i), me) for j, chip in enumerate(chips)]
        arrivals += [send(4 + j, (*chip, 1 - c_i), me) for j, chip in enumerate(chips)]
        gains = [pltpu.make_async_remote_copy(
            src_ref=gs_ref, dst_ref=gland.at[_slot(*me)], send_sem=gsend_sems.at[k - 1],
            recv_sem=grecv_sems.at[k - 1], device_id=_peer(x_i, y_i, c_i, k), device_id_type=MESH)
            for k in range(1, N_DEV)]

        @pl.when(jnp.logical_and(n == 0, m == 0))
        def _():
            own.start()
            for cp in gains + first:
                cp.start()
            pltpu.make_async_copy(shard_ref, wbuf.at[0], load_sems.at[0]).start()
            gland[_slot(*me)] = gs_ref[...]
            for k in range(1, N_DEV):
                pltpu.make_async_remote_copy(
                    src_ref=gs_ref, dst_ref=gland.at[_slot(*_peer(x_i, y_i, c_i, k))],
                    send_sem=gsend_sems.at[k - 1], recv_sem=grecv_sems.at[k - 1],
                    device_id=me, device_id_type=MESH).wait_recv()
            for s in range(N_DEV):
                gain_ref[:, 128 * s:128 * (s + 1)] = gland[s]

        rows = pl.ds(pl.multiple_of(m * tm, tm), tm)

        @pl.when(n == 0)
        def _():
            xf = x_ref[...]
            xn = ((xf * _rstd(xf)) * gain_ref[...]).astype(BF16)
            xn_all[rows, :] = xn
            xn_ref[...] = xn

        @pl.when(m == 0)
        def _():
            pltpu.make_async_copy(full_ref.at[0], wbuf.at[n % 2], load_sems.at[n % 2]).wait()

        o_ref[...] = _dot(xn_all[rows, :], wbuf[n % 2]).astype(BF16)

        for k, (arrival, forward) in enumerate(_GATHER_SEQUENCE):
            @pl.when(jnp.logical_and(n == k, m == n_m - 1))
            def _(k=k, arrival=arrival, forward=forward):
                arrivals[arrival].wait_recv()
                if forward is not None:
                    forwards[forward].start()
                pltpu.make_async_copy(full_ref.at[order_ref[k + 1]], wbuf.at[(k + 1) % 2],
                                      load_sems.at[(k + 1) % 2]).start()

        @pl.when(jnp.logical_and(n == N_DEV - 1, m == n_m - 1))
        def _():
            for cp in gains + first + forwards:
                cp.wait_send()
            own.wait()

    held = lambda n, m, order: (jnp.where(n == 0, m, n_m - 1), 0)
    return pl.pallas_call(
        body, name="norm_matmul_gather",
        grid_spec=pltpu.PrefetchScalarGridSpec(
            num_scalar_prefetch=1, grid=(N_DEV, n_m),
            in_specs=[pl.BlockSpec((tm, D_MODEL), held),
                      pl.BlockSpec((1, 128), lambda n, m, order: (0, 0)), ANY],
            out_specs=[pl.BlockSpec((tm, D_MODEL), held),
                       pl.BlockSpec((tm, tn), lambda n, m, order: (m, order[n])),
                       ANY, pl.BlockSpec((1, D_MODEL), lambda n, m, order: (0, 0))],
            scratch_shapes=[pltpu.VMEM((t, D_MODEL), BF16), pltpu.VMEM((2, dw, tn), BF16),
                            pltpu.VMEM((N_DEV, 1, 128), F32),
                            pltpu.SemaphoreType.DMA((7,)), pltpu.SemaphoreType.DMA((7,)),
                            pltpu.SemaphoreType.DMA((7,)), pltpu.SemaphoreType.DMA((7,)),
                            pltpu.SemaphoreType.DMA((2,)), pltpu.SemaphoreType.DMA]),
        out_shape=[SDS((t, D_MODEL), BF16), SDS((t, N_DEV * tn), BF16), SDS((N_DEV, dw, tn), BF16),
                   SDS((1, D_MODEL), F32)],
        compiler_params=_cparams(),
    )(order, x, gain_shard, w_shard)


def _layer_a_out(x, z, w_out, kv_gain, b_gain, kv_w, w_in_b):
    t = x.shape[0]
    tm = min(t, 1024)
    nb, _, tn = w_in_b.shape

    def body(x_ref, z_ref, wo_ref, kvg_ref, bg_ref, kvw_ref, wb_ref,
             h1_ref, kvn_ref, hb_ref, kv_ref, qg_ref):
        h1 = x_ref[...] + _dot(z_ref[...], wo_ref[...])
        h1_ref[...] = h1
        y0 = h1 * _rstd(h1)
        kvn = (y0 * kvg_ref[...]).astype(BF16)
        hb = (y0 * bg_ref[...]).astype(BF16)
        kvn_ref[...] = kvn
        hb_ref[...] = hb
        kv_ref[...] = _dot(kvn, kvw_ref[...]).astype(BF16)
        for i in range(nb):
            qg_ref[:, i * tn:(i + 1) * tn] = _dot(hb, wb_ref[i]).astype(BF16)

    row = lambda m: (m, 0)
    fix2 = lambda m: (0, 0)
    return pl.pallas_call(
        body, name="layer_a_out", grid=(t // tm,),
        in_specs=[pl.BlockSpec((tm, D_MODEL), row), pl.BlockSpec((tm, D_MODEL), row),
                  pl.BlockSpec((D_MODEL, D_MODEL), fix2),
                  pl.BlockSpec((1, D_MODEL), fix2), pl.BlockSpec((1, D_MODEL), fix2),
                  pl.BlockSpec((D_MODEL, 256), fix2),
                  pl.BlockSpec((nb, D_MODEL, tn), lambda m: (0, 0, 0))],
        out_specs=[pl.BlockSpec((tm, D_MODEL), row), pl.BlockSpec((tm, D_MODEL), row),
                   pl.BlockSpec((tm, D_MODEL), row), pl.BlockSpec((tm, 256), row),
                   pl.BlockSpec((tm, nb * tn), row)],
        out_shape=[SDS((t, D_MODEL), F32), SDS((t, D_MODEL), BF16), SDS((t, D_MODEL), BF16),
                   SDS((t, 256), BF16), SDS((t, nb * tn), BF16)],
        compiler_params=_cparams(),
    )(x, z, w_out, kv_gain, b_gain, kv_w, w_in_b)


def _layer_b_out_loss(h1, z, w_out, f_gain, target):
    t = h1.shape[0]
    tm = min(t, 1024)

    def body(h1_ref, z_ref, wo_ref, fg_ref, tgt_ref,
             dh2_ref, dh2b_ref, dz_ref, loss_ref, dfn_ref):
        @pl.when(pl.program_id(0) == 0)
        def _():
            loss_ref[...] = jnp.zeros_like(loss_ref)
            dfn_ref[...] = jnp.zeros_like(dfn_ref)

        h2 = h1_ref[...] + _dot(z_ref[...], wo_ref[...])
        r = _rstd(h2)
        yn = h2 * r
        fg = fg_ref[...]
        err = yn * fg - tgt_ref[...]
        loss_ref[...] += (0.5 / D_MODEL) * jnp.sum(err * err)
        dy = err * (1.0 / D_MODEL)
        dfn_ref[...] += jnp.sum(dy * yn, axis=0, keepdims=True)
        u = dy * fg
        dh2 = r * u - h2 * ((r * r * r) * jnp.mean(u * h2, axis=-1, keepdims=True))
        dh2_ref[...] = dh2
        dh2b = dh2.astype(BF16)
        dh2b_ref[...] = dh2b
        dz_ref[...] = _dot_nt(dh2b, wo_ref[...]).astype(BF16)

    row = lambda m: (m, 0)
    fix2 = lambda m: (0, 0)
    return pl.pallas_call(
        body, name="layer_b_out_loss", grid=(t // tm,),
        in_specs=[pl.BlockSpec((tm, D_MODEL), row), pl.BlockSpec((tm, D_MODEL), row),
                  pl.BlockSpec((D_MODEL, D_MODEL), fix2), pl.BlockSpec((1, D_MODEL), fix2),
                  pl.BlockSpec((tm, D_MODEL), row)],
        out_specs=[pl.BlockSpec((tm, D_MODEL), row), pl.BlockSpec((tm, D_MODEL), row),
                   pl.BlockSpec((tm, D_MODEL), row), pl.BlockSpec((1, 128), fix2),
                   pl.BlockSpec((1, D_MODEL), fix2)],
        out_shape=[SDS((t, D_MODEL), F32), SDS((t, D_MODEL), BF16), SDS((t, D_MODEL), BF16),
                   SDS((1, 128), F32), SDS((1, D_MODEL), F32)],
        compiler_params=_cparams(),
    )(h1, z, w_out, f_gain, target)


def _layer_b_in_bwd(dqg, dkv, w_in_b, kv_w, h1, dh2, b_gain, kv_gain, w_out_a):
    t = h1.shape[0]
    tm = min(t, 512)
    nb, _, tn = w_in_b.shape
    per = D_MODEL // tn

    def body(dqg_ref, dkv_ref, wb_ref, kvw_ref, h1_ref, dh2_ref, bg_ref, kvg_ref, wo_ref,
             dh1_ref, dh1b_ref, dz_ref, dbn_ref, dkn_ref):
        @pl.when(pl.program_id(0) == 0)
        def _():
            dbn_ref[...] = jnp.zeros_like(dbn_ref)
            dkn_ref[...] = jnp.zeros_like(dkn_ref)

        dhb = jnp.zeros((tm, D_MODEL), F32)
        for i in range(nb):
            blk = dqg_ref[i // per, :, (i % per) * tn:(i % per + 1) * tn]
            dhb = dhb + _dot_nt(blk, wb_ref[i])
        dkn = (_dot_nt(dkv_ref[0].astype(BF16), kvw_ref[:, 0:128])
               + _dot_nt(dkv_ref[1].astype(BF16), kvw_ref[:, 128:256]))
        h1 = h1_ref[...]
        r = _rstd(h1)
        xr = h1 * r
        dbn_ref[...] += jnp.sum(dhb * xr, axis=0, keepdims=True)
        dkn_ref[...] += jnp.sum(dkn * xr, axis=0, keepdims=True)
        u = dhb * bg_ref[...] + dkn * kvg_ref[...]
        dh1 = dh2_ref[...] + r * u - h1 * ((r * r * r) * jnp.mean(u * h1, axis=-1, keepdims=True))
        dh1_ref[...] = dh1
        dh1b = dh1.astype(BF16)
        dh1b_ref[...] = dh1b
        dz_ref[...] = _dot_nt(dh1b, wo_ref[...]).astype(BF16)

    row = lambda m: (m, 0)
    fix2 = lambda m: (0, 0)
    return pl.pallas_call(
        body, name="layer_b_in_bwd", grid=(t // tm,),
        in_specs=[pl.BlockSpec((2, tm, D_MODEL), lambda m: (0, m, 0)),
                  pl.BlockSpec((2, tm, 128), lambda m: (0, m, 0)),
                  pl.BlockSpec((nb, D_MODEL, tn), lambda m: (0, 0, 0)),
                  pl.BlockSpec((D_MODEL, 256), fix2),
                  pl.BlockSpec((tm, D_MODEL), row), pl.BlockSpec((tm, D_MODEL), row),
                  pl.BlockSpec((1, D_MODEL), fix2), pl.BlockSpec((1, D_MODEL), fix2),
                  pl.BlockSpec((D_MODEL, D_MODEL), fix2)],
        out_specs=[pl.BlockSpec((tm, D_MODEL), row), pl.BlockSpec((tm, D_MODEL), row),
                   pl.BlockSpec((tm, D_MODEL), row), pl.BlockSpec((1, D_MODEL), fix2),
                   pl.BlockSpec((1, D_MODEL), fix2)],
        out_shape=[SDS((t, D_MODEL), F32), SDS((t, D_MODEL), BF16), SDS((t, D_MODEL), BF16),
                   SDS((1, D_MODEL), F32), SDS((1, D_MODEL), F32)],
        compiler_params=_cparams(),
    )(dqg, dkv, w_in_b, kv_w, h1, dh2, b_gain, kv_gain, w_out_a)


def _layer_a_in_bwd(dqg, dkv, w_in_a, x, dh1, a_gain, chip_sums):
    t = x.shape[0]
    tm = min(t, 512)
    nb, _, tn = w_in_a.shape
    per = D_MODEL // tn

    def body(dqg_ref, dkv_ref, w_ref, x_ref, dh1_ref, ag_ref, sums_ref, dx_ref, dan_ref, land_ref,
             send_sems, recv_sems):
        @pl.when(pl.program_id(0) == 0)
        def _():
            dan_ref[...] = jnp.zeros_like(dan_ref)
            for cp in _chip_copies(sums_ref, land_ref, send_sems, recv_sems):
                cp.start()

        dxn = jnp.zeros((tm, D_MODEL), F32)
        for i in range(nb):
            part = i // per
            src = dqg_ref if part in (0, 3) else dkv_ref
            outer = {0: 0, 3: 1, 1: 0, 2: 1}[part]
            blk = src[outer, :, (i % per) * tn:(i % per + 1) * tn]
            dxn = dxn + _dot_nt(blk, w_ref[i])
        xf = x_ref[...]
        r = _rstd(xf)
        dan_ref[...] += jnp.sum(dxn * (xf * r), axis=0, keepdims=True)
        u = dxn * ag_ref[...]
        dx_ref[...] = dh1_ref[...] + r * u - xf * ((r * r * r) * jnp.mean(u * xf, axis=-1, keepdims=True))

        @pl.when(pl.program_id(0) == t // tm - 1)
        def _():
            for cp in _chip_copies(sums_ref, land_ref, send_sems, recv_sems):
                cp.wait()

    row = lambda m: (m, 0)
    fix2 = lambda m: (0, 0)
    return pl.pallas_call(
        body, name="layer_a_in_bwd", grid=(t // tm,),
        in_specs=[pl.BlockSpec((2, tm, D_MODEL), lambda m: (0, m, 0)),
                  pl.BlockSpec((2, tm, D_MODEL), lambda m: (0, m, 0)),
                  pl.BlockSpec((nb, D_MODEL, tn), lambda m: (0, 0, 0)),
                  pl.BlockSpec((tm, D_MODEL), row), pl.BlockSpec((tm, D_MODEL), row),
                  pl.BlockSpec((1, D_MODEL), fix2), ANY],
        out_specs=[pl.BlockSpec((tm, D_MODEL), row), pl.BlockSpec((1, D_MODEL), fix2), ANY],
        out_shape=[SDS((t, D_MODEL), F32), SDS((1, D_MODEL), F32), SDS(chip_sums.shape, chip_sums.dtype)],
        scratch_shapes=[pltpu.SemaphoreType.DMA((3,)), pltpu.SemaphoreType.DMA((3,))],
        compiler_params=_cparams(),
    )(dqg, dkv, w_in_a, x, dh1, a_gain, chip_sums)


def _lut(s, vals):
    r = jnp.int32(vals[0])
    for i in range(1, len(vals)):
        r = jnp.where(s == i, jnp.int32(vals[i]), r)
    return r


def _held(steps, i):
    seq, cur = [None] * len(steps), None
    for k in range(len(steps) - 1, -1, -1):
        if steps[k][0] == i:
            cur = steps[k][1:3]
        seq[k] = cur
    for k in range(len(steps)):
        cur = seq[k] = seq[k] if seq[k] is not None else cur
    return seq


def _weight_grad_cols(name, my_slot, a, bs, steps, tn):
    t, dw = a.shape
    n_arr = len(bs)
    which = [s[0] for s in steps]
    blks = [s[3] for s in steps]

    def body(slot_ref, a_ref, *rest):
        b_refs, (o_ref, own_ref, at_ref) = rest[:n_arr], rest[n_arr:]
        s = pl.program_id(0)

        @pl.when(s == 0)
        def _():
            at_ref[...] = a_ref[...].T

        for i in range(n_arr):
            @pl.when(_lut(s, which) == i)
            def _(i=i):
                res = _dot(at_ref[...], b_refs[i][0])
                o_ref[0] = res.astype(BF16)

                @pl.when(_lut(s, blks) == slot_ref[0])
                def _():
                    own_ref[...] = res

    def b_spec(i):
        held = _held(steps, i)
        return pl.BlockSpec((1, t, tn), lambda s, slot: (_lut(s, [h[0] for h in held]), 0,
                                                         _lut(s, [h[1] for h in held])))

    return pl.pallas_call(
        body, name=name,
        grid_spec=pltpu.PrefetchScalarGridSpec(
            num_scalar_prefetch=1, grid=(len(steps),),
            in_specs=[pl.BlockSpec((t, dw), lambda s, slot: (0, 0))] + [b_spec(i) for i in range(n_arr)],
            out_specs=[pl.BlockSpec((1, dw, tn), lambda s, slot: (_lut(s, blks), 0, 0)),
                       pl.BlockSpec((dw, tn), lambda s, slot: (0, 0))],
            scratch_shapes=[pltpu.VMEM((dw, t), BF16)]),
        out_shape=[SDS((N_DEV, dw, tn), BF16), SDS((dw, tn), F32)],
        compiler_params=_cparams(),
    )(my_slot, a, *bs)


def _weight_grad_rows(name, my_slot, a, b):
    t, dw = a.shape
    n_o, _, c = b.shape
    rows = dw // N_DEV
    tn = min(c, 256)
    per = c // tn

    def body(slot_ref, a_ref, b_ref, o_ref, own_ref, at_ref, res_ref):
        @pl.when(pl.program_id(0) == 0)
        def _():
            at_ref[...] = a_ref[...].T

        res_ref[...] = _dot(at_ref[...], b_ref[0].astype(BF16))
        o_ref[...] = res_ref[...].astype(BF16)
        own_ref[...] = res_ref[pl.ds(pl.multiple_of(slot_ref[0] * rows, rows), rows), :]

    all_rows, own = pl.pallas_call(
        body, name=name,
        grid_spec=pltpu.PrefetchScalarGridSpec(
            num_scalar_prefetch=1, grid=(n_o * per,),
            in_specs=[pl.BlockSpec((t, dw), lambda s, slot: (0, 0)),
                      pl.BlockSpec((1, t, tn), lambda s, slot: (s // per, 0, s % per))],
            out_specs=[pl.BlockSpec((dw, tn), lambda s, slot: (0, s)),
                       pl.BlockSpec((rows, tn), lambda s, slot: (0, s))],
            scratch_shapes=[pltpu.VMEM((dw, t), BF16), pltpu.VMEM((dw, tn), F32)]),
        out_shape=[SDS((dw, n_o * c), BF16), SDS((rows, n_o * c), F32)],
        compiler_params=_cparams(),
    )(my_slot, a, b)
    return all_rows.reshape(N_DEV, rows, n_o * c), own


def _lane_lo():
    return lax.broadcasted_iota(jnp.int32, (1, 128), 1) < HEAD_DIM


def _collapse_chunks(ds, keys):
    if ds.shape[1] < keys:
        ds = jnp.concatenate([jnp.zeros((ds.shape[0], keys - ds.shape[1]), F32), ds], axis=1)
    gc = ds[0:CHUNK]
    for cc in range(1, ds.shape[0] // CHUNK):
        gc = gc + pltpu.roll(ds[cc * CHUNK:(cc + 1) * CHUNK], keys - cc * CHUNK, 1)
    return gc


def _offset_sums(gc):
    hi = gc.astype(BF16)
    lo = (gc - hi.astype(F32)).astype(BF16)
    flip = (lax.broadcasted_iota(jnp.int32, (CHUNK, CHUNK), 0)
            + lax.broadcasted_iota(jnp.int32, (CHUNK, CHUNK), 1) == CHUNK - 1).astype(BF16)
    gf = _dot(flip, hi) + _dot(flip, lo)
    skew = pltpu.roll(gf, 0, 1, stride=1, stride_axis=0)
    return jnp.sum(skew, axis=0, keepdims=True)


def _band_bias(w_row, band, rows):
    keys = w_row.shape[1]
    base = jnp.broadcast_to(w_row, (CHUNK, keys))
    skew = pltpu.roll(base, 0, 1, stride=1, stride_axis=0)
    skew = pltpu.roll(skew, keys - (CHUNK - 1), 1)
    col = lax.broadcasted_iota(jnp.int32, (CHUNK, keys), 1)
    chunk0 = jnp.where(col < band, skew, NEG)
    return jnp.concatenate(
        [chunk0] + [pltpu.roll(chunk0, cc * CHUNK, 1) for cc in range(1, rows // CHUNK)], axis=0)


def _silu_parts(g):
    sg = _sigmoid(g)
    return g * sg, sg * (1.0 + g * (1.0 - sg))


A_PAIRS_FWD = 8
A_PAIRS_BWD = 4


def _a_specs(pairs):
    lanes = 128 * pairs
    steps = D_MODEL // lanes
    q = pl.BlockSpec((QBLK, lanes), lambda p, j: (j, p))
    ks = [pl.BlockSpec((QBLK, lanes), lambda p, j, b=b: (jnp.maximum(j - 2 + b, 0), steps + p)) for b in range(3)]
    vs = [pl.BlockSpec((QBLK, lanes), lambda p, j, b=b: (jnp.maximum(j - 2 + b, 0), 2 * steps + p))
          for b in range(3)]
    g = pl.BlockSpec((QBLK, lanes), lambda p, j: (j, 3 * steps + p))
    bias = pl.BlockSpec((pairs, 8, A_KEYS), lambda p, j: (p, 0, 0))
    return q, ks, vs, g, bias


def _a_fill_bias(w_ref, b_ref, j, pairs):
    _fill_bias(2 * pairs, lambda h: w_ref[h // 2, h % 2:h % 2 + 1, :], A_BAND, b_ref, j)


def _by_valid_key_blocks(j, fn):
    pl.when(j == 0)(functools.partial(fn, 1))
    pl.when(j == 1)(functools.partial(fn, 2))
    pl.when(j >= 2)(functools.partial(fn, 3))


def _fill_bias(n, get_row, band, bias_scr, j):
    @pl.when(j == 0)
    def _():
        for h in range(n):
            bias_scr[h] = _band_bias(get_row(h), band, bias_scr.shape[1])


def _normalise_pair(rs, mxs, lane_lo, extra=None):
    num = jnp.where(lane_lo, rs[0], rs[1])
    den = pltpu.roll(jnp.where(lane_lo, rs[1], rs[0]), HEAD_DIM, 1)
    if extra is not None:
        den = den + jnp.where(lane_lo, extra[0], extra[1])
    return num / den, jnp.where(lane_lo, mxs[0], mxs[1]) + jnp.log(den)


def _own_everywhere(x, sel):
    return jnp.where(sel, x, pltpu.roll(x, HEAD_DIM, 1))


def _minus_rows(s, row_full):
    return jnp.concatenate([s[:, i:i + 128] - row_full for i in range(0, s.shape[1], 128)], axis=1)


def _attn_a_fwd(qkvg, bias, gather):
    t = qkvg.shape[0]
    nq = t // QBLK
    n_g = len(gather)
    pairs = A_PAIRS_FWD
    lanes = 128 * pairs
    steps = D_MODEL // lanes
    q_spec, k_specs, v_specs, g_spec, bias_spec = _a_specs(pairs)

    def body(q_ref, k0, k1, k2, v0, v1, v2, g_ref, w_ref, *rest):
        shard_refs, rest = rest[:n_g], rest[n_g:]
        z_ref, o_ref, lse_ref = rest[:3]
        full_refs, (b_ref, *comm) = rest[3:3 + n_g], rest[3 + n_g:]
        p = pl.program_id(0)
        j = pl.program_id(1)
        start, forward, finish = _gather_phases(shard_refs, full_refs, *comm)
        at = p * nq + j
        pl.when(at == 0)(start)
        pl.when(at == steps * nq // 2)(forward)
        _a_fill_bias(w_ref, b_ref, j, pairs)
        lane_lo = _lane_lo()
        sels = (lane_lo, jnp.logical_not(lane_lo))

        def attend(n_blocks):
            first_col = (3 - n_blocks) * QBLK
            for pp in range(pairs):
                cols = slice(128 * pp, 128 * (pp + 1))
                k = jnp.concatenate([r[:, cols] for r in (k0, k1, k2)[3 - n_blocks:]], axis=0)
                v = jnp.concatenate([r[:, cols] for r in (v0, v1, v2)[3 - n_blocks:]], axis=0)
                q = q_ref[:, cols]
                qm2 = jnp.concatenate([jnp.where(sel, q, jnp.zeros_like(q)) for sel in sels], axis=0) * SCALE
                s2 = _dot_nt(qm2, k)
                rs, mxs = [], []
                for hh, sel in enumerate(sels):
                    s = s2[hh * QBLK:(hh + 1) * QBLK] + b_ref[2 * pp + hh, :, first_col:]
                    mxs.append(jnp.max(s, axis=-1, keepdims=True))
                    e = jnp.exp(s - mxs[hh]).astype(BF16)
                    rs.append(_dot(e, jnp.where(sel, v, jnp.ones_like(v))))
                o, lse = _normalise_pair(rs, mxs, lane_lo)
                silu, _ = _silu_parts(g_ref[:, cols].astype(F32))
                o_ref[:, cols] = o.astype(BF16)
                z_ref[:, cols] = (o * silu).astype(BF16)
                lse_ref[:, cols] = lse

        _by_valid_key_blocks(j, attend)
        pl.when(at == steps * nq - 1)(finish)

    out_spec = pl.BlockSpec((QBLK, lanes), lambda p, j: (j, p))
    outs = pl.pallas_call(
        body, name="attn_a_fwd", grid=(steps, nq),
        in_specs=[q_spec, *k_specs, *v_specs, g_spec, bias_spec] + [ANY] * n_g,
        out_specs=[out_spec, out_spec, out_spec] + [ANY] * n_g,
        out_shape=[SDS((t, D_MODEL), BF16), SDS((t, D_MODEL), BF16), SDS((t, D_MODEL), F32)]
        + [SDS((N_DEV, *s.shape), s.dtype) for s in gather],
        scratch_shapes=[pltpu.VMEM((2 * pairs, QBLK, A_KEYS), F32)] + _gather_scratch(n_g),
        compiler_params=_cparams(),
    )(qkvg, qkvg, qkvg, qkvg, qkvg, qkvg, qkvg, qkvg, bias, *gather)
    return outs[0], outs[1], outs[2], list(outs[3:])


def _attn_a_bwd(qkvg, bias, out_a, lse, dz, scatter):
    t = qkvg.shape[0]
    nq = t // QBLK
    n_sc = len(scatter)
    pairs = A_PAIRS_BWD
    lanes = 128 * pairs
    steps = D_MODEL // lanes
    q_spec, k_specs, v_specs, g_spec, bias_spec = _a_specs(pairs)

    def body(q_ref, k0, k1, k2, v0, v1, v2, g_ref, w_ref, o_ref, lse_ref, dz_ref, *rest):
        sc_refs, rest = rest[:n_sc], rest[n_sc:]
        dqg_ref, dkv_ref, dg_ref = rest[:3]
        land_refs, rest = rest[3:3 + n_sc], rest[3 + n_sc:]
        dk_acc, dv_acc, gt_acc, b_ref, send_sems, recv_sems = rest
        j = pl.program_id(1)
        first = jnp.logical_and(pl.program_id(0) == 0, j == 0)
        last = jnp.logical_and(pl.program_id(0) == steps - 1, j == nq - 1)

        @pl.when(first)
        def _():
            for cp in _scatter_copies(sc_refs, land_refs, send_sems, recv_sems):
                cp.start()

        _a_fill_bias(w_ref, b_ref, j, pairs)

        @pl.when(j == 0)
        def _():
            dk_acc[...] = jnp.zeros_like(dk_acc)
            dv_acc[...] = jnp.zeros_like(dv_acc)
            gt_acc[...] = jnp.zeros_like(gt_acc)

        lane_lo = _lane_lo()
        sels = (lane_lo, jnp.logical_not(lane_lo))

        def attend(n_blocks):
            first_col = (3 - n_blocks) * QBLK
            for pp in range(pairs):
                cols = slice(128 * pp, 128 * (pp + 1))
                q = q_ref[:, cols]
                k = jnp.concatenate([r[:, cols] for r in (k0, k1, k2)[3 - n_blocks:]], axis=0)
                v = jnp.concatenate([r[:, cols] for r in (v0, v1, v2)[3 - n_blocks:]], axis=0)
                o = o_ref[:, cols].astype(F32)
                lse_pair = lse_ref[:, cols]
                dzf = dz_ref[:, cols].astype(F32)
                silu, dsilu = _silu_parts(g_ref[:, cols].astype(F32))
                do = dzf * silu
                dqg_ref[1, :, cols] = (dzf * o * dsilu).astype(BF16)
                doo = do * o
                qm2 = jnp.concatenate([jnp.where(sel, q, jnp.zeros_like(q)) for sel in sels], axis=0) * SCALE
                dom2 = jnp.concatenate([jnp.where(sel, do, 0.0) for sel in sels], axis=0).astype(BF16)
                s2 = _dot_nt(qm2, k)
                dp2 = _dot_nt(dom2, v)
                ps, dss = [], []
                for hh, sel in enumerate(sels):
                    rows = slice(hh * QBLK, (hh + 1) * QBLK)
                    s = s2[rows] + b_ref[2 * pp + hh, :, first_col:]
                    p = jnp.exp(_minus_rows(s, _own_everywhere(lse_pair, sel)))
                    delta = jnp.sum(jnp.where(sel, doo, 0.0), axis=-1, keepdims=True)
                    ds = p * (dp2[rows] - delta)
                    gt_acc[2 * pp + hh] += _collapse_chunks(ds, A_KEYS)
                    ps.append(p.astype(BF16))
                    dss.append(ds.astype(BF16))
                dsb2 = jnp.concatenate(dss, axis=0)
                dq2 = _dot(dsb2, k) * SCALE
                dk_blk = _dot_tn(dsb2, qm2)
                dv_blk = _dot_tn(jnp.concatenate(ps, axis=0), dom2)
                dqg_ref[0, :, cols] = jnp.where(lane_lo, dq2[0:QBLK], dq2[QBLK:2 * QBLK]).astype(BF16)
                for b in range(n_blocks):
                    rows = pl.ds(pl.multiple_of((j - n_blocks + 1 + b) * QBLK, QBLK), QBLK)
                    dk_acc[rows, cols] += dk_blk[b * QBLK:(b + 1) * QBLK]
                    dv_acc[rows, cols] += dv_blk[b * QBLK:(b + 1) * QBLK]

        _by_valid_key_blocks(j, attend)

        @pl.when(j == nq - 1)
        def _():
            dkv_ref[0] = dk_acc[...].astype(BF16)
            dkv_ref[1] = dv_acc[...].astype(BF16)
            for pp in range(pairs):
                dg_ref[pp] = jnp.concatenate([_offset_sums(gt_acc[2 * pp]), _offset_sums(gt_acc[2 * pp + 1]),
                                              jnp.zeros((6, A_DIAG), F32)], axis=0)

        @pl.when(last)
        def _():
            for cp in _scatter_copies(sc_refs, land_refs, send_sems, recv_sems):
                cp.wait()

    blk = pl.BlockSpec((QBLK, lanes), lambda p, j: (j, p))
    outs = pl.pallas_call(
        body, name="attn_a_bwd", grid=(steps, nq),
        in_specs=[q_spec, *k_specs, *v_specs, g_spec, bias_spec, blk, blk, blk] + [ANY] * n_sc,
        out_specs=[pl.BlockSpec((2, QBLK, lanes), lambda p, j: (0, j, p)),
                   pl.BlockSpec((2, t, lanes), lambda p, j: (0, 0, p)),
                   pl.BlockSpec((pairs, 8, A_DIAG), lambda p, j: (p, 0, 0))] + [ANY] * n_sc,
        out_shape=[SDS((2, t, D_MODEL), BF16), SDS((2, t, D_MODEL), BF16), SDS((N_HEADS // 2, 8, A_DIAG), F32)]
        + [SDS((N_DEV - 1, *g.shape[1:]), g.dtype) for g in scatter],
        scratch_shapes=[pltpu.VMEM((t, lanes), F32), pltpu.VMEM((t, lanes), F32),
                        pltpu.VMEM((2 * pairs, CHUNK, A_KEYS), F32), pltpu.VMEM((2 * pairs, QBLK, A_KEYS), F32),
                        pltpu.SemaphoreType.DMA(((N_DEV - 1) * n_sc,)),
                        pltpu.SemaphoreType.DMA(((N_DEV - 1) * n_sc,))],
        compiler_params=_cparams(),
    )(qkvg, qkvg, qkvg, qkvg, qkvg, qkvg, qkvg, qkvg, bias, out_a, lse, dz, *scatter)
    return outs[0], outs[1], outs[2], list(outs[3:])


def _b_specs(qblk):
    per = qblk // B_PREV
    q = pl.BlockSpec((qblk, 512), lambda h, j: (j, h))
    g = pl.BlockSpec((qblk, 512), lambda h, j: (j, 2 + h))
    kp = pl.BlockSpec((B_PREV, 128), lambda h, j: (jnp.maximum(per * j - 1, 0), 0))
    kc = pl.BlockSpec((qblk, 128), lambda h, j: (j, 0))
    vp = pl.BlockSpec((B_PREV, 128), lambda h, j: (jnp.maximum(per * j - 1, 0), 1))
    vc = pl.BlockSpec((qblk, 128), lambda h, j: (j, 1))
    bias = pl.BlockSpec((B_GROUP, qblk + B_PREV), lambda h, j: (h, 0))
    sinks = pl.BlockSpec(memory_space=pltpu.SMEM)
    return q, g, kp, kc, vp, vc, bias, sinks


def _b_operands(kp, kc, vp, vc, kvh, with_prev):
    k = jnp.concatenate([kp[...], kc[...]], axis=0) if with_prev else kc[...]
    v = jnp.concatenate([vp[...], vc[...]], axis=0) if with_prev else vc[...]
    kr = pltpu.roll(k, HEAD_DIM, 1)
    vr = pltpu.roll(v, HEAD_DIM, 1)
    first = kvh == 0
    return (jnp.where(first, k, kr), jnp.where(first, kr, k),
            jnp.where(first, v, vr), jnp.where(first, vr, v))


def _attn_b_fwd(qg, kv, bias, sinks):
    t = qg.shape[0]
    qblk = B_QBLK_FWD
    per_step = 4
    step = per_step * qblk
    q_spec, g_spec, kp_spec, kc_spec, vp_spec, vc_spec, _, sink_spec = _b_specs(step)
    bias_spec = pl.BlockSpec((B_GROUP, qblk + B_PREV), lambda h, j: (h, 0))

    def body(q_ref, g_ref, kp, kc, vp, vc, w_ref, sink_ref, z_ref, o_ref, lse_ref, b_ref):
        kvh = pl.program_id(0)
        j = pl.program_id(1)
        _fill_bias(B_GROUP, lambda h: w_ref[h:h + 1, :], B_BAND, b_ref, j)
        lane_lo = _lane_lo()
        n_pairs = B_GROUP // 2

        def attend(first):
            k_lo, k_hi, v_lo, v_hi = _b_operands(kp, kc, vp, vc, kvh, True)
            for sb in range(per_step):
                no_prev = first and sb == 0
                first_col = B_PREV if no_prev else 0
                keys = slice(sb * qblk + first_col, (sb + 1) * qblk + B_PREV)
                qrows = slice(sb * qblk, (sb + 1) * qblk)
                halves = []
                for hh, sel in enumerate((lane_lo, jnp.logical_not(lane_lo))):
                    kk = (k_lo if hh == 0 else k_hi)[keys]
                    vv = (v_lo if hh == 0 else v_hi)[keys]
                    qm4 = jnp.concatenate(
                        [jnp.where(sel, q_ref[qrows, 128 * pp:128 * (pp + 1)], jnp.zeros((qblk, 128), BF16))
                         for pp in range(n_pairs)], axis=0) * SCALE
                    s4 = _dot_nt(qm4, kk)
                    es, mxs = [], []
                    for pp in range(n_pairs):
                        g = 2 * pp + hh
                        s = s4[pp * qblk:(pp + 1) * qblk] + b_ref[g, :, first_col:]
                        mxs.append(jnp.maximum(jnp.max(s, axis=-1, keepdims=True), sink_ref[kvh * B_GROUP + g]))
                        es.append(jnp.exp(s - mxs[pp]).astype(BF16))
                    r4 = _dot(jnp.concatenate(es, axis=0), jnp.where(sel, vv, jnp.ones_like(vv)))
                    halves.append((r4, mxs))
                for pp in range(n_pairs):
                    cols = slice(128 * pp, 128 * (pp + 1))
                    rows = slice(pp * qblk, (pp + 1) * qblk)
                    mxs = [halves[hh][1][pp] for hh in range(2)]
                    sink_terms = [jnp.exp(sink_ref[kvh * B_GROUP + 2 * pp + hh] - mxs[hh]) for hh in range(2)]
                    o, lse = _normalise_pair([halves[hh][0][rows] for hh in range(2)], mxs, lane_lo, sink_terms)
                    silu, _ = _silu_parts(g_ref[qrows, cols].astype(F32))
                    o_ref[qrows, cols] = o.astype(BF16)
                    z_ref[qrows, cols] = (o * silu).astype(BF16)
                    lse_ref[qrows, cols] = lse

        pl.when(j == 0)(functools.partial(attend, True))
        pl.when(j >= 1)(functools.partial(attend, False))

    out_spec = pl.BlockSpec((step, 512), lambda h, j: (j, h))
    return pl.pallas_call(
        body, name="attn_b_fwd", grid=(B_KV_HEADS, t // step),
        in_specs=[q_spec, g_spec, kp_spec, kc_spec, vp_spec, vc_spec, bias_spec, sink_spec],
        out_specs=[out_spec, out_spec, out_spec],
        out_shape=[SDS((t, D_MODEL), BF16), SDS((t, D_MODEL), BF16), SDS((t, D_MODEL), F32)],
        scratch_shapes=[pltpu.VMEM((B_GROUP, qblk, qblk + B_PREV), F32)],
        compiler_params=_cparams(),
    )(qg, qg, kv, kv, kv, kv, bias, sinks)


def _attn_b_bwd(qg, kv, bias, sinks, out_b, lse, dz, bucket_onehot):
    t = qg.shape[0]
    qblk = B_QBLK_BWD
    keys = qblk + B_PREV
    nq = t // qblk
    per = qblk // B_PREV

    def body(q_ref, g_ref, kp, kc, vp, vc, w_ref, sink_ref, o_ref, lse_ref, dz_ref, oh_ref,
             dqg_ref, dkv_ref, dt5_ref, dsink_ref, gt_acc, b_ref):
        j = pl.program_id(0)
        _fill_bias(N_HEADS, lambda h: w_ref[h:h + 1, :], B_BAND, b_ref, j)

        @pl.when(j == 0)
        def _():
            dkv_ref[...] = jnp.zeros_like(dkv_ref)
            gt_acc[...] = jnp.zeros_like(gt_acc)
            dsink_ref[...] = jnp.zeros_like(dsink_ref)

        lane_lo = _lane_lo()

        def attend(with_prev):
            first_col = 0 if with_prev else B_PREV
            dk_add = jnp.zeros((keys - first_col, 128), F32)
            dv_add = jnp.zeros((keys - first_col, 128), F32)
            for kvh in range(B_KV_HEADS):
                k_lo, k_hi, v_lo, v_hi = _b_operands(kp, kc, vp, vc, kvh, with_prev)
                dk_blk = jnp.zeros((keys - first_col, 128), F32)
                dv_blk = jnp.zeros((keys - first_col, 128), F32)
                for pp in range(B_GROUP // 2):
                    cols = slice(512 * kvh + 128 * pp, 512 * kvh + 128 * (pp + 1))
                    qp = q_ref[:, cols]
                    o = o_ref[:, cols].astype(F32)
                    lse_pair = lse_ref[:, cols]
                    dzf = dz_ref[:, cols].astype(F32)
                    silu, dsilu = _silu_parts(g_ref[:, cols].astype(F32))
                    do = dzf * silu
                    dqg_ref[1, :, cols] = (dzf * o * dsilu).astype(BF16)
                    doo = do * o
                    dqs = []
                    for hh in range(2):
                        g = kvh * B_GROUP + 2 * pp + hh
                        sel = lane_lo if hh == 0 else jnp.logical_not(lane_lo)
                        kk = k_lo if hh == 0 else k_hi
                        vv = v_lo if hh == 0 else v_hi
                        qm = jnp.where(sel, qp, jnp.zeros_like(qp)) * SCALE
                        s = _dot_nt(qm, kk) + b_ref[g, :, first_col:]
                        lse_h = _own_everywhere(lse_pair, sel)
                        p = jnp.exp(_minus_rows(s, lse_h))
                        delta = jnp.sum(jnp.where(sel, doo, 0.0), axis=-1, keepdims=True)
                        dom = jnp.where(sel, do, 0.0).astype(BF16)
                        dp = _dot_nt(dom, vv)
                        ds = p * (dp - delta)
                        gt_acc[g, :, first_col:] += ds
                        dsink_ref[g:g + 1, :] -= jnp.sum(jnp.exp(sink_ref[g] - lse_h) * delta, axis=0, keepdims=True)
                        dsb = ds.astype(BF16)
                        dqs.append(_dot(dsb, kk) * SCALE)
                        dk_blk = dk_blk + _dot_tn(dsb, qm)
                        dv_blk = dv_blk + _dot_tn(p.astype(BF16), dom)
                    dqg_ref[0, :, cols] = jnp.where(lane_lo, dqs[0], dqs[1]).astype(BF16)
                mine = lane_lo if kvh == 0 else jnp.logical_not(lane_lo)
                dk_add = dk_add + jnp.where(mine, dk_blk + pltpu.roll(dk_blk, HEAD_DIM, 1), 0.0)
                dv_add = dv_add + jnp.where(mine, dv_blk + pltpu.roll(dv_blk, HEAD_DIM, 1), 0.0)
            first_key = B_PREV if with_prev else 0
            if with_prev:
                rows = pl.ds(pl.multiple_of(j * qblk - B_PREV, B_PREV), B_PREV)
                dkv_ref[0, rows, :] += dk_add[0:B_PREV]
                dkv_ref[1, rows, :] += dv_add[0:B_PREV]
            rows = pl.ds(pl.multiple_of(j * qblk, qblk), qblk)
            dkv_ref[0, rows, :] += dk_add[first_key:first_key + qblk]
            dkv_ref[1, rows, :] += dv_add[first_key:first_key + qblk]

        pl.when(j == 0)(functools.partial(attend, False))
        pl.when(j >= 1)(functools.partial(attend, True))

        @pl.when(j == nq - 1)
        def _():
            dd = jnp.concatenate([_offset_sums(_collapse_chunks(gt_acc[g], keys)) for g in range(N_HEADS)], axis=0)
            hi = dd.astype(BF16)
            lo = (dd - hi.astype(F32)).astype(BF16)
            dt5_ref[...] = _dot(hi, oh_ref[...]) + _dot(lo, oh_ref[...])

    wide = lambda col: pl.BlockSpec((qblk, D_MODEL), lambda j, col=col: (j, col))
    prev = lambda col: pl.BlockSpec((B_PREV, 128), lambda j, col=col: (jnp.maximum(per * j - 1, 0), col))
    cur = lambda col: pl.BlockSpec((qblk, 128), lambda j, col=col: (j, col))
    fixed = lambda shape: pl.BlockSpec(shape, lambda j: (0,) * len(shape))
    return pl.pallas_call(
        body, name="attn_b_bwd", grid=(nq,),
        in_specs=[wide(0), wide(1), prev(0), cur(0), prev(1), cur(1), fixed((N_HEADS, keys)),
                  pl.BlockSpec(memory_space=pltpu.SMEM), wide(0), wide(0), wide(0), fixed((keys, 128))],
        out_specs=[pl.BlockSpec((2, qblk, D_MODEL), lambda j: (0, j, 0)), fixed((2, t, 128)),
                   fixed((N_HEADS, 128)), fixed((N_HEADS, 128))],
        out_shape=[SDS((2, t, D_MODEL), BF16), SDS((2, t, 128), F32),
                   SDS((N_HEADS, 128), F32), SDS((N_HEADS, 128), F32)],
        scratch_shapes=[pltpu.VMEM((N_HEADS, qblk, keys), F32), pltpu.VMEM((N_HEADS, qblk, keys), F32)],
        compiler_params=_cparams(),
    )(qg, qg, kv, kv, kv, kv, bias, sinks, out_b, lse, dz, bucket_onehot)


def _a_bias_by_offset(rel_bias):
    m = np.arange(A_DIAG)
    idx = np.clip(A_BAND - 1 - m, -A_REL_CLIP, A_REL_CLIP) + A_REL_CLIP
    by_head = rel_bias[idx].T.reshape(N_HEADS // 2, 2, A_DIAG)
    return jnp.concatenate([by_head, jnp.zeros((N_HEADS // 2, 6, A_DIAG), F32)], axis=1)


def _a_bias_grad(offset_sums):
    first = 319
    tail = jnp.sum(offset_sums[:, :first], axis=1)
    body = jnp.flip(offset_sums[:, first:first + 320], axis=1)
    body = body.at[:, -1].add(tail)
    full = jnp.concatenate([jnp.zeros((N_HEADS, 193), F32), body], axis=1)
    return full


def _t5_bucket(rel):
    nb = T5_BUCKETS // 2
    max_exact = nb // 2
    ret = jnp.where(rel > 0, nb, 0)
    n = jnp.abs(rel)
    nf = jnp.maximum(n, 1).astype(jnp.float32)
    large = max_exact + (jnp.log(nf / max_exact) / math.log(T5_MAX_DIST / max_exact)
                         * (nb - max_exact)).astype(jnp.int32)
    large = jnp.minimum(large, nb - 1)
    return ret + jnp.where(n < max_exact, n, large)


def _b_offset_buckets(keys):
    return _t5_bucket(jnp.arange(keys, dtype=jnp.int32) - (B_LEFT_CHUNKS * CHUNK + CHUNK - 1))


def _b_bias_by_offset(t5_table, keys):
    return t5_table[_b_offset_buckets(keys)].T


def _b_bucket_onehot(keys):
    return (_b_offset_buckets(keys)[:, None] == jnp.arange(128)[None, :]).astype(BF16)


def _local_step(my_slot, order, x, target, a_gain_shard, w_in_a_shard, rel_bias, late_shards, kv_gain,
                t5_table, b_gain, sinks, f_gain):
    a_bias = _a_bias_by_offset(rel_bias)
    b_bias_fwd = _b_bias_by_offset(t5_table, B_QBLK_FWD + B_PREV)
    b_bias_bwd = _b_bias_by_offset(t5_table, B_QBLK_BWD + B_PREV)
    sinks_flat = sinks.reshape(N_HEADS)

    xn, qkvg, w_in_a, a_gain = _norm_matmul_gather(order, x, a_gain_shard, w_in_a_shard)
    z_a, out_a, lse_a, (w_in_b, w_out_a, w_out_b, kv_w) = _attn_a_fwd(qkvg, a_bias, late_shards)
    w_out_a = w_out_a.reshape(D_MODEL, D_MODEL)
    w_out_b = w_out_b.reshape(D_MODEL, D_MODEL)
    kv_w = kv_w.reshape(D_MODEL, 2 * 128)
    h1, kvn, hb, kv, qg = _layer_a_out(x, z_a, w_out_a, kv_gain, b_gain, kv_w, w_in_b)
    z_b, out_b, lse_b = _attn_b_fwd(qg, kv, b_bias_fwd, sinks_flat)
    dh2, dh2b, dz_b, loss, d_fn = _layer_b_out_loss(h1, z_b, w_out_b, f_gain, target)

    dqg_b, dkv_b, d_t5, d_sink = _attn_b_bwd(qg, kv, b_bias_bwd, sinks_flat, out_b, lse_b, dz_b,
                                             _b_bucket_onehot(B_QBLK_BWD + B_PREV))
    dh1, dh1b, dz_a, d_bn, d_kn = _layer_b_in_bwd(dqg_b, dkv_b, w_in_b, kv_w, h1, dh2, b_gain, kv_gain, w_out_a)
    early = dict(
        b_w_out=_weight_grad_rows("grad_b_w_out", my_slot, z_b, dh2b[None]),
        b_w_in=_weight_grad_cols("grad_b_w_in", my_slot, hb, [dqg_b],
                                 [(0, o, c, 4 * o + c) for o in range(2) for c in range(4)], 256),
        kv_w=_weight_grad_rows("grad_kv_w", my_slot, kvn, dkv_b),
        a_w_out=_weight_grad_rows("grad_a_w_out", my_slot, z_a, dh1b[None]))
    dqg_a, dkv_a, d_rel, landed = _attn_a_bwd(qkvg, a_bias, out_a, lse_a, dz_a, [g[0] for g in early.values()])
    g_w_in_a = _weight_grad_cols(
        "grad_a_w_in", my_slot, xn, [dqg_a, dkv_a],
        [(0, 0, 0, 0), (0, 0, 1, 1), (1, 0, 0, 2), (1, 0, 1, 3), (1, 1, 0, 4), (1, 1, 1, 5), (0, 1, 0, 6), (0, 1, 1, 7)], 512)
    chip_sums, from_sibling = _chip_sums(g_w_in_a[0])
    grad_x, d_an, from_chips = _layer_a_in_bwd(dqg_a, dkv_a, w_in_a, x, dh1, a_gain, chip_sums)

    matrices = {n: (g[1], [(land, 0, N_DEV - 1)]) for (n, g), land in zip(early.items(), landed)}
    matrices["a_w_in"] = (g_w_in_a[1], [(from_sibling, 0, 1), (from_chips, 0, 3)])
    small = dict(
        loss=loss, a_norm=d_an, a_rel_bias=d_rel[:, :2].reshape(N_HEADS, A_DIAG),
        kv_norm=d_kn, t5_bias=d_t5, b_norm=d_bn, b_sinks=d_sink, final_norm=d_fn)
    return grad_x, small, matrices


def _place():
    x, y, c = lax.axis_index("x"), lax.axis_index("y"), lax.axis_index("c")
    chips = [(1 - x, y), (x, 1 - y), (1 - x, 1 - y)]
    return x, y, c, chips


def _slot(px, py, pc):
    return 4 * px + 2 * py + pc


ANY = pl.BlockSpec(memory_space=pl.ANY)


def _peer(x, y, c, k):
    return (x ^ (k >> 2), y ^ ((k >> 1) & 1), c ^ (k & 1))


def _scatter_copies(grad_refs, land_refs, send_sems, recv_sems):
    x, y, c, _ = _place()
    copies = []
    for t, (grad, land) in enumerate(zip(grad_refs, land_refs)):
        for k in range(1, N_DEV):
            peer = _peer(x, y, c, k)
            sem = (N_DEV - 1) * t + k - 1
            copies.append(pltpu.make_async_remote_copy(
                src_ref=grad.at[_slot(*peer)], dst_ref=land.at[k - 1],
                send_sem=send_sems.at[sem], recv_sem=recv_sems.at[sem],
                device_id=peer, device_id_type=MESH))
    return copies


def _gather_phases(ins, outs, send_sems, recv_sems, local_sems):
    n = len(ins)
    x, y, c, chips = _place()
    me, sibling = (x, y, c), (x, y, 1 - c)

    def copy(t, k, block, to, src=None):
        dst = outs[t].at[_slot(*block)]
        return pltpu.make_async_remote_copy(
            src_ref=dst if src is None else src, dst_ref=dst,
            send_sem=send_sems.at[7 * t + k], recv_sem=recv_sems.at[7 * t + k],
            device_id=to, device_id_type=MESH)

    def lists():
        mine = [pltpu.make_async_copy(ins[t], outs[t].at[_slot(*me)], local_sems.at[t]) for t in range(n)]
        first = []
        for t in range(n):
            first.append(copy(t, 0, me, sibling, src=ins[t]))
            first += [copy(t, 1 + j, me, (*chip, c), src=ins[t]) for j, chip in enumerate(chips)]
        passed = [copy(t, 4 + j, (*chip, c), sibling) for t in range(n) for j, chip in enumerate(chips)]
        return mine, first, passed

    def start():
        mine, first, _ = lists()
        for cp in mine + first:
            cp.start()

    def forward():
        _, _, passed = lists()
        for t in range(n):
            for j, chip in enumerate(chips):
                copy(t, 1 + j, (*chip, c), me).wait_recv()
                passed[3 * t + j].start()

    def finish():
        mine, first, passed = lists()
        for t in range(n):
            copy(t, 0, sibling, me).wait_recv()
            for j, chip in enumerate(chips):
                copy(t, 4 + j, (*chip, 1 - c), me).wait_recv()
        for cp in first + passed:
            cp.wait_send()
        for cp in mine:
            cp.wait()

    return start, forward, finish


def _gather_scratch(n):
    return [pltpu.SemaphoreType.DMA((7 * n,)), pltpu.SemaphoreType.DMA((7 * n,)), pltpu.SemaphoreType.DMA((n,))]


def _chip_sums(g):
    _, r, c = g.shape

    def body(g_ref, sums_ref, mine_ref, land, own, send_sems, recv_sems, load_sems):
        x, y, c_i, chips = _place()
        sibling = (x, y, 1 - c_i)
        blocks = [(*chip, 1 - c_i) for chip in chips] + [sibling]
        sends = [pltpu.make_async_remote_copy(
            src_ref=g_ref.at[_slot(*block)], dst_ref=land.at[k], send_sem=send_sems.at[k],
            recv_sem=recv_sems.at[k], device_id=sibling, device_id_type=MESH) for k, block in enumerate(blocks)]
        loads = [pltpu.make_async_copy(g_ref.at[_slot(*chip, c_i)], own.at[j], load_sems.at[j])
                 for j, chip in enumerate(chips)]
        for cp in sends + loads:
            cp.start()
        for cp in sends + loads:
            cp.wait()
        for j in range(3):
            sums_ref[j] = (own[j].astype(F32) + land[j].astype(F32)).astype(BF16)
        mine_ref[0] = land[3]

    return pl.pallas_call(
        body, name="chip_sums",
        in_specs=[ANY], out_specs=[VM, VM],
        out_shape=[SDS((3, r, c), BF16), SDS((1, r, c), BF16)],
        scratch_shapes=[pltpu.VMEM((4, r, c), BF16), pltpu.VMEM((3, r, c), BF16),
                        pltpu.SemaphoreType.DMA((4,)), pltpu.SemaphoreType.DMA((4,)), pltpu.SemaphoreType.DMA((3,))],
        compiler_params=_cparams(),
    )(g)


def _chip_copies(sums_ref, land_ref, send_sems, recv_sems):
    x, y, c, chips = _place()
    del x, y
    return [pltpu.make_async_remote_copy(
        src_ref=sums_ref.at[j], dst_ref=land_ref.at[j], send_sem=send_sems.at[j], recv_sem=recv_sems.at[j],
        device_id=(*chip, c), device_id_type=MESH) for j, chip in enumerate(chips)]


def _row_tile(rows):
    return min(rows, 512)


def _adamw(w, g, m, v):
    m2 = ADAM_B1 * m + (1.0 - ADAM_B1) * g
    v2 = ADAM_B2 * v + (1.0 - ADAM_B2) * jnp.square(g)
    m_hat = m2 / (1.0 - ADAM_B1 ** ADAM_STEP)
    v_hat = v2 / (1.0 - ADAM_B2 ** ADAM_STEP)
    delta = -ADAM_LR * (m_hat / (jnp.sqrt(v_hat) + ADAM_EPS) + ADAM_WD * w)
    return delta, m2, v2


def _reduce_adamw(name, own, partials, w, m, v):
    r, c = own.shape
    tr = _row_tile(r)
    n_p = len(partials)

    def body(own_ref, *rest):
        p_refs, (w_ref, m_ref, v_ref, grad_ref, d_ref, nm_ref, nv_ref) = rest[:n_p], rest[n_p:]
        grad = own_ref[...]
        for p_ref, (_, _, count) in zip(p_refs, partials):
            for j in range(count):
                grad = grad + p_ref[j].astype(F32)
        grad_ref[...] = grad
        d_ref[...], nm_ref[...], nv_ref[...] = _adamw(w_ref[...], grad, m_ref[...], v_ref[...])

    flat = pl.BlockSpec((tr, c), lambda i: (i, 0))
    return pl.pallas_call(
        body, name=name, grid=(r // tr,),
        in_specs=[flat] + [pl.BlockSpec((count, tr, c), lambda i, first=first, count=count: (first // count, i, 0))
                           for _, first, count in partials] + [flat, flat, flat],
        out_specs=[flat, flat, flat, flat],
        out_shape=[SDS((r, c), F32)] * 4,
        compiler_params=_cparams(),
    )(own, *[p[0] for p in partials], w, m, v)


VM = pl.BlockSpec()


def _small_allreduce(parts):
    n = len(parts)

    def body(*refs):
        ins, outs, lands = refs[:n], refs[n:2 * n], refs[2 * n:3 * n]
        send_sems, recv_sems = refs[3 * n:]
        x, y, c, _ = _place()
        my_slot = _slot(x, y, c)
        copies = []
        for t in range(n):
            lands[t][my_slot] = ins[t][...]
            for k in range(1, N_DEV):
                sem = (N_DEV - 1) * t + k - 1
                copies.append(pltpu.make_async_remote_copy(
                    src_ref=ins[t], dst_ref=lands[t].at[my_slot],
                    send_sem=send_sems.at[sem], recv_sem=recv_sems.at[sem],
                    device_id=_peer(x, y, c, k), device_id_type=MESH))
        for cp in copies:
            cp.start()
        for t in range(n):
            for k in range(1, N_DEV):
                sem = (N_DEV - 1) * t + k - 1
                pltpu.make_async_remote_copy(
                    src_ref=ins[t], dst_ref=lands[t].at[_slot(*_peer(x, y, c, k))],
                    send_sem=send_sems.at[sem], recv_sem=recv_sems.at[sem],
                    device_id=(x, y, c), device_id_type=MESH).wait_recv()
        for cp in copies:
            cp.wait_send()
        for t in range(n):
            total = lands[t][0]
            for s in range(1, N_DEV):
                total = total + lands[t][s]
            outs[t][...] = total

    n_sems = (N_DEV - 1) * n
    return pl.pallas_call(
        body, name="small_allreduce",
        in_specs=[VM] * n, out_specs=[VM] * n, out_shape=[SDS(p.shape, F32) for p in parts],
        scratch_shapes=[pltpu.VMEM((N_DEV, *p.shape), F32) for p in parts]
        + [pltpu.SemaphoreType.DMA((n_sems,)), pltpu.SemaphoreType.DMA((n_sems,))],
    )(*parts)


def _small_adamw(my_slot, sums, ws, ms, vs):
    n = len(ws)

    def body(slot_ref, *refs):
        sum_refs, refs = refs[:n + 1], refs[n + 1:]
        w_refs, m_refs, v_refs, refs = refs[:n], refs[n:2 * n], refs[2 * n:3 * n], refs[3 * n:]
        g_refs, d_refs, nm_refs, nv_refs = refs[:n + 1], refs[n + 1:2 * n + 1], refs[2 * n + 1:3 * n + 1], refs[3 * n + 1:]
        for t in range(n + 1):
            if t == 0:
                g = sum_refs[0][:, pl.ds(pl.multiple_of(slot_ref[0] * 128, 128), 128)]
            else:
                g = sum_refs[t][...]
            g_refs[t][...] = g
            if t < n:
                d_refs[t][...], nm_refs[t][...], nv_refs[t][...] = _adamw(w_refs[t][...], g, m_refs[t][...], v_refs[t][...])

    shapes = [SDS(w.shape, F32) for w in ws]
    outs = pl.pallas_call(
        body, name="small_adamw",
        in_specs=[pl.BlockSpec(memory_space=pltpu.SMEM)] + [VM] * (4 * n + 1),
        out_specs=[VM] * (4 * n + 1),
        out_shape=shapes + [SDS(sums[-1].shape, F32)] + shapes * 3,
    )(my_slot, *sums, *ws, *ms, *vs)
    return outs[:n + 1], outs[n + 1:2 * n + 1], outs[2 * n + 1:3 * n + 1], outs[3 * n + 1:]


def kernel(x, a_norm, a_w_in, a_rel_bias, a_w_out, kv_norm, kv_w, t5_bias, b_norm, b_w_in, b_sinks, b_w_out, final_norm, loss_target, m_a_norm, m_a_w_in, m_a_rel_bias, m_a_w_out, m_kv_norm, m_kv_w, m_t5_bias, m_b_norm, m_b_w_in, m_b_sinks, m_b_w_out, m_final_norm, v_a_norm, v_a_w_in, v_a_rel_bias, v_a_w_out, v_kv_norm, v_kv_w, v_t5_bias, v_b_norm, v_b_w_in, v_b_sinks, v_b_w_out, v_final_norm):
    xi, yi, ci = lax.axis_index("x"), lax.axis_index("y"), lax.axis_index("c")
    my_slot = _slot(xi, yi, ci)

    slot_arr = jnp.reshape(my_slot, (1,)).astype(jnp.int32)
    order = _gather_order(xi, yi, ci)
    late_shards = [b_w_in[0].astype(BF16), a_w_out[0].astype(BF16), b_w_out[0].astype(BF16), kv_w.astype(BF16)]
    grad_x, loc, matrices = _local_step(
        slot_arr, order, x[0], loss_target[0], a_norm, a_w_in[0].astype(BF16), a_rel_bias[0], late_shards,
        kv_norm.reshape(1, D_MODEL), t5_bias, b_norm, b_sinks, final_norm.reshape(1, D_MODEL))

    shard_w = dict(a_w_in=a_w_in[0], b_w_in=b_w_in[0], a_w_out=a_w_out[0], b_w_out=b_w_out[0], kv_w=kv_w)
    shard_m = dict(a_w_in=m_a_w_in[0], b_w_in=m_b_w_in[0], a_w_out=m_a_w_out[0], b_w_out=m_b_w_out[0], kv_w=m_kv_w)
    shard_v = dict(a_w_in=v_a_w_in[0], b_w_in=v_b_w_in[0], a_w_out=v_a_w_out[0], b_w_out=v_b_w_out[0], kv_w=v_kv_w)
    big = {n: _reduce_adamw("adamw_" + n, own, partials, shard_w[n], shard_m[n], shard_v[n])
           for n, (own, partials) in matrices.items()}

    names = ("a_norm", "a_rel_bias", "kv_norm", "t5_bias", "b_norm", "b_sinks", "final_norm")
    tables = ("a_rel_bias", "t5_bias")

    def row(n, a):
        return a.reshape(-1, a.shape[-1]).T if n in tables else a.reshape(1, -1)

    small_w = [row(n, a) for n, a in zip(names, (a_norm, a_rel_bias, kv_norm, t5_bias, b_norm, b_sinks, final_norm))]
    small_m = [row(n, a) for n, a in zip(names, (m_a_norm, m_a_rel_bias, m_kv_norm, m_t5_bias, m_b_norm, m_b_sinks,
                                                 m_final_norm))]
    small_v = [row(n, a) for n, a in zip(names, (v_a_norm, v_a_rel_bias, v_kv_norm, v_t5_bias, v_b_norm, v_b_sinks,
                                                 v_final_norm))]
    sums = dict(zip(names + ("loss",), _small_allreduce([loc[n] for n in names] + [loc["loss"]])))
    sums["a_rel_bias"] = _a_bias_grad(sums["a_rel_bias"])
    sums["t5_bias"] = sums["t5_bias"][:, :T5_BUCKETS]
    sums["b_sinks"] = sums["b_sinks"][:, 0].reshape(1, N_HEADS)
    results = _small_adamw(slot_arr, [sums[n] for n in names + ("loss",)], small_w, small_m, small_v)
    like = dict(a_norm=a_norm, a_rel_bias=a_rel_bias, kv_norm=kv_norm, t5_bias=t5_bias, b_norm=b_norm,
                b_sinks=b_sinks, final_norm=final_norm)
    sm = [{n: (part[i].T if n in tables else part[i]).reshape(like[n].shape) for i, n in enumerate(names)}
          for part in results]
    loss = results[0][len(names)][0, 0]

    order = ("a_norm", "a_w_in", "a_rel_bias", "a_w_out", "kv_norm", "kv_w", "t5_bias", "b_norm",
             "b_w_in", "b_sinks", "b_w_out", "final_norm")
    lead = dict(a_w_in=True, b_w_in=True, a_w_out=True, b_w_out=True, kv_w=False)

    def pick(kind, name):
        if name in big:
            val = big[name][kind]
            return val[None] if lead[name] else val
        return sm[kind][name]

    outs = [loss, grad_x[None]]
    for kind in range(4):
        outs += [pick(kind, n) for n in order]
    return tuple(outs)
```

```python
import functools
import math

import numpy as np
import jax
import jax.numpy as jnp
from jax import lax
from jax.experimental import pallas as pl
from jax.experimental.pallas import tpu as pltpu

F32 = jnp.float32
BF16 = jnp.bfloat16
SDS = jax.ShapeDtypeStruct

D_MODEL = 1024
HEAD_DIM = 64
CHUNK = 64
N_HEADS = 16
RMS_EPS = 1e-6
A_LEFT_CHUNKS = 8
A_BAND = (A_LEFT_CHUNKS + 1) * CHUNK
A_REL_CLIP = 256
B_KV_HEADS = 2
B_GROUP = 8
B_LEFT_CHUNKS = 2
B_BAND = (B_LEFT_CHUNKS + 1) * CHUNK
T5_BUCKETS = 32
T5_MAX_DIST = 128
QBLK = 256
A_KEYS = 3 * QBLK
B_QBLK_FWD = 128
B_QBLK_BWD = 256
B_PREV = 128
A_DIAG = A_KEYS
NEG = -1e30
SCALE = HEAD_DIM ** -0.5
N_DEV = 8

ADAM_LR = 0.001
ADAM_B1 = 0.9
ADAM_B2 = 0.999
ADAM_EPS = 1e-08
ADAM_WD = 0.01
ADAM_STEP = 10

VMEM_LIMIT_BYTES = 56 * 1024 * 1024
MESH = pl.DeviceIdType.MESH


def _cparams():
    return pltpu.CompilerParams(vmem_limit_bytes=VMEM_LIMIT_BYTES)


def _dot(a, b):
    return jnp.dot(a, b, preferred_element_type=F32)


def _dot_nt(a, b):
    return lax.dot_general(a, b, (((1,), (1,)), ((), ())), preferred_element_type=F32)


def _dot_tn(a, b):
    return lax.dot_general(a, b, (((0,), (0,)), ((), ())), preferred_element_type=F32)


def _rstd(xf):
    return lax.rsqrt(jnp.mean(xf * xf, axis=-1, keepdims=True) + RMS_EPS)


def _sigmoid(x):
    return 1.0 / (1.0 + jnp.exp(-x))


_GATHER_SEQUENCE = ((0, None), (1, 0), (2, 1), (4, None), (5, None), (3, 2), (6, None))


def _gather_order(x, y, c):
    others = [(1 - x, y), (x, 1 - y), (1 - x, 1 - y)]
    arrivals = [_slot(x, y, 1 - c)] + [_slot(*chip, c) for chip in others] + [_slot(*chip, 1 - c) for chip in others]
    return jnp.stack([_slot(x, y, c)] + [arrivals[a] for a, _ in _GATHER_SEQUENCE]).astype(jnp.int32)


def _norm_matmul_gather(order, x, gain_shard, w_shard):
    t = x.shape[0]
    dw, tn = w_shard.shape
    tm = min(t, 2048)
    n_m = t // tm

    def body(order_ref, x_ref, gs_ref, shard_ref, xn_ref, o_ref, full_ref, gain_ref,
             xn_all, wbuf, gland, send_sems, recv_sems, gsend_sems, grecv_sems, load_sems, own_sem):
        n, m = pl.program_id(0), pl.program_id(1)
        x_i, y_i, c_i, chips = _place()
        me, sibling = (x_i, y_i, c_i), (x_i, y_i, 1 - c_i)

        def send(k, block, to, src=None):
            dst = full_ref.at[_slot(*block)]
            return pltpu.make_async_remote_copy(
                src_ref=dst if src is None else src, dst_ref=dst,
                send_sem=send_sems.at[k], recv_sem=recv_sems.at[k], device_id=to, device_id_type=MESH)

        own = pltpu.make_async_copy(shard_ref, full_ref.at[_slot(*me)], own_sem)
        first = [send(0, me, sibling, src=shard_ref)]
        first += [send(1 + j, me, (*chip, c_i), src=shard_ref) for j, chip in enumerate(chips)]
        forwards = [send(4 + j, (*chip, c_i), sibling) for j, chip in enumerate(chips)]
        arrivals = [send(0, sibling, me)] + [send(1 + j, (*chip, c_i), me) for j, chip in enumerate(chips)]
        arrivals += [send(4 + j, (*chip, 1 - c_i), me) for j, chip in enumerate(chips)]
        gains = [pltpu.make_async_remote_copy(
            src_ref=gs_ref, dst_ref=gland.at[_slot(*me)], send_sem=gsend_sems.at[k - 1],
            recv_sem=grecv_sems.at[k - 1], device_id=_peer(x_i, y_i, c_i, k), device_id_type=MESH)
            for k in range(1, N_DEV)]

        @pl.when(jnp.logical_and(n == 0, m == 0))
        def _():
            own.start()
            for cp in gains + first:
                cp.start()
            pltpu.make_async_copy(shard_ref, wbuf.at[0], load_sems.at[0]).start()
            gland[_slot(*me)] = gs_ref[...]
            for k in range(1, N_DEV):
                pltpu.make_async_remote_copy(
                    src_ref=gs_ref, dst_ref=gland.at[_slot(*_peer(x_i, y_i, c_i, k))],
                    send_sem=gsend_sems.at[k - 1], recv_sem=grecv_sems.at[k - 1],
                    device_id=me, device_id_type=MESH).wait_recv()
            for s in range(N_DEV):
                gain_ref[:, 128 * s:128 * (s + 1)] = gland[s]

        rows = pl.ds(pl.multiple_of(m * tm, tm), tm)

        @pl.when(n == 0)
        def _():
            xf = x_ref[...]
            xn = ((xf * _rstd(xf)) * gain_ref[...]).astype(BF16)
            xn_all[rows, :] = xn
            xn_ref[...] = xn

        @pl.when(m == 0)
        def _():
            pltpu.make_async_copy(full_ref.at[0], wbuf.at[n % 2], load_sems.at[n % 2]).wait()

        o_ref[...] = _dot(xn_all[rows, :], wbuf[n % 2]).astype(BF16)

        for k, (arrival, forward) in enumerate(_GATHER_SEQUENCE):
            @pl.when(jnp.logical_and(n == k, m == n_m - 1))
            def _(k=k, arrival=arrival, forward=forward):
                arrivals[arrival].wait_recv()
                if forward is not None:
                    forwards[forward].start()
                pltpu.make_async_copy(full_ref.at[order_ref[k + 1]], wbuf.at[(k + 1) % 2],
                                      load_sems.at[(k + 1) % 2]).start()

        @pl.when(jnp.logical_and(n == N_DEV - 1, m == n_m - 1))
        def _():
            for cp in gains + first + forwards:
                cp.wait_send()
            own.wait()

    held = lambda n, m, order: (jnp.where(n == 0, m, n_m - 1), 0)
    return pl.pallas_call(
        body, name="norm_matmul_gather",
        grid_spec=pltpu.PrefetchScalarGridSpec(
            num_scalar_prefetch=1, grid=(N_DEV, n_m),
            in_specs=[pl.BlockSpec((tm, D_MODEL), held),
                      pl.BlockSpec((1, 128), lambda n, m, order: (0, 0)), ANY],
            out_specs=[pl.BlockSpec((tm, D_MODEL), held),
                       pl.BlockSpec((tm, tn), lambda n, m, order: (m, order[n])),
                       ANY, pl.BlockSpec((1, D_MODEL), lambda n, m, order: (0, 0))],
            scratch_shapes=[pltpu.VMEM((t, D_MODEL), BF16), pltpu.VMEM((2, dw, tn), BF16),
                            pltpu.VMEM((N_DEV, 1, 128), F32),
                            pltpu.SemaphoreType.DMA((7,)), pltpu.SemaphoreType.DMA((7,)),
                            pltpu.SemaphoreType.DMA((7,)), pltpu.SemaphoreType.DMA((7,)),
                            pltpu.SemaphoreType.DMA((2,)), pltpu.SemaphoreType.DMA]),
        out_shape=[SDS((t, D_MODEL), BF16), SDS((t, N_DEV * tn), BF16), SDS((N_DEV, dw, tn), BF16),
                   SDS((1, D_MODEL), F32)],
        compiler_params=_cparams(),
    )(order, x, gain_shard, w_shard)


def _layer_a_out(x, z, w_out, kv_gain, b_gain, kv_w, w_in_b):
    t = x.shape[0]
    tm = min(t, 1024)
    nb, _, tn = w_in_b.shape

    def body(x_ref, z_ref, wo_ref, kvg_ref, bg_ref, kvw_ref, wb_ref,
             h1_ref, kvn_ref, hb_ref, kv_ref, qg_ref):
        h1 = x_ref[...] + _dot(z_ref[...], wo_ref[...])
        h1_ref[...] = h1
        y0 = h1 * _rstd(h1)
        kvn = (y0 * kvg_ref[...]).astype(BF16)
        hb = (y0 * bg_ref[...]).astype(BF16)
        kvn_ref[...] = kvn
        hb_ref[...] = hb
        kv_ref[...] = _dot(kvn, kvw_ref[...]).astype(BF16)
        for i in range(nb):
            qg_ref[:, i * tn:(i + 1) * tn] = _dot(hb, wb_ref[i]).astype(BF16)

    row = lambda m: (m, 0)
    fix2 = lambda m: (0, 0)
    return pl.pallas_call(
        body, name="layer_a_out", grid=(t // tm,),
        in_specs=[pl.BlockSpec((tm, D_MODEL), row), pl.BlockSpec((tm, D_MODEL), row),
                  pl.BlockSpec((D_MODEL, D_MODEL), fix2),
                  pl.BlockSpec((1, D_MODEL), fix2), pl.BlockSpec((1, D_MODEL), fix2),
                  pl.BlockSpec((D_MODEL, 256), fix2),
                  pl.BlockSpec((nb, D_MODEL, tn), lambda m: (0, 0, 0))],
        out_specs=[pl.BlockSpec((tm, D_MODEL), row), pl.BlockSpec((tm, D_MODEL), row),
                   pl.BlockSpec((tm, D_MODEL), row), pl.BlockSpec((tm, 256), row),
                   pl.BlockSpec((tm, nb * tn), row)],
        out_shape=[SDS((t, D_MODEL), F32), SDS((t, D_MODEL), BF16), SDS((t, D_MODEL), BF16),
                   SDS((t, 256), BF16), SDS((t, nb * tn), BF16)],
        compiler_params=_cparams(),
    )(x, z, w_out, kv_gain, b_gain, kv_w, w_in_b)


def _layer_b_out_loss(h1, z, w_out, f_gain, target):
    t = h1.shape[0]
    tm = min(t, 1024)

    def body(h1_ref, z_ref, wo_ref, fg_ref, tgt_ref,
             dh2_ref, dh2b_ref, dz_ref, loss_ref, dfn_ref):
        @pl.when(pl.program_id(0) == 0)
        def _():
            loss_ref[...] = jnp.zeros_like(loss_ref)
            dfn_ref[...] = jnp.zeros_like(dfn_ref)

        h2 = h1_ref[...] + _dot(z_ref[...], wo_ref[...])
        r = _rstd(h2)
        yn = h2 * r
        fg = fg_ref[...]
        err = yn * fg - tgt_ref[...]
        loss_ref[...] += (0.5 / D_MODEL) * jnp.sum(err * err)
        dy = err * (1.0 / D_MODEL)
        dfn_ref[...] += jnp.sum(dy * yn, axis=0, keepdims=True)
        u = dy * fg
        dh2 = r * u - h2 * ((r * r * r) * jnp.mean(u * h2, axis=-1, keepdims=True))
        dh2_ref[...] = dh2
        dh2b = dh2.astype(BF16)
        dh2b_ref[...] = dh2b
        dz_ref[...] = _dot_nt(dh2b, wo_ref[...]).astype(BF16)

    row = lambda m: (m, 0)
    fix2 = lambda m: (0, 0)
    return pl.pallas_call(
        body, name="layer_b_out_loss", grid=(t // tm,),
        in_specs=[pl.BlockSpec((tm, D_MODEL), row), pl.BlockSpec((tm, D_MODEL), row),
                  pl.BlockSpec((D_MODEL, D_MODEL), fix2), pl.BlockSpec((1, D_MODEL), fix2),
                  pl.BlockSpec((tm, D_MODEL), row)],
        out_specs=[pl.BlockSpec((tm, D_MODEL), row), pl.BlockSpec((tm, D_MODEL), row),
                   pl.BlockSpec((tm, D_MODEL), row), pl.BlockSpec((1, 128), fix2),
                   pl.BlockSpec((1, D_MODEL), fix2)],
        out_shape=[SDS((t, D_MODEL), F32), SDS((t, D_MODEL), BF16), SDS((t, D_MODEL), BF16),
                   SDS((1, 128), F32), SDS((1, D_MODEL), F32)],
        compiler_params=_cparams(),
    )(h1, z, w_out, f_gain, target)


def _layer_b_in_bwd(dqg, dkv, w_in_b, kv_w, h1, dh2, b_gain, kv_gain, w_out_a):
    t = h1.shape[0]
    tm = min(t, 512)
    nb, _, tn = w_in_b.shape
    per = D_MODEL // tn

    def body(dqg_ref, dkv_ref, wb_ref, kvw_ref, h1_ref, dh2_ref, bg_ref, kvg_ref, wo_ref,
             dh1_ref, dh1b_ref, dz_ref, dbn_ref, dkn_ref):
        @pl.when(pl.program_id(0) == 0)
        def _():
            dbn_ref[...] = jnp.zeros_like(dbn_ref)
            dkn_ref[...] = jnp.zeros_like(dkn_ref)

        dhb = jnp.zeros((tm, D_MODEL), F32)
        for i in range(nb):
            blk = dqg_ref[i // per, :, (i % per) * tn:(i % per + 1) * tn]
            dhb = dhb + _dot_nt(blk, wb_ref[i])
        dkn = (_dot_nt(dkv_ref[0].astype(BF16), kvw_ref[:, 0:128])
               + _dot_nt(dkv_ref[1].astype(BF16), kvw_ref[:, 128:256]))
        h1 = h1_ref[...]
        r = _rstd(h1)
        xr = h1 * r
        dbn_ref[...] += jnp.sum(dhb * xr, axis=0, keepdims=True)
        dkn_ref[...] += jnp.sum(dkn * xr, axis=0, keepdims=True)
        u = dhb * bg_ref[...] + dkn * kvg_ref[...]
        dh1 = dh2_ref[...] + r * u - h1 * ((r * r * r) * jnp.mean(u * h1, axis=-1, keepdims=True))
        dh1_ref[...] = dh1
        dh1b = dh1.astype(BF16)
        dh1b_ref[...] = dh1b
        dz_ref[...] = _dot_nt(dh1b, wo_ref[...]).astype(BF16)

    row = lambda m: (m, 0)
    fix2 = lambda m: (0, 0)
    return pl.pallas_call(
        body, name="layer_b_in_bwd", grid=(t // tm,),
        in_specs=[pl.BlockSpec((2, tm, D_MODEL), lambda m: (0, m, 0)),
                  pl.BlockSpec((2, tm, 128), lambda m: (0, m, 0)),
                  pl.BlockSpec((nb, D_MODEL, tn), lambda m: (0, 0, 0)),
                  pl.BlockSpec((D_MODEL, 256), fix2),
                  pl.BlockSpec((tm, D_MODEL), row), pl.BlockSpec((tm, D_MODEL), row),
                  pl.BlockSpec((1, D_MODEL), fix2), pl.BlockSpec((1, D_MODEL), fix2),
                  pl.BlockSpec((D_MODEL, D_MODEL), fix2)],
        out_specs=[pl.BlockSpec((tm, D_MODEL), row), pl.BlockSpec((tm, D_MODEL), row),
                   pl.BlockSpec((tm, D_MODEL), row), pl.BlockSpec((1, D_MODEL), fix2),
                   pl.BlockSpec((1, D_MODEL), fix2)],
        out_shape=[SDS((t, D_MODEL), F32), SDS((t, D_MODEL), BF16), SDS((t, D_MODEL), BF16),
                   SDS((1, D_MODEL), F32), SDS((1, D_MODEL), F32)],
        compiler_params=_cparams(),
    )(dqg, dkv, w_in_b, kv_w, h1, dh2, b_gain, kv_gain, w_out_a)


def _layer_a_in_bwd(dqg, dkv, w_in_a, x, dh1, a_gain, chip_sums):
    t = x.shape[0]
    tm = min(t, 512)
    nb, _, tn = w_in_a.shape
    per = D_MODEL // tn

    def body(dqg_ref, dkv_ref, w_ref, x_ref, dh1_ref, ag_ref, sums_ref, dx_ref, dan_ref, land_ref,
             send_sems, recv_sems, dan_land, dan_send, dan_recv):
        @pl.when(pl.program_id(0) == 0)
        def _():
            dan_ref[...] = jnp.zeros_like(dan_ref)
            for cp in _chip_copies(sums_ref, land_ref, send_sems, recv_sems):
                cp.start()

        dxn = jnp.zeros((tm, D_MODEL), F32)
        for i in range(nb):
            part = i // per
            src = dqg_ref if part in (0, 3) else dkv_ref
            outer = {0: 0, 3: 1, 1: 0, 2: 1}[part]
            blk = src[outer, :, (i % per) * tn:(i % per + 1) * tn]
            dxn = dxn + _dot_nt(blk, w_ref[i])
        xf = x_ref[...]
        r = _rstd(xf)
        dan_ref[...] += jnp.sum(dxn * (xf * r), axis=0, keepdims=True)
        u = dxn * ag_ref[...]
        dx_ref[...] = dh1_ref[...] + r * u - xf * ((r * r * r) * jnp.mean(u * xf, axis=-1, keepdims=True))

        @pl.when(pl.program_id(0) == t // tm - 1)
        def _():
            _all_reduce_small([dan_ref], [dan_ref], [dan_land], dan_send, dan_recv)
            for cp in _chip_copies(sums_ref, land_ref, send_sems, recv_sems):
                cp.wait()

    row = lambda m: (m, 0)
    fix2 = lambda m: (0, 0)
    return pl.pallas_call(
        body, name="layer_a_in_bwd", grid=(t // tm,),
        in_specs=[pl.BlockSpec((2, tm, D_MODEL), lambda m: (0, m, 0)),
                  pl.BlockSpec((2, tm, D_MODEL), lambda m: (0, m, 0)),
                  pl.BlockSpec((nb, D_MODEL, tn), lambda m: (0, 0, 0)),
                  pl.BlockSpec((tm, D_MODEL), row), pl.BlockSpec((tm, D_MODEL), row),
                  pl.BlockSpec((1, D_MODEL), fix2), ANY],
        out_specs=[pl.BlockSpec((tm, D_MODEL), row), pl.BlockSpec((1, D_MODEL), fix2), ANY],
        out_shape=[SDS((t, D_MODEL), F32), SDS((1, D_MODEL), F32), SDS(chip_sums.shape, chip_sums.dtype)],
        scratch_shapes=[pltpu.SemaphoreType.DMA((3,)), pltpu.SemaphoreType.DMA((3,))]
        + _all_reduce_scratch([(1, D_MODEL)]),
        compiler_params=_cparams(),
    )(dqg, dkv, w_in_a, x, dh1, a_gain, chip_sums)


def _lut(s, vals):
    r = jnp.int32(vals[0])
    for i in range(1, len(vals)):
        r = jnp.where(s == i, jnp.int32(vals[i]), r)
    return r


def _held(steps, i):
    seq, cur = [None] * len(steps), None
    for k in range(len(steps) - 1, -1, -1):
        if steps[k][0] == i:
            cur = steps[k][1:3]
        seq[k] = cur
    for k in range(len(steps)):
        cur = seq[k] = seq[k] if seq[k] is not None else cur
    return seq


def _weight_grad_cols(name, my_slot, a, bs, steps, tn):
    t, dw = a.shape
    n_arr = len(bs)
    which = [s[0] for s in steps]
    blks = [s[3] for s in steps]

    def body(slot_ref, a_ref, *rest):
        b_refs, (o_ref, own_ref, at_ref) = rest[:n_arr], rest[n_arr:]
        s = pl.program_id(0)

        @pl.when(s == 0)
        def _():
            at_ref[...] = a_ref[...].T

        for i in range(n_arr):
            @pl.when(_lut(s, which) == i)
            def _(i=i):
                res = _dot(at_ref[...], b_refs[i][0])
                o_ref[0] = res.astype(BF16)

                @pl.when(_lut(s, blks) == slot_ref[0])
                def _():
                    own_ref[...] = res

    def b_spec(i):
        held = _held(steps, i)
        return pl.BlockSpec((1, t, tn), lambda s, slot: (_lut(s, [h[0] for h in held]), 0,
                                                         _lut(s, [h[1] for h in held])))

    return pl.pallas_call(
        body, name=name,
        grid_spec=pltpu.PrefetchScalarGridSpec(
            num_scalar_prefetch=1, grid=(len(steps),),
            in_specs=[pl.BlockSpec((t, dw), lambda s, slot: (0, 0))] + [b_spec(i) for i in range(n_arr)],
            out_specs=[pl.BlockSpec((1, dw, tn), lambda s, slot: (_lut(s, blks), 0, 0)),
                       pl.BlockSpec((dw, tn), lambda s, slot: (0, 0))],
            scratch_shapes=[pltpu.VMEM((dw, t), BF16)]),
        out_shape=[SDS((N_DEV, dw, tn), BF16), SDS((dw, tn), F32)],
        compiler_params=_cparams(),
    )(my_slot, a, *bs)


def _weight_grad_rows(name, my_slot, a, b):
    t, dw = a.shape
    n_o, _, c = b.shape
    rows = dw // N_DEV
    tn = min(c, 256)
    per = c // tn

    def body(slot_ref, a_ref, b_ref, o_ref, own_ref, at_ref, res_ref):
        @pl.when(pl.program_id(0) == 0)
        def _():
            at_ref[...] = a_ref[...].T

        res_ref[...] = _dot(at_ref[...], b_ref[0].astype(BF16))
        o_ref[...] = res_ref[...].astype(BF16)
        own_ref[...] = res_ref[pl.ds(pl.multiple_of(slot_ref[0] * rows, rows), rows), :]

    all_rows, own = pl.pallas_call(
        body, name=name,
        grid_spec=pltpu.PrefetchScalarGridSpec(
            num_scalar_prefetch=1, grid=(n_o * per,),
            in_specs=[pl.BlockSpec((t, dw), lambda s, slot: (0, 0)),
                      pl.BlockSpec((1, t, tn), lambda s, slot: (s // per, 0, s % per))],
            out_specs=[pl.BlockSpec((dw, tn), lambda s, slot: (0, s)),
                       pl.BlockSpec((rows, tn), lambda s, slot: (0, s))],
            scratch_shapes=[pltpu.VMEM((dw, t), BF16), pltpu.VMEM((dw, tn), F32)]),
        out_shape=[SDS((dw, n_o * c), BF16), SDS((rows, n_o * c), F32)],
        compiler_params=_cparams(),
    )(my_slot, a, b)
    return all_rows.reshape(N_DEV, rows, n_o * c), own


def _lane_lo():
    return lax.broadcasted_iota(jnp.int32, (1, 128), 1) < HEAD_DIM


def _collapse_chunks(ds, keys):
    if ds.shape[1] < keys:
        ds = jnp.concatenate([jnp.zeros((ds.shape[0], keys - ds.shape[1]), F32), ds], axis=1)
    gc = ds[0:CHUNK]
    for cc in range(1, ds.shape[0] // CHUNK):
        gc = gc + pltpu.roll(ds[cc * CHUNK:(cc + 1) * CHUNK], keys - cc * CHUNK, 1)
    return gc


def _offset_sums(gc):
    hi = gc.astype(BF16)
    lo = (gc - hi.astype(F32)).astype(BF16)
    flip = (lax.broadcasted_iota(jnp.int32, (CHUNK, CHUNK), 0)
            + lax.broadcasted_iota(jnp.int32, (CHUNK, CHUNK), 1) == CHUNK - 1).astype(BF16)
    gf = _dot(flip, hi) + _dot(flip, lo)
    skew = pltpu.roll(gf, 0, 1, stride=1, stride_axis=0)
    return jnp.sum(skew, axis=0, keepdims=True)


def _band_bias(w_row, band, rows):
    keys = w_row.shape[1]
    base = jnp.broadcast_to(w_row, (CHUNK, keys))
    skew = pltpu.roll(base, 0, 1, stride=1, stride_axis=0)
    skew = pltpu.roll(skew, keys - (CHUNK - 1), 1)
    col = lax.broadcasted_iota(jnp.int32, (CHUNK, keys), 1)
    chunk0 = jnp.where(col < band, skew, NEG)
    return jnp.concatenate(
        [chunk0] + [pltpu.roll(chunk0, cc * CHUNK, 1) for cc in range(1, rows // CHUNK)], axis=0)


def _silu_parts(g):
    sg = _sigmoid(g)
    return g * sg, sg * (1.0 + g * (1.0 - sg))


A_PAIRS_FWD = 8
A_PAIRS_BWD = 4


def _a_specs(pairs):
    lanes = 128 * pairs
    steps = D_MODEL // lanes
    q = pl.BlockSpec((QBLK, lanes), lambda p, j: (j, p))
    ks = [pl.BlockSpec((QBLK, lanes), lambda p, j, b=b: (jnp.maximum(j - 2 + b, 0), steps + p)) for b in range(3)]
    vs = [pl.BlockSpec((QBLK, lanes), lambda p, j, b=b: (jnp.maximum(j - 2 + b, 0), 2 * steps + p))
          for b in range(3)]
    g = pl.BlockSpec((QBLK, lanes), lambda p, j: (j, 3 * steps + p))
    bias = pl.BlockSpec((pairs, 8, A_KEYS), lambda p, j: (p, 0, 0))
    return q, ks, vs, g, bias


def _a_fill_bias(w_ref, b_ref, j, pairs):
    _fill_bias(2 * pairs, lambda h: w_ref[h // 2, h % 2:h % 2 + 1, :], A_BAND, b_ref, j)


def _by_valid_key_blocks(j, fn):
    pl.when(j == 0)(functools.partial(fn, 1))
    pl.when(j == 1)(functools.partial(fn, 2))
    pl.when(j >= 2)(functools.partial(fn, 3))


def _fill_bias(n, get_row, band, bias_scr, j):
    @pl.when(j == 0)
    def _():
        for h in range(n):
            bias_scr[h] = _band_bias(get_row(h), band, bias_scr.shape[1])


def _normalise_pair(rs, mxs, lane_lo, extra=None):
    num = jnp.where(lane_lo, rs[0], rs[1])
    den = pltpu.roll(jnp.where(lane_lo, rs[1], rs[0]), HEAD_DIM, 1)
    if extra is not None:
        den = den + jnp.where(lane_lo, extra[0], extra[1])
    return num / den, jnp.where(lane_lo, mxs[0], mxs[1]) + jnp.log(den)


def _own_everywhere(x, sel):
    return jnp.where(sel, x, pltpu.roll(x, HEAD_DIM, 1))


def _minus_rows(s, row_full):
    return jnp.concatenate([s[:, i:i + 128] - row_full for i in range(0, s.shape[1], 128)], axis=1)


def _attn_a_fwd(qkvg, bias, gather):
    t = qkvg.shape[0]
    nq = t // QBLK
    n_g = len(gather)
    pairs = A_PAIRS_FWD
    lanes = 128 * pairs
    steps = D_MODEL // lanes
    q_spec, k_specs, v_specs, g_spec, bias_spec = _a_specs(pairs)

    def body(q_ref, k0, k1, k2, v0, v1, v2, g_ref, w_ref, *rest):
        shard_refs, rest = rest[:n_g], rest[n_g:]
        z_ref, o_ref, lse_ref = rest[:3]
        full_refs, (b_ref, *comm) = rest[3:3 + n_g], rest[3 + n_g:]
        p = pl.program_id(0)
        j = pl.program_id(1)
        start, forward, finish = _gather_phases(shard_refs, full_refs, *comm)
        at = p * nq + j
        pl.when(at == 0)(start)
        pl.when(at == steps * nq // 2)(forward)
        _a_fill_bias(w_ref, b_ref, j, pairs)
        lane_lo = _lane_lo()
        sels = (lane_lo, jnp.logical_not(lane_lo))

        def attend(n_blocks):
            first_col = (3 - n_blocks) * QBLK
            for pp in range(pairs):
                cols = slice(128 * pp, 128 * (pp + 1))
                k = jnp.concatenate([r[:, cols] for r in (k0, k1, k2)[3 - n_blocks:]], axis=0)
                v = jnp.concatenate([r[:, cols] for r in (v0, v1, v2)[3 - n_blocks:]], axis=0)
                q = q_ref[:, cols]
                qm2 = jnp.concatenate([jnp.where(sel, q, jnp.zeros_like(q)) for sel in sels], axis=0) * SCALE
                s2 = _dot_nt(qm2, k)
                rs, mxs = [], []
                for hh, sel in enumerate(sels):
                    s = s2[hh * QBLK:(hh + 1) * QBLK] + b_ref[2 * pp + hh, :, first_col:]
                    mxs.append(jnp.max(s, axis=-1, keepdims=True))
                    e = jnp.exp(s - mxs[hh]).astype(BF16)
                    rs.append(_dot(e, jnp.where(sel, v, jnp.ones_like(v))))
                o, lse = _normalise_pair(rs, mxs, lane_lo)
                silu, _ = _silu_parts(g_ref[:, cols].astype(F32))
                o_ref[:, cols] = o.astype(BF16)
                z_ref[:, cols] = (o * silu).astype(BF16)
                lse_ref[:, cols] = lse

        _by_valid_key_blocks(j, attend)
        pl.when(at == steps * nq - 1)(finish)

    out_spec = pl.BlockSpec((QBLK, lanes), lambda p, j: (j, p))
    outs = pl.pallas_call(
        body, name="attn_a_fwd", grid=(steps, nq),
        in_specs=[q_spec, *k_specs, *v_specs, g_spec, bias_spec] + [ANY] * n_g,
        out_specs=[out_spec, out_spec, out_spec] + [ANY] * n_g,
        out_shape=[SDS((t, D_MODEL), BF16), SDS((t, D_MODEL), BF16), SDS((t, D_MODEL), F32)]
        + [SDS((N_DEV, *s.shape), s.dtype) for s in gather],
        scratch_shapes=[pltpu.VMEM((2 * pairs, QBLK, A_KEYS), F32)] + _gather_scratch(n_g),
        compiler_params=_cparams(),
    )(qkvg, qkvg, qkvg, qkvg, qkvg, qkvg, qkvg, qkvg, bias, *gather)
    return outs[0], outs[1], outs[2], list(outs[3:])


def _attn_a_bwd(qkvg, bias, out_a, lse, dz, scatter):
    t = qkvg.shape[0]
    nq = t // QBLK
    n_sc = len(scatter)
    pairs = A_PAIRS_BWD
    lanes = 128 * pairs
    steps = D_MODEL // lanes
    q_spec, k_specs, v_specs, g_spec, bias_spec = _a_specs(pairs)

    def body(q_ref, k0, k1, k2, v0, v1, v2, g_ref, w_ref, o_ref, lse_ref, dz_ref, *rest):
        sc_refs, rest = rest[:n_sc], rest[n_sc:]
        dqg_ref, dkv_ref, dg_ref = rest[:3]
        land_refs, rest = rest[3:3 + n_sc], rest[3 + n_sc:]
        dk_acc, dv_acc, gt_acc, b_ref, send_sems, recv_sems = rest
        j = pl.program_id(1)
        first = jnp.logical_and(pl.program_id(0) == 0, j == 0)
        last = jnp.logical_and(pl.program_id(0) == steps - 1, j == nq - 1)

        @pl.when(first)
        def _():
            for cp in _scatter_copies(sc_refs, land_refs, send_sems, recv_sems):
                cp.start()

        _a_fill_bias(w_ref, b_ref, j, pairs)

        @pl.when(j == 0)
        def _():
            dk_acc[...] = jnp.zeros_like(dk_acc)
            dv_acc[...] = jnp.zeros_like(dv_acc)
            gt_acc[...] = jnp.zeros_like(gt_acc)

        lane_lo = _lane_lo()
        sels = (lane_lo, jnp.logical_not(lane_lo))

        def attend(n_blocks):
            first_col = (3 - n_blocks) * QBLK
            for pp in range(pairs):
                cols = slice(128 * pp, 128 * (pp + 1))
                q = q_ref[:, cols]
                k = jnp.concatenate([r[:, cols] for r in (k0, k1, k2)[3 - n_blocks:]], axis=0)
                v = jnp.concatenate([r[:, cols] for r in (v0, v1, v2)[3 - n_blocks:]], axis=0)
                o = o_ref[:, cols].astype(F32)
                lse_pair = lse_ref[:, cols]
                dzf = dz_ref[:, cols].astype(F32)
                silu, dsilu = _silu_parts(g_ref[:, cols].astype(F32))
                do = dzf * silu
                dqg_ref[1, :, cols] = (dzf * o * dsilu).astype(BF16)
                doo = do * o
                qm2 = jnp.concatenate([jnp.where(sel, q, jnp.zeros_like(q)) for sel in sels], axis=0) * SCALE
                dom2 = jnp.concatenate([jnp.where(sel, do, 0.0) for sel in sels], axis=0).astype(BF16)
                s2 = _dot_nt(qm2, k)
                dp2 = _dot_nt(dom2, v)
                ps, dss = [], []
                for hh, sel in enumerate(sels):
                    rows = slice(hh * QBLK, (hh + 1) * QBLK)
                    s = s2[rows] + b_ref[2 * pp + hh, :, first_col:]
                    p = jnp.exp(_minus_rows(s, _own_everywhere(lse_pair, sel)))
                    delta = jnp.sum(jnp.where(sel, doo, 0.0), axis=-1, keepdims=True)
                    ds = p * (dp2[rows] - delta)
                    gt_acc[2 * pp + hh] += _collapse_chunks(ds, A_KEYS)
                    ps.append(p.astype(BF16))
                    dss.append(ds.astype(BF16))
                dsb2 = jnp.concatenate(dss, axis=0)
                dq2 = _dot(dsb2, k) * SCALE
                dk_blk = _dot_tn(dsb2, qm2)
                dv_blk = _dot_tn(jnp.concatenate(ps, axis=0), dom2)
                dqg_ref[0, :, cols] = jnp.where(lane_lo, dq2[0:QBLK], dq2[QBLK:2 * QBLK]).astype(BF16)
                for b in range(n_blocks):
                    rows = pl.ds(pl.multiple_of((j - n_blocks + 1 + b) * QBLK, QBLK), QBLK)
                    dk_acc[rows, cols] += dk_blk[b * QBLK:(b + 1) * QBLK]
                    dv_acc[rows, cols] += dv_blk[b * QBLK:(b + 1) * QBLK]

        _by_valid_key_blocks(j, attend)

        @pl.when(j == nq - 1)
        def _():
            dkv_ref[0] = dk_acc[...].astype(BF16)
            dkv_ref[1] = dv_acc[...].astype(BF16)
            for pp in range(pairs):
                dg_ref[pp] = jnp.concatenate([_offset_sums(gt_acc[2 * pp]), _offset_sums(gt_acc[2 * pp + 1]),
                                              jnp.zeros((6, A_DIAG), F32)], axis=0)

        @pl.when(last)
        def _():
            for cp in _scatter_copies(sc_refs, land_refs, send_sems, recv_sems):
                cp.wait()

    blk = pl.BlockSpec((QBLK, lanes), lambda p, j: (j, p))
    outs = pl.pallas_call(
        body, name="attn_a_bwd", grid=(steps, nq),
        in_specs=[q_spec, *k_specs, *v_specs, g_spec, bias_spec, blk, blk, blk] + [ANY] * n_sc,
        out_specs=[pl.BlockSpec((2, QBLK, lanes), lambda p, j: (0, j, p)),
                   pl.BlockSpec((2, t, lanes), lambda p, j: (0, 0, p)),
                   pl.BlockSpec((pairs, 8, A_DIAG), lambda p, j: (p, 0, 0))] + [ANY] * n_sc,
        out_shape=[SDS((2, t, D_MODEL), BF16), SDS((2, t, D_MODEL), BF16), SDS((N_HEADS // 2, 8, A_DIAG), F32)]
        + [SDS((N_DEV - 1, *g.shape[1:]), g.dtype) for g in scatter],
        scratch_shapes=[pltpu.VMEM((t, lanes), F32), pltpu.VMEM((t, lanes), F32),
                        pltpu.VMEM((2 * pairs, CHUNK, A_KEYS), F32), pltpu.VMEM((2 * pairs, QBLK, A_KEYS), F32),
                        pltpu.SemaphoreType.DMA(((N_DEV - 1) * n_sc,)),
                        pltpu.SemaphoreType.DMA(((N_DEV - 1) * n_sc,))],
        compiler_params=_cparams(),
    )(qkvg, qkvg, qkvg, qkvg, qkvg, qkvg, qkvg, qkvg, bias, out_a, lse, dz, *scatter)
    return outs[0], outs[1], outs[2], list(outs[3:])


def _b_specs(qblk):
    per = qblk // B_PREV
    q = pl.BlockSpec((qblk, 512), lambda h, j: (j, h))
    g = pl.BlockSpec((qblk, 512), lambda h, j: (j, 2 + h))
    kp = pl.BlockSpec((B_PREV, 128), lambda h, j: (jnp.maximum(per * j - 1, 0), 0))
    kc = pl.BlockSpec((qblk, 128), lambda h, j: (j, 0))
    vp = pl.BlockSpec((B_PREV, 128), lambda h, j: (jnp.maximum(per * j - 1, 0), 1))
    vc = pl.BlockSpec((qblk, 128), lambda h, j: (j, 1))
    bias = pl.BlockSpec((B_GROUP, qblk + B_PREV), lambda h, j: (h, 0))
    sinks = pl.BlockSpec(memory_space=pltpu.SMEM)
    return q, g, kp, kc, vp, vc, bias, sinks


def _b_operands(kp, kc, vp, vc, kvh, with_prev):
    k = jnp.concatenate([kp[...], kc[...]], axis=0) if with_prev else kc[...]
    v = jnp.concatenate([vp[...], vc[...]], axis=0) if with_prev else vc[...]
    kr = pltpu.roll(k, HEAD_DIM, 1)
    vr = pltpu.roll(v, HEAD_DIM, 1)
    first = kvh == 0
    return (jnp.where(first, k, kr), jnp.where(first, kr, k),
            jnp.where(first, v, vr), jnp.where(first, vr, v))


def _attn_b_fwd(qg, kv, bias, sinks):
    t = qg.shape[0]
    qblk = B_QBLK_FWD
    per_step = 4
    step = per_step * qblk
    q_spec, g_spec, kp_spec, kc_spec, vp_spec, vc_spec, _, sink_spec = _b_specs(step)
    bias_spec = pl.BlockSpec((B_GROUP, qblk + B_PREV), lambda h, j: (h, 0))

    def body(q_ref, g_ref, kp, kc, vp, vc, w_ref, sink_ref, z_ref, o_ref, lse_ref, b_ref):
        kvh = pl.program_id(0)
        j = pl.program_id(1)
        _fill_bias(B_GROUP, lambda h: w_ref[h:h + 1, :], B_BAND, b_ref, j)
        lane_lo = _lane_lo()
        n_pairs = B_GROUP // 2

        def attend(first):
            k_lo, k_hi, v_lo, v_hi = _b_operands(kp, kc, vp, vc, kvh, True)
            for sb in range(per_step):
                no_prev = first and sb == 0
                first_col = B_PREV if no_prev else 0
                keys = slice(sb * qblk + first_col, (sb + 1) * qblk + B_PREV)
                qrows = slice(sb * qblk, (sb + 1) * qblk)
                halves = []
                for hh, sel in enumerate((lane_lo, jnp.logical_not(lane_lo))):
                    kk = (k_lo if hh == 0 else k_hi)[keys]
                    vv = (v_lo if hh == 0 else v_hi)[keys]
                    qm4 = jnp.concatenate(
                        [jnp.where(sel, q_ref[qrows, 128 * pp:128 * (pp + 1)], jnp.zeros((qblk, 128), BF16))
                         for pp in range(n_pairs)], axis=0) * SCALE
                    s4 = _dot_nt(qm4, kk)
                    es, mxs = [], []
                    for pp in range(n_pairs):
                        g = 2 * pp + hh
                        s = s4[pp * qblk:(pp + 1) * qblk] + b_ref[g, :, first_col:]
                        mxs.append(jnp.maximum(jnp.max(s, axis=-1, keepdims=True), sink_ref[kvh * B_GROUP + g]))
                        es.append(jnp.exp(s - mxs[pp]).astype(BF16))
                    r4 = _dot(jnp.concatenate(es, axis=0), jnp.where(sel, vv, jnp.ones_like(vv)))
                    halves.append((r4, mxs))
                for pp in range(n_pairs):
                    cols = slice(128 * pp, 128 * (pp + 1))
                    rows = slice(pp * qblk, (pp + 1) * qblk)
                    mxs = [halves[hh][1][pp] for hh in range(2)]
                    sink_terms = [jnp.exp(sink_ref[kvh * B_GROUP + 2 * pp + hh] - mxs[hh]) for hh in range(2)]
                    o, lse = _normalise_pair([halves[hh][0][rows] for hh in range(2)], mxs, lane_lo, sink_terms)
                    silu, _ = _silu_parts(g_ref[qrows, cols].astype(F32))
                    o_ref[qrows, cols] = o.astype(BF16)
                    z_ref[qrows, cols] = (o * silu).astype(BF16)
                    lse_ref[qrows, cols] = lse

        pl.when(j == 0)(functools.partial(attend, True))
        pl.when(j >= 1)(functools.partial(attend, False))

    out_spec = pl.BlockSpec((step, 512), lambda h, j: (j, h))
    return pl.pallas_call(
        body, name="attn_b_fwd", grid=(B_KV_HEADS, t // step),
        in_specs=[q_spec, g_spec, kp_spec, kc_spec, vp_spec, vc_spec, bias_spec, sink_spec],
        out_specs=[out_spec, out_spec, out_spec],
        out_shape=[SDS((t, D_MODEL), BF16), SDS((t, D_MODEL), BF16), SDS((t, D_MODEL), F32)],
        scratch_shapes=[pltpu.VMEM((B_GROUP, qblk, qblk + B_PREV), F32)],
        compiler_params=_cparams(),
    )(qg, qg, kv, kv, kv, kv, bias, sinks)


def _attn_b_bwd(qg, kv, bias, sinks, out_b, lse, dz, bucket_onehot):
    t = qg.shape[0]
    qblk = B_QBLK_BWD
    keys = qblk + B_PREV
    nq = t // qblk
    per = qblk // B_PREV

    def body(q_ref, g_ref, kp, kc, vp, vc, w_ref, sink_ref, o_ref, lse_ref, dz_ref, oh_ref,
             dqg_ref, dkv_ref, dt5_ref, dsink_ref, gt_acc, b_ref):
        j = pl.program_id(0)
        _fill_bias(N_HEADS, lambda h: w_ref[h:h + 1, :], B_BAND, b_ref, j)

        @pl.when(j == 0)
        def _():
            dkv_ref[...] = jnp.zeros_like(dkv_ref)
            gt_acc[...] = jnp.zeros_like(gt_acc)
            dsink_ref[...] = jnp.zeros_like(dsink_ref)

        lane_lo = _lane_lo()

        def attend(with_prev):
            first_col = 0 if with_prev else B_PREV
            dk_add = jnp.zeros((keys - first_col, 128), F32)
            dv_add = jnp.zeros((keys - first_col, 128), F32)
            for kvh in range(B_KV_HEADS):
                k_lo, k_hi, v_lo, v_hi = _b_operands(kp, kc, vp, vc, kvh, with_prev)
                dk_blk = jnp.zeros((keys - first_col, 128), F32)
                dv_blk = jnp.zeros((keys - first_col, 128), F32)
                for pp in range(B_GROUP // 2):
                    cols = slice(512 * kvh + 128 * pp, 512 * kvh + 128 * (pp + 1))
                    qp = q_ref[:, cols]
                    o = o_ref[:, cols].astype(F32)
                    lse_pair = lse_ref[:, cols]
                    dzf = dz_ref[:, cols].astype(F32)
                    silu, dsilu = _silu_parts(g_ref[:, cols].astype(F32))
                    do = dzf * silu
                    dqg_ref[1, :, cols] = (dzf * o * dsilu).astype(BF16)
                    doo = do * o
                    dqs = []
                    for hh in range(2):
                        g = kvh * B_GROUP + 2 * pp + hh
                        sel = lane_lo if hh == 0 else jnp.logical_not(lane_lo)
                        kk = k_lo if hh == 0 else k_hi
                        vv = v_lo if hh == 0 else v_hi
                        qm = jnp.where(sel, qp, jnp.zeros_like(qp)) * SCALE
                        s = _dot_nt(qm, kk) + b_ref[g, :, first_col:]
                        lse_h = _own_everywhere(lse_pair, sel)
                        p = jnp.exp(_minus_rows(s, lse_h))
                        delta = jnp.sum(jnp.where(sel, doo, 0.0), axis=-1, keepdims=True)
                        dom = jnp.where(sel, do, 0.0).astype(BF16)
                        dp = _dot_nt(dom, vv)
                        ds = p * (dp - delta)
                        gt_acc[g, :, first_col:] += ds
                        dsink_ref[g:g + 1, :] -= jnp.sum(jnp.exp(sink_ref[g] - lse_h) * delta, axis=0, keepdims=True)
                        dsb = ds.astype(BF16)
                        dqs.append(_dot(dsb, kk) * SCALE)
                        dk_blk = dk_blk + _dot_tn(dsb, qm)
                        dv_blk = dv_blk + _dot_tn(p.astype(BF16), dom)
                    dqg_ref[0, :, cols] = jnp.where(lane_lo, dqs[0], dqs[1]).astype(BF16)
                mine = lane_lo if kvh == 0 else jnp.logical_not(lane_lo)
                dk_add = dk_add + jnp.where(mine, dk_blk + pltpu.roll(dk_blk, HEAD_DIM, 1), 0.0)
                dv_add = dv_add + jnp.where(mine, dv_blk + pltpu.roll(dv_blk, HEAD_DIM, 1), 0.0)
            first_key = B_PREV if with_prev else 0
            if with_prev:
                rows = pl.ds(pl.multiple_of(j * qblk - B_PREV, B_PREV), B_PREV)
                dkv_ref[0, rows, :] += dk_add[0:B_PREV]
                dkv_ref[1, rows, :] += dv_add[0:B_PREV]
            rows = pl.ds(pl.multiple_of(j * qblk, qblk), qblk)
            dkv_ref[0, rows, :] += dk_add[first_key:first_key + qblk]
            dkv_ref[1, rows, :] += dv_add[first_key:first_key + qblk]

        pl.when(j == 0)(functools.partial(attend, False))
        pl.when(j >= 1)(functools.partial(attend, True))

        @pl.when(j == nq - 1)
        def _():
            dd = jnp.concatenate([_offset_sums(_collapse_chunks(gt_acc[g], keys)) for g in range(N_HEADS)], axis=0)
            hi = dd.astype(BF16)
            lo = (dd - hi.astype(F32)).astype(BF16)
            dt5_ref[...] = _dot(hi, oh_ref[...]) + _dot(lo, oh_ref[...])

    wide = lambda col: pl.BlockSpec((qblk, D_MODEL), lambda j, col=col: (j, col))
    prev = lambda col: pl.BlockSpec((B_PREV, 128), lambda j, col=col: (jnp.maximum(per * j - 1, 0), col))
    cur = lambda col: pl.BlockSpec((qblk, 128), lambda j, col=col: (j, col))
    fixed = lambda shape: pl.BlockSpec(shape, lambda j: (0,) * len(shape))
    return pl.pallas_call(
        body, name="attn_b_bwd", grid=(nq,),
        in_specs=[wide(0), wide(1), prev(0), cur(0), prev(1), cur(1), fixed((N_HEADS, keys)),
                  pl.BlockSpec(memory_space=pltpu.SMEM), wide(0), wide(0), wide(0), fixed((keys, 128))],
        out_specs=[pl.BlockSpec((2, qblk, D_MODEL), lambda j: (0, j, 0)), fixed((2, t, 128)),
                   fixed((N_HEADS, 128)), fixed((N_HEADS, 128))],
        out_shape=[SDS((2, t, D_MODEL), BF16), SDS((2, t, 128), F32),
                   SDS((N_HEADS, 128), F32), SDS((N_HEADS, 128), F32)],
        scratch_shapes=[pltpu.VMEM((N_HEADS, qblk, keys), F32), pltpu.VMEM((N_HEADS, qblk, keys), F32)],
        compiler_params=_cparams(),
    )(qg, qg, kv, kv, kv, kv, bias, sinks, out_b, lse, dz, bucket_onehot)


def _a_bias_by_offset(rel_bias):
    m = np.arange(A_DIAG)
    idx = np.clip(A_BAND - 1 - m, -A_REL_CLIP, A_REL_CLIP) + A_REL_CLIP
    by_head = rel_bias[idx].T.reshape(N_HEADS // 2, 2, A_DIAG)
    return jnp.concatenate([by_head, jnp.zeros((N_HEADS // 2, 6, A_DIAG), F32)], axis=1)


def _a_bias_grad(offset_sums):
    first = 319
    tail = jnp.sum(offset_sums[:, :first], axis=1)
    body = jnp.flip(offset_sums[:, first:first + 320], axis=1)
    body = body.at[:, -1].add(tail)
    full = jnp.concatenate([jnp.zeros((N_HEADS, 193), F32), body], axis=1)
    return full


def _t5_bucket(rel):
    nb = T5_BUCKETS // 2
    max_exact = nb // 2
    ret = jnp.where(rel > 0, nb, 0)
    n = jnp.abs(rel)
    nf = jnp.maximum(n, 1).astype(jnp.float32)
    large = max_exact + (jnp.log(nf / max_exact) / math.log(T5_MAX_DIST / max_exact)
                         * (nb - max_exact)).astype(jnp.int32)
    large = jnp.minimum(large, nb - 1)
    return ret + jnp.where(n < max_exact, n, large)


def _b_offset_buckets(keys):
    return _t5_bucket(jnp.arange(keys, dtype=jnp.int32) - (B_LEFT_CHUNKS * CHUNK + CHUNK - 1))


def _b_bias_by_offset(t5_table, keys):
    return t5_table[_b_offset_buckets(keys)].T


def _b_bucket_onehot(keys):
    return (_b_offset_buckets(keys)[:, None] == jnp.arange(128)[None, :]).astype(BF16)


def _local_step(my_slot, order, x, target, a_gain_shard, w_in_a_shard, rel_bias, late_shards, kv_gain,
                t5_table, b_gain, sinks, f_gain):
    a_bias = _a_bias_by_offset(rel_bias)
    b_bias_fwd = _b_bias_by_offset(t5_table, B_QBLK_FWD + B_PREV)
    b_bias_bwd = _b_bias_by_offset(t5_table, B_QBLK_BWD + B_PREV)
    sinks_flat = sinks.reshape(N_HEADS)

    xn, qkvg, w_in_a, a_gain = _norm_matmul_gather(order, x, a_gain_shard, w_in_a_shard)
    z_a, out_a, lse_a, (w_in_b, w_out_a, w_out_b, kv_w) = _attn_a_fwd(qkvg, a_bias, late_shards)
    w_out_a = w_out_a.reshape(D_MODEL, D_MODEL)
    w_out_b = w_out_b.reshape(D_MODEL, D_MODEL)
    kv_w = kv_w.reshape(D_MODEL, 2 * 128)
    h1, kvn, hb, kv, qg = _layer_a_out(x, z_a, w_out_a, kv_gain, b_gain, kv_w, w_in_b)
    z_b, out_b, lse_b = _attn_b_fwd(qg, kv, b_bias_fwd, sinks_flat)
    dh2, dh2b, dz_b, loss, d_fn = _layer_b_out_loss(h1, z_b, w_out_b, f_gain, target)

    dqg_b, dkv_b, d_t5, d_sink = _attn_b_bwd(qg, kv, b_bias_bwd, sinks_flat, out_b, lse_b, dz_b,
                                             _b_bucket_onehot(B_QBLK_BWD + B_PREV))
    dh1, dh1b, dz_a, d_bn, d_kn = _layer_b_in_bwd(dqg_b, dkv_b, w_in_b, kv_w, h1, dh2, b_gain, kv_gain, w_out_a)
    early = dict(
        b_w_out=_weight_grad_rows("grad_b_w_out", my_slot, z_b, dh2b[None]),
        b_w_in=_weight_grad_cols("grad_b_w_in", my_slot, hb, [dqg_b],
                                 [(0, o, c, 4 * o + c) for o in range(2) for c in range(4)], 256),
        kv_w=_weight_grad_rows("grad_kv_w", my_slot, kvn, dkv_b),
        a_w_out=_weight_grad_rows("grad_a_w_out", my_slot, z_a, dh1b[None]))
    dqg_a, dkv_a, d_rel, landed = _attn_a_bwd(qkvg, a_bias, out_a, lse_a, dz_a, [g[0] for g in early.values()])
    g_w_in_a = _weight_grad_cols(
        "grad_a_w_in", my_slot, xn, [dqg_a, dkv_a],
        [(0, 0, 0, 0), (0, 0, 1, 1), (1, 0, 0, 2), (1, 0, 1, 3), (1, 1, 0, 4), (1, 1, 1, 5), (0, 1, 0, 6), (0, 1, 1, 7)], 512)
    ready = dict(
        loss=loss, a_rel_bias=d_rel[:, :2].reshape(N_HEADS, A_DIAG),
        kv_norm=d_kn, t5_bias=d_t5, b_norm=d_bn, b_sinks=d_sink, final_norm=d_fn)
    chip_sums, from_sibling, ready_sums = _chip_sums(g_w_in_a[0], list(ready.values()))
    grad_x, d_an, from_chips = _layer_a_in_bwd(dqg_a, dkv_a, w_in_a, x, dh1, a_gain, chip_sums)

    matrices = {n: (g[1], [(land, 0, N_DEV - 1)]) for (n, g), land in zip(early.items(), landed)}
    matrices["a_w_in"] = (g_w_in_a[1], [(from_sibling, 0, 1), (from_chips, 0, 3)])
    small = dict(zip(ready.keys(), ready_sums), a_norm=d_an)
    return grad_x, small, matrices


def _place():
    x, y, c = lax.axis_index("x"), lax.axis_index("y"), lax.axis_index("c")
    chips = [(1 - x, y), (x, 1 - y), (1 - x, 1 - y)]
    return x, y, c, chips


def _slot(px, py, pc):
    return 4 * px + 2 * py + pc


ANY = pl.BlockSpec(memory_space=pl.ANY)


def _peer(x, y, c, k):
    return (x ^ (k >> 2), y ^ ((k >> 1) & 1), c ^ (k & 1))


def _scatter_copies(grad_refs, land_refs, send_sems, recv_sems):
    x, y, c, _ = _place()
    copies = []
    for t, (grad, land) in enumerate(zip(grad_refs, land_refs)):
        for k in range(1, N_DEV):
            peer = _peer(x, y, c, k)
            sem = (N_DEV - 1) * t + k - 1
            copies.append(pltpu.make_async_remote_copy(
                src_ref=grad.at[_slot(*peer)], dst_ref=land.at[k - 1],
                send_sem=send_sems.at[sem], recv_sem=recv_sems.at[sem],
                device_id=peer, device_id_type=MESH))
    return copies


def _gather_phases(ins, outs, send_sems, recv_sems, local_sems):
    n = len(ins)
    x, y, c, chips = _place()
    me, sibling = (x, y, c), (x, y, 1 - c)

    def copy(t, k, block, to, src=None):
        dst = outs[t].at[_slot(*block)]
        return pltpu.make_async_remote_copy(
            src_ref=dst if src is None else src, dst_ref=dst,
            send_sem=send_sems.at[7 * t + k], recv_sem=recv_sems.at[7 * t + k],
            device_id=to, device_id_type=MESH)

    def lists():
        mine = [pltpu.make_async_copy(ins[t], outs[t].at[_slot(*me)], local_sems.at[t]) for t in range(n)]
        first = []
        for t in range(n):
            first.append(copy(t, 0, me, sibling, src=ins[t]))
            first += [copy(t, 1 + j, me, (*chip, c), src=ins[t]) for j, chip in enumerate(chips)]
        passed = [copy(t, 4 + j, (*chip, c), sibling) for t in range(n) for j, chip in enumerate(chips)]
        return mine, first, passed

    def start():
        mine, first, _ = lists()
        for cp in mine + first:
            cp.start()

    def forward():
        _, _, passed = lists()
        for t in range(n):
            for j, chip in enumerate(chips):
                copy(t, 1 + j, (*chip, c), me).wait_recv()
                passed[3 * t + j].start()

    def finish():
        mine, first, passed = lists()
        for t in range(n):
            copy(t, 0, sibling, me).wait_recv()
            for j, chip in enumerate(chips):
                copy(t, 4 + j, (*chip, 1 - c), me).wait_recv()
        for cp in first + passed:
            cp.wait_send()
        for cp in mine:
            cp.wait()

    return start, forward, finish


def _gather_scratch(n):
    return [pltpu.SemaphoreType.DMA((7 * n,)), pltpu.SemaphoreType.DMA((7 * n,)), pltpu.SemaphoreType.DMA((n,))]


def _chip_sums(g, small):
    _, r, c = g.shape
    n_s = len(small)

    def body(g_ref, *rest):
        small_refs, rest = rest[:n_s], rest[n_s:]
        sums_ref, mine_ref = rest[:2]
        small_out, rest = rest[2:2 + n_s], rest[2 + n_s:]
        land, own, send_sems, recv_sems, load_sems = rest[:5]
        small_lands, (small_send, small_recv) = rest[5:5 + n_s], rest[5 + n_s:]
        x, y, c_i, chips = _place()
        sibling = (x, y, 1 - c_i)
        blocks = [(*chip, 1 - c_i) for chip in chips] + [sibling]
        sends = [pltpu.make_async_remote_copy(
            src_ref=g_ref.at[_slot(*block)], dst_ref=land.at[k], send_sem=send_sems.at[k],
            recv_sem=recv_sems.at[k], device_id=sibling, device_id_type=MESH) for k, block in enumerate(blocks)]
        loads = [pltpu.make_async_copy(g_ref.at[_slot(*chip, c_i)], own.at[j], load_sems.at[j])
                 for j, chip in enumerate(chips)]
        for cp in sends + loads:
            cp.start()
        _all_reduce_small(small_refs, small_out, small_lands, small_send, small_recv)
        for cp in sends + loads:
            cp.wait()
        for j in range(3):
            sums_ref[j] = (own[j].astype(F32) + land[j].astype(F32)).astype(BF16)
        mine_ref[0] = land[3]

    outs = pl.pallas_call(
        body, name="chip_sums",
        in_specs=[ANY] + [VM] * n_s, out_specs=[VM, VM] + [VM] * n_s,
        out_shape=[SDS((3, r, c), BF16), SDS((1, r, c), BF16)] + [SDS(s.shape, F32) for s in small],
        scratch_shapes=[pltpu.VMEM((4, r, c), BF16), pltpu.VMEM((3, r, c), BF16),
                        pltpu.SemaphoreType.DMA((4,)), pltpu.SemaphoreType.DMA((4,)), pltpu.SemaphoreType.DMA((3,))]
        + _all_reduce_scratch([s.shape for s in small]),
        compiler_params=_cparams(),
    )(g, *small)
    return outs[0], outs[1], list(outs[2:])


def _chip_copies(sums_ref, land_ref, send_sems, recv_sems):
    x, y, c, chips = _place()
    del x, y
    return [pltpu.make_async_remote_copy(
        src_ref=sums_ref.at[j], dst_ref=land_ref.at[j], send_sem=send_sems.at[j], recv_sem=recv_sems.at[j],
        device_id=(*chip, c), device_id_type=MESH) for j, chip in enumerate(chips)]


def _row_tile(rows):
    return min(rows, 512)


def _adamw(w, g, m, v):
    m2 = ADAM_B1 * m + (1.0 - ADAM_B1) * g
    v2 = ADAM_B2 * v + (1.0 - ADAM_B2) * jnp.square(g)
    m_hat = m2 / (1.0 - ADAM_B1 ** ADAM_STEP)
    v_hat = v2 / (1.0 - ADAM_B2 ** ADAM_STEP)
    delta = -ADAM_LR * (m_hat / (jnp.sqrt(v_hat) + ADAM_EPS) + ADAM_WD * w)
    return delta, m2, v2


def _reduce_adamw(name, own, partials, w, m, v):
    r, c = own.shape
    tr = _row_tile(r)
    n_p = len(partials)

    def body(own_ref, *rest):
        p_refs, (w_ref, m_ref, v_ref, grad_ref, d_ref, nm_ref, nv_ref) = rest[:n_p], rest[n_p:]
        grad = own_ref[...]
        for p_ref, (_, _, count) in zip(p_refs, partials):
            for j in range(count):
                grad = grad + p_ref[j].astype(F32)
        grad_ref[...] = grad
        d_ref[...], nm_ref[...], nv_ref[...] = _adamw(w_ref[...], grad, m_ref[...], v_ref[...])

    flat = pl.BlockSpec((tr, c), lambda i: (i, 0))
    return pl.pallas_call(
        body, name=name, grid=(r // tr,),
        in_specs=[flat] + [pl.BlockSpec((count, tr, c), lambda i, first=first, count=count: (first // count, i, 0))
                           for _, first, count in partials] + [flat, flat, flat],
        out_specs=[flat, flat, flat, flat],
        out_shape=[SDS((r, c), F32)] * 4,
        compiler_params=_cparams(),
    )(own, *[p[0] for p in partials], w, m, v)


VM = pl.BlockSpec()


def _all_reduce_small(ins, outs, lands, send_sems, recv_sems):
    x, y, c, _ = _place()
    my_slot = _slot(x, y, c)
    copies = []
    for t, (src, land) in enumerate(zip(ins, lands)):
        land[my_slot] = src[...]
        for k in range(1, N_DEV):
            sem = (N_DEV - 1) * t + k - 1
            copies.append(pltpu.make_async_remote_copy(
                src_ref=src, dst_ref=land.at[my_slot],
                send_sem=send_sems.at[sem], recv_sem=recv_sems.at[sem],
                device_id=_peer(x, y, c, k), device_id_type=MESH))
    for cp in copies:
        cp.start()
    for t, (src, land) in enumerate(zip(ins, lands)):
        for k in range(1, N_DEV):
            sem = (N_DEV - 1) * t + k - 1
            pltpu.make_async_remote_copy(
                src_ref=src, dst_ref=land.at[_slot(*_peer(x, y, c, k))],
                send_sem=send_sems.at[sem], recv_sem=recv_sems.at[sem],
                device_id=(x, y, c), device_id_type=MESH).wait_recv()
    for cp in copies:
        cp.wait_send()
    for out, land in zip(outs, lands):
        total = land[0]
        for s in range(1, N_DEV):
            total = total + land[s]
        out[...] = total


def _all_reduce_scratch(shapes):
    n_sems = (N_DEV - 1) * len(shapes)
    return ([pltpu.VMEM((N_DEV, *s), F32) for s in shapes]
            + [pltpu.SemaphoreType.DMA((n_sems,)), pltpu.SemaphoreType.DMA((n_sems,))])


def _small_adamw(my_slot, sums, ws, ms, vs):
    n = len(ws)

    def body(slot_ref, *refs):
        sum_refs, refs = refs[:n + 1], refs[n + 1:]
        w_refs, m_refs, v_refs, refs = refs[:n], refs[n:2 * n], refs[2 * n:3 * n], refs[3 * n:]
        g_refs, d_refs, nm_refs, nv_refs = refs[:n + 1], refs[n + 1:2 * n + 1], refs[2 * n + 1:3 * n + 1], refs[3 * n + 1:]
        for t in range(n + 1):
            if t == 0:
                g = sum_refs[0][:, pl.ds(pl.multiple_of(slot_ref[0] * 128, 128), 128)]
            else:
                g = sum_refs[t][...]
            g_refs[t][...] = g
            if t < n:
                d_refs[t][...], nm_refs[t][...], nv_refs[t][...] = _adamw(w_refs[t][...], g, m_refs[t][...], v_refs[t][...])

    shapes = [SDS(w.shape, F32) for w in ws]
    outs = pl.pallas_call(
        body, name="small_adamw",
        in_specs=[pl.BlockSpec(memory_space=pltpu.SMEM)] + [VM] * (4 * n + 1),
        out_specs=[VM] * (4 * n + 1),
        out_shape=shapes + [SDS(sums[-1].shape, F32)] + shapes * 3,
    )(my_slot, *sums, *ws, *ms, *vs)
    return outs[:n + 1], outs[n + 1:2 * n + 1], outs[2 * n + 1:3 * n + 1], outs[3 * n + 1:]


def kernel(x, a_norm, a_w_in, a_rel_bias, a_w_out, kv_norm, kv_w, t5_bias, b_norm, b_w_in, b_sinks, b_w_out, final_norm, loss_target, m_a_norm, m_a_w_in, m_a_rel_bias, m_a_w_out, m_kv_norm, m_kv_w, m_t5_bias, m_b_norm, m_b_w_in, m_b_sinks, m_b_w_out, m_final_norm, v_a_norm, v_a_w_in, v_a_rel_bias, v_a_w_out, v_kv_norm, v_kv_w, v_t5_bias, v_b_norm, v_b_w_in, v_b_sinks, v_b_w_out, v_final_norm):
    xi, yi, ci = lax.axis_index("x"), lax.axis_index("y"), lax.axis_index("c")
    my_slot = _slot(xi, yi, ci)

    slot_arr = jnp.reshape(my_slot, (1,)).astype(jnp.int32)
    order = _gather_order(xi, yi, ci)
    late_shards = [b_w_in[0].astype(BF16), a_w_out[0].astype(BF16), b_w_out[0].astype(BF16), kv_w.astype(BF16)]
    grad_x, loc, matrices = _local_step(
        slot_arr, order, x[0], loss_target[0], a_norm, a_w_in[0].astype(BF16), a_rel_bias[0], late_shards,
        kv_norm.reshape(1, D_MODEL), t5_bias, b_norm, b_sinks, final_norm.reshape(1, D_MODEL))

    shard_w = dict(a_w_in=a_w_in[0], b_w_in=b_w_in[0], a_w_out=a_w_out[0], b_w_out=b_w_out[0], kv_w=kv_w)
    shard_m = dict(a_w_in=m_a_w_in[0], b_w_in=m_b_w_in[0], a_w_out=m_a_w_out[0], b_w_out=m_b_w_out[0], kv_w=m_kv_w)
    shard_v = dict(a_w_in=v_a_w_in[0], b_w_in=v_b_w_in[0], a_w_out=v_a_w_out[0], b_w_out=v_b_w_out[0], kv_w=v_kv_w)
    big = {n: _reduce_adamw("adamw_" + n, own, partials, shard_w[n], shard_m[n], shard_v[n])
           for n, (own, partials) in matrices.items()}

    names = ("a_norm", "a_rel_bias", "kv_norm", "t5_bias", "b_norm", "b_sinks", "final_norm")
    tables = ("a_rel_bias", "t5_bias")

    def row(n, a):
        return a.reshape(-1, a.shape[-1]).T if n in tables else a.reshape(1, -1)

    small_w = [row(n, a) for n, a in zip(names, (a_norm, a_rel_bias, kv_norm, t5_bias, b_norm, b_sinks, final_norm))]
    small_m = [row(n, a) for n, a in zip(names, (m_a_norm, m_a_rel_bias, m_kv_norm, m_t5_bias, m_b_norm, m_b_sinks,
                                                 m_final_norm))]
    small_v = [row(n, a) for n, a in zip(names, (v_a_norm, v_a_rel_bias, v_kv_norm, v_t5_bias, v_b_norm, v_b_sinks,
                                                 v_final_norm))]
    sums = dict(loc)
    sums["a_rel_bias"] = _a_bias_grad(sums["a_rel_bias"])
    sums["t5_bias"] = sums["t5_bias"][:, :T5_BUCKETS]
    sums["b_sinks"] = sums["b_sinks"][:, 0].reshape(1, N_HEADS)
    results = _small_adamw(slot_arr, [sums[n] for n in names + ("loss",)], small_w, small_m, small_v)
    like = dict(a_norm=a_norm, a_rel_bias=a_rel_bias, kv_norm=kv_norm, t5_bias=t5_bias, b_norm=b_norm,
                b_sinks=b_sinks, final_norm=final_norm)
    sm = [{n: (part[i].T if n in tables else part[i]).reshape(like[n].shape) for i, n in enumerate(names)}
          for part in results]
    loss = results[0][len(names)][0, 0]

    order = ("a_norm", "a_w_in", "a_rel_bias", "a_w_out", "kv_norm", "kv_w", "t5_bias", "b_norm",
             "b_w_in", "b_sinks", "b_w_out", "final_norm")
    lead = dict(a_w_in=True, b_w_in=True, a_w_out=True, b_w_out=True, kv_w=False)

    def pick(kind, name):
        if name in big:
            val = big[name][kind]
            return val[None] if lead[name] else val
        return sm[kind][name]

    outs = [loss, grad_x[None]]
    for kind in range(4):
        outs += [pick(kind, n) for n in order]
    return tuple(outs)
```

```python
import functools
import math

import numpy as np
import jax
import jax.numpy as jnp
from jax import lax
from jax.experimental import pallas as pl
from jax.experimental.pallas import tpu as pltpu

F32 = jnp.float32
BF16 = jnp.bfloat16
SDS = jax.ShapeDtypeStruct

D_MODEL = 1024
HEAD_DIM = 64
CHUNK = 64
N_HEADS = 16
RMS_EPS = 1e-6
A_LEFT_CHUNKS = 8
A_BAND = (A_LEFT_CHUNKS + 1) * CHUNK
A_REL_CLIP = 256
B_KV_HEADS = 2
B_GROUP = 8
B_LEFT_CHUNKS = 2
B_BAND = (B_LEFT_CHUNKS + 1) * CHUNK
T5_BUCKETS = 32
T5_MAX_DIST = 128
QBLK = 256
A_KEYS = 3 * QBLK
B_QBLK_FWD = 128
B_QBLK_BWD = 256
B_PREV = 128
A_DIAG = A_KEYS
NEG = -1e30
SCALE = HEAD_DIM ** -0.5
N_DEV = 8

ADAM_LR = 0.001
ADAM_B1 = 0.9
ADAM_B2 = 0.999
ADAM_EPS = 1e-08
ADAM_WD = 0.01
ADAM_STEP = 10

VMEM_LIMIT_BYTES = 56 * 1024 * 1024
MESH = pl.DeviceIdType.MESH


def _cparams():
    return pltpu.CompilerParams(vmem_limit_bytes=VMEM_LIMIT_BYTES)


def _dot(a, b):
    return jnp.dot(a, b, preferred_element_type=F32)


def _dot_nt(a, b):
    return lax.dot_general(a, b, (((1,), (1,)), ((), ())), preferred_element_type=F32)


def _dot_tn(a, b):
    return lax.dot_general(a, b, (((0,), (0,)), ((), ())), preferred_element_type=F32)


def _rstd(xf):
    return lax.rsqrt(jnp.mean(xf * xf, axis=-1, keepdims=True) + RMS_EPS)


def _sigmoid(x):
    return 1.0 / (1.0 + jnp.exp(-x))


_GATHER_SEQUENCE = ((0, None), (1, 0), (2, 1), (4, None), (5, None), (3, 2), (6, None))


def _gather_order(x, y, c):
    others = [(1 - x, y), (x, 1 - y), (1 - x, 1 - y)]
    arrivals = [_slot(x, y, 1 - c)] + [_slot(*chip, c) for chip in others] + [_slot(*chip, 1 - c) for chip in others]
    return jnp.stack([_slot(x, y, c)] + [arrivals[a] for a, _ in _GATHER_SEQUENCE]).astype(jnp.int32)


def _norm_matmul_gather(order, x, gain_shard, w_shard):
    t = x.shape[0]
    dw, tn = w_shard.shape
    tm = min(t, 2048)
    n_m = t // tm

    def body(order_ref, x_ref, gs_ref, shard_ref, xn_ref, o_ref, full_ref, gain_ref,
             xn_all, wbuf, gland, send_sems, recv_sems, gsend_sems, grecv_sems, load_sems, own_sem):
        n, m = pl.program_id(0), pl.program_id(1)
        x_i, y_i, c_i, chips = _place()
        me, sibling = (x_i, y_i, c_i), (x_i, y_i, 1 - c_i)

        def send(k, block, to, src=None):
            dst = full_ref.at[_slot(*block)]
            return pltpu.make_async_remote_copy(
                src_ref=dst if src is None else src, dst_ref=dst,
                send_sem=send_sems.at[k], recv_sem=recv_sems.at[k], device_id=to, device_id_type=MESH)

        own = pltpu.make_async_copy(shard_ref, full_ref.at[_slot(*me)], own_sem)
        first = [send(0, me, sibling, src=shard_ref)]
        first += [send(1 + j, me, (*chip, c_i), src=shard_ref) for j, chip in enumerate(chips)]
        forwards = [send(4 + j, (*chip, c_i), sibling) for j, chip in enumerate(chips)]
        arrivals = [send(0, sibling, me)] + [send(1 + j, (*chip, c_i), me) for j, chip in enumerate(chips)]
        arrivals += [send(4 + j, (*chip, 1 - c_i), me) for j, chip in enumerate(chips)]
        gains = [pltpu.make_async_remote_copy(
            src_ref=gs_ref, dst_ref=gland.at[_slot(*me)], send_sem=gsend_sems.at[k - 1],
            recv_sem=grecv_sems.at[k - 1], device_id=_peer(x_i, y_i, c_i, k), device_id_type=MESH)
            for k in range(1, N_DEV)]

        @pl.when(jnp.logical_and(n == 0, m == 0))
        def _():
            own.start()
            for cp in gains + first:
                cp.start()
            pltpu.make_async_copy(shard_ref, wbuf.at[0], load_sems.at[0]).start()
            gland[_slot(*me)] = gs_ref[...]
            for k in range(1, N_DEV):
                pltpu.make_async_remote_copy(
                    src_ref=gs_ref, dst_ref=gland.at[_slot(*_peer(x_i, y_i, c_i, k))],
                    send_sem=gsend_sems.at[k - 1], recv_sem=grecv_sems.at[k - 1],
                    device_id=me, device_id_type=MESH).wait_recv()
            for s in range(N_DEV):
                gain_ref[:, 128 * s:128 * (s + 1)] = gland[s]

        rows = pl.ds(pl.multiple_of(m * tm, tm), tm)

        @pl.when(n == 0)
        def _():
            xf = x_ref[...]
            xn = ((xf * _rstd(xf)) * gain_ref[...]).astype(BF16)
            xn_all[rows, :] = xn
            xn_ref[...] = xn

        @pl.when(m == 0)
        def _():
            pltpu.make_async_copy(full_ref.at[0], wbuf.at[n % 2], load_sems.at[n % 2]).wait()

        o_ref[...] = _dot(xn_all[rows, :], wbuf[n % 2]).astype(BF16)

        for k, (arrival, forward) in enumerate(_GATHER_SEQUENCE):
            @pl.when(jnp.logical_and(n == k, m == n_m - 1))
            def _(k=k, arrival=arrival, forward=forward):
                arrivals[arrival].wait_recv()
                if forward is not None:
                    forwards[forward].start()
                pltpu.make_async_copy(full_ref.at[order_ref[k + 1]], wbuf.at[(k + 1) % 2],
                                      load_sems.at[(k + 1) % 2]).start()

        @pl.when(jnp.logical_and(n == N_DEV - 1, m == n_m - 1))
        def _():
            for cp in gains + first + forwards:
                cp.wait_send()
            own.wait()

    held = lambda n, m, order: (jnp.where(n == 0, m, n_m - 1), 0)
    return pl.pallas_call(
        body, name="norm_matmul_gather",
        grid_spec=pltpu.PrefetchScalarGridSpec(
            num_scalar_prefetch=1, grid=(N_DEV, n_m),
            in_specs=[pl.BlockSpec((tm, D_MODEL), held),
                      pl.BlockSpec((1, 128), lambda n, m, order: (0, 0)), ANY],
            out_specs=[pl.BlockSpec((tm, D_MODEL), held),
                       pl.BlockSpec((tm, tn), lambda n, m, order: (m, order[n])),
                       ANY, pl.BlockSpec((1, D_MODEL), lambda n, m, order: (0, 0))],
            scratch_shapes=[pltpu.VMEM((t, D_MODEL), BF16), pltpu.VMEM((2, dw, tn), BF16),
                            pltpu.VMEM((N_DEV, 1, 128), F32),
                            pltpu.SemaphoreType.DMA((7,)), pltpu.SemaphoreType.DMA((7,)),
                            pltpu.SemaphoreType.DMA((7,)), pltpu.SemaphoreType.DMA((7,)),
                            pltpu.SemaphoreType.DMA((2,)), pltpu.SemaphoreType.DMA]),
        out_shape=[SDS((t, D_MODEL), BF16), SDS((t, N_DEV * tn), BF16), SDS((N_DEV, dw, tn), BF16),
                   SDS((1, D_MODEL), F32)],
        compiler_params=_cparams(),
    )(order, x, gain_shard, w_shard)


def _layer_a_out(x, z, w_out, kv_gain, b_gain, kv_w, w_in_b):
    t = x.shape[0]
    tm = min(t, 1024)
    nb, _, tn = w_in_b.shape

    def body(x_ref, z_ref, wo_ref, kvg_ref, bg_ref, kvw_ref, wb_ref,
             h1_ref, kvn_ref, hb_ref, kv_ref, qg_ref):
        h1 = x_ref[...] + _dot(z_ref[...], wo_ref[...])
        h1_ref[...] = h1
        y0 = h1 * _rstd(h1)
        kvn = (y0 * kvg_ref[...]).astype(BF16)
        hb = (y0 * bg_ref[...]).astype(BF16)
        kvn_ref[...] = kvn
        hb_ref[...] = hb
        kv_ref[...] = _dot(kvn, kvw_ref[...]).astype(BF16)
        for i in range(nb):
            qg_ref[:, i * tn:(i + 1) * tn] = _dot(hb, wb_ref[i]).astype(BF16)

    row = lambda m: (m, 0)
    fix2 = lambda m: (0, 0)
    return pl.pallas_call(
        body, name="layer_a_out", grid=(t // tm,),
        in_specs=[pl.BlockSpec((tm, D_MODEL), row), pl.BlockSpec((tm, D_MODEL), row),
                  pl.BlockSpec((D_MODEL, D_MODEL), fix2),
                  pl.BlockSpec((1, D_MODEL), fix2), pl.BlockSpec((1, D_MODEL), fix2),
                  pl.BlockSpec((D_MODEL, 256), fix2),
                  pl.BlockSpec((nb, D_MODEL, tn), lambda m: (0, 0, 0))],
        out_specs=[pl.BlockSpec((tm, D_MODEL), row), pl.BlockSpec((tm, D_MODEL), row),
                   pl.BlockSpec((tm, D_MODEL), row), pl.BlockSpec((tm, 256), row),
                   pl.BlockSpec((tm, nb * tn), row)],
        out_shape=[SDS((t, D_MODEL), F32), SDS((t, D_MODEL), BF16), SDS((t, D_MODEL), BF16),
                   SDS((t, 256), BF16), SDS((t, nb * tn), BF16)],
        compiler_params=_cparams(),
    )(x, z, w_out, kv_gain, b_gain, kv_w, w_in_b)


def _layer_b_out_loss(h1, z, w_out, f_gain, target):
    t = h1.shape[0]
    tm = min(t, 1024)

    def body(h1_ref, z_ref, wo_ref, fg_ref, tgt_ref,
             dh2_ref, dh2b_ref, dz_ref, loss_ref, dfn_ref):
        @pl.when(pl.program_id(0) == 0)
        def _():
            loss_ref[...] = jnp.zeros_like(loss_ref)
            dfn_ref[...] = jnp.zeros_like(dfn_ref)

        h2 = h1_ref[...] + _dot(z_ref[...], wo_ref[...])
        r = _rstd(h2)
        yn = h2 * r
        fg = fg_ref[...]
        err = yn * fg - tgt_ref[...]
        loss_ref[...] += (0.5 / D_MODEL) * jnp.sum(err * err)
        dy = err * (1.0 / D_MODEL)
        dfn_ref[...] += jnp.sum(dy * yn, axis=0, keepdims=True)
        u = dy * fg
        dh2 = r * u - h2 * ((r * r * r) * jnp.mean(u * h2, axis=-1, keepdims=True))
        dh2_ref[...] = dh2
        dh2b = dh2.astype(BF16)
        dh2b_ref[...] = dh2b
        dz_ref[...] = _dot_nt(dh2b, wo_ref[...]).astype(BF16)

    row = lambda m: (m, 0)
    fix2 = lambda m: (0, 0)
    return pl.pallas_call(
        body, name="layer_b_out_loss", grid=(t // tm,),
        in_specs=[pl.BlockSpec((tm, D_MODEL), row), pl.BlockSpec((tm, D_MODEL), row),
                  pl.BlockSpec((D_MODEL, D_MODEL), fix2), pl.BlockSpec((1, D_MODEL), fix2),
                  pl.BlockSpec((tm, D_MODEL), row)],
        out_specs=[pl.BlockSpec((tm, D_MODEL), row), pl.BlockSpec((tm, D_MODEL), row),
                   pl.BlockSpec((tm, D_MODEL), row), pl.BlockSpec((1, 128), fix2),
                   pl.BlockSpec((1, D_MODEL), fix2)],
        out_shape=[SDS((t, D_MODEL), F32), SDS((t, D_MODEL), BF16), SDS((t, D_MODEL), BF16),
                   SDS((1, 128), F32), SDS((1, D_MODEL), F32)],
        compiler_params=_cparams(),
    )(h1, z, w_out, f_gain, target)


def _layer_b_in_bwd(dqg, dkv, w_in_b, kv_w, h1, dh2, b_gain, kv_gain, w_out_a):
    t = h1.shape[0]
    tm = min(t, 512)
    nb, _, tn = w_in_b.shape
    per = D_MODEL // tn

    def body(dqg_ref, dkv_ref, wb_ref, kvw_ref, h1_ref, dh2_ref, bg_ref, kvg_ref, wo_ref,
             dh1_ref, dh1b_ref, dz_ref, dbn_ref, dkn_ref):
        @pl.when(pl.program_id(0) == 0)
        def _():
            dbn_ref[...] = jnp.zeros_like(dbn_ref)
            dkn_ref[...] = jnp.zeros_like(dkn_ref)

        dhb = jnp.zeros((tm, D_MODEL), F32)
        for i in range(nb):
            blk = dqg_ref[i // per, :, (i % per) * tn:(i % per + 1) * tn]
            dhb = dhb + _dot_nt(blk, wb_ref[i])
        dkn = (_dot_nt(dkv_ref[0].astype(BF16), kvw_ref[:, 0:128])
               + _dot_nt(dkv_ref[1].astype(BF16), kvw_ref[:, 128:256]))
        h1 = h1_ref[...]
        r = _rstd(h1)
        xr = h1 * r
        dbn_ref[...] += jnp.sum(dhb * xr, axis=0, keepdims=True)
        dkn_ref[...] += jnp.sum(dkn * xr, axis=0, keepdims=True)
        u = dhb * bg_ref[...] + dkn * kvg_ref[...]
        dh1 = dh2_ref[...] + r * u - h1 * ((r * r * r) * jnp.mean(u * h1, axis=-1, keepdims=True))
        dh1_ref[...] = dh1
        dh1b = dh1.astype(BF16)
        dh1b_ref[...] = dh1b
        dz_ref[...] = _dot_nt(dh1b, wo_ref[...]).astype(BF16)

    row = lambda m: (m, 0)
    fix2 = lambda m: (0, 0)
    return pl.pallas_call(
        body, name="layer_b_in_bwd", grid=(t // tm,),
        in_specs=[pl.BlockSpec((2, tm, D_MODEL), lambda m: (0, m, 0)),
                  pl.BlockSpec((2, tm, 128), lambda m: (0, m, 0)),
                  pl.BlockSpec((nb, D_MODEL, tn), lambda m: (0, 0, 0)),
                  pl.BlockSpec((D_MODEL, 256), fix2),
                  pl.BlockSpec((tm, D_MODEL), row), pl.BlockSpec((tm, D_MODEL), row),
                  pl.BlockSpec((1, D_MODEL), fix2), pl.BlockSpec((1, D_MODEL), fix2),
                  pl.BlockSpec((D_MODEL, D_MODEL), fix2)],
        out_specs=[pl.BlockSpec((tm, D_MODEL), row), pl.BlockSpec((tm, D_MODEL), row),
                   pl.BlockSpec((tm, D_MODEL), row), pl.BlockSpec((1, D_MODEL), fix2),
                   pl.BlockSpec((1, D_MODEL), fix2)],
        out_shape=[SDS((t, D_MODEL), F32), SDS((t, D_MODEL), BF16), SDS((t, D_MODEL), BF16),
                   SDS((1, D_MODEL), F32), SDS((1, D_MODEL), F32)],
        compiler_params=_cparams(),
    )(dqg, dkv, w_in_b, kv_w, h1, dh2, b_gain, kv_gain, w_out_a)


def _layer_a_in_bwd(dqg, dkv, w_in_a, x, dh1, a_gain):
    t = x.shape[0]
    tm = min(t, 512)
    nb, _, tn = w_in_a.shape
    per = D_MODEL // tn

    def body(dqg_ref, dkv_ref, w_ref, x_ref, dh1_ref, ag_ref, dx_ref, dan_ref):
        @pl.when(pl.program_id(0) == 0)
        def _():
            dan_ref[...] = jnp.zeros_like(dan_ref)

        dxn = jnp.zeros((tm, D_MODEL), F32)
        for i in range(nb):
            part = i // per
            src = dqg_ref if part in (0, 3) else dkv_ref
            outer = {0: 0, 3: 1, 1: 0, 2: 1}[part]
            blk = src[outer, :, (i % per) * tn:(i % per + 1) * tn]
            dxn = dxn + _dot_nt(blk, w_ref[i])
        xf = x_ref[...]
        r = _rstd(xf)
        dan_ref[...] += jnp.sum(dxn * (xf * r), axis=0, keepdims=True)
        u = dxn * ag_ref[...]
        dx_ref[...] = dh1_ref[...] + r * u - xf * ((r * r * r) * jnp.mean(u * xf, axis=-1, keepdims=True))

    row = lambda m: (m, 0)
    fix2 = lambda m: (0, 0)
    return pl.pallas_call(
        body, name="layer_a_in_bwd", grid=(t // tm,),
        in_specs=[pl.BlockSpec((2, tm, D_MODEL), lambda m: (0, m, 0)),
                  pl.BlockSpec((2, tm, D_MODEL), lambda m: (0, m, 0)),
                  pl.BlockSpec((nb, D_MODEL, tn), lambda m: (0, 0, 0)),
                  pl.BlockSpec((tm, D_MODEL), row), pl.BlockSpec((tm, D_MODEL), row),
                  pl.BlockSpec((1, D_MODEL), fix2)],
        out_specs=[pl.BlockSpec((tm, D_MODEL), row), pl.BlockSpec((1, D_MODEL), fix2)],
        out_shape=[SDS((t, D_MODEL), F32), SDS((1, D_MODEL), F32)],
        compiler_params=_cparams(),
    )(dqg, dkv, w_in_a, x, dh1, a_gain)


def _lut(s, vals):
    r = jnp.int32(vals[0])
    for i in range(1, len(vals)):
        r = jnp.where(s == i, jnp.int32(vals[i]), r)
    return r


def _held(steps, i):
    seq, cur = [None] * len(steps), None
    for k in range(len(steps) - 1, -1, -1):
        if steps[k][0] == i:
            cur = steps[k][1:3]
        seq[k] = cur
    for k in range(len(steps)):
        cur = seq[k] = seq[k] if seq[k] is not None else cur
    return seq


def _weight_grad_cols(name, my_slot, a, bs, steps, tn):
    t, dw = a.shape
    n_arr = len(bs)
    which = [s[0] for s in steps]
    blks = [s[3] for s in steps]

    def body(slot_ref, a_ref, *rest):
        b_refs, (o_ref, own_ref, at_ref) = rest[:n_arr], rest[n_arr:]
        s = pl.program_id(0)

        @pl.when(s == 0)
        def _():
            at_ref[...] = a_ref[...].T

        for i in range(n_arr):
            @pl.when(_lut(s, which) == i)
            def _(i=i):
                res = _dot(at_ref[...], b_refs[i][0])
                o_ref[0] = res.astype(BF16)

                @pl.when(_lut(s, blks) == slot_ref[0])
                def _():
                    own_ref[...] = res

    def b_spec(i):
        held = _held(steps, i)
        return pl.BlockSpec((1, t, tn), lambda s, slot: (_lut(s, [h[0] for h in held]), 0,
                                                         _lut(s, [h[1] for h in held])))

    return pl.pallas_call(
        body, name=name,
        grid_spec=pltpu.PrefetchScalarGridSpec(
            num_scalar_prefetch=1, grid=(len(steps),),
            in_specs=[pl.BlockSpec((t, dw), lambda s, slot: (0, 0))] + [b_spec(i) for i in range(n_arr)],
            out_specs=[pl.BlockSpec((1, dw, tn), lambda s, slot: (_lut(s, blks), 0, 0)),
                       pl.BlockSpec((dw, tn), lambda s, slot: (0, 0))],
            scratch_shapes=[pltpu.VMEM((dw, t), BF16)]),
        out_shape=[SDS((N_DEV, dw, tn), BF16), SDS((dw, tn), F32)],
        compiler_params=_cparams(),
    )(my_slot, a, *bs)


def _weight_grad_rows(name, my_slot, a, b):
    t, dw = a.shape
    n_o, _, c = b.shape
    rows = dw // N_DEV
    tn = min(c, 256)
    per = c // tn

    def body(slot_ref, a_ref, b_ref, o_ref, own_ref, at_ref, res_ref):
        @pl.when(pl.program_id(0) == 0)
        def _():
            at_ref[...] = a_ref[...].T

        res_ref[...] = _dot(at_ref[...], b_ref[0].astype(BF16))
        o_ref[...] = res_ref[...].astype(BF16)
        own_ref[...] = res_ref[pl.ds(pl.multiple_of(slot_ref[0] * rows, rows), rows), :]

    all_rows, own = pl.pallas_call(
        body, name=name,
        grid_spec=pltpu.PrefetchScalarGridSpec(
            num_scalar_prefetch=1, grid=(n_o * per,),
            in_specs=[pl.BlockSpec((t, dw), lambda s, slot: (0, 0)),
                      pl.BlockSpec((1, t, tn), lambda s, slot: (s // per, 0, s % per))],
            out_specs=[pl.BlockSpec((dw, tn), lambda s, slot: (0, s)),
                       pl.BlockSpec((rows, tn), lambda s, slot: (0, s))],
            scratch_shapes=[pltpu.VMEM((dw, t), BF16), pltpu.VMEM((dw, tn), F32)]),
        out_shape=[SDS((dw, n_o * c), BF16), SDS((rows, n_o * c), F32)],
        compiler_params=_cparams(),
    )(my_slot, a, b)
    return all_rows.reshape(N_DEV, rows, n_o * c), own


def _lane_lo():
    return lax.broadcasted_iota(jnp.int32, (1, 128), 1) < HEAD_DIM


def _collapse_chunks(ds, keys):
    if ds.shape[1] < keys:
        ds = jnp.concatenate([jnp.zeros((ds.shape[0], keys - ds.shape[1]), F32), ds], axis=1)
    gc = ds[0:CHUNK]
    for cc in range(1, ds.shape[0] // CHUNK):
        gc = gc + pltpu.roll(ds[cc * CHUNK:(cc + 1) * CHUNK], keys - cc * CHUNK, 1)
    return gc


def _offset_sums(gc):
    hi = gc.astype(BF16)
    lo = (gc - hi.astype(F32)).astype(BF16)
    flip = (lax.broadcasted_iota(jnp.int32, (CHUNK, CHUNK), 0)
            + lax.broadcasted_iota(jnp.int32, (CHUNK, CHUNK), 1) == CHUNK - 1).astype(BF16)
    gf = _dot(flip, hi) + _dot(flip, lo)
    skew = pltpu.roll(gf, 0, 1, stride=1, stride_axis=0)
    return jnp.sum(skew, axis=0, keepdims=True)


def _band_bias(w_row, band, rows):
    keys = w_row.shape[1]
    base = jnp.broadcast_to(w_row, (CHUNK, keys))
    skew = pltpu.roll(base, 0, 1, stride=1, stride_axis=0)
    skew = pltpu.roll(skew, keys - (CHUNK - 1), 1)
    col = lax.broadcasted_iota(jnp.int32, (CHUNK, keys), 1)
    chunk0 = jnp.where(col < band, skew, NEG)
    return jnp.concatenate(
        [chunk0] + [pltpu.roll(chunk0, cc * CHUNK, 1) for cc in range(1, rows // CHUNK)], axis=0)


def _silu_parts(g):
    sg = _sigmoid(g)
    return g * sg, sg * (1.0 + g * (1.0 - sg))


A_PAIRS_FWD = 8
A_PAIRS_BWD = 4


def _a_specs(pairs):
    lanes = 128 * pairs
    steps = D_MODEL // lanes
    q = pl.BlockSpec((QBLK, lanes), lambda p, j: (j, p))
    ks = [pl.BlockSpec((QBLK, lanes), lambda p, j, b=b: (jnp.maximum(j - 2 + b, 0), steps + p)) for b in range(3)]
    vs = [pl.BlockSpec((QBLK, lanes), lambda p, j, b=b: (jnp.maximum(j - 2 + b, 0), 2 * steps + p))
          for b in range(3)]
    g = pl.BlockSpec((QBLK, lanes), lambda p, j: (j, 3 * steps + p))
    bias = pl.BlockSpec((pairs, 8, A_KEYS), lambda p, j: (p, 0, 0))
    return q, ks, vs, g, bias


def _a_fill_bias(w_ref, b_ref, j, pairs):
    _fill_bias(2 * pairs, lambda h: w_ref[h // 2, h % 2:h % 2 + 1, :], A_BAND, b_ref, j)


def _by_valid_key_blocks(j, fn):
    pl.when(j == 0)(functools.partial(fn, 1))
    pl.when(j == 1)(functools.partial(fn, 2))
    pl.when(j >= 2)(functools.partial(fn, 3))


def _fill_bias(n, get_row, band, bias_scr, j):
    @pl.when(j == 0)
    def _():
        for h in range(n):
            bias_scr[h] = _band_bias(get_row(h), band, bias_scr.shape[1])


def _normalise_pair(rs, mxs, lane_lo, extra=None):
    num = jnp.where(lane_lo, rs[0], rs[1])
    den = pltpu.roll(jnp.where(lane_lo, rs[1], rs[0]), HEAD_DIM, 1)
    if extra is not None:
        den = den + jnp.where(lane_lo, extra[0], extra[1])
    return num / den, jnp.where(lane_lo, mxs[0], mxs[1]) + jnp.log(den)


def _own_everywhere(x, sel):
    return jnp.where(sel, x, pltpu.roll(x, HEAD_DIM, 1))


def _minus_rows(s, row_full):
    return jnp.concatenate([s[:, i:i + 128] - row_full for i in range(0, s.shape[1], 128)], axis=1)


def _attn_a_fwd(qkvg, bias, gather):
    t = qkvg.shape[0]
    nq = t // QBLK
    n_g = len(gather)
    pairs = A_PAIRS_FWD
    lanes = 128 * pairs
    steps = D_MODEL // lanes
    q_spec, k_specs, v_specs, g_spec, bias_spec = _a_specs(pairs)

    def body(q_ref, k0, k1, k2, v0, v1, v2, g_ref, w_ref, *rest):
        shard_refs, rest = rest[:n_g], rest[n_g:]
        z_ref, o_ref, lse_ref = rest[:3]
        full_refs, (b_ref, *comm) = rest[3:3 + n_g], rest[3 + n_g:]
        p = pl.program_id(0)
        j = pl.program_id(1)
        start, forward, finish = _gather_phases(shard_refs, full_refs, *comm)
        at = p * nq + j
        pl.when(at == 0)(start)
        pl.when(at == steps * nq // 2)(forward)
        _a_fill_bias(w_ref, b_ref, j, pairs)
        lane_lo = _lane_lo()
        sels = (lane_lo, jnp.logical_not(lane_lo))

        def attend(n_blocks):
            first_col = (3 - n_blocks) * QBLK
            for pp in range(pairs):
                cols = slice(128 * pp, 128 * (pp + 1))
                k = jnp.concatenate([r[:, cols] for r in (k0, k1, k2)[3 - n_blocks:]], axis=0)
                v = jnp.concatenate([r[:, cols] for r in (v0, v1, v2)[3 - n_blocks:]], axis=0)
                q = q_ref[:, cols]
                qm2 = jnp.concatenate([jnp.where(sel, q, jnp.zeros_like(q)) for sel in sels], axis=0) * SCALE
                s2 = _dot_nt(qm2, k)
                rs, mxs = [], []
                for hh, sel in enumerate(sels):
                    s = s2[hh * QBLK:(hh + 1) * QBLK] + b_ref[2 * pp + hh, :, first_col:]
                    mxs.append(jnp.max(s, axis=-1, keepdims=True))
                    e = jnp.exp(s - mxs[hh]).astype(BF16)
                    rs.append(_dot(e, jnp.where(sel, v, jnp.ones_like(v))))
                o, lse = _normalise_pair(rs, mxs, lane_lo)
                silu, _ = _silu_parts(g_ref[:, cols].astype(F32))
                o_ref[:, cols] = o.astype(BF16)
                z_ref[:, cols] = (o * silu).astype(BF16)
                lse_ref[:, cols] = lse

        _by_valid_key_blocks(j, attend)
        pl.when(at == steps * nq - 1)(finish)

    out_spec = pl.BlockSpec((QBLK, lanes), lambda p, j: (j, p))
    outs = pl.pallas_call(
        body, name="attn_a_fwd", grid=(steps, nq),
        in_specs=[q_spec, *k_specs, *v_specs, g_spec, bias_spec] + [ANY] * n_g,
        out_specs=[out_spec, out_spec, out_spec] + [ANY] * n_g,
        out_shape=[SDS((t, D_MODEL), BF16), SDS((t, D_MODEL), BF16), SDS((t, D_MODEL), F32)]
        + [SDS((N_DEV, *s.shape), s.dtype) for s in gather],
        scratch_shapes=[pltpu.VMEM((2 * pairs, QBLK, A_KEYS), F32)] + _gather_scratch(n_g),
        compiler_params=_cparams(),
    )(qkvg, qkvg, qkvg, qkvg, qkvg, qkvg, qkvg, qkvg, bias, *gather)
    return outs[0], outs[1], outs[2], list(outs[3:])


def _attn_a_bwd(qkvg, bias, out_a, lse, dz, scatter):
    t = qkvg.shape[0]
    nq = t // QBLK
    n_sc = len(scatter)
    pairs = A_PAIRS_BWD
    lanes = 128 * pairs
    steps = D_MODEL // lanes
    q_spec, k_specs, v_specs, g_spec, bias_spec = _a_specs(pairs)

    def body(q_ref, k0, k1, k2, v0, v1, v2, g_ref, w_ref, o_ref, lse_ref, dz_ref, *rest):
        sc_refs, rest = rest[:n_sc], rest[n_sc:]
        dqg_ref, dkv_ref, dg_ref = rest[:3]
        land_refs, rest = rest[3:3 + n_sc], rest[3 + n_sc:]
        dk_acc, dv_acc, gt_acc, b_ref, send_sems, recv_sems = rest
        j = pl.program_id(1)
        first = jnp.logical_and(pl.program_id(0) == 0, j == 0)
        last = jnp.logical_and(pl.program_id(0) == steps - 1, j == nq - 1)

        @pl.when(first)
        def _():
            for cp in _scatter_copies(sc_refs, land_refs, send_sems, recv_sems):
                cp.start()

        _a_fill_bias(w_ref, b_ref, j, pairs)

        @pl.when(j == 0)
        def _():
            dk_acc[...] = jnp.zeros_like(dk_acc)
            dv_acc[...] = jnp.zeros_like(dv_acc)
            gt_acc[...] = jnp.zeros_like(gt_acc)

        lane_lo = _lane_lo()
        sels = (lane_lo, jnp.logical_not(lane_lo))

        def attend(n_blocks):
            first_col = (3 - n_blocks) * QBLK
            for pp in range(pairs):
                cols = slice(128 * pp, 128 * (pp + 1))
                q = q_ref[:, cols]
                k = jnp.concatenate([r[:, cols] for r in (k0, k1, k2)[3 - n_blocks:]], axis=0)
                v = jnp.concatenate([r[:, cols] for r in (v0, v1, v2)[3 - n_blocks:]], axis=0)
                o = o_ref[:, cols].astype(F32)
                lse_pair = lse_ref[:, cols]
                dzf = dz_ref[:, cols].astype(F32)
                silu, dsilu = _silu_parts(g_ref[:, cols].astype(F32))
                do = dzf * silu
                dqg_ref[1, :, cols] = (dzf * o * dsilu).astype(BF16)
                doo = do * o
                qm2 = jnp.concatenate([jnp.where(sel, q, jnp.zeros_like(q)) for sel in sels], axis=0) * SCALE
                dom2 = jnp.concatenate([jnp.where(sel, do, 0.0) for sel in sels], axis=0).astype(BF16)
                s2 = _dot_nt(qm2, k)
                dp2 = _dot_nt(dom2, v)
                ps, dss = [], []
                for hh, sel in enumerate(sels):
                    rows = slice(hh * QBLK, (hh + 1) * QBLK)
                    s = s2[rows] + b_ref[2 * pp + hh, :, first_col:]
                    p = jnp.exp(_minus_rows(s, _own_everywhere(lse_pair, sel)))
                    delta = jnp.sum(jnp.where(sel, doo, 0.0), axis=-1, keepdims=True)
                    ds = p * (dp2[rows] - delta)
                    gt_acc[2 * pp + hh] += _collapse_chunks(ds, A_KEYS)
                    ps.append(p.astype(BF16))
                    dss.append(ds.astype(BF16))
                dsb2 = jnp.concatenate(dss, axis=0)
                dq2 = _dot(dsb2, k) * SCALE
                dk_blk = _dot_tn(dsb2, qm2)
                dv_blk = _dot_tn(jnp.concatenate(ps, axis=0), dom2)
                dqg_ref[0, :, cols] = jnp.where(lane_lo, dq2[0:QBLK], dq2[QBLK:2 * QBLK]).astype(BF16)
                for b in range(n_blocks):
                    rows = pl.ds(pl.multiple_of((j - n_blocks + 1 + b) * QBLK, QBLK), QBLK)
                    dk_acc[rows, cols] += dk_blk[b * QBLK:(b + 1) * QBLK]
                    dv_acc[rows, cols] += dv_blk[b * QBLK:(b + 1) * QBLK]

        _by_valid_key_blocks(j, attend)

        @pl.when(j == nq - 1)
        def _():
            dkv_ref[0] = dk_acc[...].astype(BF16)
            dkv_ref[1] = dv_acc[...].astype(BF16)
            for pp in range(pairs):
                dg_ref[pp] = jnp.concatenate([_offset_sums(gt_acc[2 * pp]), _offset_sums(gt_acc[2 * pp + 1]),
                                              jnp.zeros((6, A_DIAG), F32)], axis=0)

        @pl.when(last)
        def _():
            for cp in _scatter_copies(sc_refs, land_refs, send_sems, recv_sems):
                cp.wait()

    blk = pl.BlockSpec((QBLK, lanes), lambda p, j: (j, p))
    outs = pl.pallas_call(
        body, name="attn_a_bwd", grid=(steps, nq),
        in_specs=[q_spec, *k_specs, *v_specs, g_spec, bias_spec, blk, blk, blk] + [ANY] * n_sc,
        out_specs=[pl.BlockSpec((2, QBLK, lanes), lambda p, j: (0, j, p)),
                   pl.BlockSpec((2, t, lanes), lambda p, j: (0, 0, p)),
                   pl.BlockSpec((pairs, 8, A_DIAG), lambda p, j: (p, 0, 0))] + [ANY] * n_sc,
        out_shape=[SDS((2, t, D_MODEL), BF16), SDS((2, t, D_MODEL), BF16), SDS((N_HEADS // 2, 8, A_DIAG), F32)]
        + [SDS((N_DEV - 1, *g.shape[1:]), g.dtype) for g in scatter],
        scratch_shapes=[pltpu.VMEM((t, lanes), F32), pltpu.VMEM((t, lanes), F32),
                        pltpu.VMEM((2 * pairs, CHUNK, A_KEYS), F32), pltpu.VMEM((2 * pairs, QBLK, A_KEYS), F32),
                        pltpu.SemaphoreType.DMA(((N_DEV - 1) * n_sc,)),
                        pltpu.SemaphoreType.DMA(((N_DEV - 1) * n_sc,))],
        compiler_params=_cparams(),
    )(qkvg, qkvg, qkvg, qkvg, qkvg, qkvg, qkvg, qkvg, bias, out_a, lse, dz, *scatter)
    return outs[0], outs[1], outs[2], list(outs[3:])


def _b_specs(qblk):
    per = qblk // B_PREV
    q = pl.BlockSpec((qblk, 512), lambda h, j: (j, h))
    g = pl.BlockSpec((qblk, 512), lambda h, j: (j, 2 + h))
    kp = pl.BlockSpec((B_PREV, 128), lambda h, j: (jnp.maximum(per * j - 1, 0), 0))
    kc = pl.BlockSpec((qblk, 128), lambda h, j: (j, 0))
    vp = pl.BlockSpec((B_PREV, 128), lambda h, j: (jnp.maximum(per * j - 1, 0), 1))
    vc = pl.BlockSpec((qblk, 128), lambda h, j: (j, 1))
    bias = pl.BlockSpec((B_GROUP, qblk + B_PREV), lambda h, j: (h, 0))
    sinks = pl.BlockSpec(memory_space=pltpu.SMEM)
    return q, g, kp, kc, vp, vc, bias, sinks


def _b_operands(kp, kc, vp, vc, kvh, with_prev):
    k = jnp.concatenate([kp[...], kc[...]], axis=0) if with_prev else kc[...]
    v = jnp.concatenate([vp[...], vc[...]], axis=0) if with_prev else vc[...]
    kr = pltpu.roll(k, HEAD_DIM, 1)
    vr = pltpu.roll(v, HEAD_DIM, 1)
    first = kvh == 0
    return (jnp.where(first, k, kr), jnp.where(first, kr, k),
            jnp.where(first, v, vr), jnp.where(first, vr, v))


def _attn_b_fwd(qg, kv, bias, sinks):
    t = qg.shape[0]
    qblk = B_QBLK_FWD
    per_step = 4
    step = per_step * qblk
    q_spec, g_spec, kp_spec, kc_spec, vp_spec, vc_spec, _, sink_spec = _b_specs(step)
    bias_spec = pl.BlockSpec((B_GROUP, qblk + B_PREV), lambda h, j: (h, 0))

    def body(q_ref, g_ref, kp, kc, vp, vc, w_ref, sink_ref, z_ref, o_ref, lse_ref, b_ref):
        kvh = pl.program_id(0)
        j = pl.program_id(1)
        _fill_bias(B_GROUP, lambda h: w_ref[h:h + 1, :], B_BAND, b_ref, j)
        lane_lo = _lane_lo()
        n_pairs = B_GROUP // 2

        def attend(first):
            k_lo, k_hi, v_lo, v_hi = _b_operands(kp, kc, vp, vc, kvh, True)
            for sb in range(per_step):
                no_prev = first and sb == 0
                first_col = B_PREV if no_prev else 0
                keys = slice(sb * qblk + first_col, (sb + 1) * qblk + B_PREV)
                qrows = slice(sb * qblk, (sb + 1) * qblk)
                halves = []
                for hh, sel in enumerate((lane_lo, jnp.logical_not(lane_lo))):
                    kk = (k_lo if hh == 0 else k_hi)[keys]
                    vv = (v_lo if hh == 0 else v_hi)[keys]
                    qm4 = jnp.concatenate(
                        [jnp.where(sel, q_ref[qrows, 128 * pp:128 * (pp + 1)], jnp.zeros((qblk, 128), BF16))
                         for pp in range(n_pairs)], axis=0) * SCALE
                    s4 = _dot_nt(qm4, kk)
                    es, mxs = [], []
                    for pp in range(n_pairs):
                        g = 2 * pp + hh
                        s = s4[pp * qblk:(pp + 1) * qblk] + b_ref[g, :, first_col:]
                        mxs.append(jnp.maximum(jnp.max(s, axis=-1, keepdims=True), sink_ref[kvh * B_GROUP + g]))
                        es.append(jnp.exp(s - mxs[pp]).astype(BF16))
                    r4 = _dot(jnp.concatenate(es, axis=0), jnp.where(sel, vv, jnp.ones_like(vv)))
                    halves.append((r4, mxs))
                for pp in range(n_pairs):
                    cols = slice(128 * pp, 128 * (pp + 1))
                    rows = slice(pp * qblk, (pp + 1) * qblk)
                    mxs = [halves[hh][1][pp] for hh in range(2)]
                    sink_terms = [jnp.exp(sink_ref[kvh * B_GROUP + 2 * pp + hh] - mxs[hh]) for hh in range(2)]
                    o, lse = _normalise_pair([halves[hh][0][rows] for hh in range(2)], mxs, lane_lo, sink_terms)
                    silu, _ = _silu_parts(g_ref[qrows, cols].astype(F32))
                    o_ref[qrows, cols] = o.astype(BF16)
                    z_ref[qrows, cols] = (o * silu).astype(BF16)
                    lse_ref[qrows, cols] = lse

        pl.when(j == 0)(functools.partial(attend, True))
        pl.when(j >= 1)(functools.partial(attend, False))

    out_spec = pl.BlockSpec((step, 512), lambda h, j: (j, h))
    return pl.pallas_call(
        body, name="attn_b_fwd", grid=(B_KV_HEADS, t // step),
        in_specs=[q_spec, g_spec, kp_spec, kc_spec, vp_spec, vc_spec, bias_spec, sink_spec],
        out_specs=[out_spec, out_spec, out_spec],
        out_shape=[SDS((t, D_MODEL), BF16), SDS((t, D_MODEL), BF16), SDS((t, D_MODEL), F32)],
        scratch_shapes=[pltpu.VMEM((B_GROUP, qblk, qblk + B_PREV), F32)],
        compiler_params=_cparams(),
    )(qg, qg, kv, kv, kv, kv, bias, sinks)


def _attn_b_bwd(qg, kv, bias, sinks, out_b, lse, dz, bucket_onehot):
    t = qg.shape[0]
    qblk = B_QBLK_BWD
    keys = qblk + B_PREV
    nq = t // qblk
    per = qblk // B_PREV

    def body(q_ref, g_ref, kp, kc, vp, vc, w_ref, sink_ref, o_ref, lse_ref, dz_ref, oh_ref,
             dqg_ref, dkv_ref, dt5_ref, dsink_ref, gt_acc, b_ref):
        j = pl.program_id(0)
        _fill_bias(N_HEADS, lambda h: w_ref[h:h + 1, :], B_BAND, b_ref, j)

        @pl.when(j == 0)
        def _():
            dkv_ref[...] = jnp.zeros_like(dkv_ref)
            gt_acc[...] = jnp.zeros_like(gt_acc)
            dsink_ref[...] = jnp.zeros_like(dsink_ref)

        lane_lo = _lane_lo()

        def attend(with_prev):
            first_col = 0 if with_prev else B_PREV
            dk_add = jnp.zeros((keys - first_col, 128), F32)
            dv_add = jnp.zeros((keys - first_col, 128), F32)
            for kvh in range(B_KV_HEADS):
                k_lo, k_hi, v_lo, v_hi = _b_operands(kp, kc, vp, vc, kvh, with_prev)
                dk_blk = jnp.zeros((keys - first_col, 128), F32)
                dv_blk = jnp.zeros((keys - first_col, 128), F32)
                for pp in range(B_GROUP // 2):
                    cols = slice(512 * kvh + 128 * pp, 512 * kvh + 128 * (pp + 1))
                    qp = q_ref[:, cols]
                    o = o_ref[:, cols].astype(F32)
                    lse_pair = lse_ref[:, cols]
                    dzf = dz_ref[:, cols].astype(F32)
                    silu, dsilu = _silu_parts(g_ref[:, cols].astype(F32))
                    do = dzf * silu
                    dqg_ref[1, :, cols] = (dzf * o * dsilu).astype(BF16)
                    doo = do * o
                    dqs = []
                    for hh in range(2):
                        g = kvh * B_GROUP + 2 * pp + hh
                        sel = lane_lo if hh == 0 else jnp.logical_not(lane_lo)
                        kk = k_lo if hh == 0 else k_hi
                        vv = v_lo if hh == 0 else v_hi
                        qm = jnp.where(sel, qp, jnp.zeros_like(qp)) * SCALE
                        s = _dot_nt(qm, kk) + b_ref[g, :, first_col:]
                        lse_h = _own_everywhere(lse_pair, sel)
                        p = jnp.exp(_minus_rows(s, lse_h))
                        delta = jnp.sum(jnp.where(sel, doo, 0.0), axis=-1, keepdims=True)
                        dom = jnp.where(sel, do, 0.0).astype(BF16)
                        dp = _dot_nt(dom, vv)
                        ds = p * (dp - delta)
                        gt_acc[g, :, first_col:] += ds
                        dsink_ref[g:g + 1, :] -= jnp.sum(jnp.exp(sink_ref[g] - lse_h) * delta, axis=0, keepdims=True)
                        dsb = ds.astype(BF16)
                        dqs.append(_dot(dsb, kk) * SCALE)
                        dk_blk = dk_blk + _dot_tn(dsb, qm)
                        dv_blk = dv_blk + _dot_tn(p.astype(BF16), dom)
                    dqg_ref[0, :, cols] = jnp.where(lane_lo, dqs[0], dqs[1]).astype(BF16)
                mine = lane_lo if kvh == 0 else jnp.logical_not(lane_lo)
                dk_add = dk_add + jnp.where(mine, dk_blk + pltpu.roll(dk_blk, HEAD_DIM, 1), 0.0)
                dv_add = dv_add + jnp.where(mine, dv_blk + pltpu.roll(dv_blk, HEAD_DIM, 1), 0.0)
            first_key = B_PREV if with_prev else 0
            if with_prev:
                rows = pl.ds(pl.multiple_of(j * qblk - B_PREV, B_PREV), B_PREV)
                dkv_ref[0, rows, :] += dk_add[0:B_PREV]
                dkv_ref[1, rows, :] += dv_add[0:B_PREV]
            rows = pl.ds(pl.multiple_of(j * qblk, qblk), qblk)
            dkv_ref[0, rows, :] += dk_add[first_key:first_key + qblk]
            dkv_ref[1, rows, :] += dv_add[first_key:first_key + qblk]

        pl.when(j == 0)(functools.partial(attend, False))
        pl.when(j >= 1)(functools.partial(attend, True))

        @pl.when(j == nq - 1)
        def _():
            dd = jnp.concatenate([_offset_sums(_collapse_chunks(gt_acc[g], keys)) for g in range(N_HEADS)], axis=0)
            hi = dd.astype(BF16)
            lo = (dd - hi.astype(F32)).astype(BF16)
            dt5_ref[...] = _dot(hi, oh_ref[...]) + _dot(lo, oh_ref[...])

    wide = lambda col: pl.BlockSpec((qblk, D_MODEL), lambda j, col=col: (j, col))
    prev = lambda col: pl.BlockSpec((B_PREV, 128), lambda j, col=col: (jnp.maximum(per * j - 1, 0), col))
    cur = lambda col: pl.BlockSpec((qblk, 128), lambda j, col=col: (j, col))
    fixed = lambda shape: pl.BlockSpec(shape, lambda j: (0,) * len(shape))
    return pl.pallas_call(
        body, name="attn_b_bwd", grid=(nq,),
        in_specs=[wide(0), wide(1), prev(0), cur(0), prev(1), cur(1), fixed((N_HEADS, keys)),
                  pl.BlockSpec(memory_space=pltpu.SMEM), wide(0), wide(0), wide(0), fixed((keys, 128))],
        out_specs=[pl.BlockSpec((2, qblk, D_MODEL), lambda j: (0, j, 0)), fixed((2, t, 128)),
                   fixed((N_HEADS, 128)), fixed((N_HEADS, 128))],
        out_shape=[SDS((2, t, D_MODEL), BF16), SDS((2, t, 128), F32),
                   SDS((N_HEADS, 128), F32), SDS((N_HEADS, 128), F32)],
        scratch_shapes=[pltpu.VMEM((N_HEADS, qblk, keys), F32), pltpu.VMEM((N_HEADS, qblk, keys), F32)],
        compiler_params=_cparams(),
    )(qg, qg, kv, kv, kv, kv, bias, sinks, out_b, lse, dz, bucket_onehot)


def _a_bias_by_offset(rel_bias):
    m = np.arange(A_DIAG)
    idx = np.clip(A_BAND - 1 - m, -A_REL_CLIP, A_REL_CLIP) + A_REL_CLIP
    by_head = rel_bias[idx].T.reshape(N_HEADS // 2, 2, A_DIAG)
    return jnp.concatenate([by_head, jnp.zeros((N_HEADS // 2, 6, A_DIAG), F32)], axis=1)


def _a_bias_grad(offset_sums):
    first = 319
    tail = jnp.sum(offset_sums[:, :first], axis=1)
    body = jnp.flip(offset_sums[:, first:first + 320], axis=1)
    body = body.at[:, -1].add(tail)
    full = jnp.concatenate([jnp.zeros((N_HEADS, 193), F32), body], axis=1)
    return full


def _t5_bucket(rel):
    nb = T5_BUCKETS // 2
    max_exact = nb // 2
    ret = jnp.where(rel > 0, nb, 0)
    n = jnp.abs(rel)
    nf = jnp.maximum(n, 1).astype(jnp.float32)
    large = max_exact + (jnp.log(nf / max_exact) / math.log(T5_MAX_DIST / max_exact)
                         * (nb - max_exact)).astype(jnp.int32)
    large = jnp.minimum(large, nb - 1)
    return ret + jnp.where(n < max_exact, n, large)


def _b_offset_buckets(keys):
    return _t5_bucket(jnp.arange(keys, dtype=jnp.int32) - (B_LEFT_CHUNKS * CHUNK + CHUNK - 1))


def _b_bias_by_offset(t5_table, keys):
    return t5_table[_b_offset_buckets(keys)].T


def _b_bucket_onehot(keys):
    return (_b_offset_buckets(keys)[:, None] == jnp.arange(128)[None, :]).astype(BF16)


def _local_step(my_slot, order, x, target, a_gain_shard, w_in_a_shard, rel_bias, late_shards, kv_gain,
                t5_table, b_gain, sinks, f_gain):
    a_bias = _a_bias_by_offset(rel_bias)
    b_bias_fwd = _b_bias_by_offset(t5_table, B_QBLK_FWD + B_PREV)
    b_bias_bwd = _b_bias_by_offset(t5_table, B_QBLK_BWD + B_PREV)
    sinks_flat = sinks.reshape(N_HEADS)

    xn, qkvg, w_in_a, a_gain = _norm_matmul_gather(order, x, a_gain_shard, w_in_a_shard)
    z_a, out_a, lse_a, (w_in_b, w_out_a, w_out_b, kv_w) = _attn_a_fwd(qkvg, a_bias, late_shards)
    w_out_a = w_out_a.reshape(D_MODEL, D_MODEL)
    w_out_b = w_out_b.reshape(D_MODEL, D_MODEL)
    kv_w = kv_w.reshape(D_MODEL, 2 * 128)
    h1, kvn, hb, kv, qg = _layer_a_out(x, z_a, w_out_a, kv_gain, b_gain, kv_w, w_in_b)
    z_b, out_b, lse_b = _attn_b_fwd(qg, kv, b_bias_fwd, sinks_flat)
    dh2, dh2b, dz_b, loss, d_fn = _layer_b_out_loss(h1, z_b, w_out_b, f_gain, target)

    dqg_b, dkv_b, d_t5, d_sink = _attn_b_bwd(qg, kv, b_bias_bwd, sinks_flat, out_b, lse_b, dz_b,
                                             _b_bucket_onehot(B_QBLK_BWD + B_PREV))
    dh1, dh1b, dz_a, d_bn, d_kn = _layer_b_in_bwd(dqg_b, dkv_b, w_in_b, kv_w, h1, dh2, b_gain, kv_gain, w_out_a)
    early = dict(
        b_w_out=_weight_grad_rows("grad_b_w_out", my_slot, z_b, dh2b[None]),
        b_w_in=_weight_grad_cols("grad_b_w_in", my_slot, hb, [dqg_b],
                                 [(0, o, c, 4 * o + c) for o in range(2) for c in range(4)], 256),
        kv_w=_weight_grad_rows("grad_kv_w", my_slot, kvn, dkv_b),
        a_w_out=_weight_grad_rows("grad_a_w_out", my_slot, z_a, dh1b[None]))
    dqg_a, dkv_a, d_rel, landed = _attn_a_bwd(qkvg, a_bias, out_a, lse_a, dz_a, [g[0] for g in early.values()])
    ready = dict(
        loss=loss, a_rel_bias=d_rel[:, :2].reshape(N_HEADS, A_DIAG),
        kv_norm=d_kn, t5_bias=d_t5, b_norm=d_bn, b_sinks=d_sink, final_norm=d_fn)
    g_own, from_sibling, from_chips, ready_sums = _grad_a_w_in_reduce(
        _a_w_in_grad_order(), xn, dqg_a, dkv_a, list(ready.values()))
    grad_x, d_an = _layer_a_in_bwd(dqg_a, dkv_a, w_in_a, x, dh1, a_gain)

    matrices = {n: (g[1], [(land, 0, N_DEV - 1)]) for (n, g), land in zip(early.items(), landed)}
    matrices["a_w_in"] = (g_own, [(from_sibling, 0, 1), (from_chips, 0, 3)])
    small = dict(zip(ready.keys(), ready_sums), a_norm=d_an)
    return grad_x, small, matrices


def _place():
    x, y, c = lax.axis_index("x"), lax.axis_index("y"), lax.axis_index("c")
    chips = [(1 - x, y), (x, 1 - y), (1 - x, 1 - y)]
    return x, y, c, chips


def _slot(px, py, pc):
    return 4 * px + 2 * py + pc


ANY = pl.BlockSpec(memory_space=pl.ANY)


def _peer(x, y, c, k):
    return (x ^ (k >> 2), y ^ ((k >> 1) & 1), c ^ (k & 1))


def _scatter_copies(grad_refs, land_refs, send_sems, recv_sems):
    x, y, c, _ = _place()
    copies = []
    for t, (grad, land) in enumerate(zip(grad_refs, land_refs)):
        for k in range(1, N_DEV):
            peer = _peer(x, y, c, k)
            sem = (N_DEV - 1) * t + k - 1
            copies.append(pltpu.make_async_remote_copy(
                src_ref=grad.at[_slot(*peer)], dst_ref=land.at[k - 1],
                send_sem=send_sems.at[sem], recv_sem=recv_sems.at[sem],
                device_id=peer, device_id_type=MESH))
    return copies


def _gather_phases(ins, outs, send_sems, recv_sems, local_sems):
    n = len(ins)
    x, y, c, chips = _place()
    me, sibling = (x, y, c), (x, y, 1 - c)

    def copy(t, k, block, to, src=None):
        dst = outs[t].at[_slot(*block)]
        return pltpu.make_async_remote_copy(
            src_ref=dst if src is None else src, dst_ref=dst,
            send_sem=send_sems.at[7 * t + k], recv_sem=recv_sems.at[7 * t + k],
            device_id=to, device_id_type=MESH)

    def lists():
        mine = [pltpu.make_async_copy(ins[t], outs[t].at[_slot(*me)], local_sems.at[t]) for t in range(n)]
        first = []
        for t in range(n):
            first.append(copy(t, 0, me, sibling, src=ins[t]))
            first += [copy(t, 1 + j, me, (*chip, c), src=ins[t]) for j, chip in enumerate(chips)]
        passed = [copy(t, 4 + j, (*chip, c), sibling) for t in range(n) for j, chip in enumerate(chips)]
        return mine, first, passed

    def start():
        mine, first, _ = lists()
        for cp in mine + first:
            cp.start()

    def forward():
        _, _, passed = lists()
        for t in range(n):
            for j, chip in enumerate(chips):
                copy(t, 1 + j, (*chip, c), me).wait_recv()
                passed[3 * t + j].start()

    def finish():
        mine, first, passed = lists()
        for t in range(n):
            copy(t, 0, sibling, me).wait_recv()
            for j, chip in enumerate(chips):
                copy(t, 4 + j, (*chip, 1 - c), me).wait_recv()
        for cp in first + passed:
            cp.wait_send()
        for cp in mine:
            cp.wait()

    return start, forward, finish


def _gather_scratch(n):
    return [pltpu.SemaphoreType.DMA((7 * n,)), pltpu.SemaphoreType.DMA((7 * n,)), pltpu.SemaphoreType.DMA((n,))]


_FAR_CHIP_FIRST = (2, 0, 1)


def _a_w_in_grad_order():
    x, y, c, chips = _place()
    slots = []
    for j in _FAR_CHIP_FIRST:
        slots += [_slot(*chips[j], 1 - c), _slot(*chips[j], c)]
    slots += [_slot(x, y, 1 - c), _slot(x, y, c)]
    return jnp.stack(slots).astype(jnp.int32)


def _grad_a_w_in_reduce(order, a, dqg, dkv, small):
    t, dw = a.shape
    tn = dqg.shape[2] // 2
    n_s = len(small)
    n_far = len(_FAR_CHIP_FIRST)

    def body(order_ref, a_ref, dqg_ref, dkv_ref, *rest):
        small_refs, rest = rest[:n_s], rest[n_s:]
        own_ref, sib_ref, chips_ref = rest[:3]
        small_out, rest = rest[3:3 + n_s], rest[3 + n_s:]
        a_buf, at_ref, res_ref, stage, land, load_sem, d2d_send, d2d_recv, ici_send, ici_recv = rest[:10]
        small_lands, (small_send, small_recv) = rest[10:10 + n_s], rest[10 + n_s:]
        s = pl.program_id(0)
        x, y, c, chips = _place()

        def to_sibling(i):
            return pltpu.make_async_remote_copy(
                src_ref=stage.at[i], dst_ref=land.at[i] if i < n_far else sib_ref.at[0],
                send_sem=d2d_send.at[i], recv_sem=d2d_recv.at[i], device_id=(x, y, 1 - c), device_id_type=MESH)

        def to_chip(i):
            j = _FAR_CHIP_FIRST[i]
            return pltpu.make_async_remote_copy(
                src_ref=land.at[i], dst_ref=chips_ref.at[j], send_sem=ici_send.at[i], recv_sem=ici_recv.at[i],
                device_id=(*chips[j], c), device_id_type=MESH)

        @pl.when(s == 0)
        def _():
            load = pltpu.make_async_copy(a_ref, a_buf, load_sem)
            load.start()
            load.wait()
            at_ref[...] = a_buf[...].T

        blk = order_ref[s]
        from_qg = jnp.logical_or(blk < 2, blk >= 6)

        @pl.when(from_qg)
        def _():
            res_ref[...] = _dot(at_ref[...], dqg_ref[0])

        @pl.when(jnp.logical_not(from_qg))
        def _():
            res_ref[...] = _dot(at_ref[...], dkv_ref[0])

        for i in range(n_far + 1):
            @pl.when(s == 2 * i)
            def _(i=i):
                stage[i] = res_ref[...].astype(BF16)
                to_sibling(i).start()

        for i in range(n_far):
            @pl.when(s == 2 * i + 1)
            def _(i=i):
                to_sibling(i).wait_recv()
                land[i] = (res_ref[...] + land[i].astype(F32)).astype(BF16)
                to_chip(i).start()

        @pl.when(s == N_DEV - 1)
        def _():
            own_ref[...] = res_ref[...]
            _all_reduce_small(small_refs, small_out, small_lands, small_send, small_recv)
            for i in range(n_far + 1):
                to_sibling(i).wait_send()
            to_sibling(n_far).wait_recv()
            for i in range(n_far):
                to_chip(i).wait()

    whole = pl.BlockSpec(memory_space=pltpu.VMEM)
    outs = pl.pallas_call(
        body, name="grad_a_w_in",
        grid_spec=pltpu.PrefetchScalarGridSpec(
            num_scalar_prefetch=1, grid=(N_DEV,),
            in_specs=[ANY,
                      pl.BlockSpec((1, t, tn), lambda s, o: (o[s] // 6, 0, o[s] % 2)),
                      pl.BlockSpec((1, t, tn), lambda s, o: ((o[s] // 4) % 2, 0, o[s] % 2))] + [whole] * n_s,
            out_specs=[pl.BlockSpec((dw, tn), lambda s, o: (0, 0)), ANY, ANY] + [whole] * n_s,
            scratch_shapes=[pltpu.VMEM((t, dw), BF16), pltpu.VMEM((dw, t), BF16), pltpu.VMEM((dw, tn), F32),
                            pltpu.VMEM((n_far + 1, dw, tn), BF16), pltpu.VMEM((n_far, dw, tn), BF16),
                            pltpu.SemaphoreType.DMA,
                            pltpu.SemaphoreType.DMA((n_far + 1,)), pltpu.SemaphoreType.DMA((n_far + 1,)),
                            pltpu.SemaphoreType.DMA((n_far,)), pltpu.SemaphoreType.DMA((n_far,))]
            + _all_reduce_scratch([s.shape for s in small])),
        out_shape=[SDS((dw, tn), F32), SDS((1, dw, tn), BF16), SDS((n_far, dw, tn), BF16)]
        + [SDS(s.shape, F32) for s in small],
        compiler_params=_cparams(),
    )(order, a, dqg, dkv, *small)
    return outs[0], outs[1], outs[2], list(outs[3:])


def _row_tile(rows):
    return min(rows, 512)


def _adamw(w, g, m, v):
    m2 = ADAM_B1 * m + (1.0 - ADAM_B1) * g
    v2 = ADAM_B2 * v + (1.0 - ADAM_B2) * jnp.square(g)
    m_hat = m2 / (1.0 - ADAM_B1 ** ADAM_STEP)
    v_hat = v2 / (1.0 - ADAM_B2 ** ADAM_STEP)
    delta = -ADAM_LR * (m_hat / (jnp.sqrt(v_hat) + ADAM_EPS) + ADAM_WD * w)
    return delta, m2, v2


def _reduce_adamw(name, own, partials, w, m, v):
    r, c = own.shape
    tr = _row_tile(r)
    n_p = len(partials)

    def body(own_ref, *rest):
        p_refs, (w_ref, m_ref, v_ref, grad_ref, d_ref, nm_ref, nv_ref) = rest[:n_p], rest[n_p:]
        grad = own_ref[...]
        for p_ref, (_, _, count) in zip(p_refs, partials):
            for j in range(count):
                grad = grad + p_ref[j].astype(F32)
        grad_ref[...] = grad
        d_ref[...], nm_ref[...], nv_ref[...] = _adamw(w_ref[...], grad, m_ref[...], v_ref[...])

    flat = pl.BlockSpec((tr, c), lambda i: (i, 0))
    return pl.pallas_call(
        body, name=name, grid=(r // tr,),
        in_specs=[flat] + [pl.BlockSpec((count, tr, c), lambda i, first=first, count=count: (first // count, i, 0))
                           for _, first, count in partials] + [flat, flat, flat],
        out_specs=[flat, flat, flat, flat],
        out_shape=[SDS((r, c), F32)] * 4,
        compiler_params=_cparams(),
    )(own, *[p[0] for p in partials], w, m, v)


VM = pl.BlockSpec()


def _all_reduce_small(ins, outs, lands, send_sems, recv_sems):
    x, y, c, _ = _place()
    my_slot = _slot(x, y, c)
    copies = []
    for t, (src, land) in enumerate(zip(ins, lands)):
        land[my_slot] = src[...]
        for k in range(1, N_DEV):
            sem = (N_DEV - 1) * t + k - 1
            copies.append(pltpu.make_async_remote_copy(
                src_ref=src, dst_ref=land.at[my_slot],
                send_sem=send_sems.at[sem], recv_sem=recv_sems.at[sem],
                device_id=_peer(x, y, c, k), device_id_type=MESH))
    for cp in copies:
        cp.start()
    for t, (src, land) in enumerate(zip(ins, lands)):
        for k in range(1, N_DEV):
            sem = (N_DEV - 1) * t + k - 1
            pltpu.make_async_remote_copy(
                src_ref=src, dst_ref=land.at[_slot(*_peer(x, y, c, k))],
                send_sem=send_sems.at[sem], recv_sem=recv_sems.at[sem],
                device_id=(x, y, c), device_id_type=MESH).wait_recv()
    for cp in copies:
        cp.wait_send()
    for out, land in zip(outs, lands):
        total = land[0]
        for s in range(1, N_DEV):
            total = total + land[s]
        out[...] = total


def _all_reduce_scratch(shapes):
    n_sems = (N_DEV - 1) * len(shapes)
    return ([pltpu.VMEM((N_DEV, *s), F32) for s in shapes]
            + [pltpu.SemaphoreType.DMA((n_sems,)), pltpu.SemaphoreType.DMA((n_sems,))])


def _small_adamw(my_slot, sums, ws, ms, vs):
    n = len(ws)

    def body(slot_ref, *refs):
        refs, (an_sum, an_land, an_send, an_recv) = refs[:-4], refs[-4:]
        sum_refs, refs = refs[:n + 1], refs[n + 1:]
        w_refs, m_refs, v_refs, refs = refs[:n], refs[n:2 * n], refs[2 * n:3 * n], refs[3 * n:]
        g_refs, d_refs, nm_refs, nv_refs = refs[:n + 1], refs[n + 1:2 * n + 1], refs[2 * n + 1:3 * n + 1], refs[3 * n + 1:]
        _all_reduce_small([sum_refs[0]], [an_sum], [an_land], an_send, an_recv)
        for t in range(n + 1):
            if t == 0:
                g = an_sum[:, pl.ds(pl.multiple_of(slot_ref[0] * 128, 128), 128)]
            else:
                g = sum_refs[t][...]
            g_refs[t][...] = g
            if t < n:
                d_refs[t][...], nm_refs[t][...], nv_refs[t][...] = _adamw(w_refs[t][...], g, m_refs[t][...], v_refs[t][...])

    shapes = [SDS(w.shape, F32) for w in ws]
    outs = pl.pallas_call(
        body, name="small_adamw",
        in_specs=[pl.BlockSpec(memory_space=pltpu.SMEM)] + [VM] * (4 * n + 1),
        out_specs=[VM] * (4 * n + 1),
        out_shape=shapes + [SDS(sums[-1].shape, F32)] + shapes * 3,
        scratch_shapes=[pltpu.VMEM(sums[0].shape, F32)] + _all_reduce_scratch([sums[0].shape]),
    )(my_slot, *sums, *ws, *ms, *vs)
    return outs[:n + 1], outs[n + 1:2 * n + 1], outs[2 * n + 1:3 * n + 1], outs[3 * n + 1:]


def kernel(x, a_norm, a_w_in, a_rel_bias, a_w_out, kv_norm, kv_w, t5_bias, b_norm, b_w_in, b_sinks, b_w_out, final_norm, loss_target, m_a_norm, m_a_w_in, m_a_rel_bias, m_a_w_out, m_kv_norm, m_kv_w, m_t5_bias, m_b_norm, m_b_w_in, m_b_sinks, m_b_w_out, m_final_norm, v_a_norm, v_a_w_in, v_a_rel_bias, v_a_w_out, v_kv_norm, v_kv_w, v_t5_bias, v_b_norm, v_b_w_in, v_b_sinks, v_b_w_out, v_final_norm):
    xi, yi, ci = lax.axis_index("x"), lax.axis_index("y"), lax.axis_index("c")
    my_slot = _slot(xi, yi, ci)

    slot_arr = jnp.reshape(my_slot, (1,)).astype(jnp.int32)
    order = _gather_order(xi, yi, ci)
    late_shards = [b_w_in[0].astype(BF16), a_w_out[0].astype(BF16), b_w_out[0].astype(BF16), kv_w.astype(BF16)]
    grad_x, loc, matrices = _local_step(
        slot_arr, order, x[0], loss_target[0], a_norm, a_w_in[0].astype(BF16), a_rel_bias[0], late_shards,
        kv_norm.reshape(1, D_MODEL), t5_bias, b_norm, b_sinks, final_norm.reshape(1, D_MODEL))

    shard_w = dict(a_w_in=a_w_in[0], b_w_in=b_w_in[0], a_w_out=a_w_out[0], b_w_out=b_w_out[0], kv_w=kv_w)
    shard_m = dict(a_w_in=m_a_w_in[0], b_w_in=m_b_w_in[0], a_w_out=m_a_w_out[0], b_w_out=m_b_w_out[0], kv_w=m_kv_w)
    shard_v = dict(a_w_in=v_a_w_in[0], b_w_in=v_b_w_in[0], a_w_out=v_a_w_out[0], b_w_out=v_b_w_out[0], kv_w=v_kv_w)
    big = {n: _reduce_adamw("adamw_" + n, own, partials, shard_w[n], shard_m[n], shard_v[n])
           for n, (own, partials) in matrices.items()}

    names = ("a_norm", "a_rel_bias", "kv_norm", "t5_bias", "b_norm", "b_sinks", "final_norm")
    tables = ("a_rel_bias", "t5_bias")

    def row(n, a):
        return a.reshape(-1, a.shape[-1]).T if n in tables else a.reshape(1, -1)

    small_w = [row(n, a) for n, a in zip(names, (a_norm, a_rel_bias, kv_norm, t5_bias, b_norm, b_sinks, final_norm))]
    small_m = [row(n, a) for n, a in zip(names, (m_a_norm, m_a_rel_bias, m_kv_norm, m_t5_bias, m_b_norm, m_b_sinks,
                                                 m_final_norm))]
    small_v = [row(n, a) for n, a in zip(names, (v_a_norm, v_a_rel_bias, v_kv_norm, v_t5_bias, v_b_norm, v_b_sinks,
                                                 v_final_norm))]
    sums = dict(loc)
    sums["a_rel_bias"] = _a_bias_grad(sums["a_rel_bias"])
    sums["t5_bias"] = sums["t5_bias"][:, :T5_BUCKETS]
    sums["b_sinks"] = sums["b_sinks"][:, 0].reshape(1, N_HEADS)
    results = _small_adamw(slot_arr, [sums[n] for n in names + ("loss",)], small_w, small_m, small_v)
    like = dict(a_norm=a_norm, a_rel_bias=a_rel_bias, kv_norm=kv_norm, t5_bias=t5_bias, b_norm=b_norm,
                b_sinks=b_sinks, final_norm=final_norm)
    sm = [{n: (part[i].T if n in tables else part[i]).reshape(like[n].shape) for i, n in enumerate(names)}
          for part in results]
    loss = results[0][len(names)][0, 0]

    order = ("a_norm", "a_w_in", "a_rel_bias", "a_w_out", "kv_norm", "kv_w", "t5_bias", "b_norm",
             "b_w_in", "b_sinks", "b_w_out", "final_norm")
    lead = dict(a_w_in=True, b_w_in=True, a_w_out=True, b_w_out=True, kv_w=False)

    def pick(kind, name):
        if name in big:
            val = big[name][kind]
            return val[None] if lead[name] else val
        return sm[kind][name]

    outs = [loss, grad_x[None]]
    for kind in range(4):
        outs += [pick(kind, n) for n in order]
    return tuple(outs)
```

```python
import functools
import math

import numpy as np
import jax
import jax.numpy as jnp
from jax import lax
from jax.experimental import pallas as pl
from jax.experimental.pallas import tpu as pltpu

F32 = jnp.float32
BF16 = jnp.bfloat16
SDS = jax.ShapeDtypeStruct

D_MODEL = 1024
HEAD_DIM = 64
CHUNK = 64
N_HEADS = 16
RMS_EPS = 1e-6
A_LEFT_CHUNKS = 8
A_BAND = (A_LEFT_CHUNKS + 1) * CHUNK
A_REL_CLIP = 256
B_KV_HEADS = 2
B_GROUP = 8
B_LEFT_CHUNKS = 2
B_BAND = (B_LEFT_CHUNKS + 1) * CHUNK
T5_BUCKETS = 32
T5_MAX_DIST = 128
QBLK = 256
A_KEYS = 3 * QBLK
B_QBLK_FWD = 128
B_QBLK_BWD = 256
B_PREV = 128
A_DIAG = A_KEYS
NEG = -1e30
SCALE = HEAD_DIM ** -0.5
N_DEV = 8

ADAM_LR = 0.001
ADAM_B1 = 0.9
ADAM_B2 = 0.999
ADAM_EPS = 1e-08
ADAM_WD = 0.01
ADAM_STEP = 10

VMEM_LIMIT_BYTES = 56 * 1024 * 1024
MESH = pl.DeviceIdType.MESH


def _cparams():
    return pltpu.CompilerParams(vmem_limit_bytes=VMEM_LIMIT_BYTES)


def _dot(a, b):
    return jnp.dot(a, b, preferred_element_type=F32)


def _dot_nt(a, b):
    return lax.dot_general(a, b, (((1,), (1,)), ((), ())), preferred_element_type=F32)


def _dot_tn(a, b):
    return lax.dot_general(a, b, (((0,), (0,)), ((), ())), preferred_element_type=F32)


def _rstd(xf):
    return lax.rsqrt(jnp.mean(xf * xf, axis=-1, keepdims=True) + RMS_EPS)


def _sigmoid(x):
    return 1.0 / (1.0 + jnp.exp(-x))


_GATHER_SEQUENCE = ((0, None), (1, 0), (2, 1), (4, None), (5, None), (3, 2), (6, None))


def _gather_order(x, y, c):
    others = [(1 - x, y), (x, 1 - y), (1 - x, 1 - y)]
    arrivals = [_slot(x, y, 1 - c)] + [_slot(*chip, c) for chip in others] + [_slot(*chip, 1 - c) for chip in others]
    return jnp.stack([_slot(x, y, c)] + [arrivals[a] for a, _ in _GATHER_SEQUENCE]).astype(jnp.int32)


def _norm_matmul_gather(order, x, gain_shard, w_shard):
    t = x.shape[0]
    dw, tn = w_shard.shape
    tm = min(t, 2048)
    n_m = t // tm

    def body(order_ref, x_ref, gs_ref, shard_ref, xn_ref, o_ref, full_ref, gain_ref,
             xn_all, wbuf, gland, send_sems, recv_sems, gsend_sems, grecv_sems, load_sems, own_sem):
        n, m = pl.program_id(0), pl.program_id(1)
        x_i, y_i, c_i, chips = _place()
        me, sibling = (x_i, y_i, c_i), (x_i, y_i, 1 - c_i)

        def send(k, block, to, src=None):
            dst = full_ref.at[_slot(*block)]
            return pltpu.make_async_remote_copy(
                src_ref=dst if src is None else src, dst_ref=dst,
                send_sem=send_sems.at[k], recv_sem=recv_sems.at[k], device_id=to, device_id_type=MESH)

        own = pltpu.make_async_copy(shard_ref, full_ref.at[_slot(*me)], own_sem)
        first = [send(0, me, sibling, src=shard_ref)]
        first += [send(1 + j, me, (*chip, c_i), src=shard_ref) for j, chip in enumerate(chips)]
        forwards = [send(4 + j, (*chip, c_i), sibling) for j, chip in enumerate(chips)]
        arrivals = [send(0, sibling, me)] + [send(1 + j, (*chip, c_i), me) for j, chip in enumerate(chips)]
        arrivals += [send(4 + j, (*chip, 1 - c_i), me) for j, chip in enumerate(chips)]
        gains = [pltpu.make_async_remote_copy(
            src_ref=gs_ref, dst_ref=gland.at[_slot(*me)], send_sem=gsend_sems.at[k - 1],
            recv_sem=grecv_sems.at[k - 1], device_id=_peer(x_i, y_i, c_i, k), device_id_type=MESH)
            for k in range(1, N_DEV)]

        @pl.when(jnp.logical_and(n == 0, m == 0))
        def _():
            own.start()
            for cp in gains + first:
                cp.start()
            pltpu.make_async_copy(shard_ref, wbuf.at[0], load_sems.at[0]).start()
            gland[_slot(*me)] = gs_ref[...]
            for k in range(1, N_DEV):
                pltpu.make_async_remote_copy(
                    src_ref=gs_ref, dst_ref=gland.at[_slot(*_peer(x_i, y_i, c_i, k))],
                    send_sem=gsend_sems.at[k - 1], recv_sem=grecv_sems.at[k - 1],
                    device_id=me, device_id_type=MESH).wait_recv()
            for s in range(N_DEV):
                gain_ref[:, 128 * s:128 * (s + 1)] = gland[s]

        rows = pl.ds(pl.multiple_of(m * tm, tm), tm)

        @pl.when(n == 0)
        def _():
            xf = x_ref[...]
            xn = ((xf * _rstd(xf)) * gain_ref[...]).astype(BF16)
            xn_all[rows, :] = xn
            xn_ref[...] = xn

        @pl.when(m == 0)
        def _():
            pltpu.make_async_copy(full_ref.at[0], wbuf.at[n % 2], load_sems.at[n % 2]).wait()

        o_ref[...] = _dot(xn_all[rows, :], wbuf[n % 2]).astype(BF16)

        for k, (arrival, forward) in enumerate(_GATHER_SEQUENCE):
            @pl.when(jnp.logical_and(n == k, m == n_m - 1))
            def _(k=k, arrival=arrival, forward=forward):
                arrivals[arrival].wait_recv()
                if forward is not None:
                    forwards[forward].start()
                pltpu.make_async_copy(full_ref.at[order_ref[k + 1]], wbuf.at[(k + 1) % 2],
                                      load_sems.at[(k + 1) % 2]).start()

        @pl.when(jnp.logical_and(n == N_DEV - 1, m == n_m - 1))
        def _():
            for cp in gains + first + forwards:
                cp.wait_send()
            own.wait()

    held = lambda n, m, order: (jnp.where(n == 0, m, n_m - 1), 0)
    return pl.pallas_call(
        body, name="norm_matmul_gather",
        grid_spec=pltpu.PrefetchScalarGridSpec(
            num_scalar_prefetch=1, grid=(N_DEV, n_m),
            in_specs=[pl.BlockSpec((tm, D_MODEL), held),
                      pl.BlockSpec((1, 128), lambda n, m, order: (0, 0)), ANY],
            out_specs=[pl.BlockSpec((tm, D_MODEL), held),
                       pl.BlockSpec((tm, tn), lambda n, m, order: (m, order[n])),
                       ANY, pl.BlockSpec((1, D_MODEL), lambda n, m, order: (0, 0))],
            scratch_shapes=[pltpu.VMEM((t, D_MODEL), BF16), pltpu.VMEM((2, dw, tn), BF16),
                            pltpu.VMEM((N_DEV, 1, 128), F32),
                            pltpu.SemaphoreType.DMA((7,)), pltpu.SemaphoreType.DMA((7,)),
                            pltpu.SemaphoreType.DMA((7,)), pltpu.SemaphoreType.DMA((7,)),
                            pltpu.SemaphoreType.DMA((2,)), pltpu.SemaphoreType.DMA]),
        out_shape=[SDS((t, D_MODEL), BF16), SDS((t, N_DEV * tn), BF16), SDS((N_DEV, dw, tn), BF16),
                   SDS((1, D_MODEL), F32)],
        compiler_params=_cparams(),
    )(order, x, gain_shard, w_shard)


def _layer_a_out(x, z, w_out, kv_gain, b_gain, kv_w, w_in_b):
    t = x.shape[0]
    tm = min(t, 1024)
    nb, _, tn = w_in_b.shape

    def body(x_ref, z_ref, wo_ref, kvg_ref, bg_ref, kvw_ref, wb_ref,
             h1_ref, kvn_ref, hb_ref, kv_ref, qg_ref):
        h1 = x_ref[...] + _dot(z_ref[...], wo_ref[...])
        h1_ref[...] = h1
        y0 = h1 * _rstd(h1)
        kvn = (y0 * kvg_ref[...]).astype(BF16)
        hb = (y0 * bg_ref[...]).astype(BF16)
        kvn_ref[...] = kvn
        hb_ref[...] = hb
        kv_ref[...] = _dot(kvn, kvw_ref[...]).astype(BF16)
        for i in range(nb):
            qg_ref[:, i * tn:(i + 1) * tn] = _dot(hb, wb_ref[i]).astype(BF16)

    row = lambda m: (m, 0)
    fix2 = lambda m: (0, 0)
    return pl.pallas_call(
        body, name="layer_a_out", grid=(t // tm,),
        in_specs=[pl.BlockSpec((tm, D_MODEL), row), pl.BlockSpec((tm, D_MODEL), row),
                  pl.BlockSpec((D_MODEL, D_MODEL), fix2),
                  pl.BlockSpec((1, D_MODEL), fix2), pl.BlockSpec((1, D_MODEL), fix2),
                  pl.BlockSpec((D_MODEL, 256), fix2),
                  pl.BlockSpec((nb, D_MODEL, tn), lambda m: (0, 0, 0))],
        out_specs=[pl.BlockSpec((tm, D_MODEL), row), pl.BlockSpec((tm, D_MODEL), row),
                   pl.BlockSpec((tm, D_MODEL), row), pl.BlockSpec((tm, 256), row),
                   pl.BlockSpec((tm, nb * tn), row)],
        out_shape=[SDS((t, D_MODEL), F32), SDS((t, D_MODEL), BF16), SDS((t, D_MODEL), BF16),
                   SDS((t, 256), BF16), SDS((t, nb * tn), BF16)],
        compiler_params=_cparams(),
    )(x, z, w_out, kv_gain, b_gain, kv_w, w_in_b)


def _layer_b_out_loss(h1, z, w_out, f_gain, target):
    t = h1.shape[0]
    tm = min(t, 1024)

    def body(h1_ref, z_ref, wo_ref, fg_ref, tgt_ref,
             dh2_ref, dh2b_ref, dz_ref, loss_ref, dfn_ref):
        @pl.when(pl.program_id(0) == 0)
        def _():
            loss_ref[...] = jnp.zeros_like(loss_ref)
            dfn_ref[...] = jnp.zeros_like(dfn_ref)

        h2 = h1_ref[...] + _dot(z_ref[...], wo_ref[...])
        r = _rstd(h2)
        yn = h2 * r
        fg = fg_ref[...]
        err = yn * fg - tgt_ref[...]
        loss_ref[...] += (0.5 / D_MODEL) * jnp.sum(err * err)
        dy = err * (1.0 / D_MODEL)
        dfn_ref[...] += jnp.sum(dy * yn, axis=0, keepdims=True)
        u = dy * fg
        dh2 = r * u - h2 * ((r * r * r) * jnp.mean(u * h2, axis=-1, keepdims=True))
        dh2_ref[...] = dh2
        dh2b = dh2.astype(BF16)
        dh2b_ref[...] = dh2b
        dz_ref[...] = _dot_nt(dh2b, wo_ref[...]).astype(BF16)

    row = lambda m: (m, 0)
    fix2 = lambda m: (0, 0)
    return pl.pallas_call(
        body, name="layer_b_out_loss", grid=(t // tm,),
        in_specs=[pl.BlockSpec((tm, D_MODEL), row), pl.BlockSpec((tm, D_MODEL), row),
                  pl.BlockSpec((D_MODEL, D_MODEL), fix2), pl.BlockSpec((1, D_MODEL), fix2),
                  pl.BlockSpec((tm, D_MODEL), row)],
        out_specs=[pl.BlockSpec((tm, D_MODEL), row), pl.BlockSpec((tm, D_MODEL), row),
                   pl.BlockSpec((tm, D_MODEL), row), pl.BlockSpec((1, 128), fix2),
                   pl.BlockSpec((1, D_MODEL), fix2)],
        out_shape=[SDS((t, D_MODEL), F32), SDS((t, D_MODEL), BF16), SDS((t, D_MODEL), BF16),
                   SDS((1, 128), F32), SDS((1, D_MODEL), F32)],
        compiler_params=_cparams(),
    )(h1, z, w_out, f_gain, target)


def _layer_b_in_bwd(dqg, dkv, w_in_b, kv_w, h1, dh2, b_gain, kv_gain, w_out_a):
    t = h1.shape[0]
    tm = min(t, 512)
    nb, _, tn = w_in_b.shape
    per = D_MODEL // tn

    def body(dqg_ref, dkv_ref, wb_ref, kvw_ref, h1_ref, dh2_ref, bg_ref, kvg_ref, wo_ref,
             dh1_ref, dh1b_ref, dz_ref, dbn_ref, dkn_ref):
        @pl.when(pl.program_id(0) == 0)
        def _():
            dbn_ref[...] = jnp.zeros_like(dbn_ref)
            dkn_ref[...] = jnp.zeros_like(dkn_ref)

        dhb = jnp.zeros((tm, D_MODEL), F32)
        for i in range(nb):
            blk = dqg_ref[i // per, :, (i % per) * tn:(i % per + 1) * tn]
            dhb = dhb + _dot_nt(blk, wb_ref[i])
        dkn = (_dot_nt(dkv_ref[0].astype(BF16), kvw_ref[:, 0:128])
               + _dot_nt(dkv_ref[1].astype(BF16), kvw_ref[:, 128:256]))
        h1 = h1_ref[...]
        r = _rstd(h1)
        xr = h1 * r
        dbn_ref[...] += jnp.sum(dhb * xr, axis=0, keepdims=True)
        dkn_ref[...] += jnp.sum(dkn * xr, axis=0, keepdims=True)
        u = dhb * bg_ref[...] + dkn * kvg_ref[...]
        dh1 = dh2_ref[...] + r * u - h1 * ((r * r * r) * jnp.mean(u * h1, axis=-1, keepdims=True))
        dh1_ref[...] = dh1
        dh1b = dh1.astype(BF16)
        dh1b_ref[...] = dh1b
        dz_ref[...] = _dot_nt(dh1b, wo_ref[...]).astype(BF16)

    row = lambda m: (m, 0)
    fix2 = lambda m: (0, 0)
    return pl.pallas_call(
        body, name="layer_b_in_bwd", grid=(t // tm,),
        in_specs=[pl.BlockSpec((2, tm, D_MODEL), lambda m: (0, m, 0)),
                  pl.BlockSpec((2, tm, 128), lambda m: (0, m, 0)),
                  pl.BlockSpec((nb, D_MODEL, tn), lambda m: (0, 0, 0)),
                  pl.BlockSpec((D_MODEL, 256), fix2),
                  pl.BlockSpec((tm, D_MODEL), row), pl.BlockSpec((tm, D_MODEL), row),
                  pl.BlockSpec((1, D_MODEL), fix2), pl.BlockSpec((1, D_MODEL), fix2),
                  pl.BlockSpec((D_MODEL, D_MODEL), fix2)],
        out_specs=[pl.BlockSpec((tm, D_MODEL), row), pl.BlockSpec((tm, D_MODEL), row),
                   pl.BlockSpec((tm, D_MODEL), row), pl.BlockSpec((1, D_MODEL), fix2),
                   pl.BlockSpec((1, D_MODEL), fix2)],
        out_shape=[SDS((t, D_MODEL), F32), SDS((t, D_MODEL), BF16), SDS((t, D_MODEL), BF16),
                   SDS((1, D_MODEL), F32), SDS((1, D_MODEL), F32)],
        compiler_params=_cparams(),
    )(dqg, dkv, w_in_b, kv_w, h1, dh2, b_gain, kv_gain, w_out_a)


def _layer_a_in_bwd(dqg, dkv, w_in_a, x, dh1, a_gain, chip_sums):
    t = x.shape[0]
    tm = min(t, 512)
    nb, _, tn = w_in_a.shape
    per = D_MODEL // tn
    n_sums = chip_sums.shape[0]

    def body(dqg_ref, dkv_ref, w_ref, x_ref, dh1_ref, ag_ref, sums_ref, dx_ref, dan_ref, land_ref,
             send_sems, recv_sems, dan_land, dan_send, dan_recv):
        @pl.when(pl.program_id(0) == 0)
        def _():
            dan_ref[...] = jnp.zeros_like(dan_ref)
            for cp in _later_chip_copies(sums_ref, land_ref, send_sems, recv_sems):
                cp.start()

        dxn = jnp.zeros((tm, D_MODEL), F32)
        for i in range(nb):
            part = i // per
            src = dqg_ref if part in (0, 3) else dkv_ref
            outer = {0: 0, 3: 1, 1: 0, 2: 1}[part]
            blk = src[outer, :, (i % per) * tn:(i % per + 1) * tn]
            dxn = dxn + _dot_nt(blk, w_ref[i])
        xf = x_ref[...]
        r = _rstd(xf)
        dan_ref[...] += jnp.sum(dxn * (xf * r), axis=0, keepdims=True)
        u = dxn * ag_ref[...]
        dx_ref[...] = dh1_ref[...] + r * u - xf * ((r * r * r) * jnp.mean(u * xf, axis=-1, keepdims=True))

        @pl.when(pl.program_id(0) == t // tm - 1)
        def _():
            _all_reduce_small([dan_ref], [dan_ref], [dan_land], dan_send, dan_recv)
            for cp in _later_chip_copies(sums_ref, land_ref, send_sems, recv_sems):
                cp.wait()

    row = lambda m: (m, 0)
    fix2 = lambda m: (0, 0)
    return pl.pallas_call(
        body, name="layer_a_in_bwd", grid=(t // tm,),
        in_specs=[pl.BlockSpec((2, tm, D_MODEL), lambda m: (0, m, 0)),
                  pl.BlockSpec((2, tm, D_MODEL), lambda m: (0, m, 0)),
                  pl.BlockSpec((nb, D_MODEL, tn), lambda m: (0, 0, 0)),
                  pl.BlockSpec((tm, D_MODEL), row), pl.BlockSpec((tm, D_MODEL), row),
                  pl.BlockSpec((1, D_MODEL), fix2), ANY],
        out_specs=[pl.BlockSpec((tm, D_MODEL), row), pl.BlockSpec((1, D_MODEL), fix2), ANY],
        out_shape=[SDS((t, D_MODEL), F32), SDS((1, D_MODEL), F32), SDS(chip_sums.shape, chip_sums.dtype)],
        scratch_shapes=[pltpu.SemaphoreType.DMA((n_sums,)), pltpu.SemaphoreType.DMA((n_sums,))]
        + _all_reduce_scratch([(1, D_MODEL)]),
        compiler_params=_cparams(),
    )(dqg, dkv, w_in_a, x, dh1, a_gain, chip_sums)


def _lut(s, vals):
    r = jnp.int32(vals[0])
    for i in range(1, len(vals)):
        r = jnp.where(s == i, jnp.int32(vals[i]), r)
    return r


def _held(steps, i):
    seq, cur = [None] * len(steps), None
    for k in range(len(steps) - 1, -1, -1):
        if steps[k][0] == i:
            cur = steps[k][1:3]
        seq[k] = cur
    for k in range(len(steps)):
        cur = seq[k] = seq[k] if seq[k] is not None else cur
    return seq


def _weight_grad_cols(name, my_slot, a, bs, steps, tn):
    t, dw = a.shape
    n_arr = len(bs)
    which = [s[0] for s in steps]
    blks = [s[3] for s in steps]

    def body(slot_ref, a_ref, *rest):
        b_refs, (o_ref, own_ref, at_ref) = rest[:n_arr], rest[n_arr:]
        s = pl.program_id(0)

        @pl.when(s == 0)
        def _():
            at_ref[...] = a_ref[...].T

        for i in range(n_arr):
            @pl.when(_lut(s, which) == i)
            def _(i=i):
                res = _dot(at_ref[...], b_refs[i][0])
                o_ref[0] = res.astype(BF16)

                @pl.when(_lut(s, blks) == slot_ref[0])
                def _():
                    own_ref[...] = res

    def b_spec(i):
        held = _held(steps, i)
        return pl.BlockSpec((1, t, tn), lambda s, slot: (_lut(s, [h[0] for h in held]), 0,
                                                         _lut(s, [h[1] for h in held])))

    return pl.pallas_call(
        body, name=name,
        grid_spec=pltpu.PrefetchScalarGridSpec(
            num_scalar_prefetch=1, grid=(len(steps),),
            in_specs=[pl.BlockSpec((t, dw), lambda s, slot: (0, 0))] + [b_spec(i) for i in range(n_arr)],
            out_specs=[pl.BlockSpec((1, dw, tn), lambda s, slot: (_lut(s, blks), 0, 0)),
                       pl.BlockSpec((dw, tn), lambda s, slot: (0, 0))],
            scratch_shapes=[pltpu.VMEM((dw, t), BF16)]),
        out_shape=[SDS((N_DEV, dw, tn), BF16), SDS((dw, tn), F32)],
        compiler_params=_cparams(),
    )(my_slot, a, *bs)


def _weight_grad_rows(name, my_slot, a, b):
    t, dw = a.shape
    n_o, _, c = b.shape
    rows = dw // N_DEV
    tn = min(c, 256)
    per = c // tn

    def body(slot_ref, a_ref, b_ref, o_ref, own_ref, at_ref, res_ref):
        @pl.when(pl.program_id(0) == 0)
        def _():
            at_ref[...] = a_ref[...].T

        res_ref[...] = _dot(at_ref[...], b_ref[0].astype(BF16))
        o_ref[...] = res_ref[...].astype(BF16)
        own_ref[...] = res_ref[pl.ds(pl.multiple_of(slot_ref[0] * rows, rows), rows), :]

    all_rows, own = pl.pallas_call(
        body, name=name,
        grid_spec=pltpu.PrefetchScalarGridSpec(
            num_scalar_prefetch=1, grid=(n_o * per,),
            in_specs=[pl.BlockSpec((t, dw), lambda s, slot: (0, 0)),
                      pl.BlockSpec((1, t, tn), lambda s, slot: (s // per, 0, s % per))],
            out_specs=[pl.BlockSpec((dw, tn), lambda s, slot: (0, s)),
                       pl.BlockSpec((rows, tn), lambda s, slot: (0, s))],
            scratch_shapes=[pltpu.VMEM((dw, t), BF16), pltpu.VMEM((dw, tn), F32)]),
        out_shape=[SDS((dw, n_o * c), BF16), SDS((rows, n_o * c), F32)],
        compiler_params=_cparams(),
    )(my_slot, a, b)
    return all_rows.reshape(N_DEV, rows, n_o * c), own


def _lane_lo():
    return lax.broadcasted_iota(jnp.int32, (1, 128), 1) < HEAD_DIM


def _collapse_chunks(ds, keys):
    if ds.shape[1] < keys:
        ds = jnp.concatenate([jnp.zeros((ds.shape[0], keys - ds.shape[1]), F32), ds], axis=1)
    gc = ds[0:CHUNK]
    for cc in range(1, ds.shape[0] // CHUNK):
        gc = gc + pltpu.roll(ds[cc * CHUNK:(cc + 1) * CHUNK], keys - cc * CHUNK, 1)
    return gc


def _offset_sums(gc):
    hi = gc.astype(BF16)
    lo = (gc - hi.astype(F32)).astype(BF16)
    flip = (lax.broadcasted_iota(jnp.int32, (CHUNK, CHUNK), 0)
            + lax.broadcasted_iota(jnp.int32, (CHUNK, CHUNK), 1) == CHUNK - 1).astype(BF16)
    gf = _dot(flip, hi) + _dot(flip, lo)
    skew = pltpu.roll(gf, 0, 1, stride=1, stride_axis=0)
    return jnp.sum(skew, axis=0, keepdims=True)


def _band_bias(w_row, band, rows):
    keys = w_row.shape[1]
    base = jnp.broadcast_to(w_row, (CHUNK, keys))
    skew = pltpu.roll(base, 0, 1, stride=1, stride_axis=0)
    skew = pltpu.roll(skew, keys - (CHUNK - 1), 1)
    col = lax.broadcasted_iota(jnp.int32, (CHUNK, keys), 1)
    chunk0 = jnp.where(col < band, skew, NEG)
    return jnp.concatenate(
        [chunk0] + [pltpu.roll(chunk0, cc * CHUNK, 1) for cc in range(1, rows // CHUNK)], axis=0)


def _silu_parts(g):
    sg = _sigmoid(g)
    return g * sg, sg * (1.0 + g * (1.0 - sg))


A_PAIRS_FWD = 8
A_PAIRS_BWD = 4


def _a_specs(pairs):
    lanes = 128 * pairs
    steps = D_MODEL // lanes
    q = pl.BlockSpec((QBLK, lanes), lambda p, j: (j, p))
    ks = [pl.BlockSpec((QBLK, lanes), lambda p, j, b=b: (jnp.maximum(j - 2 + b, 0), steps + p)) for b in range(3)]
    vs = [pl.BlockSpec((QBLK, lanes), lambda p, j, b=b: (jnp.maximum(j - 2 + b, 0), 2 * steps + p))
          for b in range(3)]
    g = pl.BlockSpec((QBLK, lanes), lambda p, j: (j, 3 * steps + p))
    bias = pl.BlockSpec((pairs, 8, A_KEYS), lambda p, j: (p, 0, 0))
    return q, ks, vs, g, bias


def _a_fill_bias(w_ref, b_ref, j, pairs):
    _fill_bias(2 * pairs, lambda h: w_ref[h // 2, h % 2:h % 2 + 1, :], A_BAND, b_ref, j)


def _by_valid_key_blocks(j, fn):
    pl.when(j == 0)(functools.partial(fn, 1))
    pl.when(j == 1)(functools.partial(fn, 2))
    pl.when(j >= 2)(functools.partial(fn, 3))


def _fill_bias(n, get_row, band, bias_scr, j):
    @pl.when(j == 0)
    def _():
        for h in range(n):
            bias_scr[h] = _band_bias(get_row(h), band, bias_scr.shape[1])


def _normalise_pair(rs, mxs, lane_lo, extra=None):
    num = jnp.where(lane_lo, rs[0], rs[1])
    den = pltpu.roll(jnp.where(lane_lo, rs[1], rs[0]), HEAD_DIM, 1)
    if extra is not None:
        den = den + jnp.where(lane_lo, extra[0], extra[1])
    return num / den, jnp.where(lane_lo, mxs[0], mxs[1]) + jnp.log(den)


def _own_everywhere(x, sel):
    return jnp.where(sel, x, pltpu.roll(x, HEAD_DIM, 1))


def _minus_rows(s, row_full):
    return jnp.concatenate([s[:, i:i + 128] - row_full for i in range(0, s.shape[1], 128)], axis=1)


def _attn_a_fwd(qkvg, bias, gather):
    t = qkvg.shape[0]
    nq = t // QBLK
    n_g = len(gather)
    pairs = A_PAIRS_FWD
    lanes = 128 * pairs
    steps = D_MODEL // lanes
    q_spec, k_specs, v_specs, g_spec, bias_spec = _a_specs(pairs)

    def body(q_ref, k0, k1, k2, v0, v1, v2, g_ref, w_ref, *rest):
        shard_refs, rest = rest[:n_g], rest[n_g:]
        z_ref, o_ref, lse_ref = rest[:3]
        full_refs, (b_ref, *comm) = rest[3:3 + n_g], rest[3 + n_g:]
        p = pl.program_id(0)
        j = pl.program_id(1)
        start, forward, finish = _gather_phases(shard_refs, full_refs, *comm)
        at = p * nq + j
        pl.when(at == 0)(start)
        pl.when(at == steps * nq // 2)(forward)
        _a_fill_bias(w_ref, b_ref, j, pairs)
        lane_lo = _lane_lo()
        sels = (lane_lo, jnp.logical_not(lane_lo))

        def attend(n_blocks):
            first_col = (3 - n_blocks) * QBLK
            for pp in range(pairs):
                cols = slice(128 * pp, 128 * (pp + 1))
                k = jnp.concatenate([r[:, cols] for r in (k0, k1, k2)[3 - n_blocks:]], axis=0)
                v = jnp.concatenate([r[:, cols] for r in (v0, v1, v2)[3 - n_blocks:]], axis=0)
                q = q_ref[:, cols]
                qm2 = jnp.concatenate([jnp.where(sel, q, jnp.zeros_like(q)) for sel in sels], axis=0) * SCALE
                s2 = _dot_nt(qm2, k)
                rs, mxs = [], []
                for hh, sel in enumerate(sels):
                    s = s2[hh * QBLK:(hh + 1) * QBLK] + b_ref[2 * pp + hh, :, first_col:]
                    mxs.append(jnp.max(s, axis=-1, keepdims=True))
                    e = jnp.exp(s - mxs[hh]).astype(BF16)
                    rs.append(_dot(e, jnp.where(sel, v, jnp.ones_like(v))))
                o, lse = _normalise_pair(rs, mxs, lane_lo)
                silu, _ = _silu_parts(g_ref[:, cols].astype(F32))
                o_ref[:, cols] = o.astype(BF16)
                z_ref[:, cols] = (o * silu).astype(BF16)
                lse_ref[:, cols] = lse

        _by_valid_key_blocks(j, attend)
        pl.when(at == steps * nq - 1)(finish)

    out_spec = pl.BlockSpec((QBLK, lanes), lambda p, j: (j, p))
    outs = pl.pallas_call(
        body, name="attn_a_fwd", grid=(steps, nq),
        in_specs=[q_spec, *k_specs, *v_specs, g_spec, bias_spec] + [ANY] * n_g,
        out_specs=[out_spec, out_spec, out_spec] + [ANY] * n_g,
        out_shape=[SDS((t, D_MODEL), BF16), SDS((t, D_MODEL), BF16), SDS((t, D_MODEL), F32)]
        + [SDS((N_DEV, *s.shape), s.dtype) for s in gather],
        scratch_shapes=[pltpu.VMEM((2 * pairs, QBLK, A_KEYS), F32)] + _gather_scratch(n_g),
        compiler_params=_cparams(),
    )(qkvg, qkvg, qkvg, qkvg, qkvg, qkvg, qkvg, qkvg, bias, *gather)
    return outs[0], outs[1], outs[2], list(outs[3:])


def _attn_a_bwd(qkvg, bias, out_a, lse, dz, scatter):
    t = qkvg.shape[0]
    nq = t // QBLK
    n_sc = len(scatter)
    pairs = A_PAIRS_BWD
    lanes = 128 * pairs
    steps = D_MODEL // lanes
    q_spec, k_specs, v_specs, g_spec, bias_spec = _a_specs(pairs)

    def body(q_ref, k0, k1, k2, v0, v1, v2, g_ref, w_ref, o_ref, lse_ref, dz_ref, *rest):
        sc_refs, rest = rest[:n_sc], rest[n_sc:]
        dqg_ref, dkv_ref, dg_ref = rest[:3]
        land_refs, rest = rest[3:3 + n_sc], rest[3 + n_sc:]
        dk_acc, dv_acc, gt_acc, b_ref, send_sems, recv_sems = rest
        j = pl.program_id(1)
        first = jnp.logical_and(pl.program_id(0) == 0, j == 0)
        last = jnp.logical_and(pl.program_id(0) == steps - 1, j == nq - 1)

        @pl.when(first)
        def _():
            for cp in _scatter_copies(sc_refs, land_refs, send_sems, recv_sems):
                cp.start()

        _a_fill_bias(w_ref, b_ref, j, pairs)

        @pl.when(j == 0)
        def _():
            dk_acc[...] = jnp.zeros_like(dk_acc)
            dv_acc[...] = jnp.zeros_like(dv_acc)
            gt_acc[...] = jnp.zeros_like(gt_acc)

        lane_lo = _lane_lo()
        sels = (lane_lo, jnp.logical_not(lane_lo))

        def attend(n_blocks):
            first_col = (3 - n_blocks) * QBLK
            for pp in range(pairs):
                cols = slice(128 * pp, 128 * (pp + 1))
                q = q_ref[:, cols]
                k = jnp.concatenate([r[:, cols] for r in (k0, k1, k2)[3 - n_blocks:]], axis=0)
                v = jnp.concatenate([r[:, cols] for r in (v0, v1, v2)[3 - n_blocks:]], axis=0)
                o = o_ref[:, cols].astype(F32)
                lse_pair = lse_ref[:, cols]
                dzf = dz_ref[:, cols].astype(F32)
                silu, dsilu = _silu_parts(g_ref[:, cols].astype(F32))
                do = dzf * silu
                dqg_ref[1, :, cols] = (dzf * o * dsilu).astype(BF16)
                doo = do * o
                qm2 = jnp.concatenate([jnp.where(sel, q, jnp.zeros_like(q)) for sel in sels], axis=0) * SCALE
                dom2 = jnp.concatenate([jnp.where(sel, do, 0.0) for sel in sels], axis=0).astype(BF16)
                s2 = _dot_nt(qm2, k)
                dp2 = _dot_nt(dom2, v)
                ps, dss = [], []
                for hh, sel in enumerate(sels):
                    rows = slice(hh * QBLK, (hh + 1) * QBLK)
                    s = s2[rows] + b_ref[2 * pp + hh, :, first_col:]
                    p = jnp.exp(_minus_rows(s, _own_everywhere(lse_pair, sel)))
                    delta = jnp.sum(jnp.where(sel, doo, 0.0), axis=-1, keepdims=True)
                    ds = p * (dp2[rows] - delta)
                    gt_acc[2 * pp + hh] += _collapse_chunks(ds, A_KEYS)
                    ps.append(p.astype(BF16))
                    dss.append(ds.astype(BF16))
                dsb2 = jnp.concatenate(dss, axis=0)
                dq2 = _dot(dsb2, k) * SCALE
                dk_blk = _dot_tn(dsb2, qm2)
                dv_blk = _dot_tn(jnp.concatenate(ps, axis=0), dom2)
                dqg_ref[0, :, cols] = jnp.where(lane_lo, dq2[0:QBLK], dq2[QBLK:2 * QBLK]).astype(BF16)
                for b in range(n_blocks):
                    rows = pl.ds(pl.multiple_of((j - n_blocks + 1 + b) * QBLK, QBLK), QBLK)
                    dk_acc[rows, cols] += dk_blk[b * QBLK:(b + 1) * QBLK]
                    dv_acc[rows, cols] += dv_blk[b * QBLK:(b + 1) * QBLK]

        _by_valid_key_blocks(j, attend)

        @pl.when(j == nq - 1)
        def _():
            dkv_ref[0] = dk_acc[...].astype(BF16)
            dkv_ref[1] = dv_acc[...].astype(BF16)
            for pp in range(pairs):
                dg_ref[pp] = jnp.concatenate([_offset_sums(gt_acc[2 * pp]), _offset_sums(gt_acc[2 * pp + 1]),
                                              jnp.zeros((6, A_DIAG), F32)], axis=0)

        @pl.when(last)
        def _():
            for cp in _scatter_copies(sc_refs, land_refs, send_sems, recv_sems):
                cp.wait()

    blk = pl.BlockSpec((QBLK, lanes), lambda p, j: (j, p))
    outs = pl.pallas_call(
        body, name="attn_a_bwd", grid=(steps, nq),
        in_specs=[q_spec, *k_specs, *v_specs, g_spec, bias_spec, blk, blk, blk] + [ANY] * n_sc,
        out_specs=[pl.BlockSpec((2, QBLK, lanes), lambda p, j: (0, j, p)),
                   pl.BlockSpec((2, t, lanes), lambda p, j: (0, 0, p)),
                   pl.BlockSpec((pairs, 8, A_DIAG), lambda p, j: (p, 0, 0))] + [ANY] * n_sc,
        out_shape=[SDS((2, t, D_MODEL), BF16), SDS((2, t, D_MODEL), BF16), SDS((N_HEADS // 2, 8, A_DIAG), F32)]
        + [SDS((N_DEV - 1, *g.shape[1:]), g.dtype) for g in scatter],
        scratch_shapes=[pltpu.VMEM((t, lanes), F32), pltpu.VMEM((t, lanes), F32),
                        pltpu.VMEM((2 * pairs, CHUNK, A_KEYS), F32), pltpu.VMEM((2 * pairs, QBLK, A_KEYS), F32),
                        pltpu.SemaphoreType.DMA(((N_DEV - 1) * n_sc,)),
                        pltpu.SemaphoreType.DMA(((N_DEV - 1) * n_sc,))],
        compiler_params=_cparams(),
    )(qkvg, qkvg, qkvg, qkvg, qkvg, qkvg, qkvg, qkvg, bias, out_a, lse, dz, *scatter)
    return outs[0], outs[1], outs[2], list(outs[3:])


def _b_specs(qblk):
    per = qblk // B_PREV
    q = pl.BlockSpec((qblk, 512), lambda h, j: (j, h))
    g = pl.BlockSpec((qblk, 512), lambda h, j: (j, 2 + h))
    kp = pl.BlockSpec((B_PREV, 128), lambda h, j: (jnp.maximum(per * j - 1, 0), 0))
    kc = pl.BlockSpec((qblk, 128), lambda h, j: (j, 0))
    vp = pl.BlockSpec((B_PREV, 128), lambda h, j: (jnp.maximum(per * j - 1, 0), 1))
    vc = pl.BlockSpec((qblk, 128), lambda h, j: (j, 1))
    bias = pl.BlockSpec((B_GROUP, qblk + B_PREV), lambda h, j: (h, 0))
    sinks = pl.BlockSpec(memory_space=pltpu.SMEM)
    return q, g, kp, kc, vp, vc, bias, sinks


def _b_operands(kp, kc, vp, vc, kvh, with_prev):
    k = jnp.concatenate([kp[...], kc[...]], axis=0) if with_prev else kc[...]
    v = jnp.concatenate([vp[...], vc[...]], axis=0) if with_prev else vc[...]
    kr = pltpu.roll(k, HEAD_DIM, 1)
    vr = pltpu.roll(v, HEAD_DIM, 1)
    first = kvh == 0
    return (jnp.where(first, k, kr), jnp.where(first, kr, k),
            jnp.where(first, v, vr), jnp.where(first, vr, v))


def _attn_b_fwd(qg, kv, bias, sinks):
    t = qg.shape[0]
    qblk = B_QBLK_FWD
    per_step = 4
    step = per_step * qblk
    q_spec, g_spec, kp_spec, kc_spec, vp_spec, vc_spec, _, sink_spec = _b_specs(step)
    bias_spec = pl.BlockSpec((B_GROUP, qblk + B_PREV), lambda h, j: (h, 0))

    def body(q_ref, g_ref, kp, kc, vp, vc, w_ref, sink_ref, z_ref, o_ref, lse_ref, b_ref):
        kvh = pl.program_id(0)
        j = pl.program_id(1)
        _fill_bias(B_GROUP, lambda h: w_ref[h:h + 1, :], B_BAND, b_ref, j)
        lane_lo = _lane_lo()
        n_pairs = B_GROUP // 2

        def attend(first):
            k_lo, k_hi, v_lo, v_hi = _b_operands(kp, kc, vp, vc, kvh, True)
            for sb in range(per_step):
                no_prev = first and sb == 0
                first_col = B_PREV if no_prev else 0
                keys = slice(sb * qblk + first_col, (sb + 1) * qblk + B_PREV)
                qrows = slice(sb * qblk, (sb + 1) * qblk)
                halves = []
                for hh, sel in enumerate((lane_lo, jnp.logical_not(lane_lo))):
                    kk = (k_lo if hh == 0 else k_hi)[keys]
                    vv = (v_lo if hh == 0 else v_hi)[keys]
                    qm4 = jnp.concatenate(
                        [jnp.where(sel, q_ref[qrows, 128 * pp:128 * (pp + 1)], jnp.zeros((qblk, 128), BF16))
                         for pp in range(n_pairs)], axis=0) * SCALE
                    s4 = _dot_nt(qm4, kk)
                    es, mxs = [], []
                    for pp in range(n_pairs):
                        g = 2 * pp + hh
                        s = s4[pp * qblk:(pp + 1) * qblk] + b_ref[g, :, first_col:]
                        mxs.append(jnp.maximum(jnp.max(s, axis=-1, keepdims=True), sink_ref[kvh * B_GROUP + g]))
                        es.append(jnp.exp(s - mxs[pp]).astype(BF16))
                    r4 = _dot(jnp.concatenate(es, axis=0), jnp.where(sel, vv, jnp.ones_like(vv)))
                    halves.append((r4, mxs))
                for pp in range(n_pairs):
                    cols = slice(128 * pp, 128 * (pp + 1))
                    rows = slice(pp * qblk, (pp + 1) * qblk)
                    mxs = [halves[hh][1][pp] for hh in range(2)]
                    sink_terms = [jnp.exp(sink_ref[kvh * B_GROUP + 2 * pp + hh] - mxs[hh]) for hh in range(2)]
                    o, lse = _normalise_pair([halves[hh][0][rows] for hh in range(2)], mxs, lane_lo, sink_terms)
                    silu, _ = _silu_parts(g_ref[qrows, cols].astype(F32))
                    o_ref[qrows, cols] = o.astype(BF16)
                    z_ref[qrows, cols] = (o * silu).astype(BF16)
                    lse_ref[qrows, cols] = lse

        pl.when(j == 0)(functools.partial(attend, True))
        pl.when(j >= 1)(functools.partial(attend, False))

    out_spec = pl.BlockSpec((step, 512), lambda h, j: (j, h))
    return pl.pallas_call(
        body, name="attn_b_fwd", grid=(B_KV_HEADS, t // step),
        in_specs=[q_spec, g_spec, kp_spec, kc_spec, vp_spec, vc_spec, bias_spec, sink_spec],
        out_specs=[out_spec, out_spec, out_spec],
        out_shape=[SDS((t, D_MODEL), BF16), SDS((t, D_MODEL), BF16), SDS((t, D_MODEL), F32)],
        scratch_shapes=[pltpu.VMEM((B_GROUP, qblk, qblk + B_PREV), F32)],
        compiler_params=_cparams(),
    )(qg, qg, kv, kv, kv, kv, bias, sinks)


def _attn_b_bwd(qg, kv, bias, sinks, out_b, lse, dz, bucket_onehot):
    t = qg.shape[0]
    qblk = B_QBLK_BWD
    keys = qblk + B_PREV
    nq = t // qblk
    per = qblk // B_PREV

    def body(q_ref, g_ref, kp, kc, vp, vc, w_ref, sink_ref, o_ref, lse_ref, dz_ref, oh_ref,
             dqg_ref, dkv_ref, dt5_ref, dsink_ref, gt_acc, b_ref):
        j = pl.program_id(0)
        _fill_bias(N_HEADS, lambda h: w_ref[h:h + 1, :], B_BAND, b_ref, j)

        @pl.when(j == 0)
        def _():
            dkv_ref[...] = jnp.zeros_like(dkv_ref)
            gt_acc[...] = jnp.zeros_like(gt_acc)
            dsink_ref[...] = jnp.zeros_like(dsink_ref)

        lane_lo = _lane_lo()

        def attend(with_prev):
            first_col = 0 if with_prev else B_PREV
            dk_add = jnp.zeros((keys - first_col, 128), F32)
            dv_add = jnp.zeros((keys - first_col, 128), F32)
            for kvh in range(B_KV_HEADS):
                k_lo, k_hi, v_lo, v_hi = _b_operands(kp, kc, vp, vc, kvh, with_prev)
                dk_blk = jnp.zeros((keys - first_col, 128), F32)
                dv_blk = jnp.zeros((keys - first_col, 128), F32)
                for pp in range(B_GROUP // 2):
                    cols = slice(512 * kvh + 128 * pp, 512 * kvh + 128 * (pp + 1))
                    qp = q_ref[:, cols]
                    o = o_ref[:, cols].astype(F32)
                    lse_pair = lse_ref[:, cols]
                    dzf = dz_ref[:, cols].astype(F32)
                    silu, dsilu = _silu_parts(g_ref[:, cols].astype(F32))
                    do = dzf * silu
                    dqg_ref[1, :, cols] = (dzf * o * dsilu).astype(BF16)
                    doo = do * o
                    dqs = []
                    for hh in range(2):
                        g = kvh * B_GROUP + 2 * pp + hh
                        sel = lane_lo if hh == 0 else jnp.logical_not(lane_lo)
                        kk = k_lo if hh == 0 else k_hi
                        vv = v_lo if hh == 0 else v_hi
                        qm = jnp.where(sel, qp, jnp.zeros_like(qp)) * SCALE
                        s = _dot_nt(qm, kk) + b_ref[g, :, first_col:]
                        lse_h = _own_everywhere(lse_pair, sel)
                        p = jnp.exp(_minus_rows(s, lse_h))
                        delta = jnp.sum(jnp.where(sel, doo, 0.0), axis=-1, keepdims=True)
                        dom = jnp.where(sel, do, 0.0).astype(BF16)
                        dp = _dot_nt(dom, vv)
                        ds = p * (dp - delta)
                        gt_acc[g, :, first_col:] += ds
                        dsink_ref[g:g + 1, :] -= jnp.sum(jnp.exp(sink_ref[g] - lse_h) * delta, axis=0, keepdims=True)
                        dsb = ds.astype(BF16)
                        dqs.append(_dot(dsb, kk) * SCALE)
                        dk_blk = dk_blk + _dot_tn(dsb, qm)
                        dv_blk = dv_blk + _dot_tn(p.astype(BF16), dom)
                    dqg_ref[0, :, cols] = jnp.where(lane_lo, dqs[0], dqs[1]).astype(BF16)
                mine = lane_lo if kvh == 0 else jnp.logical_not(lane_lo)
                dk_add = dk_add + jnp.where(mine, dk_blk + pltpu.roll(dk_blk, HEAD_DIM, 1), 0.0)
                dv_add = dv_add + jnp.where(mine, dv_blk + pltpu.roll(dv_blk, HEAD_DIM, 1), 0.0)
            first_key = B_PREV if with_prev else 0
            if with_prev:
                rows = pl.ds(pl.multiple_of(j * qblk - B_PREV, B_PREV), B_PREV)
                dkv_ref[0, rows, :] += dk_add[0:B_PREV]
                dkv_ref[1, rows, :] += dv_add[0:B_PREV]
            rows = pl.ds(pl.multiple_of(j * qblk, qblk), qblk)
            dkv_ref[0, rows, :] += dk_add[first_key:first_key + qblk]
            dkv_ref[1, rows, :] += dv_add[first_key:first_key + qblk]

        pl.when(j == 0)(functools.partial(attend, False))
        pl.when(j >= 1)(functools.partial(attend, True))

        @pl.when(j == nq - 1)
        def _():
            dd = jnp.concatenate([_offset_sums(_collapse_chunks(gt_acc[g], keys)) for g in range(N_HEADS)], axis=0)
            hi = dd.astype(BF16)
            lo = (dd - hi.astype(F32)).astype(BF16)
            dt5_ref[...] = _dot(hi, oh_ref[...]) + _dot(lo, oh_ref[...])

    wide = lambda col: pl.BlockSpec((qblk, D_MODEL), lambda j, col=col: (j, col))
    prev = lambda col: pl.BlockSpec((B_PREV, 128), lambda j, col=col: (jnp.maximum(per * j - 1, 0), col))
    cur = lambda col: pl.BlockSpec((qblk, 128), lambda j, col=col: (j, col))
    fixed = lambda shape: pl.BlockSpec(shape, lambda j: (0,) * len(shape))
    return pl.pallas_call(
        body, name="attn_b_bwd", grid=(nq,),
        in_specs=[wide(0), wide(1), prev(0), cur(0), prev(1), cur(1), fixed((N_HEADS, keys)),
                  pl.BlockSpec(memory_space=pltpu.SMEM), wide(0), wide(0), wide(0), fixed((keys, 128))],
        out_specs=[pl.BlockSpec((2, qblk, D_MODEL), lambda j: (0, j, 0)), fixed((2, t, 128)),
                   fixed((N_HEADS, 128)), fixed((N_HEADS, 128))],
        out_shape=[SDS((2, t, D_MODEL), BF16), SDS((2, t, 128), F32),
                   SDS((N_HEADS, 128), F32), SDS((N_HEADS, 128), F32)],
        scratch_shapes=[pltpu.VMEM((N_HEADS, qblk, keys), F32), pltpu.VMEM((N_HEADS, qblk, keys), F32)],
        compiler_params=_cparams(),
    )(qg, qg, kv, kv, kv, kv, bias, sinks, out_b, lse, dz, bucket_onehot)


def _a_bias_by_offset(rel_bias):
    m = np.arange(A_DIAG)
    idx = np.clip(A_BAND - 1 - m, -A_REL_CLIP, A_REL_CLIP) + A_REL_CLIP
    by_head = rel_bias[idx].T.reshape(N_HEADS // 2, 2, A_DIAG)
    return jnp.concatenate([by_head, jnp.zeros((N_HEADS // 2, 6, A_DIAG), F32)], axis=1)


def _a_bias_grad(offset_sums):
    first = 319
    tail = jnp.sum(offset_sums[:, :first], axis=1)
    body = jnp.flip(offset_sums[:, first:first + 320], axis=1)
    body = body.at[:, -1].add(tail)
    full = jnp.concatenate([jnp.zeros((N_HEADS, 193), F32), body], axis=1)
    return full


def _t5_bucket(rel):
    nb = T5_BUCKETS // 2
    max_exact = nb // 2
    ret = jnp.where(rel > 0, nb, 0)
    n = jnp.abs(rel)
    nf = jnp.maximum(n, 1).astype(jnp.float32)
    large = max_exact + (jnp.log(nf / max_exact) / math.log(T5_MAX_DIST / max_exact)
                         * (nb - max_exact)).astype(jnp.int32)
    large = jnp.minimum(large, nb - 1)
    return ret + jnp.where(n < max_exact, n, large)


def _b_offset_buckets(keys):
    return _t5_bucket(jnp.arange(keys, dtype=jnp.int32) - (B_LEFT_CHUNKS * CHUNK + CHUNK - 1))


def _b_bias_by_offset(t5_table, keys):
    return t5_table[_b_offset_buckets(keys)].T


def _b_bucket_onehot(keys):
    return (_b_offset_buckets(keys)[:, None] == jnp.arange(128)[None, :]).astype(BF16)


def _local_step(my_slot, order, x, target, a_gain_shard, w_in_a_shard, rel_bias, late_shards, kv_gain,
                t5_table, b_gain, sinks, f_gain):
    a_bias = _a_bias_by_offset(rel_bias)
    b_bias_fwd = _b_bias_by_offset(t5_table, B_QBLK_FWD + B_PREV)
    b_bias_bwd = _b_bias_by_offset(t5_table, B_QBLK_BWD + B_PREV)
    sinks_flat = sinks.reshape(N_HEADS)

    xn, qkvg, w_in_a, a_gain = _norm_matmul_gather(order, x, a_gain_shard, w_in_a_shard)
    z_a, out_a, lse_a, (w_in_b, w_out_a, w_out_b, kv_w) = _attn_a_fwd(qkvg, a_bias, late_shards)
    w_out_a = w_out_a.reshape(D_MODEL, D_MODEL)
    w_out_b = w_out_b.reshape(D_MODEL, D_MODEL)
    kv_w = kv_w.reshape(D_MODEL, 2 * 128)
    h1, kvn, hb, kv, qg = _layer_a_out(x, z_a, w_out_a, kv_gain, b_gain, kv_w, w_in_b)
    z_b, out_b, lse_b = _attn_b_fwd(qg, kv, b_bias_fwd, sinks_flat)
    dh2, dh2b, dz_b, loss, d_fn = _layer_b_out_loss(h1, z_b, w_out_b, f_gain, target)

    dqg_b, dkv_b, d_t5, d_sink = _attn_b_bwd(qg, kv, b_bias_bwd, sinks_flat, out_b, lse_b, dz_b,
                                             _b_bucket_onehot(B_QBLK_BWD + B_PREV))
    dh1, dh1b, dz_a, d_bn, d_kn = _layer_b_in_bwd(dqg_b, dkv_b, w_in_b, kv_w, h1, dh2, b_gain, kv_gain, w_out_a)
    early = dict(
        b_w_out=_weight_grad_rows("grad_b_w_out", my_slot, z_b, dh2b[None]),
        b_w_in=_weight_grad_cols("grad_b_w_in", my_slot, hb, [dqg_b],
                                 [(0, o, c, 4 * o + c) for o in range(2) for c in range(4)], 256),
        kv_w=_weight_grad_rows("grad_kv_w", my_slot, kvn, dkv_b),
        a_w_out=_weight_grad_rows("grad_a_w_out", my_slot, z_a, dh1b[None]))
    dqg_a, dkv_a, d_rel, landed = _attn_a_bwd(qkvg, a_bias, out_a, lse_a, dz_a, [g[0] for g in early.values()])
    ready = dict(
        loss=loss, a_rel_bias=d_rel[:, :2].reshape(N_HEADS, A_DIAG),
        kv_norm=d_kn, t5_bias=d_t5, b_norm=d_bn, b_sinks=d_sink, final_norm=d_fn)
    g_own, from_sibling, from_far, chip_sums, ready_sums = _grad_a_w_in_reduce(
        _a_w_in_grad_order(), xn, dqg_a, dkv_a, list(ready.values()))
    grad_x, d_an, from_near = _layer_a_in_bwd(dqg_a, dkv_a, w_in_a, x, dh1, a_gain, chip_sums)

    matrices = {n: (g[1], [(land, 0, N_DEV - 1)]) for (n, g), land in zip(early.items(), landed)}
    matrices["a_w_in"] = (g_own, [(from_sibling, 0, 1), (from_far, 0, from_far.shape[0]),
                                  (from_near, 0, from_near.shape[0])])
    small = dict(zip(ready.keys(), ready_sums), a_norm=d_an)
    return grad_x, small, matrices


def _place():
    x, y, c = lax.axis_index("x"), lax.axis_index("y"), lax.axis_index("c")
    chips = [(1 - x, y), (x, 1 - y), (1 - x, 1 - y)]
    return x, y, c, chips


def _slot(px, py, pc):
    return 4 * px + 2 * py + pc


ANY = pl.BlockSpec(memory_space=pl.ANY)


def _peer(x, y, c, k):
    return (x ^ (k >> 2), y ^ ((k >> 1) & 1), c ^ (k & 1))


def _scatter_copies(grad_refs, land_refs, send_sems, recv_sems):
    x, y, c, _ = _place()
    copies = []
    for t, (grad, land) in enumerate(zip(grad_refs, land_refs)):
        for k in range(1, N_DEV):
            peer = _peer(x, y, c, k)
            sem = (N_DEV - 1) * t + k - 1
            copies.append(pltpu.make_async_remote_copy(
                src_ref=grad.at[_slot(*peer)], dst_ref=land.at[k - 1],
                send_sem=send_sems.at[sem], recv_sem=recv_sems.at[sem],
                device_id=peer, device_id_type=MESH))
    return copies


def _gather_phases(ins, outs, send_sems, recv_sems, local_sems):
    n = len(ins)
    x, y, c, chips = _place()
    me, sibling = (x, y, c), (x, y, 1 - c)

    def copy(t, k, block, to, src=None):
        dst = outs[t].at[_slot(*block)]
        return pltpu.make_async_remote_copy(
            src_ref=dst if src is None else src, dst_ref=dst,
            send_sem=send_sems.at[7 * t + k], recv_sem=recv_sems.at[7 * t + k],
            device_id=to, device_id_type=MESH)

    def lists():
        mine = [pltpu.make_async_copy(ins[t], outs[t].at[_slot(*me)], local_sems.at[t]) for t in range(n)]
        first = []
        for t in range(n):
            first.append(copy(t, 0, me, sibling, src=ins[t]))
            first += [copy(t, 1 + j, me, (*chip, c), src=ins[t]) for j, chip in enumerate(chips)]
        passed = [copy(t, 4 + j, (*chip, c), sibling) for t in range(n) for j, chip in enumerate(chips)]
        return mine, first, passed

    def start():
        mine, first, _ = lists()
        for cp in mine + first:
            cp.start()

    def forward():
        _, _, passed = lists()
        for t in range(n):
            for j, chip in enumerate(chips):
                copy(t, 1 + j, (*chip, c), me).wait_recv()
                passed[3 * t + j].start()

    def finish():
        mine, first, passed = lists()
        for t in range(n):
            copy(t, 0, sibling, me).wait_recv()
            for j, chip in enumerate(chips):
                copy(t, 4 + j, (*chip, 1 - c), me).wait_recv()
        for cp in first + passed:
            cp.wait_send()
        for cp in mine:
            cp.wait()

    return start, forward, finish


def _gather_scratch(n):
    return [pltpu.SemaphoreType.DMA((7 * n,)), pltpu.SemaphoreType.DMA((7 * n,)), pltpu.SemaphoreType.DMA((n,))]


_FAR_CHIP_FIRST = (2, 0, 1)
_SUMS_SENT_AT_ONCE = 1


def _a_w_in_grad_order():
    x, y, c, chips = _place()
    slots = []
    for j in _FAR_CHIP_FIRST:
        slots += [_slot(*chips[j], 1 - c), _slot(*chips[j], c)]
    slots += [_slot(x, y, 1 - c), _slot(x, y, c)]
    return jnp.stack(slots).astype(jnp.int32)


def _grad_a_w_in_reduce(order, a, dqg, dkv, small):
    t, dw = a.shape
    tn = dqg.shape[2] // 2
    n_s = len(small)
    n_far, n_sent = len(_FAR_CHIP_FIRST), _SUMS_SENT_AT_ONCE

    def body(order_ref, a_ref, dqg_ref, dkv_ref, *rest):
        small_refs, rest = rest[:n_s], rest[n_s:]
        own_ref, sib_ref, chips_ref, later_ref = rest[:4]
        small_out, rest = rest[4:4 + n_s], rest[4 + n_s:]
        a_buf, at_ref, res_ref, stage, land, load_sem, d2d_send, d2d_recv, ici_send, ici_recv = rest[:10]
        small_lands, (small_send, small_recv) = rest[10:10 + n_s], rest[10 + n_s:]
        s = pl.program_id(0)
        x, y, c, chips = _place()

        def to_sibling(i):
            return pltpu.make_async_remote_copy(
                src_ref=stage.at[i], dst_ref=land.at[i] if i < n_far else sib_ref.at[0],
                send_sem=d2d_send.at[i], recv_sem=d2d_recv.at[i], device_id=(x, y, 1 - c), device_id_type=MESH)

        def to_chip(i):
            return pltpu.make_async_remote_copy(
                src_ref=land.at[i], dst_ref=chips_ref.at[i], send_sem=ici_send.at[i], recv_sem=ici_recv.at[i],
                device_id=(*chips[_FAR_CHIP_FIRST[i]], c), device_id_type=MESH)

        @pl.when(s == 0)
        def _():
            load = pltpu.make_async_copy(a_ref, a_buf, load_sem)
            load.start()
            load.wait()
            at_ref[...] = a_buf[...].T

        blk = order_ref[s]
        from_qg = jnp.logical_or(blk < 2, blk >= 6)

        @pl.when(from_qg)
        def _():
            res_ref[...] = _dot(at_ref[...], dqg_ref[0])

        @pl.when(jnp.logical_not(from_qg))
        def _():
            res_ref[...] = _dot(at_ref[...], dkv_ref[0])

        for i in range(n_far + 1):
            @pl.when(s == 2 * i)
            def _(i=i):
                stage[i] = res_ref[...].astype(BF16)
                to_sibling(i).start()

        for i in range(n_far):
            @pl.when(s == 2 * i + 1)
            def _(i=i):
                to_sibling(i).wait_recv()
                total = (res_ref[...] + land[i].astype(F32)).astype(BF16)
                if i < n_sent:
                    land[i] = total
                    to_chip(i).start()
                else:
                    later_ref[i - n_sent] = total

        @pl.when(s == N_DEV - 1)
        def _():
            own_ref[...] = res_ref[...]
            _all_reduce_small(small_refs, small_out, small_lands, small_send, small_recv)
            for i in range(n_far + 1):
                to_sibling(i).wait_send()
            to_sibling(n_far).wait_recv()
            for i in range(n_sent):
                to_chip(i).wait()

    whole = pl.BlockSpec(memory_space=pltpu.VMEM)
    outs = pl.pallas_call(
        body, name="grad_a_w_in",
        grid_spec=pltpu.PrefetchScalarGridSpec(
            num_scalar_prefetch=1, grid=(N_DEV,),
            in_specs=[ANY,
                      pl.BlockSpec((1, t, tn), lambda s, o: (o[s] // 6, 0, o[s] % 2)),
                      pl.BlockSpec((1, t, tn), lambda s, o: ((o[s] // 4) % 2, 0, o[s] % 2))] + [whole] * n_s,
            out_specs=[pl.BlockSpec((dw, tn), lambda s, o: (0, 0)), ANY, ANY, whole] + [whole] * n_s,
            scratch_shapes=[pltpu.VMEM((t, dw), BF16), pltpu.VMEM((dw, t), BF16), pltpu.VMEM((dw, tn), F32),
                            pltpu.VMEM((n_far + 1, dw, tn), BF16), pltpu.VMEM((n_far, dw, tn), BF16),
                            pltpu.SemaphoreType.DMA,
                            pltpu.SemaphoreType.DMA((n_far + 1,)), pltpu.SemaphoreType.DMA((n_far + 1,)),
                            pltpu.SemaphoreType.DMA((n_sent,)), pltpu.SemaphoreType.DMA((n_sent,))]
            + _all_reduce_scratch([s.shape for s in small])),
        out_shape=[SDS((dw, tn), F32), SDS((1, dw, tn), BF16), SDS((n_sent, dw, tn), BF16),
                   SDS((n_far - n_sent, dw, tn), BF16)] + [SDS(s.shape, F32) for s in small],
        compiler_params=_cparams(),
    )(order, a, dqg, dkv, *small)
    return outs[0], outs[1], outs[2], outs[3], list(outs[4:])


def _later_chip_copies(sums_ref, land_ref, send_sems, recv_sems):
    x, y, c, chips = _place()
    del x, y
    return [pltpu.make_async_remote_copy(
        src_ref=sums_ref.at[i], dst_ref=land_ref.at[i], send_sem=send_sems.at[i], recv_sem=recv_sems.at[i],
        device_id=(*chips[j], c), device_id_type=MESH) for i, j in enumerate(_FAR_CHIP_FIRST[_SUMS_SENT_AT_ONCE:])]


def _row_tile(rows):
    return min(rows, 512)


def _adamw(w, g, m, v):
    m2 = ADAM_B1 * m + (1.0 - ADAM_B1) * g
    v2 = ADAM_B2 * v + (1.0 - ADAM_B2) * jnp.square(g)
    m_hat = m2 / (1.0 - ADAM_B1 ** ADAM_STEP)
    v_hat = v2 / (1.0 - ADAM_B2 ** ADAM_STEP)
    delta = -ADAM_LR * (m_hat / (jnp.sqrt(v_hat) + ADAM_EPS) + ADAM_WD * w)
    return delta, m2, v2


def _reduce_adamw(name, own, partials, w, m, v):
    r, c = own.shape
    tr = _row_tile(r)
    n_p = len(partials)

    def body(own_ref, *rest):
        p_refs, (w_ref, m_ref, v_ref, grad_ref, d_ref, nm_ref, nv_ref) = rest[:n_p], rest[n_p:]
        grad = own_ref[...]
        for p_ref, (_, _, count) in zip(p_refs, partials):
            for j in range(count):
                grad = grad + p_ref[j].astype(F32)
        grad_ref[...] = grad
        d_ref[...], nm_ref[...], nv_ref[...] = _adamw(w_ref[...], grad, m_ref[...], v_ref[...])

    flat = pl.BlockSpec((tr, c), lambda i: (i, 0))
    return pl.pallas_call(
        body, name=name, grid=(r // tr,),
        in_specs=[flat] + [pl.BlockSpec((count, tr, c), lambda i, first=first, count=count: (first // count, i, 0))
                           for _, first, count in partials] + [flat, flat, flat],
        out_specs=[flat, flat, flat, flat],
        out_shape=[SDS((r, c), F32)] * 4,
        compiler_params=_cparams(),
    )(own, *[p[0] for p in partials], w, m, v)


VM = pl.BlockSpec()


def _all_reduce_small(ins, outs, lands, send_sems, recv_sems):
    x, y, c, _ = _place()
    my_slot = _slot(x, y, c)
    copies = []
    for t, (src, land) in enumerate(zip(ins, lands)):
        land[my_slot] = src[...]
        for k in range(1, N_DEV):
            sem = (N_DEV - 1) * t + k - 1
            copies.append(pltpu.make_async_remote_copy(
                src_ref=src, dst_ref=land.at[my_slot],
                send_sem=send_sems.at[sem], recv_sem=recv_sems.at[sem],
                device_id=_peer(x, y, c, k), device_id_type=MESH))
    for cp in copies:
        cp.start()
    for t, (src, land) in enumerate(zip(ins, lands)):
        for k in range(1, N_DEV):
            sem = (N_DEV - 1) * t + k - 1
            pltpu.make_async_remote_copy(
                src_ref=src, dst_ref=land.at[_slot(*_peer(x, y, c, k))],
                send_sem=send_sems.at[sem], recv_sem=recv_sems.at[sem],
                device_id=(x, y, c), device_id_type=MESH).wait_recv()
    for cp in copies:
        cp.wait_send()
    for out, land in zip(outs, lands):
        total = land[0]
        for s in range(1, N_DEV):
            total = total + land[s]
        out[...] = total


def _all_reduce_scratch(shapes):
    n_sems = (N_DEV - 1) * len(shapes)
    return ([pltpu.VMEM((N_DEV, *s), F32) for s in shapes]
            + [pltpu.SemaphoreType.DMA((n_sems,)), pltpu.SemaphoreType.DMA((n_sems,))])


def _small_adamw(my_slot, sums, ws, ms, vs):
    n = len(ws)

    def body(slot_ref, *refs):
        sum_refs, refs = refs[:n + 1], refs[n + 1:]
        w_refs, m_refs, v_refs, refs = refs[:n], refs[n:2 * n], refs[2 * n:3 * n], refs[3 * n:]
        g_refs, d_refs, nm_refs, nv_refs = refs[:n + 1], refs[n + 1:2 * n + 1], refs[2 * n + 1:3 * n + 1], refs[3 * n + 1:]
        for t in range(n + 1):
            if t == 0:
                g = sum_refs[0][:, pl.ds(pl.multiple_of(slot_ref[0] * 128, 128), 128)]
            else:
                g = sum_refs[t][...]
            g_refs[t][...] = g
            if t < n:
                d_refs[t][...], nm_refs[t][...], nv_refs[t][...] = _adamw(w_refs[t][...], g, m_refs[t][...], v_refs[t][...])

    shapes = [SDS(w.shape, F32) for w in ws]
    outs = pl.pallas_call(
        body, name="small_adamw",
        in_specs=[pl.BlockSpec(memory_space=pltpu.SMEM)] + [VM] * (4 * n + 1),
        out_specs=[VM] * (4 * n + 1),
        out_shape=shapes + [SDS(sums[-1].shape, F32)] + shapes * 3,
    )(my_slot, *sums, *ws, *ms, *vs)
    return outs[:n + 1], outs[n + 1:2 * n + 1], outs[2 * n + 1:3 * n + 1], outs[3 * n + 1:]


def kernel(x, a_norm, a_w_in, a_rel_bias, a_w_out, kv_norm, kv_w, t5_bias, b_norm, b_w_in, b_sinks, b_w_out, final_norm, loss_target, m_a_norm, m_a_w_in, m_a_rel_bias, m_a_w_out, m_kv_norm, m_kv_w, m_t5_bias, m_b_norm, m_b_w_in, m_b_sinks, m_b_w_out, m_final_norm, v_a_norm, v_a_w_in, v_a_rel_bias, v_a_w_out, v_kv_norm, v_kv_w, v_t5_bias, v_b_norm, v_b_w_in, v_b_sinks, v_b_w_out, v_final_norm):
    xi, yi, ci = lax.axis_index("x"), lax.axis_index("y"), lax.axis_index("c")
    my_slot = _slot(xi, yi, ci)

    slot_arr = jnp.reshape(my_slot, (1,)).astype(jnp.int32)
    order = _gather_order(xi, yi, ci)
    late_shards = [b_w_in[0].astype(BF16), a_w_out[0].astype(BF16), b_w_out[0].astype(BF16), kv_w.astype(BF16)]
    grad_x, loc, matrices = _local_step(
        slot_arr, order, x[0], loss_target[0], a_norm, a_w_in[0].astype(BF16), a_rel_bias[0], late_shards,
        kv_norm.reshape(1, D_MODEL), t5_bias, b_norm, b_sinks, final_norm.reshape(1, D_MODEL))

    shard_w = dict(a_w_in=a_w_in[0], b_w_in=b_w_in[0], a_w_out=a_w_out[0], b_w_out=b_w_out[0], kv_w=kv_w)
    shard_m = dict(a_w_in=m_a_w_in[0], b_w_in=m_b_w_in[0], a_w_out=m_a_w_out[0], b_w_out=m_b_w_out[0], kv_w=m_kv_w)
    shard_v = dict(a_w_in=v_a_w_in[0], b_w_in=v_b_w_in[0], a_w_out=v_a_w_out[0], b_w_out=v_b_w_out[0], kv_w=v_kv_w)
    big = {n: _reduce_adamw("adamw_" + n, own, partials, shard_w[n], shard_m[n], shard_v[n])
           for n, (own, partials) in matrices.items()}

    names = ("a_norm", "a_rel_bias", "kv_norm", "t5_bias", "b_norm", "b_sinks", "final_norm")
    tables = ("a_rel_bias", "t5_bias")

    def row(n, a):
        return a.reshape(-1, a.shape[-1]).T if n in tables else a.reshape(1, -1)

    small_w = [row(n, a) for n, a in zip(names, (a_norm, a_rel_bias, kv_norm, t5_bias, b_norm, b_sinks, final_norm))]
    small_m = [row(n, a) for n, a in zip(names, (m_a_norm, m_a_rel_bias, m_kv_norm, m_t5_bias, m_b_norm, m_b_sinks,
                                                 m_final_norm))]
    small_v = [row(n, a) for n, a in zip(names, (v_a_norm, v_a_rel_bias, v_kv_norm, v_t5_bias, v_b_norm, v_b_sinks,
                                                 v_final_norm))]
    sums = dict(loc)
    sums["a_rel_bias"] = _a_bias_grad(sums["a_rel_bias"])
    sums["t5_bias"] = sums["t5_bias"][:, :T5_BUCKETS]
    sums["b_sinks"] = sums["b_sinks"][:, 0].reshape(1, N_HEADS)
    results = _small_adamw(slot_arr, [sums[n] for n in names + ("loss",)], small_w, small_m, small_v)
    like = dict(a_norm=a_norm, a_rel_bias=a_rel_bias, kv_norm=kv_norm, t5_bias=t5_bias, b_norm=b_norm,
                b_sinks=b_sinks, final_norm=final_norm)
    sm = [{n: (part[i].T if n in tables else part[i]).reshape(like[n].shape) for i, n in enumerate(names)}
          for part in results]
    loss = results[0][len(names)][0, 0]

    order = ("a_norm", "a_w_in", "a_rel_bias", "a_w_out", "kv_norm", "kv_w", "t5_bias", "b_norm",
             "b_w_in", "b_sinks", "b_w_out", "final_norm")
    lead = dict(a_w_in=True, b_w_in=True, a_w_out=True, b_w_out=True, kv_w=False)

    def pick(kind, name):
        if name in big:
            val = big[name][kind]
            return val[None] if lead[name] else val
        return sm[kind][name]

    outs = [loss, grad_x[None]]
    for kind in range(4):
        outs += [pick(kind, n) for n in order]
    return tuple(outs)
```

```python
import functools
import math

import numpy as np
import jax
import jax.numpy as jnp
from jax import lax
from jax.experimental import pallas as pl
from jax.experimental.pallas import tpu as pltpu

F32 = jnp.float32
BF16 = jnp.bfloat16
SDS = jax.ShapeDtypeStruct

D_MODEL = 1024
HEAD_DIM = 64
CHUNK = 64
N_HEADS = 16
RMS_EPS = 1e-6
A_LEFT_CHUNKS = 8
A_BAND = (A_LEFT_CHUNKS + 1) * CHUNK
A_REL_CLIP = 256
B_KV_HEADS = 2
B_GROUP = 8
B_LEFT_CHUNKS = 2
B_BAND = (B_LEFT_CHUNKS + 1) * CHUNK
T5_BUCKETS = 32
T5_MAX_DIST = 128
QBLK = 256
A_KEYS = 3 * QBLK
B_QBLK_FWD = 128
B_QBLK_BWD = 256
B_PREV = 128
A_DIAG = A_KEYS
NEG = -1e30
SCALE = HEAD_DIM ** -0.5
N_DEV = 8

ADAM_LR = 0.001
ADAM_B1 = 0.9
ADAM_B2 = 0.999
ADAM_EPS = 1e-08
ADAM_WD = 0.01
ADAM_STEP = 10

VMEM_LIMIT_BYTES = 56 * 1024 * 1024
MESH = pl.DeviceIdType.MESH


def _cparams():
    return pltpu.CompilerParams(vmem_limit_bytes=VMEM_LIMIT_BYTES)


def _dot(a, b):
    return jnp.dot(a, b, preferred_element_type=F32)


def _dot_nt(a, b):
    return lax.dot_general(a, b, (((1,), (1,)), ((), ())), preferred_element_type=F32)


def _dot_tn(a, b):
    return lax.dot_general(a, b, (((0,), (0,)), ((), ())), preferred_element_type=F32)


def _rstd(xf):
    return lax.rsqrt(jnp.mean(xf * xf, axis=-1, keepdims=True) + RMS_EPS)


def _sigmoid(x):
    return 1.0 / (1.0 + jnp.exp(-x))


_GATHER_SEQUENCE = ((0, None), (1, 0), (2, 1), (4, None), (5, None), (3, 2), (6, None))


def _gather_order(x, y, c):
    others = [(1 - x, y), (x, 1 - y), (1 - x, 1 - y)]
    arrivals = [_slot(x, y, 1 - c)] + [_slot(*chip, c) for chip in others] + [_slot(*chip, 1 - c) for chip in others]
    return jnp.stack([_slot(x, y, c)] + [arrivals[a] for a, _ in _GATHER_SEQUENCE]).astype(jnp.int32)


def _norm_matmul_gather(order, x, gain_shard, w_shard):
    t = x.shape[0]
    dw, tn = w_shard.shape
    tm = min(t, 2048)
    n_m = t // tm

    def body(order_ref, x_ref, gs_ref, shard_ref, xn_ref, o_ref, full_ref, gain_ref,
             xn_all, wbuf, gland, send_sems, recv_sems, gsend_sems, grecv_sems, load_sems, own_sem):
        n, m = pl.program_id(0), pl.program_id(1)
        x_i, y_i, c_i, chips = _place()
        me, sibling = (x_i, y_i, c_i), (x_i, y_i, 1 - c_i)

        def send(k, block, to, src=None):
            dst = full_ref.at[_slot(*block)]
            return pltpu.make_async_remote_copy(
                src_ref=dst if src is None else src, dst_ref=dst,
                send_sem=send_sems.at[k], recv_sem=recv_sems.at[k], device_id=to, device_id_type=MESH)

        own = pltpu.make_async_copy(shard_ref, full_ref.at[_slot(*me)], own_sem)
        first = [send(0, me, sibling, src=shard_ref)]
        first += [send(1 + j, me, (*chip, c_i), src=shard_ref) for j, chip in enumerate(chips)]
        forwards = [send(4 + j, (*chip, c_i), sibling) for j, chip in enumerate(chips)]
        arrivals = [send(0, sibling, me)] + [send(1 + j, (*chip, c_i), me) for j, chip in enumerate(chips)]
        arrivals += [send(4 + j, (*chip, 1 - c_i), me) for j, chip in enumerate(chips)]
        gains = [pltpu.make_async_remote_copy(
            src_ref=gs_ref, dst_ref=gland.at[_slot(*me)], send_sem=gsend_sems.at[k - 1],
            recv_sem=grecv_sems.at[k - 1], device_id=_peer(x_i, y_i, c_i, k), device_id_type=MESH)
            for k in range(1, N_DEV)]

        @pl.when(jnp.logical_and(n == 0, m == 0))
        def _():
            own.start()
            for cp in gains + first:
                cp.start()
            pltpu.make_async_copy(shard_ref, wbuf.at[0], load_sems.at[0]).start()
            gland[_slot(*me)] = gs_ref[...]
            for k in range(1, N_DEV):
                pltpu.make_async_remote_copy(
                    src_ref=gs_ref, dst_ref=gland.at[_slot(*_peer(x_i, y_i, c_i, k))],
                    send_sem=gsend_sems.at[k - 1], recv_sem=grecv_sems.at[k - 1],
                    device_id=me, device_id_type=MESH).wait_recv()
            for s in range(N_DEV):
                gain_ref[:, 128 * s:128 * (s + 1)] = gland[s]

        rows = pl.ds(pl.multiple_of(m * tm, tm), tm)

        @pl.when(n == 0)
        def _():
            xf = x_ref[...]
            xn = ((xf * _rstd(xf)) * gain_ref[...]).astype(BF16)
            xn_all[rows, :] = xn
            xn_ref[...] = xn

        @pl.when(m == 0)
        def _():
            pltpu.make_async_copy(full_ref.at[0], wbuf.at[n % 2], load_sems.at[n % 2]).wait()

        o_ref[...] = _dot(xn_all[rows, :], wbuf[n % 2]).astype(BF16)

        for k, (arrival, forward) in enumerate(_GATHER_SEQUENCE):
            @pl.when(jnp.logical_and(n == k, m == n_m - 1))
            def _(k=k, arrival=arrival, forward=forward):
                arrivals[arrival].wait_recv()
                if forward is not None:
                    forwards[forward].start()
                pltpu.make_async_copy(full_ref.at[order_ref[k + 1]], wbuf.at[(k + 1) % 2],
                                      load_sems.at[(k + 1) % 2]).start()

        @pl.when(jnp.logical_and(n == N_DEV - 1, m == n_m - 1))
        def _():
            for cp in gains + first + forwards:
                cp.wait_send()
            own.wait()

    held = lambda n, m, order: (jnp.where(n == 0, m, n_m - 1), 0)
    return pl.pallas_call(
        body, name="norm_matmul_gather",
        grid_spec=pltpu.PrefetchScalarGridSpec(
            num_scalar_prefetch=1, grid=(N_DEV, n_m),
            in_specs=[pl.BlockSpec((tm, D_MODEL), held),
                      pl.BlockSpec((1, 128), lambda n, m, order: (0, 0)), ANY],
            out_specs=[pl.BlockSpec((tm, D_MODEL), held),
                       pl.BlockSpec((tm, tn), lambda n, m, order: (m, order[n])),
                       ANY, pl.BlockSpec((1, D_MODEL), lambda n, m, order: (0, 0))],
            scratch_shapes=[pltpu.VMEM((t, D_MODEL), BF16), pltpu.VMEM((2, dw, tn), BF16),
                            pltpu.VMEM((N_DEV, 1, 128), F32),
                            pltpu.SemaphoreType.DMA((7,)), pltpu.SemaphoreType.DMA((7,)),
                            pltpu.SemaphoreType.DMA((7,)), pltpu.SemaphoreType.DMA((7,)),
                            pltpu.SemaphoreType.DMA((2,)), pltpu.SemaphoreType.DMA]),
        out_shape=[SDS((t, D_MODEL), BF16), SDS((t, N_DEV * tn), BF16), SDS((N_DEV, dw, tn), BF16),
                   SDS((1, D_MODEL), F32)],
        compiler_params=_cparams(),
    )(order, x, gain_shard, w_shard)


def _layer_a_out(x, z, w_out, kv_gain, b_gain, kv_w, w_in_b):
    t = x.shape[0]
    tm = min(t, 1024)
    nb, _, tn = w_in_b.shape

    def body(x_ref, z_ref, wo_ref, kvg_ref, bg_ref, kvw_ref, wb_ref,
             h1_ref, kvn_ref, hb_ref, kv_ref, qg_ref):
        h1 = x_ref[...] + _dot(z_ref[...], wo_ref[...])
        h1_ref[...] = h1
        y0 = h1 * _rstd(h1)
        kvn = (y0 * kvg_ref[...]).astype(BF16)
        hb = (y0 * bg_ref[...]).astype(BF16)
        kvn_ref[...] = kvn
        hb_ref[...] = hb
        kv_ref[...] = _dot(kvn, kvw_ref[...]).astype(BF16)
        for i in range(nb):
            qg_ref[:, i * tn:(i + 1) * tn] = _dot(hb, wb_ref[i]).astype(BF16)

    row = lambda m: (m, 0)
    fix2 = lambda m: (0, 0)
    return pl.pallas_call(
        body, name="layer_a_out", grid=(t // tm,),
        in_specs=[pl.BlockSpec((tm, D_MODEL), row), pl.BlockSpec((tm, D_MODEL), row),
                  pl.BlockSpec((D_MODEL, D_MODEL), fix2),
                  pl.BlockSpec((1, D_MODEL), fix2), pl.BlockSpec((1, D_MODEL), fix2),
                  pl.BlockSpec((D_MODEL, 256), fix2),
                  pl.BlockSpec((nb, D_MODEL, tn), lambda m: (0, 0, 0))],
        out_specs=[pl.BlockSpec((tm, D_MODEL), row), pl.BlockSpec((tm, D_MODEL), row),
                   pl.BlockSpec((tm, D_MODEL), row), pl.BlockSpec((tm, 256), row),
                   pl.BlockSpec((tm, nb * tn), row)],
        out_shape=[SDS((t, D_MODEL), F32), SDS((t, D_MODEL), BF16), SDS((t, D_MODEL), BF16),
                   SDS((t, 256), BF16), SDS((t, nb * tn), BF16)],
        compiler_params=_cparams(),
    )(x, z, w_out, kv_gain, b_gain, kv_w, w_in_b)


def _layer_b_out_loss(h1, z, w_out, f_gain, target):
    t = h1.shape[0]
    tm = min(t, 1024)

    def body(h1_ref, z_ref, wo_ref, fg_ref, tgt_ref,
             dh2_ref, dh2b_ref, dz_ref, loss_ref, dfn_ref):
        @pl.when(pl.program_id(0) == 0)
        def _():
            loss_ref[...] = jnp.zeros_like(loss_ref)
            dfn_ref[...] = jnp.zeros_like(dfn_ref)

        h2 = h1_ref[...] + _dot(z_ref[...], wo_ref[...])
        r = _rstd(h2)
        yn = h2 * r
        fg = fg_ref[...]
        err = yn * fg - tgt_ref[...]
        loss_ref[...] += (0.5 / D_MODEL) * jnp.sum(err * err)
        dy = err * (1.0 / D_MODEL)
        dfn_ref[...] += jnp.sum(dy * yn, axis=0, keepdims=True)
        u = dy * fg
        dh2 = r * u - h2 * ((r * r * r) * jnp.mean(u * h2, axis=-1, keepdims=True))
        dh2_ref[...] = dh2
        dh2b = dh2.astype(BF16)
        dh2b_ref[...] = dh2b
        dz_ref[...] = _dot_nt(dh2b, wo_ref[...]).astype(BF16)

    row = lambda m: (m, 0)
    fix2 = lambda m: (0, 0)
    return pl.pallas_call(
        body, name="layer_b_out_loss", grid=(t // tm,),
        in_specs=[pl.BlockSpec((tm, D_MODEL), row), pl.BlockSpec((tm, D_MODEL), row),
                  pl.BlockSpec((D_MODEL, D_MODEL), fix2), pl.BlockSpec((1, D_MODEL), fix2),
                  pl.BlockSpec((tm, D_MODEL), row)],
        out_specs=[pl.BlockSpec((tm, D_MODEL), row), pl.BlockSpec((tm, D_MODEL), row),
                   pl.BlockSpec((tm, D_MODEL), row), pl.BlockSpec((1, 128), fix2),
                   pl.BlockSpec((1, D_MODEL), fix2)],
        out_shape=[SDS((t, D_MODEL), F32), SDS((t, D_MODEL), BF16), SDS((t, D_MODEL), BF16),
                   SDS((1, 128), F32), SDS((1, D_MODEL), F32)],
        compiler_params=_cparams(),
    )(h1, z, w_out, f_gain, target)


def _layer_b_in_bwd(dqg, dkv, w_in_b, kv_w, h1, dh2, b_gain, kv_gain, w_out_a):
    t = h1.shape[0]
    tm = min(t, 512)
    nb, _, tn = w_in_b.shape
    per = D_MODEL // tn

    def body(dqg_ref, dkv_ref, wb_ref, kvw_ref, h1_ref, dh2_ref, bg_ref, kvg_ref, wo_ref,
             dh1_ref, dh1b_ref, dz_ref, dbn_ref, dkn_ref):
        @pl.when(pl.program_id(0) == 0)
        def _():
            dbn_ref[...] = jnp.zeros_like(dbn_ref)
            dkn_ref[...] = jnp.zeros_like(dkn_ref)

        dhb = jnp.zeros((tm, D_MODEL), F32)
        for i in range(nb):
            blk = dqg_ref[i // per, :, (i % per) * tn:(i % per + 1) * tn]
            dhb = dhb + _dot_nt(blk, wb_ref[i])
        dkn = (_dot_nt(dkv_ref[0].astype(BF16), kvw_ref[:, 0:128])
               + _dot_nt(dkv_ref[1].astype(BF16), kvw_ref[:, 128:256]))
        h1 = h1_ref[...]
        r = _rstd(h1)
        xr = h1 * r
        dbn_ref[...] += jnp.sum(dhb * xr, axis=0, keepdims=True)
        dkn_ref[...] += jnp.sum(dkn * xr, axis=0, keepdims=True)
        u = dhb * bg_ref[...] + dkn * kvg_ref[...]
        dh1 = dh2_ref[...] + r * u - h1 * ((r * r * r) * jnp.mean(u * h1, axis=-1, keepdims=True))
        dh1_ref[...] = dh1
        dh1b = dh1.astype(BF16)
        dh1b_ref[...] = dh1b
        dz_ref[...] = _dot_nt(dh1b, wo_ref[...]).astype(BF16)

    row = lambda m: (m, 0)
    fix2 = lambda m: (0, 0)
    return pl.pallas_call(
        body, name="layer_b_in_bwd", grid=(t // tm,),
        in_specs=[pl.BlockSpec((2, tm, D_MODEL), lambda m: (0, m, 0)),
                  pl.BlockSpec((2, tm, 128), lambda m: (0, m, 0)),
                  pl.BlockSpec((nb, D_MODEL, tn), lambda m: (0, 0, 0)),
                  pl.BlockSpec((D_MODEL, 256), fix2),
                  pl.BlockSpec((tm, D_MODEL), row), pl.BlockSpec((tm, D_MODEL), row),
                  pl.BlockSpec((1, D_MODEL), fix2), pl.BlockSpec((1, D_MODEL), fix2),
                  pl.BlockSpec((D_MODEL, D_MODEL), fix2)],
        out_specs=[pl.BlockSpec((tm, D_MODEL), row), pl.BlockSpec((tm, D_MODEL), row),
                   pl.BlockSpec((tm, D_MODEL), row), pl.BlockSpec((1, D_MODEL), fix2),
                   pl.BlockSpec((1, D_MODEL), fix2)],
        out_shape=[SDS((t, D_MODEL), F32), SDS((t, D_MODEL), BF16), SDS((t, D_MODEL), BF16),
                   SDS((1, D_MODEL), F32), SDS((1, D_MODEL), F32)],
        compiler_params=_cparams(),
    )(dqg, dkv, w_in_b, kv_w, h1, dh2, b_gain, kv_gain, w_out_a)


def _layer_a_in_bwd(dqg, dkv, w_in_a, x, dh1, a_gain, chip_sums):
    t = x.shape[0]
    tm = min(t, 512)
    nb, _, tn = w_in_a.shape
    per = D_MODEL // tn
    n_sums = chip_sums.shape[0]

    def body(dqg_ref, dkv_ref, w_ref, x_ref, dh1_ref, ag_ref, sums_ref, dx_ref, dan_ref, land_ref,
             send_sems, recv_sems, dan_land, dan_send, dan_recv):
        @pl.when(pl.program_id(0) == 0)
        def _():
            dan_ref[...] = jnp.zeros_like(dan_ref)
            for cp in _later_chip_copies(sums_ref, land_ref, send_sems, recv_sems):
                cp.start()

        dxn = jnp.zeros((tm, D_MODEL), F32)
        for i in range(nb):
            part = i // per
            src = dqg_ref if part in (0, 3) else dkv_ref
            outer = {0: 0, 3: 1, 1: 0, 2: 1}[part]
            blk = src[outer, :, (i % per) * tn:(i % per + 1) * tn]
            dxn = dxn + _dot_nt(blk, w_ref[i])
        xf = x_ref[...]
        r = _rstd(xf)
        dan_ref[...] += jnp.sum(dxn * (xf * r), axis=0, keepdims=True)
        u = dxn * ag_ref[...]
        dx_ref[...] = dh1_ref[...] + r * u - xf * ((r * r * r) * jnp.mean(u * xf, axis=-1, keepdims=True))

        @pl.when(pl.program_id(0) == t // tm - 1)
        def _():
            _all_reduce_small([dan_ref], [dan_ref], [dan_land], dan_send, dan_recv)
            for cp in _later_chip_copies(sums_ref, land_ref, send_sems, recv_sems):
                cp.wait()

    row = lambda m: (m, 0)
    fix2 = lambda m: (0, 0)
    return pl.pallas_call(
        body, name="layer_a_in_bwd", grid=(t // tm,),
        in_specs=[pl.BlockSpec((2, tm, D_MODEL), lambda m: (0, m, 0)),
                  pl.BlockSpec((2, tm, D_MODEL), lambda m: (0, m, 0)),
                  pl.BlockSpec((nb, D_MODEL, tn), lambda m: (0, 0, 0)),
                  pl.BlockSpec((tm, D_MODEL), row), pl.BlockSpec((tm, D_MODEL), row),
                  pl.BlockSpec((1, D_MODEL), fix2), ANY],
        out_specs=[pl.BlockSpec((tm, D_MODEL), row), pl.BlockSpec((1, D_MODEL), fix2), ANY],
        out_shape=[SDS((t, D_MODEL), F32), SDS((1, D_MODEL), F32), SDS(chip_sums.shape, chip_sums.dtype)],
        scratch_shapes=[pltpu.SemaphoreType.DMA((n_sums,)), pltpu.SemaphoreType.DMA((n_sums,))]
        + _all_reduce_scratch([(1, D_MODEL)]),
        compiler_params=_cparams(),
    )(dqg, dkv, w_in_a, x, dh1, a_gain, chip_sums)


def _lut(s, vals):
    r = jnp.int32(vals[0])
    for i in range(1, len(vals)):
        r = jnp.where(s == i, jnp.int32(vals[i]), r)
    return r


def _held(steps, i):
    seq, cur = [None] * len(steps), None
    for k in range(len(steps) - 1, -1, -1):
        if steps[k][0] == i:
            cur = steps[k][1:3]
        seq[k] = cur
    for k in range(len(steps)):
        cur = seq[k] = seq[k] if seq[k] is not None else cur
    return seq


def _weight_grad_cols(name, my_slot, a, bs, steps, tn):
    t, dw = a.shape
    n_arr = len(bs)
    which = [s[0] for s in steps]
    blks = [s[3] for s in steps]

    def body(slot_ref, a_ref, *rest):
        b_refs, (o_ref, own_ref, at_ref) = rest[:n_arr], rest[n_arr:]
        s = pl.program_id(0)

        @pl.when(s == 0)
        def _():
            at_ref[...] = a_ref[...].T

        for i in range(n_arr):
            @pl.when(_lut(s, which) == i)
            def _(i=i):
                res = _dot(at_ref[...], b_refs[i][0])
                o_ref[0] = res.astype(BF16)

                @pl.when(_lut(s, blks) == slot_ref[0])
                def _():
                    own_ref[...] = res

    def b_spec(i):
        held = _held(steps, i)
        return pl.BlockSpec((1, t, tn), lambda s, slot: (_lut(s, [h[0] for h in held]), 0,
                                                         _lut(s, [h[1] for h in held])))

    return pl.pallas_call(
        body, name=name,
        grid_spec=pltpu.PrefetchScalarGridSpec(
            num_scalar_prefetch=1, grid=(len(steps),),
            in_specs=[pl.BlockSpec((t, dw), lambda s, slot: (0, 0))] + [b_spec(i) for i in range(n_arr)],
            out_specs=[pl.BlockSpec((1, dw, tn), lambda s, slot: (_lut(s, blks), 0, 0)),
                       pl.BlockSpec((dw, tn), lambda s, slot: (0, 0))],
            scratch_shapes=[pltpu.VMEM((dw, t), BF16)]),
        out_shape=[SDS((N_DEV, dw, tn), BF16), SDS((dw, tn), F32)],
        compiler_params=_cparams(),
    )(my_slot, a, *bs)


def _weight_grad_rows(name, my_slot, a, b):
    t, dw = a.shape
    n_o, _, c = b.shape
    rows = dw // N_DEV
    tn = min(c, 256)
    per = c // tn

    def body(slot_ref, a_ref, b_ref, o_ref, own_ref, at_ref, res_ref):
        @pl.when(pl.program_id(0) == 0)
        def _():
            at_ref[...] = a_ref[...].T

        res_ref[...] = _dot(at_ref[...], b_ref[0].astype(BF16))
        o_ref[...] = res_ref[...].astype(BF16)
        own_ref[...] = res_ref[pl.ds(pl.multiple_of(slot_ref[0] * rows, rows), rows), :]

    all_rows, own = pl.pallas_call(
        body, name=name,
        grid_spec=pltpu.PrefetchScalarGridSpec(
            num_scalar_prefetch=1, grid=(n_o * per,),
            in_specs=[pl.BlockSpec((t, dw), lambda s, slot: (0, 0)),
                      pl.BlockSpec((1, t, tn), lambda s, slot: (s // per, 0, s % per))],
            out_specs=[pl.BlockSpec((dw, tn), lambda s, slot: (0, s)),
                       pl.BlockSpec((rows, tn), lambda s, slot: (0, s))],
            scratch_shapes=[pltpu.VMEM((dw, t), BF16), pltpu.VMEM((dw, tn), F32)]),
        out_shape=[SDS((dw, n_o * c), BF16), SDS((rows, n_o * c), F32)],
        compiler_params=_cparams(),
    )(my_slot, a, b)
    return all_rows.reshape(N_DEV, rows, n_o * c), own


def _lane_lo():
    return lax.broadcasted_iota(jnp.int32, (1, 128), 1) < HEAD_DIM


def _collapse_chunks(ds, keys):
    if ds.shape[1] < keys:
        ds = jnp.concatenate([jnp.zeros((ds.shape[0], keys - ds.shape[1]), F32), ds], axis=1)
    gc = ds[0:CHUNK]
    for cc in range(1, ds.shape[0] // CHUNK):
        gc = gc + pltpu.roll(ds[cc * CHUNK:(cc + 1) * CHUNK], keys - cc * CHUNK, 1)
    return gc


def _offset_sums(gc):
    hi = gc.astype(BF16)
    lo = (gc - hi.astype(F32)).astype(BF16)
    flip = (lax.broadcasted_iota(jnp.int32, (CHUNK, CHUNK), 0)
            + lax.broadcasted_iota(jnp.int32, (CHUNK, CHUNK), 1) == CHUNK - 1).astype(BF16)
    gf = _dot(flip, hi) + _dot(flip, lo)
    skew = pltpu.roll(gf, 0, 1, stride=1, stride_axis=0)
    return jnp.sum(skew, axis=0, keepdims=True)


def _band_bias(w_row, band, rows):
    keys = w_row.shape[1]
    base = jnp.broadcast_to(w_row, (CHUNK, keys))
    skew = pltpu.roll(base, 0, 1, stride=1, stride_axis=0)
    skew = pltpu.roll(skew, keys - (CHUNK - 1), 1)
    col = lax.broadcasted_iota(jnp.int32, (CHUNK, keys), 1)
    chunk0 = jnp.where(col < band, skew, NEG)
    return jnp.concatenate(
        [chunk0] + [pltpu.roll(chunk0, cc * CHUNK, 1) for cc in range(1, rows // CHUNK)], axis=0)


def _silu_parts(g):
    sg = _sigmoid(g)
    return g * sg, sg * (1.0 + g * (1.0 - sg))


A_PAIRS_FWD = 8
A_PAIRS_BWD = 4


def _a_specs(pairs):
    lanes = 128 * pairs
    steps = D_MODEL // lanes
    q = pl.BlockSpec((QBLK, lanes), lambda p, j: (j, p))
    ks = [pl.BlockSpec((QBLK, lanes), lambda p, j, b=b: (jnp.maximum(j - 2 + b, 0), steps + p)) for b in range(3)]
    vs = [pl.BlockSpec((QBLK, lanes), lambda p, j, b=b: (jnp.maximum(j - 2 + b, 0), 2 * steps + p))
          for b in range(3)]
    g = pl.BlockSpec((QBLK, lanes), lambda p, j: (j, 3 * steps + p))
    bias = pl.BlockSpec((pairs, 8, A_KEYS), lambda p, j: (p, 0, 0))
    return q, ks, vs, g, bias


def _a_fill_bias(w_ref, b_ref, j, pairs):
    _fill_bias(2 * pairs, lambda h: w_ref[h // 2, h % 2:h % 2 + 1, :], A_BAND, b_ref, j)


def _by_valid_key_blocks(j, fn):
    pl.when(j == 0)(functools.partial(fn, 1))
    pl.when(j == 1)(functools.partial(fn, 2))
    pl.when(j >= 2)(functools.partial(fn, 3))


def _fill_bias(n, get_row, band, bias_scr, j):
    @pl.when(j == 0)
    def _():
        for h in range(n):
            bias_scr[h] = _band_bias(get_row(h), band, bias_scr.shape[1])


def _normalise_pair(rs, mxs, lane_lo, extra=None):
    num = jnp.where(lane_lo, rs[0], rs[1])
    den = pltpu.roll(jnp.where(lane_lo, rs[1], rs[0]), HEAD_DIM, 1)
    if extra is not None:
        den = den + jnp.where(lane_lo, extra[0], extra[1])
    return num / den, jnp.where(lane_lo, mxs[0], mxs[1]) + jnp.log(den)


def _own_everywhere(x, sel):
    return jnp.where(sel, x, pltpu.roll(x, HEAD_DIM, 1))


def _minus_rows(s, row_full):
    return jnp.concatenate([s[:, i:i + 128] - row_full for i in range(0, s.shape[1], 128)], axis=1)


def _attn_a_fwd(qkvg, bias, gather):
    t = qkvg.shape[0]
    nq = t // QBLK
    n_g = len(gather)
    pairs = A_PAIRS_FWD
    lanes = 128 * pairs
    steps = D_MODEL // lanes
    q_spec, k_specs, v_specs, g_spec, bias_spec = _a_specs(pairs)

    def body(q_ref, k0, k1, k2, v0, v1, v2, g_ref, w_ref, *rest):
        shard_refs, rest = rest[:n_g], rest[n_g:]
        z_ref, o_ref, lse_ref = rest[:3]
        full_refs, (b_ref, *comm) = rest[3:3 + n_g], rest[3 + n_g:]
        p = pl.program_id(0)
        j = pl.program_id(1)
        start, forward, finish = _gather_phases(shard_refs, full_refs, *comm)
        at = p * nq + j
        pl.when(at == 0)(start)
        pl.when(at == steps * nq // 2)(forward)
        _a_fill_bias(w_ref, b_ref, j, pairs)
        lane_lo = _lane_lo()
        sels = (lane_lo, jnp.logical_not(lane_lo))

        def attend(n_blocks):
            first_col = (3 - n_blocks) * QBLK
            for pp in range(pairs):
                cols = slice(128 * pp, 128 * (pp + 1))
                k = jnp.concatenate([r[:, cols] for r in (k0, k1, k2)[3 - n_blocks:]], axis=0)
                v = jnp.concatenate([r[:, cols] for r in (v0, v1, v2)[3 - n_blocks:]], axis=0)
                q = q_ref[:, cols]
                qm2 = jnp.concatenate([jnp.where(sel, q, jnp.zeros_like(q)) for sel in sels], axis=0) * SCALE
                s2 = _dot_nt(qm2, k)
                rs, mxs = [], []
                for hh, sel in enumerate(sels):
                    s = s2[hh * QBLK:(hh + 1) * QBLK] + b_ref[2 * pp + hh, :, first_col:]
                    mxs.append(jnp.max(s, axis=-1, keepdims=True))
                    e = jnp.exp(s - mxs[hh]).astype(BF16)
                    rs.append(_dot(e, jnp.where(sel, v, jnp.ones_like(v))))
                o, lse = _normalise_pair(rs, mxs, lane_lo)
                silu, _ = _silu_parts(g_ref[:, cols].astype(F32))
                o_ref[:, cols] = o.astype(BF16)
                z_ref[:, cols] = (o * silu).astype(BF16)
                lse_ref[:, cols] = lse

        _by_valid_key_blocks(j, attend)
        pl.when(at == steps * nq - 1)(finish)

    out_spec = pl.BlockSpec((QBLK, lanes), lambda p, j: (j, p))
    outs = pl.pallas_call(
        body, name="attn_a_fwd", grid=(steps, nq),
        in_specs=[q_spec, *k_specs, *v_specs, g_spec, bias_spec] + [ANY] * n_g,
        out_specs=[out_spec, out_spec, out_spec] + [ANY] * n_g,
        out_shape=[SDS((t, D_MODEL), BF16), SDS((t, D_MODEL), BF16), SDS((t, D_MODEL), F32)]
        + [SDS((N_DEV, *s.shape), s.dtype) for s in gather],
        scratch_shapes=[pltpu.VMEM((2 * pairs, QBLK, A_KEYS), F32)] + _gather_scratch(n_g),
        compiler_params=_cparams(),
    )(qkvg, qkvg, qkvg, qkvg, qkvg, qkvg, qkvg, qkvg, bias, *gather)
    return outs[0], outs[1], outs[2], list(outs[3:])


def _attn_a_bwd(qkvg, bias, out_a, lse, dz, scatter):
    t = qkvg.shape[0]
    nq = t // QBLK
    n_sc = len(scatter)
    pairs = A_PAIRS_BWD
    lanes = 128 * pairs
    steps = D_MODEL // lanes
    q_spec, k_specs, v_specs, g_spec, bias_spec = _a_specs(pairs)

    def body(q_ref, k0, k1, k2, v0, v1, v2, g_ref, w_ref, o_ref, lse_ref, dz_ref, *rest):
        sc_refs, rest = rest[:n_sc], rest[n_sc:]
        dqg_ref, dkv_ref, dg_ref = rest[:3]
        land_refs, rest = rest[3:3 + n_sc], rest[3 + n_sc:]
        dk_acc, dv_acc, gt_acc, b_ref, send_sems, recv_sems = rest
        j = pl.program_id(1)
        first = jnp.logical_and(pl.program_id(0) == 0, j == 0)
        last = jnp.logical_and(pl.program_id(0) == steps - 1, j == nq - 1)

        @pl.when(first)
        def _():
            for cp in _scatter_copies(sc_refs, land_refs, send_sems, recv_sems):
                cp.start()

        _a_fill_bias(w_ref, b_ref, j, pairs)

        @pl.when(j == 0)
        def _():
            dk_acc[...] = jnp.zeros_like(dk_acc)
            dv_acc[...] = jnp.zeros_like(dv_acc)
            gt_acc[...] = jnp.zeros_like(gt_acc)

        lane_lo = _lane_lo()
        sels = (lane_lo, jnp.logical_not(lane_lo))

        def attend(n_blocks):
            first_col = (3 - n_blocks) * QBLK
            for pp in range(pairs):
                cols = slice(128 * pp, 128 * (pp + 1))
                q = q_ref[:, cols]
                k = jnp.concatenate([r[:, cols] for r in (k0, k1, k2)[3 - n_blocks:]], axis=0)
                v = jnp.concatenate([r[:, cols] for r in (v0, v1, v2)[3 - n_blocks:]], axis=0)
                o = o_ref[:, cols].astype(F32)
                lse_pair = lse_ref[:, cols]
                dzf = dz_ref[:, cols].astype(F32)
                silu, dsilu = _silu_parts(g_ref[:, cols].astype(F32))
                do = dzf * silu
                dqg_ref[1, :, cols] = (dzf * o * dsilu).astype(BF16)
                doo = do * o
                qm2 = jnp.concatenate([jnp.where(sel, q, jnp.zeros_like(q)) for sel in sels], axis=0) * SCALE
                dom2 = jnp.concatenate([jnp.where(sel, do, 0.0) for sel in sels], axis=0).astype(BF16)
                s2 = _dot_nt(qm2, k)
                dp2 = _dot_nt(dom2, v)
                ps, dss = [], []
                for hh, sel in enumerate(sels):
                    rows = slice(hh * QBLK, (hh + 1) * QBLK)
                    s = s2[rows] + b_ref[2 * pp + hh, :, first_col:]
                    p = jnp.exp(_minus_rows(s, _own_everywhere(lse_pair, sel)))
                    delta = jnp.sum(jnp.where(sel, doo, 0.0), axis=-1, keepdims=True)
                    ds = p * (dp2[rows] - delta)
                    gt_acc[2 * pp + hh] += _collapse_chunks(ds, A_KEYS)
                    ps.append(p.astype(BF16))
                    dss.append(ds.astype(BF16))
                dsb2 = jnp.concatenate(dss, axis=0)
                dq2 = _dot(dsb2, k) * SCALE
                dk_blk = _dot_tn(dsb2, qm2)
                dv_blk = _dot_tn(jnp.concatenate(ps, axis=0), dom2)
                dqg_ref[0, :, cols] = jnp.where(lane_lo, dq2[0:QBLK], dq2[QBLK:2 * QBLK]).astype(BF16)
                for b in range(n_blocks):
                    rows = pl.ds(pl.multiple_of((j - n_blocks + 1 + b) * QBLK, QBLK), QBLK)
                    dk_acc[rows, cols] += dk_blk[b * QBLK:(b + 1) * QBLK]
                    dv_acc[rows, cols] += dv_blk[b * QBLK:(b + 1) * QBLK]

        _by_valid_key_blocks(j, attend)

        @pl.when(j == nq - 1)
        def _():
            dkv_ref[0] = dk_acc[...].astype(BF16)
            dkv_ref[1] = dv_acc[...].astype(BF16)
            for pp in range(pairs):
                dg_ref[pp] = jnp.concatenate([_offset_sums(gt_acc[2 * pp]), _offset_sums(gt_acc[2 * pp + 1]),
                                              jnp.zeros((6, A_DIAG), F32)], axis=0)

        @pl.when(last)
        def _():
            for cp in _scatter_copies(sc_refs, land_refs, send_sems, recv_sems):
                cp.wait()

    blk = pl.BlockSpec((QBLK, lanes), lambda p, j: (j, p))
    outs = pl.pallas_call(
        body, name="attn_a_bwd", grid=(steps, nq),
        in_specs=[q_spec, *k_specs, *v_specs, g_spec, bias_spec, blk, blk, blk] + [ANY] * n_sc,
        out_specs=[pl.BlockSpec((2, QBLK, lanes), lambda p, j: (0, j, p)),
                   pl.BlockSpec((2, t, lanes), lambda p, j: (0, 0, p)),
                   pl.BlockSpec((pairs, 8, A_DIAG), lambda p, j: (p, 0, 0))] + [ANY] * n_sc,
        out_shape=[SDS((2, t, D_MODEL), BF16), SDS((2, t, D_MODEL), BF16), SDS((N_HEADS // 2, 8, A_DIAG), F32)]
        + [SDS((N_DEV - 1, *g.shape[1:]), g.dtype) for g in scatter],
        scratch_shapes=[pltpu.VMEM((t, lanes), F32), pltpu.VMEM((t, lanes), F32),
                        pltpu.VMEM((2 * pairs, CHUNK, A_KEYS), F32), pltpu.VMEM((2 * pairs, QBLK, A_KEYS), F32),
                        pltpu.SemaphoreType.DMA(((N_DEV - 1) * n_sc,)),
                        pltpu.SemaphoreType.DMA(((N_DEV - 1) * n_sc,))],
        compiler_params=_cparams(),
    )(qkvg, qkvg, qkvg, qkvg, qkvg, qkvg, qkvg, qkvg, bias, out_a, lse, dz, *scatter)
    return outs[0], outs[1], outs[2], list(outs[3:])


def _b_specs(qblk):
    per = qblk // B_PREV
    q = pl.BlockSpec((qblk, 512), lambda h, j: (j, h))
    g = pl.BlockSpec((qblk, 512), lambda h, j: (j, 2 + h))
    kp = pl.BlockSpec((B_PREV, 128), lambda h, j: (jnp.maximum(per * j - 1, 0), 0))
    kc = pl.BlockSpec((qblk, 128), lambda h, j: (j, 0))
    vp = pl.BlockSpec((B_PREV, 128), lambda h, j: (jnp.maximum(per * j - 1, 0), 1))
    vc = pl.BlockSpec((qblk, 128), lambda h, j: (j, 1))
    bias = pl.BlockSpec((B_GROUP, qblk + B_PREV), lambda h, j: (h, 0))
    sinks = pl.BlockSpec(memory_space=pltpu.SMEM)
    return q, g, kp, kc, vp, vc, bias, sinks


def _b_operands(kp, kc, vp, vc, kvh, with_prev):
    k = jnp.concatenate([kp[...], kc[...]], axis=0) if with_prev else kc[...]
    v = jnp.concatenate([vp[...], vc[...]], axis=0) if with_prev else vc[...]
    kr = pltpu.roll(k, HEAD_DIM, 1)
    vr = pltpu.roll(v, HEAD_DIM, 1)
    first = kvh == 0
    return (jnp.where(first, k, kr), jnp.where(first, kr, k),
            jnp.where(first, v, vr), jnp.where(first, vr, v))


def _attn_b_fwd(qg, kv, bias, sinks):
    t = qg.shape[0]
    qblk = B_QBLK_FWD
    per_step = 4
    step = per_step * qblk
    q_spec, g_spec, kp_spec, kc_spec, vp_spec, vc_spec, _, sink_spec = _b_specs(step)
    bias_spec = pl.BlockSpec((B_GROUP, qblk + B_PREV), lambda h, j: (h, 0))

    def body(q_ref, g_ref, kp, kc, vp, vc, w_ref, sink_ref, z_ref, o_ref, lse_ref, b_ref):
        kvh = pl.program_id(0)
        j = pl.program_id(1)
        _fill_bias(B_GROUP, lambda h: w_ref[h:h + 1, :], B_BAND, b_ref, j)
        lane_lo = _lane_lo()
        n_pairs = B_GROUP // 2

        def attend(first):
            k_lo, k_hi, v_lo, v_hi = _b_operands(kp, kc, vp, vc, kvh, True)
            for sb in range(per_step):
                no_prev = first and sb == 0
                first_col = B_PREV if no_prev else 0
                keys = slice(sb * qblk + first_col, (sb + 1) * qblk + B_PREV)
                qrows = slice(sb * qblk, (sb + 1) * qblk)
                halves = []
                for hh, sel in enumerate((lane_lo, jnp.logical_not(lane_lo))):
                    kk = (k_lo if hh == 0 else k_hi)[keys]
                    vv = (v_lo if hh == 0 else v_hi)[keys]
                    qm4 = jnp.concatenate(
                        [jnp.where(sel, q_ref[qrows, 128 * pp:128 * (pp + 1)], jnp.zeros((qblk, 128), BF16))
                         for pp in range(n_pairs)], axis=0) * SCALE
                    s4 = _dot_nt(qm4, kk)
                    es, mxs = [], []
                    for pp in range(n_pairs):
                        g = 2 * pp + hh
                        s = s4[pp * qblk:(pp + 1) * qblk] + b_ref[g, :, first_col:]
                        mxs.append(jnp.maximum(jnp.max(s, axis=-1, keepdims=True), sink_ref[kvh * B_GROUP + g]))
                        es.append(jnp.exp(s - mxs[pp]).astype(BF16))
                    r4 = _dot(jnp.concatenate(es, axis=0), jnp.where(sel, vv, jnp.ones_like(vv)))
                    halves.append((r4, mxs))
                for pp in range(n_pairs):
                    cols = slice(128 * pp, 128 * (pp + 1))
                    rows = slice(pp * qblk, (pp + 1) * qblk)
                    mxs = [halves[hh][1][pp] for hh in range(2)]
                    sink_terms = [jnp.exp(sink_ref[kvh * B_GROUP + 2 * pp + hh] - mxs[hh]) for hh in range(2)]
                    o, lse = _normalise_pair([halves[hh][0][rows] for hh in range(2)], mxs, lane_lo, sink_terms)
                    silu, _ = _silu_parts(g_ref[qrows, cols].astype(F32))
                    o_ref[qrows, cols] = o.astype(BF16)
                    z_ref[qrows, cols] = (o * silu).astype(BF16)
                    lse_ref[qrows, cols] = lse

        pl.when(j == 0)(functools.partial(attend, True))
        pl.when(j >= 1)(functools.partial(attend, False))

    out_spec = pl.BlockSpec((step, 512), lambda h, j: (j, h))
    return pl.pallas_call(
        body, name="attn_b_fwd", grid=(B_KV_HEADS, t // step),
        in_specs=[q_spec, g_spec, kp_spec, kc_spec, vp_spec, vc_spec, bias_spec, sink_spec],
        out_specs=[out_spec, out_spec, out_spec],
        out_shape=[SDS((t, D_MODEL), BF16), SDS((t, D_MODEL), BF16), SDS((t, D_MODEL), F32)],
        scratch_shapes=[pltpu.VMEM((B_GROUP, qblk, qblk + B_PREV), F32)],
        compiler_params=_cparams(),
    )(qg, qg, kv, kv, kv, kv, bias, sinks)


def _attn_b_bwd(qg, kv, bias, sinks, out_b, lse, dz, bucket_onehot):
    t = qg.shape[0]
    qblk = B_QBLK_BWD
    keys = qblk + B_PREV
    nq = t // qblk
    per = qblk // B_PREV

    def body(q_ref, g_ref, kp, kc, vp, vc, w_ref, sink_ref, o_ref, lse_ref, dz_ref, oh_ref,
             dqg_ref, dkv_ref, dt5_ref, dsink_ref, gt_acc, b_ref):
        j = pl.program_id(0)
        _fill_bias(N_HEADS, lambda h: w_ref[h:h + 1, :], B_BAND, b_ref, j)

        @pl.when(j == 0)
        def _():
            dkv_ref[...] = jnp.zeros_like(dkv_ref)
            gt_acc[...] = jnp.zeros_like(gt_acc)
            dsink_ref[...] = jnp.zeros_like(dsink_ref)

        lane_lo = _lane_lo()

        def attend(with_prev):
            first_col = 0 if with_prev else B_PREV
            dk_add = jnp.zeros((keys - first_col, 128), F32)
            dv_add = jnp.zeros((keys - first_col, 128), F32)
            for kvh in range(B_KV_HEADS):
                k_lo, k_hi, v_lo, v_hi = _b_operands(kp, kc, vp, vc, kvh, with_prev)
                dk_blk = jnp.zeros((keys - first_col, 128), F32)
                dv_blk = jnp.zeros((keys - first_col, 128), F32)
                for pp in range(B_GROUP // 2):
                    cols = slice(512 * kvh + 128 * pp, 512 * kvh + 128 * (pp + 1))
                    qp = q_ref[:, cols]
                    o = o_ref[:, cols].astype(F32)
                    lse_pair = lse_ref[:, cols]
                    dzf = dz_ref[:, cols].astype(F32)
                    silu, dsilu = _silu_parts(g_ref[:, cols].astype(F32))
                    do = dzf * silu
                    dqg_ref[1, :, cols] = (dzf * o * dsilu).astype(BF16)
                    doo = do * o
                    dqs = []
                    for hh in range(2):
                        g = kvh * B_GROUP + 2 * pp + hh
                        sel = lane_lo if hh == 0 else jnp.logical_not(lane_lo)
                        kk = k_lo if hh == 0 else k_hi
                        vv = v_lo if hh == 0 else v_hi
                        qm = jnp.where(sel, qp, jnp.zeros_like(qp)) * SCALE
                        s = _dot_nt(qm, kk) + b_ref[g, :, first_col:]
                        lse_h = _own_everywhere(lse_pair, sel)
                        p = jnp.exp(_minus_rows(s, lse_h))
                        delta = jnp.sum(jnp.where(sel, doo, 0.0), axis=-1, keepdims=True)
                        dom = jnp.where(sel, do, 0.0).astype(BF16)
                        dp = _dot_nt(dom, vv)
                        ds = p * (dp - delta)
                        gt_acc[g, :, first_col:] += ds
                        dsink_ref[g:g + 1, :] -= jnp.sum(jnp.exp(sink_ref[g] - lse_h) * delta, axis=0, keepdims=True)
                        dsb = ds.astype(BF16)
                        dqs.append(_dot(dsb, kk) * SCALE)
                        dk_blk = dk_blk + _dot_tn(dsb, qm)
                        dv_blk = dv_blk + _dot_tn(p.astype(BF16), dom)
                    dqg_ref[0, :, cols] = jnp.where(lane_lo, dqs[0], dqs[1]).astype(BF16)
                mine = lane_lo if kvh == 0 else jnp.logical_not(lane_lo)
                dk_add = dk_add + jnp.where(mine, dk_blk + pltpu.roll(dk_blk, HEAD_DIM, 1), 0.0)
                dv_add = dv_add + jnp.where(mine, dv_blk + pltpu.roll(dv_blk, HEAD_DIM, 1), 0.0)
            first_key = B_PREV if with_prev else 0
            if with_prev:
                rows = pl.ds(pl.multiple_of(j * qblk - B_PREV, B_PREV), B_PREV)
                dkv_ref[0, rows, :] += dk_add[0:B_PREV]
                dkv_ref[1, rows, :] += dv_add[0:B_PREV]
            rows = pl.ds(pl.multiple_of(j * qblk, qblk), qblk)
            dkv_ref[0, rows, :] += dk_add[first_key:first_key + qblk]
            dkv_ref[1, rows, :] += dv_add[first_key:first_key + qblk]

        pl.when(j == 0)(functools.partial(attend, False))
        pl.when(j >= 1)(functools.partial(attend, True))

        @pl.when(j == nq - 1)
        def _():
            dd = jnp.concatenate([_offset_sums(_collapse_chunks(gt_acc[g], keys)) for g in range(N_HEADS)], axis=0)
            hi = dd.astype(BF16)
            lo = (dd - hi.astype(F32)).astype(BF16)
            dt5_ref[...] = _dot(hi, oh_ref[...]) + _dot(lo, oh_ref[...])

    wide = lambda col: pl.BlockSpec((qblk, D_MODEL), lambda j, col=col: (j, col))
    prev = lambda col: pl.BlockSpec((B_PREV, 128), lambda j, col=col: (jnp.maximum(per * j - 1, 0), col))
    cur = lambda col: pl.BlockSpec((qblk, 128), lambda j, col=col: (j, col))
    fixed = lambda shape: pl.BlockSpec(shape, lambda j: (0,) * len(shape))
    return pl.pallas_call(
        body, name="attn_b_bwd", grid=(nq,),
        in_specs=[wide(0), wide(1), prev(0), cur(0), prev(1), cur(1), fixed((N_HEADS, keys)),
                  pl.BlockSpec(memory_space=pltpu.SMEM), wide(0), wide(0), wide(0), fixed((keys, 128))],
        out_specs=[pl.BlockSpec((2, qblk, D_MODEL), lambda j: (0, j, 0)), fixed((2, t, 128)),
                   fixed((N_HEADS, 128)), fixed((N_HEADS, 128))],
        out_shape=[SDS((2, t, D_MODEL), BF16), SDS((2, t, 128), F32),
                   SDS((N_HEADS, 128), F32), SDS((N_HEADS, 128), F32)],
        scratch_shapes=[pltpu.VMEM((N_HEADS, qblk, keys), F32), pltpu.VMEM((N_HEADS, qblk, keys), F32)],
        compiler_params=_cparams(),
    )(qg, qg, kv, kv, kv, kv, bias, sinks, out_b, lse, dz, bucket_onehot)


def _a_bias_by_offset(rel_bias):
    m = np.arange(A_DIAG)
    idx = np.clip(A_BAND - 1 - m, -A_REL_CLIP, A_REL_CLIP) + A_REL_CLIP
    by_head = rel_bias[idx].T.reshape(N_HEADS // 2, 2, A_DIAG)
    return jnp.concatenate([by_head, jnp.zeros((N_HEADS // 2, 6, A_DIAG), F32)], axis=1)


def _a_bias_grad(offset_sums):
    first = 319
    tail = jnp.sum(offset_sums[:, :first], axis=1)
    body = jnp.flip(offset_sums[:, first:first + 320], axis=1)
    body = body.at[:, -1].add(tail)
    full = jnp.concatenate([jnp.zeros((N_HEADS, 193), F32), body], axis=1)
    return full


def _t5_bucket(rel):
    nb = T5_BUCKETS // 2
    max_exact = nb // 2
    ret = jnp.where(rel > 0, nb, 0)
    n = jnp.abs(rel)
    nf = jnp.maximum(n, 1).astype(jnp.float32)
    large = max_exact + (jnp.log(nf / max_exact) / math.log(T5_MAX_DIST / max_exact)
                         * (nb - max_exact)).astype(jnp.int32)
    large = jnp.minimum(large, nb - 1)
    return ret + jnp.where(n < max_exact, n, large)


def _b_offset_buckets(keys):
    return _t5_bucket(jnp.arange(keys, dtype=jnp.int32) - (B_LEFT_CHUNKS * CHUNK + CHUNK - 1))


def _b_bias_by_offset(t5_table, keys):
    return t5_table[_b_offset_buckets(keys)].T


def _b_bucket_onehot(keys):
    return (_b_offset_buckets(keys)[:, None] == jnp.arange(128)[None, :]).astype(BF16)


def _local_step(my_slot, order, x, target, a_gain_shard, w_in_a_shard, rel_bias, late_shards, kv_gain,
                t5_table, b_gain, sinks, f_gain):
    a_bias = _a_bias_by_offset(rel_bias)
    b_bias_fwd = _b_bias_by_offset(t5_table, B_QBLK_FWD + B_PREV)
    b_bias_bwd = _b_bias_by_offset(t5_table, B_QBLK_BWD + B_PREV)
    sinks_flat = sinks.reshape(N_HEADS)

    xn, qkvg, w_in_a, a_gain = _norm_matmul_gather(order, x, a_gain_shard, w_in_a_shard)
    z_a, out_a, lse_a, (w_in_b, w_out_a, w_out_b, kv_w) = _attn_a_fwd(qkvg, a_bias, late_shards)
    w_out_a = w_out_a.reshape(D_MODEL, D_MODEL)
    w_out_b = w_out_b.reshape(D_MODEL, D_MODEL)
    kv_w = kv_w.reshape(D_MODEL, 2 * 128)
    h1, kvn, hb, kv, qg = _layer_a_out(x, z_a, w_out_a, kv_gain, b_gain, kv_w, w_in_b)
    z_b, out_b, lse_b = _attn_b_fwd(qg, kv, b_bias_fwd, sinks_flat)
    dh2, dh2b, dz_b, loss, d_fn = _layer_b_out_loss(h1, z_b, w_out_b, f_gain, target)

    dqg_b, dkv_b, d_t5, d_sink = _attn_b_bwd(qg, kv, b_bias_bwd, sinks_flat, out_b, lse_b, dz_b,
                                             _b_bucket_onehot(B_QBLK_BWD + B_PREV))
    dh1, dh1b, dz_a, d_bn, d_kn = _layer_b_in_bwd(dqg_b, dkv_b, w_in_b, kv_w, h1, dh2, b_gain, kv_gain, w_out_a)
    early = dict(
        b_w_out=_weight_grad_rows("grad_b_w_out", my_slot, z_b, dh2b[None]),
        b_w_in=_weight_grad_cols("grad_b_w_in", my_slot, hb, [dqg_b],
                                 [(0, o, c, 4 * o + c) for o in range(2) for c in range(4)], 256),
        kv_w=_weight_grad_rows("grad_kv_w", my_slot, kvn, dkv_b),
        a_w_out=_weight_grad_rows("grad_a_w_out", my_slot, z_a, dh1b[None]))
    dqg_a, dkv_a, d_rel, landed = _attn_a_bwd(qkvg, a_bias, out_a, lse_a, dz_a, [g[0] for g in early.values()])
    ready = dict(
        loss=loss, a_rel_bias=d_rel[:, :2].reshape(N_HEADS, A_DIAG),
        kv_norm=d_kn, t5_bias=d_t5, b_norm=d_bn, b_sinks=d_sink, final_norm=d_fn)
    g_own, from_sibling, from_far, chip_sums, ready_sums = _grad_a_w_in_reduce(
        _a_w_in_grad_order(), xn, dqg_a, dkv_a, list(ready.values()))
    grad_x, d_an, from_near = _layer_a_in_bwd(dqg_a, dkv_a, w_in_a, x, dh1, a_gain, chip_sums)

    matrices = {n: (g[1], [(land, 0, N_DEV - 1)]) for (n, g), land in zip(early.items(), landed)}
    matrices["a_w_in"] = (g_own, [(from_sibling, 0, 1), (from_far, 0, from_far.shape[0]),
                                  (from_near, 0, from_near.shape[0])])
    small = dict(zip(ready.keys(), ready_sums), a_norm=d_an)
    return grad_x, small, matrices


def _place():
    x, y, c = lax.axis_index("x"), lax.axis_index("y"), lax.axis_index("c")
    chips = [(1 - x, y), (x, 1 - y), (1 - x, 1 - y)]
    return x, y, c, chips


def _slot(px, py, pc):
    return 4 * px + 2 * py + pc


ANY = pl.BlockSpec(memory_space=pl.ANY)


def _peer(x, y, c, k):
    return (x ^ (k >> 2), y ^ ((k >> 1) & 1), c ^ (k & 1))


def _scatter_copies(grad_refs, land_refs, send_sems, recv_sems):
    x, y, c, _ = _place()
    copies = []
    for t, (grad, land) in enumerate(zip(grad_refs, land_refs)):
        for k in range(1, N_DEV):
            peer = _peer(x, y, c, k)
            sem = (N_DEV - 1) * t + k - 1
            copies.append(pltpu.make_async_remote_copy(
                src_ref=grad.at[_slot(*peer)], dst_ref=land.at[k - 1],
                send_sem=send_sems.at[sem], recv_sem=recv_sems.at[sem],
                device_id=peer, device_id_type=MESH))
    return copies


def _gather_phases(ins, outs, send_sems, recv_sems, local_sems):
    n = len(ins)
    x, y, c, chips = _place()
    me, sibling = (x, y, c), (x, y, 1 - c)

    def copy(t, k, block, to, src=None):
        dst = outs[t].at[_slot(*block)]
        return pltpu.make_async_remote_copy(
            src_ref=dst if src is None else src, dst_ref=dst,
            send_sem=send_sems.at[7 * t + k], recv_sem=recv_sems.at[7 * t + k],
            device_id=to, device_id_type=MESH)

    def lists():
        mine = [pltpu.make_async_copy(ins[t], outs[t].at[_slot(*me)], local_sems.at[t]) for t in range(n)]
        first = []
        for t in range(n):
            first.append(copy(t, 0, me, sibling, src=ins[t]))
            first += [copy(t, 1 + j, me, (*chip, c), src=ins[t]) for j, chip in enumerate(chips)]
        passed = [copy(t, 4 + j, (*chip, c), sibling) for t in range(n) for j, chip in enumerate(chips)]
        return mine, first, passed

    def start():
        mine, first, _ = lists()
        for cp in mine + first:
            cp.start()

    def forward():
        _, _, passed = lists()
        for t in range(n):
            for j, chip in enumerate(chips):
                copy(t, 1 + j, (*chip, c), me).wait_recv()
                passed[3 * t + j].start()

    def finish():
        mine, first, passed = lists()
        for t in range(n):
            copy(t, 0, sibling, me).wait_recv()
            for j, chip in enumerate(chips):
                copy(t, 4 + j, (*chip, 1 - c), me).wait_recv()
        for cp in first + passed:
            cp.wait_send()
        for cp in mine:
            cp.wait()

    return start, forward, finish


def _gather_scratch(n):
    return [pltpu.SemaphoreType.DMA((7 * n,)), pltpu.SemaphoreType.DMA((7 * n,)), pltpu.SemaphoreType.DMA((n,))]


_FAR_CHIP_FIRST = (2, 0, 1)
_SUMS_SENT_AT_ONCE = 1


def _a_w_in_grad_order():
    x, y, c, chips = _place()
    slots = []
    for j in _FAR_CHIP_FIRST:
        slots += [_slot(*chips[j], 1 - c), _slot(*chips[j], c)]
    slots += [_slot(x, y, 1 - c), _slot(x, y, c)]
    return jnp.stack(slots).astype(jnp.int32)


def _grad_a_w_in_reduce(order, a, dqg, dkv, small):
    t, dw = a.shape
    tn = dqg.shape[2] // 2
    n_s = len(small)
    n_far, n_sent = len(_FAR_CHIP_FIRST), _SUMS_SENT_AT_ONCE

    def body(order_ref, a_ref, dqg_ref, dkv_ref, *rest):
        small_refs, rest = rest[:n_s], rest[n_s:]
        own_ref, sib_ref, chips_ref, later_ref = rest[:4]
        small_out, rest = rest[4:4 + n_s], rest[4 + n_s:]
        a_buf, at_ref, res_ref, stage, land, load_sem, d2d_send, d2d_recv, ici_send, ici_recv = rest[:10]
        small_lands, (small_send, small_recv) = rest[10:10 + n_s], rest[10 + n_s:]
        s = pl.program_id(0)
        x, y, c, chips = _place()

        def to_sibling(i):
            return pltpu.make_async_remote_copy(
                src_ref=stage.at[i], dst_ref=land.at[i] if i < n_far else sib_ref.at[0],
                send_sem=d2d_send.at[i], recv_sem=d2d_recv.at[i], device_id=(x, y, 1 - c), device_id_type=MESH)

        def to_chip(i):
            return pltpu.make_async_remote_copy(
                src_ref=land.at[i], dst_ref=chips_ref.at[i], send_sem=ici_send.at[i], recv_sem=ici_recv.at[i],
                device_id=(*chips[_FAR_CHIP_FIRST[i]], c), device_id_type=MESH)

        @pl.when(s == 0)
        def _():
            load = pltpu.make_async_copy(a_ref, a_buf, load_sem)
            load.start()
            _all_reduce_small_start(small_refs, small_lands, small_send, small_recv)
            load.wait()
            at_ref[...] = a_buf[...].T

        blk = order_ref[s]
        from_qg = jnp.logical_or(blk < 2, blk >= 6)

        @pl.when(from_qg)
        def _():
            res_ref[...] = _dot(at_ref[...], dqg_ref[0])

        @pl.when(jnp.logical_not(from_qg))
        def _():
            res_ref[...] = _dot(at_ref[...], dkv_ref[0])

        for i in range(n_far + 1):
            @pl.when(s == 2 * i)
            def _(i=i):
                stage[i] = res_ref[...].astype(BF16)
                to_sibling(i).start()

        for i in range(n_far):
            @pl.when(s == 2 * i + 1)
            def _(i=i):
                to_sibling(i).wait_recv()
                total = (res_ref[...] + land[i].astype(F32)).astype(BF16)
                if i < n_sent:
                    land[i] = total
                    to_chip(i).start()
                else:
                    later_ref[i - n_sent] = total

        @pl.when(s == N_DEV - 1)
        def _():
            own_ref[...] = res_ref[...]
            _all_reduce_small_finish(small_refs, small_out, small_lands, small_send, small_recv)
            for i in range(n_far + 1):
                to_sibling(i).wait_send()
            to_sibling(n_far).wait_recv()
            for i in range(n_sent):
                to_chip(i).wait()

    whole = pl.BlockSpec(memory_space=pltpu.VMEM)
    outs = pl.pallas_call(
        body, name="grad_a_w_in",
        grid_spec=pltpu.PrefetchScalarGridSpec(
            num_scalar_prefetch=1, grid=(N_DEV,),
            in_specs=[ANY,
                      pl.BlockSpec((1, t, tn), lambda s, o: (o[s] // 6, 0, o[s] % 2)),
                      pl.BlockSpec((1, t, tn), lambda s, o: ((o[s] // 4) % 2, 0, o[s] % 2))] + [whole] * n_s,
            out_specs=[pl.BlockSpec((dw, tn), lambda s, o: (0, 0)), ANY, ANY, whole] + [whole] * n_s,
            scratch_shapes=[pltpu.VMEM((t, dw), BF16), pltpu.VMEM((dw, t), BF16), pltpu.VMEM((dw, tn), F32),
                            pltpu.VMEM((n_far + 1, dw, tn), BF16), pltpu.VMEM((n_far, dw, tn), BF16),
                            pltpu.SemaphoreType.DMA,
                            pltpu.SemaphoreType.DMA((n_far + 1,)), pltpu.SemaphoreType.DMA((n_far + 1,)),
                            pltpu.SemaphoreType.DMA((n_sent,)), pltpu.SemaphoreType.DMA((n_sent,))]
            + _all_reduce_scratch([s.shape for s in small])),
        out_shape=[SDS((dw, tn), F32), SDS((1, dw, tn), BF16), SDS((n_sent, dw, tn), BF16),
                   SDS((n_far - n_sent, dw, tn), BF16)] + [SDS(s.shape, F32) for s in small],
        compiler_params=_cparams(),
    )(order, a, dqg, dkv, *small)
    return outs[0], outs[1], outs[2], outs[3], list(outs[4:])


def _later_chip_copies(sums_ref, land_ref, send_sems, recv_sems):
    x, y, c, chips = _place()
    del x, y
    return [pltpu.make_async_remote_copy(
        src_ref=sums_ref.at[i], dst_ref=land_ref.at[i], send_sem=send_sems.at[i], recv_sem=recv_sems.at[i],
        device_id=(*chips[j], c), device_id_type=MESH) for i, j in enumerate(_FAR_CHIP_FIRST[_SUMS_SENT_AT_ONCE:])]


def _row_tile(rows):
    return min(rows, 512)


def _adamw(w, g, m, v):
    m2 = ADAM_B1 * m + (1.0 - ADAM_B1) * g
    v2 = ADAM_B2 * v + (1.0 - ADAM_B2) * jnp.square(g)
    m_hat = m2 / (1.0 - ADAM_B1 ** ADAM_STEP)
    v_hat = v2 / (1.0 - ADAM_B2 ** ADAM_STEP)
    delta = -ADAM_LR * (m_hat / (jnp.sqrt(v_hat) + ADAM_EPS) + ADAM_WD * w)
    return delta, m2, v2


def _reduce_adamw(name, own, partials, w, m, v):
    r, c = own.shape
    tr = _row_tile(r)
    n_p = len(partials)

    def body(own_ref, *rest):
        p_refs, (w_ref, m_ref, v_ref, grad_ref, d_ref, nm_ref, nv_ref) = rest[:n_p], rest[n_p:]
        grad = own_ref[...]
        for p_ref, (_, _, count) in zip(p_refs, partials):
            for j in range(count):
                grad = grad + p_ref[j].astype(F32)
        grad_ref[...] = grad
        d_ref[...], nm_ref[...], nv_ref[...] = _adamw(w_ref[...], grad, m_ref[...], v_ref[...])

    flat = pl.BlockSpec((tr, c), lambda i: (i, 0))
    return pl.pallas_call(
        body, name=name, grid=(r // tr,),
        in_specs=[flat] + [pl.BlockSpec((count, tr, c), lambda i, first=first, count=count: (first // count, i, 0))
                           for _, first, count in partials] + [flat, flat, flat],
        out_specs=[flat, flat, flat, flat],
        out_shape=[SDS((r, c), F32)] * 4,
        compiler_params=_cparams(),
    )(own, *[p[0] for p in partials], w, m, v)


VM = pl.BlockSpec()


def _all_reduce_small(ins, outs, lands, send_sems, recv_sems):
    _all_reduce_small_start(ins, lands, send_sems, recv_sems)
    _all_reduce_small_finish(ins, outs, lands, send_sems, recv_sems)


def _all_reduce_small_sends(ins, lands, send_sems, recv_sems):
    x, y, c, _ = _place()
    return [pltpu.make_async_remote_copy(
        src_ref=src, dst_ref=land.at[_slot(x, y, c)],
        send_sem=send_sems.at[(N_DEV - 1) * t + k - 1], recv_sem=recv_sems.at[(N_DEV - 1) * t + k - 1],
        device_id=_peer(x, y, c, k), device_id_type=MESH)
        for t, (src, land) in enumerate(zip(ins, lands)) for k in range(1, N_DEV)]


def _all_reduce_small_start(ins, lands, send_sems, recv_sems):
    x, y, c, _ = _place()
    for src, land in zip(ins, lands):
        land[_slot(x, y, c)] = src[...]
    for cp in _all_reduce_small_sends(ins, lands, send_sems, recv_sems):
        cp.start()


def _all_reduce_small_finish(ins, outs, lands, send_sems, recv_sems):
    x, y, c, _ = _place()
    copies = _all_reduce_small_sends(ins, lands, send_sems, recv_sems)
    for t, (src, land) in enumerate(zip(ins, lands)):
        for k in range(1, N_DEV):
            sem = (N_DEV - 1) * t + k - 1
            pltpu.make_async_remote_copy(
                src_ref=src, dst_ref=land.at[_slot(*_peer(x, y, c, k))],
                send_sem=send_sems.at[sem], recv_sem=recv_sems.at[sem],
                device_id=(x, y, c), device_id_type=MESH).wait_recv()
    for cp in copies:
        cp.wait_send()
    for out, land in zip(outs, lands):
        total = land[0]
        for s in range(1, N_DEV):
            total = total + land[s]
        out[...] = total


def _all_reduce_scratch(shapes):
    n_sems = (N_DEV - 1) * len(shapes)
    return ([pltpu.VMEM((N_DEV, *s), F32) for s in shapes]
            + [pltpu.SemaphoreType.DMA((n_sems,)), pltpu.SemaphoreType.DMA((n_sems,))])


def _small_adamw(my_slot, sums, ws, ms, vs):
    n = len(ws)

    def body(slot_ref, *refs):
        sum_refs, refs = refs[:n + 1], refs[n + 1:]
        w_refs, m_refs, v_refs, refs = refs[:n], refs[n:2 * n], refs[2 * n:3 * n], refs[3 * n:]
        g_refs, d_refs, nm_refs, nv_refs = refs[:n + 1], refs[n + 1:2 * n + 1], refs[2 * n + 1:3 * n + 1], refs[3 * n + 1:]
        for t in range(n + 1):
            if t == 0:
                g = sum_refs[0][:, pl.ds(pl.multiple_of(slot_ref[0] * 128, 128), 128)]
            else:
                g = sum_refs[t][...]
            g_refs[t][...] = g
            if t < n:
                d_refs[t][...], nm_refs[t][...], nv_refs[t][...] = _adamw(w_refs[t][...], g, m_refs[t][...], v_refs[t][...])

    shapes = [SDS(w.shape, F32) for w in ws]
    outs = pl.pallas_call(
        body, name="small_adamw",
        in_specs=[pl.BlockSpec(memory_space=pltpu.SMEM)] + [VM] * (4 * n + 1),
        out_specs=[VM] * (4 * n + 1),
        out_shape=shapes + [SDS(sums[-1].shape, F32)] + shapes * 3,
    )(my_slot, *sums, *ws, *ms, *vs)
    return outs[:n + 1], outs[n + 1:2 * n + 1], outs[2 * n + 1:3 * n + 1], outs[3 * n + 1:]


def kernel(x, a_norm, a_w_in, a_rel_bias, a_w_out, kv_norm, kv_w, t5_bias, b_norm, b_w_in, b_sinks, b_w_out, final_norm, loss_target, m_a_norm, m_a_w_in, m_a_rel_bias, m_a_w_out, m_kv_norm, m_kv_w, m_t5_bias, m_b_norm, m_b_w_in, m_b_sinks, m_b_w_out, m_final_norm, v_a_norm, v_a_w_in, v_a_rel_bias, v_a_w_out, v_kv_norm, v_kv_w, v_t5_bias, v_b_norm, v_b_w_in, v_b_sinks, v_b_w_out, v_final_norm):
    xi, yi, ci = lax.axis_index("x"), lax.axis_index("y"), lax.axis_index("c")
    my_slot = _slot(xi, yi, ci)

    slot_arr = jnp.reshape(my_slot, (1,)).astype(jnp.int32)
    order = _gather_order(xi, yi, ci)
    late_shards = [b_w_in[0].astype(BF16), a_w_out[0].astype(BF16), b_w_out[0].astype(BF16), kv_w.astype(BF16)]
    grad_x, loc, matrices = _local_step(
        slot_arr, order, x[0], loss_target[0], a_norm, a_w_in[0].astype(BF16), a_rel_bias[0], late_shards,
        kv_norm.reshape(1, D_MODEL), t5_bias, b_norm, b_sinks, final_norm.reshape(1, D_MODEL))

    shard_w = dict(a_w_in=a_w_in[0], b_w_in=b_w_in[0], a_w_out=a_w_out[0], b_w_out=b_w_out[0], kv_w=kv_w)
    shard_m = dict(a_w_in=m_a_w_in[0], b_w_in=m_b_w_in[0], a_w_out=m_a_w_out[0], b_w_out=m_b_w_out[0], kv_w=m_kv_w)
    shard_v = dict(a_w_in=v_a_w_in[0], b_w_in=v_b_w_in[0], a_w_out=v_a_w_out[0], b_w_out=v_b_w_out[0], kv_w=v_kv_w)
    big = {n: _reduce_adamw("adamw_" + n, own, partials, shard_w[n], shard_m[n], shard_v[n])
           for n, (own, partials) in matrices.items()}

    names = ("a_norm", "a_rel_bias", "kv_norm", "t5_bias", "b_norm", "b_sinks", "final_norm")
    tables = ("a_rel_bias", "t5_bias")

    def row(n, a):
        return a.reshape(-1, a.shape[-1]).T if n in tables else a.reshape(1, -1)

    small_w = [row(n, a) for n, a in zip(names, (a_norm, a_rel_bias, kv_norm, t5_bias, b_norm, b_sinks, final_norm))]
    small_m = [row(n, a) for n, a in zip(names, (m_a_norm, m_a_rel_bias, m_kv_norm, m_t5_bias, m_b_norm, m_b_sinks,
                                                 m_final_norm))]
    small_v = [row(n, a) for n, a in zip(names, (v_a_norm, v_a_rel_bias, v_kv_norm, v_t5_bias, v_b_norm, v_b_sinks,
                                                 v_final_norm))]
    sums = dict(loc)
    sums["a_rel_bias"] = _a_bias_grad(sums["a_rel_bias"])
    sums["t5_bias"] = sums["t5_bias"][:, :T5_BUCKETS]
    sums["b_sinks"] = sums["b_sinks"][:, 0].reshape(1, N_HEADS)
    results = _small_adamw(slot_arr, [sums[n] for n in names + ("loss",)], small_w, small_m, small_v)
    like = dict(a_norm=a_norm, a_rel_bias=a_rel_bias, kv_norm=kv_norm, t5_bias=t5_bias, b_norm=b_norm,
                b_sinks=b_sinks, final_norm=final_norm)
    sm = [{n: (part[i].T if n in tables else part[i]).reshape(like[n].shape) for i, n in enumerate(names)}
          for part in results]
    loss = results[0][len(names)][0, 0]

    order = ("a_norm", "a_w_in", "a_rel_bias", "a_w_out", "kv_norm", "kv_w", "t5_bias", "b_norm",
             "b_w_in", "b_sinks", "b_w_out", "final_norm")
    lead = dict(a_w_in=True, b_w_in=True, a_w_out=True, b_w_out=True, kv_w=False)

    def pick(kind, name):
        if name in big:
            val = big[name][kind]
            return val[None] if lead[name] else val
        return sm[kind][name]

    outs = [loss, grad_x[None]]
    for kind in range(4):
        outs += [pick(kind, n) for n in order]
    return tuple(outs)
```

```python
import functools
import math

import numpy as np
import jax
import jax.numpy as jnp
from jax import lax
from jax.experimental import pallas as pl
from jax.experimental.pallas import tpu as pltpu

F32 = jnp.float32
BF16 = jnp.bfloat16
SDS = jax.ShapeDtypeStruct

D_MODEL = 1024
HEAD_DIM = 64
CHUNK = 64
N_HEADS = 16
RMS_EPS = 1e-6
A_LEFT_CHUNKS = 8
A_BAND = (A_LEFT_CHUNKS + 1) * CHUNK
A_REL_CLIP = 256
B_KV_HEADS = 2
B_GROUP = 8
B_LEFT_CHUNKS = 2
B_BAND = (B_LEFT_CHUNKS + 1) * CHUNK
T5_BUCKETS = 32
T5_MAX_DIST = 128
QBLK = 256
A_KEYS = 3 * QBLK
B_QBLK_FWD = 128
B_QBLK_BWD = 256
B_PREV = 128
A_DIAG = A_KEYS
NEG = -1e30
SCALE = HEAD_DIM ** -0.5
N_DEV = 8

ADAM_LR = 0.001
ADAM_B1 = 0.9
ADAM_B2 = 0.999
ADAM_EPS = 1e-08
ADAM_WD = 0.01
ADAM_STEP = 10

VMEM_LIMIT_BYTES = 56 * 1024 * 1024
MESH = pl.DeviceIdType.MESH


def _cparams():
    return pltpu.CompilerParams(vmem_limit_bytes=VMEM_LIMIT_BYTES)


def _dot(a, b):
    return jnp.dot(a, b, preferred_element_type=F32)


def _dot_nt(a, b):
    return lax.dot_general(a, b, (((1,), (1,)), ((), ())), preferred_element_type=F32)


def _dot_tn(a, b):
    return lax.dot_general(a, b, (((0,), (0,)), ((), ())), preferred_element_type=F32)


def _rstd(xf):
    return lax.rsqrt(jnp.mean(xf * xf, axis=-1, keepdims=True) + RMS_EPS)


def _sigmoid(x):
    return 1.0 / (1.0 + jnp.exp(-x))


_GATHER_SEQUENCE = ((0, None), (1, 0), (2, 1), (4, None), (5, None), (3, 2), (6, None))


def _gather_order(x, y, c):
    others = [(1 - x, y), (x, 1 - y), (1 - x, 1 - y)]
    arrivals = [_slot(x, y, 1 - c)] + [_slot(*chip, c) for chip in others] + [_slot(*chip, 1 - c) for chip in others]
    return jnp.stack([_slot(x, y, c)] + [arrivals[a] for a, _ in _GATHER_SEQUENCE]).astype(jnp.int32)


def _norm_matmul_gather(order, x, gain_shard, w_shard):
    t = x.shape[0]
    dw, tn = w_shard.shape
    tm = min(t, 2048)
    n_m = t // tm

    def body(order_ref, x_ref, gs_ref, shard_ref, xn_ref, o_ref, full_ref, gain_ref,
             xn_all, wbuf, gland, send_sems, recv_sems, gsend_sems, grecv_sems, load_sems, own_sem):
        n, m = pl.program_id(0), pl.program_id(1)
        x_i, y_i, c_i, chips = _place()
        me, sibling = (x_i, y_i, c_i), (x_i, y_i, 1 - c_i)

        def send(k, block, to, src=None):
            dst = full_ref.at[_slot(*block)]
            return pltpu.make_async_remote_copy(
                src_ref=dst if src is None else src, dst_ref=dst,
                send_sem=send_sems.at[k], recv_sem=recv_sems.at[k], device_id=to, device_id_type=MESH)

        own = pltpu.make_async_copy(shard_ref, full_ref.at[_slot(*me)], own_sem)
        first = [send(0, me, sibling, src=shard_ref)]
        first += [send(1 + j, me, (*chip, c_i), src=shard_ref) for j, chip in enumerate(chips)]
        forwards = [send(4 + j, (*chip, c_i), sibling) for j, chip in enumerate(chips)]
        arrivals = [send(0, sibling, me)] + [send(1 + j, (*chip, c_i), me) for j, chip in enumerate(chips)]
        arrivals += [send(4 + j, (*chip, 1 - c_i), me) for j, chip in enumerate(chips)]
        gains = [pltpu.make_async_remote_copy(
            src_ref=gs_ref, dst_ref=gland.at[_slot(*me)], send_sem=gsend_sems.at[k - 1],
            recv_sem=grecv_sems.at[k - 1], device_id=_peer(x_i, y_i, c_i, k), device_id_type=MESH)
            for k in range(1, N_DEV)]

        @pl.when(jnp.logical_and(n == 0, m == 0))
        def _():
            own.start()
            for cp in gains + first:
                cp.start()
            pltpu.make_async_copy(shard_ref, wbuf.at[0], load_sems.at[0]).start()
            gland[_slot(*me)] = gs_ref[...]
            for k in range(1, N_DEV):
                pltpu.make_async_remote_copy(
                    src_ref=gs_ref, dst_ref=gland.at[_slot(*_peer(x_i, y_i, c_i, k))],
                    send_sem=gsend_sems.at[k - 1], recv_sem=grecv_sems.at[k - 1],
                    device_id=me, device_id_type=MESH).wait_recv()
            for s in range(N_DEV):
                gain_ref[:, 128 * s:128 * (s + 1)] = gland[s]

        rows = pl.ds(pl.multiple_of(m * tm, tm), tm)

        @pl.when(n == 0)
        def _():
            xf = x_ref[...]
            xn = ((xf * _rstd(xf)) * gain_ref[...]).astype(BF16)
            xn_all[rows, :] = xn
            xn_ref[...] = xn

        @pl.when(m == 0)
        def _():
            pltpu.make_async_copy(full_ref.at[0], wbuf.at[n % 2], load_sems.at[n % 2]).wait()

        o_ref[...] = _dot(xn_all[rows, :], wbuf[n % 2]).astype(BF16)

        for k, (arrival, forward) in enumerate(_GATHER_SEQUENCE):
            @pl.when(jnp.logical_and(n == k, m == n_m - 1))
            def _(k=k, arrival=arrival, forward=forward):
                arrivals[arrival].wait_recv()
                if forward is not None:
                    forwards[forward].start()
                pltpu.make_async_copy(full_ref.at[order_ref[k + 1]], wbuf.at[(k + 1) % 2],
                                      load_sems.at[(k + 1) % 2]).start()

        @pl.when(jnp.logical_and(n == N_DEV - 1, m == n_m - 1))
        def _():
            for cp in gains + first + forwards:
                cp.wait_send()
            own.wait()

    held = lambda n, m, order: (jnp.where(n == 0, m, n_m - 1), 0)
    return pl.pallas_call(
        body, name="norm_matmul_gather",
        grid_spec=pltpu.PrefetchScalarGridSpec(
            num_scalar_prefetch=1, grid=(N_DEV, n_m),
            in_specs=[pl.BlockSpec((tm, D_MODEL), held),
                      pl.BlockSpec((1, 128), lambda n, m, order: (0, 0)), ANY],
            out_specs=[pl.BlockSpec((tm, D_MODEL), held),
                       pl.BlockSpec((tm, tn), lambda n, m, order: (m, order[n])),
                       ANY, pl.BlockSpec((1, D_MODEL), lambda n, m, order: (0, 0))],
            scratch_shapes=[pltpu.VMEM((t, D_MODEL), BF16), pltpu.VMEM((2, dw, tn), BF16),
                            pltpu.VMEM((N_DEV, 1, 128), F32),
                            pltpu.SemaphoreType.DMA((7,)), pltpu.SemaphoreType.DMA((7,)),
                            pltpu.SemaphoreType.DMA((7,)), pltpu.SemaphoreType.DMA((7,)),
                            pltpu.SemaphoreType.DMA((2,)), pltpu.SemaphoreType.DMA]),
        out_shape=[SDS((t, D_MODEL), BF16), SDS((t, N_DEV * tn), BF16), SDS((N_DEV, dw, tn), BF16),
                   SDS((1, D_MODEL), F32)],
        compiler_params=_cparams(),
    )(order, x, gain_shard, w_shard)


def _layer_a_out(x, z, w_out, kv_gain, b_gain, kv_w, w_in_b):
    t = x.shape[0]
    tm = min(t, 1024)
    nb, _, tn = w_in_b.shape

    def body(x_ref, z_ref, wo_ref, kvg_ref, bg_ref, kvw_ref, wb_ref,
             h1_ref, kvn_ref, hb_ref, kv_ref, qg_ref):
        h1 = x_ref[...] + _dot(z_ref[...], wo_ref[...])
        h1_ref[...] = h1
        y0 = h1 * _rstd(h1)
        kvn = (y0 * kvg_ref[...]).astype(BF16)
        hb = (y0 * bg_ref[...]).astype(BF16)
        kvn_ref[...] = kvn
        hb_ref[...] = hb
        kv_ref[...] = _dot(kvn, kvw_ref[...]).astype(BF16)
        for i in range(nb):
            qg_ref[:, i * tn:(i + 1) * tn] = _dot(hb, wb_ref[i]).astype(BF16)

    row = lambda m: (m, 0)
    fix2 = lambda m: (0, 0)
    return pl.pallas_call(
        body, name="layer_a_out", grid=(t // tm,),
        in_specs=[pl.BlockSpec((tm, D_MODEL), row), pl.BlockSpec((tm, D_MODEL), row),
                  pl.BlockSpec((D_MODEL, D_MODEL), fix2),
                  pl.BlockSpec((1, D_MODEL), fix2), pl.BlockSpec((1, D_MODEL), fix2),
                  pl.BlockSpec((D_MODEL, 256), fix2),
                  pl.BlockSpec((nb, D_MODEL, tn), lambda m: (0, 0, 0))],
        out_specs=[pl.BlockSpec((tm, D_MODEL), row), pl.BlockSpec((tm, D_MODEL), row),
                   pl.BlockSpec((tm, D_MODEL), row), pl.BlockSpec((tm, 256), row),
                   pl.BlockSpec((tm, nb * tn), row)],
        out_shape=[SDS((t, D_MODEL), F32), SDS((t, D_MODEL), BF16), SDS((t, D_MODEL), BF16),
                   SDS((t, 256), BF16), SDS((t, nb * tn), BF16)],
        compiler_params=_cparams(),
    )(x, z, w_out, kv_gain, b_gain, kv_w, w_in_b)


def _layer_b_out_loss(h1, z, w_out, f_gain, target):
    t = h1.shape[0]
    tm = min(t, 1024)

    def body(h1_ref, z_ref, wo_ref, fg_ref, tgt_ref,
             dh2_ref, dh2b_ref, dz_ref, loss_ref, dfn_ref):
        @pl.when(pl.program_id(0) == 0)
        def _():
            loss_ref[...] = jnp.zeros_like(loss_ref)
            dfn_ref[...] = jnp.zeros_like(dfn_ref)

        h2 = h1_ref[...] + _dot(z_ref[...], wo_ref[...])
        r = _rstd(h2)
        yn = h2 * r
        fg = fg_ref[...]
        err = yn * fg - tgt_ref[...]
        loss_ref[...] += (0.5 / D_MODEL) * jnp.sum(err * err)
        dy = err * (1.0 / D_MODEL)
        dfn_ref[...] += jnp.sum(dy * yn, axis=0, keepdims=True)
        u = dy * fg
        dh2 = r * u - h2 * ((r * r * r) * jnp.mean(u * h2, axis=-1, keepdims=True))
        dh2_ref[...] = dh2
        dh2b = dh2.astype(BF16)
        dh2b_ref[...] = dh2b
        dz_ref[...] = _dot_nt(dh2b, wo_ref[...]).astype(BF16)

    row = lambda m: (m, 0)
    fix2 = lambda m: (0, 0)
    return pl.pallas_call(
        body, name="layer_b_out_loss", grid=(t // tm,),
        in_specs=[pl.BlockSpec((tm, D_MODEL), row), pl.BlockSpec((tm, D_MODEL), row),
                  pl.BlockSpec((D_MODEL, D_MODEL), fix2), pl.BlockSpec((1, D_MODEL), fix2),
                  pl.BlockSpec((tm, D_MODEL), row)],
        out_specs=[pl.BlockSpec((tm, D_MODEL), row), pl.BlockSpec((tm, D_MODEL), row),
                   pl.BlockSpec((tm, D_MODEL), row), pl.BlockSpec((1, 128), fix2),
                   pl.BlockSpec((1, D_MODEL), fix2)],
        out_shape=[SDS((t, D_MODEL), F32), SDS((t, D_MODEL), BF16), SDS((t, D_MODEL), BF16),
                   SDS((1, 128), F32), SDS((1, D_MODEL), F32)],
        compiler_params=_cparams(),
    )(h1, z, w_out, f_gain, target)


def _layer_b_in_bwd(dqg, dkv, w_in_b, kv_w, h1, dh2, b_gain, kv_gain, w_out_a):
    t = h1.shape[0]
    tm = min(t, 512)
    nb, _, tn = w_in_b.shape
    per = D_MODEL // tn

    def body(dqg_ref, dkv_ref, wb_ref, kvw_ref, h1_ref, dh2_ref, bg_ref, kvg_ref, wo_ref,
             dh1_ref, dh1b_ref, dz_ref, dbn_ref, dkn_ref):
        @pl.when(pl.program_id(0) == 0)
        def _():
            dbn_ref[...] = jnp.zeros_like(dbn_ref)
            dkn_ref[...] = jnp.zeros_like(dkn_ref)

        dhb = jnp.zeros((tm, D_MODEL), F32)
        for i in range(nb):
            blk = dqg_ref[i // per, :, (i % per) * tn:(i % per + 1) * tn]
            dhb = dhb + _dot_nt(blk, wb_ref[i])
        dkn = (_dot_nt(dkv_ref[0].astype(BF16), kvw_ref[:, 0:128])
               + _dot_nt(dkv_ref[1].astype(BF16), kvw_ref[:, 128:256]))
        h1 = h1_ref[...]
        r = _rstd(h1)
        xr = h1 * r
        dbn_ref[...] += jnp.sum(dhb * xr, axis=0, keepdims=True)
        dkn_ref[...] += jnp.sum(dkn * xr, axis=0, keepdims=True)
        u = dhb * bg_ref[...] + dkn * kvg_ref[...]
        dh1 = dh2_ref[...] + r * u - h1 * ((r * r * r) * jnp.mean(u * h1, axis=-1, keepdims=True))
        dh1_ref[...] = dh1
        dh1b = dh1.astype(BF16)
        dh1b_ref[...] = dh1b
        dz_ref[...] = _dot_nt(dh1b, wo_ref[...]).astype(BF16)

    row = lambda m: (m, 0)
    fix2 = lambda m: (0, 0)
    return pl.pallas_call(
        body, name="layer_b_in_bwd", grid=(t // tm,),
        in_specs=[pl.BlockSpec((2, tm, D_MODEL), lambda m: (0, m, 0)),
                  pl.BlockSpec((2, tm, 128), lambda m: (0, m, 0)),
                  pl.BlockSpec((nb, D_MODEL, tn), lambda m: (0, 0, 0)),
                  pl.BlockSpec((D_MODEL, 256), fix2),
                  pl.BlockSpec((tm, D_MODEL), row), pl.BlockSpec((tm, D_MODEL), row),
                  pl.BlockSpec((1, D_MODEL), fix2), pl.BlockSpec((1, D_MODEL), fix2),
                  pl.BlockSpec((D_MODEL, D_MODEL), fix2)],
        out_specs=[pl.BlockSpec((tm, D_MODEL), row), pl.BlockSpec((tm, D_MODEL), row),
                   pl.BlockSpec((tm, D_MODEL), row), pl.BlockSpec((1, D_MODEL), fix2),
                   pl.BlockSpec((1, D_MODEL), fix2)],
        out_shape=[SDS((t, D_MODEL), F32), SDS((t, D_MODEL), BF16), SDS((t, D_MODEL), BF16),
                   SDS((1, D_MODEL), F32), SDS((1, D_MODEL), F32)],
        compiler_params=_cparams(),
    )(dqg, dkv, w_in_b, kv_w, h1, dh2, b_gain, kv_gain, w_out_a)


def _layer_a_in_bwd(dqg, dkv, w_in_a, x, dh1, a_gain, chip_sums):
    t = x.shape[0]
    tm = min(t, 512)
    nb, _, tn = w_in_a.shape
    per = D_MODEL // tn
    n_sums = chip_sums.shape[0]

    def body(dqg_ref, dkv_ref, w_ref, x_ref, dh1_ref, ag_ref, sums_ref, dx_ref, dan_ref, land_ref,
             send_sems, recv_sems, dan_land, dan_send, dan_recv):
        @pl.when(pl.program_id(0) == 0)
        def _():
            dan_ref[...] = jnp.zeros_like(dan_ref)
            for cp in _later_chip_copies(sums_ref, land_ref, send_sems, recv_sems):
                cp.start()

        dxn = jnp.zeros((tm, D_MODEL), F32)
        for i in range(nb):
            part = i // per
            src = dqg_ref if part in (0, 3) else dkv_ref
            outer = {0: 0, 3: 1, 1: 0, 2: 1}[part]
            blk = src[outer, :, (i % per) * tn:(i % per + 1) * tn]
            dxn = dxn + _dot_nt(blk, w_ref[i])
        xf = x_ref[...]
        r = _rstd(xf)
        dan_ref[...] += jnp.sum(dxn * (xf * r), axis=0, keepdims=True)
        u = dxn * ag_ref[...]
        dx_ref[...] = dh1_ref[...] + r * u - xf * ((r * r * r) * jnp.mean(u * xf, axis=-1, keepdims=True))

        @pl.when(pl.program_id(0) == t // tm - 1)
        def _():
            _all_reduce_small([dan_ref], [dan_ref], [dan_land], dan_send, dan_recv)
            for cp in _later_chip_copies(sums_ref, land_ref, send_sems, recv_sems):
                cp.wait()

    row = lambda m: (m, 0)
    fix2 = lambda m: (0, 0)
    return pl.pallas_call(
        body, name="layer_a_in_bwd", grid=(t // tm,),
        in_specs=[pl.BlockSpec((2, tm, D_MODEL), lambda m: (0, m, 0)),
                  pl.BlockSpec((2, tm, D_MODEL), lambda m: (0, m, 0)),
                  pl.BlockSpec((nb, D_MODEL, tn), lambda m: (0, 0, 0)),
                  pl.BlockSpec((tm, D_MODEL), row), pl.BlockSpec((tm, D_MODEL), row),
                  pl.BlockSpec((1, D_MODEL), fix2), ANY],
        out_specs=[pl.BlockSpec((tm, D_MODEL), row), pl.BlockSpec((1, D_MODEL), fix2), ANY],
        out_shape=[SDS((t, D_MODEL), F32), SDS((1, D_MODEL), F32), SDS(chip_sums.shape, chip_sums.dtype)],
        scratch_shapes=[pltpu.SemaphoreType.DMA((n_sums,)), pltpu.SemaphoreType.DMA((n_sums,))]
        + _all_reduce_scratch([(1, D_MODEL)]),
        compiler_params=_cparams(),
    )(dqg, dkv, w_in_a, x, dh1, a_gain, chip_sums)


def _lut(s, vals):
    r = jnp.int32(vals[0])
    for i in range(1, len(vals)):
        r = jnp.where(s == i, jnp.int32(vals[i]), r)
    return r


def _held(steps, i):
    seq, cur = [None] * len(steps), None
    for k in range(len(steps) - 1, -1, -1):
        if steps[k][0] == i:
            cur = steps[k][1:3]
        seq[k] = cur
    for k in range(len(steps)):
        cur = seq[k] = seq[k] if seq[k] is not None else cur
    return seq


def _weight_grad_cols(name, my_slot, a, bs, steps, tn):
    t, dw = a.shape
    n_arr = len(bs)
    which = [s[0] for s in steps]
    blks = [s[3] for s in steps]

    def body(slot_ref, a_ref, *rest):
        b_refs, (o_ref, own_ref, at_ref) = rest[:n_arr], rest[n_arr:]
        s = pl.program_id(0)

        @pl.when(s == 0)
        def _():
            at_ref[...] = a_ref[...].T

        for i in range(n_arr):
            @pl.when(_lut(s, which) == i)
            def _(i=i):
                res = _dot(at_ref[...], b_refs[i][0])
                o_ref[0] = res.astype(BF16)

                @pl.when(_lut(s, blks) == slot_ref[0])
                def _():
                    own_ref[...] = res

    def b_spec(i):
        held = _held(steps, i)
        return pl.BlockSpec((1, t, tn), lambda s, slot: (_lut(s, [h[0] for h in held]), 0,
                                                         _lut(s, [h[1] for h in held])))

    return pl.pallas_call(
        body, name=name,
        grid_spec=pltpu.PrefetchScalarGridSpec(
            num_scalar_prefetch=1, grid=(len(steps),),
            in_specs=[pl.BlockSpec((t, dw), lambda s, slot: (0, 0))] + [b_spec(i) for i in range(n_arr)],
            out_specs=[pl.BlockSpec((1, dw, tn), lambda s, slot: (_lut(s, blks), 0, 0)),
                       pl.BlockSpec((dw, tn), lambda s, slot: (0, 0))],
            scratch_shapes=[pltpu.VMEM((dw, t), BF16)]),
        out_shape=[SDS((N_DEV, dw, tn), BF16), SDS((dw, tn), F32)],
        compiler_params=_cparams(),
    )(my_slot, a, *bs)


def _weight_grad_rows(name, my_slot, a, b):
    t, dw = a.shape
    n_o, _, c = b.shape
    rows = dw // N_DEV
    tn = min(c, 256)
    per = c // tn

    def body(slot_ref, a_ref, b_ref, o_ref, own_ref, at_ref, res_ref):
        @pl.when(pl.program_id(0) == 0)
        def _():
            at_ref[...] = a_ref[...].T

        res_ref[...] = _dot(at_ref[...], b_ref[0].astype(BF16))
        o_ref[...] = res_ref[...].astype(BF16)
        own_ref[...] = res_ref[pl.ds(pl.multiple_of(slot_ref[0] * rows, rows), rows), :]

    all_rows, own = pl.pallas_call(
        body, name=name,
        grid_spec=pltpu.PrefetchScalarGridSpec(
            num_scalar_prefetch=1, grid=(n_o * per,),
            in_specs=[pl.BlockSpec((t, dw), lambda s, slot: (0, 0)),
                      pl.BlockSpec((1, t, tn), lambda s, slot: (s // per, 0, s % per))],
            out_specs=[pl.BlockSpec((dw, tn), lambda s, slot: (0, s)),
                       pl.BlockSpec((rows, tn), lambda s, slot: (0, s))],
            scratch_shapes=[pltpu.VMEM((dw, t), BF16), pltpu.VMEM((dw, tn), F32)]),
        out_shape=[SDS((dw, n_o * c), BF16), SDS((rows, n_o * c), F32)],
        compiler_params=_cparams(),
    )(my_slot, a, b)
    return all_rows.reshape(N_DEV, rows, n_o * c), own


def _lane_lo():
    return lax.broadcasted_iota(jnp.int32, (1, 128), 1) < HEAD_DIM


def _collapse_chunks(ds, keys):
    if ds.shape[1] < keys:
        ds = jnp.concatenate([jnp.zeros((ds.shape[0], keys - ds.shape[1]), F32), ds], axis=1)
    gc = ds[0:CHUNK]
    for cc in range(1, ds.shape[0] // CHUNK):
        gc = gc + pltpu.roll(ds[cc * CHUNK:(cc + 1) * CHUNK], keys - cc * CHUNK, 1)
    return gc


def _offset_sums(gc):
    hi = gc.astype(BF16)
    lo = (gc - hi.astype(F32)).astype(BF16)
    flip = (lax.broadcasted_iota(jnp.int32, (CHUNK, CHUNK), 0)
            + lax.broadcasted_iota(jnp.int32, (CHUNK, CHUNK), 1) == CHUNK - 1).astype(BF16)
    gf = _dot(flip, hi) + _dot(flip, lo)
    skew = pltpu.roll(gf, 0, 1, stride=1, stride_axis=0)
    return jnp.sum(skew, axis=0, keepdims=True)


def _band_bias(w_row, band, rows):
    keys = w_row.shape[1]
    base = jnp.broadcast_to(w_row, (CHUNK, keys))
    skew = pltpu.roll(base, 0, 1, stride=1, stride_axis=0)
    skew = pltpu.roll(skew, keys - (CHUNK - 1), 1)
    col = lax.broadcasted_iota(jnp.int32, (CHUNK, keys), 1)
    chunk0 = jnp.where(col < band, skew, NEG)
    return jnp.concatenate(
        [chunk0] + [pltpu.roll(chunk0, cc * CHUNK, 1) for cc in range(1, rows // CHUNK)], axis=0)


def _silu_parts(g):
    sg = _sigmoid(g)
    return g * sg, sg * (1.0 + g * (1.0 - sg))


A_PAIRS_FWD = 8
A_PAIRS_BWD = 4


def _a_specs(pairs):
    lanes = 128 * pairs
    steps = D_MODEL // lanes
    q = pl.BlockSpec((QBLK, lanes), lambda p, j: (j, p))
    ks = [pl.BlockSpec((QBLK, lanes), lambda p, j, b=b: (jnp.maximum(j - 2 + b, 0), steps + p)) for b in range(3)]
    vs = [pl.BlockSpec((QBLK, lanes), lambda p, j, b=b: (jnp.maximum(j - 2 + b, 0), 2 * steps + p))
          for b in range(3)]
    g = pl.BlockSpec((QBLK, lanes), lambda p, j: (j, 3 * steps + p))
    bias = pl.BlockSpec((pairs, 8, A_KEYS), lambda p, j: (p, 0, 0))
    return q, ks, vs, g, bias


def _a_fill_bias(w_ref, b_ref, j, pairs):
    _fill_bias(2 * pairs, lambda h: w_ref[h // 2, h % 2:h % 2 + 1, :], A_BAND, b_ref, j)


def _by_valid_key_blocks(j, fn):
    pl.when(j == 0)(functools.partial(fn, 1))
    pl.when(j == 1)(functools.partial(fn, 2))
    pl.when(j >= 2)(functools.partial(fn, 3))


def _fill_bias(n, get_row, band, bias_scr, j):
    @pl.when(j == 0)
    def _():
        for h in range(n):
            bias_scr[h] = _band_bias(get_row(h), band, bias_scr.shape[1])


def _normalise_pair(rs, mxs, lane_lo, extra=None):
    num = jnp.where(lane_lo, rs[0], rs[1])
    den = pltpu.roll(jnp.where(lane_lo, rs[1], rs[0]), HEAD_DIM, 1)
    if extra is not None:
        den = den + jnp.where(lane_lo, extra[0], extra[1])
    return num / den, jnp.where(lane_lo, mxs[0], mxs[1]) + jnp.log(den)


def _own_everywhere(x, sel):
    return jnp.where(sel, x, pltpu.roll(x, HEAD_DIM, 1))


def _minus_rows(s, row_full):
    return jnp.concatenate([s[:, i:i + 128] - row_full for i in range(0, s.shape[1], 128)], axis=1)


def _attn_a_fwd(qkvg, bias, gather):
    t = qkvg.shape[0]
    nq = t // QBLK
    n_g = len(gather)
    pairs = A_PAIRS_FWD
    lanes = 128 * pairs
    steps = D_MODEL // lanes
    q_spec, k_specs, v_specs, g_spec, bias_spec = _a_specs(pairs)

    def body(q_ref, k0, k1, k2, v0, v1, v2, g_ref, w_ref, *rest):
        shard_refs, rest = rest[:n_g], rest[n_g:]
        z_ref, o_ref, lse_ref = rest[:3]
        full_refs, (b_ref, *comm) = rest[3:3 + n_g], rest[3 + n_g:]
        p = pl.program_id(0)
        j = pl.program_id(1)
        start, forward, finish = _gather_phases(shard_refs, full_refs, *comm)
        at = p * nq + j
        pl.when(at == 0)(start)
        pl.when(at == steps * nq // 2)(forward)
        _a_fill_bias(w_ref, b_ref, j, pairs)
        lane_lo = _lane_lo()
        sels = (lane_lo, jnp.logical_not(lane_lo))

        def attend(n_blocks):
            first_col = (3 - n_blocks) * QBLK
            for pp in range(pairs):
                cols = slice(128 * pp, 128 * (pp + 1))
                k = jnp.concatenate([r[:, cols] for r in (k0, k1, k2)[3 - n_blocks:]], axis=0)
                v = jnp.concatenate([r[:, cols] for r in (v0, v1, v2)[3 - n_blocks:]], axis=0)
                q = q_ref[:, cols]
                qm2 = jnp.concatenate([jnp.where(sel, q, jnp.zeros_like(q)) for sel in sels], axis=0) * SCALE
                s2 = _dot_nt(qm2, k)
                rs, mxs = [], []
                for hh, sel in enumerate(sels):
                    s = s2[hh * QBLK:(hh + 1) * QBLK] + b_ref[2 * pp + hh, :, first_col:]
                    mxs.append(jnp.max(s, axis=-1, keepdims=True))
                    e = jnp.exp(s - mxs[hh]).astype(BF16)
                    rs.append(_dot(e, jnp.where(sel, v, jnp.ones_like(v))))
                o, lse = _normalise_pair(rs, mxs, lane_lo)
                silu, _ = _silu_parts(g_ref[:, cols].astype(F32))
                o_ref[:, cols] = o.astype(BF16)
                z_ref[:, cols] = (o * silu).astype(BF16)
                lse_ref[:, cols] = lse

        _by_valid_key_blocks(j, attend)
        pl.when(at == steps * nq - 1)(finish)

    out_spec = pl.BlockSpec((QBLK, lanes), lambda p, j: (j, p))
    outs = pl.pallas_call(
        body, name="attn_a_fwd", grid=(steps, nq),
        in_specs=[q_spec, *k_specs, *v_specs, g_spec, bias_spec] + [ANY] * n_g,
        out_specs=[out_spec, out_spec, out_spec] + [ANY] * n_g,
        out_shape=[SDS((t, D_MODEL), BF16), SDS((t, D_MODEL), BF16), SDS((t, D_MODEL), F32)]
        + [SDS((N_DEV, *s.shape), s.dtype) for s in gather],
        scratch_shapes=[pltpu.VMEM((2 * pairs, QBLK, A_KEYS), F32)] + _gather_scratch(n_g),
        compiler_params=_cparams(),
    )(qkvg, qkvg, qkvg, qkvg, qkvg, qkvg, qkvg, qkvg, bias, *gather)
    return outs[0], outs[1], outs[2], list(outs[3:])


def _attn_a_bwd(qkvg, bias, out_a, lse, dz, scatter):
    t = qkvg.shape[0]
    nq = t // QBLK
    n_sc = len(scatter)
    pairs = A_PAIRS_BWD
    lanes = 128 * pairs
    steps = D_MODEL // lanes
    q_spec, k_specs, v_specs, g_spec, bias_spec = _a_specs(pairs)

    def body(q_ref, k0, k1, k2, v0, v1, v2, g_ref, w_ref, o_ref, lse_ref, dz_ref, *rest):
        sc_refs, rest = rest[:n_sc], rest[n_sc:]
        dqg_ref, dkv_ref, dg_ref = rest[:3]
        land_refs, rest = rest[3:3 + n_sc], rest[3 + n_sc:]
        dk_acc, dv_acc, gt_acc, b_ref, send_sems, recv_sems = rest
        j = pl.program_id(1)
        first = jnp.logical_and(pl.program_id(0) == 0, j == 0)
        last = jnp.logical_and(pl.program_id(0) == steps - 1, j == nq - 1)

        @pl.when(first)
        def _():
            for cp in _scatter_copies(sc_refs, land_refs, send_sems, recv_sems):
                cp.start()

        _a_fill_bias(w_ref, b_ref, j, pairs)

        @pl.when(j == 0)
        def _():
            dk_acc[...] = jnp.zeros_like(dk_acc)
            dv_acc[...] = jnp.zeros_like(dv_acc)
            gt_acc[...] = jnp.zeros_like(gt_acc)

        lane_lo = _lane_lo()
        sels = (lane_lo, jnp.logical_not(lane_lo))

        def attend(n_blocks):
            first_col = (3 - n_blocks) * QBLK
            for pp in range(pairs):
                cols = slice(128 * pp, 128 * (pp + 1))
                q = q_ref[:, cols]
                k = jnp.concatenate([r[:, cols] for r in (k0, k1, k2)[3 - n_blocks:]], axis=0)
                v = jnp.concatenate([r[:, cols] for r in (v0, v1, v2)[3 - n_blocks:]], axis=0)
                o = o_ref[:, cols].astype(F32)
                lse_pair = lse_ref[:, cols]
                dzf = dz_ref[:, cols].astype(F32)
                silu, dsilu = _silu_parts(g_ref[:, cols].astype(F32))
                do = dzf * silu
                dqg_ref[1, :, cols] = (dzf * o * dsilu).astype(BF16)
                doo = do * o
                qm2 = jnp.concatenate([jnp.where(sel, q, jnp.zeros_like(q)) for sel in sels], axis=0) * SCALE
                dom2 = jnp.concatenate([jnp.where(sel, do, 0.0) for sel in sels], axis=0).astype(BF16)
                s2 = _dot_nt(qm2, k)
                dp2 = _dot_nt(dom2, v)
                ps, dss = [], []
                for hh, sel in enumerate(sels):
                    rows = slice(hh * QBLK, (hh + 1) * QBLK)
                    s = s2[rows] + b_ref[2 * pp + hh, :, first_col:]
                    p = jnp.exp(_minus_rows(s, _own_everywhere(lse_pair, sel)))
                    delta = jnp.sum(jnp.where(sel, doo, 0.0), axis=-1, keepdims=True)
                    ds = p * (dp2[rows] - delta)
                    gt_acc[2 * pp + hh] += _collapse_chunks(ds, A_KEYS)
                    ps.append(p.astype(BF16))
                    dss.append(ds.astype(BF16))
                dsb2 = jnp.concatenate(dss, axis=0)
                dq2 = _dot(dsb2, k) * SCALE
                dk_blk = _dot_tn(dsb2, qm2)
                dv_blk = _dot_tn(jnp.concatenate(ps, axis=0), dom2)
                dqg_ref[0, :, cols] = jnp.where(lane_lo, dq2[0:QBLK], dq2[QBLK:2 * QBLK]).astype(BF16)
                for b in range(n_blocks):
                    rows = pl.ds(pl.multiple_of((j - n_blocks + 1 + b) * QBLK, QBLK), QBLK)
                    dk_acc[rows, cols] += dk_blk[b * QBLK:(b + 1) * QBLK]
                    dv_acc[rows, cols] += dv_blk[b * QBLK:(b + 1) * QBLK]

        _by_valid_key_blocks(j, attend)

        @pl.when(j == nq - 1)
        def _():
            dkv_ref[0] = dk_acc[...].astype(BF16)
            dkv_ref[1] = dv_acc[...].astype(BF16)
            for pp in range(pairs):
                dg_ref[pp] = jnp.concatenate([_offset_sums(gt_acc[2 * pp]), _offset_sums(gt_acc[2 * pp + 1]),
                                              jnp.zeros((6, A_DIAG), F32)], axis=0)

        @pl.when(last)
        def _():
            for cp in _scatter_copies(sc_refs, land_refs, send_sems, recv_sems):
                cp.wait()

    blk = pl.BlockSpec((QBLK, lanes), lambda p, j: (j, p))
    outs = pl.pallas_call(
        body, name="attn_a_bwd", grid=(steps, nq),
        in_specs=[q_spec, *k_specs, *v_specs, g_spec, bias_spec, blk, blk, blk] + [ANY] * n_sc,
        out_specs=[pl.BlockSpec((2, QBLK, lanes), lambda p, j: (0, j, p)),
                   pl.BlockSpec((2, t, lanes), lambda p, j: (0, 0, p)),
                   pl.BlockSpec((pairs, 8, A_DIAG), lambda p, j: (p, 0, 0))] + [ANY] * n_sc,
        out_shape=[SDS((2, t, D_MODEL), BF16), SDS((2, t, D_MODEL), BF16), SDS((N_HEADS // 2, 8, A_DIAG), F32)]
        + [SDS((N_DEV - 1, *g.shape[1:]), g.dtype) for g in scatter],
        scratch_shapes=[pltpu.VMEM((t, lanes), F32), pltpu.VMEM((t, lanes), F32),
                        pltpu.VMEM((2 * pairs, CHUNK, A_KEYS), F32), pltpu.VMEM((2 * pairs, QBLK, A_KEYS), F32),
                        pltpu.SemaphoreType.DMA(((N_DEV - 1) * n_sc,)),
                        pltpu.SemaphoreType.DMA(((N_DEV - 1) * n_sc,))],
        compiler_params=_cparams(),
    )(qkvg, qkvg, qkvg, qkvg, qkvg, qkvg, qkvg, qkvg, bias, out_a, lse, dz, *scatter)
    return outs[0], outs[1], outs[2], list(outs[3:])


def _b_specs(qblk):
    per = qblk // B_PREV
    q = pl.BlockSpec((qblk, 512), lambda h, j: (j, h))
    g = pl.BlockSpec((qblk, 512), lambda h, j: (j, 2 + h))
    kp = pl.BlockSpec((B_PREV, 128), lambda h, j: (jnp.maximum(per * j - 1, 0), 0))
    kc = pl.BlockSpec((qblk, 128), lambda h, j: (j, 0))
    vp = pl.BlockSpec((B_PREV, 128), lambda h, j: (jnp.maximum(per * j - 1, 0), 1))
    vc = pl.BlockSpec((qblk, 128), lambda h, j: (j, 1))
    bias = pl.BlockSpec((B_GROUP, qblk + B_PREV), lambda h, j: (h, 0))
    sinks = pl.BlockSpec(memory_space=pltpu.SMEM)
    return q, g, kp, kc, vp, vc, bias, sinks


def _b_operands(kp, kc, vp, vc, kvh, with_prev):
    k = jnp.concatenate([kp[...], kc[...]], axis=0) if with_prev else kc[...]
    v = jnp.concatenate([vp[...], vc[...]], axis=0) if with_prev else vc[...]
    kr = pltpu.roll(k, HEAD_DIM, 1)
    vr = pltpu.roll(v, HEAD_DIM, 1)
    first = kvh == 0
    return (jnp.where(first, k, kr), jnp.where(first, kr, k),
            jnp.where(first, v, vr), jnp.where(first, vr, v))


def _attn_b_fwd(qg, kv, bias, sinks):
    t = qg.shape[0]
    qblk = B_QBLK_FWD
    per_step = 4
    step = per_step * qblk
    q_spec, g_spec, kp_spec, kc_spec, vp_spec, vc_spec, _, sink_spec = _b_specs(step)
    bias_spec = pl.BlockSpec((B_GROUP, qblk + B_PREV), lambda h, j: (h, 0))

    def body(q_ref, g_ref, kp, kc, vp, vc, w_ref, sink_ref, z_ref, o_ref, lse_ref, b_ref):
        kvh = pl.program_id(0)
        j = pl.program_id(1)
        _fill_bias(B_GROUP, lambda h: w_ref[h:h + 1, :], B_BAND, b_ref, j)
        lane_lo = _lane_lo()
        n_pairs = B_GROUP // 2

        def attend(first):
            k_lo, k_hi, v_lo, v_hi = _b_operands(kp, kc, vp, vc, kvh, True)
            for sb in range(per_step):
                no_prev = first and sb == 0
                first_col = B_PREV if no_prev else 0
                keys = slice(sb * qblk + first_col, (sb + 1) * qblk + B_PREV)
                qrows = slice(sb * qblk, (sb + 1) * qblk)
                halves = []
                for hh, sel in enumerate((lane_lo, jnp.logical_not(lane_lo))):
                    kk = (k_lo if hh == 0 else k_hi)[keys]
                    vv = (v_lo if hh == 0 else v_hi)[keys]
                    qm4 = jnp.concatenate(
                        [jnp.where(sel, q_ref[qrows, 128 * pp:128 * (pp + 1)], jnp.zeros((qblk, 128), BF16))
                         for pp in range(n_pairs)], axis=0) * SCALE
                    s4 = _dot_nt(qm4, kk)
                    es, mxs = [], []
                    for pp in range(n_pairs):
                        g = 2 * pp + hh
                        s = s4[pp * qblk:(pp + 1) * qblk] + b_ref[g, :, first_col:]
                        mxs.append(jnp.maximum(jnp.max(s, axis=-1, keepdims=True), sink_ref[kvh * B_GROUP + g]))
                        es.append(jnp.exp(s - mxs[pp]).astype(BF16))
                    r4 = _dot(jnp.concatenate(es, axis=0), jnp.where(sel, vv, jnp.ones_like(vv)))
                    halves.append((r4, mxs))
                for pp in range(n_pairs):
                    cols = slice(128 * pp, 128 * (pp + 1))
                    rows = slice(pp * qblk, (pp + 1) * qblk)
                    mxs = [halves[hh][1][pp] for hh in range(2)]
                    sink_terms = [jnp.exp(sink_ref[kvh * B_GROUP + 2 * pp + hh] - mxs[hh]) for hh in range(2)]
                    o, lse = _normalise_pair([halves[hh][0][rows] for hh in range(2)], mxs, lane_lo, sink_terms)
                    silu, _ = _silu_parts(g_ref[qrows, cols].astype(F32))
                    o_ref[qrows, cols] = o.astype(BF16)
                    z_ref[qrows, cols] = (o * silu).astype(BF16)
                    lse_ref[qrows, cols] = lse

        pl.when(j == 0)(functools.partial(attend, True))
        pl.when(j >= 1)(functools.partial(attend, False))

    out_spec = pl.BlockSpec((step, 512), lambda h, j: (j, h))
    return pl.pallas_call(
        body, name="attn_b_fwd", grid=(B_KV_HEADS, t // step),
        in_specs=[q_spec, g_spec, kp_spec, kc_spec, vp_spec, vc_spec, bias_spec, sink_spec],
        out_specs=[out_spec, out_spec, out_spec],
        out_shape=[SDS((t, D_MODEL), BF16), SDS((t, D_MODEL), BF16), SDS((t, D_MODEL), F32)],
        scratch_shapes=[pltpu.VMEM((B_GROUP, qblk, qblk + B_PREV), F32)],
        compiler_params=_cparams(),
    )(qg, qg, kv, kv, kv, kv, bias, sinks)


def _attn_b_bwd(qg, kv, bias, sinks, out_b, lse, dz, bucket_onehot):
    t = qg.shape[0]
    qblk = B_QBLK_BWD
    keys = qblk + B_PREV
    nq = t // qblk
    per = qblk // B_PREV

    def body(q_ref, g_ref, kp, kc, vp, vc, w_ref, sink_ref, o_ref, lse_ref, dz_ref, oh_ref,
             dqg_ref, dkv_ref, dt5_ref, dsink_ref, gt_acc, b_ref):
        j = pl.program_id(0)
        _fill_bias(N_HEADS, lambda h: w_ref[h:h + 1, :], B_BAND, b_ref, j)

        @pl.when(j == 0)
        def _():
            dkv_ref[...] = jnp.zeros_like(dkv_ref)
            gt_acc[...] = jnp.zeros_like(gt_acc)
            dsink_ref[...] = jnp.zeros_like(dsink_ref)

        lane_lo = _lane_lo()

        def attend(with_prev):
            first_col = 0 if with_prev else B_PREV
            dk_add = jnp.zeros((keys - first_col, 128), F32)
            dv_add = jnp.zeros((keys - first_col, 128), F32)
            for kvh in range(B_KV_HEADS):
                k_lo, k_hi, v_lo, v_hi = _b_operands(kp, kc, vp, vc, kvh, with_prev)
                dk_blk = jnp.zeros((keys - first_col, 128), F32)
                dv_blk = jnp.zeros((keys - first_col, 128), F32)
                for pp in range(B_GROUP // 2):
                    cols = slice(512 * kvh + 128 * pp, 512 * kvh + 128 * (pp + 1))
                    qp = q_ref[:, cols]
                    o = o_ref[:, cols].astype(F32)
                    lse_pair = lse_ref[:, cols]
                    dzf = dz_ref[:, cols].astype(F32)
                    silu, dsilu = _silu_parts(g_ref[:, cols].astype(F32))
                    do = dzf * silu
                    dqg_ref[1, :, cols] = (dzf * o * dsilu).astype(BF16)
                    doo = do * o
                    dqs = []
                    for hh in range(2):
                        g = kvh * B_GROUP + 2 * pp + hh
                        sel = lane_lo if hh == 0 else jnp.logical_not(lane_lo)
                        kk = k_lo if hh == 0 else k_hi
                        vv = v_lo if hh == 0 else v_hi
                        qm = jnp.where(sel, qp, jnp.zeros_like(qp)) * SCALE
                        s = _dot_nt(qm, kk) + b_ref[g, :, first_col:]
                        lse_h = _own_everywhere(lse_pair, sel)
                        p = jnp.exp(_minus_rows(s, lse_h))
                        delta = jnp.sum(jnp.where(sel, doo, 0.0), axis=-1, keepdims=True)
                        dom = jnp.where(sel, do, 0.0).astype(BF16)
                        dp = _dot_nt(dom, vv)
                        ds = p * (dp - delta)
                        gt_acc[g, :, first_col:] += ds
                        dsink_ref[g:g + 1, :] -= jnp.sum(jnp.exp(sink_ref[g] - lse_h) * delta, axis=0, keepdims=True)
                        dsb = ds.astype(BF16)
                        dqs.append(_dot(dsb, kk) * SCALE)
                        dk_blk = dk_blk + _dot_tn(dsb, qm)
                        dv_blk = dv_blk + _dot_tn(p.astype(BF16), dom)
                    dqg_ref[0, :, cols] = jnp.where(lane_lo, dqs[0], dqs[1]).astype(BF16)
                mine = lane_lo if kvh == 0 else jnp.logical_not(lane_lo)
                dk_add = dk_add + jnp.where(mine, dk_blk + pltpu.roll(dk_blk, HEAD_DIM, 1), 0.0)
                dv_add = dv_add + jnp.where(mine, dv_blk + pltpu.roll(dv_blk, HEAD_DIM, 1), 0.0)
            first_key = B_PREV if with_prev else 0
            if with_prev:
                rows = pl.ds(pl.multiple_of(j * qblk - B_PREV, B_PREV), B_PREV)
                dkv_ref[0, rows, :] += dk_add[0:B_PREV]
                dkv_ref[1, rows, :] += dv_add[0:B_PREV]
            rows = pl.ds(pl.multiple_of(j * qblk, qblk), qblk)
            dkv_ref[0, rows, :] += dk_add[first_key:first_key + qblk]
            dkv_ref[1, rows, :] += dv_add[first_key:first_key + qblk]

        pl.when(j == 0)(functools.partial(attend, False))
        pl.when(j >= 1)(functools.partial(attend, True))

        @pl.when(j == nq - 1)
        def _():
            dd = jnp.concatenate([_offset_sums(_collapse_chunks(gt_acc[g], keys)) for g in range(N_HEADS)], axis=0)
            hi = dd.astype(BF16)
            lo = (dd - hi.astype(F32)).astype(BF16)
            dt5_ref[...] = _dot(hi, oh_ref[...]) + _dot(lo, oh_ref[...])

    wide = lambda col: pl.BlockSpec((qblk, D_MODEL), lambda j, col=col: (j, col))
    prev = lambda col: pl.BlockSpec((B_PREV, 128), lambda j, col=col: (jnp.maximum(per * j - 1, 0), col))
    cur = lambda col: pl.BlockSpec((qblk, 128), lambda j, col=col: (j, col))
    fixed = lambda shape: pl.BlockSpec(shape, lambda j: (0,) * len(shape))
    return pl.pallas_call(
        body, name="attn_b_bwd", grid=(nq,),
        in_specs=[wide(0), wide(1), prev(0), cur(0), prev(1), cur(1), fixed((N_HEADS, keys)),
                  pl.BlockSpec(memory_space=pltpu.SMEM), wide(0), wide(0), wide(0), fixed((keys, 128))],
        out_specs=[pl.BlockSpec((2, qblk, D_MODEL), lambda j: (0, j, 0)), fixed((2, t, 128)),
                   fixed((N_HEADS, 128)), fixed((N_HEADS, 128))],
        out_shape=[SDS((2, t, D_MODEL), BF16), SDS((2, t, 128), F32),
                   SDS((N_HEADS, 128), F32), SDS((N_HEADS, 128), F32)],
        scratch_shapes=[pltpu.VMEM((N_HEADS, qblk, keys), F32), pltpu.VMEM((N_HEADS, qblk, keys), F32)],
        compiler_params=_cparams(),
    )(qg, qg, kv, kv, kv, kv, bias, sinks, out_b, lse, dz, bucket_onehot)


def _a_bias_by_offset(rel_bias):
    m = np.arange(A_DIAG)
    idx = np.clip(A_BAND - 1 - m, -A_REL_CLIP, A_REL_CLIP) + A_REL_CLIP
    by_head = rel_bias[idx].T.reshape(N_HEADS // 2, 2, A_DIAG)
    return jnp.concatenate([by_head, jnp.zeros((N_HEADS // 2, 6, A_DIAG), F32)], axis=1)


def _a_bias_grad(offset_sums):
    first = 319
    tail = jnp.sum(offset_sums[:, :first], axis=1)
    body = jnp.flip(offset_sums[:, first:first + 320], axis=1)
    body = body.at[:, -1].add(tail)
    full = jnp.concatenate([jnp.zeros((N_HEADS, 193), F32), body], axis=1)
    return full


def _t5_bucket(rel):
    nb = T5_BUCKETS // 2
    max_exact = nb // 2
    ret = jnp.where(rel > 0, nb, 0)
    n = jnp.abs(rel)
    nf = jnp.maximum(n, 1).astype(jnp.float32)
    large = max_exact + (jnp.log(nf / max_exact) / math.log(T5_MAX_DIST / max_exact)
                         * (nb - max_exact)).astype(jnp.int32)
    large = jnp.minimum(large, nb - 1)
    return ret + jnp.where(n < max_exact, n, large)


def _b_offset_buckets(keys):
    return _t5_bucket(jnp.arange(keys, dtype=jnp.int32) - (B_LEFT_CHUNKS * CHUNK + CHUNK - 1))


def _b_bias_by_offset(t5_table, keys):
    return t5_table[_b_offset_buckets(keys)].T


def _b_bucket_onehot(keys):
    return (_b_offset_buckets(keys)[:, None] == jnp.arange(128)[None, :]).astype(BF16)


def _local_step(my_slot, order, x, target, a_gain_shard, w_in_a_shard, rel_bias, late_shards, kv_gain,
                t5_table, b_gain, sinks, f_gain):
    a_bias = _a_bias_by_offset(rel_bias)
    b_bias_fwd = _b_bias_by_offset(t5_table, B_QBLK_FWD + B_PREV)
    b_bias_bwd = _b_bias_by_offset(t5_table, B_QBLK_BWD + B_PREV)
    sinks_flat = sinks.reshape(N_HEADS)

    xn, qkvg, w_in_a, a_gain = _norm_matmul_gather(order, x, a_gain_shard, w_in_a_shard)
    z_a, out_a, lse_a, (w_in_b, w_out_a, w_out_b, kv_w) = _attn_a_fwd(qkvg, a_bias, late_shards)
    w_out_a = w_out_a.reshape(D_MODEL, D_MODEL)
    w_out_b = w_out_b.reshape(D_MODEL, D_MODEL)
    kv_w = kv_w.reshape(D_MODEL, 2 * 128)
    h1, kvn, hb, kv, qg = _layer_a_out(x, z_a, w_out_a, kv_gain, b_gain, kv_w, w_in_b)
    z_b, out_b, lse_b = _attn_b_fwd(qg, kv, b_bias_fwd, sinks_flat)
    dh2, dh2b, dz_b, loss, d_fn = _layer_b_out_loss(h1, z_b, w_out_b, f_gain, target)

    dqg_b, dkv_b, d_t5, d_sink = _attn_b_bwd(qg, kv, b_bias_bwd, sinks_flat, out_b, lse_b, dz_b,
                                             _b_bucket_onehot(B_QBLK_BWD + B_PREV))
    dh1, dh1b, dz_a, d_bn, d_kn = _layer_b_in_bwd(dqg_b, dkv_b, w_in_b, kv_w, h1, dh2, b_gain, kv_gain, w_out_a)
    early = dict(
        b_w_out=_weight_grad_rows("grad_b_w_out", my_slot, z_b, dh2b[None]),
        b_w_in=_weight_grad_cols("grad_b_w_in", my_slot, hb, [dqg_b],
                                 [(0, o, c, 4 * o + c) for o in range(2) for c in range(4)], 256),
        kv_w=_weight_grad_rows("grad_kv_w", my_slot, kvn, dkv_b),
        a_w_out=_weight_grad_rows("grad_a_w_out", my_slot, z_a, dh1b[None]))
    dqg_a, dkv_a, d_rel, landed = _attn_a_bwd(qkvg, a_bias, out_a, lse_a, dz_a, [g[0] for g in early.values()])
    ready = dict(
        loss=loss, a_rel_bias=d_rel[:, :2].reshape(N_HEADS, A_DIAG),
        kv_norm=d_kn, t5_bias=d_t5, b_norm=d_bn, b_sinks=d_sink, final_norm=d_fn)
    g_own, from_sibling, from_far, chip_sums, ready_sums = _grad_a_w_in_reduce(
        _a_w_in_grad_order(), xn, dqg_a, dkv_a, list(ready.values()))
    grad_x, d_an, from_near = _layer_a_in_bwd(dqg_a, dkv_a, w_in_a, x, dh1, a_gain, chip_sums)

    matrices = {n: (g[1], [(land, 0, N_DEV - 1)]) for (n, g), land in zip(early.items(), landed)}
    matrices["a_w_in"] = (g_own, [(from_sibling, 0, 1), (from_far, 0, from_far.shape[0]),
                                  (from_near, 0, from_near.shape[0])])
    small = dict(zip(ready.keys(), ready_sums), a_norm=d_an)
    return grad_x, small, matrices


def _place():
    x, y, c = lax.axis_index("x"), lax.axis_index("y"), lax.axis_index("c")
    chips = [(1 - x, y), (x, 1 - y), (1 - x, 1 - y)]
    return x, y, c, chips


def _slot(px, py, pc):
    return 4 * px + 2 * py + pc


ANY = pl.BlockSpec(memory_space=pl.ANY)


def _peer(x, y, c, k):
    return (x ^ (k >> 2), y ^ ((k >> 1) & 1), c ^ (k & 1))


def _scatter_copies(grad_refs, land_refs, send_sems, recv_sems):
    x, y, c, _ = _place()
    copies = []
    for t, (grad, land) in enumerate(zip(grad_refs, land_refs)):
        for k in range(1, N_DEV):
            peer = _peer(x, y, c, k)
            sem = (N_DEV - 1) * t + k - 1
            copies.append(pltpu.make_async_remote_copy(
                src_ref=grad.at[_slot(*peer)], dst_ref=land.at[k - 1],
                send_sem=send_sems.at[sem], recv_sem=recv_sems.at[sem],
                device_id=peer, device_id_type=MESH))
    return copies


def _gather_phases(ins, outs, send_sems, recv_sems, local_sems):
    n = len(ins)
    x, y, c, chips = _place()
    me, sibling = (x, y, c), (x, y, 1 - c)

    def copy(t, k, block, to, src=None):
        dst = outs[t].at[_slot(*block)]
        return pltpu.make_async_remote_copy(
            src_ref=dst if src is None else src, dst_ref=dst,
            send_sem=send_sems.at[7 * t + k], recv_sem=recv_sems.at[7 * t + k],
            device_id=to, device_id_type=MESH)

    def lists():
        mine = [pltpu.make_async_copy(ins[t], outs[t].at[_slot(*me)], local_sems.at[t]) for t in range(n)]
        first = []
        for t in range(n):
            first.append(copy(t, 0, me, sibling, src=ins[t]))
            first += [copy(t, 1 + j, me, (*chip, c), src=ins[t]) for j, chip in enumerate(chips)]
        passed = [copy(t, 4 + j, (*chip, c), sibling) for t in range(n) for j, chip in enumerate(chips)]
        return mine, first, passed

    def start():
        mine, first, _ = lists()
        for cp in mine + first:
            cp.start()

    def forward():
        _, _, passed = lists()
        for t in range(n):
            for j, chip in enumerate(chips):
                copy(t, 1 + j, (*chip, c), me).wait_recv()
                passed[3 * t + j].start()

    def finish():
        mine, first, passed = lists()
        for t in range(n):
            copy(t, 0, sibling, me).wait_recv()
            for j, chip in enumerate(chips):
                copy(t, 4 + j, (*chip, 1 - c), me).wait_recv()
        for cp in first + passed:
            cp.wait_send()
        for cp in mine:
            cp.wait()

    return start, forward, finish


def _gather_scratch(n):
    return [pltpu.SemaphoreType.DMA((7 * n,)), pltpu.SemaphoreType.DMA((7 * n,)), pltpu.SemaphoreType.DMA((n,))]


_FAR_CHIP_FIRST = (2, 0, 1)
_SUMS_SENT_AT_ONCE = 2


def _a_w_in_grad_order():
    x, y, c, chips = _place()
    slots = []
    for j in _FAR_CHIP_FIRST:
        slots += [_slot(*chips[j], 1 - c), _slot(*chips[j], c)]
    slots += [_slot(x, y, 1 - c), _slot(x, y, c)]
    return jnp.stack(slots).astype(jnp.int32)


def _grad_a_w_in_reduce(order, a, dqg, dkv, small):
    t, dw = a.shape
    tn = dqg.shape[2] // 2
    n_s = len(small)
    n_far, n_sent = len(_FAR_CHIP_FIRST), _SUMS_SENT_AT_ONCE

    def body(order_ref, a_ref, dqg_ref, dkv_ref, *rest):
        small_refs, rest = rest[:n_s], rest[n_s:]
        own_ref, sib_ref, chips_ref, later_ref = rest[:4]
        small_out, rest = rest[4:4 + n_s], rest[4 + n_s:]
        a_buf, at_ref, res_ref, stage, land, load_sem, d2d_send, d2d_recv, ici_send, ici_recv = rest[:10]
        small_lands, (small_send, small_recv) = rest[10:10 + n_s], rest[10 + n_s:]
        s = pl.program_id(0)
        x, y, c, chips = _place()

        def to_sibling(i):
            return pltpu.make_async_remote_copy(
                src_ref=stage.at[i], dst_ref=land.at[i] if i < n_far else sib_ref.at[0],
                send_sem=d2d_send.at[i], recv_sem=d2d_recv.at[i], device_id=(x, y, 1 - c), device_id_type=MESH)

        def to_chip(i):
            return pltpu.make_async_remote_copy(
                src_ref=land.at[i], dst_ref=chips_ref.at[i], send_sem=ici_send.at[i], recv_sem=ici_recv.at[i],
                device_id=(*chips[_FAR_CHIP_FIRST[i]], c), device_id_type=MESH)

        @pl.when(s == 0)
        def _():
            load = pltpu.make_async_copy(a_ref, a_buf, load_sem)
            load.start()
            _all_reduce_small_start(small_refs, small_lands, small_send, small_recv)
            load.wait()
            at_ref[...] = a_buf[...].T

        blk = order_ref[s]
        from_qg = jnp.logical_or(blk < 2, blk >= 6)

        @pl.when(from_qg)
        def _():
            res_ref[...] = _dot(at_ref[...], dqg_ref[0])

        @pl.when(jnp.logical_not(from_qg))
        def _():
            res_ref[...] = _dot(at_ref[...], dkv_ref[0])

        for i in range(n_far + 1):
            @pl.when(s == 2 * i)
            def _(i=i):
                stage[i] = res_ref[...].astype(BF16)
                to_sibling(i).start()

        for i in range(n_far):
            @pl.when(s == 2 * i + 1)
            def _(i=i):
                to_sibling(i).wait_recv()
                total = (res_ref[...] + land[i].astype(F32)).astype(BF16)
                if i < n_sent:
                    land[i] = total
                    to_chip(i).start()
                else:
                    later_ref[i - n_sent] = total

        @pl.when(s == N_DEV - 1)
        def _():
            own_ref[...] = res_ref[...]
            _all_reduce_small_finish(small_refs, small_out, small_lands, small_send, small_recv)
            for i in range(n_far + 1):
                to_sibling(i).wait_send()
            to_sibling(n_far).wait_recv()
            for i in range(n_sent):
                to_chip(i).wait()

    whole = pl.BlockSpec(memory_space=pltpu.VMEM)
    outs = pl.pallas_call(
        body, name="grad_a_w_in",
        grid_spec=pltpu.PrefetchScalarGridSpec(
            num_scalar_prefetch=1, grid=(N_DEV,),
            in_specs=[ANY,
                      pl.BlockSpec((1, t, tn), lambda s, o: (o[s] // 6, 0, o[s] % 2)),
                      pl.BlockSpec((1, t, tn), lambda s, o: ((o[s] // 4) % 2, 0, o[s] % 2))] + [whole] * n_s,
            out_specs=[pl.BlockSpec((dw, tn), lambda s, o: (0, 0)), ANY, ANY, whole] + [whole] * n_s,
            scratch_shapes=[pltpu.VMEM((t, dw), BF16), pltpu.VMEM((dw, t), BF16), pltpu.VMEM((dw, tn), F32),
                            pltpu.VMEM((n_far + 1, dw, tn), BF16), pltpu.VMEM((n_far, dw, tn), BF16),
                            pltpu.SemaphoreType.DMA,
                            pltpu.SemaphoreType.DMA((n_far + 1,)), pltpu.SemaphoreType.DMA((n_far + 1,)),
                            pltpu.SemaphoreType.DMA((n_sent,)), pltpu.SemaphoreType.DMA((n_sent,))]
            + _all_reduce_scratch([s.shape for s in small])),
        out_shape=[SDS((dw, tn), F32), SDS((1, dw, tn), BF16), SDS((n_sent, dw, tn), BF16),
                   SDS((n_far - n_sent, dw, tn), BF16)] + [SDS(s.shape, F32) for s in small],
        compiler_params=_cparams(),
    )(order, a, dqg, dkv, *small)
    return outs[0], outs[1], outs[2], outs[3], list(outs[4:])


def _later_chip_copies(sums_ref, land_ref, send_sems, recv_sems):
    x, y, c, chips = _place()
    del x, y
    return [pltpu.make_async_remote_copy(
        src_ref=sums_ref.at[i], dst_ref=land_ref.at[i], send_sem=send_sems.at[i], recv_sem=recv_sems.at[i],
        device_id=(*chips[j], c), device_id_type=MESH) for i, j in enumerate(_FAR_CHIP_FIRST[_SUMS_SENT_AT_ONCE:])]


def _row_tile(rows):
    return min(rows, 512)


def _adamw(w, g, m, v):
    m2 = ADAM_B1 * m + (1.0 - ADAM_B1) * g
    v2 = ADAM_B2 * v + (1.0 - ADAM_B2) * jnp.square(g)
    m_hat = m2 / (1.0 - ADAM_B1 ** ADAM_STEP)
    v_hat = v2 / (1.0 - ADAM_B2 ** ADAM_STEP)
    delta = -ADAM_LR * (m_hat / (jnp.sqrt(v_hat) + ADAM_EPS) + ADAM_WD * w)
    return delta, m2, v2


def _reduce_adamw(name, own, partials, w, m, v):
    r, c = own.shape
    tr = _row_tile(r)
    n_p = len(partials)

    def body(own_ref, *rest):
        p_refs, (w_ref, m_ref, v_ref, grad_ref, d_ref, nm_ref, nv_ref) = rest[:n_p], rest[n_p:]
        grad = own_ref[...]
        for p_ref, (_, _, count) in zip(p_refs, partials):
            for j in range(count):
                grad = grad + p_ref[j].astype(F32)
        grad_ref[...] = grad
        d_ref[...], nm_ref[...], nv_ref[...] = _adamw(w_ref[...], grad, m_ref[...], v_ref[...])

    flat = pl.BlockSpec((tr, c), lambda i: (i, 0))
    return pl.pallas_call(
        body, name=name, grid=(r // tr,),
        in_specs=[flat] + [pl.BlockSpec((count, tr, c), lambda i, first=first, count=count: (first // count, i, 0))
                           for _, first, count in partials] + [flat, flat, flat],
        out_specs=[flat, flat, flat, flat],
        out_shape=[SDS((r, c), F32)] * 4,
        compiler_params=_cparams(),
    )(own, *[p[0] for p in partials], w, m, v)


VM = pl.BlockSpec()


def _all_reduce_small(ins, outs, lands, send_sems, recv_sems):
    _all_reduce_small_start(ins, lands, send_sems, recv_sems)
    _all_reduce_small_finish(ins, outs, lands, send_sems, recv_sems)


def _all_reduce_small_sends(ins, lands, send_sems, recv_sems):
    x, y, c, _ = _place()
    return [pltpu.make_async_remote_copy(
        src_ref=src, dst_ref=land.at[_slot(x, y, c)],
        send_sem=send_sems.at[(N_DEV - 1) * t + k - 1], recv_sem=recv_sems.at[(N_DEV - 1) * t + k - 1],
        device_id=_peer(x, y, c, k), device_id_type=MESH)
        for t, (src, land) in enumerate(zip(ins, lands)) for k in range(1, N_DEV)]


def _all_reduce_small_start(ins, lands, send_sems, recv_sems):
    x, y, c, _ = _place()
    for src, land in zip(ins, lands):
        land[_slot(x, y, c)] = src[...]
    for cp in _all_reduce_small_sends(ins, lands, send_sems, recv_sems):
        cp.start()


def _all_reduce_small_finish(ins, outs, lands, send_sems, recv_sems):
    x, y, c, _ = _place()
    copies = _all_reduce_small_sends(ins, lands, send_sems, recv_sems)
    for t, (src, land) in enumerate(zip(ins, lands)):
        for k in range(1, N_DEV):
            sem = (N_DEV - 1) * t + k - 1
            pltpu.make_async_remote_copy(
                src_ref=src, dst_ref=land.at[_slot(*_peer(x, y, c, k))],
                send_sem=send_sems.at[sem], recv_sem=recv_sems.at[sem],
                device_id=(x, y, c), device_id_type=MESH).wait_recv()
    for cp in copies:
        cp.wait_send()
    for out, land in zip(outs, lands):
        total = land[0]
        for s in range(1, N_DEV):
            total = total + land[s]
        out[...] = total


def _all_reduce_scratch(shapes):
    n_sems = (N_DEV - 1) * len(shapes)
    return ([pltpu.VMEM((N_DEV, *s), F32) for s in shapes]
            + [pltpu.SemaphoreType.DMA((n_sems,)), pltpu.SemaphoreType.DMA((n_sems,))])


def _small_adamw(my_slot, sums, ws, ms, vs):
    n = len(ws)

    def body(slot_ref, *refs):
        sum_refs, refs = refs[:n + 1], refs[n + 1:]
        w_refs, m_refs, v_refs, refs = refs[:n], refs[n:2 * n], refs[2 * n:3 * n], refs[3 * n:]
        g_refs, d_refs, nm_refs, nv_refs = refs[:n + 1], refs[n + 1:2 * n + 1], refs[2 * n + 1:3 * n + 1], refs[3 * n + 1:]
        for t in range(n + 1):
            if t == 0:
                g = sum_refs[0][:, pl.ds(pl.multiple_of(slot_ref[0] * 128, 128), 128)]
            else:
                g = sum_refs[t][...]
            g_refs[t][...] = g
            if t < n:
                d_refs[t][...], nm_refs[t][...], nv_refs[t][...] = _adamw(w_refs[t][...], g, m_refs[t][...], v_refs[t][...])

    shapes = [SDS(w.shape, F32) for w in ws]
    outs = pl.pallas_call(
        body, name="small_adamw",
        in_specs=[pl.BlockSpec(memory_space=pltpu.SMEM)] + [VM] * (4 * n + 1),
        out_specs=[VM] * (4 * n + 1),
        out_shape=shapes + [SDS(sums[-1].shape, F32)] + shapes * 3,
    )(my_slot, *sums, *ws, *ms, *vs)
    return outs[:n + 1], outs[n + 1:2 * n + 1], outs[2 * n + 1:3 * n + 1], outs[3 * n + 1:]


def kernel(x, a_norm, a_w_in, a_rel_bias, a_w_out, kv_norm, kv_w, t5_bias, b_norm, b_w_in, b_sinks, b_w_out, final_norm, loss_target, m_a_norm, m_a_w_in, m_a_rel_bias, m_a_w_out, m_kv_norm, m_kv_w, m_t5_bias, m_b_norm, m_b_w_in, m_b_sinks, m_b_w_out, m_final_norm, v_a_norm, v_a_w_in, v_a_rel_bias, v_a_w_out, v_kv_norm, v_kv_w, v_t5_bias, v_b_norm, v_b_w_in, v_b_sinks, v_b_w_out, v_final_norm):
    xi, yi, ci = lax.axis_index("x"), lax.axis_index("y"), lax.axis_index("c")
    my_slot = _slot(xi, yi, ci)

    slot_arr = jnp.reshape(my_slot, (1,)).astype(jnp.int32)
    order = _gather_order(xi, yi, ci)
    late_shards = [b_w_in[0].astype(BF16), a_w_out[0].astype(BF16), b_w_out[0].astype(BF16), kv_w.astype(BF16)]
    grad_x, loc, matrices = _local_step(
        slot_arr, order, x[0], loss_target[0], a_norm, a_w_in[0].astype(BF16), a_rel_bias[0], late_shards,
        kv_norm.reshape(1, D_MODEL), t5_bias, b_norm, b_sinks, final_norm.reshape(1, D_MODEL))

    shard_w = dict(a_w_in=a_w_in[0], b_w_in=b_w_in[0], a_w_out=a_w_out[0], b_w_out=b_w_out[0], kv_w=kv_w)
    shard_m = dict(a_w_in=m_a_w_in[0], b_w_in=m_b_w_in[0], a_w_out=m_a_w_out[0], b_w_out=m_b_w_out[0], kv_w=m_kv_w)
    shard_v = dict(a_w_in=v_a_w_in[0], b_w_in=v_b_w_in[0], a_w_out=v_a_w_out[0], b_w_out=v_b_w_out[0], kv_w=v_kv_w)
    big = {n: _reduce_adamw("adamw_" + n, own, partials, shard_w[n], shard_m[n], shard_v[n])
           for n, (own, partials) in matrices.items()}

    names = ("a_norm", "a_rel_bias", "kv_norm", "t5_bias", "b_norm", "b_sinks", "final_norm")
    tables = ("a_rel_bias", "t5_bias")

    def row(n, a):
        return a.reshape(-1, a.shape[-1]).T if n in tables else a.reshape(1, -1)

    small_w = [row(n, a) for n, a in zip(names, (a_norm, a_rel_bias, kv_norm, t5_bias, b_norm, b_sinks, final_norm))]
    small_m = [row(n, a) for n, a in zip(names, (m_a_norm, m_a_rel_bias, m_kv_norm, m_t5_bias, m_b_norm, m_b_sinks,
                                                 m_final_norm))]
    small_v = [row(n, a) for n, a in zip(names, (v_a_norm, v_a_rel_bias, v_kv_norm, v_t5_bias, v_b_norm, v_b_sinks,
                                                 v_final_norm))]
    sums = dict(loc)
    sums["a_rel_bias"] = _a_bias_grad(sums["a_rel_bias"])
    sums["t5_bias"] = sums["t5_bias"][:, :T5_BUCKETS]
    sums["b_sinks"] = sums["b_sinks"][:, 0].reshape(1, N_HEADS)
    results = _small_adamw(slot_arr, [sums[n] for n in names + ("loss",)], small_w, small_m, small_v)
    like = dict(a_norm=a_norm, a_rel_bias=a_rel_bias, kv_norm=kv_norm, t5_bias=t5_bias, b_norm=b_norm,
                b_sinks=b_sinks, final_norm=final_norm)
    sm = [{n: (part[i].T if n in tables else part[i]).reshape(like[n].shape) for i, n in enumerate(names)}
          for part in results]
    loss = results[0][len(names)][0, 0]

    order = ("a_norm", "a_w_in", "a_rel_bias", "a_w_out", "kv_norm", "kv_w", "t5_bias", "b_norm",
             "b_w_in", "b_sinks", "b_w_out", "final_norm")
    lead = dict(a_w_in=True, b_w_in=True, a_w_out=True, b_w_out=True, kv_w=False)

    def pick(kind, name):
        if name in big:
            val = big[name][kind]
            return val[None] if lead[name] else val
        return sm[kind][name]

    outs = [loss, grad_x[None]]
    for kind in range(4):
        outs += [pick(kind, n) for n in order]
    return tuple(outs)
```

```python
import functools
import math

import numpy as np
import jax
import jax.numpy as jnp
from jax import lax
from jax.experimental import pallas as pl
from jax.experimental.pallas import tpu as pltpu

F32 = jnp.float32
BF16 = jnp.bfloat16
SDS = jax.ShapeDtypeStruct

D_MODEL = 1024
HEAD_DIM = 64
CHUNK = 64
N_HEADS = 16
RMS_EPS = 1e-6
A_LEFT_CHUNKS = 8
A_BAND = (A_LEFT_CHUNKS + 1) * CHUNK
A_REL_CLIP = 256
B_KV_HEADS = 2
B_GROUP = 8
B_LEFT_CHUNKS = 2
B_BAND = (B_LEFT_CHUNKS + 1) * CHUNK
T5_BUCKETS = 32
T5_MAX_DIST = 128
QBLK = 256
A_KEYS = 3 * QBLK
B_QBLK_FWD = 128
B_QBLK_BWD = 256
B_PREV = 128
A_DIAG = A_KEYS
NEG = -1e30
SCALE = HEAD_DIM ** -0.5
N_DEV = 8

ADAM_LR = 0.001
ADAM_B1 = 0.9
ADAM_B2 = 0.999
ADAM_EPS = 1e-08
ADAM_WD = 0.01
ADAM_STEP = 10

VMEM_LIMIT_BYTES = 56 * 1024 * 1024
INPUT_RING_SLOTS = 3
MESH = pl.DeviceIdType.MESH


def _cparams():
    return pltpu.CompilerParams(vmem_limit_bytes=VMEM_LIMIT_BYTES)


def _dot(a, b):
    return jnp.dot(a, b, preferred_element_type=F32)


def _dot_nt(a, b):
    return lax.dot_general(a, b, (((1,), (1,)), ((), ())), preferred_element_type=F32)


def _dot_tn(a, b):
    return lax.dot_general(a, b, (((0,), (0,)), ((), ())), preferred_element_type=F32)


def _rstd(xf):
    return lax.rsqrt(jnp.mean(xf * xf, axis=-1, keepdims=True) + RMS_EPS)


def _sigmoid(x):
    return 1.0 / (1.0 + jnp.exp(-x))


_GATHER_SEQUENCE = ((0, None), (1, 0), (2, 1), (4, None), (5, None), (3, 2), (6, None))


def _gather_order(x, y, c):
    others = [(1 - x, y), (x, 1 - y), (1 - x, 1 - y)]
    arrivals = [_slot(x, y, 1 - c)] + [_slot(*chip, c) for chip in others] + [_slot(*chip, 1 - c) for chip in others]
    return jnp.stack([_slot(x, y, c)] + [arrivals[a] for a, _ in _GATHER_SEQUENCE]).astype(jnp.int32)


def _norm_matmul_gather(order, x, gain_shard, w_shard):
    t = x.shape[0]
    dw, tn = w_shard.shape
    tm = min(t, 2048)
    n_m = t // tm

    def body(order_ref, x_ref, gs_ref, shard_ref, xn_ref, o_ref, full_ref, gain_ref,
             xn_all, wbuf, gland, send_sems, recv_sems, gsend_sems, grecv_sems, load_sems, own_sem):
        n, m = pl.program_id(0), pl.program_id(1)
        x_i, y_i, c_i, chips = _place()
        me, sibling = (x_i, y_i, c_i), (x_i, y_i, 1 - c_i)

        def send(k, block, to, src=None):
            dst = full_ref.at[_slot(*block)]
            return pltpu.make_async_remote_copy(
                src_ref=dst if src is None else src, dst_ref=dst,
                send_sem=send_sems.at[k], recv_sem=recv_sems.at[k], device_id=to, device_id_type=MESH)

        own = pltpu.make_async_copy(shard_ref, full_ref.at[_slot(*me)], own_sem)
        first = [send(0, me, sibling, src=shard_ref)]
        first += [send(1 + j, me, (*chip, c_i), src=shard_ref) for j, chip in enumerate(chips)]
        forwards = [send(4 + j, (*chip, c_i), sibling) for j, chip in enumerate(chips)]
        arrivals = [send(0, sibling, me)] + [send(1 + j, (*chip, c_i), me) for j, chip in enumerate(chips)]
        arrivals += [send(4 + j, (*chip, 1 - c_i), me) for j, chip in enumerate(chips)]
        gains = [pltpu.make_async_remote_copy(
            src_ref=gs_ref, dst_ref=gland.at[_slot(*me)], send_sem=gsend_sems.at[k - 1],
            recv_sem=grecv_sems.at[k - 1], device_id=_peer(x_i, y_i, c_i, k), device_id_type=MESH)
            for k in range(1, N_DEV)]

        @pl.when(jnp.logical_and(n == 0, m == 0))
        def _():
            own.start()
            for cp in gains + first:
                cp.start()
            pltpu.make_async_copy(shard_ref, wbuf.at[0], load_sems.at[0]).start()
            gland[_slot(*me)] = gs_ref[...]
            for k in range(1, N_DEV):
                pltpu.make_async_remote_copy(
                    src_ref=gs_ref, dst_ref=gland.at[_slot(*_peer(x_i, y_i, c_i, k))],
                    send_sem=gsend_sems.at[k - 1], recv_sem=grecv_sems.at[k - 1],
                    device_id=me, device_id_type=MESH).wait_recv()
            for s in range(N_DEV):
                gain_ref[:, 128 * s:128 * (s + 1)] = gland[s]

        rows = pl.ds(pl.multiple_of(m * tm, tm), tm)

        @pl.when(n == 0)
        def _():
            xf = x_ref[...]
            xn = ((xf * _rstd(xf)) * gain_ref[...]).astype(BF16)
            xn_all[rows, :] = xn
            xn_ref[...] = xn

        @pl.when(m == 0)
        def _():
            pltpu.make_async_copy(full_ref.at[0], wbuf.at[n % 2], load_sems.at[n % 2]).wait()

        o_ref[...] = _dot(xn_all[rows, :], wbuf[n % 2]).astype(BF16)

        for k, (arrival, forward) in enumerate(_GATHER_SEQUENCE):
            @pl.when(jnp.logical_and(n == k, m == n_m - 1))
            def _(k=k, arrival=arrival, forward=forward):
                arrivals[arrival].wait_recv()
                if forward is not None:
                    forwards[forward].start()
                pltpu.make_async_copy(full_ref.at[order_ref[k + 1]], wbuf.at[(k + 1) % 2],
                                      load_sems.at[(k + 1) % 2]).start()

        @pl.when(jnp.logical_and(n == N_DEV - 1, m == n_m - 1))
        def _():
            for cp in gains + first + forwards:
                cp.wait_send()
            own.wait()

    held = lambda n, m, order: (jnp.where(n == 0, m, n_m - 1), 0)
    return pl.pallas_call(
        body, name="norm_matmul_gather",
        grid_spec=pltpu.PrefetchScalarGridSpec(
            num_scalar_prefetch=1, grid=(N_DEV, n_m),
            in_specs=[pl.BlockSpec((tm, D_MODEL), held),
                      pl.BlockSpec((1, 128), lambda n, m, order: (0, 0)), ANY],
            out_specs=[pl.BlockSpec((tm, D_MODEL), held),
                       pl.BlockSpec((tm, tn), lambda n, m, order: (m, order[n])),
                       ANY, pl.BlockSpec((1, D_MODEL), lambda n, m, order: (0, 0))],
            scratch_shapes=[pltpu.VMEM((t, D_MODEL), BF16), pltpu.VMEM((2, dw, tn), BF16),
                            pltpu.VMEM((N_DEV, 1, 128), F32),
                            pltpu.SemaphoreType.DMA((7,)), pltpu.SemaphoreType.DMA((7,)),
                            pltpu.SemaphoreType.DMA((7,)), pltpu.SemaphoreType.DMA((7,)),
                            pltpu.SemaphoreType.DMA((2,)), pltpu.SemaphoreType.DMA]),
        out_shape=[SDS((t, D_MODEL), BF16), SDS((t, N_DEV * tn), BF16), SDS((N_DEV, dw, tn), BF16),
                   SDS((1, D_MODEL), F32)],
        compiler_params=_cparams(),
    )(order, x, gain_shard, w_shard)


def _layer_a_out(x, z, w_out, kv_gain, b_gain, kv_w, w_in_b):
    t = x.shape[0]
    tm = min(t, 1024)
    nb, _, tn = w_in_b.shape

    def body(x_ref, z_ref, wo_ref, kvg_ref, bg_ref, kvw_ref, wb_ref,
             h1_ref, kvn_ref, hb_ref, kv_ref, qg_ref):
        h1 = x_ref[...] + _dot(z_ref[...], wo_ref[...])
        h1_ref[...] = h1
        y0 = h1 * _rstd(h1)
        kvn = (y0 * kvg_ref[...]).astype(BF16)
        hb = (y0 * bg_ref[...]).astype(BF16)
        kvn_ref[...] = kvn
        hb_ref[...] = hb
        kv_ref[...] = _dot(kvn, kvw_ref[...]).astype(BF16)
        for i in range(nb):
            qg_ref[:, i * tn:(i + 1) * tn] = _dot(hb, wb_ref[i]).astype(BF16)

    row = lambda m: (m, 0)
    fix2 = lambda m: (0, 0)
    return pl.pallas_call(
        body, name="layer_a_out", grid=(t // tm,),
        in_specs=[pl.BlockSpec((tm, D_MODEL), row), pl.BlockSpec((tm, D_MODEL), row),
                  pl.BlockSpec((D_MODEL, D_MODEL), fix2),
                  pl.BlockSpec((1, D_MODEL), fix2), pl.BlockSpec((1, D_MODEL), fix2),
                  pl.BlockSpec((D_MODEL, 256), fix2),
                  pl.BlockSpec((nb, D_MODEL, tn), lambda m: (0, 0, 0))],
        out_specs=[pl.BlockSpec((tm, D_MODEL), row), pl.BlockSpec((tm, D_MODEL), row),
                   pl.BlockSpec((tm, D_MODEL), row), pl.BlockSpec((tm, 256), row),
                   pl.BlockSpec((tm, nb * tn), row)],
        out_shape=[SDS((t, D_MODEL), F32), SDS((t, D_MODEL), BF16), SDS((t, D_MODEL), BF16),
                   SDS((t, 256), BF16), SDS((t, nb * tn), BF16)],
        compiler_params=_cparams(),
    )(x, z, w_out, kv_gain, b_gain, kv_w, w_in_b)


def _layer_b_out_loss(h1, z, w_out, f_gain, target):
    t = h1.shape[0]
    tm = min(t, 1024)

    def body(h1_ref, z_ref, wo_ref, fg_ref, tgt_ref,
             dh2_ref, dh2b_ref, dz_ref, loss_ref, dfn_ref):
        @pl.when(pl.program_id(0) == 0)
        def _():
            loss_ref[...] = jnp.zeros_like(loss_ref)
            dfn_ref[...] = jnp.zeros_like(dfn_ref)

        h2 = h1_ref[...] + _dot(z_ref[...], wo_ref[...])
        r = _rstd(h2)
        yn = h2 * r
        fg = fg_ref[...]
        err = yn * fg - tgt_ref[...]
        loss_ref[...] += (0.5 / D_MODEL) * jnp.sum(err * err)
        dy = err * (1.0 / D_MODEL)
        dfn_ref[...] += jnp.sum(dy * yn, axis=0, keepdims=True)
        u = dy * fg
        dh2 = r * u - h2 * ((r * r * r) * jnp.mean(u * h2, axis=-1, keepdims=True))
        dh2_ref[...] = dh2
        dh2b = dh2.astype(BF16)
        dh2b_ref[...] = dh2b
        dz_ref[...] = _dot_nt(dh2b, wo_ref[...]).astype(BF16)

    row = lambda m: (m, 0)
    fix2 = lambda m: (0, 0)
    return pl.pallas_call(
        body, name="layer_b_out_loss", grid=(t // tm,),
        in_specs=[pl.BlockSpec((tm, D_MODEL), row), pl.BlockSpec((tm, D_MODEL), row),
                  pl.BlockSpec((D_MODEL, D_MODEL), fix2), pl.BlockSpec((1, D_MODEL), fix2),
                  pl.BlockSpec((tm, D_MODEL), row)],
        out_specs=[pl.BlockSpec((tm, D_MODEL), row), pl.BlockSpec((tm, D_MODEL), row),
                   pl.BlockSpec((tm, D_MODEL), row), pl.BlockSpec((1, 128), fix2),
                   pl.BlockSpec((1, D_MODEL), fix2)],
        out_shape=[SDS((t, D_MODEL), F32), SDS((t, D_MODEL), BF16), SDS((t, D_MODEL), BF16),
                   SDS((1, 128), F32), SDS((1, D_MODEL), F32)],
        compiler_params=_cparams(),
    )(h1, z, w_out, f_gain, target)


def _layer_b_in_bwd(dqg, dkv, w_in_b, kv_w, h1, dh2, b_gain, kv_gain, w_out_a):
    t = h1.shape[0]
    tm = min(t, 512)
    nb, _, tn = w_in_b.shape
    per = D_MODEL // tn

    def body(dqg_ref, dkv_ref, wb_ref, kvw_ref, h1_ref, dh2_ref, bg_ref, kvg_ref, wo_ref,
             dh1_ref, dh1b_ref, dz_ref, dbn_ref, dkn_ref):
        @pl.when(pl.program_id(0) == 0)
        def _():
            dbn_ref[...] = jnp.zeros_like(dbn_ref)
            dkn_ref[...] = jnp.zeros_like(dkn_ref)

        dhb = jnp.zeros((tm, D_MODEL), F32)
        for i in range(nb):
            blk = dqg_ref[i // per, :, (i % per) * tn:(i % per + 1) * tn]
            dhb = dhb + _dot_nt(blk, wb_ref[i])
        dkn = (_dot_nt(dkv_ref[0].astype(BF16), kvw_ref[:, 0:128])
               + _dot_nt(dkv_ref[1].astype(BF16), kvw_ref[:, 128:256]))
        h1 = h1_ref[...]
        r = _rstd(h1)
        xr = h1 * r
        dbn_ref[...] += jnp.sum(dhb * xr, axis=0, keepdims=True)
        dkn_ref[...] += jnp.sum(dkn * xr, axis=0, keepdims=True)
        u = dhb * bg_ref[...] + dkn * kvg_ref[...]
        dh1 = dh2_ref[...] + r * u - h1 * ((r * r * r) * jnp.mean(u * h1, axis=-1, keepdims=True))
        dh1_ref[...] = dh1
        dh1b = dh1.astype(BF16)
        dh1b_ref[...] = dh1b
        dz_ref[...] = _dot_nt(dh1b, wo_ref[...]).astype(BF16)

    row = lambda m: (m, 0)
    fix2 = lambda m: (0, 0)
    return pl.pallas_call(
        body, name="layer_b_in_bwd", grid=(t // tm,),
        in_specs=[pl.BlockSpec((2, tm, D_MODEL), lambda m: (0, m, 0)),
                  pl.BlockSpec((2, tm, 128), lambda m: (0, m, 0)),
                  pl.BlockSpec((nb, D_MODEL, tn), lambda m: (0, 0, 0)),
                  pl.BlockSpec((D_MODEL, 256), fix2),
                  pl.BlockSpec((tm, D_MODEL), row), pl.BlockSpec((tm, D_MODEL), row),
                  pl.BlockSpec((1, D_MODEL), fix2), pl.BlockSpec((1, D_MODEL), fix2),
                  pl.BlockSpec((D_MODEL, D_MODEL), fix2)],
        out_specs=[pl.BlockSpec((tm, D_MODEL), row), pl.BlockSpec((tm, D_MODEL), row),
                   pl.BlockSpec((tm, D_MODEL), row), pl.BlockSpec((1, D_MODEL), fix2),
                   pl.BlockSpec((1, D_MODEL), fix2)],
        out_shape=[SDS((t, D_MODEL), F32), SDS((t, D_MODEL), BF16), SDS((t, D_MODEL), BF16),
                   SDS((1, D_MODEL), F32), SDS((1, D_MODEL), F32)],
        compiler_params=_cparams(),
    )(dqg, dkv, w_in_b, kv_w, h1, dh2, b_gain, kv_gain, w_out_a)


def _layer_a_in_bwd(dqg, dkv, w_in_a, x, dh1, a_gain, chip_sums):
    t = x.shape[0]
    tm = min(t, 512)
    nb, _, tn = w_in_a.shape
    per = D_MODEL // tn
    n_sums = chip_sums.shape[0]
    n_steps = t // tm
    ring = min(INPUT_RING_SLOTS, n_steps)

    def body(dqg_ref, dkv_ref, w_ref, x_ref, dh1_ref, ag_ref, sums_ref, dx_ref, dan_ref, land_ref,
             dqg_buf, dkv_buf, x_buf, dh1_buf, ring_sems,
             send_sems, recv_sems, dan_land, dan_send, dan_recv):
        s = pl.program_id(0)

        def fetch(step):
            slot = step % ring
            rows = pl.ds(pl.multiple_of(step * tm, tm), tm)
            return [pltpu.make_async_copy(dqg_ref.at[:, rows, :], dqg_buf.at[slot], ring_sems.at[0, slot]),
                    pltpu.make_async_copy(dkv_ref.at[:, rows, :], dkv_buf.at[slot], ring_sems.at[1, slot]),
                    pltpu.make_async_copy(x_ref.at[rows, :], x_buf.at[slot], ring_sems.at[2, slot]),
                    pltpu.make_async_copy(dh1_ref.at[rows, :], dh1_buf.at[slot], ring_sems.at[3, slot])]

        @pl.when(s == 0)
        def _():
            for ahead in range(ring - 1):
                for cp in fetch(ahead):
                    cp.start()
            dan_ref[...] = jnp.zeros_like(dan_ref)
            for cp in _later_chip_copies(sums_ref, land_ref, send_sems, recv_sems):
                cp.start()

        @pl.when(s + ring - 1 < n_steps)
        def _():
            for cp in fetch(s + ring - 1):
                cp.start()

        for cp in fetch(s):
            cp.wait()
        slot = s % ring

        dxn = jnp.zeros((tm, D_MODEL), F32)
        for i in range(nb):
            part = i // per
            src = dqg_buf if part in (0, 3) else dkv_buf
            outer = {0: 0, 3: 1, 1: 0, 2: 1}[part]
            blk = src[slot, outer, :, (i % per) * tn:(i % per + 1) * tn]
            dxn = dxn + _dot_nt(blk, w_ref[i])
        xf = x_buf[slot]
        r = _rstd(xf)
        dan_ref[...] += jnp.sum(dxn * (xf * r), axis=0, keepdims=True)
        u = dxn * ag_ref[...]
        dx_ref[...] = dh1_buf[slot] + r * u - xf * ((r * r * r) * jnp.mean(u * xf, axis=-1, keepdims=True))

        @pl.when(pl.program_id(0) == t // tm - 1)
        def _():
            _all_reduce_small([dan_ref], [dan_ref], [dan_land], dan_send, dan_recv)
            for cp in _later_chip_copies(sums_ref, land_ref, send_sems, recv_sems):
                cp.wait()

    row = lambda m: (m, 0)
    fix2 = lambda m: (0, 0)
    return pl.pallas_call(
        body, name="layer_a_in_bwd", grid=(t // tm,),
        in_specs=[ANY, ANY, pl.BlockSpec((nb, D_MODEL, tn), lambda m: (0, 0, 0)), ANY, ANY,
                  pl.BlockSpec((1, D_MODEL), fix2), ANY],
        out_specs=[pl.BlockSpec((tm, D_MODEL), row), pl.BlockSpec((1, D_MODEL), fix2), ANY],
        out_shape=[SDS((t, D_MODEL), F32), SDS((1, D_MODEL), F32), SDS(chip_sums.shape, chip_sums.dtype)],
        scratch_shapes=[pltpu.VMEM((ring, 2, tm, D_MODEL), dqg.dtype), pltpu.VMEM((ring, 2, tm, D_MODEL), dkv.dtype),
                        pltpu.VMEM((ring, tm, D_MODEL), F32), pltpu.VMEM((ring, tm, D_MODEL), F32),
                        pltpu.SemaphoreType.DMA((4, ring)),
                        pltpu.SemaphoreType.DMA((n_sums,)), pltpu.SemaphoreType.DMA((n_sums,))]
        + _all_reduce_scratch([(1, D_MODEL)]),
        compiler_params=_cparams(),
    )(dqg, dkv, w_in_a, x, dh1, a_gain, chip_sums)


def _lut(s, vals):
    r = jnp.int32(vals[0])
    for i in range(1, len(vals)):
        r = jnp.where(s == i, jnp.int32(vals[i]), r)
    return r


def _held(steps, i):
    seq, cur = [None] * len(steps), None
    for k in range(len(steps) - 1, -1, -1):
        if steps[k][0] == i:
            cur = steps[k][1:3]
        seq[k] = cur
    for k in range(len(steps)):
        cur = seq[k] = seq[k] if seq[k] is not None else cur
    return seq


def _weight_grad_cols(name, my_slot, a, bs, steps, tn):
    t, dw = a.shape
    n_arr = len(bs)
    which = [s[0] for s in steps]
    blks = [s[3] for s in steps]

    def body(slot_ref, a_ref, *rest):
        b_refs, (o_ref, own_ref, at_ref) = rest[:n_arr], rest[n_arr:]
        s = pl.program_id(0)

        @pl.when(s == 0)
        def _():
            at_ref[...] = a_ref[...].T

        for i in range(n_arr):
            @pl.when(_lut(s, which) == i)
            def _(i=i):
                res = _dot(at_ref[...], b_refs[i][0])
                o_ref[0] = res.astype(BF16)

                @pl.when(_lut(s, blks) == slot_ref[0])
                def _():
                    own_ref[...] = res

    def b_spec(i):
        held = _held(steps, i)
        return pl.BlockSpec((1, t, tn), lambda s, slot: (_lut(s, [h[0] for h in held]), 0,
                                                         _lut(s, [h[1] for h in held])))

    return pl.pallas_call(
        body, name=name,
        grid_spec=pltpu.PrefetchScalarGridSpec(
            num_scalar_prefetch=1, grid=(len(steps),),
            in_specs=[pl.BlockSpec((t, dw), lambda s, slot: (0, 0))] + [b_spec(i) for i in range(n_arr)],
            out_specs=[pl.BlockSpec((1, dw, tn), lambda s, slot: (_lut(s, blks), 0, 0)),
                       pl.BlockSpec((dw, tn), lambda s, slot: (0, 0))],
            scratch_shapes=[pltpu.VMEM((dw, t), BF16)]),
        out_shape=[SDS((N_DEV, dw, tn), BF16), SDS((dw, tn), F32)],
        compiler_params=_cparams(),
    )(my_slot, a, *bs)


def _weight_grad_rows(name, my_slot, a, b):
    t, dw = a.shape
    n_o, _, c = b.shape
    rows = dw // N_DEV
    tn = min(c, 256)
    per = c // tn

    def body(slot_ref, a_ref, b_ref, o_ref, own_ref, at_ref, res_ref):
        @pl.when(pl.program_id(0) == 0)
        def _():
            at_ref[...] = a_ref[...].T

        res_ref[...] = _dot(at_ref[...], b_ref[0].astype(BF16))
        o_ref[...] = res_ref[...].astype(BF16)
        own_ref[...] = res_ref[pl.ds(pl.multiple_of(slot_ref[0] * rows, rows), rows), :]

    all_rows, own = pl.pallas_call(
        body, name=name,
        grid_spec=pltpu.PrefetchScalarGridSpec(
            num_scalar_prefetch=1, grid=(n_o * per,),
            in_specs=[pl.BlockSpec((t, dw), lambda s, slot: (0, 0)),
                      pl.BlockSpec((1, t, tn), lambda s, slot: (s // per, 0, s % per))],
            out_specs=[pl.BlockSpec((dw, tn), lambda s, slot: (0, s)),
                       pl.BlockSpec((rows, tn), lambda s, slot: (0, s))],
            scratch_shapes=[pltpu.VMEM((dw, t), BF16), pltpu.VMEM((dw, tn), F32)]),
        out_shape=[SDS((dw, n_o * c), BF16), SDS((rows, n_o * c), F32)],
        compiler_params=_cparams(),
    )(my_slot, a, b)
    return all_rows.reshape(N_DEV, rows, n_o * c), own


def _lane_lo():
    return lax.broadcasted_iota(jnp.int32, (1, 128), 1) < HEAD_DIM


def _collapse_chunks(ds, keys):
    if ds.shape[1] < keys:
        ds = jnp.concatenate([jnp.zeros((ds.shape[0], keys - ds.shape[1]), F32), ds], axis=1)
    gc = ds[0:CHUNK]
    for cc in range(1, ds.shape[0] // CHUNK):
        gc = gc + pltpu.roll(ds[cc * CHUNK:(cc + 1) * CHUNK], keys - cc * CHUNK, 1)
    return gc


def _offset_sums(gc):
    hi = gc.astype(BF16)
    lo = (gc - hi.astype(F32)).astype(BF16)
    flip = (lax.broadcasted_iota(jnp.int32, (CHUNK, CHUNK), 0)
            + lax.broadcasted_iota(jnp.int32, (CHUNK, CHUNK), 1) == CHUNK - 1).astype(BF16)
    gf = _dot(flip, hi) + _dot(flip, lo)
    skew = pltpu.roll(gf, 0, 1, stride=1, stride_axis=0)
    return jnp.sum(skew, axis=0, keepdims=True)


def _band_bias(w_row, band, rows):
    keys = w_row.shape[1]
    base = jnp.broadcast_to(w_row, (CHUNK, keys))
    skew = pltpu.roll(base, 0, 1, stride=1, stride_axis=0)
    skew = pltpu.roll(skew, keys - (CHUNK - 1), 1)
    col = lax.broadcasted_iota(jnp.int32, (CHUNK, keys), 1)
    chunk0 = jnp.where(col < band, skew, NEG)
    return jnp.concatenate(
        [chunk0] + [pltpu.roll(chunk0, cc * CHUNK, 1) for cc in range(1, rows // CHUNK)], axis=0)


def _silu_parts(g):
    sg = _sigmoid(g)
    return g * sg, sg * (1.0 + g * (1.0 - sg))


A_PAIRS_FWD = 8
A_PAIRS_BWD = 4


def _a_specs(pairs):
    lanes = 128 * pairs
    steps = D_MODEL // lanes
    q = pl.BlockSpec((QBLK, lanes), lambda p, j: (j, p))
    ks = [pl.BlockSpec((QBLK, lanes), lambda p, j, b=b: (jnp.maximum(j - 2 + b, 0), steps + p)) for b in range(3)]
    vs = [pl.BlockSpec((QBLK, lanes), lambda p, j, b=b: (jnp.maximum(j - 2 + b, 0), 2 * steps + p))
          for b in range(3)]
    g = pl.BlockSpec((QBLK, lanes), lambda p, j: (j, 3 * steps + p))
    bias = pl.BlockSpec((pairs, 8, A_KEYS), lambda p, j: (p, 0, 0))
    return q, ks, vs, g, bias


def _a_fill_bias(w_ref, b_ref, j, pairs):
    _fill_bias(2 * pairs, lambda h: w_ref[h // 2, h % 2:h % 2 + 1, :], A_BAND, b_ref, j)


def _by_valid_key_blocks(j, fn):
    pl.when(j == 0)(functools.partial(fn, 1))
    pl.when(j == 1)(functools.partial(fn, 2))
    pl.when(j >= 2)(functools.partial(fn, 3))


def _fill_bias(n, get_row, band, bias_scr, j):
    @pl.when(j == 0)
    def _():
        for h in range(n):
            bias_scr[h] = _band_bias(get_row(h), band, bias_scr.shape[1])


def _normalise_pair(rs, mxs, lane_lo, extra=None):
    num = jnp.where(lane_lo, rs[0], rs[1])
    den = pltpu.roll(jnp.where(lane_lo, rs[1], rs[0]), HEAD_DIM, 1)
    if extra is not None:
        den = den + jnp.where(lane_lo, extra[0], extra[1])
    return num / den, jnp.where(lane_lo, mxs[0], mxs[1]) + jnp.log(den)


def _own_everywhere(x, sel):
    return jnp.where(sel, x, pltpu.roll(x, HEAD_DIM, 1))


def _minus_rows(s, row_full):
    return jnp.concatenate([s[:, i:i + 128] - row_full for i in range(0, s.shape[1], 128)], axis=1)


def _attn_a_fwd(qkvg, bias, gather):
    t = qkvg.shape[0]
    nq = t // QBLK
    n_g = len(gather)
    pairs = A_PAIRS_FWD
    lanes = 128 * pairs
    steps = D_MODEL // lanes
    q_spec, k_specs, v_specs, g_spec, bias_spec = _a_specs(pairs)

    def body(q_ref, k0, k1, k2, v0, v1, v2, g_ref, w_ref, *rest):
        shard_refs, rest = rest[:n_g], rest[n_g:]
        z_ref, o_ref, lse_ref = rest[:3]
        full_refs, (b_ref, *comm) = rest[3:3 + n_g], rest[3 + n_g:]
        p = pl.program_id(0)
        j = pl.program_id(1)
        start, forward, finish = _gather_phases(shard_refs, full_refs, *comm)
        at = p * nq + j
        pl.when(at == 0)(start)
        pl.when(at == steps * nq // 2)(forward)
        _a_fill_bias(w_ref, b_ref, j, pairs)
        lane_lo = _lane_lo()
        sels = (lane_lo, jnp.logical_not(lane_lo))

        def attend(n_blocks):
            first_col = (3 - n_blocks) * QBLK
            for pp in range(pairs):
                cols = slice(128 * pp, 128 * (pp + 1))
                k = jnp.concatenate([r[:, cols] for r in (k0, k1, k2)[3 - n_blocks:]], axis=0)
                v = jnp.concatenate([r[:, cols] for r in (v0, v1, v2)[3 - n_blocks:]], axis=0)
                q = q_ref[:, cols]
                qm2 = jnp.concatenate([jnp.where(sel, q, jnp.zeros_like(q)) for sel in sels], axis=0) * SCALE
                s2 = _dot_nt(qm2, k)
                rs, mxs = [], []
                for hh, sel in enumerate(sels):
                    s = s2[hh * QBLK:(hh + 1) * QBLK] + b_ref[2 * pp + hh, :, first_col:]
                    mxs.append(jnp.max(s, axis=-1, keepdims=True))
                    e = jnp.exp(s - mxs[hh]).astype(BF16)
                    rs.append(_dot(e, jnp.where(sel, v, jnp.ones_like(v))))
                o, lse = _normalise_pair(rs, mxs, lane_lo)
                silu, _ = _silu_parts(g_ref[:, cols].astype(F32))
                o_ref[:, cols] = o.astype(BF16)
                z_ref[:, cols] = (o * silu).astype(BF16)
                lse_ref[:, cols] = lse

        _by_valid_key_blocks(j, attend)
        pl.when(at == steps * nq - 1)(finish)

    out_spec = pl.BlockSpec((QBLK, lanes), lambda p, j: (j, p))
    outs = pl.pallas_call(
        body, name="attn_a_fwd", grid=(steps, nq),
        in_specs=[q_spec, *k_specs, *v_specs, g_spec, bias_spec] + [ANY] * n_g,
        out_specs=[out_spec, out_spec, out_spec] + [ANY] * n_g,
        out_shape=[SDS((t, D_MODEL), BF16), SDS((t, D_MODEL), BF16), SDS((t, D_MODEL), F32)]
        + [SDS((N_DEV, *s.shape), s.dtype) for s in gather],
        scratch_shapes=[pltpu.VMEM((2 * pairs, QBLK, A_KEYS), F32)] + _gather_scratch(n_g),
        compiler_params=_cparams(),
    )(qkvg, qkvg, qkvg, qkvg, qkvg, qkvg, qkvg, qkvg, bias, *gather)
    return outs[0], outs[1], outs[2], list(outs[3:])


def _attn_a_bwd(qkvg, bias, out_a, lse, dz, scatter):
    t = qkvg.shape[0]
    nq = t // QBLK
    n_sc = len(scatter)
    pairs = A_PAIRS_BWD
    lanes = 128 * pairs
    steps = D_MODEL // lanes
    q_spec, k_specs, v_specs, g_spec, bias_spec = _a_specs(pairs)

    def body(q_ref, k0, k1, k2, v0, v1, v2, g_ref, w_ref, o_ref, lse_ref, dz_ref, *rest):
        sc_refs, rest = rest[:n_sc], rest[n_sc:]
        dqg_ref, dkv_ref, dg_ref = rest[:3]
        land_refs, rest = rest[3:3 + n_sc], rest[3 + n_sc:]
        dk_acc, dv_acc, gt_acc, b_ref, send_sems, recv_sems = rest
        j = pl.program_id(1)
        first = jnp.logical_and(pl.program_id(0) == 0, j == 0)
        last = jnp.logical_and(pl.program_id(0) == steps - 1, j == nq - 1)

        @pl.when(first)
        def _():
            for cp in _scatter_copies(sc_refs, land_refs, send_sems, recv_sems):
                cp.start()

        _a_fill_bias(w_ref, b_ref, j, pairs)

        @pl.when(j == 0)
        def _():
            dk_acc[...] = jnp.zeros_like(dk_acc)
            dv_acc[...] = jnp.zeros_like(dv_acc)
            gt_acc[...] = jnp.zeros_like(gt_acc)

        lane_lo = _lane_lo()
        sels = (lane_lo, jnp.logical_not(lane_lo))

        def attend(n_blocks):
            first_col = (3 - n_blocks) * QBLK
            for pp in range(pairs):
                cols = slice(128 * pp, 128 * (pp + 1))
                q = q_ref[:, cols]
                k = jnp.concatenate([r[:, cols] for r in (k0, k1, k2)[3 - n_blocks:]], axis=0)
                v = jnp.concatenate([r[:, cols] for r in (v0, v1, v2)[3 - n_blocks:]], axis=0)
                o = o_ref[:, cols].astype(F32)
                lse_pair = lse_ref[:, cols]
                dzf = dz_ref[:, cols].astype(F32)
                silu, dsilu = _silu_parts(g_ref[:, cols].astype(F32))
                do = dzf * silu
                dqg_ref[1, :, cols] = (dzf * o * dsilu).astype(BF16)
                doo = do * o
                qm2 = jnp.concatenate([jnp.where(sel, q, jnp.zeros_like(q)) for sel in sels], axis=0) * SCALE
                dom2 = jnp.concatenate([jnp.where(sel, do, 0.0) for sel in sels], axis=0).astype(BF16)
                s2 = _dot_nt(qm2, k)
                dp2 = _dot_nt(dom2, v)
                ps, dss = [], []
                for hh, sel in enumerate(sels):
                    rows = slice(hh * QBLK, (hh + 1) * QBLK)
                    s = s2[rows] + b_ref[2 * pp + hh, :, first_col:]
                    p = jnp.exp(_minus_rows(s, _own_everywhere(lse_pair, sel)))
                    delta = jnp.sum(jnp.where(sel, doo, 0.0), axis=-1, keepdims=True)
                    ds = p * (dp2[rows] - delta)
                    gt_acc[2 * pp + hh] += _collapse_chunks(ds, A_KEYS)
                    ps.append(p.astype(BF16))
                    dss.append(ds.astype(BF16))
                dsb2 = jnp.concatenate(dss, axis=0)
                dq2 = _dot(dsb2, k) * SCALE
                dk_blk = _dot_tn(dsb2, qm2)
                dv_blk = _dot_tn(jnp.concatenate(ps, axis=0), dom2)
                dqg_ref[0, :, cols] = jnp.where(lane_lo, dq2[0:QBLK], dq2[QBLK:2 * QBLK]).astype(BF16)
                for b in range(n_blocks):
                    rows = pl.ds(pl.multiple_of((j - n_blocks + 1 + b) * QBLK, QBLK), QBLK)
                    dk_acc[rows, cols] += dk_blk[b * QBLK:(b + 1) * QBLK]
                    dv_acc[rows, cols] += dv_blk[b * QBLK:(b + 1) * QBLK]

        _by_valid_key_blocks(j, attend)

        @pl.when(j == nq - 1)
        def _():
            dkv_ref[0] = dk_acc[...].astype(BF16)
            dkv_ref[1] = dv_acc[...].astype(BF16)
            for pp in range(pairs):
                dg_ref[pp] = jnp.concatenate([_offset_sums(gt_acc[2 * pp]), _offset_sums(gt_acc[2 * pp + 1]),
                                              jnp.zeros((6, A_DIAG), F32)], axis=0)

        @pl.when(last)
        def _():
            for cp in _scatter_copies(sc_refs, land_refs, send_sems, recv_sems):
                cp.wait()

    blk = pl.BlockSpec((QBLK, lanes), lambda p, j: (j, p))
    outs = pl.pallas_call(
        body, name="attn_a_bwd", grid=(steps, nq),
        in_specs=[q_spec, *k_specs, *v_specs, g_spec, bias_spec, blk, blk, blk] + [ANY] * n_sc,
        out_specs=[pl.BlockSpec((2, QBLK, lanes), lambda p, j: (0, j, p)),
                   pl.BlockSpec((2, t, lanes), lambda p, j: (0, 0, p)),
                   pl.BlockSpec((pairs, 8, A_DIAG), lambda p, j: (p, 0, 0))] + [ANY] * n_sc,
        out_shape=[SDS((2, t, D_MODEL), BF16), SDS((2, t, D_MODEL), BF16), SDS((N_HEADS // 2, 8, A_DIAG), F32)]
        + [SDS((N_DEV - 1, *g.shape[1:]), g.dtype) for g in scatter],
        scratch_shapes=[pltpu.VMEM((t, lanes), F32), pltpu.VMEM((t, lanes), F32),
                        pltpu.VMEM((2 * pairs, CHUNK, A_KEYS), F32), pltpu.VMEM((2 * pairs, QBLK, A_KEYS), F32),
                        pltpu.SemaphoreType.DMA(((N_DEV - 1) * n_sc,)),
                        pltpu.SemaphoreType.DMA(((N_DEV - 1) * n_sc,))],
        compiler_params=_cparams(),
    )(qkvg, qkvg, qkvg, qkvg, qkvg, qkvg, qkvg, qkvg, bias, out_a, lse, dz, *scatter)
    return outs[0], outs[1], outs[2], list(outs[3:])


def _b_specs(qblk):
    per = qblk // B_PREV
    q = pl.BlockSpec((qblk, 512), lambda h, j: (j, h))
    g = pl.BlockSpec((qblk, 512), lambda h, j: (j, 2 + h))
    kp = pl.BlockSpec((B_PREV, 128), lambda h, j: (jnp.maximum(per * j - 1, 0), 0))
    kc = pl.BlockSpec((qblk, 128), lambda h, j: (j, 0))
    vp = pl.BlockSpec((B_PREV, 128), lambda h, j: (jnp.maximum(per * j - 1, 0), 1))
    vc = pl.BlockSpec((qblk, 128), lambda h, j: (j, 1))
    bias = pl.BlockSpec((B_GROUP, qblk + B_PREV), lambda h, j: (h, 0))
    sinks = pl.BlockSpec(memory_space=pltpu.SMEM)
    return q, g, kp, kc, vp, vc, bias, sinks


def _b_operands(kp, kc, vp, vc, kvh, with_prev):
    k = jnp.concatenate([kp[...], kc[...]], axis=0) if with_prev else kc[...]
    v = jnp.concatenate([vp[...], vc[...]], axis=0) if with_prev else vc[...]
    kr = pltpu.roll(k, HEAD_DIM, 1)
    vr = pltpu.roll(v, HEAD_DIM, 1)
    first = kvh == 0
    return (jnp.where(first, k, kr), jnp.where(first, kr, k),
            jnp.where(first, v, vr), jnp.where(first, vr, v))


def _attn_b_fwd(qg, kv, bias, sinks):
    t = qg.shape[0]
    qblk = B_QBLK_FWD
    per_step = 4
    step = per_step * qblk
    q_spec, g_spec, kp_spec, kc_spec, vp_spec, vc_spec, _, sink_spec = _b_specs(step)
    bias_spec = pl.BlockSpec((B_GROUP, qblk + B_PREV), lambda h, j: (h, 0))

    def body(q_ref, g_ref, kp, kc, vp, vc, w_ref, sink_ref, z_ref, o_ref, lse_ref, b_ref):
        kvh = pl.program_id(0)
        j = pl.program_id(1)
        _fill_bias(B_GROUP, lambda h: w_ref[h:h + 1, :], B_BAND, b_ref, j)
        lane_lo = _lane_lo()
        n_pairs = B_GROUP // 2

        def attend(first):
            k_lo, k_hi, v_lo, v_hi = _b_operands(kp, kc, vp, vc, kvh, True)
            for sb in range(per_step):
                no_prev = first and sb == 0
                first_col = B_PREV if no_prev else 0
                keys = slice(sb * qblk + first_col, (sb + 1) * qblk + B_PREV)
                qrows = slice(sb * qblk, (sb + 1) * qblk)
                halves = []
                for hh, sel in enumerate((lane_lo, jnp.logical_not(lane_lo))):
                    kk = (k_lo if hh == 0 else k_hi)[keys]
                    vv = (v_lo if hh == 0 else v_hi)[keys]
                    qm4 = jnp.concatenate(
                        [jnp.where(sel, q_ref[qrows, 128 * pp:128 * (pp + 1)], jnp.zeros((qblk, 128), BF16))
                         for pp in range(n_pairs)], axis=0) * SCALE
                    s4 = _dot_nt(qm4, kk)
                    es, mxs = [], []
                    for pp in range(n_pairs):
                        g = 2 * pp + hh
                        s = s4[pp * qblk:(pp + 1) * qblk] + b_ref[g, :, first_col:]
                        mxs.append(jnp.maximum(jnp.max(s, axis=-1, keepdims=True), sink_ref[kvh * B_GROUP + g]))
                        es.append(jnp.exp(s - mxs[pp]).astype(BF16))
                    r4 = _dot(jnp.concatenate(es, axis=0), jnp.where(sel, vv, jnp.ones_like(vv)))
                    halves.append((r4, mxs))
                for pp in range(n_pairs):
                    cols = slice(128 * pp, 128 * (pp + 1))
                    rows = slice(pp * qblk, (pp + 1) * qblk)
                    mxs = [halves[hh][1][pp] for hh in range(2)]
                    sink_terms = [jnp.exp(sink_ref[kvh * B_GROUP + 2 * pp + hh] - mxs[hh]) for hh in range(2)]
                    o, lse = _normalise_pair([halves[hh][0][rows] for hh in range(2)], mxs, lane_lo, sink_terms)
                    silu, _ = _silu_parts(g_ref[qrows, cols].astype(F32))
                    o_ref[qrows, cols] = o.astype(BF16)
                    z_ref[qrows, cols] = (o * silu).astype(BF16)
                    lse_ref[qrows, cols] = lse

        pl.when(j == 0)(functools.partial(attend, True))
        pl.when(j >= 1)(functools.partial(attend, False))

    out_spec = pl.BlockSpec((step, 512), lambda h, j: (j, h))
    return pl.pallas_call(
        body, name="attn_b_fwd", grid=(B_KV_HEADS, t // step),
        in_specs=[q_spec, g_spec, kp_spec, kc_spec, vp_spec, vc_spec, bias_spec, sink_spec],
        out_specs=[out_spec, out_spec, out_spec],
        out_shape=[SDS((t, D_MODEL), BF16), SDS((t, D_MODEL), BF16), SDS((t, D_MODEL), F32)],
        scratch_shapes=[pltpu.VMEM((B_GROUP, qblk, qblk + B_PREV), F32)],
        compiler_params=_cparams(),
    )(qg, qg, kv, kv, kv, kv, bias, sinks)


def _attn_b_bwd(qg, kv, bias, sinks, out_b, lse, dz, bucket_onehot):
    t = qg.shape[0]
    qblk = B_QBLK_BWD
    keys = qblk + B_PREV
    nq = t // qblk
    per = qblk // B_PREV

    def body(q_ref, g_ref, kp, kc, vp, vc, w_ref, sink_ref, o_ref, lse_ref, dz_ref, oh_ref,
             dqg_ref, dkv_ref, dt5_ref, dsink_ref, gt_acc, b_ref):
        j = pl.program_id(0)
        _fill_bias(N_HEADS, lambda h: w_ref[h:h + 1, :], B_BAND, b_ref, j)

        @pl.when(j == 0)
        def _():
            dkv_ref[...] = jnp.zeros_like(dkv_ref)
            gt_acc[...] = jnp.zeros_like(gt_acc)
            dsink_ref[...] = jnp.zeros_like(dsink_ref)

        lane_lo = _lane_lo()

        def attend(with_prev):
            first_col = 0 if with_prev else B_PREV
            dk_add = jnp.zeros((keys - first_col, 128), F32)
            dv_add = jnp.zeros((keys - first_col, 128), F32)
            for kvh in range(B_KV_HEADS):
                k_lo, k_hi, v_lo, v_hi = _b_operands(kp, kc, vp, vc, kvh, with_prev)
                dk_blk = jnp.zeros((keys - first_col, 128), F32)
                dv_blk = jnp.zeros((keys - first_col, 128), F32)
                for pp in range(B_GROUP // 2):
                    cols = slice(512 * kvh + 128 * pp, 512 * kvh + 128 * (pp + 1))
                    qp = q_ref[:, cols]
                    o = o_ref[:, cols].astype(F32)
                    lse_pair = lse_ref[:, cols]
                    dzf = dz_ref[:, cols].astype(F32)
                    silu, dsilu = _silu_parts(g_ref[:, cols].astype(F32))
                    do = dzf * silu
                    dqg_ref[1, :, cols] = (dzf * o * dsilu).astype(BF16)
                    doo = do * o
                    dqs = []
                    for hh in range(2):
                        g = kvh * B_GROUP + 2 * pp + hh
                        sel = lane_lo if hh == 0 else jnp.logical_not(lane_lo)
                        kk = k_lo if hh == 0 else k_hi
                        vv = v_lo if hh == 0 else v_hi
                        qm = jnp.where(sel, qp, jnp.zeros_like(qp)) * SCALE
                        s = _dot_nt(qm, kk) + b_ref[g, :, first_col:]
                        lse_h = _own_everywhere(lse_pair, sel)
                        p = jnp.exp(_minus_rows(s, lse_h))
                        delta = jnp.sum(jnp.where(sel, doo, 0.0), axis=-1, keepdims=True)
                        dom = jnp.where(sel, do, 0.0).astype(BF16)
                        dp = _dot_nt(dom, vv)
                        ds = p * (dp - delta)
                        gt_acc[g, :, first_col:] += ds
                        dsink_ref[g:g + 1, :] -= jnp.sum(jnp.exp(sink_ref[g] - lse_h) * delta, axis=0, keepdims=True)
                        dsb = ds.astype(BF16)
                        dqs.append(_dot(dsb, kk) * SCALE)
                        dk_blk = dk_blk + _dot_tn(dsb, qm)
                        dv_blk = dv_blk + _dot_tn(p.astype(BF16), dom)
                    dqg_ref[0, :, cols] = jnp.where(lane_lo, dqs[0], dqs[1]).astype(BF16)
                mine = lane_lo if kvh == 0 else jnp.logical_not(lane_lo)
                dk_add = dk_add + jnp.where(mine, dk_blk + pltpu.roll(dk_blk, HEAD_DIM, 1), 0.0)
                dv_add = dv_add + jnp.where(mine, dv_blk + pltpu.roll(dv_blk, HEAD_DIM, 1), 0.0)
            first_key = B_PREV if with_prev else 0
            if with_prev:
                rows = pl.ds(pl.multiple_of(j * qblk - B_PREV, B_PREV), B_PREV)
                dkv_ref[0, rows, :] += dk_add[0:B_PREV]
                dkv_ref[1, rows, :] += dv_add[0:B_PREV]
            rows = pl.ds(pl.multiple_of(j * qblk, qblk), qblk)
            dkv_ref[0, rows, :] += dk_add[first_key:first_key + qblk]
            dkv_ref[1, rows, :] += dv_add[first_key:first_key + qblk]

        pl.when(j == 0)(functools.partial(attend, False))
        pl.when(j >= 1)(functools.partial(attend, True))

        @pl.when(j == nq - 1)
        def _():
            dd = jnp.concatenate([_offset_sums(_collapse_chunks(gt_acc[g], keys)) for g in range(N_HEADS)], axis=0)
            hi = dd.astype(BF16)
            lo = (dd - hi.astype(F32)).astype(BF16)
            dt5_ref[...] = _dot(hi, oh_ref[...]) + _dot(lo, oh_ref[...])

    wide = lambda col: pl.BlockSpec((qblk, D_MODEL), lambda j, col=col: (j, col))
    prev = lambda col: pl.BlockSpec((B_PREV, 128), lambda j, col=col: (jnp.maximum(per * j - 1, 0), col))
    cur = lambda col: pl.BlockSpec((qblk, 128), lambda j, col=col: (j, col))
    fixed = lambda shape: pl.BlockSpec(shape, lambda j: (0,) * len(shape))
    return pl.pallas_call(
        body, name="attn_b_bwd", grid=(nq,),
        in_specs=[wide(0), wide(1), prev(0), cur(0), prev(1), cur(1), fixed((N_HEADS, keys)),
                  pl.BlockSpec(memory_space=pltpu.SMEM), wide(0), wide(0), wide(0), fixed((keys, 128))],
        out_specs=[pl.BlockSpec((2, qblk, D_MODEL), lambda j: (0, j, 0)), fixed((2, t, 128)),
                   fixed((N_HEADS, 128)), fixed((N_HEADS, 128))],
        out_shape=[SDS((2, t, D_MODEL), BF16), SDS((2, t, 128), F32),
                   SDS((N_HEADS, 128), F32), SDS((N_HEADS, 128), F32)],
        scratch_shapes=[pltpu.VMEM((N_HEADS, qblk, keys), F32), pltpu.VMEM((N_HEADS, qblk, keys), F32)],
        compiler_params=_cparams(),
    )(qg, qg, kv, kv, kv, kv, bias, sinks, out_b, lse, dz, bucket_onehot)


def _a_bias_by_offset(rel_bias):
    m = np.arange(A_DIAG)
    idx = np.clip(A_BAND - 1 - m, -A_REL_CLIP, A_REL_CLIP) + A_REL_CLIP
    by_head = rel_bias[idx].T.reshape(N_HEADS // 2, 2, A_DIAG)
    return jnp.concatenate([by_head, jnp.zeros((N_HEADS // 2, 6, A_DIAG), F32)], axis=1)


def _a_bias_grad(offset_sums):
    first = 319
    tail = jnp.sum(offset_sums[:, :first], axis=1)
    body = jnp.flip(offset_sums[:, first:first + 320], axis=1)
    body = body.at[:, -1].add(tail)
    full = jnp.concatenate([jnp.zeros((N_HEADS, 193), F32), body], axis=1)
    return full


def _t5_bucket(rel):
    nb = T5_BUCKETS // 2
    max_exact = nb // 2
    ret = jnp.where(rel > 0, nb, 0)
    n = jnp.abs(rel)
    nf = jnp.maximum(n, 1).astype(jnp.float32)
    large = max_exact + (jnp.log(nf / max_exact) / math.log(T5_MAX_DIST / max_exact)
                         * (nb - max_exact)).astype(jnp.int32)
    large = jnp.minimum(large, nb - 1)
    return ret + jnp.where(n < max_exact, n, large)


def _b_offset_buckets(keys):
    return _t5_bucket(jnp.arange(keys, dtype=jnp.int32) - (B_LEFT_CHUNKS * CHUNK + CHUNK - 1))


def _b_bias_by_offset(t5_table, keys):
    return t5_table[_b_offset_buckets(keys)].T


def _b_bucket_onehot(keys):
    return (_b_offset_buckets(keys)[:, None] == jnp.arange(128)[None, :]).astype(BF16)


def _local_step(my_slot, order, x, target, a_gain_shard, w_in_a_shard, rel_bias, late_shards, kv_gain,
                t5_table, b_gain, sinks, f_gain):
    a_bias = _a_bias_by_offset(rel_bias)
    b_bias_fwd = _b_bias_by_offset(t5_table, B_QBLK_FWD + B_PREV)
    b_bias_bwd = _b_bias_by_offset(t5_table, B_QBLK_BWD + B_PREV)
    sinks_flat = sinks.reshape(N_HEADS)

    xn, qkvg, w_in_a, a_gain = _norm_matmul_gather(order, x, a_gain_shard, w_in_a_shard)
    z_a, out_a, lse_a, (w_in_b, w_out_a, w_out_b, kv_w) = _attn_a_fwd(qkvg, a_bias, late_shards)
    w_out_a = w_out_a.reshape(D_MODEL, D_MODEL)
    w_out_b = w_out_b.reshape(D_MODEL, D_MODEL)
    kv_w = kv_w.reshape(D_MODEL, 2 * 128)
    h1, kvn, hb, kv, qg = _layer_a_out(x, z_a, w_out_a, kv_gain, b_gain, kv_w, w_in_b)
    z_b, out_b, lse_b = _attn_b_fwd(qg, kv, b_bias_fwd, sinks_flat)
    dh2, dh2b, dz_b, loss, d_fn = _layer_b_out_loss(h1, z_b, w_out_b, f_gain, target)

    dqg_b, dkv_b, d_t5, d_sink = _attn_b_bwd(qg, kv, b_bias_bwd, sinks_flat, out_b, lse_b, dz_b,
                                             _b_bucket_onehot(B_QBLK_BWD + B_PREV))
    dh1, dh1b, dz_a, d_bn, d_kn = _layer_b_in_bwd(dqg_b, dkv_b, w_in_b, kv_w, h1, dh2, b_gain, kv_gain, w_out_a)
    early = dict(
        b_w_out=_weight_grad_rows("grad_b_w_out", my_slot, z_b, dh2b[None]),
        b_w_in=_weight_grad_cols("grad_b_w_in", my_slot, hb, [dqg_b],
                                 [(0, o, c, 4 * o + c) for o in range(2) for c in range(4)], 256),
        kv_w=_weight_grad_rows("grad_kv_w", my_slot, kvn, dkv_b),
        a_w_out=_weight_grad_rows("grad_a_w_out", my_slot, z_a, dh1b[None]))
    dqg_a, dkv_a, d_rel, landed = _attn_a_bwd(qkvg, a_bias, out_a, lse_a, dz_a, [g[0] for g in early.values()])
    ready = dict(
        loss=loss, a_rel_bias=d_rel[:, :2].reshape(N_HEADS, A_DIAG),
        kv_norm=d_kn, t5_bias=d_t5, b_norm=d_bn, b_sinks=d_sink, final_norm=d_fn)
    g_own, from_sibling, from_far, chip_sums, ready_sums = _grad_a_w_in_reduce(
        _a_w_in_grad_order(), xn, dqg_a, dkv_a, list(ready.values()))
    grad_x, d_an, from_near = _layer_a_in_bwd(dqg_a, dkv_a, w_in_a, x, dh1, a_gain, chip_sums)

    matrices = {n: (g[1], [(land, 0, N_DEV - 1)]) for (n, g), land in zip(early.items(), landed)}
    matrices["a_w_in"] = (g_own, [(from_sibling, 0, 1), (from_far, 0, from_far.shape[0]),
                                  (from_near, 0, from_near.shape[0])])
    small = dict(zip(ready.keys(), ready_sums), a_norm=d_an)
    return grad_x, small, matrices


def _place():
    x, y, c = lax.axis_index("x"), lax.axis_index("y"), lax.axis_index("c")
    chips = [(1 - x, y), (x, 1 - y), (1 - x, 1 - y)]
    return x, y, c, chips


def _slot(px, py, pc):
    return 4 * px + 2 * py + pc


ANY = pl.BlockSpec(memory_space=pl.ANY)


def _peer(x, y, c, k):
    return (x ^ (k >> 2), y ^ ((k >> 1) & 1), c ^ (k & 1))


def _scatter_copies(grad_refs, land_refs, send_sems, recv_sems):
    x, y, c, _ = _place()
    copies = []
    for t, (grad, land) in enumerate(zip(grad_refs, land_refs)):
        for k in range(1, N_DEV):
            peer = _peer(x, y, c, k)
            sem = (N_DEV - 1) * t + k - 1
            copies.append(pltpu.make_async_remote_copy(
                src_ref=grad.at[_slot(*peer)], dst_ref=land.at[k - 1],
                send_sem=send_sems.at[sem], recv_sem=recv_sems.at[sem],
                device_id=peer, device_id_type=MESH))
    return copies


def _gather_phases(ins, outs, send_sems, recv_sems, local_sems):
    n = len(ins)
    x, y, c, chips = _place()
    me, sibling = (x, y, c), (x, y, 1 - c)

    def copy(t, k, block, to, src=None):
        dst = outs[t].at[_slot(*block)]
        return pltpu.make_async_remote_copy(
            src_ref=dst if src is None else src, dst_ref=dst,
            send_sem=send_sems.at[7 * t + k], recv_sem=recv_sems.at[7 * t + k],
            device_id=to, device_id_type=MESH)

    def lists():
        mine = [pltpu.make_async_copy(ins[t], outs[t].at[_slot(*me)], local_sems.at[t]) for t in range(n)]
        first = []
        for t in range(n):
            first.append(copy(t, 0, me, sibling, src=ins[t]))
            first += [copy(t, 1 + j, me, (*chip, c), src=ins[t]) for j, chip in enumerate(chips)]
        passed = [copy(t, 4 + j, (*chip, c), sibling) for t in range(n) for j, chip in enumerate(chips)]
        return mine, first, passed

    def start():
        mine, first, _ = lists()
        for cp in mine + first:
            cp.start()

    def forward():
        _, _, passed = lists()
        for t in range(n):
            for j, chip in enumerate(chips):
                copy(t, 1 + j, (*chip, c), me).wait_recv()
                passed[3 * t + j].start()

    def finish():
        mine, first, passed = lists()
        for t in range(n):
            copy(t, 0, sibling, me).wait_recv()
            for j, chip in enumerate(chips):
                copy(t, 4 + j, (*chip, 1 - c), me).wait_recv()
        for cp in first + passed:
            cp.wait_send()
        for cp in mine:
            cp.wait()

    return start, forward, finish


def _gather_scratch(n):
    return [pltpu.SemaphoreType.DMA((7 * n,)), pltpu.SemaphoreType.DMA((7 * n,)), pltpu.SemaphoreType.DMA((n,))]


_FAR_CHIP_FIRST = (2, 0, 1)
_SUMS_SENT_AT_ONCE = 1


def _a_w_in_grad_order():
    x, y, c, chips = _place()
    slots = []
    for j in _FAR_CHIP_FIRST:
        slots += [_slot(*chips[j], 1 - c), _slot(*chips[j], c)]
    slots += [_slot(x, y, 1 - c), _slot(x, y, c)]
    return jnp.stack(slots).astype(jnp.int32)


def _grad_a_w_in_reduce(order, a, dqg, dkv, small):
    t, dw = a.shape
    tn = dqg.shape[2] // 2
    n_s = len(small)
    n_far, n_sent = len(_FAR_CHIP_FIRST), _SUMS_SENT_AT_ONCE

    def body(order_ref, a_ref, dqg_ref, dkv_ref, *rest):
        small_refs, rest = rest[:n_s], rest[n_s:]
        own_ref, sib_ref, chips_ref, later_ref = rest[:4]
        small_out, rest = rest[4:4 + n_s], rest[4 + n_s:]
        a_buf, at_ref, res_ref, stage, land, load_sem, d2d_send, d2d_recv, ici_send, ici_recv = rest[:10]
        small_lands, (small_send, small_recv) = rest[10:10 + n_s], rest[10 + n_s:]
        s = pl.program_id(0)
        x, y, c, chips = _place()

        def to_sibling(i):
            return pltpu.make_async_remote_copy(
                src_ref=stage.at[i], dst_ref=land.at[i] if i < n_far else sib_ref.at[0],
                send_sem=d2d_send.at[i], recv_sem=d2d_recv.at[i], device_id=(x, y, 1 - c), device_id_type=MESH)

        def to_chip(i):
            return pltpu.make_async_remote_copy(
                src_ref=land.at[i], dst_ref=chips_ref.at[i], send_sem=ici_send.at[i], recv_sem=ici_recv.at[i],
                device_id=(*chips[_FAR_CHIP_FIRST[i]], c), device_id_type=MESH)

        @pl.when(s == 0)
        def _():
            load = pltpu.make_async_copy(a_ref, a_buf, load_sem)
            load.start()
            _all_reduce_small_start(small_refs, small_lands, small_send, small_recv)
            load.wait()
            at_ref[...] = a_buf[...].T

        blk = order_ref[s]
        from_qg = jnp.logical_or(blk < 2, blk >= 6)

        @pl.when(from_qg)
        def _():
            res_ref[...] = _dot(at_ref[...], dqg_ref[0])

        @pl.when(jnp.logical_not(from_qg))
        def _():
            res_ref[...] = _dot(at_ref[...], dkv_ref[0])

        for i in range(n_far + 1):
            @pl.when(s == 2 * i)
            def _(i=i):
                stage[i] = res_ref[...].astype(BF16)
                to_sibling(i).start()

        for i in range(n_far):
            @pl.when(s == 2 * i + 1)
            def _(i=i):
                to_sibling(i).wait_recv()
                total = (res_ref[...] + land[i].astype(F32)).astype(BF16)
                if i < n_sent:
                    land[i] = total
                    to_chip(i).start()
                else:
                    later_ref[i - n_sent] = total

        @pl.when(s == N_DEV - 1)
        def _():
            own_ref[...] = res_ref[...]
            _all_reduce_small_finish(small_refs, small_out, small_lands, small_send, small_recv)
            for i in range(n_far + 1):
                to_sibling(i).wait_send()
            to_sibling(n_far).wait_recv()
            for i in range(n_sent):
                to_chip(i).wait()

    whole = pl.BlockSpec(memory_space=pltpu.VMEM)
    outs = pl.pallas_call(
        body, name="grad_a_w_in",
        grid_spec=pltpu.PrefetchScalarGridSpec(
            num_scalar_prefetch=1, grid=(N_DEV,),
            in_specs=[ANY,
                      pl.BlockSpec((1, t, tn), lambda s, o: (o[s] // 6, 0, o[s] % 2)),
                      pl.BlockSpec((1, t, tn), lambda s, o: ((o[s] // 4) % 2, 0, o[s] % 2))] + [whole] * n_s,
            out_specs=[pl.BlockSpec((dw, tn), lambda s, o: (0, 0)), ANY, ANY, whole] + [whole] * n_s,
            scratch_shapes=[pltpu.VMEM((t, dw), BF16), pltpu.VMEM((dw, t), BF16), pltpu.VMEM((dw, tn), F32),
                            pltpu.VMEM((n_far + 1, dw, tn), BF16), pltpu.VMEM((n_far, dw, tn), BF16),
                            pltpu.SemaphoreType.DMA,
                            pltpu.SemaphoreType.DMA((n_far + 1,)), pltpu.SemaphoreType.DMA((n_far + 1,)),
                            pltpu.SemaphoreType.DMA((n_sent,)), pltpu.SemaphoreType.DMA((n_sent,))]
            + _all_reduce_scratch([s.shape for s in small])),
        out_shape=[SDS((dw, tn), F32), SDS((1, dw, tn), BF16), SDS((n_sent, dw, tn), BF16),
                   SDS((n_far - n_sent, dw, tn), BF16)] + [SDS(s.shape, F32) for s in small],
        compiler_params=_cparams(),
    )(order, a, dqg, dkv, *small)
    return outs[0], outs[1], outs[2], outs[3], list(outs[4:])


def _later_chip_copies(sums_ref, land_ref, send_sems, recv_sems):
    x, y, c, chips = _place()
    del x, y
    return [pltpu.make_async_remote_copy(
        src_ref=sums_ref.at[i], dst_ref=land_ref.at[i], send_sem=send_sems.at[i], recv_sem=recv_sems.at[i],
        device_id=(*chips[j], c), device_id_type=MESH) for i, j in enumerate(_FAR_CHIP_FIRST[_SUMS_SENT_AT_ONCE:])]


def _row_tile(rows):
    return min(rows, 512)


def _adamw(w, g, m, v):
    m2 = ADAM_B1 * m + (1.0 - ADAM_B1) * g
    v2 = ADAM_B2 * v + (1.0 - ADAM_B2) * jnp.square(g)
    m_hat = m2 / (1.0 - ADAM_B1 ** ADAM_STEP)
    v_hat = v2 / (1.0 - ADAM_B2 ** ADAM_STEP)
    delta = -ADAM_LR * (m_hat / (jnp.sqrt(v_hat) + ADAM_EPS) + ADAM_WD * w)
    return delta, m2, v2


def _reduce_adamw(name, own, partials, w, m, v):
    r, c = own.shape
    tr = _row_tile(r)
    n_p = len(partials)

    def body(own_ref, *rest):
        p_refs, (w_ref, m_ref, v_ref, grad_ref, d_ref, nm_ref, nv_ref) = rest[:n_p], rest[n_p:]
        grad = own_ref[...]
        for p_ref, (_, _, count) in zip(p_refs, partials):
            for j in range(count):
                grad = grad + p_ref[j].astype(F32)
        grad_ref[...] = grad
        d_ref[...], nm_ref[...], nv_ref[...] = _adamw(w_ref[...], grad, m_ref[...], v_ref[...])

    flat = pl.BlockSpec((tr, c), lambda i: (i, 0))
    return pl.pallas_call(
        body, name=name, grid=(r // tr,),
        in_specs=[flat] + [pl.BlockSpec((count, tr, c), lambda i, first=first, count=count: (first // count, i, 0))
                           for _, first, count in partials] + [flat, flat, flat],
        out_specs=[flat, flat, flat, flat],
        out_shape=[SDS((r, c), F32)] * 4,
        compiler_params=_cparams(),
    )(own, *[p[0] for p in partials], w, m, v)


VM = pl.BlockSpec()


def _all_reduce_small(ins, outs, lands, send_sems, recv_sems):
    _all_reduce_small_start(ins, lands, send_sems, recv_sems)
    _all_reduce_small_finish(ins, outs, lands, send_sems, recv_sems)


def _all_reduce_small_sends(ins, lands, send_sems, recv_sems):
    x, y, c, _ = _place()
    return [pltpu.make_async_remote_copy(
        src_ref=src, dst_ref=land.at[_slot(x, y, c)],
        send_sem=send_sems.at[(N_DEV - 1) * t + k - 1], recv_sem=recv_sems.at[(N_DEV - 1) * t + k - 1],
        device_id=_peer(x, y, c, k), device_id_type=MESH)
        for t, (src, land) in enumerate(zip(ins, lands)) for k in range(1, N_DEV)]


def _all_reduce_small_start(ins, lands, send_sems, recv_sems):
    x, y, c, _ = _place()
    for src, land in zip(ins, lands):
        land[_slot(x, y, c)] = src[...]
    for cp in _all_reduce_small_sends(ins, lands, send_sems, recv_sems):
        cp.start()


def _all_reduce_small_finish(ins, outs, lands, send_sems, recv_sems):
    x, y, c, _ = _place()
    copies = _all_reduce_small_sends(ins, lands, send_sems, recv_sems)
    for t, (src, land) in enumerate(zip(ins, lands)):
        for k in range(1, N_DEV):
            sem = (N_DEV - 1) * t + k - 1
            pltpu.make_async_remote_copy(
                src_ref=src, dst_ref=land.at[_slot(*_peer(x, y, c, k))],
                send_sem=send_sems.at[sem], recv_sem=recv_sems.at[sem],
                device_id=(x, y, c), device_id_type=MESH).wait_recv()
    for cp in copies:
        cp.wait_send()
    for out, land in zip(outs, lands):
        total = land[0]
        for s in range(1, N_DEV):
            total = total + land[s]
        out[...] = total


def _all_reduce_scratch(shapes):
    n_sems = (N_DEV - 1) * len(shapes)
    return ([pltpu.VMEM((N_DEV, *s), F32) for s in shapes]
            + [pltpu.SemaphoreType.DMA((n_sems,)), pltpu.SemaphoreType.DMA((n_sems,))])


def _small_adamw(my_slot, sums, ws, ms, vs):
    n = len(ws)

    def body(slot_ref, *refs):
        sum_refs, refs = refs[:n + 1], refs[n + 1:]
        w_refs, m_refs, v_refs, refs = refs[:n], refs[n:2 * n], refs[2 * n:3 * n], refs[3 * n:]
        g_refs, d_refs, nm_refs, nv_refs = refs[:n + 1], refs[n + 1:2 * n + 1], refs[2 * n + 1:3 * n + 1], refs[3 * n + 1:]
        for t in range(n + 1):
            if t == 0:
                g = sum_refs[0][:, pl.ds(pl.multiple_of(slot_ref[0] * 128, 128), 128)]
            else:
                g = sum_refs[t][...]
            g_refs[t][...] = g
            if t < n:
                d_refs[t][...], nm_refs[t][...], nv_refs[t][...] = _adamw(w_refs[t][...], g, m_refs[t][...], v_refs[t][...])

    shapes = [SDS(w.shape, F32) for w in ws]
    outs = pl.pallas_call(
        body, name="small_adamw",
        in_specs=[pl.BlockSpec(memory_space=pltpu.SMEM)] + [VM] * (4 * n + 1),
        out_specs=[VM] * (4 * n + 1),
        out_shape=shapes + [SDS(sums[-1].shape, F32)] + shapes * 3,
    )(my_slot, *sums, *ws, *ms, *vs)
    return outs[:n + 1], outs[n + 1:2 * n + 1], outs[2 * n + 1:3 * n + 1], outs[3 * n + 1:]


def kernel(x, a_norm, a_w_in, a_rel_bias, a_w_out, kv_norm, kv_w, t5_bias, b_norm, b_w_in, b_sinks, b_w_out, final_norm, loss_target, m_a_norm, m_a_w_in, m_a_rel_bias, m_a_w_out, m_kv_norm, m_kv_w, m_t5_bias, m_b_norm, m_b_w_in, m_b_sinks, m_b_w_out, m_final_norm, v_a_norm, v_a_w_in, v_a_rel_bias, v_a_w_out, v_kv_norm, v_kv_w, v_t5_bias, v_b_norm, v_b_w_in, v_b_sinks, v_b_w_out, v_final_norm):
    xi, yi, ci = lax.axis_index("x"), lax.axis_index("y"), lax.axis_index("c")
    my_slot = _slot(xi, yi, ci)

    slot_arr = jnp.reshape(my_slot, (1,)).astype(jnp.int32)
    order = _gather_order(xi, yi, ci)
    late_shards = [b_w_in[0].astype(BF16), a_w_out[0].astype(BF16), b_w_out[0].astype(BF16), kv_w.astype(BF16)]
    grad_x, loc, matrices = _local_step(
        slot_arr, order, x[0], loss_target[0], a_norm, a_w_in[0].astype(BF16), a_rel_bias[0], late_shards,
        kv_norm.reshape(1, D_MODEL), t5_bias, b_norm, b_sinks, final_norm.reshape(1, D_MODEL))

    shard_w = dict(a_w_in=a_w_in[0], b_w_in=b_w_in[0], a_w_out=a_w_out[0], b_w_out=b_w_out[0], kv_w=kv_w)
    shard_m = dict(a_w_in=m_a_w_in[0], b_w_in=m_b_w_in[0], a_w_out=m_a_w_out[0], b_w_out=m_b_w_out[0], kv_w=m_kv_w)
    shard_v = dict(a_w_in=v_a_w_in[0], b_w_in=v_b_w_in[0], a_w_out=v_a_w_out[0], b_w_out=v_b_w_out[0], kv_w=v_kv_w)
    big = {n: _reduce_adamw("adamw_" + n, own, partials, shard_w[n], shard_m[n], shard_v[n])
           for n, (own, partials) in matrices.items()}

    names = ("a_norm", "a_rel_bias", "kv_norm", "t5_bias", "b_norm", "b_sinks", "final_norm")
    tables = ("a_rel_bias", "t5_bias")

    def row(n, a):
        return a.reshape(-1, a.shape[-1]).T if n in tables else a.reshape(1, -1)

    small_w = [row(n, a) for n, a in zip(names, (a_norm, a_rel_bias, kv_norm, t5_bias, b_norm, b_sinks, final_norm))]
    small_m = [row(n, a) for n, a in zip(names, (m_a_norm, m_a_rel_bias, m_kv_norm, m_t5_bias, m_b_norm, m_b_sinks,
                                                 m_final_norm))]
    small_v = [row(n, a) for n, a in zip(names, (v_a_norm, v_a_rel_bias, v_kv_norm, v_t5_bias, v_b_norm, v_b_sinks,
                                                 v_final_norm))]
    sums = dict(loc)
    sums["a_rel_bias"] = _a_bias_grad(sums["a_rel_bias"])
    sums["t5_bias"] = sums["t5_bias"][:, :T5_BUCKETS]
    sums["b_sinks"] = sums["b_sinks"][:, 0].reshape(1, N_HEADS)
    results = _small_adamw(slot_arr, [sums[n] for n in names + ("loss",)], small_w, small_m, small_v)
    like = dict(a_norm=a_norm, a_rel_bias=a_rel_bias, kv_norm=kv_norm, t5_bias=t5_bias, b_norm=b_norm,
                b_sinks=b_sinks, final_norm=final_norm)
    sm = [{n: (part[i].T if n in tables else part[i]).reshape(like[n].shape) for i, n in enumerate(names)}
          for part in results]
    loss = results[0][len(names)][0, 0]

    order = ("a_norm", "a_w_in", "a_rel_bias", "a_w_out", "kv_norm", "kv_w", "t5_bias", "b_norm",
             "b_w_in", "b_sinks", "b_w_out", "final_norm")
    lead = dict(a_w_in=True, b_w_in=True, a_w_out=True, b_w_out=True, kv_w=False)

    def pick(kind, name):
        if name in big:
            val = big[name][kind]
            return val[None] if lead[name] else val
        return sm[kind][name]

    outs = [loss, grad_x[None]]
    for kind in range(4):
        outs += [pick(kind, n) for n in order]
    return tuple(outs)
```

```python
import functools
import math

import numpy as np
import jax
import jax.numpy as jnp
from jax import lax
from jax.experimental import pallas as pl
from jax.experimental.pallas import tpu as pltpu

F32 = jnp.float32
BF16 = jnp.bfloat16
SDS = jax.ShapeDtypeStruct

D_MODEL = 1024
HEAD_DIM = 64
CHUNK = 64
N_HEADS = 16
RMS_EPS = 1e-6
A_LEFT_CHUNKS = 8
A_BAND = (A_LEFT_CHUNKS + 1) * CHUNK
A_REL_CLIP = 256
B_KV_HEADS = 2
B_GROUP = 8
B_LEFT_CHUNKS = 2
B_BAND = (B_LEFT_CHUNKS + 1) * CHUNK
T5_BUCKETS = 32
T5_MAX_DIST = 128
QBLK = 256
A_KEYS = 3 * QBLK
B_QBLK_FWD = 128
B_QBLK_BWD = 256
B_PREV = 128
A_DIAG = A_KEYS
NEG = -1e30
SCALE = HEAD_DIM ** -0.5
N_DEV = 8

ADAM_LR = 0.001
ADAM_B1 = 0.9
ADAM_B2 = 0.999
ADAM_EPS = 1e-08
ADAM_WD = 0.01
ADAM_STEP = 10

VMEM_LIMIT_BYTES = 56 * 1024 * 1024
MESH = pl.DeviceIdType.MESH


def _cparams():
    return pltpu.CompilerParams(vmem_limit_bytes=VMEM_LIMIT_BYTES)


def _dot(a, b):
    return jnp.dot(a, b, preferred_element_type=F32)


def _dot_nt(a, b):
    return lax.dot_general(a, b, (((1,), (1,)), ((), ())), preferred_element_type=F32)


def _dot_tn(a, b):
    return lax.dot_general(a, b, (((0,), (0,)), ((), ())), preferred_element_type=F32)


def _rstd(xf):
    return lax.rsqrt(jnp.mean(xf * xf, axis=-1, keepdims=True) + RMS_EPS)


def _sigmoid(x):
    return 1.0 / (1.0 + jnp.exp(-x))


_GATHER_SEQUENCE = ((0, None), (1, 0), (2, 1), (4, None), (5, None), (3, 2), (6, None))


def _gather_order(x, y, c):
    others = [(1 - x, y), (x, 1 - y), (1 - x, 1 - y)]
    arrivals = [_slot(x, y, 1 - c)] + [_slot(*chip, c) for chip in others] + [_slot(*chip, 1 - c) for chip in others]
    return jnp.stack([_slot(x, y, c)] + [arrivals[a] for a, _ in _GATHER_SEQUENCE]).astype(jnp.int32)


def _norm_matmul_gather(order, x, gain_shard, w_shard):
    t = x.shape[0]
    dw, tn = w_shard.shape
    tm = min(t, 2048)
    n_m = t // tm

    def body(order_ref, x_ref, gs_ref, shard_ref, xn_ref, o_ref, full_ref, gain_ref,
             xn_all, wbuf, gland, send_sems, recv_sems, gsend_sems, grecv_sems, load_sems, own_sem):
        n, m = pl.program_id(0), pl.program_id(1)
        x_i, y_i, c_i, chips = _place()
        me, sibling = (x_i, y_i, c_i), (x_i, y_i, 1 - c_i)

        def send(k, block, to, src=None):
            dst = full_ref.at[_slot(*block)]
            return pltpu.make_async_remote_copy(
                src_ref=dst if src is None else src, dst_ref=dst,
                send_sem=send_sems.at[k], recv_sem=recv_sems.at[k], device_id=to, device_id_type=MESH)

        own = pltpu.make_async_copy(shard_ref, full_ref.at[_slot(*me)], own_sem)
        first = [send(0, me, sibling, src=shard_ref)]
        first += [send(1 + j, me, (*chip, c_i), src=shard_ref) for j, chip in enumerate(chips)]
        forwards = [send(4 + j, (*chip, c_i), sibling) for j, chip in enumerate(chips)]
        arrivals = [send(0, sibling, me)] + [send(1 + j, (*chip, c_i), me) for j, chip in enumerate(chips)]
        arrivals += [send(4 + j, (*chip, 1 - c_i), me) for j, chip in enumerate(chips)]
        gains = [pltpu.make_async_remote_copy(
            src_ref=gs_ref, dst_ref=gland.at[_slot(*me)], send_sem=gsend_sems.at[k - 1],
            recv_sem=grecv_sems.at[k - 1], device_id=_peer(x_i, y_i, c_i, k), device_id_type=MESH)
            for k in range(1, N_DEV)]

        @pl.when(jnp.logical_and(n == 0, m == 0))
        def _():
            own.start()
            for cp in gains + first:
                cp.start()
            pltpu.make_async_copy(shard_ref, wbuf.at[0], load_sems.at[0]).start()
            gland[_slot(*me)] = gs_ref[...]
            for k in range(1, N_DEV):
                pltpu.make_async_remote_copy(
                    src_ref=gs_ref, dst_ref=gland.at[_slot(*_peer(x_i, y_i, c_i, k))],
                    send_sem=gsend_sems.at[k - 1], recv_sem=grecv_sems.at[k - 1],
                    device_id=me, device_id_type=MESH).wait_recv()
            for s in range(N_DEV):
                gain_ref[:, 128 * s:128 * (s + 1)] = gland[s]

        rows = pl.ds(pl.multiple_of(m * tm, tm), tm)

        @pl.when(n == 0)
        def _():
            xf = x_ref[...]
            xn = ((xf * _rstd(xf)) * gain_ref[...]).astype(BF16)
            xn_all[rows, :] = xn
            xn_ref[...] = xn

        @pl.when(m == 0)
        def _():
            pltpu.make_async_copy(full_ref.at[0], wbuf.at[n % 2], load_sems.at[n % 2]).wait()

        o_ref[...] = _dot(xn_all[rows, :], wbuf[n % 2]).astype(BF16)

        for k, (arrival, forward) in enumerate(_GATHER_SEQUENCE):
            @pl.when(jnp.logical_and(n == k, m == n_m - 1))
            def _(k=k, arrival=arrival, forward=forward):
                arrivals[arrival].wait_recv()
                if forward is not None:
                    forwards[forward].start()
                pltpu.make_async_copy(full_ref.at[order_ref[k + 1]], wbuf.at[(k + 1) % 2],
                                      load_sems.at[(k + 1) % 2]).start()

        @pl.when(jnp.logical_and(n == N_DEV - 1, m == n_m - 1))
        def _():
            for cp in gains + first + forwards:
                cp.wait_send()
            own.wait()

    held = lambda n, m, order: (jnp.where(n == 0, m, n_m - 1), 0)
    return pl.pallas_call(
        body, name="norm_matmul_gather",
        grid_spec=pltpu.PrefetchScalarGridSpec(
            num_scalar_prefetch=1, grid=(N_DEV, n_m),
            in_specs=[pl.BlockSpec((tm, D_MODEL), held),
                      pl.BlockSpec((1, 128), lambda n, m, order: (0, 0)), ANY],
            out_specs=[pl.BlockSpec((tm, D_MODEL), held),
                       pl.BlockSpec((tm, tn), lambda n, m, order: (m, order[n])),
                       ANY, pl.BlockSpec((1, D_MODEL), lambda n, m, order: (0, 0))],
            scratch_shapes=[pltpu.VMEM((t, D_MODEL), BF16), pltpu.VMEM((2, dw, tn), BF16),
                            pltpu.VMEM((N_DEV, 1, 128), F32),
                            pltpu.SemaphoreType.DMA((7,)), pltpu.SemaphoreType.DMA((7,)),
                            pltpu.SemaphoreType.DMA((7,)), pltpu.SemaphoreType.DMA((7,)),
                            pltpu.SemaphoreType.DMA((2,)), pltpu.SemaphoreType.DMA]),
        out_shape=[SDS((t, D_MODEL), BF16), SDS((t, N_DEV * tn), BF16), SDS((N_DEV, dw, tn), BF16),
                   SDS((1, D_MODEL), F32)],
        compiler_params=_cparams(),
    )(order, x, gain_shard, w_shard)


def _layer_a_out(x, z, w_out, kv_gain, b_gain, kv_w, w_in_b):
    t = x.shape[0]
    tm = min(t, 1024)
    nb, _, tn = w_in_b.shape

    def body(x_ref, z_ref, wo_ref, kvg_ref, bg_ref, kvw_ref, wb_ref,
             h1_ref, kvn_ref, hb_ref, kv_ref, qg_ref):
        h1 = x_ref[...] + _dot(z_ref[...], wo_ref[...])
        h1_ref[...] = h1
        y0 = h1 * _rstd(h1)
        kvn = (y0 * kvg_ref[...]).astype(BF16)
        hb = (y0 * bg_ref[...]).astype(BF16)
        kvn_ref[...] = kvn
        hb_ref[...] = hb
        kv_ref[...] = _dot(kvn, kvw_ref[...]).astype(BF16)
        for i in range(nb):
            qg_ref[:, i * tn:(i + 1) * tn] = _dot(hb, wb_ref[i]).astype(BF16)

    row = lambda m: (m, 0)
    fix2 = lambda m: (0, 0)
    return pl.pallas_call(
        body, name="layer_a_out", grid=(t // tm,),
        in_specs=[pl.BlockSpec((tm, D_MODEL), row), pl.BlockSpec((tm, D_MODEL), row),
                  pl.BlockSpec((D_MODEL, D_MODEL), fix2),
                  pl.BlockSpec((1, D_MODEL), fix2), pl.BlockSpec((1, D_MODEL), fix2),
                  pl.BlockSpec((D_MODEL, 256), fix2),
                  pl.BlockSpec((nb, D_MODEL, tn), lambda m: (0, 0, 0))],
        out_specs=[pl.BlockSpec((tm, D_MODEL), row), pl.BlockSpec((tm, D_MODEL), row),
                   pl.BlockSpec((tm, D_MODEL), row), pl.BlockSpec((tm, 256), row),
                   pl.BlockSpec((tm, nb * tn), row)],
        out_shape=[SDS((t, D_MODEL), F32), SDS((t, D_MODEL), BF16), SDS((t, D_MODEL), BF16),
                   SDS((t, 256), BF16), SDS((t, nb * tn), BF16)],
        compiler_params=_cparams(),
    )(x, z, w_out, kv_gain, b_gain, kv_w, w_in_b)


def _layer_b_out_loss(h1, z, w_out, f_gain, target):
    t = h1.shape[0]
    tm = min(t, 1024)

    def body(h1_ref, z_ref, wo_ref, fg_ref, tgt_ref,
             dh2_ref, dh2b_ref, dz_ref, loss_ref, dfn_ref):
        @pl.when(pl.program_id(0) == 0)
        def _():
            loss_ref[...] = jnp.zeros_like(loss_ref)
            dfn_ref[...] = jnp.zeros_like(dfn_ref)

        h2 = h1_ref[...] + _dot(z_ref[...], wo_ref[...])
        r = _rstd(h2)
        yn = h2 * r
        fg = fg_ref[...]
        err = yn * fg - tgt_ref[...]
        loss_ref[...] += (0.5 / D_MODEL) * jnp.sum(err * err)
        dy = err * (1.0 / D_MODEL)
        dfn_ref[...] += jnp.sum(dy * yn, axis=0, keepdims=True)
        u = dy * fg
        dh2 = r * u - h2 * ((r * r * r) * jnp.mean(u * h2, axis=-1, keepdims=True))
        dh2_ref[...] = dh2
        dh2b = dh2.astype(BF16)
        dh2b_ref[...] = dh2b
        dz_ref[...] = _dot_nt(dh2b, wo_ref[...]).astype(BF16)

    row = lambda m: (m, 0)
    fix2 = lambda m: (0, 0)
    return pl.pallas_call(
        body, name="layer_b_out_loss", grid=(t // tm,),
        in_specs=[pl.BlockSpec((tm, D_MODEL), row), pl.BlockSpec((tm, D_MODEL), row),
                  pl.BlockSpec((D_MODEL, D_MODEL), fix2), pl.BlockSpec((1, D_MODEL), fix2),
                  pl.BlockSpec((tm, D_MODEL), row)],
        out_specs=[pl.BlockSpec((tm, D_MODEL), row), pl.BlockSpec((tm, D_MODEL), row),
                   pl.BlockSpec((tm, D_MODEL), row), pl.BlockSpec((1, 128), fix2),
                   pl.BlockSpec((1, D_MODEL), fix2)],
        out_shape=[SDS((t, D_MODEL), F32), SDS((t, D_MODEL), BF16), SDS((t, D_MODEL), BF16),
                   SDS((1, 128), F32), SDS((1, D_MODEL), F32)],
        compiler_params=_cparams(),
    )(h1, z, w_out, f_gain, target)


def _layer_b_in_bwd(dqg, dkv, w_in_b, kv_w, h1, dh2, b_gain, kv_gain, w_out_a):
    t = h1.shape[0]
    tm = min(t, 512)
    nb, _, tn = w_in_b.shape
    per = D_MODEL // tn

    def body(dqg_ref, dkv_ref, wb_ref, kvw_ref, h1_ref, dh2_ref, bg_ref, kvg_ref, wo_ref,
             dh1_ref, dh1b_ref, dz_ref, dbn_ref, dkn_ref):
        @pl.when(pl.program_id(0) == 0)
        def _():
            dbn_ref[...] = jnp.zeros_like(dbn_ref)
            dkn_ref[...] = jnp.zeros_like(dkn_ref)

        dhb = jnp.zeros((tm, D_MODEL), F32)
        for i in range(nb):
            blk = dqg_ref[i // per, :, (i % per) * tn:(i % per + 1) * tn]
            dhb = dhb + _dot_nt(blk, wb_ref[i])
        dkn = (_dot_nt(dkv_ref[0].astype(BF16), kvw_ref[:, 0:128])
               + _dot_nt(dkv_ref[1].astype(BF16), kvw_ref[:, 128:256]))
        h1 = h1_ref[...]
        r = _rstd(h1)
        xr = h1 * r
        dbn_ref[...] += jnp.sum(dhb * xr, axis=0, keepdims=True)
        dkn_ref[...] += jnp.sum(dkn * xr, axis=0, keepdims=True)
        u = dhb * bg_ref[...] + dkn * kvg_ref[...]
        dh1 = dh2_ref[...] + r * u - h1 * ((r * r * r) * jnp.mean(u * h1, axis=-1, keepdims=True))
        dh1_ref[...] = dh1
        dh1b = dh1.astype(BF16)
        dh1b_ref[...] = dh1b
        dz_ref[...] = _dot_nt(dh1b, wo_ref[...]).astype(BF16)

    row = lambda m: (m, 0)
    fix2 = lambda m: (0, 0)
    return pl.pallas_call(
        body, name="layer_b_in_bwd", grid=(t // tm,),
        in_specs=[pl.BlockSpec((2, tm, D_MODEL), lambda m: (0, m, 0)),
                  pl.BlockSpec((2, tm, 128), lambda m: (0, m, 0)),
                  pl.BlockSpec((nb, D_MODEL, tn), lambda m: (0, 0, 0)),
                  pl.BlockSpec((D_MODEL, 256), fix2),
                  pl.BlockSpec((tm, D_MODEL), row), pl.BlockSpec((tm, D_MODEL), row),
                  pl.BlockSpec((1, D_MODEL), fix2), pl.BlockSpec((1, D_MODEL), fix2),
                  pl.BlockSpec((D_MODEL, D_MODEL), fix2)],
        out_specs=[pl.BlockSpec((tm, D_MODEL), row), pl.BlockSpec((tm, D_MODEL), row),
                   pl.BlockSpec((tm, D_MODEL), row), pl.BlockSpec((1, D_MODEL), fix2),
                   pl.BlockSpec((1, D_MODEL), fix2)],
        out_shape=[SDS((t, D_MODEL), F32), SDS((t, D_MODEL), BF16), SDS((t, D_MODEL), BF16),
                   SDS((1, D_MODEL), F32), SDS((1, D_MODEL), F32)],
        compiler_params=_cparams(),
    )(dqg, dkv, w_in_b, kv_w, h1, dh2, b_gain, kv_gain, w_out_a)


def _layer_a_in_bwd(dqg, dkv, w_in_a, x, dh1, a_gain, chip_sums):
    t = x.shape[0]
    tm = min(t, 512)
    nb, _, tn = w_in_a.shape
    per = D_MODEL // tn
    n_sums = chip_sums.shape[0]

    def body(dqg_ref, dkv_ref, w_ref, x_ref, dh1_ref, ag_ref, sums_ref, dx_ref, dan_ref, land_ref,
             send_sems, recv_sems, dan_land, dan_send, dan_recv):
        @pl.when(pl.program_id(0) == 0)
        def _():
            dan_ref[...] = jnp.zeros_like(dan_ref)
            for cp in _later_chip_copies(sums_ref, land_ref, send_sems, recv_sems):
                cp.start()

        dxn = jnp.zeros((tm, D_MODEL), F32)
        for i in range(nb):
            part = i // per
            src = dqg_ref if part in (0, 3) else dkv_ref
            outer = {0: 0, 3: 1, 1: 0, 2: 1}[part]
            blk = src[outer, :, (i % per) * tn:(i % per + 1) * tn]
            dxn = dxn + _dot_nt(blk, w_ref[i])
        xf = x_ref[...]
        r = _rstd(xf)
        dan_ref[...] += jnp.sum(dxn * (xf * r), axis=0, keepdims=True)
        u = dxn * ag_ref[...]
        dx_ref[...] = dh1_ref[...] + r * u - xf * ((r * r * r) * jnp.mean(u * xf, axis=-1, keepdims=True))

        @pl.when(pl.program_id(0) == t // tm - 1)
        def _():
            _all_reduce_small([dan_ref], [dan_ref], [dan_land], dan_send, dan_recv)
            for cp in _later_chip_copies(sums_ref, land_ref, send_sems, recv_sems):
                cp.wait()

    row = lambda m: (m, 0)
    fix2 = lambda m: (0, 0)
    return pl.pallas_call(
        body, name="layer_a_in_bwd", grid=(t // tm,),
        in_specs=[pl.BlockSpec((2, tm, D_MODEL), lambda m: (0, m, 0)),
                  pl.BlockSpec((2, tm, D_MODEL), lambda m: (0, m, 0)),
                  pl.BlockSpec((nb, D_MODEL, tn), lambda m: (0, 0, 0)),
                  pl.BlockSpec((tm, D_MODEL), row), pl.BlockSpec((tm, D_MODEL), row),
                  pl.BlockSpec((1, D_MODEL), fix2), ANY],
        out_specs=[pl.BlockSpec((tm, D_MODEL), row), pl.BlockSpec((1, D_MODEL), fix2), ANY],
        out_shape=[SDS((t, D_MODEL), F32), SDS((1, D_MODEL), F32), SDS(chip_sums.shape, chip_sums.dtype)],
        scratch_shapes=[pltpu.SemaphoreType.DMA((n_sums,)), pltpu.SemaphoreType.DMA((n_sums,))]
        + _all_reduce_scratch([(1, D_MODEL)]),
        compiler_params=_cparams(),
    )(dqg, dkv, w_in_a, x, dh1, a_gain, chip_sums)


def _lut(s, vals):
    r = jnp.int32(vals[0])
    for i in range(1, len(vals)):
        r = jnp.where(s == i, jnp.int32(vals[i]), r)
    return r


def _held(steps, i):
    seq, cur = [None] * len(steps), None
    for k in range(len(steps) - 1, -1, -1):
        if steps[k][0] == i:
            cur = steps[k][1:3]
        seq[k] = cur
    for k in range(len(steps)):
        cur = seq[k] = seq[k] if seq[k] is not None else cur
    return seq


def _weight_grad_cols(name, my_slot, a, bs, steps, tn):
    t, dw = a.shape
    n_arr = len(bs)
    which = [s[0] for s in steps]
    blks = [s[3] for s in steps]

    def body(slot_ref, a_ref, *rest):
        b_refs, (o_ref, own_ref, at_ref) = rest[:n_arr], rest[n_arr:]
        s = pl.program_id(0)

        @pl.when(s == 0)
        def _():
            at_ref[...] = a_ref[...].T

        for i in range(n_arr):
            @pl.when(_lut(s, which) == i)
            def _(i=i):
                res = _dot(at_ref[...], b_refs[i][0])
                o_ref[0] = res.astype(BF16)

                @pl.when(_lut(s, blks) == slot_ref[0])
                def _():
                    own_ref[...] = res

    def b_spec(i):
        held = _held(steps, i)
        return pl.BlockSpec((1, t, tn), lambda s, slot: (_lut(s, [h[0] for h in held]), 0,
                                                         _lut(s, [h[1] for h in held])))

    return pl.pallas_call(
        body, name=name,
        grid_spec=pltpu.PrefetchScalarGridSpec(
            num_scalar_prefetch=1, grid=(len(steps),),
            in_specs=[pl.BlockSpec((t, dw), lambda s, slot: (0, 0))] + [b_spec(i) for i in range(n_arr)],
            out_specs=[pl.BlockSpec((1, dw, tn), lambda s, slot: (_lut(s, blks), 0, 0)),
                       pl.BlockSpec((dw, tn), lambda s, slot: (0, 0))],
            scratch_shapes=[pltpu.VMEM((dw, t), BF16)]),
        out_shape=[SDS((N_DEV, dw, tn), BF16), SDS((dw, tn), F32)],
        compiler_params=_cparams(),
    )(my_slot, a, *bs)


def _weight_grad_rows(name, my_slot, a, b):
    t, dw = a.shape
    n_o, _, c = b.shape
    rows = dw // N_DEV
    tn = min(c, 256)
    per = c // tn

    def body(slot_ref, a_ref, b_ref, o_ref, own_ref, at_ref, res_ref):
        @pl.when(pl.program_id(0) == 0)
        def _():
            at_ref[...] = a_ref[...].T

        res_ref[...] = _dot(at_ref[...], b_ref[0].astype(BF16))
        o_ref[...] = res_ref[...].astype(BF16)
        own_ref[...] = res_ref[pl.ds(pl.multiple_of(slot_ref[0] * rows, rows), rows), :]

    all_rows, own = pl.pallas_call(
        body, name=name,
        grid_spec=pltpu.PrefetchScalarGridSpec(
            num_scalar_prefetch=1, grid=(n_o * per,),
            in_specs=[pl.BlockSpec((t, dw), lambda s, slot: (0, 0)),
                      pl.BlockSpec((1, t, tn), lambda s, slot: (s // per, 0, s % per))],
            out_specs=[pl.BlockSpec((dw, tn), lambda s, slot: (0, s)),
                       pl.BlockSpec((rows, tn), lambda s, slot: (0, s))],
            scratch_shapes=[pltpu.VMEM((dw, t), BF16), pltpu.VMEM((dw, tn), F32)]),
        out_shape=[SDS((dw, n_o * c), BF16), SDS((rows, n_o * c), F32)],
        compiler_params=_cparams(),
    )(my_slot, a, b)
    return all_rows.reshape(N_DEV, rows, n_o * c), own


def _lane_lo():
    return lax.broadcasted_iota(jnp.int32, (1, 128), 1) < HEAD_DIM


def _collapse_chunks(ds, keys):
    if ds.shape[1] < keys:
        ds = jnp.concatenate([jnp.zeros((ds.shape[0], keys - ds.shape[1]), F32), ds], axis=1)
    gc = ds[0:CHUNK]
    for cc in range(1, ds.shape[0] // CHUNK):
        gc = gc + pltpu.roll(ds[cc * CHUNK:(cc + 1) * CHUNK], keys - cc * CHUNK, 1)
    return gc


def _offset_sums(gc):
    hi = gc.astype(BF16)
    lo = (gc - hi.astype(F32)).astype(BF16)
    flip = (lax.broadcasted_iota(jnp.int32, (CHUNK, CHUNK), 0)
            + lax.broadcasted_iota(jnp.int32, (CHUNK, CHUNK), 1) == CHUNK - 1).astype(BF16)
    gf = _dot(flip, hi) + _dot(flip, lo)
    skew = pltpu.roll(gf, 0, 1, stride=1, stride_axis=0)
    return jnp.sum(skew, axis=0, keepdims=True)


def _band_bias(w_row, band, rows):
    keys = w_row.shape[1]
    base = jnp.broadcast_to(w_row, (CHUNK, keys))
    skew = pltpu.roll(base, 0, 1, stride=1, stride_axis=0)
    skew = pltpu.roll(skew, keys - (CHUNK - 1), 1)
    col = lax.broadcasted_iota(jnp.int32, (CHUNK, keys), 1)
    chunk0 = jnp.where(col < band, skew, NEG)
    return jnp.concatenate(
        [chunk0] + [pltpu.roll(chunk0, cc * CHUNK, 1) for cc in range(1, rows // CHUNK)], axis=0)


def _silu_parts(g):
    sg = _sigmoid(g)
    return g * sg, sg * (1.0 + g * (1.0 - sg))


A_PAIRS_FWD = 8
A_PAIRS_BWD = 4


def _a_specs(pairs):
    lanes = 128 * pairs
    steps = D_MODEL // lanes
    q = pl.BlockSpec((QBLK, lanes), lambda p, j: (j, p))
    ks = [pl.BlockSpec((QBLK, lanes), lambda p, j, b=b: (jnp.maximum(j - 2 + b, 0), steps + p)) for b in range(3)]
    vs = [pl.BlockSpec((QBLK, lanes), lambda p, j, b=b: (jnp.maximum(j - 2 + b, 0), 2 * steps + p))
          for b in range(3)]
    g = pl.BlockSpec((QBLK, lanes), lambda p, j: (j, 3 * steps + p))
    bias = pl.BlockSpec((pairs, 8, A_KEYS), lambda p, j: (p, 0, 0))
    return q, ks, vs, g, bias


def _a_fill_bias(w_ref, b_ref, j, pairs):
    _fill_bias(2 * pairs, lambda h: w_ref[h // 2, h % 2:h % 2 + 1, :], A_BAND, b_ref, j)


def _by_valid_key_blocks(j, fn):
    pl.when(j == 0)(functools.partial(fn, 1))
    pl.when(j == 1)(functools.partial(fn, 2))
    pl.when(j >= 2)(functools.partial(fn, 3))


def _fill_bias(n, get_row, band, bias_scr, j):
    @pl.when(j == 0)
    def _():
        for h in range(n):
            bias_scr[h] = _band_bias(get_row(h), band, bias_scr.shape[1])


def _normalise_pair(rs, mxs, lane_lo, extra=None):
    num = jnp.where(lane_lo, rs[0], rs[1])
    den = pltpu.roll(jnp.where(lane_lo, rs[1], rs[0]), HEAD_DIM, 1)
    if extra is not None:
        den = den + jnp.where(lane_lo, extra[0], extra[1])
    return num / den, jnp.where(lane_lo, mxs[0], mxs[1]) + jnp.log(den)


def _own_everywhere(x, sel):
    return jnp.where(sel, x, pltpu.roll(x, HEAD_DIM, 1))


def _minus_rows(s, row_full):
    return jnp.concatenate([s[:, i:i + 128] - row_full for i in range(0, s.shape[1], 128)], axis=1)


def _attn_a_fwd(qkvg, bias, gather):
    t = qkvg.shape[0]
    nq = t // QBLK
    n_g = len(gather)
    pairs = A_PAIRS_FWD
    lanes = 128 * pairs
    steps = D_MODEL // lanes
    q_spec, k_specs, v_specs, g_spec, bias_spec = _a_specs(pairs)

    def body(q_ref, k0, k1, k2, v0, v1, v2, g_ref, w_ref, *rest):
        shard_refs, rest = rest[:n_g], rest[n_g:]
        z_ref, o_ref, lse_ref = rest[:3]
        full_refs, (b_ref, *comm) = rest[3:3 + n_g], rest[3 + n_g:]
        p = pl.program_id(0)
        j = pl.program_id(1)
        start, forward, finish = _gather_phases(shard_refs, full_refs, *comm)
        at = p * nq + j
        pl.when(at == 0)(start)
        pl.when(at == steps * nq // 2)(forward)
        _a_fill_bias(w_ref, b_ref, j, pairs)
        lane_lo = _lane_lo()
        sels = (lane_lo, jnp.logical_not(lane_lo))

        def attend(n_blocks):
            first_col = (3 - n_blocks) * QBLK
            for pp in range(pairs):
                cols = slice(128 * pp, 128 * (pp + 1))
                k = jnp.concatenate([r[:, cols] for r in (k0, k1, k2)[3 - n_blocks:]], axis=0)
                v = jnp.concatenate([r[:, cols] for r in (v0, v1, v2)[3 - n_blocks:]], axis=0)
                q = q_ref[:, cols]
                qm2 = jnp.concatenate([jnp.where(sel, q, jnp.zeros_like(q)) for sel in sels], axis=0) * SCALE
                s2 = _dot_nt(qm2, k)
                rs, mxs = [], []
                for hh, sel in enumerate(sels):
                    s = s2[hh * QBLK:(hh + 1) * QBLK] + b_ref[2 * pp + hh, :, first_col:]
                    mxs.append(jnp.max(s, axis=-1, keepdims=True))
                    e = jnp.exp(s - mxs[hh]).astype(BF16)
                    rs.append(_dot(e, jnp.where(sel, v, jnp.ones_like(v))))
                o, lse = _normalise_pair(rs, mxs, lane_lo)
                silu, _ = _silu_parts(g_ref[:, cols].astype(F32))
                o_ref[:, cols] = o.astype(BF16)
                z_ref[:, cols] = (o * silu).astype(BF16)
                lse_ref[:, cols] = lse

        _by_valid_key_blocks(j, attend)
        pl.when(at == steps * nq - 1)(finish)

    out_spec = pl.BlockSpec((QBLK, lanes), lambda p, j: (j, p))
    outs = pl.pallas_call(
        body, name="attn_a_fwd", grid=(steps, nq),
        in_specs=[q_spec, *k_specs, *v_specs, g_spec, bias_spec] + [ANY] * n_g,
        out_specs=[out_spec, out_spec, out_spec] + [ANY] * n_g,
        out_shape=[SDS((t, D_MODEL), BF16), SDS((t, D_MODEL), BF16), SDS((t, D_MODEL), F32)]
        + [SDS((N_DEV, *s.shape), s.dtype) for s in gather],
        scratch_shapes=[pltpu.VMEM((2 * pairs, QBLK, A_KEYS), F32)] + _gather_scratch(n_g),
        compiler_params=_cparams(),
    )(qkvg, qkvg, qkvg, qkvg, qkvg, qkvg, qkvg, qkvg, bias, *gather)
    return outs[0], outs[1], outs[2], list(outs[3:])


def _attn_a_bwd(qkvg, bias, out_a, lse, dz, scatter):
    t = qkvg.shape[0]
    nq = t // QBLK
    n_sc = len(scatter)
    pairs = A_PAIRS_BWD
    lanes = 128 * pairs
    steps = D_MODEL // lanes
    q_spec, k_specs, v_specs, g_spec, bias_spec = _a_specs(pairs)

    def body(q_ref, k0, k1, k2, v0, v1, v2, g_ref, w_ref, o_ref, lse_ref, dz_ref, *rest):
        sc_refs, rest = rest[:n_sc], rest[n_sc:]
        dqg_ref, dkv_ref, dg_ref = rest[:3]
        land_refs, rest = rest[3:3 + n_sc], rest[3 + n_sc:]
        dk_acc, dv_acc, gt_acc, b_ref, send_sems, recv_sems = rest
        j = pl.program_id(1)
        first = jnp.logical_and(pl.program_id(0) == 0, j == 0)
        last = jnp.logical_and(pl.program_id(0) == steps - 1, j == nq - 1)

        @pl.when(first)
        def _():
            for cp in _scatter_copies(sc_refs, land_refs, send_sems, recv_sems):
                cp.start()

        _a_fill_bias(w_ref, b_ref, j, pairs)

        @pl.when(j == 0)
        def _():
            dk_acc[...] = jnp.zeros_like(dk_acc)
            dv_acc[...] = jnp.zeros_like(dv_acc)
            gt_acc[...] = jnp.zeros_like(gt_acc)

        lane_lo = _lane_lo()
        sels = (lane_lo, jnp.logical_not(lane_lo))

        def attend(n_blocks):
            first_col = (3 - n_blocks) * QBLK
            for pp in range(pairs):
                cols = slice(128 * pp, 128 * (pp + 1))
                q = q_ref[:, cols]
                k = jnp.concatenate([r[:, cols] for r in (k0, k1, k2)[3 - n_blocks:]], axis=0)
                v = jnp.concatenate([r[:, cols] for r in (v0, v1, v2)[3 - n_blocks:]], axis=0)
                o = o_ref[:, cols].astype(F32)
                lse_pair = lse_ref[:, cols]
                dzf = dz_ref[:, cols].astype(F32)
                silu, dsilu = _silu_parts(g_ref[:, cols].astype(F32))
                do = dzf * silu
                dqg_ref[1, :, cols] = (dzf * o * dsilu).astype(BF16)
                doo = do * o
                qm2 = jnp.concatenate([jnp.where(sel, q, jnp.zeros_like(q)) for sel in sels], axis=0) * SCALE
                dom2 = jnp.concatenate([jnp.where(sel, do, 0.0) for sel in sels], axis=0).astype(BF16)
                s2 = _dot_nt(qm2, k)
                dp2 = _dot_nt(dom2, v)
                ps, dss = [], []
                for hh, sel in enumerate(sels):
                    rows = slice(hh * QBLK, (hh + 1) * QBLK)
                    s = s2[rows] + b_ref[2 * pp + hh, :, first_col:]
                    p = jnp.exp(_minus_rows(s, _own_everywhere(lse_pair, sel)))
                    delta = jnp.sum(jnp.where(sel, doo, 0.0), axis=-1, keepdims=True)
                    ds = p * (dp2[rows] - delta)
                    gt_acc[2 * pp + hh] += _collapse_chunks(ds, A_KEYS)
                    ps.append(p.astype(BF16))
                    dss.append(ds.astype(BF16))
                dsb2 = jnp.concatenate(dss, axis=0)
                dq2 = _dot(dsb2, k) * SCALE
                dk_blk = _dot_tn(dsb2, qm2)
                dv_blk = _dot_tn(jnp.concatenate(ps, axis=0), dom2)
                dqg_ref[0, :, cols] = jnp.where(lane_lo, dq2[0:QBLK], dq2[QBLK:2 * QBLK]).astype(BF16)
                for b in range(n_blocks):
                    rows = pl.ds(pl.multiple_of((j - n_blocks + 1 + b) * QBLK, QBLK), QBLK)
                    dk_acc[rows, cols] += dk_blk[b * QBLK:(b + 1) * QBLK]
                    dv_acc[rows, cols] += dv_blk[b * QBLK:(b + 1) * QBLK]

        _by_valid_key_blocks(j, attend)

        @pl.when(j == nq - 1)
        def _():
            dkv_ref[0] = dk_acc[...].astype(BF16)
            dkv_ref[1] = dv_acc[...].astype(BF16)
            for pp in range(pairs):
                dg_ref[pp] = jnp.concatenate([_offset_sums(gt_acc[2 * pp]), _offset_sums(gt_acc[2 * pp + 1]),
                                              jnp.zeros((6, A_DIAG), F32)], axis=0)

        @pl.when(last)
        def _():
            for cp in _scatter_copies(sc_refs, land_refs, send_sems, recv_sems):
                cp.wait()

    blk = pl.BlockSpec((QBLK, lanes), lambda p, j: (j, p))
    outs = pl.pallas_call(
        body, name="attn_a_bwd", grid=(steps, nq),
        in_specs=[q_spec, *k_specs, *v_specs, g_spec, bias_spec, blk, blk, blk] + [ANY] * n_sc,
        out_specs=[pl.BlockSpec((2, QBLK, lanes), lambda p, j: (0, j, p)),
                   pl.BlockSpec((2, t, lanes), lambda p, j: (0, 0, p)),
                   pl.BlockSpec((pairs, 8, A_DIAG), lambda p, j: (p, 0, 0))] + [ANY] * n_sc,
        out_shape=[SDS((2, t, D_MODEL), BF16), SDS((2, t, D_MODEL), BF16), SDS((N_HEADS // 2, 8, A_DIAG), F32)]
        + [SDS((N_DEV - 1, *g.shape[1:]), g.dtype) for g in scatter],
        scratch_shapes=[pltpu.VMEM((t, lanes), F32), pltpu.VMEM((t, lanes), F32),
                        pltpu.VMEM((2 * pairs, CHUNK, A_KEYS), F32), pltpu.VMEM((2 * pairs, QBLK, A_KEYS), F32),
                        pltpu.SemaphoreType.DMA(((N_DEV - 1) * n_sc,)),
                        pltpu.SemaphoreType.DMA(((N_DEV - 1) * n_sc,))],
        compiler_params=_cparams(),
    )(qkvg, qkvg, qkvg, qkvg, qkvg, qkvg, qkvg, qkvg, bias, out_a, lse, dz, *scatter)
    return outs[0], outs[1], outs[2], list(outs[3:])


def _b_specs(qblk):
    per = qblk // B_PREV
    q = pl.BlockSpec((qblk, 512), lambda h, j: (j, h))
    g = pl.BlockSpec((qblk, 512), lambda h, j: (j, 2 + h))
    kp = pl.BlockSpec((B_PREV, 128), lambda h, j: (jnp.maximum(per * j - 1, 0), 0))
    kc = pl.BlockSpec((qblk, 128), lambda h, j: (j, 0))
    vp = pl.BlockSpec((B_PREV, 128), lambda h, j: (jnp.maximum(per * j - 1, 0), 1))
    vc = pl.BlockSpec((qblk, 128), lambda h, j: (j, 1))
    bias = pl.BlockSpec((B_GROUP, qblk + B_PREV), lambda h, j: (h, 0))
    sinks = pl.BlockSpec(memory_space=pltpu.SMEM)
    return q, g, kp, kc, vp, vc, bias, sinks


def _b_operands(kp, kc, vp, vc, kvh, with_prev):
    k = jnp.concatenate([kp[...], kc[...]], axis=0) if with_prev else kc[...]
    v = jnp.concatenate([vp[...], vc[...]], axis=0) if with_prev else vc[...]
    kr = pltpu.roll(k, HEAD_DIM, 1)
    vr = pltpu.roll(v, HEAD_DIM, 1)
    first = kvh == 0
    return (jnp.where(first, k, kr), jnp.where(first, kr, k),
            jnp.where(first, v, vr), jnp.where(first, vr, v))


def _attn_b_fwd(qg, kv, bias, sinks):
    t = qg.shape[0]
    qblk = B_QBLK_FWD
    per_step = 4
    step = per_step * qblk
    q_spec, g_spec, kp_spec, kc_spec, vp_spec, vc_spec, _, sink_spec = _b_specs(step)
    bias_spec = pl.BlockSpec((B_GROUP, qblk + B_PREV), lambda h, j: (h, 0))

    def body(q_ref, g_ref, kp, kc, vp, vc, w_ref, sink_ref, z_ref, o_ref, lse_ref, b_ref):
        kvh = pl.program_id(0)
        j = pl.program_id(1)
        _fill_bias(B_GROUP, lambda h: w_ref[h:h + 1, :], B_BAND, b_ref, j)
        lane_lo = _lane_lo()
        n_pairs = B_GROUP // 2

        def attend(first):
            k_lo, k_hi, v_lo, v_hi = _b_operands(kp, kc, vp, vc, kvh, True)
            for sb in range(per_step):
                no_prev = first and sb == 0
                first_col = B_PREV if no_prev else 0
                keys = slice(sb * qblk + first_col, (sb + 1) * qblk + B_PREV)
                qrows = slice(sb * qblk, (sb + 1) * qblk)
                halves = []
                for hh, sel in enumerate((lane_lo, jnp.logical_not(lane_lo))):
                    kk = (k_lo if hh == 0 else k_hi)[keys]
                    vv = (v_lo if hh == 0 else v_hi)[keys]
                    qm4 = jnp.concatenate(
                        [jnp.where(sel, q_ref[qrows, 128 * pp:128 * (pp + 1)], jnp.zeros((qblk, 128), BF16))
                         for pp in range(n_pairs)], axis=0) * SCALE
                    s4 = _dot_nt(qm4, kk)
                    es, mxs = [], []
                    for pp in range(n_pairs):
                        g = 2 * pp + hh
                        s = s4[pp * qblk:(pp + 1) * qblk] + b_ref[g, :, first_col:]
                        mxs.append(jnp.maximum(jnp.max(s, axis=-1, keepdims=True), sink_ref[kvh * B_GROUP + g]))
                        es.append(jnp.exp(s - mxs[pp]).astype(BF16))
                    r4 = _dot(jnp.concatenate(es, axis=0), jnp.where(sel, vv, jnp.ones_like(vv)))
                    halves.append((r4, mxs))
                for pp in range(n_pairs):
                    cols = slice(128 * pp, 128 * (pp + 1))
                    rows = slice(pp * qblk, (pp + 1) * qblk)
                    mxs = [halves[hh][1][pp] for hh in range(2)]
                    sink_terms = [jnp.exp(sink_ref[kvh * B_GROUP + 2 * pp + hh] - mxs[hh]) for hh in range(2)]
                    o, lse = _normalise_pair([halves[hh][0][rows] for hh in range(2)], mxs, lane_lo, sink_terms)
                    silu, _ = _silu_parts(g_ref[qrows, cols].astype(F32))
                    o_ref[qrows, cols] = o.astype(BF16)
                    z_ref[qrows, cols] = (o * silu).astype(BF16)
                    lse_ref[qrows, cols] = lse

        pl.when(j == 0)(functools.partial(attend, True))
        pl.when(j >= 1)(functools.partial(attend, False))

    out_spec = pl.BlockSpec((step, 512), lambda h, j: (j, h))
    return pl.pallas_call(
        body, name="attn_b_fwd", grid=(B_KV_HEADS, t // step),
        in_specs=[q_spec, g_spec, kp_spec, kc_spec, vp_spec, vc_spec, bias_spec, sink_spec],
        out_specs=[out_spec, out_spec, out_spec],
        out_shape=[SDS((t, D_MODEL), BF16), SDS((t, D_MODEL), BF16), SDS((t, D_MODEL), F32)],
        scratch_shapes=[pltpu.VMEM((B_GROUP, qblk, qblk + B_PREV), F32)],
        compiler_params=_cparams(),
    )(qg, qg, kv, kv, kv, kv, bias, sinks)


def _attn_b_bwd(qg, kv, bias, sinks, out_b, lse, dz, bucket_onehot):
    t = qg.shape[0]
    qblk = B_QBLK_BWD
    keys = qblk + B_PREV
    nq = t // qblk
    per = qblk // B_PREV

    def body(q_ref, g_ref, kp, kc, vp, vc, w_ref, sink_ref, o_ref, lse_ref, dz_ref, oh_ref,
             dqg_ref, dkv_ref, dt5_ref, dsink_ref, gt_acc, b_ref):
        j = pl.program_id(0)
        _fill_bias(N_HEADS, lambda h: w_ref[h:h + 1, :], B_BAND, b_ref, j)

        @pl.when(j == 0)
        def _():
            dkv_ref[...] = jnp.zeros_like(dkv_ref)
            gt_acc[...] = jnp.zeros_like(gt_acc)
            dsink_ref[...] = jnp.zeros_like(dsink_ref)

        lane_lo = _lane_lo()

        def attend(with_prev):
            first_col = 0 if with_prev else B_PREV
            dk_add = jnp.zeros((keys - first_col, 128), F32)
            dv_add = jnp.zeros((keys - first_col, 128), F32)
            for kvh in range(B_KV_HEADS):
                k_lo, k_hi, v_lo, v_hi = _b_operands(kp, kc, vp, vc, kvh, with_prev)
                dk_blk = jnp.zeros((keys - first_col, 128), F32)
                dv_blk = jnp.zeros((keys - first_col, 128), F32)
                for pp in range(B_GROUP // 2):
                    cols = slice(512 * kvh + 128 * pp, 512 * kvh + 128 * (pp + 1))
                    qp = q_ref[:, cols]
                    o = o_ref[:, cols].astype(F32)
                    lse_pair = lse_ref[:, cols]
                    dzf = dz_ref[:, cols].astype(F32)
                    silu, dsilu = _silu_parts(g_ref[:, cols].astype(F32))
                    do = dzf * silu
                    dqg_ref[1, :, cols] = (dzf * o * dsilu).astype(BF16)
                    doo = do * o
                    dqs = []
                    for hh in range(2):
                        g = kvh * B_GROUP + 2 * pp + hh
                        sel = lane_lo if hh == 0 else jnp.logical_not(lane_lo)
                        kk = k_lo if hh == 0 else k_hi
                        vv = v_lo if hh == 0 else v_hi
                        qm = jnp.where(sel, qp, jnp.zeros_like(qp)) * SCALE
                        s = _dot_nt(qm, kk) + b_ref[g, :, first_col:]
                        lse_h = _own_everywhere(lse_pair, sel)
                        p = jnp.exp(_minus_rows(s, lse_h))
                        delta = jnp.sum(jnp.where(sel, doo, 0.0), axis=-1, keepdims=True)
                        dom = jnp.where(sel, do, 0.0).astype(BF16)
                        dp = _dot_nt(dom, vv)
                        ds = p * (dp - delta)
                        gt_acc[g, :, first_col:] += ds
                        dsink_ref[g:g + 1, :] -= jnp.sum(jnp.exp(sink_ref[g] - lse_h) * delta, axis=0, keepdims=True)
                        dsb = ds.astype(BF16)
                        dqs.append(_dot(dsb, kk) * SCALE)
                        dk_blk = dk_blk + _dot_tn(dsb, qm)
                        dv_blk = dv_blk + _dot_tn(p.astype(BF16), dom)
                    dqg_ref[0, :, cols] = jnp.where(lane_lo, dqs[0], dqs[1]).astype(BF16)
                mine = lane_lo if kvh == 0 else jnp.logical_not(lane_lo)
                dk_add = dk_add + jnp.where(mine, dk_blk + pltpu.roll(dk_blk, HEAD_DIM, 1), 0.0)
                dv_add = dv_add + jnp.where(mine, dv_blk + pltpu.roll(dv_blk, HEAD_DIM, 1), 0.0)
            first_key = B_PREV if with_prev else 0
            if with_prev:
                rows = pl.ds(pl.multiple_of(j * qblk - B_PREV, B_PREV), B_PREV)
                dkv_ref[0, rows, :] += dk_add[0:B_PREV]
                dkv_ref[1, rows, :] += dv_add[0:B_PREV]
            rows = pl.ds(pl.multiple_of(j * qblk, qblk), qblk)
            dkv_ref[0, rows, :] += dk_add[first_key:first_key + qblk]
            dkv_ref[1, rows, :] += dv_add[first_key:first_key + qblk]

        pl.when(j == 0)(functools.partial(attend, False))
        pl.when(j >= 1)(functools.partial(attend, True))

        @pl.when(j == nq - 1)
        def _():
            dd = jnp.concatenate([_offset_sums(_collapse_chunks(gt_acc[g], keys)) for g in range(N_HEADS)], axis=0)
            hi = dd.astype(BF16)
            lo = (dd - hi.astype(F32)).astype(BF16)
            dt5_ref[...] = _dot(hi, oh_ref[...]) + _dot(lo, oh_ref[...])

    wide = lambda col: pl.BlockSpec((qblk, D_MODEL), lambda j, col=col: (j, col))
    prev = lambda col: pl.BlockSpec((B_PREV, 128), lambda j, col=col: (jnp.maximum(per * j - 1, 0), col))
    cur = lambda col: pl.BlockSpec((qblk, 128), lambda j, col=col: (j, col))
    fixed = lambda shape: pl.BlockSpec(shape, lambda j: (0,) * len(shape))
    return pl.pallas_call(
        body, name="attn_b_bwd", grid=(nq,),
        in_specs=[wide(0), wide(1), prev(0), cur(0), prev(1), cur(1), fixed((N_HEADS, keys)),
                  pl.BlockSpec(memory_space=pltpu.SMEM), wide(0), wide(0), wide(0), fixed((keys, 128))],
        out_specs=[pl.BlockSpec((2, qblk, D_MODEL), lambda j: (0, j, 0)), fixed((2, t, 128)),
                   fixed((N_HEADS, 128)), fixed((N_HEADS, 128))],
        out_shape=[SDS((2, t, D_MODEL), BF16), SDS((2, t, 128), F32),
                   SDS((N_HEADS, 128), F32), SDS((N_HEADS, 128), F32)],
        scratch_shapes=[pltpu.VMEM((N_HEADS, qblk, keys), F32), pltpu.VMEM((N_HEADS, qblk, keys), F32)],
        compiler_params=_cparams(),
    )(qg, qg, kv, kv, kv, kv, bias, sinks, out_b, lse, dz, bucket_onehot)


def _a_bias_by_offset(rel_bias):
    m = np.arange(A_DIAG)
    idx = np.clip(A_BAND - 1 - m, -A_REL_CLIP, A_REL_CLIP) + A_REL_CLIP
    by_head = rel_bias[idx].T.reshape(N_HEADS // 2, 2, A_DIAG)
    return jnp.concatenate([by_head, jnp.zeros((N_HEADS // 2, 6, A_DIAG), F32)], axis=1)


def _a_bias_grad(offset_sums):
    first = 319
    tail = jnp.sum(offset_sums[:, :first], axis=1)
    body = jnp.flip(offset_sums[:, first:first + 320], axis=1)
    body = body.at[:, -1].add(tail)
    full = jnp.concatenate([jnp.zeros((N_HEADS, 193), F32), body], axis=1)
    return full


def _t5_bucket(rel):
    nb = T5_BUCKETS // 2
    max_exact = nb // 2
    ret = jnp.where(rel > 0, nb, 0)
    n = jnp.abs(rel)
    nf = jnp.maximum(n, 1).astype(jnp.float32)
    large = max_exact + (jnp.log(nf / max_exact) / math.log(T5_MAX_DIST / max_exact)
                         * (nb - max_exact)).astype(jnp.int32)
    large = jnp.minimum(large, nb - 1)
    return ret + jnp.where(n < max_exact, n, large)


def _b_offset_buckets(keys):
    return _t5_bucket(jnp.arange(keys, dtype=jnp.int32) - (B_LEFT_CHUNKS * CHUNK + CHUNK - 1))


def _b_bias_by_offset(t5_table, keys):
    return t5_table[_b_offset_buckets(keys)].T


def _b_bucket_onehot(keys):
    return (_b_offset_buckets(keys)[:, None] == jnp.arange(128)[None, :]).astype(BF16)


def _local_step(my_slot, order, x, target, a_gain_shard, w_in_a_shard, rel_bias, late_shards, kv_gain,
                t5_table, b_gain, sinks, f_gain):
    a_bias = _a_bias_by_offset(rel_bias)
    b_bias_fwd = _b_bias_by_offset(t5_table, B_QBLK_FWD + B_PREV)
    b_bias_bwd = _b_bias_by_offset(t5_table, B_QBLK_BWD + B_PREV)
    sinks_flat = sinks.reshape(N_HEADS)

    xn, qkvg, w_in_a, a_gain = _norm_matmul_gather(order, x, a_gain_shard, w_in_a_shard)
    z_a, out_a, lse_a, (w_in_b, w_out_a, w_out_b, kv_w) = _attn_a_fwd(qkvg, a_bias, late_shards)
    w_out_a = w_out_a.reshape(D_MODEL, D_MODEL)
    w_out_b = w_out_b.reshape(D_MODEL, D_MODEL)
    kv_w = kv_w.reshape(D_MODEL, 2 * 128)
    h1, kvn, hb, kv, qg = _layer_a_out(x, z_a, w_out_a, kv_gain, b_gain, kv_w, w_in_b)
    z_b, out_b, lse_b = _attn_b_fwd(qg, kv, b_bias_fwd, sinks_flat)
    dh2, dh2b, dz_b, loss, d_fn = _layer_b_out_loss(h1, z_b, w_out_b, f_gain, target)

    dqg_b, dkv_b, d_t5, d_sink = _attn_b_bwd(qg, kv, b_bias_bwd, sinks_flat, out_b, lse_b, dz_b,
                                             _b_bucket_onehot(B_QBLK_BWD + B_PREV))
    dh1, dh1b, dz_a, d_bn, d_kn = _layer_b_in_bwd(dqg_b, dkv_b, w_in_b, kv_w, h1, dh2, b_gain, kv_gain, w_out_a)
    early = dict(
        b_w_out=_weight_grad_rows("grad_b_w_out", my_slot, z_b, dh2b[None]),
        b_w_in=_weight_grad_cols("grad_b_w_in", my_slot, hb, [dqg_b],
                                 [(0, o, c, 4 * o + c) for o in range(2) for c in range(4)], 256),
        kv_w=_weight_grad_rows("grad_kv_w", my_slot, kvn, dkv_b),
        a_w_out=_weight_grad_rows("grad_a_w_out", my_slot, z_a, dh1b[None]))
    dqg_a, dkv_a, d_rel, landed = _attn_a_bwd(qkvg, a_bias, out_a, lse_a, dz_a, [g[0] for g in early.values()])
    ready = dict(
        loss=loss, a_rel_bias=d_rel[:, :2].reshape(N_HEADS, A_DIAG),
        kv_norm=d_kn, t5_bias=d_t5, b_norm=d_bn, b_sinks=d_sink, final_norm=d_fn)
    g_own, from_sibling, from_far, chip_sums, ready_sums = _grad_a_w_in_reduce(
        _a_w_in_grad_order(), xn, dqg_a, dkv_a, list(ready.values()))
    grad_x, d_an, from_near = _layer_a_in_bwd(dqg_a, dkv_a, w_in_a, x, dh1, a_gain, chip_sums)

    matrices = {n: (g[1], [(land, 0, N_DEV - 1)]) for (n, g), land in zip(early.items(), landed)}
    matrices["a_w_in"] = (g_own, [(from_sibling, 0, 1), (from_far, 0, from_far.shape[0]),
                                  (from_near, 0, from_near.shape[0])])
    small = dict(zip(ready.keys(), ready_sums), a_norm=d_an)
    return grad_x, small, matrices


def _place():
    x, y, c = lax.axis_index("x"), lax.axis_index("y"), lax.axis_index("c")
    chips = [(1 - x, y), (x, 1 - y), (1 - x, 1 - y)]
    return x, y, c, chips


def _slot(px, py, pc):
    return 4 * px + 2 * py + pc


ANY = pl.BlockSpec(memory_space=pl.ANY)


def _peer(x, y, c, k):
    return (x ^ (k >> 2), y ^ ((k >> 1) & 1), c ^ (k & 1))


def _scatter_copies(grad_refs, land_refs, send_sems, recv_sems):
    x, y, c, _ = _place()
    copies = []
    for t, (grad, land) in enumerate(zip(grad_refs, land_refs)):
        for k in range(1, N_DEV):
            peer = _peer(x, y, c, k)
            sem = (N_DEV - 1) * t + k - 1
            copies.append(pltpu.make_async_remote_copy(
                src_ref=grad.at[_slot(*peer)], dst_ref=land.at[k - 1],
                send_sem=send_sems.at[sem], recv_sem=recv_sems.at[sem],
                device_id=peer, device_id_type=MESH))
    return copies


def _gather_phases(ins, outs, send_sems, recv_sems, local_sems):
    n = len(ins)
    x, y, c, chips = _place()
    me, sibling = (x, y, c), (x, y, 1 - c)

    def copy(t, k, block, to, src=None):
        dst = outs[t].at[_slot(*block)]
        return pltpu.make_async_remote_copy(
            src_ref=dst if src is None else src, dst_ref=dst,
            send_sem=send_sems.at[7 * t + k], recv_sem=recv_sems.at[7 * t + k],
            device_id=to, device_id_type=MESH)

    def lists():
        mine = [pltpu.make_async_copy(ins[t], outs[t].at[_slot(*me)], local_sems.at[t]) for t in range(n)]
        first = []
        for t in range(n):
            first.append(copy(t, 0, me, sibling, src=ins[t]))
            first += [copy(t, 1 + j, me, (*chip, c), src=ins[t]) for j, chip in enumerate(chips)]
        passed = [copy(t, 4 + j, (*chip, c), sibling) for t in range(n) for j, chip in enumerate(chips)]
        return mine, first, passed

    def start():
        mine, first, _ = lists()
        for cp in mine + first:
            cp.start()

    def forward():
        _, _, passed = lists()
        for t in range(n):
            for j, chip in enumerate(chips):
                copy(t, 1 + j, (*chip, c), me).wait_recv()
                passed[3 * t + j].start()

    def finish():
        mine, first, passed = lists()
        for t in range(n):
            copy(t, 0, sibling, me).wait_recv()
            for j, chip in enumerate(chips):
                copy(t, 4 + j, (*chip, 1 - c), me).wait_recv()
        for cp in first + passed:
            cp.wait_send()
        for cp in mine:
            cp.wait()

    return start, forward, finish


def _gather_scratch(n):
    return [pltpu.SemaphoreType.DMA((7 * n,)), pltpu.SemaphoreType.DMA((7 * n,)), pltpu.SemaphoreType.DMA((n,))]


_FAR_CHIP_FIRST = (2, 0, 1)
_SUMS_SENT_AT_ONCE = 1


def _a_w_in_grad_order():
    x, y, c, chips = _place()
    slots = []
    for j in _FAR_CHIP_FIRST:
        slots += [_slot(*chips[j], 1 - c), _slot(*chips[j], c)]
    slots += [_slot(x, y, 1 - c), _slot(x, y, c)]

    def kept(reads):
        out, nxt = [None] * N_DEV, slots[-1]
        for s in reversed(range(N_DEV)):
            nxt = jnp.where(reads(slots[s]), slots[s], nxt)
            out[s] = nxt
        return out

    from_dqg = lambda b: jnp.logical_or(b < 2, b >= 6)
    return jnp.stack(slots + kept(from_dqg) + kept(lambda b: jnp.logical_not(from_dqg(b)))).astype(jnp.int32)


def _grad_a_w_in_reduce(order, a, dqg, dkv, small):
    t, dw = a.shape
    tn = dqg.shape[2] // 2
    n_s = len(small)
    n_far, n_sent = len(_FAR_CHIP_FIRST), _SUMS_SENT_AT_ONCE

    def body(order_ref, a_ref, dqg_ref, dkv_ref, *rest):
        small_refs, rest = rest[:n_s], rest[n_s:]
        own_ref, sib_ref, chips_ref, later_ref = rest[:4]
        small_out, rest = rest[4:4 + n_s], rest[4 + n_s:]
        a_buf, at_ref, res_ref, stage, land, load_sem, d2d_send, d2d_recv, ici_send, ici_recv = rest[:10]
        small_lands, (small_send, small_recv) = rest[10:10 + n_s], rest[10 + n_s:]
        s = pl.program_id(0)
        x, y, c, chips = _place()

        def to_sibling(i):
            return pltpu.make_async_remote_copy(
                src_ref=stage.at[i], dst_ref=land.at[i] if i < n_far else sib_ref.at[0],
                send_sem=d2d_send.at[i], recv_sem=d2d_recv.at[i], device_id=(x, y, 1 - c), device_id_type=MESH)

        def to_chip(i):
            return pltpu.make_async_remote_copy(
                src_ref=land.at[i], dst_ref=chips_ref.at[i], send_sem=ici_send.at[i], recv_sem=ici_recv.at[i],
                device_id=(*chips[_FAR_CHIP_FIRST[i]], c), device_id_type=MESH)

        @pl.when(s == 0)
        def _():
            load = pltpu.make_async_copy(a_ref, a_buf, load_sem)
            load.start()
            _all_reduce_small_start(small_refs, small_lands, small_send, small_recv)
            load.wait()
            at_ref[...] = a_buf[...].T

        blk = order_ref[s]
        from_qg = jnp.logical_or(blk < 2, blk >= 6)

        @pl.when(from_qg)
        def _():
            res_ref[...] = _dot(at_ref[...], dqg_ref[0])

        @pl.when(jnp.logical_not(from_qg))
        def _():
            res_ref[...] = _dot(at_ref[...], dkv_ref[0])

        for i in range(n_far + 1):
            @pl.when(s == 2 * i)
            def _(i=i):
                stage[i] = res_ref[...].astype(BF16)
                to_sibling(i).start()

        for i in range(n_far):
            @pl.when(s == 2 * i + 1)
            def _(i=i):
                to_sibling(i).wait_recv()
                total = (res_ref[...] + land[i].astype(F32)).astype(BF16)
                if i < n_sent:
                    land[i] = total
                    to_chip(i).start()
                else:
                    later_ref[i - n_sent] = total

        @pl.when(s == N_DEV - 1)
        def _():
            own_ref[...] = res_ref[...]
            _all_reduce_small_finish(small_refs, small_out, small_lands, small_send, small_recv)
            for i in range(n_far + 1):
                to_sibling(i).wait_send()
            to_sibling(n_far).wait_recv()
            for i in range(n_sent):
                to_chip(i).wait()

    whole = pl.BlockSpec(memory_space=pltpu.VMEM)
    outs = pl.pallas_call(
        body, name="grad_a_w_in",
        grid_spec=pltpu.PrefetchScalarGridSpec(
            num_scalar_prefetch=1, grid=(N_DEV,),
            in_specs=[ANY,
                      pl.BlockSpec((1, t, tn), lambda s, o: (o[N_DEV + s] // 6, 0, o[N_DEV + s] % 2)),
                      pl.BlockSpec((1, t, tn), lambda s, o: ((o[2 * N_DEV + s] // 4) % 2, 0, o[2 * N_DEV + s] % 2))]
            + [whole] * n_s,
            out_specs=[pl.BlockSpec((dw, tn), lambda s, o: (0, 0)), ANY, ANY, whole] + [whole] * n_s,
            scratch_shapes=[pltpu.VMEM((t, dw), BF16), pltpu.VMEM((dw, t), BF16), pltpu.VMEM((dw, tn), F32),
                            pltpu.VMEM((n_far + 1, dw, tn), BF16), pltpu.VMEM((n_far, dw, tn), BF16),
                            pltpu.SemaphoreType.DMA,
                            pltpu.SemaphoreType.DMA((n_far + 1,)), pltpu.SemaphoreType.DMA((n_far + 1,)),
                            pltpu.SemaphoreType.DMA((n_sent,)), pltpu.SemaphoreType.DMA((n_sent,))]
            + _all_reduce_scratch([s.shape for s in small])),
        out_shape=[SDS((dw, tn), F32), SDS((1, dw, tn), BF16), SDS((n_sent, dw, tn), BF16),
                   SDS((n_far - n_sent, dw, tn), BF16)] + [SDS(s.shape, F32) for s in small],
        compiler_params=_cparams(),
    )(order, a, dqg, dkv, *small)
    return outs[0], outs[1], outs[2], outs[3], list(outs[4:])


def _later_chip_copies(sums_ref, land_ref, send_sems, recv_sems):
    x, y, c, chips = _place()
    del x, y
    return [pltpu.make_async_remote_copy(
        src_ref=sums_ref.at[i], dst_ref=land_ref.at[i], send_sem=send_sems.at[i], recv_sem=recv_sems.at[i],
        device_id=(*chips[j], c), device_id_type=MESH) for i, j in enumerate(_FAR_CHIP_FIRST[_SUMS_SENT_AT_ONCE:])]


def _row_tile(rows):
    return min(rows, 512)


def _adamw(w, g, m, v):
    m2 = ADAM_B1 * m + (1.0 - ADAM_B1) * g
    v2 = ADAM_B2 * v + (1.0 - ADAM_B2) * jnp.square(g)
    m_hat = m2 / (1.0 - ADAM_B1 ** ADAM_STEP)
    v_hat = v2 / (1.0 - ADAM_B2 ** ADAM_STEP)
    delta = -ADAM_LR * (m_hat / (jnp.sqrt(v_hat) + ADAM_EPS) + ADAM_WD * w)
    return delta, m2, v2


def _reduce_adamw(name, own, partials, w, m, v):
    r, c = own.shape
    tr = _row_tile(r)
    n_p = len(partials)

    def body(own_ref, *rest):
        p_refs, (w_ref, m_ref, v_ref, grad_ref, d_ref, nm_ref, nv_ref) = rest[:n_p], rest[n_p:]
        grad = own_ref[...]
        for p_ref, (_, _, count) in zip(p_refs, partials):
            for j in range(count):
                grad = grad + p_ref[j].astype(F32)
        grad_ref[...] = grad
        d_ref[...], nm_ref[...], nv_ref[...] = _adamw(w_ref[...], grad, m_ref[...], v_ref[...])

    flat = pl.BlockSpec((tr, c), lambda i: (i, 0))
    return pl.pallas_call(
        body, name=name, grid=(r // tr,),
        in_specs=[flat] + [pl.BlockSpec((count, tr, c), lambda i, first=first, count=count: (first // count, i, 0))
                           for _, first, count in partials] + [flat, flat, flat],
        out_specs=[flat, flat, flat, flat],
        out_shape=[SDS((r, c), F32)] * 4,
        compiler_params=_cparams(),
    )(own, *[p[0] for p in partials], w, m, v)


VM = pl.BlockSpec()


def _all_reduce_small(ins, outs, lands, send_sems, recv_sems):
    _all_reduce_small_start(ins, lands, send_sems, recv_sems)
    _all_reduce_small_finish(ins, outs, lands, send_sems, recv_sems)


def _all_reduce_small_sends(ins, lands, send_sems, recv_sems):
    x, y, c, _ = _place()
    return [pltpu.make_async_remote_copy(
        src_ref=src, dst_ref=land.at[_slot(x, y, c)],
        send_sem=send_sems.at[(N_DEV - 1) * t + k - 1], recv_sem=recv_sems.at[(N_DEV - 1) * t + k - 1],
        device_id=_peer(x, y, c, k), device_id_type=MESH)
        for t, (src, land) in enumerate(zip(ins, lands)) for k in range(1, N_DEV)]


def _all_reduce_small_start(ins, lands, send_sems, recv_sems):
    x, y, c, _ = _place()
    for src, land in zip(ins, lands):
        land[_slot(x, y, c)] = src[...]
    for cp in _all_reduce_small_sends(ins, lands, send_sems, recv_sems):
        cp.start()


def _all_reduce_small_finish(ins, outs, lands, send_sems, recv_sems):
    x, y, c, _ = _place()
    copies = _all_reduce_small_sends(ins, lands, send_sems, recv_sems)
    for t, (src, land) in enumerate(zip(ins, lands)):
        for k in range(1, N_DEV):
            sem = (N_DEV - 1) * t + k - 1
            pltpu.make_async_remote_copy(
                src_ref=src, dst_ref=land.at[_slot(*_peer(x, y, c, k))],
                send_sem=send_sems.at[sem], recv_sem=recv_sems.at[sem],
                device_id=(x, y, c), device_id_type=MESH).wait_recv()
    for cp in copies:
        cp.wait_send()
    for out, land in zip(outs, lands):
        total = land[0]
        for s in range(1, N_DEV):
            total = total + land[s]
        out[...] = total


def _all_reduce_scratch(shapes):
    n_sems = (N_DEV - 1) * len(shapes)
    return ([pltpu.VMEM((N_DEV, *s), F32) for s in shapes]
            + [pltpu.SemaphoreType.DMA((n_sems,)), pltpu.SemaphoreType.DMA((n_sems,))])


def _small_adamw(my_slot, sums, ws, ms, vs):
    n = len(ws)

    def body(slot_ref, *refs):
        sum_refs, refs = refs[:n + 1], refs[n + 1:]
        w_refs, m_refs, v_refs, refs = refs[:n], refs[n:2 * n], refs[2 * n:3 * n], refs[3 * n:]
        g_refs, d_refs, nm_refs, nv_refs = refs[:n + 1], refs[n + 1:2 * n + 1], refs[2 * n + 1:3 * n + 1], refs[3 * n + 1:]
        for t in range(n + 1):
            if t == 0:
                g = sum_refs[0][:, pl.ds(pl.multiple_of(slot_ref[0] * 128, 128), 128)]
            else:
                g = sum_refs[t][...]
            g_refs[t][...] = g
            if t < n:
                d_refs[t][...], nm_refs[t][...], nv_refs[t][...] = _adamw(w_refs[t][...], g, m_refs[t][...], v_refs[t][...])

    shapes = [SDS(w.shape, F32) for w in ws]
    outs = pl.pallas_call(
        body, name="small_adamw",
        in_specs=[pl.BlockSpec(memory_space=pltpu.SMEM)] + [VM] * (4 * n + 1),
        out_specs=[VM] * (4 * n + 1),
        out_shape=shapes + [SDS(sums[-1].shape, F32)] + shapes * 3,
    )(my_slot, *sums, *ws, *ms, *vs)
    return outs[:n + 1], outs[n + 1:2 * n + 1], outs[2 * n + 1:3 * n + 1], outs[3 * n + 1:]


def kernel(x, a_norm, a_w_in, a_rel_bias, a_w_out, kv_norm, kv_w, t5_bias, b_norm, b_w_in, b_sinks, b_w_out, final_norm, loss_target, m_a_norm, m_a_w_in, m_a_rel_bias, m_a_w_out, m_kv_norm, m_kv_w, m_t5_bias, m_b_norm, m_b_w_in, m_b_sinks, m_b_w_out, m_final_norm, v_a_norm, v_a_w_in, v_a_rel_bias, v_a_w_out, v_kv_norm, v_kv_w, v_t5_bias, v_b_norm, v_b_w_in, v_b_sinks, v_b_w_out, v_final_norm):
    xi, yi, ci = lax.axis_index("x"), lax.axis_index("y"), lax.axis_index("c")
    my_slot = _slot(xi, yi, ci)

    slot_arr = jnp.reshape(my_slot, (1,)).astype(jnp.int32)
    order = _gather_order(xi, yi, ci)
    late_shards = [b_w_in[0].astype(BF16), a_w_out[0].astype(BF16), b_w_out[0].astype(BF16), kv_w.astype(BF16)]
    grad_x, loc, matrices = _local_step(
        slot_arr, order, x[0], loss_target[0], a_norm, a_w_in[0].astype(BF16), a_rel_bias[0], late_shards,
        kv_norm.reshape(1, D_MODEL), t5_bias, b_norm, b_sinks, final_norm.reshape(1, D_MODEL))

    shard_w = dict(a_w_in=a_w_in[0], b_w_in=b_w_in[0], a_w_out=a_w_out[0], b_w_out=b_w_out[0], kv_w=kv_w)
    shard_m = dict(a_w_in=m_a_w_in[0], b_w_in=m_b_w_in[0], a_w_out=m_a_w_out[0], b_w_out=m_b_w_out[0], kv_w=m_kv_w)
    shard_v = dict(a_w_in=v_a_w_in[0], b_w_in=v_b_w_in[0], a_w_out=v_a_w_out[0], b_w_out=v_b_w_out[0], kv_w=v_kv_w)
    big = {n: _reduce_adamw("adamw_" + n, own, partials, shard_w[n], shard_m[n], shard_v[n])
           for n, (own, partials) in matrices.items()}

    names = ("a_norm", "a_rel_bias", "kv_norm", "t5_bias", "b_norm", "b_sinks", "final_norm")
    tables = ("a_rel_bias", "t5_bias")

    def row(n, a):
        return a.reshape(-1, a.shape[-1]).T if n in tables else a.reshape(1, -1)

    small_w = [row(n, a) for n, a in zip(names, (a_norm, a_rel_bias, kv_norm, t5_bias, b_norm, b_sinks, final_norm))]
    small_m = [row(n, a) for n, a in zip(names, (m_a_norm, m_a_rel_bias, m_kv_norm, m_t5_bias, m_b_norm, m_b_sinks,
                                                 m_final_norm))]
    small_v = [row(n, a) for n, a in zip(names, (v_a_norm, v_a_rel_bias, v_kv_norm, v_t5_bias, v_b_norm, v_b_sinks,
                                                 v_final_norm))]
    sums = dict(loc)
    sums["a_rel_bias"] = _a_bias_grad(sums["a_rel_bias"])
    sums["t5_bias"] = sums["t5_bias"][:, :T5_BUCKETS]
    sums["b_sinks"] = sums["b_sinks"][:, 0].reshape(1, N_HEADS)
    results = _small_adamw(slot_arr, [sums[n] for n in names + ("loss",)], small_w, small_m, small_v)
    like = dict(a_norm=a_norm, a_rel_bias=a_rel_bias, kv_norm=kv_norm, t5_bias=t5_bias, b_norm=b_norm,
                b_sinks=b_sinks, final_norm=final_norm)
    sm = [{n: (part[i].T if n in tables else part[i]).reshape(like[n].shape) for i, n in enumerate(names)}
          for part in results]
    loss = results[0][len(names)][0, 0]

    order = ("a_norm", "a_w_in", "a_rel_bias", "a_w_out", "kv_norm", "kv_w", "t5_bias", "b_norm",
             "b_w_in", "b_sinks", "b_w_out", "final_norm")
    lead = dict(a_w_in=True, b_w_in=True, a_w_out=True, b_w_out=True, kv_w=False)

    def pick(kind, name):
        if name in big:
            val = big[name][kind]
            return val[None] if lead[name] else val
        return sm[kind][name]

    outs = [loss, grad_x[None]]
    for kind in range(4):
        outs += [pick(kind, n) for n in order]
    return tuple(outs)
```

```python
import functools
import math

import numpy as np
import jax
import jax.numpy as jnp
from jax import lax
from jax.experimental import pallas as pl
from jax.experimental.pallas import tpu as pltpu

F32 = jnp.float32
BF16 = jnp.bfloat16
SDS = jax.ShapeDtypeStruct

D_MODEL = 1024
HEAD_DIM = 64
CHUNK = 64
N_HEADS = 16
RMS_EPS = 1e-6
A_LEFT_CHUNKS = 8
A_BAND = (A_LEFT_CHUNKS + 1) * CHUNK
A_REL_CLIP = 256
B_KV_HEADS = 2
B_GROUP = 8
B_LEFT_CHUNKS = 2
B_BAND = (B_LEFT_CHUNKS + 1) * CHUNK
T5_BUCKETS = 32
T5_MAX_DIST = 128
QBLK = 256
A_KEYS = 3 * QBLK
B_QBLK_FWD = 128
B_QBLK_BWD = 256
B_PREV = 128
A_DIAG = A_KEYS
NEG = -1e30
SCALE = HEAD_DIM ** -0.5
N_DEV = 8

ADAM_LR = 0.001
ADAM_B1 = 0.9
ADAM_B2 = 0.999
ADAM_EPS = 1e-08
ADAM_WD = 0.01
ADAM_STEP = 10

VMEM_LIMIT_BYTES = 56 * 1024 * 1024
MESH = pl.DeviceIdType.MESH


def _cparams():
    return pltpu.CompilerParams(vmem_limit_bytes=VMEM_LIMIT_BYTES)


def _dot(a, b):
    return jnp.dot(a, b, preferred_element_type=F32)


def _dot_nt(a, b):
    return lax.dot_general(a, b, (((1,), (1,)), ((), ())), preferred_element_type=F32)


def _dot_tn(a, b):
    return lax.dot_general(a, b, (((0,), (0,)), ((), ())), preferred_element_type=F32)


def _rstd(xf):
    return lax.rsqrt(jnp.mean(xf * xf, axis=-1, keepdims=True) + RMS_EPS)


def _sigmoid(x):
    return 1.0 / (1.0 + jnp.exp(-x))


_GATHER_SEQUENCE = ((0, None), (1, 0), (2, 1), (4, None), (5, None), (3, 2), (6, None))


def _gather_order(x, y, c):
    others = [(1 - x, y), (x, 1 - y), (1 - x, 1 - y)]
    arrivals = [_slot(x, y, 1 - c)] + [_slot(*chip, c) for chip in others] + [_slot(*chip, 1 - c) for chip in others]
    return jnp.stack([_slot(x, y, c)] + [arrivals[a] for a, _ in _GATHER_SEQUENCE]).astype(jnp.int32)


def _norm_matmul_gather(order, x, gain_shard, w_shard):
    t = x.shape[0]
    dw, tn = w_shard.shape
    tm = min(t, 2048)
    n_m = t // tm

    def body(order_ref, x_ref, gs_ref, shard_ref, xn_ref, o_ref, full_ref, gain_ref,
             xn_all, wbuf, gland, send_sems, recv_sems, gsend_sems, grecv_sems, load_sems, own_sem):
        n, m = pl.program_id(0), pl.program_id(1)
        x_i, y_i, c_i, chips = _place()
        me, sibling = (x_i, y_i, c_i), (x_i, y_i, 1 - c_i)

        def send(k, block, to, src=None):
            dst = full_ref.at[_slot(*block)]
            return pltpu.make_async_remote_copy(
                src_ref=dst if src is None else src, dst_ref=dst,
                send_sem=send_sems.at[k], recv_sem=recv_sems.at[k], device_id=to, device_id_type=MESH)

        own = pltpu.make_async_copy(shard_ref, full_ref.at[_slot(*me)], own_sem)
        first = [send(0, me, sibling, src=shard_ref)]
        first += [send(1 + j, me, (*chip, c_i), src=shard_ref) for j, chip in enumerate(chips)]
        forwards = [send(4 + j, (*chip, c_i), sibling) for j, chip in enumerate(chips)]
        arrivals = [send(0, sibling, me)] + [send(1 + j, (*chip, c_i), me) for j, chip in enumerate(chips)]
        arrivals += [send(4 + j, (*chip, 1 - c_i), me) for j, chip in enumerate(chips)]
        gains = [pltpu.make_async_remote_copy(
            src_ref=gs_ref, dst_ref=gland.at[_slot(*me)], send_sem=gsend_sems.at[k - 1],
            recv_sem=grecv_sems.at[k - 1], device_id=_peer(x_i, y_i, c_i, k), device_id_type=MESH)
            for k in range(1, N_DEV)]

        @pl.when(jnp.logical_and(n == 0, m == 0))
        def _():
            own.start()
            for cp in gains + first:
                cp.start()
            pltpu.make_async_copy(shard_ref, wbuf.at[0], load_sems.at[0]).start()
            gland[_slot(*me)] = gs_ref[...]
            for k in range(1, N_DEV):
                pltpu.make_async_remote_copy(
                    src_ref=gs_ref, dst_ref=gland.at[_slot(*_peer(x_i, y_i, c_i, k))],
                    send_sem=gsend_sems.at[k - 1], recv_sem=grecv_sems.at[k - 1],
                    device_id=me, device_id_type=MESH).wait_recv()
            for s in range(N_DEV):
                gain_ref[:, 128 * s:128 * (s + 1)] = gland[s]

        rows = pl.ds(pl.multiple_of(m * tm, tm), tm)

        @pl.when(n == 0)
        def _():
            xf = x_ref[...]
            xn = ((xf * _rstd(xf)) * gain_ref[...]).astype(BF16)
            xn_all[rows, :] = xn
            xn_ref[...] = xn

        @pl.when(m == 0)
        def _():
            pltpu.make_async_copy(full_ref.at[0], wbuf.at[n % 2], load_sems.at[n % 2]).wait()

        o_ref[...] = _dot(xn_all[rows, :], wbuf[n % 2]).astype(BF16)

        for k, (arrival, forward) in enumerate(_GATHER_SEQUENCE):
            @pl.when(jnp.logical_and(n == k, m == n_m - 1))
            def _(k=k, arrival=arrival, forward=forward):
                arrivals[arrival].wait_recv()
                if forward is not None:
                    forwards[forward].start()
                pltpu.make_async_copy(full_ref.at[order_ref[k + 1]], wbuf.at[(k + 1) % 2],
                                      load_sems.at[(k + 1) % 2]).start()

        @pl.when(jnp.logical_and(n == N_DEV - 1, m == n_m - 1))
        def _():
            for cp in gains + first + forwards:
                cp.wait_send()
            own.wait()

    held = lambda n, m, order: (jnp.where(n == 0, m, n_m - 1), 0)
    return pl.pallas_call(
        body, name="norm_matmul_gather",
        grid_spec=pltpu.PrefetchScalarGridSpec(
            num_scalar_prefetch=1, grid=(N_DEV, n_m),
            in_specs=[pl.BlockSpec((tm, D_MODEL), held),
                      pl.BlockSpec((1, 128), lambda n, m, order: (0, 0)), ANY],
            out_specs=[pl.BlockSpec((tm, D_MODEL), held),
                       pl.BlockSpec((tm, tn), lambda n, m, order: (m, order[n])),
                       ANY, pl.BlockSpec((1, D_MODEL), lambda n, m, order: (0, 0))],
            scratch_shapes=[pltpu.VMEM((t, D_MODEL), BF16), pltpu.VMEM((2, dw, tn), BF16),
                            pltpu.VMEM((N_DEV, 1, 128), F32),
                            pltpu.SemaphoreType.DMA((7,)), pltpu.SemaphoreType.DMA((7,)),
                            pltpu.SemaphoreType.DMA((7,)), pltpu.SemaphoreType.DMA((7,)),
                            pltpu.SemaphoreType.DMA((2,)), pltpu.SemaphoreType.DMA]),
        out_shape=[SDS((t, D_MODEL), BF16), SDS((t, N_DEV * tn), BF16), SDS((N_DEV, dw, tn), BF16),
                   SDS((1, D_MODEL), F32)],
        compiler_params=_cparams(),
    )(order, x, gain_shard, w_shard)


def _layer_a_out(x, z, w_out, kv_gain, b_gain, kv_w, w_in_b):
    t = x.shape[0]
    tm = min(t, 1024)
    nb, _, tn = w_in_b.shape

    def body(x_ref, z_ref, wo_ref, kvg_ref, bg_ref, kvw_ref, wb_ref,
             h1_ref, kvn_ref, hb_ref, kv_ref, qg_ref):
        h1 = x_ref[...] + _dot(z_ref[...], wo_ref[...])
        h1_ref[...] = h1
        y0 = h1 * _rstd(h1)
        kvn = (y0 * kvg_ref[...]).astype(BF16)
        hb = (y0 * bg_ref[...]).astype(BF16)
        kvn_ref[...] = kvn
        hb_ref[...] = hb
        kv_ref[...] = _dot(kvn, kvw_ref[...]).astype(BF16)
        for i in range(nb):
            qg_ref[:, i * tn:(i + 1) * tn] = _dot(hb, wb_ref[i]).astype(BF16)

    row = lambda m: (m, 0)
    fix2 = lambda m: (0, 0)
    return pl.pallas_call(
        body, name="layer_a_out", grid=(t // tm,),
        in_specs=[pl.BlockSpec((tm, D_MODEL), row), pl.BlockSpec((tm, D_MODEL), row),
                  pl.BlockSpec((D_MODEL, D_MODEL), fix2),
                  pl.BlockSpec((1, D_MODEL), fix2), pl.BlockSpec((1, D_MODEL), fix2),
                  pl.BlockSpec((D_MODEL, 256), fix2),
                  pl.BlockSpec((nb, D_MODEL, tn), lambda m: (0, 0, 0))],
        out_specs=[pl.BlockSpec((tm, D_MODEL), row), pl.BlockSpec((tm, D_MODEL), row),
                   pl.BlockSpec((tm, D_MODEL), row), pl.BlockSpec((tm, 256), row),
                   pl.BlockSpec((tm, nb * tn), row)],
        out_shape=[SDS((t, D_MODEL), F32), SDS((t, D_MODEL), BF16), SDS((t, D_MODEL), BF16),
                   SDS((t, 256), BF16), SDS((t, nb * tn), BF16)],
        compiler_params=_cparams(),
    )(x, z, w_out, kv_gain, b_gain, kv_w, w_in_b)


def _layer_b_out_loss(h1, z, w_out, f_gain, target):
    t = h1.shape[0]
    tm = min(t, 1024)

    def body(h1_ref, z_ref, wo_ref, fg_ref, tgt_ref,
             dh2_ref, dh2b_ref, dz_ref, loss_ref, dfn_ref):
        @pl.when(pl.program_id(0) == 0)
        def _():
            loss_ref[...] = jnp.zeros_like(loss_ref)
            dfn_ref[...] = jnp.zeros_like(dfn_ref)

        h2 = h1_ref[...] + _dot(z_ref[...], wo_ref[...])
        r = _rstd(h2)
        yn = h2 * r
        fg = fg_ref[...]
        err = yn * fg - tgt_ref[...]
        loss_ref[...] += (0.5 / D_MODEL) * jnp.sum(err * err)
        dy = err * (1.0 / D_MODEL)
        dfn_ref[...] += jnp.sum(dy * yn, axis=0, keepdims=True)
        u = dy * fg
        dh2 = r * u - h2 * ((r * r * r) * jnp.mean(u * h2, axis=-1, keepdims=True))
        dh2_ref[...] = dh2
        dh2b = dh2.astype(BF16)
        dh2b_ref[...] = dh2b
        dz_ref[...] = _dot_nt(dh2b, wo_ref[...]).astype(BF16)

    row = lambda m: (m, 0)
    fix2 = lambda m: (0, 0)
    return pl.pallas_call(
        body, name="layer_b_out_loss", grid=(t // tm,),
        in_specs=[pl.BlockSpec((tm, D_MODEL), row), pl.BlockSpec((tm, D_MODEL), row),
                  pl.BlockSpec((D_MODEL, D_MODEL), fix2), pl.BlockSpec((1, D_MODEL), fix2),
                  pl.BlockSpec((tm, D_MODEL), row)],
        out_specs=[pl.BlockSpec((tm, D_MODEL), row), pl.BlockSpec((tm, D_MODEL), row),
                   pl.BlockSpec((tm, D_MODEL), row), pl.BlockSpec((1, 128), fix2),
                   pl.BlockSpec((1, D_MODEL), fix2)],
        out_shape=[SDS((t, D_MODEL), F32), SDS((t, D_MODEL), BF16), SDS((t, D_MODEL), BF16),
                   SDS((1, 128), F32), SDS((1, D_MODEL), F32)],
        compiler_params=_cparams(),
    )(h1, z, w_out, f_gain, target)


def _layer_b_in_bwd(dqg, dkv, w_in_b, kv_w, h1, dh2, b_gain, kv_gain, w_out_a):
    t = h1.shape[0]
    tm = min(t, 512)
    nb, _, tn = w_in_b.shape
    per = D_MODEL // tn

    def body(dqg_ref, dkv_ref, wb_ref, kvw_ref, h1_ref, dh2_ref, bg_ref, kvg_ref, wo_ref,
             dh1_ref, dh1b_ref, dz_ref, dbn_ref, dkn_ref):
        @pl.when(pl.program_id(0) == 0)
        def _():
            dbn_ref[...] = jnp.zeros_like(dbn_ref)
            dkn_ref[...] = jnp.zeros_like(dkn_ref)

        dhb = jnp.zeros((tm, D_MODEL), F32)
        for i in range(nb):
            blk = dqg_ref[i // per, :, (i % per) * tn:(i % per + 1) * tn]
            dhb = dhb + _dot_nt(blk, wb_ref[i])
        dkn = (_dot_nt(dkv_ref[0].astype(BF16), kvw_ref[:, 0:128])
               + _dot_nt(dkv_ref[1].astype(BF16), kvw_ref[:, 128:256]))
        h1 = h1_ref[...]
        r = _rstd(h1)
        xr = h1 * r
        dbn_ref[...] += jnp.sum(dhb * xr, axis=0, keepdims=True)
        dkn_ref[...] += jnp.sum(dkn * xr, axis=0, keepdims=True)
        u = dhb * bg_ref[...] + dkn * kvg_ref[...]
        dh1 = dh2_ref[...] + r * u - h1 * ((r * r * r) * jnp.mean(u * h1, axis=-1, keepdims=True))
        dh1_ref[...] = dh1
        dh1b = dh1.astype(BF16)
        dh1b_ref[...] = dh1b
        dz_ref[...] = _dot_nt(dh1b, wo_ref[...]).astype(BF16)

    row = lambda m: (m, 0)
    fix2 = lambda m: (0, 0)
    return pl.pallas_call(
        body, name="layer_b_in_bwd", grid=(t // tm,),
        in_specs=[pl.BlockSpec((2, tm, D_MODEL), lambda m: (0, m, 0)),
                  pl.BlockSpec((2, tm, 128), lambda m: (0, m, 0)),
                  pl.BlockSpec((nb, D_MODEL, tn), lambda m: (0, 0, 0)),
                  pl.BlockSpec((D_MODEL, 256), fix2),
                  pl.BlockSpec((tm, D_MODEL), row), pl.BlockSpec((tm, D_MODEL), row),
                  pl.BlockSpec((1, D_MODEL), fix2), pl.BlockSpec((1, D_MODEL), fix2),
                  pl.BlockSpec((D_MODEL, D_MODEL), fix2)],
        out_specs=[pl.BlockSpec((tm, D_MODEL), row), pl.BlockSpec((tm, D_MODEL), row),
                   pl.BlockSpec((tm, D_MODEL), row), pl.BlockSpec((1, D_MODEL), fix2),
                   pl.BlockSpec((1, D_MODEL), fix2)],
        out_shape=[SDS((t, D_MODEL), F32), SDS((t, D_MODEL), BF16), SDS((t, D_MODEL), BF16),
                   SDS((1, D_MODEL), F32), SDS((1, D_MODEL), F32)],
        compiler_params=_cparams(),
    )(dqg, dkv, w_in_b, kv_w, h1, dh2, b_gain, kv_gain, w_out_a)


def _layer_a_in_bwd(dqg, dkv, w_in_a, x, dh1, a_gain, chip_sums):
    t = x.shape[0]
    tm = min(t, 512)
    nb, _, tn = w_in_a.shape
    per = D_MODEL // tn
    n_sums = chip_sums.shape[0]

    def body(dqg_ref, dkv_ref, w_hbm, x_ref, dh1_ref, ag_ref, sums_ref, dx_ref, dan_ref, land_ref,
             w_ref, w_sems, send_sems, recv_sems, dan_land, dan_send, dan_recv):
        def w_copy(i):
            return pltpu.make_async_copy(w_hbm.at[i], w_ref.at[i], w_sems.at[i])

        @pl.when(pl.program_id(0) == 0)
        def _():
            for i in range(nb):
                w_copy(i).start()
            dan_ref[...] = jnp.zeros_like(dan_ref)
            for cp in _later_chip_copies(sums_ref, land_ref, send_sems, recv_sems):
                cp.start()

        dxn = jnp.zeros((tm, D_MODEL), F32)
        for i in range(nb):
            @pl.when(pl.program_id(0) == 0)
            def _(i=i):
                w_copy(i).wait()

            part = i // per
            src = dqg_ref if part in (0, 3) else dkv_ref
            outer = {0: 0, 3: 1, 1: 0, 2: 1}[part]
            blk = src[outer, :, (i % per) * tn:(i % per + 1) * tn]
            dxn = dxn + _dot_nt(blk, w_ref[i])
        xf = x_ref[...]
        r = _rstd(xf)
        dan_ref[...] += jnp.sum(dxn * (xf * r), axis=0, keepdims=True)
        u = dxn * ag_ref[...]
        dx_ref[...] = dh1_ref[...] + r * u - xf * ((r * r * r) * jnp.mean(u * xf, axis=-1, keepdims=True))

        @pl.when(pl.program_id(0) == t // tm - 1)
        def _():
            _all_reduce_small([dan_ref], [dan_ref], [dan_land], dan_send, dan_recv)
            for cp in _later_chip_copies(sums_ref, land_ref, send_sems, recv_sems):
                cp.wait()

    row = lambda m: (m, 0)
    fix2 = lambda m: (0, 0)
    return pl.pallas_call(
        body, name="layer_a_in_bwd", grid=(t // tm,),
        in_specs=[pl.BlockSpec((2, tm, D_MODEL), lambda m: (0, m, 0)),
                  pl.BlockSpec((2, tm, D_MODEL), lambda m: (0, m, 0)),
                  ANY,
                  pl.BlockSpec((tm, D_MODEL), row), pl.BlockSpec((tm, D_MODEL), row),
                  pl.BlockSpec((1, D_MODEL), fix2), ANY],
        out_specs=[pl.BlockSpec((tm, D_MODEL), row), pl.BlockSpec((1, D_MODEL), fix2), ANY],
        out_shape=[SDS((t, D_MODEL), F32), SDS((1, D_MODEL), F32), SDS(chip_sums.shape, chip_sums.dtype)],
        scratch_shapes=[pltpu.VMEM(w_in_a.shape, w_in_a.dtype), pltpu.SemaphoreType.DMA((nb,)),
                        pltpu.SemaphoreType.DMA((n_sums,)), pltpu.SemaphoreType.DMA((n_sums,))]
        + _all_reduce_scratch([(1, D_MODEL)]),
        compiler_params=_cparams(),
    )(dqg, dkv, w_in_a, x, dh1, a_gain, chip_sums)


def _lut(s, vals):
    r = jnp.int32(vals[0])
    for i in range(1, len(vals)):
        r = jnp.where(s == i, jnp.int32(vals[i]), r)
    return r


def _held(steps, i):
    seq, cur = [None] * len(steps), None
    for k in range(len(steps) - 1, -1, -1):
        if steps[k][0] == i:
            cur = steps[k][1:3]
        seq[k] = cur
    for k in range(len(steps)):
        cur = seq[k] = seq[k] if seq[k] is not None else cur
    return seq


def _weight_grad_cols(name, my_slot, a, bs, steps, tn):
    t, dw = a.shape
    n_arr = len(bs)
    which = [s[0] for s in steps]
    blks = [s[3] for s in steps]

    def body(slot_ref, a_ref, *rest):
        b_refs, (o_ref, own_ref, at_ref) = rest[:n_arr], rest[n_arr:]
        s = pl.program_id(0)

        @pl.when(s == 0)
        def _():
            at_ref[...] = a_ref[...].T

        for i in range(n_arr):
            @pl.when(_lut(s, which) == i)
            def _(i=i):
                res = _dot(at_ref[...], b_refs[i][0])
                o_ref[0] = res.astype(BF16)

                @pl.when(_lut(s, blks) == slot_ref[0])
                def _():
                    own_ref[...] = res

    def b_spec(i):
        held = _held(steps, i)
        return pl.BlockSpec((1, t, tn), lambda s, slot: (_lut(s, [h[0] for h in held]), 0,
                                                         _lut(s, [h[1] for h in held])))

    return pl.pallas_call(
        body, name=name,
        grid_spec=pltpu.PrefetchScalarGridSpec(
            num_scalar_prefetch=1, grid=(len(steps),),
            in_specs=[pl.BlockSpec((t, dw), lambda s, slot: (0, 0))] + [b_spec(i) for i in range(n_arr)],
            out_specs=[pl.BlockSpec((1, dw, tn), lambda s, slot: (_lut(s, blks), 0, 0)),
                       pl.BlockSpec((dw, tn), lambda s, slot: (0, 0))],
            scratch_shapes=[pltpu.VMEM((dw, t), BF16)]),
        out_shape=[SDS((N_DEV, dw, tn), BF16), SDS((dw, tn), F32)],
        compiler_params=_cparams(),
    )(my_slot, a, *bs)


def _weight_grad_rows(name, my_slot, a, b):
    t, dw = a.shape
    n_o, _, c = b.shape
    rows = dw // N_DEV
    tn = min(c, 256)
    per = c // tn

    def body(slot_ref, a_ref, b_ref, o_ref, own_ref, at_ref, res_ref):
        @pl.when(pl.program_id(0) == 0)
        def _():
            at_ref[...] = a_ref[...].T

        res_ref[...] = _dot(at_ref[...], b_ref[0].astype(BF16))
        o_ref[...] = res_ref[...].astype(BF16)
        own_ref[...] = res_ref[pl.ds(pl.multiple_of(slot_ref[0] * rows, rows), rows), :]

    all_rows, own = pl.pallas_call(
        body, name=name,
        grid_spec=pltpu.PrefetchScalarGridSpec(
            num_scalar_prefetch=1, grid=(n_o * per,),
            in_specs=[pl.BlockSpec((t, dw), lambda s, slot: (0, 0)),
                      pl.BlockSpec((1, t, tn), lambda s, slot: (s // per, 0, s % per))],
            out_specs=[pl.BlockSpec((dw, tn), lambda s, slot: (0, s)),
                       pl.BlockSpec((rows, tn), lambda s, slot: (0, s))],
            scratch_shapes=[pltpu.VMEM((dw, t), BF16), pltpu.VMEM((dw, tn), F32)]),
        out_shape=[SDS((dw, n_o * c), BF16), SDS((rows, n_o * c), F32)],
        compiler_params=_cparams(),
    )(my_slot, a, b)
    return all_rows.reshape(N_DEV, rows, n_o * c), own


def _lane_lo():
    return lax.broadcasted_iota(jnp.int32, (1, 128), 1) < HEAD_DIM


def _collapse_chunks(ds, keys):
    if ds.shape[1] < keys:
        ds = jnp.concatenate([jnp.zeros((ds.shape[0], keys - ds.shape[1]), F32), ds], axis=1)
    gc = ds[0:CHUNK]
    for cc in range(1, ds.shape[0] // CHUNK):
        gc = gc + pltpu.roll(ds[cc * CHUNK:(cc + 1) * CHUNK], keys - cc * CHUNK, 1)
    return gc


def _offset_sums(gc):
    hi = gc.astype(BF16)
    lo = (gc - hi.astype(F32)).astype(BF16)
    flip = (lax.broadcasted_iota(jnp.int32, (CHUNK, CHUNK), 0)
            + lax.broadcasted_iota(jnp.int32, (CHUNK, CHUNK), 1) == CHUNK - 1).astype(BF16)
    gf = _dot(flip, hi) + _dot(flip, lo)
    skew = pltpu.roll(gf, 0, 1, stride=1, stride_axis=0)
    return jnp.sum(skew, axis=0, keepdims=True)


def _band_bias(w_row, band, rows):
    keys = w_row.shape[1]
    base = jnp.broadcast_to(w_row, (CHUNK, keys))
    skew = pltpu.roll(base, 0, 1, stride=1, stride_axis=0)
    skew = pltpu.roll(skew, keys - (CHUNK - 1), 1)
    col = lax.broadcasted_iota(jnp.int32, (CHUNK, keys), 1)
    chunk0 = jnp.where(col < band, skew, NEG)
    return jnp.concatenate(
        [chunk0] + [pltpu.roll(chunk0, cc * CHUNK, 1) for cc in range(1, rows // CHUNK)], axis=0)


def _silu_parts(g):
    sg = _sigmoid(g)
    return g * sg, sg * (1.0 + g * (1.0 - sg))


A_PAIRS_FWD = 8
A_PAIRS_BWD = 4


def _a_specs(pairs):
    lanes = 128 * pairs
    steps = D_MODEL // lanes
    q = pl.BlockSpec((QBLK, lanes), lambda p, j: (j, p))
    ks = [pl.BlockSpec((QBLK, lanes), lambda p, j, b=b: (jnp.maximum(j - 2 + b, 0), steps + p)) for b in range(3)]
    vs = [pl.BlockSpec((QBLK, lanes), lambda p, j, b=b: (jnp.maximum(j - 2 + b, 0), 2 * steps + p))
          for b in range(3)]
    g = pl.BlockSpec((QBLK, lanes), lambda p, j: (j, 3 * steps + p))
    bias = pl.BlockSpec((pairs, 8, A_KEYS), lambda p, j: (p, 0, 0))
    return q, ks, vs, g, bias


def _a_fill_bias(w_ref, b_ref, j, pairs):
    _fill_bias(2 * pairs, lambda h: w_ref[h // 2, h % 2:h % 2 + 1, :], A_BAND, b_ref, j)


def _by_valid_key_blocks(j, fn):
    pl.when(j == 0)(functools.partial(fn, 1))
    pl.when(j == 1)(functools.partial(fn, 2))
    pl.when(j >= 2)(functools.partial(fn, 3))


def _fill_bias(n, get_row, band, bias_scr, j):
    @pl.when(j == 0)
    def _():
        for h in range(n):
            bias_scr[h] = _band_bias(get_row(h), band, bias_scr.shape[1])


def _normalise_pair(rs, mxs, lane_lo, extra=None):
    num = jnp.where(lane_lo, rs[0], rs[1])
    den = pltpu.roll(jnp.where(lane_lo, rs[1], rs[0]), HEAD_DIM, 1)
    if extra is not None:
        den = den + jnp.where(lane_lo, extra[0], extra[1])
    return num / den, jnp.where(lane_lo, mxs[0], mxs[1]) + jnp.log(den)


def _own_everywhere(x, sel):
    return jnp.where(sel, x, pltpu.roll(x, HEAD_DIM, 1))


def _minus_rows(s, row_full):
    return jnp.concatenate([s[:, i:i + 128] - row_full for i in range(0, s.shape[1], 128)], axis=1)


def _attn_a_fwd(qkvg, bias, gather):
    t = qkvg.shape[0]
    nq = t // QBLK
    n_g = len(gather)
    pairs = A_PAIRS_FWD
    lanes = 128 * pairs
    steps = D_MODEL // lanes
    q_spec, k_specs, v_specs, g_spec, bias_spec = _a_specs(pairs)

    def body(q_ref, k0, k1, k2, v0, v1, v2, g_ref, w_ref, *rest):
        shard_refs, rest = rest[:n_g], rest[n_g:]
        z_ref, o_ref, lse_ref = rest[:3]
        full_refs, (b_ref, *comm) = rest[3:3 + n_g], rest[3 + n_g:]
        p = pl.program_id(0)
        j = pl.program_id(1)
        start, forward, finish = _gather_phases(shard_refs, full_refs, *comm)
        at = p * nq + j
        pl.when(at == 0)(start)
        pl.when(at == steps * nq // 2)(forward)
        _a_fill_bias(w_ref, b_ref, j, pairs)
        lane_lo = _lane_lo()
        sels = (lane_lo, jnp.logical_not(lane_lo))

        def attend(n_blocks):
            first_col = (3 - n_blocks) * QBLK
            for pp in range(pairs):
                cols = slice(128 * pp, 128 * (pp + 1))
                k = jnp.concatenate([r[:, cols] for r in (k0, k1, k2)[3 - n_blocks:]], axis=0)
                v = jnp.concatenate([r[:, cols] for r in (v0, v1, v2)[3 - n_blocks:]], axis=0)
                q = q_ref[:, cols]
                qm2 = jnp.concatenate([jnp.where(sel, q, jnp.zeros_like(q)) for sel in sels], axis=0) * SCALE
                s2 = _dot_nt(qm2, k)
                rs, mxs = [], []
                for hh, sel in enumerate(sels):
                    s = s2[hh * QBLK:(hh + 1) * QBLK] + b_ref[2 * pp + hh, :, first_col:]
                    mxs.append(jnp.max(s, axis=-1, keepdims=True))
                    e = jnp.exp(s - mxs[hh]).astype(BF16)
                    rs.append(_dot(e, jnp.where(sel, v, jnp.ones_like(v))))
                o, lse = _normalise_pair(rs, mxs, lane_lo)
                silu, _ = _silu_parts(g_ref[:, cols].astype(F32))
                o_ref[:, cols] = o.astype(BF16)
                z_ref[:, cols] = (o * silu).astype(BF16)
                lse_ref[:, cols] = lse

        _by_valid_key_blocks(j, attend)
        pl.when(at == steps * nq - 1)(finish)

    out_spec = pl.BlockSpec((QBLK, lanes), lambda p, j: (j, p))
    outs = pl.pallas_call(
        body, name="attn_a_fwd", grid=(steps, nq),
        in_specs=[q_spec, *k_specs, *v_specs, g_spec, bias_spec] + [ANY] * n_g,
        out_specs=[out_spec, out_spec, out_spec] + [ANY] * n_g,
        out_shape=[SDS((t, D_MODEL), BF16), SDS((t, D_MODEL), BF16), SDS((t, D_MODEL), F32)]
        + [SDS((N_DEV, *s.shape), s.dtype) for s in gather],
        scratch_shapes=[pltpu.VMEM((2 * pairs, QBLK, A_KEYS), F32)] + _gather_scratch(n_g),
        compiler_params=_cparams(),
    )(qkvg, qkvg, qkvg, qkvg, qkvg, qkvg, qkvg, qkvg, bias, *gather)
    return outs[0], outs[1], outs[2], list(outs[3:])


def _attn_a_bwd(qkvg, bias, out_a, lse, dz, scatter):
    t = qkvg.shape[0]
    nq = t // QBLK
    n_sc = len(scatter)
    pairs = A_PAIRS_BWD
    lanes = 128 * pairs
    steps = D_MODEL // lanes
    q_spec, k_specs, v_specs, g_spec, bias_spec = _a_specs(pairs)

    def body(q_ref, k0, k1, k2, v0, v1, v2, g_ref, w_ref, o_ref, lse_ref, dz_ref, *rest):
        sc_refs, rest = rest[:n_sc], rest[n_sc:]
        dqg_ref, dkv_ref, dg_ref = rest[:3]
        land_refs, rest = rest[3:3 + n_sc], rest[3 + n_sc:]
        dk_acc, dv_acc, gt_acc, b_ref, send_sems, recv_sems = rest
        j = pl.program_id(1)
        first = jnp.logical_and(pl.program_id(0) == 0, j == 0)
        last = jnp.logical_and(pl.program_id(0) == steps - 1, j == nq - 1)

        @pl.when(first)
        def _():
            for cp in _scatter_copies(sc_refs, land_refs, send_sems, recv_sems):
                cp.start()

        _a_fill_bias(w_ref, b_ref, j, pairs)

        @pl.when(j == 0)
        def _():
            dk_acc[...] = jnp.zeros_like(dk_acc)
            dv_acc[...] = jnp.zeros_like(dv_acc)
            gt_acc[...] = jnp.zeros_like(gt_acc)

        lane_lo = _lane_lo()
        sels = (lane_lo, jnp.logical_not(lane_lo))

        def attend(n_blocks):
            first_col = (3 - n_blocks) * QBLK
            for pp in range(pairs):
                cols = slice(128 * pp, 128 * (pp + 1))
                q = q_ref[:, cols]
                k = jnp.concatenate([r[:, cols] for r in (k0, k1, k2)[3 - n_blocks:]], axis=0)
                v = jnp.concatenate([r[:, cols] for r in (v0, v1, v2)[3 - n_blocks:]], axis=0)
                o = o_ref[:, cols].astype(F32)
                lse_pair = lse_ref[:, cols]
                dzf = dz_ref[:, cols].astype(F32)
                silu, dsilu = _silu_parts(g_ref[:, cols].astype(F32))
                do = dzf * silu
                dqg_ref[1, :, cols] = (dzf * o * dsilu).astype(BF16)
                doo = do * o
                qm2 = jnp.concatenate([jnp.where(sel, q, jnp.zeros_like(q)) for sel in sels], axis=0) * SCALE
                dom2 = jnp.concatenate([jnp.where(sel, do, 0.0) for sel in sels], axis=0).astype(BF16)
                s2 = _dot_nt(qm2, k)
                dp2 = _dot_nt(dom2, v)
                ps, dss = [], []
                for hh, sel in enumerate(sels):
                    rows = slice(hh * QBLK, (hh + 1) * QBLK)
                    s = s2[rows] + b_ref[2 * pp + hh, :, first_col:]
                    p = jnp.exp(_minus_rows(s, _own_everywhere(lse_pair, sel)))
                    delta = jnp.sum(jnp.where(sel, doo, 0.0), axis=-1, keepdims=True)
                    ds = p * (dp2[rows] - delta)
                    gt_acc[2 * pp + hh] += _collapse_chunks(ds, A_KEYS)
                    ps.append(p.astype(BF16))
                    dss.append(ds.astype(BF16))
                dsb2 = jnp.concatenate(dss, axis=0)
                dq2 = _dot(dsb2, k) * SCALE
                dk_blk = _dot_tn(dsb2, qm2)
                dv_blk = _dot_tn(jnp.concatenate(ps, axis=0), dom2)
                dqg_ref[0, :, cols] = jnp.where(lane_lo, dq2[0:QBLK], dq2[QBLK:2 * QBLK]).astype(BF16)
                for b in range(n_blocks):
                    rows = pl.ds(pl.multiple_of((j - n_blocks + 1 + b) * QBLK, QBLK), QBLK)
                    dk_acc[rows, cols] += dk_blk[b * QBLK:(b + 1) * QBLK]
                    dv_acc[rows, cols] += dv_blk[b * QBLK:(b + 1) * QBLK]

        _by_valid_key_blocks(j, attend)

        @pl.when(j == nq - 1)
        def _():
            dkv_ref[0] = dk_acc[...].astype(BF16)
            dkv_ref[1] = dv_acc[...].astype(BF16)
            for pp in range(pairs):
                dg_ref[pp] = jnp.concatenate([_offset_sums(gt_acc[2 * pp]), _offset_sums(gt_acc[2 * pp + 1]),
                                              jnp.zeros((6, A_DIAG), F32)], axis=0)

        @pl.when(last)
        def _():
            for cp in _scatter_copies(sc_refs, land_refs, send_sems, recv_sems):
                cp.wait()

    blk = pl.BlockSpec((QBLK, lanes), lambda p, j: (j, p))
    outs = pl.pallas_call(
        body, name="attn_a_bwd", grid=(steps, nq),
        in_specs=[q_spec, *k_specs, *v_specs, g_spec, bias_spec, blk, blk, blk] + [ANY] * n_sc,
        out_specs=[pl.BlockSpec((2, QBLK, lanes), lambda p, j: (0, j, p)),
                   pl.BlockSpec((2, t, lanes), lambda p, j: (0, 0, p)),
                   pl.BlockSpec((pairs, 8, A_DIAG), lambda p, j: (p, 0, 0))] + [ANY] * n_sc,
        out_shape=[SDS((2, t, D_MODEL), BF16), SDS((2, t, D_MODEL), BF16), SDS((N_HEADS // 2, 8, A_DIAG), F32)]
        + [SDS((N_DEV - 1, *g.shape[1:]), g.dtype) for g in scatter],
        scratch_shapes=[pltpu.VMEM((t, lanes), F32), pltpu.VMEM((t, lanes), F32),
                        pltpu.VMEM((2 * pairs, CHUNK, A_KEYS), F32), pltpu.VMEM((2 * pairs, QBLK, A_KEYS), F32),
                        pltpu.SemaphoreType.DMA(((N_DEV - 1) * n_sc,)),
                        pltpu.SemaphoreType.DMA(((N_DEV - 1) * n_sc,))],
        compiler_params=_cparams(),
    )(qkvg, qkvg, qkvg, qkvg, qkvg, qkvg, qkvg, qkvg, bias, out_a, lse, dz, *scatter)
    return outs[0], outs[1], outs[2], list(outs[3:])


def _b_specs(qblk):
    per = qblk // B_PREV
    q = pl.BlockSpec((qblk, 512), lambda h, j: (j, h))
    g = pl.BlockSpec((qblk, 512), lambda h, j: (j, 2 + h))
    kp = pl.BlockSpec((B_PREV, 128), lambda h, j: (jnp.maximum(per * j - 1, 0), 0))
    kc = pl.BlockSpec((qblk, 128), lambda h, j: (j, 0))
    vp = pl.BlockSpec((B_PREV, 128), lambda h, j: (jnp.maximum(per * j - 1, 0), 1))
    vc = pl.BlockSpec((qblk, 128), lambda h, j: (j, 1))
    bias = pl.BlockSpec((B_GROUP, qblk + B_PREV), lambda h, j: (h, 0))
    sinks = pl.BlockSpec(memory_space=pltpu.SMEM)
    return q, g, kp, kc, vp, vc, bias, sinks


def _b_operands(kp, kc, vp, vc, kvh, with_prev):
    k = jnp.concatenate([kp[...], kc[...]], axis=0) if with_prev else kc[...]
    v = jnp.concatenate([vp[...], vc[...]], axis=0) if with_prev else vc[...]
    kr = pltpu.roll(k, HEAD_DIM, 1)
    vr = pltpu.roll(v, HEAD_DIM, 1)
    first = kvh == 0
    return (jnp.where(first, k, kr), jnp.where(first, kr, k),
            jnp.where(first, v, vr), jnp.where(first, vr, v))


def _attn_b_fwd(qg, kv, bias, sinks):
    t = qg.shape[0]
    qblk = B_QBLK_FWD
    per_step = 4
    step = per_step * qblk
    q_spec, g_spec, kp_spec, kc_spec, vp_spec, vc_spec, _, sink_spec = _b_specs(step)
    bias_spec = pl.BlockSpec((B_GROUP, qblk + B_PREV), lambda h, j: (h, 0))

    def body(q_ref, g_ref, kp, kc, vp, vc, w_ref, sink_ref, z_ref, o_ref, lse_ref, b_ref):
        kvh = pl.program_id(0)
        j = pl.program_id(1)
        _fill_bias(B_GROUP, lambda h: w_ref[h:h + 1, :], B_BAND, b_ref, j)
        lane_lo = _lane_lo()
        n_pairs = B_GROUP // 2

        def attend(first):
            k_lo, k_hi, v_lo, v_hi = _b_operands(kp, kc, vp, vc, kvh, True)
            for sb in range(per_step):
                no_prev = first and sb == 0
                first_col = B_PREV if no_prev else 0
                keys = slice(sb * qblk + first_col, (sb + 1) * qblk + B_PREV)
                qrows = slice(sb * qblk, (sb + 1) * qblk)
                halves = []
                for hh, sel in enumerate((lane_lo, jnp.logical_not(lane_lo))):
                    kk = (k_lo if hh == 0 else k_hi)[keys]
                    vv = (v_lo if hh == 0 else v_hi)[keys]
                    qm4 = jnp.concatenate(
                        [jnp.where(sel, q_ref[qrows, 128 * pp:128 * (pp + 1)], jnp.zeros((qblk, 128), BF16))
                         for pp in range(n_pairs)], axis=0) * SCALE
                    s4 = _dot_nt(qm4, kk)
                    es, mxs = [], []
                    for pp in range(n_pairs):
                        g = 2 * pp + hh
                        s = s4[pp * qblk:(pp + 1) * qblk] + b_ref[g, :, first_col:]
                        mxs.append(jnp.maximum(jnp.max(s, axis=-1, keepdims=True), sink_ref[kvh * B_GROUP + g]))
                        es.append(jnp.exp(s - mxs[pp]).astype(BF16))
                    r4 = _dot(jnp.concatenate(es, axis=0), jnp.where(sel, vv, jnp.ones_like(vv)))
                    halves.append((r4, mxs))
                for pp in range(n_pairs):
                    cols = slice(128 * pp, 128 * (pp + 1))
                    rows = slice(pp * qblk, (pp + 1) * qblk)
                    mxs = [halves[hh][1][pp] for hh in range(2)]
                    sink_terms = [jnp.exp(sink_ref[kvh * B_GROUP + 2 * pp + hh] - mxs[hh]) for hh in range(2)]
                    o, lse = _normalise_pair([halves[hh][0][rows] for hh in range(2)], mxs, lane_lo, sink_terms)
                    silu, _ = _silu_parts(g_ref[qrows, cols].astype(F32))
                    o_ref[qrows, cols] = o.astype(BF16)
                    z_ref[qrows, cols] = (o * silu).astype(BF16)
                    lse_ref[qrows, cols] = lse

        pl.when(j == 0)(functools.partial(attend, True))
        pl.when(j >= 1)(functools.partial(attend, False))

    out_spec = pl.BlockSpec((step, 512), lambda h, j: (j, h))
    return pl.pallas_call(
        body, name="attn_b_fwd", grid=(B_KV_HEADS, t // step),
        in_specs=[q_spec, g_spec, kp_spec, kc_spec, vp_spec, vc_spec, bias_spec, sink_spec],
        out_specs=[out_spec, out_spec, out_spec],
        out_shape=[SDS((t, D_MODEL), BF16), SDS((t, D_MODEL), BF16), SDS((t, D_MODEL), F32)],
        scratch_shapes=[pltpu.VMEM((B_GROUP, qblk, qblk + B_PREV), F32)],
        compiler_params=_cparams(),
    )(qg, qg, kv, kv, kv, kv, bias, sinks)


def _attn_b_bwd(qg, kv, bias, sinks, out_b, lse, dz, bucket_onehot):
    t = qg.shape[0]
    qblk = B_QBLK_BWD
    keys = qblk + B_PREV
    nq = t // qblk
    per = qblk // B_PREV

    def body(q_ref, g_ref, kp, kc, vp, vc, w_ref, sink_ref, o_ref, lse_ref, dz_ref, oh_ref,
             dqg_ref, dkv_ref, dt5_ref, dsink_ref, gt_acc, b_ref):
        j = pl.program_id(0)
        _fill_bias(N_HEADS, lambda h: w_ref[h:h + 1, :], B_BAND, b_ref, j)

        @pl.when(j == 0)
        def _():
            dkv_ref[...] = jnp.zeros_like(dkv_ref)
            gt_acc[...] = jnp.zeros_like(gt_acc)
            dsink_ref[...] = jnp.zeros_like(dsink_ref)

        lane_lo = _lane_lo()

        def attend(with_prev):
            first_col = 0 if with_prev else B_PREV
            dk_add = jnp.zeros((keys - first_col, 128), F32)
            dv_add = jnp.zeros((keys - first_col, 128), F32)
            for kvh in range(B_KV_HEADS):
                k_lo, k_hi, v_lo, v_hi = _b_operands(kp, kc, vp, vc, kvh, with_prev)
                dk_blk = jnp.zeros((keys - first_col, 128), F32)
                dv_blk = jnp.zeros((keys - first_col, 128), F32)
                for pp in range(B_GROUP // 2):
                    cols = slice(512 * kvh + 128 * pp, 512 * kvh + 128 * (pp + 1))
                    qp = q_ref[:, cols]
                    o = o_ref[:, cols].astype(F32)
                    lse_pair = lse_ref[:, cols]
                    dzf = dz_ref[:, cols].astype(F32)
                    silu, dsilu = _silu_parts(g_ref[:, cols].astype(F32))
                    do = dzf * silu
                    dqg_ref[1, :, cols] = (dzf * o * dsilu).astype(BF16)
                    doo = do * o
                    dqs = []
                    for hh in range(2):
                        g = kvh * B_GROUP + 2 * pp + hh
                        sel = lane_lo if hh == 0 else jnp.logical_not(lane_lo)
                        kk = k_lo if hh == 0 else k_hi
                        vv = v_lo if hh == 0 else v_hi
                        qm = jnp.where(sel, qp, jnp.zeros_like(qp)) * SCALE
                        s = _dot_nt(qm, kk) + b_ref[g, :, first_col:]
                        lse_h = _own_everywhere(lse_pair, sel)
                        p = jnp.exp(_minus_rows(s, lse_h))
                        delta = jnp.sum(jnp.where(sel, doo, 0.0), axis=-1, keepdims=True)
                        dom = jnp.where(sel, do, 0.0).astype(BF16)
                        dp = _dot_nt(dom, vv)
                        ds = p * (dp - delta)
                        gt_acc[g, :, first_col:] += ds
                        dsink_ref[g:g + 1, :] -= jnp.sum(jnp.exp(sink_ref[g] - lse_h) * delta, axis=0, keepdims=True)
                        dsb = ds.astype(BF16)
                        dqs.append(_dot(dsb, kk) * SCALE)
                        dk_blk = dk_blk + _dot_tn(dsb, qm)
                        dv_blk = dv_blk + _dot_tn(p.astype(BF16), dom)
                    dqg_ref[0, :, cols] = jnp.where(lane_lo, dqs[0], dqs[1]).astype(BF16)
                mine = lane_lo if kvh == 0 else jnp.logical_not(lane_lo)
                dk_add = dk_add + jnp.where(mine, dk_blk + pltpu.roll(dk_blk, HEAD_DIM, 1), 0.0)
                dv_add = dv_add + jnp.where(mine, dv_blk + pltpu.roll(dv_blk, HEAD_DIM, 1), 0.0)
            first_key = B_PREV if with_prev else 0
            if with_prev:
                rows = pl.ds(pl.multiple_of(j * qblk - B_PREV, B_PREV), B_PREV)
                dkv_ref[0, rows, :] += dk_add[0:B_PREV]
                dkv_ref[1, rows, :] += dv_add[0:B_PREV]
            rows = pl.ds(pl.multiple_of(j * qblk, qblk), qblk)
            dkv_ref[0, rows, :] += dk_add[first_key:first_key + qblk]
            dkv_ref[1, rows, :] += dv_add[first_key:first_key + qblk]

        pl.when(j == 0)(functools.partial(attend, False))
        pl.when(j >= 1)(functools.partial(attend, True))

        @pl.when(j == nq - 1)
        def _():
            dd = jnp.concatenate([_offset_sums(_collapse_chunks(gt_acc[g], keys)) for g in range(N_HEADS)], axis=0)
            hi = dd.astype(BF16)
            lo = (dd - hi.astype(F32)).astype(BF16)
            dt5_ref[...] = _dot(hi, oh_ref[...]) + _dot(lo, oh_ref[...])

    wide = lambda col: pl.BlockSpec((qblk, D_MODEL), lambda j, col=col: (j, col))
    prev = lambda col: pl.BlockSpec((B_PREV, 128), lambda j, col=col: (jnp.maximum(per * j - 1, 0), col))
    cur = lambda col: pl.BlockSpec((qblk, 128), lambda j, col=col: (j, col))
    fixed = lambda shape: pl.BlockSpec(shape, lambda j: (0,) * len(shape))
    return pl.pallas_call(
        body, name="attn_b_bwd", grid=(nq,),
        in_specs=[wide(0), wide(1), prev(0), cur(0), prev(1), cur(1), fixed((N_HEADS, keys)),
                  pl.BlockSpec(memory_space=pltpu.SMEM), wide(0), wide(0), wide(0), fixed((keys, 128))],
        out_specs=[pl.BlockSpec((2, qblk, D_MODEL), lambda j: (0, j, 0)), fixed((2, t, 128)),
                   fixed((N_HEADS, 128)), fixed((N_HEADS, 128))],
        out_shape=[SDS((2, t, D_MODEL), BF16), SDS((2, t, 128), F32),
                   SDS((N_HEADS, 128), F32), SDS((N_HEADS, 128), F32)],
        scratch_shapes=[pltpu.VMEM((N_HEADS, qblk, keys), F32), pltpu.VMEM((N_HEADS, qblk, keys), F32)],
        compiler_params=_cparams(),
    )(qg, qg, kv, kv, kv, kv, bias, sinks, out_b, lse, dz, bucket_onehot)


def _a_bias_by_offset(rel_bias):
    m = np.arange(A_DIAG)
    idx = np.clip(A_BAND - 1 - m, -A_REL_CLIP, A_REL_CLIP) + A_REL_CLIP
    by_head = rel_bias[idx].T.reshape(N_HEADS // 2, 2, A_DIAG)
    return jnp.concatenate([by_head, jnp.zeros((N_HEADS // 2, 6, A_DIAG), F32)], axis=1)


def _a_bias_grad(offset_sums):
    first = 319
    tail = jnp.sum(offset_sums[:, :first], axis=1)
    body = jnp.flip(offset_sums[:, first:first + 320], axis=1)
    body = body.at[:, -1].add(tail)
    full = jnp.concatenate([jnp.zeros((N_HEADS, 193), F32), body], axis=1)
    return full


def _t5_bucket(rel):
    nb = T5_BUCKETS // 2
    max_exact = nb // 2
    ret = jnp.where(rel > 0, nb, 0)
    n = jnp.abs(rel)
    nf = jnp.maximum(n, 1).astype(jnp.float32)
    large = max_exact + (jnp.log(nf / max_exact) / math.log(T5_MAX_DIST / max_exact)
                         * (nb - max_exact)).astype(jnp.int32)
    large = jnp.minimum(large, nb - 1)
    return ret + jnp.where(n < max_exact, n, large)


def _b_offset_buckets(keys):
    return _t5_bucket(jnp.arange(keys, dtype=jnp.int32) - (B_LEFT_CHUNKS * CHUNK + CHUNK - 1))


def _b_bias_by_offset(t5_table, keys):
    return t5_table[_b_offset_buckets(keys)].T


def _b_bucket_onehot(keys):
    return (_b_offset_buckets(keys)[:, None] == jnp.arange(128)[None, :]).astype(BF16)


def _local_step(my_slot, order, x, target, a_gain_shard, w_in_a_shard, rel_bias, late_shards, kv_gain,
                t5_table, b_gain, sinks, f_gain):
    a_bias = _a_bias_by_offset(rel_bias)
    b_bias_fwd = _b_bias_by_offset(t5_table, B_QBLK_FWD + B_PREV)
    b_bias_bwd = _b_bias_by_offset(t5_table, B_QBLK_BWD + B_PREV)
    sinks_flat = sinks.reshape(N_HEADS)

    xn, qkvg, w_in_a, a_gain = _norm_matmul_gather(order, x, a_gain_shard, w_in_a_shard)
    z_a, out_a, lse_a, (w_in_b, w_out_a, w_out_b, kv_w) = _attn_a_fwd(qkvg, a_bias, late_shards)
    w_out_a = w_out_a.reshape(D_MODEL, D_MODEL)
    w_out_b = w_out_b.reshape(D_MODEL, D_MODEL)
    kv_w = kv_w.reshape(D_MODEL, 2 * 128)
    h1, kvn, hb, kv, qg = _layer_a_out(x, z_a, w_out_a, kv_gain, b_gain, kv_w, w_in_b)
    z_b, out_b, lse_b = _attn_b_fwd(qg, kv, b_bias_fwd, sinks_flat)
    dh2, dh2b, dz_b, loss, d_fn = _layer_b_out_loss(h1, z_b, w_out_b, f_gain, target)

    dqg_b, dkv_b, d_t5, d_sink = _attn_b_bwd(qg, kv, b_bias_bwd, sinks_flat, out_b, lse_b, dz_b,
                                             _b_bucket_onehot(B_QBLK_BWD + B_PREV))
    dh1, dh1b, dz_a, d_bn, d_kn = _layer_b_in_bwd(dqg_b, dkv_b, w_in_b, kv_w, h1, dh2, b_gain, kv_gain, w_out_a)
    early = dict(
        b_w_out=_weight_grad_rows("grad_b_w_out", my_slot, z_b, dh2b[None]),
        b_w_in=_weight_grad_cols("grad_b_w_in", my_slot, hb, [dqg_b],
                                 [(0, o, c, 4 * o + c) for o in range(2) for c in range(4)], 256),
        kv_w=_weight_grad_rows("grad_kv_w", my_slot, kvn, dkv_b),
        a_w_out=_weight_grad_rows("grad_a_w_out", my_slot, z_a, dh1b[None]))
    dqg_a, dkv_a, d_rel, landed = _attn_a_bwd(qkvg, a_bias, out_a, lse_a, dz_a, [g[0] for g in early.values()])
    ready = dict(
        loss=loss, a_rel_bias=d_rel[:, :2].reshape(N_HEADS, A_DIAG),
        kv_norm=d_kn, t5_bias=d_t5, b_norm=d_bn, b_sinks=d_sink, final_norm=d_fn)
    g_own, from_sibling, from_far, chip_sums, ready_sums = _grad_a_w_in_reduce(
        _a_w_in_grad_order(), xn, dqg_a, dkv_a, list(ready.values()))
    grad_x, d_an, from_near = _layer_a_in_bwd(dqg_a, dkv_a, w_in_a, x, dh1, a_gain, chip_sums)

    matrices = {n: (g[1], [(land, 0, N_DEV - 1)]) for (n, g), land in zip(early.items(), landed)}
    matrices["a_w_in"] = (g_own, [(from_sibling, 0, 1), (from_far, 0, from_far.shape[0]),
                                  (from_near, 0, from_near.shape[0])])
    small = dict(zip(ready.keys(), ready_sums), a_norm=d_an)
    return grad_x, small, matrices


def _place():
    x, y, c = lax.axis_index("x"), lax.axis_index("y"), lax.axis_index("c")
    chips = [(1 - x, y), (x, 1 - y), (1 - x, 1 - y)]
    return x, y, c, chips


def _slot(px, py, pc):
    return 4 * px + 2 * py + pc


ANY = pl.BlockSpec(memory_space=pl.ANY)


def _peer(x, y, c, k):
    return (x ^ (k >> 2), y ^ ((k >> 1) & 1), c ^ (k & 1))


def _scatter_copies(grad_refs, land_refs, send_sems, recv_sems):
    x, y, c, _ = _place()
    copies = []
    for t, (grad, land) in enumerate(zip(grad_refs, land_refs)):
        for k in range(1, N_DEV):
            peer = _peer(x, y, c, k)
            sem = (N_DEV - 1) * t + k - 1
            copies.append(pltpu.make_async_remote_copy(
                src_ref=grad.at[_slot(*peer)], dst_ref=land.at[k - 1],
                send_sem=send_sems.at[sem], recv_sem=recv_sems.at[sem],
                device_id=peer, device_id_type=MESH))
    return copies


def _gather_phases(ins, outs, send_sems, recv_sems, local_sems):
    n = len(ins)
    x, y, c, chips = _place()
    me, sibling = (x, y, c), (x, y, 1 - c)

    def copy(t, k, block, to, src=None):
        dst = outs[t].at[_slot(*block)]
        return pltpu.make_async_remote_copy(
            src_ref=dst if src is None else src, dst_ref=dst,
            send_sem=send_sems.at[7 * t + k], recv_sem=recv_sems.at[7 * t + k],
            device_id=to, device_id_type=MESH)

    def lists():
        mine = [pltpu.make_async_copy(ins[t], outs[t].at[_slot(*me)], local_sems.at[t]) for t in range(n)]
        first = []
        for t in range(n):
            first.append(copy(t, 0, me, sibling, src=ins[t]))
            first += [copy(t, 1 + j, me, (*chip, c), src=ins[t]) for j, chip in enumerate(chips)]
        passed = [copy(t, 4 + j, (*chip, c), sibling) for t in range(n) for j, chip in enumerate(chips)]
        return mine, first, passed

    def start():
        mine, first, _ = lists()
        for cp in mine + first:
            cp.start()

    def forward():
        _, _, passed = lists()
        for t in range(n):
            for j, chip in enumerate(chips):
                copy(t, 1 + j, (*chip, c), me).wait_recv()
                passed[3 * t + j].start()

    def finish():
        mine, first, passed = lists()
        for t in range(n):
            copy(t, 0, sibling, me).wait_recv()
            for j, chip in enumerate(chips):
                copy(t, 4 + j, (*chip, 1 - c), me).wait_recv()
        for cp in first + passed:
            cp.wait_send()
        for cp in mine:
            cp.wait()

    return start, forward, finish


def _gather_scratch(n):
    return [pltpu.SemaphoreType.DMA((7 * n,)), pltpu.SemaphoreType.DMA((7 * n,)), pltpu.SemaphoreType.DMA((n,))]


_FAR_CHIP_FIRST = (2, 0, 1)
_SUMS_SENT_AT_ONCE = 1


def _a_w_in_grad_order():
    x, y, c, chips = _place()
    slots = []
    for j in _FAR_CHIP_FIRST:
        slots += [_slot(*chips[j], 1 - c), _slot(*chips[j], c)]
    slots += [_slot(x, y, 1 - c), _slot(x, y, c)]

    def kept(reads):
        out, nxt = [None] * N_DEV, slots[-1]
        for s in reversed(range(N_DEV)):
            nxt = jnp.where(reads(slots[s]), slots[s], nxt)
            out[s] = nxt
        return out

    from_dqg = lambda b: jnp.logical_or(b < 2, b >= 6)
    return jnp.stack(slots + kept(from_dqg) + kept(lambda b: jnp.logical_not(from_dqg(b)))).astype(jnp.int32)


def _grad_a_w_in_reduce(order, a, dqg, dkv, small):
    t, dw = a.shape
    tn = dqg.shape[2] // 2
    n_s = len(small)
    n_far, n_sent = len(_FAR_CHIP_FIRST), _SUMS_SENT_AT_ONCE

    def body(order_ref, a_ref, dqg_ref, dkv_ref, *rest):
        small_refs, rest = rest[:n_s], rest[n_s:]
        own_ref, sib_ref, chips_ref, later_ref = rest[:4]
        small_out, rest = rest[4:4 + n_s], rest[4 + n_s:]
        a_buf, at_ref, res_ref, stage, land, load_sem, d2d_send, d2d_recv, ici_send, ici_recv = rest[:10]
        small_lands, (small_send, small_recv) = rest[10:10 + n_s], rest[10 + n_s:]
        s = pl.program_id(0)
        x, y, c, chips = _place()

        def to_sibling(i):
            return pltpu.make_async_remote_copy(
                src_ref=stage.at[i], dst_ref=land.at[i] if i < n_far else sib_ref.at[0],
                send_sem=d2d_send.at[i], recv_sem=d2d_recv.at[i], device_id=(x, y, 1 - c), device_id_type=MESH)

        def to_chip(i):
            return pltpu.make_async_remote_copy(
                src_ref=land.at[i], dst_ref=chips_ref.at[i], send_sem=ici_send.at[i], recv_sem=ici_recv.at[i],
                device_id=(*chips[_FAR_CHIP_FIRST[i]], c), device_id_type=MESH)

        @pl.when(s == 0)
        def _():
            load = pltpu.make_async_copy(a_ref, a_buf, load_sem)
            load.start()
            _all_reduce_small_start(small_refs, small_lands, small_send, small_recv)
            load.wait()
            at_ref[...] = a_buf[...].T

        blk = order_ref[s]
        from_qg = jnp.logical_or(blk < 2, blk >= 6)

        @pl.when(from_qg)
        def _():
            res_ref[...] = _dot(at_ref[...], dqg_ref[0])

        @pl.when(jnp.logical_not(from_qg))
        def _():
            res_ref[...] = _dot(at_ref[...], dkv_ref[0])

        for i in range(n_far + 1):
            @pl.when(s == 2 * i)
            def _(i=i):
                stage[i] = res_ref[...].astype(BF16)
                to_sibling(i).start()

        for i in range(n_far):
            @pl.when(s == 2 * i + 1)
            def _(i=i):
                to_sibling(i).wait_recv()
                total = (res_ref[...] + land[i].astype(F32)).astype(BF16)
                if i < n_sent:
                    land[i] = total
                    to_chip(i).start()
                else:
                    later_ref[i - n_sent] = total

        @pl.when(s == N_DEV - 1)
        def _():
            own_ref[...] = res_ref[...]
            _all_reduce_small_finish(small_refs, small_out, small_lands, small_send, small_recv)
            for i in range(n_far + 1):
                to_sibling(i).wait_send()
            to_sibling(n_far).wait_recv()
            for i in range(n_sent):
                to_chip(i).wait()

    whole = pl.BlockSpec(memory_space=pltpu.VMEM)
    outs = pl.pallas_call(
        body, name="grad_a_w_in",
        grid_spec=pltpu.PrefetchScalarGridSpec(
            num_scalar_prefetch=1, grid=(N_DEV,),
            in_specs=[ANY,
                      pl.BlockSpec((1, t, tn), lambda s, o: (o[N_DEV + s] // 6, 0, o[N_DEV + s] % 2)),
                      pl.BlockSpec((1, t, tn), lambda s, o: ((o[2 * N_DEV + s] // 4) % 2, 0, o[2 * N_DEV + s] % 2))]
            + [whole] * n_s,
            out_specs=[pl.BlockSpec((dw, tn), lambda s, o: (0, 0)), ANY, ANY, whole] + [whole] * n_s,
            scratch_shapes=[pltpu.VMEM((t, dw), BF16), pltpu.VMEM((dw, t), BF16), pltpu.VMEM((dw, tn), F32),
                            pltpu.VMEM((n_far + 1, dw, tn), BF16), pltpu.VMEM((n_far, dw, tn), BF16),
                            pltpu.SemaphoreType.DMA,
                            pltpu.SemaphoreType.DMA((n_far + 1,)), pltpu.SemaphoreType.DMA((n_far + 1,)),
                            pltpu.SemaphoreType.DMA((n_sent,)), pltpu.SemaphoreType.DMA((n_sent,))]
            + _all_reduce_scratch([s.shape for s in small])),
        out_shape=[SDS((dw, tn), F32), SDS((1, dw, tn), BF16), SDS((n_sent, dw, tn), BF16),
                   SDS((n_far - n_sent, dw, tn), BF16)] + [SDS(s.shape, F32) for s in small],
        compiler_params=_cparams(),
    )(order, a, dqg, dkv, *small)
    return outs[0], outs[1], outs[2], outs[3], list(outs[4:])


def _later_chip_copies(sums_ref, land_ref, send_sems, recv_sems):
    x, y, c, chips = _place()
    del x, y
    return [pltpu.make_async_remote_copy(
        src_ref=sums_ref.at[i], dst_ref=land_ref.at[i], send_sem=send_sems.at[i], recv_sem=recv_sems.at[i],
        device_id=(*chips[j], c), device_id_type=MESH) for i, j in enumerate(_FAR_CHIP_FIRST[_SUMS_SENT_AT_ONCE:])]


def _row_tile(rows):
    return min(rows, 512)


def _adamw(w, g, m, v):
    m2 = ADAM_B1 * m + (1.0 - ADAM_B1) * g
    v2 = ADAM_B2 * v + (1.0 - ADAM_B2) * jnp.square(g)
    m_hat = m2 / (1.0 - ADAM_B1 ** ADAM_STEP)
    v_hat = v2 / (1.0 - ADAM_B2 ** ADAM_STEP)
    delta = -ADAM_LR * (m_hat / (jnp.sqrt(v_hat) + ADAM_EPS) + ADAM_WD * w)
    return delta, m2, v2


def _reduce_adamw(name, own, partials, w, m, v):
    r, c = own.shape
    tr = _row_tile(r)
    n_p = len(partials)

    def body(own_ref, *rest):
        p_refs, (w_ref, m_ref, v_ref, grad_ref, d_ref, nm_ref, nv_ref) = rest[:n_p], rest[n_p:]
        grad = own_ref[...]
        for p_ref, (_, _, count) in zip(p_refs, partials):
            for j in range(count):
                grad = grad + p_ref[j].astype(F32)
        grad_ref[...] = grad
        d_ref[...], nm_ref[...], nv_ref[...] = _adamw(w_ref[...], grad, m_ref[...], v_ref[...])

    flat = pl.BlockSpec((tr, c), lambda i: (i, 0))
    return pl.pallas_call(
        body, name=name, grid=(r // tr,),
        in_specs=[flat] + [pl.BlockSpec((count, tr, c), lambda i, first=first, count=count: (first // count, i, 0))
                           for _, first, count in partials] + [flat, flat, flat],
        out_specs=[flat, flat, flat, flat],
        out_shape=[SDS((r, c), F32)] * 4,
        compiler_params=_cparams(),
    )(own, *[p[0] for p in partials], w, m, v)


VM = pl.BlockSpec()


def _all_reduce_small(ins, outs, lands, send_sems, recv_sems):
    _all_reduce_small_start(ins, lands, send_sems, recv_sems)
    _all_reduce_small_finish(ins, outs, lands, send_sems, recv_sems)


def _all_reduce_small_sends(ins, lands, send_sems, recv_sems):
    x, y, c, _ = _place()
    return [pltpu.make_async_remote_copy(
        src_ref=src, dst_ref=land.at[_slot(x, y, c)],
        send_sem=send_sems.at[(N_DEV - 1) * t + k - 1], recv_sem=recv_sems.at[(N_DEV - 1) * t + k - 1],
        device_id=_peer(x, y, c, k), device_id_type=MESH)
        for t, (src, land) in enumerate(zip(ins, lands)) for k in range(1, N_DEV)]


def _all_reduce_small_start(ins, lands, send_sems, recv_sems):
    x, y, c, _ = _place()
    for src, land in zip(ins, lands):
        land[_slot(x, y, c)] = src[...]
    for cp in _all_reduce_small_sends(ins, lands, send_sems, recv_sems):
        cp.start()


def _all_reduce_small_finish(ins, outs, lands, send_sems, recv_sems):
    x, y, c, _ = _place()
    copies = _all_reduce_small_sends(ins, lands, send_sems, recv_sems)
    for t, (src, land) in enumerate(zip(ins, lands)):
        for k in range(1, N_DEV):
            sem = (N_DEV - 1) * t + k - 1
            pltpu.make_async_remote_copy(
                src_ref=src, dst_ref=land.at[_slot(*_peer(x, y, c, k))],
                send_sem=send_sems.at[sem], recv_sem=recv_sems.at[sem],
                device_id=(x, y, c), device_id_type=MESH).wait_recv()
    for cp in copies:
        cp.wait_send()
    for out, land in zip(outs, lands):
        total = land[0]
        for s in range(1, N_DEV):
            total = total + land[s]
        out[...] = total


def _all_reduce_scratch(shapes):
    n_sems = (N_DEV - 1) * len(shapes)
    return ([pltpu.VMEM((N_DEV, *s), F32) for s in shapes]
            + [pltpu.SemaphoreType.DMA((n_sems,)), pltpu.SemaphoreType.DMA((n_sems,))])


def _small_adamw(my_slot, sums, ws, ms, vs):
    n = len(ws)

    def body(slot_ref, *refs):
        sum_refs, refs = refs[:n + 1], refs[n + 1:]
        w_refs, m_refs, v_refs, refs = refs[:n], refs[n:2 * n], refs[2 * n:3 * n], refs[3 * n:]
        g_refs, d_refs, nm_refs, nv_refs = refs[:n + 1], refs[n + 1:2 * n + 1], refs[2 * n + 1:3 * n + 1], refs[3 * n + 1:]
        for t in range(n + 1):
            if t == 0:
                g = sum_refs[0][:, pl.ds(pl.multiple_of(slot_ref[0] * 128, 128), 128)]
            else:
                g = sum_refs[t][...]
            g_refs[t][...] = g
            if t < n:
                d_refs[t][...], nm_refs[t][...], nv_refs[t][...] = _adamw(w_refs[t][...], g, m_refs[t][...], v_refs[t][...])

    shapes = [SDS(w.shape, F32) for w in ws]
    outs = pl.pallas_call(
        body, name="small_adamw",
        in_specs=[pl.BlockSpec(memory_space=pltpu.SMEM)] + [VM] * (4 * n + 1),
        out_specs=[VM] * (4 * n + 1),
        out_shape=shapes + [SDS(sums[-1].shape, F32)] + shapes * 3,
    )(my_slot, *sums, *ws, *ms, *vs)
    return outs[:n + 1], outs[n + 1:2 * n + 1], outs[2 * n + 1:3 * n + 1], outs[3 * n + 1:]


def kernel(x, a_norm, a_w_in, a_rel_bias, a_w_out, kv_norm, kv_w, t5_bias, b_norm, b_w_in, b_sinks, b_w_out, final_norm, loss_target, m_a_norm, m_a_w_in, m_a_rel_bias, m_a_w_out, m_kv_norm, m_kv_w, m_t5_bias, m_b_norm, m_b_w_in, m_b_sinks, m_b_w_out, m_final_norm, v_a_norm, v_a_w_in, v_a_rel_bias, v_a_w_out, v_kv_norm, v_kv_w, v_t5_bias, v_b_norm, v_b_w_in, v_b_sinks, v_b_w_out, v_final_norm):
    xi, yi, ci = lax.axis_index("x"), lax.axis_index("y"), lax.axis_index("c")
    my_slot = _slot(xi, yi, ci)

    slot_arr = jnp.reshape(my_slot, (1,)).astype(jnp.int32)
    order = _gather_order(xi, yi, ci)
    late_shards = [b_w_in[0].astype(BF16), a_w_out[0].astype(BF16), b_w_out[0].astype(BF16), kv_w.astype(BF16)]
    grad_x, loc, matrices = _local_step(
        slot_arr, order, x[0], loss_target[0], a_norm, a_w_in[0].astype(BF16), a_rel_bias[0], late_shards,
        kv_norm.reshape(1, D_MODEL), t5_bias, b_norm, b_sinks, final_norm.reshape(1, D_MODEL))

    shard_w = dict(a_w_in=a_w_in[0], b_w_in=b_w_in[0], a_w_out=a_w_out[0], b_w_out=b_w_out[0], kv_w=kv_w)
    shard_m = dict(a_w_in=m_a_w_in[0], b_w_in=m_b_w_in[0], a_w_out=m_a_w_out[0], b_w_out=m_b_w_out[0], kv_w=m_kv_w)
    shard_v = dict(a_w_in=v_a_w_in[0], b_w_in=v_b_w_in[0], a_w_out=v_a_w_out[0], b_w_out=v_b_w_out[0], kv_w=v_kv_w)
    big = {n: _reduce_adamw("adamw_" + n, own, partials, shard_w[n], shard_m[n], shard_v[n])
           for n, (own, partials) in matrices.items()}

    names = ("a_norm", "a_rel_bias", "kv_norm", "t5_bias", "b_norm", "b_sinks", "final_norm")
    tables = ("a_rel_bias", "t5_bias")

    def row(n, a):
        return a.reshape(-1, a.shape[-1]).T if n in tables else a.reshape(1, -1)

    small_w = [row(n, a) for n, a in zip(names, (a_norm, a_rel_bias, kv_norm, t5_bias, b_norm, b_sinks, final_norm))]
    small_m = [row(n, a) for n, a in zip(names, (m_a_norm, m_a_rel_bias, m_kv_norm, m_t5_bias, m_b_norm, m_b_sinks,
                                                 m_final_norm))]
    small_v = [row(n, a) for n, a in zip(names, (v_a_norm, v_a_rel_bias, v_kv_norm, v_t5_bias, v_b_norm, v_b_sinks,
                                                 v_final_norm))]
    sums = dict(loc)
    sums["a_rel_bias"] = _a_bias_grad(sums["a_rel_bias"])
    sums["t5_bias"] = sums["t5_bias"][:, :T5_BUCKETS]
    sums["b_sinks"] = sums["b_sinks"][:, 0].reshape(1, N_HEADS)
    results = _small_adamw(slot_arr, [sums[n] for n in names + ("loss",)], small_w, small_m, small_v)
    like = dict(a_norm=a_norm, a_rel_bias=a_rel_bias, kv_norm=kv_norm, t5_bias=t5_bias, b_norm=b_norm,
                b_sinks=b_sinks, final_norm=final_norm)
    sm = [{n: (part[i].T if n in tables else part[i]).reshape(like[n].shape) for i, n in enumerate(names)}
          for part in results]
    loss = results[0][len(names)][0, 0]

    order = ("a_norm", "a_w_in", "a_rel_bias", "a_w_out", "kv_norm", "kv_w", "t5_bias", "b_norm",
             "b_w_in", "b_sinks", "b_w_out", "final_norm")
    lead = dict(a_w_in=True, b_w_in=True, a_w_out=True, b_w_out=True, kv_w=False)

    def pick(kind, name):
        if name in big:
            val = big[name][kind]
            return val[None] if lead[name] else val
        return sm[kind][name]

    outs = [loss, grad_x[None]]
    for kind in range(4):
        outs += [pick(kind, n) for n in order]
    return tuple(outs)
```
